```python
import jax, jax.numpy as jnp
from jax import lax
import numpy as np

D_MODEL = 1024
BATCH = 8
SEQ = 8192
DEPTH = 1

ATT_HEADS = 8
ATT_KV_HEADS = 2
ATT_HEAD_DIM = 64
WINDOW = 128
ATT_BLOCK = 128
RET_HEADS = 4
RET_QK_DIM = 64
RET_V_DIM = 128
RET_CHUNK = 128
ROT_BASE = 10000.0

ATT_WIDTH = ATT_HEADS * ATT_HEAD_DIM
ATT_KV_WIDTH = ATT_KV_HEADS * ATT_HEAD_DIM
RET_QK_WIDTH = RET_HEADS * RET_QK_DIM
RET_WIDTH = RET_HEADS * RET_V_DIM
MIX_WIDTH = ATT_WIDTH + RET_WIDTH
IN_SPLITS = (ATT_WIDTH, ATT_KV_WIDTH, ATT_KV_WIDTH, ATT_WIDTH,
             RET_QK_WIDTH, RET_QK_WIDTH, RET_WIDTH, RET_WIDTH)
IN_WIDTH = sum(IN_SPLITS)
RMS_EPS = 1e-6
GN_EPS = 1e-6
NEG_INF = -1e30

kernel_name = "hymba_swa_sink_retention_hybrid"


def rmsnorm(x, g):
    xf = x.astype(jnp.float32)
    y = xf * lax.rsqrt(jnp.mean(xf * xf, axis=-1, keepdims=True) + RMS_EPS)
    return (y * g.astype(jnp.float32)).astype(x.dtype)


def sliding_window_attention(q, k, v, sinks):
    B, S = q.shape[0], q.shape[1]
    N = S // ATT_BLOCK
    G = ATT_HEADS // ATT_KV_HEADS
    q = q.reshape(B, N, ATT_BLOCK, ATT_KV_HEADS, G, ATT_HEAD_DIM)
    k = k.reshape(B, N, ATT_BLOCK, ATT_KV_HEADS, ATT_HEAD_DIM)
    v = v.reshape(B, N, ATT_BLOCK, ATT_KV_HEADS, ATT_HEAD_DIM)

    def with_prev(t):
        prev = jnp.concatenate([jnp.zeros_like(t[:, :1]), t[:, :-1]], axis=1)
        return jnp.concatenate([prev, t], axis=2)

    kk, vv = with_prev(k), with_prev(v)
    scale = ATT_HEAD_DIM ** -0.5
    s = jnp.einsum('bnqhgd,bnkhd->bnhgqk', q, kk).astype(jnp.float32) * scale
    qi = jnp.arange(ATT_BLOCK)[:, None]
    kj = jnp.arange(2 * ATT_BLOCK)[None, :]
    diff = qi + ATT_BLOCK - kj
    band = (diff >= 0) & (diff < WINDOW)
    in_cur = kj >= ATT_BLOCK
    blk = jnp.arange(N)[:, None, None]
    valid = band[None] & ((blk > 0) | in_cur[None])
    s = jnp.where(valid[None, :, None, None], s, NEG_INF)
    sink = jnp.broadcast_to(
        sinks.astype(jnp.float32).reshape(ATT_KV_HEADS, G)[None, None, :, :, None, None],
        s.shape[:-1] + (1,))
    p = jax.nn.softmax(jnp.concatenate([s, sink], axis=-1), axis=-1)[..., :-1]
    o = jnp.einsum('bnhgqk,bnkhd->bnqhgd', p.astype(v.dtype), vv)
    return o.reshape(B, S, ATT_WIDTH)


def rotate_pairs(t, cos, sin):
    B, S, H, D = t.shape
    tf = t.astype(jnp.float32).reshape(B, S, H, D // 2, 2)
    a, b = tf[..., 0], tf[..., 1]
    c, s = cos[None, :, None, :], sin[None, :, None, :]
    out = jnp.stack([a * c - b * s, a * s + b * c], axis=-1)
    return out.reshape(B, S, H, D).astype(t.dtype)


def retention(q, k, v, gn_gain):
    B, S = q.shape[0], q.shape[1]
    C = RET_CHUNK
    N = S // C
    pos = jnp.arange(S, dtype=jnp.float32)
    theta = 1.0 / (ROT_BASE ** jnp.linspace(0.0, 1.0, RET_QK_DIM // 2, dtype=jnp.float32))
    ang = pos[:, None] * theta[None, :]
    cos, sin = jnp.cos(ang), jnp.sin(ang)
    q = rotate_pairs(q, cos, sin)
    k = rotate_pairs(k, cos, sin) * (RET_QK_DIM ** -0.5)

    log_gamma = jnp.log(1.0 - 2.0 ** (-5.0 - jnp.arange(RET_HEADS, dtype=jnp.float32)))
    idx = jnp.arange(C, dtype=jnp.float32)
    rel = idx[:, None] - idx[None, :]
    decay_in = jnp.where(rel >= 0, jnp.exp(log_gamma[:, None, None] * jnp.maximum(rel, 0.0)), 0.0)
    k_dec = jnp.exp(log_gamma[:, None] * (C - 1 - idx)[None, :])
    q_dec = jnp.exp(log_gamma[:, None] * (idx + 1)[None, :])
    chunk_decay = jnp.exp(log_gamma * C)

    q = q.reshape(B, N, C, RET_HEADS, RET_QK_DIM)
    k = k.reshape(B, N, C, RET_HEADS, RET_QK_DIM)
    v = v.reshape(B, N, C, RET_HEADS, RET_V_DIM)

    sc = jnp.einsum('bnchd,bnmhd->bnhcm', q, k) * decay_in[None, None]
    o_inner = jnp.einsum('bnhcm,bnmhe->bnche', sc, v)
    kv_chunk = jnp.einsum('bnmhd,bnmhe,hm->bnhde', k, v, k_dec).astype(jnp.float32)

    def step(state, kv):
        return state * chunk_decay[None, :, None, None] + kv, state

    init = jnp.zeros((B, RET_HEADS, RET_QK_DIM, RET_V_DIM), jnp.float32)
    _, prev = lax.scan(step, init, jnp.moveaxis(kv_chunk, 1, 0))
    prev = jnp.moveaxis(prev, 0, 1)
    o_cross = jnp.einsum('bnchd,bnhde,hc->bnche', q, prev, q_dec)
    o = (o_inner + o_cross).astype(jnp.float32).reshape(B, S, RET_HEADS, RET_V_DIM)

    mu = jnp.mean(o, axis=-1, keepdims=True)
    var = jnp.mean(jnp.square(o - mu), axis=-1, keepdims=True)
    o = (o - mu) * lax.rsqrt(var + GN_EPS)
    o = o.reshape(B, S, RET_WIDTH) * gn_gain.astype(jnp.float32)
    return o.astype(v.dtype)


def _fwd_setup_inputs(seed: int = 0) -> dict:
    key = jax.random.key(seed)
    ks = jax.random.split(key, 8)
    x = jax.random.normal(ks[0], (BATCH, SEQ, D_MODEL), jnp.float32)
    norm_g = 1.0 + 0.02 * jax.random.normal(ks[1], (DEPTH, D_MODEL), jnp.float32)
    w_in = jax.random.normal(ks[2], (DEPTH, D_MODEL, IN_WIDTH), jnp.float32) * D_MODEL ** -0.5
    att_sinks = 0.5 * jax.random.normal(ks[3], (DEPTH, ATT_HEADS), jnp.float32)
    ret_gn_g = 1.0 + 0.02 * jax.random.normal(ks[4], (DEPTH, RET_WIDTH), jnp.float32)
    w_out = jax.random.normal(ks[5], (DEPTH, MIX_WIDTH, D_MODEL), jnp.float32) * MIX_WIDTH ** -0.5
    final_g = 1.0 + 0.02 * jax.random.normal(ks[6], (D_MODEL,), jnp.float32)
    return {"x": x, "norm_g": norm_g, "w_in": w_in, "att_sinks": att_sinks,
            "ret_gn_g": ret_gn_g, "w_out": w_out, "final_g": final_g}


def _fwd_reference(x, norm_g, w_in, att_sinks, ret_gn_g, w_out, final_g):
    B, S = x.shape[0], x.shape[1]
    cuts = np.cumsum(IN_SPLITS)[:-1].tolist()
    for l in range(DEPTH):
        h = rmsnorm(x, norm_g[l])
        proj = jnp.einsum('bsd,de->bse', h, w_in[l])
        aq, ak, av, az, rq, rk, rv, rz = jnp.split(proj, cuts, axis=-1)
        a = sliding_window_attention(
            aq.reshape(B, S, ATT_HEADS, ATT_HEAD_DIM),
            ak.reshape(B, S, ATT_KV_HEADS, ATT_HEAD_DIM),
            av.reshape(B, S, ATT_KV_HEADS, ATT_HEAD_DIM),
            att_sinks[l]) * jax.nn.silu(az)
        r = retention(
            rq.reshape(B, S, RET_HEADS, RET_QK_DIM),
            rk.reshape(B, S, RET_HEADS, RET_QK_DIM),
            rv.reshape(B, S, RET_HEADS, RET_V_DIM),
            ret_gn_g[l]) * jax.nn.silu(rz)
        mix = jnp.concatenate([a, r], axis=-1)
        x = x + jnp.einsum('bse,ed->bsd', mix, w_out[l])
    return rmsnorm(x, final_g)


import jax as _jax
import jax.numpy as _jnp

TWIN_FORMAT = 'train_step'
FWD_PARAMS = ['x', 'norm_g', 'w_in', 'att_sinks', 'ret_gn_g', 'w_out', 'final_g']
TWIN_WEIGHTS = ['norm_g', 'w_in', 'att_sinks', 'ret_gn_g', 'w_out', 'final_g']
TWIN_DIFF_INPUT = 'x'
TWIN_INPUTS = ['x', 'norm_g', 'w_in', 'att_sinks', 'ret_gn_g', 'w_out', 'final_g', 'loss_target', 'm_norm_g', 'm_w_in', 'm_att_sinks', 'm_ret_gn_g', 'm_w_out', 'm_final_g', 'v_norm_g', 'v_w_in', 'v_att_sinks', 'v_ret_gn_g', 'v_w_out', 'v_final_g']
TWIN_OUTPUTS = ['loss', 'grad_x', 'grad_norm_g', 'grad_w_in', 'grad_att_sinks', 'grad_ret_gn_g', 'grad_w_out', 'grad_final_g', 'delta_norm_g', 'delta_w_in', 'delta_att_sinks', 'delta_ret_gn_g', 'delta_w_out', 'delta_final_g', 'new_m_norm_g', 'new_m_w_in', 'new_m_att_sinks', 'new_m_ret_gn_g', 'new_m_w_out', 'new_m_final_g', 'new_v_norm_g', 'new_v_w_in', 'new_v_att_sinks', 'new_v_ret_gn_g', 'new_v_w_out', 'new_v_final_g']
TWIN_LEAF_KINDS = {'loss': 'loss', 'grad_x': 'grad_x', 'grad_norm_g': 'grad_w', 'grad_w_in': 'grad_w', 'grad_att_sinks': 'grad_w', 'grad_ret_gn_g': 'grad_w', 'grad_w_out': 'grad_w', 'grad_final_g': 'grad_w', 'delta_norm_g': 'delta_w', 'delta_w_in': 'delta_w', 'delta_att_sinks': 'delta_w', 'delta_ret_gn_g': 'delta_w', 'delta_w_out': 'delta_w', 'delta_final_g': 'delta_w', 'new_m_norm_g': 'new_m', 'new_m_w_in': 'new_m', 'new_m_att_sinks': 'new_m', 'new_m_ret_gn_g': 'new_m', 'new_m_w_out': 'new_m', 'new_m_final_g': 'new_m', 'new_v_norm_g': 'new_v', 'new_v_w_in': 'new_v', 'new_v_att_sinks': 'new_v', 'new_v_ret_gn_g': 'new_v', 'new_v_w_out': 'new_v', 'new_v_final_g': 'new_v'}


def _forward(args):
    return _fwd_reference(*[args[k] for k in FWD_PARAMS])


def _output_shape():
    def fwd():
        inp = _fwd_setup_inputs(0)
        return _fwd_reference(*[inp[k] for k in FWD_PARAMS])
    out = _jax.eval_shape(fwd)
    return out.shape, out.dtype

N_MICROBATCH = 1
ADAM_LR = 0.001
ADAM_B1 = 0.9
ADAM_B2 = 0.999
ADAM_EPS = 1e-08
ADAM_WD = 0.01
ADAM_STEP = 10
PER_EXAMPLE_BATCH_AXIS = {'x': 0, 'loss_target': 0}
SHARED_INPUTS = []
_WEIGHT_DTYPES = {'norm_g': _jnp.float32, 'w_in': _jnp.float32, 'att_sinks': _jnp.float32, 'ret_gn_g': _jnp.float32, 'w_out': _jnp.float32, 'final_g': _jnp.float32}
MOMENT_SCALE = {'norm_g': 2.087427e-01, 'w_in': 1.211896e-01, 'att_sinks': 1.719680e-02, 'ret_gn_g': 1.426794e-01, 'w_out': 9.873871e-02, 'final_g': 6.399608e+01}


def _to_microbatches(a, axis):
    t = _jnp.moveaxis(a, axis, 0)
    t = t.reshape((N_MICROBATCH, t.shape[0] // N_MICROBATCH) + t.shape[1:])
    return _jnp.moveaxis(t, 1, axis + 1)


def setup_inputs(seed: int = 0) -> dict:
    inp = _fwd_setup_inputs(seed)
    key = _jax.random.fold_in(_jax.random.key(seed), 7919)
    shape, _ = _output_shape()
    out = dict(inp)
    out["loss_target"] = _jax.random.normal(_jax.random.fold_in(key, 0), shape, _jnp.float32)
    for i, name in enumerate(TWIN_WEIGHTS):
        w = inp[name].astype(_jnp.float32)
        if MOMENT_SCALE is None:
            s = _jnp.sqrt(_jnp.mean(_jnp.square(w)) + 1e-30)
        else:
            s = MOMENT_SCALE[name]
        km, kv = _jax.random.split(_jax.random.fold_in(key, i + 1))
        out[name] = w
        out["m_" + name] = s * _jax.random.normal(km, w.shape, _jnp.float32)
        out["v_" + name] = (s * s) * _jax.random.uniform(kv, w.shape, _jnp.float32, 0.5, 1.5)
    if N_MICROBATCH > 1:
        for name, axis in PER_EXAMPLE_BATCH_AXIS.items():
            out[name] = _to_microbatches(out[name], axis)
    return {'x': out['x'], 'norm_g': out['norm_g'], 'w_in': out['w_in'], 'att_sinks': out['att_sinks'], 'ret_gn_g': out['ret_gn_g'], 'w_out': out['w_out'], 'final_g': out['final_g'], 'loss_target': out['loss_target'], 'm_norm_g': out['m_norm_g'], 'm_w_in': out['m_w_in'], 'm_att_sinks': out['m_att_sinks'], 'm_ret_gn_g': out['m_ret_gn_g'], 'm_w_out': out['m_w_out'], 'm_final_g': out['m_final_g'], 'v_norm_g': out['v_norm_g'], 'v_w_in': out['v_w_in'], 'v_att_sinks': out['v_att_sinks'], 'v_ret_gn_g': out['v_ret_gn_g'], 'v_w_out': out['v_w_out'], 'v_final_g': out['v_final_g']}


def _loss(weights, diff, rest, loss_target):
    with _jax.named_scope("forward"):
        args = {**rest, TWIN_DIFF_INPUT: diff, **{k: w.astype(_WEIGHT_DTYPES[k]) for k, w in weights.items()}}
        y = _forward(args)
    with _jax.named_scope("loss_head"):
        err = _jnp.square(y.astype(_jnp.float32) - loss_target)
        return 0.5 * _jnp.sum(_jnp.mean(err, axis=-1)) if err.ndim else 0.5 * err


def _adamw(w, g, m, v):
    m = ADAM_B1 * m + (1.0 - ADAM_B1) * g
    v = ADAM_B2 * v + (1.0 - ADAM_B2) * _jnp.square(g)
    m_hat = m / (1.0 - ADAM_B1 ** ADAM_STEP)
    v_hat = v / (1.0 - ADAM_B2 ** ADAM_STEP)
    delta = -ADAM_LR * (m_hat / (_jnp.sqrt(v_hat) + ADAM_EPS) + ADAM_WD * w)
    return delta, m, v


def reference(x, norm_g, w_in, att_sinks, ret_gn_g, w_out, final_g, loss_target, m_norm_g, m_w_in, m_att_sinks, m_ret_gn_g, m_w_out, m_final_g, v_norm_g, v_w_in, v_att_sinks, v_ret_gn_g, v_w_out, v_final_g):
    given = dict(x=x, norm_g=norm_g, w_in=w_in, att_sinks=att_sinks, ret_gn_g=ret_gn_g, w_out=w_out, final_g=final_g, loss_target=loss_target, m_norm_g=m_norm_g, m_w_in=m_w_in, m_att_sinks=m_att_sinks, m_ret_gn_g=m_ret_gn_g, m_w_out=m_w_out, m_final_g=m_final_g, v_norm_g=v_norm_g, v_w_in=v_w_in, v_att_sinks=v_att_sinks, v_ret_gn_g=v_ret_gn_g, v_w_out=v_w_out, v_final_g=v_final_g)
    weights = {n: given[n] for n in TWIN_WEIGHTS}
    shared = {n: given[n] for n in SHARED_INPUTS}
    per_example = {n: given[n] for n in ['x']}
    grad_fn = _jax.value_and_grad(_loss, argnums=(0, 1))

    def one_microbatch(ex, loss_target):
        ex = dict(ex)
        diff = ex.pop(TWIN_DIFF_INPUT)
        return grad_fn(weights, diff, {**shared, **ex}, loss_target)

    if N_MICROBATCH == 1:
        loss, (grad_w, grad_x) = one_microbatch(per_example, given["loss_target"])
    else:
        def body(carry, xs):
            loss_sum, grad_sum = carry
            l_k, (gw_k, gx_k) = one_microbatch(xs[0], xs[1])
            with _jax.named_scope("update"):
                return (loss_sum + l_k, _jax.tree.map(_jnp.add, grad_sum, gw_k)), gx_k

        init = (_jnp.zeros((), _jnp.float32), _jax.tree.map(_jnp.zeros_like, weights))
        (loss, grad_w), grad_x = _jax.lax.scan(body, init, (per_example, given["loss_target"]))
    with _jax.named_scope("update"):
        delta_w, new_m, new_v = {}, {}, {}
        for n in TWIN_WEIGHTS:
            delta_w[n], new_m[n], new_v[n] = _adamw(weights[n], grad_w[n], given["m_" + n], given["v_" + n])
    return (loss, grad_x, *[grad_w[n] for n in TWIN_WEIGHTS], *[delta_w[n] for n in TWIN_WEIGHTS],
            *[new_m[n] for n in TWIN_WEIGHTS], *[new_v[n] for n in TWIN_WEIGHTS])
```

```python
import functools

import jax
import jax.numpy as jnp
from jax import lax
from jax.experimental import pallas as pl
from jax.experimental.pallas import tpu as pltpu

F32 = jnp.float32
BF16 = jnp.bfloat16

D_MODEL = 1024
ATT_HEADS = 8
ATT_KV_HEADS = 2
ATT_GROUP = ATT_HEADS // ATT_KV_HEADS
ATT_HEAD_DIM = 64
RET_HEADS = 4
RET_QK_DIM = 64
RET_V_DIM = 128
BLK = 128
ROT_BASE = 10000.0
RMS_EPS = 1e-6
GN_EPS = 1e-6
NEG_INF = -1e30
ATT_SCALE = ATT_HEAD_DIM ** -0.5
RET_SCALE = RET_QK_DIM ** -0.5

ATT_WIDTH = ATT_HEADS * ATT_HEAD_DIM
ATT_KV_WIDTH = ATT_KV_HEADS * ATT_HEAD_DIM
RET_QK_WIDTH = RET_HEADS * RET_QK_DIM
RET_WIDTH = RET_HEADS * RET_V_DIM
MIX_WIDTH = ATT_WIDTH + RET_WIDTH
OFF_AQ = 0
OFF_AK = OFF_AQ + ATT_WIDTH
OFF_AV = OFF_AK + ATT_KV_WIDTH
OFF_AZ = OFF_AV + ATT_KV_WIDTH
OFF_RQ = OFF_AZ + ATT_WIDTH
OFF_RK = OFF_RQ + RET_QK_WIDTH
OFF_RV = OFF_RK + RET_QK_WIDTH
OFF_RZ = OFF_RV + RET_WIDTH
IN_WIDTH = OFF_RZ + RET_WIDTH

N_CHIPS = 4
N_DEV = 8
SHARD_IN = IN_WIDTH // N_CHIPS
LANE = 128
SHARD_PAD = 768
SHARD_SHIFT = SHARD_PAD - SHARD_IN
WIN_START = tuple((j * SHARD_IN) // LANE * LANE for j in range(N_CHIPS))
SHARD_OUT = MIX_WIDTH // N_CHIPS

ADAM_LR = 0.001
ADAM_B1 = 0.9
ADAM_B2 = 0.999
ADAM_EPS = 1e-08
ADAM_WD = 0.01
ADAM_STEP = 10

VMEM_CAP = 64 * 1024 * 1024
MESH_ID = pl.DeviceIdType.MESH


def _call(body, **kw):
    return pl.pallas_call(body, **kw)


def _params(vmem_mb, semantics=None):
    assert vmem_mb * 1024 * 1024 < VMEM_CAP
    return pltpu.CompilerParams(dimension_semantics=semantics, vmem_limit_bytes=vmem_mb * 1024 * 1024)


def _dot(a, b):
    return jnp.dot(a, b, preferred_element_type=F32)


def _dot_nt(a, b):
    return lax.dot_general(a, b, (((1,), (1,)), ((), ())), preferred_element_type=F32)


def _dot_tn(a, b):
    return lax.dot_general(a, b, (((0,), (0,)), ((), ())), preferred_element_type=F32)


def _sigmoid(z):
    return 1.0 / (1.0 + jnp.exp(-z))


def _const_spec(shape):
    nd = len(shape)
    return pl.BlockSpec(shape, lambda i: (0,) * nd)


def _tables(seq):
    pos = jnp.arange(seq, dtype=F32)
    theta = 1.0 / (ROT_BASE ** jnp.linspace(0.0, 1.0, RET_QK_DIM // 2, dtype=F32))
    ang = pos[:, None] * theta[None, :]
    cos, sin = jnp.cos(ang), jnp.sin(ang)
    cos2 = jnp.repeat(cos, 2, axis=1)
    sin2 = jnp.stack([-sin, sin], axis=-1).reshape(seq, RET_QK_DIM)
    cos_t = jnp.tile(cos2, (1, RET_HEADS))
    sin_t = jnp.tile(sin2, (1, RET_HEADS))

    log_gamma = jnp.log(1.0 - 2.0 ** (-5.0 - jnp.arange(RET_HEADS, dtype=F32)))
    idx = jnp.arange(BLK, dtype=F32)
    rel = idx[:, None] - idx[None, :]
    decay_in = jnp.where(rel >= 0, jnp.exp(log_gamma[:, None, None] * jnp.maximum(rel, 0.0)), 0.0)
    k_dec = jnp.exp(log_gamma[:, None] * (BLK - 1 - idx)[None, :])
    q_dec = jnp.exp(log_gamma[:, None] * (idx + 1)[None, :])
    chunk_decay = jnp.exp(log_gamma * BLK)
    kdec_t = jnp.repeat(k_dec.T, RET_QK_DIM, axis=1)
    qdec_t = jnp.repeat(q_dec.T, RET_QK_DIM, axis=1)
    cd_t = jnp.broadcast_to(chunk_decay[:, None, None], (RET_HEADS, RET_QK_DIM, RET_V_DIM))
    return cos_t, sin_t, decay_in, qdec_t, kdec_t, cd_t


def _swap_pairs(t):
    lane = lax.broadcasted_iota(jnp.int32, t.shape, 1)
    nxt = pltpu.roll(t, t.shape[1] - 1, 1)
    prv = pltpu.roll(t, 1, 1)
    return jnp.where((lane & 1) == 0, nxt, prv)


def _rotate(t, cos_t, sin_t):
    return t * cos_t + _swap_pairs(t) * sin_t


def _rotate_transposed(d, cos_t, sin_t):
    return d * cos_t + _swap_pairs(d * sin_t)


def _heads_to_rows(t, first_head, n_heads, width):
    return jnp.concatenate([t[:, (first_head + h) * width:(first_head + h + 1) * width] for h in range(n_heads)], axis=0)


def _attn_mask(not_first):
    rows = ATT_GROUP * BLK
    row = lax.broadcasted_iota(jnp.int32, (rows, 2 * BLK), 0) & (BLK - 1)
    col = lax.broadcasted_iota(jnp.int32, (rows, 2 * BLK), 1)
    diff = row + BLK - col
    in_window = jnp.logical_and(diff >= 0, diff < BLK)
    return jnp.logical_and(in_window, col >= jnp.where(not_first, 0, BLK))


def _attn_group_fwd(qg, kk, vv, sink_col, mask):
    s = _dot_nt(qg, kk) * ATT_SCALE
    s = jnp.where(mask, s, NEG_INF)
    m = jnp.maximum(jnp.max(s, axis=1, keepdims=True), sink_col)
    e = jnp.exp(s - m)
    es = jnp.exp(sink_col - m)
    inv = 1.0 / (jnp.sum(e, axis=1, keepdims=True) + es)
    p = e * inv
    o = _dot(p.astype(BF16), vv)
    return p, es * inv, o


def _sink_column(sinks_ref, group):
    return jnp.concatenate(
        [jnp.full((BLK, 1), sinks_ref[0, group * ATT_GROUP + h], F32) for h in range(ATT_GROUP)], axis=0)


def _group_to_lanes(og):
    return [og[h * BLK:(h + 1) * BLK, :] for h in range(ATT_GROUP)]


def _group_norm(o):
    mu = jnp.mean(o, axis=1, keepdims=True)
    xc = o - mu
    var = jnp.mean(xc * xc, axis=1, keepdims=True)
    rstd = lax.rsqrt(var + GN_EPS)
    return xc * rstd, rstd


def _weights_allgather(w_in_pad, w_out_shard):
    half_in = D_MODEL // 2
    half_out = SHARD_OUT // 2

    def body(wi_ref, wo_ref, win_ref, wout_ref, sh_ref, blk_ref, blko_ref, send_sems, recv_sems):
        x, y, c = lax.axis_index("x"), lax.axis_index("y"), lax.axis_index("c")
        me, sibling = (x, y, c), (x, y, 1 - c)
        chips = [(1 - x, y), (x, 1 - y), (1 - x, 1 - y)]

        def rows_in(px, py, pc):
            return sh_ref.at[pl.ds(pl.multiple_of((4 * px + 2 * py + pc) * half_in, half_in), half_in), :]

        def rows_out(px, py, pc):
            return wout_ref.at[pl.ds(pl.multiple_of((4 * px + 2 * py + pc) * half_out, half_out), half_out), :]

        def copy(k, rows, block, to, src=None):
            return pltpu.make_async_remote_copy(
                src_ref=rows(*block) if src is None else src, dst_ref=rows(*block),
                send_sem=send_sems.at[k], recv_sem=recv_sems.at[k], device_id=to, device_id_type=MESH_ID)

        mine = wi_ref[pl.ds(pl.multiple_of(c * half_in, half_in), half_in), :]
        mine = pltpu.roll(mine, y * SHARD_SHIFT, 1)
        blk_ref[...] = mine.astype(BF16)
        blko_ref[...] = wo_ref[pl.ds(pl.multiple_of(c * half_out, half_out), half_out), :].astype(BF16)
        rows_in(*me)[...] = blk_ref[...]
        rows_out(*me)[...] = blko_ref[...]

        sets = ((0, rows_in, blk_ref), (7, rows_out, blko_ref))
        first, passed = [], []
        for base, rows, src in sets:
            first.append(copy(base, rows, me, sibling, src=src))
            first += [copy(base + 1 + j, rows, me, (*chip, c), src=src) for j, chip in enumerate(chips)]
        for cp in first:
            cp.start()
        for base, rows, src in sets:
            for j, chip in enumerate(chips):
                copy(base + 1 + j, rows, (*chip, c), me).wait_recv()
                fwd = copy(base + 4 + j, rows, (*chip, c), sibling)
                fwd.start()
                passed.append(fwd)
        for base, rows, src in sets:
            copy(base, rows, sibling, me).wait_recv()
            for j, chip in enumerate(chips):
                copy(base + 4 + j, rows, (*chip, 1 - c), me).wait_recv()
        for cp in first + passed:
            cp.wait_send()

        def shard(j, lo, hi):
            return sh_ref[j * D_MODEL:(j + 1) * D_MODEL, lo:hi]

        keep = SHARD_PAD - LANE
        for pair in range(2):
            a, b = 2 * pair, 2 * pair + 1
            start = WIN_START[a]
            win_ref[:, start:start + keep] = shard(a, 0, keep)
            win_ref[:, start + keep:start + SHARD_PAD] = shard(a, keep, SHARD_PAD) + shard(b, 0, LANE)
            win_ref[:, start + SHARD_PAD:start + SHARD_PAD + keep] = shard(b, LANE, SHARD_PAD)

    vmem = pl.BlockSpec(memory_space=pltpu.VMEM)
    return _call(
        body, name="weights_allgather",
        out_shape=(jax.ShapeDtypeStruct((D_MODEL, IN_WIDTH), BF16), jax.ShapeDtypeStruct((MIX_WIDTH, D_MODEL), BF16)),
        in_specs=[vmem, vmem], out_specs=(vmem, vmem),
        scratch_shapes=[
            pltpu.VMEM((N_CHIPS * D_MODEL, SHARD_PAD), BF16),
            pltpu.VMEM((half_in, SHARD_PAD), BF16),
            pltpu.VMEM((half_out, D_MODEL), BF16),
            pltpu.SemaphoreType.DMA((14,)),
            pltpu.SemaphoreType.DMA((14,)),
        ],
        compiler_params=_params(40),
    )(w_in_pad, w_out_shard)


def _in_proj(x, norm_g, w_in, tb):
    seq = x.shape[0]

    def body(x_ref, g_ref, w_ref, p_ref):
        xv = x_ref[...]
        r = lax.rsqrt(jnp.mean(xv * xv, axis=1, keepdims=True) + RMS_EPS)
        h = (xv * r) * g_ref[...]
        p_ref[...] = _dot(h.astype(BF16), w_ref[...])

    return _call(
        body, name="in_proj", grid=(seq // tb,),
        out_shape=jax.ShapeDtypeStruct((seq, IN_WIDTH), F32),
        in_specs=[pl.BlockSpec((tb, D_MODEL), lambda i: (i, 0)), _const_spec((1, D_MODEL)),
                  _const_spec((D_MODEL, IN_WIDTH))],
        out_specs=pl.BlockSpec((tb, IN_WIDTH), lambda i: (i, 0)),
        compiler_params=_params(48, ("arbitrary",)),
    )(x, norm_g, w_in)


def _mix_fwd(proj, x, target, w_out, final_g, gn_gain, sinks, tables, tb):
    seq = x.shape[0]
    nsub = tb // BLK
    cos_t, sin_t, decay_in, qdec_t, kdec_t, cd_t = tables

    def body(p_ref, x_ref, t_ref, cos_ref, sin_ref, wout_ref, fg_ref, gain_ref, sinks_ref, din_ref, qdec_ref,
             kdec_ref, cd_ref, mix_ref, dxo_ref, st_ref, loss_ref, gfin_ref, kprev_ref, vprev_ref, state_ref):
        i = pl.program_id(0)

        @pl.when(i == 0)
        def _():
            kprev_ref[...] = jnp.zeros_like(kprev_ref)
            vprev_ref[...] = jnp.zeros_like(vprev_ref)
            state_ref[...] = jnp.zeros_like(state_ref)
            loss_ref[...] = jnp.zeros_like(loss_ref)
            gfin_ref[...] = jnp.zeros_like(gfin_ref)

        def sub(j, carry):
            rows = pl.ds(pl.multiple_of(j * BLK, BLK), BLK)
            not_first = jnp.logical_or(i > 0, j > 0)
            mask = _attn_mask(not_first)

            aq = p_ref[rows, OFF_AQ:OFF_AQ + ATT_WIDTH]
            ak = p_ref[rows, OFF_AK:OFF_AK + ATT_KV_WIDTH]
            av = p_ref[rows, OFF_AV:OFF_AV + ATT_KV_WIDTH]
            az = p_ref[rows, OFF_AZ:OFF_AZ + ATT_WIDTH]
            kp, vp = kprev_ref[...], vprev_ref[...]
            outs = []
            for g in range(ATT_KV_HEADS):
                lo, hi = g * ATT_HEAD_DIM, (g + 1) * ATT_HEAD_DIM
                kk = jnp.concatenate([kp[:, lo:hi], ak[:, lo:hi]], axis=0).astype(BF16)
                vv = jnp.concatenate([vp[:, lo:hi], av[:, lo:hi]], axis=0).astype(BF16)
                qg = _heads_to_rows(aq, g * ATT_GROUP, ATT_GROUP, ATT_HEAD_DIM).astype(BF16)
                _, _, og = _attn_group_fwd(qg, kk, vv, _sink_column(sinks_ref, g), mask)
                outs += _group_to_lanes(og)
            kprev_ref[...] = ak
            vprev_ref[...] = av
            o_att = jnp.concatenate(outs, axis=1)
            mix_ref[rows, 0:ATT_WIDTH] = (o_att * (az * _sigmoid(az))).astype(BF16)

            cos_b, sin_b = cos_ref[rows, :], sin_ref[rows, :]
            qr = _rotate(p_ref[rows, OFF_RQ:OFF_RQ + RET_QK_WIDTH], cos_b, sin_b)
            kr = _rotate(p_ref[rows, OFF_RK:OFF_RK + RET_QK_WIDTH], cos_b, sin_b) * RET_SCALE
            qd = qr * qdec_ref[...]
            kd = kr * kdec_ref[...]
            for h in range(RET_HEADS):
                qs = slice(h * RET_QK_DIM, (h + 1) * RET_QK_DIM)
                vs = slice(OFF_RV + h * RET_V_DIM, OFF_RV + (h + 1) * RET_V_DIM)
                zs = slice(OFF_RZ + h * RET_V_DIM, OFF_RZ + (h + 1) * RET_V_DIM)
                vh = p_ref[rows, vs].astype(BF16)
                sc = _dot_nt(qr[:, qs].astype(BF16), kr[:, qs].astype(BF16)) * din_ref[h]
                state = state_ref[h]
                o = _dot(sc.astype(BF16), vh) + _dot(qd[:, qs].astype(BF16), state.astype(BF16))
                st_ref[j, h] = state
                state_ref[h] = state * cd_ref[h] + _dot_tn(kd[:, qs].astype(BF16), vh)
                on, _ = _group_norm(o)
                rz = p_ref[rows, zs]
                r = (on * gain_ref[:, h * RET_V_DIM:(h + 1) * RET_V_DIM]) * (rz * _sigmoid(rz))
                mix_ref[rows, ATT_WIDTH + h * RET_V_DIM:ATT_WIDTH + (h + 1) * RET_V_DIM] = r.astype(BF16)
            return carry

        lax.fori_loop(0, nsub, sub, 0)

        xo = x_ref[...] + _dot(mix_ref[...], wout_ref[...])
        r2 = lax.rsqrt(jnp.mean(xo * xo, axis=1, keepdims=True) + RMS_EPS)
        xn = xo * r2
        err = xn * fg_ref[...] - t_ref[...]
        loss_ref[...] += jnp.sum(err * err) * (0.5 / D_MODEL)
        dy = err * (1.0 / D_MODEL)
        gfin_ref[...] += jnp.sum(dy * xn, axis=0, keepdims=True)
        u = dy * fg_ref[...]
        dxo_ref[...] = r2 * u - xn * (r2 * jnp.mean(u * xn, axis=1, keepdims=True))

    blk_rows = lambda w: pl.BlockSpec((tb, w), lambda i: (i, 0))
    return _call(
        body, name="mix_fwd", grid=(seq // tb,),
        out_shape=(
            jax.ShapeDtypeStruct((seq, MIX_WIDTH), BF16),
            jax.ShapeDtypeStruct((seq, D_MODEL), F32),
            jax.ShapeDtypeStruct((seq // BLK, RET_HEADS, RET_QK_DIM, RET_V_DIM), F32),
            jax.ShapeDtypeStruct((8, LANE), F32),
            jax.ShapeDtypeStruct((1, D_MODEL), F32),
        ),
        in_specs=[
            blk_rows(IN_WIDTH), blk_rows(D_MODEL), blk_rows(D_MODEL), blk_rows(RET_QK_WIDTH), blk_rows(RET_QK_WIDTH),
            _const_spec((MIX_WIDTH, D_MODEL)), _const_spec((1, D_MODEL)), _const_spec((1, RET_WIDTH)),
            pl.BlockSpec(memory_space=pltpu.SMEM),
            _const_spec((RET_HEADS, BLK, BLK)), _const_spec((BLK, RET_QK_WIDTH)), _const_spec((BLK, RET_QK_WIDTH)),
            _const_spec((RET_HEADS, RET_QK_DIM, RET_V_DIM)),
        ],
        out_specs=(
            blk_rows(MIX_WIDTH), blk_rows(D_MODEL),
            pl.BlockSpec((nsub, RET_HEADS, RET_QK_DIM, RET_V_DIM), lambda i: (i, 0, 0, 0)),
            _const_spec((8, LANE)), _const_spec((1, D_MODEL)),
        ),
        scratch_shapes=[
            pltpu.VMEM((BLK, ATT_KV_WIDTH), F32), pltpu.VMEM((BLK, ATT_KV_WIDTH), F32),
            pltpu.VMEM((RET_HEADS, RET_QK_DIM, RET_V_DIM), F32),
        ],
        compiler_params=_params(48, ("arbitrary",)),
    )(proj, x, target, cos_t, sin_t, w_out, final_g, gn_gain, sinks, decay_in, qdec_t, kdec_t, cd_t)


def _mix_bwd(proj, dxo, mix, states, w_out, gn_gain, sinks, tables, tb):
    seq = dxo.shape[0]
    nsub = tb // BLK
    nblk = seq // tb
    cos_t, sin_t, decay_in, qdec_t, kdec_t, cd_t = tables
    kv_cols = OFF_AK // (2 * ATT_KV_WIDTH)

    def body(p_ref, pkv_ref, dxo_ref, mix_ref, st_ref, cos_ref, sin_ref, wout_ref, gain_ref, sinks_ref, din_ref,
             qdec_ref, kdec_ref, cd_ref, dp_ref, gwout_ref, dgain_ref, dsink_ref,
             dmix_ref, kv_ref, dkc_ref, dvc_ref, gst_ref):
        i = pl.program_id(0)
        blk = nblk - 1 - i

        @pl.when(i == 0)
        def _():
            gwout_ref[...] = jnp.zeros_like(gwout_ref)
            dgain_ref[...] = jnp.zeros_like(dgain_ref)
            dsink_ref[...] = jnp.zeros_like(dsink_ref)
            dkc_ref[...] = jnp.zeros_like(dkc_ref)
            dvc_ref[...] = jnp.zeros_like(dvc_ref)
            gst_ref[...] = jnp.zeros_like(gst_ref)

        dxo_b = dxo_ref[...].astype(BF16)
        dmix_ref[...] = _dot_nt(dxo_b, wout_ref[...])
        gwout_ref[...] += _dot_tn(mix_ref[...], dxo_b)
        kv_ref[0:BLK, :] = pkv_ref[...]
        kv_ref[BLK:, :] = p_ref[:, OFF_AK:OFF_AK + 2 * ATT_KV_WIDTH]

        def sub(jj, carry):
            j = nsub - 1 - jj
            rows = pl.ds(pl.multiple_of(j * BLK, BLK), BLK)
            rows_next = pl.ds(pl.multiple_of(j * BLK + BLK, BLK), BLK)
            not_first = jnp.logical_or(blk > 0, j > 0)
            mask = _attn_mask(not_first)

            aq = p_ref[rows, OFF_AQ:OFF_AQ + ATT_WIDTH]
            az = p_ref[rows, OFF_AZ:OFF_AZ + ATT_WIDTH]
            kvp = kv_ref[rows, :]
            kvc = kv_ref[rows_next, :]
            da = dmix_ref[rows, 0:ATT_WIDTH]
            sig = _sigmoid(az)
            d_o = da * (az * sig)
            outs, dqs, dk_prev, dk_cur, dv_prev, dv_cur = [], [], [], [], [], []
            dsink = jnp.zeros((1, LANE), F32)
            lane = lax.broadcasted_iota(jnp.int32, (1, LANE), 1)
            for g in range(ATT_KV_HEADS):
                lo, hi = g * ATT_HEAD_DIM, (g + 1) * ATT_HEAD_DIM
                kk = jnp.concatenate([kvp[:, lo:hi], kvc[:, lo:hi]], axis=0).astype(BF16)
                vv = jnp.concatenate([kvp[:, ATT_KV_WIDTH + lo:ATT_KV_WIDTH + hi],
                                      kvc[:, ATT_KV_WIDTH + lo:ATT_KV_WIDTH + hi]], axis=0).astype(BF16)
                qg = _heads_to_rows(aq, g * ATT_GROUP, ATT_GROUP, ATT_HEAD_DIM).astype(BF16)
                p, p_sink, og = _attn_group_fwd(qg, kk, vv, _sink_column(sinks_ref, g), mask)
                outs += _group_to_lanes(og)
                dog = _heads_to_rows(d_o, g * ATT_GROUP, ATT_GROUP, ATT_HEAD_DIM).astype(BF16)
                dpr = _dot_nt(dog, vv)
                delta = jnp.sum(p * dpr, axis=1, keepdims=True)
                ds = (p * (dpr - delta) * ATT_SCALE).astype(BF16)
                dqs += _group_to_lanes(_dot(ds, kk))
                dkk = _dot_tn(ds, qg)
                dvv = _dot_tn(p.astype(BF16), dog)
                dk_prev.append(dkk[0:BLK]); dk_cur.append(dkk[BLK:])
                dv_prev.append(dvv[0:BLK]); dv_cur.append(dvv[BLK:])
                sink_rows = p_sink * delta
                for h in range(ATT_GROUP):
                    tot = jnp.sum(sink_rows[h * BLK:(h + 1) * BLK, :], axis=0, keepdims=True)
                    dsink = dsink - jnp.where(lane == g * ATT_GROUP + h, tot, 0.0)
            dsink_ref[...] += dsink
            o_att = jnp.concatenate(outs, axis=1)
            dp_ref[rows, OFF_AQ:OFF_AQ + ATT_WIDTH] = jnp.concatenate(dqs, axis=1).astype(BF16)
            dp_ref[rows, OFF_AZ:OFF_AZ + ATT_WIDTH] = (da * o_att * (sig * (1.0 + az * (1.0 - sig)))).astype(BF16)
            dp_ref[rows, OFF_AK:OFF_AK + ATT_KV_WIDTH] = (jnp.concatenate(dk_cur, axis=1) + dkc_ref[...]).astype(BF16)
            dp_ref[rows, OFF_AV:OFF_AV + ATT_KV_WIDTH] = (jnp.concatenate(dv_cur, axis=1) + dvc_ref[...]).astype(BF16)
            dkc_ref[...] = jnp.concatenate(dk_prev, axis=1)
            dvc_ref[...] = jnp.concatenate(dv_prev, axis=1)

            cos_b, sin_b = cos_ref[rows, :], sin_ref[rows, :]
            qr = _rotate(p_ref[rows, OFF_RQ:OFF_RQ + RET_QK_WIDTH], cos_b, sin_b)
            kr = _rotate(p_ref[rows, OFF_RK:OFF_RK + RET_QK_WIDTH], cos_b, sin_b) * RET_SCALE
            qdec, kdec = qdec_ref[...], kdec_ref[...]
            qd = qr * qdec
            kd = kr * kdec
            dq_parts, dk_parts = [], []
            for h in range(RET_HEADS):
                qs = slice(h * RET_QK_DIM, (h + 1) * RET_QK_DIM)
                gs = slice(h * RET_V_DIM, (h + 1) * RET_V_DIM)
                vs = slice(OFF_RV + h * RET_V_DIM, OFF_RV + (h + 1) * RET_V_DIM)
                zs = slice(OFF_RZ + h * RET_V_DIM, OFF_RZ + (h + 1) * RET_V_DIM)
                vh = p_ref[rows, vs].astype(BF16)
                qh, kh = qr[:, qs].astype(BF16), kr[:, qs].astype(BF16)
                qdh, kdh = qd[:, qs].astype(BF16), kd[:, qs].astype(BF16)
                din = din_ref[h]
                sc = (_dot_nt(qh, kh) * din).astype(BF16)
                state = st_ref[j, h].astype(BF16)
                o = _dot(sc, vh) + _dot(qdh, state)
                on, rstd = _group_norm(o)
                rz = p_ref[rows, zs]
                sig_r = _sigmoid(rz)
                gain = gain_ref[:, gs]
                dr = dmix_ref[rows, ATT_WIDTH + h * RET_V_DIM:ATT_WIDTH + (h + 1) * RET_V_DIM]
                dgn = dr * (rz * sig_r)
                dp_ref[rows, zs] = (dr * (on * gain) * (sig_r * (1.0 + rz * (1.0 - sig_r)))).astype(BF16)
                dgain_ref[:, gs] += jnp.sum(dgn * on, axis=0, keepdims=True)
                don = dgn * gain
                d_o = rstd * (don - jnp.mean(don, axis=1, keepdims=True)
                              - on * jnp.mean(don * on, axis=1, keepdims=True))
                dob = d_o.astype(BF16)
                gst = gst_ref[h]
                gstb = gst.astype(BF16)
                da_sc = (_dot_nt(dob, vh) * din).astype(BF16)
                dp_ref[rows, vs] = (_dot_tn(sc, dob) + _dot(kdh, gstb)).astype(BF16)
                dq_parts.append(_dot(da_sc, kh) + _dot_nt(dob, state) * qdec[:, qs])
                dk_parts.append(_dot_tn(da_sc, qh) + _dot_nt(vh, gstb) * kdec[:, qs])
                gst_ref[h] = _dot_tn(qdh, dob) + gst * cd_ref[h]
            dqr = jnp.concatenate(dq_parts, axis=1)
            dkr = jnp.concatenate(dk_parts, axis=1) * RET_SCALE
            dp_ref[rows, OFF_RQ:OFF_RQ + RET_QK_WIDTH] = _rotate_transposed(dqr, cos_b, sin_b).astype(BF16)
            dp_ref[rows, OFF_RK:OFF_RK + RET_QK_WIDTH] = _rotate_transposed(dkr, cos_b, sin_b).astype(BF16)
            return carry

        lax.fori_loop(0, nsub, sub, 0)

    rev_rows = lambda w: pl.BlockSpec((tb, w), lambda i: (nblk - 1 - i, 0))
    prev_kv = pl.BlockSpec((BLK, 2 * ATT_KV_WIDTH), lambda i: (jnp.maximum((nblk - 1 - i) * nsub - 1, 0), kv_cols))
    return _call(
        body, name="mix_bwd", grid=(nblk,),
        out_shape=(
            jax.ShapeDtypeStruct((seq, IN_WIDTH), BF16),
            jax.ShapeDtypeStruct((MIX_WIDTH, D_MODEL), F32),
            jax.ShapeDtypeStruct((1, RET_WIDTH), F32),
            jax.ShapeDtypeStruct((1, LANE), F32),
        ),
        in_specs=[
            rev_rows(IN_WIDTH), prev_kv, rev_rows(D_MODEL), rev_rows(MIX_WIDTH),
            pl.BlockSpec((nsub, RET_HEADS, RET_QK_DIM, RET_V_DIM), lambda i: (nblk - 1 - i, 0, 0, 0)),
            rev_rows(RET_QK_WIDTH), rev_rows(RET_QK_WIDTH),
            _const_spec((MIX_WIDTH, D_MODEL)), _const_spec((1, RET_WIDTH)),
            pl.BlockSpec(memory_space=pltpu.SMEM),
            _const_spec((RET_HEADS, BLK, BLK)), _const_spec((BLK, RET_QK_WIDTH)), _const_spec((BLK, RET_QK_WIDTH)),
            _const_spec((RET_HEADS, RET_QK_DIM, RET_V_DIM)),
        ],
        out_specs=(
            rev_rows(IN_WIDTH), _const_spec((MIX_WIDTH, D_MODEL)), _const_spec((1, RET_WIDTH)), _const_spec((1, LANE)),
        ),
        scratch_shapes=[
            pltpu.VMEM((tb, MIX_WIDTH), F32),
            pltpu.VMEM((tb + BLK, 2 * ATT_KV_WIDTH), F32),
            pltpu.VMEM((BLK, ATT_KV_WIDTH), F32), pltpu.VMEM((BLK, ATT_KV_WIDTH), F32),
            pltpu.VMEM((RET_HEADS, RET_QK_DIM, RET_V_DIM), F32),
        ],
        compiler_params=_params(56, ("arbitrary",)),
    )(proj, proj, dxo, mix, states, cos_t, sin_t, w_out, gn_gain, sinks, decay_in, qdec_t, kdec_t, cd_t)


def _in_proj_bwd(dproj, x, dxo, norm_g, w_in, tb):
    seq = x.shape[0]

    def body(dp_ref, x_ref, dxo_ref, g_ref, w_ref, gx_ref, gw_ref, gnorm_ref):
        i = pl.program_id(0)

        @pl.when(i == 0)
        def _():
            gw_ref[...] = jnp.zeros_like(gw_ref)
            gnorm_ref[...] = jnp.zeros_like(gnorm_ref)

        dp = dp_ref[...]
        xv = x_ref[...]
        r = lax.rsqrt(jnp.mean(xv * xv, axis=1, keepdims=True) + RMS_EPS)
        xn = xv * r
        h = xn * g_ref[...]
        gw_ref[...] += _dot_tn(h.astype(BF16), dp)
        dh = _dot_nt(dp, w_ref[...])
        gnorm_ref[...] += jnp.sum(dh * xn, axis=0, keepdims=True)
        u = dh * g_ref[...]
        gx_ref[...] = dxo_ref[...] + r * u - xn * (r * jnp.mean(u * xn, axis=1, keepdims=True))

    rows = lambda w: pl.BlockSpec((tb, w), lambda i: (i, 0))
    return _call(
        body, name="in_proj_bwd", grid=(seq // tb,),
        out_shape=(
            jax.ShapeDtypeStruct((seq, D_MODEL), F32),
            jax.ShapeDtypeStruct((D_MODEL, IN_WIDTH), F32),
            jax.ShapeDtypeStruct((1, D_MODEL), F32),
        ),
        in_specs=[rows(IN_WIDTH), rows(D_MODEL), rows(D_MODEL), _const_spec((1, D_MODEL)),
                  _const_spec((D_MODEL, IN_WIDTH))],
        out_specs=(rows(D_MODEL), _const_spec((D_MODEL, IN_WIDTH)), _const_spec((1, D_MODEL))),
        compiler_params=_params(60, ("arbitrary",)),
    )(dproj, x, dxo, norm_g, w_in)


def _grads_reduce(gw_in, gw_out, pack):
    half_in = D_MODEL // 2
    half_out = SHARD_OUT // 2
    A_IN, A_OUT, B_IN, B_OUT, C_IN, C_OUT, PACK, N_SEMS = 0, 1, 5, 9, 13, 14, 15, 23

    def body(gwi_hbm, gwo_hbm, pack_ref, fin_ref, fout_ref, packsum_ref,
             mine_in, sib_in, mine_out, sib_out, b_in, b_out, packs, send_sems, recv_sems, local_sems):
        x, y, c = lax.axis_index("x"), lax.axis_index("y"), lax.axis_index("c")
        chip = 2 * x + y
        dev = 2 * chip + c
        sibling = (x, y, 1 - c)

        def remote(src, dst, send_k, recv_k, to):
            return pltpu.make_async_remote_copy(src_ref=src, dst_ref=dst, send_sem=send_sems.at[send_k],
                                                recv_sem=recv_sems.at[recv_k], device_id=to, device_id_type=MESH_ID)

        def in_rows(core):
            return pl.ds(pl.multiple_of(core * half_in, half_in), half_in)

        def out_rows(j, core):
            return pl.ds(pl.multiple_of(j * SHARD_OUT + core * half_out, half_out), half_out)

        packs[dev] = pack_ref[...]
        for d in range(N_DEV):
            to = (d // 4, (d // 2) % 2, d % 2)

            @pl.when(d != dev)
            def _():
                remote(pack_ref, packs.at[dev], PACK + d, PACK + dev, to).start()

        local = [pltpu.make_async_copy(gwi_hbm.at[in_rows(c), :], mine_in, local_sems.at[0])]
        local += [pltpu.make_async_copy(gwo_hbm.at[out_rows(j, c), :], mine_out.at[j], local_sems.at[1 + j])
                  for j in range(N_CHIPS)]
        for cp in local:
            cp.start()
        stage_a = [remote(gwi_hbm.at[in_rows(1 - c), :], sib_in, A_IN, A_IN, sibling)]
        stage_a += [remote(gwo_hbm.at[out_rows(j, 1 - c), :], sib_out.at[j], A_OUT + j, A_OUT + j, sibling)
                    for j in range(N_CHIPS)]
        for cp in stage_a:
            cp.start()
        for cp in local:
            cp.wait()
        for cp in stage_a:
            cp.wait_recv()

        def add_rows(k, carry):
            rows = pl.ds(pl.multiple_of(k * 64, 64), 64)
            mine_in[rows, :] = mine_in[rows, :] + sib_in[rows, :]
            return carry

        lax.fori_loop(0, half_in // 64, add_rows, 0)
        for j in range(N_CHIPS):
            mine_out[j] = mine_out[j] + sib_out[j]

        for j in range(N_CHIPS):
            window = mine_in.at[:, WIN_START[j]:WIN_START[j] + SHARD_PAD]
            to = (j // 2, j % 2, c)

            @pl.when(j != chip)
            def _():
                remote(window, b_in.at[chip], B_IN + j, B_IN + chip, to).start()
                remote(mine_out.at[j], b_out.at[chip], B_OUT + j, B_OUT + chip, to).start()

            @pl.when(j == chip)
            def _():
                b_in[j] = mine_in[:, WIN_START[j]:WIN_START[j] + SHARD_PAD]
                b_out[j] = mine_out[j]

        for j in range(N_CHIPS):
            @pl.when(j != chip)
            def _():
                remote(b_in.at[j], b_in.at[j], B_IN + j, B_IN + j, sibling).wait_recv()
                remote(b_out.at[j], b_out.at[j], B_OUT + j, B_OUT + j, sibling).wait_recv()

        fin_ref[in_rows(c), :] = (b_in[0] + b_in[1]) + (b_in[2] + b_in[3])
        fout_ref[pl.ds(pl.multiple_of(c * half_out, half_out), half_out), :] = (b_out[0] + b_out[1]) + (b_out[2] + b_out[3])
        mine_half = fin_ref.at[in_rows(c), :]
        mine_half_out = fout_ref.at[pl.ds(pl.multiple_of(c * half_out, half_out), half_out), :]
        stage_c = [remote(mine_half, mine_half, C_IN, C_IN, sibling),
                   remote(mine_half_out, mine_half_out, C_OUT, C_OUT, sibling)]
        for cp in stage_c:
            cp.start()
        other_half = fin_ref.at[in_rows(1 - c), :]
        other_half_out = fout_ref.at[pl.ds(pl.multiple_of((1 - c) * half_out, half_out), half_out), :]
        remote(other_half, other_half, C_IN, C_IN, sibling).wait_recv()
        remote(other_half_out, other_half_out, C_OUT, C_OUT, sibling).wait_recv()

        for d in range(N_DEV):
            @pl.when(d != dev)
            def _():
                remote(pack_ref, packs.at[d], PACK + d, PACK + d, sibling).wait_recv()
        total = packs[0]
        for d in range(1, N_DEV):
            total = total + packs[d]
        packsum_ref[...] = total

        for cp in stage_a + stage_c:
            cp.wait_send()
        for j in range(N_CHIPS):
            @pl.when(j != chip)
            def _():
                remote(b_in.at[j], b_in.at[j], B_IN + j, B_IN + j, sibling).wait_send()
                remote(b_out.at[j], b_out.at[j], B_OUT + j, B_OUT + j, sibling).wait_send()
        for d in range(N_DEV):
            @pl.when(d != dev)
            def _():
                remote(pack_ref, packs.at[d], PACK + d, PACK + d, sibling).wait_send()

    vmem = pl.BlockSpec(memory_space=pltpu.VMEM)
    hbm = pl.BlockSpec(memory_space=pl.ANY)
    return _call(
        body, name="grads_reduce",
        out_shape=(jax.ShapeDtypeStruct((D_MODEL, SHARD_PAD), F32), jax.ShapeDtypeStruct((SHARD_OUT, D_MODEL), F32),
                   jax.ShapeDtypeStruct((PACK_ROWS, D_MODEL), F32)),
        in_specs=[hbm, hbm, vmem], out_specs=(vmem, vmem, vmem),
        scratch_shapes=[
            pltpu.VMEM((half_in, IN_WIDTH), F32), pltpu.VMEM((half_in, IN_WIDTH), F32),
            pltpu.VMEM((N_CHIPS, half_out, D_MODEL), F32), pltpu.VMEM((N_CHIPS, half_out, D_MODEL), F32),
            pltpu.VMEM((N_CHIPS, half_in, SHARD_PAD), F32), pltpu.VMEM((N_CHIPS, half_out, D_MODEL), F32),
            pltpu.VMEM((N_DEV, PACK_ROWS, D_MODEL), F32),
            pltpu.SemaphoreType.DMA((N_SEMS,)), pltpu.SemaphoreType.DMA((N_SEMS,)),
            pltpu.SemaphoreType.DMA((1 + N_CHIPS,)),
        ],
        compiler_params=_params(48),
    )(gw_in, gw_out, pack)


def _adamw(name, w, g, m, v, tb, unshift):
    rows, cols = w.shape

    def body(w_ref, g_ref, m_ref, v_ref, go_ref, d_ref, mo_ref, vo_ref):
        gv = g_ref[...]
        if unshift:
            gv = pltpu.roll(gv, lax.axis_index("y") * (cols - SHARD_SHIFT), 1)
        mn = ADAM_B1 * m_ref[...] + (1.0 - ADAM_B1) * gv
        vn = ADAM_B2 * v_ref[...] + (1.0 - ADAM_B2) * (gv * gv)
        m_hat = mn / (1.0 - ADAM_B1 ** ADAM_STEP)
        v_hat = vn / (1.0 - ADAM_B2 ** ADAM_STEP)
        go_ref[...] = gv
        d_ref[...] = -ADAM_LR * (m_hat / (jnp.sqrt(v_hat) + ADAM_EPS) + ADAM_WD * w_ref[...])
        mo_ref[...] = mn
        vo_ref[...] = vn

    spec = pl.BlockSpec((tb, cols), lambda i: (i, 0))
    shape = jax.ShapeDtypeStruct((rows, cols), F32)
    return _call(
        body, name=name, grid=(rows // tb,), out_shape=(shape,) * 4,
        in_specs=[spec] * 4, out_specs=(spec,) * 4,
        compiler_params=_params(32, ("arbitrary",)),
    )(w, g, m, v)


def _pad_cols(a, width):
    return jnp.pad(a, ((0, 0), (0, width - a.shape[1])))


PACK_PARTS = 5
PACK_ROWS = 8 * PACK_PARTS


def _pack_rows(rows):
    parts = [jnp.pad(r.reshape(1, -1), ((0, 7), (0, D_MODEL - r.size))) for r in rows]
    parts += [jnp.zeros((8, D_MODEL), F32)] * (PACK_PARTS - len(rows))
    return jnp.concatenate(parts, axis=0)


def kernel(x, norm_g, w_in, att_sinks, ret_gn_g, w_out, final_g, loss_target, m_norm_g, m_w_in, m_att_sinks, m_ret_gn_g, m_w_out, m_final_g, v_norm_g, v_w_in, v_att_sinks, v_ret_gn_g, v_w_out, v_final_g):
    seq = x.shape[1]
    xs, tgt = x[0], loss_target[0]
    final_g2 = final_g.reshape(1, D_MODEL)
    tables = _tables(seq)

    w_in_full, w_out_full = _weights_allgather(_pad_cols(w_in[0], SHARD_PAD), w_out[0])
    proj = _in_proj(xs, norm_g, w_in_full, min(512, seq))
    mix, dxo, states, loss_part, gfin = _mix_fwd(proj, xs, tgt, w_out_full, final_g2, ret_gn_g, att_sinks, tables,
                                                 min(256, seq))
    dproj, gw_out, dgain, dsink = _mix_bwd(proj, dxo, mix, states, w_out_full, ret_gn_g, att_sinks, tables,
                                           min(256, seq))
    grad_x, gw_in, gnorm = _in_proj_bwd(dproj, xs, dxo, norm_g, w_in_full, min(256, seq))

    pack = _pack_rows([gnorm, gfin, dgain, dsink, loss_part[0:1, :]])
    g_in, g_out, sums = _grads_reduce(gw_in, gw_out, pack)

    small_w = _pack_rows([norm_g, final_g, ret_gn_g, att_sinks])
    small_m = _pack_rows([m_norm_g, m_final_g, m_ret_gn_g, m_att_sinks])
    small_v = _pack_rows([v_norm_g, v_final_g, v_ret_gn_g, v_att_sinks])
    res_in = _adamw("adamw_w_in", _pad_cols(w_in[0], SHARD_PAD), g_in, _pad_cols(m_w_in[0], SHARD_PAD),
                    _pad_cols(v_w_in[0], SHARD_PAD), 256, True)
    res_out = _adamw("adamw_w_out", w_out[0], g_out, m_w_out[0], v_w_out[0], SHARD_OUT, False)
    res_small = _adamw("adamw_small", small_w, sums, small_m, small_v, PACK_ROWS, False)

    outs = []
    for k in range(4):
        a = res_small[k]
        outs.append((a[0:1, :], res_in[k][None, :, :SHARD_IN], a[24:25, :ATT_HEADS], a[16:17, :RET_WIDTH],
                     res_out[k][None], a[8, :]))
    loss = sums[32, 0]
    return (loss, grad_x[None], *outs[0], *outs[1], *outs[2], *outs[3])
```

```python
import jax
import jax.numpy as jnp
from jax import lax
from jax.experimental import pallas as pl
from jax.experimental.pallas import tpu as pltpu

F32 = jnp.float32
BF16 = jnp.bfloat16

D_MODEL = 1024
ATT_HEADS = 8
ATT_KV_HEADS = 2
ATT_HEAD_DIM = 64
RET_HEADS = 4
RET_QK_DIM = 64
RET_V_DIM = 128
BLK = 128
ROT_BASE = 10000.0
RMS_EPS = 1e-6
GN_EPS = 1e-6
NEG_INF = -1e30
ATT_SCALE = ATT_HEAD_DIM ** -0.5
RET_SCALE = RET_QK_DIM ** -0.5

ATT_WIDTH = ATT_HEADS * ATT_HEAD_DIM
ATT_KV_WIDTH = ATT_KV_HEADS * ATT_HEAD_DIM
RET_QK_WIDTH = RET_HEADS * RET_QK_DIM
RET_WIDTH = RET_HEADS * RET_V_DIM
MIX_WIDTH = ATT_WIDTH + RET_WIDTH
OFF_AQ = 0
OFF_AK = OFF_AQ + ATT_WIDTH
OFF_AV = OFF_AK + ATT_KV_WIDTH
OFF_AZ = OFF_AV + ATT_KV_WIDTH
OFF_RQ = OFF_AZ + ATT_WIDTH
OFF_RK = OFF_RQ + RET_QK_WIDTH
OFF_RV = OFF_RK + RET_QK_WIDTH
OFF_RZ = OFF_RV + RET_WIDTH
IN_WIDTH = OFF_RZ + RET_WIDTH

LANE = 128
HALF_LANE = LANE // 2
PAIRS = RET_QK_WIDTH // LANE
assert ATT_HEAD_DIM == HALF_LANE and RET_QK_DIM == HALF_LANE and RET_V_DIM == LANE and ATT_KV_WIDTH == LANE

N_CHIPS = 4
N_DEV = 8
SHARD_IN = IN_WIDTH // N_CHIPS
SHARD_PAD = 768
SHARD_SHIFT = SHARD_PAD - SHARD_IN
WIN_START = tuple((j * SHARD_IN) // LANE * LANE for j in range(N_CHIPS))
SHARD_OUT = MIX_WIDTH // N_CHIPS

PACK_PARTS = 5
PACK_ROWS = 8 * PACK_PARTS

ADAM_LR = 0.001
ADAM_B1 = 0.9
ADAM_B2 = 0.999
ADAM_EPS = 1e-08
ADAM_WD = 0.01
ADAM_STEP = 10

VMEM_CAP = 64 * 1024 * 1024
MESH_ID = pl.DeviceIdType.MESH


def _call(body, **kw):
    return pl.pallas_call(body, **kw)


def _params(vmem_mb, semantics=None):
    assert vmem_mb * 1024 * 1024 < VMEM_CAP
    return pltpu.CompilerParams(dimension_semantics=semantics, vmem_limit_bytes=vmem_mb * 1024 * 1024)


def _dot(a, b):
    return jnp.dot(a, b, preferred_element_type=F32)


def _dot_nt(a, b):
    return lax.dot_general(a, b, (((1,), (1,)), ((), ())), preferred_element_type=F32)


def _dot_tn(a, b):
    return lax.dot_general(a, b, (((0,), (0,)), ((), ())), preferred_element_type=F32)


def _sigmoid(z):
    return 1.0 / (1.0 + jnp.exp(-z))


def _const_spec(shape):
    nd = len(shape)
    return pl.BlockSpec(shape, lambda i: (0,) * nd)


def _tables(seq):
    pos = jnp.arange(seq, dtype=F32)
    theta = 1.0 / (ROT_BASE ** jnp.linspace(0.0, 1.0, RET_QK_DIM // 2, dtype=F32))
    ang = pos[:, None] * theta[None, :]
    cos, sin = jnp.cos(ang), jnp.sin(ang)
    cos2 = jnp.repeat(cos, 2, axis=1)
    sin2 = jnp.stack([-sin, sin], axis=-1).reshape(seq, RET_QK_DIM)
    cos_t = jnp.tile(cos2, (1, RET_HEADS))
    sin_t = jnp.tile(sin2, (1, RET_HEADS))

    log_gamma = jnp.log(1.0 - 2.0 ** (-5.0 - jnp.arange(RET_HEADS, dtype=F32)))
    idx = jnp.arange(BLK, dtype=F32)
    rel = idx[:, None] - idx[None, :]
    decay_in = jnp.where(rel >= 0, jnp.exp(log_gamma[:, None, None] * jnp.maximum(rel, 0.0)), 0.0)
    k_dec = jnp.exp(log_gamma[:, None] * (BLK - 1 - idx)[None, :])
    q_dec = jnp.exp(log_gamma[:, None] * (idx + 1)[None, :])
    chunk_decay = jnp.exp(log_gamma * BLK)
    kdec_t = jnp.repeat(k_dec.T, RET_QK_DIM, axis=1)
    qdec_t = jnp.repeat(q_dec.T, RET_QK_DIM, axis=1)
    cd_t = jnp.broadcast_to(chunk_decay[:, None, None], (RET_HEADS, RET_QK_DIM, RET_V_DIM))
    decay_in = decay_in.reshape(PAIRS, 2 * BLK, BLK)
    cd_t = cd_t.reshape(PAIRS, 2 * RET_QK_DIM, RET_V_DIM)

    key = jnp.arange(2 * BLK)[:, None]
    query = jnp.arange(2 * BLK)[None, :] % BLK
    diff = query + BLK - key
    valid = (diff >= 0) & (diff < BLK)
    bias = jnp.stack([jnp.where(valid & (key >= BLK), 0.0, NEG_INF), jnp.where(valid, 0.0, NEG_INF)]).astype(F32)
    return cos_t, sin_t, decay_in, qdec_t, kdec_t, cd_t, bias


def _low_lanes(shape):
    lane = lax.broadcasted_iota(jnp.int32, shape, len(shape) - 1)
    return (lane & HALF_LANE) == 0


def _split_heads(t):
    low = _low_lanes(t.shape)
    zero = jnp.zeros_like(t)
    return jnp.where(low, t, zero), jnp.where(low, zero, t)


def _swap_pairs(t):
    lane = lax.broadcasted_iota(jnp.int32, t.shape, 1)
    nxt = pltpu.roll(t, t.shape[1] - 1, 1)
    prv = pltpu.roll(t, 1, 1)
    return jnp.where((lane & 1) == 0, nxt, prv)


def _rotate(t, cos_t, sin_t):
    return t * cos_t + _swap_pairs(t) * sin_t


def _rotate_transposed(d, cos_t, sin_t):
    return d * cos_t + _swap_pairs(d * sin_t)


def _kv_operands(cat):
    low = _low_lanes(cat.shape)
    swapped = pltpu.roll(cat, HALF_LANE, 1)
    zero = jnp.zeros_like(cat)
    pick = lambda a, b: jnp.where(low, a, b).astype(BF16)
    return ((pick(cat, zero), pick(zero, swapped)), (pick(swapped, zero), pick(zero, cat)))


def _stack_tiles(t, first_tile):
    a = t[:, first_tile * LANE:(first_tile + 1) * LANE]
    b = t[:, (first_tile + 1) * LANE:(first_tile + 2) * LANE]
    return jnp.concatenate([a, b], axis=0)


def _sink_rows(sinks_ref, group):
    first = lax.broadcasted_iota(jnp.int32, (1, 2 * BLK), 1) < BLK

    def row(h0, h1):
        return jnp.where(first, sinks_ref[0, group * 4 + h0], sinks_ref[0, group * 4 + h1])
    return row(0, 2), row(1, 3)


def _attn_group(qs, k_ops, v_ops, bias, sink_rows):
    probs = []
    for k_op, sink_row in zip(k_ops, sink_rows):
        s = _dot_nt(k_op, qs) + bias
        m = jnp.maximum(jnp.max(s, axis=0, keepdims=True), sink_row)
        e = jnp.exp(s - m)
        es = jnp.exp(sink_row - m)
        inv = 1.0 / (jnp.sum(e, axis=0, keepdims=True) + es)
        probs.append((e * inv, es * inv))
    p_cat = jnp.concatenate([probs[0][0].astype(BF16), probs[1][0].astype(BF16)], axis=0)
    o = _dot_tn(p_cat, jnp.concatenate(v_ops, axis=0))
    return probs, p_cat, o


def _group_norm(o):
    mu = jnp.mean(o, axis=1, keepdims=True)
    xc = o - mu
    var = jnp.mean(xc * xc, axis=1, keepdims=True)
    rstd = lax.rsqrt(var + GN_EPS)
    return xc * rstd, rstd


def _retention_operands(p_ref, rows, cos_b, sin_b, qdec, kdec):
    qr = _rotate(p_ref[rows, OFF_RQ:OFF_RQ + RET_QK_WIDTH], cos_b, sin_b)
    kr = _rotate(p_ref[rows, OFF_RK:OFF_RK + RET_QK_WIDTH], cos_b, sin_b) * RET_SCALE
    return qr, kr, qr * qdec, kr * kdec


def _tile(t, i):
    return t[:, i * LANE:(i + 1) * LANE]


def _weights_allgather(w_in_pad, w_out_shard):
    half_in = D_MODEL // 2
    half_out = SHARD_OUT // 2

    def body(wi_ref, wo_ref, win_ref, wout_ref, sh_ref, blk_ref, blko_ref, send_sems, recv_sems):
        x, y, c = lax.axis_index("x"), lax.axis_index("y"), lax.axis_index("c")
        me, sibling = (x, y, c), (x, y, 1 - c)
        chips = [(1 - x, y), (x, 1 - y), (1 - x, 1 - y)]

        def rows_in(px, py, pc):
            return sh_ref.at[pl.ds(pl.multiple_of((4 * px + 2 * py + pc) * half_in, half_in), half_in), :]

        def rows_out(px, py, pc):
            return wout_ref.at[pl.ds(pl.multiple_of((4 * px + 2 * py + pc) * half_out, half_out), half_out), :]

        def copy(k, rows, block, to, src=None):
            return pltpu.make_async_remote_copy(
                src_ref=rows(*block) if src is None else src, dst_ref=rows(*block),
                send_sem=send_sems.at[k], recv_sem=recv_sems.at[k], device_id=to, device_id_type=MESH_ID)

        mine = wi_ref[pl.ds(pl.multiple_of(c * half_in, half_in), half_in), :]
        mine = pltpu.roll(mine, y * SHARD_SHIFT, 1)
        blk_ref[...] = mine.astype(BF16)
        blko_ref[...] = wo_ref[pl.ds(pl.multiple_of(c * half_out, half_out), half_out), :].astype(BF16)
        rows_in(*me)[...] = blk_ref[...]
        rows_out(*me)[...] = blko_ref[...]

        sets = ((0, rows_in, blk_ref), (7, rows_out, blko_ref))
        first, passed = [], []
        for base, rows, src in sets:
            first.append(copy(base, rows, me, sibling, src=src))
            first += [copy(base + 1 + j, rows, me, (*chip, c), src=src) for j, chip in enumerate(chips)]
        for cp in first:
            cp.start()
        for base, rows, src in sets:
            for j, chip in enumerate(chips):
                copy(base + 1 + j, rows, (*chip, c), me).wait_recv()
                fwd = copy(base + 4 + j, rows, (*chip, c), sibling)
                fwd.start()
                passed.append(fwd)
        for base, rows, src in sets:
            copy(base, rows, sibling, me).wait_recv()
            for j, chip in enumerate(chips):
                copy(base + 4 + j, rows, (*chip, 1 - c), me).wait_recv()
        for cp in first + passed:
            cp.wait_send()

        def shard(j, lo, hi):
            return sh_ref[j * D_MODEL:(j + 1) * D_MODEL, lo:hi]

        keep = SHARD_PAD - LANE
        for pair in range(2):
            a, b = 2 * pair, 2 * pair + 1
            start = WIN_START[a]
            win_ref[:, start:start + keep] = shard(a, 0, keep)
            win_ref[:, start + keep:start + SHARD_PAD] = shard(a, keep, SHARD_PAD) + shard(b, 0, LANE)
            win_ref[:, start + SHARD_PAD:start + SHARD_PAD + keep] = shard(b, LANE, SHARD_PAD)

    vmem = pl.BlockSpec(memory_space=pltpu.VMEM)
    return _call(
        body, name="weights_allgather",
        out_shape=(jax.ShapeDtypeStruct((D_MODEL, IN_WIDTH), BF16), jax.ShapeDtypeStruct((MIX_WIDTH, D_MODEL), BF16)),
        in_specs=[vmem, vmem], out_specs=(vmem, vmem),
        scratch_shapes=[
            pltpu.VMEM((N_CHIPS * D_MODEL, SHARD_PAD), BF16),
            pltpu.VMEM((half_in, SHARD_PAD), BF16),
            pltpu.VMEM((half_out, D_MODEL), BF16),
            pltpu.SemaphoreType.DMA((14,)),
            pltpu.SemaphoreType.DMA((14,)),
        ],
        compiler_params=_params(40),
    )(w_in_pad, w_out_shard)


def _in_proj(x, norm_g, w_in, tb):
    seq = x.shape[0]

    def body(x_ref, g_ref, w_ref, p_ref):
        xv = x_ref[...]
        r = lax.rsqrt(jnp.mean(xv * xv, axis=1, keepdims=True) + RMS_EPS)
        h = (xv * r) * g_ref[...]
        p_ref[...] = _dot(h.astype(BF16), w_ref[...])

    return _call(
        body, name="in_proj", grid=(seq // tb,),
        out_shape=jax.ShapeDtypeStruct((seq, IN_WIDTH), F32),
        in_specs=[pl.BlockSpec((tb, D_MODEL), lambda i: (i, 0)), _const_spec((1, D_MODEL)),
                  _const_spec((D_MODEL, IN_WIDTH))],
        out_specs=pl.BlockSpec((tb, IN_WIDTH), lambda i: (i, 0)),
        compiler_params=_params(48, ("arbitrary",)),
    )(x, norm_g, w_in)


def _mix_fwd(proj, x, target, w_out, final_g, gn_gain, sinks, tables, tb):
    seq = x.shape[0]
    nsub = tb // BLK
    cos_t, sin_t, decay_in, qdec_t, kdec_t, cd_t, bias_t = tables

    def body(p_ref, x_ref, t_ref, cos_ref, sin_ref, wout_ref, fg_ref, gain_ref, sinks_ref, din_ref, qdec_ref,
             kdec_ref, cd_ref, bias_ref, mix_ref, dxo_ref, st_ref, loss_ref, gfin_ref, kprev_ref, vprev_ref, state_ref):
        i = pl.program_id(0)

        @pl.when(i == 0)
        def _():
            kprev_ref[...] = jnp.zeros_like(kprev_ref)
            vprev_ref[...] = jnp.zeros_like(vprev_ref)
            state_ref[...] = jnp.zeros_like(state_ref)
            loss_ref[...] = jnp.zeros_like(loss_ref)
            gfin_ref[...] = jnp.zeros_like(gfin_ref)

        def sub(j, carry):
            kp, vp, states = carry
            rows = pl.ds(pl.multiple_of(j * BLK, BLK), BLK)
            bias = bias_ref[jnp.where(jnp.logical_or(i > 0, j > 0), 1, 0)]

            aq = p_ref[rows, OFF_AQ:OFF_AQ + ATT_WIDTH]
            ak = p_ref[rows, OFF_AK:OFF_AK + ATT_KV_WIDTH]
            av = p_ref[rows, OFF_AV:OFF_AV + ATT_KV_WIDTH]
            az = p_ref[rows, OFF_AZ:OFF_AZ + ATT_WIDTH]
            k_ops = _kv_operands(jnp.concatenate([kp, ak], axis=0))
            v_ops = _kv_operands(jnp.concatenate([vp, av], axis=0))
            o_tiles = []
            for g in range(ATT_KV_HEADS):
                qs = (_stack_tiles(aq, 2 * g) * ATT_SCALE).astype(BF16)
                _, _, o = _attn_group(qs, k_ops[g], v_ops[g], bias, _sink_rows(sinks_ref, g))
                o_tiles += [o[0:BLK], o[BLK:]]
            out = [jnp.concatenate(o_tiles, axis=1) * (az * _sigmoid(az))]

            qr, kr, qd, kd = _retention_operands(p_ref, rows, cos_ref[rows, :], sin_ref[rows, :],
                                                 qdec_ref[...], kdec_ref[...])
            new_states = []
            for t in range(PAIRS):
                q_lo, q_hi = _split_heads(_tile(qr, t))
                sc = _dot_nt(jnp.concatenate([q_lo, q_hi], axis=0).astype(BF16), _tile(kr, t).astype(BF16)) * din_ref[t]
                qd_heads = _split_heads(_tile(qd, t))
                kd_lo, kd_hi = _split_heads(_tile(kd, t))
                state_b = states[t].astype(BF16)
                vs = []
                for hh in range(2):
                    h = 2 * t + hh
                    vh = p_ref[rows, OFF_RV + h * RET_V_DIM:OFF_RV + (h + 1) * RET_V_DIM].astype(BF16)
                    vs.append(vh)
                    lhs = jnp.concatenate([sc[hh * BLK:(hh + 1) * BLK].astype(BF16), qd_heads[hh].astype(BF16)], axis=1)
                    on, _ = _group_norm(_dot(lhs, jnp.concatenate([vh, state_b], axis=0)))
                    rz = p_ref[rows, OFF_RZ + h * RET_V_DIM:OFF_RZ + (h + 1) * RET_V_DIM]
                    out.append((on * gain_ref[:, h * RET_V_DIM:(h + 1) * RET_V_DIM]) * (rz * _sigmoid(rz)))
                kd_rows = jnp.concatenate([kd_lo, kd_hi], axis=0).astype(BF16)
                new_states.append(states[t] * cd_ref[t] + _dot_tn(kd_rows, jnp.concatenate(vs, axis=0)))
            mix_ref[rows, :] = jnp.concatenate(out, axis=1).astype(BF16)
            st_ref[j] = jnp.stack(states)
            return ak, av, tuple(new_states)

        carry = (kprev_ref[...], vprev_ref[...], tuple(state_ref[t] for t in range(PAIRS)))
        kp, vp, states = lax.fori_loop(0, nsub, sub, carry)
        kprev_ref[...] = kp
        vprev_ref[...] = vp
        state_ref[...] = jnp.stack(states)

        xo = x_ref[...] + _dot(mix_ref[...], wout_ref[...])
        r2 = lax.rsqrt(jnp.mean(xo * xo, axis=1, keepdims=True) + RMS_EPS)
        xn = xo * r2
        err = xn * fg_ref[...] - t_ref[...]
        loss_ref[...] += jnp.sum(err * err) * (0.5 / D_MODEL)
        dy = err * (1.0 / D_MODEL)
        gfin_ref[...] += jnp.sum(dy * xn, axis=0, keepdims=True)
        u = dy * fg_ref[...]
        dxo_ref[...] = r2 * u - xn * (r2 * jnp.mean(u * xn, axis=1, keepdims=True))

    blk_rows = lambda w: pl.BlockSpec((tb, w), lambda i: (i, 0))
    state_shape = (PAIRS, 2 * RET_QK_DIM, RET_V_DIM)
    return _call(
        body, name="mix_fwd", grid=(seq // tb,),
        out_shape=(
            jax.ShapeDtypeStruct((seq, MIX_WIDTH), BF16),
            jax.ShapeDtypeStruct((seq, D_MODEL), F32),
            jax.ShapeDtypeStruct((seq // BLK,) + state_shape, F32),
            jax.ShapeDtypeStruct((8, LANE), F32),
            jax.ShapeDtypeStruct((1, D_MODEL), F32),
        ),
        in_specs=[
            blk_rows(IN_WIDTH), blk_rows(D_MODEL), blk_rows(D_MODEL), blk_rows(RET_QK_WIDTH), blk_rows(RET_QK_WIDTH),
            _const_spec((MIX_WIDTH, D_MODEL)), _const_spec((1, D_MODEL)), _const_spec((1, RET_WIDTH)),
            pl.BlockSpec(memory_space=pltpu.SMEM),
            _const_spec((PAIRS, 2 * BLK, BLK)), _const_spec((BLK, RET_QK_WIDTH)), _const_spec((BLK, RET_QK_WIDTH)),
            _const_spec(state_shape), _const_spec((2, 2 * BLK, 2 * BLK)),
        ],
        out_specs=(
            blk_rows(MIX_WIDTH), blk_rows(D_MODEL),
            pl.BlockSpec((nsub,) + state_shape, lambda i: (i, 0, 0, 0)),
            _const_spec((8, LANE)), _const_spec((1, D_MODEL)),
        ),
        scratch_shapes=[
            pltpu.VMEM((BLK, ATT_KV_WIDTH), F32), pltpu.VMEM((BLK, ATT_KV_WIDTH), F32),
            pltpu.VMEM(state_shape, F32),
        ],
        compiler_params=_params(48, ("arbitrary",)),
    )(proj, x, target, cos_t, sin_t, w_out, final_g, gn_gain, sinks, decay_in, qdec_t, kdec_t, cd_t, bias_t)


def _mix_bwd(proj, dxo, mix, states, w_out, gn_gain, sinks, tables, tb):
    seq = dxo.shape[0]
    nsub = tb // BLK
    nblk = seq // tb
    cos_t, sin_t, decay_in, qdec_t, kdec_t, cd_t, bias_t = tables
    kv_cols = OFF_AK // (2 * ATT_KV_WIDTH)
    state_shape = (PAIRS, 2 * RET_QK_DIM, RET_V_DIM)

    def body(p_ref, pkv_ref, dxo_ref, mix_ref, st_ref, cos_ref, sin_ref, wout_ref, gain_ref, sinks_ref, din_ref,
             qdec_ref, kdec_ref, cd_ref, bias_ref, dp_ref, gwout_ref, dgain_ref, dsink_ref,
             dmix_ref, kv_ref, dkc_ref, dvc_ref, gst_ref):
        i = pl.program_id(0)
        blk = nblk - 1 - i

        @pl.when(i == 0)
        def _():
            gwout_ref[...] = jnp.zeros_like(gwout_ref)
            dgain_ref[...] = jnp.zeros_like(dgain_ref)
            dsink_ref[...] = jnp.zeros_like(dsink_ref)
            dkc_ref[...] = jnp.zeros_like(dkc_ref)
            dvc_ref[...] = jnp.zeros_like(dvc_ref)
            gst_ref[...] = jnp.zeros_like(gst_ref)

        dxo_b = dxo_ref[...].astype(BF16)
        dmix_ref[...] = _dot_nt(dxo_b, wout_ref[...])
        gwout_ref[...] += _dot_tn(mix_ref[...], dxo_b)
        kv_ref[0:BLK, :] = pkv_ref[...]
        kv_ref[BLK:, :] = p_ref[:, OFF_AK:OFF_AK + 2 * ATT_KV_WIDTH]
        low = _low_lanes((BLK, LANE))
        low2 = _low_lanes((2 * BLK, LANE))
        lane = lax.broadcasted_iota(jnp.int32, (1, LANE), 1)

        def sub(jj, carry):
            dkc, dvc, gsts, dgain, dsink = carry
            j = nsub - 1 - jj
            rows = pl.ds(pl.multiple_of(j * BLK, BLK), BLK)
            both = pl.ds(pl.multiple_of(j * BLK, BLK), 2 * BLK)
            bias = bias_ref[jnp.where(jnp.logical_or(blk > 0, j > 0), 1, 0)]

            aq = p_ref[rows, OFF_AQ:OFF_AQ + ATT_WIDTH]
            az = p_ref[rows, OFF_AZ:OFF_AZ + ATT_WIDTH]
            k_ops = _kv_operands(kv_ref[both, 0:ATT_KV_WIDTH])
            v_ops = _kv_operands(kv_ref[both, ATT_KV_WIDTH:2 * ATT_KV_WIDTH])
            da = dmix_ref[rows, 0:ATT_WIDTH]
            sig = _sigmoid(az)
            d_o = da * (az * sig)
            o_tiles, dq_tiles, dk_sums, dv_sums = [], [], [], []
            for g in range(ATT_KV_HEADS):
                qs = (_stack_tiles(aq, 2 * g) * ATT_SCALE).astype(BF16)
                probs, p_cat, o = _attn_group(qs, k_ops[g], v_ops[g], bias, _sink_rows(sinks_ref, g))
                o_tiles += [o[0:BLK], o[BLK:]]
                dos = _stack_tiles(d_o, 2 * g).astype(BF16)
                ds_parts = []
                for hi, (v_op, (p, p_sink)) in enumerate(zip(v_ops[g], probs)):
                    dpr = _dot_nt(v_op, dos)
                    delta = jnp.sum(p * dpr, axis=0, keepdims=True)
                    ds_parts.append((p * (dpr - delta)).astype(BF16))
                    sink_part = p_sink * delta
                    for half in range(2):
                        tot = jnp.sum(sink_part[:, half * BLK:(half + 1) * BLK], axis=1, keepdims=True)
                        dsink = dsink - jnp.where(lane == 4 * g + 2 * half + hi, tot, 0.0)
                ds_cat = jnp.concatenate(ds_parts, axis=0)
                dqs = _dot_tn(ds_cat, jnp.concatenate(k_ops[g], axis=0)) * ATT_SCALE
                dq_tiles += [dqs[0:BLK], dqs[BLK:]]
                dk_sums.append(_dot(ds_cat, qs))
                dv_sums.append(_dot(p_cat, dos))
            o_att = jnp.concatenate(o_tiles, axis=1)
            daz = da * o_att * (sig * (1.0 + az * (1.0 - sig)))

            def kv_grad(sums):
                (a0, b0), (a1, b1) = [(s[0:2 * BLK], s[2 * BLK:]) for s in sums]
                return jnp.where(low2, a0, b1) + pltpu.roll(jnp.where(low2, a1, b0), HALF_LANE, 1)

            dk_both, dv_both = kv_grad(dk_sums), kv_grad(dv_sums)
            dak = dk_both[BLK:] + dkc
            dav = dv_both[BLK:] + dvc

            cos_b, sin_b = cos_ref[rows, :], sin_ref[rows, :]
            qdec, kdec = qdec_ref[...], kdec_ref[...]
            qr, kr, qd, kd = _retention_operands(p_ref, rows, cos_b, sin_b, qdec, kdec)
            dq_parts, dk_parts, dv_parts, dz_parts, dgain_parts, new_gsts = [], [], [], [], [], []
            for t in range(PAIRS):
                q_rows = jnp.concatenate(_split_heads(_tile(qr, t)), axis=0).astype(BF16)
                k_rows = jnp.concatenate(_split_heads(_tile(kr, t)), axis=0).astype(BF16)
                din = din_ref[t]
                sc = (_dot_nt(q_rows, _tile(kr, t).astype(BF16)) * din).astype(BF16)
                qd_heads = _split_heads(_tile(qd, t))
                kd_heads = _split_heads(_tile(kd, t))
                state_b = st_ref[j, t].astype(BF16)
                gst = gsts[t]
                gst_b = gst.astype(BF16)
                cross = gst * cd_ref[t]
                das, dqds, dkds = [], [], []
                for hh in range(2):
                    h = 2 * t + hh
                    cols = slice(h * RET_V_DIM, (h + 1) * RET_V_DIM)
                    vh = p_ref[rows, OFF_RV + h * RET_V_DIM:OFF_RV + (h + 1) * RET_V_DIM].astype(BF16)
                    lhs = jnp.concatenate([sc[hh * BLK:(hh + 1) * BLK], qd_heads[hh].astype(BF16)], axis=1)
                    rhs = jnp.concatenate([vh, state_b], axis=0)
                    on, rstd = _group_norm(_dot(lhs, rhs))
                    rz = p_ref[rows, OFF_RZ + h * RET_V_DIM:OFF_RZ + (h + 1) * RET_V_DIM]
                    sig_r = _sigmoid(rz)
                    gain = gain_ref[:, cols]
                    dr = dmix_ref[rows, ATT_WIDTH + h * RET_V_DIM:ATT_WIDTH + (h + 1) * RET_V_DIM]
                    dgn = dr * (rz * sig_r)
                    dz_parts.append(dr * (on * gain) * (sig_r * (1.0 + rz * (1.0 - sig_r))))
                    dgain_parts.append(jnp.sum(dgn * on, axis=0, keepdims=True))
                    don = dgn * gain
                    d_out = rstd * (don - jnp.mean(don, axis=1, keepdims=True)
                                    - on * jnp.mean(don * on, axis=1, keepdims=True))
                    dob = d_out.astype(BF16)
                    dlhs = _dot_nt(dob, rhs)
                    drhs = _dot_tn(lhs, dob)
                    das.append((dlhs[:, 0:BLK] * din[hh * BLK:(hh + 1) * BLK]).astype(BF16))
                    dqds.append(dlhs[:, BLK:])
                    dkds.append(_dot_nt(vh, gst_b))
                    dv_parts.append(drhs[0:BLK] + _dot(kd_heads[hh].astype(BF16), gst_b))
                    cross = cross + drhs[BLK:]
                new_gsts.append(cross)
                dq_parts.append(_dot(jnp.concatenate(das, axis=1), k_rows)
                                + jnp.where(low, dqds[0], dqds[1]) * _tile(qdec, t))
                dk_parts.append(_dot_tn(jnp.concatenate(das, axis=0), q_rows)
                                + jnp.where(low, dkds[0], dkds[1]) * _tile(kdec, t))
            drq = _rotate_transposed(jnp.concatenate(dq_parts, axis=1), cos_b, sin_b)
            drk = _rotate_transposed(jnp.concatenate(dk_parts, axis=1) * RET_SCALE, cos_b, sin_b)

            dp_ref[rows, :] = jnp.concatenate(
                [jnp.concatenate(dq_tiles, axis=1), dak, dav, daz, drq, drk] + dv_parts + dz_parts, axis=1).astype(BF16)
            dgain = dgain + jnp.concatenate(dgain_parts, axis=1)
            return dk_both[0:BLK], dv_both[0:BLK], tuple(new_gsts), dgain, dsink

        carry = (dkc_ref[...], dvc_ref[...], tuple(gst_ref[t] for t in range(PAIRS)), dgain_ref[...], dsink_ref[...])
        dkc, dvc, gsts, dgain, dsink = lax.fori_loop(0, nsub, sub, carry)
        dkc_ref[...] = dkc
        dvc_ref[...] = dvc
        gst_ref[...] = jnp.stack(gsts)
        dgain_ref[...] = dgain
        dsink_ref[...] = dsink

    rev_rows = lambda w: pl.BlockSpec((tb, w), lambda i: (nblk - 1 - i, 0))
    prev_kv = pl.BlockSpec((BLK, 2 * ATT_KV_WIDTH), lambda i: (jnp.maximum((nblk - 1 - i) * nsub - 1, 0), kv_cols))
    return _call(
        body, name="mix_bwd", grid=(nblk,),
        out_shape=(
            jax.ShapeDtypeStruct((seq, IN_WIDTH), BF16),
            jax.ShapeDtypeStruct((MIX_WIDTH, D_MODEL), F32),
            jax.ShapeDtypeStruct((1, RET_WIDTH), F32),
            jax.ShapeDtypeStruct((1, LANE), F32),
        ),
        in_specs=[
            rev_rows(IN_WIDTH), prev_kv, rev_rows(D_MODEL), rev_rows(MIX_WIDTH),
            pl.BlockSpec((nsub,) + state_shape, lambda i: (nblk - 1 - i, 0, 0, 0)),
            rev_rows(RET_QK_WIDTH), rev_rows(RET_QK_WIDTH),
            _const_spec((MIX_WIDTH, D_MODEL)), _const_spec((1, RET_WIDTH)),
            pl.BlockSpec(memory_space=pltpu.SMEM),
            _const_spec((PAIRS, 2 * BLK, BLK)), _const_spec((BLK, RET_QK_WIDTH)), _const_spec((BLK, RET_QK_WIDTH)),
            _const_spec(state_shape), _const_spec((2, 2 * BLK, 2 * BLK)),
        ],
        out_specs=(
            rev_rows(IN_WIDTH), _const_spec((MIX_WIDTH, D_MODEL)), _const_spec((1, RET_WIDTH)), _const_spec((1, LANE)),
        ),
        scratch_shapes=[
            pltpu.VMEM((tb, MIX_WIDTH), F32),
            pltpu.VMEM((tb + BLK, 2 * ATT_KV_WIDTH), F32),
            pltpu.VMEM((BLK, ATT_KV_WIDTH), F32), pltpu.VMEM((BLK, ATT_KV_WIDTH), F32),
            pltpu.VMEM(state_shape, F32),
        ],
        compiler_params=_params(56, ("arbitrary",)),
    )(proj, proj, dxo, mix, states, cos_t, sin_t, w_out, gn_gain, sinks, decay_in, qdec_t, kdec_t, cd_t, bias_t)


def _in_proj_bwd(dproj, x, dxo, norm_g, w_in, tb):
    seq = x.shape[0]

    def body(dp_ref, x_ref, dxo_ref, g_ref, w_ref, gx_ref, gw_ref, gnorm_ref):
        i = pl.program_id(0)

        @pl.when(i == 0)
        def _():
            gw_ref[...] = jnp.zeros_like(gw_ref)
            gnorm_ref[...] = jnp.zeros_like(gnorm_ref)

        dp = dp_ref[...]
        xv = x_ref[...]
        r = lax.rsqrt(jnp.mean(xv * xv, axis=1, keepdims=True) + RMS_EPS)
        xn = xv * r
        h = xn * g_ref[...]
        gw_ref[...] += _dot_tn(h.astype(BF16), dp)
        dh = _dot_nt(dp, w_ref[...])
        gnorm_ref[...] += jnp.sum(dh * xn, axis=0, keepdims=True)
        u = dh * g_ref[...]
        gx_ref[...] = dxo_ref[...] + r * u - xn * (r * jnp.mean(u * xn, axis=1, keepdims=True))

    rows = lambda w: pl.BlockSpec((tb, w), lambda i: (i, 0))
    return _call(
        body, name="in_proj_bwd", grid=(seq // tb,),
        out_shape=(
            jax.ShapeDtypeStruct((seq, D_MODEL), F32),
            jax.ShapeDtypeStruct((D_MODEL, IN_WIDTH), F32),
            jax.ShapeDtypeStruct((1, D_MODEL), F32),
        ),
        in_specs=[rows(IN_WIDTH), rows(D_MODEL), rows(D_MODEL), _const_spec((1, D_MODEL)),
                  _const_spec((D_MODEL, IN_WIDTH))],
        out_specs=(rows(D_MODEL), _const_spec((D_MODEL, IN_WIDTH)), _const_spec((1, D_MODEL))),
        compiler_params=_params(60, ("arbitrary",)),
    )(dproj, x, dxo, norm_g, w_in)


def _grads_reduce(gw_in, gw_out, pack):
    half_in = D_MODEL // 2
    half_out = SHARD_OUT // 2
    A_IN, A_OUT, B_IN, B_OUT, C_IN, C_OUT, PACK, N_SEMS = 0, 1, 5, 9, 13, 14, 15, 23

    def body(gwi_hbm, gwo_hbm, pack_ref, fin_ref, fout_ref, packsum_ref,
             mine_in, sib_in, mine_out, sib_out, b_in, b_out, packs, send_sems, recv_sems, local_sems):
        x, y, c = lax.axis_index("x"), lax.axis_index("y"), lax.axis_index("c")
        chip = 2 * x + y
        dev = 2 * chip + c
        sibling = (x, y, 1 - c)

        def remote(src, dst, send_k, recv_k, to):
            return pltpu.make_async_remote_copy(src_ref=src, dst_ref=dst, send_sem=send_sems.at[send_k],
                                                recv_sem=recv_sems.at[recv_k], device_id=to, device_id_type=MESH_ID)

        def in_rows(core):
            return pl.ds(pl.multiple_of(core * half_in, half_in), half_in)

        def out_rows(j, core):
            return pl.ds(pl.multiple_of(j * SHARD_OUT + core * half_out, half_out), half_out)

        packs[dev] = pack_ref[...]
        for d in range(N_DEV):
            to = (d // 4, (d // 2) % 2, d % 2)

            @pl.when(d != dev)
            def _():
                remote(pack_ref, packs.at[dev], PACK + d, PACK + dev, to).start()

        local = [pltpu.make_async_copy(gwi_hbm.at[in_rows(c), :], mine_in, local_sems.at[0])]
        local += [pltpu.make_async_copy(gwo_hbm.at[out_rows(j, c), :], mine_out.at[j], local_sems.at[1 + j])
                  for j in range(N_CHIPS)]
        for cp in local:
            cp.start()
        stage_a = [remote(gwi_hbm.at[in_rows(1 - c), :], sib_in, A_IN, A_IN, sibling)]
        stage_a += [remote(gwo_hbm.at[out_rows(j, 1 - c), :], sib_out.at[j], A_OUT + j, A_OUT + j, sibling)
                    for j in range(N_CHIPS)]
        for cp in stage_a:
            cp.start()
        for cp in local:
            cp.wait()
        for cp in stage_a:
            cp.wait_recv()

        def add_rows(k, carry):
            rows = pl.ds(pl.multiple_of(k * 64, 64), 64)
            mine_in[rows, :] = mine_in[rows, :] + sib_in[rows, :]
            return carry

        lax.fori_loop(0, half_in // 64, add_rows, 0)
        for j in range(N_CHIPS):
            mine_out[j] = mine_out[j] + sib_out[j]

        for j in range(N_CHIPS):
            window = mine_in.at[:, WIN_START[j]:WIN_START[j] + SHARD_PAD]
            to = (j // 2, j % 2, c)

            @pl.when(j != chip)
            def _():
                remote(window, b_in.at[chip], B_IN + j, B_IN + chip, to).start()
                remote(mine_out.at[j], b_out.at[chip], B_OUT + j, B_OUT + chip, to).start()

            @pl.when(j == chip)
            def _():
                b_in[j] = mine_in[:, WIN_START[j]:WIN_START[j] + SHARD_PAD]
                b_out[j] = mine_out[j]

        for j in range(N_CHIPS):
            @pl.when(j != chip)
            def _():
                remote(b_in.at[j], b_in.at[j], B_IN + j, B_IN + j, sibling).wait_recv()
                remote(b_out.at[j], b_out.at[j], B_OUT + j, B_OUT + j, sibling).wait_recv()

        fin_ref[in_rows(c), :] = (b_in[0] + b_in[1]) + (b_in[2] + b_in[3])
        fout_ref[pl.ds(pl.multiple_of(c * half_out, half_out), half_out), :] = (b_out[0] + b_out[1]) + (b_out[2] + b_out[3])
        mine_half = fin_ref.at[in_rows(c), :]
        mine_half_out = fout_ref.at[pl.ds(pl.multiple_of(c * half_out, half_out), half_out), :]
        stage_c = [remote(mine_half, mine_half, C_IN, C_IN, sibling),
                   remote(mine_half_out, mine_half_out, C_OUT, C_OUT, sibling)]
        for cp in stage_c:
            cp.start()
        other_half = fin_ref.at[in_rows(1 - c), :]
        other_half_out = fout_ref.at[pl.ds(pl.multiple_of((1 - c) * half_out, half_out), half_out), :]
        remote(other_half, other_half, C_IN, C_IN, sibling).wait_recv()
        remote(other_half_out, other_half_out, C_OUT, C_OUT, sibling).wait_recv()

        for d in range(N_DEV):
            @pl.when(d != dev)
            def _():
                remote(pack_ref, packs.at[d], PACK + d, PACK + d, sibling).wait_recv()
        total = packs[0]
        for d in range(1, N_DEV):
            total = total + packs[d]
        packsum_ref[...] = total

        for cp in stage_a + stage_c:
            cp.wait_send()
        for j in range(N_CHIPS):
            @pl.when(j != chip)
            def _():
                remote(b_in.at[j], b_in.at[j], B_IN + j, B_IN + j, sibling).wait_send()
                remote(b_out.at[j], b_out.at[j], B_OUT + j, B_OUT + j, sibling).wait_send()
        for d in range(N_DEV):
            @pl.when(d != dev)
            def _():
                remote(pack_ref, packs.at[d], PACK + d, PACK + d, sibling).wait_send()

    vmem = pl.BlockSpec(memory_space=pltpu.VMEM)
    hbm = pl.BlockSpec(memory_space=pl.ANY)
    return _call(
        body, name="grads_reduce",
        out_shape=(jax.ShapeDtypeStruct((D_MODEL, SHARD_PAD), F32), jax.ShapeDtypeStruct((SHARD_OUT, D_MODEL), F32),
                   jax.ShapeDtypeStruct((PACK_ROWS, D_MODEL), F32)),
        in_specs=[hbm, hbm, vmem], out_specs=(vmem, vmem, vmem),
        scratch_shapes=[
            pltpu.VMEM((half_in, IN_WIDTH), F32), pltpu.VMEM((half_in, IN_WIDTH), F32),
            pltpu.VMEM((N_CHIPS, half_out, D_MODEL), F32), pltpu.VMEM((N_CHIPS, half_out, D_MODEL), F32),
            pltpu.VMEM((N_CHIPS, half_in, SHARD_PAD), F32), pltpu.VMEM((N_CHIPS, half_out, D_MODEL), F32),
            pltpu.VMEM((N_DEV, PACK_ROWS, D_MODEL), F32),
            pltpu.SemaphoreType.DMA((N_SEMS,)), pltpu.SemaphoreType.DMA((N_SEMS,)),
            pltpu.SemaphoreType.DMA((1 + N_CHIPS,)),
        ],
        compiler_params=_params(48),
    )(gw_in, gw_out, pack)


def _adamw(name, w, g, m, v, tb, unshift):
    rows, cols = w.shape

    def body(w_ref, g_ref, m_ref, v_ref, go_ref, d_ref, mo_ref, vo_ref):
        gv = g_ref[...]
        if unshift:
            gv = pltpu.roll(gv, lax.axis_index("y") * (cols - SHARD_SHIFT), 1)
        mn = ADAM_B1 * m_ref[...] + (1.0 - ADAM_B1) * gv
        vn = ADAM_B2 * v_ref[...] + (1.0 - ADAM_B2) * (gv * gv)
        m_hat = mn / (1.0 - ADAM_B1 ** ADAM_STEP)
        v_hat = vn / (1.0 - ADAM_B2 ** ADAM_STEP)
        go_ref[...] = gv
        d_ref[...] = -ADAM_LR * (m_hat / (jnp.sqrt(v_hat) + ADAM_EPS) + ADAM_WD * w_ref[...])
        mo_ref[...] = mn
        vo_ref[...] = vn

    spec = pl.BlockSpec((tb, cols), lambda i: (i, 0))
    shape = jax.ShapeDtypeStruct((rows, cols), F32)
    return _call(
        body, name=name, grid=(rows // tb,), out_shape=(shape,) * 4,
        in_specs=[spec] * 4, out_specs=(spec,) * 4,
        compiler_params=_params(32, ("arbitrary",)),
    )(w, g, m, v)


def _pad_cols(a, width):
    return jnp.pad(a, ((0, 0), (0, width - a.shape[1])))


def _pack_rows(rows):
    parts = [jnp.pad(r.reshape(1, -1), ((0, 7), (0, D_MODEL - r.size))) for r in rows]
    parts += [jnp.zeros((8, D_MODEL), F32)] * (PACK_PARTS - len(rows))
    return jnp.concatenate(parts, axis=0)


def kernel(x, norm_g, w_in, att_sinks, ret_gn_g, w_out, final_g, loss_target, m_norm_g, m_w_in, m_att_sinks, m_ret_gn_g, m_w_out, m_final_g, v_norm_g, v_w_in, v_att_sinks, v_ret_gn_g, v_w_out, v_final_g):
    seq = x.shape[1]
    xs, tgt = x[0], loss_target[0]
    final_g2 = final_g.reshape(1, D_MODEL)
    tables = _tables(seq)

    w_in_full, w_out_full = _weights_allgather(_pad_cols(w_in[0], SHARD_PAD), w_out[0])
    proj = _in_proj(xs, norm_g, w_in_full, min(512, seq))
    mix, dxo, states, loss_part, gfin = _mix_fwd(proj, xs, tgt, w_out_full, final_g2, ret_gn_g, att_sinks, tables,
                                                 min(256, seq))
    dproj, gw_out, dgain, dsink = _mix_bwd(proj, dxo, mix, states, w_out_full, ret_gn_g, att_sinks, tables,
                                           min(256, seq))
    grad_x, gw_in, gnorm = _in_proj_bwd(dproj, xs, dxo, norm_g, w_in_full, min(256, seq))

    pack = _pack_rows([gnorm, gfin, dgain, dsink, loss_part[0:1, :]])
    g_in, g_out, sums = _grads_reduce(gw_in, gw_out, pack)

    small_w = _pack_rows([norm_g, final_g, ret_gn_g, att_sinks])
    small_m = _pack_rows([m_norm_g, m_final_g, m_ret_gn_g, m_att_sinks])
    small_v = _pack_rows([v_norm_g, v_final_g, v_ret_gn_g, v_att_sinks])
    res_in = _adamw("adamw_w_in", _pad_cols(w_in[0], SHARD_PAD), g_in, _pad_cols(m_w_in[0], SHARD_PAD),
                    _pad_cols(v_w_in[0], SHARD_PAD), 256, True)
    res_out = _adamw("adamw_w_out", w_out[0], g_out, m_w_out[0], v_w_out[0], SHARD_OUT, False)
    res_small = _adamw("adamw_small", small_w, sums, small_m, small_v, PACK_ROWS, False)

    outs = []
    for k in range(4):
        a = res_small[k]
        outs.append((a[0:1, :], res_in[k][None, :, :SHARD_IN], a[24:25, :ATT_HEADS], a[16:17, :RET_WIDTH],
                     res_out[k][None], a[8, :]))
    loss = sums[32, 0]
    return (loss, grad_x[None], *outs[0], *outs[1], *outs[2], *outs[3])
```

```python
import jax
import jax.numpy as jnp
from jax import lax
from jax.experimental import pallas as pl
from jax.experimental.pallas import tpu as pltpu

F32 = jnp.float32
BF16 = jnp.bfloat16

D_MODEL = 1024
ATT_HEADS = 8
ATT_KV_HEADS = 2
ATT_HEAD_DIM = 64
RET_HEADS = 4
RET_QK_DIM = 64
RET_V_DIM = 128
BLK = 128
ROT_BASE = 10000.0
RMS_EPS = 1e-6
GN_EPS = 1e-6
NEG_INF = -1e30
ATT_SCALE = ATT_HEAD_DIM ** -0.5
RET_SCALE = RET_QK_DIM ** -0.5

ATT_WIDTH = ATT_HEADS * ATT_HEAD_DIM
ATT_KV_WIDTH = ATT_KV_HEADS * ATT_HEAD_DIM
RET_QK_WIDTH = RET_HEADS * RET_QK_DIM
RET_WIDTH = RET_HEADS * RET_V_DIM
MIX_WIDTH = ATT_WIDTH + RET_WIDTH
OFF_AQ = 0
OFF_AK = OFF_AQ + ATT_WIDTH
OFF_AV = OFF_AK + ATT_KV_WIDTH
OFF_AZ = OFF_AV + ATT_KV_WIDTH
OFF_RQ = OFF_AZ + ATT_WIDTH
OFF_RK = OFF_RQ + RET_QK_WIDTH
OFF_RV = OFF_RK + RET_QK_WIDTH
OFF_RZ = OFF_RV + RET_WIDTH
IN_WIDTH = OFF_RZ + RET_WIDTH

LANE = 128
HALF_LANE = LANE // 2
PAIRS = RET_QK_WIDTH // LANE
assert ATT_HEAD_DIM == HALF_LANE and RET_QK_DIM == HALF_LANE and RET_V_DIM == LANE and ATT_KV_WIDTH == LANE

N_CHIPS = 4
N_DEV = 8
SHARD_IN = IN_WIDTH // N_CHIPS
SHARD_PAD = 768
SHARD_SHIFT = SHARD_PAD - SHARD_IN
WIN_START = tuple((j * SHARD_IN) // LANE * LANE for j in range(N_CHIPS))
SHARD_OUT = MIX_WIDTH // N_CHIPS

PACK_PARTS = 5
PACK_ROWS = 8 * PACK_PARTS

ADAM_LR = 0.001
ADAM_B1 = 0.9
ADAM_B2 = 0.999
ADAM_EPS = 1e-08
ADAM_WD = 0.01
ADAM_STEP = 10

VMEM_CAP = 64 * 1024 * 1024
MESH_ID = pl.DeviceIdType.MESH


def _call(body, **kw):
    return pl.pallas_call(body, **kw)


def _params(vmem_mb, semantics=None):
    assert vmem_mb * 1024 * 1024 < VMEM_CAP
    return pltpu.CompilerParams(dimension_semantics=semantics, vmem_limit_bytes=vmem_mb * 1024 * 1024)


def _dot(a, b):
    return jnp.dot(a, b, preferred_element_type=F32)


def _dot_nt(a, b):
    return lax.dot_general(a, b, (((1,), (1,)), ((), ())), preferred_element_type=F32)


def _dot_tn(a, b):
    return lax.dot_general(a, b, (((0,), (0,)), ((), ())), preferred_element_type=F32)


def _sigmoid(z):
    return 1.0 / (1.0 + jnp.exp(-z))


def _const_spec(shape):
    nd = len(shape)
    return pl.BlockSpec(shape, lambda i: (0,) * nd)


def _tables(seq):
    pos = jnp.arange(seq, dtype=F32)
    theta = 1.0 / (ROT_BASE ** jnp.linspace(0.0, 1.0, RET_QK_DIM // 2, dtype=F32))
    ang = pos[:, None] * theta[None, :]
    cos, sin = jnp.cos(ang), jnp.sin(ang)
    cos2 = jnp.repeat(cos, 2, axis=1)
    sin2 = jnp.stack([-sin, sin], axis=-1).reshape(seq, RET_QK_DIM)
    cos_t = jnp.tile(cos2, (1, RET_HEADS))
    sin_t = jnp.tile(sin2, (1, RET_HEADS))

    log_gamma = jnp.log(1.0 - 2.0 ** (-5.0 - jnp.arange(RET_HEADS, dtype=F32)))
    idx = jnp.arange(BLK, dtype=F32)
    rel = idx[:, None] - idx[None, :]
    decay_in = jnp.where(rel >= 0, jnp.exp(log_gamma[:, None, None] * jnp.maximum(rel, 0.0)), 0.0)
    k_dec = jnp.exp(log_gamma[:, None] * (BLK - 1 - idx)[None, :])
    q_dec = jnp.exp(log_gamma[:, None] * (idx + 1)[None, :])
    chunk_decay = jnp.exp(log_gamma * BLK)
    kdec_t = jnp.repeat(k_dec.T, RET_QK_DIM, axis=1)
    qdec_t = jnp.repeat(q_dec.T, RET_QK_DIM, axis=1)
    cd_t = jnp.broadcast_to(chunk_decay[:, None, None], (RET_HEADS, RET_QK_DIM, RET_V_DIM))
    decay_in = decay_in.reshape(PAIRS, 2 * BLK, BLK)
    cd_t = cd_t.reshape(PAIRS, 2 * RET_QK_DIM, RET_V_DIM)

    key = jnp.arange(2 * BLK)[:, None]
    query = jnp.arange(2 * BLK)[None, :] % BLK
    diff = query + BLK - key
    valid = (diff >= 0) & (diff < BLK)
    bias = jnp.stack([jnp.where(valid & (key >= BLK), 0.0, NEG_INF), jnp.where(valid, 0.0, NEG_INF)]).astype(F32)
    return cos_t, sin_t, decay_in, qdec_t, kdec_t, cd_t, bias


def _low_lanes(shape):
    lane = lax.broadcasted_iota(jnp.int32, shape, len(shape) - 1)
    return (lane & HALF_LANE) == 0


def _split_heads(t):
    low = _low_lanes(t.shape)
    zero = jnp.zeros_like(t)
    return jnp.where(low, t, zero), jnp.where(low, zero, t)


def _swap_pairs(t):
    lane = lax.broadcasted_iota(jnp.int32, t.shape, 1)
    nxt = pltpu.roll(t, t.shape[1] - 1, 1)
    prv = pltpu.roll(t, 1, 1)
    return jnp.where((lane & 1) == 0, nxt, prv)


def _rotate(t, cos_t, sin_t):
    return t * cos_t + _swap_pairs(t) * sin_t


def _rotate_transposed(d, cos_t, sin_t):
    return d * cos_t + _swap_pairs(d * sin_t)


def _kv_operands(cat):
    low = _low_lanes(cat.shape)
    swapped = pltpu.roll(cat, HALF_LANE, 1)
    zero = jnp.zeros_like(cat)
    pick = lambda a, b: jnp.where(low, a, b).astype(BF16)
    return ((pick(cat, zero), pick(zero, swapped)), (pick(swapped, zero), pick(zero, cat)))


def _stack_tiles(t, first_tile):
    a = t[:, first_tile * LANE:(first_tile + 1) * LANE]
    b = t[:, (first_tile + 1) * LANE:(first_tile + 2) * LANE]
    return jnp.concatenate([a, b], axis=0)


def _sink_rows(sinks_ref, group):
    first = lax.broadcasted_iota(jnp.int32, (1, 2 * BLK), 1) < BLK

    def row(h0, h1):
        return jnp.where(first, sinks_ref[0, group * 4 + h0], sinks_ref[0, group * 4 + h1])
    return row(0, 2), row(1, 3)


ATT_PROBLEMS = tuple((g, hi) for g in range(ATT_KV_HEADS) for hi in range(2))


def _attn_probs(qs, k_ops, bias, sinks_ref):
    sink = [_sink_rows(sinks_ref, g)[hi] for g, hi in ATT_PROBLEMS]
    s = [_dot_nt(k_ops[g][hi], qs[g]) + bias for g, hi in ATT_PROBLEMS]
    m = [jnp.maximum(jnp.max(si, axis=0, keepdims=True), ki) for si, ki in zip(s, sink)]
    e = [jnp.exp(si - mi) for si, mi in zip(s, m)]
    es = [jnp.exp(ki - mi) for ki, mi in zip(sink, m)]
    inv = [1.0 / (jnp.sum(ei, axis=0, keepdims=True) + esi) for ei, esi in zip(e, es)]
    return [ei * ii for ei, ii in zip(e, inv)], [esi * ii for esi, ii in zip(es, inv)]


def _group_norm_all(outs):
    mu = [jnp.mean(o, axis=1, keepdims=True) for o in outs]
    xc = [o - m for o, m in zip(outs, mu)]
    var = [jnp.mean(c * c, axis=1, keepdims=True) for c in xc]
    rstd = [lax.rsqrt(v + GN_EPS) for v in var]
    return [c * r for c, r in zip(xc, rstd)], rstd


def _retention_operands(p_ref, rows, cos_b, sin_b, qdec, kdec):
    qr = _rotate(p_ref[rows, OFF_RQ:OFF_RQ + RET_QK_WIDTH], cos_b, sin_b)
    kr = _rotate(p_ref[rows, OFF_RK:OFF_RK + RET_QK_WIDTH], cos_b, sin_b) * RET_SCALE
    return qr, kr, qr * qdec, kr * kdec


def _tile(t, i):
    return t[:, i * LANE:(i + 1) * LANE]


def _weights_allgather(w_in_pad, w_out_shard):
    half_in = D_MODEL // 2
    half_out = SHARD_OUT // 2

    def body(wi_ref, wo_ref, win_ref, wout_ref, sh_ref, blk_ref, blko_ref, send_sems, recv_sems):
        x, y, c = lax.axis_index("x"), lax.axis_index("y"), lax.axis_index("c")
        me, sibling = (x, y, c), (x, y, 1 - c)
        chips = [(1 - x, y), (x, 1 - y), (1 - x, 1 - y)]

        def rows_in(px, py, pc):
            return sh_ref.at[pl.ds(pl.multiple_of((4 * px + 2 * py + pc) * half_in, half_in), half_in), :]

        def rows_out(px, py, pc):
            return wout_ref.at[pl.ds(pl.multiple_of((4 * px + 2 * py + pc) * half_out, half_out), half_out), :]

        def copy(k, rows, block, to, src=None):
            return pltpu.make_async_remote_copy(
                src_ref=rows(*block) if src is None else src, dst_ref=rows(*block),
                send_sem=send_sems.at[k], recv_sem=recv_sems.at[k], device_id=to, device_id_type=MESH_ID)

        mine = wi_ref[pl.ds(pl.multiple_of(c * half_in, half_in), half_in), :]
        mine = pltpu.roll(mine, y * SHARD_SHIFT, 1)
        blk_ref[...] = mine.astype(BF16)
        blko_ref[...] = wo_ref[pl.ds(pl.multiple_of(c * half_out, half_out), half_out), :].astype(BF16)
        rows_in(*me)[...] = blk_ref[...]
        rows_out(*me)[...] = blko_ref[...]

        sets = ((0, rows_in, blk_ref), (7, rows_out, blko_ref))
        first, passed = [], []
        for base, rows, src in sets:
            first.append(copy(base, rows, me, sibling, src=src))
            first += [copy(base + 1 + j, rows, me, (*chip, c), src=src) for j, chip in enumerate(chips)]
        for cp in first:
            cp.start()
        for base, rows, src in sets:
            for j, chip in enumerate(chips):
                copy(base + 1 + j, rows, (*chip, c), me).wait_recv()
                fwd = copy(base + 4 + j, rows, (*chip, c), sibling)
                fwd.start()
                passed.append(fwd)
        for base, rows, src in sets:
            copy(base, rows, sibling, me).wait_recv()
            for j, chip in enumerate(chips):
                copy(base + 4 + j, rows, (*chip, 1 - c), me).wait_recv()
        for cp in first + passed:
            cp.wait_send()

        def shard(j, lo, hi):
            return sh_ref[j * D_MODEL:(j + 1) * D_MODEL, lo:hi]

        keep = SHARD_PAD - LANE
        for pair in range(2):
            a, b = 2 * pair, 2 * pair + 1
            start = WIN_START[a]
            win_ref[:, start:start + keep] = shard(a, 0, keep)
            win_ref[:, start + keep:start + SHARD_PAD] = shard(a, keep, SHARD_PAD) + shard(b, 0, LANE)
            win_ref[:, start + SHARD_PAD:start + SHARD_PAD + keep] = shard(b, LANE, SHARD_PAD)

    vmem = pl.BlockSpec(memory_space=pltpu.VMEM)
    return _call(
        body, name="weights_allgather",
        out_shape=(jax.ShapeDtypeStruct((D_MODEL, IN_WIDTH), BF16), jax.ShapeDtypeStruct((MIX_WIDTH, D_MODEL), BF16)),
        in_specs=[vmem, vmem], out_specs=(vmem, vmem),
        scratch_shapes=[
            pltpu.VMEM((N_CHIPS * D_MODEL, SHARD_PAD), BF16),
            pltpu.VMEM((half_in, SHARD_PAD), BF16),
            pltpu.VMEM((half_out, D_MODEL), BF16),
            pltpu.SemaphoreType.DMA((14,)),
            pltpu.SemaphoreType.DMA((14,)),
        ],
        compiler_params=_params(40),
    )(w_in_pad, w_out_shard)


def _in_proj(x, norm_g, w_in, tb):
    seq = x.shape[0]

    def body(x_ref, g_ref, w_ref, p_ref):
        xv = x_ref[...]
        r = lax.rsqrt(jnp.mean(xv * xv, axis=1, keepdims=True) + RMS_EPS)
        h = (xv * r) * g_ref[...]
        p_ref[...] = _dot(h.astype(BF16), w_ref[...])

    return _call(
        body, name="in_proj", grid=(seq // tb,),
        out_shape=jax.ShapeDtypeStruct((seq, IN_WIDTH), F32),
        in_specs=[pl.BlockSpec((tb, D_MODEL), lambda i: (i, 0)), _const_spec((1, D_MODEL)),
                  _const_spec((D_MODEL, IN_WIDTH))],
        out_specs=pl.BlockSpec((tb, IN_WIDTH), lambda i: (i, 0)),
        compiler_params=_params(48, ("arbitrary",)),
    )(x, norm_g, w_in)


def _mix_fwd(proj, x, target, w_out, final_g, gn_gain, sinks, tables, tb):
    seq = x.shape[0]
    nsub = tb // BLK
    cos_t, sin_t, decay_in, qdec_t, kdec_t, cd_t, bias_t = tables

    def body(p_ref, x_ref, t_ref, cos_ref, sin_ref, wout_ref, fg_ref, gain_ref, sinks_ref, din_ref, qdec_ref,
             kdec_ref, cd_ref, bias_ref, mix_ref, dxo_ref, st_ref, loss_ref, gfin_ref, oatt_ref,
             kprev_ref, vprev_ref, state_ref):
        i = pl.program_id(0)

        @pl.when(i == 0)
        def _():
            kprev_ref[...] = jnp.zeros_like(kprev_ref)
            vprev_ref[...] = jnp.zeros_like(vprev_ref)
            state_ref[...] = jnp.zeros_like(state_ref)
            loss_ref[...] = jnp.zeros_like(loss_ref)
            gfin_ref[...] = jnp.zeros_like(gfin_ref)

        def sub(j, carry):
            kp, vp, states = carry
            rows = pl.ds(pl.multiple_of(j * BLK, BLK), BLK)
            bias = bias_ref[jnp.where(jnp.logical_or(i > 0, j > 0), 1, 0)]

            aq = p_ref[rows, OFF_AQ:OFF_AQ + ATT_WIDTH]
            ak = p_ref[rows, OFF_AK:OFF_AK + ATT_KV_WIDTH]
            av = p_ref[rows, OFF_AV:OFF_AV + ATT_KV_WIDTH]
            az = p_ref[rows, OFF_AZ:OFF_AZ + ATT_WIDTH]
            k_ops = _kv_operands(jnp.concatenate([kp, ak], axis=0))
            v_ops = _kv_operands(jnp.concatenate([vp, av], axis=0))
            qs = [(_stack_tiles(aq, 2 * g) * ATT_SCALE).astype(BF16) for g in range(ATT_KV_HEADS)]
            p, _ = _attn_probs(qs, k_ops, bias, sinks_ref)
            o_tiles = []
            for g in range(ATT_KV_HEADS):
                p_cat = jnp.concatenate([p[2 * g].astype(BF16), p[2 * g + 1].astype(BF16)], axis=0)
                o = _dot_tn(p_cat, jnp.concatenate(v_ops[g], axis=0))
                o_tiles += [o[0:BLK], o[BLK:]]
            o_att = jnp.concatenate(o_tiles, axis=1)
            oatt_ref[rows, :] = o_att
            out = [o_att * (az * _sigmoid(az))]

            qr, kr, qd, kd = _retention_operands(p_ref, rows, cos_ref[rows, :], sin_ref[rows, :],
                                                 qdec_ref[...], kdec_ref[...])
            heads = [(t, hh) for t in range(PAIRS) for hh in range(2)]
            sc = [_dot_nt(jnp.concatenate(_split_heads(_tile(qr, t)), axis=0).astype(BF16), _tile(kr, t).astype(BF16))
                  * din_ref[t] for t in range(PAIRS)]
            qd_heads = [_split_heads(_tile(qd, t)) for t in range(PAIRS)]
            state_b = [states[t].astype(BF16) for t in range(PAIRS)]
            vs = [p_ref[rows, OFF_RV + h * RET_V_DIM:OFF_RV + (h + 1) * RET_V_DIM].astype(BF16) for h in range(RET_HEADS)]
            rzs = [p_ref[rows, OFF_RZ + h * RET_V_DIM:OFF_RZ + (h + 1) * RET_V_DIM] for h in range(RET_HEADS)]
            lhs = [jnp.concatenate([sc[t][hh * BLK:(hh + 1) * BLK].astype(BF16), qd_heads[t][hh].astype(BF16)], axis=1)
                   for t, hh in heads]
            ons, _ = _group_norm_all([_dot(lhs[2 * t + hh], jnp.concatenate([vs[2 * t + hh], state_b[t]], axis=0))
                                      for t, hh in heads])
            out += [(ons[h] * gain_ref[:, h * RET_V_DIM:(h + 1) * RET_V_DIM]) * (rzs[h] * _sigmoid(rzs[h]))
                    for h in range(RET_HEADS)]
            new_states = [states[t] * cd_ref[t]
                          + _dot_tn(jnp.concatenate(_split_heads(_tile(kd, t)), axis=0).astype(BF16),
                                    jnp.concatenate([vs[2 * t], vs[2 * t + 1]], axis=0)) for t in range(PAIRS)]
            mix_ref[rows, :] = jnp.concatenate(out, axis=1).astype(BF16)
            st_ref[j] = jnp.stack(states)
            return ak, av, tuple(new_states)

        carry = (kprev_ref[...], vprev_ref[...], tuple(state_ref[t] for t in range(PAIRS)))
        kp, vp, states = lax.fori_loop(0, nsub, sub, carry)
        kprev_ref[...] = kp
        vprev_ref[...] = vp
        state_ref[...] = jnp.stack(states)

        xo = x_ref[...] + _dot(mix_ref[...], wout_ref[...])
        r2 = lax.rsqrt(jnp.mean(xo * xo, axis=1, keepdims=True) + RMS_EPS)
        xn = xo * r2
        err = xn * fg_ref[...] - t_ref[...]
        loss_ref[...] += jnp.sum(err * err) * (0.5 / D_MODEL)
        dy = err * (1.0 / D_MODEL)
        gfin_ref[...] += jnp.sum(dy * xn, axis=0, keepdims=True)
        u = dy * fg_ref[...]
        dxo_ref[...] = r2 * u - xn * (r2 * jnp.mean(u * xn, axis=1, keepdims=True))

    blk_rows = lambda w: pl.BlockSpec((tb, w), lambda i: (i, 0))
    state_shape = (PAIRS, 2 * RET_QK_DIM, RET_V_DIM)
    return _call(
        body, name="mix_fwd", grid=(seq // tb,),
        out_shape=(
            jax.ShapeDtypeStruct((seq, MIX_WIDTH), BF16),
            jax.ShapeDtypeStruct((seq, D_MODEL), F32),
            jax.ShapeDtypeStruct((seq // BLK,) + state_shape, F32),
            jax.ShapeDtypeStruct((8, LANE), F32),
            jax.ShapeDtypeStruct((1, D_MODEL), F32),
            jax.ShapeDtypeStruct((seq, ATT_WIDTH), F32),
        ),
        in_specs=[
            blk_rows(IN_WIDTH), blk_rows(D_MODEL), blk_rows(D_MODEL), blk_rows(RET_QK_WIDTH), blk_rows(RET_QK_WIDTH),
            _const_spec((MIX_WIDTH, D_MODEL)), _const_spec((1, D_MODEL)), _const_spec((1, RET_WIDTH)),
            pl.BlockSpec(memory_space=pltpu.SMEM),
            _const_spec((PAIRS, 2 * BLK, BLK)), _const_spec((BLK, RET_QK_WIDTH)), _const_spec((BLK, RET_QK_WIDTH)),
            _const_spec(state_shape), _const_spec((2, 2 * BLK, 2 * BLK)),
        ],
        out_specs=(
            blk_rows(MIX_WIDTH), blk_rows(D_MODEL),
            pl.BlockSpec((nsub,) + state_shape, lambda i: (i, 0, 0, 0)),
            _const_spec((8, LANE)), _const_spec((1, D_MODEL)), blk_rows(ATT_WIDTH),
        ),
        scratch_shapes=[
            pltpu.VMEM((BLK, ATT_KV_WIDTH), F32), pltpu.VMEM((BLK, ATT_KV_WIDTH), F32),
            pltpu.VMEM(state_shape, F32),
        ],
        compiler_params=_params(48, ("arbitrary",)),
    )(proj, x, target, cos_t, sin_t, w_out, final_g, gn_gain, sinks, decay_in, qdec_t, kdec_t, cd_t, bias_t)


def _mix_bwd(proj, dxo, mix, o_att, states, w_out, gn_gain, sinks, tables, tb):
    seq = dxo.shape[0]
    nsub = tb // BLK
    nblk = seq // tb
    cos_t, sin_t, decay_in, qdec_t, kdec_t, cd_t, bias_t = tables
    kv_cols = OFF_AK // (2 * ATT_KV_WIDTH)
    state_shape = (PAIRS, 2 * RET_QK_DIM, RET_V_DIM)

    def body(p_ref, pkv_ref, dxo_ref, mix_ref, oatt_ref, st_ref, cos_ref, sin_ref, wout_ref, gain_ref, sinks_ref, din_ref,
             qdec_ref, kdec_ref, cd_ref, bias_ref, dp_ref, gwout_ref, dgain_ref, dsink_ref,
             dmix_ref, kv_ref, dkc_ref, dvc_ref, gst_ref):
        i = pl.program_id(0)
        blk = nblk - 1 - i

        @pl.when(i == 0)
        def _():
            gwout_ref[...] = jnp.zeros_like(gwout_ref)
            dgain_ref[...] = jnp.zeros_like(dgain_ref)
            dsink_ref[...] = jnp.zeros_like(dsink_ref)
            dkc_ref[...] = jnp.zeros_like(dkc_ref)
            dvc_ref[...] = jnp.zeros_like(dvc_ref)
            gst_ref[...] = jnp.zeros_like(gst_ref)

        dxo_b = dxo_ref[...].astype(BF16)
        dmix_ref[...] = _dot_nt(dxo_b, wout_ref[...])
        gwout_ref[...] += _dot_tn(mix_ref[...], dxo_b)
        kv_ref[0:BLK, :] = pkv_ref[...]
        kv_ref[BLK:, :] = p_ref[:, OFF_AK:OFF_AK + 2 * ATT_KV_WIDTH]
        low = _low_lanes((BLK, LANE))
        low2 = _low_lanes((2 * BLK, LANE))
        lane = lax.broadcasted_iota(jnp.int32, (1, LANE), 1)

        def sub(jj, carry):
            dkc, dvc, gsts, dgain, dsink = carry
            j = nsub - 1 - jj
            rows = pl.ds(pl.multiple_of(j * BLK, BLK), BLK)
            both = pl.ds(pl.multiple_of(j * BLK, BLK), 2 * BLK)
            bias = bias_ref[jnp.where(jnp.logical_or(blk > 0, j > 0), 1, 0)]

            aq = p_ref[rows, OFF_AQ:OFF_AQ + ATT_WIDTH]
            az = p_ref[rows, OFF_AZ:OFF_AZ + ATT_WIDTH]
            k_ops = _kv_operands(kv_ref[both, 0:ATT_KV_WIDTH])
            v_ops = _kv_operands(kv_ref[both, ATT_KV_WIDTH:2 * ATT_KV_WIDTH])
            da = dmix_ref[rows, 0:ATT_WIDTH]
            sig = _sigmoid(az)
            d_o = da * (az * sig)
            qs = [(_stack_tiles(aq, 2 * g) * ATT_SCALE).astype(BF16) for g in range(ATT_KV_HEADS)]
            dos = [_stack_tiles(d_o, 2 * g).astype(BF16) for g in range(ATT_KV_HEADS)]
            p, p_sink = _attn_probs(qs, k_ops, bias, sinks_ref)
            dpr = [_dot_nt(v_ops[g][hi], dos[g]) for g, hi in ATT_PROBLEMS]
            delta = [jnp.sum(pi * di, axis=0, keepdims=True) for pi, di in zip(p, dpr)]
            ds = [(pi * (di - ti)).astype(BF16) for pi, di, ti in zip(p, dpr, delta)]
            for (g, hi), ki, ti in zip(ATT_PROBLEMS, p_sink, delta):
                sink_part = ki * ti
                for half in range(2):
                    tot = jnp.sum(sink_part[:, half * BLK:(half + 1) * BLK], axis=1, keepdims=True)
                    dsink = dsink - jnp.where(lane == 4 * g + 2 * half + hi, tot, 0.0)
            dq_tiles, dk_sums, dv_sums = [], [], []
            for g in range(ATT_KV_HEADS):
                ds_cat = jnp.concatenate([ds[2 * g], ds[2 * g + 1]], axis=0)
                p_cat = jnp.concatenate([p[2 * g].astype(BF16), p[2 * g + 1].astype(BF16)], axis=0)
                dqs = _dot_tn(ds_cat, jnp.concatenate(k_ops[g], axis=0)) * ATT_SCALE
                dq_tiles += [dqs[0:BLK], dqs[BLK:]]
                dk_sums.append(_dot(ds_cat, qs[g]))
                dv_sums.append(_dot(p_cat, dos[g]))
            daz = da * oatt_ref[rows, :] * (sig * (1.0 + az * (1.0 - sig)))

            def kv_grad(sums):
                (a0, b0), (a1, b1) = [(s[0:2 * BLK], s[2 * BLK:]) for s in sums]
                return jnp.where(low2, a0, b1) + pltpu.roll(jnp.where(low2, a1, b0), HALF_LANE, 1)

            dk_both, dv_both = kv_grad(dk_sums), kv_grad(dv_sums)
            dak = dk_both[BLK:] + dkc
            dav = dv_both[BLK:] + dvc

            cos_b, sin_b = cos_ref[rows, :], sin_ref[rows, :]
            qdec, kdec = qdec_ref[...], kdec_ref[...]
            qr, kr, qd, kd = _retention_operands(p_ref, rows, cos_b, sin_b, qdec, kdec)
            heads = [(t, hh) for t in range(PAIRS) for hh in range(2)]
            head_cols = [slice(h * RET_V_DIM, (h + 1) * RET_V_DIM) for h in range(RET_HEADS)]
            q_rows = [jnp.concatenate(_split_heads(_tile(qr, t)), axis=0).astype(BF16) for t in range(PAIRS)]
            k_rows = [jnp.concatenate(_split_heads(_tile(kr, t)), axis=0).astype(BF16) for t in range(PAIRS)]
            din = [din_ref[t] for t in range(PAIRS)]
            sc = [(_dot_nt(q_rows[t], _tile(kr, t).astype(BF16)) * din[t]).astype(BF16) for t in range(PAIRS)]
            qd_heads = [_split_heads(_tile(qd, t)) for t in range(PAIRS)]
            kd_heads = [_split_heads(_tile(kd, t)) for t in range(PAIRS)]
            state_b = [st_ref[j, t].astype(BF16) for t in range(PAIRS)]
            gst_b = [gsts[t].astype(BF16) for t in range(PAIRS)]
            vs = [p_ref[rows, OFF_RV + h * RET_V_DIM:OFF_RV + (h + 1) * RET_V_DIM].astype(BF16) for h in range(RET_HEADS)]
            rzs = [p_ref[rows, OFF_RZ + h * RET_V_DIM:OFF_RZ + (h + 1) * RET_V_DIM] for h in range(RET_HEADS)]
            drs = [dmix_ref[rows, ATT_WIDTH + h * RET_V_DIM:ATT_WIDTH + (h + 1) * RET_V_DIM] for h in range(RET_HEADS)]
            gains = [gain_ref[:, c] for c in head_cols]
            lhs = [jnp.concatenate([sc[t][hh * BLK:(hh + 1) * BLK], qd_heads[t][hh].astype(BF16)], axis=1) for t, hh in heads]
            rhs = [jnp.concatenate([vs[2 * t + hh], state_b[t]], axis=0) for t, hh in heads]
            ons, rstds = _group_norm_all([_dot(l, r) for l, r in zip(lhs, rhs)])
            sig_r = [_sigmoid(z) for z in rzs]
            dgn = [d * (z * g) for d, z, g in zip(drs, rzs, sig_r)]
            dz_parts = [d * (o * gn) * (g * (1.0 + z * (1.0 - g))) for d, o, gn, g, z in zip(drs, ons, gains, sig_r, rzs)]
            dgain_parts = [jnp.sum(d * o, axis=0, keepdims=True) for d, o in zip(dgn, ons)]
            don = [d * gn for d, gn in zip(dgn, gains)]
            mean_don = [jnp.mean(d, axis=1, keepdims=True) for d in don]
            mean_don_on = [jnp.mean(d * o, axis=1, keepdims=True) for d, o in zip(don, ons)]
            dob = [(r * (d - a - o * b)).astype(BF16) for r, d, a, o, b in zip(rstds, don, mean_don, ons, mean_don_on)]
            dlhs = [_dot_nt(d, r) for d, r in zip(dob, rhs)]
            drhs = [_dot_tn(l, d) for l, d in zip(lhs, dob)]
            dkds = [_dot_nt(vs[2 * t + hh], gst_b[t]) for t, hh in heads]
            dv_parts = [drhs[2 * t + hh][0:BLK] + _dot(kd_heads[t][hh].astype(BF16), gst_b[t]) for t, hh in heads]
            das = [(dlhs[2 * t + hh][:, 0:BLK] * din[t][hh * BLK:(hh + 1) * BLK]).astype(BF16) for t, hh in heads]
            new_gsts = [gsts[t] * cd_ref[t] + drhs[2 * t][BLK:] + drhs[2 * t + 1][BLK:] for t in range(PAIRS)]
            dq_parts = [_dot(jnp.concatenate([das[2 * t], das[2 * t + 1]], axis=1), k_rows[t])
                        + jnp.where(low, dlhs[2 * t][:, BLK:], dlhs[2 * t + 1][:, BLK:]) * _tile(qdec, t)
                        for t in range(PAIRS)]
            dk_parts = [_dot_tn(jnp.concatenate([das[2 * t], das[2 * t + 1]], axis=0), q_rows[t])
                        + jnp.where(low, dkds[2 * t], dkds[2 * t + 1]) * _tile(kdec, t) for t in range(PAIRS)]
            drq = _rotate_transposed(jnp.concatenate(dq_parts, axis=1), cos_b, sin_b)
            drk = _rotate_transposed(jnp.concatenate(dk_parts, axis=1) * RET_SCALE, cos_b, sin_b)

            dp_ref[rows, :] = jnp.concatenate(
                [jnp.concatenate(dq_tiles, axis=1), dak, dav, daz, drq, drk] + dv_parts + dz_parts, axis=1).astype(BF16)
            dgain = dgain + jnp.concatenate(dgain_parts, axis=1)
            return dk_both[0:BLK], dv_both[0:BLK], tuple(new_gsts), dgain, dsink

        carry = (dkc_ref[...], dvc_ref[...], tuple(gst_ref[t] for t in range(PAIRS)), dgain_ref[...], dsink_ref[...])
        dkc, dvc, gsts, dgain, dsink = lax.fori_loop(0, nsub, sub, carry)
        dkc_ref[...] = dkc
        dvc_ref[...] = dvc
        gst_ref[...] = jnp.stack(gsts)
        dgain_ref[...] = dgain
        dsink_ref[...] = dsink

    rev_rows = lambda w: pl.BlockSpec((tb, w), lambda i: (nblk - 1 - i, 0))
    prev_kv = pl.BlockSpec((BLK, 2 * ATT_KV_WIDTH), lambda i: (jnp.maximum((nblk - 1 - i) * nsub - 1, 0), kv_cols))
    return _call(
        body, name="mix_bwd", grid=(nblk,),
        out_shape=(
            jax.ShapeDtypeStruct((seq, IN_WIDTH), BF16),
            jax.ShapeDtypeStruct((MIX_WIDTH, D_MODEL), F32),
            jax.ShapeDtypeStruct((1, RET_WIDTH), F32),
            jax.ShapeDtypeStruct((1, LANE), F32),
        ),
        in_specs=[
            rev_rows(IN_WIDTH), prev_kv, rev_rows(D_MODEL), rev_rows(MIX_WIDTH), rev_rows(ATT_WIDTH),
            pl.BlockSpec((nsub,) + state_shape, lambda i: (nblk - 1 - i, 0, 0, 0)),
            rev_rows(RET_QK_WIDTH), rev_rows(RET_QK_WIDTH),
            _const_spec((MIX_WIDTH, D_MODEL)), _const_spec((1, RET_WIDTH)),
            pl.BlockSpec(memory_space=pltpu.SMEM),
            _const_spec((PAIRS, 2 * BLK, BLK)), _const_spec((BLK, RET_QK_WIDTH)), _const_spec((BLK, RET_QK_WIDTH)),
            _const_spec(state_shape), _const_spec((2, 2 * BLK, 2 * BLK)),
        ],
        out_specs=(
            rev_rows(IN_WIDTH), _const_spec((MIX_WIDTH, D_MODEL)), _const_spec((1, RET_WIDTH)), _const_spec((1, LANE)),
        ),
        scratch_shapes=[
            pltpu.VMEM((tb, MIX_WIDTH), F32),
            pltpu.VMEM((tb + BLK, 2 * ATT_KV_WIDTH), F32),
            pltpu.VMEM((BLK, ATT_KV_WIDTH), F32), pltpu.VMEM((BLK, ATT_KV_WIDTH), F32),
            pltpu.VMEM(state_shape, F32),
        ],
        compiler_params=_params(56, ("arbitrary",)),
    )(proj, proj, dxo, mix, o_att, states, cos_t, sin_t, w_out, gn_gain, sinks, decay_in, qdec_t, kdec_t, cd_t, bias_t)


def _in_proj_bwd(dproj, x, dxo, norm_g, w_in, tb):
    seq = x.shape[0]

    def body(dp_ref, x_ref, dxo_ref, g_ref, w_ref, gx_ref, gw_ref, gnorm_ref):
        i = pl.program_id(0)

        @pl.when(i == 0)
        def _():
            gw_ref[...] = jnp.zeros_like(gw_ref)
            gnorm_ref[...] = jnp.zeros_like(gnorm_ref)

        dp = dp_ref[...]
        xv = x_ref[...]
        r = lax.rsqrt(jnp.mean(xv * xv, axis=1, keepdims=True) + RMS_EPS)
        xn = xv * r
        h = xn * g_ref[...]
        gw_ref[...] += _dot_tn(h.astype(BF16), dp)
        dh = _dot_nt(dp, w_ref[...])
        gnorm_ref[...] += jnp.sum(dh * xn, axis=0, keepdims=True)
        u = dh * g_ref[...]
        gx_ref[...] = dxo_ref[...] + r * u - xn * (r * jnp.mean(u * xn, axis=1, keepdims=True))

    rows = lambda w: pl.BlockSpec((tb, w), lambda i: (i, 0))
    return _call(
        body, name="in_proj_bwd", grid=(seq // tb,),
        out_shape=(
            jax.ShapeDtypeStruct((seq, D_MODEL), F32),
            jax.ShapeDtypeStruct((D_MODEL, IN_WIDTH), F32),
            jax.ShapeDtypeStruct((1, D_MODEL), F32),
        ),
        in_specs=[rows(IN_WIDTH), rows(D_MODEL), rows(D_MODEL), _const_spec((1, D_MODEL)),
                  _const_spec((D_MODEL, IN_WIDTH))],
        out_specs=(rows(D_MODEL), _const_spec((D_MODEL, IN_WIDTH)), _const_spec((1, D_MODEL))),
        compiler_params=_params(60, ("arbitrary",)),
    )(dproj, x, dxo, norm_g, w_in)


def _grads_reduce(gw_in, gw_out, pack):
    half_in = D_MODEL // 2
    half_out = SHARD_OUT // 2
    A_IN, A_OUT, B_IN, B_OUT, C_IN, C_OUT, PACK, N_SEMS = 0, 1, 5, 9, 13, 14, 15, 23

    def body(gwi_hbm, gwo_hbm, pack_ref, fin_ref, fout_ref, packsum_ref,
             mine_in, sib_in, mine_out, sib_out, b_in, b_out, packs, send_sems, recv_sems, local_sems):
        x, y, c = lax.axis_index("x"), lax.axis_index("y"), lax.axis_index("c")
        chip = 2 * x + y
        dev = 2 * chip + c
        sibling = (x, y, 1 - c)

        def remote(src, dst, send_k, recv_k, to):
            return pltpu.make_async_remote_copy(src_ref=src, dst_ref=dst, send_sem=send_sems.at[send_k],
                                                recv_sem=recv_sems.at[recv_k], device_id=to, device_id_type=MESH_ID)

        def in_rows(core):
            return pl.ds(pl.multiple_of(core * half_in, half_in), half_in)

        def out_rows(j, core):
            return pl.ds(pl.multiple_of(j * SHARD_OUT + core * half_out, half_out), half_out)

        packs[dev] = pack_ref[...]
        for d in range(N_DEV):
            to = (d // 4, (d // 2) % 2, d % 2)

            @pl.when(d != dev)
            def _():
                remote(pack_ref, packs.at[dev], PACK + d, PACK + dev, to).start()

        local = [pltpu.make_async_copy(gwi_hbm.at[in_rows(c), :], mine_in, local_sems.at[0])]
        local += [pltpu.make_async_copy(gwo_hbm.at[out_rows(j, c), :], mine_out.at[j], local_sems.at[1 + j])
                  for j in range(N_CHIPS)]
        for cp in local:
            cp.start()
        stage_a = [remote(gwi_hbm.at[in_rows(1 - c), :], sib_in, A_IN, A_IN, sibling)]
        stage_a += [remote(gwo_hbm.at[out_rows(j, 1 - c), :], sib_out.at[j], A_OUT + j, A_OUT + j, sibling)
                    for j in range(N_CHIPS)]
        for cp in stage_a:
            cp.start()
        for cp in local:
            cp.wait()
        for cp in stage_a:
            cp.wait_recv()

        def add_rows(k, carry):
            rows = pl.ds(pl.multiple_of(k * 64, 64), 64)
            mine_in[rows, :] = mine_in[rows, :] + sib_in[rows, :]
            return carry

        lax.fori_loop(0, half_in // 64, add_rows, 0)
        for j in range(N_CHIPS):
            mine_out[j] = mine_out[j] + sib_out[j]

        for j in range(N_CHIPS):
            window = mine_in.at[:, WIN_START[j]:WIN_START[j] + SHARD_PAD]
            to = (j // 2, j % 2, c)

            @pl.when(j != chip)
            def _():
                remote(window, b_in.at[chip], B_IN + j, B_IN + chip, to).start()
                remote(mine_out.at[j], b_out.at[chip], B_OUT + j, B_OUT + chip, to).start()

            @pl.when(j == chip)
            def _():
                b_in[j] = mine_in[:, WIN_START[j]:WIN_START[j] + SHARD_PAD]
                b_out[j] = mine_out[j]

        for j in range(N_CHIPS):
            @pl.when(j != chip)
            def _():
                remote(b_in.at[j], b_in.at[j], B_IN + j, B_IN + j, sibling).wait_recv()
                remote(b_out.at[j], b_out.at[j], B_OUT + j, B_OUT + j, sibling).wait_recv()

        fin_ref[in_rows(c), :] = (b_in[0] + b_in[1]) + (b_in[2] + b_in[3])
        fout_ref[pl.ds(pl.multiple_of(c * half_out, half_out), half_out), :] = (b_out[0] + b_out[1]) + (b_out[2] + b_out[3])
        mine_half = fin_ref.at[in_rows(c), :]
        mine_half_out = fout_ref.at[pl.ds(pl.multiple_of(c * half_out, half_out), half_out), :]
        stage_c = [remote(mine_half, mine_half, C_IN, C_IN, sibling),
                   remote(mine_half_out, mine_half_out, C_OUT, C_OUT, sibling)]
        for cp in stage_c:
            cp.start()
        other_half = fin_ref.at[in_rows(1 - c), :]
        other_half_out = fout_ref.at[pl.ds(pl.multiple_of((1 - c) * half_out, half_out), half_out), :]
        remote(other_half, other_half, C_IN, C_IN, sibling).wait_recv()
        remote(other_half_out, other_half_out, C_OUT, C_OUT, sibling).wait_recv()

        for d in range(N_DEV):
            @pl.when(d != dev)
            def _():
                remote(pack_ref, packs.at[d], PACK + d, PACK + d, sibling).wait_recv()
        total = packs[0]
        for d in range(1, N_DEV):
            total = total + packs[d]
        packsum_ref[...] = total

        for cp in stage_a + stage_c:
            cp.wait_send()
        for j in range(N_CHIPS):
            @pl.when(j != chip)
            def _():
                remote(b_in.at[j], b_in.at[j], B_IN + j, B_IN + j, sibling).wait_send()
                remote(b_out.at[j], b_out.at[j], B_OUT + j, B_OUT + j, sibling).wait_send()
        for d in range(N_DEV):
            @pl.when(d != dev)
            def _():
                remote(pack_ref, packs.at[d], PACK + d, PACK + d, sibling).wait_send()

    vmem = pl.BlockSpec(memory_space=pltpu.VMEM)
    hbm = pl.BlockSpec(memory_space=pl.ANY)
    return _call(
        body, name="grads_reduce",
        out_shape=(jax.ShapeDtypeStruct((D_MODEL, SHARD_PAD), F32), jax.ShapeDtypeStruct((SHARD_OUT, D_MODEL), F32),
                   jax.ShapeDtypeStruct((PACK_ROWS, D_MODEL), F32)),
        in_specs=[hbm, hbm, vmem], out_specs=(vmem, vmem, vmem),
        scratch_shapes=[
            pltpu.VMEM((half_in, IN_WIDTH), F32), pltpu.VMEM((half_in, IN_WIDTH), F32),
            pltpu.VMEM((N_CHIPS, half_out, D_MODEL), F32), pltpu.VMEM((N_CHIPS, half_out, D_MODEL), F32),
            pltpu.VMEM((N_CHIPS, half_in, SHARD_PAD), F32), pltpu.VMEM((N_CHIPS, half_out, D_MODEL), F32),
            pltpu.VMEM((N_DEV, PACK_ROWS, D_MODEL), F32),
            pltpu.SemaphoreType.DMA((N_SEMS,)), pltpu.SemaphoreType.DMA((N_SEMS,)),
            pltpu.SemaphoreType.DMA((1 + N_CHIPS,)),
        ],
        compiler_params=_params(48),
    )(gw_in, gw_out, pack)


def _adamw(name, w, g, m, v, tb, unshift):
    rows, cols = w.shape

    def body(w_ref, g_ref, m_ref, v_ref, go_ref, d_ref, mo_ref, vo_ref):
        gv = g_ref[...]
        if unshift:
            gv = pltpu.roll(gv, lax.axis_index("y") * (cols - SHARD_SHIFT), 1)
        mn = ADAM_B1 * m_ref[...] + (1.0 - ADAM_B1) * gv
        vn = ADAM_B2 * v_ref[...] + (1.0 - ADAM_B2) * (gv * gv)
        m_hat = mn / (1.0 - ADAM_B1 ** ADAM_STEP)
        v_hat = vn / (1.0 - ADAM_B2 ** ADAM_STEP)
        go_ref[...] = gv
        d_ref[...] = -ADAM_LR * (m_hat / (jnp.sqrt(v_hat) + ADAM_EPS) + ADAM_WD * w_ref[...])
        mo_ref[...] = mn
        vo_ref[...] = vn

    spec = pl.BlockSpec((tb, cols), lambda i: (i, 0))
    shape = jax.ShapeDtypeStruct((rows, cols), F32)
    return _call(
        body, name=name, grid=(rows // tb,), out_shape=(shape,) * 4,
        in_specs=[spec] * 4, out_specs=(spec,) * 4,
        compiler_params=_params(32, ("arbitrary",)),
    )(w, g, m, v)


def _pad_cols(a, width):
    return jnp.pad(a, ((0, 0), (0, width - a.shape[1])))


def _pack_rows(rows):
    parts = [jnp.pad(r.reshape(1, -1), ((0, 7), (0, D_MODEL - r.size))) for r in rows]
    parts += [jnp.zeros((8, D_MODEL), F32)] * (PACK_PARTS - len(rows))
    return jnp.concatenate(parts, axis=0)


def kernel(x, norm_g, w_in, att_sinks, ret_gn_g, w_out, final_g, loss_target, m_norm_g, m_w_in, m_att_sinks, m_ret_gn_g, m_w_out, m_final_g, v_norm_g, v_w_in, v_att_sinks, v_ret_gn_g, v_w_out, v_final_g):
    seq = x.shape[1]
    xs, tgt = x[0], loss_target[0]
    final_g2 = final_g.reshape(1, D_MODEL)
    tables = _tables(seq)

    w_in_full, w_out_full = _weights_allgather(_pad_cols(w_in[0], SHARD_PAD), w_out[0])
    proj = _in_proj(xs, norm_g, w_in_full, min(512, seq))
    mix, dxo, states, loss_part, gfin, o_att = _mix_fwd(proj, xs, tgt, w_out_full, final_g2, ret_gn_g, att_sinks,
                                                        tables, min(256, seq))
    dproj, gw_out, dgain, dsink = _mix_bwd(proj, dxo, mix, o_att, states, w_out_full, ret_gn_g, att_sinks, tables,
                                           min(256, seq))
    grad_x, gw_in, gnorm = _in_proj_bwd(dproj, xs, dxo, norm_g, w_in_full, min(256, seq))

    pack = _pack_rows([gnorm, gfin, dgain, dsink, loss_part[0:1, :]])
    g_in, g_out, sums = _grads_reduce(gw_in, gw_out, pack)

    small_w = _pack_rows([norm_g, final_g, ret_gn_g, att_sinks])
    small_m = _pack_rows([m_norm_g, m_final_g, m_ret_gn_g, m_att_sinks])
    small_v = _pack_rows([v_norm_g, v_final_g, v_ret_gn_g, v_att_sinks])
    res_in = _adamw("adamw_w_in", _pad_cols(w_in[0], SHARD_PAD), g_in, _pad_cols(m_w_in[0], SHARD_PAD),
                    _pad_cols(v_w_in[0], SHARD_PAD), 256, True)
    res_out = _adamw("adamw_w_out", w_out[0], g_out, m_w_out[0], v_w_out[0], SHARD_OUT, False)
    res_small = _adamw("adamw_small", small_w, sums, small_m, small_v, PACK_ROWS, False)

    outs = []
    for k in range(4):
        a = res_small[k]
        outs.append((a[0:1, :], res_in[k][None, :, :SHARD_IN], a[24:25, :ATT_HEADS], a[16:17, :RET_WIDTH],
                     res_out[k][None], a[8, :]))
    loss = sums[32, 0]
    return (loss, grad_x[None], *outs[0], *outs[1], *outs[2], *outs[3])
```

```python
import jax
import jax.numpy as jnp
import numpy as np
from jax import lax
from jax.experimental import pallas as pl
from jax.experimental.pallas import tpu as pltpu

F32 = jnp.float32
BF16 = jnp.bfloat16

D_MODEL = 1024
ATT_HEADS = 8
ATT_KV_HEADS = 2
ATT_HEAD_DIM = 64
RET_HEADS = 4
RET_QK_DIM = 64
RET_V_DIM = 128
BLK = 128
ROT_BASE = 10000.0
RMS_EPS = 1e-6
GN_EPS = 1e-6
NEG_INF = -1e30
ATT_SCALE = ATT_HEAD_DIM ** -0.5
RET_SCALE = RET_QK_DIM ** -0.5

ATT_WIDTH = ATT_HEADS * ATT_HEAD_DIM
ATT_KV_WIDTH = ATT_KV_HEADS * ATT_HEAD_DIM
RET_QK_WIDTH = RET_HEADS * RET_QK_DIM
RET_WIDTH = RET_HEADS * RET_V_DIM
MIX_WIDTH = ATT_WIDTH + RET_WIDTH
OFF_AQ = 0
OFF_AK = OFF_AQ + ATT_WIDTH
OFF_AV = OFF_AK + ATT_KV_WIDTH
OFF_AZ = OFF_AV + ATT_KV_WIDTH
OFF_RQ = OFF_AZ + ATT_WIDTH
OFF_RK = OFF_RQ + RET_QK_WIDTH
OFF_RV = OFF_RK + RET_QK_WIDTH
OFF_RZ = OFF_RV + RET_WIDTH
IN_WIDTH = OFF_RZ + RET_WIDTH

LANE = 128
HALF_LANE = LANE // 2
PAIRS = RET_QK_WIDTH // LANE
assert ATT_HEAD_DIM == HALF_LANE and RET_QK_DIM == HALF_LANE and RET_V_DIM == LANE and ATT_KV_WIDTH == LANE

N_CHIPS = 4
N_DEV = 8
SHARD_IN = IN_WIDTH // N_CHIPS
SHARD_PAD = 768
SHARD_SHIFT = SHARD_PAD - SHARD_IN
WIN_START = tuple((j * SHARD_IN) // LANE * LANE for j in range(N_CHIPS))
SHARD_OUT = MIX_WIDTH // N_CHIPS

PACK_PARTS = 5
PACK_ROWS = 8 * PACK_PARTS

ADAM_LR = 0.001
ADAM_B1 = 0.9
ADAM_B2 = 0.999
ADAM_EPS = 1e-08
ADAM_WD = 0.01
ADAM_STEP = 10

VMEM_CAP = 64 * 1024 * 1024
MESH_ID = pl.DeviceIdType.MESH


def _call(body, **kw):
    return pl.pallas_call(body, **kw)


def _params(vmem_mb, semantics=None):
    assert vmem_mb * 1024 * 1024 < VMEM_CAP
    return pltpu.CompilerParams(dimension_semantics=semantics, vmem_limit_bytes=vmem_mb * 1024 * 1024)


def _dot(a, b):
    return jnp.dot(a, b, preferred_element_type=F32)


def _dot_nt(a, b):
    return lax.dot_general(a, b, (((1,), (1,)), ((), ())), preferred_element_type=F32)


def _dot_tn(a, b):
    return lax.dot_general(a, b, (((0,), (0,)), ((), ())), preferred_element_type=F32)


def _sigmoid(z):
    return 1.0 / (1.0 + jnp.exp(-z))


def _const_spec(shape):
    nd = len(shape)
    return pl.BlockSpec(shape, lambda i: (0,) * nd)


def _tables(seq):
    f32 = np.float32
    pos = np.arange(seq, dtype=f32)
    theta = (f32(1.0) / (f32(ROT_BASE) ** np.linspace(0.0, 1.0, RET_QK_DIM // 2, dtype=f32))).astype(f32)
    ang = (pos[:, None] * theta[None, :]).astype(f32)
    cos, sin = np.cos(ang), np.sin(ang)
    cos2 = np.repeat(cos, 2, axis=1)
    sin2 = np.stack([-sin, sin], axis=-1).reshape(seq, RET_QK_DIM)
    cos_t = np.tile(cos2, (1, 2))
    sin_t = np.tile(sin2, (1, 2))

    log_gamma = np.log(f32(1.0) - f32(2.0) ** (f32(-5.0) - np.arange(RET_HEADS, dtype=f32))).astype(f32)
    idx = np.arange(BLK, dtype=f32)
    rel = idx[:, None] - idx[None, :]
    decay_in = np.where(rel >= 0, np.exp(log_gamma[:, None, None] * np.maximum(rel, f32(0.0))), f32(0.0))
    k_dec = np.exp(log_gamma[:, None] * (BLK - 1 - idx)[None, :])
    q_dec = np.exp(log_gamma[:, None] * (idx + 1)[None, :])
    chunk_decay = np.exp(log_gamma * f32(BLK))
    kdec_t = np.repeat(k_dec.T, RET_QK_DIM, axis=1)
    qdec_t = np.repeat(q_dec.T, RET_QK_DIM, axis=1)
    cd_t = np.broadcast_to(chunk_decay[:, None, None], (RET_HEADS, RET_QK_DIM, RET_V_DIM))
    decay_in = decay_in.reshape(PAIRS, 2 * BLK, BLK)
    cd_t = cd_t.reshape(PAIRS, 2 * RET_QK_DIM, RET_V_DIM)

    key = np.arange(2 * BLK)[:, None]
    query = np.arange(2 * BLK)[None, :] % BLK
    diff = query + BLK - key
    valid = (diff >= 0) & (diff < BLK)
    bias = np.stack([np.where(valid & (key >= BLK), 0.0, NEG_INF), np.where(valid, 0.0, NEG_INF)])
    return tuple(jnp.asarray(np.ascontiguousarray(a), F32) for a in (cos_t, sin_t, decay_in, qdec_t, kdec_t, cd_t, bias))


def _low_lanes(shape):
    lane = lax.broadcasted_iota(jnp.int32, shape, len(shape) - 1)
    return (lane & HALF_LANE) == 0


def _split_heads(t):
    low = _low_lanes(t.shape)
    zero = jnp.zeros_like(t)
    return jnp.where(low, t, zero), jnp.where(low, zero, t)


def _swap_pairs(t):
    lane = lax.broadcasted_iota(jnp.int32, t.shape, 1)
    nxt = pltpu.roll(t, t.shape[1] - 1, 1)
    prv = pltpu.roll(t, 1, 1)
    return jnp.where((lane & 1) == 0, nxt, prv)


def _per_tile(fn, t):
    return jnp.concatenate([fn(_tile(t, i)) for i in range(t.shape[1] // LANE)], axis=1)


def _rotate(t, cos_t, sin_t):
    return _per_tile(lambda a: a * cos_t + _swap_pairs(a) * sin_t, t)


def _rotate_transposed(d, cos_t, sin_t):
    return _per_tile(lambda a: a * cos_t + _swap_pairs(a * sin_t), d)


def _kv_operands(cat):
    low = _low_lanes(cat.shape)
    swapped = pltpu.roll(cat, HALF_LANE, 1)
    zero = jnp.zeros_like(cat)
    pick = lambda a, b: jnp.where(low, a, b).astype(BF16)
    return ((pick(cat, zero), pick(zero, swapped)), (pick(swapped, zero), pick(zero, cat)))


def _stack_tiles(t, first_tile):
    a = t[:, first_tile * LANE:(first_tile + 1) * LANE]
    b = t[:, (first_tile + 1) * LANE:(first_tile + 2) * LANE]
    return jnp.concatenate([a, b], axis=0)


def _sink_rows(sinks_ref, group):
    first = lax.broadcasted_iota(jnp.int32, (1, 2 * BLK), 1) < BLK

    def row(h0, h1):
        return jnp.where(first, sinks_ref[0, group * 4 + h0], sinks_ref[0, group * 4 + h1])
    return row(0, 2), row(1, 3)


ATT_PROBLEMS = tuple((g, hi) for g in range(ATT_KV_HEADS) for hi in range(2))


def _attn_probs(qs, k_ops, bias, sinks_ref):
    sink = [_sink_rows(sinks_ref, g)[hi] for g, hi in ATT_PROBLEMS]
    s = [_dot_nt(k_ops[g][hi], qs[g]) + bias for g, hi in ATT_PROBLEMS]
    m = [jnp.maximum(jnp.max(si, axis=0, keepdims=True), ki) for si, ki in zip(s, sink)]
    e = [jnp.exp(si - mi) for si, mi in zip(s, m)]
    es = [jnp.exp(ki - mi) for ki, mi in zip(sink, m)]
    inv = [1.0 / (jnp.sum(ei, axis=0, keepdims=True) + esi) for ei, esi in zip(e, es)]
    return [ei * ii for ei, ii in zip(e, inv)], [esi * ii for esi, ii in zip(es, inv)]


def _group_norm_all(outs):
    mu = [jnp.mean(o, axis=1, keepdims=True) for o in outs]
    xc = [o - m for o, m in zip(outs, mu)]
    var = [jnp.mean(c * c, axis=1, keepdims=True) for c in xc]
    rstd = [lax.rsqrt(v + GN_EPS) for v in var]
    return [c * r for c, r in zip(xc, rstd)], rstd


def _retention_operands(p_ref, rows, cos_b, sin_b, qdec, kdec):
    qr = _rotate(p_ref[rows, OFF_RQ:OFF_RQ + RET_QK_WIDTH], cos_b, sin_b)
    kr = _rotate(p_ref[rows, OFF_RK:OFF_RK + RET_QK_WIDTH], cos_b, sin_b) * RET_SCALE
    return qr, kr, qr * qdec, kr * kdec


def _tile(t, i):
    return t[:, i * LANE:(i + 1) * LANE]


def _weights_allgather(w_in_shard, w_out_shard):
    half_in = D_MODEL // 2
    half_out = SHARD_OUT // 2

    def body(wi_ref, wo_ref, win_ref, wout_ref, sh_ref, pad_ref, blk_ref, blko_ref, send_sems, recv_sems):
        x, y, c = lax.axis_index("x"), lax.axis_index("y"), lax.axis_index("c")
        me, sibling = (x, y, c), (x, y, 1 - c)
        chips = [(1 - x, y), (x, 1 - y), (1 - x, 1 - y)]

        def rows_in(px, py, pc):
            return sh_ref.at[pl.ds(pl.multiple_of((4 * px + 2 * py + pc) * half_in, half_in), half_in), :]

        def rows_out(px, py, pc):
            return wout_ref.at[pl.ds(pl.multiple_of((4 * px + 2 * py + pc) * half_out, half_out), half_out), :]

        def copy(k, rows, block, to, src=None):
            return pltpu.make_async_remote_copy(
                src_ref=rows(*block) if src is None else src, dst_ref=rows(*block),
                send_sem=send_sems.at[k], recv_sem=recv_sems.at[k], device_id=to, device_id_type=MESH_ID)

        pad_ref[...] = jnp.zeros_like(pad_ref)
        pad_ref[:, 0:SHARD_IN] = wi_ref[pl.ds(pl.multiple_of(c * half_in, half_in), half_in), :]
        blk_ref[...] = pltpu.roll(pad_ref[...], y * SHARD_SHIFT, 1).astype(BF16)
        blko_ref[...] = wo_ref[pl.ds(pl.multiple_of(c * half_out, half_out), half_out), :].astype(BF16)
        rows_in(*me)[...] = blk_ref[...]
        rows_out(*me)[...] = blko_ref[...]

        sets = ((0, rows_in, blk_ref), (7, rows_out, blko_ref))
        first, passed = [], []
        for base, rows, src in sets:
            first.append(copy(base, rows, me, sibling, src=src))
            first += [copy(base + 1 + j, rows, me, (*chip, c), src=src) for j, chip in enumerate(chips)]
        for cp in first:
            cp.start()
        for base, rows, src in sets:
            for j, chip in enumerate(chips):
                copy(base + 1 + j, rows, (*chip, c), me).wait_recv()
                fwd = copy(base + 4 + j, rows, (*chip, c), sibling)
                fwd.start()
                passed.append(fwd)
        for base, rows, src in sets:
            copy(base, rows, sibling, me).wait_recv()
            for j, chip in enumerate(chips):
                copy(base + 4 + j, rows, (*chip, 1 - c), me).wait_recv()
        for cp in first + passed:
            cp.wait_send()

        def shard(j, lo, hi):
            return sh_ref[j * D_MODEL:(j + 1) * D_MODEL, lo:hi]

        keep = SHARD_PAD - LANE
        for pair in range(2):
            a, b = 2 * pair, 2 * pair + 1
            start = WIN_START[a]
            win_ref[:, start:start + keep] = shard(a, 0, keep)
            win_ref[:, start + keep:start + SHARD_PAD] = shard(a, keep, SHARD_PAD) + shard(b, 0, LANE)
            win_ref[:, start + SHARD_PAD:start + SHARD_PAD + keep] = shard(b, LANE, SHARD_PAD)

    vmem = pl.BlockSpec(memory_space=pltpu.VMEM)
    return _call(
        body, name="weights_allgather",
        out_shape=(jax.ShapeDtypeStruct((D_MODEL, IN_WIDTH), BF16), jax.ShapeDtypeStruct((MIX_WIDTH, D_MODEL), BF16)),
        in_specs=[vmem, vmem], out_specs=(vmem, vmem),
        scratch_shapes=[
            pltpu.VMEM((N_CHIPS * D_MODEL, SHARD_PAD), BF16),
            pltpu.VMEM((half_in, SHARD_PAD), F32),
            pltpu.VMEM((half_in, SHARD_PAD), BF16),
            pltpu.VMEM((half_out, D_MODEL), BF16),
            pltpu.SemaphoreType.DMA((14,)),
            pltpu.SemaphoreType.DMA((14,)),
        ],
        compiler_params=_params(40),
    )(w_in_shard, w_out_shard)


def _in_proj(x, norm_g, w_in, tb):
    seq = x.shape[0]

    def body(x_ref, g_ref, w_ref, p_ref):
        xv = x_ref[...]
        r = lax.rsqrt(jnp.mean(xv * xv, axis=1, keepdims=True) + RMS_EPS)
        h = (xv * r) * g_ref[...]
        p_ref[...] = _dot(h.astype(BF16), w_ref[...])

    return _call(
        body, name="in_proj", grid=(seq // tb,),
        out_shape=jax.ShapeDtypeStruct((seq, IN_WIDTH), F32),
        in_specs=[pl.BlockSpec((tb, D_MODEL), lambda i: (i, 0)), _const_spec((1, D_MODEL)),
                  _const_spec((D_MODEL, IN_WIDTH))],
        out_specs=pl.BlockSpec((tb, IN_WIDTH), lambda i: (i, 0)),
        compiler_params=_params(48, ("arbitrary",)),
    )(x, norm_g, w_in)


def _mix_fwd(proj, x, target, w_out, final_g, gn_gain, sinks, tables, tb):
    seq = x.shape[0]
    nsub = tb // BLK
    cos_t, sin_t, decay_in, qdec_t, kdec_t, cd_t, bias_t = tables

    def body(p_ref, x_ref, t_ref, cos_ref, sin_ref, wout_ref, fg_ref, gain_ref, sinks_ref, din_ref, qdec_ref,
             kdec_ref, cd_ref, bias_ref, mix_ref, dxo_ref, st_ref, loss_ref, gfin_ref, oatt_ref,
             kprev_ref, vprev_ref, state_ref):
        i = pl.program_id(0)

        @pl.when(i == 0)
        def _():
            kprev_ref[...] = jnp.zeros_like(kprev_ref)
            vprev_ref[...] = jnp.zeros_like(vprev_ref)
            state_ref[...] = jnp.zeros_like(state_ref)
            loss_ref[...] = jnp.zeros_like(loss_ref)
            gfin_ref[...] = jnp.zeros_like(gfin_ref)

        def sub(j, carry):
            kp, vp, states = carry
            rows = pl.ds(pl.multiple_of(j * BLK, BLK), BLK)
            bias = bias_ref[jnp.where(jnp.logical_or(i > 0, j > 0), 1, 0)]

            aq = p_ref[rows, OFF_AQ:OFF_AQ + ATT_WIDTH]
            ak = p_ref[rows, OFF_AK:OFF_AK + ATT_KV_WIDTH]
            av = p_ref[rows, OFF_AV:OFF_AV + ATT_KV_WIDTH]
            az = p_ref[rows, OFF_AZ:OFF_AZ + ATT_WIDTH]
            k_ops = _kv_operands(jnp.concatenate([kp, ak], axis=0))
            v_ops = _kv_operands(jnp.concatenate([vp, av], axis=0))
            qs = [(_stack_tiles(aq, 2 * g) * ATT_SCALE).astype(BF16) for g in range(ATT_KV_HEADS)]
            p, _ = _attn_probs(qs, k_ops, bias, sinks_ref)
            o_tiles = []
            for g in range(ATT_KV_HEADS):
                p_cat = jnp.concatenate([p[2 * g].astype(BF16), p[2 * g + 1].astype(BF16)], axis=0)
                o = _dot_tn(p_cat, jnp.concatenate(v_ops[g], axis=0))
                o_tiles += [o[0:BLK], o[BLK:]]
            o_att = jnp.concatenate(o_tiles, axis=1)
            oatt_ref[rows, :] = o_att
            out = [o_att * (az * _sigmoid(az))]

            qr, kr, qd, kd = _retention_operands(p_ref, rows, cos_ref[rows, :], sin_ref[rows, :],
                                                 qdec_ref[...], kdec_ref[...])
            heads = [(t, hh) for t in range(PAIRS) for hh in range(2)]
            sc = [_dot_nt(jnp.concatenate(_split_heads(_tile(qr, t)), axis=0).astype(BF16), _tile(kr, t).astype(BF16))
                  * din_ref[t] for t in range(PAIRS)]
            qd_heads = [_split_heads(_tile(qd, t)) for t in range(PAIRS)]
            state_b = [states[t].astype(BF16) for t in range(PAIRS)]
            vs = [p_ref[rows, OFF_RV + h * RET_V_DIM:OFF_RV + (h + 1) * RET_V_DIM].astype(BF16) for h in range(RET_HEADS)]
            rzs = [p_ref[rows, OFF_RZ + h * RET_V_DIM:OFF_RZ + (h + 1) * RET_V_DIM] for h in range(RET_HEADS)]
            lhs = [jnp.concatenate([sc[t][hh * BLK:(hh + 1) * BLK].astype(BF16), qd_heads[t][hh].astype(BF16)], axis=1)
                   for t, hh in heads]
            ons, _ = _group_norm_all([_dot(lhs[2 * t + hh], jnp.concatenate([vs[2 * t + hh], state_b[t]], axis=0))
                                      for t, hh in heads])
            out += [(ons[h] * gain_ref[:, h * RET_V_DIM:(h + 1) * RET_V_DIM]) * (rzs[h] * _sigmoid(rzs[h]))
                    for h in range(RET_HEADS)]
            new_states = [states[t] * cd_ref[t]
                          + _dot_tn(jnp.concatenate(_split_heads(_tile(kd, t)), axis=0).astype(BF16),
                                    jnp.concatenate([vs[2 * t], vs[2 * t + 1]], axis=0)) for t in range(PAIRS)]
            mix_ref[rows, :] = jnp.concatenate(out, axis=1).astype(BF16)
            st_ref[j] = jnp.stack(states)
            return ak, av, tuple(new_states)

        carry = (kprev_ref[...], vprev_ref[...], tuple(state_ref[t] for t in range(PAIRS)))
        kp, vp, states = lax.fori_loop(0, nsub, sub, carry)
        kprev_ref[...] = kp
        vprev_ref[...] = vp
        state_ref[...] = jnp.stack(states)

        xo = x_ref[...] + _dot(mix_ref[...], wout_ref[...])
        r2 = lax.rsqrt(jnp.mean(xo * xo, axis=1, keepdims=True) + RMS_EPS)
        xn = xo * r2
        err = xn * fg_ref[...] - t_ref[...]
        loss_ref[...] += jnp.sum(err * err) * (0.5 / D_MODEL)
        dy = err * (1.0 / D_MODEL)
        gfin_ref[...] += jnp.sum(dy * xn, axis=0, keepdims=True)
        u = dy * fg_ref[...]
        dxo_ref[...] = r2 * u - xn * (r2 * jnp.mean(u * xn, axis=1, keepdims=True))

    blk_rows = lambda w: pl.BlockSpec((tb, w), lambda i: (i, 0))
    state_shape = (PAIRS, 2 * RET_QK_DIM, RET_V_DIM)
    return _call(
        body, name="mix_fwd", grid=(seq // tb,),
        out_shape=(
            jax.ShapeDtypeStruct((seq, MIX_WIDTH), BF16),
            jax.ShapeDtypeStruct((seq, D_MODEL), F32),
            jax.ShapeDtypeStruct((seq // BLK,) + state_shape, F32),
            jax.ShapeDtypeStruct((8, LANE), F32),
            jax.ShapeDtypeStruct((1, D_MODEL), F32),
            jax.ShapeDtypeStruct((seq, ATT_WIDTH), F32),
        ),
        in_specs=[
            blk_rows(IN_WIDTH), blk_rows(D_MODEL), blk_rows(D_MODEL), blk_rows(LANE), blk_rows(LANE),
            _const_spec((MIX_WIDTH, D_MODEL)), _const_spec((1, D_MODEL)), _const_spec((1, RET_WIDTH)),
            pl.BlockSpec(memory_space=pltpu.SMEM),
            _const_spec((PAIRS, 2 * BLK, BLK)), _const_spec((BLK, RET_QK_WIDTH)), _const_spec((BLK, RET_QK_WIDTH)),
            _const_spec(state_shape), _const_spec((2, 2 * BLK, 2 * BLK)),
        ],
        out_specs=(
            blk_rows(MIX_WIDTH), blk_rows(D_MODEL),
            pl.BlockSpec((nsub,) + state_shape, lambda i: (i, 0, 0, 0)),
            _const_spec((8, LANE)), _const_spec((1, D_MODEL)), blk_rows(ATT_WIDTH),
        ),
        scratch_shapes=[
            pltpu.VMEM((BLK, ATT_KV_WIDTH), F32), pltpu.VMEM((BLK, ATT_KV_WIDTH), F32),
            pltpu.VMEM(state_shape, F32),
        ],
        compiler_params=_params(48, ("arbitrary",)),
    )(proj, x, target, cos_t, sin_t, w_out, final_g, gn_gain, sinks, decay_in, qdec_t, kdec_t, cd_t, bias_t)


def _mix_bwd(proj, dxo, mix, o_att, states, w_out, gn_gain, sinks, tables, tb):
    seq = dxo.shape[0]
    nsub = tb // BLK
    nblk = seq // tb
    cos_t, sin_t, decay_in, qdec_t, kdec_t, cd_t, bias_t = tables
    kv_cols = OFF_AK // (2 * ATT_KV_WIDTH)
    state_shape = (PAIRS, 2 * RET_QK_DIM, RET_V_DIM)

    def body(p_ref, pkv_ref, dxo_ref, mix_ref, oatt_ref, st_ref, cos_ref, sin_ref, wout_ref, gain_ref, sinks_ref, din_ref,
             qdec_ref, kdec_ref, cd_ref, bias_ref, dp_ref, gwout_ref, dgain_ref, dsink_ref,
             dmix_ref, kv_ref, dkc_ref, dvc_ref, gst_ref):
        i = pl.program_id(0)
        blk = nblk - 1 - i

        @pl.when(i == 0)
        def _():
            gwout_ref[...] = jnp.zeros_like(gwout_ref)
            dgain_ref[...] = jnp.zeros_like(dgain_ref)
            dsink_ref[...] = jnp.zeros_like(dsink_ref)
            dkc_ref[...] = jnp.zeros_like(dkc_ref)
            dvc_ref[...] = jnp.zeros_like(dvc_ref)
            gst_ref[...] = jnp.zeros_like(gst_ref)

        dxo_b = dxo_ref[...].astype(BF16)
        dmix_ref[...] = _dot_nt(dxo_b, wout_ref[...])
        gwout_ref[...] += _dot_tn(mix_ref[...], dxo_b)
        kv_ref[0:BLK, :] = pkv_ref[...]
        kv_ref[BLK:, :] = p_ref[:, OFF_AK:OFF_AK + 2 * ATT_KV_WIDTH]
        low = _low_lanes((BLK, LANE))
        low2 = _low_lanes((2 * BLK, LANE))
        lane = lax.broadcasted_iota(jnp.int32, (1, LANE), 1)

        def sub(jj, carry):
            dkc, dvc, gsts, dgain, dsink = carry
            j = nsub - 1 - jj
            rows = pl.ds(pl.multiple_of(j * BLK, BLK), BLK)
            both = pl.ds(pl.multiple_of(j * BLK, BLK), 2 * BLK)
            bias = bias_ref[jnp.where(jnp.logical_or(blk > 0, j > 0), 1, 0)]

            aq = p_ref[rows, OFF_AQ:OFF_AQ + ATT_WIDTH]
            az = p_ref[rows, OFF_AZ:OFF_AZ + ATT_WIDTH]
            k_ops = _kv_operands(kv_ref[both, 0:ATT_KV_WIDTH])
            v_ops = _kv_operands(kv_ref[both, ATT_KV_WIDTH:2 * ATT_KV_WIDTH])
            da = dmix_ref[rows, 0:ATT_WIDTH]
            sig = _sigmoid(az)
            d_o = da * (az * sig)
            qs = [(_stack_tiles(aq, 2 * g) * ATT_SCALE).astype(BF16) for g in range(ATT_KV_HEADS)]
            dos = [_stack_tiles(d_o, 2 * g).astype(BF16) for g in range(ATT_KV_HEADS)]
            p, p_sink = _attn_probs(qs, k_ops, bias, sinks_ref)
            dpr = [_dot_nt(v_ops[g][hi], dos[g]) for g, hi in ATT_PROBLEMS]
            delta = [jnp.sum(pi * di, axis=0, keepdims=True) for pi, di in zip(p, dpr)]
            ds = [(pi * (di - ti)).astype(BF16) for pi, di, ti in zip(p, dpr, delta)]
            for (g, hi), ki, ti in zip(ATT_PROBLEMS, p_sink, delta):
                sink_part = ki * ti
                for half in range(2):
                    tot = jnp.sum(sink_part[:, half * BLK:(half + 1) * BLK], axis=1, keepdims=True)
                    dsink = dsink - jnp.where(lane == 4 * g + 2 * half + hi, tot, 0.0)
            dq_tiles, dk_sums, dv_sums = [], [], []
            for g in range(ATT_KV_HEADS):
                ds_cat = jnp.concatenate([ds[2 * g], ds[2 * g + 1]], axis=0)
                p_cat = jnp.concatenate([p[2 * g].astype(BF16), p[2 * g + 1].astype(BF16)], axis=0)
                dqs = _dot_tn(ds_cat, jnp.concatenate(k_ops[g], axis=0)) * ATT_SCALE
                dq_tiles += [dqs[0:BLK], dqs[BLK:]]
                dk_sums.append(_dot(ds_cat, qs[g]))
                dv_sums.append(_dot(p_cat, dos[g]))
            daz = da * oatt_ref[rows, :] * (sig * (1.0 + az * (1.0 - sig)))

            def kv_grad(sums):
                (a0, b0), (a1, b1) = [(s[0:2 * BLK], s[2 * BLK:]) for s in sums]
                return jnp.where(low2, a0, b1) + pltpu.roll(jnp.where(low2, a1, b0), HALF_LANE, 1)

            dk_both, dv_both = kv_grad(dk_sums), kv_grad(dv_sums)
            dak = dk_both[BLK:] + dkc
            dav = dv_both[BLK:] + dvc

            cos_b, sin_b = cos_ref[rows, :], sin_ref[rows, :]
            qdec, kdec = qdec_ref[...], kdec_ref[...]
            qr, kr, qd, kd = _retention_operands(p_ref, rows, cos_b, sin_b, qdec, kdec)
            heads = [(t, hh) for t in range(PAIRS) for hh in range(2)]
            head_cols = [slice(h * RET_V_DIM, (h + 1) * RET_V_DIM) for h in range(RET_HEADS)]
            q_rows = [jnp.concatenate(_split_heads(_tile(qr, t)), axis=0).astype(BF16) for t in range(PAIRS)]
            k_rows = [jnp.concatenate(_split_heads(_tile(kr, t)), axis=0).astype(BF16) for t in range(PAIRS)]
            din = [din_ref[t] for t in range(PAIRS)]
            sc = [(_dot_nt(q_rows[t], _tile(kr, t).astype(BF16)) * din[t]).astype(BF16) for t in range(PAIRS)]
            qd_heads = [_split_heads(_tile(qd, t)) for t in range(PAIRS)]
            kd_heads = [_split_heads(_tile(kd, t)) for t in range(PAIRS)]
            state_b = [st_ref[j, t].astype(BF16) for t in range(PAIRS)]
            gst_b = [gsts[t].astype(BF16) for t in range(PAIRS)]
            vs = [p_ref[rows, OFF_RV + h * RET_V_DIM:OFF_RV + (h + 1) * RET_V_DIM].astype(BF16) for h in range(RET_HEADS)]
            rzs = [p_ref[rows, OFF_RZ + h * RET_V_DIM:OFF_RZ + (h + 1) * RET_V_DIM] for h in range(RET_HEADS)]
            drs = [dmix_ref[rows, ATT_WIDTH + h * RET_V_DIM:ATT_WIDTH + (h + 1) * RET_V_DIM] for h in range(RET_HEADS)]
            gains = [gain_ref[:, c] for c in head_cols]
            lhs = [jnp.concatenate([sc[t][hh * BLK:(hh + 1) * BLK], qd_heads[t][hh].astype(BF16)], axis=1) for t, hh in heads]
            rhs = [jnp.concatenate([vs[2 * t + hh], state_b[t]], axis=0) for t, hh in heads]
            ons, rstds = _group_norm_all([_dot(l, r) for l, r in zip(lhs, rhs)])
            sig_r = [_sigmoid(z) for z in rzs]
            dgn = [d * (z * g) for d, z, g in zip(drs, rzs, sig_r)]
            dz_parts = [d * (o * gn) * (g * (1.0 + z * (1.0 - g))) for d, o, gn, g, z in zip(drs, ons, gains, sig_r, rzs)]
            dgain_parts = [jnp.sum(d * o, axis=0, keepdims=True) for d, o in zip(dgn, ons)]
            don = [d * gn for d, gn in zip(dgn, gains)]
            mean_don = [jnp.mean(d, axis=1, keepdims=True) for d in don]
            mean_don_on = [jnp.mean(d * o, axis=1, keepdims=True) for d, o in zip(don, ons)]
            dob = [(r * (d - a - o * b)).astype(BF16) for r, d, a, o, b in zip(rstds, don, mean_don, ons, mean_don_on)]
            dlhs = [_dot_nt(d, r) for d, r in zip(dob, rhs)]
            drhs = [_dot_tn(l, d) for l, d in zip(lhs, dob)]
            dkds = [_dot_nt(vs[2 * t + hh], gst_b[t]) for t, hh in heads]
            dv_parts = [drhs[2 * t + hh][0:BLK] + _dot(kd_heads[t][hh].astype(BF16), gst_b[t]) for t, hh in heads]
            das = [(dlhs[2 * t + hh][:, 0:BLK] * din[t][hh * BLK:(hh + 1) * BLK]).astype(BF16) for t, hh in heads]
            new_gsts = [gsts[t] * cd_ref[t] + drhs[2 * t][BLK:] + drhs[2 * t + 1][BLK:] for t in range(PAIRS)]
            dq_parts = [_dot(jnp.concatenate([das[2 * t], das[2 * t + 1]], axis=1), k_rows[t])
                        + jnp.where(low, dlhs[2 * t][:, BLK:], dlhs[2 * t + 1][:, BLK:]) * _tile(qdec, t)
                        for t in range(PAIRS)]
            dk_parts = [_dot_tn(jnp.concatenate([das[2 * t], das[2 * t + 1]], axis=0), q_rows[t])
                        + jnp.where(low, dkds[2 * t], dkds[2 * t + 1]) * _tile(kdec, t) for t in range(PAIRS)]
            drq = _rotate_transposed(jnp.concatenate(dq_parts, axis=1), cos_b, sin_b)
            drk = _rotate_transposed(jnp.concatenate(dk_parts, axis=1) * RET_SCALE, cos_b, sin_b)

            dp_ref[rows, :] = jnp.concatenate(
                [jnp.concatenate(dq_tiles, axis=1), dak, dav, daz, drq, drk] + dv_parts + dz_parts, axis=1).astype(BF16)
            dgain = dgain + jnp.concatenate(dgain_parts, axis=1)
            return dk_both[0:BLK], dv_both[0:BLK], tuple(new_gsts), dgain, dsink

        carry = (dkc_ref[...], dvc_ref[...], tuple(gst_ref[t] for t in range(PAIRS)), dgain_ref[...], dsink_ref[...])
        dkc, dvc, gsts, dgain, dsink = lax.fori_loop(0, nsub, sub, carry)
        dkc_ref[...] = dkc
        dvc_ref[...] = dvc
        gst_ref[...] = jnp.stack(gsts)
        dgain_ref[...] = dgain
        dsink_ref[...] = dsink

    rev_rows = lambda w: pl.BlockSpec((tb, w), lambda i: (nblk - 1 - i, 0))
    prev_kv = pl.BlockSpec((BLK, 2 * ATT_KV_WIDTH), lambda i: (jnp.maximum((nblk - 1 - i) * nsub - 1, 0), kv_cols))
    return _call(
        body, name="mix_bwd", grid=(nblk,),
        out_shape=(
            jax.ShapeDtypeStruct((seq, IN_WIDTH), BF16),
            jax.ShapeDtypeStruct((MIX_WIDTH, D_MODEL), F32),
            jax.ShapeDtypeStruct((1, RET_WIDTH), F32),
            jax.ShapeDtypeStruct((1, LANE), F32),
        ),
        in_specs=[
            rev_rows(IN_WIDTH), prev_kv, rev_rows(D_MODEL), rev_rows(MIX_WIDTH), rev_rows(ATT_WIDTH),
            pl.BlockSpec((nsub,) + state_shape, lambda i: (nblk - 1 - i, 0, 0, 0)),
            rev_rows(LANE), rev_rows(LANE),
            _const_spec((MIX_WIDTH, D_MODEL)), _const_spec((1, RET_WIDTH)),
            pl.BlockSpec(memory_space=pltpu.SMEM),
            _const_spec((PAIRS, 2 * BLK, BLK)), _const_spec((BLK, RET_QK_WIDTH)), _const_spec((BLK, RET_QK_WIDTH)),
            _const_spec(state_shape), _const_spec((2, 2 * BLK, 2 * BLK)),
        ],
        out_specs=(
            rev_rows(IN_WIDTH), _const_spec((MIX_WIDTH, D_MODEL)), _const_spec((1, RET_WIDTH)), _const_spec((1, LANE)),
        ),
        scratch_shapes=[
            pltpu.VMEM((tb, MIX_WIDTH), F32),
            pltpu.VMEM((tb + BLK, 2 * ATT_KV_WIDTH), F32),
            pltpu.VMEM((BLK, ATT_KV_WIDTH), F32), pltpu.VMEM((BLK, ATT_KV_WIDTH), F32),
            pltpu.VMEM(state_shape, F32),
        ],
        compiler_params=_params(56, ("arbitrary",)),
    )(proj, proj, dxo, mix, o_att, states, cos_t, sin_t, w_out, gn_gain, sinks, decay_in, qdec_t, kdec_t, cd_t, bias_t)


def _in_proj_bwd(dproj, x, dxo, norm_g, w_in, tb):
    seq = x.shape[0]

    def body(dp_ref, x_ref, dxo_ref, g_ref, w_ref, gx_ref, gw_ref, gnorm_ref):
        i = pl.program_id(0)

        @pl.when(i == 0)
        def _():
            gw_ref[...] = jnp.zeros_like(gw_ref)
            gnorm_ref[...] = jnp.zeros_like(gnorm_ref)

        dp = dp_ref[...]
        xv = x_ref[...]
        r = lax.rsqrt(jnp.mean(xv * xv, axis=1, keepdims=True) + RMS_EPS)
        xn = xv * r
        h = xn * g_ref[...]
        gw_ref[...] += _dot_tn(h.astype(BF16), dp)
        dh = _dot_nt(dp, w_ref[...])
        gnorm_ref[...] += jnp.sum(dh * xn, axis=0, keepdims=True)
        u = dh * g_ref[...]
        gx_ref[...] = dxo_ref[...] + r * u - xn * (r * jnp.mean(u * xn, axis=1, keepdims=True))

    rows = lambda w: pl.BlockSpec((tb, w), lambda i: (i, 0))
    return _call(
        body, name="in_proj_bwd", grid=(seq // tb,),
        out_shape=(
            jax.ShapeDtypeStruct((seq, D_MODEL), F32),
            jax.ShapeDtypeStruct((D_MODEL, IN_WIDTH), F32),
            jax.ShapeDtypeStruct((1, D_MODEL), F32),
        ),
        in_specs=[rows(IN_WIDTH), rows(D_MODEL), rows(D_MODEL), _const_spec((1, D_MODEL)),
                  _const_spec((D_MODEL, IN_WIDTH))],
        out_specs=(rows(D_MODEL), _const_spec((D_MODEL, IN_WIDTH)), _const_spec((1, D_MODEL))),
        compiler_params=_params(60, ("arbitrary",)),
    )(dproj, x, dxo, norm_g, w_in)


def _grads_reduce(gw_in, gw_out, small):
    half_in = D_MODEL // 2
    half_out = SHARD_OUT // 2
    A_IN, A_OUT, B_IN, B_OUT, C_IN, C_OUT, PACK, N_SEMS = 0, 1, 5, 9, 13, 14, 15, 23
    assert len(small) == PACK_PARTS

    def body(gwi_hbm, gwo_hbm, s0_ref, s1_ref, s2_ref, s3_ref, s4_ref, fin_ref, fout_ref, packsum_ref,
             mine_in, sib_in, mine_out, sib_out, send_in, send_out, b_in, b_out, pack_ref, packs,
             send_sems, recv_sems, local_sems):
        x, y, c = lax.axis_index("x"), lax.axis_index("y"), lax.axis_index("c")
        chip = 2 * x + y
        dev = 2 * chip + c
        sibling = (x, y, 1 - c)
        pack_ref[...] = jnp.zeros_like(pack_ref)
        for k, s_ref in enumerate((s0_ref, s1_ref, s2_ref, s3_ref, s4_ref)):
            pack_ref[8 * k:8 * k + 1, 0:s_ref.shape[1]] = s_ref[0:1, :]

        def remote(src, dst, send_k, recv_k, to):
            return pltpu.make_async_remote_copy(src_ref=src, dst_ref=dst, send_sem=send_sems.at[send_k],
                                                recv_sem=recv_sems.at[recv_k], device_id=to, device_id_type=MESH_ID)

        def in_rows(core):
            return pl.ds(pl.multiple_of(core * half_in, half_in), half_in)

        def out_rows(j, core):
            return pl.ds(pl.multiple_of(j * SHARD_OUT + core * half_out, half_out), half_out)

        packs[dev] = pack_ref[...]
        for d in range(N_DEV):
            to = (d // 4, (d // 2) % 2, d % 2)

            @pl.when(d != dev)
            def _():
                remote(pack_ref, packs.at[dev], PACK + d, PACK + dev, to).start()

        local = [pltpu.make_async_copy(gwi_hbm.at[in_rows(c), :], mine_in, local_sems.at[0])]
        local += [pltpu.make_async_copy(gwo_hbm.at[out_rows(j, c), :], mine_out.at[j], local_sems.at[1 + j])
                  for j in range(N_CHIPS)]
        for cp in local:
            cp.start()
        stage_a = [remote(gwi_hbm.at[in_rows(1 - c), :], sib_in, A_IN, A_IN, sibling)]
        stage_a += [remote(gwo_hbm.at[out_rows(j, 1 - c), :], sib_out.at[j], A_OUT + j, A_OUT + j, sibling)
                    for j in range(N_CHIPS)]
        for cp in stage_a:
            cp.start()
        for cp in local:
            cp.wait()
        for cp in stage_a:
            cp.wait_recv()

        def add_rows(k, carry):
            rows = pl.ds(pl.multiple_of(k * 64, 64), 64)
            mine_in[rows, :] = mine_in[rows, :] + sib_in[rows, :]
            return carry

        lax.fori_loop(0, half_in // 64, add_rows, 0)
        for j in range(N_CHIPS):
            mine_out[j] = mine_out[j] + sib_out[j]

        my_out_rows = pl.ds(pl.multiple_of(c * half_out, half_out), half_out)
        for j in range(N_CHIPS):
            to = (j // 2, j % 2, c)

            @pl.when(j != chip)
            def _():
                send_in[j] = mine_in[:, WIN_START[j]:WIN_START[j] + SHARD_PAD].astype(BF16)
                send_out[j] = mine_out[j].astype(BF16)
                remote(send_in.at[j], b_in.at[chip], B_IN + j, B_IN + chip, to).start()
                remote(send_out.at[j], b_out.at[chip], B_OUT + j, B_OUT + chip, to).start()

            @pl.when(j == chip)
            def _():
                fin_ref[in_rows(c), :] = mine_in[:, WIN_START[j]:WIN_START[j] + SHARD_PAD]
                fout_ref[my_out_rows, :] = mine_out[j]

        for j in range(N_CHIPS):
            @pl.when(j != chip)
            def _():
                remote(b_in.at[j], b_in.at[j], B_IN + j, B_IN + j, sibling).wait_recv()
                remote(b_out.at[j], b_out.at[j], B_OUT + j, B_OUT + j, sibling).wait_recv()
                fin_ref[in_rows(c), :] += b_in[j].astype(F32)
                fout_ref[my_out_rows, :] += b_out[j].astype(F32)

        mine_half = fin_ref.at[in_rows(c), :]
        mine_half_out = fout_ref.at[my_out_rows, :]
        stage_c = [remote(mine_half, mine_half, C_IN, C_IN, sibling),
                   remote(mine_half_out, mine_half_out, C_OUT, C_OUT, sibling)]
        for cp in stage_c:
            cp.start()
        other_half = fin_ref.at[in_rows(1 - c), :]
        other_half_out = fout_ref.at[pl.ds(pl.multiple_of((1 - c) * half_out, half_out), half_out), :]
        remote(other_half, other_half, C_IN, C_IN, sibling).wait_recv()
        remote(other_half_out, other_half_out, C_OUT, C_OUT, sibling).wait_recv()

        for d in range(N_DEV):
            @pl.when(d != dev)
            def _():
                remote(pack_ref, packs.at[d], PACK + d, PACK + d, sibling).wait_recv()
        total = packs[0]
        for d in range(1, N_DEV):
            total = total + packs[d]
        packsum_ref[...] = total

        for cp in stage_a + stage_c:
            cp.wait_send()
        for j in range(N_CHIPS):
            @pl.when(j != chip)
            def _():
                remote(b_in.at[j], b_in.at[j], B_IN + j, B_IN + j, sibling).wait_send()
                remote(b_out.at[j], b_out.at[j], B_OUT + j, B_OUT + j, sibling).wait_send()
        for d in range(N_DEV):
            @pl.when(d != dev)
            def _():
                remote(pack_ref, packs.at[d], PACK + d, PACK + d, sibling).wait_send()

    vmem = pl.BlockSpec(memory_space=pltpu.VMEM)
    hbm = pl.BlockSpec(memory_space=pl.ANY)
    return _call(
        body, name="grads_reduce",
        out_shape=(jax.ShapeDtypeStruct((D_MODEL, SHARD_PAD), F32), jax.ShapeDtypeStruct((SHARD_OUT, D_MODEL), F32),
                   jax.ShapeDtypeStruct((PACK_ROWS, D_MODEL), F32)),
        in_specs=[hbm, hbm] + [vmem] * PACK_PARTS, out_specs=(vmem, vmem, vmem),
        scratch_shapes=[
            pltpu.VMEM((half_in, IN_WIDTH), F32), pltpu.VMEM((half_in, IN_WIDTH), F32),
            pltpu.VMEM((N_CHIPS, half_out, D_MODEL), F32), pltpu.VMEM((N_CHIPS, half_out, D_MODEL), F32),
            pltpu.VMEM((N_CHIPS, half_in, SHARD_PAD), BF16), pltpu.VMEM((N_CHIPS, half_out, D_MODEL), BF16),
            pltpu.VMEM((N_CHIPS, half_in, SHARD_PAD), BF16), pltpu.VMEM((N_CHIPS, half_out, D_MODEL), BF16),
            pltpu.VMEM((PACK_ROWS, D_MODEL), F32), pltpu.VMEM((N_DEV, PACK_ROWS, D_MODEL), F32),
            pltpu.SemaphoreType.DMA((N_SEMS,)), pltpu.SemaphoreType.DMA((N_SEMS,)),
            pltpu.SemaphoreType.DMA((1 + N_CHIPS,)),
        ],
        compiler_params=_params(48),
    )(gw_in, gw_out, *small)


def _adam_math(w, g, m, v):
    mn = ADAM_B1 * m + (1.0 - ADAM_B1) * g
    vn = ADAM_B2 * v + (1.0 - ADAM_B2) * (g * g)
    m_hat = mn / (1.0 - ADAM_B1 ** ADAM_STEP)
    v_hat = vn / (1.0 - ADAM_B2 ** ADAM_STEP)
    return -ADAM_LR * (m_hat / (jnp.sqrt(v_hat) + ADAM_EPS) + ADAM_WD * w), mn, vn


def _adamw(name, w, g, m, v, tb):
    rows, cols = w.shape
    g_cols = g.shape[1]

    def body(w_ref, g_ref, m_ref, v_ref, go_ref, d_ref, mo_ref, vo_ref):
        gv = g_ref[...]
        if g_cols != cols:
            gv = pltpu.roll(gv, lax.axis_index("y") * (g_cols - SHARD_SHIFT), 1)[:, 0:cols]
        go_ref[...] = gv
        d_ref[...], mo_ref[...], vo_ref[...] = _adam_math(w_ref[...], gv, m_ref[...], v_ref[...])

    spec = pl.BlockSpec((tb, cols), lambda i: (i, 0))
    shape = jax.ShapeDtypeStruct((rows, cols), F32)
    return _call(
        body, name=name, grid=(rows // tb,), out_shape=(shape,) * 4,
        in_specs=[spec, pl.BlockSpec((tb, g_cols), lambda i: (i, 0)), spec, spec], out_specs=(spec,) * 4,
        compiler_params=_params(32, ("arbitrary",)),
    )(w, g, m, v)


def _adamw_small(sums, params):
    def body(sums_ref, *refs):
        ins, outs = refs[:3 * len(params)], refs[3 * len(params):]
        for k in range(len(params)):
            w_ref, m_ref, v_ref = ins[3 * k:3 * k + 3]
            g = sums_ref[8 * k:8 * k + 1, 0:w_ref.shape[1]]
            go_ref, d_ref, mo_ref, vo_ref = outs[4 * k:4 * k + 4]
            go_ref[...] = g
            d_ref[...], mo_ref[...], vo_ref[...] = _adam_math(w_ref[...], g, m_ref[...], v_ref[...])

    vmem = pl.BlockSpec(memory_space=pltpu.VMEM)
    flat = [a for p in params for a in p]
    shapes = tuple(jax.ShapeDtypeStruct(p[0].shape, F32) for p in params for _ in range(4))
    res = _call(body, name="adamw_small", out_shape=shapes, in_specs=[vmem] * (1 + len(flat)),
                out_specs=(vmem,) * len(shapes), compiler_params=_params(16))(sums, *flat)
    return [res[4 * k:4 * k + 4] for k in range(len(params))]


def kernel(x, norm_g, w_in, att_sinks, ret_gn_g, w_out, final_g, loss_target, m_norm_g, m_w_in, m_att_sinks, m_ret_gn_g, m_w_out, m_final_g, v_norm_g, v_w_in, v_att_sinks, v_ret_gn_g, v_w_out, v_final_g):
    seq = x.shape[1]
    xs, tgt = x[0], loss_target[0]
    final_g2 = final_g.reshape(1, D_MODEL)
    tables = _tables(seq)

    w_in_full, w_out_full = _weights_allgather(w_in[0], w_out[0])
    proj = _in_proj(xs, norm_g, w_in_full, min(512, seq))
    mix, dxo, states, loss_part, gfin, o_att = _mix_fwd(proj, xs, tgt, w_out_full, final_g2, ret_gn_g, att_sinks,
                                                        tables, min(256, seq))
    dproj, gw_out, dgain, dsink = _mix_bwd(proj, dxo, mix, o_att, states, w_out_full, ret_gn_g, att_sinks, tables,
                                           min(256, seq))
    grad_x, gw_in, gnorm = _in_proj_bwd(dproj, xs, dxo, norm_g, w_in_full, min(256, seq))

    g_in, g_out, sums = _grads_reduce(gw_in, gw_out, (gnorm, gfin, dgain, dsink, loss_part))

    res_in = _adamw("adamw_w_in", w_in[0], g_in, m_w_in[0], v_w_in[0], 256)
    res_out = _adamw("adamw_w_out", w_out[0], g_out, m_w_out[0], v_w_out[0], SHARD_OUT)
    as_row = lambda a: a.reshape(1, D_MODEL)
    r_norm, r_final, r_gain, r_sink = _adamw_small(sums, [
        (norm_g, m_norm_g, v_norm_g), (final_g2, as_row(m_final_g), as_row(v_final_g)),
        (ret_gn_g, m_ret_gn_g, v_ret_gn_g), (att_sinks, m_att_sinks, v_att_sinks)])

    outs = []
    for k in range(4):
        outs += [r_norm[k], res_in[k][None], r_sink[k], r_gain[k], res_out[k][None], r_final[k].reshape(D_MODEL)]
    return (sums[4 * 8, 0], grad_x[None], *outs)
```

```python
import jax
import jax.numpy as jnp
import numpy as np
from jax import lax
from jax.experimental import pallas as pl
from jax.experimental.pallas import tpu as pltpu

F32 = jnp.float32
BF16 = jnp.bfloat16

D_MODEL = 1024
ATT_HEADS = 8
ATT_KV_HEADS = 2
ATT_HEAD_DIM = 64
RET_HEADS = 4
RET_QK_DIM = 64
RET_V_DIM = 128
BLK = 128
ROT_BASE = 10000.0
RMS_EPS = 1e-6
GN_EPS = 1e-6
NEG_INF = -1e30
ATT_SCALE = ATT_HEAD_DIM ** -0.5
RET_SCALE = RET_QK_DIM ** -0.5

ATT_WIDTH = ATT_HEADS * ATT_HEAD_DIM
ATT_KV_WIDTH = ATT_KV_HEADS * ATT_HEAD_DIM
RET_QK_WIDTH = RET_HEADS * RET_QK_DIM
RET_WIDTH = RET_HEADS * RET_V_DIM
MIX_WIDTH = ATT_WIDTH + RET_WIDTH
OFF_AQ = 0
OFF_AK = OFF_AQ + ATT_WIDTH
OFF_AV = OFF_AK + ATT_KV_WIDTH
OFF_AZ = OFF_AV + ATT_KV_WIDTH
OFF_RQ = OFF_AZ + ATT_WIDTH
OFF_RK = OFF_RQ + RET_QK_WIDTH
OFF_RV = OFF_RK + RET_QK_WIDTH
OFF_RZ = OFF_RV + RET_WIDTH
IN_WIDTH = OFF_RZ + RET_WIDTH

LANE = 128
BF16_ROWS = 16
HALF_LANE = LANE // 2
PAIRS = RET_QK_WIDTH // LANE
assert ATT_HEAD_DIM == HALF_LANE and RET_QK_DIM == HALF_LANE and RET_V_DIM == LANE and ATT_KV_WIDTH == LANE

N_CHIPS = 4
N_DEV = 8
SHARD_IN = IN_WIDTH // N_CHIPS
SHARD_PAD = 768
SHARD_SHIFT = SHARD_PAD - SHARD_IN
WIN_START = tuple((j * SHARD_IN) // LANE * LANE for j in range(N_CHIPS))
SHARD_OUT = MIX_WIDTH // N_CHIPS

PACK_PARTS = 5
PACK_ROWS = 8 * PACK_PARTS

ADAM_LR = 0.001
ADAM_B1 = 0.9
ADAM_B2 = 0.999
ADAM_EPS = 1e-08
ADAM_WD = 0.01
ADAM_STEP = 10

VMEM_CAP = 64 * 1024 * 1024
MESH_ID = pl.DeviceIdType.MESH


def _call(body, **kw):
    return pl.pallas_call(body, **kw)


def _params(vmem_mb, semantics=None):
    assert vmem_mb * 1024 * 1024 < VMEM_CAP
    return pltpu.CompilerParams(dimension_semantics=semantics, vmem_limit_bytes=vmem_mb * 1024 * 1024)


def _dot(a, b):
    return jnp.dot(a, b, preferred_element_type=F32)


def _dot_nt(a, b):
    return lax.dot_general(a, b, (((1,), (1,)), ((), ())), preferred_element_type=F32)


def _dot_tn(a, b):
    return lax.dot_general(a, b, (((0,), (0,)), ((), ())), preferred_element_type=F32)


def _sigmoid(z):
    return 1.0 / (1.0 + jnp.exp(-z))


def _const_spec(shape):
    nd = len(shape)
    return pl.BlockSpec(shape, lambda i: (0,) * nd)


def _tables(seq):
    f32 = np.float32
    pos = np.arange(seq, dtype=f32)
    theta = (f32(1.0) / (f32(ROT_BASE) ** np.linspace(0.0, 1.0, RET_QK_DIM // 2, dtype=f32))).astype(f32)
    ang = (pos[:, None] * theta[None, :]).astype(f32)
    cos, sin = np.cos(ang), np.sin(ang)
    cos2 = np.repeat(cos, 2, axis=1)
    sin2 = np.stack([-sin, sin], axis=-1).reshape(seq, RET_QK_DIM)
    cos_t = np.tile(cos2, (1, 2))
    sin_t = np.tile(sin2, (1, 2))

    log_gamma = np.log(f32(1.0) - f32(2.0) ** (f32(-5.0) - np.arange(RET_HEADS, dtype=f32))).astype(f32)
    idx = np.arange(BLK, dtype=f32)
    rel = idx[:, None] - idx[None, :]
    decay_in = np.where(rel >= 0, np.exp(log_gamma[:, None, None] * np.maximum(rel, f32(0.0))), f32(0.0))
    k_dec = np.exp(log_gamma[:, None] * (BLK - 1 - idx)[None, :])
    q_dec = np.exp(log_gamma[:, None] * (idx + 1)[None, :])
    chunk_decay = np.exp(log_gamma * f32(BLK))
    kdec_t = np.repeat(k_dec.T, RET_QK_DIM, axis=1)
    qdec_t = np.repeat(q_dec.T, RET_QK_DIM, axis=1)
    cd_t = np.broadcast_to(chunk_decay[:, None, None], (RET_HEADS, RET_QK_DIM, RET_V_DIM))
    decay_in = decay_in.reshape(PAIRS, 2 * BLK, BLK)
    cd_t = cd_t.reshape(PAIRS, 2 * RET_QK_DIM, RET_V_DIM)

    key = np.arange(2 * BLK)[:, None]
    query = np.arange(2 * BLK)[None, :] % BLK
    diff = query + BLK - key
    valid = (diff >= 0) & (diff < BLK)
    bias = np.stack([np.where(valid & (key >= BLK), 0.0, NEG_INF), np.where(valid, 0.0, NEG_INF)])
    return tuple(jnp.asarray(np.ascontiguousarray(a), F32) for a in (cos_t, sin_t, decay_in, qdec_t, kdec_t, cd_t, bias))


def _low_lanes(shape):
    lane = lax.broadcasted_iota(jnp.int32, shape, len(shape) - 1)
    return (lane & HALF_LANE) == 0


def _split_heads(t):
    low = _low_lanes(t.shape)
    zero = jnp.zeros_like(t)
    return jnp.where(low, t, zero), jnp.where(low, zero, t)


def _swap_pairs(t):
    lane = lax.broadcasted_iota(jnp.int32, t.shape, 1)
    nxt = pltpu.roll(t, t.shape[1] - 1, 1)
    prv = pltpu.roll(t, 1, 1)
    return jnp.where((lane & 1) == 0, nxt, prv)


def _per_tile(fn, t):
    return jnp.concatenate([fn(_tile(t, i)) for i in range(t.shape[1] // LANE)], axis=1)


def _rotate(t, cos_t, sin_t):
    return _per_tile(lambda a: a * cos_t + _swap_pairs(a) * sin_t, t)


def _rotate_transposed(d, cos_t, sin_t):
    return _per_tile(lambda a: a * cos_t + _swap_pairs(a * sin_t), d)


def _kv_operands(cat):
    low = _low_lanes(cat.shape)
    swapped = pltpu.roll(cat, HALF_LANE, 1)
    zero = jnp.zeros_like(cat)
    pick = lambda a, b: jnp.where(low, a, b).astype(BF16)
    return ((pick(cat, zero), pick(zero, swapped)), (pick(swapped, zero), pick(zero, cat)))


def _stack_tiles(t, first_tile):
    a = t[:, first_tile * LANE:(first_tile + 1) * LANE]
    b = t[:, (first_tile + 1) * LANE:(first_tile + 2) * LANE]
    return jnp.concatenate([a, b], axis=0)


def _sink_rows(sinks_ref, group):
    first = lax.broadcasted_iota(jnp.int32, (1, 2 * BLK), 1) < BLK

    def row(h0, h1):
        return jnp.where(first, sinks_ref[0, group * 4 + h0], sinks_ref[0, group * 4 + h1])
    return row(0, 2), row(1, 3)


ATT_PROBLEMS = tuple((g, hi) for g in range(ATT_KV_HEADS) for hi in range(2))


def _attn_probs(qs, k_ops, bias, sinks_ref):
    sink = [_sink_rows(sinks_ref, g)[hi] for g, hi in ATT_PROBLEMS]
    s = [_dot_nt(k_ops[g][hi], qs[g]) + bias for g, hi in ATT_PROBLEMS]
    m = [jnp.maximum(jnp.max(si, axis=0, keepdims=True), ki) for si, ki in zip(s, sink)]
    e = [jnp.exp(si - mi) for si, mi in zip(s, m)]
    es = [jnp.exp(ki - mi) for ki, mi in zip(sink, m)]
    inv = [1.0 / (jnp.sum(ei, axis=0, keepdims=True) + esi) for ei, esi in zip(e, es)]
    return [ei * ii for ei, ii in zip(e, inv)], [esi * ii for esi, ii in zip(es, inv)]


def _group_norm_all(outs):
    mu = [jnp.mean(o, axis=1, keepdims=True) for o in outs]
    xc = [o - m for o, m in zip(outs, mu)]
    var = [jnp.mean(c * c, axis=1, keepdims=True) for c in xc]
    rstd = [lax.rsqrt(v + GN_EPS) for v in var]
    return [c * r for c, r in zip(xc, rstd)], rstd


def _retention_operands(p_ref, rows, cos_b, sin_b, qdec, kdec):
    qr = _rotate(p_ref[rows, OFF_RQ:OFF_RQ + RET_QK_WIDTH], cos_b, sin_b)
    kr = _rotate(p_ref[rows, OFF_RK:OFF_RK + RET_QK_WIDTH], cos_b, sin_b) * RET_SCALE
    return qr, kr, qr * qdec, kr * kdec


def _tile(t, i):
    return t[:, i * LANE:(i + 1) * LANE]


def _weights_allgather(w_in_t_shard, w_out_shard):
    half_in = SHARD_IN // 2
    half_out = SHARD_OUT // 2
    assert half_in % BF16_ROWS == 0 and half_out % BF16_ROWS == 0

    def body(wi_ref, wo_ref, win_ref, wout_ref, blk_ref, blko_ref, send_sems, recv_sems):
        x, y, c = lax.axis_index("x"), lax.axis_index("y"), lax.axis_index("c")
        me, sibling = (x, y, c), (x, y, 1 - c)
        chips = [(1 - x, y), (x, 1 - y), (1 - x, 1 - y)]

        def rows_in(px, py, pc):
            return win_ref.at[pl.ds(pl.multiple_of((4 * px + 2 * py + pc) * half_in, BF16_ROWS), half_in), :]

        def rows_out(px, py, pc):
            return wout_ref.at[pl.ds(pl.multiple_of((4 * px + 2 * py + pc) * half_out, half_out), half_out), :]

        def copy(k, rows, block, to, src=None):
            return pltpu.make_async_remote_copy(
                src_ref=rows(*block) if src is None else src, dst_ref=rows(*block),
                send_sem=send_sems.at[k], recv_sem=recv_sems.at[k], device_id=to, device_id_type=MESH_ID)

        blk_ref[...] = wi_ref[pl.ds(pl.multiple_of(c * half_in, BF16_ROWS), half_in), :].astype(BF16)
        blko_ref[...] = wo_ref[pl.ds(pl.multiple_of(c * half_out, half_out), half_out), :].astype(BF16)
        rows_in(*me)[...] = blk_ref[...]
        rows_out(*me)[...] = blko_ref[...]

        sets = ((0, rows_in, blk_ref), (7, rows_out, blko_ref))
        first, passed = [], []
        for base, rows, src in sets:
            first.append(copy(base, rows, me, sibling, src=src))
            first += [copy(base + 1 + j, rows, me, (*chip, c), src=src) for j, chip in enumerate(chips)]
        for cp in first:
            cp.start()
        for base, rows, src in sets:
            for j, chip in enumerate(chips):
                copy(base + 1 + j, rows, (*chip, c), me).wait_recv()
                fwd = copy(base + 4 + j, rows, (*chip, c), sibling)
                fwd.start()
                passed.append(fwd)
        for base, rows, src in sets:
            copy(base, rows, sibling, me).wait_recv()
            for j, chip in enumerate(chips):
                copy(base + 4 + j, rows, (*chip, 1 - c), me).wait_recv()
        for cp in first + passed:
            cp.wait_send()

    vmem = pl.BlockSpec(memory_space=pltpu.VMEM)
    return _call(
        body, name="weights_allgather",
        out_shape=(jax.ShapeDtypeStruct((IN_WIDTH, D_MODEL), BF16), jax.ShapeDtypeStruct((MIX_WIDTH, D_MODEL), BF16)),
        in_specs=[vmem, vmem], out_specs=(vmem, vmem),
        scratch_shapes=[
            pltpu.VMEM((half_in, D_MODEL), BF16),
            pltpu.VMEM((half_out, D_MODEL), BF16),
            pltpu.SemaphoreType.DMA((14,)),
            pltpu.SemaphoreType.DMA((14,)),
        ],
        compiler_params=_params(32),
    )(w_in_t_shard, w_out_shard)


def _in_proj(x, norm_g, w_in, tb):
    seq = x.shape[0]

    def body(x_ref, g_ref, w_ref, p_ref):
        xv = x_ref[...]
        r = lax.rsqrt(jnp.mean(xv * xv, axis=1, keepdims=True) + RMS_EPS)
        h = (xv * r) * g_ref[...]
        p_ref[...] = _dot_nt(h.astype(BF16), w_ref[...])

    return _call(
        body, name="in_proj", grid=(seq // tb,),
        out_shape=jax.ShapeDtypeStruct((seq, IN_WIDTH), F32),
        in_specs=[pl.BlockSpec((tb, D_MODEL), lambda i: (i, 0)), _const_spec((1, D_MODEL)),
                  _const_spec((IN_WIDTH, D_MODEL))],
        out_specs=pl.BlockSpec((tb, IN_WIDTH), lambda i: (i, 0)),
        compiler_params=_params(48, ("arbitrary",)),
    )(x, norm_g, w_in)


def _mix_fwd(proj, x, target, w_out, final_g, gn_gain, sinks, tables, tb):
    seq = x.shape[0]
    nsub = tb // BLK
    cos_t, sin_t, decay_in, qdec_t, kdec_t, cd_t, bias_t = tables

    def body(p_ref, x_ref, t_ref, cos_ref, sin_ref, wout_ref, fg_ref, gain_ref, sinks_ref, din_ref, qdec_ref,
             kdec_ref, cd_ref, bias_ref, mix_ref, dxo_ref, st_ref, loss_ref, gfin_ref, oatt_ref,
             kprev_ref, vprev_ref, state_ref):
        i = pl.program_id(0)

        @pl.when(i == 0)
        def _():
            kprev_ref[...] = jnp.zeros_like(kprev_ref)
            vprev_ref[...] = jnp.zeros_like(vprev_ref)
            state_ref[...] = jnp.zeros_like(state_ref)
            loss_ref[...] = jnp.zeros_like(loss_ref)
            gfin_ref[...] = jnp.zeros_like(gfin_ref)

        def sub(j, carry):
            kp, vp, states = carry
            rows = pl.ds(pl.multiple_of(j * BLK, BLK), BLK)
            bias = bias_ref[jnp.where(jnp.logical_or(i > 0, j > 0), 1, 0)]

            aq = p_ref[rows, OFF_AQ:OFF_AQ + ATT_WIDTH]
            ak = p_ref[rows, OFF_AK:OFF_AK + ATT_KV_WIDTH]
            av = p_ref[rows, OFF_AV:OFF_AV + ATT_KV_WIDTH]
            az = p_ref[rows, OFF_AZ:OFF_AZ + ATT_WIDTH]
            k_ops = _kv_operands(jnp.concatenate([kp, ak], axis=0))
            v_ops = _kv_operands(jnp.concatenate([vp, av], axis=0))
            qs = [(_stack_tiles(aq, 2 * g) * ATT_SCALE).astype(BF16) for g in range(ATT_KV_HEADS)]
            p, _ = _attn_probs(qs, k_ops, bias, sinks_ref)
            o_tiles = []
            for g in range(ATT_KV_HEADS):
                p_cat = jnp.concatenate([p[2 * g].astype(BF16), p[2 * g + 1].astype(BF16)], axis=0)
                o = _dot_tn(p_cat, jnp.concatenate(v_ops[g], axis=0))
                o_tiles += [o[0:BLK], o[BLK:]]
            o_att = jnp.concatenate(o_tiles, axis=1)
            oatt_ref[rows, :] = o_att
            out = [o_att * (az * _sigmoid(az))]

            qr, kr, qd, kd = _retention_operands(p_ref, rows, cos_ref[rows, :], sin_ref[rows, :],
                                                 qdec_ref[...], kdec_ref[...])
            heads = [(t, hh) for t in range(PAIRS) for hh in range(2)]
            sc = [_dot_nt(jnp.concatenate(_split_heads(_tile(qr, t)), axis=0).astype(BF16), _tile(kr, t).astype(BF16))
                  * din_ref[t] for t in range(PAIRS)]
            qd_heads = [_split_heads(_tile(qd, t)) for t in range(PAIRS)]
            state_b = [states[t].astype(BF16) for t in range(PAIRS)]
            vs = [p_ref[rows, OFF_RV + h * RET_V_DIM:OFF_RV + (h + 1) * RET_V_DIM].astype(BF16) for h in range(RET_HEADS)]
            rzs = [p_ref[rows, OFF_RZ + h * RET_V_DIM:OFF_RZ + (h + 1) * RET_V_DIM] for h in range(RET_HEADS)]
            lhs = [jnp.concatenate([sc[t][hh * BLK:(hh + 1) * BLK].astype(BF16), qd_heads[t][hh].astype(BF16)], axis=1)
                   for t, hh in heads]
            ons, _ = _group_norm_all([_dot(lhs[2 * t + hh], jnp.concatenate([vs[2 * t + hh], state_b[t]], axis=0))
                                      for t, hh in heads])
            out += [(ons[h] * gain_ref[:, h * RET_V_DIM:(h + 1) * RET_V_DIM]) * (rzs[h] * _sigmoid(rzs[h]))
                    for h in range(RET_HEADS)]
            new_states = [states[t] * cd_ref[t]
                          + _dot_tn(jnp.concatenate(_split_heads(_tile(kd, t)), axis=0).astype(BF16),
                                    jnp.concatenate([vs[2 * t], vs[2 * t + 1]], axis=0)) for t in range(PAIRS)]
            mix_ref[rows, :] = jnp.concatenate(out, axis=1).astype(BF16)
            st_ref[j] = jnp.stack(states)
            return ak, av, tuple(new_states)

        carry = (kprev_ref[...], vprev_ref[...], tuple(state_ref[t] for t in range(PAIRS)))
        kp, vp, states = lax.fori_loop(0, nsub, sub, carry)
        kprev_ref[...] = kp
        vprev_ref[...] = vp
        state_ref[...] = jnp.stack(states)

        xo = x_ref[...] + _dot(mix_ref[...], wout_ref[...])
        r2 = lax.rsqrt(jnp.mean(xo * xo, axis=1, keepdims=True) + RMS_EPS)
        xn = xo * r2
        err = xn * fg_ref[...] - t_ref[...]
        loss_ref[...] += jnp.sum(err * err) * (0.5 / D_MODEL)
        dy = err * (1.0 / D_MODEL)
        gfin_ref[...] += jnp.sum(dy * xn, axis=0, keepdims=True)
        u = dy * fg_ref[...]
        dxo_ref[...] = r2 * u - xn * (r2 * jnp.mean(u * xn, axis=1, keepdims=True))

    blk_rows = lambda w: pl.BlockSpec((tb, w), lambda i: (i, 0))
    state_shape = (PAIRS, 2 * RET_QK_DIM, RET_V_DIM)
    return _call(
        body, name="mix_fwd", grid=(seq // tb,),
        out_shape=(
            jax.ShapeDtypeStruct((seq, MIX_WIDTH), BF16),
            jax.ShapeDtypeStruct((seq, D_MODEL), F32),
            jax.ShapeDtypeStruct((seq // BLK,) + state_shape, F32),
            jax.ShapeDtypeStruct((8, LANE), F32),
            jax.ShapeDtypeStruct((1, D_MODEL), F32),
            jax.ShapeDtypeStruct((seq, ATT_WIDTH), F32),
        ),
        in_specs=[
            blk_rows(IN_WIDTH), blk_rows(D_MODEL), blk_rows(D_MODEL), blk_rows(LANE), blk_rows(LANE),
            _const_spec((MIX_WIDTH, D_MODEL)), _const_spec((1, D_MODEL)), _const_spec((1, RET_WIDTH)),
            pl.BlockSpec(memory_space=pltpu.SMEM),
            _const_spec((PAIRS, 2 * BLK, BLK)), _const_spec((BLK, RET_QK_WIDTH)), _const_spec((BLK, RET_QK_WIDTH)),
            _const_spec(state_shape), _const_spec((2, 2 * BLK, 2 * BLK)),
        ],
        out_specs=(
            blk_rows(MIX_WIDTH), blk_rows(D_MODEL),
            pl.BlockSpec((nsub,) + state_shape, lambda i: (i, 0, 0, 0)),
            _const_spec((8, LANE)), _const_spec((1, D_MODEL)), blk_rows(ATT_WIDTH),
        ),
        scratch_shapes=[
            pltpu.VMEM((BLK, ATT_KV_WIDTH), F32), pltpu.VMEM((BLK, ATT_KV_WIDTH), F32),
            pltpu.VMEM(state_shape, F32),
        ],
        compiler_params=_params(48, ("arbitrary",)),
    )(proj, x, target, cos_t, sin_t, w_out, final_g, gn_gain, sinks, decay_in, qdec_t, kdec_t, cd_t, bias_t)


def _mix_bwd(proj, dxo, mix, o_att, states, w_out, gn_gain, sinks, tables, tb):
    seq = dxo.shape[0]
    nsub = tb // BLK
    nblk = seq // tb
    cos_t, sin_t, decay_in, qdec_t, kdec_t, cd_t, bias_t = tables
    kv_cols = OFF_AK // (2 * ATT_KV_WIDTH)
    state_shape = (PAIRS, 2 * RET_QK_DIM, RET_V_DIM)

    def body(p_ref, pkv_ref, dxo_ref, mix_ref, oatt_ref, st_ref, cos_ref, sin_ref, wout_ref, gain_ref, sinks_ref, din_ref,
             qdec_ref, kdec_ref, cd_ref, bias_ref, dp_ref, gwout_ref, dgain_ref, dsink_ref,
             dmix_ref, kv_ref, dkc_ref, dvc_ref, gst_ref):
        i = pl.program_id(0)
        blk = nblk - 1 - i

        @pl.when(i == 0)
        def _():
            gwout_ref[...] = jnp.zeros_like(gwout_ref)
            dgain_ref[...] = jnp.zeros_like(dgain_ref)
            dsink_ref[...] = jnp.zeros_like(dsink_ref)
            dkc_ref[...] = jnp.zeros_like(dkc_ref)
            dvc_ref[...] = jnp.zeros_like(dvc_ref)
            gst_ref[...] = jnp.zeros_like(gst_ref)

        dxo_b = dxo_ref[...].astype(BF16)
        dmix_ref[...] = _dot_nt(dxo_b, wout_ref[...])
        gwout_ref[...] += _dot_tn(mix_ref[...], dxo_b)
        kv_ref[0:BLK, :] = pkv_ref[...]
        kv_ref[BLK:, :] = p_ref[:, OFF_AK:OFF_AK + 2 * ATT_KV_WIDTH]
        low = _low_lanes((BLK, LANE))
        low2 = _low_lanes((2 * BLK, LANE))
        lane = lax.broadcasted_iota(jnp.int32, (1, LANE), 1)

        def sub(jj, carry):
            dkc, dvc, gsts, dgain, dsink = carry
            j = nsub - 1 - jj
            rows = pl.ds(pl.multiple_of(j * BLK, BLK), BLK)
            both = pl.ds(pl.multiple_of(j * BLK, BLK), 2 * BLK)
            bias = bias_ref[jnp.where(jnp.logical_or(blk > 0, j > 0), 1, 0)]

            aq = p_ref[rows, OFF_AQ:OFF_AQ + ATT_WIDTH]
            az = p_ref[rows, OFF_AZ:OFF_AZ + ATT_WIDTH]
            k_ops = _kv_operands(kv_ref[both, 0:ATT_KV_WIDTH])
            v_ops = _kv_operands(kv_ref[both, ATT_KV_WIDTH:2 * ATT_KV_WIDTH])
            da = dmix_ref[rows, 0:ATT_WIDTH]
            sig = _sigmoid(az)
            d_o = da * (az * sig)
            qs = [(_stack_tiles(aq, 2 * g) * ATT_SCALE).astype(BF16) for g in range(ATT_KV_HEADS)]
            dos = [_stack_tiles(d_o, 2 * g).astype(BF16) for g in range(ATT_KV_HEADS)]
            p, p_sink = _attn_probs(qs, k_ops, bias, sinks_ref)
            dpr = [_dot_nt(v_ops[g][hi], dos[g]) for g, hi in ATT_PROBLEMS]
            delta = [jnp.sum(pi * di, axis=0, keepdims=True) for pi, di in zip(p, dpr)]
            ds = [(pi * (di - ti)).astype(BF16) for pi, di, ti in zip(p, dpr, delta)]
            for (g, hi), ki, ti in zip(ATT_PROBLEMS, p_sink, delta):
                sink_part = ki * ti
                for half in range(2):
                    tot = jnp.sum(sink_part[:, half * BLK:(half + 1) * BLK], axis=1, keepdims=True)
                    dsink = dsink - jnp.where(lane == 4 * g + 2 * half + hi, tot, 0.0)
            dq_tiles, dk_sums, dv_sums = [], [], []
            for g in range(ATT_KV_HEADS):
                ds_cat = jnp.concatenate([ds[2 * g], ds[2 * g + 1]], axis=0)
                p_cat = jnp.concatenate([p[2 * g].astype(BF16), p[2 * g + 1].astype(BF16)], axis=0)
                dqs = _dot_tn(ds_cat, jnp.concatenate(k_ops[g], axis=0)) * ATT_SCALE
                dq_tiles += [dqs[0:BLK], dqs[BLK:]]
                dk_sums.append(_dot(ds_cat, qs[g]))
                dv_sums.append(_dot(p_cat, dos[g]))
            daz = da * oatt_ref[rows, :] * (sig * (1.0 + az * (1.0 - sig)))

            def kv_grad(sums):
                (a0, b0), (a1, b1) = [(s[0:2 * BLK], s[2 * BLK:]) for s in sums]
                return jnp.where(low2, a0, b1) + pltpu.roll(jnp.where(low2, a1, b0), HALF_LANE, 1)

            dk_both, dv_both = kv_grad(dk_sums), kv_grad(dv_sums)
            dak = dk_both[BLK:] + dkc
            dav = dv_both[BLK:] + dvc

            cos_b, sin_b = cos_ref[rows, :], sin_ref[rows, :]
            qdec, kdec = qdec_ref[...], kdec_ref[...]
            qr, kr, qd, kd = _retention_operands(p_ref, rows, cos_b, sin_b, qdec, kdec)
            heads = [(t, hh) for t in range(PAIRS) for hh in range(2)]
            head_cols = [slice(h * RET_V_DIM, (h + 1) * RET_V_DIM) for h in range(RET_HEADS)]
            q_rows = [jnp.concatenate(_split_heads(_tile(qr, t)), axis=0).astype(BF16) for t in range(PAIRS)]
            k_rows = [jnp.concatenate(_split_heads(_tile(kr, t)), axis=0).astype(BF16) for t in range(PAIRS)]
            din = [din_ref[t] for t in range(PAIRS)]
            sc = [(_dot_nt(q_rows[t], _tile(kr, t).astype(BF16)) * din[t]).astype(BF16) for t in range(PAIRS)]
            qd_heads = [_split_heads(_tile(qd, t)) for t in range(PAIRS)]
            kd_heads = [_split_heads(_tile(kd, t)) for t in range(PAIRS)]
            state_b = [st_ref[j, t].astype(BF16) for t in range(PAIRS)]
            gst_b = [gsts[t].astype(BF16) for t in range(PAIRS)]
            vs = [p_ref[rows, OFF_RV + h * RET_V_DIM:OFF_RV + (h + 1) * RET_V_DIM].astype(BF16) for h in range(RET_HEADS)]
            rzs = [p_ref[rows, OFF_RZ + h * RET_V_DIM:OFF_RZ + (h + 1) * RET_V_DIM] for h in range(RET_HEADS)]
            drs = [dmix_ref[rows, ATT_WIDTH + h * RET_V_DIM:ATT_WIDTH + (h + 1) * RET_V_DIM] for h in range(RET_HEADS)]
            gains = [gain_ref[:, c] for c in head_cols]
            lhs = [jnp.concatenate([sc[t][hh * BLK:(hh + 1) * BLK], qd_heads[t][hh].astype(BF16)], axis=1) for t, hh in heads]
            rhs = [jnp.concatenate([vs[2 * t + hh], state_b[t]], axis=0) for t, hh in heads]
            ons, rstds = _group_norm_all([_dot(l, r) for l, r in zip(lhs, rhs)])
            sig_r = [_sigmoid(z) for z in rzs]
            dgn = [d * (z * g) for d, z, g in zip(drs, rzs, sig_r)]
            dz_parts = [d * (o * gn) * (g * (1.0 + z * (1.0 - g))) for d, o, gn, g, z in zip(drs, ons, gains, sig_r, rzs)]
            dgain_parts = [jnp.sum(d * o, axis=0, keepdims=True) for d, o in zip(dgn, ons)]
            don = [d * gn for d, gn in zip(dgn, gains)]
            mean_don = [jnp.mean(d, axis=1, keepdims=True) for d in don]
            mean_don_on = [jnp.mean(d * o, axis=1, keepdims=True) for d, o in zip(don, ons)]
            dob = [(r * (d - a - o * b)).astype(BF16) for r, d, a, o, b in zip(rstds, don, mean_don, ons, mean_don_on)]
            dlhs = [_dot_nt(d, r) for d, r in zip(dob, rhs)]
            drhs = [_dot_tn(l, d) for l, d in zip(lhs, dob)]
            dkds = [_dot_nt(vs[2 * t + hh], gst_b[t]) for t, hh in heads]
            dv_parts = [drhs[2 * t + hh][0:BLK] + _dot(kd_heads[t][hh].astype(BF16), gst_b[t]) for t, hh in heads]
            das = [(dlhs[2 * t + hh][:, 0:BLK] * din[t][hh * BLK:(hh + 1) * BLK]).astype(BF16) for t, hh in heads]
            new_gsts = [gsts[t] * cd_ref[t] + drhs[2 * t][BLK:] + drhs[2 * t + 1][BLK:] for t in range(PAIRS)]
            dq_parts = [_dot(jnp.concatenate([das[2 * t], das[2 * t + 1]], axis=1), k_rows[t])
                        + jnp.where(low, dlhs[2 * t][:, BLK:], dlhs[2 * t + 1][:, BLK:]) * _tile(qdec, t)
                        for t in range(PAIRS)]
            dk_parts = [_dot_tn(jnp.concatenate([das[2 * t], das[2 * t + 1]], axis=0), q_rows[t])
                        + jnp.where(low, dkds[2 * t], dkds[2 * t + 1]) * _tile(kdec, t) for t in range(PAIRS)]
            drq = _rotate_transposed(jnp.concatenate(dq_parts, axis=1), cos_b, sin_b)
            drk = _rotate_transposed(jnp.concatenate(dk_parts, axis=1) * RET_SCALE, cos_b, sin_b)

            dp_ref[rows, :] = jnp.concatenate(
                [jnp.concatenate(dq_tiles, axis=1), dak, dav, daz, drq, drk] + dv_parts + dz_parts, axis=1).astype(BF16)
            dgain = dgain + jnp.concatenate(dgain_parts, axis=1)
            return dk_both[0:BLK], dv_both[0:BLK], tuple(new_gsts), dgain, dsink

        carry = (dkc_ref[...], dvc_ref[...], tuple(gst_ref[t] for t in range(PAIRS)), dgain_ref[...], dsink_ref[...])
        dkc, dvc, gsts, dgain, dsink = lax.fori_loop(0, nsub, sub, carry)
        dkc_ref[...] = dkc
        dvc_ref[...] = dvc
        gst_ref[...] = jnp.stack(gsts)
        dgain_ref[...] = dgain
        dsink_ref[...] = dsink

    rev_rows = lambda w: pl.BlockSpec((tb, w), lambda i: (nblk - 1 - i, 0))
    prev_kv = pl.BlockSpec((BLK, 2 * ATT_KV_WIDTH), lambda i: (jnp.maximum((nblk - 1 - i) * nsub - 1, 0), kv_cols))
    return _call(
        body, name="mix_bwd", grid=(nblk,),
        out_shape=(
            jax.ShapeDtypeStruct((seq, IN_WIDTH), BF16),
            jax.ShapeDtypeStruct((MIX_WIDTH, D_MODEL), F32),
            jax.ShapeDtypeStruct((1, RET_WIDTH), F32),
            jax.ShapeDtypeStruct((1, LANE), F32),
        ),
        in_specs=[
            rev_rows(IN_WIDTH), prev_kv, rev_rows(D_MODEL), rev_rows(MIX_WIDTH), rev_rows(ATT_WIDTH),
            pl.BlockSpec((nsub,) + state_shape, lambda i: (nblk - 1 - i, 0, 0, 0)),
            rev_rows(LANE), rev_rows(LANE),
            _const_spec((MIX_WIDTH, D_MODEL)), _const_spec((1, RET_WIDTH)),
            pl.BlockSpec(memory_space=pltpu.SMEM),
            _const_spec((PAIRS, 2 * BLK, BLK)), _const_spec((BLK, RET_QK_WIDTH)), _const_spec((BLK, RET_QK_WIDTH)),
            _const_spec(state_shape), _const_spec((2, 2 * BLK, 2 * BLK)),
        ],
        out_specs=(
            rev_rows(IN_WIDTH), _const_spec((MIX_WIDTH, D_MODEL)), _const_spec((1, RET_WIDTH)), _const_spec((1, LANE)),
        ),
        scratch_shapes=[
            pltpu.VMEM((tb, MIX_WIDTH), F32),
            pltpu.VMEM((tb + BLK, 2 * ATT_KV_WIDTH), F32),
            pltpu.VMEM((BLK, ATT_KV_WIDTH), F32), pltpu.VMEM((BLK, ATT_KV_WIDTH), F32),
            pltpu.VMEM(state_shape, F32),
        ],
        compiler_params=_params(56, ("arbitrary",)),
    )(proj, proj, dxo, mix, o_att, states, cos_t, sin_t, w_out, gn_gain, sinks, decay_in, qdec_t, kdec_t, cd_t, bias_t)


def _in_proj_bwd(dproj, x, dxo, norm_g, w_in, tb):
    seq = x.shape[0]

    def body(dp_ref, x_ref, dxo_ref, g_ref, w_ref, gx_ref, gw_ref, gnorm_ref):
        i = pl.program_id(0)

        @pl.when(i == 0)
        def _():
            gw_ref[...] = jnp.zeros_like(gw_ref)
            gnorm_ref[...] = jnp.zeros_like(gnorm_ref)

        dp = dp_ref[...]
        xv = x_ref[...]
        r = lax.rsqrt(jnp.mean(xv * xv, axis=1, keepdims=True) + RMS_EPS)
        xn = xv * r
        h = xn * g_ref[...]
        gw_ref[...] += _dot_tn(h.astype(BF16), dp)
        dh = _dot(dp, w_ref[...])
        gnorm_ref[...] += jnp.sum(dh * xn, axis=0, keepdims=True)
        u = dh * g_ref[...]
        gx_ref[...] = dxo_ref[...] + r * u - xn * (r * jnp.mean(u * xn, axis=1, keepdims=True))

    rows = lambda w: pl.BlockSpec((tb, w), lambda i: (i, 0))
    return _call(
        body, name="in_proj_bwd", grid=(seq // tb,),
        out_shape=(
            jax.ShapeDtypeStruct((seq, D_MODEL), F32),
            jax.ShapeDtypeStruct((D_MODEL, IN_WIDTH), F32),
            jax.ShapeDtypeStruct((1, D_MODEL), F32),
        ),
        in_specs=[rows(IN_WIDTH), rows(D_MODEL), rows(D_MODEL), _const_spec((1, D_MODEL)),
                  _const_spec((IN_WIDTH, D_MODEL))],
        out_specs=(rows(D_MODEL), _const_spec((D_MODEL, IN_WIDTH)), _const_spec((1, D_MODEL))),
        compiler_params=_params(60, ("arbitrary",)),
    )(dproj, x, dxo, norm_g, w_in)


def _grads_reduce(gw_in, gw_out, small):
    half_in = D_MODEL // 2
    half_out = SHARD_OUT // 2
    A_IN, A_OUT, B_IN, B_OUT, C_IN, C_OUT, PACK, N_SEMS = 0, 1, 5, 9, 13, 14, 15, 23
    assert len(small) == PACK_PARTS

    def body(gwi_hbm, gwo_hbm, s0_ref, s1_ref, s2_ref, s3_ref, s4_ref, fin_ref, fout_ref, packsum_ref,
             mine_in, sib_in, mine_out, sib_out, send_in, send_out, b_in, b_out, pack_ref, packs,
             send_sems, recv_sems, local_sems):
        x, y, c = lax.axis_index("x"), lax.axis_index("y"), lax.axis_index("c")
        chip = 2 * x + y
        dev = 2 * chip + c
        sibling = (x, y, 1 - c)
        pack_ref[...] = jnp.zeros_like(pack_ref)
        for k, s_ref in enumerate((s0_ref, s1_ref, s2_ref, s3_ref, s4_ref)):
            pack_ref[8 * k:8 * k + 1, 0:s_ref.shape[1]] = s_ref[0:1, :]

        def remote(src, dst, send_k, recv_k, to):
            return pltpu.make_async_remote_copy(src_ref=src, dst_ref=dst, send_sem=send_sems.at[send_k],
                                                recv_sem=recv_sems.at[recv_k], device_id=to, device_id_type=MESH_ID)

        def in_rows(core):
            return pl.ds(pl.multiple_of(core * half_in, half_in), half_in)

        def out_rows(j, core):
            return pl.ds(pl.multiple_of(j * SHARD_OUT + core * half_out, half_out), half_out)

        packs[dev] = pack_ref[...]
        for d in range(N_DEV):
            to = (d // 4, (d // 2) % 2, d % 2)

            @pl.when(d != dev)
            def _():
                remote(pack_ref, packs.at[dev], PACK + d, PACK + dev, to).start()

        local = [pltpu.make_async_copy(gwi_hbm.at[in_rows(c), :], mine_in, local_sems.at[0])]
        local += [pltpu.make_async_copy(gwo_hbm.at[out_rows(j, c), :], mine_out.at[j], local_sems.at[1 + j])
                  for j in range(N_CHIPS)]
        for cp in local:
            cp.start()
        stage_a = [remote(gwi_hbm.at[in_rows(1 - c), :], sib_in, A_IN, A_IN, sibling)]
        stage_a += [remote(gwo_hbm.at[out_rows(j, 1 - c), :], sib_out.at[j], A_OUT + j, A_OUT + j, sibling)
                    for j in range(N_CHIPS)]
        for cp in stage_a:
            cp.start()
        for cp in local:
            cp.wait()
        for cp in stage_a:
            cp.wait_recv()

        def add_rows(k, carry):
            rows = pl.ds(pl.multiple_of(k * 64, 64), 64)
            mine_in[rows, :] = mine_in[rows, :] + sib_in[rows, :]
            return carry

        lax.fori_loop(0, half_in // 64, add_rows, 0)
        for j in range(N_CHIPS):
            mine_out[j] = mine_out[j] + sib_out[j]

        my_out_rows = pl.ds(pl.multiple_of(c * half_out, half_out), half_out)
        for j in range(N_CHIPS):
            to = (j // 2, j % 2, c)

            @pl.when(j != chip)
            def _():
                send_in[j] = mine_in[:, WIN_START[j]:WIN_START[j] + SHARD_PAD].astype(BF16)
                send_out[j] = mine_out[j].astype(BF16)
                remote(send_in.at[j], b_in.at[chip], B_IN + j, B_IN + chip, to).start()
                remote(send_out.at[j], b_out.at[chip], B_OUT + j, B_OUT + chip, to).start()

            @pl.when(j == chip)
            def _():
                fin_ref[in_rows(c), :] = mine_in[:, WIN_START[j]:WIN_START[j] + SHARD_PAD]
                fout_ref[my_out_rows, :] = mine_out[j]

        for j in range(N_CHIPS):
            @pl.when(j != chip)
            def _():
                remote(b_in.at[j], b_in.at[j], B_IN + j, B_IN + j, sibling).wait_recv()
                remote(b_out.at[j], b_out.at[j], B_OUT + j, B_OUT + j, sibling).wait_recv()
                fin_ref[in_rows(c), :] += b_in[j].astype(F32)
                fout_ref[my_out_rows, :] += b_out[j].astype(F32)

        mine_half = fin_ref.at[in_rows(c), :]
        mine_half_out = fout_ref.at[my_out_rows, :]
        stage_c = [remote(mine_half, mine_half, C_IN, C_IN, sibling),
                   remote(mine_half_out, mine_half_out, C_OUT, C_OUT, sibling)]
        for cp in stage_c:
            cp.start()
        other_half = fin_ref.at[in_rows(1 - c), :]
        other_half_out = fout_ref.at[pl.ds(pl.multiple_of((1 - c) * half_out, half_out), half_out), :]
        remote(other_half, other_half, C_IN, C_IN, sibling).wait_recv()
        remote(other_half_out, other_half_out, C_OUT, C_OUT, sibling).wait_recv()

        for d in range(N_DEV):
            @pl.when(d != dev)
            def _():
                remote(pack_ref, packs.at[d], PACK + d, PACK + d, sibling).wait_recv()
        total = packs[0]
        for d in range(1, N_DEV):
            total = total + packs[d]
        packsum_ref[...] = total

        for cp in stage_a + stage_c:
            cp.wait_send()
        for j in range(N_CHIPS):
            @pl.when(j != chip)
            def _():
                remote(b_in.at[j], b_in.at[j], B_IN + j, B_IN + j, sibling).wait_send()
                remote(b_out.at[j], b_out.at[j], B_OUT + j, B_OUT + j, sibling).wait_send()
        for d in range(N_DEV):
            @pl.when(d != dev)
            def _():
                remote(pack_ref, packs.at[d], PACK + d, PACK + d, sibling).wait_send()

    vmem = pl.BlockSpec(memory_space=pltpu.VMEM)
    hbm = pl.BlockSpec(memory_space=pl.ANY)
    return _call(
        body, name="grads_reduce",
        out_shape=(jax.ShapeDtypeStruct((D_MODEL, SHARD_PAD), F32), jax.ShapeDtypeStruct((SHARD_OUT, D_MODEL), F32),
                   jax.ShapeDtypeStruct((PACK_ROWS, D_MODEL), F32)),
        in_specs=[hbm, hbm] + [vmem] * PACK_PARTS, out_specs=(vmem, vmem, vmem),
        scratch_shapes=[
            pltpu.VMEM((half_in, IN_WIDTH), F32), pltpu.VMEM((half_in, IN_WIDTH), F32),
            pltpu.VMEM((N_CHIPS, half_out, D_MODEL), F32), pltpu.VMEM((N_CHIPS, half_out, D_MODEL), F32),
            pltpu.VMEM((N_CHIPS, half_in, SHARD_PAD), BF16), pltpu.VMEM((N_CHIPS, half_out, D_MODEL), BF16),
            pltpu.VMEM((N_CHIPS, half_in, SHARD_PAD), BF16), pltpu.VMEM((N_CHIPS, half_out, D_MODEL), BF16),
            pltpu.VMEM((PACK_ROWS, D_MODEL), F32), pltpu.VMEM((N_DEV, PACK_ROWS, D_MODEL), F32),
            pltpu.SemaphoreType.DMA((N_SEMS,)), pltpu.SemaphoreType.DMA((N_SEMS,)),
            pltpu.SemaphoreType.DMA((1 + N_CHIPS,)),
        ],
        compiler_params=_params(48),
    )(gw_in, gw_out, *small)


def _adam_math(w, g, m, v):
    mn = ADAM_B1 * m + (1.0 - ADAM_B1) * g
    vn = ADAM_B2 * v + (1.0 - ADAM_B2) * (g * g)
    m_hat = mn / (1.0 - ADAM_B1 ** ADAM_STEP)
    v_hat = vn / (1.0 - ADAM_B2 ** ADAM_STEP)
    return -ADAM_LR * (m_hat / (jnp.sqrt(v_hat) + ADAM_EPS) + ADAM_WD * w), mn, vn


def _adamw(name, w, g, m, v, tb):
    rows, cols = w.shape

    def body(w_ref, g_ref, m_ref, v_ref, go_ref, d_ref, mo_ref, vo_ref):
        gv = g_ref[...]
        go_ref[...] = gv
        d_ref[...], mo_ref[...], vo_ref[...] = _adam_math(w_ref[...], gv, m_ref[...], v_ref[...])

    spec = pl.BlockSpec((tb, cols), lambda i: (i, 0))
    shape = jax.ShapeDtypeStruct((rows, cols), F32)
    return _call(
        body, name=name, grid=(rows // tb,), out_shape=(shape,) * 4,
        in_specs=[spec] * 4, out_specs=(spec,) * 4,
        compiler_params=_params(32, ("arbitrary",)),
    )(w, g, m, v)


def _adamw_w_in(w_t, g_window, m_t, v_t, tb):
    def body(w_ref, g_ref, m_ref, v_ref, go_ref, d_ref, mo_ref, vo_ref, gt_ref):
        gt_ref[...] = g_ref[...].T
        gv = gt_ref[pl.ds(pl.multiple_of(lax.axis_index("y") * SHARD_SHIFT, SHARD_SHIFT), SHARD_IN), :]
        go_ref[...] = gv
        d_ref[...], mo_ref[...], vo_ref[...] = _adam_math(w_ref[...], gv, m_ref[...], v_ref[...])

    spec = pl.BlockSpec((SHARD_IN, tb), lambda i: (0, i))
    shape = jax.ShapeDtypeStruct((SHARD_IN, D_MODEL), F32)
    return _call(
        body, name="adamw_w_in", grid=(D_MODEL // tb,), out_shape=(shape,) * 4,
        in_specs=[spec, pl.BlockSpec((tb, SHARD_PAD), lambda i: (i, 0)), spec, spec], out_specs=(spec,) * 4,
        scratch_shapes=[pltpu.VMEM((SHARD_PAD, tb), F32)],
        compiler_params=_params(32, ("arbitrary",)),
    )(w_t, g_window, m_t, v_t)


def _adamw_small(sums, params):
    def body(sums_ref, *refs):
        ins, outs = refs[:3 * len(params)], refs[3 * len(params):]
        for k in range(len(params)):
            w_ref, m_ref, v_ref = ins[3 * k:3 * k + 3]
            g = sums_ref[8 * k:8 * k + 1, 0:w_ref.shape[1]]
            go_ref, d_ref, mo_ref, vo_ref = outs[4 * k:4 * k + 4]
            go_ref[...] = g
            d_ref[...], mo_ref[...], vo_ref[...] = _adam_math(w_ref[...], g, m_ref[...], v_ref[...])

    vmem = pl.BlockSpec(memory_space=pltpu.VMEM)
    flat = [a for p in params for a in p]
    shapes = tuple(jax.ShapeDtypeStruct(p[0].shape, F32) for p in params for _ in range(4))
    res = _call(body, name="adamw_small", out_shape=shapes, in_specs=[vmem] * (1 + len(flat)),
                out_specs=(vmem,) * len(shapes), compiler_params=_params(16))(sums, *flat)
    return [res[4 * k:4 * k + 4] for k in range(len(params))]


def kernel(x, norm_g, w_in, att_sinks, ret_gn_g, w_out, final_g, loss_target, m_norm_g, m_w_in, m_att_sinks, m_ret_gn_g, m_w_out, m_final_g, v_norm_g, v_w_in, v_att_sinks, v_ret_gn_g, v_w_out, v_final_g):
    seq = x.shape[1]
    xs, tgt = x[0], loss_target[0]
    final_g2 = final_g.reshape(1, D_MODEL)
    tables = _tables(seq)

    w_in_t, m_w_in_t, v_w_in_t = w_in[0].T, m_w_in[0].T, v_w_in[0].T
    w_in_full, w_out_full = _weights_allgather(w_in_t, w_out[0])
    proj = _in_proj(xs, norm_g, w_in_full, min(512, seq))
    mix, dxo, states, loss_part, gfin, o_att = _mix_fwd(proj, xs, tgt, w_out_full, final_g2, ret_gn_g, att_sinks,
                                                        tables, min(256, seq))
    dproj, gw_out, dgain, dsink = _mix_bwd(proj, dxo, mix, o_att, states, w_out_full, ret_gn_g, att_sinks, tables,
                                           min(256, seq))
    grad_x, gw_in, gnorm = _in_proj_bwd(dproj, xs, dxo, norm_g, w_in_full, min(256, seq))

    g_in, g_out, sums = _grads_reduce(gw_in, gw_out, (gnorm, gfin, dgain, dsink, loss_part))

    res_in = [r.T for r in _adamw_w_in(w_in_t, g_in, m_w_in_t, v_w_in_t, 256)]
    res_out = _adamw("adamw_w_out", w_out[0], g_out, m_w_out[0], v_w_out[0], SHARD_OUT)
    as_row = lambda a: a.reshape(1, D_MODEL)
    r_norm, r_final, r_gain, r_sink = _adamw_small(sums, [
        (norm_g, m_norm_g, v_norm_g), (final_g2, as_row(m_final_g), as_row(v_final_g)),
        (ret_gn_g, m_ret_gn_g, v_ret_gn_g), (att_sinks, m_att_sinks, v_att_sinks)])

    outs = []
    for k in range(4):
        outs += [r_norm[k], res_in[k][None], r_sink[k], r_gain[k], res_out[k][None], r_final[k].reshape(D_MODEL)]
    return (sums[4 * 8, 0], grad_x[None], *outs)
```

```python
import jax
import jax.numpy as jnp
import numpy as np
from jax import lax
from jax.experimental import pallas as pl
from jax.experimental.pallas import tpu as pltpu

F32 = jnp.float32
BF16 = jnp.bfloat16

D_MODEL = 1024
ATT_HEADS = 8
ATT_KV_HEADS = 2
ATT_HEAD_DIM = 64
RET_HEADS = 4
RET_QK_DIM = 64
RET_V_DIM = 128
BLK = 128
ROT_BASE = 10000.0
RMS_EPS = 1e-6
GN_EPS = 1e-6
NEG_INF = -1e30
ATT_SCALE = ATT_HEAD_DIM ** -0.5
RET_SCALE = RET_QK_DIM ** -0.5

ATT_WIDTH = ATT_HEADS * ATT_HEAD_DIM
ATT_KV_WIDTH = ATT_KV_HEADS * ATT_HEAD_DIM
RET_QK_WIDTH = RET_HEADS * RET_QK_DIM
RET_WIDTH = RET_HEADS * RET_V_DIM
MIX_WIDTH = ATT_WIDTH + RET_WIDTH
OFF_AQ = 0
OFF_AK = OFF_AQ + ATT_WIDTH
OFF_AV = OFF_AK + ATT_KV_WIDTH
OFF_AZ = OFF_AV + ATT_KV_WIDTH
OFF_RQ = OFF_AZ + ATT_WIDTH
OFF_RK = OFF_RQ + RET_QK_WIDTH
OFF_RV = OFF_RK + RET_QK_WIDTH
OFF_RZ = OFF_RV + RET_WIDTH
IN_WIDTH = OFF_RZ + RET_WIDTH

LANE = 128
BF16_ROWS = 16
HALF_LANE = LANE // 2
PAIRS = RET_QK_WIDTH // LANE
assert ATT_HEAD_DIM == HALF_LANE and RET_QK_DIM == HALF_LANE and RET_V_DIM == LANE and ATT_KV_WIDTH == LANE

N_CHIPS = 4
N_DEV = 8
SHARD_IN = IN_WIDTH // N_CHIPS
SHARD_PAD = 768
SHARD_SHIFT = SHARD_PAD - SHARD_IN
WIN_START = tuple((j * SHARD_IN) // LANE * LANE for j in range(N_CHIPS))
SHARD_OUT = MIX_WIDTH // N_CHIPS

PACK_PARTS = 5
PACK_ROWS = 8 * PACK_PARTS

ADAM_LR = 0.001
ADAM_B1 = 0.9
ADAM_B2 = 0.999
ADAM_EPS = 1e-08
ADAM_WD = 0.01
ADAM_STEP = 10

VMEM_CAP = 64 * 1024 * 1024
TOKENS_PROJ = 1024
TOKENS_MIX = 512
TOKENS_PROJ_BWD = 512
MESH_ID = pl.DeviceIdType.MESH


def _call(body, **kw):
    return pl.pallas_call(body, **kw)


def _params(vmem_mb, semantics=None):
    assert vmem_mb * 1024 * 1024 < VMEM_CAP
    return pltpu.CompilerParams(dimension_semantics=semantics, vmem_limit_bytes=vmem_mb * 1024 * 1024)


def _dot(a, b):
    return jnp.dot(a, b, preferred_element_type=F32)


def _dot_nt(a, b):
    return lax.dot_general(a, b, (((1,), (1,)), ((), ())), preferred_element_type=F32)


def _dot_tn(a, b):
    return lax.dot_general(a, b, (((0,), (0,)), ((), ())), preferred_element_type=F32)


def _sigmoid(z):
    return 1.0 / (1.0 + jnp.exp(-z))


def _const_spec(shape):
    nd = len(shape)
    return pl.BlockSpec(shape, lambda i: (0,) * nd, pipeline_mode=pl.Buffered(1))


def _tables(seq):
    f32 = np.float32
    pos = np.arange(seq, dtype=f32)
    theta = (f32(1.0) / (f32(ROT_BASE) ** np.linspace(0.0, 1.0, RET_QK_DIM // 2, dtype=f32))).astype(f32)
    ang = (pos[:, None] * theta[None, :]).astype(f32)
    cos, sin = np.cos(ang), np.sin(ang)
    cos2 = np.repeat(cos, 2, axis=1)
    sin2 = np.stack([-sin, sin], axis=-1).reshape(seq, RET_QK_DIM)
    cos_t = np.tile(cos2, (1, 2))
    sin_t = np.tile(sin2, (1, 2))

    log_gamma = np.log(f32(1.0) - f32(2.0) ** (f32(-5.0) - np.arange(RET_HEADS, dtype=f32))).astype(f32)
    idx = np.arange(BLK, dtype=f32)
    rel = idx[:, None] - idx[None, :]
    decay_in = np.where(rel >= 0, np.exp(log_gamma[:, None, None] * np.maximum(rel, f32(0.0))), f32(0.0))
    k_dec = np.exp(log_gamma[:, None] * (BLK - 1 - idx)[None, :])
    q_dec = np.exp(log_gamma[:, None] * (idx + 1)[None, :])
    chunk_decay = np.exp(log_gamma * f32(BLK))
    kdec_t = np.repeat(k_dec.T, RET_QK_DIM, axis=1)
    qdec_t = np.repeat(q_dec.T, RET_QK_DIM, axis=1)
    cd_t = np.broadcast_to(chunk_decay[:, None, None], (RET_HEADS, RET_QK_DIM, RET_V_DIM))
    decay_in = decay_in.reshape(PAIRS, 2 * BLK, BLK)
    cd_t = cd_t.reshape(PAIRS, 2 * RET_QK_DIM, RET_V_DIM)

    key = np.arange(2 * BLK)[:, None]
    query = np.arange(2 * BLK)[None, :] % BLK
    diff = query + BLK - key
    valid = (diff >= 0) & (diff < BLK)
    bias = np.stack([np.where(valid & (key >= BLK), 0.0, NEG_INF), np.where(valid, 0.0, NEG_INF)])
    return tuple(jnp.asarray(np.ascontiguousarray(a), F32) for a in (cos_t, sin_t, decay_in, qdec_t, kdec_t, cd_t, bias))


def _low_lanes(shape):
    lane = lax.broadcasted_iota(jnp.int32, shape, len(shape) - 1)
    return (lane & HALF_LANE) == 0


def _split_heads(t):
    low = _low_lanes(t.shape)
    zero = jnp.zeros_like(t)
    return jnp.where(low, t, zero), jnp.where(low, zero, t)


def _swap_pairs(t):
    lane = lax.broadcasted_iota(jnp.int32, t.shape, 1)
    nxt = pltpu.roll(t, t.shape[1] - 1, 1)
    prv = pltpu.roll(t, 1, 1)
    return jnp.where((lane & 1) == 0, nxt, prv)


def _per_tile(fn, t):
    return jnp.concatenate([fn(_tile(t, i)) for i in range(t.shape[1] // LANE)], axis=1)


def _rotate(t, cos_t, sin_t):
    return _per_tile(lambda a: a * cos_t + _swap_pairs(a) * sin_t, t)


def _rotate_transposed(d, cos_t, sin_t):
    return _per_tile(lambda a: a * cos_t + _swap_pairs(a * sin_t), d)


def _kv_operands(cat):
    low = _low_lanes(cat.shape)
    swapped = pltpu.roll(cat, HALF_LANE, 1)
    zero = jnp.zeros_like(cat)
    pick = lambda a, b: jnp.where(low, a, b).astype(BF16)
    return ((pick(cat, zero), pick(zero, swapped)), (pick(swapped, zero), pick(zero, cat)))


def _stack_tiles(t, first_tile):
    a = t[:, first_tile * LANE:(first_tile + 1) * LANE]
    b = t[:, (first_tile + 1) * LANE:(first_tile + 2) * LANE]
    return jnp.concatenate([a, b], axis=0)


def _sink_rows(sinks_ref, group):
    first = lax.broadcasted_iota(jnp.int32, (1, 2 * BLK), 1) < BLK

    def row(h0, h1):
        return jnp.where(first, sinks_ref[0, group * 4 + h0], sinks_ref[0, group * 4 + h1])
    return row(0, 2), row(1, 3)


ATT_PROBLEMS = tuple((g, hi) for g in range(ATT_KV_HEADS) for hi in range(2))


def _attn_probs(qs, k_ops, bias, sinks_ref):
    sink = [_sink_rows(sinks_ref, g)[hi] for g, hi in ATT_PROBLEMS]
    s = [_dot_nt(k_ops[g][hi], qs[g]) + bias for g, hi in ATT_PROBLEMS]
    m = [jnp.maximum(jnp.max(si, axis=0, keepdims=True), ki) for si, ki in zip(s, sink)]
    e = [jnp.exp(si - mi) for si, mi in zip(s, m)]
    es = [jnp.exp(ki - mi) for ki, mi in zip(sink, m)]
    inv = [1.0 / (jnp.sum(ei, axis=0, keepdims=True) + esi) for ei, esi in zip(e, es)]
    return [ei * ii for ei, ii in zip(e, inv)], [esi * ii for esi, ii in zip(es, inv)]


def _group_norm_all(outs):
    mu = [jnp.mean(o, axis=1, keepdims=True) for o in outs]
    xc = [o - m for o, m in zip(outs, mu)]
    var = [jnp.mean(c * c, axis=1, keepdims=True) for c in xc]
    rstd = [lax.rsqrt(v + GN_EPS) for v in var]
    return [c * r for c, r in zip(xc, rstd)], rstd


def _retention_operands(p_ref, rows, cos_b, sin_b, qdec, kdec):
    qr = _rotate(p_ref[rows, OFF_RQ:OFF_RQ + RET_QK_WIDTH], cos_b, sin_b)
    kr = _rotate(p_ref[rows, OFF_RK:OFF_RK + RET_QK_WIDTH], cos_b, sin_b) * RET_SCALE
    return qr, kr, qr * qdec, kr * kdec


def _tile(t, i):
    return t[:, i * LANE:(i + 1) * LANE]


def _weights_allgather(w_in_t_shard, w_out_shard):
    half_in = SHARD_IN // 2
    half_out = SHARD_OUT // 2
    assert half_in % BF16_ROWS == 0 and half_out % BF16_ROWS == 0

    def body(wi_ref, wo_ref, win_ref, wout_ref, blk_ref, blko_ref, send_sems, recv_sems):
        x, y, c = lax.axis_index("x"), lax.axis_index("y"), lax.axis_index("c")
        me, sibling = (x, y, c), (x, y, 1 - c)
        chips = [(1 - x, y), (x, 1 - y), (1 - x, 1 - y)]

        def rows_in(px, py, pc):
            return win_ref.at[pl.ds(pl.multiple_of((4 * px + 2 * py + pc) * half_in, BF16_ROWS), half_in), :]

        def rows_out(px, py, pc):
            return wout_ref.at[pl.ds(pl.multiple_of((4 * px + 2 * py + pc) * half_out, half_out), half_out), :]

        def copy(k, rows, block, to, src=None):
            return pltpu.make_async_remote_copy(
                src_ref=rows(*block) if src is None else src, dst_ref=rows(*block),
                send_sem=send_sems.at[k], recv_sem=recv_sems.at[k], device_id=to, device_id_type=MESH_ID)

        blk_ref[...] = wi_ref[pl.ds(pl.multiple_of(c * half_in, BF16_ROWS), half_in), :].astype(BF16)
        blko_ref[...] = wo_ref[pl.ds(pl.multiple_of(c * half_out, half_out), half_out), :].astype(BF16)
        rows_in(*me)[...] = blk_ref[...]
        rows_out(*me)[...] = blko_ref[...]

        sets = ((0, rows_in, blk_ref), (7, rows_out, blko_ref))
        first, passed = [], []
        for base, rows, src in sets:
            first.append(copy(base, rows, me, sibling, src=src))
            first += [copy(base + 1 + j, rows, me, (*chip, c), src=src) for j, chip in enumerate(chips)]
        for cp in first:
            cp.start()
        for base, rows, src in sets:
            for j, chip in enumerate(chips):
                copy(base + 1 + j, rows, (*chip, c), me).wait_recv()
                fwd = copy(base + 4 + j, rows, (*chip, c), sibling)
                fwd.start()
                passed.append(fwd)
        for base, rows, src in sets:
            copy(base, rows, sibling, me).wait_recv()
            for j, chip in enumerate(chips):
                copy(base + 4 + j, rows, (*chip, 1 - c), me).wait_recv()
        for cp in first + passed:
            cp.wait_send()

    vmem = pl.BlockSpec(memory_space=pltpu.VMEM)
    return _call(
        body, name="weights_allgather",
        out_shape=(jax.ShapeDtypeStruct((IN_WIDTH, D_MODEL), BF16), jax.ShapeDtypeStruct((MIX_WIDTH, D_MODEL), BF16)),
        in_specs=[vmem, vmem], out_specs=(vmem, vmem),
        scratch_shapes=[
            pltpu.VMEM((half_in, D_MODEL), BF16),
            pltpu.VMEM((half_out, D_MODEL), BF16),
            pltpu.SemaphoreType.DMA((14,)),
            pltpu.SemaphoreType.DMA((14,)),
        ],
        compiler_params=_params(32),
    )(w_in_t_shard, w_out_shard)


def _in_proj(x, norm_g, w_in, tb):
    seq = x.shape[0]

    def body(x_ref, g_ref, w_ref, p_ref):
        xv = x_ref[...]
        r = lax.rsqrt(jnp.mean(xv * xv, axis=1, keepdims=True) + RMS_EPS)
        h = (xv * r) * g_ref[...]
        p_ref[...] = _dot_nt(h.astype(BF16), w_ref[...])

    return _call(
        body, name="in_proj", grid=(seq // tb,),
        out_shape=jax.ShapeDtypeStruct((seq, IN_WIDTH), F32),
        in_specs=[pl.BlockSpec((tb, D_MODEL), lambda i: (i, 0)), _const_spec((1, D_MODEL)),
                  _const_spec((IN_WIDTH, D_MODEL))],
        out_specs=pl.BlockSpec((tb, IN_WIDTH), lambda i: (i, 0)),
        compiler_params=_params(48, ("arbitrary",)),
    )(x, norm_g, w_in)


def _mix_fwd(proj, x, target, w_out, final_g, gn_gain, sinks, tables, tb):
    seq = x.shape[0]
    nsub = tb // BLK
    cos_t, sin_t, decay_in, qdec_t, kdec_t, cd_t, bias_t = tables

    def body(p_ref, x_ref, t_ref, cos_ref, sin_ref, wout_ref, fg_ref, gain_ref, sinks_ref, din_ref, qdec_ref,
             kdec_ref, cd_ref, bias_ref, mix_ref, dxo_ref, st_ref, loss_ref, gfin_ref, oatt_ref,
             kprev_ref, vprev_ref, state_ref):
        i = pl.program_id(0)

        @pl.when(i == 0)
        def _():
            kprev_ref[...] = jnp.zeros_like(kprev_ref)
            vprev_ref[...] = jnp.zeros_like(vprev_ref)
            state_ref[...] = jnp.zeros_like(state_ref)
            loss_ref[...] = jnp.zeros_like(loss_ref)
            gfin_ref[...] = jnp.zeros_like(gfin_ref)

        def sub(j, carry):
            kp, vp, states = carry
            rows = pl.ds(pl.multiple_of(j * BLK, BLK), BLK)
            bias = bias_ref[jnp.where(jnp.logical_or(i > 0, j > 0), 1, 0)]

            aq = p_ref[rows, OFF_AQ:OFF_AQ + ATT_WIDTH]
            ak = p_ref[rows, OFF_AK:OFF_AK + ATT_KV_WIDTH]
            av = p_ref[rows, OFF_AV:OFF_AV + ATT_KV_WIDTH]
            az = p_ref[rows, OFF_AZ:OFF_AZ + ATT_WIDTH]
            k_ops = _kv_operands(jnp.concatenate([kp, ak], axis=0))
            v_ops = _kv_operands(jnp.concatenate([vp, av], axis=0))
            qs = [(_stack_tiles(aq, 2 * g) * ATT_SCALE).astype(BF16) for g in range(ATT_KV_HEADS)]
            p, _ = _attn_probs(qs, k_ops, bias, sinks_ref)
            o_tiles = []
            for g in range(ATT_KV_HEADS):
                p_cat = jnp.concatenate([p[2 * g].astype(BF16), p[2 * g + 1].astype(BF16)], axis=0)
                o = _dot_tn(p_cat, jnp.concatenate(v_ops[g], axis=0))
                o_tiles += [o[0:BLK], o[BLK:]]
            o_att = jnp.concatenate(o_tiles, axis=1)
            oatt_ref[rows, :] = o_att
            out = [o_att * (az * _sigmoid(az))]

            qr, kr, qd, kd = _retention_operands(p_ref, rows, cos_ref[rows, :], sin_ref[rows, :],
                                                 qdec_ref[...], kdec_ref[...])
            heads = [(t, hh) for t in range(PAIRS) for hh in range(2)]
            sc = [_dot_nt(jnp.concatenate(_split_heads(_tile(qr, t)), axis=0).astype(BF16), _tile(kr, t).astype(BF16))
                  * din_ref[t] for t in range(PAIRS)]
            qd_heads = [_split_heads(_tile(qd, t)) for t in range(PAIRS)]
            state_b = [states[t].astype(BF16) for t in range(PAIRS)]
            vs = [p_ref[rows, OFF_RV + h * RET_V_DIM:OFF_RV + (h + 1) * RET_V_DIM].astype(BF16) for h in range(RET_HEADS)]
            rzs = [p_ref[rows, OFF_RZ + h * RET_V_DIM:OFF_RZ + (h + 1) * RET_V_DIM] for h in range(RET_HEADS)]
            lhs = [jnp.concatenate([sc[t][hh * BLK:(hh + 1) * BLK].astype(BF16), qd_heads[t][hh].astype(BF16)], axis=1)
                   for t, hh in heads]
            ons, _ = _group_norm_all([_dot(lhs[2 * t + hh], jnp.concatenate([vs[2 * t + hh], state_b[t]], axis=0))
                                      for t, hh in heads])
            out += [(ons[h] * gain_ref[:, h * RET_V_DIM:(h + 1) * RET_V_DIM]) * (rzs[h] * _sigmoid(rzs[h]))
                    for h in range(RET_HEADS)]
            new_states = [states[t] * cd_ref[t]
                          + _dot_tn(jnp.concatenate(_split_heads(_tile(kd, t)), axis=0).astype(BF16),
                                    jnp.concatenate([vs[2 * t], vs[2 * t + 1]], axis=0)) for t in range(PAIRS)]
            mix_ref[rows, :] = jnp.concatenate(out, axis=1).astype(BF16)
            st_ref[j] = jnp.stack(states)
            return ak, av, tuple(new_states)

        carry = (kprev_ref[...], vprev_ref[...], tuple(state_ref[t] for t in range(PAIRS)))
        kp, vp, states = lax.fori_loop(0, nsub, sub, carry)
        kprev_ref[...] = kp
        vprev_ref[...] = vp
        state_ref[...] = jnp.stack(states)

        xo = x_ref[...] + _dot(mix_ref[...], wout_ref[...])
        r2 = lax.rsqrt(jnp.mean(xo * xo, axis=1, keepdims=True) + RMS_EPS)
        xn = xo * r2
        err = xn * fg_ref[...] - t_ref[...]
        loss_ref[...] += jnp.sum(err * err) * (0.5 / D_MODEL)
        dy = err * (1.0 / D_MODEL)
        gfin_ref[...] += jnp.sum(dy * xn, axis=0, keepdims=True)
        u = dy * fg_ref[...]
        dxo_ref[...] = r2 * u - xn * (r2 * jnp.mean(u * xn, axis=1, keepdims=True))

    blk_rows = lambda w: pl.BlockSpec((tb, w), lambda i: (i, 0))
    state_shape = (PAIRS, 2 * RET_QK_DIM, RET_V_DIM)
    return _call(
        body, name="mix_fwd", grid=(seq // tb,),
        out_shape=(
            jax.ShapeDtypeStruct((seq, MIX_WIDTH), BF16),
            jax.ShapeDtypeStruct((seq, D_MODEL), F32),
            jax.ShapeDtypeStruct((seq // BLK,) + state_shape, F32),
            jax.ShapeDtypeStruct((8, LANE), F32),
            jax.ShapeDtypeStruct((1, D_MODEL), F32),
            jax.ShapeDtypeStruct((seq, ATT_WIDTH), F32),
        ),
        in_specs=[
            blk_rows(IN_WIDTH), blk_rows(D_MODEL), blk_rows(D_MODEL), blk_rows(LANE), blk_rows(LANE),
            _const_spec((MIX_WIDTH, D_MODEL)), _const_spec((1, D_MODEL)), _const_spec((1, RET_WIDTH)),
            pl.BlockSpec(memory_space=pltpu.SMEM),
            _const_spec((PAIRS, 2 * BLK, BLK)), _const_spec((BLK, RET_QK_WIDTH)), _const_spec((BLK, RET_QK_WIDTH)),
            _const_spec(state_shape), _const_spec((2, 2 * BLK, 2 * BLK)),
        ],
        out_specs=(
            blk_rows(MIX_WIDTH), blk_rows(D_MODEL),
            pl.BlockSpec((nsub,) + state_shape, lambda i: (i, 0, 0, 0)),
            _const_spec((8, LANE)), _const_spec((1, D_MODEL)), blk_rows(ATT_WIDTH),
        ),
        scratch_shapes=[
            pltpu.VMEM((BLK, ATT_KV_WIDTH), F32), pltpu.VMEM((BLK, ATT_KV_WIDTH), F32),
            pltpu.VMEM(state_shape, F32),
        ],
        compiler_params=_params(48, ("arbitrary",)),
    )(proj, x, target, cos_t, sin_t, w_out, final_g, gn_gain, sinks, decay_in, qdec_t, kdec_t, cd_t, bias_t)


def _mix_bwd(proj, dxo, mix, o_att, states, w_out, gn_gain, sinks, tables, tb):
    seq = dxo.shape[0]
    nsub = tb // BLK
    nblk = seq // tb
    cos_t, sin_t, decay_in, qdec_t, kdec_t, cd_t, bias_t = tables
    kv_cols = OFF_AK // (2 * ATT_KV_WIDTH)
    state_shape = (PAIRS, 2 * RET_QK_DIM, RET_V_DIM)

    def body(p_ref, pkv_ref, dxo_ref, mix_ref, oatt_ref, st_ref, cos_ref, sin_ref, wout_ref, gain_ref, sinks_ref, din_ref,
             qdec_ref, kdec_ref, cd_ref, bias_ref, dp_ref, gwout_ref, dgain_ref, dsink_ref,
             dmix_ref, kv_ref, dkc_ref, dvc_ref, gst_ref):
        i = pl.program_id(0)
        blk = nblk - 1 - i

        @pl.when(i == 0)
        def _():
            gwout_ref[...] = jnp.zeros_like(gwout_ref)
            dgain_ref[...] = jnp.zeros_like(dgain_ref)
            dsink_ref[...] = jnp.zeros_like(dsink_ref)
            dkc_ref[...] = jnp.zeros_like(dkc_ref)
            dvc_ref[...] = jnp.zeros_like(dvc_ref)
            gst_ref[...] = jnp.zeros_like(gst_ref)

        dxo_b = dxo_ref[...].astype(BF16)
        dmix_ref[...] = _dot_nt(dxo_b, wout_ref[...])
        gwout_ref[...] += _dot_tn(mix_ref[...], dxo_b)
        kv_ref[0:BLK, :] = pkv_ref[...]
        kv_ref[BLK:, :] = p_ref[:, OFF_AK:OFF_AK + 2 * ATT_KV_WIDTH]
        low = _low_lanes((BLK, LANE))
        low2 = _low_lanes((2 * BLK, LANE))
        lane = lax.broadcasted_iota(jnp.int32, (1, LANE), 1)

        def sub(jj, carry):
            dkc, dvc, gsts, dgain, dsink = carry
            j = nsub - 1 - jj
            rows = pl.ds(pl.multiple_of(j * BLK, BLK), BLK)
            both = pl.ds(pl.multiple_of(j * BLK, BLK), 2 * BLK)
            bias = bias_ref[jnp.where(jnp.logical_or(blk > 0, j > 0), 1, 0)]

            aq = p_ref[rows, OFF_AQ:OFF_AQ + ATT_WIDTH]
            az = p_ref[rows, OFF_AZ:OFF_AZ + ATT_WIDTH]
            k_ops = _kv_operands(kv_ref[both, 0:ATT_KV_WIDTH])
            v_ops = _kv_operands(kv_ref[both, ATT_KV_WIDTH:2 * ATT_KV_WIDTH])
            da = dmix_ref[rows, 0:ATT_WIDTH]
            sig = _sigmoid(az)
            d_o = da * (az * sig)
            qs = [(_stack_tiles(aq, 2 * g) * ATT_SCALE).astype(BF16) for g in range(ATT_KV_HEADS)]
            dos = [_stack_tiles(d_o, 2 * g).astype(BF16) for g in range(ATT_KV_HEADS)]
            p, p_sink = _attn_probs(qs, k_ops, bias, sinks_ref)
            dpr = [_dot_nt(v_ops[g][hi], dos[g]) for g, hi in ATT_PROBLEMS]
            delta = [jnp.sum(pi * di, axis=0, keepdims=True) for pi, di in zip(p, dpr)]
            ds = [(pi * (di - ti)).astype(BF16) for pi, di, ti in zip(p, dpr, delta)]
            for (g, hi), ki, ti in zip(ATT_PROBLEMS, p_sink, delta):
                sink_part = ki * ti
                for half in range(2):
                    tot = jnp.sum(sink_part[:, half * BLK:(half + 1) * BLK], axis=1, keepdims=True)
                    dsink = dsink - jnp.where(lane == 4 * g + 2 * half + hi, tot, 0.0)
            dq_tiles, dk_sums, dv_sums = [], [], []
            for g in range(ATT_KV_HEADS):
                ds_cat = jnp.concatenate([ds[2 * g], ds[2 * g + 1]], axis=0)
                p_cat = jnp.concatenate([p[2 * g].astype(BF16), p[2 * g + 1].astype(BF16)], axis=0)
                dqs = _dot_tn(ds_cat, jnp.concatenate(k_ops[g], axis=0)) * ATT_SCALE
                dq_tiles += [dqs[0:BLK], dqs[BLK:]]
                dk_sums.append(_dot(ds_cat, qs[g]))
                dv_sums.append(_dot(p_cat, dos[g]))
            daz = da * oatt_ref[rows, :] * (sig * (1.0 + az * (1.0 - sig)))

            def kv_grad(sums):
                (a0, b0), (a1, b1) = [(s[0:2 * BLK], s[2 * BLK:]) for s in sums]
                return jnp.where(low2, a0, b1) + pltpu.roll(jnp.where(low2, a1, b0), HALF_LANE, 1)

            dk_both, dv_both = kv_grad(dk_sums), kv_grad(dv_sums)
            dak = dk_both[BLK:] + dkc
            dav = dv_both[BLK:] + dvc

            cos_b, sin_b = cos_ref[rows, :], sin_ref[rows, :]
            qdec, kdec = qdec_ref[...], kdec_ref[...]
            qr, kr, qd, kd = _retention_operands(p_ref, rows, cos_b, sin_b, qdec, kdec)
            heads = [(t, hh) for t in range(PAIRS) for hh in range(2)]
            head_cols = [slice(h * RET_V_DIM, (h + 1) * RET_V_DIM) for h in range(RET_HEADS)]
            q_rows = [jnp.concatenate(_split_heads(_tile(qr, t)), axis=0).astype(BF16) for t in range(PAIRS)]
            k_rows = [jnp.concatenate(_split_heads(_tile(kr, t)), axis=0).astype(BF16) for t in range(PAIRS)]
            din = [din_ref[t] for t in range(PAIRS)]
            sc = [(_dot_nt(q_rows[t], _tile(kr, t).astype(BF16)) * din[t]).astype(BF16) for t in range(PAIRS)]
            qd_heads = [_split_heads(_tile(qd, t)) for t in range(PAIRS)]
            kd_heads = [_split_heads(_tile(kd, t)) for t in range(PAIRS)]
            state_b = [st_ref[j, t].astype(BF16) for t in range(PAIRS)]
            gst_b = [gsts[t].astype(BF16) for t in range(PAIRS)]
            vs = [p_ref[rows, OFF_RV + h * RET_V_DIM:OFF_RV + (h + 1) * RET_V_DIM].astype(BF16) for h in range(RET_HEADS)]
            rzs = [p_ref[rows, OFF_RZ + h * RET_V_DIM:OFF_RZ + (h + 1) * RET_V_DIM] for h in range(RET_HEADS)]
            drs = [dmix_ref[rows, ATT_WIDTH + h * RET_V_DIM:ATT_WIDTH + (h + 1) * RET_V_DIM] for h in range(RET_HEADS)]
            gains = [gain_ref[:, c] for c in head_cols]
            lhs = [jnp.concatenate([sc[t][hh * BLK:(hh + 1) * BLK], qd_heads[t][hh].astype(BF16)], axis=1) for t, hh in heads]
            rhs = [jnp.concatenate([vs[2 * t + hh], state_b[t]], axis=0) for t, hh in heads]
            ons, rstds = _group_norm_all([_dot(l, r) for l, r in zip(lhs, rhs)])
            sig_r = [_sigmoid(z) for z in rzs]
            dgn = [d * (z * g) for d, z, g in zip(drs, rzs, sig_r)]
            dz_parts = [d * (o * gn) * (g * (1.0 + z * (1.0 - g))) for d, o, gn, g, z in zip(drs, ons, gains, sig_r, rzs)]
            dgain_parts = [jnp.sum(d * o, axis=0, keepdims=True) for d, o in zip(dgn, ons)]
            don = [d * gn for d, gn in zip(dgn, gains)]
            mean_don = [jnp.mean(d, axis=1, keepdims=True) for d in don]
            mean_don_on = [jnp.mean(d * o, axis=1, keepdims=True) for d, o in zip(don, ons)]
            dob = [(r * (d - a - o * b)).astype(BF16) for r, d, a, o, b in zip(rstds, don, mean_don, ons, mean_don_on)]
            dlhs = [_dot_nt(d, r) for d, r in zip(dob, rhs)]
            drhs = [_dot_tn(l, d) for l, d in zip(lhs, dob)]
            dkds = [_dot_nt(vs[2 * t + hh], gst_b[t]) for t, hh in heads]
            dv_parts = [drhs[2 * t + hh][0:BLK] + _dot(kd_heads[t][hh].astype(BF16), gst_b[t]) for t, hh in heads]
            das = [(dlhs[2 * t + hh][:, 0:BLK] * din[t][hh * BLK:(hh + 1) * BLK]).astype(BF16) for t, hh in heads]
            new_gsts = [gsts[t] * cd_ref[t] + drhs[2 * t][BLK:] + drhs[2 * t + 1][BLK:] for t in range(PAIRS)]
            dq_parts = [_dot(jnp.concatenate([das[2 * t], das[2 * t + 1]], axis=1), k_rows[t])
                        + jnp.where(low, dlhs[2 * t][:, BLK:], dlhs[2 * t + 1][:, BLK:]) * _tile(qdec, t)
                        for t in range(PAIRS)]
            dk_parts = [_dot_tn(jnp.concatenate([das[2 * t], das[2 * t + 1]], axis=0), q_rows[t])
                        + jnp.where(low, dkds[2 * t], dkds[2 * t + 1]) * _tile(kdec, t) for t in range(PAIRS)]
            drq = _rotate_transposed(jnp.concatenate(dq_parts, axis=1), cos_b, sin_b)
            drk = _rotate_transposed(jnp.concatenate(dk_parts, axis=1) * RET_SCALE, cos_b, sin_b)

            dp_ref[rows, :] = jnp.concatenate(
                [jnp.concatenate(dq_tiles, axis=1), dak, dav, daz, drq, drk] + dv_parts + dz_parts, axis=1).astype(BF16)
            dgain = dgain + jnp.concatenate(dgain_parts, axis=1)
            return dk_both[0:BLK], dv_both[0:BLK], tuple(new_gsts), dgain, dsink

        carry = (dkc_ref[...], dvc_ref[...], tuple(gst_ref[t] for t in range(PAIRS)), dgain_ref[...], dsink_ref[...])
        dkc, dvc, gsts, dgain, dsink = lax.fori_loop(0, nsub, sub, carry)
        dkc_ref[...] = dkc
        dvc_ref[...] = dvc
        gst_ref[...] = jnp.stack(gsts)
        dgain_ref[...] = dgain
        dsink_ref[...] = dsink

    rev_rows = lambda w: pl.BlockSpec((tb, w), lambda i: (nblk - 1 - i, 0))
    prev_kv = pl.BlockSpec((BLK, 2 * ATT_KV_WIDTH), lambda i: (jnp.maximum((nblk - 1 - i) * nsub - 1, 0), kv_cols))
    return _call(
        body, name="mix_bwd", grid=(nblk,),
        out_shape=(
            jax.ShapeDtypeStruct((seq, IN_WIDTH), BF16),
            jax.ShapeDtypeStruct((MIX_WIDTH, D_MODEL), F32),
            jax.ShapeDtypeStruct((1, RET_WIDTH), F32),
            jax.ShapeDtypeStruct((1, LANE), F32),
        ),
        in_specs=[
            rev_rows(IN_WIDTH), prev_kv, rev_rows(D_MODEL), rev_rows(MIX_WIDTH), rev_rows(ATT_WIDTH),
            pl.BlockSpec((nsub,) + state_shape, lambda i: (nblk - 1 - i, 0, 0, 0)),
            rev_rows(LANE), rev_rows(LANE),
            _const_spec((MIX_WIDTH, D_MODEL)), _const_spec((1, RET_WIDTH)),
            pl.BlockSpec(memory_space=pltpu.SMEM),
            _const_spec((PAIRS, 2 * BLK, BLK)), _const_spec((BLK, RET_QK_WIDTH)), _const_spec((BLK, RET_QK_WIDTH)),
            _const_spec(state_shape), _const_spec((2, 2 * BLK, 2 * BLK)),
        ],
        out_specs=(
            rev_rows(IN_WIDTH), _const_spec((MIX_WIDTH, D_MODEL)), _const_spec((1, RET_WIDTH)), _const_spec((1, LANE)),
        ),
        scratch_shapes=[
            pltpu.VMEM((tb, MIX_WIDTH), F32),
            pltpu.VMEM((tb + BLK, 2 * ATT_KV_WIDTH), F32),
            pltpu.VMEM((BLK, ATT_KV_WIDTH), F32), pltpu.VMEM((BLK, ATT_KV_WIDTH), F32),
            pltpu.VMEM(state_shape, F32),
        ],
        compiler_params=_params(56, ("arbitrary",)),
    )(proj, proj, dxo, mix, o_att, states, cos_t, sin_t, w_out, gn_gain, sinks, decay_in, qdec_t, kdec_t, cd_t, bias_t)


def _in_proj_bwd(dproj, x, dxo, norm_g, w_in, tb):
    seq = x.shape[0]

    def body(dp_ref, x_ref, dxo_ref, g_ref, w_ref, gx_ref, gw_ref, gnorm_ref):
        i = pl.program_id(0)

        @pl.when(i == 0)
        def _():
            gw_ref[...] = jnp.zeros_like(gw_ref)
            gnorm_ref[...] = jnp.zeros_like(gnorm_ref)

        dp = dp_ref[...]
        xv = x_ref[...]
        r = lax.rsqrt(jnp.mean(xv * xv, axis=1, keepdims=True) + RMS_EPS)
        xn = xv * r
        h = xn * g_ref[...]
        gw_ref[...] += _dot_tn(h.astype(BF16), dp)
        dh = _dot(dp, w_ref[...])
        gnorm_ref[...] += jnp.sum(dh * xn, axis=0, keepdims=True)
        u = dh * g_ref[...]
        gx_ref[...] = dxo_ref[...] + r * u - xn * (r * jnp.mean(u * xn, axis=1, keepdims=True))

    rows = lambda w: pl.BlockSpec((tb, w), lambda i: (i, 0))
    return _call(
        body, name="in_proj_bwd", grid=(seq // tb,),
        out_shape=(
            jax.ShapeDtypeStruct((seq, D_MODEL), F32),
            jax.ShapeDtypeStruct((D_MODEL, IN_WIDTH), F32),
            jax.ShapeDtypeStruct((1, D_MODEL), F32),
        ),
        in_specs=[rows(IN_WIDTH), rows(D_MODEL), rows(D_MODEL), _const_spec((1, D_MODEL)),
                  _const_spec((IN_WIDTH, D_MODEL))],
        out_specs=(rows(D_MODEL), _const_spec((D_MODEL, IN_WIDTH)), _const_spec((1, D_MODEL))),
        compiler_params=_params(60, ("arbitrary",)),
    )(dproj, x, dxo, norm_g, w_in)


def _grads_reduce(gw_in, gw_out, small):
    half_in = D_MODEL // 2
    half_out = SHARD_OUT // 2
    A_IN, A_OUT, B_IN, B_OUT, C_IN, C_OUT, PACK, N_SEMS = 0, 1, 5, 9, 13, 14, 15, 23
    assert len(small) == PACK_PARTS

    def body(gwi_hbm, gwo_hbm, s0_ref, s1_ref, s2_ref, s3_ref, s4_ref, fin_ref, fout_ref, packsum_ref,
             mine_in, sib_in, mine_out, sib_out, send_in, send_out, b_in, b_out, pack_ref, packs,
             send_sems, recv_sems, local_sems):
        x, y, c = lax.axis_index("x"), lax.axis_index("y"), lax.axis_index("c")
        chip = 2 * x + y
        dev = 2 * chip + c
        sibling = (x, y, 1 - c)
        pack_ref[...] = jnp.zeros_like(pack_ref)
        for k, s_ref in enumerate((s0_ref, s1_ref, s2_ref, s3_ref, s4_ref)):
            pack_ref[8 * k:8 * k + 1, 0:s_ref.shape[1]] = s_ref[0:1, :]

        def remote(src, dst, send_k, recv_k, to):
            return pltpu.make_async_remote_copy(src_ref=src, dst_ref=dst, send_sem=send_sems.at[send_k],
                                                recv_sem=recv_sems.at[recv_k], device_id=to, device_id_type=MESH_ID)

        def in_rows(core):
            return pl.ds(pl.multiple_of(core * half_in, half_in), half_in)

        def out_rows(j, core):
            return pl.ds(pl.multiple_of(j * SHARD_OUT + core * half_out, half_out), half_out)

        packs[dev] = pack_ref[...]
        for d in range(N_DEV):
            to = (d // 4, (d // 2) % 2, d % 2)

            @pl.when(d != dev)
            def _():
                remote(pack_ref, packs.at[dev], PACK + d, PACK + dev, to).start()

        local = [pltpu.make_async_copy(gwi_hbm.at[in_rows(c), :], mine_in, local_sems.at[0])]
        local += [pltpu.make_async_copy(gwo_hbm.at[out_rows(j, c), :], mine_out.at[j], local_sems.at[1 + j])
                  for j in range(N_CHIPS)]
        for cp in local:
            cp.start()
        stage_a = [remote(gwi_hbm.at[in_rows(1 - c), :], sib_in, A_IN, A_IN, sibling)]
        stage_a += [remote(gwo_hbm.at[out_rows(j, 1 - c), :], sib_out.at[j], A_OUT + j, A_OUT + j, sibling)
                    for j in range(N_CHIPS)]
        for cp in stage_a:
            cp.start()
        for cp in local:
            cp.wait()
        for cp in stage_a:
            cp.wait_recv()

        def add_rows(k, carry):
            rows = pl.ds(pl.multiple_of(k * 64, 64), 64)
            mine_in[rows, :] = mine_in[rows, :] + sib_in[rows, :]
            return carry

        lax.fori_loop(0, half_in // 64, add_rows, 0)
        for j in range(N_CHIPS):
            mine_out[j] = mine_out[j] + sib_out[j]

        my_out_rows = pl.ds(pl.multiple_of(c * half_out, half_out), half_out)
        for j in range(N_CHIPS):
            to = (j // 2, j % 2, c)

            @pl.when(j != chip)
            def _():
                send_in[j] = mine_in[:, WIN_START[j]:WIN_START[j] + SHARD_PAD].astype(BF16)
                send_out[j] = mine_out[j].astype(BF16)
                remote(send_in.at[j], b_in.at[chip], B_IN + j, B_IN + chip, to).start()
                remote(send_out.at[j], b_out.at[chip], B_OUT + j, B_OUT + chip, to).start()

            @pl.when(j == chip)
            def _():
                fin_ref[in_rows(c), :] = mine_in[:, WIN_START[j]:WIN_START[j] + SHARD_PAD]
                fout_ref[my_out_rows, :] = mine_out[j]

        for j in range(N_CHIPS):
            @pl.when(j != chip)
            def _():
                remote(b_in.at[j], b_in.at[j], B_IN + j, B_IN + j, sibling).wait_recv()
                remote(b_out.at[j], b_out.at[j], B_OUT + j, B_OUT + j, sibling).wait_recv()
                fin_ref[in_rows(c), :] += b_in[j].astype(F32)
                fout_ref[my_out_rows, :] += b_out[j].astype(F32)

        mine_half = fin_ref.at[in_rows(c), :]
        mine_half_out = fout_ref.at[my_out_rows, :]
        stage_c = [remote(mine_half, mine_half, C_IN, C_IN, sibling),
                   remote(mine_half_out, mine_half_out, C_OUT, C_OUT, sibling)]
        for cp in stage_c:
            cp.start()
        other_half = fin_ref.at[in_rows(1 - c), :]
        other_half_out = fout_ref.at[pl.ds(pl.multiple_of((1 - c) * half_out, half_out), half_out), :]
        remote(other_half, other_half, C_IN, C_IN, sibling).wait_recv()
        remote(other_half_out, other_half_out, C_OUT, C_OUT, sibling).wait_recv()

        for d in range(N_DEV):
            @pl.when(d != dev)
            def _():
                remote(pack_ref, packs.at[d], PACK + d, PACK + d, sibling).wait_recv()
        total = packs[0]
        for d in range(1, N_DEV):
            total = total + packs[d]
        packsum_ref[...] = total

        for cp in stage_a + stage_c:
            cp.wait_send()
        for j in range(N_CHIPS):
            @pl.when(j != chip)
            def _():
                remote(b_in.at[j], b_in.at[j], B_IN + j, B_IN + j, sibling).wait_send()
                remote(b_out.at[j], b_out.at[j], B_OUT + j, B_OUT + j, sibling).wait_send()
        for d in range(N_DEV):
            @pl.when(d != dev)
            def _():
                remote(pack_ref, packs.at[d], PACK + d, PACK + d, sibling).wait_send()

    vmem = pl.BlockSpec(memory_space=pltpu.VMEM)
    hbm = pl.BlockSpec(memory_space=pl.ANY)
    return _call(
        body, name="grads_reduce",
        out_shape=(jax.ShapeDtypeStruct((D_MODEL, SHARD_PAD), F32), jax.ShapeDtypeStruct((SHARD_OUT, D_MODEL), F32),
                   jax.ShapeDtypeStruct((PACK_ROWS, D_MODEL), F32)),
        in_specs=[hbm, hbm] + [vmem] * PACK_PARTS, out_specs=(vmem, vmem, vmem),
        scratch_shapes=[
            pltpu.VMEM((half_in, IN_WIDTH), F32), pltpu.VMEM((half_in, IN_WIDTH), F32),
            pltpu.VMEM((N_CHIPS, half_out, D_MODEL), F32), pltpu.VMEM((N_CHIPS, half_out, D_MODEL), F32),
            pltpu.VMEM((N_CHIPS, half_in, SHARD_PAD), BF16), pltpu.VMEM((N_CHIPS, half_out, D_MODEL), BF16),
            pltpu.VMEM((N_CHIPS, half_in, SHARD_PAD), BF16), pltpu.VMEM((N_CHIPS, half_out, D_MODEL), BF16),
            pltpu.VMEM((PACK_ROWS, D_MODEL), F32), pltpu.VMEM((N_DEV, PACK_ROWS, D_MODEL), F32),
            pltpu.SemaphoreType.DMA((N_SEMS,)), pltpu.SemaphoreType.DMA((N_SEMS,)),
            pltpu.SemaphoreType.DMA((1 + N_CHIPS,)),
        ],
        compiler_params=_params(48),
    )(gw_in, gw_out, *small)


def _adam_math(w, g, m, v):
    mn = ADAM_B1 * m + (1.0 - ADAM_B1) * g
    vn = ADAM_B2 * v + (1.0 - ADAM_B2) * (g * g)
    m_hat = mn / (1.0 - ADAM_B1 ** ADAM_STEP)
    v_hat = vn / (1.0 - ADAM_B2 ** ADAM_STEP)
    return -ADAM_LR * (m_hat / (jnp.sqrt(v_hat) + ADAM_EPS) + ADAM_WD * w), mn, vn


def _adamw(name, w, g, m, v, tb):
    rows, cols = w.shape

    def body(w_ref, g_ref, m_ref, v_ref, go_ref, d_ref, mo_ref, vo_ref):
        gv = g_ref[...]
        go_ref[...] = gv
        d_ref[...], mo_ref[...], vo_ref[...] = _adam_math(w_ref[...], gv, m_ref[...], v_ref[...])

    spec = pl.BlockSpec((tb, cols), lambda i: (i, 0))
    shape = jax.ShapeDtypeStruct((rows, cols), F32)
    return _call(
        body, name=name, grid=(rows // tb,), out_shape=(shape,) * 4,
        in_specs=[spec] * 4, out_specs=(spec,) * 4,
        compiler_params=_params(32, ("arbitrary",)),
    )(w, g, m, v)


def _adamw_w_in(w_t, g_window, m_t, v_t, tb):
    def body(w_ref, g_ref, m_ref, v_ref, go_ref, d_ref, mo_ref, vo_ref, gt_ref):
        gt_ref[...] = g_ref[...].T
        gv = gt_ref[pl.ds(pl.multiple_of(lax.axis_index("y") * SHARD_SHIFT, SHARD_SHIFT), SHARD_IN), :]
        go_ref[...] = gv
        d_ref[...], mo_ref[...], vo_ref[...] = _adam_math(w_ref[...], gv, m_ref[...], v_ref[...])

    spec = pl.BlockSpec((SHARD_IN, tb), lambda i: (0, i))
    shape = jax.ShapeDtypeStruct((SHARD_IN, D_MODEL), F32)
    return _call(
        body, name="adamw_w_in", grid=(D_MODEL // tb,), out_shape=(shape,) * 4,
        in_specs=[spec, pl.BlockSpec((tb, SHARD_PAD), lambda i: (i, 0)), spec, spec], out_specs=(spec,) * 4,
        scratch_shapes=[pltpu.VMEM((SHARD_PAD, tb), F32)],
        compiler_params=_params(32, ("arbitrary",)),
    )(w_t, g_window, m_t, v_t)


def _adamw_small(sums, params):
    def body(sums_ref, *refs):
        ins, outs = refs[:3 * len(params)], refs[3 * len(params):]
        for k in range(len(params)):
            w_ref, m_ref, v_ref = ins[3 * k:3 * k + 3]
            g = sums_ref[8 * k:8 * k + 1, 0:w_ref.shape[1]]
            go_ref, d_ref, mo_ref, vo_ref = outs[4 * k:4 * k + 4]
            go_ref[...] = g
            d_ref[...], mo_ref[...], vo_ref[...] = _adam_math(w_ref[...], g, m_ref[...], v_ref[...])

    vmem = pl.BlockSpec(memory_space=pltpu.VMEM)
    flat = [a for p in params for a in p]
    shapes = tuple(jax.ShapeDtypeStruct(p[0].shape, F32) for p in params for _ in range(4))
    res = _call(body, name="adamw_small", out_shape=shapes, in_specs=[vmem] * (1 + len(flat)),
                out_specs=(vmem,) * len(shapes), compiler_params=_params(16))(sums, *flat)
    return [res[4 * k:4 * k + 4] for k in range(len(params))]


def kernel(x, norm_g, w_in, att_sinks, ret_gn_g, w_out, final_g, loss_target, m_norm_g, m_w_in, m_att_sinks, m_ret_gn_g, m_w_out, m_final_g, v_norm_g, v_w_in, v_att_sinks, v_ret_gn_g, v_w_out, v_final_g):
    seq = x.shape[1]
    xs, tgt = x[0], loss_target[0]
    final_g2 = final_g.reshape(1, D_MODEL)
    tables = _tables(seq)

    w_in_t, m_w_in_t, v_w_in_t = w_in[0].T, m_w_in[0].T, v_w_in[0].T
    w_in_full, w_out_full = _weights_allgather(w_in_t, w_out[0])
    proj = _in_proj(xs, norm_g, w_in_full, min(TOKENS_PROJ, seq))
    mix, dxo, states, loss_part, gfin, o_att = _mix_fwd(proj, xs, tgt, w_out_full, final_g2, ret_gn_g, att_sinks,
                                                        tables, min(TOKENS_MIX, seq))
    dproj, gw_out, dgain, dsink = _mix_bwd(proj, dxo, mix, o_att, states, w_out_full, ret_gn_g, att_sinks, tables,
                                           min(TOKENS_MIX, seq))
    grad_x, gw_in, gnorm = _in_proj_bwd(dproj, xs, dxo, norm_g, w_in_full, min(TOKENS_PROJ_BWD, seq))

    g_in, g_out, sums = _grads_reduce(gw_in, gw_out, (gnorm, gfin, dgain, dsink, loss_part))

    res_in = [r.T for r in _adamw_w_in(w_in_t, g_in, m_w_in_t, v_w_in_t, 256)]
    res_out = _adamw("adamw_w_out", w_out[0], g_out, m_w_out[0], v_w_out[0], SHARD_OUT)
    as_row = lambda a: a.reshape(1, D_MODEL)
    r_norm, r_final, r_gain, r_sink = _adamw_small(sums, [
        (norm_g, m_norm_g, v_norm_g), (final_g2, as_row(m_final_g), as_row(v_final_g)),
        (ret_gn_g, m_ret_gn_g, v_ret_gn_g), (att_sinks, m_att_sinks, v_att_sinks)])

    outs = []
    for k in range(4):
        outs += [r_norm[k], res_in[k][None], r_sink[k], r_gain[k], res_out[k][None], r_final[k].reshape(D_MODEL)]
    return (sums[4 * 8, 0], grad_x[None], *outs)
```

```python
import jax
import jax.numpy as jnp
import numpy as np
from jax import lax
from jax.experimental import pallas as pl
from jax.experimental.pallas import tpu as pltpu

F32 = jnp.float32
BF16 = jnp.bfloat16

D_MODEL = 1024
ATT_HEADS = 8
ATT_KV_HEADS = 2
ATT_HEAD_DIM = 64
RET_HEADS = 4
RET_QK_DIM = 64
RET_V_DIM = 128
BLK = 128
ROT_BASE = 10000.0
RMS_EPS = 1e-6
GN_EPS = 1e-6
NEG_INF = -1e30
ATT_SCALE = ATT_HEAD_DIM ** -0.5
RET_SCALE = RET_QK_DIM ** -0.5

ATT_WIDTH = ATT_HEADS * ATT_HEAD_DIM
ATT_KV_WIDTH = ATT_KV_HEADS * ATT_HEAD_DIM
RET_QK_WIDTH = RET_HEADS * RET_QK_DIM
RET_WIDTH = RET_HEADS * RET_V_DIM
MIX_WIDTH = ATT_WIDTH + RET_WIDTH
OFF_AQ = 0
OFF_AK = OFF_AQ + ATT_WIDTH
OFF_AV = OFF_AK + ATT_KV_WIDTH
OFF_AZ = OFF_AV + ATT_KV_WIDTH
OFF_RQ = OFF_AZ + ATT_WIDTH
OFF_RK = OFF_RQ + RET_QK_WIDTH
OFF_RV = OFF_RK + RET_QK_WIDTH
OFF_RZ = OFF_RV + RET_WIDTH
IN_WIDTH = OFF_RZ + RET_WIDTH

LANE = 128
BF16_ROWS = 16
HALF_LANE = LANE // 2
PAIRS = RET_QK_WIDTH // LANE
assert ATT_HEAD_DIM == HALF_LANE and RET_QK_DIM == HALF_LANE and RET_V_DIM == LANE and ATT_KV_WIDTH == LANE

N_CHIPS = 4
N_DEV = 8
SHARD_IN = IN_WIDTH // N_CHIPS
SHARD_PAD = 768
SHARD_SHIFT = SHARD_PAD - SHARD_IN
WIN_START = tuple((j * SHARD_IN) // LANE * LANE for j in range(N_CHIPS))
SHARD_OUT = MIX_WIDTH // N_CHIPS

PACK_PARTS = 5
PACK_ROWS = 8 * PACK_PARTS

ADAM_LR = 0.001
ADAM_B1 = 0.9
ADAM_B2 = 0.999
ADAM_EPS = 1e-08
ADAM_WD = 0.01
ADAM_STEP = 10

VMEM_CAP = 64 * 1024 * 1024
TOKENS_PROJ = 1024
TOKENS_MIX = 512
TOKENS_PROJ_BWD = 512
TOKENS_GW = 512
MESH_ID = pl.DeviceIdType.MESH


def _call(body, **kw):
    return pl.pallas_call(body, **kw)


def _params(vmem_mb, semantics=None):
    assert vmem_mb * 1024 * 1024 < VMEM_CAP
    return pltpu.CompilerParams(dimension_semantics=semantics, vmem_limit_bytes=vmem_mb * 1024 * 1024)


def _dot(a, b):
    return jnp.dot(a, b, preferred_element_type=F32)


def _dot_nt(a, b):
    return lax.dot_general(a, b, (((1,), (1,)), ((), ())), preferred_element_type=F32)


def _dot_tn(a, b):
    return lax.dot_general(a, b, (((0,), (0,)), ((), ())), preferred_element_type=F32)


def _sigmoid(z):
    return 1.0 / (1.0 + jnp.exp(-z))


def _const_spec(shape):
    nd = len(shape)
    return pl.BlockSpec(shape, lambda i: (0,) * nd, pipeline_mode=pl.Buffered(1))


def _tables(seq):
    f32 = np.float32
    pos = np.arange(seq, dtype=f32)
    theta = (f32(1.0) / (f32(ROT_BASE) ** np.linspace(0.0, 1.0, RET_QK_DIM // 2, dtype=f32))).astype(f32)
    ang = (pos[:, None] * theta[None, :]).astype(f32)
    cos, sin = np.cos(ang), np.sin(ang)
    cos2 = np.repeat(cos, 2, axis=1)
    sin2 = np.stack([-sin, sin], axis=-1).reshape(seq, RET_QK_DIM)
    cos_t = np.tile(cos2, (1, 2))
    sin_t = np.tile(sin2, (1, 2))

    log_gamma = np.log(f32(1.0) - f32(2.0) ** (f32(-5.0) - np.arange(RET_HEADS, dtype=f32))).astype(f32)
    idx = np.arange(BLK, dtype=f32)
    rel = idx[:, None] - idx[None, :]
    decay_in = np.where(rel >= 0, np.exp(log_gamma[:, None, None] * np.maximum(rel, f32(0.0))), f32(0.0))
    k_dec = np.exp(log_gamma[:, None] * (BLK - 1 - idx)[None, :])
    q_dec = np.exp(log_gamma[:, None] * (idx + 1)[None, :])
    chunk_decay = np.exp(log_gamma * f32(BLK))
    kdec_t = np.repeat(k_dec.T, RET_QK_DIM, axis=1)
    qdec_t = np.repeat(q_dec.T, RET_QK_DIM, axis=1)
    cd_t = np.broadcast_to(chunk_decay[:, None, None], (RET_HEADS, RET_QK_DIM, RET_V_DIM))
    decay_in = decay_in.reshape(PAIRS, 2 * BLK, BLK)
    cd_t = cd_t.reshape(PAIRS, 2 * RET_QK_DIM, RET_V_DIM)

    key = np.arange(2 * BLK)[:, None]
    query = np.arange(2 * BLK)[None, :] % BLK
    diff = query + BLK - key
    valid = (diff >= 0) & (diff < BLK)
    bias = np.stack([np.where(valid & (key >= BLK), 0.0, NEG_INF), np.where(valid, 0.0, NEG_INF)])
    return tuple(jnp.asarray(np.ascontiguousarray(a), F32) for a in (cos_t, sin_t, decay_in, qdec_t, kdec_t, cd_t, bias))


def _low_lanes(shape):
    lane = lax.broadcasted_iota(jnp.int32, shape, len(shape) - 1)
    return (lane & HALF_LANE) == 0


def _split_heads(t):
    low = _low_lanes(t.shape)
    zero = jnp.zeros_like(t)
    return jnp.where(low, t, zero), jnp.where(low, zero, t)


def _swap_pairs(t):
    lane = lax.broadcasted_iota(jnp.int32, t.shape, 1)
    nxt = pltpu.roll(t, t.shape[1] - 1, 1)
    prv = pltpu.roll(t, 1, 1)
    return jnp.where((lane & 1) == 0, nxt, prv)


def _per_tile(fn, t):
    return jnp.concatenate([fn(_tile(t, i)) for i in range(t.shape[1] // LANE)], axis=1)


def _rotate(t, cos_t, sin_t):
    return _per_tile(lambda a: a * cos_t + _swap_pairs(a) * sin_t, t)


def _rotate_transposed(d, cos_t, sin_t):
    return _per_tile(lambda a: a * cos_t + _swap_pairs(a * sin_t), d)


def _kv_operands(cat):
    low = _low_lanes(cat.shape)
    swapped = pltpu.roll(cat, HALF_LANE, 1)
    zero = jnp.zeros_like(cat)
    pick = lambda a, b: jnp.where(low, a, b).astype(BF16)
    return ((pick(cat, zero), pick(zero, swapped)), (pick(swapped, zero), pick(zero, cat)))


def _stack_tiles(t, first_tile):
    a = t[:, first_tile * LANE:(first_tile + 1) * LANE]
    b = t[:, (first_tile + 1) * LANE:(first_tile + 2) * LANE]
    return jnp.concatenate([a, b], axis=0)


def _sink_rows(sinks_ref, group):
    first = lax.broadcasted_iota(jnp.int32, (1, 2 * BLK), 1) < BLK

    def row(h0, h1):
        return jnp.where(first, sinks_ref[0, group * 4 + h0], sinks_ref[0, group * 4 + h1])
    return row(0, 2), row(1, 3)


ATT_PROBLEMS = tuple((g, hi) for g in range(ATT_KV_HEADS) for hi in range(2))


def _attn_probs(qs, k_ops, bias, sinks_ref):
    sink = [_sink_rows(sinks_ref, g)[hi] for g, hi in ATT_PROBLEMS]
    s = [_dot_nt(k_ops[g][hi], qs[g]) + bias for g, hi in ATT_PROBLEMS]
    m = [jnp.maximum(jnp.max(si, axis=0, keepdims=True), ki) for si, ki in zip(s, sink)]
    e = [jnp.exp(si - mi) for si, mi in zip(s, m)]
    es = [jnp.exp(ki - mi) for ki, mi in zip(sink, m)]
    inv = [1.0 / (jnp.sum(ei, axis=0, keepdims=True) + esi) for ei, esi in zip(e, es)]
    return [ei * ii for ei, ii in zip(e, inv)], [esi * ii for esi, ii in zip(es, inv)]


def _group_norm_all(outs):
    mu = [jnp.mean(o, axis=1, keepdims=True) for o in outs]
    xc = [o - m for o, m in zip(outs, mu)]
    var = [jnp.mean(c * c, axis=1, keepdims=True) for c in xc]
    rstd = [lax.rsqrt(v + GN_EPS) for v in var]
    return [c * r for c, r in zip(xc, rstd)], rstd


def _retention_operands(p_ref, rows, cos_b, sin_b, qdec, kdec):
    qr = _rotate(p_ref[rows, OFF_RQ:OFF_RQ + RET_QK_WIDTH], cos_b, sin_b)
    kr = _rotate(p_ref[rows, OFF_RK:OFF_RK + RET_QK_WIDTH], cos_b, sin_b) * RET_SCALE
    return qr, kr, qr * qdec, kr * kdec


def _tile(t, i):
    return t[:, i * LANE:(i + 1) * LANE]


def _weights_allgather(w_in_t_shard, w_out_shard):
    half_in = SHARD_IN // 2
    half_out = SHARD_OUT // 2
    assert half_in % BF16_ROWS == 0 and half_out % BF16_ROWS == 0

    def body(wi_ref, wo_ref, win_ref, wout_ref, blk_ref, blko_ref, send_sems, recv_sems):
        x, y, c = lax.axis_index("x"), lax.axis_index("y"), lax.axis_index("c")
        me, sibling = (x, y, c), (x, y, 1 - c)
        chips = [(1 - x, y), (x, 1 - y), (1 - x, 1 - y)]

        def rows_in(px, py, pc):
            return win_ref.at[pl.ds(pl.multiple_of((4 * px + 2 * py + pc) * half_in, BF16_ROWS), half_in), :]

        def rows_out(px, py, pc):
            return wout_ref.at[pl.ds(pl.multiple_of((4 * px + 2 * py + pc) * half_out, half_out), half_out), :]

        def copy(k, rows, block, to, src=None):
            return pltpu.make_async_remote_copy(
                src_ref=rows(*block) if src is None else src, dst_ref=rows(*block),
                send_sem=send_sems.at[k], recv_sem=recv_sems.at[k], device_id=to, device_id_type=MESH_ID)

        blk_ref[...] = wi_ref[pl.ds(pl.multiple_of(c * half_in, BF16_ROWS), half_in), :].astype(BF16)
        blko_ref[...] = wo_ref[pl.ds(pl.multiple_of(c * half_out, half_out), half_out), :].astype(BF16)
        rows_in(*me)[...] = blk_ref[...]
        rows_out(*me)[...] = blko_ref[...]

        sets = ((0, rows_in, blk_ref), (7, rows_out, blko_ref))
        first, passed = [], []
        for base, rows, src in sets:
            first.append(copy(base, rows, me, sibling, src=src))
            first += [copy(base + 1 + j, rows, me, (*chip, c), src=src) for j, chip in enumerate(chips)]
        for cp in first:
            cp.start()
        for base, rows, src in sets:
            for j, chip in enumerate(chips):
                copy(base + 1 + j, rows, (*chip, c), me).wait_recv()
                fwd = copy(base + 4 + j, rows, (*chip, c), sibling)
                fwd.start()
                passed.append(fwd)
        for base, rows, src in sets:
            copy(base, rows, sibling, me).wait_recv()
            for j, chip in enumerate(chips):
                copy(base + 4 + j, rows, (*chip, 1 - c), me).wait_recv()
        for cp in first + passed:
            cp.wait_send()

    vmem = pl.BlockSpec(memory_space=pltpu.VMEM)
    return _call(
        body, name="weights_allgather",
        out_shape=(jax.ShapeDtypeStruct((IN_WIDTH, D_MODEL), BF16), jax.ShapeDtypeStruct((MIX_WIDTH, D_MODEL), BF16)),
        in_specs=[vmem, vmem], out_specs=(vmem, vmem),
        scratch_shapes=[
            pltpu.VMEM((half_in, D_MODEL), BF16),
            pltpu.VMEM((half_out, D_MODEL), BF16),
            pltpu.SemaphoreType.DMA((14,)),
            pltpu.SemaphoreType.DMA((14,)),
        ],
        compiler_params=_params(32),
    )(w_in_t_shard, w_out_shard)


def _in_proj(x, norm_g, w_in, tb):
    seq = x.shape[0]

    def body(x_ref, g_ref, w_ref, p_ref, ht_ref):
        xv = x_ref[...]
        r = lax.rsqrt(jnp.mean(xv * xv, axis=1, keepdims=True) + RMS_EPS)
        h = (xv * r) * g_ref[...]
        p_ref[...] = _dot_nt(h.astype(BF16), w_ref[...])
        ht_ref[...] = h.T.astype(BF16)

    return _call(
        body, name="in_proj", grid=(seq // tb,),
        out_shape=(jax.ShapeDtypeStruct((seq, IN_WIDTH), F32), jax.ShapeDtypeStruct((D_MODEL, seq), BF16)),
        in_specs=[pl.BlockSpec((tb, D_MODEL), lambda i: (i, 0)), _const_spec((1, D_MODEL)),
                  _const_spec((IN_WIDTH, D_MODEL))],
        out_specs=(pl.BlockSpec((tb, IN_WIDTH), lambda i: (i, 0)), pl.BlockSpec((D_MODEL, tb), lambda i: (0, i))),
        compiler_params=_params(56, ("arbitrary",)),
    )(x, norm_g, w_in)


def _mix_fwd(proj, x, target, w_out, final_g, gn_gain, sinks, tables, tb):
    seq = x.shape[0]
    nsub = tb // BLK
    cos_t, sin_t, decay_in, qdec_t, kdec_t, cd_t, bias_t = tables

    def body(p_ref, x_ref, t_ref, cos_ref, sin_ref, wout_ref, fg_ref, gain_ref, sinks_ref, din_ref, qdec_ref,
             kdec_ref, cd_ref, bias_ref, mix_ref, dxo_ref, st_ref, loss_ref, gfin_ref, oatt_ref,
             kprev_ref, vprev_ref, state_ref):
        i = pl.program_id(0)

        @pl.when(i == 0)
        def _():
            kprev_ref[...] = jnp.zeros_like(kprev_ref)
            vprev_ref[...] = jnp.zeros_like(vprev_ref)
            state_ref[...] = jnp.zeros_like(state_ref)
            loss_ref[...] = jnp.zeros_like(loss_ref)
            gfin_ref[...] = jnp.zeros_like(gfin_ref)

        def sub(j, carry):
            kp, vp, states = carry
            rows = pl.ds(pl.multiple_of(j * BLK, BLK), BLK)
            bias = bias_ref[jnp.where(jnp.logical_or(i > 0, j > 0), 1, 0)]

            aq = p_ref[rows, OFF_AQ:OFF_AQ + ATT_WIDTH]
            ak = p_ref[rows, OFF_AK:OFF_AK + ATT_KV_WIDTH]
            av = p_ref[rows, OFF_AV:OFF_AV + ATT_KV_WIDTH]
            az = p_ref[rows, OFF_AZ:OFF_AZ + ATT_WIDTH]
            k_ops = _kv_operands(jnp.concatenate([kp, ak], axis=0))
            v_ops = _kv_operands(jnp.concatenate([vp, av], axis=0))
            qs = [(_stack_tiles(aq, 2 * g) * ATT_SCALE).astype(BF16) for g in range(ATT_KV_HEADS)]
            p, _ = _attn_probs(qs, k_ops, bias, sinks_ref)
            o_tiles = []
            for g in range(ATT_KV_HEADS):
                p_cat = jnp.concatenate([p[2 * g].astype(BF16), p[2 * g + 1].astype(BF16)], axis=0)
                o = _dot_tn(p_cat, jnp.concatenate(v_ops[g], axis=0))
                o_tiles += [o[0:BLK], o[BLK:]]
            o_att = jnp.concatenate(o_tiles, axis=1)
            oatt_ref[rows, :] = o_att
            out = [o_att * (az * _sigmoid(az))]

            qr, kr, qd, kd = _retention_operands(p_ref, rows, cos_ref[rows, :], sin_ref[rows, :],
                                                 qdec_ref[...], kdec_ref[...])
            heads = [(t, hh) for t in range(PAIRS) for hh in range(2)]
            sc = [_dot_nt(jnp.concatenate(_split_heads(_tile(qr, t)), axis=0).astype(BF16), _tile(kr, t).astype(BF16))
                  * din_ref[t] for t in range(PAIRS)]
            qd_heads = [_split_heads(_tile(qd, t)) for t in range(PAIRS)]
            state_b = [states[t].astype(BF16) for t in range(PAIRS)]
            vs = [p_ref[rows, OFF_RV + h * RET_V_DIM:OFF_RV + (h + 1) * RET_V_DIM].astype(BF16) for h in range(RET_HEADS)]
            rzs = [p_ref[rows, OFF_RZ + h * RET_V_DIM:OFF_RZ + (h + 1) * RET_V_DIM] for h in range(RET_HEADS)]
            lhs = [jnp.concatenate([sc[t][hh * BLK:(hh + 1) * BLK].astype(BF16), qd_heads[t][hh].astype(BF16)], axis=1)
                   for t, hh in heads]
            ons, _ = _group_norm_all([_dot(lhs[2 * t + hh], jnp.concatenate([vs[2 * t + hh], state_b[t]], axis=0))
                                      for t, hh in heads])
            out += [(ons[h] * gain_ref[:, h * RET_V_DIM:(h + 1) * RET_V_DIM]) * (rzs[h] * _sigmoid(rzs[h]))
                    for h in range(RET_HEADS)]
            new_states = [states[t] * cd_ref[t]
                          + _dot_tn(jnp.concatenate(_split_heads(_tile(kd, t)), axis=0).astype(BF16),
                                    jnp.concatenate([vs[2 * t], vs[2 * t + 1]], axis=0)) for t in range(PAIRS)]
            mix_ref[rows, :] = jnp.concatenate(out, axis=1).astype(BF16)
            st_ref[j] = jnp.stack(states)
            return ak, av, tuple(new_states)

        carry = (kprev_ref[...], vprev_ref[...], tuple(state_ref[t] for t in range(PAIRS)))
        kp, vp, states = lax.fori_loop(0, nsub, sub, carry)
        kprev_ref[...] = kp
        vprev_ref[...] = vp
        state_ref[...] = jnp.stack(states)

        xo = x_ref[...] + _dot(mix_ref[...], wout_ref[...])
        r2 = lax.rsqrt(jnp.mean(xo * xo, axis=1, keepdims=True) + RMS_EPS)
        xn = xo * r2
        err = xn * fg_ref[...] - t_ref[...]
        loss_ref[...] += jnp.sum(err * err) * (0.5 / D_MODEL)
        dy = err * (1.0 / D_MODEL)
        gfin_ref[...] += jnp.sum(dy * xn, axis=0, keepdims=True)
        u = dy * fg_ref[...]
        dxo_ref[...] = r2 * u - xn * (r2 * jnp.mean(u * xn, axis=1, keepdims=True))

    blk_rows = lambda w: pl.BlockSpec((tb, w), lambda i: (i, 0))
    state_shape = (PAIRS, 2 * RET_QK_DIM, RET_V_DIM)
    return _call(
        body, name="mix_fwd", grid=(seq // tb,),
        out_shape=(
            jax.ShapeDtypeStruct((seq, MIX_WIDTH), BF16),
            jax.ShapeDtypeStruct((seq, D_MODEL), F32),
            jax.ShapeDtypeStruct((seq // BLK,) + state_shape, F32),
            jax.ShapeDtypeStruct((8, LANE), F32),
            jax.ShapeDtypeStruct((1, D_MODEL), F32),
            jax.ShapeDtypeStruct((seq, ATT_WIDTH), F32),
        ),
        in_specs=[
            blk_rows(IN_WIDTH), blk_rows(D_MODEL), blk_rows(D_MODEL), blk_rows(LANE), blk_rows(LANE),
            _const_spec((MIX_WIDTH, D_MODEL)), _const_spec((1, D_MODEL)), _const_spec((1, RET_WIDTH)),
            pl.BlockSpec(memory_space=pltpu.SMEM),
            _const_spec((PAIRS, 2 * BLK, BLK)), _const_spec((BLK, RET_QK_WIDTH)), _const_spec((BLK, RET_QK_WIDTH)),
            _const_spec(state_shape), _const_spec((2, 2 * BLK, 2 * BLK)),
        ],
        out_specs=(
            blk_rows(MIX_WIDTH), blk_rows(D_MODEL),
            pl.BlockSpec((nsub,) + state_shape, lambda i: (i, 0, 0, 0)),
            _const_spec((8, LANE)), _const_spec((1, D_MODEL)), blk_rows(ATT_WIDTH),
        ),
        scratch_shapes=[
            pltpu.VMEM((BLK, ATT_KV_WIDTH), F32), pltpu.VMEM((BLK, ATT_KV_WIDTH), F32),
            pltpu.VMEM(state_shape, F32),
        ],
        compiler_params=_params(48, ("arbitrary",)),
    )(proj, x, target, cos_t, sin_t, w_out, final_g, gn_gain, sinks, decay_in, qdec_t, kdec_t, cd_t, bias_t)


def _mix_bwd(proj, dxo, mix, o_att, states, w_out, gn_gain, sinks, tables, tb):
    seq = dxo.shape[0]
    nsub = tb // BLK
    nblk = seq // tb
    cos_t, sin_t, decay_in, qdec_t, kdec_t, cd_t, bias_t = tables
    kv_cols = OFF_AK // (2 * ATT_KV_WIDTH)
    state_shape = (PAIRS, 2 * RET_QK_DIM, RET_V_DIM)

    def body(p_ref, pkv_ref, dxo_ref, mix_ref, oatt_ref, st_ref, cos_ref, sin_ref, wout_ref, gain_ref, sinks_ref, din_ref,
             qdec_ref, kdec_ref, cd_ref, bias_ref, dp_ref, gwout_ref, dgain_ref, dsink_ref,
             dmix_ref, kv_ref, dkc_ref, dvc_ref, gst_ref):
        i = pl.program_id(0)
        blk = nblk - 1 - i

        @pl.when(i == 0)
        def _():
            gwout_ref[...] = jnp.zeros_like(gwout_ref)
            dgain_ref[...] = jnp.zeros_like(dgain_ref)
            dsink_ref[...] = jnp.zeros_like(dsink_ref)
            dkc_ref[...] = jnp.zeros_like(dkc_ref)
            dvc_ref[...] = jnp.zeros_like(dvc_ref)
            gst_ref[...] = jnp.zeros_like(gst_ref)

        dxo_b = dxo_ref[...].astype(BF16)
        dmix_ref[...] = _dot_nt(dxo_b, wout_ref[...])
        gwout_ref[...] += _dot_tn(mix_ref[...], dxo_b)
        kv_ref[0:BLK, :] = pkv_ref[...]
        kv_ref[BLK:, :] = p_ref[:, OFF_AK:OFF_AK + 2 * ATT_KV_WIDTH]
        low = _low_lanes((BLK, LANE))
        low2 = _low_lanes((2 * BLK, LANE))
        lane = lax.broadcasted_iota(jnp.int32, (1, LANE), 1)

        def sub(jj, carry):
            dkc, dvc, gsts, dgain, dsink = carry
            j = nsub - 1 - jj
            rows = pl.ds(pl.multiple_of(j * BLK, BLK), BLK)
            both = pl.ds(pl.multiple_of(j * BLK, BLK), 2 * BLK)
            bias = bias_ref[jnp.where(jnp.logical_or(blk > 0, j > 0), 1, 0)]

            aq = p_ref[rows, OFF_AQ:OFF_AQ + ATT_WIDTH]
            az = p_ref[rows, OFF_AZ:OFF_AZ + ATT_WIDTH]
            k_ops = _kv_operands(kv_ref[both, 0:ATT_KV_WIDTH])
            v_ops = _kv_operands(kv_ref[both, ATT_KV_WIDTH:2 * ATT_KV_WIDTH])
            da = dmix_ref[rows, 0:ATT_WIDTH]
            sig = _sigmoid(az)
            d_o = da * (az * sig)
            qs = [(_stack_tiles(aq, 2 * g) * ATT_SCALE).astype(BF16) for g in range(ATT_KV_HEADS)]
            dos = [_stack_tiles(d_o, 2 * g).astype(BF16) for g in range(ATT_KV_HEADS)]
            p, p_sink = _attn_probs(qs, k_ops, bias, sinks_ref)
            dpr = [_dot_nt(v_ops[g][hi], dos[g]) for g, hi in ATT_PROBLEMS]
            delta = [jnp.sum(pi * di, axis=0, keepdims=True) for pi, di in zip(p, dpr)]
            ds = [(pi * (di - ti)).astype(BF16) for pi, di, ti in zip(p, dpr, delta)]
            for (g, hi), ki, ti in zip(ATT_PROBLEMS, p_sink, delta):
                sink_part = ki * ti
                for half in range(2):
                    tot = jnp.sum(sink_part[:, half * BLK:(half + 1) * BLK], axis=1, keepdims=True)
                    dsink = dsink - jnp.where(lane == 4 * g + 2 * half + hi, tot, 0.0)
            dq_tiles, dk_sums, dv_sums = [], [], []
            for g in range(ATT_KV_HEADS):
                ds_cat = jnp.concatenate([ds[2 * g], ds[2 * g + 1]], axis=0)
                p_cat = jnp.concatenate([p[2 * g].astype(BF16), p[2 * g + 1].astype(BF16)], axis=0)
                dqs = _dot_tn(ds_cat, jnp.concatenate(k_ops[g], axis=0)) * ATT_SCALE
                dq_tiles += [dqs[0:BLK], dqs[BLK:]]
                dk_sums.append(_dot(ds_cat, qs[g]))
                dv_sums.append(_dot(p_cat, dos[g]))
            daz = da * oatt_ref[rows, :] * (sig * (1.0 + az * (1.0 - sig)))

            def kv_grad(sums):
                (a0, b0), (a1, b1) = [(s[0:2 * BLK], s[2 * BLK:]) for s in sums]
                return jnp.where(low2, a0, b1) + pltpu.roll(jnp.where(low2, a1, b0), HALF_LANE, 1)

            dk_both, dv_both = kv_grad(dk_sums), kv_grad(dv_sums)
            dak = dk_both[BLK:] + dkc
            dav = dv_both[BLK:] + dvc

            cos_b, sin_b = cos_ref[rows, :], sin_ref[rows, :]
            qdec, kdec = qdec_ref[...], kdec_ref[...]
            qr, kr, qd, kd = _retention_operands(p_ref, rows, cos_b, sin_b, qdec, kdec)
            heads = [(t, hh) for t in range(PAIRS) for hh in range(2)]
            head_cols = [slice(h * RET_V_DIM, (h + 1) * RET_V_DIM) for h in range(RET_HEADS)]
            q_rows = [jnp.concatenate(_split_heads(_tile(qr, t)), axis=0).astype(BF16) for t in range(PAIRS)]
            k_rows = [jnp.concatenate(_split_heads(_tile(kr, t)), axis=0).astype(BF16) for t in range(PAIRS)]
            din = [din_ref[t] for t in range(PAIRS)]
            sc = [(_dot_nt(q_rows[t], _tile(kr, t).astype(BF16)) * din[t]).astype(BF16) for t in range(PAIRS)]
            qd_heads = [_split_heads(_tile(qd, t)) for t in range(PAIRS)]
            kd_heads = [_split_heads(_tile(kd, t)) for t in range(PAIRS)]
            state_b = [st_ref[j, t].astype(BF16) for t in range(PAIRS)]
            gst_b = [gsts[t].astype(BF16) for t in range(PAIRS)]
            vs = [p_ref[rows, OFF_RV + h * RET_V_DIM:OFF_RV + (h + 1) * RET_V_DIM].astype(BF16) for h in range(RET_HEADS)]
            rzs = [p_ref[rows, OFF_RZ + h * RET_V_DIM:OFF_RZ + (h + 1) * RET_V_DIM] for h in range(RET_HEADS)]
            drs = [dmix_ref[rows, ATT_WIDTH + h * RET_V_DIM:ATT_WIDTH + (h + 1) * RET_V_DIM] for h in range(RET_HEADS)]
            gains = [gain_ref[:, c] for c in head_cols]
            lhs = [jnp.concatenate([sc[t][hh * BLK:(hh + 1) * BLK], qd_heads[t][hh].astype(BF16)], axis=1) for t, hh in heads]
            rhs = [jnp.concatenate([vs[2 * t + hh], state_b[t]], axis=0) for t, hh in heads]
            ons, rstds = _group_norm_all([_dot(l, r) for l, r in zip(lhs, rhs)])
            sig_r = [_sigmoid(z) for z in rzs]
            dgn = [d * (z * g) for d, z, g in zip(drs, rzs, sig_r)]
            dz_parts = [d * (o * gn) * (g * (1.0 + z * (1.0 - g))) for d, o, gn, g, z in zip(drs, ons, gains, sig_r, rzs)]
            dgain_parts = [jnp.sum(d * o, axis=0, keepdims=True) for d, o in zip(dgn, ons)]
            don = [d * gn for d, gn in zip(dgn, gains)]
            mean_don = [jnp.mean(d, axis=1, keepdims=True) for d in don]
            mean_don_on = [jnp.mean(d * o, axis=1, keepdims=True) for d, o in zip(don, ons)]
            dob = [(r * (d - a - o * b)).astype(BF16) for r, d, a, o, b in zip(rstds, don, mean_don, ons, mean_don_on)]
            dlhs = [_dot_nt(d, r) for d, r in zip(dob, rhs)]
            drhs = [_dot_tn(l, d) for l, d in zip(lhs, dob)]
            dkds = [_dot_nt(vs[2 * t + hh], gst_b[t]) for t, hh in heads]
            dv_parts = [drhs[2 * t + hh][0:BLK] + _dot(kd_heads[t][hh].astype(BF16), gst_b[t]) for t, hh in heads]
            das = [(dlhs[2 * t + hh][:, 0:BLK] * din[t][hh * BLK:(hh + 1) * BLK]).astype(BF16) for t, hh in heads]
            new_gsts = [gsts[t] * cd_ref[t] + drhs[2 * t][BLK:] + drhs[2 * t + 1][BLK:] for t in range(PAIRS)]
            dq_parts = [_dot(jnp.concatenate([das[2 * t], das[2 * t + 1]], axis=1), k_rows[t])
                        + jnp.where(low, dlhs[2 * t][:, BLK:], dlhs[2 * t + 1][:, BLK:]) * _tile(qdec, t)
                        for t in range(PAIRS)]
            dk_parts = [_dot_tn(jnp.concatenate([das[2 * t], das[2 * t + 1]], axis=0), q_rows[t])
                        + jnp.where(low, dkds[2 * t], dkds[2 * t + 1]) * _tile(kdec, t) for t in range(PAIRS)]
            drq = _rotate_transposed(jnp.concatenate(dq_parts, axis=1), cos_b, sin_b)
            drk = _rotate_transposed(jnp.concatenate(dk_parts, axis=1) * RET_SCALE, cos_b, sin_b)

            dp_ref[rows, :] = jnp.concatenate(
                [jnp.concatenate(dq_tiles, axis=1), dak, dav, daz, drq, drk] + dv_parts + dz_parts, axis=1).astype(BF16)
            dgain = dgain + jnp.concatenate(dgain_parts, axis=1)
            return dk_both[0:BLK], dv_both[0:BLK], tuple(new_gsts), dgain, dsink

        carry = (dkc_ref[...], dvc_ref[...], tuple(gst_ref[t] for t in range(PAIRS)), dgain_ref[...], dsink_ref[...])
        dkc, dvc, gsts, dgain, dsink = lax.fori_loop(0, nsub, sub, carry)
        dkc_ref[...] = dkc
        dvc_ref[...] = dvc
        gst_ref[...] = jnp.stack(gsts)
        dgain_ref[...] = dgain
        dsink_ref[...] = dsink

    rev_rows = lambda w: pl.BlockSpec((tb, w), lambda i: (nblk - 1 - i, 0))
    prev_kv = pl.BlockSpec((BLK, 2 * ATT_KV_WIDTH), lambda i: (jnp.maximum((nblk - 1 - i) * nsub - 1, 0), kv_cols))
    return _call(
        body, name="mix_bwd", grid=(nblk,),
        out_shape=(
            jax.ShapeDtypeStruct((seq, IN_WIDTH), BF16),
            jax.ShapeDtypeStruct((MIX_WIDTH, D_MODEL), F32),
            jax.ShapeDtypeStruct((1, RET_WIDTH), F32),
            jax.ShapeDtypeStruct((1, LANE), F32),
        ),
        in_specs=[
            rev_rows(IN_WIDTH), prev_kv, rev_rows(D_MODEL), rev_rows(MIX_WIDTH), rev_rows(ATT_WIDTH),
            pl.BlockSpec((nsub,) + state_shape, lambda i: (nblk - 1 - i, 0, 0, 0)),
            rev_rows(LANE), rev_rows(LANE),
            _const_spec((MIX_WIDTH, D_MODEL)), _const_spec((1, RET_WIDTH)),
            pl.BlockSpec(memory_space=pltpu.SMEM),
            _const_spec((PAIRS, 2 * BLK, BLK)), _const_spec((BLK, RET_QK_WIDTH)), _const_spec((BLK, RET_QK_WIDTH)),
            _const_spec(state_shape), _const_spec((2, 2 * BLK, 2 * BLK)),
        ],
        out_specs=(
            rev_rows(IN_WIDTH), _const_spec((MIX_WIDTH, D_MODEL)), _const_spec((1, RET_WIDTH)), _const_spec((1, LANE)),
        ),
        scratch_shapes=[
            pltpu.VMEM((tb, MIX_WIDTH), F32),
            pltpu.VMEM((tb + BLK, 2 * ATT_KV_WIDTH), F32),
            pltpu.VMEM((BLK, ATT_KV_WIDTH), F32), pltpu.VMEM((BLK, ATT_KV_WIDTH), F32),
            pltpu.VMEM(state_shape, F32),
        ],
        compiler_params=_params(56, ("arbitrary",)),
    )(proj, proj, dxo, mix, o_att, states, cos_t, sin_t, w_out, gn_gain, sinks, decay_in, qdec_t, kdec_t, cd_t, bias_t)


def _in_proj_bwd(dproj, x, dxo, norm_g, w_in, tb):
    seq = x.shape[0]

    def body(dp_ref, x_ref, dxo_ref, g_ref, w_ref, gx_ref, gnorm_ref):
        i = pl.program_id(0)

        @pl.when(i == 0)
        def _():
            gnorm_ref[...] = jnp.zeros_like(gnorm_ref)

        xv = x_ref[...]
        r = lax.rsqrt(jnp.mean(xv * xv, axis=1, keepdims=True) + RMS_EPS)
        xn = xv * r
        dh = _dot(dp_ref[...], w_ref[...])
        gnorm_ref[...] += jnp.sum(dh * xn, axis=0, keepdims=True)
        u = dh * g_ref[...]
        gx_ref[...] = dxo_ref[...] + r * u - xn * (r * jnp.mean(u * xn, axis=1, keepdims=True))

    rows = lambda w: pl.BlockSpec((tb, w), lambda i: (i, 0))
    return _call(
        body, name="in_proj_bwd", grid=(seq // tb,),
        out_shape=(jax.ShapeDtypeStruct((seq, D_MODEL), F32), jax.ShapeDtypeStruct((1, D_MODEL), F32)),
        in_specs=[rows(IN_WIDTH), rows(D_MODEL), rows(D_MODEL), _const_spec((1, D_MODEL)),
                  _const_spec((IN_WIDTH, D_MODEL))],
        out_specs=(rows(D_MODEL), _const_spec((1, D_MODEL))),
        compiler_params=_params(48, ("arbitrary",)),
    )(dproj, x, dxo, norm_g, w_in)


def _gw_in_reduce(h_t, dproj, tb):
    seq = dproj.shape[0]
    nblk = seq // tb
    last = nblk - 1
    hand_on = min(2, last)
    half = D_MODEL // 2
    A, B, C, N_SEMS = 0, N_CHIPS, 2 * N_CHIPS, 2 * N_CHIPS + 1

    def body(win_ref, ht_ref, dp_ref, out_ref, acc, sib, send_buf, b_in, fin, send_sems, recv_sems):
        p, i = pl.program_id(0), pl.program_id(1)
        x, y, c = lax.axis_index("x"), lax.axis_index("y"), lax.axis_index("c")
        chip = 2 * x + y
        sibling = (x, y, 1 - c)
        mine = pl.ds(pl.multiple_of(c * half, half), half)
        other = pl.ds(pl.multiple_of((1 - c) * half, half), half)

        def remote(src, dst, send_k, recv_k, to):
            return pltpu.make_async_remote_copy(src_ref=src, dst_ref=dst, send_sem=send_sems.at[send_k],
                                                recv_sem=recv_sems.at[recv_k], device_id=to, device_id_type=MESH_ID)

        part = _dot(ht_ref[...], dp_ref[...])
        slot = p % 2

        @pl.when(i == 0)
        def _():
            acc[slot] = part

        @pl.when(i > 0)
        def _():
            acc[slot] += part

        for q in range(N_CHIPS):
            s = q % 2
            to_sibling = remote(acc.at[s, other, :], sib.at[s], A + q, A + q, sibling)

            @pl.when(jnp.logical_and(p == q, i == last))
            def _():
                to_sibling.start()

            if q < N_CHIPS - 1:
                dest = (chip + 1 + q) % N_CHIPS

                @pl.when(jnp.logical_and(p == q + 1, i == hand_on))
                def _():
                    to_sibling.wait_recv()
                    send_buf[q] = (acc[s, mine, :] + sib[s]).astype(BF16)
                    remote(send_buf.at[q], b_in.at[chip], B + q, B + chip, (dest // 2, dest % 2, c)).start()
                    to_sibling.wait_send()
            else:
                @pl.when(jnp.logical_and(p == q, i == last))
                def _():
                    to_sibling.wait_recv()
                    fin[mine, :] = acc[s, mine, :] + sib[s]
                    for j in range(N_CHIPS):
                        @pl.when(j != chip)
                        def _():
                            remote(b_in.at[j], b_in.at[j], B + j, B + j, sibling).wait_recv()
                            fin[mine, :] += b_in[j].astype(F32)
                    to_core = remote(fin.at[mine, :], fin.at[mine, :], C, C, sibling)
                    to_core.start()
                    remote(fin.at[other, :], fin.at[other, :], C, C, sibling).wait_recv()
                    out_ref[...] = fin[...]
                    to_core.wait_send()
                    to_sibling.wait_send()
                    for k in range(N_CHIPS - 1):
                        remote(send_buf.at[k], b_in.at[chip], B + k, B + k, sibling).wait_send()

    grid_spec = pltpu.PrefetchScalarGridSpec(
        num_scalar_prefetch=1, grid=(N_CHIPS, nblk),
        in_specs=[pl.BlockSpec((D_MODEL, tb), lambda p, i, win: (0, i)),
                  pl.BlockSpec((pl.Element(tb), pl.Element(SHARD_PAD)),
                               lambda p, i, win: (i * tb, pl.multiple_of(win[p] * LANE, LANE)))],
        out_specs=pl.BlockSpec((D_MODEL, SHARD_PAD), lambda p, i, win: (0, 0), pipeline_mode=pl.Buffered(1)),
        scratch_shapes=[
            pltpu.VMEM((2, D_MODEL, SHARD_PAD), F32), pltpu.VMEM((2, half, SHARD_PAD), F32),
            pltpu.VMEM((N_CHIPS - 1, half, SHARD_PAD), BF16), pltpu.VMEM((N_CHIPS, half, SHARD_PAD), BF16),
            pltpu.VMEM((D_MODEL, SHARD_PAD), F32),
            pltpu.SemaphoreType.DMA((N_SEMS,)), pltpu.SemaphoreType.DMA((N_SEMS,)),
        ])
    chip = 2 * lax.axis_index("x") + lax.axis_index("y")
    owner = (chip + 1 + jnp.arange(N_CHIPS, dtype=jnp.int32)) % N_CHIPS
    win_start = (owner * SHARD_IN) // LANE
    return _call(
        body, name="gw_in_reduce", grid_spec=grid_spec,
        out_shape=jax.ShapeDtypeStruct((D_MODEL, SHARD_PAD), F32),
        compiler_params=_params(40, ("arbitrary", "arbitrary")),
    )(win_start.astype(jnp.int32), h_t, dproj)


def _grads_reduce(gw_out, small):
    half_out = SHARD_OUT // 2
    A_OUT, B_OUT, C_OUT, PACK, N_SEMS = 0, 4, 8, 9, 17
    assert len(small) == PACK_PARTS

    def body(gwo_hbm, s0_ref, s1_ref, s2_ref, s3_ref, s4_ref, fout_ref, packsum_ref,
             mine_out, sib_out, send_out, b_out, pack_ref, packs, send_sems, recv_sems, local_sems):
        x, y, c = lax.axis_index("x"), lax.axis_index("y"), lax.axis_index("c")
        chip = 2 * x + y
        dev = 2 * chip + c
        sibling = (x, y, 1 - c)
        pack_ref[...] = jnp.zeros_like(pack_ref)
        for k, s_ref in enumerate((s0_ref, s1_ref, s2_ref, s3_ref, s4_ref)):
            pack_ref[8 * k:8 * k + 1, 0:s_ref.shape[1]] = s_ref[0:1, :]

        def remote(src, dst, send_k, recv_k, to):
            return pltpu.make_async_remote_copy(src_ref=src, dst_ref=dst, send_sem=send_sems.at[send_k],
                                                recv_sem=recv_sems.at[recv_k], device_id=to, device_id_type=MESH_ID)

        def out_rows(j, core):
            return pl.ds(pl.multiple_of(j * SHARD_OUT + core * half_out, half_out), half_out)

        packs[dev] = pack_ref[...]
        for d in range(N_DEV):
            to = (d // 4, (d // 2) % 2, d % 2)

            @pl.when(d != dev)
            def _():
                remote(pack_ref, packs.at[dev], PACK + d, PACK + dev, to).start()

        local = [pltpu.make_async_copy(gwo_hbm.at[out_rows(j, c), :], mine_out.at[j], local_sems.at[j])
                 for j in range(N_CHIPS)]
        for cp in local:
            cp.start()
        stage_a = [remote(gwo_hbm.at[out_rows(j, 1 - c), :], sib_out.at[j], A_OUT + j, A_OUT + j, sibling)
                   for j in range(N_CHIPS)]
        for cp in stage_a:
            cp.start()
        for cp in local:
            cp.wait()
        for cp in stage_a:
            cp.wait_recv()
        for j in range(N_CHIPS):
            mine_out[j] = mine_out[j] + sib_out[j]

        my_out_rows = pl.ds(pl.multiple_of(c * half_out, half_out), half_out)
        for j in range(N_CHIPS):
            to = (j // 2, j % 2, c)

            @pl.when(j != chip)
            def _():
                send_out[j] = mine_out[j].astype(BF16)
                remote(send_out.at[j], b_out.at[chip], B_OUT + j, B_OUT + chip, to).start()

            @pl.when(j == chip)
            def _():
                fout_ref[my_out_rows, :] = mine_out[j]

        for j in range(N_CHIPS):
            @pl.when(j != chip)
            def _():
                remote(b_out.at[j], b_out.at[j], B_OUT + j, B_OUT + j, sibling).wait_recv()
                fout_ref[my_out_rows, :] += b_out[j].astype(F32)

        mine_half_out = fout_ref.at[my_out_rows, :]
        stage_c = [remote(mine_half_out, mine_half_out, C_OUT, C_OUT, sibling)]
        for cp in stage_c:
            cp.start()
        other_half_out = fout_ref.at[pl.ds(pl.multiple_of((1 - c) * half_out, half_out), half_out), :]
        remote(other_half_out, other_half_out, C_OUT, C_OUT, sibling).wait_recv()

        for d in range(N_DEV):
            @pl.when(d != dev)
            def _():
                remote(pack_ref, packs.at[d], PACK + d, PACK + d, sibling).wait_recv()
        total = packs[0]
        for d in range(1, N_DEV):
            total = total + packs[d]
        packsum_ref[...] = total

        for cp in stage_a + stage_c:
            cp.wait_send()
        for j in range(N_CHIPS):
            @pl.when(j != chip)
            def _():
                remote(b_out.at[j], b_out.at[j], B_OUT + j, B_OUT + j, sibling).wait_send()
        for d in range(N_DEV):
            @pl.when(d != dev)
            def _():
                remote(pack_ref, packs.at[d], PACK + d, PACK + d, sibling).wait_send()

    vmem = pl.BlockSpec(memory_space=pltpu.VMEM)
    hbm = pl.BlockSpec(memory_space=pl.ANY)
    return _call(
        body, name="grads_reduce",
        out_shape=(jax.ShapeDtypeStruct((SHARD_OUT, D_MODEL), F32), jax.ShapeDtypeStruct((PACK_ROWS, D_MODEL), F32)),
        in_specs=[hbm] + [vmem] * PACK_PARTS, out_specs=(vmem, vmem),
        scratch_shapes=[
            pltpu.VMEM((N_CHIPS, half_out, D_MODEL), F32), pltpu.VMEM((N_CHIPS, half_out, D_MODEL), F32),
            pltpu.VMEM((N_CHIPS, half_out, D_MODEL), BF16), pltpu.VMEM((N_CHIPS, half_out, D_MODEL), BF16),
            pltpu.VMEM((PACK_ROWS, D_MODEL), F32), pltpu.VMEM((N_DEV, PACK_ROWS, D_MODEL), F32),
            pltpu.SemaphoreType.DMA((N_SEMS,)), pltpu.SemaphoreType.DMA((N_SEMS,)),
            pltpu.SemaphoreType.DMA((N_CHIPS,)),
        ],
        compiler_params=_params(24),
    )(gw_out, *small)


def _adam_math(w, g, m, v):
    mn = ADAM_B1 * m + (1.0 - ADAM_B1) * g
    vn = ADAM_B2 * v + (1.0 - ADAM_B2) * (g * g)
    m_hat = mn / (1.0 - ADAM_B1 ** ADAM_STEP)
    v_hat = vn / (1.0 - ADAM_B2 ** ADAM_STEP)
    return -ADAM_LR * (m_hat / (jnp.sqrt(v_hat) + ADAM_EPS) + ADAM_WD * w), mn, vn


def _adamw(name, w, g, m, v, tb):
    rows, cols = w.shape

    def body(w_ref, g_ref, m_ref, v_ref, go_ref, d_ref, mo_ref, vo_ref):
        gv = g_ref[...]
        go_ref[...] = gv
        d_ref[...], mo_ref[...], vo_ref[...] = _adam_math(w_ref[...], gv, m_ref[...], v_ref[...])

    spec = pl.BlockSpec((tb, cols), lambda i: (i, 0))
    shape = jax.ShapeDtypeStruct((rows, cols), F32)
    return _call(
        body, name=name, grid=(rows // tb,), out_shape=(shape,) * 4,
        in_specs=[spec] * 4, out_specs=(spec,) * 4,
        compiler_params=_params(32, ("arbitrary",)),
    )(w, g, m, v)


def _adamw_w_in(w_t, g_window, m_t, v_t, tb):
    def body(w_ref, g_ref, m_ref, v_ref, go_ref, d_ref, mo_ref, vo_ref, gt_ref):
        gt_ref[...] = g_ref[...].T
        gv = gt_ref[pl.ds(pl.multiple_of(lax.axis_index("y") * SHARD_SHIFT, SHARD_SHIFT), SHARD_IN), :]
        go_ref[...] = gv
        d_ref[...], mo_ref[...], vo_ref[...] = _adam_math(w_ref[...], gv, m_ref[...], v_ref[...])

    spec = pl.BlockSpec((SHARD_IN, tb), lambda i: (0, i))
    shape = jax.ShapeDtypeStruct((SHARD_IN, D_MODEL), F32)
    return _call(
        body, name="adamw_w_in", grid=(D_MODEL // tb,), out_shape=(shape,) * 4,
        in_specs=[spec, pl.BlockSpec((tb, SHARD_PAD), lambda i: (i, 0)), spec, spec], out_specs=(spec,) * 4,
        scratch_shapes=[pltpu.VMEM((SHARD_PAD, tb), F32)],
        compiler_params=_params(32, ("arbitrary",)),
    )(w_t, g_window, m_t, v_t)


def _adamw_small(sums, params):
    def body(sums_ref, *refs):
        ins, outs = refs[:3 * len(params)], refs[3 * len(params):]
        for k in range(len(params)):
            w_ref, m_ref, v_ref = ins[3 * k:3 * k + 3]
            g = sums_ref[8 * k:8 * k + 1, 0:w_ref.shape[1]]
            go_ref, d_ref, mo_ref, vo_ref = outs[4 * k:4 * k + 4]
            go_ref[...] = g
            d_ref[...], mo_ref[...], vo_ref[...] = _adam_math(w_ref[...], g, m_ref[...], v_ref[...])

    vmem = pl.BlockSpec(memory_space=pltpu.VMEM)
    flat = [a for p in params for a in p]
    shapes = tuple(jax.ShapeDtypeStruct(p[0].shape, F32) for p in params for _ in range(4))
    res = _call(body, name="adamw_small", out_shape=shapes, in_specs=[vmem] * (1 + len(flat)),
                out_specs=(vmem,) * len(shapes), compiler_params=_params(16))(sums, *flat)
    return [res[4 * k:4 * k + 4] for k in range(len(params))]


def kernel(x, norm_g, w_in, att_sinks, ret_gn_g, w_out, final_g, loss_target, m_norm_g, m_w_in, m_att_sinks, m_ret_gn_g, m_w_out, m_final_g, v_norm_g, v_w_in, v_att_sinks, v_ret_gn_g, v_w_out, v_final_g):
    seq = x.shape[1]
    xs, tgt = x[0], loss_target[0]
    final_g2 = final_g.reshape(1, D_MODEL)
    tables = _tables(seq)

    w_in_t, m_w_in_t, v_w_in_t = w_in[0].T, m_w_in[0].T, v_w_in[0].T
    w_in_full, w_out_full = _weights_allgather(w_in_t, w_out[0])
    proj, h_t = _in_proj(xs, norm_g, w_in_full, min(TOKENS_PROJ, seq))
    mix, dxo, states, loss_part, gfin, o_att = _mix_fwd(proj, xs, tgt, w_out_full, final_g2, ret_gn_g, att_sinks,
                                                        tables, min(TOKENS_MIX, seq))
    dproj, gw_out, dgain, dsink = _mix_bwd(proj, dxo, mix, o_att, states, w_out_full, ret_gn_g, att_sinks, tables,
                                           min(TOKENS_MIX, seq))
    g_in = _gw_in_reduce(h_t, dproj, min(TOKENS_GW, seq))
    grad_x, gnorm = _in_proj_bwd(dproj, xs, dxo, norm_g, w_in_full, min(TOKENS_PROJ_BWD, seq))

    g_out, sums = _grads_reduce(gw_out, (gnorm, gfin, dgain, dsink, loss_part))

    res_in = [r.T for r in _adamw_w_in(w_in_t, g_in, m_w_in_t, v_w_in_t, 256)]
    res_out = _adamw("adamw_w_out", w_out[0], g_out, m_w_out[0], v_w_out[0], SHARD_OUT)
    as_row = lambda a: a.reshape(1, D_MODEL)
    r_norm, r_final, r_gain, r_sink = _adamw_small(sums, [
        (norm_g, m_norm_g, v_norm_g), (final_g2, as_row(m_final_g), as_row(v_final_g)),
        (ret_gn_g, m_ret_gn_g, v_ret_gn_g), (att_sinks, m_att_sinks, v_att_sinks)])

    outs = []
    for k in range(4):
        outs += [r_norm[k], res_in[k][None], r_sink[k], r_gain[k], res_out[k][None], r_final[k].reshape(D_MODEL)]
    return (sums[4 * 8, 0], grad_x[None], *outs)
```

```python
import jax
import jax.numpy as jnp
import numpy as np
from jax import lax
from jax.experimental import pallas as pl
from jax.experimental.pallas import tpu as pltpu

F32 = jnp.float32
BF16 = jnp.bfloat16

D_MODEL = 1024
ATT_HEADS = 8
ATT_KV_HEADS = 2
ATT_HEAD_DIM = 64
RET_HEADS = 4
RET_QK_DIM = 64
RET_V_DIM = 128
BLK = 128
ROT_BASE = 10000.0
RMS_EPS = 1e-6
GN_EPS = 1e-6
NEG_INF = -1e30
ATT_SCALE = ATT_HEAD_DIM ** -0.5
RET_SCALE = RET_QK_DIM ** -0.5

ATT_WIDTH = ATT_HEADS * ATT_HEAD_DIM
ATT_KV_WIDTH = ATT_KV_HEADS * ATT_HEAD_DIM
RET_QK_WIDTH = RET_HEADS * RET_QK_DIM
RET_WIDTH = RET_HEADS * RET_V_DIM
MIX_WIDTH = ATT_WIDTH + RET_WIDTH
OFF_AQ = 0
OFF_AK = OFF_AQ + ATT_WIDTH
OFF_AV = OFF_AK + ATT_KV_WIDTH
OFF_AZ = OFF_AV + ATT_KV_WIDTH
OFF_RQ = OFF_AZ + ATT_WIDTH
OFF_RK = OFF_RQ + RET_QK_WIDTH
OFF_RV = OFF_RK + RET_QK_WIDTH
OFF_RZ = OFF_RV + RET_WIDTH
IN_WIDTH = OFF_RZ + RET_WIDTH

LANE = 128
BF16_ROWS = 16
HALF_LANE = LANE // 2
PAIRS = RET_QK_WIDTH // LANE
assert ATT_HEAD_DIM == HALF_LANE and RET_QK_DIM == HALF_LANE and RET_V_DIM == LANE and ATT_KV_WIDTH == LANE

N_CHIPS = 4
N_DEV = 8
SHARD_IN = IN_WIDTH // N_CHIPS
SHARD_PAD = 768
SHARD_SHIFT = SHARD_PAD - SHARD_IN
WIN_START = tuple((j * SHARD_IN) // LANE * LANE for j in range(N_CHIPS))
SHARD_OUT = MIX_WIDTH // N_CHIPS

PACK_PARTS = 5
PACK_ROWS = 8
assert PACK_PARTS <= PACK_ROWS

ADAM_LR = 0.001
ADAM_B1 = 0.9
ADAM_B2 = 0.999
ADAM_EPS = 1e-08
ADAM_WD = 0.01
ADAM_STEP = 10

VMEM_CAP = 64 * 1024 * 1024
TOKENS_PROJ = 1024
TOKENS_MIX = 512
TOKENS_PROJ_BWD = 512
TOKENS_GW = 1024
MESH_ID = pl.DeviceIdType.MESH


def _call(body, **kw):
    return pl.pallas_call(body, **kw)


def _params(vmem_mb, semantics=None):
    assert vmem_mb * 1024 * 1024 < VMEM_CAP
    return pltpu.CompilerParams(dimension_semantics=semantics, vmem_limit_bytes=vmem_mb * 1024 * 1024)


def _dot(a, b):
    return jnp.dot(a, b, preferred_element_type=F32)


def _dot_nt(a, b):
    return lax.dot_general(a, b, (((1,), (1,)), ((), ())), preferred_element_type=F32)


def _dot_tn(a, b):
    return lax.dot_general(a, b, (((0,), (0,)), ((), ())), preferred_element_type=F32)


def _sigmoid(z):
    return 1.0 / (1.0 + jnp.exp(-z))


def _const_spec(shape):
    nd = len(shape)
    return pl.BlockSpec(shape, lambda i: (0,) * nd, pipeline_mode=pl.Buffered(1))


def _tables(seq):
    f32 = np.float32
    pos = np.arange(seq, dtype=f32)
    theta = (f32(1.0) / (f32(ROT_BASE) ** np.linspace(0.0, 1.0, RET_QK_DIM // 2, dtype=f32))).astype(f32)
    ang = (pos[:, None] * theta[None, :]).astype(f32)
    cos, sin = np.cos(ang), np.sin(ang)
    cos2 = np.repeat(cos, 2, axis=1)
    sin2 = np.stack([-sin, sin], axis=-1).reshape(seq, RET_QK_DIM)
    cos_t = np.tile(cos2, (1, 2))
    sin_t = np.tile(sin2, (1, 2))

    log_gamma = np.log(f32(1.0) - f32(2.0) ** (f32(-5.0) - np.arange(RET_HEADS, dtype=f32))).astype(f32)
    idx = np.arange(BLK, dtype=f32)
    rel = idx[:, None] - idx[None, :]
    decay_in = np.where(rel >= 0, np.exp(log_gamma[:, None, None] * np.maximum(rel, f32(0.0))), f32(0.0))
    k_dec = np.exp(log_gamma[:, None] * (BLK - 1 - idx)[None, :])
    q_dec = np.exp(log_gamma[:, None] * (idx + 1)[None, :])
    chunk_decay = np.exp(log_gamma * f32(BLK))
    kdec_t = np.repeat(k_dec.T, RET_QK_DIM, axis=1)
    qdec_t = np.repeat(q_dec.T, RET_QK_DIM, axis=1)
    cd_t = np.broadcast_to(chunk_decay[:, None, None], (RET_HEADS, RET_QK_DIM, RET_V_DIM))
    decay_in = decay_in.reshape(PAIRS, 2 * BLK, BLK)
    cd_t = cd_t.reshape(PAIRS, 2 * RET_QK_DIM, RET_V_DIM)

    key = np.arange(2 * BLK)[:, None]
    query = np.arange(2 * BLK)[None, :] % BLK
    diff = query + BLK - key
    valid = (diff >= 0) & (diff < BLK)
    bias = np.stack([np.where(valid & (key >= BLK), 0.0, NEG_INF), np.where(valid, 0.0, NEG_INF)])
    return tuple(jnp.asarray(np.ascontiguousarray(a), F32) for a in (cos_t, sin_t, decay_in, qdec_t, kdec_t, cd_t, bias))


def _low_lanes(shape):
    lane = lax.broadcasted_iota(jnp.int32, shape, len(shape) - 1)
    return (lane & HALF_LANE) == 0


def _split_heads(t):
    low = _low_lanes(t.shape)
    zero = jnp.zeros_like(t)
    return jnp.where(low, t, zero), jnp.where(low, zero, t)


def _swap_pairs(t):
    lane = lax.broadcasted_iota(jnp.int32, t.shape, 1)
    nxt = pltpu.roll(t, t.shape[1] - 1, 1)
    prv = pltpu.roll(t, 1, 1)
    return jnp.where((lane & 1) == 0, nxt, prv)


def _per_tile(fn, t):
    return jnp.concatenate([fn(_tile(t, i)) for i in range(t.shape[1] // LANE)], axis=1)


def _rotate(t, cos_t, sin_t):
    return _per_tile(lambda a: a * cos_t + _swap_pairs(a) * sin_t, t)


def _rotate_transposed(d, cos_t, sin_t):
    return _per_tile(lambda a: a * cos_t + _swap_pairs(a * sin_t), d)


def _kv_operands(cat):
    low = _low_lanes(cat.shape)
    swapped = pltpu.roll(cat, HALF_LANE, 1)
    zero = jnp.zeros_like(cat)
    pick = lambda a, b: jnp.where(low, a, b).astype(BF16)
    return ((pick(cat, zero), pick(zero, swapped)), (pick(swapped, zero), pick(zero, cat)))


def _stack_tiles(t, first_tile):
    a = t[:, first_tile * LANE:(first_tile + 1) * LANE]
    b = t[:, (first_tile + 1) * LANE:(first_tile + 2) * LANE]
    return jnp.concatenate([a, b], axis=0)


def _sink_rows(sinks_ref, group):
    first = lax.broadcasted_iota(jnp.int32, (1, 2 * BLK), 1) < BLK

    def row(h0, h1):
        return jnp.where(first, sinks_ref[0, group * 4 + h0], sinks_ref[0, group * 4 + h1])
    return row(0, 2), row(1, 3)


ATT_PROBLEMS = tuple((g, hi) for g in range(ATT_KV_HEADS) for hi in range(2))


def _attn_probs(qs, k_ops, bias, sinks_ref):
    sink = [_sink_rows(sinks_ref, g)[hi] for g, hi in ATT_PROBLEMS]
    s = [_dot_nt(k_ops[g][hi], qs[g]) + bias for g, hi in ATT_PROBLEMS]
    m = [jnp.maximum(jnp.max(si, axis=0, keepdims=True), ki) for si, ki in zip(s, sink)]
    e = [jnp.exp(si - mi) for si, mi in zip(s, m)]
    es = [jnp.exp(ki - mi) for ki, mi in zip(sink, m)]
    inv = [1.0 / (jnp.sum(ei, axis=0, keepdims=True) + esi) for ei, esi in zip(e, es)]
    return [ei * ii for ei, ii in zip(e, inv)], [esi * ii for esi, ii in zip(es, inv)]


def _group_norm_all(outs):
    mu = [jnp.mean(o, axis=1, keepdims=True) for o in outs]
    xc = [o - m for o, m in zip(outs, mu)]
    var = [jnp.mean(c * c, axis=1, keepdims=True) for c in xc]
    rstd = [lax.rsqrt(v + GN_EPS) for v in var]
    return [c * r for c, r in zip(xc, rstd)], rstd


def _retention_operands(p_ref, rows, cos_b, sin_b, qdec, kdec):
    qr = _rotate(p_ref[rows, OFF_RQ:OFF_RQ + RET_QK_WIDTH], cos_b, sin_b)
    kr = _rotate(p_ref[rows, OFF_RK:OFF_RK + RET_QK_WIDTH], cos_b, sin_b) * RET_SCALE
    return qr, kr, qr * qdec, kr * kdec


def _tile(t, i):
    return t[:, i * LANE:(i + 1) * LANE]


def _weights_allgather(w_in_t_shard, w_out_shard):
    half_in = SHARD_IN // 2
    half_out = SHARD_OUT // 2
    assert half_in % BF16_ROWS == 0 and half_out % BF16_ROWS == 0

    def body(wi_ref, wo_ref, win_ref, wout_ref, blk_ref, blko_ref, send_sems, recv_sems):
        x, y, c = lax.axis_index("x"), lax.axis_index("y"), lax.axis_index("c")
        me, sibling = (x, y, c), (x, y, 1 - c)
        chips = [(1 - x, y), (x, 1 - y), (1 - x, 1 - y)]

        def rows_in(px, py, pc):
            return win_ref.at[pl.ds(pl.multiple_of((4 * px + 2 * py + pc) * half_in, BF16_ROWS), half_in), :]

        def rows_out(px, py, pc):
            return wout_ref.at[pl.ds(pl.multiple_of((4 * px + 2 * py + pc) * half_out, half_out), half_out), :]

        def copy(k, rows, block, to, src=None):
            return pltpu.make_async_remote_copy(
                src_ref=rows(*block) if src is None else src, dst_ref=rows(*block),
                send_sem=send_sems.at[k], recv_sem=recv_sems.at[k], device_id=to, device_id_type=MESH_ID)

        blk_ref[...] = wi_ref[pl.ds(pl.multiple_of(c * half_in, BF16_ROWS), half_in), :].astype(BF16)
        blko_ref[...] = wo_ref[pl.ds(pl.multiple_of(c * half_out, half_out), half_out), :].astype(BF16)
        rows_in(*me)[...] = blk_ref[...]
        rows_out(*me)[...] = blko_ref[...]

        sets = ((0, rows_in, blk_ref), (7, rows_out, blko_ref))
        first, passed = [], []
        for base, rows, src in sets:
            first.append(copy(base, rows, me, sibling, src=src))
            first += [copy(base + 1 + j, rows, me, (*chip, c), src=src) for j, chip in enumerate(chips)]
        for cp in first:
            cp.start()
        for base, rows, src in sets:
            for j, chip in enumerate(chips):
                copy(base + 1 + j, rows, (*chip, c), me).wait_recv()
                fwd = copy(base + 4 + j, rows, (*chip, c), sibling)
                fwd.start()
                passed.append(fwd)
        for base, rows, src in sets:
            copy(base, rows, sibling, me).wait_recv()
            for j, chip in enumerate(chips):
                copy(base + 4 + j, rows, (*chip, 1 - c), me).wait_recv()
        for cp in first + passed:
            cp.wait_send()

    vmem = pl.BlockSpec(memory_space=pltpu.VMEM)
    return _call(
        body, name="weights_allgather",
        out_shape=(jax.ShapeDtypeStruct((IN_WIDTH, D_MODEL), BF16), jax.ShapeDtypeStruct((MIX_WIDTH, D_MODEL), BF16)),
        in_specs=[vmem, vmem], out_specs=(vmem, vmem),
        scratch_shapes=[
            pltpu.VMEM((half_in, D_MODEL), BF16),
            pltpu.VMEM((half_out, D_MODEL), BF16),
            pltpu.SemaphoreType.DMA((14,)),
            pltpu.SemaphoreType.DMA((14,)),
        ],
        compiler_params=_params(32),
    )(w_in_t_shard, w_out_shard)


def _in_proj(x, norm_g, w_in, tb):
    seq = x.shape[0]

    def body(x_ref, g_ref, w_ref, p_ref, ht_ref):
        xv = x_ref[...]
        r = lax.rsqrt(jnp.mean(xv * xv, axis=1, keepdims=True) + RMS_EPS)
        h = (xv * r) * g_ref[...]
        p_ref[...] = _dot_nt(h.astype(BF16), w_ref[...])
        ht_ref[...] = h.T.astype(BF16)

    return _call(
        body, name="in_proj", grid=(seq // tb,),
        out_shape=(jax.ShapeDtypeStruct((seq, IN_WIDTH), F32), jax.ShapeDtypeStruct((D_MODEL, seq), BF16)),
        in_specs=[pl.BlockSpec((tb, D_MODEL), lambda i: (i, 0)), _const_spec((1, D_MODEL)),
                  _const_spec((IN_WIDTH, D_MODEL))],
        out_specs=(pl.BlockSpec((tb, IN_WIDTH), lambda i: (i, 0)), pl.BlockSpec((D_MODEL, tb), lambda i: (0, i))),
        compiler_params=_params(56, ("arbitrary",)),
    )(x, norm_g, w_in)


def _mix_fwd(proj, x, target, w_out, final_g, gn_gain, sinks, tables, tb):
    seq = x.shape[0]
    nsub = tb // BLK
    cos_t, sin_t, decay_in, qdec_t, kdec_t, cd_t, bias_t = tables

    def body(p_ref, x_ref, t_ref, cos_ref, sin_ref, wout_ref, fg_ref, gain_ref, sinks_ref, din_ref, qdec_ref,
             kdec_ref, cd_ref, bias_ref, mix_ref, dxo_ref, st_ref, loss_ref, gfin_ref, oatt_ref,
             kprev_ref, vprev_ref, state_ref):
        i = pl.program_id(0)

        @pl.when(i == 0)
        def _():
            kprev_ref[...] = jnp.zeros_like(kprev_ref)
            vprev_ref[...] = jnp.zeros_like(vprev_ref)
            state_ref[...] = jnp.zeros_like(state_ref)
            loss_ref[...] = jnp.zeros_like(loss_ref)
            gfin_ref[...] = jnp.zeros_like(gfin_ref)

        def sub(j, carry):
            kp, vp, states = carry
            rows = pl.ds(pl.multiple_of(j * BLK, BLK), BLK)
            bias = bias_ref[jnp.where(jnp.logical_or(i > 0, j > 0), 1, 0)]

            aq = p_ref[rows, OFF_AQ:OFF_AQ + ATT_WIDTH]
            ak = p_ref[rows, OFF_AK:OFF_AK + ATT_KV_WIDTH]
            av = p_ref[rows, OFF_AV:OFF_AV + ATT_KV_WIDTH]
            az = p_ref[rows, OFF_AZ:OFF_AZ + ATT_WIDTH]
            k_ops = _kv_operands(jnp.concatenate([kp, ak], axis=0))
            v_ops = _kv_operands(jnp.concatenate([vp, av], axis=0))
            qs = [(_stack_tiles(aq, 2 * g) * ATT_SCALE).astype(BF16) for g in range(ATT_KV_HEADS)]
            p, _ = _attn_probs(qs, k_ops, bias, sinks_ref)
            o_tiles = []
            for g in range(ATT_KV_HEADS):
                p_cat = jnp.concatenate([p[2 * g].astype(BF16), p[2 * g + 1].astype(BF16)], axis=0)
                o = _dot_tn(p_cat, jnp.concatenate(v_ops[g], axis=0))
                o_tiles += [o[0:BLK], o[BLK:]]
            o_att = jnp.concatenate(o_tiles, axis=1)
            oatt_ref[rows, :] = o_att
            out = [o_att * (az * _sigmoid(az))]

            qr, kr, qd, kd = _retention_operands(p_ref, rows, cos_ref[rows, :], sin_ref[rows, :],
                                                 qdec_ref[...], kdec_ref[...])
            heads = [(t, hh) for t in range(PAIRS) for hh in range(2)]
            sc = [_dot_nt(jnp.concatenate(_split_heads(_tile(qr, t)), axis=0).astype(BF16), _tile(kr, t).astype(BF16))
                  * din_ref[t] for t in range(PAIRS)]
            qd_heads = [_split_heads(_tile(qd, t)) for t in range(PAIRS)]
            state_b = [states[t].astype(BF16) for t in range(PAIRS)]
            vs = [p_ref[rows, OFF_RV + h * RET_V_DIM:OFF_RV + (h + 1) * RET_V_DIM].astype(BF16) for h in range(RET_HEADS)]
            rzs = [p_ref[rows, OFF_RZ + h * RET_V_DIM:OFF_RZ + (h + 1) * RET_V_DIM] for h in range(RET_HEADS)]
            lhs = [jnp.concatenate([sc[t][hh * BLK:(hh + 1) * BLK].astype(BF16), qd_heads[t][hh].astype(BF16)], axis=1)
                   for t, hh in heads]
            ons, _ = _group_norm_all([_dot(lhs[2 * t + hh], jnp.concatenate([vs[2 * t + hh], state_b[t]], axis=0))
                                      for t, hh in heads])
            out += [(ons[h] * gain_ref[:, h * RET_V_DIM:(h + 1) * RET_V_DIM]) * (rzs[h] * _sigmoid(rzs[h]))
                    for h in range(RET_HEADS)]
            new_states = [states[t] * cd_ref[t]
                          + _dot_tn(jnp.concatenate(_split_heads(_tile(kd, t)), axis=0).astype(BF16),
                                    jnp.concatenate([vs[2 * t], vs[2 * t + 1]], axis=0)) for t in range(PAIRS)]
            mix_ref[rows, :] = jnp.concatenate(out, axis=1).astype(BF16)
            st_ref[j] = jnp.stack(states)
            return ak, av, tuple(new_states)

        carry = (kprev_ref[...], vprev_ref[...], tuple(state_ref[t] for t in range(PAIRS)))
        kp, vp, states = lax.fori_loop(0, nsub, sub, carry)
        kprev_ref[...] = kp
        vprev_ref[...] = vp
        state_ref[...] = jnp.stack(states)

        xo = x_ref[...] + _dot(mix_ref[...], wout_ref[...])
        r2 = lax.rsqrt(jnp.mean(xo * xo, axis=1, keepdims=True) + RMS_EPS)
        xn = xo * r2
        err = xn * fg_ref[...] - t_ref[...]
        loss_ref[...] += jnp.sum(err * err) * (0.5 / D_MODEL)
        dy = err * (1.0 / D_MODEL)
        gfin_ref[...] += jnp.sum(dy * xn, axis=0, keepdims=True)
        u = dy * fg_ref[...]
        dxo_ref[...] = r2 * u - xn * (r2 * jnp.mean(u * xn, axis=1, keepdims=True))

    blk_rows = lambda w: pl.BlockSpec((tb, w), lambda i: (i, 0))
    state_shape = (PAIRS, 2 * RET_QK_DIM, RET_V_DIM)
    return _call(
        body, name="mix_fwd", grid=(seq // tb,),
        out_shape=(
            jax.ShapeDtypeStruct((seq, MIX_WIDTH), BF16),
            jax.ShapeDtypeStruct((seq, D_MODEL), F32),
            jax.ShapeDtypeStruct((seq // BLK,) + state_shape, F32),
            jax.ShapeDtypeStruct((8, LANE), F32),
            jax.ShapeDtypeStruct((1, D_MODEL), F32),
            jax.ShapeDtypeStruct((seq, ATT_WIDTH), F32),
        ),
        in_specs=[
            blk_rows(IN_WIDTH), blk_rows(D_MODEL), blk_rows(D_MODEL), blk_rows(LANE), blk_rows(LANE),
            _const_spec((MIX_WIDTH, D_MODEL)), _const_spec((1, D_MODEL)), _const_spec((1, RET_WIDTH)),
            pl.BlockSpec(memory_space=pltpu.SMEM),
            _const_spec((PAIRS, 2 * BLK, BLK)), _const_spec((BLK, RET_QK_WIDTH)), _const_spec((BLK, RET_QK_WIDTH)),
            _const_spec(state_shape), _const_spec((2, 2 * BLK, 2 * BLK)),
        ],
        out_specs=(
            blk_rows(MIX_WIDTH), blk_rows(D_MODEL),
            pl.BlockSpec((nsub,) + state_shape, lambda i: (i, 0, 0, 0)),
            _const_spec((8, LANE)), _const_spec((1, D_MODEL)), blk_rows(ATT_WIDTH),
        ),
        scratch_shapes=[
            pltpu.VMEM((BLK, ATT_KV_WIDTH), F32), pltpu.VMEM((BLK, ATT_KV_WIDTH), F32),
            pltpu.VMEM(state_shape, F32),
        ],
        compiler_params=_params(48, ("arbitrary",)),
    )(proj, x, target, cos_t, sin_t, w_out, final_g, gn_gain, sinks, decay_in, qdec_t, kdec_t, cd_t, bias_t)


def _mix_bwd(proj, dxo, mix, o_att, states, w_out, gn_gain, sinks, tables, tb):
    seq = dxo.shape[0]
    nsub = tb // BLK
    nblk = seq // tb
    cos_t, sin_t, decay_in, qdec_t, kdec_t, cd_t, bias_t = tables
    kv_cols = OFF_AK // (2 * ATT_KV_WIDTH)
    state_shape = (PAIRS, 2 * RET_QK_DIM, RET_V_DIM)

    def body(p_ref, pkv_ref, dxo_ref, mix_ref, oatt_ref, st_ref, cos_ref, sin_ref, wout_ref, gain_ref, sinks_ref, din_ref,
             qdec_ref, kdec_ref, cd_ref, bias_ref, dp_ref, gwout_ref, dgain_ref, dsink_ref,
             dmix_ref, kv_ref, dkc_ref, dvc_ref, gst_ref):
        i = pl.program_id(0)
        blk = nblk - 1 - i

        @pl.when(i == 0)
        def _():
            gwout_ref[...] = jnp.zeros_like(gwout_ref)
            dgain_ref[...] = jnp.zeros_like(dgain_ref)
            dsink_ref[...] = jnp.zeros_like(dsink_ref)
            dkc_ref[...] = jnp.zeros_like(dkc_ref)
            dvc_ref[...] = jnp.zeros_like(dvc_ref)
            gst_ref[...] = jnp.zeros_like(gst_ref)

        dxo_b = dxo_ref[...].astype(BF16)
        dmix_ref[...] = _dot_nt(dxo_b, wout_ref[...])
        gwout_ref[...] += _dot_tn(mix_ref[...], dxo_b)
        kv_ref[0:BLK, :] = pkv_ref[...]
        kv_ref[BLK:, :] = p_ref[:, OFF_AK:OFF_AK + 2 * ATT_KV_WIDTH]
        low = _low_lanes((BLK, LANE))
        low2 = _low_lanes((2 * BLK, LANE))
        lane = lax.broadcasted_iota(jnp.int32, (1, LANE), 1)

        def sub(jj, carry):
            dkc, dvc, gsts, dgain, dsink = carry
            j = nsub - 1 - jj
            rows = pl.ds(pl.multiple_of(j * BLK, BLK), BLK)
            both = pl.ds(pl.multiple_of(j * BLK, BLK), 2 * BLK)
            bias = bias_ref[jnp.where(jnp.logical_or(blk > 0, j > 0), 1, 0)]

            aq = p_ref[rows, OFF_AQ:OFF_AQ + ATT_WIDTH]
            az = p_ref[rows, OFF_AZ:OFF_AZ + ATT_WIDTH]
            k_ops = _kv_operands(kv_ref[both, 0:ATT_KV_WIDTH])
            v_ops = _kv_operands(kv_ref[both, ATT_KV_WIDTH:2 * ATT_KV_WIDTH])
            da = dmix_ref[rows, 0:ATT_WIDTH]
            sig = _sigmoid(az)
            d_o = da * (az * sig)
            qs = [(_stack_tiles(aq, 2 * g) * ATT_SCALE).astype(BF16) for g in range(ATT_KV_HEADS)]
            dos = [_stack_tiles(d_o, 2 * g).astype(BF16) for g in range(ATT_KV_HEADS)]
            p, p_sink = _attn_probs(qs, k_ops, bias, sinks_ref)
            dpr = [_dot_nt(v_ops[g][hi], dos[g]) for g, hi in ATT_PROBLEMS]
            delta = [jnp.sum(pi * di, axis=0, keepdims=True) for pi, di in zip(p, dpr)]
            ds = [(pi * (di - ti)).astype(BF16) for pi, di, ti in zip(p, dpr, delta)]
            for (g, hi), ki, ti in zip(ATT_PROBLEMS, p_sink, delta):
                sink_part = ki * ti
                for half in range(2):
                    tot = jnp.sum(sink_part[:, half * BLK:(half + 1) * BLK], axis=1, keepdims=True)
                    dsink = dsink - jnp.where(lane == 4 * g + 2 * half + hi, tot, 0.0)
            dq_tiles, dk_sums, dv_sums = [], [], []
            for g in range(ATT_KV_HEADS):
                ds_cat = jnp.concatenate([ds[2 * g], ds[2 * g + 1]], axis=0)
                p_cat = jnp.concatenate([p[2 * g].astype(BF16), p[2 * g + 1].astype(BF16)], axis=0)
                dqs = _dot_tn(ds_cat, jnp.concatenate(k_ops[g], axis=0)) * ATT_SCALE
                dq_tiles += [dqs[0:BLK], dqs[BLK:]]
                dk_sums.append(_dot(ds_cat, qs[g]))
                dv_sums.append(_dot(p_cat, dos[g]))
            daz = da * oatt_ref[rows, :] * (sig * (1.0 + az * (1.0 - sig)))

            def kv_grad(sums):
                (a0, b0), (a1, b1) = [(s[0:2 * BLK], s[2 * BLK:]) for s in sums]
                return jnp.where(low2, a0, b1) + pltpu.roll(jnp.where(low2, a1, b0), HALF_LANE, 1)

            dk_both, dv_both = kv_grad(dk_sums), kv_grad(dv_sums)
            dak = dk_both[BLK:] + dkc
            dav = dv_both[BLK:] + dvc

            cos_b, sin_b = cos_ref[rows, :], sin_ref[rows, :]
            qdec, kdec = qdec_ref[...], kdec_ref[...]
            qr, kr, qd, kd = _retention_operands(p_ref, rows, cos_b, sin_b, qdec, kdec)
            heads = [(t, hh) for t in range(PAIRS) for hh in range(2)]
            head_cols = [slice(h * RET_V_DIM, (h + 1) * RET_V_DIM) for h in range(RET_HEADS)]
            q_rows = [jnp.concatenate(_split_heads(_tile(qr, t)), axis=0).astype(BF16) for t in range(PAIRS)]
            k_rows = [jnp.concatenate(_split_heads(_tile(kr, t)), axis=0).astype(BF16) for t in range(PAIRS)]
            din = [din_ref[t] for t in range(PAIRS)]
            sc = [(_dot_nt(q_rows[t], _tile(kr, t).astype(BF16)) * din[t]).astype(BF16) for t in range(PAIRS)]
            qd_heads = [_split_heads(_tile(qd, t)) for t in range(PAIRS)]
            kd_heads = [_split_heads(_tile(kd, t)) for t in range(PAIRS)]
            state_b = [st_ref[j, t].astype(BF16) for t in range(PAIRS)]
            gst_b = [gsts[t].astype(BF16) for t in range(PAIRS)]
            vs = [p_ref[rows, OFF_RV + h * RET_V_DIM:OFF_RV + (h + 1) * RET_V_DIM].astype(BF16) for h in range(RET_HEADS)]
            rzs = [p_ref[rows, OFF_RZ + h * RET_V_DIM:OFF_RZ + (h + 1) * RET_V_DIM] for h in range(RET_HEADS)]
            drs = [dmix_ref[rows, ATT_WIDTH + h * RET_V_DIM:ATT_WIDTH + (h + 1) * RET_V_DIM] for h in range(RET_HEADS)]
            gains = [gain_ref[:, c] for c in head_cols]
            lhs = [jnp.concatenate([sc[t][hh * BLK:(hh + 1) * BLK], qd_heads[t][hh].astype(BF16)], axis=1) for t, hh in heads]
            rhs = [jnp.concatenate([vs[2 * t + hh], state_b[t]], axis=0) for t, hh in heads]
            ons, rstds = _group_norm_all([_dot(l, r) for l, r in zip(lhs, rhs)])
            sig_r = [_sigmoid(z) for z in rzs]
            dgn = [d * (z * g) for d, z, g in zip(drs, rzs, sig_r)]
            dz_parts = [d * (o * gn) * (g * (1.0 + z * (1.0 - g))) for d, o, gn, g, z in zip(drs, ons, gains, sig_r, rzs)]
            dgain_parts = [jnp.sum(d * o, axis=0, keepdims=True) for d, o in zip(dgn, ons)]
            don = [d * gn for d, gn in zip(dgn, gains)]
            mean_don = [jnp.mean(d, axis=1, keepdims=True) for d in don]
            mean_don_on = [jnp.mean(d * o, axis=1, keepdims=True) for d, o in zip(don, ons)]
            dob = [(r * (d - a - o * b)).astype(BF16) for r, d, a, o, b in zip(rstds, don, mean_don, ons, mean_don_on)]
            dlhs = [_dot_nt(d, r) for d, r in zip(dob, rhs)]
            drhs = [_dot_tn(l, d) for l, d in zip(lhs, dob)]
            dkds = [_dot_nt(vs[2 * t + hh], gst_b[t]) for t, hh in heads]
            dv_parts = [drhs[2 * t + hh][0:BLK] + _dot(kd_heads[t][hh].astype(BF16), gst_b[t]) for t, hh in heads]
            das = [(dlhs[2 * t + hh][:, 0:BLK] * din[t][hh * BLK:(hh + 1) * BLK]).astype(BF16) for t, hh in heads]
            new_gsts = [gsts[t] * cd_ref[t] + drhs[2 * t][BLK:] + drhs[2 * t + 1][BLK:] for t in range(PAIRS)]
            dq_parts = [_dot(jnp.concatenate([das[2 * t], das[2 * t + 1]], axis=1), k_rows[t])
                        + jnp.where(low, dlhs[2 * t][:, BLK:], dlhs[2 * t + 1][:, BLK:]) * _tile(qdec, t)
                        for t in range(PAIRS)]
            dk_parts = [_dot_tn(jnp.concatenate([das[2 * t], das[2 * t + 1]], axis=0), q_rows[t])
                        + jnp.where(low, dkds[2 * t], dkds[2 * t + 1]) * _tile(kdec, t) for t in range(PAIRS)]
            drq = _rotate_transposed(jnp.concatenate(dq_parts, axis=1), cos_b, sin_b)
            drk = _rotate_transposed(jnp.concatenate(dk_parts, axis=1) * RET_SCALE, cos_b, sin_b)

            dp_ref[rows, :] = jnp.concatenate(
                [jnp.concatenate(dq_tiles, axis=1), dak, dav, daz, drq, drk] + dv_parts + dz_parts, axis=1).astype(BF16)
            dgain = dgain + jnp.concatenate(dgain_parts, axis=1)
            return dk_both[0:BLK], dv_both[0:BLK], tuple(new_gsts), dgain, dsink

        carry = (dkc_ref[...], dvc_ref[...], tuple(gst_ref[t] for t in range(PAIRS)), dgain_ref[...], dsink_ref[...])
        dkc, dvc, gsts, dgain, dsink = lax.fori_loop(0, nsub, sub, carry)
        dkc_ref[...] = dkc
        dvc_ref[...] = dvc
        gst_ref[...] = jnp.stack(gsts)
        dgain_ref[...] = dgain
        dsink_ref[...] = dsink

    rev_rows = lambda w: pl.BlockSpec((tb, w), lambda i: (nblk - 1 - i, 0))
    prev_kv = pl.BlockSpec((BLK, 2 * ATT_KV_WIDTH), lambda i: (jnp.maximum((nblk - 1 - i) * nsub - 1, 0), kv_cols))
    return _call(
        body, name="mix_bwd", grid=(nblk,),
        out_shape=(
            jax.ShapeDtypeStruct((seq, IN_WIDTH), BF16),
            jax.ShapeDtypeStruct((MIX_WIDTH, D_MODEL), F32),
            jax.ShapeDtypeStruct((1, RET_WIDTH), F32),
            jax.ShapeDtypeStruct((1, LANE), F32),
        ),
        in_specs=[
            rev_rows(IN_WIDTH), prev_kv, rev_rows(D_MODEL), rev_rows(MIX_WIDTH), rev_rows(ATT_WIDTH),
            pl.BlockSpec((nsub,) + state_shape, lambda i: (nblk - 1 - i, 0, 0, 0)),
            rev_rows(LANE), rev_rows(LANE),
            _const_spec((MIX_WIDTH, D_MODEL)), _const_spec((1, RET_WIDTH)),
            pl.BlockSpec(memory_space=pltpu.SMEM),
            _const_spec((PAIRS, 2 * BLK, BLK)), _const_spec((BLK, RET_QK_WIDTH)), _const_spec((BLK, RET_QK_WIDTH)),
            _const_spec(state_shape), _const_spec((2, 2 * BLK, 2 * BLK)),
        ],
        out_specs=(
            rev_rows(IN_WIDTH), _const_spec((MIX_WIDTH, D_MODEL)), _const_spec((1, RET_WIDTH)), _const_spec((1, LANE)),
        ),
        scratch_shapes=[
            pltpu.VMEM((tb, MIX_WIDTH), F32),
            pltpu.VMEM((tb + BLK, 2 * ATT_KV_WIDTH), F32),
            pltpu.VMEM((BLK, ATT_KV_WIDTH), F32), pltpu.VMEM((BLK, ATT_KV_WIDTH), F32),
            pltpu.VMEM(state_shape, F32),
        ],
        compiler_params=_params(56, ("arbitrary",)),
    )(proj, proj, dxo, mix, o_att, states, cos_t, sin_t, w_out, gn_gain, sinks, decay_in, qdec_t, kdec_t, cd_t, bias_t)


def _in_proj_bwd(dproj, x, dxo, norm_g, w_in, tb):
    seq = x.shape[0]

    def body(dp_ref, x_ref, dxo_ref, g_ref, w_ref, gx_ref, gnorm_ref):
        i = pl.program_id(0)

        @pl.when(i == 0)
        def _():
            gnorm_ref[...] = jnp.zeros_like(gnorm_ref)

        xv = x_ref[...]
        r = lax.rsqrt(jnp.mean(xv * xv, axis=1, keepdims=True) + RMS_EPS)
        xn = xv * r
        dh = _dot(dp_ref[...], w_ref[...])
        gnorm_ref[...] += jnp.sum(dh * xn, axis=0, keepdims=True)
        u = dh * g_ref[...]
        gx_ref[...] = dxo_ref[...] + r * u - xn * (r * jnp.mean(u * xn, axis=1, keepdims=True))

    rows = lambda w: pl.BlockSpec((tb, w), lambda i: (i, 0))
    return _call(
        body, name="in_proj_bwd", grid=(seq // tb,),
        out_shape=(jax.ShapeDtypeStruct((seq, D_MODEL), F32), jax.ShapeDtypeStruct((1, D_MODEL), F32)),
        in_specs=[rows(IN_WIDTH), rows(D_MODEL), rows(D_MODEL), _const_spec((1, D_MODEL)),
                  _const_spec((IN_WIDTH, D_MODEL))],
        out_specs=(rows(D_MODEL), _const_spec((1, D_MODEL))),
        compiler_params=_params(48, ("arbitrary",)),
    )(dproj, x, dxo, norm_g, w_in)


def _gw_in_reduce(h_t, dproj, tb):
    seq = dproj.shape[0]
    nblk = seq // tb
    last = nblk - 1
    hand_on = min(2, last)
    half = D_MODEL // 2
    A, B, C, N_SEMS = 0, N_CHIPS, 2 * N_CHIPS, 2 * N_CHIPS + 1

    def body(win_ref, ht_ref, dp_ref, out_ref, acc, sib, send_buf, b_in, fin, send_sems, recv_sems):
        p, i = pl.program_id(0), pl.program_id(1)
        x, y, c = lax.axis_index("x"), lax.axis_index("y"), lax.axis_index("c")
        chip = 2 * x + y
        sibling = (x, y, 1 - c)
        mine = pl.ds(pl.multiple_of(c * half, half), half)
        other = pl.ds(pl.multiple_of((1 - c) * half, half), half)

        def remote(src, dst, send_k, recv_k, to):
            return pltpu.make_async_remote_copy(src_ref=src, dst_ref=dst, send_sem=send_sems.at[send_k],
                                                recv_sem=recv_sems.at[recv_k], device_id=to, device_id_type=MESH_ID)

        part = _dot(ht_ref[...], dp_ref[...])
        slot = p % 2

        @pl.when(i == 0)
        def _():
            acc[slot] = part

        @pl.when(i > 0)
        def _():
            acc[slot] += part

        for q in range(N_CHIPS):
            s = q % 2
            to_sibling = remote(acc.at[s, other, :], sib.at[s], A + q, A + q, sibling)

            @pl.when(jnp.logical_and(p == q, i == last))
            def _():
                to_sibling.start()

            if q < N_CHIPS - 1:
                dest = (chip + 1 + q) % N_CHIPS

                @pl.when(jnp.logical_and(p == q + 1, i == hand_on))
                def _():
                    to_sibling.wait_recv()
                    send_buf[q] = (acc[s, mine, :] + sib[s]).astype(BF16)
                    remote(send_buf.at[q], b_in.at[chip], B + q, B + chip, (dest // 2, dest % 2, c)).start()
                    to_sibling.wait_send()
            else:
                @pl.when(jnp.logical_and(p == q, i == last))
                def _():
                    to_sibling.wait_recv()
                    fin[mine, :] = acc[s, mine, :] + sib[s]
                    for j in range(N_CHIPS):
                        @pl.when(j != chip)
                        def _():
                            remote(b_in.at[j], b_in.at[j], B + j, B + j, sibling).wait_recv()
                            fin[mine, :] += b_in[j].astype(F32)
                    to_core = remote(fin.at[mine, :], fin.at[mine, :], C, C, sibling)
                    to_core.start()
                    remote(fin.at[other, :], fin.at[other, :], C, C, sibling).wait_recv()
                    out_ref[...] = fin[...]
                    to_core.wait_send()
                    to_sibling.wait_send()
                    for k in range(N_CHIPS - 1):
                        remote(send_buf.at[k], b_in.at[chip], B + k, B + k, sibling).wait_send()

    grid_spec = pltpu.PrefetchScalarGridSpec(
        num_scalar_prefetch=1, grid=(N_CHIPS, nblk),
        in_specs=[pl.BlockSpec((D_MODEL, tb), lambda p, i, win: (0, i)),
                  pl.BlockSpec((pl.Element(tb), pl.Element(SHARD_PAD)),
                               lambda p, i, win: (i * tb, pl.multiple_of(win[p] * LANE, LANE)))],
        out_specs=pl.BlockSpec((D_MODEL, SHARD_PAD), lambda p, i, win: (0, 0), pipeline_mode=pl.Buffered(1)),
        scratch_shapes=[
            pltpu.VMEM((2, D_MODEL, SHARD_PAD), F32), pltpu.VMEM((2, half, SHARD_PAD), F32),
            pltpu.VMEM((N_CHIPS - 1, half, SHARD_PAD), BF16), pltpu.VMEM((N_CHIPS, half, SHARD_PAD), BF16),
            pltpu.VMEM((D_MODEL, SHARD_PAD), F32),
            pltpu.SemaphoreType.DMA((N_SEMS,)), pltpu.SemaphoreType.DMA((N_SEMS,)),
        ])
    chip = 2 * lax.axis_index("x") + lax.axis_index("y")
    owner = (chip + 1 + jnp.arange(N_CHIPS, dtype=jnp.int32)) % N_CHIPS
    win_start = (owner * SHARD_IN) // LANE
    return _call(
        body, name="gw_in_reduce", grid_spec=grid_spec,
        out_shape=jax.ShapeDtypeStruct((D_MODEL, SHARD_PAD), F32),
        compiler_params=_params(40, ("arbitrary", "arbitrary")),
    )(win_start.astype(jnp.int32), h_t, dproj)


def _grads_reduce(gw_out, small):
    half_out = SHARD_OUT // 2
    A_OUT, B_OUT, C_OUT, PACK, N_SEMS = 0, 4, 8, 9, 17
    assert len(small) == PACK_PARTS

    def body(gwo_hbm, s0_ref, s1_ref, s2_ref, s3_ref, s4_ref, fout_ref, packsum_ref,
             mine_out, sib_out, send_out, b_out, pack_ref, packs, send_sems, recv_sems, local_sems):
        x, y, c = lax.axis_index("x"), lax.axis_index("y"), lax.axis_index("c")
        chip = 2 * x + y
        dev = 2 * chip + c
        sibling = (x, y, 1 - c)
        pack_ref[...] = jnp.zeros_like(pack_ref)
        for k, s_ref in enumerate((s0_ref, s1_ref, s2_ref, s3_ref, s4_ref)):
            pack_ref[k:k + 1, 0:s_ref.shape[1]] = s_ref[0:1, :]

        def remote(src, dst, send_k, recv_k, to):
            return pltpu.make_async_remote_copy(src_ref=src, dst_ref=dst, send_sem=send_sems.at[send_k],
                                                recv_sem=recv_sems.at[recv_k], device_id=to, device_id_type=MESH_ID)

        def out_rows(j, core):
            return pl.ds(pl.multiple_of(j * SHARD_OUT + core * half_out, half_out), half_out)

        packs[dev] = pack_ref[...]
        for d in range(N_DEV):
            to = (d // 4, (d // 2) % 2, d % 2)

            @pl.when(d != dev)
            def _():
                remote(pack_ref, packs.at[dev], PACK + d, PACK + dev, to).start()

        local = [pltpu.make_async_copy(gwo_hbm.at[out_rows(j, c), :], mine_out.at[j], local_sems.at[j])
                 for j in range(N_CHIPS)]
        for cp in local:
            cp.start()
        stage_a = [remote(gwo_hbm.at[out_rows(j, 1 - c), :], sib_out.at[j], A_OUT + j, A_OUT + j, sibling)
                   for j in range(N_CHIPS)]
        for cp in stage_a:
            cp.start()
        for cp in local:
            cp.wait()
        for cp in stage_a:
            cp.wait_recv()
        for j in range(N_CHIPS):
            mine_out[j] = mine_out[j] + sib_out[j]

        my_out_rows = pl.ds(pl.multiple_of(c * half_out, half_out), half_out)
        for j in range(N_CHIPS):
            to = (j // 2, j % 2, c)

            @pl.when(j != chip)
            def _():
                send_out[j] = mine_out[j].astype(BF16)
                remote(send_out.at[j], b_out.at[chip], B_OUT + j, B_OUT + chip, to).start()

            @pl.when(j == chip)
            def _():
                fout_ref[my_out_rows, :] = mine_out[j]

        for j in range(N_CHIPS):
            @pl.when(j != chip)
            def _():
                remote(b_out.at[j], b_out.at[j], B_OUT + j, B_OUT + j, sibling).wait_recv()
                fout_ref[my_out_rows, :] += b_out[j].astype(F32)

        mine_half_out = fout_ref.at[my_out_rows, :]
        stage_c = [remote(mine_half_out, mine_half_out, C_OUT, C_OUT, sibling)]
        for cp in stage_c:
            cp.start()
        other_half_out = fout_ref.at[pl.ds(pl.multiple_of((1 - c) * half_out, half_out), half_out), :]
        remote(other_half_out, other_half_out, C_OUT, C_OUT, sibling).wait_recv()

        for d in range(N_DEV):
            @pl.when(d != dev)
            def _():
                remote(pack_ref, packs.at[d], PACK + d, PACK + d, sibling).wait_recv()
        total = packs[0]
        for d in range(1, N_DEV):
            total = total + packs[d]
        packsum_ref[...] = total

        for cp in stage_a + stage_c:
            cp.wait_send()
        for j in range(N_CHIPS):
            @pl.when(j != chip)
            def _():
                remote(b_out.at[j], b_out.at[j], B_OUT + j, B_OUT + j, sibling).wait_send()
        for d in range(N_DEV):
            @pl.when(d != dev)
            def _():
                remote(pack_ref, packs.at[d], PACK + d, PACK + d, sibling).wait_send()

    vmem = pl.BlockSpec(memory_space=pltpu.VMEM)
    hbm = pl.BlockSpec(memory_space=pl.ANY)
    return _call(
        body, name="grads_reduce",
        out_shape=(jax.ShapeDtypeStruct((SHARD_OUT, D_MODEL), F32), jax.ShapeDtypeStruct((PACK_ROWS, D_MODEL), F32)),
        in_specs=[hbm] + [vmem] * PACK_PARTS, out_specs=(vmem, vmem),
        scratch_shapes=[
            pltpu.VMEM((N_CHIPS, half_out, D_MODEL), F32), pltpu.VMEM((N_CHIPS, half_out, D_MODEL), F32),
            pltpu.VMEM((N_CHIPS, half_out, D_MODEL), BF16), pltpu.VMEM((N_CHIPS, half_out, D_MODEL), BF16),
            pltpu.VMEM((PACK_ROWS, D_MODEL), F32), pltpu.VMEM((N_DEV, PACK_ROWS, D_MODEL), F32),
            pltpu.SemaphoreType.DMA((N_SEMS,)), pltpu.SemaphoreType.DMA((N_SEMS,)),
            pltpu.SemaphoreType.DMA((N_CHIPS,)),
        ],
        compiler_params=_params(24),
    )(gw_out, *small)


def _adam_math(w, g, m, v):
    mn = ADAM_B1 * m + (1.0 - ADAM_B1) * g
    vn = ADAM_B2 * v + (1.0 - ADAM_B2) * (g * g)
    m_hat = mn / (1.0 - ADAM_B1 ** ADAM_STEP)
    v_hat = vn / (1.0 - ADAM_B2 ** ADAM_STEP)
    return -ADAM_LR * (m_hat / (jnp.sqrt(v_hat) + ADAM_EPS) + ADAM_WD * w), mn, vn


def _adamw(name, w, g, m, v, tb):
    rows, cols = w.shape

    def body(w_ref, g_ref, m_ref, v_ref, go_ref, d_ref, mo_ref, vo_ref):
        gv = g_ref[...]
        go_ref[...] = gv
        d_ref[...], mo_ref[...], vo_ref[...] = _adam_math(w_ref[...], gv, m_ref[...], v_ref[...])

    spec = pl.BlockSpec((tb, cols), lambda i: (i, 0))
    shape = jax.ShapeDtypeStruct((rows, cols), F32)
    return _call(
        body, name=name, grid=(rows // tb,), out_shape=(shape,) * 4,
        in_specs=[spec] * 4, out_specs=(spec,) * 4,
        compiler_params=_params(32, ("arbitrary",)),
    )(w, g, m, v)


def _adamw_w_in(w_t, g_window, m_t, v_t, tb):
    def body(w_ref, g_ref, m_ref, v_ref, go_ref, d_ref, mo_ref, vo_ref, gt_ref):
        gt_ref[...] = g_ref[...].T
        gv = gt_ref[pl.ds(pl.multiple_of(lax.axis_index("y") * SHARD_SHIFT, SHARD_SHIFT), SHARD_IN), :]
        go_ref[...] = gv
        d_ref[...], mo_ref[...], vo_ref[...] = _adam_math(w_ref[...], gv, m_ref[...], v_ref[...])

    spec = pl.BlockSpec((SHARD_IN, tb), lambda i: (0, i))
    shape = jax.ShapeDtypeStruct((SHARD_IN, D_MODEL), F32)
    return _call(
        body, name="adamw_w_in", grid=(D_MODEL // tb,), out_shape=(shape,) * 4,
        in_specs=[spec, pl.BlockSpec((tb, SHARD_PAD), lambda i: (i, 0)), spec, spec], out_specs=(spec,) * 4,
        scratch_shapes=[pltpu.VMEM((SHARD_PAD, tb), F32)],
        compiler_params=_params(32, ("arbitrary",)),
    )(w_t, g_window, m_t, v_t)


def _adamw_small(sums, params):
    def body(sums_ref, *refs):
        ins, outs = refs[:3 * len(params)], refs[3 * len(params):]
        for k in range(len(params)):
            w_ref, m_ref, v_ref = ins[3 * k:3 * k + 3]
            g = sums_ref[k:k + 1, 0:w_ref.shape[1]]
            go_ref, d_ref, mo_ref, vo_ref = outs[4 * k:4 * k + 4]
            go_ref[...] = g
            d_ref[...], mo_ref[...], vo_ref[...] = _adam_math(w_ref[...], g, m_ref[...], v_ref[...])

    vmem = pl.BlockSpec(memory_space=pltpu.VMEM)
    flat = [a for p in params for a in p]
    shapes = tuple(jax.ShapeDtypeStruct(p[0].shape, F32) for p in params for _ in range(4))
    res = _call(body, name="adamw_small", out_shape=shapes, in_specs=[vmem] * (1 + len(flat)),
                out_specs=(vmem,) * len(shapes), compiler_params=_params(16))(sums, *flat)
    return [res[4 * k:4 * k + 4] for k in range(len(params))]


def kernel(x, norm_g, w_in, att_sinks, ret_gn_g, w_out, final_g, loss_target, m_norm_g, m_w_in, m_att_sinks, m_ret_gn_g, m_w_out, m_final_g, v_norm_g, v_w_in, v_att_sinks, v_ret_gn_g, v_w_out, v_final_g):
    seq = x.shape[1]
    xs, tgt = x[0], loss_target[0]
    final_g2 = final_g.reshape(1, D_MODEL)
    tables = _tables(seq)

    w_in_t, m_w_in_t, v_w_in_t = w_in[0].T, m_w_in[0].T, v_w_in[0].T
    w_in_full, w_out_full = _weights_allgather(w_in_t, w_out[0])
    proj, h_t = _in_proj(xs, norm_g, w_in_full, min(TOKENS_PROJ, seq))
    mix, dxo, states, loss_part, gfin, o_att = _mix_fwd(proj, xs, tgt, w_out_full, final_g2, ret_gn_g, att_sinks,
                                                        tables, min(TOKENS_MIX, seq))
    dproj, gw_out, dgain, dsink = _mix_bwd(proj, dxo, mix, o_att, states, w_out_full, ret_gn_g, att_sinks, tables,
                                           min(TOKENS_MIX, seq))
    g_in = _gw_in_reduce(h_t, dproj, min(TOKENS_GW, seq))
    grad_x, gnorm = _in_proj_bwd(dproj, xs, dxo, norm_g, w_in_full, min(TOKENS_PROJ_BWD, seq))

    g_out, sums = _grads_reduce(gw_out, (gnorm, gfin, dgain, dsink, loss_part))

    res_in = [r.T for r in _adamw_w_in(w_in_t, g_in, m_w_in_t, v_w_in_t, 256)]
    res_out = _adamw("adamw_w_out", w_out[0], g_out, m_w_out[0], v_w_out[0], SHARD_OUT)
    as_row = lambda a: a.reshape(1, D_MODEL)
    r_norm, r_final, r_gain, r_sink = _adamw_small(sums, [
        (norm_g, m_norm_g, v_norm_g), (final_g2, as_row(m_final_g), as_row(v_final_g)),
        (ret_gn_g, m_ret_gn_g, v_ret_gn_g), (att_sinks, m_att_sinks, v_att_sinks)])

    outs = []
    for k in range(4):
        outs += [r_norm[k], res_in[k][None], r_sink[k], r_gain[k], res_out[k][None], r_final[k].reshape(D_MODEL)]
    return (sums[4, 0], grad_x[None], *outs)
```

```python
import jax
import jax.numpy as jnp
import numpy as np
from jax import lax
from jax.experimental import pallas as pl
from jax.experimental.pallas import tpu as pltpu

F32 = jnp.float32
BF16 = jnp.bfloat16

D_MODEL = 1024
ATT_HEADS = 8
ATT_KV_HEADS = 2
ATT_HEAD_DIM = 64
RET_HEADS = 4
RET_QK_DIM = 64
RET_V_DIM = 128
BLK = 128
ROT_BASE = 10000.0
RMS_EPS = 1e-6
GN_EPS = 1e-6
NEG_INF = -1e30
ATT_SCALE = ATT_HEAD_DIM ** -0.5
RET_SCALE = RET_QK_DIM ** -0.5

ATT_WIDTH = ATT_HEADS * ATT_HEAD_DIM
ATT_KV_WIDTH = ATT_KV_HEADS * ATT_HEAD_DIM
RET_QK_WIDTH = RET_HEADS * RET_QK_DIM
RET_WIDTH = RET_HEADS * RET_V_DIM
MIX_WIDTH = ATT_WIDTH + RET_WIDTH
OFF_AQ = 0
OFF_AK = OFF_AQ + ATT_WIDTH
OFF_AV = OFF_AK + ATT_KV_WIDTH
OFF_AZ = OFF_AV + ATT_KV_WIDTH
OFF_RQ = OFF_AZ + ATT_WIDTH
OFF_RK = OFF_RQ + RET_QK_WIDTH
OFF_RV = OFF_RK + RET_QK_WIDTH
OFF_RZ = OFF_RV + RET_WIDTH
IN_WIDTH = OFF_RZ + RET_WIDTH

LANE = 128
BF16_ROWS = 16
HALF_LANE = LANE // 2
PAIRS = RET_QK_WIDTH // LANE
assert ATT_HEAD_DIM == HALF_LANE and RET_QK_DIM == HALF_LANE and RET_V_DIM == LANE and ATT_KV_WIDTH == LANE

N_CHIPS = 4
N_DEV = 8
SHARD_IN = IN_WIDTH // N_CHIPS
SHARD_PAD = 768
SHARD_SHIFT = SHARD_PAD - SHARD_IN
WIN_START = tuple((j * SHARD_IN) // LANE * LANE for j in range(N_CHIPS))
SHARD_OUT = MIX_WIDTH // N_CHIPS

PACK_PARTS = 5
PACK_ROWS = 8
assert PACK_PARTS <= PACK_ROWS

ADAM_LR = 0.001
ADAM_B1 = 0.9
ADAM_B2 = 0.999
ADAM_EPS = 1e-08
ADAM_WD = 0.01
ADAM_STEP = 10

VMEM_CAP = 64 * 1024 * 1024
TOKENS_PROJ = 1024
TOKENS_MIX = 512
TOKENS_PROJ_BWD = 512
TOKENS_GW = 2048
MESH_ID = pl.DeviceIdType.MESH


def _call(body, **kw):
    return pl.pallas_call(body, **kw)


def _params(vmem_mb, semantics=None):
    assert vmem_mb * 1024 * 1024 < VMEM_CAP
    return pltpu.CompilerParams(dimension_semantics=semantics, vmem_limit_bytes=vmem_mb * 1024 * 1024)


def _dot(a, b):
    return jnp.dot(a, b, preferred_element_type=F32)


def _dot_nt(a, b):
    return lax.dot_general(a, b, (((1,), (1,)), ((), ())), preferred_element_type=F32)


def _dot_tn(a, b):
    return lax.dot_general(a, b, (((0,), (0,)), ((), ())), preferred_element_type=F32)


def _sigmoid(z):
    return 1.0 / (1.0 + jnp.exp(-z))


def _const_spec(shape):
    nd = len(shape)
    return pl.BlockSpec(shape, lambda i: (0,) * nd, pipeline_mode=pl.Buffered(1))


def _tables(seq):
    f32 = np.float32
    pos = np.arange(seq, dtype=f32)
    theta = (f32(1.0) / (f32(ROT_BASE) ** np.linspace(0.0, 1.0, RET_QK_DIM // 2, dtype=f32))).astype(f32)
    ang = (pos[:, None] * theta[None, :]).astype(f32)
    cos, sin = np.cos(ang), np.sin(ang)
    cos2 = np.repeat(cos, 2, axis=1)
    sin2 = np.stack([-sin, sin], axis=-1).reshape(seq, RET_QK_DIM)
    cos_t = np.tile(cos2, (1, 2))
    sin_t = np.tile(sin2, (1, 2))

    log_gamma = np.log(f32(1.0) - f32(2.0) ** (f32(-5.0) - np.arange(RET_HEADS, dtype=f32))).astype(f32)
    idx = np.arange(BLK, dtype=f32)
    rel = idx[:, None] - idx[None, :]
    decay_in = np.where(rel >= 0, np.exp(log_gamma[:, None, None] * np.maximum(rel, f32(0.0))), f32(0.0))
    k_dec = np.exp(log_gamma[:, None] * (BLK - 1 - idx)[None, :])
    q_dec = np.exp(log_gamma[:, None] * (idx + 1)[None, :])
    chunk_decay = np.exp(log_gamma * f32(BLK))
    kdec_t = np.repeat(k_dec.T, RET_QK_DIM, axis=1)
    qdec_t = np.repeat(q_dec.T, RET_QK_DIM, axis=1)
    cd_t = np.broadcast_to(chunk_decay[:, None, None], (RET_HEADS, RET_QK_DIM, RET_V_DIM))
    decay_in = decay_in.reshape(PAIRS, 2 * BLK, BLK)
    cd_t = cd_t.reshape(PAIRS, 2 * RET_QK_DIM, RET_V_DIM)

    key = np.arange(2 * BLK)[:, None]
    query = np.arange(2 * BLK)[None, :] % BLK
    diff = query + BLK - key
    valid = (diff >= 0) & (diff < BLK)
    bias = np.stack([np.where(valid & (key >= BLK), 0.0, NEG_INF), np.where(valid, 0.0, NEG_INF)])
    return tuple(jnp.asarray(np.ascontiguousarray(a), F32) for a in (cos_t, sin_t, decay_in, qdec_t, kdec_t, cd_t, bias))


def _low_lanes(shape):
    lane = lax.broadcasted_iota(jnp.int32, shape, len(shape) - 1)
    return (lane & HALF_LANE) == 0


def _split_heads(t):
    low = _low_lanes(t.shape)
    zero = jnp.zeros_like(t)
    return jnp.where(low, t, zero), jnp.where(low, zero, t)


def _swap_pairs(t):
    lane = lax.broadcasted_iota(jnp.int32, t.shape, 1)
    nxt = pltpu.roll(t, t.shape[1] - 1, 1)
    prv = pltpu.roll(t, 1, 1)
    return jnp.where((lane & 1) == 0, nxt, prv)


def _per_tile(fn, t):
    return jnp.concatenate([fn(_tile(t, i)) for i in range(t.shape[1] // LANE)], axis=1)


def _rotate(t, cos_t, sin_t):
    return _per_tile(lambda a: a * cos_t + _swap_pairs(a) * sin_t, t)


def _rotate_transposed(d, cos_t, sin_t):
    return _per_tile(lambda a: a * cos_t + _swap_pairs(a * sin_t), d)


def _kv_operands(cat):
    low = _low_lanes(cat.shape)
    swapped = pltpu.roll(cat, HALF_LANE, 1)
    zero = jnp.zeros_like(cat)
    pick = lambda a, b: jnp.where(low, a, b).astype(BF16)
    return ((pick(cat, zero), pick(zero, swapped)), (pick(swapped, zero), pick(zero, cat)))


def _stack_tiles(t, first_tile):
    a = t[:, first_tile * LANE:(first_tile + 1) * LANE]
    b = t[:, (first_tile + 1) * LANE:(first_tile + 2) * LANE]
    return jnp.concatenate([a, b], axis=0)


def _sink_rows(sinks_ref, group):
    first = lax.broadcasted_iota(jnp.int32, (1, 2 * BLK), 1) < BLK

    def row(h0, h1):
        return jnp.where(first, sinks_ref[0, group * 4 + h0], sinks_ref[0, group * 4 + h1])
    return row(0, 2), row(1, 3)


ATT_PROBLEMS = tuple((g, hi) for g in range(ATT_KV_HEADS) for hi in range(2))


def _attn_probs(qs, k_ops, bias, sinks_ref):
    sink = [_sink_rows(sinks_ref, g)[hi] for g, hi in ATT_PROBLEMS]
    s = [_dot_nt(k_ops[g][hi], qs[g]) + bias for g, hi in ATT_PROBLEMS]
    m = [jnp.maximum(jnp.max(si, axis=0, keepdims=True), ki) for si, ki in zip(s, sink)]
    e = [jnp.exp(si - mi) for si, mi in zip(s, m)]
    es = [jnp.exp(ki - mi) for ki, mi in zip(sink, m)]
    inv = [1.0 / (jnp.sum(ei, axis=0, keepdims=True) + esi) for ei, esi in zip(e, es)]
    return [ei * ii for ei, ii in zip(e, inv)], [esi * ii for esi, ii in zip(es, inv)]


def _group_norm_all(outs):
    mu = [jnp.mean(o, axis=1, keepdims=True) for o in outs]
    xc = [o - m for o, m in zip(outs, mu)]
    var = [jnp.mean(c * c, axis=1, keepdims=True) for c in xc]
    rstd = [lax.rsqrt(v + GN_EPS) for v in var]
    return [c * r for c, r in zip(xc, rstd)], rstd


def _retention_operands(p_ref, rows, cos_b, sin_b, qdec, kdec):
    qr = _rotate(p_ref[rows, OFF_RQ:OFF_RQ + RET_QK_WIDTH].astype(F32), cos_b, sin_b)
    kr = _rotate(p_ref[rows, OFF_RK:OFF_RK + RET_QK_WIDTH].astype(F32), cos_b, sin_b) * RET_SCALE
    return qr, kr, qr * qdec, kr * kdec


def _tile(t, i):
    return t[:, i * LANE:(i + 1) * LANE]


def _weights_allgather(w_in_t_shard, w_out_shard):
    half_in = SHARD_IN // 2
    half_out = SHARD_OUT // 2
    assert half_in % BF16_ROWS == 0 and half_out % BF16_ROWS == 0

    def body(wi_ref, wo_ref, win_ref, wout_ref, blk_ref, blko_ref, send_sems, recv_sems):
        x, y, c = lax.axis_index("x"), lax.axis_index("y"), lax.axis_index("c")
        me, sibling = (x, y, c), (x, y, 1 - c)
        chips = [(1 - x, y), (x, 1 - y), (1 - x, 1 - y)]

        def rows_in(px, py, pc):
            return win_ref.at[pl.ds(pl.multiple_of((4 * px + 2 * py + pc) * half_in, BF16_ROWS), half_in), :]

        def rows_out(px, py, pc):
            return wout_ref.at[pl.ds(pl.multiple_of((4 * px + 2 * py + pc) * half_out, half_out), half_out), :]

        def copy(k, rows, block, to, src=None):
            return pltpu.make_async_remote_copy(
                src_ref=rows(*block) if src is None else src, dst_ref=rows(*block),
                send_sem=send_sems.at[k], recv_sem=recv_sems.at[k], device_id=to, device_id_type=MESH_ID)

        blk_ref[...] = wi_ref[pl.ds(pl.multiple_of(c * half_in, BF16_ROWS), half_in), :].astype(BF16)
        blko_ref[...] = wo_ref[pl.ds(pl.multiple_of(c * half_out, half_out), half_out), :].astype(BF16)
        rows_in(*me)[...] = blk_ref[...]
        rows_out(*me)[...] = blko_ref[...]

        sets = ((0, rows_in, blk_ref), (7, rows_out, blko_ref))
        first, passed = [], []
        for base, rows, src in sets:
            first.append(copy(base, rows, me, sibling, src=src))
            first += [copy(base + 1 + j, rows, me, (*chip, c), src=src) for j, chip in enumerate(chips)]
        for cp in first:
            cp.start()
        for base, rows, src in sets:
            for j, chip in enumerate(chips):
                copy(base + 1 + j, rows, (*chip, c), me).wait_recv()
                fwd = copy(base + 4 + j, rows, (*chip, c), sibling)
                fwd.start()
                passed.append(fwd)
        for base, rows, src in sets:
            copy(base, rows, sibling, me).wait_recv()
            for j, chip in enumerate(chips):
                copy(base + 4 + j, rows, (*chip, 1 - c), me).wait_recv()
        for cp in first + passed:
            cp.wait_send()

    vmem = pl.BlockSpec(memory_space=pltpu.VMEM)
    return _call(
        body, name="weights_allgather",
        out_shape=(jax.ShapeDtypeStruct((IN_WIDTH, D_MODEL), BF16), jax.ShapeDtypeStruct((MIX_WIDTH, D_MODEL), BF16)),
        in_specs=[vmem, vmem], out_specs=(vmem, vmem),
        scratch_shapes=[
            pltpu.VMEM((half_in, D_MODEL), BF16),
            pltpu.VMEM((half_out, D_MODEL), BF16),
            pltpu.SemaphoreType.DMA((14,)),
            pltpu.SemaphoreType.DMA((14,)),
        ],
        compiler_params=_params(32),
    )(w_in_t_shard, w_out_shard)


def _in_proj(x, norm_g, w_in, tb):
    seq = x.shape[0]

    def body(x_ref, g_ref, w_ref, p_ref, ht_ref):
        xv = x_ref[...]
        r = lax.rsqrt(jnp.mean(xv * xv, axis=1, keepdims=True) + RMS_EPS)
        h = (xv * r) * g_ref[...]
        p_ref[...] = _dot_nt(h.astype(BF16), w_ref[...]).astype(BF16)
        ht_ref[...] = h.T.astype(BF16)

    return _call(
        body, name="in_proj", grid=(seq // tb,),
        out_shape=(jax.ShapeDtypeStruct((seq, IN_WIDTH), BF16), jax.ShapeDtypeStruct((D_MODEL, seq), BF16)),
        in_specs=[pl.BlockSpec((tb, D_MODEL), lambda i: (i, 0)), _const_spec((1, D_MODEL)),
                  _const_spec((IN_WIDTH, D_MODEL))],
        out_specs=(pl.BlockSpec((tb, IN_WIDTH), lambda i: (i, 0)), pl.BlockSpec((D_MODEL, tb), lambda i: (0, i))),
        compiler_params=_params(56, ("arbitrary",)),
    )(x, norm_g, w_in)


def _mix_fwd(proj, x, target, w_out, final_g, gn_gain, sinks, tables, tb):
    seq = x.shape[0]
    nsub = tb // BLK
    cos_t, sin_t, decay_in, qdec_t, kdec_t, cd_t, bias_t = tables

    def body(p_ref, x_ref, t_ref, cos_ref, sin_ref, wout_ref, fg_ref, gain_ref, sinks_ref, din_ref, qdec_ref,
             kdec_ref, cd_ref, bias_ref, mix_ref, dxo_ref, st_ref, loss_ref, gfin_ref, oatt_ref,
             kprev_ref, vprev_ref, state_ref):
        i = pl.program_id(0)

        @pl.when(i == 0)
        def _():
            kprev_ref[...] = jnp.zeros_like(kprev_ref)
            vprev_ref[...] = jnp.zeros_like(vprev_ref)
            state_ref[...] = jnp.zeros_like(state_ref)
            loss_ref[...] = jnp.zeros_like(loss_ref)
            gfin_ref[...] = jnp.zeros_like(gfin_ref)

        def sub(j, carry):
            kp, vp, states = carry
            rows = pl.ds(pl.multiple_of(j * BLK, BLK), BLK)
            bias = bias_ref[jnp.where(jnp.logical_or(i > 0, j > 0), 1, 0)]

            aq = p_ref[rows, OFF_AQ:OFF_AQ + ATT_WIDTH]
            ak = p_ref[rows, OFF_AK:OFF_AK + ATT_KV_WIDTH].astype(F32)
            av = p_ref[rows, OFF_AV:OFF_AV + ATT_KV_WIDTH].astype(F32)
            az = p_ref[rows, OFF_AZ:OFF_AZ + ATT_WIDTH].astype(F32)
            k_ops = _kv_operands(jnp.concatenate([kp, ak], axis=0))
            v_ops = _kv_operands(jnp.concatenate([vp, av], axis=0))
            qs = [(_stack_tiles(aq, 2 * g) * ATT_SCALE).astype(BF16) for g in range(ATT_KV_HEADS)]
            p, _ = _attn_probs(qs, k_ops, bias, sinks_ref)
            o_tiles = []
            for g in range(ATT_KV_HEADS):
                p_cat = jnp.concatenate([p[2 * g].astype(BF16), p[2 * g + 1].astype(BF16)], axis=0)
                o = _dot_tn(p_cat, jnp.concatenate(v_ops[g], axis=0))
                o_tiles += [o[0:BLK], o[BLK:]]
            o_att = jnp.concatenate(o_tiles, axis=1)
            oatt_ref[rows, :] = o_att
            out = [o_att * (az * _sigmoid(az))]

            qr, kr, qd, kd = _retention_operands(p_ref, rows, cos_ref[rows, :], sin_ref[rows, :],
                                                 qdec_ref[...], kdec_ref[...])
            heads = [(t, hh) for t in range(PAIRS) for hh in range(2)]
            sc = [_dot_nt(jnp.concatenate(_split_heads(_tile(qr, t)), axis=0).astype(BF16), _tile(kr, t).astype(BF16))
                  * din_ref[t] for t in range(PAIRS)]
            qd_heads = [_split_heads(_tile(qd, t)) for t in range(PAIRS)]
            state_b = [states[t].astype(BF16) for t in range(PAIRS)]
            vs = [p_ref[rows, OFF_RV + h * RET_V_DIM:OFF_RV + (h + 1) * RET_V_DIM].astype(BF16) for h in range(RET_HEADS)]
            rzs = [p_ref[rows, OFF_RZ + h * RET_V_DIM:OFF_RZ + (h + 1) * RET_V_DIM].astype(F32) for h in range(RET_HEADS)]
            lhs = [jnp.concatenate([sc[t][hh * BLK:(hh + 1) * BLK].astype(BF16), qd_heads[t][hh].astype(BF16)], axis=1)
                   for t, hh in heads]
            ons, _ = _group_norm_all([_dot(lhs[2 * t + hh], jnp.concatenate([vs[2 * t + hh], state_b[t]], axis=0))
                                      for t, hh in heads])
            out += [(ons[h] * gain_ref[:, h * RET_V_DIM:(h + 1) * RET_V_DIM]) * (rzs[h] * _sigmoid(rzs[h]))
                    for h in range(RET_HEADS)]
            new_states = [states[t] * cd_ref[t]
                          + _dot_tn(jnp.concatenate(_split_heads(_tile(kd, t)), axis=0).astype(BF16),
                                    jnp.concatenate([vs[2 * t], vs[2 * t + 1]], axis=0)) for t in range(PAIRS)]
            mix_ref[rows, :] = jnp.concatenate(out, axis=1).astype(BF16)
            st_ref[j] = jnp.stack(states)
            return ak, av, tuple(new_states)

        carry = (kprev_ref[...], vprev_ref[...], tuple(state_ref[t] for t in range(PAIRS)))
        kp, vp, states = lax.fori_loop(0, nsub, sub, carry)
        kprev_ref[...] = kp
        vprev_ref[...] = vp
        state_ref[...] = jnp.stack(states)

        xo = x_ref[...] + _dot(mix_ref[...], wout_ref[...])
        r2 = lax.rsqrt(jnp.mean(xo * xo, axis=1, keepdims=True) + RMS_EPS)
        xn = xo * r2
        err = xn * fg_ref[...] - t_ref[...]
        loss_ref[...] += jnp.sum(err * err) * (0.5 / D_MODEL)
        dy = err * (1.0 / D_MODEL)
        gfin_ref[...] += jnp.sum(dy * xn, axis=0, keepdims=True)
        u = dy * fg_ref[...]
        dxo_ref[...] = r2 * u - xn * (r2 * jnp.mean(u * xn, axis=1, keepdims=True))

    blk_rows = lambda w: pl.BlockSpec((tb, w), lambda i: (i, 0))
    state_shape = (PAIRS, 2 * RET_QK_DIM, RET_V_DIM)
    return _call(
        body, name="mix_fwd", grid=(seq // tb,),
        out_shape=(
            jax.ShapeDtypeStruct((seq, MIX_WIDTH), BF16),
            jax.ShapeDtypeStruct((seq, D_MODEL), F32),
            jax.ShapeDtypeStruct((seq // BLK,) + state_shape, F32),
            jax.ShapeDtypeStruct((8, LANE), F32),
            jax.ShapeDtypeStruct((1, D_MODEL), F32),
            jax.ShapeDtypeStruct((seq, ATT_WIDTH), F32),
        ),
        in_specs=[
            blk_rows(IN_WIDTH), blk_rows(D_MODEL), blk_rows(D_MODEL), blk_rows(LANE), blk_rows(LANE),
            _const_spec((MIX_WIDTH, D_MODEL)), _const_spec((1, D_MODEL)), _const_spec((1, RET_WIDTH)),
            pl.BlockSpec(memory_space=pltpu.SMEM),
            _const_spec((PAIRS, 2 * BLK, BLK)), _const_spec((BLK, RET_QK_WIDTH)), _const_spec((BLK, RET_QK_WIDTH)),
            _const_spec(state_shape), _const_spec((2, 2 * BLK, 2 * BLK)),
        ],
        out_specs=(
            blk_rows(MIX_WIDTH), blk_rows(D_MODEL),
            pl.BlockSpec((nsub,) + state_shape, lambda i: (i, 0, 0, 0)),
            _const_spec((8, LANE)), _const_spec((1, D_MODEL)), blk_rows(ATT_WIDTH),
        ),
        scratch_shapes=[
            pltpu.VMEM((BLK, ATT_KV_WIDTH), F32), pltpu.VMEM((BLK, ATT_KV_WIDTH), F32),
            pltpu.VMEM(state_shape, F32),
        ],
        compiler_params=_params(48, ("arbitrary",)),
    )(proj, x, target, cos_t, sin_t, w_out, final_g, gn_gain, sinks, decay_in, qdec_t, kdec_t, cd_t, bias_t)


def _mix_bwd(proj, dxo, mix, o_att, states, w_out, gn_gain, sinks, tables, tb):
    seq = dxo.shape[0]
    nsub = tb // BLK
    nblk = seq // tb
    cos_t, sin_t, decay_in, qdec_t, kdec_t, cd_t, bias_t = tables
    kv_cols = OFF_AK // (2 * ATT_KV_WIDTH)
    state_shape = (PAIRS, 2 * RET_QK_DIM, RET_V_DIM)

    def body(p_ref, pkv_ref, dxo_ref, mix_ref, oatt_ref, st_ref, cos_ref, sin_ref, wout_ref, gain_ref, sinks_ref, din_ref,
             qdec_ref, kdec_ref, cd_ref, bias_ref, dp_ref, gwout_ref, dgain_ref, dsink_ref,
             dmix_ref, kv_ref, dkc_ref, dvc_ref, gst_ref):
        i = pl.program_id(0)
        blk = nblk - 1 - i

        @pl.when(i == 0)
        def _():
            gwout_ref[...] = jnp.zeros_like(gwout_ref)
            dgain_ref[...] = jnp.zeros_like(dgain_ref)
            dsink_ref[...] = jnp.zeros_like(dsink_ref)
            dkc_ref[...] = jnp.zeros_like(dkc_ref)
            dvc_ref[...] = jnp.zeros_like(dvc_ref)
            gst_ref[...] = jnp.zeros_like(gst_ref)

        dxo_b = dxo_ref[...].astype(BF16)
        dmix_ref[...] = _dot_nt(dxo_b, wout_ref[...])
        gwout_ref[...] += _dot_tn(mix_ref[...], dxo_b)
        kv_ref[0:BLK, :] = pkv_ref[...].astype(F32)
        kv_ref[BLK:, :] = p_ref[:, OFF_AK:OFF_AK + 2 * ATT_KV_WIDTH].astype(F32)
        low = _low_lanes((BLK, LANE))
        low2 = _low_lanes((2 * BLK, LANE))
        lane = lax.broadcasted_iota(jnp.int32, (1, LANE), 1)

        def sub(jj, carry):
            dkc, dvc, gsts, dgain, dsink = carry
            j = nsub - 1 - jj
            rows = pl.ds(pl.multiple_of(j * BLK, BLK), BLK)
            both = pl.ds(pl.multiple_of(j * BLK, BLK), 2 * BLK)
            bias = bias_ref[jnp.where(jnp.logical_or(blk > 0, j > 0), 1, 0)]

            aq = p_ref[rows, OFF_AQ:OFF_AQ + ATT_WIDTH]
            az = p_ref[rows, OFF_AZ:OFF_AZ + ATT_WIDTH].astype(F32)
            k_ops = _kv_operands(kv_ref[both, 0:ATT_KV_WIDTH])
            v_ops = _kv_operands(kv_ref[both, ATT_KV_WIDTH:2 * ATT_KV_WIDTH])
            da = dmix_ref[rows, 0:ATT_WIDTH]
            sig = _sigmoid(az)
            d_o = da * (az * sig)
            qs = [(_stack_tiles(aq, 2 * g) * ATT_SCALE).astype(BF16) for g in range(ATT_KV_HEADS)]
            dos = [_stack_tiles(d_o, 2 * g).astype(BF16) for g in range(ATT_KV_HEADS)]
            p, p_sink = _attn_probs(qs, k_ops, bias, sinks_ref)
            dpr = [_dot_nt(v_ops[g][hi], dos[g]) for g, hi in ATT_PROBLEMS]
            delta = [jnp.sum(pi * di, axis=0, keepdims=True) for pi, di in zip(p, dpr)]
            ds = [(pi * (di - ti)).astype(BF16) for pi, di, ti in zip(p, dpr, delta)]
            for (g, hi), ki, ti in zip(ATT_PROBLEMS, p_sink, delta):
                sink_part = ki * ti
                for half in range(2):
                    tot = jnp.sum(sink_part[:, half * BLK:(half + 1) * BLK], axis=1, keepdims=True)
                    dsink = dsink - jnp.where(lane == 4 * g + 2 * half + hi, tot, 0.0)
            dq_tiles, dk_sums, dv_sums = [], [], []
            for g in range(ATT_KV_HEADS):
                ds_cat = jnp.concatenate([ds[2 * g], ds[2 * g + 1]], axis=0)
                p_cat = jnp.concatenate([p[2 * g].astype(BF16), p[2 * g + 1].astype(BF16)], axis=0)
                dqs = _dot_tn(ds_cat, jnp.concatenate(k_ops[g], axis=0)) * ATT_SCALE
                dq_tiles += [dqs[0:BLK], dqs[BLK:]]
                dk_sums.append(_dot(ds_cat, qs[g]))
                dv_sums.append(_dot(p_cat, dos[g]))
            daz = da * oatt_ref[rows, :] * (sig * (1.0 + az * (1.0 - sig)))

            def kv_grad(sums):
                (a0, b0), (a1, b1) = [(s[0:2 * BLK], s[2 * BLK:]) for s in sums]
                return jnp.where(low2, a0, b1) + pltpu.roll(jnp.where(low2, a1, b0), HALF_LANE, 1)

            dk_both, dv_both = kv_grad(dk_sums), kv_grad(dv_sums)
            dak = dk_both[BLK:] + dkc
            dav = dv_both[BLK:] + dvc

            cos_b, sin_b = cos_ref[rows, :], sin_ref[rows, :]
            qdec, kdec = qdec_ref[...], kdec_ref[...]
            qr, kr, qd, kd = _retention_operands(p_ref, rows, cos_b, sin_b, qdec, kdec)
            heads = [(t, hh) for t in range(PAIRS) for hh in range(2)]
            head_cols = [slice(h * RET_V_DIM, (h + 1) * RET_V_DIM) for h in range(RET_HEADS)]
            q_rows = [jnp.concatenate(_split_heads(_tile(qr, t)), axis=0).astype(BF16) for t in range(PAIRS)]
            k_rows = [jnp.concatenate(_split_heads(_tile(kr, t)), axis=0).astype(BF16) for t in range(PAIRS)]
            din = [din_ref[t] for t in range(PAIRS)]
            sc = [(_dot_nt(q_rows[t], _tile(kr, t).astype(BF16)) * din[t]).astype(BF16) for t in range(PAIRS)]
            qd_heads = [_split_heads(_tile(qd, t)) for t in range(PAIRS)]
            kd_heads = [_split_heads(_tile(kd, t)) for t in range(PAIRS)]
            state_b = [st_ref[j, t].astype(BF16) for t in range(PAIRS)]
            gst_b = [gsts[t].astype(BF16) for t in range(PAIRS)]
            vs = [p_ref[rows, OFF_RV + h * RET_V_DIM:OFF_RV + (h + 1) * RET_V_DIM].astype(BF16) for h in range(RET_HEADS)]
            rzs = [p_ref[rows, OFF_RZ + h * RET_V_DIM:OFF_RZ + (h + 1) * RET_V_DIM].astype(F32) for h in range(RET_HEADS)]
            drs = [dmix_ref[rows, ATT_WIDTH + h * RET_V_DIM:ATT_WIDTH + (h + 1) * RET_V_DIM] for h in range(RET_HEADS)]
            gains = [gain_ref[:, c] for c in head_cols]
            lhs = [jnp.concatenate([sc[t][hh * BLK:(hh + 1) * BLK], qd_heads[t][hh].astype(BF16)], axis=1) for t, hh in heads]
            rhs = [jnp.concatenate([vs[2 * t + hh], state_b[t]], axis=0) for t, hh in heads]
            ons, rstds = _group_norm_all([_dot(l, r) for l, r in zip(lhs, rhs)])
            sig_r = [_sigmoid(z) for z in rzs]
            dgn = [d * (z * g) for d, z, g in zip(drs, rzs, sig_r)]
            dz_parts = [d * (o * gn) * (g * (1.0 + z * (1.0 - g))) for d, o, gn, g, z in zip(drs, ons, gains, sig_r, rzs)]
            dgain_parts = [jnp.sum(d * o, axis=0, keepdims=True) for d, o in zip(dgn, ons)]
            don = [d * gn for d, gn in zip(dgn, gains)]
            mean_don = [jnp.mean(d, axis=1, keepdims=True) for d in don]
            mean_don_on = [jnp.mean(d * o, axis=1, keepdims=True) for d, o in zip(don, ons)]
            dob = [(r * (d - a - o * b)).astype(BF16) for r, d, a, o, b in zip(rstds, don, mean_don, ons, mean_don_on)]
            dlhs = [_dot_nt(d, r) for d, r in zip(dob, rhs)]
            drhs = [_dot_tn(l, d) for l, d in zip(lhs, dob)]
            dkds = [_dot_nt(vs[2 * t + hh], gst_b[t]) for t, hh in heads]
            dv_parts = [drhs[2 * t + hh][0:BLK] + _dot(kd_heads[t][hh].astype(BF16), gst_b[t]) for t, hh in heads]
            das = [(dlhs[2 * t + hh][:, 0:BLK] * din[t][hh * BLK:(hh + 1) * BLK]).astype(BF16) for t, hh in heads]
            new_gsts = [gsts[t] * cd_ref[t] + drhs[2 * t][BLK:] + drhs[2 * t + 1][BLK:] for t in range(PAIRS)]
            dq_parts = [_dot(jnp.concatenate([das[2 * t], das[2 * t + 1]], axis=1), k_rows[t])
                        + jnp.where(low, dlhs[2 * t][:, BLK:], dlhs[2 * t + 1][:, BLK:]) * _tile(qdec, t)
                        for t in range(PAIRS)]
            dk_parts = [_dot_tn(jnp.concatenate([das[2 * t], das[2 * t + 1]], axis=0), q_rows[t])
                        + jnp.where(low, dkds[2 * t], dkds[2 * t + 1]) * _tile(kdec, t) for t in range(PAIRS)]
            drq = _rotate_transposed(jnp.concatenate(dq_parts, axis=1), cos_b, sin_b)
            drk = _rotate_transposed(jnp.concatenate(dk_parts, axis=1) * RET_SCALE, cos_b, sin_b)

            dp_ref[rows, :] = jnp.concatenate(
                [jnp.concatenate(dq_tiles, axis=1), dak, dav, daz, drq, drk] + dv_parts + dz_parts, axis=1).astype(BF16)
            dgain = dgain + jnp.concatenate(dgain_parts, axis=1)
            return dk_both[0:BLK], dv_both[0:BLK], tuple(new_gsts), dgain, dsink

        carry = (dkc_ref[...], dvc_ref[...], tuple(gst_ref[t] for t in range(PAIRS)), dgain_ref[...], dsink_ref[...])
        dkc, dvc, gsts, dgain, dsink = lax.fori_loop(0, nsub, sub, carry)
        dkc_ref[...] = dkc
        dvc_ref[...] = dvc
        gst_ref[...] = jnp.stack(gsts)
        dgain_ref[...] = dgain
        dsink_ref[...] = dsink

    rev_rows = lambda w: pl.BlockSpec((tb, w), lambda i: (nblk - 1 - i, 0))
    prev_kv = pl.BlockSpec((BLK, 2 * ATT_KV_WIDTH), lambda i: (jnp.maximum((nblk - 1 - i) * nsub - 1, 0), kv_cols))
    return _call(
        body, name="mix_bwd", grid=(nblk,),
        out_shape=(
            jax.ShapeDtypeStruct((seq, IN_WIDTH), BF16),
            jax.ShapeDtypeStruct((MIX_WIDTH, D_MODEL), F32),
            jax.ShapeDtypeStruct((1, RET_WIDTH), F32),
            jax.ShapeDtypeStruct((1, LANE), F32),
        ),
        in_specs=[
            rev_rows(IN_WIDTH), prev_kv, rev_rows(D_MODEL), rev_rows(MIX_WIDTH), rev_rows(ATT_WIDTH),
            pl.BlockSpec((nsub,) + state_shape, lambda i: (nblk - 1 - i, 0, 0, 0)),
            rev_rows(LANE), rev_rows(LANE),
            _const_spec((MIX_WIDTH, D_MODEL)), _const_spec((1, RET_WIDTH)),
            pl.BlockSpec(memory_space=pltpu.SMEM),
            _const_spec((PAIRS, 2 * BLK, BLK)), _const_spec((BLK, RET_QK_WIDTH)), _const_spec((BLK, RET_QK_WIDTH)),
            _const_spec(state_shape), _const_spec((2, 2 * BLK, 2 * BLK)),
        ],
        out_specs=(
            rev_rows(IN_WIDTH), _const_spec((MIX_WIDTH, D_MODEL)), _const_spec((1, RET_WIDTH)), _const_spec((1, LANE)),
        ),
        scratch_shapes=[
            pltpu.VMEM((tb, MIX_WIDTH), F32),
            pltpu.VMEM((tb + BLK, 2 * ATT_KV_WIDTH), F32),
            pltpu.VMEM((BLK, ATT_KV_WIDTH), F32), pltpu.VMEM((BLK, ATT_KV_WIDTH), F32),
            pltpu.VMEM(state_shape, F32),
        ],
        compiler_params=_params(56, ("arbitrary",)),
    )(proj, proj, dxo, mix, o_att, states, cos_t, sin_t, w_out, gn_gain, sinks, decay_in, qdec_t, kdec_t, cd_t, bias_t)


def _in_proj_bwd(dproj, x, dxo, norm_g, w_in, tb):
    seq = x.shape[0]

    def body(dp_ref, x_ref, dxo_ref, g_ref, w_ref, gx_ref, gnorm_ref):
        i = pl.program_id(0)

        @pl.when(i == 0)
        def _():
            gnorm_ref[...] = jnp.zeros_like(gnorm_ref)

        xv = x_ref[...]
        r = lax.rsqrt(jnp.mean(xv * xv, axis=1, keepdims=True) + RMS_EPS)
        xn = xv * r
        dh = _dot(dp_ref[...], w_ref[...])
        gnorm_ref[...] += jnp.sum(dh * xn, axis=0, keepdims=True)
        u = dh * g_ref[...]
        gx_ref[...] = dxo_ref[...] + r * u - xn * (r * jnp.mean(u * xn, axis=1, keepdims=True))

    rows = lambda w: pl.BlockSpec((tb, w), lambda i: (i, 0))
    return _call(
        body, name="in_proj_bwd", grid=(seq // tb,),
        out_shape=(jax.ShapeDtypeStruct((seq, D_MODEL), F32), jax.ShapeDtypeStruct((1, D_MODEL), F32)),
        in_specs=[rows(IN_WIDTH), rows(D_MODEL), rows(D_MODEL), _const_spec((1, D_MODEL)),
                  _const_spec((IN_WIDTH, D_MODEL))],
        out_specs=(rows(D_MODEL), _const_spec((1, D_MODEL))),
        compiler_params=_params(48, ("arbitrary",)),
    )(dproj, x, dxo, norm_g, w_in)


def _gw_in_reduce(h_t, dproj, tb):
    seq = dproj.shape[0]
    nblk = seq // tb
    last = nblk - 1
    hand_on = min(1, last)
    half = D_MODEL // 2
    A, B, C, N_SEMS = 0, N_CHIPS, 2 * N_CHIPS, 2 * N_CHIPS + 1

    def body(win_ref, ht_ref, dp_ref, out_ref, acc, sib, send_buf, b_in, fin, send_sems, recv_sems):
        p, i = pl.program_id(0), pl.program_id(1)
        x, y, c = lax.axis_index("x"), lax.axis_index("y"), lax.axis_index("c")
        chip = 2 * x + y
        sibling = (x, y, 1 - c)
        mine = pl.ds(pl.multiple_of(c * half, half), half)
        other = pl.ds(pl.multiple_of((1 - c) * half, half), half)

        def remote(src, dst, send_k, recv_k, to):
            return pltpu.make_async_remote_copy(src_ref=src, dst_ref=dst, send_sem=send_sems.at[send_k],
                                                recv_sem=recv_sems.at[recv_k], device_id=to, device_id_type=MESH_ID)

        part = _dot(ht_ref[...], dp_ref[...])
        slot = p % 2

        @pl.when(i == 0)
        def _():
            acc[slot] = part

        @pl.when(i > 0)
        def _():
            acc[slot] += part

        for q in range(N_CHIPS):
            s = q % 2
            to_sibling = remote(acc.at[s, other, :], sib.at[s], A + q, A + q, sibling)

            @pl.when(jnp.logical_and(p == q, i == last))
            def _():
                to_sibling.start()

            if q < N_CHIPS - 1:
                dest = (chip + 1 + q) % N_CHIPS

                @pl.when(jnp.logical_and(p == q + 1, i == hand_on))
                def _():
                    to_sibling.wait_recv()
                    send_buf[q] = (acc[s, mine, :] + sib[s]).astype(BF16)
                    remote(send_buf.at[q], b_in.at[chip], B + q, B + chip, (dest // 2, dest % 2, c)).start()
                    to_sibling.wait_send()
            else:
                @pl.when(jnp.logical_and(p == q, i == last))
                def _():
                    to_sibling.wait_recv()
                    fin[mine, :] = acc[s, mine, :] + sib[s]
                    for j in range(N_CHIPS):
                        @pl.when(j != chip)
                        def _():
                            remote(b_in.at[j], b_in.at[j], B + j, B + j, sibling).wait_recv()
                            fin[mine, :] += b_in[j].astype(F32)
                    to_core = remote(fin.at[mine, :], fin.at[mine, :], C, C, sibling)
                    to_core.start()
                    remote(fin.at[other, :], fin.at[other, :], C, C, sibling).wait_recv()
                    out_ref[...] = fin[...]
                    to_core.wait_send()
                    to_sibling.wait_send()
                    for k in range(N_CHIPS - 1):
                        remote(send_buf.at[k], b_in.at[chip], B + k, B + k, sibling).wait_send()

    grid_spec = pltpu.PrefetchScalarGridSpec(
        num_scalar_prefetch=1, grid=(N_CHIPS, nblk),
        in_specs=[pl.BlockSpec((D_MODEL, tb), lambda p, i, win: (0, i)),
                  pl.BlockSpec((pl.Element(tb), pl.Element(SHARD_PAD)),
                               lambda p, i, win: (i * tb, pl.multiple_of(win[p] * LANE, LANE)))],
        out_specs=pl.BlockSpec((D_MODEL, SHARD_PAD), lambda p, i, win: (0, 0), pipeline_mode=pl.Buffered(1)),
        scratch_shapes=[
            pltpu.VMEM((2, D_MODEL, SHARD_PAD), F32), pltpu.VMEM((2, half, SHARD_PAD), F32),
            pltpu.VMEM((N_CHIPS - 1, half, SHARD_PAD), BF16), pltpu.VMEM((N_CHIPS, half, SHARD_PAD), BF16),
            pltpu.VMEM((D_MODEL, SHARD_PAD), F32),
            pltpu.SemaphoreType.DMA((N_SEMS,)), pltpu.SemaphoreType.DMA((N_SEMS,)),
        ])
    chip = 2 * lax.axis_index("x") + lax.axis_index("y")
    owner = (chip + 1 + jnp.arange(N_CHIPS, dtype=jnp.int32)) % N_CHIPS
    win_start = (owner * SHARD_IN) // LANE
    return _call(
        body, name="gw_in_reduce", grid_spec=grid_spec,
        out_shape=jax.ShapeDtypeStruct((D_MODEL, SHARD_PAD), F32),
        compiler_params=_params(40, ("arbitrary", "arbitrary")),
    )(win_start.astype(jnp.int32), h_t, dproj)


def _grads_reduce(gw_out, small):
    half_out = SHARD_OUT // 2
    A_OUT, B_OUT, C_OUT, PACK, N_SEMS = 0, 4, 8, 9, 17
    assert len(small) == PACK_PARTS

    def body(gwo_hbm, s0_ref, s1_ref, s2_ref, s3_ref, s4_ref, fout_ref, packsum_ref,
             mine_out, sib_out, send_out, b_out, pack_ref, packs, send_sems, recv_sems, local_sems):
        x, y, c = lax.axis_index("x"), lax.axis_index("y"), lax.axis_index("c")
        chip = 2 * x + y
        dev = 2 * chip + c
        sibling = (x, y, 1 - c)
        pack_ref[...] = jnp.zeros_like(pack_ref)
        for k, s_ref in enumerate((s0_ref, s1_ref, s2_ref, s3_ref, s4_ref)):
            pack_ref[k:k + 1, 0:s_ref.shape[1]] = s_ref[0:1, :]

        def remote(src, dst, send_k, recv_k, to):
            return pltpu.make_async_remote_copy(src_ref=src, dst_ref=dst, send_sem=send_sems.at[send_k],
                                                recv_sem=recv_sems.at[recv_k], device_id=to, device_id_type=MESH_ID)

        def out_rows(j, core):
            return pl.ds(pl.multiple_of(j * SHARD_OUT + core * half_out, half_out), half_out)

        packs[dev] = pack_ref[...]
        for d in range(N_DEV):
            to = (d // 4, (d // 2) % 2, d % 2)

            @pl.when(d != dev)
            def _():
                remote(pack_ref, packs.at[dev], PACK + d, PACK + dev, to).start()

        local = [pltpu.make_async_copy(gwo_hbm.at[out_rows(j, c), :], mine_out.at[j], local_sems.at[j])
                 for j in range(N_CHIPS)]
        for cp in local:
            cp.start()
        stage_a = [remote(gwo_hbm.at[out_rows(j, 1 - c), :], sib_out.at[j], A_OUT + j, A_OUT + j, sibling)
                   for j in range(N_CHIPS)]
        for cp in stage_a:
            cp.start()
        for cp in local:
            cp.wait()
        for cp in stage_a:
            cp.wait_recv()
        for j in range(N_CHIPS):
            mine_out[j] = mine_out[j] + sib_out[j]

        my_out_rows = pl.ds(pl.multiple_of(c * half_out, half_out), half_out)
        for j in range(N_CHIPS):
            to = (j // 2, j % 2, c)

            @pl.when(j != chip)
            def _():
                send_out[j] = mine_out[j].astype(BF16)
                remote(send_out.at[j], b_out.at[chip], B_OUT + j, B_OUT + chip, to).start()

            @pl.when(j == chip)
            def _():
                fout_ref[my_out_rows, :] = mine_out[j]

        for j in range(N_CHIPS):
            @pl.when(j != chip)
            def _():
                remote(b_out.at[j], b_out.at[j], B_OUT + j, B_OUT + j, sibling).wait_recv()
                fout_ref[my_out_rows, :] += b_out[j].astype(F32)

        mine_half_out = fout_ref.at[my_out_rows, :]
        stage_c = [remote(mine_half_out, mine_half_out, C_OUT, C_OUT, sibling)]
        for cp in stage_c:
            cp.start()
        other_half_out = fout_ref.at[pl.ds(pl.multiple_of((1 - c) * half_out, half_out), half_out), :]
        remote(other_half_out, other_half_out, C_OUT, C_OUT, sibling).wait_recv()

        for d in range(N_DEV):
            @pl.when(d != dev)
            def _():
                remote(pack_ref, packs.at[d], PACK + d, PACK + d, sibling).wait_recv()
        total = packs[0]
        for d in range(1, N_DEV):
            total = total + packs[d]
        packsum_ref[...] = total

        for cp in stage_a + stage_c:
            cp.wait_send()
        for j in range(N_CHIPS):
            @pl.when(j != chip)
            def _():
                remote(b_out.at[j], b_out.at[j], B_OUT + j, B_OUT + j, sibling).wait_send()
        for d in range(N_DEV):
            @pl.when(d != dev)
            def _():
                remote(pack_ref, packs.at[d], PACK + d, PACK + d, sibling).wait_send()

    vmem = pl.BlockSpec(memory_space=pltpu.VMEM)
    hbm = pl.BlockSpec(memory_space=pl.ANY)
    return _call(
        body, name="grads_reduce",
        out_shape=(jax.ShapeDtypeStruct((SHARD_OUT, D_MODEL), F32), jax.ShapeDtypeStruct((PACK_ROWS, D_MODEL), F32)),
        in_specs=[hbm] + [vmem] * PACK_PARTS, out_specs=(vmem, vmem),
        scratch_shapes=[
            pltpu.VMEM((N_CHIPS, half_out, D_MODEL), F32), pltpu.VMEM((N_CHIPS, half_out, D_MODEL), F32),
            pltpu.VMEM((N_CHIPS, half_out, D_MODEL), BF16), pltpu.VMEM((N_CHIPS, half_out, D_MODEL), BF16),
            pltpu.VMEM((PACK_ROWS, D_MODEL), F32), pltpu.VMEM((N_DEV, PACK_ROWS, D_MODEL), F32),
            pltpu.SemaphoreType.DMA((N_SEMS,)), pltpu.SemaphoreType.DMA((N_SEMS,)),
            pltpu.SemaphoreType.DMA((N_CHIPS,)),
        ],
        compiler_params=_params(24),
    )(gw_out, *small)


def _adam_math(w, g, m, v):
    mn = ADAM_B1 * m + (1.0 - ADAM_B1) * g
    vn = ADAM_B2 * v + (1.0 - ADAM_B2) * (g * g)
    m_hat = mn / (1.0 - ADAM_B1 ** ADAM_STEP)
    v_hat = vn / (1.0 - ADAM_B2 ** ADAM_STEP)
    return -ADAM_LR * (m_hat / (jnp.sqrt(v_hat) + ADAM_EPS) + ADAM_WD * w), mn, vn


def _adamw(name, w, g, m, v, tb):
    rows, cols = w.shape

    def body(w_ref, g_ref, m_ref, v_ref, go_ref, d_ref, mo_ref, vo_ref):
        gv = g_ref[...]
        go_ref[...] = gv
        d_ref[...], mo_ref[...], vo_ref[...] = _adam_math(w_ref[...], gv, m_ref[...], v_ref[...])

    spec = pl.BlockSpec((tb, cols), lambda i: (i, 0))
    shape = jax.ShapeDtypeStruct((rows, cols), F32)
    return _call(
        body, name=name, grid=(rows // tb,), out_shape=(shape,) * 4,
        in_specs=[spec] * 4, out_specs=(spec,) * 4,
        compiler_params=_params(32, ("arbitrary",)),
    )(w, g, m, v)


def _adamw_w_in(w_t, g_window, m_t, v_t, tb):
    def body(w_ref, g_ref, m_ref, v_ref, go_ref, d_ref, mo_ref, vo_ref, gt_ref):
        gt_ref[...] = g_ref[...].T
        gv = gt_ref[pl.ds(pl.multiple_of(lax.axis_index("y") * SHARD_SHIFT, SHARD_SHIFT), SHARD_IN), :]
        go_ref[...] = gv
        d_ref[...], mo_ref[...], vo_ref[...] = _adam_math(w_ref[...], gv, m_ref[...], v_ref[...])

    spec = pl.BlockSpec((SHARD_IN, tb), lambda i: (0, i))
    shape = jax.ShapeDtypeStruct((SHARD_IN, D_MODEL), F32)
    return _call(
        body, name="adamw_w_in", grid=(D_MODEL // tb,), out_shape=(shape,) * 4,
        in_specs=[spec, pl.BlockSpec((tb, SHARD_PAD), lambda i: (i, 0)), spec, spec], out_specs=(spec,) * 4,
        scratch_shapes=[pltpu.VMEM((SHARD_PAD, tb), F32)],
        compiler_params=_params(32, ("arbitrary",)),
    )(w_t, g_window, m_t, v_t)


def _adamw_small(sums, params):
    def body(sums_ref, *refs):
        ins, outs = refs[:3 * len(params)], refs[3 * len(params):]
        for k in range(len(params)):
            w_ref, m_ref, v_ref = ins[3 * k:3 * k + 3]
            g = sums_ref[k:k + 1, 0:w_ref.shape[1]]
            go_ref, d_ref, mo_ref, vo_ref = outs[4 * k:4 * k + 4]
            go_ref[...] = g
            d_ref[...], mo_ref[...], vo_ref[...] = _adam_math(w_ref[...], g, m_ref[...], v_ref[...])

    vmem = pl.BlockSpec(memory_space=pltpu.VMEM)
    flat = [a for p in params for a in p]
    shapes = tuple(jax.ShapeDtypeStruct(p[0].shape, F32) for p in params for _ in range(4))
    res = _call(body, name="adamw_small", out_shape=shapes, in_specs=[vmem] * (1 + len(flat)),
                out_specs=(vmem,) * len(shapes), compiler_params=_params(16))(sums, *flat)
    return [res[4 * k:4 * k + 4] for k in range(len(params))]


def kernel(x, norm_g, w_in, att_sinks, ret_gn_g, w_out, final_g, loss_target, m_norm_g, m_w_in, m_att_sinks, m_ret_gn_g, m_w_out, m_final_g, v_norm_g, v_w_in, v_att_sinks, v_ret_gn_g, v_w_out, v_final_g):
    seq = x.shape[1]
    xs, tgt = x[0], loss_target[0]
    final_g2 = final_g.reshape(1, D_MODEL)
    tables = _tables(seq)

    w_in_t, m_w_in_t, v_w_in_t = w_in[0].T, m_w_in[0].T, v_w_in[0].T
    w_in_full, w_out_full = _weights_allgather(w_in_t, w_out[0])
    proj, h_t = _in_proj(xs, norm_g, w_in_full, min(TOKENS_PROJ, seq))
    mix, dxo, states, loss_part, gfin, o_att = _mix_fwd(proj, xs, tgt, w_out_full, final_g2, ret_gn_g, att_sinks,
                                                        tables, min(TOKENS_MIX, seq))
    dproj, gw_out, dgain, dsink = _mix_bwd(proj, dxo, mix, o_att, states, w_out_full, ret_gn_g, att_sinks, tables,
                                           min(TOKENS_MIX, seq))
    g_in = _gw_in_reduce(h_t, dproj, min(TOKENS_GW, seq))
    grad_x, gnorm = _in_proj_bwd(dproj, xs, dxo, norm_g, w_in_full, min(TOKENS_PROJ_BWD, seq))

    g_out, sums = _grads_reduce(gw_out, (gnorm, gfin, dgain, dsink, loss_part))

    res_in = [r.T for r in _adamw_w_in(w_in_t, g_in, m_w_in_t, v_w_in_t, 256)]
    res_out = _adamw("adamw_w_out", w_out[0], g_out, m_w_out[0], v_w_out[0], SHARD_OUT)
    as_row = lambda a: a.reshape(1, D_MODEL)
    r_norm, r_final, r_gain, r_sink = _adamw_small(sums, [
        (norm_g, m_norm_g, v_norm_g), (final_g2, as_row(m_final_g), as_row(v_final_g)),
        (ret_gn_g, m_ret_gn_g, v_ret_gn_g), (att_sinks, m_att_sinks, v_att_sinks)])

    outs = []
    for k in range(4):
        outs += [r_norm[k], res_in[k][None], r_sink[k], r_gain[k], res_out[k][None], r_final[k].reshape(D_MODEL)]
    return (sums[4, 0], grad_x[None], *outs)
```

```python
import jax
import jax.numpy as jnp
import numpy as np
from jax import lax
from jax.experimental import pallas as pl
from jax.experimental.pallas import tpu as pltpu

F32 = jnp.float32
BF16 = jnp.bfloat16

D_MODEL = 1024
ATT_HEADS = 8
ATT_KV_HEADS = 2
ATT_HEAD_DIM = 64
RET_HEADS = 4
RET_QK_DIM = 64
RET_V_DIM = 128
BLK = 128
ROT_BASE = 10000.0
RMS_EPS = 1e-6
GN_EPS = 1e-6
NEG_INF = -1e30
ATT_SCALE = ATT_HEAD_DIM ** -0.5
RET_SCALE = RET_QK_DIM ** -0.5

ATT_WIDTH = ATT_HEADS * ATT_HEAD_DIM
ATT_KV_WIDTH = ATT_KV_HEADS * ATT_HEAD_DIM
RET_QK_WIDTH = RET_HEADS * RET_QK_DIM
RET_WIDTH = RET_HEADS * RET_V_DIM
MIX_WIDTH = ATT_WIDTH + RET_WIDTH
OFF_AQ = 0
OFF_AK = OFF_AQ + ATT_WIDTH
OFF_AV = OFF_AK + ATT_KV_WIDTH
OFF_AZ = OFF_AV + ATT_KV_WIDTH
OFF_RQ = OFF_AZ + ATT_WIDTH
OFF_RK = OFF_RQ + RET_QK_WIDTH
OFF_RV = OFF_RK + RET_QK_WIDTH
OFF_RZ = OFF_RV + RET_WIDTH
IN_WIDTH = OFF_RZ + RET_WIDTH

LANE = 128
BF16_ROWS = 16
HALF_LANE = LANE // 2
PAIRS = RET_QK_WIDTH // LANE
assert ATT_HEAD_DIM == HALF_LANE and RET_QK_DIM == HALF_LANE and RET_V_DIM == LANE and ATT_KV_WIDTH == LANE

N_CHIPS = 4
N_DEV = 8
SHARD_IN = IN_WIDTH // N_CHIPS
SHARD_PAD = 768
SHARD_SHIFT = SHARD_PAD - SHARD_IN
WIN_START = tuple((j * SHARD_IN) // LANE * LANE for j in range(N_CHIPS))
SHARD_OUT = MIX_WIDTH // N_CHIPS

PACK_PARTS = 5
PACK_ROWS = 8
assert PACK_PARTS <= PACK_ROWS

ADAM_LR = 0.001
ADAM_B1 = 0.9
ADAM_B2 = 0.999
ADAM_EPS = 1e-08
ADAM_WD = 0.01
ADAM_STEP = 10

VMEM_CAP = 64 * 1024 * 1024
TOKENS_PROJ = 1024
TOKENS_MIX = 512
TOKENS_PROJ_BWD = 512
TOKENS_GW = 2048
MESH_ID = pl.DeviceIdType.MESH


def _call(body, **kw):
    return pl.pallas_call(body, **kw)


def _params(vmem_mb, semantics=None):
    assert vmem_mb * 1024 * 1024 < VMEM_CAP
    return pltpu.CompilerParams(dimension_semantics=semantics, vmem_limit_bytes=vmem_mb * 1024 * 1024)


def _dot(a, b):
    return jnp.dot(a, b, preferred_element_type=F32)


def _dot_nt(a, b):
    return lax.dot_general(a, b, (((1,), (1,)), ((), ())), preferred_element_type=F32)


def _dot_tn(a, b):
    return lax.dot_general(a, b, (((0,), (0,)), ((), ())), preferred_element_type=F32)


def _sigmoid(z):
    return 1.0 / (1.0 + jnp.exp(-z))


def _const_spec(shape):
    nd = len(shape)
    return pl.BlockSpec(shape, lambda i: (0,) * nd, pipeline_mode=pl.Buffered(1))


def _tables(seq):
    f32 = np.float32
    pos = np.arange(seq, dtype=f32)
    theta = (f32(1.0) / (f32(ROT_BASE) ** np.linspace(0.0, 1.0, RET_QK_DIM // 2, dtype=f32))).astype(f32)
    ang = (pos[:, None] * theta[None, :]).astype(f32)
    cos, sin = np.cos(ang), np.sin(ang)
    cos2 = np.repeat(cos, 2, axis=1)
    sin2 = np.stack([-sin, sin], axis=-1).reshape(seq, RET_QK_DIM)
    cos_t = np.tile(cos2, (1, 2))
    sin_t = np.tile(sin2, (1, 2))

    log_gamma = np.log(f32(1.0) - f32(2.0) ** (f32(-5.0) - np.arange(RET_HEADS, dtype=f32))).astype(f32)
    idx = np.arange(BLK, dtype=f32)
    rel = idx[:, None] - idx[None, :]
    decay_in = np.where(rel >= 0, np.exp(log_gamma[:, None, None] * np.maximum(rel, f32(0.0))), f32(0.0))
    k_dec = np.exp(log_gamma[:, None] * (BLK - 1 - idx)[None, :])
    q_dec = np.exp(log_gamma[:, None] * (idx + 1)[None, :])
    chunk_decay = np.exp(log_gamma * f32(BLK))
    kdec_t = np.repeat(k_dec.T, RET_QK_DIM, axis=1)
    qdec_t = np.repeat(q_dec.T, RET_QK_DIM, axis=1)
    cd_t = np.broadcast_to(chunk_decay[:, None, None], (RET_HEADS, RET_QK_DIM, RET_V_DIM))
    decay_in = decay_in.reshape(PAIRS, 2 * BLK, BLK)
    cd_t = cd_t.reshape(PAIRS, 2 * RET_QK_DIM, RET_V_DIM)

    key = np.arange(BLK)[:, None]
    query = np.arange(2 * BLK)[None, :] % BLK
    bias = np.stack([np.where(key > query, NEG_INF, 0.0), np.zeros((BLK, 2 * BLK))])
    return tuple(jnp.asarray(np.ascontiguousarray(a), F32) for a in (cos_t, sin_t, decay_in, qdec_t, kdec_t, cd_t, bias))


def _low_lanes(shape):
    lane = lax.broadcasted_iota(jnp.int32, shape, len(shape) - 1)
    return (lane & HALF_LANE) == 0


def _split_heads(t):
    low = _low_lanes(t.shape)
    zero = jnp.zeros_like(t)
    return jnp.where(low, t, zero), jnp.where(low, zero, t)


def _swap_pairs(t):
    lane = lax.broadcasted_iota(jnp.int32, t.shape, 1)
    nxt = pltpu.roll(t, t.shape[1] - 1, 1)
    prv = pltpu.roll(t, 1, 1)
    return jnp.where((lane & 1) == 0, nxt, prv)


def _per_tile(fn, t):
    return jnp.concatenate([fn(_tile(t, i)) for i in range(t.shape[1] // LANE)], axis=1)


def _rotate(t, cos_t, sin_t):
    return _per_tile(lambda a: a * cos_t + _swap_pairs(a) * sin_t, t)


def _rotate_transposed(d, cos_t, sin_t):
    return _per_tile(lambda a: a * cos_t + _swap_pairs(a * sin_t), d)


def _kv_operands(cat):
    low = _low_lanes(cat.shape)
    swapped = pltpu.roll(cat, HALF_LANE, 1)
    zero = jnp.zeros_like(cat)
    pick = lambda a, b: jnp.where(low, a, b).astype(BF16)
    return ((pick(cat, zero), pick(zero, swapped)), (pick(swapped, zero), pick(zero, cat)))


def _stack_tiles(t, first_tile):
    a = t[:, first_tile * LANE:(first_tile + 1) * LANE]
    b = t[:, (first_tile + 1) * LANE:(first_tile + 2) * LANE]
    return jnp.concatenate([a, b], axis=0)


def _sink_rows(sinks_ref, group):
    first = lax.broadcasted_iota(jnp.int32, (1, 2 * BLK), 1) < BLK

    def row(h0, h1):
        return jnp.where(first, sinks_ref[0, group * 4 + h0], sinks_ref[0, group * 4 + h1])
    return row(0, 2), row(1, 3)


ATT_PROBLEMS = tuple((g, hi) for g in range(ATT_KV_HEADS) for hi in range(2))


def _in_previous_block():
    key = lax.broadcasted_iota(jnp.int32, (BLK, 2 * BLK), 0)
    query = lax.broadcasted_iota(jnp.int32, (BLK, 2 * BLK), 1) & (BLK - 1)
    return key > query


def _fold(t, prev):
    return jnp.where(prev, t[0:BLK], t[BLK:])


def _unfold(t, prev):
    zero = jnp.zeros_like(t)
    return jnp.concatenate([jnp.where(prev, t, zero), jnp.where(prev, zero, t)], axis=0)


def _attn_probs(qs, k_ops, bias, prev, sinks_ref):
    sink = [_sink_rows(sinks_ref, g)[hi] for g, hi in ATT_PROBLEMS]
    s = [_fold(_dot_nt(k_ops[g][hi], qs[g]), prev) + bias for g, hi in ATT_PROBLEMS]
    m = [jnp.maximum(jnp.max(si, axis=0, keepdims=True), ki) for si, ki in zip(s, sink)]
    e = [jnp.exp(si - mi) for si, mi in zip(s, m)]
    es = [jnp.exp(ki - mi) for ki, mi in zip(sink, m)]
    inv = [1.0 / (jnp.sum(ei, axis=0, keepdims=True) + esi) for ei, esi in zip(e, es)]
    return [ei * ii for ei, ii in zip(e, inv)], [esi * ii for esi, ii in zip(es, inv)]


def _group_norm_all(outs):
    mu = [jnp.mean(o, axis=1, keepdims=True) for o in outs]
    xc = [o - m for o, m in zip(outs, mu)]
    var = [jnp.mean(c * c, axis=1, keepdims=True) for c in xc]
    rstd = [lax.rsqrt(v + GN_EPS) for v in var]
    return [c * r for c, r in zip(xc, rstd)], rstd


def _retention_operands(p_ref, rows, cos_b, sin_b, qdec, kdec):
    qr = _rotate(p_ref[rows, OFF_RQ:OFF_RQ + RET_QK_WIDTH].astype(F32), cos_b, sin_b)
    kr = _rotate(p_ref[rows, OFF_RK:OFF_RK + RET_QK_WIDTH].astype(F32), cos_b, sin_b) * RET_SCALE
    return qr, kr, qr * qdec, kr * kdec


def _tile(t, i):
    return t[:, i * LANE:(i + 1) * LANE]


def _weights_allgather(w_in_t_shard, w_out_shard):
    half_in = SHARD_IN // 2
    half_out = SHARD_OUT // 2
    assert half_in % BF16_ROWS == 0 and half_out % BF16_ROWS == 0

    def body(wi_ref, wo_ref, win_ref, wout_ref, blk_ref, blko_ref, send_sems, recv_sems):
        x, y, c = lax.axis_index("x"), lax.axis_index("y"), lax.axis_index("c")
        me, sibling = (x, y, c), (x, y, 1 - c)
        chips = [(1 - x, y), (x, 1 - y), (1 - x, 1 - y)]

        def rows_in(px, py, pc):
            return win_ref.at[pl.ds(pl.multiple_of((4 * px + 2 * py + pc) * half_in, BF16_ROWS), half_in), :]

        def rows_out(px, py, pc):
            return wout_ref.at[pl.ds(pl.multiple_of((4 * px + 2 * py + pc) * half_out, half_out), half_out), :]

        def copy(k, rows, block, to, src=None):
            return pltpu.make_async_remote_copy(
                src_ref=rows(*block) if src is None else src, dst_ref=rows(*block),
                send_sem=send_sems.at[k], recv_sem=recv_sems.at[k], device_id=to, device_id_type=MESH_ID)

        blk_ref[...] = wi_ref[pl.ds(pl.multiple_of(c * half_in, BF16_ROWS), half_in), :].astype(BF16)
        blko_ref[...] = wo_ref[pl.ds(pl.multiple_of(c * half_out, half_out), half_out), :].astype(BF16)
        rows_in(*me)[...] = blk_ref[...]
        rows_out(*me)[...] = blko_ref[...]

        sets = ((0, rows_in, blk_ref), (7, rows_out, blko_ref))
        first, passed = [], []
        for base, rows, src in sets:
            first.append(copy(base, rows, me, sibling, src=src))
            first += [copy(base + 1 + j, rows, me, (*chip, c), src=src) for j, chip in enumerate(chips)]
        for cp in first:
            cp.start()
        for base, rows, src in sets:
            for j, chip in enumerate(chips):
                copy(base + 1 + j, rows, (*chip, c), me).wait_recv()
                fwd = copy(base + 4 + j, rows, (*chip, c), sibling)
                fwd.start()
                passed.append(fwd)
        for base, rows, src in sets:
            copy(base, rows, sibling, me).wait_recv()
            for j, chip in enumerate(chips):
                copy(base + 4 + j, rows, (*chip, 1 - c), me).wait_recv()
        for cp in first + passed:
            cp.wait_send()

    vmem = pl.BlockSpec(memory_space=pltpu.VMEM)
    return _call(
        body, name="weights_allgather",
        out_shape=(jax.ShapeDtypeStruct((IN_WIDTH, D_MODEL), BF16), jax.ShapeDtypeStruct((MIX_WIDTH, D_MODEL), BF16)),
        in_specs=[vmem, vmem], out_specs=(vmem, vmem),
        scratch_shapes=[
            pltpu.VMEM((half_in, D_MODEL), BF16),
            pltpu.VMEM((half_out, D_MODEL), BF16),
            pltpu.SemaphoreType.DMA((14,)),
            pltpu.SemaphoreType.DMA((14,)),
        ],
        compiler_params=_params(32),
    )(w_in_t_shard, w_out_shard)


def _in_proj(x, norm_g, w_in, tb):
    seq = x.shape[0]

    def body(x_ref, g_ref, w_ref, p_ref, ht_ref):
        xv = x_ref[...]
        r = lax.rsqrt(jnp.mean(xv * xv, axis=1, keepdims=True) + RMS_EPS)
        h = (xv * r) * g_ref[...]
        p_ref[...] = _dot_nt(h.astype(BF16), w_ref[...]).astype(BF16)
        ht_ref[...] = h.T.astype(BF16)

    return _call(
        body, name="in_proj", grid=(seq // tb,),
        out_shape=(jax.ShapeDtypeStruct((seq, IN_WIDTH), BF16), jax.ShapeDtypeStruct((D_MODEL, seq), BF16)),
        in_specs=[pl.BlockSpec((tb, D_MODEL), lambda i: (i, 0)), _const_spec((1, D_MODEL)),
                  _const_spec((IN_WIDTH, D_MODEL))],
        out_specs=(pl.BlockSpec((tb, IN_WIDTH), lambda i: (i, 0)), pl.BlockSpec((D_MODEL, tb), lambda i: (0, i))),
        compiler_params=_params(56, ("arbitrary",)),
    )(x, norm_g, w_in)


def _mix_fwd(proj, x, target, w_out, final_g, gn_gain, sinks, tables, tb):
    seq = x.shape[0]
    nsub = tb // BLK
    cos_t, sin_t, decay_in, qdec_t, kdec_t, cd_t, bias_t = tables

    def body(p_ref, x_ref, t_ref, cos_ref, sin_ref, wout_ref, fg_ref, gain_ref, sinks_ref, din_ref, qdec_ref,
             kdec_ref, cd_ref, bias_ref, mix_ref, dxo_ref, st_ref, loss_ref, gfin_ref, oatt_ref,
             kprev_ref, vprev_ref, state_ref):
        i = pl.program_id(0)

        @pl.when(i == 0)
        def _():
            kprev_ref[...] = jnp.zeros_like(kprev_ref)
            vprev_ref[...] = jnp.zeros_like(vprev_ref)
            state_ref[...] = jnp.zeros_like(state_ref)
            loss_ref[...] = jnp.zeros_like(loss_ref)
            gfin_ref[...] = jnp.zeros_like(gfin_ref)

        prev = _in_previous_block()

        def sub(j, carry):
            kp, vp, states = carry
            rows = pl.ds(pl.multiple_of(j * BLK, BLK), BLK)
            bias = bias_ref[jnp.where(jnp.logical_or(i > 0, j > 0), 1, 0)]

            aq = p_ref[rows, OFF_AQ:OFF_AQ + ATT_WIDTH]
            ak = p_ref[rows, OFF_AK:OFF_AK + ATT_KV_WIDTH].astype(F32)
            av = p_ref[rows, OFF_AV:OFF_AV + ATT_KV_WIDTH].astype(F32)
            az = p_ref[rows, OFF_AZ:OFF_AZ + ATT_WIDTH].astype(F32)
            k_ops = _kv_operands(jnp.concatenate([kp, ak], axis=0))
            v_ops = _kv_operands(jnp.concatenate([vp, av], axis=0))
            qs = [(_stack_tiles(aq, 2 * g) * ATT_SCALE).astype(BF16) for g in range(ATT_KV_HEADS)]
            p, _ = _attn_probs(qs, k_ops, bias, prev, sinks_ref)
            o_tiles = []
            for g in range(ATT_KV_HEADS):
                p_cat = jnp.concatenate([_unfold(p[2 * g].astype(BF16), prev), _unfold(p[2 * g + 1].astype(BF16), prev)],
                                        axis=0)
                o = _dot_tn(p_cat, jnp.concatenate(v_ops[g], axis=0))
                o_tiles += [o[0:BLK], o[BLK:]]
            o_att = jnp.concatenate(o_tiles, axis=1)
            oatt_ref[rows, :] = o_att
            out = [o_att * (az * _sigmoid(az))]

            qr, kr, qd, kd = _retention_operands(p_ref, rows, cos_ref[rows, :], sin_ref[rows, :],
                                                 qdec_ref[...], kdec_ref[...])
            heads = [(t, hh) for t in range(PAIRS) for hh in range(2)]
            sc = [_dot_nt(jnp.concatenate(_split_heads(_tile(qr, t)), axis=0).astype(BF16), _tile(kr, t).astype(BF16))
                  * din_ref[t] for t in range(PAIRS)]
            qd_heads = [_split_heads(_tile(qd, t)) for t in range(PAIRS)]
            state_b = [states[t].astype(BF16) for t in range(PAIRS)]
            vs = [p_ref[rows, OFF_RV + h * RET_V_DIM:OFF_RV + (h + 1) * RET_V_DIM].astype(BF16) for h in range(RET_HEADS)]
            rzs = [p_ref[rows, OFF_RZ + h * RET_V_DIM:OFF_RZ + (h + 1) * RET_V_DIM].astype(F32) for h in range(RET_HEADS)]
            lhs = [jnp.concatenate([sc[t][hh * BLK:(hh + 1) * BLK].astype(BF16), qd_heads[t][hh].astype(BF16)], axis=1)
                   for t, hh in heads]
            ons, _ = _group_norm_all([_dot(lhs[2 * t + hh], jnp.concatenate([vs[2 * t + hh], state_b[t]], axis=0))
                                      for t, hh in heads])
            out += [(ons[h] * gain_ref[:, h * RET_V_DIM:(h + 1) * RET_V_DIM]) * (rzs[h] * _sigmoid(rzs[h]))
                    for h in range(RET_HEADS)]
            new_states = [states[t] * cd_ref[t]
                          + _dot_tn(jnp.concatenate(_split_heads(_tile(kd, t)), axis=0).astype(BF16),
                                    jnp.concatenate([vs[2 * t], vs[2 * t + 1]], axis=0)) for t in range(PAIRS)]
            mix_ref[rows, :] = jnp.concatenate(out, axis=1).astype(BF16)
            st_ref[j] = jnp.stack(states)
            return ak, av, tuple(new_states)

        carry = (kprev_ref[...], vprev_ref[...], tuple(state_ref[t] for t in range(PAIRS)))
        kp, vp, states = lax.fori_loop(0, nsub, sub, carry)
        kprev_ref[...] = kp
        vprev_ref[...] = vp
        state_ref[...] = jnp.stack(states)

        xo = x_ref[...] + _dot(mix_ref[...], wout_ref[...])
        r2 = lax.rsqrt(jnp.mean(xo * xo, axis=1, keepdims=True) + RMS_EPS)
        xn = xo * r2
        err = xn * fg_ref[...] - t_ref[...]
        loss_ref[...] += jnp.sum(err * err) * (0.5 / D_MODEL)
        dy = err * (1.0 / D_MODEL)
        gfin_ref[...] += jnp.sum(dy * xn, axis=0, keepdims=True)
        u = dy * fg_ref[...]
        dxo_ref[...] = r2 * u - xn * (r2 * jnp.mean(u * xn, axis=1, keepdims=True))

    blk_rows = lambda w: pl.BlockSpec((tb, w), lambda i: (i, 0))
    state_shape = (PAIRS, 2 * RET_QK_DIM, RET_V_DIM)
    return _call(
        body, name="mix_fwd", grid=(seq // tb,),
        out_shape=(
            jax.ShapeDtypeStruct((seq, MIX_WIDTH), BF16),
            jax.ShapeDtypeStruct((seq, D_MODEL), F32),
            jax.ShapeDtypeStruct((seq // BLK,) + state_shape, F32),
            jax.ShapeDtypeStruct((8, LANE), F32),
            jax.ShapeDtypeStruct((1, D_MODEL), F32),
            jax.ShapeDtypeStruct((seq, ATT_WIDTH), F32),
        ),
        in_specs=[
            blk_rows(IN_WIDTH), blk_rows(D_MODEL), blk_rows(D_MODEL), blk_rows(LANE), blk_rows(LANE),
            _const_spec((MIX_WIDTH, D_MODEL)), _const_spec((1, D_MODEL)), _const_spec((1, RET_WIDTH)),
            pl.BlockSpec(memory_space=pltpu.SMEM),
            _const_spec((PAIRS, 2 * BLK, BLK)), _const_spec((BLK, RET_QK_WIDTH)), _const_spec((BLK, RET_QK_WIDTH)),
            _const_spec(state_shape), _const_spec((2, BLK, 2 * BLK)),
        ],
        out_specs=(
            blk_rows(MIX_WIDTH), blk_rows(D_MODEL),
            pl.BlockSpec((nsub,) + state_shape, lambda i: (i, 0, 0, 0)),
            _const_spec((8, LANE)), _const_spec((1, D_MODEL)), blk_rows(ATT_WIDTH),
        ),
        scratch_shapes=[
            pltpu.VMEM((BLK, ATT_KV_WIDTH), F32), pltpu.VMEM((BLK, ATT_KV_WIDTH), F32),
            pltpu.VMEM(state_shape, F32),
        ],
        compiler_params=_params(48, ("arbitrary",)),
    )(proj, x, target, cos_t, sin_t, w_out, final_g, gn_gain, sinks, decay_in, qdec_t, kdec_t, cd_t, bias_t)


def _mix_bwd(proj, dxo, mix, o_att, states, w_out, gn_gain, sinks, tables, tb):
    seq = dxo.shape[0]
    nsub = tb // BLK
    nblk = seq // tb
    cos_t, sin_t, decay_in, qdec_t, kdec_t, cd_t, bias_t = tables
    kv_cols = OFF_AK // (2 * ATT_KV_WIDTH)
    state_shape = (PAIRS, 2 * RET_QK_DIM, RET_V_DIM)

    def body(p_ref, pkv_ref, dxo_ref, mix_ref, oatt_ref, st_ref, cos_ref, sin_ref, wout_ref, gain_ref, sinks_ref, din_ref,
             qdec_ref, kdec_ref, cd_ref, bias_ref, dp_ref, gwout_ref, dgain_ref, dsink_ref,
             dmix_ref, kv_ref, dkc_ref, dvc_ref, gst_ref):
        i = pl.program_id(0)
        blk = nblk - 1 - i

        @pl.when(i == 0)
        def _():
            gwout_ref[...] = jnp.zeros_like(gwout_ref)
            dgain_ref[...] = jnp.zeros_like(dgain_ref)
            dsink_ref[...] = jnp.zeros_like(dsink_ref)
            dkc_ref[...] = jnp.zeros_like(dkc_ref)
            dvc_ref[...] = jnp.zeros_like(dvc_ref)
            gst_ref[...] = jnp.zeros_like(gst_ref)

        dxo_b = dxo_ref[...].astype(BF16)
        dmix_ref[...] = _dot_nt(dxo_b, wout_ref[...])
        gwout_ref[...] += _dot_tn(mix_ref[...], dxo_b)
        kv_ref[0:BLK, :] = pkv_ref[...].astype(F32)
        kv_ref[BLK:, :] = p_ref[:, OFF_AK:OFF_AK + 2 * ATT_KV_WIDTH].astype(F32)
        low = _low_lanes((BLK, LANE))
        low2 = _low_lanes((2 * BLK, LANE))
        lane = lax.broadcasted_iota(jnp.int32, (1, LANE), 1)
        prev = _in_previous_block()

        def sub(jj, carry):
            dkc, dvc, gsts, dgain, dsink = carry
            j = nsub - 1 - jj
            rows = pl.ds(pl.multiple_of(j * BLK, BLK), BLK)
            both = pl.ds(pl.multiple_of(j * BLK, BLK), 2 * BLK)
            bias = bias_ref[jnp.where(jnp.logical_or(blk > 0, j > 0), 1, 0)]

            aq = p_ref[rows, OFF_AQ:OFF_AQ + ATT_WIDTH]
            az = p_ref[rows, OFF_AZ:OFF_AZ + ATT_WIDTH].astype(F32)
            k_ops = _kv_operands(kv_ref[both, 0:ATT_KV_WIDTH])
            v_ops = _kv_operands(kv_ref[both, ATT_KV_WIDTH:2 * ATT_KV_WIDTH])
            da = dmix_ref[rows, 0:ATT_WIDTH]
            sig = _sigmoid(az)
            d_o = da * (az * sig)
            qs = [(_stack_tiles(aq, 2 * g) * ATT_SCALE).astype(BF16) for g in range(ATT_KV_HEADS)]
            dos = [_stack_tiles(d_o, 2 * g).astype(BF16) for g in range(ATT_KV_HEADS)]
            p, p_sink = _attn_probs(qs, k_ops, bias, prev, sinks_ref)
            dpr = [_fold(_dot_nt(v_ops[g][hi], dos[g]), prev) for g, hi in ATT_PROBLEMS]
            delta = [jnp.sum(pi * di, axis=0, keepdims=True) for pi, di in zip(p, dpr)]
            ds = [_unfold((pi * (di - ti)).astype(BF16), prev) for pi, di, ti in zip(p, dpr, delta)]
            for (g, hi), ki, ti in zip(ATT_PROBLEMS, p_sink, delta):
                sink_part = ki * ti
                for half in range(2):
                    tot = jnp.sum(sink_part[:, half * BLK:(half + 1) * BLK], axis=1, keepdims=True)
                    dsink = dsink - jnp.where(lane == 4 * g + 2 * half + hi, tot, 0.0)
            dq_tiles, dk_sums, dv_sums = [], [], []
            for g in range(ATT_KV_HEADS):
                ds_cat = jnp.concatenate([ds[2 * g], ds[2 * g + 1]], axis=0)
                p_cat = jnp.concatenate([_unfold(p[2 * g].astype(BF16), prev), _unfold(p[2 * g + 1].astype(BF16), prev)],
                                        axis=0)
                dqs = _dot_tn(ds_cat, jnp.concatenate(k_ops[g], axis=0)) * ATT_SCALE
                dq_tiles += [dqs[0:BLK], dqs[BLK:]]
                dk_sums.append(_dot(ds_cat, qs[g]))
                dv_sums.append(_dot(p_cat, dos[g]))
            daz = da * oatt_ref[rows, :] * (sig * (1.0 + az * (1.0 - sig)))

            def kv_grad(sums):
                (a0, b0), (a1, b1) = [(s[0:2 * BLK], s[2 * BLK:]) for s in sums]
                return jnp.where(low2, a0, b1) + pltpu.roll(jnp.where(low2, a1, b0), HALF_LANE, 1)

            dk_both, dv_both = kv_grad(dk_sums), kv_grad(dv_sums)
            dak = dk_both[BLK:] + dkc
            dav = dv_both[BLK:] + dvc

            cos_b, sin_b = cos_ref[rows, :], sin_ref[rows, :]
            qdec, kdec = qdec_ref[...], kdec_ref[...]
            qr, kr, qd, kd = _retention_operands(p_ref, rows, cos_b, sin_b, qdec, kdec)
            heads = [(t, hh) for t in range(PAIRS) for hh in range(2)]
            head_cols = [slice(h * RET_V_DIM, (h + 1) * RET_V_DIM) for h in range(RET_HEADS)]
            q_rows = [jnp.concatenate(_split_heads(_tile(qr, t)), axis=0).astype(BF16) for t in range(PAIRS)]
            k_rows = [jnp.concatenate(_split_heads(_tile(kr, t)), axis=0).astype(BF16) for t in range(PAIRS)]
            din = [din_ref[t] for t in range(PAIRS)]
            sc = [(_dot_nt(q_rows[t], _tile(kr, t).astype(BF16)) * din[t]).astype(BF16) for t in range(PAIRS)]
            qd_heads = [_split_heads(_tile(qd, t)) for t in range(PAIRS)]
            kd_heads = [_split_heads(_tile(kd, t)) for t in range(PAIRS)]
            state_b = [st_ref[j, t].astype(BF16) for t in range(PAIRS)]
            gst_b = [gsts[t].astype(BF16) for t in range(PAIRS)]
            vs = [p_ref[rows, OFF_RV + h * RET_V_DIM:OFF_RV + (h + 1) * RET_V_DIM].astype(BF16) for h in range(RET_HEADS)]
            rzs = [p_ref[rows, OFF_RZ + h * RET_V_DIM:OFF_RZ + (h + 1) * RET_V_DIM].astype(F32) for h in range(RET_HEADS)]
            drs = [dmix_ref[rows, ATT_WIDTH + h * RET_V_DIM:ATT_WIDTH + (h + 1) * RET_V_DIM] for h in range(RET_HEADS)]
            gains = [gain_ref[:, c] for c in head_cols]
            lhs = [jnp.concatenate([sc[t][hh * BLK:(hh + 1) * BLK], qd_heads[t][hh].astype(BF16)], axis=1) for t, hh in heads]
            rhs = [jnp.concatenate([vs[2 * t + hh], state_b[t]], axis=0) for t, hh in heads]
            ons, rstds = _group_norm_all([_dot(l, r) for l, r in zip(lhs, rhs)])
            sig_r = [_sigmoid(z) for z in rzs]
            dgn = [d * (z * g) for d, z, g in zip(drs, rzs, sig_r)]
            dz_parts = [d * (o * gn) * (g * (1.0 + z * (1.0 - g))) for d, o, gn, g, z in zip(drs, ons, gains, sig_r, rzs)]
            dgain_parts = [jnp.sum(d * o, axis=0, keepdims=True) for d, o in zip(dgn, ons)]
            don = [d * gn for d, gn in zip(dgn, gains)]
            mean_don = [jnp.mean(d, axis=1, keepdims=True) for d in don]
            mean_don_on = [jnp.mean(d * o, axis=1, keepdims=True) for d, o in zip(don, ons)]
            dob = [(r * (d - a - o * b)).astype(BF16) for r, d, a, o, b in zip(rstds, don, mean_don, ons, mean_don_on)]
            dlhs = [_dot_nt(d, r) for d, r in zip(dob, rhs)]
            drhs = [_dot_tn(l, d) for l, d in zip(lhs, dob)]
            dkds = [_dot_nt(vs[2 * t + hh], gst_b[t]) for t, hh in heads]
            dv_parts = [drhs[2 * t + hh][0:BLK] + _dot(kd_heads[t][hh].astype(BF16), gst_b[t]) for t, hh in heads]
            das = [(dlhs[2 * t + hh][:, 0:BLK] * din[t][hh * BLK:(hh + 1) * BLK]).astype(BF16) for t, hh in heads]
            new_gsts = [gsts[t] * cd_ref[t] + drhs[2 * t][BLK:] + drhs[2 * t + 1][BLK:] for t in range(PAIRS)]
            dq_parts = [_dot(jnp.concatenate([das[2 * t], das[2 * t + 1]], axis=1), k_rows[t])
                        + jnp.where(low, dlhs[2 * t][:, BLK:], dlhs[2 * t + 1][:, BLK:]) * _tile(qdec, t)
                        for t in range(PAIRS)]
            dk_parts = [_dot_tn(jnp.concatenate([das[2 * t], das[2 * t + 1]], axis=0), q_rows[t])
                        + jnp.where(low, dkds[2 * t], dkds[2 * t + 1]) * _tile(kdec, t) for t in range(PAIRS)]
            drq = _rotate_transposed(jnp.concatenate(dq_parts, axis=1), cos_b, sin_b)
            drk = _rotate_transposed(jnp.concatenate(dk_parts, axis=1) * RET_SCALE, cos_b, sin_b)

            dp_ref[rows, :] = jnp.concatenate(
                [jnp.concatenate(dq_tiles, axis=1), dak, dav, daz, drq, drk] + dv_parts + dz_parts, axis=1).astype(BF16)
            dgain = dgain + jnp.concatenate(dgain_parts, axis=1)
            return dk_both[0:BLK], dv_both[0:BLK], tuple(new_gsts), dgain, dsink

        carry = (dkc_ref[...], dvc_ref[...], tuple(gst_ref[t] for t in range(PAIRS)), dgain_ref[...], dsink_ref[...])
        dkc, dvc, gsts, dgain, dsink = lax.fori_loop(0, nsub, sub, carry)
        dkc_ref[...] = dkc
        dvc_ref[...] = dvc
        gst_ref[...] = jnp.stack(gsts)
        dgain_ref[...] = dgain
        dsink_ref[...] = dsink

    rev_rows = lambda w: pl.BlockSpec((tb, w), lambda i: (nblk - 1 - i, 0))
    prev_kv = pl.BlockSpec((BLK, 2 * ATT_KV_WIDTH), lambda i: (jnp.maximum((nblk - 1 - i) * nsub - 1, 0), kv_cols))
    return _call(
        body, name="mix_bwd", grid=(nblk,),
        out_shape=(
            jax.ShapeDtypeStruct((seq, IN_WIDTH), BF16),
            jax.ShapeDtypeStruct((MIX_WIDTH, D_MODEL), F32),
            jax.ShapeDtypeStruct((1, RET_WIDTH), F32),
            jax.ShapeDtypeStruct((1, LANE), F32),
        ),
        in_specs=[
            rev_rows(IN_WIDTH), prev_kv, rev_rows(D_MODEL), rev_rows(MIX_WIDTH), rev_rows(ATT_WIDTH),
            pl.BlockSpec((nsub,) + state_shape, lambda i: (nblk - 1 - i, 0, 0, 0)),
            rev_rows(LANE), rev_rows(LANE),
            _const_spec((MIX_WIDTH, D_MODEL)), _const_spec((1, RET_WIDTH)),
            pl.BlockSpec(memory_space=pltpu.SMEM),
            _const_spec((PAIRS, 2 * BLK, BLK)), _const_spec((BLK, RET_QK_WIDTH)), _const_spec((BLK, RET_QK_WIDTH)),
            _const_spec(state_shape), _const_spec((2, BLK, 2 * BLK)),
        ],
        out_specs=(
            rev_rows(IN_WIDTH), _const_spec((MIX_WIDTH, D_MODEL)), _const_spec((1, RET_WIDTH)), _const_spec((1, LANE)),
        ),
        scratch_shapes=[
            pltpu.VMEM((tb, MIX_WIDTH), F32),
            pltpu.VMEM((tb + BLK, 2 * ATT_KV_WIDTH), F32),
            pltpu.VMEM((BLK, ATT_KV_WIDTH), F32), pltpu.VMEM((BLK, ATT_KV_WIDTH), F32),
            pltpu.VMEM(state_shape, F32),
        ],
        compiler_params=_params(56, ("arbitrary",)),
    )(proj, proj, dxo, mix, o_att, states, cos_t, sin_t, w_out, gn_gain, sinks, decay_in, qdec_t, kdec_t, cd_t, bias_t)


def _in_proj_bwd(dproj, x, dxo, norm_g, w_in, tb):
    seq = x.shape[0]

    def body(dp_ref, x_ref, dxo_ref, g_ref, w_ref, gx_ref, gnorm_ref):
        i = pl.program_id(0)

        @pl.when(i == 0)
        def _():
            gnorm_ref[...] = jnp.zeros_like(gnorm_ref)

        xv = x_ref[...]
        r = lax.rsqrt(jnp.mean(xv * xv, axis=1, keepdims=True) + RMS_EPS)
        xn = xv * r
        dh = _dot(dp_ref[...], w_ref[...])
        gnorm_ref[...] += jnp.sum(dh * xn, axis=0, keepdims=True)
        u = dh * g_ref[...]
        gx_ref[...] = dxo_ref[...] + r * u - xn * (r * jnp.mean(u * xn, axis=1, keepdims=True))

    rows = lambda w: pl.BlockSpec((tb, w), lambda i: (i, 0))
    return _call(
        body, name="in_proj_bwd", grid=(seq // tb,),
        out_shape=(jax.ShapeDtypeStruct((seq, D_MODEL), F32), jax.ShapeDtypeStruct((1, D_MODEL), F32)),
        in_specs=[rows(IN_WIDTH), rows(D_MODEL), rows(D_MODEL), _const_spec((1, D_MODEL)),
                  _const_spec((IN_WIDTH, D_MODEL))],
        out_specs=(rows(D_MODEL), _const_spec((1, D_MODEL))),
        compiler_params=_params(48, ("arbitrary",)),
    )(dproj, x, dxo, norm_g, w_in)


def _gw_in_reduce(h_t, dproj, tb):
    seq = dproj.shape[0]
    nblk = seq // tb
    last = nblk - 1
    hand_on = min(1, last)
    half = D_MODEL // 2
    A, B, C, N_SEMS = 0, N_CHIPS, 2 * N_CHIPS, 2 * N_CHIPS + 1

    def body(win_ref, ht_ref, dp_ref, out_ref, acc, sib, send_buf, b_in, fin, send_sems, recv_sems):
        p, i = pl.program_id(0), pl.program_id(1)
        x, y, c = lax.axis_index("x"), lax.axis_index("y"), lax.axis_index("c")
        chip = 2 * x + y
        sibling = (x, y, 1 - c)
        mine = pl.ds(pl.multiple_of(c * half, half), half)
        other = pl.ds(pl.multiple_of((1 - c) * half, half), half)

        def remote(src, dst, send_k, recv_k, to):
            return pltpu.make_async_remote_copy(src_ref=src, dst_ref=dst, send_sem=send_sems.at[send_k],
                                                recv_sem=recv_sems.at[recv_k], device_id=to, device_id_type=MESH_ID)

        part = _dot(ht_ref[...], dp_ref[...])
        slot = p % 2

        @pl.when(i == 0)
        def _():
            acc[slot] = part

        @pl.when(i > 0)
        def _():
            acc[slot] += part

        for q in range(N_CHIPS):
            s = q % 2
            to_sibling = remote(acc.at[s, other, :], sib.at[s], A + q, A + q, sibling)

            @pl.when(jnp.logical_and(p == q, i == last))
            def _():
                to_sibling.start()

            if q < N_CHIPS - 1:
                dest = (chip + 1 + q) % N_CHIPS

                @pl.when(jnp.logical_and(p == q + 1, i == hand_on))
                def _():
                    to_sibling.wait_recv()
                    send_buf[q] = (acc[s, mine, :] + sib[s]).astype(BF16)
                    remote(send_buf.at[q], b_in.at[chip], B + q, B + chip, (dest // 2, dest % 2, c)).start()
                    to_sibling.wait_send()
            else:
                @pl.when(jnp.logical_and(p == q, i == last))
                def _():
                    to_sibling.wait_recv()
                    fin[mine, :] = acc[s, mine, :] + sib[s]
                    for j in range(N_CHIPS):
                        @pl.when(j != chip)
                        def _():
                            remote(b_in.at[j], b_in.at[j], B + j, B + j, sibling).wait_recv()
                            fin[mine, :] += b_in[j].astype(F32)
                    to_core = remote(fin.at[mine, :], fin.at[mine, :], C, C, sibling)
                    to_core.start()
                    remote(fin.at[other, :], fin.at[other, :], C, C, sibling).wait_recv()
                    out_ref[...] = fin[...]
                    to_core.wait_send()
                    to_sibling.wait_send()
                    for k in range(N_CHIPS - 1):
                        remote(send_buf.at[k], b_in.at[chip], B + k, B + k, sibling).wait_send()

    grid_spec = pltpu.PrefetchScalarGridSpec(
        num_scalar_prefetch=1, grid=(N_CHIPS, nblk),
        in_specs=[pl.BlockSpec((D_MODEL, tb), lambda p, i, win: (0, i)),
                  pl.BlockSpec((pl.Element(tb), pl.Element(SHARD_PAD)),
                               lambda p, i, win: (i * tb, pl.multiple_of(win[p] * LANE, LANE)))],
        out_specs=pl.BlockSpec((D_MODEL, SHARD_PAD), lambda p, i, win: (0, 0), pipeline_mode=pl.Buffered(1)),
        scratch_shapes=[
            pltpu.VMEM((2, D_MODEL, SHARD_PAD), F32), pltpu.VMEM((2, half, SHARD_PAD), F32),
            pltpu.VMEM((N_CHIPS - 1, half, SHARD_PAD), BF16), pltpu.VMEM((N_CHIPS, half, SHARD_PAD), BF16),
            pltpu.VMEM((D_MODEL, SHARD_PAD), F32),
            pltpu.SemaphoreType.DMA((N_SEMS,)), pltpu.SemaphoreType.DMA((N_SEMS,)),
        ])
    chip = 2 * lax.axis_index("x") + lax.axis_index("y")
    owner = (chip + 1 + jnp.arange(N_CHIPS, dtype=jnp.int32)) % N_CHIPS
    win_start = (owner * SHARD_IN) // LANE
    return _call(
        body, name="gw_in_reduce", grid_spec=grid_spec,
        out_shape=jax.ShapeDtypeStruct((D_MODEL, SHARD_PAD), F32),
        compiler_params=_params(40, ("arbitrary", "arbitrary")),
    )(win_start.astype(jnp.int32), h_t, dproj)


def _grads_reduce(gw_out, small):
    half_out = SHARD_OUT // 2
    A_OUT, B_OUT, C_OUT, PACK, N_SEMS = 0, 4, 8, 9, 17
    assert len(small) == PACK_PARTS

    def body(gwo_hbm, s0_ref, s1_ref, s2_ref, s3_ref, s4_ref, fout_ref, packsum_ref,
             mine_out, sib_out, send_out, b_out, pack_ref, packs, send_sems, recv_sems, local_sems):
        x, y, c = lax.axis_index("x"), lax.axis_index("y"), lax.axis_index("c")
        chip = 2 * x + y
        dev = 2 * chip + c
        sibling = (x, y, 1 - c)
        pack_ref[...] = jnp.zeros_like(pack_ref)
        for k, s_ref in enumerate((s0_ref, s1_ref, s2_ref, s3_ref, s4_ref)):
            pack_ref[k:k + 1, 0:s_ref.shape[1]] = s_ref[0:1, :]

        def remote(src, dst, send_k, recv_k, to):
            return pltpu.make_async_remote_copy(src_ref=src, dst_ref=dst, send_sem=send_sems.at[send_k],
                                                recv_sem=recv_sems.at[recv_k], device_id=to, device_id_type=MESH_ID)

        def out_rows(j, core):
            return pl.ds(pl.multiple_of(j * SHARD_OUT + core * half_out, half_out), half_out)

        packs[dev] = pack_ref[...]
        for d in range(N_DEV):
            to = (d // 4, (d // 2) % 2, d % 2)

            @pl.when(d != dev)
            def _():
                remote(pack_ref, packs.at[dev], PACK + d, PACK + dev, to).start()

        local = [pltpu.make_async_copy(gwo_hbm.at[out_rows(j, c), :], mine_out.at[j], local_sems.at[j])
                 for j in range(N_CHIPS)]
        for cp in local:
            cp.start()
        stage_a = [remote(gwo_hbm.at[out_rows(j, 1 - c), :], sib_out.at[j], A_OUT + j, A_OUT + j, sibling)
                   for j in range(N_CHIPS)]
        for cp in stage_a:
            cp.start()
        for cp in local:
            cp.wait()
        for cp in stage_a:
            cp.wait_recv()
        for j in range(N_CHIPS):
            mine_out[j] = mine_out[j] + sib_out[j]

        my_out_rows = pl.ds(pl.multiple_of(c * half_out, half_out), half_out)
        for j in range(N_CHIPS):
            to = (j // 2, j % 2, c)

            @pl.when(j != chip)
            def _():
                send_out[j] = mine_out[j].astype(BF16)
                remote(send_out.at[j], b_out.at[chip], B_OUT + j, B_OUT + chip, to).start()

            @pl.when(j == chip)
            def _():
                fout_ref[my_out_rows, :] = mine_out[j]

        for j in range(N_CHIPS):
            @pl.when(j != chip)
            def _():
                remote(b_out.at[j], b_out.at[j], B_OUT + j, B_OUT + j, sibling).wait_recv()
                fout_ref[my_out_rows, :] += b_out[j].astype(F32)

        mine_half_out = fout_ref.at[my_out_rows, :]
        stage_c = [remote(mine_half_out, mine_half_out, C_OUT, C_OUT, sibling)]
        for cp in stage_c:
            cp.start()
        other_half_out = fout_ref.at[pl.ds(pl.multiple_of((1 - c) * half_out, half_out), half_out), :]
        remote(other_half_out, other_half_out, C_OUT, C_OUT, sibling).wait_recv()

        for d in range(N_DEV):
            @pl.when(d != dev)
            def _():
                remote(pack_ref, packs.at[d], PACK + d, PACK + d, sibling).wait_recv()
        total = packs[0]
        for d in range(1, N_DEV):
            total = total + packs[d]
        packsum_ref[...] = total

        for cp in stage_a + stage_c:
            cp.wait_send()
        for j in range(N_CHIPS):
            @pl.when(j != chip)
            def _():
                remote(b_out.at[j], b_out.at[j], B_OUT + j, B_OUT + j, sibling).wait_send()
        for d in range(N_DEV):
            @pl.when(d != dev)
            def _():
                remote(pack_ref, packs.at[d], PACK + d, PACK + d, sibling).wait_send()

    vmem = pl.BlockSpec(memory_space=pltpu.VMEM)
    hbm = pl.BlockSpec(memory_space=pl.ANY)
    return _call(
        body, name="grads_reduce",
        out_shape=(jax.ShapeDtypeStruct((SHARD_OUT, D_MODEL), F32), jax.ShapeDtypeStruct((PACK_ROWS, D_MODEL), F32)),
        in_specs=[hbm] + [vmem] * PACK_PARTS, out_specs=(vmem, vmem),
        scratch_shapes=[
            pltpu.VMEM((N_CHIPS, half_out, D_MODEL), F32), pltpu.VMEM((N_CHIPS, half_out, D_MODEL), F32),
            pltpu.VMEM((N_CHIPS, half_out, D_MODEL), BF16), pltpu.VMEM((N_CHIPS, half_out, D_MODEL), BF16),
            pltpu.VMEM((PACK_ROWS, D_MODEL), F32), pltpu.VMEM((N_DEV, PACK_ROWS, D_MODEL), F32),
            pltpu.SemaphoreType.DMA((N_SEMS,)), pltpu.SemaphoreType.DMA((N_SEMS,)),
            pltpu.SemaphoreType.DMA((N_CHIPS,)),
        ],
        compiler_params=_params(24),
    )(gw_out, *small)


def _adam_math(w, g, m, v):
    mn = ADAM_B1 * m + (1.0 - ADAM_B1) * g
    vn = ADAM_B2 * v + (1.0 - ADAM_B2) * (g * g)
    m_hat = mn / (1.0 - ADAM_B1 ** ADAM_STEP)
    v_hat = vn / (1.0 - ADAM_B2 ** ADAM_STEP)
    return -ADAM_LR * (m_hat / (jnp.sqrt(v_hat) + ADAM_EPS) + ADAM_WD * w), mn, vn


def _adamw(name, w, g, m, v, tb):
    rows, cols = w.shape

    def body(w_ref, g_ref, m_ref, v_ref, go_ref, d_ref, mo_ref, vo_ref):
        gv = g_ref[...]
        go_ref[...] = gv
        d_ref[...], mo_ref[...], vo_ref[...] = _adam_math(w_ref[...], gv, m_ref[...], v_ref[...])

    spec = pl.BlockSpec((tb, cols), lambda i: (i, 0))
    shape = jax.ShapeDtypeStruct((rows, cols), F32)
    return _call(
        body, name=name, grid=(rows // tb,), out_shape=(shape,) * 4,
        in_specs=[spec] * 4, out_specs=(spec,) * 4,
        compiler_params=_params(32, ("arbitrary",)),
    )(w, g, m, v)


def _adamw_w_in(w_t, g_window, m_t, v_t, tb):
    def body(w_ref, g_ref, m_ref, v_ref, go_ref, d_ref, mo_ref, vo_ref, gt_ref):
        gt_ref[...] = g_ref[...].T
        gv = gt_ref[pl.ds(pl.multiple_of(lax.axis_index("y") * SHARD_SHIFT, SHARD_SHIFT), SHARD_IN), :]
        go_ref[...] = gv
        d_ref[...], mo_ref[...], vo_ref[...] = _adam_math(w_ref[...], gv, m_ref[...], v_ref[...])

    spec = pl.BlockSpec((SHARD_IN, tb), lambda i: (0, i))
    shape = jax.ShapeDtypeStruct((SHARD_IN, D_MODEL), F32)
    return _call(
        body, name="adamw_w_in", grid=(D_MODEL // tb,), out_shape=(shape,) * 4,
        in_specs=[spec, pl.BlockSpec((tb, SHARD_PAD), lambda i: (i, 0)), spec, spec], out_specs=(spec,) * 4,
        scratch_shapes=[pltpu.VMEM((SHARD_PAD, tb), F32)],
        compiler_params=_params(32, ("arbitrary",)),
    )(w_t, g_window, m_t, v_t)


def _adamw_small(sums, params):
    def body(sums_ref, *refs):
        ins, outs = refs[:3 * len(params)], refs[3 * len(params):]
        for k in range(len(params)):
            w_ref, m_ref, v_ref = ins[3 * k:3 * k + 3]
            g = sums_ref[k:k + 1, 0:w_ref.shape[1]]
            go_ref, d_ref, mo_ref, vo_ref = outs[4 * k:4 * k + 4]
            go_ref[...] = g
            d_ref[...], mo_ref[...], vo_ref[...] = _adam_math(w_ref[...], g, m_ref[...], v_ref[...])

    vmem = pl.BlockSpec(memory_space=pltpu.VMEM)
    flat = [a for p in params for a in p]
    shapes = tuple(jax.ShapeDtypeStruct(p[0].shape, F32) for p in params for _ in range(4))
    res = _call(body, name="adamw_small", out_shape=shapes, in_specs=[vmem] * (1 + len(flat)),
                out_specs=(vmem,) * len(shapes), compiler_params=_params(16))(sums, *flat)
    return [res[4 * k:4 * k + 4] for k in range(len(params))]


def kernel(x, norm_g, w_in, att_sinks, ret_gn_g, w_out, final_g, loss_target, m_norm_g, m_w_in, m_att_sinks, m_ret_gn_g, m_w_out, m_final_g, v_norm_g, v_w_in, v_att_sinks, v_ret_gn_g, v_w_out, v_final_g):
    seq = x.shape[1]
    xs, tgt = x[0], loss_target[0]
    final_g2 = final_g.reshape(1, D_MODEL)
    tables = _tables(seq)

    w_in_t, m_w_in_t, v_w_in_t = w_in[0].T, m_w_in[0].T, v_w_in[0].T
    w_in_full, w_out_full = _weights_allgather(w_in_t, w_out[0])
    proj, h_t = _in_proj(xs, norm_g, w_in_full, min(TOKENS_PROJ, seq))
    mix, dxo, states, loss_part, gfin, o_att = _mix_fwd(proj, xs, tgt, w_out_full, final_g2, ret_gn_g, att_sinks,
                                                        tables, min(TOKENS_MIX, seq))
    dproj, gw_out, dgain, dsink = _mix_bwd(proj, dxo, mix, o_att, states, w_out_full, ret_gn_g, att_sinks, tables,
                                           min(TOKENS_MIX, seq))
    g_in = _gw_in_reduce(h_t, dproj, min(TOKENS_GW, seq))
    grad_x, gnorm = _in_proj_bwd(dproj, xs, dxo, norm_g, w_in_full, min(TOKENS_PROJ_BWD, seq))

    g_out, sums = _grads_reduce(gw_out, (gnorm, gfin, dgain, dsink, loss_part))

    res_in = [r.T for r in _adamw_w_in(w_in_t, g_in, m_w_in_t, v_w_in_t, 256)]
    res_out = _adamw("adamw_w_out", w_out[0], g_out, m_w_out[0], v_w_out[0], SHARD_OUT)
    as_row = lambda a: a.reshape(1, D_MODEL)
    r_norm, r_final, r_gain, r_sink = _adamw_small(sums, [
        (norm_g, m_norm_g, v_norm_g), (final_g2, as_row(m_final_g), as_row(v_final_g)),
        (ret_gn_g, m_ret_gn_g, v_ret_gn_g), (att_sinks, m_att_sinks, v_att_sinks)])

    outs = []
    for k in range(4):
        outs += [r_norm[k], res_in[k][None], r_sink[k], r_gain[k], res_out[k][None], r_final[k].reshape(D_MODEL)]
    return (sums[4, 0], grad_x[None], *outs)
```

```python
import jax
import jax.numpy as jnp
import numpy as np
from jax import lax
from jax.experimental import pallas as pl
from jax.experimental.pallas import tpu as pltpu

F32 = jnp.float32
BF16 = jnp.bfloat16

D_MODEL = 1024
ATT_HEADS = 8
ATT_KV_HEADS = 2
ATT_HEAD_DIM = 64
RET_HEADS = 4
RET_QK_DIM = 64
RET_V_DIM = 128
BLK = 128
ROT_BASE = 10000.0
RMS_EPS = 1e-6
GN_EPS = 1e-6
NEG_INF = -1e30
ATT_SCALE = ATT_HEAD_DIM ** -0.5
RET_SCALE = RET_QK_DIM ** -0.5

ATT_WIDTH = ATT_HEADS * ATT_HEAD_DIM
ATT_KV_WIDTH = ATT_KV_HEADS * ATT_HEAD_DIM
RET_QK_WIDTH = RET_HEADS * RET_QK_DIM
RET_WIDTH = RET_HEADS * RET_V_DIM
MIX_WIDTH = ATT_WIDTH + RET_WIDTH
OFF_AQ = 0
OFF_AK = OFF_AQ + ATT_WIDTH
OFF_AV = OFF_AK + ATT_KV_WIDTH
OFF_AZ = OFF_AV + ATT_KV_WIDTH
OFF_RQ = OFF_AZ + ATT_WIDTH
OFF_RK = OFF_RQ + RET_QK_WIDTH
OFF_RV = OFF_RK + RET_QK_WIDTH
OFF_RZ = OFF_RV + RET_WIDTH
IN_WIDTH = OFF_RZ + RET_WIDTH

LANE = 128
BF16_ROWS = 16
HALF_LANE = LANE // 2
PAIRS = RET_QK_WIDTH // LANE
assert ATT_HEAD_DIM == HALF_LANE and RET_QK_DIM == HALF_LANE and RET_V_DIM == LANE and ATT_KV_WIDTH == LANE

N_CHIPS = 4
N_DEV = 8
SHARD_IN = IN_WIDTH // N_CHIPS
SHARD_PAD = 768
SHARD_SHIFT = SHARD_PAD - SHARD_IN
WIN_START = tuple((j * SHARD_IN) // LANE * LANE for j in range(N_CHIPS))
SHARD_OUT = MIX_WIDTH // N_CHIPS

PACK_PARTS = 5
PACK_ROWS = 8
assert PACK_PARTS <= PACK_ROWS

ADAM_LR = 0.001
ADAM_B1 = 0.9
ADAM_B2 = 0.999
ADAM_EPS = 1e-08
ADAM_WD = 0.01
ADAM_STEP = 10

VMEM_CAP = 64 * 1024 * 1024
TOKENS_PROJ = 1024
TOKENS_MIX = 512
TOKENS_PROJ_BWD = 512
TOKENS_GW = 2048
MESH_ID = pl.DeviceIdType.MESH


def _call(body, **kw):
    return pl.pallas_call(body, **kw)


def _params(vmem_mb, semantics=None):
    assert vmem_mb * 1024 * 1024 < VMEM_CAP
    return pltpu.CompilerParams(dimension_semantics=semantics, vmem_limit_bytes=vmem_mb * 1024 * 1024)


def _dot(a, b):
    return jnp.dot(a, b, preferred_element_type=F32)


def _dot_nt(a, b):
    return lax.dot_general(a, b, (((1,), (1,)), ((), ())), preferred_element_type=F32)


def _dot_tn(a, b):
    return lax.dot_general(a, b, (((0,), (0,)), ((), ())), preferred_element_type=F32)


def _sigmoid(z):
    return 1.0 / (1.0 + jnp.exp(-z))


def _const_spec(shape):
    nd = len(shape)
    return pl.BlockSpec(shape, lambda i: (0,) * nd, pipeline_mode=pl.Buffered(1))


def _tables(seq):
    f32 = np.float32
    pos = np.arange(seq, dtype=f32)
    theta = (f32(1.0) / (f32(ROT_BASE) ** np.linspace(0.0, 1.0, RET_QK_DIM // 2, dtype=f32))).astype(f32)
    ang = (pos[:, None] * theta[None, :]).astype(f32)
    cos, sin = np.cos(ang), np.sin(ang)
    cos2 = np.repeat(cos, 2, axis=1)
    sin2 = np.stack([-sin, sin], axis=-1).reshape(seq, RET_QK_DIM)
    cos_t = np.tile(cos2, (1, 2))
    sin_t = np.tile(sin2, (1, 2))

    log_gamma = np.log(f32(1.0) - f32(2.0) ** (f32(-5.0) - np.arange(RET_HEADS, dtype=f32))).astype(f32)
    idx = np.arange(BLK, dtype=f32)
    rel = idx[:, None] - idx[None, :]
    decay_in = np.where(rel >= 0, np.exp(log_gamma[:, None, None] * np.maximum(rel, f32(0.0))), f32(0.0))
    k_dec = np.exp(log_gamma[:, None] * (BLK - 1 - idx)[None, :])
    q_dec = np.exp(log_gamma[:, None] * (idx + 1)[None, :])
    chunk_decay = np.exp(log_gamma * f32(BLK))
    kdec_t = np.repeat(k_dec.T, RET_QK_DIM, axis=1)
    qdec_t = np.repeat(q_dec.T, RET_QK_DIM, axis=1)
    cd_t = np.broadcast_to(chunk_decay[:, None, None], (RET_HEADS, RET_QK_DIM, RET_V_DIM))
    decay_in = decay_in.reshape(PAIRS, 2 * BLK, BLK)
    cd_t = cd_t.reshape(PAIRS, 2 * RET_QK_DIM, RET_V_DIM)

    key = np.arange(BLK)[:, None]
    query = np.arange(2 * BLK)[None, :] % BLK
    bias = np.stack([np.where(key > query, NEG_INF, 0.0), np.zeros((BLK, 2 * BLK))])
    return tuple(jnp.asarray(np.ascontiguousarray(a), F32) for a in (cos_t, sin_t, decay_in, qdec_t, kdec_t, cd_t, bias))


def _low_lanes(shape):
    lane = lax.broadcasted_iota(jnp.int32, shape, len(shape) - 1)
    return (lane & HALF_LANE) == 0


def _split_heads(t):
    low = _low_lanes(t.shape)
    zero = jnp.zeros_like(t)
    return jnp.where(low, t, zero), jnp.where(low, zero, t)


def _swap_pairs(t):
    lane = lax.broadcasted_iota(jnp.int32, t.shape, 1)
    nxt = pltpu.roll(t, t.shape[1] - 1, 1)
    prv = pltpu.roll(t, 1, 1)
    return jnp.where((lane & 1) == 0, nxt, prv)


def _per_tile(fn, t):
    return jnp.concatenate([fn(_tile(t, i)) for i in range(t.shape[1] // LANE)], axis=1)


def _rotate(t, cos_t, sin_t):
    return _per_tile(lambda a: a * cos_t + _swap_pairs(a) * sin_t, t)


def _rotate_transposed(d, cos_t, sin_t):
    return _per_tile(lambda a: a * cos_t + _swap_pairs(a * sin_t), d)


def _kv_operands(cat):
    low = _low_lanes(cat.shape)
    swapped = pltpu.roll(cat, HALF_LANE, 1)
    zero = jnp.zeros_like(cat)
    pick = lambda a, b: jnp.where(low, a, b).astype(BF16)
    return ((pick(cat, zero), pick(zero, swapped)), (pick(swapped, zero), pick(zero, cat)))


def _stack_tiles(t, first_tile):
    a = t[:, first_tile * LANE:(first_tile + 1) * LANE]
    b = t[:, (first_tile + 1) * LANE:(first_tile + 2) * LANE]
    return jnp.concatenate([a, b], axis=0)


def _sink_rows(sinks_ref, group):
    first = lax.broadcasted_iota(jnp.int32, (1, 2 * BLK), 1) < BLK

    def row(h0, h1):
        return jnp.where(first, sinks_ref[0, group * 4 + h0], sinks_ref[0, group * 4 + h1])
    return row(0, 2), row(1, 3)


ATT_PROBLEMS = tuple((g, hi) for g in range(ATT_KV_HEADS) for hi in range(2))


def _in_previous_block():
    key = lax.broadcasted_iota(jnp.int32, (BLK, 2 * BLK), 0)
    query = lax.broadcasted_iota(jnp.int32, (BLK, 2 * BLK), 1) & (BLK - 1)
    return key > query


def _fold(t, prev):
    return jnp.where(prev, t[0:BLK], t[BLK:])


def _unfold(t, prev):
    zero = jnp.zeros_like(t)
    return jnp.concatenate([jnp.where(prev, t, zero), jnp.where(prev, zero, t)], axis=0)


def _attn_probs(qs, k_ops, bias, prev, sinks_ref):
    sink = [_sink_rows(sinks_ref, g)[hi] for g, hi in ATT_PROBLEMS]
    s = [_fold(_dot_nt(k_ops[g][hi], qs[g]), prev) + bias for g, hi in ATT_PROBLEMS]
    m = [jnp.maximum(jnp.max(si, axis=0, keepdims=True), ki) for si, ki in zip(s, sink)]
    e = [jnp.exp(si - mi) for si, mi in zip(s, m)]
    es = [jnp.exp(ki - mi) for ki, mi in zip(sink, m)]
    inv = [1.0 / (jnp.sum(ei, axis=0, keepdims=True) + esi) for ei, esi in zip(e, es)]
    return [ei * ii for ei, ii in zip(e, inv)], [esi * ii for esi, ii in zip(es, inv)]


def _group_norm_all(outs):
    mu = [jnp.mean(o, axis=1, keepdims=True) for o in outs]
    xc = [o - m for o, m in zip(outs, mu)]
    var = [jnp.mean(c * c, axis=1, keepdims=True) for c in xc]
    rstd = [lax.rsqrt(v + GN_EPS) for v in var]
    return [c * r for c, r in zip(xc, rstd)], rstd


def _retention_operands(p_ref, rows, cos_b, sin_b, qdec, kdec):
    qr = _rotate(p_ref[rows, OFF_RQ:OFF_RQ + RET_QK_WIDTH].astype(F32), cos_b, sin_b)
    kr = _rotate(p_ref[rows, OFF_RK:OFF_RK + RET_QK_WIDTH].astype(F32), cos_b, sin_b) * RET_SCALE
    return qr, kr, qr * qdec, kr * kdec


def _tile(t, i):
    return t[:, i * LANE:(i + 1) * LANE]


def _weights_allgather(w_in_t_shard, w_out_shard):
    half_in = SHARD_IN // 2
    half_out = SHARD_OUT // 2
    assert half_in % BF16_ROWS == 0 and half_out % BF16_ROWS == 0

    def body(wi_ref, wo_ref, win_ref, wout_ref, blk_ref, blko_ref, send_sems, recv_sems):
        x, y, c = lax.axis_index("x"), lax.axis_index("y"), lax.axis_index("c")
        me, sibling = (x, y, c), (x, y, 1 - c)
        chips = [(1 - x, y), (x, 1 - y), (1 - x, 1 - y)]

        def rows_in(px, py, pc):
            return win_ref.at[pl.ds(pl.multiple_of((4 * px + 2 * py + pc) * half_in, BF16_ROWS), half_in), :]

        def rows_out(px, py, pc):
            return wout_ref.at[pl.ds(pl.multiple_of((4 * px + 2 * py + pc) * half_out, half_out), half_out), :]

        def copy(k, rows, block, to, src=None):
            return pltpu.make_async_remote_copy(
                src_ref=rows(*block) if src is None else src, dst_ref=rows(*block),
                send_sem=send_sems.at[k], recv_sem=recv_sems.at[k], device_id=to, device_id_type=MESH_ID)

        blk_ref[...] = wi_ref[pl.ds(pl.multiple_of(c * half_in, BF16_ROWS), half_in), :].astype(BF16)
        blko_ref[...] = wo_ref[pl.ds(pl.multiple_of(c * half_out, half_out), half_out), :].astype(BF16)
        rows_in(*me)[...] = blk_ref[...]
        rows_out(*me)[...] = blko_ref[...]

        sets = ((0, rows_in, blk_ref), (7, rows_out, blko_ref))
        first, passed = [], []
        for base, rows, src in sets:
            first.append(copy(base, rows, me, sibling, src=src))
            first += [copy(base + 1 + j, rows, me, (*chip, c), src=src) for j, chip in enumerate(chips)]
        for cp in first:
            cp.start()
        for base, rows, src in sets:
            for j, chip in enumerate(chips):
                copy(base + 1 + j, rows, (*chip, c), me).wait_recv()
                fwd = copy(base + 4 + j, rows, (*chip, c), sibling)
                fwd.start()
                passed.append(fwd)
        for base, rows, src in sets:
            copy(base, rows, sibling, me).wait_recv()
            for j, chip in enumerate(chips):
                copy(base + 4 + j, rows, (*chip, 1 - c), me).wait_recv()
        for cp in first + passed:
            cp.wait_send()

    vmem = pl.BlockSpec(memory_space=pltpu.VMEM)
    return _call(
        body, name="weights_allgather",
        out_shape=(jax.ShapeDtypeStruct((IN_WIDTH, D_MODEL), BF16), jax.ShapeDtypeStruct((MIX_WIDTH, D_MODEL), BF16)),
        in_specs=[vmem, vmem], out_specs=(vmem, vmem),
        scratch_shapes=[
            pltpu.VMEM((half_in, D_MODEL), BF16),
            pltpu.VMEM((half_out, D_MODEL), BF16),
            pltpu.SemaphoreType.DMA((14,)),
            pltpu.SemaphoreType.DMA((14,)),
        ],
        compiler_params=_params(32),
    )(w_in_t_shard, w_out_shard)


def _in_proj(x, norm_g, w_in, tb):
    seq = x.shape[0]

    def body(x_ref, g_ref, w_ref, p_ref, ht_ref):
        xv = x_ref[...]
        r = lax.rsqrt(jnp.mean(xv * xv, axis=1, keepdims=True) + RMS_EPS)
        h = (xv * r) * g_ref[...]
        p_ref[...] = _dot_nt(h.astype(BF16), w_ref[...]).astype(BF16)
        ht_ref[...] = h.T.astype(BF16)

    return _call(
        body, name="in_proj", grid=(seq // tb,),
        out_shape=(jax.ShapeDtypeStruct((seq, IN_WIDTH), BF16), jax.ShapeDtypeStruct((D_MODEL, seq), BF16)),
        in_specs=[pl.BlockSpec((tb, D_MODEL), lambda i: (i, 0)), _const_spec((1, D_MODEL)),
                  _const_spec((IN_WIDTH, D_MODEL))],
        out_specs=(pl.BlockSpec((tb, IN_WIDTH), lambda i: (i, 0)), pl.BlockSpec((D_MODEL, tb), lambda i: (0, i))),
        compiler_params=_params(56, ("arbitrary",)),
    )(x, norm_g, w_in)


def _mix_fwd(proj, x, target, w_out, final_g, gn_gain, sinks, tables, tb):
    seq = x.shape[0]
    nsub = tb // BLK
    cos_t, sin_t, decay_in, qdec_t, kdec_t, cd_t, bias_t = tables

    def body(p_ref, x_ref, t_ref, cos_ref, sin_ref, wout_ref, fg_ref, gain_ref, sinks_ref, din_ref, qdec_ref,
             kdec_ref, cd_ref, bias_ref, mix_ref, dxo_ref, st_ref, loss_ref, gfin_ref, oatt_ref,
             kprev_ref, vprev_ref, state_ref):
        i = pl.program_id(0)

        @pl.when(i == 0)
        def _():
            kprev_ref[...] = jnp.zeros_like(kprev_ref)
            vprev_ref[...] = jnp.zeros_like(vprev_ref)
            state_ref[...] = jnp.zeros_like(state_ref)
            loss_ref[...] = jnp.zeros_like(loss_ref)
            gfin_ref[...] = jnp.zeros_like(gfin_ref)

        prev = _in_previous_block()

        def sub(j, carry):
            kp, vp, states = carry
            rows = pl.ds(pl.multiple_of(j * BLK, BLK), BLK)
            bias = bias_ref[jnp.where(jnp.logical_or(i > 0, j > 0), 1, 0)]

            aq = p_ref[rows, OFF_AQ:OFF_AQ + ATT_WIDTH]
            ak = p_ref[rows, OFF_AK:OFF_AK + ATT_KV_WIDTH].astype(F32)
            av = p_ref[rows, OFF_AV:OFF_AV + ATT_KV_WIDTH].astype(F32)
            az = p_ref[rows, OFF_AZ:OFF_AZ + ATT_WIDTH].astype(F32)
            k_ops = _kv_operands(jnp.concatenate([kp, ak], axis=0))
            v_ops = _kv_operands(jnp.concatenate([vp, av], axis=0))
            qs = [(_stack_tiles(aq, 2 * g) * ATT_SCALE).astype(BF16) for g in range(ATT_KV_HEADS)]
            p, _ = _attn_probs(qs, k_ops, bias, prev, sinks_ref)
            o_tiles = []
            for g in range(ATT_KV_HEADS):
                p_cat = jnp.concatenate([_unfold(p[2 * g].astype(BF16), prev), _unfold(p[2 * g + 1].astype(BF16), prev)],
                                        axis=0)
                o = _dot_tn(p_cat, jnp.concatenate(v_ops[g], axis=0))
                o_tiles += [o[0:BLK], o[BLK:]]
            o_att = jnp.concatenate(o_tiles, axis=1)
            oatt_ref[rows, :] = o_att
            out = [o_att * (az * _sigmoid(az))]

            qr, kr, qd, kd = _retention_operands(p_ref, rows, cos_ref[rows, :], sin_ref[rows, :],
                                                 qdec_ref[...], kdec_ref[...])
            heads = [(t, hh) for t in range(PAIRS) for hh in range(2)]
            sc = [_dot_nt(jnp.concatenate(_split_heads(_tile(qr, t)), axis=0).astype(BF16), _tile(kr, t).astype(BF16))
                  * din_ref[t] for t in range(PAIRS)]
            qd_heads = [_split_heads(_tile(qd, t)) for t in range(PAIRS)]
            state_b = [states[t].astype(BF16) for t in range(PAIRS)]
            vs = [p_ref[rows, OFF_RV + h * RET_V_DIM:OFF_RV + (h + 1) * RET_V_DIM].astype(BF16) for h in range(RET_HEADS)]
            rzs = [p_ref[rows, OFF_RZ + h * RET_V_DIM:OFF_RZ + (h + 1) * RET_V_DIM].astype(F32) for h in range(RET_HEADS)]
            lhs = [jnp.concatenate([sc[t][hh * BLK:(hh + 1) * BLK].astype(BF16), qd_heads[t][hh].astype(BF16)], axis=1)
                   for t, hh in heads]
            ons, _ = _group_norm_all([_dot(lhs[2 * t + hh], jnp.concatenate([vs[2 * t + hh], state_b[t]], axis=0))
                                      for t, hh in heads])
            out += [(ons[h] * gain_ref[:, h * RET_V_DIM:(h + 1) * RET_V_DIM]) * (rzs[h] * _sigmoid(rzs[h]))
                    for h in range(RET_HEADS)]
            new_states = [states[t] * cd_ref[t]
                          + _dot_tn(jnp.concatenate(_split_heads(_tile(kd, t)), axis=0).astype(BF16),
                                    jnp.concatenate([vs[2 * t], vs[2 * t + 1]], axis=0)) for t in range(PAIRS)]
            mix_ref[rows, :] = jnp.concatenate(out, axis=1).astype(BF16)
            st_ref[j] = jnp.stack(states)
            return ak, av, tuple(new_states)

        carry = (kprev_ref[...], vprev_ref[...], tuple(state_ref[t] for t in range(PAIRS)))
        kp, vp, states = lax.fori_loop(0, nsub, sub, carry)
        kprev_ref[...] = kp
        vprev_ref[...] = vp
        state_ref[...] = jnp.stack(states)

        xo = x_ref[...] + _dot(mix_ref[...], wout_ref[...])
        r2 = lax.rsqrt(jnp.mean(xo * xo, axis=1, keepdims=True) + RMS_EPS)
        xn = xo * r2
        err = xn * fg_ref[...] - t_ref[...]
        loss_ref[...] += jnp.sum(err * err) * (0.5 / D_MODEL)
        dy = err * (1.0 / D_MODEL)
        gfin_ref[...] += jnp.sum(dy * xn, axis=0, keepdims=True)
        u = dy * fg_ref[...]
        dxo_ref[...] = r2 * u - xn * (r2 * jnp.mean(u * xn, axis=1, keepdims=True))

    blk_rows = lambda w: pl.BlockSpec((tb, w), lambda i: (i, 0))
    state_shape = (PAIRS, 2 * RET_QK_DIM, RET_V_DIM)
    return _call(
        body, name="mix_fwd", grid=(seq // tb,),
        out_shape=(
            jax.ShapeDtypeStruct((seq, MIX_WIDTH), BF16),
            jax.ShapeDtypeStruct((seq, D_MODEL), F32),
            jax.ShapeDtypeStruct((seq // BLK,) + state_shape, F32),
            jax.ShapeDtypeStruct((8, LANE), F32),
            jax.ShapeDtypeStruct((1, D_MODEL), F32),
            jax.ShapeDtypeStruct((seq, ATT_WIDTH), F32),
        ),
        in_specs=[
            blk_rows(IN_WIDTH), blk_rows(D_MODEL), blk_rows(D_MODEL), blk_rows(LANE), blk_rows(LANE),
            _const_spec((MIX_WIDTH, D_MODEL)), _const_spec((1, D_MODEL)), _const_spec((1, RET_WIDTH)),
            pl.BlockSpec(memory_space=pltpu.SMEM),
            _const_spec((PAIRS, 2 * BLK, BLK)), _const_spec((BLK, RET_QK_WIDTH)), _const_spec((BLK, RET_QK_WIDTH)),
            _const_spec(state_shape), _const_spec((2, BLK, 2 * BLK)),
        ],
        out_specs=(
            blk_rows(MIX_WIDTH), blk_rows(D_MODEL),
            pl.BlockSpec((nsub,) + state_shape, lambda i: (i, 0, 0, 0)),
            _const_spec((8, LANE)), _const_spec((1, D_MODEL)), blk_rows(ATT_WIDTH),
        ),
        scratch_shapes=[
            pltpu.VMEM((BLK, ATT_KV_WIDTH), F32), pltpu.VMEM((BLK, ATT_KV_WIDTH), F32),
            pltpu.VMEM(state_shape, F32),
        ],
        compiler_params=_params(48, ("arbitrary",)),
    )(proj, x, target, cos_t, sin_t, w_out, final_g, gn_gain, sinks, decay_in, qdec_t, kdec_t, cd_t, bias_t)


def _mix_bwd(proj, dxo, mix, o_att, states, w_out, gn_gain, sinks, tables, tb):
    seq = dxo.shape[0]
    nsub = tb // BLK
    nblk = seq // tb
    cos_t, sin_t, decay_in, qdec_t, kdec_t, cd_t, bias_t = tables
    kv_cols = OFF_AK // (2 * ATT_KV_WIDTH)
    state_shape = (PAIRS, 2 * RET_QK_DIM, RET_V_DIM)

    def body(p_ref, pkv_ref, dxo_ref, mix_ref, oatt_ref, st_ref, cos_ref, sin_ref, wout_ref, gain_ref, sinks_ref, din_ref,
             qdec_ref, kdec_ref, cd_ref, bias_ref, dp_ref, gwout_ref, dgain_ref, dsink_ref,
             dmix_ref, kv_ref, dkc_ref, dvc_ref, gst_ref):
        i = pl.program_id(0)
        blk = nblk - 1 - i

        @pl.when(i == 0)
        def _():
            gwout_ref[...] = jnp.zeros_like(gwout_ref)
            dgain_ref[...] = jnp.zeros_like(dgain_ref)
            dsink_ref[...] = jnp.zeros_like(dsink_ref)
            dkc_ref[...] = jnp.zeros_like(dkc_ref)
            dvc_ref[...] = jnp.zeros_like(dvc_ref)
            gst_ref[...] = jnp.zeros_like(gst_ref)

        dxo_b = dxo_ref[...].astype(BF16)
        dmix_ref[...] = _dot_nt(dxo_b, wout_ref[...])
        gwout_ref[...] += _dot_tn(mix_ref[...], dxo_b)
        kv_ref[0:BLK, :] = pkv_ref[...].astype(F32)
        kv_ref[BLK:, :] = p_ref[:, OFF_AK:OFF_AK + 2 * ATT_KV_WIDTH].astype(F32)
        low = _low_lanes((BLK, LANE))
        low2 = _low_lanes((2 * BLK, LANE))
        lane = lax.broadcasted_iota(jnp.int32, (1, LANE), 1)
        prev = _in_previous_block()

        def sub(jj, carry):
            dkc, dvc, gsts, dgain, dsink = carry
            j = nsub - 1 - jj
            rows = pl.ds(pl.multiple_of(j * BLK, BLK), BLK)
            both = pl.ds(pl.multiple_of(j * BLK, BLK), 2 * BLK)
            bias = bias_ref[jnp.where(jnp.logical_or(blk > 0, j > 0), 1, 0)]

            aq = p_ref[rows, OFF_AQ:OFF_AQ + ATT_WIDTH]
            az = p_ref[rows, OFF_AZ:OFF_AZ + ATT_WIDTH].astype(F32)
            k_ops = _kv_operands(kv_ref[both, 0:ATT_KV_WIDTH])
            v_ops = _kv_operands(kv_ref[both, ATT_KV_WIDTH:2 * ATT_KV_WIDTH])
            da = dmix_ref[rows, 0:ATT_WIDTH]
            sig = _sigmoid(az)
            d_o = da * (az * sig)
            qs = [(_stack_tiles(aq, 2 * g) * ATT_SCALE).astype(BF16) for g in range(ATT_KV_HEADS)]
            dos = [_stack_tiles(d_o, 2 * g).astype(BF16) for g in range(ATT_KV_HEADS)]
            p, p_sink = _attn_probs(qs, k_ops, bias, prev, sinks_ref)
            dpr = [_fold(_dot_nt(v_ops[g][hi], dos[g]), prev) for g, hi in ATT_PROBLEMS]
            delta = [jnp.sum(pi * di, axis=0, keepdims=True) for pi, di in zip(p, dpr)]
            ds = [_unfold((pi * (di - ti)).astype(BF16), prev) for pi, di, ti in zip(p, dpr, delta)]
            for (g, hi), ki, ti in zip(ATT_PROBLEMS, p_sink, delta):
                sink_part = ki * ti
                for half in range(2):
                    tot = jnp.sum(sink_part[:, half * BLK:(half + 1) * BLK], axis=1, keepdims=True)
                    dsink = dsink - jnp.where(lane == 4 * g + 2 * half + hi, tot, 0.0)
            dq_tiles, dk_sums, dv_sums = [], [], []
            for g in range(ATT_KV_HEADS):
                ds_cat = jnp.concatenate([ds[2 * g], ds[2 * g + 1]], axis=0)
                p_cat = jnp.concatenate([_unfold(p[2 * g].astype(BF16), prev), _unfold(p[2 * g + 1].astype(BF16), prev)],
                                        axis=0)
                dqs = _dot_tn(ds_cat, jnp.concatenate(k_ops[g], axis=0)) * ATT_SCALE
                dq_tiles += [dqs[0:BLK], dqs[BLK:]]
                dk_sums.append(_dot(ds_cat, qs[g]))
                dv_sums.append(_dot(p_cat, dos[g]))
            daz = da * oatt_ref[rows, :] * (sig * (1.0 + az * (1.0 - sig)))

            def kv_grad(sums):
                (a0, b0), (a1, b1) = [(s[0:2 * BLK], s[2 * BLK:]) for s in sums]
                return jnp.where(low2, a0, b1) + pltpu.roll(jnp.where(low2, a1, b0), HALF_LANE, 1)

            dk_both, dv_both = kv_grad(dk_sums), kv_grad(dv_sums)
            dak = dk_both[BLK:] + dkc
            dav = dv_both[BLK:] + dvc

            cos_b, sin_b = cos_ref[rows, :], sin_ref[rows, :]
            qdec, kdec = qdec_ref[...], kdec_ref[...]
            qr, kr, qd, kd = _retention_operands(p_ref, rows, cos_b, sin_b, qdec, kdec)
            heads = [(t, hh) for t in range(PAIRS) for hh in range(2)]
            head_cols = [slice(h * RET_V_DIM, (h + 1) * RET_V_DIM) for h in range(RET_HEADS)]
            q_rows = [jnp.concatenate(_split_heads(_tile(qr, t)), axis=0).astype(BF16) for t in range(PAIRS)]
            k_rows = [jnp.concatenate(_split_heads(_tile(kr, t)), axis=0).astype(BF16) for t in range(PAIRS)]
            din = [din_ref[t] for t in range(PAIRS)]
            sc = [(_dot_nt(q_rows[t], _tile(kr, t).astype(BF16)) * din[t]).astype(BF16) for t in range(PAIRS)]
            qd_heads = [_split_heads(_tile(qd, t)) for t in range(PAIRS)]
            kd_heads = [_split_heads(_tile(kd, t)) for t in range(PAIRS)]
            state_b = [st_ref[j, t].astype(BF16) for t in range(PAIRS)]
            gst_b = [gsts[t].astype(BF16) for t in range(PAIRS)]
            vs = [p_ref[rows, OFF_RV + h * RET_V_DIM:OFF_RV + (h + 1) * RET_V_DIM].astype(BF16) for h in range(RET_HEADS)]
            rzs = [p_ref[rows, OFF_RZ + h * RET_V_DIM:OFF_RZ + (h + 1) * RET_V_DIM].astype(F32) for h in range(RET_HEADS)]
            drs = [dmix_ref[rows, ATT_WIDTH + h * RET_V_DIM:ATT_WIDTH + (h + 1) * RET_V_DIM] for h in range(RET_HEADS)]
            gains = [gain_ref[:, c] for c in head_cols]
            lhs = [jnp.concatenate([sc[t][hh * BLK:(hh + 1) * BLK], qd_heads[t][hh].astype(BF16)], axis=1) for t, hh in heads]
            rhs = [jnp.concatenate([vs[2 * t + hh], state_b[t]], axis=0) for t, hh in heads]
            ons, rstds = _group_norm_all([_dot(l, r) for l, r in zip(lhs, rhs)])
            sig_r = [_sigmoid(z) for z in rzs]
            dgn = [d * (z * g) for d, z, g in zip(drs, rzs, sig_r)]
            dz_parts = [d * (o * gn) * (g * (1.0 + z * (1.0 - g))) for d, o, gn, g, z in zip(drs, ons, gains, sig_r, rzs)]
            dgain_parts = [jnp.sum(d * o, axis=0, keepdims=True) for d, o in zip(dgn, ons)]
            don = [d * gn for d, gn in zip(dgn, gains)]
            mean_don = [jnp.mean(d, axis=1, keepdims=True) for d in don]
            mean_don_on = [jnp.mean(d * o, axis=1, keepdims=True) for d, o in zip(don, ons)]
            dob = [(r * (d - a - o * b)).astype(BF16) for r, d, a, o, b in zip(rstds, don, mean_don, ons, mean_don_on)]
            dlhs = [_dot_nt(d, r) for d, r in zip(dob, rhs)]
            drhs = [_dot_tn(l, d) for l, d in zip(lhs, dob)]
            dkds = [_dot_nt(vs[2 * t + hh], gst_b[t]) for t, hh in heads]
            dv_parts = [drhs[2 * t + hh][0:BLK] + _dot(kd_heads[t][hh].astype(BF16), gst_b[t]) for t, hh in heads]
            das = [(dlhs[2 * t + hh][:, 0:BLK] * din[t][hh * BLK:(hh + 1) * BLK]).astype(BF16) for t, hh in heads]
            new_gsts = [gsts[t] * cd_ref[t] + drhs[2 * t][BLK:] + drhs[2 * t + 1][BLK:] for t in range(PAIRS)]
            dq_parts = [_dot(jnp.concatenate([das[2 * t], das[2 * t + 1]], axis=1), k_rows[t])
                        + jnp.where(low, dlhs[2 * t][:, BLK:], dlhs[2 * t + 1][:, BLK:]) * _tile(qdec, t)
                        for t in range(PAIRS)]
            dk_parts = [_dot_tn(jnp.concatenate([das[2 * t], das[2 * t + 1]], axis=0), q_rows[t])
                        + jnp.where(low, dkds[2 * t], dkds[2 * t + 1]) * _tile(kdec, t) for t in range(PAIRS)]
            drq = _rotate_transposed(jnp.concatenate(dq_parts, axis=1), cos_b, sin_b)
            drk = _rotate_transposed(jnp.concatenate(dk_parts, axis=1) * RET_SCALE, cos_b, sin_b)

            dp_ref[rows, :] = jnp.concatenate(
                [jnp.concatenate(dq_tiles, axis=1), dak, dav, daz, drq, drk] + dv_parts + dz_parts, axis=1).astype(BF16)
            dgain = dgain + jnp.concatenate(dgain_parts, axis=1)
            return dk_both[0:BLK], dv_both[0:BLK], tuple(new_gsts), dgain, dsink

        carry = (dkc_ref[...], dvc_ref[...], tuple(gst_ref[t] for t in range(PAIRS)), dgain_ref[...], dsink_ref[...])
        dkc, dvc, gsts, dgain, dsink = lax.fori_loop(0, nsub, sub, carry)
        dkc_ref[...] = dkc
        dvc_ref[...] = dvc
        gst_ref[...] = jnp.stack(gsts)
        dgain_ref[...] = dgain
        dsink_ref[...] = dsink

    rev_rows = lambda w: pl.BlockSpec((tb, w), lambda i: (nblk - 1 - i, 0))
    prev_kv = pl.BlockSpec((BLK, 2 * ATT_KV_WIDTH), lambda i: (jnp.maximum((nblk - 1 - i) * nsub - 1, 0), kv_cols))
    return _call(
        body, name="mix_bwd", grid=(nblk,),
        out_shape=(
            jax.ShapeDtypeStruct((seq, IN_WIDTH), BF16),
            jax.ShapeDtypeStruct((MIX_WIDTH, D_MODEL), F32),
            jax.ShapeDtypeStruct((1, RET_WIDTH), F32),
            jax.ShapeDtypeStruct((1, LANE), F32),
        ),
        in_specs=[
            rev_rows(IN_WIDTH), prev_kv, rev_rows(D_MODEL), rev_rows(MIX_WIDTH), rev_rows(ATT_WIDTH),
            pl.BlockSpec((nsub,) + state_shape, lambda i: (nblk - 1 - i, 0, 0, 0)),
            rev_rows(LANE), rev_rows(LANE),
            _const_spec((MIX_WIDTH, D_MODEL)), _const_spec((1, RET_WIDTH)),
            pl.BlockSpec(memory_space=pltpu.SMEM),
            _const_spec((PAIRS, 2 * BLK, BLK)), _const_spec((BLK, RET_QK_WIDTH)), _const_spec((BLK, RET_QK_WIDTH)),
            _const_spec(state_shape), _const_spec((2, BLK, 2 * BLK)),
        ],
        out_specs=(
            rev_rows(IN_WIDTH), _const_spec((MIX_WIDTH, D_MODEL)), _const_spec((1, RET_WIDTH)), _const_spec((1, LANE)),
        ),
        scratch_shapes=[
            pltpu.VMEM((tb, MIX_WIDTH), F32),
            pltpu.VMEM((tb + BLK, 2 * ATT_KV_WIDTH), F32),
            pltpu.VMEM((BLK, ATT_KV_WIDTH), F32), pltpu.VMEM((BLK, ATT_KV_WIDTH), F32),
            pltpu.VMEM(state_shape, F32),
        ],
        compiler_params=_params(56, ("arbitrary",)),
    )(proj, proj, dxo, mix, o_att, states, cos_t, sin_t, w_out, gn_gain, sinks, decay_in, qdec_t, kdec_t, cd_t, bias_t)


def _in_proj_bwd(dproj, x, dxo, norm_g, w_in, tb):
    seq = x.shape[0]

    def body(dp_ref, x_ref, dxo_ref, g_ref, w_ref, gx_ref, gnorm_ref):
        i = pl.program_id(0)

        @pl.when(i == 0)
        def _():
            gnorm_ref[...] = jnp.zeros_like(gnorm_ref)

        xv = x_ref[...]
        r = lax.rsqrt(jnp.mean(xv * xv, axis=1, keepdims=True) + RMS_EPS)
        xn = xv * r
        dh = _dot(dp_ref[...], w_ref[...])
        gnorm_ref[...] += jnp.sum(dh * xn, axis=0, keepdims=True)
        u = dh * g_ref[...]
        gx_ref[...] = dxo_ref[...] + r * u - xn * (r * jnp.mean(u * xn, axis=1, keepdims=True))

    rows = lambda w: pl.BlockSpec((tb, w), lambda i: (i, 0))
    return _call(
        body, name="in_proj_bwd", grid=(seq // tb,),
        out_shape=(jax.ShapeDtypeStruct((seq, D_MODEL), F32), jax.ShapeDtypeStruct((1, D_MODEL), F32)),
        in_specs=[rows(IN_WIDTH), rows(D_MODEL), rows(D_MODEL), _const_spec((1, D_MODEL)),
                  _const_spec((IN_WIDTH, D_MODEL))],
        out_specs=(rows(D_MODEL), _const_spec((1, D_MODEL))),
        compiler_params=_params(48, ("arbitrary",)),
    )(dproj, x, dxo, norm_g, w_in)


def _gw_in_reduce(h_t, dproj, gw_out, small, tb):
    seq = dproj.shape[0]
    nblk = seq // tb
    last = nblk - 1
    hand_on = min(1, last)
    half = D_MODEL // 2
    A, B, C, N_SEMS = 0, N_CHIPS, 2 * N_CHIPS, 2 * N_CHIPS + 1

    def body(win_ref, ht_ref, dp_ref, gwo_hbm, s0_ref, s1_ref, s2_ref, s3_ref, s4_ref, out_ref, fout_ref, packsum_ref,
             acc, sib, send_buf, b_in, fin, fout, mine_out, sib_out, send_out, b_out, pack_ref, packs,
             send_sems, recv_sems, local_sems):
        p, i = pl.program_id(0), pl.program_id(1)
        small_start, small_hand_on, small_finish = _small_exchange(
            gwo_hbm, (s0_ref, s1_ref, s2_ref, s3_ref, s4_ref), fout, mine_out, sib_out, send_out, b_out, pack_ref,
            packs, send_sems, recv_sems, local_sems, N_SEMS)

        @pl.when(jnp.logical_and(p == 0, i == 0))
        def _():
            small_start()

        @pl.when(jnp.logical_and(p == 0, i == hand_on))
        def _():
            small_hand_on()

        x, y, c = lax.axis_index("x"), lax.axis_index("y"), lax.axis_index("c")
        chip = 2 * x + y
        sibling = (x, y, 1 - c)
        mine = pl.ds(pl.multiple_of(c * half, half), half)
        other = pl.ds(pl.multiple_of((1 - c) * half, half), half)

        def remote(src, dst, send_k, recv_k, to):
            return pltpu.make_async_remote_copy(src_ref=src, dst_ref=dst, send_sem=send_sems.at[send_k],
                                                recv_sem=recv_sems.at[recv_k], device_id=to, device_id_type=MESH_ID)

        part = _dot(ht_ref[...], dp_ref[...])
        slot = p % 2

        @pl.when(i == 0)
        def _():
            acc[slot] = part

        @pl.when(i > 0)
        def _():
            acc[slot] += part

        for q in range(N_CHIPS):
            s = q % 2
            to_sibling = remote(acc.at[s, other, :], sib.at[s], A + q, A + q, sibling)

            @pl.when(jnp.logical_and(p == q, i == last))
            def _():
                to_sibling.start()

            if q < N_CHIPS - 1:
                dest = (chip + 1 + q) % N_CHIPS

                @pl.when(jnp.logical_and(p == q + 1, i == hand_on))
                def _():
                    to_sibling.wait_recv()
                    send_buf[q] = (acc[s, mine, :] + sib[s]).astype(BF16)
                    remote(send_buf.at[q], b_in.at[chip], B + q, B + chip, (dest // 2, dest % 2, c)).start()
                    to_sibling.wait_send()
            else:
                @pl.when(jnp.logical_and(p == q, i == last))
                def _():
                    to_sibling.wait_recv()
                    fin[mine, :] = acc[s, mine, :] + sib[s]
                    for j in range(N_CHIPS):
                        @pl.when(j != chip)
                        def _():
                            remote(b_in.at[j], b_in.at[j], B + j, B + j, sibling).wait_recv()
                            fin[mine, :] += b_in[j].astype(F32)
                    to_core = remote(fin.at[mine, :], fin.at[mine, :], C, C, sibling)
                    to_core.start()
                    remote(fin.at[other, :], fin.at[other, :], C, C, sibling).wait_recv()
                    out_ref[...] = fin[...]
                    to_core.wait_send()
                    to_sibling.wait_send()
                    for k in range(N_CHIPS - 1):
                        remote(send_buf.at[k], b_in.at[chip], B + k, B + k, sibling).wait_send()
                    packsum_ref[...] = small_finish()
                    fout_ref[...] = fout[...]

    whole = lambda shape: pl.BlockSpec(shape, lambda p, i, win: (0,) * len(shape), pipeline_mode=pl.Buffered(1))
    grid_spec = pltpu.PrefetchScalarGridSpec(
        num_scalar_prefetch=1, grid=(N_CHIPS, nblk),
        in_specs=[pl.BlockSpec((D_MODEL, tb), lambda p, i, win: (0, i)),
                  pl.BlockSpec((pl.Element(tb), pl.Element(SHARD_PAD)),
                               lambda p, i, win: (i * tb, pl.multiple_of(win[p] * LANE, LANE))),
                  pl.BlockSpec(memory_space=pl.ANY)] + [whole(s.shape) for s in small],
        out_specs=(whole((D_MODEL, SHARD_PAD)), whole((SHARD_OUT, D_MODEL)), whole((PACK_ROWS, D_MODEL))),
        scratch_shapes=[
            pltpu.VMEM((2, D_MODEL, SHARD_PAD), F32), pltpu.VMEM((2, half, SHARD_PAD), F32),
            pltpu.VMEM((N_CHIPS - 1, half, SHARD_PAD), BF16), pltpu.VMEM((N_CHIPS, half, SHARD_PAD), BF16),
            pltpu.VMEM((D_MODEL, SHARD_PAD), F32),
        ] + _small_exchange_scratch() + [
            pltpu.SemaphoreType.DMA((N_SEMS + SMALL_SEMS,)), pltpu.SemaphoreType.DMA((N_SEMS + SMALL_SEMS,)),
            pltpu.SemaphoreType.DMA((N_CHIPS,)),
        ])
    chip = 2 * lax.axis_index("x") + lax.axis_index("y")
    owner = (chip + 1 + jnp.arange(N_CHIPS, dtype=jnp.int32)) % N_CHIPS
    win_start = (owner * SHARD_IN) // LANE
    return _call(
        body, name="gw_in_reduce", grid_spec=grid_spec,
        out_shape=(jax.ShapeDtypeStruct((D_MODEL, SHARD_PAD), F32), jax.ShapeDtypeStruct((SHARD_OUT, D_MODEL), F32),
                   jax.ShapeDtypeStruct((PACK_ROWS, D_MODEL), F32)),
        compiler_params=_params(52, ("arbitrary", "arbitrary")),
    )(win_start.astype(jnp.int32), h_t, dproj, gw_out, *small)


SMALL_SEMS = 17


def _small_exchange(gwo_hbm, small_refs, fout_ref, mine_out, sib_out, send_out, b_out, pack_ref, packs,
                    send_sems, recv_sems, local_sems, base):
    half_out = SHARD_OUT // 2
    A_OUT, B_OUT, C_OUT, PACK = base, base + 4, base + 8, base + 9
    assert len(small_refs) == PACK_PARTS
    x, y, c = lax.axis_index("x"), lax.axis_index("y"), lax.axis_index("c")
    chip = 2 * x + y
    dev = 2 * chip + c
    sibling = (x, y, 1 - c)

    def remote(src, dst, send_k, recv_k, to):
        return pltpu.make_async_remote_copy(src_ref=src, dst_ref=dst, send_sem=send_sems.at[send_k],
                                            recv_sem=recv_sems.at[recv_k], device_id=to, device_id_type=MESH_ID)

    def out_rows(j, core):
        return pl.ds(pl.multiple_of(j * SHARD_OUT + core * half_out, half_out), half_out)

    my_out_rows = pl.ds(pl.multiple_of(c * half_out, half_out), half_out)
    local = [pltpu.make_async_copy(gwo_hbm.at[out_rows(j, c), :], mine_out.at[j], local_sems.at[j])
             for j in range(N_CHIPS)]
    stage_a = [remote(gwo_hbm.at[out_rows(j, 1 - c), :], sib_out.at[j], A_OUT + j, A_OUT + j, sibling)
               for j in range(N_CHIPS)]
    mine_half_out = fout_ref.at[my_out_rows, :]
    stage_c = [remote(mine_half_out, mine_half_out, C_OUT, C_OUT, sibling)]

    def start():
        pack_ref[...] = jnp.zeros_like(pack_ref)
        for k, s_ref in enumerate(small_refs):
            pack_ref[k:k + 1, 0:s_ref.shape[1]] = s_ref[0:1, :]
        packs[dev] = pack_ref[...]
        for d in range(N_DEV):
            to = (d // 4, (d // 2) % 2, d % 2)

            @pl.when(d != dev)
            def _():
                remote(pack_ref, packs.at[dev], PACK + d, PACK + dev, to).start()

        for cp in local + stage_a:
            cp.start()

    def hand_on():
        for cp in local:
            cp.wait()
        for cp in stage_a:
            cp.wait_recv()
        for j in range(N_CHIPS):
            mine_out[j] = mine_out[j] + sib_out[j]

        for j in range(N_CHIPS):
            to = (j // 2, j % 2, c)

            @pl.when(j != chip)
            def _():
                send_out[j] = mine_out[j].astype(BF16)
                remote(send_out.at[j], b_out.at[chip], B_OUT + j, B_OUT + chip, to).start()

            @pl.when(j == chip)
            def _():
                fout_ref[my_out_rows, :] = mine_out[j]

    def finish():
        for j in range(N_CHIPS):
            @pl.when(j != chip)
            def _():
                remote(b_out.at[j], b_out.at[j], B_OUT + j, B_OUT + j, sibling).wait_recv()
                fout_ref[my_out_rows, :] += b_out[j].astype(F32)

        for cp in stage_c:
            cp.start()
        other_half_out = fout_ref.at[pl.ds(pl.multiple_of((1 - c) * half_out, half_out), half_out), :]
        remote(other_half_out, other_half_out, C_OUT, C_OUT, sibling).wait_recv()

        for d in range(N_DEV):
            @pl.when(d != dev)
            def _():
                remote(pack_ref, packs.at[d], PACK + d, PACK + d, sibling).wait_recv()
        total = packs[0]
        for d in range(1, N_DEV):
            total = total + packs[d]

        for cp in stage_a + stage_c:
            cp.wait_send()
        for j in range(N_CHIPS):
            @pl.when(j != chip)
            def _():
                remote(b_out.at[j], b_out.at[j], B_OUT + j, B_OUT + j, sibling).wait_send()
        for d in range(N_DEV):
            @pl.when(d != dev)
            def _():
                remote(pack_ref, packs.at[d], PACK + d, PACK + d, sibling).wait_send()
        return total

    return start, hand_on, finish


def _small_exchange_scratch():
    half_out = SHARD_OUT // 2
    return [
        pltpu.VMEM((SHARD_OUT, D_MODEL), F32),
        pltpu.VMEM((N_CHIPS, half_out, D_MODEL), F32), pltpu.VMEM((N_CHIPS, half_out, D_MODEL), F32),
        pltpu.VMEM((N_CHIPS, half_out, D_MODEL), BF16), pltpu.VMEM((N_CHIPS, half_out, D_MODEL), BF16),
        pltpu.VMEM((PACK_ROWS, D_MODEL), F32), pltpu.VMEM((N_DEV, PACK_ROWS, D_MODEL), F32),
    ]


def _adam_math(w, g, m, v):
    mn = ADAM_B1 * m + (1.0 - ADAM_B1) * g
    vn = ADAM_B2 * v + (1.0 - ADAM_B2) * (g * g)
    m_hat = mn / (1.0 - ADAM_B1 ** ADAM_STEP)
    v_hat = vn / (1.0 - ADAM_B2 ** ADAM_STEP)
    return -ADAM_LR * (m_hat / (jnp.sqrt(v_hat) + ADAM_EPS) + ADAM_WD * w), mn, vn


def _adamw(name, w, g, m, v, tb):
    rows, cols = w.shape

    def body(w_ref, g_ref, m_ref, v_ref, go_ref, d_ref, mo_ref, vo_ref):
        gv = g_ref[...]
        go_ref[...] = gv
        d_ref[...], mo_ref[...], vo_ref[...] = _adam_math(w_ref[...], gv, m_ref[...], v_ref[...])

    spec = pl.BlockSpec((tb, cols), lambda i: (i, 0))
    shape = jax.ShapeDtypeStruct((rows, cols), F32)
    return _call(
        body, name=name, grid=(rows // tb,), out_shape=(shape,) * 4,
        in_specs=[spec] * 4, out_specs=(spec,) * 4,
        compiler_params=_params(32, ("arbitrary",)),
    )(w, g, m, v)


def _adamw_w_in(w_t, g_window, m_t, v_t, tb):
    def body(w_ref, g_ref, m_ref, v_ref, go_ref, d_ref, mo_ref, vo_ref, gt_ref):
        gt_ref[...] = g_ref[...].T
        gv = gt_ref[pl.ds(pl.multiple_of(lax.axis_index("y") * SHARD_SHIFT, SHARD_SHIFT), SHARD_IN), :]
        go_ref[...] = gv
        d_ref[...], mo_ref[...], vo_ref[...] = _adam_math(w_ref[...], gv, m_ref[...], v_ref[...])

    spec = pl.BlockSpec((SHARD_IN, tb), lambda i: (0, i))
    shape = jax.ShapeDtypeStruct((SHARD_IN, D_MODEL), F32)
    return _call(
        body, name="adamw_w_in", grid=(D_MODEL // tb,), out_shape=(shape,) * 4,
        in_specs=[spec, pl.BlockSpec((tb, SHARD_PAD), lambda i: (i, 0)), spec, spec], out_specs=(spec,) * 4,
        scratch_shapes=[pltpu.VMEM((SHARD_PAD, tb), F32)],
        compiler_params=_params(32, ("arbitrary",)),
    )(w_t, g_window, m_t, v_t)


def _adamw_small(sums, params):
    def body(sums_ref, *refs):
        ins, outs = refs[:3 * len(params)], refs[3 * len(params):]
        for k in range(len(params)):
            w_ref, m_ref, v_ref = ins[3 * k:3 * k + 3]
            g = sums_ref[k:k + 1, 0:w_ref.shape[1]]
            go_ref, d_ref, mo_ref, vo_ref = outs[4 * k:4 * k + 4]
            go_ref[...] = g
            d_ref[...], mo_ref[...], vo_ref[...] = _adam_math(w_ref[...], g, m_ref[...], v_ref[...])

    vmem = pl.BlockSpec(memory_space=pltpu.VMEM)
    flat = [a for p in params for a in p]
    shapes = tuple(jax.ShapeDtypeStruct(p[0].shape, F32) for p in params for _ in range(4))
    res = _call(body, name="adamw_small", out_shape=shapes, in_specs=[vmem] * (1 + len(flat)),
                out_specs=(vmem,) * len(shapes), compiler_params=_params(16))(sums, *flat)
    return [res[4 * k:4 * k + 4] for k in range(len(params))]


def kernel(x, norm_g, w_in, att_sinks, ret_gn_g, w_out, final_g, loss_target, m_norm_g, m_w_in, m_att_sinks, m_ret_gn_g, m_w_out, m_final_g, v_norm_g, v_w_in, v_att_sinks, v_ret_gn_g, v_w_out, v_final_g):
    seq = x.shape[1]
    xs, tgt = x[0], loss_target[0]
    final_g2 = final_g.reshape(1, D_MODEL)
    tables = _tables(seq)

    w_in_t, m_w_in_t, v_w_in_t = w_in[0].T, m_w_in[0].T, v_w_in[0].T
    w_in_full, w_out_full = _weights_allgather(w_in_t, w_out[0])
    proj, h_t = _in_proj(xs, norm_g, w_in_full, min(TOKENS_PROJ, seq))
    mix, dxo, states, loss_part, gfin, o_att = _mix_fwd(proj, xs, tgt, w_out_full, final_g2, ret_gn_g, att_sinks,
                                                        tables, min(TOKENS_MIX, seq))
    dproj, gw_out, dgain, dsink = _mix_bwd(proj, dxo, mix, o_att, states, w_out_full, ret_gn_g, att_sinks, tables,
                                           min(TOKENS_MIX, seq))
    grad_x, gnorm = _in_proj_bwd(dproj, xs, dxo, norm_g, w_in_full, min(TOKENS_PROJ_BWD, seq))
    g_in, g_out, sums = _gw_in_reduce(h_t, dproj, gw_out, (gnorm, gfin, dgain, dsink, loss_part),
                                      min(TOKENS_GW, seq))

    res_in = [r.T for r in _adamw_w_in(w_in_t, g_in, m_w_in_t, v_w_in_t, 256)]
    res_out = _adamw("adamw_w_out", w_out[0], g_out, m_w_out[0], v_w_out[0], SHARD_OUT)
    as_row = lambda a: a.reshape(1, D_MODEL)
    r_norm, r_final, r_gain, r_sink = _adamw_small(sums, [
        (norm_g, m_norm_g, v_norm_g), (final_g2, as_row(m_final_g), as_row(v_final_g)),
        (ret_gn_g, m_ret_gn_g, v_ret_gn_g), (att_sinks, m_att_sinks, v_att_sinks)])

    outs = []
    for k in range(4):
        outs += [r_norm[k], res_in[k][None], r_sink[k], r_gain[k], res_out[k][None], r_final[k].reshape(D_MODEL)]
    return (sums[4, 0], grad_x[None], *outs)
```

```python
import jax
import jax.numpy as jnp
import numpy as np
from jax import lax
from jax.experimental import pallas as pl
from jax.experimental.pallas import tpu as pltpu

F32 = jnp.float32
BF16 = jnp.bfloat16

D_MODEL = 1024
ATT_HEADS = 8
ATT_KV_HEADS = 2
ATT_HEAD_DIM = 64
RET_HEADS = 4
RET_QK_DIM = 64
RET_V_DIM = 128
BLK = 128
ROT_BASE = 10000.0
RMS_EPS = 1e-6
GN_EPS = 1e-6
NEG_INF = -1e30
ATT_SCALE = ATT_HEAD_DIM ** -0.5
RET_SCALE = RET_QK_DIM ** -0.5

ATT_WIDTH = ATT_HEADS * ATT_HEAD_DIM
ATT_KV_WIDTH = ATT_KV_HEADS * ATT_HEAD_DIM
RET_QK_WIDTH = RET_HEADS * RET_QK_DIM
RET_WIDTH = RET_HEADS * RET_V_DIM
MIX_WIDTH = ATT_WIDTH + RET_WIDTH
OFF_AQ = 0
OFF_AK = OFF_AQ + ATT_WIDTH
OFF_AV = OFF_AK + ATT_KV_WIDTH
OFF_AZ = OFF_AV + ATT_KV_WIDTH
OFF_RQ = OFF_AZ + ATT_WIDTH
OFF_RK = OFF_RQ + RET_QK_WIDTH
OFF_RV = OFF_RK + RET_QK_WIDTH
OFF_RZ = OFF_RV + RET_WIDTH
IN_WIDTH = OFF_RZ + RET_WIDTH

LANE = 128
BF16_ROWS = 16
HALF_LANE = LANE // 2
PAIRS = RET_QK_WIDTH // LANE
assert ATT_HEAD_DIM == HALF_LANE and RET_QK_DIM == HALF_LANE and RET_V_DIM == LANE and ATT_KV_WIDTH == LANE

N_CHIPS = 4
N_DEV = 8
SHARD_IN = IN_WIDTH // N_CHIPS
SHARD_PAD = 768
SHARD_SHIFT = SHARD_PAD - SHARD_IN
WIN_START = tuple((j * SHARD_IN) // LANE * LANE for j in range(N_CHIPS))
SHARD_OUT = MIX_WIDTH // N_CHIPS

PACK_PARTS = 5
PACK_ROWS = 8
assert PACK_PARTS <= PACK_ROWS

ADAM_LR = 0.001
ADAM_B1 = 0.9
ADAM_B2 = 0.999
ADAM_EPS = 1e-08
ADAM_WD = 0.01
ADAM_STEP = 10

VMEM_CAP = 64 * 1024 * 1024
TOKENS_PROJ = 1024
TOKENS_MIX = 512
TOKENS_GW = 2048
MESH_ID = pl.DeviceIdType.MESH


def _call(body, **kw):
    return pl.pallas_call(body, **kw)


def _params(vmem_mb, semantics=None):
    assert vmem_mb * 1024 * 1024 < VMEM_CAP
    return pltpu.CompilerParams(dimension_semantics=semantics, vmem_limit_bytes=vmem_mb * 1024 * 1024)


def _dot(a, b):
    return jnp.dot(a, b, preferred_element_type=F32)


def _dot_nt(a, b):
    return lax.dot_general(a, b, (((1,), (1,)), ((), ())), preferred_element_type=F32)


def _dot_tn(a, b):
    return lax.dot_general(a, b, (((0,), (0,)), ((), ())), preferred_element_type=F32)


def _sigmoid(z):
    return 1.0 / (1.0 + jnp.exp(-z))


def _const_spec(shape):
    nd = len(shape)
    return pl.BlockSpec(shape, lambda i: (0,) * nd, pipeline_mode=pl.Buffered(1))


def _tables(seq):
    f32 = np.float32
    pos = np.arange(seq, dtype=f32)
    theta = (f32(1.0) / (f32(ROT_BASE) ** np.linspace(0.0, 1.0, RET_QK_DIM // 2, dtype=f32))).astype(f32)
    ang = (pos[:, None] * theta[None, :]).astype(f32)
    cos, sin = np.cos(ang), np.sin(ang)
    cos2 = np.repeat(cos, 2, axis=1)
    sin2 = np.stack([-sin, sin], axis=-1).reshape(seq, RET_QK_DIM)
    cos_t = np.tile(cos2, (1, 2))
    sin_t = np.tile(sin2, (1, 2))

    log_gamma = np.log(f32(1.0) - f32(2.0) ** (f32(-5.0) - np.arange(RET_HEADS, dtype=f32))).astype(f32)
    idx = np.arange(BLK, dtype=f32)
    rel = idx[:, None] - idx[None, :]
    decay_in = np.where(rel >= 0, np.exp(log_gamma[:, None, None] * np.maximum(rel, f32(0.0))), f32(0.0))
    k_dec = np.exp(log_gamma[:, None] * (BLK - 1 - idx)[None, :])
    q_dec = np.exp(log_gamma[:, None] * (idx + 1)[None, :])
    chunk_decay = np.exp(log_gamma * f32(BLK))
    kdec_t = np.repeat(k_dec.T, RET_QK_DIM, axis=1)
    qdec_t = np.repeat(q_dec.T, RET_QK_DIM, axis=1)
    cd_t = np.broadcast_to(chunk_decay[:, None, None], (RET_HEADS, RET_QK_DIM, RET_V_DIM))
    decay_in = decay_in.reshape(PAIRS, 2 * BLK, BLK)
    cd_t = cd_t.reshape(PAIRS, 2 * RET_QK_DIM, RET_V_DIM)

    key = np.arange(BLK)[:, None]
    query = np.arange(2 * BLK)[None, :] % BLK
    bias = np.stack([np.where(key > query, NEG_INF, 0.0), np.zeros((BLK, 2 * BLK))])
    return tuple(jnp.asarray(np.ascontiguousarray(a), F32) for a in (cos_t, sin_t, decay_in, qdec_t, kdec_t, cd_t, bias))


def _low_lanes(shape):
    lane = lax.broadcasted_iota(jnp.int32, shape, len(shape) - 1)
    return (lane & HALF_LANE) == 0


def _split_heads(t):
    low = _low_lanes(t.shape)
    zero = jnp.zeros_like(t)
    return jnp.where(low, t, zero), jnp.where(low, zero, t)


def _swap_pairs(t):
    lane = lax.broadcasted_iota(jnp.int32, t.shape, 1)
    nxt = pltpu.roll(t, t.shape[1] - 1, 1)
    prv = pltpu.roll(t, 1, 1)
    return jnp.where((lane & 1) == 0, nxt, prv)


def _per_tile(fn, t):
    return jnp.concatenate([fn(_tile(t, i)) for i in range(t.shape[1] // LANE)], axis=1)


def _rotate(t, cos_t, sin_t):
    return _per_tile(lambda a: a * cos_t + _swap_pairs(a) * sin_t, t)


def _rotate_transposed(d, cos_t, sin_t):
    return _per_tile(lambda a: a * cos_t + _swap_pairs(a * sin_t), d)


def _kv_operands(cat):
    low = _low_lanes(cat.shape)
    swapped = pltpu.roll(cat, HALF_LANE, 1)
    zero = jnp.zeros_like(cat)
    pick = lambda a, b: jnp.where(low, a, b).astype(BF16)
    return ((pick(cat, zero), pick(zero, swapped)), (pick(swapped, zero), pick(zero, cat)))


def _stack_tiles(t, first_tile):
    a = t[:, first_tile * LANE:(first_tile + 1) * LANE]
    b = t[:, (first_tile + 1) * LANE:(first_tile + 2) * LANE]
    return jnp.concatenate([a, b], axis=0)


def _sink_rows(sinks_ref, group):
    first = lax.broadcasted_iota(jnp.int32, (1, 2 * BLK), 1) < BLK

    def row(h0, h1):
        return jnp.where(first, sinks_ref[0, group * 4 + h0], sinks_ref[0, group * 4 + h1])
    return row(0, 2), row(1, 3)


ATT_PROBLEMS = tuple((g, hi) for g in range(ATT_KV_HEADS) for hi in range(2))


def _in_previous_block():
    key = lax.broadcasted_iota(jnp.int32, (BLK, 2 * BLK), 0)
    query = lax.broadcasted_iota(jnp.int32, (BLK, 2 * BLK), 1) & (BLK - 1)
    return key > query


def _fold(t, prev):
    return jnp.where(prev, t[0:BLK], t[BLK:])


def _unfold(t, prev):
    zero = jnp.zeros_like(t)
    return jnp.concatenate([jnp.where(prev, t, zero), jnp.where(prev, zero, t)], axis=0)


def _attn_probs(qs, k_ops, bias, prev, sinks_ref):
    sink = [_sink_rows(sinks_ref, g)[hi] for g, hi in ATT_PROBLEMS]
    s = [_fold(_dot_nt(k_ops[g][hi], qs[g]), prev) + bias for g, hi in ATT_PROBLEMS]
    m = [jnp.maximum(jnp.max(si, axis=0, keepdims=True), ki) for si, ki in zip(s, sink)]
    e = [jnp.exp(si - mi) for si, mi in zip(s, m)]
    es = [jnp.exp(ki - mi) for ki, mi in zip(sink, m)]
    inv = [1.0 / (jnp.sum(ei, axis=0, keepdims=True) + esi) for ei, esi in zip(e, es)]
    return [ei * ii for ei, ii in zip(e, inv)], [esi * ii for esi, ii in zip(es, inv)]


def _group_norm_all(outs):
    mu = [jnp.mean(o, axis=1, keepdims=True) for o in outs]
    xc = [o - m for o, m in zip(outs, mu)]
    var = [jnp.mean(c * c, axis=1, keepdims=True) for c in xc]
    rstd = [lax.rsqrt(v + GN_EPS) for v in var]
    return [c * r for c, r in zip(xc, rstd)], rstd


def _retention_operands(p_ref, rows, cos_b, sin_b, qdec, kdec):
    qr = _rotate(p_ref[rows, OFF_RQ:OFF_RQ + RET_QK_WIDTH].astype(F32), cos_b, sin_b)
    kr = _rotate(p_ref[rows, OFF_RK:OFF_RK + RET_QK_WIDTH].astype(F32), cos_b, sin_b) * RET_SCALE
    return qr, kr, qr * qdec, kr * kdec


def _tile(t, i):
    return t[:, i * LANE:(i + 1) * LANE]


def _weights_allgather(w_in_t_shard, w_out_shard):
    half_in = SHARD_IN // 2
    half_out = SHARD_OUT // 2
    assert half_in % BF16_ROWS == 0 and half_out % BF16_ROWS == 0

    def body(wi_ref, wo_ref, win_ref, wout_ref, blk_ref, blko_ref, send_sems, recv_sems):
        x, y, c = lax.axis_index("x"), lax.axis_index("y"), lax.axis_index("c")
        me, sibling = (x, y, c), (x, y, 1 - c)
        chips = [(1 - x, y), (x, 1 - y), (1 - x, 1 - y)]

        def rows_in(px, py, pc):
            return win_ref.at[pl.ds(pl.multiple_of((4 * px + 2 * py + pc) * half_in, BF16_ROWS), half_in), :]

        def rows_out(px, py, pc):
            return wout_ref.at[pl.ds(pl.multiple_of((4 * px + 2 * py + pc) * half_out, half_out), half_out), :]

        def copy(k, rows, block, to, src=None):
            return pltpu.make_async_remote_copy(
                src_ref=rows(*block) if src is None else src, dst_ref=rows(*block),
                send_sem=send_sems.at[k], recv_sem=recv_sems.at[k], device_id=to, device_id_type=MESH_ID)

        blk_ref[...] = wi_ref[pl.ds(pl.multiple_of(c * half_in, BF16_ROWS), half_in), :].astype(BF16)
        blko_ref[...] = wo_ref[pl.ds(pl.multiple_of(c * half_out, half_out), half_out), :].astype(BF16)
        rows_in(*me)[...] = blk_ref[...]
        rows_out(*me)[...] = blko_ref[...]

        sets = ((0, rows_in, blk_ref), (7, rows_out, blko_ref))
        first, passed = [], []
        for base, rows, src in sets:
            first.append(copy(base, rows, me, sibling, src=src))
            first += [copy(base + 1 + j, rows, me, (*chip, c), src=src) for j, chip in enumerate(chips)]
        for cp in first:
            cp.start()
        for base, rows, src in sets:
            for j, chip in enumerate(chips):
                copy(base + 1 + j, rows, (*chip, c), me).wait_recv()
                fwd = copy(base + 4 + j, rows, (*chip, c), sibling)
                fwd.start()
                passed.append(fwd)
        for base, rows, src in sets:
            copy(base, rows, sibling, me).wait_recv()
            for j, chip in enumerate(chips):
                copy(base + 4 + j, rows, (*chip, 1 - c), me).wait_recv()
        for cp in first + passed:
            cp.wait_send()

    vmem = pl.BlockSpec(memory_space=pltpu.VMEM)
    return _call(
        body, name="weights_allgather",
        out_shape=(jax.ShapeDtypeStruct((IN_WIDTH, D_MODEL), BF16), jax.ShapeDtypeStruct((MIX_WIDTH, D_MODEL), BF16)),
        in_specs=[vmem, vmem], out_specs=(vmem, vmem),
        scratch_shapes=[
            pltpu.VMEM((half_in, D_MODEL), BF16),
            pltpu.VMEM((half_out, D_MODEL), BF16),
            pltpu.SemaphoreType.DMA((14,)),
            pltpu.SemaphoreType.DMA((14,)),
        ],
        compiler_params=_params(32),
    )(w_in_t_shard, w_out_shard)


def _in_proj(x, norm_g, w_in, tb):
    seq = x.shape[0]

    def body(x_ref, g_ref, w_ref, p_ref, ht_ref):
        xv = x_ref[...]
        r = lax.rsqrt(jnp.mean(xv * xv, axis=1, keepdims=True) + RMS_EPS)
        h = (xv * r) * g_ref[...]
        p_ref[...] = _dot_nt(h.astype(BF16), w_ref[...]).astype(BF16)
        ht_ref[...] = h.T.astype(BF16)

    return _call(
        body, name="in_proj", grid=(seq // tb,),
        out_shape=(jax.ShapeDtypeStruct((seq, IN_WIDTH), BF16), jax.ShapeDtypeStruct((D_MODEL, seq), BF16)),
        in_specs=[pl.BlockSpec((tb, D_MODEL), lambda i: (i, 0)), _const_spec((1, D_MODEL)),
                  _const_spec((IN_WIDTH, D_MODEL))],
        out_specs=(pl.BlockSpec((tb, IN_WIDTH), lambda i: (i, 0)), pl.BlockSpec((D_MODEL, tb), lambda i: (0, i))),
        compiler_params=_params(56, ("arbitrary",)),
    )(x, norm_g, w_in)


def _mix_fwd(proj, x, target, w_out, final_g, gn_gain, sinks, tables, tb):
    seq = x.shape[0]
    nsub = tb // BLK
    cos_t, sin_t, decay_in, qdec_t, kdec_t, cd_t, bias_t = tables

    def body(p_ref, x_ref, t_ref, cos_ref, sin_ref, wout_ref, fg_ref, gain_ref, sinks_ref, din_ref, qdec_ref,
             kdec_ref, cd_ref, bias_ref, mix_ref, dxo_ref, st_ref, loss_ref, gfin_ref, oatt_ref,
             kprev_ref, vprev_ref, state_ref):
        i = pl.program_id(0)

        @pl.when(i == 0)
        def _():
            kprev_ref[...] = jnp.zeros_like(kprev_ref)
            vprev_ref[...] = jnp.zeros_like(vprev_ref)
            state_ref[...] = jnp.zeros_like(state_ref)
            loss_ref[...] = jnp.zeros_like(loss_ref)
            gfin_ref[...] = jnp.zeros_like(gfin_ref)

        prev = _in_previous_block()

        def sub(j, carry):
            kp, vp, states = carry
            rows = pl.ds(pl.multiple_of(j * BLK, BLK), BLK)
            bias = bias_ref[jnp.where(jnp.logical_or(i > 0, j > 0), 1, 0)]

            aq = p_ref[rows, OFF_AQ:OFF_AQ + ATT_WIDTH]
            ak = p_ref[rows, OFF_AK:OFF_AK + ATT_KV_WIDTH].astype(F32)
            av = p_ref[rows, OFF_AV:OFF_AV + ATT_KV_WIDTH].astype(F32)
            az = p_ref[rows, OFF_AZ:OFF_AZ + ATT_WIDTH].astype(F32)
            k_ops = _kv_operands(jnp.concatenate([kp, ak], axis=0))
            v_ops = _kv_operands(jnp.concatenate([vp, av], axis=0))
            qs = [(_stack_tiles(aq, 2 * g) * ATT_SCALE).astype(BF16) for g in range(ATT_KV_HEADS)]
            p, _ = _attn_probs(qs, k_ops, bias, prev, sinks_ref)
            o_tiles = []
            for g in range(ATT_KV_HEADS):
                p_cat = jnp.concatenate([_unfold(p[2 * g].astype(BF16), prev), _unfold(p[2 * g + 1].astype(BF16), prev)],
                                        axis=0)
                o = _dot_tn(p_cat, jnp.concatenate(v_ops[g], axis=0))
                o_tiles += [o[0:BLK], o[BLK:]]
            o_att = jnp.concatenate(o_tiles, axis=1)
            oatt_ref[rows, :] = o_att
            out = [o_att * (az * _sigmoid(az))]

            qr, kr, qd, kd = _retention_operands(p_ref, rows, cos_ref[rows, :], sin_ref[rows, :],
                                                 qdec_ref[...], kdec_ref[...])
            heads = [(t, hh) for t in range(PAIRS) for hh in range(2)]
            sc = [_dot_nt(jnp.concatenate(_split_heads(_tile(qr, t)), axis=0).astype(BF16), _tile(kr, t).astype(BF16))
                  * din_ref[t] for t in range(PAIRS)]
            qd_heads = [_split_heads(_tile(qd, t)) for t in range(PAIRS)]
            state_b = [states[t].astype(BF16) for t in range(PAIRS)]
            vs = [p_ref[rows, OFF_RV + h * RET_V_DIM:OFF_RV + (h + 1) * RET_V_DIM].astype(BF16) for h in range(RET_HEADS)]
            rzs = [p_ref[rows, OFF_RZ + h * RET_V_DIM:OFF_RZ + (h + 1) * RET_V_DIM].astype(F32) for h in range(RET_HEADS)]
            lhs = [jnp.concatenate([sc[t][hh * BLK:(hh + 1) * BLK].astype(BF16), qd_heads[t][hh].astype(BF16)], axis=1)
                   for t, hh in heads]
            ons, _ = _group_norm_all([_dot(lhs[2 * t + hh], jnp.concatenate([vs[2 * t + hh], state_b[t]], axis=0))
                                      for t, hh in heads])
            out += [(ons[h] * gain_ref[:, h * RET_V_DIM:(h + 1) * RET_V_DIM]) * (rzs[h] * _sigmoid(rzs[h]))
                    for h in range(RET_HEADS)]
            new_states = [states[t] * cd_ref[t]
                          + _dot_tn(jnp.concatenate(_split_heads(_tile(kd, t)), axis=0).astype(BF16),
                                    jnp.concatenate([vs[2 * t], vs[2 * t + 1]], axis=0)) for t in range(PAIRS)]
            mix_ref[rows, :] = jnp.concatenate(out, axis=1).astype(BF16)
            st_ref[j] = jnp.stack(states)
            return ak, av, tuple(new_states)

        carry = (kprev_ref[...], vprev_ref[...], tuple(state_ref[t] for t in range(PAIRS)))
        kp, vp, states = lax.fori_loop(0, nsub, sub, carry)
        kprev_ref[...] = kp
        vprev_ref[...] = vp
        state_ref[...] = jnp.stack(states)

        xo = x_ref[...] + _dot(mix_ref[...], wout_ref[...])
        r2 = lax.rsqrt(jnp.mean(xo * xo, axis=1, keepdims=True) + RMS_EPS)
        xn = xo * r2
        err = xn * fg_ref[...] - t_ref[...]
        loss_ref[...] += jnp.sum(err * err) * (0.5 / D_MODEL)
        dy = err * (1.0 / D_MODEL)
        gfin_ref[...] += jnp.sum(dy * xn, axis=0, keepdims=True)
        u = dy * fg_ref[...]
        dxo_ref[...] = r2 * u - xn * (r2 * jnp.mean(u * xn, axis=1, keepdims=True))

    blk_rows = lambda w: pl.BlockSpec((tb, w), lambda i: (i, 0))
    state_shape = (PAIRS, 2 * RET_QK_DIM, RET_V_DIM)
    return _call(
        body, name="mix_fwd", grid=(seq // tb,),
        out_shape=(
            jax.ShapeDtypeStruct((seq, MIX_WIDTH), BF16),
            jax.ShapeDtypeStruct((seq, D_MODEL), F32),
            jax.ShapeDtypeStruct((seq // BLK,) + state_shape, F32),
            jax.ShapeDtypeStruct((8, LANE), F32),
            jax.ShapeDtypeStruct((1, D_MODEL), F32),
            jax.ShapeDtypeStruct((seq, ATT_WIDTH), F32),
        ),
        in_specs=[
            blk_rows(IN_WIDTH), blk_rows(D_MODEL), blk_rows(D_MODEL), blk_rows(LANE), blk_rows(LANE),
            _const_spec((MIX_WIDTH, D_MODEL)), _const_spec((1, D_MODEL)), _const_spec((1, RET_WIDTH)),
            pl.BlockSpec(memory_space=pltpu.SMEM),
            _const_spec((PAIRS, 2 * BLK, BLK)), _const_spec((BLK, RET_QK_WIDTH)), _const_spec((BLK, RET_QK_WIDTH)),
            _const_spec(state_shape), _const_spec((2, BLK, 2 * BLK)),
        ],
        out_specs=(
            blk_rows(MIX_WIDTH), blk_rows(D_MODEL),
            pl.BlockSpec((nsub,) + state_shape, lambda i: (i, 0, 0, 0)),
            _const_spec((8, LANE)), _const_spec((1, D_MODEL)), blk_rows(ATT_WIDTH),
        ),
        scratch_shapes=[
            pltpu.VMEM((BLK, ATT_KV_WIDTH), F32), pltpu.VMEM((BLK, ATT_KV_WIDTH), F32),
            pltpu.VMEM(state_shape, F32),
        ],
        compiler_params=_params(48, ("arbitrary",)),
    )(proj, x, target, cos_t, sin_t, w_out, final_g, gn_gain, sinks, decay_in, qdec_t, kdec_t, cd_t, bias_t)


def _mix_bwd(proj, dxo, mix, o_att, states, x, w_out, w_in_t, norm_g, gn_gain, sinks, tables, tb):
    seq = dxo.shape[0]
    nsub = tb // BLK
    nblk = seq // tb
    cos_t, sin_t, decay_in, qdec_t, kdec_t, cd_t, bias_t = tables
    kv_cols = OFF_AK // (2 * ATT_KV_WIDTH)
    state_shape = (PAIRS, 2 * RET_QK_DIM, RET_V_DIM)

    def body(p_ref, pkv_ref, dxo_ref, mix_ref, oatt_ref, st_ref, cos_ref, sin_ref, x_ref, wout_ref, win_ref, g_ref,
             gain_ref, sinks_ref, din_ref, qdec_ref, kdec_ref, cd_ref, bias_ref,
             dp_ref, gx_ref, gwout_ref, gnorm_ref, dgain_ref, dsink_ref,
             dmix_ref, kv_ref, dkc_ref, dvc_ref, gst_ref):
        i = pl.program_id(0)
        blk = nblk - 1 - i

        @pl.when(i == 0)
        def _():
            gwout_ref[...] = jnp.zeros_like(gwout_ref)
            gnorm_ref[...] = jnp.zeros_like(gnorm_ref)
            dgain_ref[...] = jnp.zeros_like(dgain_ref)
            dsink_ref[...] = jnp.zeros_like(dsink_ref)
            dkc_ref[...] = jnp.zeros_like(dkc_ref)
            dvc_ref[...] = jnp.zeros_like(dvc_ref)
            gst_ref[...] = jnp.zeros_like(gst_ref)

        dxo_b = dxo_ref[...].astype(BF16)
        dmix_ref[...] = _dot_nt(dxo_b, wout_ref[...])
        gwout_ref[...] += _dot_tn(mix_ref[...], dxo_b)
        kv_ref[0:BLK, :] = pkv_ref[...].astype(F32)
        kv_ref[BLK:, :] = p_ref[:, OFF_AK:OFF_AK + 2 * ATT_KV_WIDTH].astype(F32)
        low = _low_lanes((BLK, LANE))
        low2 = _low_lanes((2 * BLK, LANE))
        lane = lax.broadcasted_iota(jnp.int32, (1, LANE), 1)
        prev = _in_previous_block()

        def sub(jj, carry):
            dkc, dvc, gsts, dgain, dsink = carry
            j = nsub - 1 - jj
            rows = pl.ds(pl.multiple_of(j * BLK, BLK), BLK)
            both = pl.ds(pl.multiple_of(j * BLK, BLK), 2 * BLK)
            bias = bias_ref[jnp.where(jnp.logical_or(blk > 0, j > 0), 1, 0)]

            aq = p_ref[rows, OFF_AQ:OFF_AQ + ATT_WIDTH]
            az = p_ref[rows, OFF_AZ:OFF_AZ + ATT_WIDTH].astype(F32)
            k_ops = _kv_operands(kv_ref[both, 0:ATT_KV_WIDTH])
            v_ops = _kv_operands(kv_ref[both, ATT_KV_WIDTH:2 * ATT_KV_WIDTH])
            da = dmix_ref[rows, 0:ATT_WIDTH]
            sig = _sigmoid(az)
            d_o = da * (az * sig)
            qs = [(_stack_tiles(aq, 2 * g) * ATT_SCALE).astype(BF16) for g in range(ATT_KV_HEADS)]
            dos = [_stack_tiles(d_o, 2 * g).astype(BF16) for g in range(ATT_KV_HEADS)]
            p, p_sink = _attn_probs(qs, k_ops, bias, prev, sinks_ref)
            dpr = [_fold(_dot_nt(v_ops[g][hi], dos[g]), prev) for g, hi in ATT_PROBLEMS]
            delta = [jnp.sum(pi * di, axis=0, keepdims=True) for pi, di in zip(p, dpr)]
            ds = [_unfold((pi * (di - ti)).astype(BF16), prev) for pi, di, ti in zip(p, dpr, delta)]
            for (g, hi), ki, ti in zip(ATT_PROBLEMS, p_sink, delta):
                sink_part = ki * ti
                for half in range(2):
                    tot = jnp.sum(sink_part[:, half * BLK:(half + 1) * BLK], axis=1, keepdims=True)
                    dsink = dsink - jnp.where(lane == 4 * g + 2 * half + hi, tot, 0.0)
            dq_tiles, dk_sums, dv_sums = [], [], []
            for g in range(ATT_KV_HEADS):
                ds_cat = jnp.concatenate([ds[2 * g], ds[2 * g + 1]], axis=0)
                p_cat = jnp.concatenate([_unfold(p[2 * g].astype(BF16), prev), _unfold(p[2 * g + 1].astype(BF16), prev)],
                                        axis=0)
                dqs = _dot_tn(ds_cat, jnp.concatenate(k_ops[g], axis=0)) * ATT_SCALE
                dq_tiles += [dqs[0:BLK], dqs[BLK:]]
                dk_sums.append(_dot(ds_cat, qs[g]))
                dv_sums.append(_dot(p_cat, dos[g]))
            daz = da * oatt_ref[rows, :] * (sig * (1.0 + az * (1.0 - sig)))

            def kv_grad(sums):
                (a0, b0), (a1, b1) = [(s[0:2 * BLK], s[2 * BLK:]) for s in sums]
                return jnp.where(low2, a0, b1) + pltpu.roll(jnp.where(low2, a1, b0), HALF_LANE, 1)

            dk_both, dv_both = kv_grad(dk_sums), kv_grad(dv_sums)
            dak = dk_both[BLK:] + dkc
            dav = dv_both[BLK:] + dvc

            cos_b, sin_b = cos_ref[rows, :], sin_ref[rows, :]
            qdec, kdec = qdec_ref[...], kdec_ref[...]
            qr, kr, qd, kd = _retention_operands(p_ref, rows, cos_b, sin_b, qdec, kdec)
            heads = [(t, hh) for t in range(PAIRS) for hh in range(2)]
            head_cols = [slice(h * RET_V_DIM, (h + 1) * RET_V_DIM) for h in range(RET_HEADS)]
            q_rows = [jnp.concatenate(_split_heads(_tile(qr, t)), axis=0).astype(BF16) for t in range(PAIRS)]
            k_rows = [jnp.concatenate(_split_heads(_tile(kr, t)), axis=0).astype(BF16) for t in range(PAIRS)]
            din = [din_ref[t] for t in range(PAIRS)]
            sc = [(_dot_nt(q_rows[t], _tile(kr, t).astype(BF16)) * din[t]).astype(BF16) for t in range(PAIRS)]
            qd_heads = [_split_heads(_tile(qd, t)) for t in range(PAIRS)]
            kd_heads = [_split_heads(_tile(kd, t)) for t in range(PAIRS)]
            state_b = [st_ref[j, t].astype(BF16) for t in range(PAIRS)]
            gst_b = [gsts[t].astype(BF16) for t in range(PAIRS)]
            vs = [p_ref[rows, OFF_RV + h * RET_V_DIM:OFF_RV + (h + 1) * RET_V_DIM].astype(BF16) for h in range(RET_HEADS)]
            rzs = [p_ref[rows, OFF_RZ + h * RET_V_DIM:OFF_RZ + (h + 1) * RET_V_DIM].astype(F32) for h in range(RET_HEADS)]
            drs = [dmix_ref[rows, ATT_WIDTH + h * RET_V_DIM:ATT_WIDTH + (h + 1) * RET_V_DIM] for h in range(RET_HEADS)]
            gains = [gain_ref[:, c] for c in head_cols]
            lhs = [jnp.concatenate([sc[t][hh * BLK:(hh + 1) * BLK], qd_heads[t][hh].astype(BF16)], axis=1) for t, hh in heads]
            rhs = [jnp.concatenate([vs[2 * t + hh], state_b[t]], axis=0) for t, hh in heads]
            ons, rstds = _group_norm_all([_dot(l, r) for l, r in zip(lhs, rhs)])
            sig_r = [_sigmoid(z) for z in rzs]
            dgn = [d * (z * g) for d, z, g in zip(drs, rzs, sig_r)]
            dz_parts = [d * (o * gn) * (g * (1.0 + z * (1.0 - g))) for d, o, gn, g, z in zip(drs, ons, gains, sig_r, rzs)]
            dgain_parts = [jnp.sum(d * o, axis=0, keepdims=True) for d, o in zip(dgn, ons)]
            don = [d * gn for d, gn in zip(dgn, gains)]
            mean_don = [jnp.mean(d, axis=1, keepdims=True) for d in don]
            mean_don_on = [jnp.mean(d * o, axis=1, keepdims=True) for d, o in zip(don, ons)]
            dob = [(r * (d - a - o * b)).astype(BF16) for r, d, a, o, b in zip(rstds, don, mean_don, ons, mean_don_on)]
            dlhs = [_dot_nt(d, r) for d, r in zip(dob, rhs)]
            drhs = [_dot_tn(l, d) for l, d in zip(lhs, dob)]
            dkds = [_dot_nt(vs[2 * t + hh], gst_b[t]) for t, hh in heads]
            dv_parts = [drhs[2 * t + hh][0:BLK] + _dot(kd_heads[t][hh].astype(BF16), gst_b[t]) for t, hh in heads]
            das = [(dlhs[2 * t + hh][:, 0:BLK] * din[t][hh * BLK:(hh + 1) * BLK]).astype(BF16) for t, hh in heads]
            new_gsts = [gsts[t] * cd_ref[t] + drhs[2 * t][BLK:] + drhs[2 * t + 1][BLK:] for t in range(PAIRS)]
            dq_parts = [_dot(jnp.concatenate([das[2 * t], das[2 * t + 1]], axis=1), k_rows[t])
                        + jnp.where(low, dlhs[2 * t][:, BLK:], dlhs[2 * t + 1][:, BLK:]) * _tile(qdec, t)
                        for t in range(PAIRS)]
            dk_parts = [_dot_tn(jnp.concatenate([das[2 * t], das[2 * t + 1]], axis=0), q_rows[t])
                        + jnp.where(low, dkds[2 * t], dkds[2 * t + 1]) * _tile(kdec, t) for t in range(PAIRS)]
            drq = _rotate_transposed(jnp.concatenate(dq_parts, axis=1), cos_b, sin_b)
            drk = _rotate_transposed(jnp.concatenate(dk_parts, axis=1) * RET_SCALE, cos_b, sin_b)

            dp_ref[rows, :] = jnp.concatenate(
                [jnp.concatenate(dq_tiles, axis=1), dak, dav, daz, drq, drk] + dv_parts + dz_parts, axis=1).astype(BF16)
            dgain = dgain + jnp.concatenate(dgain_parts, axis=1)
            return dk_both[0:BLK], dv_both[0:BLK], tuple(new_gsts), dgain, dsink

        carry = (dkc_ref[...], dvc_ref[...], tuple(gst_ref[t] for t in range(PAIRS)), dgain_ref[...], dsink_ref[...])
        dkc, dvc, gsts, dgain, dsink = lax.fori_loop(0, nsub, sub, carry)
        dkc_ref[...] = dkc
        dvc_ref[...] = dvc
        gst_ref[...] = jnp.stack(gsts)
        dgain_ref[...] = dgain
        dsink_ref[...] = dsink

        dh = _dot(dp_ref[...], win_ref[...])
        xv = x_ref[...]
        r = lax.rsqrt(jnp.mean(xv * xv, axis=1, keepdims=True) + RMS_EPS)
        xn = xv * r
        gnorm_ref[...] += jnp.sum(dh * xn, axis=0, keepdims=True)
        u = dh * g_ref[...]
        gx_ref[...] = dxo_ref[...] + r * u - xn * (r * jnp.mean(u * xn, axis=1, keepdims=True))

    rev_rows = lambda w: pl.BlockSpec((tb, w), lambda i: (nblk - 1 - i, 0))
    prev_kv = pl.BlockSpec((BLK, 2 * ATT_KV_WIDTH), lambda i: (jnp.maximum((nblk - 1 - i) * nsub - 1, 0), kv_cols))
    return _call(
        body, name="mix_bwd", grid=(nblk,),
        out_shape=(
            jax.ShapeDtypeStruct((seq, IN_WIDTH), BF16),
            jax.ShapeDtypeStruct((seq, D_MODEL), F32),
            jax.ShapeDtypeStruct((MIX_WIDTH, D_MODEL), F32),
            jax.ShapeDtypeStruct((1, D_MODEL), F32),
            jax.ShapeDtypeStruct((1, RET_WIDTH), F32),
            jax.ShapeDtypeStruct((1, LANE), F32),
        ),
        in_specs=[
            rev_rows(IN_WIDTH), prev_kv, rev_rows(D_MODEL), rev_rows(MIX_WIDTH), rev_rows(ATT_WIDTH),
            pl.BlockSpec((nsub,) + state_shape, lambda i: (nblk - 1 - i, 0, 0, 0)),
            rev_rows(LANE), rev_rows(LANE), rev_rows(D_MODEL),
            _const_spec((MIX_WIDTH, D_MODEL)), _const_spec((IN_WIDTH, D_MODEL)), _const_spec((1, D_MODEL)),
            _const_spec((1, RET_WIDTH)),
            pl.BlockSpec(memory_space=pltpu.SMEM),
            _const_spec((PAIRS, 2 * BLK, BLK)), _const_spec((BLK, RET_QK_WIDTH)), _const_spec((BLK, RET_QK_WIDTH)),
            _const_spec(state_shape), _const_spec((2, BLK, 2 * BLK)),
        ],
        out_specs=(
            rev_rows(IN_WIDTH), rev_rows(D_MODEL), _const_spec((MIX_WIDTH, D_MODEL)), _const_spec((1, D_MODEL)),
            _const_spec((1, RET_WIDTH)), _const_spec((1, LANE)),
        ),
        scratch_shapes=[
            pltpu.VMEM((tb, MIX_WIDTH), F32),
            pltpu.VMEM((tb + BLK, 2 * ATT_KV_WIDTH), F32),
            pltpu.VMEM((BLK, ATT_KV_WIDTH), F32), pltpu.VMEM((BLK, ATT_KV_WIDTH), F32),
            pltpu.VMEM(state_shape, F32),
        ],
        compiler_params=_params(60, ("arbitrary",)),
    )(proj, proj, dxo, mix, o_att, states, cos_t, sin_t, x, w_out, w_in_t, norm_g, gn_gain, sinks, decay_in, qdec_t,
      kdec_t, cd_t, bias_t)


def _gw_in_reduce(h_t, dproj, gw_out, small, tb):
    seq = dproj.shape[0]
    nblk = seq // tb
    last = nblk - 1
    hand_on = min(1, last)
    half = D_MODEL // 2
    A, B, C, N_SEMS = 0, N_CHIPS, 2 * N_CHIPS, 2 * N_CHIPS + 1

    def body(win_ref, ht_ref, dp_ref, gwo_hbm, s0_ref, s1_ref, s2_ref, s3_ref, s4_ref, out_ref, fout_ref, packsum_ref,
             acc, sib, send_buf, b_in, fin, fout, mine_out, sib_out, send_out, b_out, pack_ref, packs,
             send_sems, recv_sems, local_sems):
        p, i = pl.program_id(0), pl.program_id(1)
        small_start, small_hand_on, small_finish = _small_exchange(
            gwo_hbm, (s0_ref, s1_ref, s2_ref, s3_ref, s4_ref), fout, mine_out, sib_out, send_out, b_out, pack_ref,
            packs, send_sems, recv_sems, local_sems, N_SEMS)

        @pl.when(jnp.logical_and(p == 0, i == 0))
        def _():
            small_start()

        @pl.when(jnp.logical_and(p == 0, i == hand_on))
        def _():
            small_hand_on()

        x, y, c = lax.axis_index("x"), lax.axis_index("y"), lax.axis_index("c")
        chip = 2 * x + y
        sibling = (x, y, 1 - c)
        mine = pl.ds(pl.multiple_of(c * half, half), half)
        other = pl.ds(pl.multiple_of((1 - c) * half, half), half)

        def remote(src, dst, send_k, recv_k, to):
            return pltpu.make_async_remote_copy(src_ref=src, dst_ref=dst, send_sem=send_sems.at[send_k],
                                                recv_sem=recv_sems.at[recv_k], device_id=to, device_id_type=MESH_ID)

        part = _dot(ht_ref[...], dp_ref[...])
        slot = p % 2

        @pl.when(i == 0)
        def _():
            acc[slot] = part

        @pl.when(i > 0)
        def _():
            acc[slot] += part

        for q in range(N_CHIPS):
            s = q % 2
            to_sibling = remote(acc.at[s, other, :], sib.at[s], A + q, A + q, sibling)

            @pl.when(jnp.logical_and(p == q, i == last))
            def _():
                to_sibling.start()

            if q < N_CHIPS - 1:
                dest = (chip + 1 + q) % N_CHIPS

                @pl.when(jnp.logical_and(p == q + 1, i == hand_on))
                def _():
                    to_sibling.wait_recv()
                    send_buf[q] = (acc[s, mine, :] + sib[s]).astype(BF16)
                    remote(send_buf.at[q], b_in.at[chip], B + q, B + chip, (dest // 2, dest % 2, c)).start()
                    to_sibling.wait_send()
            else:
                @pl.when(jnp.logical_and(p == q, i == last))
                def _():
                    to_sibling.wait_recv()
                    fin[mine, :] = acc[s, mine, :] + sib[s]
                    for j in range(N_CHIPS):
                        @pl.when(j != chip)
                        def _():
                            remote(b_in.at[j], b_in.at[j], B + j, B + j, sibling).wait_recv()
                            fin[mine, :] += b_in[j].astype(F32)
                    to_core = remote(fin.at[mine, :], fin.at[mine, :], C, C, sibling)
                    to_core.start()
                    remote(fin.at[other, :], fin.at[other, :], C, C, sibling).wait_recv()
                    out_ref[...] = fin[...]
                    to_core.wait_send()
                    to_sibling.wait_send()
                    for k in range(N_CHIPS - 1):
                        remote(send_buf.at[k], b_in.at[chip], B + k, B + k, sibling).wait_send()
                    packsum_ref[...] = small_finish()
                    fout_ref[...] = fout[...]

    whole = lambda shape: pl.BlockSpec(shape, lambda p, i, win: (0,) * len(shape), pipeline_mode=pl.Buffered(1))
    grid_spec = pltpu.PrefetchScalarGridSpec(
        num_scalar_prefetch=1, grid=(N_CHIPS, nblk),
        in_specs=[pl.BlockSpec((D_MODEL, tb), lambda p, i, win: (0, i)),
                  pl.BlockSpec((pl.Element(tb), pl.Element(SHARD_PAD)),
                               lambda p, i, win: (i * tb, pl.multiple_of(win[p] * LANE, LANE))),
                  pl.BlockSpec(memory_space=pl.ANY)] + [whole(s.shape) for s in small],
        out_specs=(whole((D_MODEL, SHARD_PAD)), whole((SHARD_OUT, D_MODEL)), whole((PACK_ROWS, D_MODEL))),
        scratch_shapes=[
            pltpu.VMEM((2, D_MODEL, SHARD_PAD), F32), pltpu.VMEM((2, half, SHARD_PAD), F32),
            pltpu.VMEM((N_CHIPS - 1, half, SHARD_PAD), BF16), pltpu.VMEM((N_CHIPS, half, SHARD_PAD), BF16),
            pltpu.VMEM((D_MODEL, SHARD_PAD), F32),
        ] + _small_exchange_scratch() + [
            pltpu.SemaphoreType.DMA((N_SEMS + SMALL_SEMS,)), pltpu.SemaphoreType.DMA((N_SEMS + SMALL_SEMS,)),
            pltpu.SemaphoreType.DMA((N_CHIPS,)),
        ])
    chip = 2 * lax.axis_index("x") + lax.axis_index("y")
    owner = (chip + 1 + jnp.arange(N_CHIPS, dtype=jnp.int32)) % N_CHIPS
    win_start = (owner * SHARD_IN) // LANE
    return _call(
        body, name="gw_in_reduce", grid_spec=grid_spec,
        out_shape=(jax.ShapeDtypeStruct((D_MODEL, SHARD_PAD), F32), jax.ShapeDtypeStruct((SHARD_OUT, D_MODEL), F32),
                   jax.ShapeDtypeStruct((PACK_ROWS, D_MODEL), F32)),
        compiler_params=_params(52, ("arbitrary", "arbitrary")),
    )(win_start.astype(jnp.int32), h_t, dproj, gw_out, *small)


SMALL_SEMS = 17


def _small_exchange(gwo_hbm, small_refs, fout_ref, mine_out, sib_out, send_out, b_out, pack_ref, packs,
                    send_sems, recv_sems, local_sems, base):
    half_out = SHARD_OUT // 2
    A_OUT, B_OUT, C_OUT, PACK = base, base + 4, base + 8, base + 9
    assert len(small_refs) == PACK_PARTS
    x, y, c = lax.axis_index("x"), lax.axis_index("y"), lax.axis_index("c")
    chip = 2 * x + y
    dev = 2 * chip + c
    sibling = (x, y, 1 - c)

    def remote(src, dst, send_k, recv_k, to):
        return pltpu.make_async_remote_copy(src_ref=src, dst_ref=dst, send_sem=send_sems.at[send_k],
                                            recv_sem=recv_sems.at[recv_k], device_id=to, device_id_type=MESH_ID)

    def out_rows(j, core):
        return pl.ds(pl.multiple_of(j * SHARD_OUT + core * half_out, half_out), half_out)

    my_out_rows = pl.ds(pl.multiple_of(c * half_out, half_out), half_out)
    local = [pltpu.make_async_copy(gwo_hbm.at[out_rows(j, c), :], mine_out.at[j], local_sems.at[j])
             for j in range(N_CHIPS)]
    stage_a = [remote(gwo_hbm.at[out_rows(j, 1 - c), :], sib_out.at[j], A_OUT + j, A_OUT + j, sibling)
               for j in range(N_CHIPS)]
    mine_half_out = fout_ref.at[my_out_rows, :]
    stage_c = [remote(mine_half_out, mine_half_out, C_OUT, C_OUT, sibling)]

    def start():
        pack_ref[...] = jnp.zeros_like(pack_ref)
        for k, s_ref in enumerate(small_refs):
            pack_ref[k:k + 1, 0:s_ref.shape[1]] = s_ref[0:1, :]
        packs[dev] = pack_ref[...]
        for d in range(N_DEV):
            to = (d // 4, (d // 2) % 2, d % 2)

            @pl.when(d != dev)
            def _():
                remote(pack_ref, packs.at[dev], PACK + d, PACK + dev, to).start()

        for cp in local + stage_a:
            cp.start()

    def hand_on():
        for cp in local:
            cp.wait()
        for cp in stage_a:
            cp.wait_recv()
        for j in range(N_CHIPS):
            mine_out[j] = mine_out[j] + sib_out[j]

        for j in range(N_CHIPS):
            to = (j // 2, j % 2, c)

            @pl.when(j != chip)
            def _():
                send_out[j] = mine_out[j].astype(BF16)
                remote(send_out.at[j], b_out.at[chip], B_OUT + j, B_OUT + chip, to).start()

            @pl.when(j == chip)
            def _():
                fout_ref[my_out_rows, :] = mine_out[j]

    def finish():
        for j in range(N_CHIPS):
            @pl.when(j != chip)
            def _():
                remote(b_out.at[j], b_out.at[j], B_OUT + j, B_OUT + j, sibling).wait_recv()
                fout_ref[my_out_rows, :] += b_out[j].astype(F32)

        for cp in stage_c:
            cp.start()
        other_half_out = fout_ref.at[pl.ds(pl.multiple_of((1 - c) * half_out, half_out), half_out), :]
        remote(other_half_out, other_half_out, C_OUT, C_OUT, sibling).wait_recv()

        for d in range(N_DEV):
            @pl.when(d != dev)
            def _():
                remote(pack_ref, packs.at[d], PACK + d, PACK + d, sibling).wait_recv()
        total = packs[0]
        for d in range(1, N_DEV):
            total = total + packs[d]

        for cp in stage_a + stage_c:
            cp.wait_send()
        for j in range(N_CHIPS):
            @pl.when(j != chip)
            def _():
                remote(b_out.at[j], b_out.at[j], B_OUT + j, B_OUT + j, sibling).wait_send()
        for d in range(N_DEV):
            @pl.when(d != dev)
            def _():
                remote(pack_ref, packs.at[d], PACK + d, PACK + d, sibling).wait_send()
        return total

    return start, hand_on, finish


def _small_exchange_scratch():
    half_out = SHARD_OUT // 2
    return [
        pltpu.VMEM((SHARD_OUT, D_MODEL), F32),
        pltpu.VMEM((N_CHIPS, half_out, D_MODEL), F32), pltpu.VMEM((N_CHIPS, half_out, D_MODEL), F32),
        pltpu.VMEM((N_CHIPS, half_out, D_MODEL), BF16), pltpu.VMEM((N_CHIPS, half_out, D_MODEL), BF16),
        pltpu.VMEM((PACK_ROWS, D_MODEL), F32), pltpu.VMEM((N_DEV, PACK_ROWS, D_MODEL), F32),
    ]


def _adam_math(w, g, m, v):
    mn = ADAM_B1 * m + (1.0 - ADAM_B1) * g
    vn = ADAM_B2 * v + (1.0 - ADAM_B2) * (g * g)
    m_hat = mn / (1.0 - ADAM_B1 ** ADAM_STEP)
    v_hat = vn / (1.0 - ADAM_B2 ** ADAM_STEP)
    return -ADAM_LR * (m_hat / (jnp.sqrt(v_hat) + ADAM_EPS) + ADAM_WD * w), mn, vn


def _adamw(name, w, g, m, v, tb):
    rows, cols = w.shape

    def body(w_ref, g_ref, m_ref, v_ref, go_ref, d_ref, mo_ref, vo_ref):
        gv = g_ref[...]
        go_ref[...] = gv
        d_ref[...], mo_ref[...], vo_ref[...] = _adam_math(w_ref[...], gv, m_ref[...], v_ref[...])

    spec = pl.BlockSpec((tb, cols), lambda i: (i, 0))
    shape = jax.ShapeDtypeStruct((rows, cols), F32)
    return _call(
        body, name=name, grid=(rows // tb,), out_shape=(shape,) * 4,
        in_specs=[spec] * 4, out_specs=(spec,) * 4,
        compiler_params=_params(32, ("arbitrary",)),
    )(w, g, m, v)


def _adamw_w_in(w_t, g_window, m_t, v_t, tb):
    def body(w_ref, g_ref, m_ref, v_ref, go_ref, d_ref, mo_ref, vo_ref, gt_ref):
        gt_ref[...] = g_ref[...].T
        gv = gt_ref[pl.ds(pl.multiple_of(lax.axis_index("y") * SHARD_SHIFT, SHARD_SHIFT), SHARD_IN), :]
        go_ref[...] = gv
        d_ref[...], mo_ref[...], vo_ref[...] = _adam_math(w_ref[...], gv, m_ref[...], v_ref[...])

    spec = pl.BlockSpec((SHARD_IN, tb), lambda i: (0, i))
    shape = jax.ShapeDtypeStruct((SHARD_IN, D_MODEL), F32)
    return _call(
        body, name="adamw_w_in", grid=(D_MODEL // tb,), out_shape=(shape,) * 4,
        in_specs=[spec, pl.BlockSpec((tb, SHARD_PAD), lambda i: (i, 0)), spec, spec], out_specs=(spec,) * 4,
        scratch_shapes=[pltpu.VMEM((SHARD_PAD, tb), F32)],
        compiler_params=_params(32, ("arbitrary",)),
    )(w_t, g_window, m_t, v_t)


def _adamw_small(sums, params):
    def body(sums_ref, *refs):
        ins, outs = refs[:3 * len(params)], refs[3 * len(params):]
        for k in range(len(params)):
            w_ref, m_ref, v_ref = ins[3 * k:3 * k + 3]
            g = sums_ref[k:k + 1, 0:w_ref.shape[1]]
            go_ref, d_ref, mo_ref, vo_ref = outs[4 * k:4 * k + 4]
            go_ref[...] = g
            d_ref[...], mo_ref[...], vo_ref[...] = _adam_math(w_ref[...], g, m_ref[...], v_ref[...])

    vmem = pl.BlockSpec(memory_space=pltpu.VMEM)
    flat = [a for p in params for a in p]
    shapes = tuple(jax.ShapeDtypeStruct(p[0].shape, F32) for p in params for _ in range(4))
    res = _call(body, name="adamw_small", out_shape=shapes, in_specs=[vmem] * (1 + len(flat)),
                out_specs=(vmem,) * len(shapes), compiler_params=_params(16))(sums, *flat)
    return [res[4 * k:4 * k + 4] for k in range(len(params))]


def kernel(x, norm_g, w_in, att_sinks, ret_gn_g, w_out, final_g, loss_target, m_norm_g, m_w_in, m_att_sinks, m_ret_gn_g, m_w_out, m_final_g, v_norm_g, v_w_in, v_att_sinks, v_ret_gn_g, v_w_out, v_final_g):
    seq = x.shape[1]
    xs, tgt = x[0], loss_target[0]
    final_g2 = final_g.reshape(1, D_MODEL)
    tables = _tables(seq)

    w_in_t, m_w_in_t, v_w_in_t = w_in[0].T, m_w_in[0].T, v_w_in[0].T
    w_in_full, w_out_full = _weights_allgather(w_in_t, w_out[0])
    proj, h_t = _in_proj(xs, norm_g, w_in_full, min(TOKENS_PROJ, seq))
    mix, dxo, states, loss_part, gfin, o_att = _mix_fwd(proj, xs, tgt, w_out_full, final_g2, ret_gn_g, att_sinks,
                                                        tables, min(TOKENS_MIX, seq))
    dproj, grad_x, gw_out, gnorm, dgain, dsink = _mix_bwd(proj, dxo, mix, o_att, states, xs, w_out_full, w_in_full,
                                                          norm_g, ret_gn_g, att_sinks, tables, min(TOKENS_MIX, seq))
    g_in, g_out, sums = _gw_in_reduce(h_t, dproj, gw_out, (gnorm, gfin, dgain, dsink, loss_part),
                                      min(TOKENS_GW, seq))

    res_in = [r.T for r in _adamw_w_in(w_in_t, g_in, m_w_in_t, v_w_in_t, 256)]
    res_out = _adamw("adamw_w_out", w_out[0], g_out, m_w_out[0], v_w_out[0], SHARD_OUT)
    as_row = lambda a: a.reshape(1, D_MODEL)
    r_norm, r_final, r_gain, r_sink = _adamw_small(sums, [
        (norm_g, m_norm_g, v_norm_g), (final_g2, as_row(m_final_g), as_row(v_final_g)),
        (ret_gn_g, m_ret_gn_g, v_ret_gn_g), (att_sinks, m_att_sinks, v_att_sinks)])

    outs = []
    for k in range(4):
        outs += [r_norm[k], res_in[k][None], r_sink[k], r_gain[k], res_out[k][None], r_final[k].reshape(D_MODEL)]
    return (sums[4, 0], grad_x[None], *outs)
```

```python
import jax
import jax.numpy as jnp
import numpy as np
from jax import lax
from jax.experimental import pallas as pl
from jax.experimental.pallas import tpu as pltpu

F32 = jnp.float32
BF16 = jnp.bfloat16

D_MODEL = 1024
ATT_HEADS = 8
ATT_KV_HEADS = 2
ATT_HEAD_DIM = 64
RET_HEADS = 4
RET_QK_DIM = 64
RET_V_DIM = 128
BLK = 128
ROT_BASE = 10000.0
RMS_EPS = 1e-6
GN_EPS = 1e-6
NEG_INF = -1e30
ATT_SCALE = ATT_HEAD_DIM ** -0.5
RET_SCALE = RET_QK_DIM ** -0.5

ATT_WIDTH = ATT_HEADS * ATT_HEAD_DIM
ATT_KV_WIDTH = ATT_KV_HEADS * ATT_HEAD_DIM
RET_QK_WIDTH = RET_HEADS * RET_QK_DIM
RET_WIDTH = RET_HEADS * RET_V_DIM
MIX_WIDTH = ATT_WIDTH + RET_WIDTH
OFF_AQ = 0
OFF_AK = OFF_AQ + ATT_WIDTH
OFF_AV = OFF_AK + ATT_KV_WIDTH
OFF_AZ = OFF_AV + ATT_KV_WIDTH
OFF_RQ = OFF_AZ + ATT_WIDTH
OFF_RK = OFF_RQ + RET_QK_WIDTH
OFF_RV = OFF_RK + RET_QK_WIDTH
OFF_RZ = OFF_RV + RET_WIDTH
IN_WIDTH = OFF_RZ + RET_WIDTH

LANE = 128
BF16_ROWS = 16
HALF_LANE = LANE // 2
PAIRS = RET_QK_WIDTH // LANE
assert ATT_HEAD_DIM == HALF_LANE and RET_QK_DIM == HALF_LANE and RET_V_DIM == LANE and ATT_KV_WIDTH == LANE

N_CHIPS = 4
N_DEV = 8
SHARD_IN = IN_WIDTH // N_CHIPS
SHARD_PAD = 768
SHARD_SHIFT = SHARD_PAD - SHARD_IN
WIN_START = tuple((j * SHARD_IN) // LANE * LANE for j in range(N_CHIPS))
SHARD_OUT = MIX_WIDTH // N_CHIPS

PACK_PARTS = 5
PACK_ROWS = 8
assert PACK_PARTS <= PACK_ROWS

ADAM_LR = 0.001
ADAM_B1 = 0.9
ADAM_B2 = 0.999
ADAM_EPS = 1e-08
ADAM_WD = 0.01
ADAM_STEP = 10

VMEM_CAP = 64 * 1024 * 1024
TOKENS_PROJ = 1024
TOKENS_MIX = 512
TOKENS_GW = 2048
MESH_ID = pl.DeviceIdType.MESH


def _call(body, **kw):
    return pl.pallas_call(body, **kw)


def _params(vmem_mb, semantics=None):
    assert vmem_mb * 1024 * 1024 < VMEM_CAP
    return pltpu.CompilerParams(dimension_semantics=semantics, vmem_limit_bytes=vmem_mb * 1024 * 1024)


def _dot(a, b):
    return jnp.dot(a, b, preferred_element_type=F32)


def _dot_nt(a, b):
    return lax.dot_general(a, b, (((1,), (1,)), ((), ())), preferred_element_type=F32)


def _dot_tn(a, b):
    return lax.dot_general(a, b, (((0,), (0,)), ((), ())), preferred_element_type=F32)


def _sigmoid(z):
    return 1.0 / (1.0 + jnp.exp(-z))


def _const_spec(shape):
    nd = len(shape)
    return pl.BlockSpec(shape, lambda i: (0,) * nd, pipeline_mode=pl.Buffered(1))


def _tables(seq):
    f32 = np.float32
    pos = np.arange(seq, dtype=f32)
    theta = (f32(1.0) / (f32(ROT_BASE) ** np.linspace(0.0, 1.0, RET_QK_DIM // 2, dtype=f32))).astype(f32)
    ang = (pos[:, None] * theta[None, :]).astype(f32)
    cos, sin = np.cos(ang), np.sin(ang)
    cos2 = np.repeat(cos, 2, axis=1)
    sin2 = np.stack([-sin, sin], axis=-1).reshape(seq, RET_QK_DIM)
    cos_t = np.tile(cos2, (1, 2))
    sin_t = np.tile(sin2, (1, 2))

    log_gamma = np.log(f32(1.0) - f32(2.0) ** (f32(-5.0) - np.arange(RET_HEADS, dtype=f32))).astype(f32)
    idx = np.arange(BLK, dtype=f32)
    rel = idx[:, None] - idx[None, :]
    decay_in = np.where(rel >= 0, np.exp(log_gamma[:, None, None] * np.maximum(rel, f32(0.0))), f32(0.0))
    k_dec = np.exp(log_gamma[:, None] * (BLK - 1 - idx)[None, :])
    q_dec = np.exp(log_gamma[:, None] * (idx + 1)[None, :])
    chunk_decay = np.exp(log_gamma * f32(BLK))
    kdec_t = np.repeat(k_dec.T, RET_QK_DIM, axis=1)
    qdec_t = np.repeat(q_dec.T, RET_QK_DIM, axis=1)
    cd_t = np.broadcast_to(chunk_decay[:, None, None], (RET_HEADS, RET_QK_DIM, RET_V_DIM))
    decay_in = decay_in.reshape(PAIRS, 2 * BLK, BLK)
    cd_t = cd_t.reshape(PAIRS, 2 * RET_QK_DIM, RET_V_DIM)

    key = np.arange(BLK)[:, None]
    query = np.arange(2 * BLK)[None, :] % BLK
    bias = np.stack([np.where(key > query, NEG_INF, 0.0), np.zeros((BLK, 2 * BLK))])
    return tuple(jnp.asarray(np.ascontiguousarray(a), F32) for a in (cos_t, sin_t, decay_in, qdec_t, kdec_t, cd_t, bias))


def _low_lanes(shape):
    lane = lax.broadcasted_iota(jnp.int32, shape, len(shape) - 1)
    return (lane & HALF_LANE) == 0


def _split_heads(t):
    low = _low_lanes(t.shape)
    zero = jnp.zeros_like(t)
    return jnp.where(low, t, zero), jnp.where(low, zero, t)


def _swap_pairs(t):
    lane = lax.broadcasted_iota(jnp.int32, t.shape, 1)
    nxt = pltpu.roll(t, t.shape[1] - 1, 1)
    prv = pltpu.roll(t, 1, 1)
    return jnp.where((lane & 1) == 0, nxt, prv)


def _per_tile(fn, t):
    return jnp.concatenate([fn(_tile(t, i)) for i in range(t.shape[1] // LANE)], axis=1)


def _rotate(t, cos_t, sin_t):
    return _per_tile(lambda a: a * cos_t + _swap_pairs(a) * sin_t, t)


def _rotate_transposed(d, cos_t, sin_t):
    return _per_tile(lambda a: a * cos_t + _swap_pairs(a * sin_t), d)


def _kv_operands(cat):
    low = _low_lanes(cat.shape)
    swapped = pltpu.roll(cat, HALF_LANE, 1)
    zero = jnp.zeros_like(cat)
    pick = lambda a, b: jnp.where(low, a, b).astype(BF16)
    return ((pick(cat, zero), pick(zero, swapped)), (pick(swapped, zero), pick(zero, cat)))


def _stack_tiles(t, first_tile):
    a = t[:, first_tile * LANE:(first_tile + 1) * LANE]
    b = t[:, (first_tile + 1) * LANE:(first_tile + 2) * LANE]
    return jnp.concatenate([a, b], axis=0)


def _sink_rows(sinks_ref, group):
    first = lax.broadcasted_iota(jnp.int32, (1, 2 * BLK), 1) < BLK

    def row(h0, h1):
        return jnp.where(first, sinks_ref[0, group * 4 + h0], sinks_ref[0, group * 4 + h1])
    return row(0, 2), row(1, 3)


ATT_PROBLEMS = tuple((g, hi) for g in range(ATT_KV_HEADS) for hi in range(2))


def _in_previous_block():
    key = lax.broadcasted_iota(jnp.int32, (BLK, 2 * BLK), 0)
    query = lax.broadcasted_iota(jnp.int32, (BLK, 2 * BLK), 1) & (BLK - 1)
    return key > query


def _fold(t, prev):
    return jnp.where(prev, t[0:BLK], t[BLK:])


def _unfold(t, prev):
    zero = jnp.zeros_like(t)
    return jnp.concatenate([jnp.where(prev, t, zero), jnp.where(prev, zero, t)], axis=0)


def _attn_probs(qs, k_ops, bias, prev, sinks_ref):
    sink = [_sink_rows(sinks_ref, g)[hi] for g, hi in ATT_PROBLEMS]
    s = [_fold(_dot_nt(k_ops[g][hi], qs[g]), prev) + bias for g, hi in ATT_PROBLEMS]
    m = [jnp.maximum(jnp.max(si, axis=0, keepdims=True), ki) for si, ki in zip(s, sink)]
    e = [jnp.exp(si - mi) for si, mi in zip(s, m)]
    es = [jnp.exp(ki - mi) for ki, mi in zip(sink, m)]
    inv = [1.0 / (jnp.sum(ei, axis=0, keepdims=True) + esi) for ei, esi in zip(e, es)]
    return [ei * ii for ei, ii in zip(e, inv)], [esi * ii for esi, ii in zip(es, inv)]


def _group_norm_all(outs):
    mu = [jnp.mean(o, axis=1, keepdims=True) for o in outs]
    xc = [o - m for o, m in zip(outs, mu)]
    var = [jnp.mean(c * c, axis=1, keepdims=True) for c in xc]
    rstd = [lax.rsqrt(v + GN_EPS) for v in var]
    return [c * r for c, r in zip(xc, rstd)], rstd


def _retention_operands(p_ref, rows, cos_b, sin_b, qdec, kdec):
    qr = _rotate(p_ref[rows, OFF_RQ:OFF_RQ + RET_QK_WIDTH].astype(F32), cos_b, sin_b)
    kr = _rotate(p_ref[rows, OFF_RK:OFF_RK + RET_QK_WIDTH].astype(F32), cos_b, sin_b) * RET_SCALE
    return qr, kr, qr * qdec, kr * kdec


def _tile(t, i):
    return t[:, i * LANE:(i + 1) * LANE]


GATHER_SEMS = 7


def _allgather_steps(src_ref, full_ref, blk_ref, send_sems, recv_sems):
    block_rows = blk_ref.shape[0]
    assert block_rows % BF16_ROWS == 0 and full_ref.shape[0] == N_DEV * block_rows
    x, y, c = lax.axis_index("x"), lax.axis_index("y"), lax.axis_index("c")
    me, sibling = (x, y, c), (x, y, 1 - c)
    chips = [(1 - x, y), (x, 1 - y), (1 - x, 1 - y)]

    def rows(px, py, pc):
        return full_ref.at[pl.ds(pl.multiple_of((4 * px + 2 * py + pc) * block_rows, BF16_ROWS), block_rows), :]

    def copy(k, block, to, src=None):
        return pltpu.make_async_remote_copy(
            src_ref=rows(*block) if src is None else src, dst_ref=rows(*block),
            send_sem=send_sems.at[k], recv_sem=recv_sems.at[k], device_id=to, device_id_type=MESH_ID)

    first = [copy(0, me, sibling, src=blk_ref)] + [copy(1 + j, me, (*chip, c), src=blk_ref) for j, chip in enumerate(chips)]
    passed = [copy(4 + j, (*chip, c), sibling) for j, chip in enumerate(chips)]

    def start():
        blk_ref[...] = src_ref[pl.ds(pl.multiple_of(c * block_rows, BF16_ROWS), block_rows), :].astype(BF16)
        rows(*me)[...] = blk_ref[...]
        for cp in first:
            cp.start()

    def forward():
        for j, chip in enumerate(chips):
            copy(1 + j, (*chip, c), me).wait_recv()
            passed[j].start()

    def finish():
        copy(0, sibling, me).wait_recv()
        for j, chip in enumerate(chips):
            copy(4 + j, (*chip, 1 - c), me).wait_recv()
        for cp in first + passed:
            cp.wait_send()

    return start, forward, finish


def _w_in_allgather(w_in_t_shard):
    def body(wi_ref, win_ref, blk_ref, send_sems, recv_sems):
        for step in _allgather_steps(wi_ref, win_ref, blk_ref, send_sems, recv_sems):
            step()

    vmem = pl.BlockSpec(memory_space=pltpu.VMEM)
    return _call(
        body, name="w_in_allgather", out_shape=jax.ShapeDtypeStruct((IN_WIDTH, D_MODEL), BF16),
        in_specs=[vmem], out_specs=vmem,
        scratch_shapes=[pltpu.VMEM((SHARD_IN // 2, D_MODEL), BF16),
                        pltpu.SemaphoreType.DMA((GATHER_SEMS,)), pltpu.SemaphoreType.DMA((GATHER_SEMS,))],
        compiler_params=_params(32),
    )(w_in_t_shard)


def _in_proj(x, norm_g, w_in, w_out_shard, tb):
    seq = x.shape[0]
    last = seq // tb - 1

    def body(x_ref, g_ref, w_ref, wo_ref, p_ref, ht_ref, wout_ref, wout_full, blk_ref, send_sems, recv_sems):
        i = pl.program_id(0)
        gather_start, gather_forward, gather_finish = _allgather_steps(wo_ref, wout_full, blk_ref, send_sems, recv_sems)

        @pl.when(i == 0)
        def _():
            gather_start()

        xv = x_ref[...]
        r = lax.rsqrt(jnp.mean(xv * xv, axis=1, keepdims=True) + RMS_EPS)
        h = (xv * r) * g_ref[...]
        p_ref[...] = _dot_nt(h.astype(BF16), w_ref[...]).astype(BF16)
        ht_ref[...] = h.T.astype(BF16)

        @pl.when(i == min(1, last))
        def _():
            gather_forward()

        @pl.when(i == last)
        def _():
            gather_finish()
            wout_ref[...] = wout_full[...]

    return _call(
        body, name="in_proj", grid=(seq // tb,),
        out_shape=(jax.ShapeDtypeStruct((seq, IN_WIDTH), BF16), jax.ShapeDtypeStruct((D_MODEL, seq), BF16),
                   jax.ShapeDtypeStruct((MIX_WIDTH, D_MODEL), BF16)),
        in_specs=[pl.BlockSpec((tb, D_MODEL), lambda i: (i, 0)), _const_spec((1, D_MODEL)),
                  _const_spec((IN_WIDTH, D_MODEL)), _const_spec((SHARD_OUT, D_MODEL))],
        out_specs=(pl.BlockSpec((tb, IN_WIDTH), lambda i: (i, 0)), pl.BlockSpec((D_MODEL, tb), lambda i: (0, i)),
                   _const_spec((MIX_WIDTH, D_MODEL))),
        scratch_shapes=[pltpu.VMEM((MIX_WIDTH, D_MODEL), BF16), pltpu.VMEM((SHARD_OUT // 2, D_MODEL), BF16),
                        pltpu.SemaphoreType.DMA((GATHER_SEMS,)), pltpu.SemaphoreType.DMA((GATHER_SEMS,))],
        compiler_params=_params(56, ("arbitrary",)),
    )(x, norm_g, w_in, w_out_shard)


def _mix_fwd(proj, x, target, w_out, final_g, gn_gain, sinks, tables, tb):
    seq = x.shape[0]
    nsub = tb // BLK
    cos_t, sin_t, decay_in, qdec_t, kdec_t, cd_t, bias_t = tables

    def body(p_ref, x_ref, t_ref, cos_ref, sin_ref, wout_ref, fg_ref, gain_ref, sinks_ref, din_ref, qdec_ref,
             kdec_ref, cd_ref, bias_ref, mix_ref, dxo_ref, st_ref, loss_ref, gfin_ref, oatt_ref,
             kprev_ref, vprev_ref, state_ref):
        i = pl.program_id(0)

        @pl.when(i == 0)
        def _():
            kprev_ref[...] = jnp.zeros_like(kprev_ref)
            vprev_ref[...] = jnp.zeros_like(vprev_ref)
            state_ref[...] = jnp.zeros_like(state_ref)
            loss_ref[...] = jnp.zeros_like(loss_ref)
            gfin_ref[...] = jnp.zeros_like(gfin_ref)

        prev = _in_previous_block()

        def sub(j, carry):
            kp, vp, states = carry
            rows = pl.ds(pl.multiple_of(j * BLK, BLK), BLK)
            bias = bias_ref[jnp.where(jnp.logical_or(i > 0, j > 0), 1, 0)]

            aq = p_ref[rows, OFF_AQ:OFF_AQ + ATT_WIDTH]
            ak = p_ref[rows, OFF_AK:OFF_AK + ATT_KV_WIDTH].astype(F32)
            av = p_ref[rows, OFF_AV:OFF_AV + ATT_KV_WIDTH].astype(F32)
            az = p_ref[rows, OFF_AZ:OFF_AZ + ATT_WIDTH].astype(F32)
            k_ops = _kv_operands(jnp.concatenate([kp, ak], axis=0))
            v_ops = _kv_operands(jnp.concatenate([vp, av], axis=0))
            qs = [(_stack_tiles(aq, 2 * g) * ATT_SCALE).astype(BF16) for g in range(ATT_KV_HEADS)]
            p, _ = _attn_probs(qs, k_ops, bias, prev, sinks_ref)
            o_tiles = []
            for g in range(ATT_KV_HEADS):
                p_cat = jnp.concatenate([_unfold(p[2 * g].astype(BF16), prev), _unfold(p[2 * g + 1].astype(BF16), prev)],
                                        axis=0)
                o = _dot_tn(p_cat, jnp.concatenate(v_ops[g], axis=0))
                o_tiles += [o[0:BLK], o[BLK:]]
            o_att = jnp.concatenate(o_tiles, axis=1)
            oatt_ref[rows, :] = o_att
            out = [o_att * (az * _sigmoid(az))]

            qr, kr, qd, kd = _retention_operands(p_ref, rows, cos_ref[rows, :], sin_ref[rows, :],
                                                 qdec_ref[...], kdec_ref[...])
            heads = [(t, hh) for t in range(PAIRS) for hh in range(2)]
            sc = [_dot_nt(jnp.concatenate(_split_heads(_tile(qr, t)), axis=0).astype(BF16), _tile(kr, t).astype(BF16))
                  * din_ref[t] for t in range(PAIRS)]
            qd_heads = [_split_heads(_tile(qd, t)) for t in range(PAIRS)]
            state_b = [states[t].astype(BF16) for t in range(PAIRS)]
            vs = [p_ref[rows, OFF_RV + h * RET_V_DIM:OFF_RV + (h + 1) * RET_V_DIM].astype(BF16) for h in range(RET_HEADS)]
            rzs = [p_ref[rows, OFF_RZ + h * RET_V_DIM:OFF_RZ + (h + 1) * RET_V_DIM].astype(F32) for h in range(RET_HEADS)]
            lhs = [jnp.concatenate([sc[t][hh * BLK:(hh + 1) * BLK].astype(BF16), qd_heads[t][hh].astype(BF16)], axis=1)
                   for t, hh in heads]
            ons, _ = _group_norm_all([_dot(lhs[2 * t + hh], jnp.concatenate([vs[2 * t + hh], state_b[t]], axis=0))
                                      for t, hh in heads])
            out += [(ons[h] * gain_ref[:, h * RET_V_DIM:(h + 1) * RET_V_DIM]) * (rzs[h] * _sigmoid(rzs[h]))
                    for h in range(RET_HEADS)]
            new_states = [states[t] * cd_ref[t]
                          + _dot_tn(jnp.concatenate(_split_heads(_tile(kd, t)), axis=0).astype(BF16),
                                    jnp.concatenate([vs[2 * t], vs[2 * t + 1]], axis=0)) for t in range(PAIRS)]
            mix_ref[rows, :] = jnp.concatenate(out, axis=1).astype(BF16)
            st_ref[j] = jnp.stack(states)
            return ak, av, tuple(new_states)

        carry = (kprev_ref[...], vprev_ref[...], tuple(state_ref[t] for t in range(PAIRS)))
        kp, vp, states = lax.fori_loop(0, nsub, sub, carry)
        kprev_ref[...] = kp
        vprev_ref[...] = vp
        state_ref[...] = jnp.stack(states)

        xo = x_ref[...] + _dot(mix_ref[...], wout_ref[...])
        r2 = lax.rsqrt(jnp.mean(xo * xo, axis=1, keepdims=True) + RMS_EPS)
        xn = xo * r2
        err = xn * fg_ref[...] - t_ref[...]
        loss_ref[...] += jnp.sum(err * err) * (0.5 / D_MODEL)
        dy = err * (1.0 / D_MODEL)
        gfin_ref[...] += jnp.sum(dy * xn, axis=0, keepdims=True)
        u = dy * fg_ref[...]
        dxo_ref[...] = r2 * u - xn * (r2 * jnp.mean(u * xn, axis=1, keepdims=True))

    blk_rows = lambda w: pl.BlockSpec((tb, w), lambda i: (i, 0))
    state_shape = (PAIRS, 2 * RET_QK_DIM, RET_V_DIM)
    return _call(
        body, name="mix_fwd", grid=(seq // tb,),
        out_shape=(
            jax.ShapeDtypeStruct((seq, MIX_WIDTH), BF16),
            jax.ShapeDtypeStruct((seq, D_MODEL), F32),
            jax.ShapeDtypeStruct((seq // BLK,) + state_shape, F32),
            jax.ShapeDtypeStruct((8, LANE), F32),
            jax.ShapeDtypeStruct((1, D_MODEL), F32),
            jax.ShapeDtypeStruct((seq, ATT_WIDTH), F32),
        ),
        in_specs=[
            blk_rows(IN_WIDTH), blk_rows(D_MODEL), blk_rows(D_MODEL), blk_rows(LANE), blk_rows(LANE),
            _const_spec((MIX_WIDTH, D_MODEL)), _const_spec((1, D_MODEL)), _const_spec((1, RET_WIDTH)),
            pl.BlockSpec(memory_space=pltpu.SMEM),
            _const_spec((PAIRS, 2 * BLK, BLK)), _const_spec((BLK, RET_QK_WIDTH)), _const_spec((BLK, RET_QK_WIDTH)),
            _const_spec(state_shape), _const_spec((2, BLK, 2 * BLK)),
        ],
        out_specs=(
            blk_rows(MIX_WIDTH), blk_rows(D_MODEL),
            pl.BlockSpec((nsub,) + state_shape, lambda i: (i, 0, 0, 0)),
            _const_spec((8, LANE)), _const_spec((1, D_MODEL)), blk_rows(ATT_WIDTH),
        ),
        scratch_shapes=[
            pltpu.VMEM((BLK, ATT_KV_WIDTH), F32), pltpu.VMEM((BLK, ATT_KV_WIDTH), F32),
            pltpu.VMEM(state_shape, F32),
        ],
        compiler_params=_params(48, ("arbitrary",)),
    )(proj, x, target, cos_t, sin_t, w_out, final_g, gn_gain, sinks, decay_in, qdec_t, kdec_t, cd_t, bias_t)


def _mix_bwd(proj, dxo, mix, o_att, states, x, w_out, w_in_t, norm_g, gn_gain, sinks, tables, tb):
    seq = dxo.shape[0]
    nsub = tb // BLK
    nblk = seq // tb
    cos_t, sin_t, decay_in, qdec_t, kdec_t, cd_t, bias_t = tables
    kv_cols = OFF_AK // (2 * ATT_KV_WIDTH)
    state_shape = (PAIRS, 2 * RET_QK_DIM, RET_V_DIM)

    def body(p_ref, pkv_ref, dxo_ref, mix_ref, oatt_ref, st_ref, cos_ref, sin_ref, x_ref, wout_ref, win_ref, g_ref,
             gain_ref, sinks_ref, din_ref, qdec_ref, kdec_ref, cd_ref, bias_ref,
             dp_ref, gx_ref, gwout_ref, gnorm_ref, dgain_ref, dsink_ref,
             dmix_ref, kv_ref, dkc_ref, dvc_ref, gst_ref):
        i = pl.program_id(0)
        blk = nblk - 1 - i

        @pl.when(i == 0)
        def _():
            gwout_ref[...] = jnp.zeros_like(gwout_ref)
            gnorm_ref[...] = jnp.zeros_like(gnorm_ref)
            dgain_ref[...] = jnp.zeros_like(dgain_ref)
            dsink_ref[...] = jnp.zeros_like(dsink_ref)
            dkc_ref[...] = jnp.zeros_like(dkc_ref)
            dvc_ref[...] = jnp.zeros_like(dvc_ref)
            gst_ref[...] = jnp.zeros_like(gst_ref)

        dxo_b = dxo_ref[...].astype(BF16)
        dmix_ref[...] = _dot_nt(dxo_b, wout_ref[...])
        gwout_ref[...] += _dot_tn(mix_ref[...], dxo_b)
        kv_ref[0:BLK, :] = pkv_ref[...].astype(F32)
        kv_ref[BLK:, :] = p_ref[:, OFF_AK:OFF_AK + 2 * ATT_KV_WIDTH].astype(F32)
        low = _low_lanes((BLK, LANE))
        low2 = _low_lanes((2 * BLK, LANE))
        lane = lax.broadcasted_iota(jnp.int32, (1, LANE), 1)
        prev = _in_previous_block()

        def sub(jj, carry):
            dkc, dvc, gsts, dgain, dsink = carry
            j = nsub - 1 - jj
            rows = pl.ds(pl.multiple_of(j * BLK, BLK), BLK)
            both = pl.ds(pl.multiple_of(j * BLK, BLK), 2 * BLK)
            bias = bias_ref[jnp.where(jnp.logical_or(blk > 0, j > 0), 1, 0)]

            aq = p_ref[rows, OFF_AQ:OFF_AQ + ATT_WIDTH]
            az = p_ref[rows, OFF_AZ:OFF_AZ + ATT_WIDTH].astype(F32)
            k_ops = _kv_operands(kv_ref[both, 0:ATT_KV_WIDTH])
            v_ops = _kv_operands(kv_ref[both, ATT_KV_WIDTH:2 * ATT_KV_WIDTH])
            da = dmix_ref[rows, 0:ATT_WIDTH]
            sig = _sigmoid(az)
            d_o = da * (az * sig)
            qs = [(_stack_tiles(aq, 2 * g) * ATT_SCALE).astype(BF16) for g in range(ATT_KV_HEADS)]
            dos = [_stack_tiles(d_o, 2 * g).astype(BF16) for g in range(ATT_KV_HEADS)]
            p, p_sink = _attn_probs(qs, k_ops, bias, prev, sinks_ref)
            dpr = [_fold(_dot_nt(v_ops[g][hi], dos[g]), prev) for g, hi in ATT_PROBLEMS]
            delta = [jnp.sum(pi * di, axis=0, keepdims=True) for pi, di in zip(p, dpr)]
            ds = [_unfold((pi * (di - ti)).astype(BF16), prev) for pi, di, ti in zip(p, dpr, delta)]
            for (g, hi), ki, ti in zip(ATT_PROBLEMS, p_sink, delta):
                sink_part = ki * ti
                for half in range(2):
                    tot = jnp.sum(sink_part[:, half * BLK:(half + 1) * BLK], axis=1, keepdims=True)
                    dsink = dsink - jnp.where(lane == 4 * g + 2 * half + hi, tot, 0.0)
            dq_tiles, dk_sums, dv_sums = [], [], []
            for g in range(ATT_KV_HEADS):
                ds_cat = jnp.concatenate([ds[2 * g], ds[2 * g + 1]], axis=0)
                p_cat = jnp.concatenate([_unfold(p[2 * g].astype(BF16), prev), _unfold(p[2 * g + 1].astype(BF16), prev)],
                                        axis=0)
                dqs = _dot_tn(ds_cat, jnp.concatenate(k_ops[g], axis=0)) * ATT_SCALE
                dq_tiles += [dqs[0:BLK], dqs[BLK:]]
                dk_sums.append(_dot(ds_cat, qs[g]))
                dv_sums.append(_dot(p_cat, dos[g]))
            daz = da * oatt_ref[rows, :] * (sig * (1.0 + az * (1.0 - sig)))

            def kv_grad(sums):
                (a0, b0), (a1, b1) = [(s[0:2 * BLK], s[2 * BLK:]) for s in sums]
                return jnp.where(low2, a0, b1) + pltpu.roll(jnp.where(low2, a1, b0), HALF_LANE, 1)

            dk_both, dv_both = kv_grad(dk_sums), kv_grad(dv_sums)
            dak = dk_both[BLK:] + dkc
            dav = dv_both[BLK:] + dvc

            cos_b, sin_b = cos_ref[rows, :], sin_ref[rows, :]
            qdec, kdec = qdec_ref[...], kdec_ref[...]
            qr, kr, qd, kd = _retention_operands(p_ref, rows, cos_b, sin_b, qdec, kdec)
            heads = [(t, hh) for t in range(PAIRS) for hh in range(2)]
            head_cols = [slice(h * RET_V_DIM, (h + 1) * RET_V_DIM) for h in range(RET_HEADS)]
            q_rows = [jnp.concatenate(_split_heads(_tile(qr, t)), axis=0).astype(BF16) for t in range(PAIRS)]
            k_rows = [jnp.concatenate(_split_heads(_tile(kr, t)), axis=0).astype(BF16) for t in range(PAIRS)]
            din = [din_ref[t] for t in range(PAIRS)]
            sc = [(_dot_nt(q_rows[t], _tile(kr, t).astype(BF16)) * din[t]).astype(BF16) for t in range(PAIRS)]
            qd_heads = [_split_heads(_tile(qd, t)) for t in range(PAIRS)]
            kd_heads = [_split_heads(_tile(kd, t)) for t in range(PAIRS)]
            state_b = [st_ref[j, t].astype(BF16) for t in range(PAIRS)]
            gst_b = [gsts[t].astype(BF16) for t in range(PAIRS)]
            vs = [p_ref[rows, OFF_RV + h * RET_V_DIM:OFF_RV + (h + 1) * RET_V_DIM].astype(BF16) for h in range(RET_HEADS)]
            rzs = [p_ref[rows, OFF_RZ + h * RET_V_DIM:OFF_RZ + (h + 1) * RET_V_DIM].astype(F32) for h in range(RET_HEADS)]
            drs = [dmix_ref[rows, ATT_WIDTH + h * RET_V_DIM:ATT_WIDTH + (h + 1) * RET_V_DIM] for h in range(RET_HEADS)]
            gains = [gain_ref[:, c] for c in head_cols]
            lhs = [jnp.concatenate([sc[t][hh * BLK:(hh + 1) * BLK], qd_heads[t][hh].astype(BF16)], axis=1) for t, hh in heads]
            rhs = [jnp.concatenate([vs[2 * t + hh], state_b[t]], axis=0) for t, hh in heads]
            ons, rstds = _group_norm_all([_dot(l, r) for l, r in zip(lhs, rhs)])
            sig_r = [_sigmoid(z) for z in rzs]
            dgn = [d * (z * g) for d, z, g in zip(drs, rzs, sig_r)]
            dz_parts = [d * (o * gn) * (g * (1.0 + z * (1.0 - g))) for d, o, gn, g, z in zip(drs, ons, gains, sig_r, rzs)]
            dgain_parts = [jnp.sum(d * o, axis=0, keepdims=True) for d, o in zip(dgn, ons)]
            don = [d * gn for d, gn in zip(dgn, gains)]
            mean_don = [jnp.mean(d, axis=1, keepdims=True) for d in don]
            mean_don_on = [jnp.mean(d * o, axis=1, keepdims=True) for d, o in zip(don, ons)]
            dob = [(r * (d - a - o * b)).astype(BF16) for r, d, a, o, b in zip(rstds, don, mean_don, ons, mean_don_on)]
            dlhs = [_dot_nt(d, r) for d, r in zip(dob, rhs)]
            drhs = [_dot_tn(l, d) for l, d in zip(lhs, dob)]
            dkds = [_dot_nt(vs[2 * t + hh], gst_b[t]) for t, hh in heads]
            dv_parts = [drhs[2 * t + hh][0:BLK] + _dot(kd_heads[t][hh].astype(BF16), gst_b[t]) for t, hh in heads]
            das = [(dlhs[2 * t + hh][:, 0:BLK] * din[t][hh * BLK:(hh + 1) * BLK]).astype(BF16) for t, hh in heads]
            new_gsts = [gsts[t] * cd_ref[t] + drhs[2 * t][BLK:] + drhs[2 * t + 1][BLK:] for t in range(PAIRS)]
            dq_parts = [_dot(jnp.concatenate([das[2 * t], das[2 * t + 1]], axis=1), k_rows[t])
                        + jnp.where(low, dlhs[2 * t][:, BLK:], dlhs[2 * t + 1][:, BLK:]) * _tile(qdec, t)
                        for t in range(PAIRS)]
            dk_parts = [_dot_tn(jnp.concatenate([das[2 * t], das[2 * t + 1]], axis=0), q_rows[t])
                        + jnp.where(low, dkds[2 * t], dkds[2 * t + 1]) * _tile(kdec, t) for t in range(PAIRS)]
            drq = _rotate_transposed(jnp.concatenate(dq_parts, axis=1), cos_b, sin_b)
            drk = _rotate_transposed(jnp.concatenate(dk_parts, axis=1) * RET_SCALE, cos_b, sin_b)

            dp_ref[rows, :] = jnp.concatenate(
                [jnp.concatenate(dq_tiles, axis=1), dak, dav, daz, drq, drk] + dv_parts + dz_parts, axis=1).astype(BF16)
            dgain = dgain + jnp.concatenate(dgain_parts, axis=1)
            return dk_both[0:BLK], dv_both[0:BLK], tuple(new_gsts), dgain, dsink

        carry = (dkc_ref[...], dvc_ref[...], tuple(gst_ref[t] for t in range(PAIRS)), dgain_ref[...], dsink_ref[...])
        dkc, dvc, gsts, dgain, dsink = lax.fori_loop(0, nsub, sub, carry)
        dkc_ref[...] = dkc
        dvc_ref[...] = dvc
        gst_ref[...] = jnp.stack(gsts)
        dgain_ref[...] = dgain
        dsink_ref[...] = dsink

        dh = _dot(dp_ref[...], win_ref[...])
        xv = x_ref[...]
        r = lax.rsqrt(jnp.mean(xv * xv, axis=1, keepdims=True) + RMS_EPS)
        xn = xv * r
        gnorm_ref[...] += jnp.sum(dh * xn, axis=0, keepdims=True)
        u = dh * g_ref[...]
        gx_ref[...] = dxo_ref[...] + r * u - xn * (r * jnp.mean(u * xn, axis=1, keepdims=True))

    rev_rows = lambda w: pl.BlockSpec((tb, w), lambda i: (nblk - 1 - i, 0))
    prev_kv = pl.BlockSpec((BLK, 2 * ATT_KV_WIDTH), lambda i: (jnp.maximum((nblk - 1 - i) * nsub - 1, 0), kv_cols))
    return _call(
        body, name="mix_bwd", grid=(nblk,),
        out_shape=(
            jax.ShapeDtypeStruct((seq, IN_WIDTH), BF16),
            jax.ShapeDtypeStruct((seq, D_MODEL), F32),
            jax.ShapeDtypeStruct((MIX_WIDTH, D_MODEL), F32),
            jax.ShapeDtypeStruct((1, D_MODEL), F32),
            jax.ShapeDtypeStruct((1, RET_WIDTH), F32),
            jax.ShapeDtypeStruct((1, LANE), F32),
        ),
        in_specs=[
            rev_rows(IN_WIDTH), prev_kv, rev_rows(D_MODEL), rev_rows(MIX_WIDTH), rev_rows(ATT_WIDTH),
            pl.BlockSpec((nsub,) + state_shape, lambda i: (nblk - 1 - i, 0, 0, 0)),
            rev_rows(LANE), rev_rows(LANE), rev_rows(D_MODEL),
            _const_spec((MIX_WIDTH, D_MODEL)), _const_spec((IN_WIDTH, D_MODEL)), _const_spec((1, D_MODEL)),
            _const_spec((1, RET_WIDTH)),
            pl.BlockSpec(memory_space=pltpu.SMEM),
            _const_spec((PAIRS, 2 * BLK, BLK)), _const_spec((BLK, RET_QK_WIDTH)), _const_spec((BLK, RET_QK_WIDTH)),
            _const_spec(state_shape), _const_spec((2, BLK, 2 * BLK)),
        ],
        out_specs=(
            rev_rows(IN_WIDTH), rev_rows(D_MODEL), _const_spec((MIX_WIDTH, D_MODEL)), _const_spec((1, D_MODEL)),
            _const_spec((1, RET_WIDTH)), _const_spec((1, LANE)),
        ),
        scratch_shapes=[
            pltpu.VMEM((tb, MIX_WIDTH), F32),
            pltpu.VMEM((tb + BLK, 2 * ATT_KV_WIDTH), F32),
            pltpu.VMEM((BLK, ATT_KV_WIDTH), F32), pltpu.VMEM((BLK, ATT_KV_WIDTH), F32),
            pltpu.VMEM(state_shape, F32),
        ],
        compiler_params=_params(60, ("arbitrary",)),
    )(proj, proj, dxo, mix, o_att, states, cos_t, sin_t, x, w_out, w_in_t, norm_g, gn_gain, sinks, decay_in, qdec_t,
      kdec_t, cd_t, bias_t)


def _gw_in_reduce(h_t, dproj, gw_out, small, tb):
    seq = dproj.shape[0]
    nblk = seq // tb
    last = nblk - 1
    hand_on = min(1, last)
    half = D_MODEL // 2
    A, B, C, N_SEMS = 0, N_CHIPS, 2 * N_CHIPS, 2 * N_CHIPS + 1

    def body(win_ref, ht_ref, dp_ref, gwo_hbm, s0_ref, s1_ref, s2_ref, s3_ref, s4_ref, out_ref, fout_ref, packsum_ref,
             acc, sib, send_buf, b_in, fin, fout, mine_out, sib_out, send_out, b_out, pack_ref, packs,
             send_sems, recv_sems, local_sems):
        p, i = pl.program_id(0), pl.program_id(1)
        small_start, small_hand_on, small_finish = _small_exchange(
            gwo_hbm, (s0_ref, s1_ref, s2_ref, s3_ref, s4_ref), fout, mine_out, sib_out, send_out, b_out, pack_ref,
            packs, send_sems, recv_sems, local_sems, N_SEMS)

        @pl.when(jnp.logical_and(p == 0, i == 0))
        def _():
            small_start()

        @pl.when(jnp.logical_and(p == 0, i == hand_on))
        def _():
            small_hand_on()

        x, y, c = lax.axis_index("x"), lax.axis_index("y"), lax.axis_index("c")
        chip = 2 * x + y
        sibling = (x, y, 1 - c)
        mine = pl.ds(pl.multiple_of(c * half, half), half)
        other = pl.ds(pl.multiple_of((1 - c) * half, half), half)

        def remote(src, dst, send_k, recv_k, to):
            return pltpu.make_async_remote_copy(src_ref=src, dst_ref=dst, send_sem=send_sems.at[send_k],
                                                recv_sem=recv_sems.at[recv_k], device_id=to, device_id_type=MESH_ID)

        part = _dot(ht_ref[...], dp_ref[...])
        slot = p % 2

        @pl.when(i == 0)
        def _():
            acc[slot] = part

        @pl.when(i > 0)
        def _():
            acc[slot] += part

        for q in range(N_CHIPS):
            s = q % 2
            to_sibling = remote(acc.at[s, other, :], sib.at[s], A + q, A + q, sibling)

            @pl.when(jnp.logical_and(p == q, i == last))
            def _():
                to_sibling.start()

            if q < N_CHIPS - 1:
                dest = (chip + 1 + q) % N_CHIPS

                @pl.when(jnp.logical_and(p == q + 1, i == hand_on))
                def _():
                    to_sibling.wait_recv()
                    send_buf[q] = (acc[s, mine, :] + sib[s]).astype(BF16)
                    remote(send_buf.at[q], b_in.at[chip], B + q, B + chip, (dest // 2, dest % 2, c)).start()
                    to_sibling.wait_send()
            else:
                @pl.when(jnp.logical_and(p == q, i == last))
                def _():
                    to_sibling.wait_recv()
                    fin[mine, :] = acc[s, mine, :] + sib[s]
                    for j in range(N_CHIPS):
                        @pl.when(j != chip)
                        def _():
                            remote(b_in.at[j], b_in.at[j], B + j, B + j, sibling).wait_recv()
                            fin[mine, :] += b_in[j].astype(F32)
                    to_core = remote(fin.at[mine, :], fin.at[mine, :], C, C, sibling)
                    to_core.start()
                    remote(fin.at[other, :], fin.at[other, :], C, C, sibling).wait_recv()
                    out_ref[...] = fin[...]
                    to_core.wait_send()
                    to_sibling.wait_send()
                    for k in range(N_CHIPS - 1):
                        remote(send_buf.at[k], b_in.at[chip], B + k, B + k, sibling).wait_send()
                    packsum_ref[...] = small_finish()
                    fout_ref[...] = fout[...]

    whole = lambda shape: pl.BlockSpec(shape, lambda p, i, win: (0,) * len(shape), pipeline_mode=pl.Buffered(1))
    grid_spec = pltpu.PrefetchScalarGridSpec(
        num_scalar_prefetch=1, grid=(N_CHIPS, nblk),
        in_specs=[pl.BlockSpec((D_MODEL, tb), lambda p, i, win: (0, i)),
                  pl.BlockSpec((pl.Element(tb), pl.Element(SHARD_PAD)),
                               lambda p, i, win: (i * tb, pl.multiple_of(win[p] * LANE, LANE))),
                  pl.BlockSpec(memory_space=pl.ANY)] + [whole(s.shape) for s in small],
        out_specs=(whole((D_MODEL, SHARD_PAD)), whole((SHARD_OUT, D_MODEL)), whole((PACK_ROWS, D_MODEL))),
        scratch_shapes=[
            pltpu.VMEM((2, D_MODEL, SHARD_PAD), F32), pltpu.VMEM((2, half, SHARD_PAD), F32),
            pltpu.VMEM((N_CHIPS - 1, half, SHARD_PAD), BF16), pltpu.VMEM((N_CHIPS, half, SHARD_PAD), BF16),
            pltpu.VMEM((D_MODEL, SHARD_PAD), F32),
        ] + _small_exchange_scratch() + [
            pltpu.SemaphoreType.DMA((N_SEMS + SMALL_SEMS,)), pltpu.SemaphoreType.DMA((N_SEMS + SMALL_SEMS,)),
            pltpu.SemaphoreType.DMA((N_CHIPS,)),
        ])
    chip = 2 * lax.axis_index("x") + lax.axis_index("y")
    owner = (chip + 1 + jnp.arange(N_CHIPS, dtype=jnp.int32)) % N_CHIPS
    win_start = (owner * SHARD_IN) // LANE
    return _call(
        body, name="gw_in_reduce", grid_spec=grid_spec,
        out_shape=(jax.ShapeDtypeStruct((D_MODEL, SHARD_PAD), F32), jax.ShapeDtypeStruct((SHARD_OUT, D_MODEL), F32),
                   jax.ShapeDtypeStruct((PACK_ROWS, D_MODEL), F32)),
        compiler_params=_params(52, ("arbitrary", "arbitrary")),
    )(win_start.astype(jnp.int32), h_t, dproj, gw_out, *small)


SMALL_SEMS = 17


def _small_exchange(gwo_hbm, small_refs, fout_ref, mine_out, sib_out, send_out, b_out, pack_ref, packs,
                    send_sems, recv_sems, local_sems, base):
    half_out = SHARD_OUT // 2
    A_OUT, B_OUT, C_OUT, PACK = base, base + 4, base + 8, base + 9
    assert len(small_refs) == PACK_PARTS
    x, y, c = lax.axis_index("x"), lax.axis_index("y"), lax.axis_index("c")
    chip = 2 * x + y
    dev = 2 * chip + c
    sibling = (x, y, 1 - c)

    def remote(src, dst, send_k, recv_k, to):
        return pltpu.make_async_remote_copy(src_ref=src, dst_ref=dst, send_sem=send_sems.at[send_k],
                                            recv_sem=recv_sems.at[recv_k], device_id=to, device_id_type=MESH_ID)

    def out_rows(j, core):
        return pl.ds(pl.multiple_of(j * SHARD_OUT + core * half_out, half_out), half_out)

    my_out_rows = pl.ds(pl.multiple_of(c * half_out, half_out), half_out)
    local = [pltpu.make_async_copy(gwo_hbm.at[out_rows(j, c), :], mine_out.at[j], local_sems.at[j])
             for j in range(N_CHIPS)]
    stage_a = [remote(gwo_hbm.at[out_rows(j, 1 - c), :], sib_out.at[j], A_OUT + j, A_OUT + j, sibling)
               for j in range(N_CHIPS)]
    mine_half_out = fout_ref.at[my_out_rows, :]
    stage_c = [remote(mine_half_out, mine_half_out, C_OUT, C_OUT, sibling)]

    def start():
        pack_ref[...] = jnp.zeros_like(pack_ref)
        for k, s_ref in enumerate(small_refs):
            pack_ref[k:k + 1, 0:s_ref.shape[1]] = s_ref[0:1, :]
        packs[dev] = pack_ref[...]
        for d in range(N_DEV):
            to = (d // 4, (d // 2) % 2, d % 2)

            @pl.when(d != dev)
            def _():
                remote(pack_ref, packs.at[dev], PACK + d, PACK + dev, to).start()

        for cp in local + stage_a:
            cp.start()

    def hand_on():
        for cp in local:
            cp.wait()
        for cp in stage_a:
            cp.wait_recv()
        for j in range(N_CHIPS):
            mine_out[j] = mine_out[j] + sib_out[j]

        for j in range(N_CHIPS):
            to = (j // 2, j % 2, c)

            @pl.when(j != chip)
            def _():
                send_out[j] = mine_out[j].astype(BF16)
                remote(send_out.at[j], b_out.at[chip], B_OUT + j, B_OUT + chip, to).start()

            @pl.when(j == chip)
            def _():
                fout_ref[my_out_rows, :] = mine_out[j]

    def finish():
        for j in range(N_CHIPS):
            @pl.when(j != chip)
            def _():
                remote(b_out.at[j], b_out.at[j], B_OUT + j, B_OUT + j, sibling).wait_recv()
                fout_ref[my_out_rows, :] += b_out[j].astype(F32)

        for cp in stage_c:
            cp.start()
        other_half_out = fout_ref.at[pl.ds(pl.multiple_of((1 - c) * half_out, half_out), half_out), :]
        remote(other_half_out, other_half_out, C_OUT, C_OUT, sibling).wait_recv()

        for d in range(N_DEV):
            @pl.when(d != dev)
            def _():
                remote(pack_ref, packs.at[d], PACK + d, PACK + d, sibling).wait_recv()
        total = packs[0]
        for d in range(1, N_DEV):
            total = total + packs[d]

        for cp in stage_a + stage_c:
            cp.wait_send()
        for j in range(N_CHIPS):
            @pl.when(j != chip)
            def _():
                remote(b_out.at[j], b_out.at[j], B_OUT + j, B_OUT + j, sibling).wait_send()
        for d in range(N_DEV):
            @pl.when(d != dev)
            def _():
                remote(pack_ref, packs.at[d], PACK + d, PACK + d, sibling).wait_send()
        return total

    return start, hand_on, finish


def _small_exchange_scratch():
    half_out = SHARD_OUT // 2
    return [
        pltpu.VMEM((SHARD_OUT, D_MODEL), F32),
        pltpu.VMEM((N_CHIPS, half_out, D_MODEL), F32), pltpu.VMEM((N_CHIPS, half_out, D_MODEL), F32),
        pltpu.VMEM((N_CHIPS, half_out, D_MODEL), BF16), pltpu.VMEM((N_CHIPS, half_out, D_MODEL), BF16),
        pltpu.VMEM((PACK_ROWS, D_MODEL), F32), pltpu.VMEM((N_DEV, PACK_ROWS, D_MODEL), F32),
    ]


def _adam_math(w, g, m, v):
    mn = ADAM_B1 * m + (1.0 - ADAM_B1) * g
    vn = ADAM_B2 * v + (1.0 - ADAM_B2) * (g * g)
    m_hat = mn / (1.0 - ADAM_B1 ** ADAM_STEP)
    v_hat = vn / (1.0 - ADAM_B2 ** ADAM_STEP)
    return -ADAM_LR * (m_hat / (jnp.sqrt(v_hat) + ADAM_EPS) + ADAM_WD * w), mn, vn


def _adamw(name, w, g, m, v, tb):
    rows, cols = w.shape

    def body(w_ref, g_ref, m_ref, v_ref, go_ref, d_ref, mo_ref, vo_ref):
        gv = g_ref[...]
        go_ref[...] = gv
        d_ref[...], mo_ref[...], vo_ref[...] = _adam_math(w_ref[...], gv, m_ref[...], v_ref[...])

    spec = pl.BlockSpec((tb, cols), lambda i: (i, 0))
    shape = jax.ShapeDtypeStruct((rows, cols), F32)
    return _call(
        body, name=name, grid=(rows // tb,), out_shape=(shape,) * 4,
        in_specs=[spec] * 4, out_specs=(spec,) * 4,
        compiler_params=_params(32, ("arbitrary",)),
    )(w, g, m, v)


def _adamw_w_in(w_t, g_window, m_t, v_t, tb):
    def body(w_ref, g_ref, m_ref, v_ref, go_ref, d_ref, mo_ref, vo_ref, gt_ref):
        gt_ref[...] = g_ref[...].T
        gv = gt_ref[pl.ds(pl.multiple_of(lax.axis_index("y") * SHARD_SHIFT, SHARD_SHIFT), SHARD_IN), :]
        go_ref[...] = gv
        d_ref[...], mo_ref[...], vo_ref[...] = _adam_math(w_ref[...], gv, m_ref[...], v_ref[...])

    spec = pl.BlockSpec((SHARD_IN, tb), lambda i: (0, i))
    shape = jax.ShapeDtypeStruct((SHARD_IN, D_MODEL), F32)
    return _call(
        body, name="adamw_w_in", grid=(D_MODEL // tb,), out_shape=(shape,) * 4,
        in_specs=[spec, pl.BlockSpec((tb, SHARD_PAD), lambda i: (i, 0)), spec, spec], out_specs=(spec,) * 4,
        scratch_shapes=[pltpu.VMEM((SHARD_PAD, tb), F32)],
        compiler_params=_params(32, ("arbitrary",)),
    )(w_t, g_window, m_t, v_t)


def _adamw_small(sums, params):
    def body(sums_ref, *refs):
        ins, outs = refs[:3 * len(params)], refs[3 * len(params):]
        for k in range(len(params)):
            w_ref, m_ref, v_ref = ins[3 * k:3 * k + 3]
            g = sums_ref[k:k + 1, 0:w_ref.shape[1]]
            go_ref, d_ref, mo_ref, vo_ref = outs[4 * k:4 * k + 4]
            go_ref[...] = g
            d_ref[...], mo_ref[...], vo_ref[...] = _adam_math(w_ref[...], g, m_ref[...], v_ref[...])

    vmem = pl.BlockSpec(memory_space=pltpu.VMEM)
    flat = [a for p in params for a in p]
    shapes = tuple(jax.ShapeDtypeStruct(p[0].shape, F32) for p in params for _ in range(4))
    res = _call(body, name="adamw_small", out_shape=shapes, in_specs=[vmem] * (1 + len(flat)),
                out_specs=(vmem,) * len(shapes), compiler_params=_params(16))(sums, *flat)
    return [res[4 * k:4 * k + 4] for k in range(len(params))]


def kernel(x, norm_g, w_in, att_sinks, ret_gn_g, w_out, final_g, loss_target, m_norm_g, m_w_in, m_att_sinks, m_ret_gn_g, m_w_out, m_final_g, v_norm_g, v_w_in, v_att_sinks, v_ret_gn_g, v_w_out, v_final_g):
    seq = x.shape[1]
    xs, tgt = x[0], loss_target[0]
    final_g2 = final_g.reshape(1, D_MODEL)
    tables = _tables(seq)

    w_in_t, m_w_in_t, v_w_in_t = w_in[0].T, m_w_in[0].T, v_w_in[0].T
    w_in_full = _w_in_allgather(w_in_t)
    proj, h_t, w_out_full = _in_proj(xs, norm_g, w_in_full, w_out[0], min(TOKENS_PROJ, seq))
    mix, dxo, states, loss_part, gfin, o_att = _mix_fwd(proj, xs, tgt, w_out_full, final_g2, ret_gn_g, att_sinks,
                                                        tables, min(TOKENS_MIX, seq))
    dproj, grad_x, gw_out, gnorm, dgain, dsink = _mix_bwd(proj, dxo, mix, o_att, states, xs, w_out_full, w_in_full,
                                                          norm_g, ret_gn_g, att_sinks, tables, min(TOKENS_MIX, seq))
    g_in, g_out, sums = _gw_in_reduce(h_t, dproj, gw_out, (gnorm, gfin, dgain, dsink, loss_part),
                                      min(TOKENS_GW, seq))

    res_in = [r.T for r in _adamw_w_in(w_in_t, g_in, m_w_in_t, v_w_in_t, 256)]
    res_out = _adamw("adamw_w_out", w_out[0], g_out, m_w_out[0], v_w_out[0], SHARD_OUT)
    as_row = lambda a: a.reshape(1, D_MODEL)
    r_norm, r_final, r_gain, r_sink = _adamw_small(sums, [
        (norm_g, m_norm_g, v_norm_g), (final_g2, as_row(m_final_g), as_row(v_final_g)),
        (ret_gn_g, m_ret_gn_g, v_ret_gn_g), (att_sinks, m_att_sinks, v_att_sinks)])

    outs = []
    for k in range(4):
        outs += [r_norm[k], res_in[k][None], r_sink[k], r_gain[k], res_out[k][None], r_final[k].reshape(D_MODEL)]
    return (sums[4, 0], grad_x[None], *outs)
```

```python
import jax
import jax.numpy as jnp
import numpy as np
from jax import lax
from jax.experimental import pallas as pl
from jax.experimental.pallas import tpu as pltpu

F32 = jnp.float32
BF16 = jnp.bfloat16

D_MODEL = 1024
ATT_HEADS = 8
ATT_KV_HEADS = 2
ATT_HEAD_DIM = 64
RET_HEADS = 4
RET_QK_DIM = 64
RET_V_DIM = 128
BLK = 128
ROT_BASE = 10000.0
RMS_EPS = 1e-6
GN_EPS = 1e-6
NEG_INF = -1e30
ATT_SCALE = ATT_HEAD_DIM ** -0.5
RET_SCALE = RET_QK_DIM ** -0.5

ATT_WIDTH = ATT_HEADS * ATT_HEAD_DIM
ATT_KV_WIDTH = ATT_KV_HEADS * ATT_HEAD_DIM
RET_QK_WIDTH = RET_HEADS * RET_QK_DIM
RET_WIDTH = RET_HEADS * RET_V_DIM
MIX_WIDTH = ATT_WIDTH + RET_WIDTH
OFF_AQ = 0
OFF_AK = OFF_AQ + ATT_WIDTH
OFF_AV = OFF_AK + ATT_KV_WIDTH
OFF_AZ = OFF_AV + ATT_KV_WIDTH
OFF_RQ = OFF_AZ + ATT_WIDTH
OFF_RK = OFF_RQ + RET_QK_WIDTH
OFF_RV = OFF_RK + RET_QK_WIDTH
OFF_RZ = OFF_RV + RET_WIDTH
IN_WIDTH = OFF_RZ + RET_WIDTH

LANE = 128
BF16_ROWS = 16
HALF_LANE = LANE // 2
PAIRS = RET_QK_WIDTH // LANE
assert ATT_HEAD_DIM == HALF_LANE and RET_QK_DIM == HALF_LANE and RET_V_DIM == LANE and ATT_KV_WIDTH == LANE

N_CHIPS = 4
N_DEV = 8
SHARD_IN = IN_WIDTH // N_CHIPS
SHARD_PAD = 768
SHARD_SHIFT = SHARD_PAD - SHARD_IN
WIN_START = tuple((j * SHARD_IN) // LANE * LANE for j in range(N_CHIPS))
SHARD_OUT = MIX_WIDTH // N_CHIPS

PACK_PARTS = 5
PACK_ROWS = 8
assert PACK_PARTS <= PACK_ROWS

ADAM_LR = 0.001
ADAM_B1 = 0.9
ADAM_B2 = 0.999
ADAM_EPS = 1e-08
ADAM_WD = 0.01
ADAM_STEP = 10

VMEM_CAP = 64 * 1024 * 1024
TOKENS_PROJ = 1024
TOKENS_MIX = 512
TOKENS_GW = 2048
MESH_ID = pl.DeviceIdType.MESH


def _call(body, **kw):
    return pl.pallas_call(body, **kw)


def _params(vmem_mb, semantics=None):
    assert vmem_mb * 1024 * 1024 < VMEM_CAP
    return pltpu.CompilerParams(dimension_semantics=semantics, vmem_limit_bytes=vmem_mb * 1024 * 1024)


def _dot(a, b):
    return jnp.dot(a, b, preferred_element_type=F32)


def _dot_nt(a, b):
    return lax.dot_general(a, b, (((1,), (1,)), ((), ())), preferred_element_type=F32)


def _dot_tn(a, b):
    return lax.dot_general(a, b, (((0,), (0,)), ((), ())), preferred_element_type=F32)


def _sigmoid(z):
    return 1.0 / (1.0 + jnp.exp(-z))


def _const_spec(shape):
    nd = len(shape)
    return pl.BlockSpec(shape, lambda i: (0,) * nd, pipeline_mode=pl.Buffered(1))


def _tables(seq):
    f32 = np.float32
    pos = np.arange(seq, dtype=f32)
    theta = (f32(1.0) / (f32(ROT_BASE) ** np.linspace(0.0, 1.0, RET_QK_DIM // 2, dtype=f32))).astype(f32)
    ang = (pos[:, None] * theta[None, :]).astype(f32)
    cos, sin = np.cos(ang), np.sin(ang)
    cos2 = np.repeat(cos, 2, axis=1)
    sin2 = np.stack([-sin, sin], axis=-1).reshape(seq, RET_QK_DIM)
    cos_t = np.tile(cos2, (1, 2))
    sin_t = np.tile(sin2, (1, 2))

    log_gamma = np.log(f32(1.0) - f32(2.0) ** (f32(-5.0) - np.arange(RET_HEADS, dtype=f32))).astype(f32)
    idx = np.arange(BLK, dtype=f32)
    rel = idx[:, None] - idx[None, :]
    decay_in = np.where(rel >= 0, np.exp(log_gamma[:, None, None] * np.maximum(rel, f32(0.0))), f32(0.0))
    k_dec = np.exp(log_gamma[:, None] * (BLK - 1 - idx)[None, :])
    q_dec = np.exp(log_gamma[:, None] * (idx + 1)[None, :])
    chunk_decay = np.exp(log_gamma * f32(BLK))
    kdec_t = np.repeat(k_dec.T, RET_QK_DIM, axis=1)
    qdec_t = np.repeat(q_dec.T, RET_QK_DIM, axis=1)
    cd_t = np.broadcast_to(chunk_decay[:, None, None], (RET_HEADS, RET_QK_DIM, RET_V_DIM))
    decay_in = decay_in.reshape(PAIRS, 2 * BLK, BLK)
    cd_t = cd_t.reshape(PAIRS, 2 * RET_QK_DIM, RET_V_DIM)

    key = np.arange(BLK)[:, None]
    query = np.arange(2 * BLK)[None, :] % BLK
    bias = np.stack([np.where(key > query, NEG_INF, 0.0), np.zeros((BLK, 2 * BLK))])
    return tuple(jnp.asarray(np.ascontiguousarray(a), F32) for a in (cos_t, sin_t, decay_in, qdec_t, kdec_t, cd_t, bias))


def _low_lanes(shape):
    lane = lax.broadcasted_iota(jnp.int32, shape, len(shape) - 1)
    return (lane & HALF_LANE) == 0


def _split_heads(t):
    low = _low_lanes(t.shape)
    zero = jnp.zeros_like(t)
    return jnp.where(low, t, zero), jnp.where(low, zero, t)


def _swap_pairs(t):
    lane = lax.broadcasted_iota(jnp.int32, t.shape, 1)
    nxt = pltpu.roll(t, t.shape[1] - 1, 1)
    prv = pltpu.roll(t, 1, 1)
    return jnp.where((lane & 1) == 0, nxt, prv)


def _per_tile(fn, t):
    return jnp.concatenate([fn(_tile(t, i)) for i in range(t.shape[1] // LANE)], axis=1)


def _rotate(t, cos_t, sin_t):
    return _per_tile(lambda a: a * cos_t + _swap_pairs(a) * sin_t, t)


def _rotate_transposed(d, cos_t, sin_t):
    return _per_tile(lambda a: a * cos_t + _swap_pairs(a * sin_t), d)


def _kv_operands(cat):
    low = _low_lanes(cat.shape)
    swapped = pltpu.roll(cat, HALF_LANE, 1)
    zero = jnp.zeros_like(cat)
    pick = lambda a, b: jnp.where(low, a, b).astype(BF16)
    return ((pick(cat, zero), pick(zero, swapped)), (pick(swapped, zero), pick(zero, cat)))


def _stack_tiles(t, first_tile):
    a = t[:, first_tile * LANE:(first_tile + 1) * LANE]
    b = t[:, (first_tile + 1) * LANE:(first_tile + 2) * LANE]
    return jnp.concatenate([a, b], axis=0)


def _sink_rows(sinks_ref, group):
    first = lax.broadcasted_iota(jnp.int32, (1, 2 * BLK), 1) < BLK

    def row(h0, h1):
        return jnp.where(first, sinks_ref[0, group * 4 + h0], sinks_ref[0, group * 4 + h1])
    return row(0, 2), row(1, 3)


ATT_PROBLEMS = tuple((g, hi) for g in range(ATT_KV_HEADS) for hi in range(2))


def _in_previous_block():
    key = lax.broadcasted_iota(jnp.int32, (BLK, 2 * BLK), 0)
    query = lax.broadcasted_iota(jnp.int32, (BLK, 2 * BLK), 1) & (BLK - 1)
    return key > query


def _fold(t, prev):
    return jnp.where(prev, t[0:BLK], t[BLK:])


def _unfold(t, prev):
    zero = jnp.zeros_like(t)
    return jnp.concatenate([jnp.where(prev, t, zero), jnp.where(prev, zero, t)], axis=0)


def _attn_probs(qs, k_ops, bias, prev, sinks_ref):
    sink = [_sink_rows(sinks_ref, g)[hi] for g, hi in ATT_PROBLEMS]
    s = [_fold(_dot_nt(k_ops[g][hi], qs[g]), prev) + bias for g, hi in ATT_PROBLEMS]
    m = [jnp.maximum(jnp.max(si, axis=0, keepdims=True), ki) for si, ki in zip(s, sink)]
    e = [jnp.exp(si - mi) for si, mi in zip(s, m)]
    es = [jnp.exp(ki - mi) for ki, mi in zip(sink, m)]
    inv = [1.0 / (jnp.sum(ei, axis=0, keepdims=True) + esi) for ei, esi in zip(e, es)]
    return [ei * ii for ei, ii in zip(e, inv)], [esi * ii for esi, ii in zip(es, inv)]


def _group_norm_all(outs):
    mu = [jnp.mean(o, axis=1, keepdims=True) for o in outs]
    xc = [o - m for o, m in zip(outs, mu)]
    var = [jnp.mean(c * c, axis=1, keepdims=True) for c in xc]
    rstd = [lax.rsqrt(v + GN_EPS) for v in var]
    return [c * r for c, r in zip(xc, rstd)], rstd


def _proj_cols(p_ref, rows, start, stop):
    pieces = []
    while start < stop:
        k = max(j for j in range(N_CHIPS) if WIN_START[j] <= start)
        end = min(stop, WIN_START[k] + SHARD_PAD)
        pieces.append(p_ref[k, rows, start - WIN_START[k]:end - WIN_START[k]])
        start = end
    return pieces[0] if len(pieces) == 1 else jnp.concatenate(pieces, axis=1)


def _retention_operands(p_ref, rows, cos_b, sin_b, qdec, kdec):
    qr = _rotate(_proj_cols(p_ref, rows, OFF_RQ, OFF_RQ + RET_QK_WIDTH).astype(F32), cos_b, sin_b)
    kr = _rotate(_proj_cols(p_ref, rows, OFF_RK, OFF_RK + RET_QK_WIDTH).astype(F32), cos_b, sin_b) * RET_SCALE
    return qr, kr, qr * qdec, kr * kdec


def _tile(t, i):
    return t[:, i * LANE:(i + 1) * LANE]


GATHER_SEMS = 7


def _allgather_steps(src_ref, full_ref, blk_ref, send_sems, recv_sems):
    block_rows = blk_ref.shape[0]
    assert block_rows % BF16_ROWS == 0 and full_ref.shape[0] == N_DEV * block_rows
    x, y, c = lax.axis_index("x"), lax.axis_index("y"), lax.axis_index("c")
    me, sibling = (x, y, c), (x, y, 1 - c)
    chips = [(1 - x, y), (x, 1 - y), (1 - x, 1 - y)]

    def rows(px, py, pc):
        return full_ref.at[pl.ds(pl.multiple_of((4 * px + 2 * py + pc) * block_rows, BF16_ROWS), block_rows), :]

    def copy(k, block, to, src=None):
        return pltpu.make_async_remote_copy(
            src_ref=rows(*block) if src is None else src, dst_ref=rows(*block),
            send_sem=send_sems.at[k], recv_sem=recv_sems.at[k], device_id=to, device_id_type=MESH_ID)

    first = [copy(0, me, sibling, src=blk_ref)] + [copy(1 + j, me, (*chip, c), src=blk_ref) for j, chip in enumerate(chips)]
    passed = [copy(4 + j, (*chip, c), sibling) for j, chip in enumerate(chips)]

    def start():
        blk_ref[...] = src_ref[pl.ds(pl.multiple_of(c * block_rows, BF16_ROWS), block_rows), :].astype(BF16)
        rows(*me)[...] = blk_ref[...]
        for cp in first:
            cp.start()

    def forward():
        for j, chip in enumerate(chips):
            copy(1 + j, (*chip, c), me).wait_recv()
            passed[j].start()

    def finish():
        copy(0, sibling, me).wait_recv()
        for j, chip in enumerate(chips):
            copy(4 + j, (*chip, 1 - c), me).wait_recv()
        for cp in first + passed:
            cp.wait_send()

    return start, forward, finish


WINDOW_HALF = SHARD_PAD // 2
WINDOW_SEMS = GATHER_SEMS + 1


def _window_gather(wi_ref, w4_ref, blk_ref, edge_ref, send_sems, recv_sems):
    x, y, c = lax.axis_index("x"), lax.axis_index("y"), lax.axis_index("c")
    me, sibling, pair = (x, y, c), (x, y, 1 - c), (x, 1 - y, c)
    chips = [(1 - x, y), (x, 1 - y), (1 - x, 1 - y)]
    own_rows, edge_rows = SHARD_IN - WINDOW_HALF, SHARD_SHIFT
    sends_edge = y == c

    def rows(px, py, pc):
        return w4_ref.at[pl.ds(pl.multiple_of((4 * px + 2 * py + pc) * WINDOW_HALF, BF16_ROWS), WINDOW_HALF), :]

    def copy(k, block, to, src=None):
        return pltpu.make_async_remote_copy(
            src_ref=rows(*block) if src is None else src, dst_ref=rows(*block),
            send_sem=send_sems.at[k], recv_sem=recv_sems.at[k], device_id=to, device_id_type=MESH_ID)

    edge_copy = pltpu.make_async_remote_copy(
        src_ref=edge_ref, dst_ref=edge_ref, send_sem=send_sems.at[GATHER_SEMS], recv_sem=recv_sems.at[GATHER_SEMS],
        device_id=pair, device_id_type=MESH_ID)
    first = [copy(0, me, sibling, src=blk_ref)] + [copy(1 + j, me, (*chip, c), src=blk_ref) for j, chip in enumerate(chips)]
    passed = [copy(4 + j, (*chip, c), sibling) for j, chip in enumerate(chips)]

    def at(start, size):
        return pl.ds(pl.multiple_of(start, BF16_ROWS), size)

    def start():
        @pl.when(sends_edge)
        def _():
            edge_ref[...] = wi_ref[at((1 - y) * (SHARD_IN - edge_rows), edge_rows), :].astype(BF16)
            edge_copy.start()
            blk_ref[...] = wi_ref[at(c * own_rows, WINDOW_HALF), :].astype(BF16)

        @pl.when(jnp.logical_not(sends_edge))
        def _():
            blk_ref[at(y * edge_rows, own_rows), :] = wi_ref[at(c * WINDOW_HALF, own_rows), :].astype(BF16)
            edge_copy.wait_recv()
            blk_ref[at((1 - y) * own_rows, edge_rows), :] = edge_ref[...]

        rows(*me)[...] = blk_ref[...]
        for cp in first:
            cp.start()
        copy(0, sibling, me).wait_recv()

    def forward(j):
        copy(1 + j, (*chips[j], c), me).wait_recv()
        passed[j].start()

    def wait_forwarded(j):
        copy(4 + j, (*chips[j], 1 - c), me).wait_recv()

    def finish():
        for cp in first + passed:
            cp.wait_send()

        @pl.when(sends_edge)
        def _():
            edge_copy.wait_send()

    return start, forward, wait_forwarded, finish


def _in_proj(x, norm_g, w_in_t_shard, w_out_shard, tb):
    seq = x.shape[0]
    nblk = seq // tb
    last = nblk - 1
    chip_of_panel = (None, 1, 0, 2)

    def body(win_ref, x_ref, g_ref, wi_ref, wo_ref, p_ref, ht_ref, wt_ref, wout_ref,
             w4, blk, edge, hbuf, wout_full, blko, send_sems, recv_sems, send_sems_o, recv_sems_o):
        q, i = pl.program_id(0), pl.program_id(1)
        chip = 2 * lax.axis_index("x") + lax.axis_index("y")
        in_start, in_forward, in_wait_forwarded, in_finish = _window_gather(wi_ref, w4, blk, edge, send_sems, recv_sems)
        out_start, out_forward, out_finish = _allgather_steps(wo_ref, wout_full, blko, send_sems_o, recv_sems_o)
        rows = pl.ds(pl.multiple_of(i * tb, tb), tb)

        @pl.when(jnp.logical_and(q == 0, i == 0))
        def _():
            in_start()
            out_start()

        for panel in range(1, N_CHIPS):
            @pl.when(jnp.logical_and(q == panel, i == 0))
            def _():
                in_wait_forwarded(chip_of_panel[panel])

        @pl.when(q == 0)
        def _():
            xv = x_ref[...]
            r = lax.rsqrt(jnp.mean(xv * xv, axis=1, keepdims=True) + RMS_EPS)
            h = (xv * r) * g_ref[...]
            hbuf[rows, :] = h.astype(BF16)
            ht_ref[...] = h.T.astype(BF16)

        owner = jnp.bitwise_xor(chip, q)
        window = w4[pl.ds(pl.multiple_of(owner * SHARD_PAD, SHARD_PAD), SHARD_PAD), :]
        p_ref[...] = _dot_nt(hbuf[rows, :], window).astype(BF16)

        for panel in range(1, N_CHIPS):
            @pl.when(jnp.logical_and(q == panel - 1, i == last))
            def _():
                in_forward(chip_of_panel[panel])

        @pl.when(jnp.logical_and(q == 0, i == min(1, last)))
        def _():
            out_forward()

        @pl.when(jnp.logical_and(q == N_CHIPS - 1, i == last))
        def _():
            in_finish()
            for k in range(N_CHIPS):
                first = k * SHARD_PAD + (k % 2) * SHARD_SHIFT
                wt_ref[k * SHARD_IN:(k + 1) * SHARD_IN, :] = w4[first:first + SHARD_IN, :]
            out_finish()
            wout_ref[...] = wout_full[...]

    whole = lambda shape: pl.BlockSpec(shape, lambda q, i, win: (0,) * len(shape), pipeline_mode=pl.Buffered(1))
    first_panel = lambda q, i: jnp.where(q == 0, i, last)
    grid_spec = pltpu.PrefetchScalarGridSpec(
        num_scalar_prefetch=1, grid=(N_CHIPS, nblk),
        in_specs=[pl.BlockSpec((tb, D_MODEL), lambda q, i, win: (first_panel(q, i), 0)), whole((1, D_MODEL)),
                  whole((SHARD_IN, D_MODEL)), whole((SHARD_OUT, D_MODEL))],
        out_specs=(pl.BlockSpec((None, tb, SHARD_PAD), lambda q, i, win: (win[q], i, 0)),
                   pl.BlockSpec((D_MODEL, tb), lambda q, i, win: (0, first_panel(q, i))),
                   whole((IN_WIDTH, D_MODEL)), whole((MIX_WIDTH, D_MODEL))),
        scratch_shapes=[
            pltpu.VMEM((N_CHIPS * SHARD_PAD, D_MODEL), BF16), pltpu.VMEM((WINDOW_HALF, D_MODEL), BF16),
            pltpu.VMEM((SHARD_SHIFT, D_MODEL), BF16), pltpu.VMEM((seq, D_MODEL), BF16),
            pltpu.VMEM((MIX_WIDTH, D_MODEL), BF16), pltpu.VMEM((SHARD_OUT // 2, D_MODEL), BF16),
            pltpu.SemaphoreType.DMA((WINDOW_SEMS,)), pltpu.SemaphoreType.DMA((WINDOW_SEMS,)),
            pltpu.SemaphoreType.DMA((GATHER_SEMS,)), pltpu.SemaphoreType.DMA((GATHER_SEMS,)),
        ])
    chip = 2 * lax.axis_index("x") + lax.axis_index("y")
    owner = jnp.bitwise_xor(chip, jnp.arange(N_CHIPS, dtype=jnp.int32))
    return _call(
        body, name="in_proj", grid_spec=grid_spec,
        out_shape=(jax.ShapeDtypeStruct((N_CHIPS, seq, SHARD_PAD), BF16), jax.ShapeDtypeStruct((D_MODEL, seq), BF16),
                   jax.ShapeDtypeStruct((IN_WIDTH, D_MODEL), BF16), jax.ShapeDtypeStruct((MIX_WIDTH, D_MODEL), BF16)),
        compiler_params=_params(60, ("arbitrary", "arbitrary")),
    )(owner.astype(jnp.int32), x, norm_g, w_in_t_shard, w_out_shard)


def _mix_fwd(proj, x, target, w_out, final_g, gn_gain, sinks, tables, tb):
    seq = x.shape[0]
    nsub = tb // BLK
    cos_t, sin_t, decay_in, qdec_t, kdec_t, cd_t, bias_t = tables

    def body(p_ref, x_ref, t_ref, cos_ref, sin_ref, wout_ref, fg_ref, gain_ref, sinks_ref, din_ref, qdec_ref,
             kdec_ref, cd_ref, bias_ref, mix_ref, dxo_ref, st_ref, loss_ref, gfin_ref, oatt_ref,
             kprev_ref, vprev_ref, state_ref):
        i = pl.program_id(0)

        @pl.when(i == 0)
        def _():
            kprev_ref[...] = jnp.zeros_like(kprev_ref)
            vprev_ref[...] = jnp.zeros_like(vprev_ref)
            state_ref[...] = jnp.zeros_like(state_ref)
            loss_ref[...] = jnp.zeros_like(loss_ref)
            gfin_ref[...] = jnp.zeros_like(gfin_ref)

        prev = _in_previous_block()

        def sub(j, carry):
            kp, vp, states = carry
            rows = pl.ds(pl.multiple_of(j * BLK, BLK), BLK)
            bias = bias_ref[jnp.where(jnp.logical_or(i > 0, j > 0), 1, 0)]

            aq = _proj_cols(p_ref, rows, OFF_AQ, OFF_AQ + ATT_WIDTH)
            ak = _proj_cols(p_ref, rows, OFF_AK, OFF_AK + ATT_KV_WIDTH).astype(F32)
            av = _proj_cols(p_ref, rows, OFF_AV, OFF_AV + ATT_KV_WIDTH).astype(F32)
            az = _proj_cols(p_ref, rows, OFF_AZ, OFF_AZ + ATT_WIDTH).astype(F32)
            k_ops = _kv_operands(jnp.concatenate([kp, ak], axis=0))
            v_ops = _kv_operands(jnp.concatenate([vp, av], axis=0))
            qs = [(_stack_tiles(aq, 2 * g) * ATT_SCALE).astype(BF16) for g in range(ATT_KV_HEADS)]
            p, _ = _attn_probs(qs, k_ops, bias, prev, sinks_ref)
            o_tiles = []
            for g in range(ATT_KV_HEADS):
                p_cat = jnp.concatenate([_unfold(p[2 * g].astype(BF16), prev), _unfold(p[2 * g + 1].astype(BF16), prev)],
                                        axis=0)
                o = _dot_tn(p_cat, jnp.concatenate(v_ops[g], axis=0))
                o_tiles += [o[0:BLK], o[BLK:]]
            o_att = jnp.concatenate(o_tiles, axis=1)
            oatt_ref[rows, :] = o_att
            out = [o_att * (az * _sigmoid(az))]

            qr, kr, qd, kd = _retention_operands(p_ref, rows, cos_ref[rows, :], sin_ref[rows, :],
                                                 qdec_ref[...], kdec_ref[...])
            heads = [(t, hh) for t in range(PAIRS) for hh in range(2)]
            sc = [_dot_nt(jnp.concatenate(_split_heads(_tile(qr, t)), axis=0).astype(BF16), _tile(kr, t).astype(BF16))
                  * din_ref[t] for t in range(PAIRS)]
            qd_heads = [_split_heads(_tile(qd, t)) for t in range(PAIRS)]
            state_b = [states[t].astype(BF16) for t in range(PAIRS)]
            vs = [_proj_cols(p_ref, rows, OFF_RV + h * RET_V_DIM, OFF_RV + (h + 1) * RET_V_DIM) for h in range(RET_HEADS)]
            rzs = [_proj_cols(p_ref, rows, OFF_RZ + h * RET_V_DIM, OFF_RZ + (h + 1) * RET_V_DIM).astype(F32)
                   for h in range(RET_HEADS)]
            lhs = [jnp.concatenate([sc[t][hh * BLK:(hh + 1) * BLK].astype(BF16), qd_heads[t][hh].astype(BF16)], axis=1)
                   for t, hh in heads]
            ons, _ = _group_norm_all([_dot(lhs[2 * t + hh], jnp.concatenate([vs[2 * t + hh], state_b[t]], axis=0))
                                      for t, hh in heads])
            out += [(ons[h] * gain_ref[:, h * RET_V_DIM:(h + 1) * RET_V_DIM]) * (rzs[h] * _sigmoid(rzs[h]))
                    for h in range(RET_HEADS)]
            new_states = [states[t] * cd_ref[t]
                          + _dot_tn(jnp.concatenate(_split_heads(_tile(kd, t)), axis=0).astype(BF16),
                                    jnp.concatenate([vs[2 * t], vs[2 * t + 1]], axis=0)) for t in range(PAIRS)]
            mix_ref[rows, :] = jnp.concatenate(out, axis=1).astype(BF16)
            st_ref[j] = jnp.stack(states)
            return ak, av, tuple(new_states)

        carry = (kprev_ref[...], vprev_ref[...], tuple(state_ref[t] for t in range(PAIRS)))
        kp, vp, states = lax.fori_loop(0, nsub, sub, carry)
        kprev_ref[...] = kp
        vprev_ref[...] = vp
        state_ref[...] = jnp.stack(states)

        xo = x_ref[...] + _dot(mix_ref[...], wout_ref[...])
        r2 = lax.rsqrt(jnp.mean(xo * xo, axis=1, keepdims=True) + RMS_EPS)
        xn = xo * r2
        err = xn * fg_ref[...] - t_ref[...]
        loss_ref[...] += jnp.sum(err * err) * (0.5 / D_MODEL)
        dy = err * (1.0 / D_MODEL)
        gfin_ref[...] += jnp.sum(dy * xn, axis=0, keepdims=True)
        u = dy * fg_ref[...]
        dxo_ref[...] = r2 * u - xn * (r2 * jnp.mean(u * xn, axis=1, keepdims=True))

    blk_rows = lambda w: pl.BlockSpec((tb, w), lambda i: (i, 0))
    state_shape = (PAIRS, 2 * RET_QK_DIM, RET_V_DIM)
    return _call(
        body, name="mix_fwd", grid=(seq // tb,),
        out_shape=(
            jax.ShapeDtypeStruct((seq, MIX_WIDTH), BF16),
            jax.ShapeDtypeStruct((seq, D_MODEL), F32),
            jax.ShapeDtypeStruct((seq // BLK,) + state_shape, F32),
            jax.ShapeDtypeStruct((8, LANE), F32),
            jax.ShapeDtypeStruct((1, D_MODEL), F32),
            jax.ShapeDtypeStruct((seq, ATT_WIDTH), F32),
        ),
        in_specs=[
            pl.BlockSpec((N_CHIPS, tb, SHARD_PAD), lambda i: (0, i, 0)),
            blk_rows(D_MODEL), blk_rows(D_MODEL), blk_rows(LANE), blk_rows(LANE),
            _const_spec((MIX_WIDTH, D_MODEL)), _const_spec((1, D_MODEL)), _const_spec((1, RET_WIDTH)),
            pl.BlockSpec(memory_space=pltpu.SMEM),
            _const_spec((PAIRS, 2 * BLK, BLK)), _const_spec((BLK, RET_QK_WIDTH)), _const_spec((BLK, RET_QK_WIDTH)),
            _const_spec(state_shape), _const_spec((2, BLK, 2 * BLK)),
        ],
        out_specs=(
            blk_rows(MIX_WIDTH), blk_rows(D_MODEL),
            pl.BlockSpec((nsub,) + state_shape, lambda i: (i, 0, 0, 0)),
            _const_spec((8, LANE)), _const_spec((1, D_MODEL)), blk_rows(ATT_WIDTH),
        ),
        scratch_shapes=[
            pltpu.VMEM((BLK, ATT_KV_WIDTH), F32), pltpu.VMEM((BLK, ATT_KV_WIDTH), F32),
            pltpu.VMEM(state_shape, F32),
        ],
        compiler_params=_params(48, ("arbitrary",)),
    )(proj, x, target, cos_t, sin_t, w_out, final_g, gn_gain, sinks, decay_in, qdec_t, kdec_t, cd_t, bias_t)


def _mix_bwd(proj, dxo, mix, o_att, states, x, w_out, w_in_t, norm_g, gn_gain, sinks, tables, tb):
    seq = dxo.shape[0]
    nsub = tb // BLK
    nblk = seq // tb
    cos_t, sin_t, decay_in, qdec_t, kdec_t, cd_t, bias_t = tables
    kv_cols = OFF_AK // (2 * ATT_KV_WIDTH)
    state_shape = (PAIRS, 2 * RET_QK_DIM, RET_V_DIM)

    def body(p_ref, pkv_ref, dxo_ref, mix_ref, oatt_ref, st_ref, cos_ref, sin_ref, x_ref, wout_ref, win_ref, g_ref,
             gain_ref, sinks_ref, din_ref, qdec_ref, kdec_ref, cd_ref, bias_ref,
             dp_ref, gx_ref, gwout_ref, gnorm_ref, dgain_ref, dsink_ref,
             dmix_ref, kv_ref, dkc_ref, dvc_ref, gst_ref):
        i = pl.program_id(0)
        blk = nblk - 1 - i

        @pl.when(i == 0)
        def _():
            gwout_ref[...] = jnp.zeros_like(gwout_ref)
            gnorm_ref[...] = jnp.zeros_like(gnorm_ref)
            dgain_ref[...] = jnp.zeros_like(dgain_ref)
            dsink_ref[...] = jnp.zeros_like(dsink_ref)
            dkc_ref[...] = jnp.zeros_like(dkc_ref)
            dvc_ref[...] = jnp.zeros_like(dvc_ref)
            gst_ref[...] = jnp.zeros_like(gst_ref)

        dxo_b = dxo_ref[...].astype(BF16)
        dmix_ref[...] = _dot_nt(dxo_b, wout_ref[...])
        gwout_ref[...] += _dot_tn(mix_ref[...], dxo_b)
        kv_ref[0:BLK, :] = pkv_ref[...].astype(F32)
        kv_ref[BLK:, :] = _proj_cols(p_ref, slice(None), OFF_AK, OFF_AK + 2 * ATT_KV_WIDTH).astype(F32)
        low = _low_lanes((BLK, LANE))
        low2 = _low_lanes((2 * BLK, LANE))
        lane = lax.broadcasted_iota(jnp.int32, (1, LANE), 1)
        prev = _in_previous_block()

        def sub(jj, carry):
            dkc, dvc, gsts, dgain, dsink = carry
            j = nsub - 1 - jj
            rows = pl.ds(pl.multiple_of(j * BLK, BLK), BLK)
            both = pl.ds(pl.multiple_of(j * BLK, BLK), 2 * BLK)
            bias = bias_ref[jnp.where(jnp.logical_or(blk > 0, j > 0), 1, 0)]

            aq = _proj_cols(p_ref, rows, OFF_AQ, OFF_AQ + ATT_WIDTH)
            az = _proj_cols(p_ref, rows, OFF_AZ, OFF_AZ + ATT_WIDTH).astype(F32)
            k_ops = _kv_operands(kv_ref[both, 0:ATT_KV_WIDTH])
            v_ops = _kv_operands(kv_ref[both, ATT_KV_WIDTH:2 * ATT_KV_WIDTH])
            da = dmix_ref[rows, 0:ATT_WIDTH]
            sig = _sigmoid(az)
            d_o = da * (az * sig)
            qs = [(_stack_tiles(aq, 2 * g) * ATT_SCALE).astype(BF16) for g in range(ATT_KV_HEADS)]
            dos = [_stack_tiles(d_o, 2 * g).astype(BF16) for g in range(ATT_KV_HEADS)]
            p, p_sink = _attn_probs(qs, k_ops, bias, prev, sinks_ref)
            dpr = [_fold(_dot_nt(v_ops[g][hi], dos[g]), prev) for g, hi in ATT_PROBLEMS]
            delta = [jnp.sum(pi * di, axis=0, keepdims=True) for pi, di in zip(p, dpr)]
            ds = [_unfold((pi * (di - ti)).astype(BF16), prev) for pi, di, ti in zip(p, dpr, delta)]
            for (g, hi), ki, ti in zip(ATT_PROBLEMS, p_sink, delta):
                sink_part = ki * ti
                for half in range(2):
                    tot = jnp.sum(sink_part[:, half * BLK:(half + 1) * BLK], axis=1, keepdims=True)
                    dsink = dsink - jnp.where(lane == 4 * g + 2 * half + hi, tot, 0.0)
            dq_tiles, dk_sums, dv_sums = [], [], []
            for g in range(ATT_KV_HEADS):
                ds_cat = jnp.concatenate([ds[2 * g], ds[2 * g + 1]], axis=0)
                p_cat = jnp.concatenate([_unfold(p[2 * g].astype(BF16), prev), _unfold(p[2 * g + 1].astype(BF16), prev)],
                                        axis=0)
                dqs = _dot_tn(ds_cat, jnp.concatenate(k_ops[g], axis=0)) * ATT_SCALE
                dq_tiles += [dqs[0:BLK], dqs[BLK:]]
                dk_sums.append(_dot(ds_cat, qs[g]))
                dv_sums.append(_dot(p_cat, dos[g]))
            daz = da * oatt_ref[rows, :] * (sig * (1.0 + az * (1.0 - sig)))

            def kv_grad(sums):
                (a0, b0), (a1, b1) = [(s[0:2 * BLK], s[2 * BLK:]) for s in sums]
                return jnp.where(low2, a0, b1) + pltpu.roll(jnp.where(low2, a1, b0), HALF_LANE, 1)

            dk_both, dv_both = kv_grad(dk_sums), kv_grad(dv_sums)
            dak = dk_both[BLK:] + dkc
            dav = dv_both[BLK:] + dvc

            cos_b, sin_b = cos_ref[rows, :], sin_ref[rows, :]
            qdec, kdec = qdec_ref[...], kdec_ref[...]
            qr, kr, qd, kd = _retention_operands(p_ref, rows, cos_b, sin_b, qdec, kdec)
            heads = [(t, hh) for t in range(PAIRS) for hh in range(2)]
            head_cols = [slice(h * RET_V_DIM, (h + 1) * RET_V_DIM) for h in range(RET_HEADS)]
            q_rows = [jnp.concatenate(_split_heads(_tile(qr, t)), axis=0).astype(BF16) for t in range(PAIRS)]
            k_rows = [jnp.concatenate(_split_heads(_tile(kr, t)), axis=0).astype(BF16) for t in range(PAIRS)]
            din = [din_ref[t] for t in range(PAIRS)]
            sc = [(_dot_nt(q_rows[t], _tile(kr, t).astype(BF16)) * din[t]).astype(BF16) for t in range(PAIRS)]
            qd_heads = [_split_heads(_tile(qd, t)) for t in range(PAIRS)]
            kd_heads = [_split_heads(_tile(kd, t)) for t in range(PAIRS)]
            state_b = [st_ref[j, t].astype(BF16) for t in range(PAIRS)]
            gst_b = [gsts[t].astype(BF16) for t in range(PAIRS)]
            vs = [_proj_cols(p_ref, rows, OFF_RV + h * RET_V_DIM, OFF_RV + (h + 1) * RET_V_DIM) for h in range(RET_HEADS)]
            rzs = [_proj_cols(p_ref, rows, OFF_RZ + h * RET_V_DIM, OFF_RZ + (h + 1) * RET_V_DIM).astype(F32)
                   for h in range(RET_HEADS)]
            drs = [dmix_ref[rows, ATT_WIDTH + h * RET_V_DIM:ATT_WIDTH + (h + 1) * RET_V_DIM] for h in range(RET_HEADS)]
            gains = [gain_ref[:, c] for c in head_cols]
            lhs = [jnp.concatenate([sc[t][hh * BLK:(hh + 1) * BLK], qd_heads[t][hh].astype(BF16)], axis=1) for t, hh in heads]
            rhs = [jnp.concatenate([vs[2 * t + hh], state_b[t]], axis=0) for t, hh in heads]
            ons, rstds = _group_norm_all([_dot(l, r) for l, r in zip(lhs, rhs)])
            sig_r = [_sigmoid(z) for z in rzs]
            dgn = [d * (z * g) for d, z, g in zip(drs, rzs, sig_r)]
            dz_parts = [d * (o * gn) * (g * (1.0 + z * (1.0 - g))) for d, o, gn, g, z in zip(drs, ons, gains, sig_r, rzs)]
            dgain_parts = [jnp.sum(d * o, axis=0, keepdims=True) for d, o in zip(dgn, ons)]
            don = [d * gn for d, gn in zip(dgn, gains)]
            mean_don = [jnp.mean(d, axis=1, keepdims=True) for d in don]
            mean_don_on = [jnp.mean(d * o, axis=1, keepdims=True) for d, o in zip(don, ons)]
            dob = [(r * (d - a - o * b)).astype(BF16) for r, d, a, o, b in zip(rstds, don, mean_don, ons, mean_don_on)]
            dlhs = [_dot_nt(d, r) for d, r in zip(dob, rhs)]
            drhs = [_dot_tn(l, d) for l, d in zip(lhs, dob)]
            dkds = [_dot_nt(vs[2 * t + hh], gst_b[t]) for t, hh in heads]
            dv_parts = [drhs[2 * t + hh][0:BLK] + _dot(kd_heads[t][hh].astype(BF16), gst_b[t]) for t, hh in heads]
            das = [(dlhs[2 * t + hh][:, 0:BLK] * din[t][hh * BLK:(hh + 1) * BLK]).astype(BF16) for t, hh in heads]
            new_gsts = [gsts[t] * cd_ref[t] + drhs[2 * t][BLK:] + drhs[2 * t + 1][BLK:] for t in range(PAIRS)]
            dq_parts = [_dot(jnp.concatenate([das[2 * t], das[2 * t + 1]], axis=1), k_rows[t])
                        + jnp.where(low, dlhs[2 * t][:, BLK:], dlhs[2 * t + 1][:, BLK:]) * _tile(qdec, t)
                        for t in range(PAIRS)]
            dk_parts = [_dot_tn(jnp.concatenate([das[2 * t], das[2 * t + 1]], axis=0), q_rows[t])
                        + jnp.where(low, dkds[2 * t], dkds[2 * t + 1]) * _tile(kdec, t) for t in range(PAIRS)]
            drq = _rotate_transposed(jnp.concatenate(dq_parts, axis=1), cos_b, sin_b)
            drk = _rotate_transposed(jnp.concatenate(dk_parts, axis=1) * RET_SCALE, cos_b, sin_b)

            dp_ref[rows, :] = jnp.concatenate(
                [jnp.concatenate(dq_tiles, axis=1), dak, dav, daz, drq, drk] + dv_parts + dz_parts, axis=1).astype(BF16)
            dgain = dgain + jnp.concatenate(dgain_parts, axis=1)
            return dk_both[0:BLK], dv_both[0:BLK], tuple(new_gsts), dgain, dsink

        carry = (dkc_ref[...], dvc_ref[...], tuple(gst_ref[t] for t in range(PAIRS)), dgain_ref[...], dsink_ref[...])
        dkc, dvc, gsts, dgain, dsink = lax.fori_loop(0, nsub, sub, carry)
        dkc_ref[...] = dkc
        dvc_ref[...] = dvc
        gst_ref[...] = jnp.stack(gsts)
        dgain_ref[...] = dgain
        dsink_ref[...] = dsink

        dh = _dot(dp_ref[...], win_ref[...])
        xv = x_ref[...]
        r = lax.rsqrt(jnp.mean(xv * xv, axis=1, keepdims=True) + RMS_EPS)
        xn = xv * r
        gnorm_ref[...] += jnp.sum(dh * xn, axis=0, keepdims=True)
        u = dh * g_ref[...]
        gx_ref[...] = dxo_ref[...] + r * u - xn * (r * jnp.mean(u * xn, axis=1, keepdims=True))

    rev_rows = lambda w: pl.BlockSpec((tb, w), lambda i: (nblk - 1 - i, 0))
    prev_kv = pl.BlockSpec((None, BLK, 2 * ATT_KV_WIDTH),
                           lambda i: (0, jnp.maximum((nblk - 1 - i) * nsub - 1, 0), kv_cols))
    return _call(
        body, name="mix_bwd", grid=(nblk,),
        out_shape=(
            jax.ShapeDtypeStruct((seq, IN_WIDTH), BF16),
            jax.ShapeDtypeStruct((seq, D_MODEL), F32),
            jax.ShapeDtypeStruct((MIX_WIDTH, D_MODEL), F32),
            jax.ShapeDtypeStruct((1, D_MODEL), F32),
            jax.ShapeDtypeStruct((1, RET_WIDTH), F32),
            jax.ShapeDtypeStruct((1, LANE), F32),
        ),
        in_specs=[
            pl.BlockSpec((N_CHIPS, tb, SHARD_PAD), lambda i: (0, nblk - 1 - i, 0)),
            prev_kv, rev_rows(D_MODEL), rev_rows(MIX_WIDTH), rev_rows(ATT_WIDTH),
            pl.BlockSpec((nsub,) + state_shape, lambda i: (nblk - 1 - i, 0, 0, 0)),
            rev_rows(LANE), rev_rows(LANE), rev_rows(D_MODEL),
            _const_spec((MIX_WIDTH, D_MODEL)), _const_spec((IN_WIDTH, D_MODEL)), _const_spec((1, D_MODEL)),
            _const_spec((1, RET_WIDTH)),
            pl.BlockSpec(memory_space=pltpu.SMEM),
            _const_spec((PAIRS, 2 * BLK, BLK)), _const_spec((BLK, RET_QK_WIDTH)), _const_spec((BLK, RET_QK_WIDTH)),
            _const_spec(state_shape), _const_spec((2, BLK, 2 * BLK)),
        ],
        out_specs=(
            rev_rows(IN_WIDTH), rev_rows(D_MODEL), _const_spec((MIX_WIDTH, D_MODEL)), _const_spec((1, D_MODEL)),
            _const_spec((1, RET_WIDTH)), _const_spec((1, LANE)),
        ),
        scratch_shapes=[
            pltpu.VMEM((tb, MIX_WIDTH), F32),
            pltpu.VMEM((tb + BLK, 2 * ATT_KV_WIDTH), F32),
            pltpu.VMEM((BLK, ATT_KV_WIDTH), F32), pltpu.VMEM((BLK, ATT_KV_WIDTH), F32),
            pltpu.VMEM(state_shape, F32),
        ],
        compiler_params=_params(60, ("arbitrary",)),
    )(proj, proj, dxo, mix, o_att, states, cos_t, sin_t, x, w_out, w_in_t, norm_g, gn_gain, sinks, decay_in, qdec_t,
      kdec_t, cd_t, bias_t)


def _gw_in_reduce(h_t, dproj, gw_out, small, tb):
    seq = dproj.shape[0]
    nblk = seq // tb
    last = nblk - 1
    hand_on = min(1, last)
    half = D_MODEL // 2
    A, B, C, N_SEMS = 0, N_CHIPS, 2 * N_CHIPS, 2 * N_CHIPS + 1

    def body(win_ref, ht_ref, dp_ref, gwo_hbm, s0_ref, s1_ref, s2_ref, s3_ref, s4_ref, out_ref, fout_ref, packsum_ref,
             acc, sib, send_buf, b_in, fin, fout, mine_out, sib_out, send_out, b_out, pack_ref, packs,
             send_sems, recv_sems, local_sems):
        p, i = pl.program_id(0), pl.program_id(1)
        small_start, small_hand_on, small_finish = _small_exchange(
            gwo_hbm, (s0_ref, s1_ref, s2_ref, s3_ref, s4_ref), fout, mine_out, sib_out, send_out, b_out, pack_ref,
            packs, send_sems, recv_sems, local_sems, N_SEMS)

        @pl.when(jnp.logical_and(p == 0, i == 0))
        def _():
            small_start()

        @pl.when(jnp.logical_and(p == 0, i == hand_on))
        def _():
            small_hand_on()

        x, y, c = lax.axis_index("x"), lax.axis_index("y"), lax.axis_index("c")
        chip = 2 * x + y
        sibling = (x, y, 1 - c)
        mine = pl.ds(pl.multiple_of(c * half, half), half)
        other = pl.ds(pl.multiple_of((1 - c) * half, half), half)

        def remote(src, dst, send_k, recv_k, to):
            return pltpu.make_async_remote_copy(src_ref=src, dst_ref=dst, send_sem=send_sems.at[send_k],
                                                recv_sem=recv_sems.at[recv_k], device_id=to, device_id_type=MESH_ID)

        part = _dot(ht_ref[...], dp_ref[...])
        slot = p % 2

        @pl.when(i == 0)
        def _():
            acc[slot] = part

        @pl.when(i > 0)
        def _():
            acc[slot] += part

        for q in range(N_CHIPS):
            s = q % 2
            to_sibling = remote(acc.at[s, other, :], sib.at[s], A + q, A + q, sibling)

            @pl.when(jnp.logical_and(p == q, i == last))
            def _():
                to_sibling.start()

            if q < N_CHIPS - 1:
                dest = (chip + 1 + q) % N_CHIPS

                @pl.when(jnp.logical_and(p == q + 1, i == hand_on))
                def _():
                    to_sibling.wait_recv()
                    send_buf[q] = (acc[s, mine, :] + sib[s]).astype(BF16)
                    remote(send_buf.at[q], b_in.at[chip], B + q, B + chip, (dest // 2, dest % 2, c)).start()
                    to_sibling.wait_send()
            else:
                @pl.when(jnp.logical_and(p == q, i == last))
                def _():
                    to_sibling.wait_recv()
                    fin[mine, :] = acc[s, mine, :] + sib[s]
                    for j in range(N_CHIPS):
                        @pl.when(j != chip)
                        def _():
                            remote(b_in.at[j], b_in.at[j], B + j, B + j, sibling).wait_recv()
                            fin[mine, :] += b_in[j].astype(F32)
                    to_core = remote(fin.at[mine, :], fin.at[mine, :], C, C, sibling)
                    to_core.start()
                    remote(fin.at[other, :], fin.at[other, :], C, C, sibling).wait_recv()
                    out_ref[...] = fin[...]
                    to_core.wait_send()
                    to_sibling.wait_send()
                    for k in range(N_CHIPS - 1):
                        remote(send_buf.at[k], b_in.at[chip], B + k, B + k, sibling).wait_send()
                    packsum_ref[...] = small_finish()
                    fout_ref[...] = fout[...]

    whole = lambda shape: pl.BlockSpec(shape, lambda p, i, win: (0,) * len(shape), pipeline_mode=pl.Buffered(1))
    grid_spec = pltpu.PrefetchScalarGridSpec(
        num_scalar_prefetch=1, grid=(N_CHIPS, nblk),
        in_specs=[pl.BlockSpec((D_MODEL, tb), lambda p, i, win: (0, i)),
                  pl.BlockSpec((pl.Element(tb), pl.Element(SHARD_PAD)),
                               lambda p, i, win: (i * tb, pl.multiple_of(win[p] * LANE, LANE))),
                  pl.BlockSpec(memory_space=pl.ANY)] + [whole(s.shape) for s in small],
        out_specs=(whole((D_MODEL, SHARD_PAD)), whole((SHARD_OUT, D_MODEL)), whole((PACK_ROWS, D_MODEL))),
        scratch_shapes=[
            pltpu.VMEM((2, D_MODEL, SHARD_PAD), F32), pltpu.VMEM((2, half, SHARD_PAD), F32),
            pltpu.VMEM((N_CHIPS - 1, half, SHARD_PAD), BF16), pltpu.VMEM((N_CHIPS, half, SHARD_PAD), BF16),
            pltpu.VMEM((D_MODEL, SHARD_PAD), F32),
        ] + _small_exchange_scratch() + [
            pltpu.SemaphoreType.DMA((N_SEMS + SMALL_SEMS,)), pltpu.SemaphoreType.DMA((N_SEMS + SMALL_SEMS,)),
            pltpu.SemaphoreType.DMA((N_CHIPS,)),
        ])
    chip = 2 * lax.axis_index("x") + lax.axis_index("y")
    owner = (chip + 1 + jnp.arange(N_CHIPS, dtype=jnp.int32)) % N_CHIPS
    win_start = (owner * SHARD_IN) // LANE
    return _call(
        body, name="gw_in_reduce", grid_spec=grid_spec,
        out_shape=(jax.ShapeDtypeStruct((D_MODEL, SHARD_PAD), F32), jax.ShapeDtypeStruct((SHARD_OUT, D_MODEL), F32),
                   jax.ShapeDtypeStruct((PACK_ROWS, D_MODEL), F32)),
        compiler_params=_params(52, ("arbitrary", "arbitrary")),
    )(win_start.astype(jnp.int32), h_t, dproj, gw_out, *small)


SMALL_SEMS = 17


def _small_exchange(gwo_hbm, small_refs, fout_ref, mine_out, sib_out, send_out, b_out, pack_ref, packs,
                    send_sems, recv_sems, local_sems, base):
    half_out = SHARD_OUT // 2
    A_OUT, B_OUT, C_OUT, PACK = base, base + 4, base + 8, base + 9
    assert len(small_refs) == PACK_PARTS
    x, y, c = lax.axis_index("x"), lax.axis_index("y"), lax.axis_index("c")
    chip = 2 * x + y
    dev = 2 * chip + c
    sibling = (x, y, 1 - c)

    def remote(src, dst, send_k, recv_k, to):
        return pltpu.make_async_remote_copy(src_ref=src, dst_ref=dst, send_sem=send_sems.at[send_k],
                                            recv_sem=recv_sems.at[recv_k], device_id=to, device_id_type=MESH_ID)

    def out_rows(j, core):
        return pl.ds(pl.multiple_of(j * SHARD_OUT + core * half_out, half_out), half_out)

    my_out_rows = pl.ds(pl.multiple_of(c * half_out, half_out), half_out)
    local = [pltpu.make_async_copy(gwo_hbm.at[out_rows(j, c), :], mine_out.at[j], local_sems.at[j])
             for j in range(N_CHIPS)]
    stage_a = [remote(gwo_hbm.at[out_rows(j, 1 - c), :], sib_out.at[j], A_OUT + j, A_OUT + j, sibling)
               for j in range(N_CHIPS)]
    mine_half_out = fout_ref.at[my_out_rows, :]
    stage_c = [remote(mine_half_out, mine_half_out, C_OUT, C_OUT, sibling)]

    def start():
        pack_ref[...] = jnp.zeros_like(pack_ref)
        for k, s_ref in enumerate(small_refs):
            pack_ref[k:k + 1, 0:s_ref.shape[1]] = s_ref[0:1, :]
        packs[dev] = pack_ref[...]
        for d in range(N_DEV):
            to = (d // 4, (d // 2) % 2, d % 2)

            @pl.when(d != dev)
            def _():
                remote(pack_ref, packs.at[dev], PACK + d, PACK + dev, to).start()

        for cp in local + stage_a:
            cp.start()

    def hand_on():
        for cp in local:
            cp.wait()
        for cp in stage_a:
            cp.wait_recv()
        for j in range(N_CHIPS):
            mine_out[j] = mine_out[j] + sib_out[j]

        for j in range(N_CHIPS):
            to = (j // 2, j % 2, c)

            @pl.when(j != chip)
            def _():
                send_out[j] = mine_out[j].astype(BF16)
                remote(send_out.at[j], b_out.at[chip], B_OUT + j, B_OUT + chip, to).start()

            @pl.when(j == chip)
            def _():
                fout_ref[my_out_rows, :] = mine_out[j]

    def finish():
        for j in range(N_CHIPS):
            @pl.when(j != chip)
            def _():
                remote(b_out.at[j], b_out.at[j], B_OUT + j, B_OUT + j, sibling).wait_recv()
                fout_ref[my_out_rows, :] += b_out[j].astype(F32)

        for cp in stage_c:
            cp.start()
        other_half_out = fout_ref.at[pl.ds(pl.multiple_of((1 - c) * half_out, half_out), half_out), :]
        remote(other_half_out, other_half_out, C_OUT, C_OUT, sibling).wait_recv()

        for d in range(N_DEV):
            @pl.when(d != dev)
            def _():
                remote(pack_ref, packs.at[d], PACK + d, PACK + d, sibling).wait_recv()
        total = packs[0]
        for d in range(1, N_DEV):
            total = total + packs[d]

        for cp in stage_a + stage_c:
            cp.wait_send()
        for j in range(N_CHIPS):
            @pl.when(j != chip)
            def _():
                remote(b_out.at[j], b_out.at[j], B_OUT + j, B_OUT + j, sibling).wait_send()
        for d in range(N_DEV):
            @pl.when(d != dev)
            def _():
                remote(pack_ref, packs.at[d], PACK + d, PACK + d, sibling).wait_send()
        return total

    return start, hand_on, finish


def _small_exchange_scratch():
    half_out = SHARD_OUT // 2
    return [
        pltpu.VMEM((SHARD_OUT, D_MODEL), F32),
        pltpu.VMEM((N_CHIPS, half_out, D_MODEL), F32), pltpu.VMEM((N_CHIPS, half_out, D_MODEL), F32),
        pltpu.VMEM((N_CHIPS, half_out, D_MODEL), BF16), pltpu.VMEM((N_CHIPS, half_out, D_MODEL), BF16),
        pltpu.VMEM((PACK_ROWS, D_MODEL), F32), pltpu.VMEM((N_DEV, PACK_ROWS, D_MODEL), F32),
    ]


def _adam_math(w, g, m, v):
    mn = ADAM_B1 * m + (1.0 - ADAM_B1) * g
    vn = ADAM_B2 * v + (1.0 - ADAM_B2) * (g * g)
    m_hat = mn / (1.0 - ADAM_B1 ** ADAM_STEP)
    v_hat = vn / (1.0 - ADAM_B2 ** ADAM_STEP)
    return -ADAM_LR * (m_hat / (jnp.sqrt(v_hat) + ADAM_EPS) + ADAM_WD * w), mn, vn


def _adamw(name, w, g, m, v, tb):
    rows, cols = w.shape

    def body(w_ref, g_ref, m_ref, v_ref, go_ref, d_ref, mo_ref, vo_ref):
        gv = g_ref[...]
        go_ref[...] = gv
        d_ref[...], mo_ref[...], vo_ref[...] = _adam_math(w_ref[...], gv, m_ref[...], v_ref[...])

    spec = pl.BlockSpec((tb, cols), lambda i: (i, 0))
    shape = jax.ShapeDtypeStruct((rows, cols), F32)
    return _call(
        body, name=name, grid=(rows // tb,), out_shape=(shape,) * 4,
        in_specs=[spec] * 4, out_specs=(spec,) * 4,
        compiler_params=_params(32, ("arbitrary",)),
    )(w, g, m, v)


def _adamw_w_in(w_t, g_window, m_t, v_t, tb):
    def body(w_ref, g_ref, m_ref, v_ref, go_ref, d_ref, mo_ref, vo_ref, gt_ref):
        gt_ref[...] = g_ref[...].T
        gv = gt_ref[pl.ds(pl.multiple_of(lax.axis_index("y") * SHARD_SHIFT, SHARD_SHIFT), SHARD_IN), :]
        go_ref[...] = gv
        d_ref[...], mo_ref[...], vo_ref[...] = _adam_math(w_ref[...], gv, m_ref[...], v_ref[...])

    spec = pl.BlockSpec((SHARD_IN, tb), lambda i: (0, i))
    shape = jax.ShapeDtypeStruct((SHARD_IN, D_MODEL), F32)
    return _call(
        body, name="adamw_w_in", grid=(D_MODEL // tb,), out_shape=(shape,) * 4,
        in_specs=[spec, pl.BlockSpec((tb, SHARD_PAD), lambda i: (i, 0)), spec, spec], out_specs=(spec,) * 4,
        scratch_shapes=[pltpu.VMEM((SHARD_PAD, tb), F32)],
        compiler_params=_params(32, ("arbitrary",)),
    )(w_t, g_window, m_t, v_t)


def _adamw_small(sums, params):
    def body(sums_ref, *refs):
        ins, outs = refs[:3 * len(params)], refs[3 * len(params):]
        for k in range(len(params)):
            w_ref, m_ref, v_ref = ins[3 * k:3 * k + 3]
            g = sums_ref[k:k + 1, 0:w_ref.shape[1]]
            go_ref, d_ref, mo_ref, vo_ref = outs[4 * k:4 * k + 4]
            go_ref[...] = g
            d_ref[...], mo_ref[...], vo_ref[...] = _adam_math(w_ref[...], g, m_ref[...], v_ref[...])

    vmem = pl.BlockSpec(memory_space=pltpu.VMEM)
    flat = [a for p in params for a in p]
    shapes = tuple(jax.ShapeDtypeStruct(p[0].shape, F32) for p in params for _ in range(4))
    res = _call(body, name="adamw_small", out_shape=shapes, in_specs=[vmem] * (1 + len(flat)),
                out_specs=(vmem,) * len(shapes), compiler_params=_params(16))(sums, *flat)
    return [res[4 * k:4 * k + 4] for k in range(len(params))]


def kernel(x, norm_g, w_in, att_sinks, ret_gn_g, w_out, final_g, loss_target, m_norm_g, m_w_in, m_att_sinks, m_ret_gn_g, m_w_out, m_final_g, v_norm_g, v_w_in, v_att_sinks, v_ret_gn_g, v_w_out, v_final_g):
    seq = x.shape[1]
    xs, tgt = x[0], loss_target[0]
    final_g2 = final_g.reshape(1, D_MODEL)
    tables = _tables(seq)

    w_in_t, m_w_in_t, v_w_in_t = w_in[0].T, m_w_in[0].T, v_w_in[0].T
    proj, h_t, w_in_full, w_out_full = _in_proj(xs, norm_g, w_in_t, w_out[0], min(TOKENS_PROJ, seq))
    mix, dxo, states, loss_part, gfin, o_att = _mix_fwd(proj, xs, tgt, w_out_full, final_g2, ret_gn_g, att_sinks,
                                                        tables, min(TOKENS_MIX, seq))
    dproj, grad_x, gw_out, gnorm, dgain, dsink = _mix_bwd(proj, dxo, mix, o_att, states, xs, w_out_full, w_in_full,
                                                          norm_g, ret_gn_g, att_sinks, tables, min(TOKENS_MIX, seq))
    g_in, g_out, sums = _gw_in_reduce(h_t, dproj, gw_out, (gnorm, gfin, dgain, dsink, loss_part),
                                      min(TOKENS_GW, seq))

    res_in = [r.T for r in _adamw_w_in(w_in_t, g_in, m_w_in_t, v_w_in_t, 256)]
    res_out = _adamw("adamw_w_out", w_out[0], g_out, m_w_out[0], v_w_out[0], SHARD_OUT)
    as_row = lambda a: a.reshape(1, D_MODEL)
    r_norm, r_final, r_gain, r_sink = _adamw_small(sums, [
        (norm_g, m_norm_g, v_norm_g), (final_g2, as_row(m_final_g), as_row(v_final_g)),
        (ret_gn_g, m_ret_gn_g, v_ret_gn_g), (att_sinks, m_att_sinks, v_att_sinks)])

    outs = []
    for k in range(4):
        outs += [r_norm[k], res_in[k][None], r_sink[k], r_gain[k], res_out[k][None], r_final[k].reshape(D_MODEL)]
    return (sums[4, 0], grad_x[None], *outs)
```

```python
import jax
import jax.numpy as jnp
import numpy as np
from jax import lax
from jax.experimental import pallas as pl
from jax.experimental.pallas import tpu as pltpu

F32 = jnp.float32
BF16 = jnp.bfloat16

D_MODEL = 1024
ATT_HEADS = 8
ATT_KV_HEADS = 2
ATT_HEAD_DIM = 64
RET_HEADS = 4
RET_QK_DIM = 64
RET_V_DIM = 128
BLK = 128
ROT_BASE = 10000.0
RMS_EPS = 1e-6
GN_EPS = 1e-6
NEG_INF = -1e30
ATT_SCALE = ATT_HEAD_DIM ** -0.5
RET_SCALE = RET_QK_DIM ** -0.5

ATT_WIDTH = ATT_HEADS * ATT_HEAD_DIM
ATT_KV_WIDTH = ATT_KV_HEADS * ATT_HEAD_DIM
RET_QK_WIDTH = RET_HEADS * RET_QK_DIM
RET_WIDTH = RET_HEADS * RET_V_DIM
MIX_WIDTH = ATT_WIDTH + RET_WIDTH
OFF_AQ = 0
OFF_AK = OFF_AQ + ATT_WIDTH
OFF_AV = OFF_AK + ATT_KV_WIDTH
OFF_AZ = OFF_AV + ATT_KV_WIDTH
OFF_RQ = OFF_AZ + ATT_WIDTH
OFF_RK = OFF_RQ + RET_QK_WIDTH
OFF_RV = OFF_RK + RET_QK_WIDTH
OFF_RZ = OFF_RV + RET_WIDTH
IN_WIDTH = OFF_RZ + RET_WIDTH

LANE = 128
BF16_ROWS = 16
HALF_LANE = LANE // 2
PAIRS = RET_QK_WIDTH // LANE
assert ATT_HEAD_DIM == HALF_LANE and RET_QK_DIM == HALF_LANE and RET_V_DIM == LANE and ATT_KV_WIDTH == LANE

N_CHIPS = 4
N_DEV = 8
SHARD_IN = IN_WIDTH // N_CHIPS
SHARD_PAD = 768
SHARD_SHIFT = SHARD_PAD - SHARD_IN
WIN_START = tuple((j * SHARD_IN) // LANE * LANE for j in range(N_CHIPS))
SHARD_OUT = MIX_WIDTH // N_CHIPS

PACK_PARTS = 5
PACK_ROWS = 8
assert PACK_PARTS <= PACK_ROWS

ADAM_LR = 0.001
ADAM_B1 = 0.9
ADAM_B2 = 0.999
ADAM_EPS = 1e-08
ADAM_WD = 0.01
ADAM_STEP = 10

VMEM_CAP = 64 * 1024 * 1024
TOKENS_PROJ = 1024
TOKENS_MIX = 512
TOKENS_GW = 2048
MESH_ID = pl.DeviceIdType.MESH


def _call(body, **kw):
    return pl.pallas_call(body, **kw)


def _params(vmem_mb, semantics=None):
    assert vmem_mb * 1024 * 1024 < VMEM_CAP
    return pltpu.CompilerParams(dimension_semantics=semantics, vmem_limit_bytes=vmem_mb * 1024 * 1024)


def _dot(a, b):
    return jnp.dot(a, b, preferred_element_type=F32)


def _dot_nt(a, b):
    return lax.dot_general(a, b, (((1,), (1,)), ((), ())), preferred_element_type=F32)


def _dot_tn(a, b):
    return lax.dot_general(a, b, (((0,), (0,)), ((), ())), preferred_element_type=F32)


def _sigmoid(z):
    return 1.0 / (1.0 + jnp.exp(-z))


def _const_spec(shape):
    nd = len(shape)
    return pl.BlockSpec(shape, lambda i: (0,) * nd, pipeline_mode=pl.Buffered(1))


def _tables(seq):
    f32 = np.float32
    pos = np.arange(seq, dtype=f32)
    theta = (f32(1.0) / (f32(ROT_BASE) ** np.linspace(0.0, 1.0, RET_QK_DIM // 2, dtype=f32))).astype(f32)
    ang = (pos[:, None] * theta[None, :]).astype(f32)
    cos, sin = np.cos(ang), np.sin(ang)
    cos2 = np.repeat(cos, 2, axis=1)
    sin2 = np.stack([-sin, sin], axis=-1).reshape(seq, RET_QK_DIM)
    cos_t = np.tile(cos2, (1, 2))
    sin_t = np.tile(sin2, (1, 2))

    log_gamma = np.log(f32(1.0) - f32(2.0) ** (f32(-5.0) - np.arange(RET_HEADS, dtype=f32))).astype(f32)
    idx = np.arange(BLK, dtype=f32)
    rel = idx[:, None] - idx[None, :]
    decay_in = np.where(rel >= 0, np.exp(log_gamma[:, None, None] * np.maximum(rel, f32(0.0))), f32(0.0))
    k_dec = np.exp(log_gamma[:, None] * (BLK - 1 - idx)[None, :])
    q_dec = np.exp(log_gamma[:, None] * (idx + 1)[None, :])
    chunk_decay = np.exp(log_gamma * f32(BLK))
    kdec_t = np.repeat(k_dec.T, RET_QK_DIM, axis=1)
    qdec_t = np.repeat(q_dec.T, RET_QK_DIM, axis=1)
    cd_t = np.broadcast_to(chunk_decay[:, None, None], (RET_HEADS, RET_QK_DIM, RET_V_DIM))
    decay_in = decay_in.reshape(PAIRS, 2 * BLK, BLK)
    cd_t = cd_t.reshape(PAIRS, 2 * RET_QK_DIM, RET_V_DIM)

    key = np.arange(BLK)[:, None]
    query = np.arange(2 * BLK)[None, :] % BLK
    bias = np.stack([np.where(key > query, NEG_INF, 0.0), np.zeros((BLK, 2 * BLK))])
    return tuple(jnp.asarray(np.ascontiguousarray(a), F32) for a in (cos_t, sin_t, decay_in, qdec_t, kdec_t, cd_t, bias))


def _low_lanes(shape):
    lane = lax.broadcasted_iota(jnp.int32, shape, len(shape) - 1)
    return (lane & HALF_LANE) == 0


def _split_heads(t):
    low = _low_lanes(t.shape)
    zero = jnp.zeros_like(t)
    return jnp.where(low, t, zero), jnp.where(low, zero, t)


def _swap_pairs(t):
    lane = lax.broadcasted_iota(jnp.int32, t.shape, 1)
    nxt = pltpu.roll(t, t.shape[1] - 1, 1)
    prv = pltpu.roll(t, 1, 1)
    return jnp.where((lane & 1) == 0, nxt, prv)


def _per_tile(fn, t):
    return jnp.concatenate([fn(_tile(t, i)) for i in range(t.shape[1] // LANE)], axis=1)


def _rotate(t, cos_t, sin_t):
    return _per_tile(lambda a: a * cos_t + _swap_pairs(a) * sin_t, t)


def _rotate_transposed(d, cos_t, sin_t):
    return _per_tile(lambda a: a * cos_t + _swap_pairs(a * sin_t), d)


def _kv_operands(cat):
    low = _low_lanes(cat.shape)
    swapped = pltpu.roll(cat, HALF_LANE, 1)
    zero = jnp.zeros_like(cat)
    pick = lambda a, b: jnp.where(low, a, b).astype(BF16)
    return ((pick(cat, zero), pick(zero, swapped)), (pick(swapped, zero), pick(zero, cat)))


def _stack_tiles(t, first_tile):
    a = t[:, first_tile * LANE:(first_tile + 1) * LANE]
    b = t[:, (first_tile + 1) * LANE:(first_tile + 2) * LANE]
    return jnp.concatenate([a, b], axis=0)


def _sink_rows(sinks_ref, group):
    first = lax.broadcasted_iota(jnp.int32, (1, 2 * BLK), 1) < BLK

    def row(h0, h1):
        return jnp.where(first, sinks_ref[0, group * 4 + h0], sinks_ref[0, group * 4 + h1])
    return row(0, 2), row(1, 3)


ATT_PROBLEMS = tuple((g, hi) for g in range(ATT_KV_HEADS) for hi in range(2))


def _in_previous_block():
    key = lax.broadcasted_iota(jnp.int32, (BLK, 2 * BLK), 0)
    query = lax.broadcasted_iota(jnp.int32, (BLK, 2 * BLK), 1) & (BLK - 1)
    return key > query


def _fold(t, prev):
    return jnp.where(prev, t[0:BLK], t[BLK:])


def _unfold(t, prev):
    zero = jnp.zeros_like(t)
    return jnp.concatenate([jnp.where(prev, t, zero), jnp.where(prev, zero, t)], axis=0)


def _attn_probs(qs, k_ops, bias, prev, sinks_ref):
    sink = [_sink_rows(sinks_ref, g)[hi] for g, hi in ATT_PROBLEMS]
    s = [_fold(_dot_nt(k_ops[g][hi], qs[g]), prev) + bias for g, hi in ATT_PROBLEMS]
    m = [jnp.maximum(jnp.max(si, axis=0, keepdims=True), ki) for si, ki in zip(s, sink)]
    e = [jnp.exp(si - mi) for si, mi in zip(s, m)]
    es = [jnp.exp(ki - mi) for ki, mi in zip(sink, m)]
    inv = [1.0 / (jnp.sum(ei, axis=0, keepdims=True) + esi) for ei, esi in zip(e, es)]
    return [ei * ii for ei, ii in zip(e, inv)], [esi * ii for esi, ii in zip(es, inv)]


def _group_norm_all(outs):
    mu = [jnp.mean(o, axis=1, keepdims=True) for o in outs]
    xc = [o - m for o, m in zip(outs, mu)]
    var = [jnp.mean(c * c, axis=1, keepdims=True) for c in xc]
    rstd = [lax.rsqrt(v + GN_EPS) for v in var]
    return [c * r for c, r in zip(xc, rstd)], rstd


def _proj_cols(p_ref, rows, start, stop):
    pieces = []
    while start < stop:
        k = max(j for j in range(N_CHIPS) if WIN_START[j] <= start)
        end = min(stop, WIN_START[k] + SHARD_PAD)
        pieces.append(p_ref[k, rows, start - WIN_START[k]:end - WIN_START[k]])
        start = end
    return pieces[0] if len(pieces) == 1 else jnp.concatenate(pieces, axis=1)


def _retention_operands(p_ref, rows, cos_b, sin_b, qdec, kdec):
    qr = _rotate(_proj_cols(p_ref, rows, OFF_RQ, OFF_RQ + RET_QK_WIDTH).astype(F32), cos_b, sin_b)
    kr = _rotate(_proj_cols(p_ref, rows, OFF_RK, OFF_RK + RET_QK_WIDTH).astype(F32), cos_b, sin_b) * RET_SCALE
    return qr, kr, qr * qdec, kr * kdec


def _tile(t, i):
    return t[:, i * LANE:(i + 1) * LANE]


GATHER_SEMS = 7


def _allgather_steps(src_ref, full_ref, blk_ref, send_sems, recv_sems):
    block_rows = blk_ref.shape[0]
    assert block_rows % BF16_ROWS == 0 and full_ref.shape[0] == N_DEV * block_rows
    x, y, c = lax.axis_index("x"), lax.axis_index("y"), lax.axis_index("c")
    me, sibling = (x, y, c), (x, y, 1 - c)
    chips = [(1 - x, y), (x, 1 - y), (1 - x, 1 - y)]

    def rows(px, py, pc):
        return full_ref.at[pl.ds(pl.multiple_of((4 * px + 2 * py + pc) * block_rows, BF16_ROWS), block_rows), :]

    def copy(k, block, to, src=None):
        return pltpu.make_async_remote_copy(
            src_ref=rows(*block) if src is None else src, dst_ref=rows(*block),
            send_sem=send_sems.at[k], recv_sem=recv_sems.at[k], device_id=to, device_id_type=MESH_ID)

    first = [copy(0, me, sibling, src=blk_ref)] + [copy(1 + j, me, (*chip, c), src=blk_ref) for j, chip in enumerate(chips)]
    passed = [copy(4 + j, (*chip, c), sibling) for j, chip in enumerate(chips)]

    def start():
        blk_ref[...] = src_ref[pl.ds(pl.multiple_of(c * block_rows, BF16_ROWS), block_rows), :].astype(BF16)
        rows(*me)[...] = blk_ref[...]
        for cp in first:
            cp.start()

    def forward():
        for j, chip in enumerate(chips):
            copy(1 + j, (*chip, c), me).wait_recv()
            passed[j].start()

    def finish():
        copy(0, sibling, me).wait_recv()
        for j, chip in enumerate(chips):
            copy(4 + j, (*chip, 1 - c), me).wait_recv()
        for cp in first + passed:
            cp.wait_send()

    return start, forward, finish


WINDOW_HALF = SHARD_PAD // 2
WINDOW_SEMS = GATHER_SEMS + 1


def _window_gather(wi_ref, w4_ref, blk_ref, edge_ref, send_sems, recv_sems):
    x, y, c = lax.axis_index("x"), lax.axis_index("y"), lax.axis_index("c")
    me, sibling, pair = (x, y, c), (x, y, 1 - c), (x, 1 - y, c)
    chips = [(1 - x, y), (x, 1 - y), (1 - x, 1 - y)]
    own_rows, edge_rows = SHARD_IN - WINDOW_HALF, SHARD_SHIFT
    sends_edge = y == c

    def rows(px, py, pc):
        return w4_ref.at[pl.ds(pl.multiple_of((4 * px + 2 * py + pc) * WINDOW_HALF, BF16_ROWS), WINDOW_HALF), :]

    def copy(k, block, to, src=None):
        return pltpu.make_async_remote_copy(
            src_ref=rows(*block) if src is None else src, dst_ref=rows(*block),
            send_sem=send_sems.at[k], recv_sem=recv_sems.at[k], device_id=to, device_id_type=MESH_ID)

    edge_copy = pltpu.make_async_remote_copy(
        src_ref=edge_ref, dst_ref=edge_ref, send_sem=send_sems.at[GATHER_SEMS], recv_sem=recv_sems.at[GATHER_SEMS],
        device_id=pair, device_id_type=MESH_ID)
    first = [copy(0, me, sibling, src=blk_ref)] + [copy(1 + j, me, (*chip, c), src=blk_ref) for j, chip in enumerate(chips)]
    passed = [copy(4 + j, (*chip, c), sibling) for j, chip in enumerate(chips)]

    def at(start, size):
        return pl.ds(pl.multiple_of(start, BF16_ROWS), size)

    def start():
        @pl.when(sends_edge)
        def _():
            edge_ref[...] = wi_ref[at((1 - y) * (SHARD_IN - edge_rows), edge_rows), :].astype(BF16)
            edge_copy.start()
            blk_ref[...] = wi_ref[at(c * own_rows, WINDOW_HALF), :].astype(BF16)

        @pl.when(jnp.logical_not(sends_edge))
        def _():
            blk_ref[at(y * edge_rows, own_rows), :] = wi_ref[at(c * WINDOW_HALF, own_rows), :].astype(BF16)
            edge_copy.wait_recv()
            blk_ref[at((1 - y) * own_rows, edge_rows), :] = edge_ref[...]

        rows(*me)[...] = blk_ref[...]
        for cp in first:
            cp.start()
        copy(0, sibling, me).wait_recv()

    def forward(j):
        copy(1 + j, (*chips[j], c), me).wait_recv()
        passed[j].start()

    def wait_forwarded(j):
        copy(4 + j, (*chips[j], 1 - c), me).wait_recv()

    def finish():
        for cp in first + passed:
            cp.wait_send()

        @pl.when(sends_edge)
        def _():
            edge_copy.wait_send()

    return start, forward, wait_forwarded, finish


def _in_proj(x, norm_g, w_in_t_shard, w_out_shard, tb):
    seq = x.shape[0]
    nblk = seq // tb
    last = nblk - 1
    chip_of_panel = (None, 1, 0, 2)

    def body(win_ref, x_ref, g_ref, wi_ref, wo_ref, p_ref, ht_ref, wt_ref, wout_ref,
             w4, blk, edge, hbuf, wout_full, blko, send_sems, recv_sems, send_sems_o, recv_sems_o):
        q, i = pl.program_id(0), pl.program_id(1)
        chip = 2 * lax.axis_index("x") + lax.axis_index("y")
        in_start, in_forward, in_wait_forwarded, in_finish = _window_gather(wi_ref, w4, blk, edge, send_sems, recv_sems)
        out_start, out_forward, out_finish = _allgather_steps(wo_ref, wout_full, blko, send_sems_o, recv_sems_o)
        rows = pl.ds(pl.multiple_of(i * tb, tb), tb)

        @pl.when(jnp.logical_and(q == 0, i == 0))
        def _():
            in_start()
            out_start()

        for panel in range(1, N_CHIPS):
            @pl.when(jnp.logical_and(q == panel, i == 0))
            def _():
                in_wait_forwarded(chip_of_panel[panel])

        @pl.when(q == 0)
        def _():
            xv = x_ref[...]
            r = lax.rsqrt(jnp.mean(xv * xv, axis=1, keepdims=True) + RMS_EPS)
            h = (xv * r) * g_ref[...]
            hbuf[rows, :] = h.astype(BF16)
            ht_ref[...] = h.T.astype(BF16)

        owner = jnp.bitwise_xor(chip, q)
        window = w4[pl.ds(pl.multiple_of(owner * SHARD_PAD, SHARD_PAD), SHARD_PAD), :]
        p_ref[...] = _dot_nt(hbuf[rows, :], window).astype(BF16)

        for panel in range(1, N_CHIPS):
            @pl.when(jnp.logical_and(q == panel - 1, i == last))
            def _():
                in_forward(chip_of_panel[panel])

        @pl.when(jnp.logical_and(q == N_CHIPS - 1, i == 0))
        def _():
            out_forward()

        @pl.when(jnp.logical_and(q == N_CHIPS - 1, i == last))
        def _():
            in_finish()
            for k in range(N_CHIPS):
                first = k * SHARD_PAD + (k % 2) * SHARD_SHIFT
                wt_ref[k * SHARD_IN:(k + 1) * SHARD_IN, :] = w4[first:first + SHARD_IN, :]
            out_finish()
            wout_ref[...] = wout_full[...]

    whole = lambda shape: pl.BlockSpec(shape, lambda q, i, win: (0,) * len(shape), pipeline_mode=pl.Buffered(1))
    first_panel = lambda q, i: jnp.where(q == 0, i, last)
    grid_spec = pltpu.PrefetchScalarGridSpec(
        num_scalar_prefetch=1, grid=(N_CHIPS, nblk),
        in_specs=[pl.BlockSpec((tb, D_MODEL), lambda q, i, win: (first_panel(q, i), 0)), whole((1, D_MODEL)),
                  whole((SHARD_IN, D_MODEL)), whole((SHARD_OUT, D_MODEL))],
        out_specs=(pl.BlockSpec((None, tb, SHARD_PAD), lambda q, i, win: (win[q], i, 0)),
                   pl.BlockSpec((D_MODEL, tb), lambda q, i, win: (0, first_panel(q, i))),
                   whole((IN_WIDTH, D_MODEL)), whole((MIX_WIDTH, D_MODEL))),
        scratch_shapes=[
            pltpu.VMEM((N_CHIPS * SHARD_PAD, D_MODEL), BF16), pltpu.VMEM((WINDOW_HALF, D_MODEL), BF16),
            pltpu.VMEM((SHARD_SHIFT, D_MODEL), BF16), pltpu.VMEM((seq, D_MODEL), BF16),
            pltpu.VMEM((MIX_WIDTH, D_MODEL), BF16), pltpu.VMEM((SHARD_OUT // 2, D_MODEL), BF16),
            pltpu.SemaphoreType.DMA((WINDOW_SEMS,)), pltpu.SemaphoreType.DMA((WINDOW_SEMS,)),
            pltpu.SemaphoreType.DMA((GATHER_SEMS,)), pltpu.SemaphoreType.DMA((GATHER_SEMS,)),
        ])
    chip = 2 * lax.axis_index("x") + lax.axis_index("y")
    owner = jnp.bitwise_xor(chip, jnp.arange(N_CHIPS, dtype=jnp.int32))
    return _call(
        body, name="in_proj", grid_spec=grid_spec,
        out_shape=(jax.ShapeDtypeStruct((N_CHIPS, seq, SHARD_PAD), BF16), jax.ShapeDtypeStruct((D_MODEL, seq), BF16),
                   jax.ShapeDtypeStruct((IN_WIDTH, D_MODEL), BF16), jax.ShapeDtypeStruct((MIX_WIDTH, D_MODEL), BF16)),
        compiler_params=_params(60, ("arbitrary", "arbitrary")),
    )(owner.astype(jnp.int32), x, norm_g, w_in_t_shard, w_out_shard)


def _mix_fwd(proj, x, target, w_out, final_g, gn_gain, sinks, tables, tb):
    seq = x.shape[0]
    nsub = tb // BLK
    cos_t, sin_t, decay_in, qdec_t, kdec_t, cd_t, bias_t = tables

    def body(p_ref, x_ref, t_ref, cos_ref, sin_ref, wout_ref, fg_ref, gain_ref, sinks_ref, din_ref, qdec_ref,
             kdec_ref, cd_ref, bias_ref, mix_ref, dxo_ref, st_ref, loss_ref, gfin_ref, oatt_ref,
             kprev_ref, vprev_ref, state_ref):
        i = pl.program_id(0)

        @pl.when(i == 0)
        def _():
            kprev_ref[...] = jnp.zeros_like(kprev_ref)
            vprev_ref[...] = jnp.zeros_like(vprev_ref)
            state_ref[...] = jnp.zeros_like(state_ref)
            loss_ref[...] = jnp.zeros_like(loss_ref)
            gfin_ref[...] = jnp.zeros_like(gfin_ref)

        prev = _in_previous_block()

        def sub(j, carry):
            kp, vp, states = carry
            rows = pl.ds(pl.multiple_of(j * BLK, BLK), BLK)
            bias = bias_ref[jnp.where(jnp.logical_or(i > 0, j > 0), 1, 0)]

            aq = _proj_cols(p_ref, rows, OFF_AQ, OFF_AQ + ATT_WIDTH)
            ak = _proj_cols(p_ref, rows, OFF_AK, OFF_AK + ATT_KV_WIDTH).astype(F32)
            av = _proj_cols(p_ref, rows, OFF_AV, OFF_AV + ATT_KV_WIDTH).astype(F32)
            az = _proj_cols(p_ref, rows, OFF_AZ, OFF_AZ + ATT_WIDTH).astype(F32)
            k_ops = _kv_operands(jnp.concatenate([kp, ak], axis=0))
            v_ops = _kv_operands(jnp.concatenate([vp, av], axis=0))
            qs = [(_stack_tiles(aq, 2 * g) * ATT_SCALE).astype(BF16) for g in range(ATT_KV_HEADS)]
            p, _ = _attn_probs(qs, k_ops, bias, prev, sinks_ref)
            o_tiles = []
            for g in range(ATT_KV_HEADS):
                p_cat = jnp.concatenate([_unfold(p[2 * g].astype(BF16), prev), _unfold(p[2 * g + 1].astype(BF16), prev)],
                                        axis=0)
                o = _dot_tn(p_cat, jnp.concatenate(v_ops[g], axis=0))
                o_tiles += [o[0:BLK], o[BLK:]]
            o_att = jnp.concatenate(o_tiles, axis=1)
            oatt_ref[rows, :] = o_att
            out = [o_att * (az * _sigmoid(az))]

            qr, kr, qd, kd = _retention_operands(p_ref, rows, cos_ref[rows, :], sin_ref[rows, :],
                                                 qdec_ref[...], kdec_ref[...])
            heads = [(t, hh) for t in range(PAIRS) for hh in range(2)]
            sc = [_dot_nt(jnp.concatenate(_split_heads(_tile(qr, t)), axis=0).astype(BF16), _tile(kr, t).astype(BF16))
                  * din_ref[t] for t in range(PAIRS)]
            qd_heads = [_split_heads(_tile(qd, t)) for t in range(PAIRS)]
            state_b = [states[t].astype(BF16) for t in range(PAIRS)]
            vs = [_proj_cols(p_ref, rows, OFF_RV + h * RET_V_DIM, OFF_RV + (h + 1) * RET_V_DIM) for h in range(RET_HEADS)]
            rzs = [_proj_cols(p_ref, rows, OFF_RZ + h * RET_V_DIM, OFF_RZ + (h + 1) * RET_V_DIM).astype(F32)
                   for h in range(RET_HEADS)]
            lhs = [jnp.concatenate([sc[t][hh * BLK:(hh + 1) * BLK].astype(BF16), qd_heads[t][hh].astype(BF16)], axis=1)
                   for t, hh in heads]
            ons, _ = _group_norm_all([_dot(lhs[2 * t + hh], jnp.concatenate([vs[2 * t + hh], state_b[t]], axis=0))
                                      for t, hh in heads])
            out += [(ons[h] * gain_ref[:, h * RET_V_DIM:(h + 1) * RET_V_DIM]) * (rzs[h] * _sigmoid(rzs[h]))
                    for h in range(RET_HEADS)]
            new_states = [states[t] * cd_ref[t]
                          + _dot_tn(jnp.concatenate(_split_heads(_tile(kd, t)), axis=0).astype(BF16),
                                    jnp.concatenate([vs[2 * t], vs[2 * t + 1]], axis=0)) for t in range(PAIRS)]
            mix_ref[rows, :] = jnp.concatenate(out, axis=1).astype(BF16)
            st_ref[j] = jnp.stack(states)
            return ak, av, tuple(new_states)

        carry = (kprev_ref[...], vprev_ref[...], tuple(state_ref[t] for t in range(PAIRS)))
        kp, vp, states = lax.fori_loop(0, nsub, sub, carry)
        kprev_ref[...] = kp
        vprev_ref[...] = vp
        state_ref[...] = jnp.stack(states)

        xo = x_ref[...] + _dot(mix_ref[...], wout_ref[...])
        r2 = lax.rsqrt(jnp.mean(xo * xo, axis=1, keepdims=True) + RMS_EPS)
        xn = xo * r2
        err = xn * fg_ref[...] - t_ref[...]
        loss_ref[...] += jnp.sum(err * err) * (0.5 / D_MODEL)
        dy = err * (1.0 / D_MODEL)
        gfin_ref[...] += jnp.sum(dy * xn, axis=0, keepdims=True)
        u = dy * fg_ref[...]
        dxo_ref[...] = r2 * u - xn * (r2 * jnp.mean(u * xn, axis=1, keepdims=True))

    blk_rows = lambda w: pl.BlockSpec((tb, w), lambda i: (i, 0))
    state_shape = (PAIRS, 2 * RET_QK_DIM, RET_V_DIM)
    return _call(
        body, name="mix_fwd", grid=(seq // tb,),
        out_shape=(
            jax.ShapeDtypeStruct((seq, MIX_WIDTH), BF16),
            jax.ShapeDtypeStruct((seq, D_MODEL), F32),
            jax.ShapeDtypeStruct((seq // BLK,) + state_shape, F32),
            jax.ShapeDtypeStruct((8, LANE), F32),
            jax.ShapeDtypeStruct((1, D_MODEL), F32),
            jax.ShapeDtypeStruct((seq, ATT_WIDTH), F32),
        ),
        in_specs=[
            pl.BlockSpec((N_CHIPS, tb, SHARD_PAD), lambda i: (0, i, 0)),
            blk_rows(D_MODEL), blk_rows(D_MODEL), blk_rows(LANE), blk_rows(LANE),
            _const_spec((MIX_WIDTH, D_MODEL)), _const_spec((1, D_MODEL)), _const_spec((1, RET_WIDTH)),
            pl.BlockSpec(memory_space=pltpu.SMEM),
            _const_spec((PAIRS, 2 * BLK, BLK)), _const_spec((BLK, RET_QK_WIDTH)), _const_spec((BLK, RET_QK_WIDTH)),
            _const_spec(state_shape), _const_spec((2, BLK, 2 * BLK)),
        ],
        out_specs=(
            blk_rows(MIX_WIDTH), blk_rows(D_MODEL),
            pl.BlockSpec((nsub,) + state_shape, lambda i: (i, 0, 0, 0)),
            _const_spec((8, LANE)), _const_spec((1, D_MODEL)), blk_rows(ATT_WIDTH),
        ),
        scratch_shapes=[
            pltpu.VMEM((BLK, ATT_KV_WIDTH), F32), pltpu.VMEM((BLK, ATT_KV_WIDTH), F32),
            pltpu.VMEM(state_shape, F32),
        ],
        compiler_params=_params(48, ("arbitrary",)),
    )(proj, x, target, cos_t, sin_t, w_out, final_g, gn_gain, sinks, decay_in, qdec_t, kdec_t, cd_t, bias_t)


def _mix_bwd(proj, dxo, mix, o_att, states, x, w_out, w_in_t, norm_g, gn_gain, sinks, tables, tb):
    seq = dxo.shape[0]
    nsub = tb // BLK
    nblk = seq // tb
    cos_t, sin_t, decay_in, qdec_t, kdec_t, cd_t, bias_t = tables
    kv_cols = OFF_AK // (2 * ATT_KV_WIDTH)
    state_shape = (PAIRS, 2 * RET_QK_DIM, RET_V_DIM)

    def body(p_ref, pkv_ref, dxo_ref, mix_ref, oatt_ref, st_ref, cos_ref, sin_ref, x_ref, wout_ref, win_ref, g_ref,
             gain_ref, sinks_ref, din_ref, qdec_ref, kdec_ref, cd_ref, bias_ref,
             dp_ref, gx_ref, gwout_ref, gnorm_ref, dgain_ref, dsink_ref,
             dmix_ref, kv_ref, dkc_ref, dvc_ref, gst_ref):
        i = pl.program_id(0)
        blk = nblk - 1 - i

        @pl.when(i == 0)
        def _():
            gwout_ref[...] = jnp.zeros_like(gwout_ref)
            gnorm_ref[...] = jnp.zeros_like(gnorm_ref)
            dgain_ref[...] = jnp.zeros_like(dgain_ref)
            dsink_ref[...] = jnp.zeros_like(dsink_ref)
            dkc_ref[...] = jnp.zeros_like(dkc_ref)
            dvc_ref[...] = jnp.zeros_like(dvc_ref)
            gst_ref[...] = jnp.zeros_like(gst_ref)

        dxo_b = dxo_ref[...].astype(BF16)
        dmix_ref[...] = _dot_nt(dxo_b, wout_ref[...])
        gwout_ref[...] += _dot_tn(mix_ref[...], dxo_b)
        kv_ref[0:BLK, :] = pkv_ref[...].astype(F32)
        kv_ref[BLK:, :] = _proj_cols(p_ref, slice(None), OFF_AK, OFF_AK + 2 * ATT_KV_WIDTH).astype(F32)
        low = _low_lanes((BLK, LANE))
        low2 = _low_lanes((2 * BLK, LANE))
        lane = lax.broadcasted_iota(jnp.int32, (1, LANE), 1)
        prev = _in_previous_block()

        def sub(jj, carry):
            dkc, dvc, gsts, dgain, dsink = carry
            j = nsub - 1 - jj
            rows = pl.ds(pl.multiple_of(j * BLK, BLK), BLK)
            both = pl.ds(pl.multiple_of(j * BLK, BLK), 2 * BLK)
            bias = bias_ref[jnp.where(jnp.logical_or(blk > 0, j > 0), 1, 0)]

            aq = _proj_cols(p_ref, rows, OFF_AQ, OFF_AQ + ATT_WIDTH)
            az = _proj_cols(p_ref, rows, OFF_AZ, OFF_AZ + ATT_WIDTH).astype(F32)
            k_ops = _kv_operands(kv_ref[both, 0:ATT_KV_WIDTH])
            v_ops = _kv_operands(kv_ref[both, ATT_KV_WIDTH:2 * ATT_KV_WIDTH])
            da = dmix_ref[rows, 0:ATT_WIDTH]
            sig = _sigmoid(az)
            d_o = da * (az * sig)
            qs = [(_stack_tiles(aq, 2 * g) * ATT_SCALE).astype(BF16) for g in range(ATT_KV_HEADS)]
            dos = [_stack_tiles(d_o, 2 * g).astype(BF16) for g in range(ATT_KV_HEADS)]
            p, p_sink = _attn_probs(qs, k_ops, bias, prev, sinks_ref)
            dpr = [_fold(_dot_nt(v_ops[g][hi], dos[g]), prev) for g, hi in ATT_PROBLEMS]
            delta = [jnp.sum(pi * di, axis=0, keepdims=True) for pi, di in zip(p, dpr)]
            ds = [_unfold((pi * (di - ti)).astype(BF16), prev) for pi, di, ti in zip(p, dpr, delta)]
            for (g, hi), ki, ti in zip(ATT_PROBLEMS, p_sink, delta):
                sink_part = ki * ti
                for half in range(2):
                    tot = jnp.sum(sink_part[:, half * BLK:(half + 1) * BLK], axis=1, keepdims=True)
                    dsink = dsink - jnp.where(lane == 4 * g + 2 * half + hi, tot, 0.0)
            dq_tiles, dk_sums, dv_sums = [], [], []
            for g in range(ATT_KV_HEADS):
                ds_cat = jnp.concatenate([ds[2 * g], ds[2 * g + 1]], axis=0)
                p_cat = jnp.concatenate([_unfold(p[2 * g].astype(BF16), prev), _unfold(p[2 * g + 1].astype(BF16), prev)],
                                        axis=0)
                dqs = _dot_tn(ds_cat, jnp.concatenate(k_ops[g], axis=0)) * ATT_SCALE
                dq_tiles += [dqs[0:BLK], dqs[BLK:]]
                dk_sums.append(_dot(ds_cat, qs[g]))
                dv_sums.append(_dot(p_cat, dos[g]))
            daz = da * oatt_ref[rows, :] * (sig * (1.0 + az * (1.0 - sig)))

            def kv_grad(sums):
                (a0, b0), (a1, b1) = [(s[0:2 * BLK], s[2 * BLK:]) for s in sums]
                return jnp.where(low2, a0, b1) + pltpu.roll(jnp.where(low2, a1, b0), HALF_LANE, 1)

            dk_both, dv_both = kv_grad(dk_sums), kv_grad(dv_sums)
            dak = dk_both[BLK:] + dkc
            dav = dv_both[BLK:] + dvc

            cos_b, sin_b = cos_ref[rows, :], sin_ref[rows, :]
            qdec, kdec = qdec_ref[...], kdec_ref[...]
            qr, kr, qd, kd = _retention_operands(p_ref, rows, cos_b, sin_b, qdec, kdec)
            heads = [(t, hh) for t in range(PAIRS) for hh in range(2)]
            head_cols = [slice(h * RET_V_DIM, (h + 1) * RET_V_DIM) for h in range(RET_HEADS)]
            q_rows = [jnp.concatenate(_split_heads(_tile(qr, t)), axis=0).astype(BF16) for t in range(PAIRS)]
            k_rows = [jnp.concatenate(_split_heads(_tile(kr, t)), axis=0).astype(BF16) for t in range(PAIRS)]
            din = [din_ref[t] for t in range(PAIRS)]
            sc = [(_dot_nt(q_rows[t], _tile(kr, t).astype(BF16)) * din[t]).astype(BF16) for t in range(PAIRS)]
            qd_heads = [_split_heads(_tile(qd, t)) for t in range(PAIRS)]
            kd_heads = [_split_heads(_tile(kd, t)) for t in range(PAIRS)]
            state_b = [st_ref[j, t].astype(BF16) for t in range(PAIRS)]
            gst_b = [gsts[t].astype(BF16) for t in range(PAIRS)]
            vs = [_proj_cols(p_ref, rows, OFF_RV + h * RET_V_DIM, OFF_RV + (h + 1) * RET_V_DIM) for h in range(RET_HEADS)]
            rzs = [_proj_cols(p_ref, rows, OFF_RZ + h * RET_V_DIM, OFF_RZ + (h + 1) * RET_V_DIM).astype(F32)
                   for h in range(RET_HEADS)]
            drs = [dmix_ref[rows, ATT_WIDTH + h * RET_V_DIM:ATT_WIDTH + (h + 1) * RET_V_DIM] for h in range(RET_HEADS)]
            gains = [gain_ref[:, c] for c in head_cols]
            lhs = [jnp.concatenate([sc[t][hh * BLK:(hh + 1) * BLK], qd_heads[t][hh].astype(BF16)], axis=1) for t, hh in heads]
            rhs = [jnp.concatenate([vs[2 * t + hh], state_b[t]], axis=0) for t, hh in heads]
            ons, rstds = _group_norm_all([_dot(l, r) for l, r in zip(lhs, rhs)])
            sig_r = [_sigmoid(z) for z in rzs]
            dgn = [d * (z * g) for d, z, g in zip(drs, rzs, sig_r)]
            dz_parts = [d * (o * gn) * (g * (1.0 + z * (1.0 - g))) for d, o, gn, g, z in zip(drs, ons, gains, sig_r, rzs)]
            dgain_parts = [jnp.sum(d * o, axis=0, keepdims=True) for d, o in zip(dgn, ons)]
            don = [d * gn for d, gn in zip(dgn, gains)]
            mean_don = [jnp.mean(d, axis=1, keepdims=True) for d in don]
            mean_don_on = [jnp.mean(d * o, axis=1, keepdims=True) for d, o in zip(don, ons)]
            dob = [(r * (d - a - o * b)).astype(BF16) for r, d, a, o, b in zip(rstds, don, mean_don, ons, mean_don_on)]
            dlhs = [_dot_nt(d, r) for d, r in zip(dob, rhs)]
            drhs = [_dot_tn(l, d) for l, d in zip(lhs, dob)]
            dkds = [_dot_nt(vs[2 * t + hh], gst_b[t]) for t, hh in heads]
            dv_parts = [drhs[2 * t + hh][0:BLK] + _dot(kd_heads[t][hh].astype(BF16), gst_b[t]) for t, hh in heads]
            das = [(dlhs[2 * t + hh][:, 0:BLK] * din[t][hh * BLK:(hh + 1) * BLK]).astype(BF16) for t, hh in heads]
            new_gsts = [gsts[t] * cd_ref[t] + drhs[2 * t][BLK:] + drhs[2 * t + 1][BLK:] for t in range(PAIRS)]
            dq_parts = [_dot(jnp.concatenate([das[2 * t], das[2 * t + 1]], axis=1), k_rows[t])
                        + jnp.where(low, dlhs[2 * t][:, BLK:], dlhs[2 * t + 1][:, BLK:]) * _tile(qdec, t)
                        for t in range(PAIRS)]
            dk_parts = [_dot_tn(jnp.concatenate([das[2 * t], das[2 * t + 1]], axis=0), q_rows[t])
                        + jnp.where(low, dkds[2 * t], dkds[2 * t + 1]) * _tile(kdec, t) for t in range(PAIRS)]
            drq = _rotate_transposed(jnp.concatenate(dq_parts, axis=1), cos_b, sin_b)
            drk = _rotate_transposed(jnp.concatenate(dk_parts, axis=1) * RET_SCALE, cos_b, sin_b)

            dp_ref[rows, :] = jnp.concatenate(
                [jnp.concatenate(dq_tiles, axis=1), dak, dav, daz, drq, drk] + dv_parts + dz_parts, axis=1).astype(BF16)
            dgain = dgain + jnp.concatenate(dgain_parts, axis=1)
            return dk_both[0:BLK], dv_both[0:BLK], tuple(new_gsts), dgain, dsink

        carry = (dkc_ref[...], dvc_ref[...], tuple(gst_ref[t] for t in range(PAIRS)), dgain_ref[...], dsink_ref[...])
        dkc, dvc, gsts, dgain, dsink = lax.fori_loop(0, nsub, sub, carry)
        dkc_ref[...] = dkc
        dvc_ref[...] = dvc
        gst_ref[...] = jnp.stack(gsts)
        dgain_ref[...] = dgain
        dsink_ref[...] = dsink

        dh = _dot(dp_ref[...], win_ref[...])
        xv = x_ref[...]
        r = lax.rsqrt(jnp.mean(xv * xv, axis=1, keepdims=True) + RMS_EPS)
        xn = xv * r
        gnorm_ref[...] += jnp.sum(dh * xn, axis=0, keepdims=True)
        u = dh * g_ref[...]
        gx_ref[...] = dxo_ref[...] + r * u - xn * (r * jnp.mean(u * xn, axis=1, keepdims=True))

    rev_rows = lambda w: pl.BlockSpec((tb, w), lambda i: (nblk - 1 - i, 0))
    prev_kv = pl.BlockSpec((None, BLK, 2 * ATT_KV_WIDTH),
                           lambda i: (0, jnp.maximum((nblk - 1 - i) * nsub - 1, 0), kv_cols))
    return _call(
        body, name="mix_bwd", grid=(nblk,),
        out_shape=(
            jax.ShapeDtypeStruct((seq, IN_WIDTH), BF16),
            jax.ShapeDtypeStruct((seq, D_MODEL), F32),
            jax.ShapeDtypeStruct((MIX_WIDTH, D_MODEL), F32),
            jax.ShapeDtypeStruct((1, D_MODEL), F32),
            jax.ShapeDtypeStruct((1, RET_WIDTH), F32),
            jax.ShapeDtypeStruct((1, LANE), F32),
        ),
        in_specs=[
            pl.BlockSpec((N_CHIPS, tb, SHARD_PAD), lambda i: (0, nblk - 1 - i, 0)),
            prev_kv, rev_rows(D_MODEL), rev_rows(MIX_WIDTH), rev_rows(ATT_WIDTH),
            pl.BlockSpec((nsub,) + state_shape, lambda i: (nblk - 1 - i, 0, 0, 0)),
            rev_rows(LANE), rev_rows(LANE), rev_rows(D_MODEL),
            _const_spec((MIX_WIDTH, D_MODEL)), _const_spec((IN_WIDTH, D_MODEL)), _const_spec((1, D_MODEL)),
            _const_spec((1, RET_WIDTH)),
            pl.BlockSpec(memory_space=pltpu.SMEM),
            _const_spec((PAIRS, 2 * BLK, BLK)), _const_spec((BLK, RET_QK_WIDTH)), _const_spec((BLK, RET_QK_WIDTH)),
            _const_spec(state_shape), _const_spec((2, BLK, 2 * BLK)),
        ],
        out_specs=(
            rev_rows(IN_WIDTH), rev_rows(D_MODEL), _const_spec((MIX_WIDTH, D_MODEL)), _const_spec((1, D_MODEL)),
            _const_spec((1, RET_WIDTH)), _const_spec((1, LANE)),
        ),
        scratch_shapes=[
            pltpu.VMEM((tb, MIX_WIDTH), F32),
            pltpu.VMEM((tb + BLK, 2 * ATT_KV_WIDTH), F32),
            pltpu.VMEM((BLK, ATT_KV_WIDTH), F32), pltpu.VMEM((BLK, ATT_KV_WIDTH), F32),
            pltpu.VMEM(state_shape, F32),
        ],
        compiler_params=_params(60, ("arbitrary",)),
    )(proj, proj, dxo, mix, o_att, states, cos_t, sin_t, x, w_out, w_in_t, norm_g, gn_gain, sinks, decay_in, qdec_t,
      kdec_t, cd_t, bias_t)


def _gw_in_reduce(h_t, dproj, gw_out, small, tb):
    seq = dproj.shape[0]
    nblk = seq // tb
    last = nblk - 1
    hand_on = min(1, last)
    half = D_MODEL // 2
    A, B, C, N_SEMS = 0, N_CHIPS, 2 * N_CHIPS, 2 * N_CHIPS + 1

    def body(win_ref, ht_ref, dp_ref, gwo_hbm, s0_ref, s1_ref, s2_ref, s3_ref, s4_ref, out_ref, fout_ref, packsum_ref,
             acc, sib, send_buf, b_in, fin, fout, mine_out, sib_out, send_out, b_out, pack_ref, packs,
             send_sems, recv_sems, local_sems):
        p, i = pl.program_id(0), pl.program_id(1)
        small_start, small_hand_on, small_finish = _small_exchange(
            gwo_hbm, (s0_ref, s1_ref, s2_ref, s3_ref, s4_ref), fout, mine_out, sib_out, send_out, b_out, pack_ref,
            packs, send_sems, recv_sems, local_sems, N_SEMS)

        @pl.when(jnp.logical_and(p == 0, i == 0))
        def _():
            small_start()

        @pl.when(jnp.logical_and(p == 0, i == hand_on))
        def _():
            small_hand_on()

        x, y, c = lax.axis_index("x"), lax.axis_index("y"), lax.axis_index("c")
        chip = 2 * x + y
        sibling = (x, y, 1 - c)
        mine = pl.ds(pl.multiple_of(c * half, half), half)
        other = pl.ds(pl.multiple_of((1 - c) * half, half), half)

        def remote(src, dst, send_k, recv_k, to):
            return pltpu.make_async_remote_copy(src_ref=src, dst_ref=dst, send_sem=send_sems.at[send_k],
                                                recv_sem=recv_sems.at[recv_k], device_id=to, device_id_type=MESH_ID)

        part = _dot(ht_ref[...], dp_ref[...])
        slot = p % 2

        @pl.when(i == 0)
        def _():
            acc[slot] = part

        @pl.when(i > 0)
        def _():
            acc[slot] += part

        for q in range(N_CHIPS):
            s = q % 2
            to_sibling = remote(acc.at[s, other, :], sib.at[s], A + q, A + q, sibling)

            @pl.when(jnp.logical_and(p == q, i == last))
            def _():
                to_sibling.start()

            if q < N_CHIPS - 1:
                dest = (chip + 1 + q) % N_CHIPS

                @pl.when(jnp.logical_and(p == q + 1, i == hand_on))
                def _():
                    to_sibling.wait_recv()
                    send_buf[q] = (acc[s, mine, :] + sib[s]).astype(BF16)
                    remote(send_buf.at[q], b_in.at[chip], B + q, B + chip, (dest // 2, dest % 2, c)).start()
                    to_sibling.wait_send()
            else:
                @pl.when(jnp.logical_and(p == q, i == last))
                def _():
                    to_sibling.wait_recv()
                    fin[mine, :] = acc[s, mine, :] + sib[s]
                    for j in range(N_CHIPS):
                        @pl.when(j != chip)
                        def _():
                            remote(b_in.at[j], b_in.at[j], B + j, B + j, sibling).wait_recv()
                            fin[mine, :] += b_in[j].astype(F32)
                    to_core = remote(fin.at[mine, :], fin.at[mine, :], C, C, sibling)
                    to_core.start()
                    remote(fin.at[other, :], fin.at[other, :], C, C, sibling).wait_recv()
                    out_ref[...] = fin[...]
                    to_core.wait_send()
                    to_sibling.wait_send()
                    for k in range(N_CHIPS - 1):
                        remote(send_buf.at[k], b_in.at[chip], B + k, B + k, sibling).wait_send()
                    packsum_ref[...] = small_finish()
                    fout_ref[...] = fout[...]

    whole = lambda shape: pl.BlockSpec(shape, lambda p, i, win: (0,) * len(shape), pipeline_mode=pl.Buffered(1))
    grid_spec = pltpu.PrefetchScalarGridSpec(
        num_scalar_prefetch=1, grid=(N_CHIPS, nblk),
        in_specs=[pl.BlockSpec((D_MODEL, tb), lambda p, i, win: (0, i)),
                  pl.BlockSpec((pl.Element(tb), pl.Element(SHARD_PAD)),
                               lambda p, i, win: (i * tb, pl.multiple_of(win[p] * LANE, LANE))),
                  pl.BlockSpec(memory_space=pl.ANY)] + [whole(s.shape) for s in small],
        out_specs=(whole((D_MODEL, SHARD_PAD)), whole((SHARD_OUT, D_MODEL)), whole((PACK_ROWS, D_MODEL))),
        scratch_shapes=[
            pltpu.VMEM((2, D_MODEL, SHARD_PAD), F32), pltpu.VMEM((2, half, SHARD_PAD), F32),
            pltpu.VMEM((N_CHIPS - 1, half, SHARD_PAD), BF16), pltpu.VMEM((N_CHIPS, half, SHARD_PAD), BF16),
            pltpu.VMEM((D_MODEL, SHARD_PAD), F32),
        ] + _small_exchange_scratch() + [
            pltpu.SemaphoreType.DMA((N_SEMS + SMALL_SEMS,)), pltpu.SemaphoreType.DMA((N_SEMS + SMALL_SEMS,)),
            pltpu.SemaphoreType.DMA((N_CHIPS,)),
        ])
    chip = 2 * lax.axis_index("x") + lax.axis_index("y")
    owner = (chip + 1 + jnp.arange(N_CHIPS, dtype=jnp.int32)) % N_CHIPS
    win_start = (owner * SHARD_IN) // LANE
    return _call(
        body, name="gw_in_reduce", grid_spec=grid_spec,
        out_shape=(jax.ShapeDtypeStruct((D_MODEL, SHARD_PAD), F32), jax.ShapeDtypeStruct((SHARD_OUT, D_MODEL), F32),
                   jax.ShapeDtypeStruct((PACK_ROWS, D_MODEL), F32)),
        compiler_params=_params(52, ("arbitrary", "arbitrary")),
    )(win_start.astype(jnp.int32), h_t, dproj, gw_out, *small)


SMALL_SEMS = 17


def _small_exchange(gwo_hbm, small_refs, fout_ref, mine_out, sib_out, send_out, b_out, pack_ref, packs,
                    send_sems, recv_sems, local_sems, base):
    half_out = SHARD_OUT // 2
    A_OUT, B_OUT, C_OUT, PACK = base, base + 4, base + 8, base + 9
    assert len(small_refs) == PACK_PARTS
    x, y, c = lax.axis_index("x"), lax.axis_index("y"), lax.axis_index("c")
    chip = 2 * x + y
    dev = 2 * chip + c
    sibling = (x, y, 1 - c)

    def remote(src, dst, send_k, recv_k, to):
        return pltpu.make_async_remote_copy(src_ref=src, dst_ref=dst, send_sem=send_sems.at[send_k],
                                            recv_sem=recv_sems.at[recv_k], device_id=to, device_id_type=MESH_ID)

    def out_rows(j, core):
        return pl.ds(pl.multiple_of(j * SHARD_OUT + core * half_out, half_out), half_out)

    my_out_rows = pl.ds(pl.multiple_of(c * half_out, half_out), half_out)
    local = [pltpu.make_async_copy(gwo_hbm.at[out_rows(j, c), :], mine_out.at[j], local_sems.at[j])
             for j in range(N_CHIPS)]
    stage_a = [remote(gwo_hbm.at[out_rows(j, 1 - c), :], sib_out.at[j], A_OUT + j, A_OUT + j, sibling)
               for j in range(N_CHIPS)]
    mine_half_out = fout_ref.at[my_out_rows, :]
    stage_c = [remote(mine_half_out, mine_half_out, C_OUT, C_OUT, sibling)]

    def start():
        pack_ref[...] = jnp.zeros_like(pack_ref)
        for k, s_ref in enumerate(small_refs):
            pack_ref[k:k + 1, 0:s_ref.shape[1]] = s_ref[0:1, :]
        packs[dev] = pack_ref[...]
        for d in range(N_DEV):
            to = (d // 4, (d // 2) % 2, d % 2)

            @pl.when(d != dev)
            def _():
                remote(pack_ref, packs.at[dev], PACK + d, PACK + dev, to).start()

        for cp in local + stage_a:
            cp.start()

    def hand_on():
        for cp in local:
            cp.wait()
        for cp in stage_a:
            cp.wait_recv()
        for j in range(N_CHIPS):
            mine_out[j] = mine_out[j] + sib_out[j]

        for j in range(N_CHIPS):
            to = (j // 2, j % 2, c)

            @pl.when(j != chip)
            def _():
                send_out[j] = mine_out[j].astype(BF16)
                remote(send_out.at[j], b_out.at[chip], B_OUT + j, B_OUT + chip, to).start()

            @pl.when(j == chip)
            def _():
                fout_ref[my_out_rows, :] = mine_out[j]

    def finish():
        for j in range(N_CHIPS):
            @pl.when(j != chip)
            def _():
                remote(b_out.at[j], b_out.at[j], B_OUT + j, B_OUT + j, sibling).wait_recv()
                fout_ref[my_out_rows, :] += b_out[j].astype(F32)

        for cp in stage_c:
            cp.start()
        other_half_out = fout_ref.at[pl.ds(pl.multiple_of((1 - c) * half_out, half_out), half_out), :]
        remote(other_half_out, other_half_out, C_OUT, C_OUT, sibling).wait_recv()

        for d in range(N_DEV):
            @pl.when(d != dev)
            def _():
                remote(pack_ref, packs.at[d], PACK + d, PACK + d, sibling).wait_recv()
        total = packs[0]
        for d in range(1, N_DEV):
            total = total + packs[d]

        for cp in stage_a + stage_c:
            cp.wait_send()
        for j in range(N_CHIPS):
            @pl.when(j != chip)
            def _():
                remote(b_out.at[j], b_out.at[j], B_OUT + j, B_OUT + j, sibling).wait_send()
        for d in range(N_DEV):
            @pl.when(d != dev)
            def _():
                remote(pack_ref, packs.at[d], PACK + d, PACK + d, sibling).wait_send()
        return total

    return start, hand_on, finish


def _small_exchange_scratch():
    half_out = SHARD_OUT // 2
    return [
        pltpu.VMEM((SHARD_OUT, D_MODEL), F32),
        pltpu.VMEM((N_CHIPS, half_out, D_MODEL), F32), pltpu.VMEM((N_CHIPS, half_out, D_MODEL), F32),
        pltpu.VMEM((N_CHIPS, half_out, D_MODEL), BF16), pltpu.VMEM((N_CHIPS, half_out, D_MODEL), BF16),
        pltpu.VMEM((PACK_ROWS, D_MODEL), F32), pltpu.VMEM((N_DEV, PACK_ROWS, D_MODEL), F32),
    ]


def _adam_math(w, g, m, v):
    mn = ADAM_B1 * m + (1.0 - ADAM_B1) * g
    vn = ADAM_B2 * v + (1.0 - ADAM_B2) * (g * g)
    m_hat = mn / (1.0 - ADAM_B1 ** ADAM_STEP)
    v_hat = vn / (1.0 - ADAM_B2 ** ADAM_STEP)
    return -ADAM_LR * (m_hat / (jnp.sqrt(v_hat) + ADAM_EPS) + ADAM_WD * w), mn, vn


def _adamw(name, w, g, m, v, tb):
    rows, cols = w.shape

    def body(w_ref, g_ref, m_ref, v_ref, go_ref, d_ref, mo_ref, vo_ref):
        gv = g_ref[...]
        go_ref[...] = gv
        d_ref[...], mo_ref[...], vo_ref[...] = _adam_math(w_ref[...], gv, m_ref[...], v_ref[...])

    spec = pl.BlockSpec((tb, cols), lambda i: (i, 0))
    shape = jax.ShapeDtypeStruct((rows, cols), F32)
    return _call(
        body, name=name, grid=(rows // tb,), out_shape=(shape,) * 4,
        in_specs=[spec] * 4, out_specs=(spec,) * 4,
        compiler_params=_params(32, ("arbitrary",)),
    )(w, g, m, v)


def _adamw_w_in(w_t, g_window, m_t, v_t, tb):
    def body(w_ref, g_ref, m_ref, v_ref, go_ref, d_ref, mo_ref, vo_ref, gt_ref):
        gt_ref[...] = g_ref[...].T
        gv = gt_ref[pl.ds(pl.multiple_of(lax.axis_index("y") * SHARD_SHIFT, SHARD_SHIFT), SHARD_IN), :]
        go_ref[...] = gv
        d_ref[...], mo_ref[...], vo_ref[...] = _adam_math(w_ref[...], gv, m_ref[...], v_ref[...])

    spec = pl.BlockSpec((SHARD_IN, tb), lambda i: (0, i))
    shape = jax.ShapeDtypeStruct((SHARD_IN, D_MODEL), F32)
    return _call(
        body, name="adamw_w_in", grid=(D_MODEL // tb,), out_shape=(shape,) * 4,
        in_specs=[spec, pl.BlockSpec((tb, SHARD_PAD), lambda i: (i, 0)), spec, spec], out_specs=(spec,) * 4,
        scratch_shapes=[pltpu.VMEM((SHARD_PAD, tb), F32)],
        compiler_params=_params(32, ("arbitrary",)),
    )(w_t, g_window, m_t, v_t)


def _adamw_small(sums, params):
    def body(sums_ref, *refs):
        ins, outs = refs[:3 * len(params)], refs[3 * len(params):]
        for k in range(len(params)):
            w_ref, m_ref, v_ref = ins[3 * k:3 * k + 3]
            g = sums_ref[k:k + 1, 0:w_ref.shape[1]]
            go_ref, d_ref, mo_ref, vo_ref = outs[4 * k:4 * k + 4]
            go_ref[...] = g
            d_ref[...], mo_ref[...], vo_ref[...] = _adam_math(w_ref[...], g, m_ref[...], v_ref[...])

    vmem = pl.BlockSpec(memory_space=pltpu.VMEM)
    flat = [a for p in params for a in p]
    shapes = tuple(jax.ShapeDtypeStruct(p[0].shape, F32) for p in params for _ in range(4))
    res = _call(body, name="adamw_small", out_shape=shapes, in_specs=[vmem] * (1 + len(flat)),
                out_specs=(vmem,) * len(shapes), compiler_params=_params(16))(sums, *flat)
    return [res[4 * k:4 * k + 4] for k in range(len(params))]


def kernel(x, norm_g, w_in, att_sinks, ret_gn_g, w_out, final_g, loss_target, m_norm_g, m_w_in, m_att_sinks, m_ret_gn_g, m_w_out, m_final_g, v_norm_g, v_w_in, v_att_sinks, v_ret_gn_g, v_w_out, v_final_g):
    seq = x.shape[1]
    xs, tgt = x[0], loss_target[0]
    final_g2 = final_g.reshape(1, D_MODEL)
    tables = _tables(seq)

    w_in_t, m_w_in_t, v_w_in_t = w_in[0].T, m_w_in[0].T, v_w_in[0].T
    proj, h_t, w_in_full, w_out_full = _in_proj(xs, norm_g, w_in_t, w_out[0], min(TOKENS_PROJ, seq))
    mix, dxo, states, loss_part, gfin, o_att = _mix_fwd(proj, xs, tgt, w_out_full, final_g2, ret_gn_g, att_sinks,
                                                        tables, min(TOKENS_MIX, seq))
    dproj, grad_x, gw_out, gnorm, dgain, dsink = _mix_bwd(proj, dxo, mix, o_att, states, xs, w_out_full, w_in_full,
                                                          norm_g, ret_gn_g, att_sinks, tables, min(TOKENS_MIX, seq))
    g_in, g_out, sums = _gw_in_reduce(h_t, dproj, gw_out, (gnorm, gfin, dgain, dsink, loss_part),
                                      min(TOKENS_GW, seq))

    res_in = [r.T for r in _adamw_w_in(w_in_t, g_in, m_w_in_t, v_w_in_t, 256)]
    res_out = _adamw("adamw_w_out", w_out[0], g_out, m_w_out[0], v_w_out[0], SHARD_OUT)
    as_row = lambda a: a.reshape(1, D_MODEL)
    r_norm, r_final, r_gain, r_sink = _adamw_small(sums, [
        (norm_g, m_norm_g, v_norm_g), (final_g2, as_row(m_final_g), as_row(v_final_g)),
        (ret_gn_g, m_ret_gn_g, v_ret_gn_g), (att_sinks, m_att_sinks, v_att_sinks)])

    outs = []
    for k in range(4):
        outs += [r_norm[k], res_in[k][None], r_sink[k], r_gain[k], res_out[k][None], r_final[k].reshape(D_MODEL)]
    return (sums[4, 0], grad_x[None], *outs)
```

```python
import jax
import jax.numpy as jnp
import numpy as np
from jax import lax
from jax.experimental import pallas as pl
from jax.experimental.pallas import tpu as pltpu

F32 = jnp.float32
BF16 = jnp.bfloat16

D_MODEL = 1024
ATT_HEADS = 8
ATT_KV_HEADS = 2
ATT_HEAD_DIM = 64
RET_HEADS = 4
RET_QK_DIM = 64
RET_V_DIM = 128
BLK = 128
ROT_BASE = 10000.0
RMS_EPS = 1e-6
GN_EPS = 1e-6
NEG_INF = -1e30
ATT_SCALE = ATT_HEAD_DIM ** -0.5
RET_SCALE = RET_QK_DIM ** -0.5

ATT_WIDTH = ATT_HEADS * ATT_HEAD_DIM
ATT_KV_WIDTH = ATT_KV_HEADS * ATT_HEAD_DIM
RET_QK_WIDTH = RET_HEADS * RET_QK_DIM
RET_WIDTH = RET_HEADS * RET_V_DIM
MIX_WIDTH = ATT_WIDTH + RET_WIDTH
OFF_AQ = 0
OFF_AK = OFF_AQ + ATT_WIDTH
OFF_AV = OFF_AK + ATT_KV_WIDTH
OFF_AZ = OFF_AV + ATT_KV_WIDTH
OFF_RQ = OFF_AZ + ATT_WIDTH
OFF_RK = OFF_RQ + RET_QK_WIDTH
OFF_RV = OFF_RK + RET_QK_WIDTH
OFF_RZ = OFF_RV + RET_WIDTH
IN_WIDTH = OFF_RZ + RET_WIDTH

LANE = 128
BF16_ROWS = 16
HALF_LANE = LANE // 2
PAIRS = RET_QK_WIDTH // LANE
assert ATT_HEAD_DIM == HALF_LANE and RET_QK_DIM == HALF_LANE and RET_V_DIM == LANE and ATT_KV_WIDTH == LANE

N_CHIPS = 4
N_DEV = 8
SHARD_IN = IN_WIDTH // N_CHIPS
SHARD_PAD = 768
SHARD_SHIFT = SHARD_PAD - SHARD_IN
WIN_START = tuple((j * SHARD_IN) // LANE * LANE for j in range(N_CHIPS))
SHARD_OUT = MIX_WIDTH // N_CHIPS

PACK_PARTS = 5
PACK_ROWS = 8
assert PACK_PARTS <= PACK_ROWS

ADAM_LR = 0.001
ADAM_B1 = 0.9
ADAM_B2 = 0.999
ADAM_EPS = 1e-08
ADAM_WD = 0.01
ADAM_STEP = 10

VMEM_CAP = 64 * 1024 * 1024
TOKENS_PROJ = 1024
TOKENS_MIX = 512
TOKENS_GW = 2048
MESH_ID = pl.DeviceIdType.MESH


def _call(body, **kw):
    return pl.pallas_call(body, **kw)


def _params(vmem_mb, semantics=None):
    assert vmem_mb * 1024 * 1024 < VMEM_CAP
    return pltpu.CompilerParams(dimension_semantics=semantics, vmem_limit_bytes=vmem_mb * 1024 * 1024)


def _dot(a, b):
    return jnp.dot(a, b, preferred_element_type=F32)


def _dot_nt(a, b):
    return lax.dot_general(a, b, (((1,), (1,)), ((), ())), preferred_element_type=F32)


def _dot_tn(a, b):
    return lax.dot_general(a, b, (((0,), (0,)), ((), ())), preferred_element_type=F32)


def _sigmoid(z):
    return 1.0 / (1.0 + jnp.exp(-z))


def _const_spec(shape):
    nd = len(shape)
    return pl.BlockSpec(shape, lambda i: (0,) * nd, pipeline_mode=pl.Buffered(1))


def _tables(seq):
    f32 = np.float32
    pos = np.arange(seq, dtype=f32)
    theta = (f32(1.0) / (f32(ROT_BASE) ** np.linspace(0.0, 1.0, RET_QK_DIM // 2, dtype=f32))).astype(f32)
    ang = (pos[:, None] * theta[None, :]).astype(f32)
    cos, sin = np.cos(ang), np.sin(ang)
    cos2 = np.repeat(cos, 2, axis=1)
    sin2 = np.stack([-sin, sin], axis=-1).reshape(seq, RET_QK_DIM)
    cos_t = np.tile(cos2, (1, 2))
    sin_t = np.tile(sin2, (1, 2))

    log_gamma = np.log(f32(1.0) - f32(2.0) ** (f32(-5.0) - np.arange(RET_HEADS, dtype=f32))).astype(f32)
    idx = np.arange(BLK, dtype=f32)
    rel = idx[:, None] - idx[None, :]
    decay_in = np.where(rel >= 0, np.exp(log_gamma[:, None, None] * np.maximum(rel, f32(0.0))), f32(0.0))
    k_dec = np.exp(log_gamma[:, None] * (BLK - 1 - idx)[None, :])
    q_dec = np.exp(log_gamma[:, None] * (idx + 1)[None, :])
    chunk_decay = np.exp(log_gamma * f32(BLK))
    kdec_t = np.repeat(k_dec.T, RET_QK_DIM, axis=1)
    qdec_t = np.repeat(q_dec.T, RET_QK_DIM, axis=1)
    cd_t = np.broadcast_to(chunk_decay[:, None, None], (RET_HEADS, RET_QK_DIM, RET_V_DIM))
    decay_in = decay_in.reshape(PAIRS, 2 * BLK, BLK)
    cd_t = cd_t.reshape(PAIRS, 2 * RET_QK_DIM, RET_V_DIM)

    key = np.arange(BLK)[:, None]
    query = np.arange(2 * BLK)[None, :] % BLK
    bias = np.stack([np.where(key > query, NEG_INF, 0.0), np.zeros((BLK, 2 * BLK))])
    return tuple(jnp.asarray(np.ascontiguousarray(a), F32) for a in (cos_t, sin_t, decay_in, qdec_t, kdec_t, cd_t, bias))


def _low_lanes(shape):
    lane = lax.broadcasted_iota(jnp.int32, shape, len(shape) - 1)
    return (lane & HALF_LANE) == 0


def _split_heads(t):
    low = _low_lanes(t.shape)
    zero = jnp.zeros_like(t)
    return jnp.where(low, t, zero), jnp.where(low, zero, t)


def _swap_pairs(t):
    lane = lax.broadcasted_iota(jnp.int32, t.shape, 1)
    nxt = pltpu.roll(t, t.shape[1] - 1, 1)
    prv = pltpu.roll(t, 1, 1)
    return jnp.where((lane & 1) == 0, nxt, prv)


def _per_tile(fn, t):
    return jnp.concatenate([fn(_tile(t, i)) for i in range(t.shape[1] // LANE)], axis=1)


def _rotate(t, cos_t, sin_t):
    return _per_tile(lambda a: a * cos_t + _swap_pairs(a) * sin_t, t)


def _rotate_transposed(d, cos_t, sin_t):
    return _per_tile(lambda a: a * cos_t + _swap_pairs(a * sin_t), d)


def _kv_operands(cat):
    low = _low_lanes(cat.shape)
    swapped = pltpu.roll(cat, HALF_LANE, 1)
    zero = jnp.zeros_like(cat)
    pick = lambda a, b: jnp.where(low, a, b).astype(BF16)
    return ((pick(cat, zero), pick(zero, swapped)), (pick(swapped, zero), pick(zero, cat)))


def _stack_tiles(t, first_tile):
    a = t[:, first_tile * LANE:(first_tile + 1) * LANE]
    b = t[:, (first_tile + 1) * LANE:(first_tile + 2) * LANE]
    return jnp.concatenate([a, b], axis=0)


def _sink_rows(sinks_ref, group):
    first = lax.broadcasted_iota(jnp.int32, (1, 2 * BLK), 1) < BLK

    def row(h0, h1):
        return jnp.where(first, sinks_ref[0, group * 4 + h0], sinks_ref[0, group * 4 + h1])
    return row(0, 2), row(1, 3)


ATT_PROBLEMS = tuple((g, hi) for g in range(ATT_KV_HEADS) for hi in range(2))


def _in_previous_block():
    key = lax.broadcasted_iota(jnp.int32, (BLK, 2 * BLK), 0)
    query = lax.broadcasted_iota(jnp.int32, (BLK, 2 * BLK), 1) & (BLK - 1)
    return key > query


def _fold(t, prev):
    return jnp.where(prev, t[0:BLK], t[BLK:])


def _unfold(t, prev):
    zero = jnp.zeros_like(t)
    return jnp.concatenate([jnp.where(prev, t, zero), jnp.where(prev, zero, t)], axis=0)


def _attn_probs(qs, k_ops, bias, prev, sinks_ref):
    sink = [_sink_rows(sinks_ref, g)[hi] for g, hi in ATT_PROBLEMS]
    s = [_fold(_dot_nt(k_ops[g][hi], qs[g]), prev) + bias for g, hi in ATT_PROBLEMS]
    m = [jnp.maximum(jnp.max(si, axis=0, keepdims=True), ki) for si, ki in zip(s, sink)]
    e = [jnp.exp(si - mi) for si, mi in zip(s, m)]
    es = [jnp.exp(ki - mi) for ki, mi in zip(sink, m)]
    inv = [1.0 / (jnp.sum(ei, axis=0, keepdims=True) + esi) for ei, esi in zip(e, es)]
    return [ei * ii for ei, ii in zip(e, inv)], [esi * ii for esi, ii in zip(es, inv)]


def _group_norm_all(outs):
    mu = [jnp.mean(o, axis=1, keepdims=True) for o in outs]
    xc = [o - m for o, m in zip(outs, mu)]
    var = [jnp.mean(c * c, axis=1, keepdims=True) for c in xc]
    rstd = [lax.rsqrt(v + GN_EPS) for v in var]
    return [c * r for c, r in zip(xc, rstd)], rstd


def _proj_cols(p_ref, rows, start, stop):
    pieces = []
    while start < stop:
        k = max(j for j in range(N_CHIPS) if WIN_START[j] <= start)
        end = min(stop, WIN_START[k] + SHARD_PAD)
        pieces.append(p_ref[k, rows, start - WIN_START[k]:end - WIN_START[k]])
        start = end
    return pieces[0] if len(pieces) == 1 else jnp.concatenate(pieces, axis=1)


def _retention_operands(p_ref, rows, cos_b, sin_b, qdec, kdec):
    qr = _rotate(_proj_cols(p_ref, rows, OFF_RQ, OFF_RQ + RET_QK_WIDTH).astype(F32), cos_b, sin_b)
    kr = _rotate(_proj_cols(p_ref, rows, OFF_RK, OFF_RK + RET_QK_WIDTH).astype(F32), cos_b, sin_b) * RET_SCALE
    return qr, kr, qr * qdec, kr * kdec


def _tile(t, i):
    return t[:, i * LANE:(i + 1) * LANE]


GATHER_SEMS = 7


def _allgather_steps(src_ref, full_ref, blk_ref, send_sems, recv_sems):
    block_rows = blk_ref.shape[0]
    assert block_rows % BF16_ROWS == 0 and full_ref.shape[0] == N_DEV * block_rows
    x, y, c = lax.axis_index("x"), lax.axis_index("y"), lax.axis_index("c")
    me, sibling = (x, y, c), (x, y, 1 - c)
    chips = [(1 - x, y), (x, 1 - y), (1 - x, 1 - y)]

    def rows(px, py, pc):
        return full_ref.at[pl.ds(pl.multiple_of((4 * px + 2 * py + pc) * block_rows, BF16_ROWS), block_rows), :]

    def copy(k, block, to, src=None):
        return pltpu.make_async_remote_copy(
            src_ref=rows(*block) if src is None else src, dst_ref=rows(*block),
            send_sem=send_sems.at[k], recv_sem=recv_sems.at[k], device_id=to, device_id_type=MESH_ID)

    first = [copy(0, me, sibling, src=blk_ref)] + [copy(1 + j, me, (*chip, c), src=blk_ref) for j, chip in enumerate(chips)]
    passed = [copy(4 + j, (*chip, c), sibling) for j, chip in enumerate(chips)]

    def start():
        blk_ref[...] = src_ref[pl.ds(pl.multiple_of(c * block_rows, BF16_ROWS), block_rows), :].astype(BF16)
        rows(*me)[...] = blk_ref[...]
        for cp in first:
            cp.start()

    def forward():
        for j, chip in enumerate(chips):
            copy(1 + j, (*chip, c), me).wait_recv()
            passed[j].start()

    def finish():
        copy(0, sibling, me).wait_recv()
        for j, chip in enumerate(chips):
            copy(4 + j, (*chip, 1 - c), me).wait_recv()
        for cp in first + passed:
            cp.wait_send()

    return start, forward, finish


WINDOW_HALF = SHARD_PAD // 2
WINDOW_SEMS = GATHER_SEMS + 1


def _window_gather(wi_ref, w4_ref, blk_ref, edge_ref, send_sems, recv_sems):
    x, y, c = lax.axis_index("x"), lax.axis_index("y"), lax.axis_index("c")
    me, sibling, pair = (x, y, c), (x, y, 1 - c), (x, 1 - y, c)
    chips = [(1 - x, y), (x, 1 - y), (1 - x, 1 - y)]
    own_rows, edge_rows = SHARD_IN - WINDOW_HALF, SHARD_SHIFT
    sends_edge = y == c

    def rows(px, py, pc):
        return w4_ref.at[pl.ds(pl.multiple_of((4 * px + 2 * py + pc) * WINDOW_HALF, BF16_ROWS), WINDOW_HALF), :]

    def copy(k, block, to, src=None):
        return pltpu.make_async_remote_copy(
            src_ref=rows(*block) if src is None else src, dst_ref=rows(*block),
            send_sem=send_sems.at[k], recv_sem=recv_sems.at[k], device_id=to, device_id_type=MESH_ID)

    edge_copy = pltpu.make_async_remote_copy(
        src_ref=edge_ref, dst_ref=edge_ref, send_sem=send_sems.at[GATHER_SEMS], recv_sem=recv_sems.at[GATHER_SEMS],
        device_id=pair, device_id_type=MESH_ID)
    first = [copy(0, me, sibling, src=blk_ref)] + [copy(1 + j, me, (*chip, c), src=blk_ref) for j, chip in enumerate(chips)]
    passed = [copy(4 + j, (*chip, c), sibling) for j, chip in enumerate(chips)]

    def at(start, size):
        return pl.ds(pl.multiple_of(start, BF16_ROWS), size)

    def start():
        @pl.when(sends_edge)
        def _():
            edge_ref[...] = wi_ref[at((1 - y) * (SHARD_IN - edge_rows), edge_rows), :].astype(BF16)
            edge_copy.start()
            blk_ref[...] = wi_ref[at(c * own_rows, WINDOW_HALF), :].astype(BF16)

        @pl.when(jnp.logical_not(sends_edge))
        def _():
            blk_ref[at(y * edge_rows, own_rows), :] = wi_ref[at(c * WINDOW_HALF, own_rows), :].astype(BF16)
            edge_copy.wait_recv()
            blk_ref[at((1 - y) * own_rows, edge_rows), :] = edge_ref[...]

        rows(*me)[...] = blk_ref[...]
        for cp in first[:-1]:
            cp.start()
        copy(0, sibling, me).wait_recv()

    def send_far():
        first[-1].start()

    def forward(j):
        copy(1 + j, (*chips[j], c), me).wait_recv()
        passed[j].start()

    def wait_forwarded(j):
        copy(4 + j, (*chips[j], 1 - c), me).wait_recv()

    def finish():
        for cp in first + passed:
            cp.wait_send()

        @pl.when(sends_edge)
        def _():
            edge_copy.wait_send()

    return start, send_far, forward, wait_forwarded, finish


def _in_proj(x, norm_g, w_in_t_shard, w_out_shard, tb):
    seq = x.shape[0]
    nblk = seq // tb
    last = nblk - 1
    chip_of_panel = (None, 1, 0, 2)

    def body(win_ref, x_ref, g_ref, wi_ref, wo_ref, p_ref, ht_ref, wt_ref, wout_ref,
             w4, blk, edge, hbuf, wout_full, blko, send_sems, recv_sems, send_sems_o, recv_sems_o):
        q, i = pl.program_id(0), pl.program_id(1)
        chip = 2 * lax.axis_index("x") + lax.axis_index("y")
        in_start, in_send_far, in_forward, in_wait_forwarded, in_finish = _window_gather(
            wi_ref, w4, blk, edge, send_sems, recv_sems)
        out_start, out_forward, out_finish = _allgather_steps(wo_ref, wout_full, blko, send_sems_o, recv_sems_o)
        rows = pl.ds(pl.multiple_of(i * tb, tb), tb)

        @pl.when(jnp.logical_and(q == 0, i == 0))
        def _():
            in_start()

        @pl.when(jnp.logical_and(q == min(1, N_CHIPS - 1), i == 0))
        def _():
            in_send_far()
            out_start()

        for panel in range(1, N_CHIPS):
            @pl.when(jnp.logical_and(q == panel, i == 0))
            def _():
                in_wait_forwarded(chip_of_panel[panel])

        @pl.when(q == 0)
        def _():
            xv = x_ref[...]
            r = lax.rsqrt(jnp.mean(xv * xv, axis=1, keepdims=True) + RMS_EPS)
            h = (xv * r) * g_ref[...]
            hbuf[rows, :] = h.astype(BF16)
            ht_ref[...] = h.T.astype(BF16)

        owner = jnp.bitwise_xor(chip, q)
        window = w4[pl.ds(pl.multiple_of(owner * SHARD_PAD, SHARD_PAD), SHARD_PAD), :]
        p_ref[...] = _dot_nt(hbuf[rows, :], window).astype(BF16)

        for panel in range(1, N_CHIPS):
            @pl.when(jnp.logical_and(q == panel - 1, i == last))
            def _():
                in_forward(chip_of_panel[panel])

        @pl.when(jnp.logical_and(q == N_CHIPS - 1, i == 0))
        def _():
            out_forward()

        @pl.when(jnp.logical_and(q == N_CHIPS - 1, i == last))
        def _():
            in_finish()
            for k in range(N_CHIPS):
                first = k * SHARD_PAD + (k % 2) * SHARD_SHIFT
                wt_ref[k * SHARD_IN:(k + 1) * SHARD_IN, :] = w4[first:first + SHARD_IN, :]
            out_finish()
            wout_ref[...] = wout_full[...]

    whole = lambda shape: pl.BlockSpec(shape, lambda q, i, win: (0,) * len(shape), pipeline_mode=pl.Buffered(1))
    first_panel = lambda q, i: jnp.where(q == 0, i, last)
    grid_spec = pltpu.PrefetchScalarGridSpec(
        num_scalar_prefetch=1, grid=(N_CHIPS, nblk),
        in_specs=[pl.BlockSpec((tb, D_MODEL), lambda q, i, win: (first_panel(q, i), 0)), whole((1, D_MODEL)),
                  whole((SHARD_IN, D_MODEL)), whole((SHARD_OUT, D_MODEL))],
        out_specs=(pl.BlockSpec((None, tb, SHARD_PAD), lambda q, i, win: (win[q], i, 0)),
                   pl.BlockSpec((D_MODEL, tb), lambda q, i, win: (0, first_panel(q, i))),
                   whole((IN_WIDTH, D_MODEL)), whole((MIX_WIDTH, D_MODEL))),
        scratch_shapes=[
            pltpu.VMEM((N_CHIPS * SHARD_PAD, D_MODEL), BF16), pltpu.VMEM((WINDOW_HALF, D_MODEL), BF16),
            pltpu.VMEM((SHARD_SHIFT, D_MODEL), BF16), pltpu.VMEM((seq, D_MODEL), BF16),
            pltpu.VMEM((MIX_WIDTH, D_MODEL), BF16), pltpu.VMEM((SHARD_OUT // 2, D_MODEL), BF16),
            pltpu.SemaphoreType.DMA((WINDOW_SEMS,)), pltpu.SemaphoreType.DMA((WINDOW_SEMS,)),
            pltpu.SemaphoreType.DMA((GATHER_SEMS,)), pltpu.SemaphoreType.DMA((GATHER_SEMS,)),
        ])
    chip = 2 * lax.axis_index("x") + lax.axis_index("y")
    owner = jnp.bitwise_xor(chip, jnp.arange(N_CHIPS, dtype=jnp.int32))
    return _call(
        body, name="in_proj", grid_spec=grid_spec,
        out_shape=(jax.ShapeDtypeStruct((N_CHIPS, seq, SHARD_PAD), BF16), jax.ShapeDtypeStruct((D_MODEL, seq), BF16),
                   jax.ShapeDtypeStruct((IN_WIDTH, D_MODEL), BF16), jax.ShapeDtypeStruct((MIX_WIDTH, D_MODEL), BF16)),
        compiler_params=_params(60, ("arbitrary", "arbitrary")),
    )(owner.astype(jnp.int32), x, norm_g, w_in_t_shard, w_out_shard)


def _mix_fwd(proj, x, target, w_out, final_g, gn_gain, sinks, tables, tb):
    seq = x.shape[0]
    nsub = tb // BLK
    cos_t, sin_t, decay_in, qdec_t, kdec_t, cd_t, bias_t = tables

    def body(p_ref, x_ref, t_ref, cos_ref, sin_ref, wout_ref, fg_ref, gain_ref, sinks_ref, din_ref, qdec_ref,
             kdec_ref, cd_ref, bias_ref, mix_ref, dxo_ref, st_ref, loss_ref, gfin_ref, oatt_ref,
             kprev_ref, vprev_ref, state_ref):
        i = pl.program_id(0)

        @pl.when(i == 0)
        def _():
            kprev_ref[...] = jnp.zeros_like(kprev_ref)
            vprev_ref[...] = jnp.zeros_like(vprev_ref)
            state_ref[...] = jnp.zeros_like(state_ref)
            loss_ref[...] = jnp.zeros_like(loss_ref)
            gfin_ref[...] = jnp.zeros_like(gfin_ref)

        prev = _in_previous_block()

        def sub(j, carry):
            kp, vp, states = carry
            rows = pl.ds(pl.multiple_of(j * BLK, BLK), BLK)
            bias = bias_ref[jnp.where(jnp.logical_or(i > 0, j > 0), 1, 0)]

            aq = _proj_cols(p_ref, rows, OFF_AQ, OFF_AQ + ATT_WIDTH)
            ak = _proj_cols(p_ref, rows, OFF_AK, OFF_AK + ATT_KV_WIDTH).astype(F32)
            av = _proj_cols(p_ref, rows, OFF_AV, OFF_AV + ATT_KV_WIDTH).astype(F32)
            az = _proj_cols(p_ref, rows, OFF_AZ, OFF_AZ + ATT_WIDTH).astype(F32)
            k_ops = _kv_operands(jnp.concatenate([kp, ak], axis=0))
            v_ops = _kv_operands(jnp.concatenate([vp, av], axis=0))
            qs = [(_stack_tiles(aq, 2 * g) * ATT_SCALE).astype(BF16) for g in range(ATT_KV_HEADS)]
            p, _ = _attn_probs(qs, k_ops, bias, prev, sinks_ref)
            o_tiles = []
            for g in range(ATT_KV_HEADS):
                p_cat = jnp.concatenate([_unfold(p[2 * g].astype(BF16), prev), _unfold(p[2 * g + 1].astype(BF16), prev)],
                                        axis=0)
                o = _dot_tn(p_cat, jnp.concatenate(v_ops[g], axis=0))
                o_tiles += [o[0:BLK], o[BLK:]]
            o_att = jnp.concatenate(o_tiles, axis=1)
            oatt_ref[rows, :] = o_att
            out = [o_att * (az * _sigmoid(az))]

            qr, kr, qd, kd = _retention_operands(p_ref, rows, cos_ref[rows, :], sin_ref[rows, :],
                                                 qdec_ref[...], kdec_ref[...])
            heads = [(t, hh) for t in range(PAIRS) for hh in range(2)]
            sc = [_dot_nt(jnp.concatenate(_split_heads(_tile(qr, t)), axis=0).astype(BF16), _tile(kr, t).astype(BF16))
                  * din_ref[t] for t in range(PAIRS)]
            qd_heads = [_split_heads(_tile(qd, t)) for t in range(PAIRS)]
            state_b = [states[t].astype(BF16) for t in range(PAIRS)]
            vs = [_proj_cols(p_ref, rows, OFF_RV + h * RET_V_DIM, OFF_RV + (h + 1) * RET_V_DIM) for h in range(RET_HEADS)]
            rzs = [_proj_cols(p_ref, rows, OFF_RZ + h * RET_V_DIM, OFF_RZ + (h + 1) * RET_V_DIM).astype(F32)
                   for h in range(RET_HEADS)]
            lhs = [jnp.concatenate([sc[t][hh * BLK:(hh + 1) * BLK].astype(BF16), qd_heads[t][hh].astype(BF16)], axis=1)
                   for t, hh in heads]
            ons, _ = _group_norm_all([_dot(lhs[2 * t + hh], jnp.concatenate([vs[2 * t + hh], state_b[t]], axis=0))
                                      for t, hh in heads])
            out += [(ons[h] * gain_ref[:, h * RET_V_DIM:(h + 1) * RET_V_DIM]) * (rzs[h] * _sigmoid(rzs[h]))
                    for h in range(RET_HEADS)]
            new_states = [states[t] * cd_ref[t]
                          + _dot_tn(jnp.concatenate(_split_heads(_tile(kd, t)), axis=0).astype(BF16),
                                    jnp.concatenate([vs[2 * t], vs[2 * t + 1]], axis=0)) for t in range(PAIRS)]
            mix_ref[rows, :] = jnp.concatenate(out, axis=1).astype(BF16)
            st_ref[j] = jnp.stack(states)
            return ak, av, tuple(new_states)

        carry = (kprev_ref[...], vprev_ref[...], tuple(state_ref[t] for t in range(PAIRS)))
        kp, vp, states = lax.fori_loop(0, nsub, sub, carry)
        kprev_ref[...] = kp
        vprev_ref[...] = vp
        state_ref[...] = jnp.stack(states)

        xo = x_ref[...] + _dot(mix_ref[...], wout_ref[...])
        r2 = lax.rsqrt(jnp.mean(xo * xo, axis=1, keepdims=True) + RMS_EPS)
        xn = xo * r2
        err = xn * fg_ref[...] - t_ref[...]
        loss_ref[...] += jnp.sum(err * err) * (0.5 / D_MODEL)
        dy = err * (1.0 / D_MODEL)
        gfin_ref[...] += jnp.sum(dy * xn, axis=0, keepdims=True)
        u = dy * fg_ref[...]
        dxo_ref[...] = r2 * u - xn * (r2 * jnp.mean(u * xn, axis=1, keepdims=True))

    blk_rows = lambda w: pl.BlockSpec((tb, w), lambda i: (i, 0))
    state_shape = (PAIRS, 2 * RET_QK_DIM, RET_V_DIM)
    return _call(
        body, name="mix_fwd", grid=(seq // tb,),
        out_shape=(
            jax.ShapeDtypeStruct((seq, MIX_WIDTH), BF16),
            jax.ShapeDtypeStruct((seq, D_MODEL), F32),
            jax.ShapeDtypeStruct((seq // BLK,) + state_shape, F32),
            jax.ShapeDtypeStruct((8, LANE), F32),
            jax.ShapeDtypeStruct((1, D_MODEL), F32),
            jax.ShapeDtypeStruct((seq, ATT_WIDTH), F32),
        ),
        in_specs=[
            pl.BlockSpec((N_CHIPS, tb, SHARD_PAD), lambda i: (0, i, 0)),
            blk_rows(D_MODEL), blk_rows(D_MODEL), blk_rows(LANE), blk_rows(LANE),
            _const_spec((MIX_WIDTH, D_MODEL)), _const_spec((1, D_MODEL)), _const_spec((1, RET_WIDTH)),
            pl.BlockSpec(memory_space=pltpu.SMEM),
            _const_spec((PAIRS, 2 * BLK, BLK)), _const_spec((BLK, RET_QK_WIDTH)), _const_spec((BLK, RET_QK_WIDTH)),
            _const_spec(state_shape), _const_spec((2, BLK, 2 * BLK)),
        ],
        out_specs=(
            blk_rows(MIX_WIDTH), blk_rows(D_MODEL),
            pl.BlockSpec((nsub,) + state_shape, lambda i: (i, 0, 0, 0)),
            _const_spec((8, LANE)), _const_spec((1, D_MODEL)), blk_rows(ATT_WIDTH),
        ),
        scratch_shapes=[
            pltpu.VMEM((BLK, ATT_KV_WIDTH), F32), pltpu.VMEM((BLK, ATT_KV_WIDTH), F32),
            pltpu.VMEM(state_shape, F32),
        ],
        compiler_params=_params(48, ("arbitrary",)),
    )(proj, x, target, cos_t, sin_t, w_out, final_g, gn_gain, sinks, decay_in, qdec_t, kdec_t, cd_t, bias_t)


def _mix_bwd(proj, dxo, mix, o_att, states, x, w_out, w_in_t, norm_g, gn_gain, sinks, tables, tb):
    seq = dxo.shape[0]
    nsub = tb // BLK
    nblk = seq // tb
    cos_t, sin_t, decay_in, qdec_t, kdec_t, cd_t, bias_t = tables
    kv_cols = OFF_AK // (2 * ATT_KV_WIDTH)
    state_shape = (PAIRS, 2 * RET_QK_DIM, RET_V_DIM)

    def body(p_ref, pkv_ref, dxo_ref, mix_ref, oatt_ref, st_ref, cos_ref, sin_ref, x_ref, wout_ref, win_ref, g_ref,
             gain_ref, sinks_ref, din_ref, qdec_ref, kdec_ref, cd_ref, bias_ref,
             dp_ref, gx_ref, gwout_ref, gnorm_ref, dgain_ref, dsink_ref,
             dmix_ref, kv_ref, dkc_ref, dvc_ref, gst_ref):
        i = pl.program_id(0)
        blk = nblk - 1 - i

        @pl.when(i == 0)
        def _():
            gwout_ref[...] = jnp.zeros_like(gwout_ref)
            gnorm_ref[...] = jnp.zeros_like(gnorm_ref)
            dgain_ref[...] = jnp.zeros_like(dgain_ref)
            dsink_ref[...] = jnp.zeros_like(dsink_ref)
            dkc_ref[...] = jnp.zeros_like(dkc_ref)
            dvc_ref[...] = jnp.zeros_like(dvc_ref)
            gst_ref[...] = jnp.zeros_like(gst_ref)

        dxo_b = dxo_ref[...].astype(BF16)
        dmix_ref[...] = _dot_nt(dxo_b, wout_ref[...])
        gwout_ref[...] += _dot_tn(mix_ref[...], dxo_b)
        kv_ref[0:BLK, :] = pkv_ref[...].astype(F32)
        kv_ref[BLK:, :] = _proj_cols(p_ref, slice(None), OFF_AK, OFF_AK + 2 * ATT_KV_WIDTH).astype(F32)
        low = _low_lanes((BLK, LANE))
        low2 = _low_lanes((2 * BLK, LANE))
        lane = lax.broadcasted_iota(jnp.int32, (1, LANE), 1)
        prev = _in_previous_block()

        def sub(jj, carry):
            dkc, dvc, gsts, dgain, dsink = carry
            j = nsub - 1 - jj
            rows = pl.ds(pl.multiple_of(j * BLK, BLK), BLK)
            both = pl.ds(pl.multiple_of(j * BLK, BLK), 2 * BLK)
            bias = bias_ref[jnp.where(jnp.logical_or(blk > 0, j > 0), 1, 0)]

            aq = _proj_cols(p_ref, rows, OFF_AQ, OFF_AQ + ATT_WIDTH)
            az = _proj_cols(p_ref, rows, OFF_AZ, OFF_AZ + ATT_WIDTH).astype(F32)
            k_ops = _kv_operands(kv_ref[both, 0:ATT_KV_WIDTH])
            v_ops = _kv_operands(kv_ref[both, ATT_KV_WIDTH:2 * ATT_KV_WIDTH])
            da = dmix_ref[rows, 0:ATT_WIDTH]
            sig = _sigmoid(az)
            d_o = da * (az * sig)
            qs = [(_stack_tiles(aq, 2 * g) * ATT_SCALE).astype(BF16) for g in range(ATT_KV_HEADS)]
            dos = [_stack_tiles(d_o, 2 * g).astype(BF16) for g in range(ATT_KV_HEADS)]
            p, p_sink = _attn_probs(qs, k_ops, bias, prev, sinks_ref)
            dpr = [_fold(_dot_nt(v_ops[g][hi], dos[g]), prev) for g, hi in ATT_PROBLEMS]
            delta = [jnp.sum(pi * di, axis=0, keepdims=True) for pi, di in zip(p, dpr)]
            ds = [_unfold((pi * (di - ti)).astype(BF16), prev) for pi, di, ti in zip(p, dpr, delta)]
            for (g, hi), ki, ti in zip(ATT_PROBLEMS, p_sink, delta):
                sink_part = ki * ti
                for half in range(2):
                    tot = jnp.sum(sink_part[:, half * BLK:(half + 1) * BLK], axis=1, keepdims=True)
                    dsink = dsink - jnp.where(lane == 4 * g + 2 * half + hi, tot, 0.0)
            dq_tiles, dk_sums, dv_sums = [], [], []
            for g in range(ATT_KV_HEADS):
                ds_cat = jnp.concatenate([ds[2 * g], ds[2 * g + 1]], axis=0)
                p_cat = jnp.concatenate([_unfold(p[2 * g].astype(BF16), prev), _unfold(p[2 * g + 1].astype(BF16), prev)],
                                        axis=0)
                dqs = _dot_tn(ds_cat, jnp.concatenate(k_ops[g], axis=0)) * ATT_SCALE
                dq_tiles += [dqs[0:BLK], dqs[BLK:]]
                dk_sums.append(_dot(ds_cat, qs[g]))
                dv_sums.append(_dot(p_cat, dos[g]))
            daz = da * oatt_ref[rows, :] * (sig * (1.0 + az * (1.0 - sig)))

            def kv_grad(sums):
                (a0, b0), (a1, b1) = [(s[0:2 * BLK], s[2 * BLK:]) for s in sums]
                return jnp.where(low2, a0, b1) + pltpu.roll(jnp.where(low2, a1, b0), HALF_LANE, 1)

            dk_both, dv_both = kv_grad(dk_sums), kv_grad(dv_sums)
            dak = dk_both[BLK:] + dkc
            dav = dv_both[BLK:] + dvc

            cos_b, sin_b = cos_ref[rows, :], sin_ref[rows, :]
            qdec, kdec = qdec_ref[...], kdec_ref[...]
            qr, kr, qd, kd = _retention_operands(p_ref, rows, cos_b, sin_b, qdec, kdec)
            heads = [(t, hh) for t in range(PAIRS) for hh in range(2)]
            head_cols = [slice(h * RET_V_DIM, (h + 1) * RET_V_DIM) for h in range(RET_HEADS)]
            q_rows = [jnp.concatenate(_split_heads(_tile(qr, t)), axis=0).astype(BF16) for t in range(PAIRS)]
            k_rows = [jnp.concatenate(_split_heads(_tile(kr, t)), axis=0).astype(BF16) for t in range(PAIRS)]
            din = [din_ref[t] for t in range(PAIRS)]
            sc = [(_dot_nt(q_rows[t], _tile(kr, t).astype(BF16)) * din[t]).astype(BF16) for t in range(PAIRS)]
            qd_heads = [_split_heads(_tile(qd, t)) for t in range(PAIRS)]
            kd_heads = [_split_heads(_tile(kd, t)) for t in range(PAIRS)]
            state_b = [st_ref[j, t].astype(BF16) for t in range(PAIRS)]
            gst_b = [gsts[t].astype(BF16) for t in range(PAIRS)]
            vs = [_proj_cols(p_ref, rows, OFF_RV + h * RET_V_DIM, OFF_RV + (h + 1) * RET_V_DIM) for h in range(RET_HEADS)]
            rzs = [_proj_cols(p_ref, rows, OFF_RZ + h * RET_V_DIM, OFF_RZ + (h + 1) * RET_V_DIM).astype(F32)
                   for h in range(RET_HEADS)]
            drs = [dmix_ref[rows, ATT_WIDTH + h * RET_V_DIM:ATT_WIDTH + (h + 1) * RET_V_DIM] for h in range(RET_HEADS)]
            gains = [gain_ref[:, c] for c in head_cols]
            lhs = [jnp.concatenate([sc[t][hh * BLK:(hh + 1) * BLK], qd_heads[t][hh].astype(BF16)], axis=1) for t, hh in heads]
            rhs = [jnp.concatenate([vs[2 * t + hh], state_b[t]], axis=0) for t, hh in heads]
            ons, rstds = _group_norm_all([_dot(l, r) for l, r in zip(lhs, rhs)])
            sig_r = [_sigmoid(z) for z in rzs]
            dgn = [d * (z * g) for d, z, g in zip(drs, rzs, sig_r)]
            dz_parts = [d * (o * gn) * (g * (1.0 + z * (1.0 - g))) for d, o, gn, g, z in zip(drs, ons, gains, sig_r, rzs)]
            dgain_parts = [jnp.sum(d * o, axis=0, keepdims=True) for d, o in zip(dgn, ons)]
            don = [d * gn for d, gn in zip(dgn, gains)]
            mean_don = [jnp.mean(d, axis=1, keepdims=True) for d in don]
            mean_don_on = [jnp.mean(d * o, axis=1, keepdims=True) for d, o in zip(don, ons)]
            dob = [(r * (d - a - o * b)).astype(BF16) for r, d, a, o, b in zip(rstds, don, mean_don, ons, mean_don_on)]
            dlhs = [_dot_nt(d, r) for d, r in zip(dob, rhs)]
            drhs = [_dot_tn(l, d) for l, d in zip(lhs, dob)]
            dkds = [_dot_nt(vs[2 * t + hh], gst_b[t]) for t, hh in heads]
            dv_parts = [drhs[2 * t + hh][0:BLK] + _dot(kd_heads[t][hh].astype(BF16), gst_b[t]) for t, hh in heads]
            das = [(dlhs[2 * t + hh][:, 0:BLK] * din[t][hh * BLK:(hh + 1) * BLK]).astype(BF16) for t, hh in heads]
            new_gsts = [gsts[t] * cd_ref[t] + drhs[2 * t][BLK:] + drhs[2 * t + 1][BLK:] for t in range(PAIRS)]
            dq_parts = [_dot(jnp.concatenate([das[2 * t], das[2 * t + 1]], axis=1), k_rows[t])
                        + jnp.where(low, dlhs[2 * t][:, BLK:], dlhs[2 * t + 1][:, BLK:]) * _tile(qdec, t)
                        for t in range(PAIRS)]
            dk_parts = [_dot_tn(jnp.concatenate([das[2 * t], das[2 * t + 1]], axis=0), q_rows[t])
                        + jnp.where(low, dkds[2 * t], dkds[2 * t + 1]) * _tile(kdec, t) for t in range(PAIRS)]
            drq = _rotate_transposed(jnp.concatenate(dq_parts, axis=1), cos_b, sin_b)
            drk = _rotate_transposed(jnp.concatenate(dk_parts, axis=1) * RET_SCALE, cos_b, sin_b)

            dp_ref[rows, :] = jnp.concatenate(
                [jnp.concatenate(dq_tiles, axis=1), dak, dav, daz, drq, drk] + dv_parts + dz_parts, axis=1).astype(BF16)
            dgain = dgain + jnp.concatenate(dgain_parts, axis=1)
            return dk_both[0:BLK], dv_both[0:BLK], tuple(new_gsts), dgain, dsink

        carry = (dkc_ref[...], dvc_ref[...], tuple(gst_ref[t] for t in range(PAIRS)), dgain_ref[...], dsink_ref[...])
        dkc, dvc, gsts, dgain, dsink = lax.fori_loop(0, nsub, sub, carry)
        dkc_ref[...] = dkc
        dvc_ref[...] = dvc
        gst_ref[...] = jnp.stack(gsts)
        dgain_ref[...] = dgain
        dsink_ref[...] = dsink

        dh = _dot(dp_ref[...], win_ref[...])
        xv = x_ref[...]
        r = lax.rsqrt(jnp.mean(xv * xv, axis=1, keepdims=True) + RMS_EPS)
        xn = xv * r
        gnorm_ref[...] += jnp.sum(dh * xn, axis=0, keepdims=True)
        u = dh * g_ref[...]
        gx_ref[...] = dxo_ref[...] + r * u - xn * (r * jnp.mean(u * xn, axis=1, keepdims=True))

    rev_rows = lambda w: pl.BlockSpec((tb, w), lambda i: (nblk - 1 - i, 0))
    prev_kv = pl.BlockSpec((None, BLK, 2 * ATT_KV_WIDTH),
                           lambda i: (0, jnp.maximum((nblk - 1 - i) * nsub - 1, 0), kv_cols))
    return _call(
        body, name="mix_bwd", grid=(nblk,),
        out_shape=(
            jax.ShapeDtypeStruct((seq, IN_WIDTH), BF16),
            jax.ShapeDtypeStruct((seq, D_MODEL), F32),
            jax.ShapeDtypeStruct((MIX_WIDTH, D_MODEL), F32),
            jax.ShapeDtypeStruct((1, D_MODEL), F32),
            jax.ShapeDtypeStruct((1, RET_WIDTH), F32),
            jax.ShapeDtypeStruct((1, LANE), F32),
        ),
        in_specs=[
            pl.BlockSpec((N_CHIPS, tb, SHARD_PAD), lambda i: (0, nblk - 1 - i, 0)),
            prev_kv, rev_rows(D_MODEL), rev_rows(MIX_WIDTH), rev_rows(ATT_WIDTH),
            pl.BlockSpec((nsub,) + state_shape, lambda i: (nblk - 1 - i, 0, 0, 0)),
            rev_rows(LANE), rev_rows(LANE), rev_rows(D_MODEL),
            _const_spec((MIX_WIDTH, D_MODEL)), _const_spec((IN_WIDTH, D_MODEL)), _const_spec((1, D_MODEL)),
            _const_spec((1, RET_WIDTH)),
            pl.BlockSpec(memory_space=pltpu.SMEM),
            _const_spec((PAIRS, 2 * BLK, BLK)), _const_spec((BLK, RET_QK_WIDTH)), _const_spec((BLK, RET_QK_WIDTH)),
            _const_spec(state_shape), _const_spec((2, BLK, 2 * BLK)),
        ],
        out_specs=(
            rev_rows(IN_WIDTH), rev_rows(D_MODEL), _const_spec((MIX_WIDTH, D_MODEL)), _const_spec((1, D_MODEL)),
            _const_spec((1, RET_WIDTH)), _const_spec((1, LANE)),
        ),
        scratch_shapes=[
            pltpu.VMEM((tb, MIX_WIDTH), F32),
            pltpu.VMEM((tb + BLK, 2 * ATT_KV_WIDTH), F32),
            pltpu.VMEM((BLK, ATT_KV_WIDTH), F32), pltpu.VMEM((BLK, ATT_KV_WIDTH), F32),
            pltpu.VMEM(state_shape, F32),
        ],
        compiler_params=_params(60, ("arbitrary",)),
    )(proj, proj, dxo, mix, o_att, states, cos_t, sin_t, x, w_out, w_in_t, norm_g, gn_gain, sinks, decay_in, qdec_t,
      kdec_t, cd_t, bias_t)


def _gw_in_reduce(h_t, dproj, gw_out, small, tb):
    seq = dproj.shape[0]
    nblk = seq // tb
    last = nblk - 1
    hand_on = min(1, last)
    half = D_MODEL // 2
    A, B, C, N_SEMS = 0, N_CHIPS, 2 * N_CHIPS, 2 * N_CHIPS + 1

    def body(win_ref, ht_ref, dp_ref, gwo_hbm, s0_ref, s1_ref, s2_ref, s3_ref, s4_ref, out_ref, fout_ref, packsum_ref,
             acc, sib, send_buf, b_in, fin, fout, mine_out, sib_out, send_out, b_out, pack_ref, packs,
             send_sems, recv_sems, local_sems):
        p, i = pl.program_id(0), pl.program_id(1)
        small_start, small_hand_on, small_finish = _small_exchange(
            gwo_hbm, (s0_ref, s1_ref, s2_ref, s3_ref, s4_ref), fout, mine_out, sib_out, send_out, b_out, pack_ref,
            packs, send_sems, recv_sems, local_sems, N_SEMS)

        @pl.when(jnp.logical_and(p == 0, i == 0))
        def _():
            small_start()

        @pl.when(jnp.logical_and(p == 0, i == hand_on))
        def _():
            small_hand_on()

        x, y, c = lax.axis_index("x"), lax.axis_index("y"), lax.axis_index("c")
        chip = 2 * x + y
        sibling = (x, y, 1 - c)
        mine = pl.ds(pl.multiple_of(c * half, half), half)
        other = pl.ds(pl.multiple_of((1 - c) * half, half), half)

        def remote(src, dst, send_k, recv_k, to):
            return pltpu.make_async_remote_copy(src_ref=src, dst_ref=dst, send_sem=send_sems.at[send_k],
                                                recv_sem=recv_sems.at[recv_k], device_id=to, device_id_type=MESH_ID)

        part = _dot(ht_ref[...], dp_ref[...])
        slot = p % 2

        @pl.when(i == 0)
        def _():
            acc[slot] = part

        @pl.when(i > 0)
        def _():
            acc[slot] += part

        for q in range(N_CHIPS):
            s = q % 2
            to_sibling = remote(acc.at[s, other, :], sib.at[s], A + q, A + q, sibling)

            @pl.when(jnp.logical_and(p == q, i == last))
            def _():
                to_sibling.start()

            if q < N_CHIPS - 1:
                dest = (chip + 1 + q) % N_CHIPS

                @pl.when(jnp.logical_and(p == q + 1, i == hand_on))
                def _():
                    to_sibling.wait_recv()
                    send_buf[q] = (acc[s, mine, :] + sib[s]).astype(BF16)
                    remote(send_buf.at[q], b_in.at[chip], B + q, B + chip, (dest // 2, dest % 2, c)).start()
                    to_sibling.wait_send()
            else:
                @pl.when(jnp.logical_and(p == q, i == last))
                def _():
                    to_sibling.wait_recv()
                    fin[mine, :] = acc[s, mine, :] + sib[s]
                    for j in range(N_CHIPS):
                        @pl.when(j != chip)
                        def _():
                            remote(b_in.at[j], b_in.at[j], B + j, B + j, sibling).wait_recv()
                            fin[mine, :] += b_in[j].astype(F32)
                    to_core = remote(fin.at[mine, :], fin.at[mine, :], C, C, sibling)
                    to_core.start()
                    remote(fin.at[other, :], fin.at[other, :], C, C, sibling).wait_recv()
                    out_ref[...] = fin[...]
                    to_core.wait_send()
                    to_sibling.wait_send()
                    for k in range(N_CHIPS - 1):
                        remote(send_buf.at[k], b_in.at[chip], B + k, B + k, sibling).wait_send()
                    packsum_ref[...] = small_finish()
                    fout_ref[...] = fout[...]

    whole = lambda shape: pl.BlockSpec(shape, lambda p, i, win: (0,) * len(shape), pipeline_mode=pl.Buffered(1))
    grid_spec = pltpu.PrefetchScalarGridSpec(
        num_scalar_prefetch=1, grid=(N_CHIPS, nblk),
        in_specs=[pl.BlockSpec((D_MODEL, tb), lambda p, i, win: (0, i)),
                  pl.BlockSpec((pl.Element(tb), pl.Element(SHARD_PAD)),
                               lambda p, i, win: (i * tb, pl.multiple_of(win[p] * LANE, LANE))),
                  pl.BlockSpec(memory_space=pl.ANY)] + [whole(s.shape) for s in small],
        out_specs=(whole((D_MODEL, SHARD_PAD)), whole((SHARD_OUT, D_MODEL)), whole((PACK_ROWS, D_MODEL))),
        scratch_shapes=[
            pltpu.VMEM((2, D_MODEL, SHARD_PAD), F32), pltpu.VMEM((2, half, SHARD_PAD), F32),
            pltpu.VMEM((N_CHIPS - 1, half, SHARD_PAD), BF16), pltpu.VMEM((N_CHIPS, half, SHARD_PAD), BF16),
            pltpu.VMEM((D_MODEL, SHARD_PAD), F32),
        ] + _small_exchange_scratch() + [
            pltpu.SemaphoreType.DMA((N_SEMS + SMALL_SEMS,)), pltpu.SemaphoreType.DMA((N_SEMS + SMALL_SEMS,)),
            pltpu.SemaphoreType.DMA((N_CHIPS,)),
        ])
    chip = 2 * lax.axis_index("x") + lax.axis_index("y")
    owner = (chip + 1 + jnp.arange(N_CHIPS, dtype=jnp.int32)) % N_CHIPS
    win_start = (owner * SHARD_IN) // LANE
    return _call(
        body, name="gw_in_reduce", grid_spec=grid_spec,
        out_shape=(jax.ShapeDtypeStruct((D_MODEL, SHARD_PAD), F32), jax.ShapeDtypeStruct((SHARD_OUT, D_MODEL), F32),
                   jax.ShapeDtypeStruct((PACK_ROWS, D_MODEL), F32)),
        compiler_params=_params(52, ("arbitrary", "arbitrary")),
    )(win_start.astype(jnp.int32), h_t, dproj, gw_out, *small)


SMALL_SEMS = 17


def _small_exchange(gwo_hbm, small_refs, fout_ref, mine_out, sib_out, send_out, b_out, pack_ref, packs,
                    send_sems, recv_sems, local_sems, base):
    half_out = SHARD_OUT // 2
    A_OUT, B_OUT, C_OUT, PACK = base, base + 4, base + 8, base + 9
    assert len(small_refs) == PACK_PARTS
    x, y, c = lax.axis_index("x"), lax.axis_index("y"), lax.axis_index("c")
    chip = 2 * x + y
    dev = 2 * chip + c
    sibling = (x, y, 1 - c)

    def remote(src, dst, send_k, recv_k, to):
        return pltpu.make_async_remote_copy(src_ref=src, dst_ref=dst, send_sem=send_sems.at[send_k],
                                            recv_sem=recv_sems.at[recv_k], device_id=to, device_id_type=MESH_ID)

    def out_rows(j, core):
        return pl.ds(pl.multiple_of(j * SHARD_OUT + core * half_out, half_out), half_out)

    my_out_rows = pl.ds(pl.multiple_of(c * half_out, half_out), half_out)
    local = [pltpu.make_async_copy(gwo_hbm.at[out_rows(j, c), :], mine_out.at[j], local_sems.at[j])
             for j in range(N_CHIPS)]
    stage_a = [remote(gwo_hbm.at[out_rows(j, 1 - c), :], sib_out.at[j], A_OUT + j, A_OUT + j, sibling)
               for j in range(N_CHIPS)]
    mine_half_out = fout_ref.at[my_out_rows, :]
    stage_c = [remote(mine_half_out, mine_half_out, C_OUT, C_OUT, sibling)]

    def start():
        pack_ref[...] = jnp.zeros_like(pack_ref)
        for k, s_ref in enumerate(small_refs):
            pack_ref[k:k + 1, 0:s_ref.shape[1]] = s_ref[0:1, :]
        packs[dev] = pack_ref[...]
        for d in range(N_DEV):
            to = (d // 4, (d // 2) % 2, d % 2)

            @pl.when(d != dev)
            def _():
                remote(pack_ref, packs.at[dev], PACK + d, PACK + dev, to).start()

        for cp in local + stage_a:
            cp.start()

    def hand_on():
        for cp in local:
            cp.wait()
        for cp in stage_a:
            cp.wait_recv()
        for j in range(N_CHIPS):
            mine_out[j] = mine_out[j] + sib_out[j]

        for j in range(N_CHIPS):
            to = (j // 2, j % 2, c)

            @pl.when(j != chip)
            def _():
                send_out[j] = mine_out[j].astype(BF16)
                remote(send_out.at[j], b_out.at[chip], B_OUT + j, B_OUT + chip, to).start()

            @pl.when(j == chip)
            def _():
                fout_ref[my_out_rows, :] = mine_out[j]

    def finish():
        for j in range(N_CHIPS):
            @pl.when(j != chip)
            def _():
                remote(b_out.at[j], b_out.at[j], B_OUT + j, B_OUT + j, sibling).wait_recv()
                fout_ref[my_out_rows, :] += b_out[j].astype(F32)

        for cp in stage_c:
            cp.start()
        other_half_out = fout_ref.at[pl.ds(pl.multiple_of((1 - c) * half_out, half_out), half_out), :]
        remote(other_half_out, other_half_out, C_OUT, C_OUT, sibling).wait_recv()

        for d in range(N_DEV):
            @pl.when(d != dev)
            def _():
                remote(pack_ref, packs.at[d], PACK + d, PACK + d, sibling).wait_recv()
        total = packs[0]
        for d in range(1, N_DEV):
            total = total + packs[d]

        for cp in stage_a + stage_c:
            cp.wait_send()
        for j in range(N_CHIPS):
            @pl.when(j != chip)
            def _():
                remote(b_out.at[j], b_out.at[j], B_OUT + j, B_OUT + j, sibling).wait_send()
        for d in range(N_DEV):
            @pl.when(d != dev)
            def _():
                remote(pack_ref, packs.at[d], PACK + d, PACK + d, sibling).wait_send()
        return total

    return start, hand_on, finish


def _small_exchange_scratch():
    half_out = SHARD_OUT // 2
    return [
        pltpu.VMEM((SHARD_OUT, D_MODEL), F32),
        pltpu.VMEM((N_CHIPS, half_out, D_MODEL), F32), pltpu.VMEM((N_CHIPS, half_out, D_MODEL), F32),
        pltpu.VMEM((N_CHIPS, half_out, D_MODEL), BF16), pltpu.VMEM((N_CHIPS, half_out, D_MODEL), BF16),
        pltpu.VMEM((PACK_ROWS, D_MODEL), F32), pltpu.VMEM((N_DEV, PACK_ROWS, D_MODEL), F32),
    ]


def _adam_math(w, g, m, v):
    mn = ADAM_B1 * m + (1.0 - ADAM_B1) * g
    vn = ADAM_B2 * v + (1.0 - ADAM_B2) * (g * g)
    m_hat = mn / (1.0 - ADAM_B1 ** ADAM_STEP)
    v_hat = vn / (1.0 - ADAM_B2 ** ADAM_STEP)
    return -ADAM_LR * (m_hat / (jnp.sqrt(v_hat) + ADAM_EPS) + ADAM_WD * w), mn, vn


def _adamw(name, w, g, m, v, tb):
    rows, cols = w.shape

    def body(w_ref, g_ref, m_ref, v_ref, go_ref, d_ref, mo_ref, vo_ref):
        gv = g_ref[...]
        go_ref[...] = gv
        d_ref[...], mo_ref[...], vo_ref[...] = _adam_math(w_ref[...], gv, m_ref[...], v_ref[...])

    spec = pl.BlockSpec((tb, cols), lambda i: (i, 0))
    shape = jax.ShapeDtypeStruct((rows, cols), F32)
    return _call(
        body, name=name, grid=(rows // tb,), out_shape=(shape,) * 4,
        in_specs=[spec] * 4, out_specs=(spec,) * 4,
        compiler_params=_params(32, ("arbitrary",)),
    )(w, g, m, v)


def _adamw_w_in(w_t, g_window, m_t, v_t, tb):
    def body(w_ref, g_ref, m_ref, v_ref, go_ref, d_ref, mo_ref, vo_ref, gt_ref):
        gt_ref[...] = g_ref[...].T
        gv = gt_ref[pl.ds(pl.multiple_of(lax.axis_index("y") * SHARD_SHIFT, SHARD_SHIFT), SHARD_IN), :]
        go_ref[...] = gv
        d_ref[...], mo_ref[...], vo_ref[...] = _adam_math(w_ref[...], gv, m_ref[...], v_ref[...])

    spec = pl.BlockSpec((SHARD_IN, tb), lambda i: (0, i))
    shape = jax.ShapeDtypeStruct((SHARD_IN, D_MODEL), F32)
    return _call(
        body, name="adamw_w_in", grid=(D_MODEL // tb,), out_shape=(shape,) * 4,
        in_specs=[spec, pl.BlockSpec((tb, SHARD_PAD), lambda i: (i, 0)), spec, spec], out_specs=(spec,) * 4,
        scratch_shapes=[pltpu.VMEM((SHARD_PAD, tb), F32)],
        compiler_params=_params(32, ("arbitrary",)),
    )(w_t, g_window, m_t, v_t)


def _adamw_small(sums, params):
    def body(sums_ref, *refs):
        ins, outs = refs[:3 * len(params)], refs[3 * len(params):]
        for k in range(len(params)):
            w_ref, m_ref, v_ref = ins[3 * k:3 * k + 3]
            g = sums_ref[k:k + 1, 0:w_ref.shape[1]]
            go_ref, d_ref, mo_ref, vo_ref = outs[4 * k:4 * k + 4]
            go_ref[...] = g
            d_ref[...], mo_ref[...], vo_ref[...] = _adam_math(w_ref[...], g, m_ref[...], v_ref[...])

    vmem = pl.BlockSpec(memory_space=pltpu.VMEM)
    flat = [a for p in params for a in p]
    shapes = tuple(jax.ShapeDtypeStruct(p[0].shape, F32) for p in params for _ in range(4))
    res = _call(body, name="adamw_small", out_shape=shapes, in_specs=[vmem] * (1 + len(flat)),
                out_specs=(vmem,) * len(shapes), compiler_params=_params(16))(sums, *flat)
    return [res[4 * k:4 * k + 4] for k in range(len(params))]


def kernel(x, norm_g, w_in, att_sinks, ret_gn_g, w_out, final_g, loss_target, m_norm_g, m_w_in, m_att_sinks, m_ret_gn_g, m_w_out, m_final_g, v_norm_g, v_w_in, v_att_sinks, v_ret_gn_g, v_w_out, v_final_g):
    seq = x.shape[1]
    xs, tgt = x[0], loss_target[0]
    final_g2 = final_g.reshape(1, D_MODEL)
    tables = _tables(seq)

    w_in_t, m_w_in_t, v_w_in_t = w_in[0].T, m_w_in[0].T, v_w_in[0].T
    proj, h_t, w_in_full, w_out_full = _in_proj(xs, norm_g, w_in_t, w_out[0], min(TOKENS_PROJ, seq))
    mix, dxo, states, loss_part, gfin, o_att = _mix_fwd(proj, xs, tgt, w_out_full, final_g2, ret_gn_g, att_sinks,
                                                        tables, min(TOKENS_MIX, seq))
    dproj, grad_x, gw_out, gnorm, dgain, dsink = _mix_bwd(proj, dxo, mix, o_att, states, xs, w_out_full, w_in_full,
                                                          norm_g, ret_gn_g, att_sinks, tables, min(TOKENS_MIX, seq))
    g_in, g_out, sums = _gw_in_reduce(h_t, dproj, gw_out, (gnorm, gfin, dgain, dsink, loss_part),
                                      min(TOKENS_GW, seq))

    res_in = [r.T for r in _adamw_w_in(w_in_t, g_in, m_w_in_t, v_w_in_t, 256)]
    res_out = _adamw("adamw_w_out", w_out[0], g_out, m_w_out[0], v_w_out[0], SHARD_OUT)
    as_row = lambda a: a.reshape(1, D_MODEL)
    r_norm, r_final, r_gain, r_sink = _adamw_small(sums, [
        (norm_g, m_norm_g, v_norm_g), (final_g2, as_row(m_final_g), as_row(v_final_g)),
        (ret_gn_g, m_ret_gn_g, v_ret_gn_g), (att_sinks, m_att_sinks, v_att_sinks)])

    outs = []
    for k in range(4):
        outs += [r_norm[k], res_in[k][None], r_sink[k], r_gain[k], res_out[k][None], r_final[k].reshape(D_MODEL)]
    return (sums[4, 0], grad_x[None], *outs)
```

```python
import jax
import jax.numpy as jnp
import numpy as np
from jax import lax
from jax.experimental import pallas as pl
from jax.experimental.pallas import tpu as pltpu

F32 = jnp.float32
BF16 = jnp.bfloat16

D_MODEL = 1024
ATT_HEADS = 8
ATT_KV_HEADS = 2
ATT_HEAD_DIM = 64
RET_HEADS = 4
RET_QK_DIM = 64
RET_V_DIM = 128
BLK = 128
ROT_BASE = 10000.0
RMS_EPS = 1e-6
GN_EPS = 1e-6
NEG_INF = -1e30
ATT_SCALE = ATT_HEAD_DIM ** -0.5
RET_SCALE = RET_QK_DIM ** -0.5

ATT_WIDTH = ATT_HEADS * ATT_HEAD_DIM
ATT_KV_WIDTH = ATT_KV_HEADS * ATT_HEAD_DIM
RET_QK_WIDTH = RET_HEADS * RET_QK_DIM
RET_WIDTH = RET_HEADS * RET_V_DIM
MIX_WIDTH = ATT_WIDTH + RET_WIDTH
OFF_AQ = 0
OFF_AK = OFF_AQ + ATT_WIDTH
OFF_AV = OFF_AK + ATT_KV_WIDTH
OFF_AZ = OFF_AV + ATT_KV_WIDTH
OFF_RQ = OFF_AZ + ATT_WIDTH
OFF_RK = OFF_RQ + RET_QK_WIDTH
OFF_RV = OFF_RK + RET_QK_WIDTH
OFF_RZ = OFF_RV + RET_WIDTH
IN_WIDTH = OFF_RZ + RET_WIDTH

LANE = 128
BF16_ROWS = 16
HALF_LANE = LANE // 2
PAIRS = RET_QK_WIDTH // LANE
assert ATT_HEAD_DIM == HALF_LANE and RET_QK_DIM == HALF_LANE and RET_V_DIM == LANE and ATT_KV_WIDTH == LANE

N_CHIPS = 4
N_DEV = 8
SHARD_IN = IN_WIDTH // N_CHIPS
SHARD_PAD = 768
SHARD_SHIFT = SHARD_PAD - SHARD_IN
WIN_START = tuple((j * SHARD_IN) // LANE * LANE for j in range(N_CHIPS))
SHARD_OUT = MIX_WIDTH // N_CHIPS

PACK_PARTS = 5
PACK_ROWS = 8
assert PACK_PARTS <= PACK_ROWS

ADAM_LR = 0.001
ADAM_B1 = 0.9
ADAM_B2 = 0.999
ADAM_EPS = 1e-08
ADAM_WD = 0.01
ADAM_STEP = 10

VMEM_CAP = 64 * 1024 * 1024
TOKENS_PROJ = 1024
TOKENS_MIX = 512
TOKENS_GW = 2048
MESH_ID = pl.DeviceIdType.MESH


def _call(body, **kw):
    return pl.pallas_call(body, **kw)


def _params(vmem_mb, semantics=None):
    assert vmem_mb * 1024 * 1024 < VMEM_CAP
    return pltpu.CompilerParams(dimension_semantics=semantics, vmem_limit_bytes=vmem_mb * 1024 * 1024)


def _dot(a, b):
    return jnp.dot(a, b, preferred_element_type=F32)


def _dot_nt(a, b):
    return lax.dot_general(a, b, (((1,), (1,)), ((), ())), preferred_element_type=F32)


def _dot_tn(a, b):
    return lax.dot_general(a, b, (((0,), (0,)), ((), ())), preferred_element_type=F32)


def _sigmoid(z):
    return 1.0 / (1.0 + jnp.exp(-z))


def _const_spec(shape):
    nd = len(shape)
    return pl.BlockSpec(shape, lambda i: (0,) * nd, pipeline_mode=pl.Buffered(1))


def _tables(seq):
    f32 = np.float32
    pos = np.arange(seq, dtype=f32)
    theta = (f32(1.0) / (f32(ROT_BASE) ** np.linspace(0.0, 1.0, RET_QK_DIM // 2, dtype=f32))).astype(f32)
    ang = (pos[:, None] * theta[None, :]).astype(f32)
    cos, sin = np.cos(ang), np.sin(ang)
    cos2 = np.repeat(cos, 2, axis=1)
    sin2 = np.stack([-sin, sin], axis=-1).reshape(seq, RET_QK_DIM)
    cos_t = np.tile(cos2, (1, 2))
    sin_t = np.tile(sin2, (1, 2))

    log_gamma = np.log(f32(1.0) - f32(2.0) ** (f32(-5.0) - np.arange(RET_HEADS, dtype=f32))).astype(f32)
    idx = np.arange(BLK, dtype=f32)
    rel = idx[:, None] - idx[None, :]
    decay_in = np.where(rel >= 0, np.exp(log_gamma[:, None, None] * np.maximum(rel, f32(0.0))), f32(0.0))
    k_dec = np.exp(log_gamma[:, None] * (BLK - 1 - idx)[None, :])
    q_dec = np.exp(log_gamma[:, None] * (idx + 1)[None, :])
    chunk_decay = np.exp(log_gamma * f32(BLK))
    kdec_t = np.repeat(k_dec.T, RET_QK_DIM, axis=1)
    qdec_t = np.repeat(q_dec.T, RET_QK_DIM, axis=1)
    cd_t = np.broadcast_to(chunk_decay[:, None, None], (RET_HEADS, RET_QK_DIM, RET_V_DIM))
    decay_in = decay_in.reshape(PAIRS, 2 * BLK, BLK)
    cd_t = cd_t.reshape(PAIRS, 2 * RET_QK_DIM, RET_V_DIM)

    key = np.arange(BLK)[:, None]
    query = np.arange(2 * BLK)[None, :] % BLK
    bias = np.stack([np.where(key > query, NEG_INF, 0.0), np.zeros((BLK, 2 * BLK))])
    return tuple(jnp.asarray(np.ascontiguousarray(a), F32) for a in (cos_t, sin_t, decay_in, qdec_t, kdec_t, cd_t, bias))


def _low_lanes(shape):
    lane = lax.broadcasted_iota(jnp.int32, shape, len(shape) - 1)
    return (lane & HALF_LANE) == 0


def _split_heads(t):
    low = _low_lanes(t.shape)
    zero = jnp.zeros_like(t)
    return jnp.where(low, t, zero), jnp.where(low, zero, t)


def _swap_pairs(t):
    lane = lax.broadcasted_iota(jnp.int32, t.shape, 1)
    nxt = pltpu.roll(t, t.shape[1] - 1, 1)
    prv = pltpu.roll(t, 1, 1)
    return jnp.where((lane & 1) == 0, nxt, prv)


def _per_tile(fn, t):
    return jnp.concatenate([fn(_tile(t, i)) for i in range(t.shape[1] // LANE)], axis=1)


def _rotate(t, cos_t, sin_t):
    return _per_tile(lambda a: a * cos_t + _swap_pairs(a) * sin_t, t)


def _rotate_transposed(d, cos_t, sin_t):
    return _per_tile(lambda a: a * cos_t + _swap_pairs(a * sin_t), d)


def _kv_operands(cat):
    low = _low_lanes(cat.shape)
    swapped = pltpu.roll(cat, HALF_LANE, 1)
    zero = jnp.zeros_like(cat)
    pick = lambda a, b: jnp.where(low, a, b).astype(BF16)
    return ((pick(cat, zero), pick(zero, swapped)), (pick(swapped, zero), pick(zero, cat)))


def _stack_tiles(t, first_tile):
    a = t[:, first_tile * LANE:(first_tile + 1) * LANE]
    b = t[:, (first_tile + 1) * LANE:(first_tile + 2) * LANE]
    return jnp.concatenate([a, b], axis=0)


def _sink_rows(sinks_ref, group):
    first = lax.broadcasted_iota(jnp.int32, (1, 2 * BLK), 1) < BLK

    def row(h0, h1):
        return jnp.where(first, sinks_ref[0, group * 4 + h0], sinks_ref[0, group * 4 + h1])
    return row(0, 2), row(1, 3)


ATT_PROBLEMS = tuple((g, hi) for g in range(ATT_KV_HEADS) for hi in range(2))


def _in_previous_block():
    key = lax.broadcasted_iota(jnp.int32, (BLK, 2 * BLK), 0)
    query = lax.broadcasted_iota(jnp.int32, (BLK, 2 * BLK), 1) & (BLK - 1)
    return key > query


def _fold(t, prev):
    return jnp.where(prev, t[0:BLK], t[BLK:])


def _unfold(t, prev):
    zero = jnp.zeros_like(t)
    return jnp.concatenate([jnp.where(prev, t, zero), jnp.where(prev, zero, t)], axis=0)


def _attn_probs(qs, k_ops, bias, prev, sinks_ref):
    sink = [_sink_rows(sinks_ref, g)[hi] for g, hi in ATT_PROBLEMS]
    s = [_fold(_dot_nt(k_ops[g][hi], qs[g]), prev) + bias for g, hi in ATT_PROBLEMS]
    m = [jnp.maximum(jnp.max(si, axis=0, keepdims=True), ki) for si, ki in zip(s, sink)]
    e = [jnp.exp(si - mi) for si, mi in zip(s, m)]
    es = [jnp.exp(ki - mi) for ki, mi in zip(sink, m)]
    inv = [1.0 / (jnp.sum(ei, axis=0, keepdims=True) + esi) for ei, esi in zip(e, es)]
    return [ei * ii for ei, ii in zip(e, inv)], [esi * ii for esi, ii in zip(es, inv)]


def _group_norm_all(outs):
    mu = [jnp.mean(o, axis=1, keepdims=True) for o in outs]
    xc = [o - m for o, m in zip(outs, mu)]
    var = [jnp.mean(c * c, axis=1, keepdims=True) for c in xc]
    rstd = [lax.rsqrt(v + GN_EPS) for v in var]
    return [c * r for c, r in zip(xc, rstd)], rstd


def _proj_cols(p_ref, rows, start, stop):
    pieces = []
    while start < stop:
        k = max(j for j in range(N_CHIPS) if WIN_START[j] <= start)
        end = min(stop, WIN_START[k] + SHARD_PAD)
        pieces.append(p_ref[k, rows, start - WIN_START[k]:end - WIN_START[k]])
        start = end
    return pieces[0] if len(pieces) == 1 else jnp.concatenate(pieces, axis=1)


def _retention_operands(p_ref, rows, cos_b, sin_b, qdec, kdec):
    qr = _rotate(_proj_cols(p_ref, rows, OFF_RQ, OFF_RQ + RET_QK_WIDTH).astype(F32), cos_b, sin_b)
    kr = _rotate(_proj_cols(p_ref, rows, OFF_RK, OFF_RK + RET_QK_WIDTH).astype(F32), cos_b, sin_b) * RET_SCALE
    return qr, kr, qr * qdec, kr * kdec


def _tile(t, i):
    return t[:, i * LANE:(i + 1) * LANE]


GATHER_SEMS = 7


def _allgather_steps(src_ref, full_ref, blk_ref, send_sems, recv_sems):
    block_rows = blk_ref.shape[0]
    assert block_rows % BF16_ROWS == 0 and full_ref.shape[0] == N_DEV * block_rows
    x, y, c = lax.axis_index("x"), lax.axis_index("y"), lax.axis_index("c")
    me, sibling = (x, y, c), (x, y, 1 - c)
    chips = [(1 - x, y), (x, 1 - y), (1 - x, 1 - y)]

    def rows(px, py, pc):
        return full_ref.at[pl.ds(pl.multiple_of((4 * px + 2 * py + pc) * block_rows, BF16_ROWS), block_rows), :]

    def copy(k, block, to, src=None):
        return pltpu.make_async_remote_copy(
            src_ref=rows(*block) if src is None else src, dst_ref=rows(*block),
            send_sem=send_sems.at[k], recv_sem=recv_sems.at[k], device_id=to, device_id_type=MESH_ID)

    first = [copy(0, me, sibling, src=blk_ref)] + [copy(1 + j, me, (*chip, c), src=blk_ref) for j, chip in enumerate(chips)]
    passed = [copy(4 + j, (*chip, c), sibling) for j, chip in enumerate(chips)]

    def start():
        blk_ref[...] = src_ref[pl.ds(pl.multiple_of(c * block_rows, BF16_ROWS), block_rows), :].astype(BF16)
        rows(*me)[...] = blk_ref[...]
        for cp in first:
            cp.start()

    def forward():
        for j, chip in enumerate(chips):
            copy(1 + j, (*chip, c), me).wait_recv()
            passed[j].start()

    def finish():
        copy(0, sibling, me).wait_recv()
        for j, chip in enumerate(chips):
            copy(4 + j, (*chip, 1 - c), me).wait_recv()
        for cp in first + passed:
            cp.wait_send()

    return start, forward, finish


WINDOW_HALF = SHARD_PAD // 2
WINDOW_SEMS = GATHER_SEMS + 1


def _window_gather(wi_ref, w4_ref, blk_ref, edge_ref, send_sems, recv_sems):
    x, y, c = lax.axis_index("x"), lax.axis_index("y"), lax.axis_index("c")
    me, sibling, pair = (x, y, c), (x, y, 1 - c), (x, 1 - y, c)
    chips = [(1 - x, y), (x, 1 - y), (1 - x, 1 - y)]
    own_rows, edge_rows = SHARD_IN - WINDOW_HALF, SHARD_SHIFT
    sends_edge = y == c

    def rows(px, py, pc):
        return w4_ref.at[pl.ds(pl.multiple_of((4 * px + 2 * py + pc) * WINDOW_HALF, BF16_ROWS), WINDOW_HALF), :]

    def copy(k, block, to, src=None):
        return pltpu.make_async_remote_copy(
            src_ref=rows(*block) if src is None else src, dst_ref=rows(*block),
            send_sem=send_sems.at[k], recv_sem=recv_sems.at[k], device_id=to, device_id_type=MESH_ID)

    edge_copy = pltpu.make_async_remote_copy(
        src_ref=edge_ref, dst_ref=edge_ref, send_sem=send_sems.at[GATHER_SEMS], recv_sem=recv_sems.at[GATHER_SEMS],
        device_id=pair, device_id_type=MESH_ID)
    first = [copy(0, me, sibling, src=blk_ref)] + [copy(1 + j, me, (*chip, c), src=blk_ref) for j, chip in enumerate(chips)]
    passed = [copy(4 + j, (*chip, c), sibling) for j, chip in enumerate(chips)]

    def at(start, size):
        return pl.ds(pl.multiple_of(start, BF16_ROWS), size)

    def start():
        @pl.when(sends_edge)
        def _():
            edge_ref[...] = wi_ref[at((1 - y) * (SHARD_IN - edge_rows), edge_rows), :].astype(BF16)
            edge_copy.start()
            blk_ref[...] = wi_ref[at(c * own_rows, WINDOW_HALF), :].astype(BF16)

        @pl.when(jnp.logical_not(sends_edge))
        def _():
            blk_ref[at(y * edge_rows, own_rows), :] = wi_ref[at(c * WINDOW_HALF, own_rows), :].astype(BF16)
            edge_copy.wait_recv()
            blk_ref[at((1 - y) * own_rows, edge_rows), :] = edge_ref[...]

        rows(*me)[...] = blk_ref[...]
        for cp in first[:-1]:
            cp.start()
        copy(0, sibling, me).wait_recv()

    def send_far():
        first[-1].start()

    def forward(j):
        copy(1 + j, (*chips[j], c), me).wait_recv()
        passed[j].start()

    def wait_forwarded(j):
        copy(4 + j, (*chips[j], 1 - c), me).wait_recv()

    def finish():
        for cp in first + passed:
            cp.wait_send()

        @pl.when(sends_edge)
        def _():
            edge_copy.wait_send()

    return start, send_far, forward, wait_forwarded, finish


def _in_proj(x, norm_g, w_in_t_shard, w_out_shard, tb):
    seq = x.shape[0]
    nblk = seq // tb
    last = nblk - 1
    chip_of_panel = (None, 1, 0, 2)

    def body(win_ref, x_ref, g_ref, wi_ref, wo_ref, p_ref, ht_ref, wt_ref, wout_ref,
             w4, blk, edge, hbuf, wout_full, blko, send_sems, recv_sems, send_sems_o, recv_sems_o):
        q, i = pl.program_id(0), pl.program_id(1)
        chip = 2 * lax.axis_index("x") + lax.axis_index("y")
        in_start, in_send_far, in_forward, in_wait_forwarded, in_finish = _window_gather(
            wi_ref, w4, blk, edge, send_sems, recv_sems)
        out_start, out_forward, out_finish = _allgather_steps(wo_ref, wout_full, blko, send_sems_o, recv_sems_o)
        rows = pl.ds(pl.multiple_of(i * tb, tb), tb)

        @pl.when(jnp.logical_and(q == 0, i == 0))
        def _():
            in_start()

        @pl.when(jnp.logical_and(q == 0, i == min(last // 2 + 1, last)))
        def _():
            in_send_far()

        @pl.when(jnp.logical_and(q == 1, i == 0))
        def _():
            out_start()

        for panel in range(1, N_CHIPS):
            @pl.when(jnp.logical_and(q == panel, i == 0))
            def _():
                in_wait_forwarded(chip_of_panel[panel])

        @pl.when(q == 0)
        def _():
            xv = x_ref[...]
            r = lax.rsqrt(jnp.mean(xv * xv, axis=1, keepdims=True) + RMS_EPS)
            h = (xv * r) * g_ref[...]
            hbuf[rows, :] = h.astype(BF16)
            ht_ref[...] = h.T.astype(BF16)

        owner = jnp.bitwise_xor(chip, q)
        window = w4[pl.ds(pl.multiple_of(owner * SHARD_PAD, SHARD_PAD), SHARD_PAD), :]
        p_ref[...] = _dot_nt(hbuf[rows, :], window).astype(BF16)

        for panel in range(1, N_CHIPS):
            @pl.when(jnp.logical_and(q == panel - 1, i == last))
            def _():
                in_forward(chip_of_panel[panel])

        @pl.when(jnp.logical_and(q == N_CHIPS - 1, i == 0))
        def _():
            out_forward()

        @pl.when(jnp.logical_and(q == N_CHIPS - 1, i == last))
        def _():
            in_finish()
            for k in range(N_CHIPS):
                first = k * SHARD_PAD + (k % 2) * SHARD_SHIFT
                wt_ref[k * SHARD_IN:(k + 1) * SHARD_IN, :] = w4[first:first + SHARD_IN, :]
            out_finish()
            wout_ref[...] = wout_full[...]

    whole = lambda shape: pl.BlockSpec(shape, lambda q, i, win: (0,) * len(shape), pipeline_mode=pl.Buffered(1))
    first_panel = lambda q, i: jnp.where(q == 0, i, last)
    grid_spec = pltpu.PrefetchScalarGridSpec(
        num_scalar_prefetch=1, grid=(N_CHIPS, nblk),
        in_specs=[pl.BlockSpec((tb, D_MODEL), lambda q, i, win: (first_panel(q, i), 0)), whole((1, D_MODEL)),
                  whole((SHARD_IN, D_MODEL)), whole((SHARD_OUT, D_MODEL))],
        out_specs=(pl.BlockSpec((None, tb, SHARD_PAD), lambda q, i, win: (win[q], i, 0)),
                   pl.BlockSpec((D_MODEL, tb), lambda q, i, win: (0, first_panel(q, i))),
                   whole((IN_WIDTH, D_MODEL)), whole((MIX_WIDTH, D_MODEL))),
        scratch_shapes=[
            pltpu.VMEM((N_CHIPS * SHARD_PAD, D_MODEL), BF16), pltpu.VMEM((WINDOW_HALF, D_MODEL), BF16),
            pltpu.VMEM((SHARD_SHIFT, D_MODEL), BF16), pltpu.VMEM((seq, D_MODEL), BF16),
            pltpu.VMEM((MIX_WIDTH, D_MODEL), BF16), pltpu.VMEM((SHARD_OUT // 2, D_MODEL), BF16),
            pltpu.SemaphoreType.DMA((WINDOW_SEMS,)), pltpu.SemaphoreType.DMA((WINDOW_SEMS,)),
            pltpu.SemaphoreType.DMA((GATHER_SEMS,)), pltpu.SemaphoreType.DMA((GATHER_SEMS,)),
        ])
    chip = 2 * lax.axis_index("x") + lax.axis_index("y")
    owner = jnp.bitwise_xor(chip, jnp.arange(N_CHIPS, dtype=jnp.int32))
    return _call(
        body, name="in_proj", grid_spec=grid_spec,
        out_shape=(jax.ShapeDtypeStruct((N_CHIPS, seq, SHARD_PAD), BF16), jax.ShapeDtypeStruct((D_MODEL, seq), BF16),
                   jax.ShapeDtypeStruct((IN_WIDTH, D_MODEL), BF16), jax.ShapeDtypeStruct((MIX_WIDTH, D_MODEL), BF16)),
        compiler_params=_params(60, ("arbitrary", "arbitrary")),
    )(owner.astype(jnp.int32), x, norm_g, w_in_t_shard, w_out_shard)


def _mix_fwd(proj, x, target, w_out, final_g, gn_gain, sinks, tables, tb):
    seq = x.shape[0]
    nsub = tb // BLK
    cos_t, sin_t, decay_in, qdec_t, kdec_t, cd_t, bias_t = tables

    def body(p_ref, x_ref, t_ref, cos_ref, sin_ref, wout_ref, fg_ref, gain_ref, sinks_ref, din_ref, qdec_ref,
             kdec_ref, cd_ref, bias_ref, mix_ref, dxo_ref, st_ref, loss_ref, gfin_ref, oatt_ref, probs_ref, psink_ref,
             kprev_ref, vprev_ref, state_ref):
        i = pl.program_id(0)

        @pl.when(i == 0)
        def _():
            kprev_ref[...] = jnp.zeros_like(kprev_ref)
            vprev_ref[...] = jnp.zeros_like(vprev_ref)
            state_ref[...] = jnp.zeros_like(state_ref)
            loss_ref[...] = jnp.zeros_like(loss_ref)
            gfin_ref[...] = jnp.zeros_like(gfin_ref)

        prev = _in_previous_block()

        def sub(j, carry):
            kp, vp, states = carry
            rows = pl.ds(pl.multiple_of(j * BLK, BLK), BLK)
            bias = bias_ref[jnp.where(jnp.logical_or(i > 0, j > 0), 1, 0)]

            aq = _proj_cols(p_ref, rows, OFF_AQ, OFF_AQ + ATT_WIDTH)
            ak = _proj_cols(p_ref, rows, OFF_AK, OFF_AK + ATT_KV_WIDTH).astype(F32)
            av = _proj_cols(p_ref, rows, OFF_AV, OFF_AV + ATT_KV_WIDTH).astype(F32)
            az = _proj_cols(p_ref, rows, OFF_AZ, OFF_AZ + ATT_WIDTH).astype(F32)
            k_ops = _kv_operands(jnp.concatenate([kp, ak], axis=0))
            v_ops = _kv_operands(jnp.concatenate([vp, av], axis=0))
            qs = [(_stack_tiles(aq, 2 * g) * ATT_SCALE).astype(BF16) for g in range(ATT_KV_HEADS)]
            p, p_sink = _attn_probs(qs, k_ops, bias, prev, sinks_ref)
            p = [pi.astype(BF16) for pi in p]
            probs_ref[j] = jnp.stack(p)
            psink_ref[j] = jnp.concatenate(p_sink, axis=0)
            o_tiles = []
            for g in range(ATT_KV_HEADS):
                p_cat = jnp.concatenate([_unfold(p[2 * g], prev), _unfold(p[2 * g + 1], prev)], axis=0)
                o = _dot_tn(p_cat, jnp.concatenate(v_ops[g], axis=0))
                o_tiles += [o[0:BLK], o[BLK:]]
            o_att = jnp.concatenate(o_tiles, axis=1)
            oatt_ref[rows, :] = o_att
            out = [o_att * (az * _sigmoid(az))]

            qr, kr, qd, kd = _retention_operands(p_ref, rows, cos_ref[rows, :], sin_ref[rows, :],
                                                 qdec_ref[...], kdec_ref[...])
            heads = [(t, hh) for t in range(PAIRS) for hh in range(2)]
            sc = [_dot_nt(jnp.concatenate(_split_heads(_tile(qr, t)), axis=0).astype(BF16), _tile(kr, t).astype(BF16))
                  * din_ref[t] for t in range(PAIRS)]
            qd_heads = [_split_heads(_tile(qd, t)) for t in range(PAIRS)]
            state_b = [states[t].astype(BF16) for t in range(PAIRS)]
            vs = [_proj_cols(p_ref, rows, OFF_RV + h * RET_V_DIM, OFF_RV + (h + 1) * RET_V_DIM) for h in range(RET_HEADS)]
            rzs = [_proj_cols(p_ref, rows, OFF_RZ + h * RET_V_DIM, OFF_RZ + (h + 1) * RET_V_DIM).astype(F32)
                   for h in range(RET_HEADS)]
            lhs = [jnp.concatenate([sc[t][hh * BLK:(hh + 1) * BLK].astype(BF16), qd_heads[t][hh].astype(BF16)], axis=1)
                   for t, hh in heads]
            ons, _ = _group_norm_all([_dot(lhs[2 * t + hh], jnp.concatenate([vs[2 * t + hh], state_b[t]], axis=0))
                                      for t, hh in heads])
            out += [(ons[h] * gain_ref[:, h * RET_V_DIM:(h + 1) * RET_V_DIM]) * (rzs[h] * _sigmoid(rzs[h]))
                    for h in range(RET_HEADS)]
            new_states = [states[t] * cd_ref[t]
                          + _dot_tn(jnp.concatenate(_split_heads(_tile(kd, t)), axis=0).astype(BF16),
                                    jnp.concatenate([vs[2 * t], vs[2 * t + 1]], axis=0)) for t in range(PAIRS)]
            mix_ref[rows, :] = jnp.concatenate(out, axis=1).astype(BF16)
            st_ref[j] = jnp.stack(states)
            return ak, av, tuple(new_states)

        carry = (kprev_ref[...], vprev_ref[...], tuple(state_ref[t] for t in range(PAIRS)))
        kp, vp, states = lax.fori_loop(0, nsub, sub, carry)
        kprev_ref[...] = kp
        vprev_ref[...] = vp
        state_ref[...] = jnp.stack(states)

        xo = x_ref[...] + _dot(mix_ref[...], wout_ref[...])
        r2 = lax.rsqrt(jnp.mean(xo * xo, axis=1, keepdims=True) + RMS_EPS)
        xn = xo * r2
        err = xn * fg_ref[...] - t_ref[...]
        loss_ref[...] += jnp.sum(err * err) * (0.5 / D_MODEL)
        dy = err * (1.0 / D_MODEL)
        gfin_ref[...] += jnp.sum(dy * xn, axis=0, keepdims=True)
        u = dy * fg_ref[...]
        dxo_ref[...] = r2 * u - xn * (r2 * jnp.mean(u * xn, axis=1, keepdims=True))

    blk_rows = lambda w: pl.BlockSpec((tb, w), lambda i: (i, 0))
    state_shape = (PAIRS, 2 * RET_QK_DIM, RET_V_DIM)
    return _call(
        body, name="mix_fwd", grid=(seq // tb,),
        out_shape=(
            jax.ShapeDtypeStruct((seq, MIX_WIDTH), BF16),
            jax.ShapeDtypeStruct((seq, D_MODEL), F32),
            jax.ShapeDtypeStruct((seq // BLK,) + state_shape, F32),
            jax.ShapeDtypeStruct((8, LANE), F32),
            jax.ShapeDtypeStruct((1, D_MODEL), F32),
            jax.ShapeDtypeStruct((seq, ATT_WIDTH), F32),
            jax.ShapeDtypeStruct((seq // BLK, len(ATT_PROBLEMS), BLK, 2 * BLK), BF16),
            jax.ShapeDtypeStruct((seq // BLK, len(ATT_PROBLEMS), 2 * BLK), F32),
        ),
        in_specs=[
            pl.BlockSpec((N_CHIPS, tb, SHARD_PAD), lambda i: (0, i, 0)),
            blk_rows(D_MODEL), blk_rows(D_MODEL), blk_rows(LANE), blk_rows(LANE),
            _const_spec((MIX_WIDTH, D_MODEL)), _const_spec((1, D_MODEL)), _const_spec((1, RET_WIDTH)),
            pl.BlockSpec(memory_space=pltpu.SMEM),
            _const_spec((PAIRS, 2 * BLK, BLK)), _const_spec((BLK, RET_QK_WIDTH)), _const_spec((BLK, RET_QK_WIDTH)),
            _const_spec(state_shape), _const_spec((2, BLK, 2 * BLK)),
        ],
        out_specs=(
            blk_rows(MIX_WIDTH), blk_rows(D_MODEL),
            pl.BlockSpec((nsub,) + state_shape, lambda i: (i, 0, 0, 0)),
            _const_spec((8, LANE)), _const_spec((1, D_MODEL)), blk_rows(ATT_WIDTH),
            pl.BlockSpec((nsub, len(ATT_PROBLEMS), BLK, 2 * BLK), lambda i: (i, 0, 0, 0)),
            pl.BlockSpec((nsub, len(ATT_PROBLEMS), 2 * BLK), lambda i: (i, 0, 0)),
        ),
        scratch_shapes=[
            pltpu.VMEM((BLK, ATT_KV_WIDTH), F32), pltpu.VMEM((BLK, ATT_KV_WIDTH), F32),
            pltpu.VMEM(state_shape, F32),
        ],
        compiler_params=_params(48, ("arbitrary",)),
    )(proj, x, target, cos_t, sin_t, w_out, final_g, gn_gain, sinks, decay_in, qdec_t, kdec_t, cd_t, bias_t)


def _mix_bwd(proj, dxo, mix, o_att, probs, p_sinks, states, x, w_out, w_in_t, norm_g, gn_gain, tables, tb):
    seq = dxo.shape[0]
    nsub = tb // BLK
    nblk = seq // tb
    cos_t, sin_t, decay_in, qdec_t, kdec_t, cd_t, _ = tables
    kv_cols = OFF_AK // (2 * ATT_KV_WIDTH)
    state_shape = (PAIRS, 2 * RET_QK_DIM, RET_V_DIM)

    def body(p_ref, pkv_ref, dxo_ref, mix_ref, oatt_ref, probs_ref, psink_ref, st_ref, cos_ref, sin_ref, x_ref, wout_ref,
             win_ref, g_ref, gain_ref, din_ref, qdec_ref, kdec_ref, cd_ref,
             dp_ref, gx_ref, gwout_ref, gnorm_ref, dgain_ref, dsink_ref,
             dmix_ref, kv_ref, dkc_ref, dvc_ref, gst_ref):
        i = pl.program_id(0)

        @pl.when(i == 0)
        def _():
            gwout_ref[...] = jnp.zeros_like(gwout_ref)
            gnorm_ref[...] = jnp.zeros_like(gnorm_ref)
            dgain_ref[...] = jnp.zeros_like(dgain_ref)
            dsink_ref[...] = jnp.zeros_like(dsink_ref)
            dkc_ref[...] = jnp.zeros_like(dkc_ref)
            dvc_ref[...] = jnp.zeros_like(dvc_ref)
            gst_ref[...] = jnp.zeros_like(gst_ref)

        dxo_b = dxo_ref[...].astype(BF16)
        dmix_ref[...] = _dot_nt(dxo_b, wout_ref[...])
        gwout_ref[...] += _dot_tn(mix_ref[...], dxo_b)
        kv_ref[0:BLK, :] = pkv_ref[...].astype(F32)
        kv_ref[BLK:, :] = _proj_cols(p_ref, slice(None), OFF_AK, OFF_AK + 2 * ATT_KV_WIDTH).astype(F32)
        low = _low_lanes((BLK, LANE))
        low2 = _low_lanes((2 * BLK, LANE))
        lane = lax.broadcasted_iota(jnp.int32, (1, LANE), 1)
        prev = _in_previous_block()

        def sub(jj, carry):
            dkc, dvc, gsts, dgain, dsink = carry
            j = nsub - 1 - jj
            rows = pl.ds(pl.multiple_of(j * BLK, BLK), BLK)
            both = pl.ds(pl.multiple_of(j * BLK, BLK), 2 * BLK)

            aq = _proj_cols(p_ref, rows, OFF_AQ, OFF_AQ + ATT_WIDTH)
            az = _proj_cols(p_ref, rows, OFF_AZ, OFF_AZ + ATT_WIDTH).astype(F32)
            k_ops = _kv_operands(kv_ref[both, 0:ATT_KV_WIDTH])
            v_ops = _kv_operands(kv_ref[both, ATT_KV_WIDTH:2 * ATT_KV_WIDTH])
            da = dmix_ref[rows, 0:ATT_WIDTH]
            sig = _sigmoid(az)
            d_o = da * (az * sig)
            qs = [(_stack_tiles(aq, 2 * g) * ATT_SCALE).astype(BF16) for g in range(ATT_KV_HEADS)]
            dos = [_stack_tiles(d_o, 2 * g).astype(BF16) for g in range(ATT_KV_HEADS)]
            p_b = [probs_ref[j, k] for k in range(len(ATT_PROBLEMS))]
            p = [pk.astype(F32) for pk in p_b]
            sink_all = psink_ref[j]
            p_sink = [sink_all[k:k + 1, :] for k in range(len(ATT_PROBLEMS))]
            dpr = [_fold(_dot_nt(v_ops[g][hi], dos[g]), prev) for g, hi in ATT_PROBLEMS]
            delta = [jnp.sum(pi * di, axis=0, keepdims=True) for pi, di in zip(p, dpr)]
            ds = [_unfold((pi * (di - ti)).astype(BF16), prev) for pi, di, ti in zip(p, dpr, delta)]
            for (g, hi), ki, ti in zip(ATT_PROBLEMS, p_sink, delta):
                sink_part = ki * ti
                for half in range(2):
                    tot = jnp.sum(sink_part[:, half * BLK:(half + 1) * BLK], axis=1, keepdims=True)
                    dsink = dsink - jnp.where(lane == 4 * g + 2 * half + hi, tot, 0.0)
            dq_tiles, dk_sums, dv_sums = [], [], []
            for g in range(ATT_KV_HEADS):
                ds_cat = jnp.concatenate([ds[2 * g], ds[2 * g + 1]], axis=0)
                p_cat = jnp.concatenate([_unfold(p_b[2 * g], prev), _unfold(p_b[2 * g + 1], prev)], axis=0)
                dqs = _dot_tn(ds_cat, jnp.concatenate(k_ops[g], axis=0)) * ATT_SCALE
                dq_tiles += [dqs[0:BLK], dqs[BLK:]]
                dk_sums.append(_dot(ds_cat, qs[g]))
                dv_sums.append(_dot(p_cat, dos[g]))
            daz = da * oatt_ref[rows, :] * (sig * (1.0 + az * (1.0 - sig)))

            def kv_grad(sums):
                (a0, b0), (a1, b1) = [(s[0:2 * BLK], s[2 * BLK:]) for s in sums]
                return jnp.where(low2, a0, b1) + pltpu.roll(jnp.where(low2, a1, b0), HALF_LANE, 1)

            dk_both, dv_both = kv_grad(dk_sums), kv_grad(dv_sums)
            dak = dk_both[BLK:] + dkc
            dav = dv_both[BLK:] + dvc

            cos_b, sin_b = cos_ref[rows, :], sin_ref[rows, :]
            qdec, kdec = qdec_ref[...], kdec_ref[...]
            qr, kr, qd, kd = _retention_operands(p_ref, rows, cos_b, sin_b, qdec, kdec)
            heads = [(t, hh) for t in range(PAIRS) for hh in range(2)]
            head_cols = [slice(h * RET_V_DIM, (h + 1) * RET_V_DIM) for h in range(RET_HEADS)]
            q_rows = [jnp.concatenate(_split_heads(_tile(qr, t)), axis=0).astype(BF16) for t in range(PAIRS)]
            k_rows = [jnp.concatenate(_split_heads(_tile(kr, t)), axis=0).astype(BF16) for t in range(PAIRS)]
            din = [din_ref[t] for t in range(PAIRS)]
            sc = [(_dot_nt(q_rows[t], _tile(kr, t).astype(BF16)) * din[t]).astype(BF16) for t in range(PAIRS)]
            qd_heads = [_split_heads(_tile(qd, t)) for t in range(PAIRS)]
            kd_heads = [_split_heads(_tile(kd, t)) for t in range(PAIRS)]
            state_b = [st_ref[j, t].astype(BF16) for t in range(PAIRS)]
            gst_b = [gsts[t].astype(BF16) for t in range(PAIRS)]
            vs = [_proj_cols(p_ref, rows, OFF_RV + h * RET_V_DIM, OFF_RV + (h + 1) * RET_V_DIM) for h in range(RET_HEADS)]
            rzs = [_proj_cols(p_ref, rows, OFF_RZ + h * RET_V_DIM, OFF_RZ + (h + 1) * RET_V_DIM).astype(F32)
                   for h in range(RET_HEADS)]
            drs = [dmix_ref[rows, ATT_WIDTH + h * RET_V_DIM:ATT_WIDTH + (h + 1) * RET_V_DIM] for h in range(RET_HEADS)]
            gains = [gain_ref[:, c] for c in head_cols]
            lhs = [jnp.concatenate([sc[t][hh * BLK:(hh + 1) * BLK], qd_heads[t][hh].astype(BF16)], axis=1) for t, hh in heads]
            rhs = [jnp.concatenate([vs[2 * t + hh], state_b[t]], axis=0) for t, hh in heads]
            ons, rstds = _group_norm_all([_dot(l, r) for l, r in zip(lhs, rhs)])
            sig_r = [_sigmoid(z) for z in rzs]
            dgn = [d * (z * g) for d, z, g in zip(drs, rzs, sig_r)]
            dz_parts = [d * (o * gn) * (g * (1.0 + z * (1.0 - g))) for d, o, gn, g, z in zip(drs, ons, gains, sig_r, rzs)]
            dgain_parts = [jnp.sum(d * o, axis=0, keepdims=True) for d, o in zip(dgn, ons)]
            don = [d * gn for d, gn in zip(dgn, gains)]
            mean_don = [jnp.mean(d, axis=1, keepdims=True) for d in don]
            mean_don_on = [jnp.mean(d * o, axis=1, keepdims=True) for d, o in zip(don, ons)]
            dob = [(r * (d - a - o * b)).astype(BF16) for r, d, a, o, b in zip(rstds, don, mean_don, ons, mean_don_on)]
            dlhs = [_dot_nt(d, r) for d, r in zip(dob, rhs)]
            drhs = [_dot_tn(l, d) for l, d in zip(lhs, dob)]
            dkds = [_dot_nt(vs[2 * t + hh], gst_b[t]) for t, hh in heads]
            dv_parts = [drhs[2 * t + hh][0:BLK] + _dot(kd_heads[t][hh].astype(BF16), gst_b[t]) for t, hh in heads]
            das = [(dlhs[2 * t + hh][:, 0:BLK] * din[t][hh * BLK:(hh + 1) * BLK]).astype(BF16) for t, hh in heads]
            new_gsts = [gsts[t] * cd_ref[t] + drhs[2 * t][BLK:] + drhs[2 * t + 1][BLK:] for t in range(PAIRS)]
            dq_parts = [_dot(jnp.concatenate([das[2 * t], das[2 * t + 1]], axis=1), k_rows[t])
                        + jnp.where(low, dlhs[2 * t][:, BLK:], dlhs[2 * t + 1][:, BLK:]) * _tile(qdec, t)
                        for t in range(PAIRS)]
            dk_parts = [_dot_tn(jnp.concatenate([das[2 * t], das[2 * t + 1]], axis=0), q_rows[t])
                        + jnp.where(low, dkds[2 * t], dkds[2 * t + 1]) * _tile(kdec, t) for t in range(PAIRS)]
            drq = _rotate_transposed(jnp.concatenate(dq_parts, axis=1), cos_b, sin_b)
            drk = _rotate_transposed(jnp.concatenate(dk_parts, axis=1) * RET_SCALE, cos_b, sin_b)

            dp_ref[rows, :] = jnp.concatenate(
                [jnp.concatenate(dq_tiles, axis=1), dak, dav, daz, drq, drk] + dv_parts + dz_parts, axis=1).astype(BF16)
            dgain = dgain + jnp.concatenate(dgain_parts, axis=1)
            return dk_both[0:BLK], dv_both[0:BLK], tuple(new_gsts), dgain, dsink

        carry = (dkc_ref[...], dvc_ref[...], tuple(gst_ref[t] for t in range(PAIRS)), dgain_ref[...], dsink_ref[...])
        dkc, dvc, gsts, dgain, dsink = lax.fori_loop(0, nsub, sub, carry)
        dkc_ref[...] = dkc
        dvc_ref[...] = dvc
        gst_ref[...] = jnp.stack(gsts)
        dgain_ref[...] = dgain
        dsink_ref[...] = dsink

        dh = _dot(dp_ref[...], win_ref[...])
        xv = x_ref[...]
        r = lax.rsqrt(jnp.mean(xv * xv, axis=1, keepdims=True) + RMS_EPS)
        xn = xv * r
        gnorm_ref[...] += jnp.sum(dh * xn, axis=0, keepdims=True)
        u = dh * g_ref[...]
        gx_ref[...] = dxo_ref[...] + r * u - xn * (r * jnp.mean(u * xn, axis=1, keepdims=True))

    rev_rows = lambda w: pl.BlockSpec((tb, w), lambda i: (nblk - 1 - i, 0))
    prev_kv = pl.BlockSpec((None, BLK, 2 * ATT_KV_WIDTH),
                           lambda i: (0, jnp.maximum((nblk - 1 - i) * nsub - 1, 0), kv_cols))
    return _call(
        body, name="mix_bwd", grid=(nblk,),
        out_shape=(
            jax.ShapeDtypeStruct((seq, IN_WIDTH), BF16),
            jax.ShapeDtypeStruct((seq, D_MODEL), F32),
            jax.ShapeDtypeStruct((MIX_WIDTH, D_MODEL), F32),
            jax.ShapeDtypeStruct((1, D_MODEL), F32),
            jax.ShapeDtypeStruct((1, RET_WIDTH), F32),
            jax.ShapeDtypeStruct((1, LANE), F32),
        ),
        in_specs=[
            pl.BlockSpec((N_CHIPS, tb, SHARD_PAD), lambda i: (0, nblk - 1 - i, 0)),
            prev_kv, rev_rows(D_MODEL), rev_rows(MIX_WIDTH), rev_rows(ATT_WIDTH),
            pl.BlockSpec((nsub, len(ATT_PROBLEMS), BLK, 2 * BLK), lambda i: (nblk - 1 - i, 0, 0, 0)),
            pl.BlockSpec((nsub, len(ATT_PROBLEMS), 2 * BLK), lambda i: (nblk - 1 - i, 0, 0)),
            pl.BlockSpec((nsub,) + state_shape, lambda i: (nblk - 1 - i, 0, 0, 0)),
            rev_rows(LANE), rev_rows(LANE), rev_rows(D_MODEL),
            _const_spec((MIX_WIDTH, D_MODEL)), _const_spec((IN_WIDTH, D_MODEL)), _const_spec((1, D_MODEL)),
            _const_spec((1, RET_WIDTH)),
            _const_spec((PAIRS, 2 * BLK, BLK)), _const_spec((BLK, RET_QK_WIDTH)), _const_spec((BLK, RET_QK_WIDTH)),
            _const_spec(state_shape),
        ],
        out_specs=(
            rev_rows(IN_WIDTH), rev_rows(D_MODEL), _const_spec((MIX_WIDTH, D_MODEL)), _const_spec((1, D_MODEL)),
            _const_spec((1, RET_WIDTH)), _const_spec((1, LANE)),
        ),
        scratch_shapes=[
            pltpu.VMEM((tb, MIX_WIDTH), F32),
            pltpu.VMEM((tb + BLK, 2 * ATT_KV_WIDTH), F32),
            pltpu.VMEM((BLK, ATT_KV_WIDTH), F32), pltpu.VMEM((BLK, ATT_KV_WIDTH), F32),
            pltpu.VMEM(state_shape, F32),
        ],
        compiler_params=_params(60, ("arbitrary",)),
    )(proj, proj, dxo, mix, o_att, probs, p_sinks, states, cos_t, sin_t, x, w_out, w_in_t, norm_g, gn_gain, decay_in,
      qdec_t, kdec_t, cd_t)


def _gw_in_reduce(h_t, dproj, gw_out, small, tb):
    seq = dproj.shape[0]
    nblk = seq // tb
    last = nblk - 1
    hand_on = min(1, last)
    half = D_MODEL // 2
    A, B, C, N_SEMS = 0, N_CHIPS, 2 * N_CHIPS, 2 * N_CHIPS + 1

    def body(win_ref, ht_ref, dp_ref, gwo_hbm, s0_ref, s1_ref, s2_ref, s3_ref, s4_ref, out_ref, fout_ref, packsum_ref,
             acc, sib, send_buf, b_in, fin, fout, mine_out, sib_out, send_out, b_out, pack_ref, packs,
             send_sems, recv_sems, local_sems):
        p, i = pl.program_id(0), pl.program_id(1)
        small_start, small_hand_on, small_finish = _small_exchange(
            gwo_hbm, (s0_ref, s1_ref, s2_ref, s3_ref, s4_ref), fout, mine_out, sib_out, send_out, b_out, pack_ref,
            packs, send_sems, recv_sems, local_sems, N_SEMS)

        @pl.when(jnp.logical_and(p == 0, i == 0))
        def _():
            small_start()

        @pl.when(jnp.logical_and(p == 0, i == hand_on))
        def _():
            small_hand_on()

        x, y, c = lax.axis_index("x"), lax.axis_index("y"), lax.axis_index("c")
        chip = 2 * x + y
        sibling = (x, y, 1 - c)
        mine = pl.ds(pl.multiple_of(c * half, half), half)
        other = pl.ds(pl.multiple_of((1 - c) * half, half), half)

        def remote(src, dst, send_k, recv_k, to):
            return pltpu.make_async_remote_copy(src_ref=src, dst_ref=dst, send_sem=send_sems.at[send_k],
                                                recv_sem=recv_sems.at[recv_k], device_id=to, device_id_type=MESH_ID)

        part = _dot(ht_ref[...], dp_ref[...])
        slot = p % 2

        @pl.when(i == 0)
        def _():
            acc[slot] = part

        @pl.when(i > 0)
        def _():
            acc[slot] += part

        for q in range(N_CHIPS):
            s = q % 2
            to_sibling = remote(acc.at[s, other, :], sib.at[s], A + q, A + q, sibling)

            @pl.when(jnp.logical_and(p == q, i == last))
            def _():
                to_sibling.start()

            if q < N_CHIPS - 1:
                dest = (chip + 1 + q) % N_CHIPS

                @pl.when(jnp.logical_and(p == q + 1, i == hand_on))
                def _():
                    to_sibling.wait_recv()
                    send_buf[q] = (acc[s, mine, :] + sib[s]).astype(BF16)
                    remote(send_buf.at[q], b_in.at[chip], B + q, B + chip, (dest // 2, dest % 2, c)).start()
                    to_sibling.wait_send()
            else:
                @pl.when(jnp.logical_and(p == q, i == last))
                def _():
                    to_sibling.wait_recv()
                    fin[mine, :] = acc[s, mine, :] + sib[s]
                    for j in range(N_CHIPS):
                        @pl.when(j != chip)
                        def _():
                            remote(b_in.at[j], b_in.at[j], B + j, B + j, sibling).wait_recv()
                            fin[mine, :] += b_in[j].astype(F32)
                    to_core = remote(fin.at[mine, :], fin.at[mine, :], C, C, sibling)
                    to_core.start()
                    remote(fin.at[other, :], fin.at[other, :], C, C, sibling).wait_recv()
                    out_ref[...] = fin[...]
                    to_core.wait_send()
                    to_sibling.wait_send()
                    for k in range(N_CHIPS - 1):
                        remote(send_buf.at[k], b_in.at[chip], B + k, B + k, sibling).wait_send()
                    packsum_ref[...] = small_finish()
                    fout_ref[...] = fout[...]

    whole = lambda shape: pl.BlockSpec(shape, lambda p, i, win: (0,) * len(shape), pipeline_mode=pl.Buffered(1))
    grid_spec = pltpu.PrefetchScalarGridSpec(
        num_scalar_prefetch=1, grid=(N_CHIPS, nblk),
        in_specs=[pl.BlockSpec((D_MODEL, tb), lambda p, i, win: (0, i)),
                  pl.BlockSpec((pl.Element(tb), pl.Element(SHARD_PAD)),
                               lambda p, i, win: (i * tb, pl.multiple_of(win[p] * LANE, LANE))),
                  pl.BlockSpec(memory_space=pl.ANY)] + [whole(s.shape) for s in small],
        out_specs=(whole((D_MODEL, SHARD_PAD)), whole((SHARD_OUT, D_MODEL)), whole((PACK_ROWS, D_MODEL))),
        scratch_shapes=[
            pltpu.VMEM((2, D_MODEL, SHARD_PAD), F32), pltpu.VMEM((2, half, SHARD_PAD), F32),
            pltpu.VMEM((N_CHIPS - 1, half, SHARD_PAD), BF16), pltpu.VMEM((N_CHIPS, half, SHARD_PAD), BF16),
            pltpu.VMEM((D_MODEL, SHARD_PAD), F32),
        ] + _small_exchange_scratch() + [
            pltpu.SemaphoreType.DMA((N_SEMS + SMALL_SEMS,)), pltpu.SemaphoreType.DMA((N_SEMS + SMALL_SEMS,)),
            pltpu.SemaphoreType.DMA((N_CHIPS,)),
        ])
    chip = 2 * lax.axis_index("x") + lax.axis_index("y")
    owner = (chip + 1 + jnp.arange(N_CHIPS, dtype=jnp.int32)) % N_CHIPS
    win_start = (owner * SHARD_IN) // LANE
    return _call(
        body, name="gw_in_reduce", grid_spec=grid_spec,
        out_shape=(jax.ShapeDtypeStruct((D_MODEL, SHARD_PAD), F32), jax.ShapeDtypeStruct((SHARD_OUT, D_MODEL), F32),
                   jax.ShapeDtypeStruct((PACK_ROWS, D_MODEL), F32)),
        compiler_params=_params(52, ("arbitrary", "arbitrary")),
    )(win_start.astype(jnp.int32), h_t, dproj, gw_out, *small)


SMALL_SEMS = 17


def _small_exchange(gwo_hbm, small_refs, fout_ref, mine_out, sib_out, send_out, b_out, pack_ref, packs,
                    send_sems, recv_sems, local_sems, base):
    half_out = SHARD_OUT // 2
    A_OUT, B_OUT, C_OUT, PACK = base, base + 4, base + 8, base + 9
    assert len(small_refs) == PACK_PARTS
    x, y, c = lax.axis_index("x"), lax.axis_index("y"), lax.axis_index("c")
    chip = 2 * x + y
    dev = 2 * chip + c
    sibling = (x, y, 1 - c)

    def remote(src, dst, send_k, recv_k, to):
        return pltpu.make_async_remote_copy(src_ref=src, dst_ref=dst, send_sem=send_sems.at[send_k],
                                            recv_sem=recv_sems.at[recv_k], device_id=to, device_id_type=MESH_ID)

    def out_rows(j, core):
        return pl.ds(pl.multiple_of(j * SHARD_OUT + core * half_out, half_out), half_out)

    my_out_rows = pl.ds(pl.multiple_of(c * half_out, half_out), half_out)
    local = [pltpu.make_async_copy(gwo_hbm.at[out_rows(j, c), :], mine_out.at[j], local_sems.at[j])
             for j in range(N_CHIPS)]
    stage_a = [remote(gwo_hbm.at[out_rows(j, 1 - c), :], sib_out.at[j], A_OUT + j, A_OUT + j, sibling)
               for j in range(N_CHIPS)]
    mine_half_out = fout_ref.at[my_out_rows, :]
    stage_c = [remote(mine_half_out, mine_half_out, C_OUT, C_OUT, sibling)]

    def start():
        pack_ref[...] = jnp.zeros_like(pack_ref)
        for k, s_ref in enumerate(small_refs):
            pack_ref[k:k + 1, 0:s_ref.shape[1]] = s_ref[0:1, :]
        packs[dev] = pack_ref[...]
        for d in range(N_DEV):
            to = (d // 4, (d // 2) % 2, d % 2)

            @pl.when(d != dev)
            def _():
                remote(pack_ref, packs.at[dev], PACK + d, PACK + dev, to).start()

        for cp in local + stage_a:
            cp.start()

    def hand_on():
        for cp in local:
            cp.wait()
        for cp in stage_a:
            cp.wait_recv()
        for j in range(N_CHIPS):
            mine_out[j] = mine_out[j] + sib_out[j]

        for j in range(N_CHIPS):
            to = (j // 2, j % 2, c)

            @pl.when(j != chip)
            def _():
                send_out[j] = mine_out[j].astype(BF16)
                remote(send_out.at[j], b_out.at[chip], B_OUT + j, B_OUT + chip, to).start()

            @pl.when(j == chip)
            def _():
                fout_ref[my_out_rows, :] = mine_out[j]

    def finish():
        for j in range(N_CHIPS):
            @pl.when(j != chip)
            def _():
                remote(b_out.at[j], b_out.at[j], B_OUT + j, B_OUT + j, sibling).wait_recv()
                fout_ref[my_out_rows, :] += b_out[j].astype(F32)

        for cp in stage_c:
            cp.start()
        other_half_out = fout_ref.at[pl.ds(pl.multiple_of((1 - c) * half_out, half_out), half_out), :]
        remote(other_half_out, other_half_out, C_OUT, C_OUT, sibling).wait_recv()

        for d in range(N_DEV):
            @pl.when(d != dev)
            def _():
                remote(pack_ref, packs.at[d], PACK + d, PACK + d, sibling).wait_recv()
        total = packs[0]
        for d in range(1, N_DEV):
            total = total + packs[d]

        for cp in stage_a + stage_c:
            cp.wait_send()
        for j in range(N_CHIPS):
            @pl.when(j != chip)
            def _():
                remote(b_out.at[j], b_out.at[j], B_OUT + j, B_OUT + j, sibling).wait_send()
        for d in range(N_DEV):
            @pl.when(d != dev)
            def _():
                remote(pack_ref, packs.at[d], PACK + d, PACK + d, sibling).wait_send()
        return total

    return start, hand_on, finish


def _small_exchange_scratch():
    half_out = SHARD_OUT // 2
    return [
        pltpu.VMEM((SHARD_OUT, D_MODEL), F32),
        pltpu.VMEM((N_CHIPS, half_out, D_MODEL), F32), pltpu.VMEM((N_CHIPS, half_out, D_MODEL), F32),
        pltpu.VMEM((N_CHIPS, half_out, D_MODEL), BF16), pltpu.VMEM((N_CHIPS, half_out, D_MODEL), BF16),
        pltpu.VMEM((PACK_ROWS, D_MODEL), F32), pltpu.VMEM((N_DEV, PACK_ROWS, D_MODEL), F32),
    ]


def _adam_math(w, g, m, v):
    mn = ADAM_B1 * m + (1.0 - ADAM_B1) * g
    vn = ADAM_B2 * v + (1.0 - ADAM_B2) * (g * g)
    m_hat = mn / (1.0 - ADAM_B1 ** ADAM_STEP)
    v_hat = vn / (1.0 - ADAM_B2 ** ADAM_STEP)
    return -ADAM_LR * (m_hat / (jnp.sqrt(v_hat) + ADAM_EPS) + ADAM_WD * w), mn, vn


def _adamw(name, w, g, m, v, tb):
    rows, cols = w.shape

    def body(w_ref, g_ref, m_ref, v_ref, go_ref, d_ref, mo_ref, vo_ref):
        gv = g_ref[...]
        go_ref[...] = gv
        d_ref[...], mo_ref[...], vo_ref[...] = _adam_math(w_ref[...], gv, m_ref[...], v_ref[...])

    spec = pl.BlockSpec((tb, cols), lambda i: (i, 0))
    shape = jax.ShapeDtypeStruct((rows, cols), F32)
    return _call(
        body, name=name, grid=(rows // tb,), out_shape=(shape,) * 4,
        in_specs=[spec] * 4, out_specs=(spec,) * 4,
        compiler_params=_params(32, ("arbitrary",)),
    )(w, g, m, v)


def _adamw_w_in(w_t, g_window, m_t, v_t, tb):
    def body(w_ref, g_ref, m_ref, v_ref, go_ref, d_ref, mo_ref, vo_ref, gt_ref):
        gt_ref[...] = g_ref[...].T
        gv = gt_ref[pl.ds(pl.multiple_of(lax.axis_index("y") * SHARD_SHIFT, SHARD_SHIFT), SHARD_IN), :]
        go_ref[...] = gv
        d_ref[...], mo_ref[...], vo_ref[...] = _adam_math(w_ref[...], gv, m_ref[...], v_ref[...])

    spec = pl.BlockSpec((SHARD_IN, tb), lambda i: (0, i))
    shape = jax.ShapeDtypeStruct((SHARD_IN, D_MODEL), F32)
    return _call(
        body, name="adamw_w_in", grid=(D_MODEL // tb,), out_shape=(shape,) * 4,
        in_specs=[spec, pl.BlockSpec((tb, SHARD_PAD), lambda i: (i, 0)), spec, spec], out_specs=(spec,) * 4,
        scratch_shapes=[pltpu.VMEM((SHARD_PAD, tb), F32)],
        compiler_params=_params(32, ("arbitrary",)),
    )(w_t, g_window, m_t, v_t)


def _adamw_small(sums, params):
    def body(sums_ref, *refs):
        ins, outs = refs[:3 * len(params)], refs[3 * len(params):]
        for k in range(len(params)):
            w_ref, m_ref, v_ref = ins[3 * k:3 * k + 3]
            g = sums_ref[k:k + 1, 0:w_ref.shape[1]]
            go_ref, d_ref, mo_ref, vo_ref = outs[4 * k:4 * k + 4]
            go_ref[...] = g
            d_ref[...], mo_ref[...], vo_ref[...] = _adam_math(w_ref[...], g, m_ref[...], v_ref[...])

    vmem = pl.BlockSpec(memory_space=pltpu.VMEM)
    flat = [a for p in params for a in p]
    shapes = tuple(jax.ShapeDtypeStruct(p[0].shape, F32) for p in params for _ in range(4))
    res = _call(body, name="adamw_small", out_shape=shapes, in_specs=[vmem] * (1 + len(flat)),
                out_specs=(vmem,) * len(shapes), compiler_params=_params(16))(sums, *flat)
    return [res[4 * k:4 * k + 4] for k in range(len(params))]


def kernel(x, norm_g, w_in, att_sinks, ret_gn_g, w_out, final_g, loss_target, m_norm_g, m_w_in, m_att_sinks, m_ret_gn_g, m_w_out, m_final_g, v_norm_g, v_w_in, v_att_sinks, v_ret_gn_g, v_w_out, v_final_g):
    seq = x.shape[1]
    xs, tgt = x[0], loss_target[0]
    final_g2 = final_g.reshape(1, D_MODEL)
    tables = _tables(seq)

    w_in_t, m_w_in_t, v_w_in_t = w_in[0].T, m_w_in[0].T, v_w_in[0].T
    proj, h_t, w_in_full, w_out_full = _in_proj(xs, norm_g, w_in_t, w_out[0], min(TOKENS_PROJ, seq))
    mix, dxo, states, loss_part, gfin, o_att, probs, p_sinks = _mix_fwd(
        proj, xs, tgt, w_out_full, final_g2, ret_gn_g, att_sinks, tables, min(TOKENS_MIX, seq))
    dproj, grad_x, gw_out, gnorm, dgain, dsink = _mix_bwd(
        proj, dxo, mix, o_att, probs, p_sinks, states, xs, w_out_full, w_in_full, norm_g, ret_gn_g, tables,
        min(TOKENS_MIX, seq))
    g_in, g_out, sums = _gw_in_reduce(h_t, dproj, gw_out, (gnorm, gfin, dgain, dsink, loss_part),
                                      min(TOKENS_GW, seq))

    res_in = [r.T for r in _adamw_w_in(w_in_t, g_in, m_w_in_t, v_w_in_t, 256)]
    res_out = _adamw("adamw_w_out", w_out[0], g_out, m_w_out[0], v_w_out[0], SHARD_OUT)
    as_row = lambda a: a.reshape(1, D_MODEL)
    r_norm, r_final, r_gain, r_sink = _adamw_small(sums, [
        (norm_g, m_norm_g, v_norm_g), (final_g2, as_row(m_final_g), as_row(v_final_g)),
        (ret_gn_g, m_ret_gn_g, v_ret_gn_g), (att_sinks, m_att_sinks, v_att_sinks)])

    outs = []
    for k in range(4):
        outs += [r_norm[k], res_in[k][None], r_sink[k], r_gain[k], res_out[k][None], r_final[k].reshape(D_MODEL)]
    return (sums[4, 0], grad_x[None], *outs)
```

```python
import jax
import jax.numpy as jnp
import numpy as np
from jax import lax
from jax.experimental import pallas as pl
from jax.experimental.pallas import tpu as pltpu

F32 = jnp.float32
BF16 = jnp.bfloat16

D_MODEL = 1024
ATT_HEADS = 8
ATT_KV_HEADS = 2
ATT_HEAD_DIM = 64
RET_HEADS = 4
RET_QK_DIM = 64
RET_V_DIM = 128
BLK = 128
ROT_BASE = 10000.0
RMS_EPS = 1e-6
GN_EPS = 1e-6
NEG_INF = -1e30
ATT_SCALE = ATT_HEAD_DIM ** -0.5
RET_SCALE = RET_QK_DIM ** -0.5

ATT_WIDTH = ATT_HEADS * ATT_HEAD_DIM
ATT_KV_WIDTH = ATT_KV_HEADS * ATT_HEAD_DIM
RET_QK_WIDTH = RET_HEADS * RET_QK_DIM
RET_WIDTH = RET_HEADS * RET_V_DIM
MIX_WIDTH = ATT_WIDTH + RET_WIDTH
OFF_AQ = 0
OFF_AK = OFF_AQ + ATT_WIDTH
OFF_AV = OFF_AK + ATT_KV_WIDTH
OFF_AZ = OFF_AV + ATT_KV_WIDTH
OFF_RQ = OFF_AZ + ATT_WIDTH
OFF_RK = OFF_RQ + RET_QK_WIDTH
OFF_RV = OFF_RK + RET_QK_WIDTH
OFF_RZ = OFF_RV + RET_WIDTH
IN_WIDTH = OFF_RZ + RET_WIDTH

LANE = 128
BF16_ROWS = 16
HALF_LANE = LANE // 2
PAIRS = RET_QK_WIDTH // LANE
assert ATT_HEAD_DIM == HALF_LANE and RET_QK_DIM == HALF_LANE and RET_V_DIM == LANE and ATT_KV_WIDTH == LANE

N_CHIPS = 4
N_DEV = 8
SHARD_IN = IN_WIDTH // N_CHIPS
SHARD_PAD = 768
SHARD_SHIFT = SHARD_PAD - SHARD_IN
WIN_START = tuple((j * SHARD_IN) // LANE * LANE for j in range(N_CHIPS))
SHARD_OUT = MIX_WIDTH // N_CHIPS

PACK_PARTS = 5
PACK_ROWS = 8
assert PACK_PARTS <= PACK_ROWS

ADAM_LR = 0.001
ADAM_B1 = 0.9
ADAM_B2 = 0.999
ADAM_EPS = 1e-08
ADAM_WD = 0.01
ADAM_STEP = 10

VMEM_CAP = 64 * 1024 * 1024
TOKENS_PROJ = 1024
TOKENS_MIX = 512
TOKENS_GW = 2048
MESH_ID = pl.DeviceIdType.MESH


def _call(body, **kw):
    return pl.pallas_call(body, **kw)


def _params(vmem_mb, semantics=None):
    assert vmem_mb * 1024 * 1024 < VMEM_CAP
    return pltpu.CompilerParams(dimension_semantics=semantics, vmem_limit_bytes=vmem_mb * 1024 * 1024)


def _dot(a, b):
    return jnp.dot(a, b, preferred_element_type=F32)


def _dot_nt(a, b):
    return lax.dot_general(a, b, (((1,), (1,)), ((), ())), preferred_element_type=F32)


def _dot_tn(a, b):
    return lax.dot_general(a, b, (((0,), (0,)), ((), ())), preferred_element_type=F32)


def _sigmoid(z):
    return 1.0 / (1.0 + jnp.exp(-z))


def _const_spec(shape):
    nd = len(shape)
    return pl.BlockSpec(shape, lambda i: (0,) * nd, pipeline_mode=pl.Buffered(1))


def _tables(seq):
    f32 = np.float32
    pos = np.arange(seq, dtype=f32)
    theta = (f32(1.0) / (f32(ROT_BASE) ** np.linspace(0.0, 1.0, RET_QK_DIM // 2, dtype=f32))).astype(f32)
    ang = (pos[:, None] * theta[None, :]).astype(f32)
    cos, sin = np.cos(ang), np.sin(ang)
    cos2 = np.repeat(cos, 2, axis=1)
    sin2 = np.stack([-sin, sin], axis=-1).reshape(seq, RET_QK_DIM)
    cos_t = np.tile(cos2, (1, 2))
    sin_t = np.tile(sin2, (1, 2))

    log_gamma = np.log(f32(1.0) - f32(2.0) ** (f32(-5.0) - np.arange(RET_HEADS, dtype=f32))).astype(f32)
    idx = np.arange(BLK, dtype=f32)
    rel = idx[:, None] - idx[None, :]
    decay_in = np.where(rel >= 0, np.exp(log_gamma[:, None, None] * np.maximum(rel, f32(0.0))), f32(0.0))
    k_dec = np.exp(log_gamma[:, None] * (BLK - 1 - idx)[None, :])
    q_dec = np.exp(log_gamma[:, None] * (idx + 1)[None, :])
    chunk_decay = np.exp(log_gamma * f32(BLK))
    kdec_t = np.repeat(k_dec.T, RET_QK_DIM, axis=1)
    qdec_t = np.repeat(q_dec.T, RET_QK_DIM, axis=1)
    cd_t = np.broadcast_to(chunk_decay[:, None, None], (RET_HEADS, RET_QK_DIM, RET_V_DIM))
    decay_in = decay_in.reshape(PAIRS, 2 * BLK, BLK)
    cd_t = cd_t.reshape(PAIRS, 2 * RET_QK_DIM, RET_V_DIM)

    key = np.arange(BLK)[:, None]
    query = np.arange(2 * BLK)[None, :] % BLK
    bias = np.stack([np.where(key > query, NEG_INF, 0.0), np.zeros((BLK, 2 * BLK))])
    return tuple(jnp.asarray(np.ascontiguousarray(a), F32) for a in (cos_t, sin_t, decay_in, qdec_t, kdec_t, cd_t, bias))


def _low_lanes(shape):
    lane = lax.broadcasted_iota(jnp.int32, shape, len(shape) - 1)
    return (lane & HALF_LANE) == 0


def _split_heads(t):
    low = _low_lanes(t.shape)
    zero = jnp.zeros_like(t)
    return jnp.where(low, t, zero), jnp.where(low, zero, t)


def _swap_pairs(t):
    lane = lax.broadcasted_iota(jnp.int32, t.shape, 1)
    nxt = pltpu.roll(t, t.shape[1] - 1, 1)
    prv = pltpu.roll(t, 1, 1)
    return jnp.where((lane & 1) == 0, nxt, prv)


def _per_tile(fn, t):
    return jnp.concatenate([fn(_tile(t, i)) for i in range(t.shape[1] // LANE)], axis=1)


def _rotate(t, cos_t, sin_t):
    return _per_tile(lambda a: a * cos_t + _swap_pairs(a) * sin_t, t)


def _rotate_transposed(d, cos_t, sin_t):
    return _per_tile(lambda a: a * cos_t + _swap_pairs(a * sin_t), d)


def _kv_operands(cat):
    low = _low_lanes(cat.shape)
    swapped = pltpu.roll(cat, HALF_LANE, 1)
    zero = jnp.zeros_like(cat)
    pick = lambda a, b: jnp.where(low, a, b).astype(BF16)
    return ((pick(cat, zero), pick(zero, swapped)), (pick(swapped, zero), pick(zero, cat)))


def _stack_tiles(t, first_tile):
    a = t[:, first_tile * LANE:(first_tile + 1) * LANE]
    b = t[:, (first_tile + 1) * LANE:(first_tile + 2) * LANE]
    return jnp.concatenate([a, b], axis=0)


def _sink_rows(sinks_ref, group):
    first = lax.broadcasted_iota(jnp.int32, (1, 2 * BLK), 1) < BLK

    def row(h0, h1):
        return jnp.where(first, sinks_ref[0, group * 4 + h0], sinks_ref[0, group * 4 + h1])
    return row(0, 2), row(1, 3)


ATT_PROBLEMS = tuple((g, hi) for g in range(ATT_KV_HEADS) for hi in range(2))


def _in_previous_block():
    key = lax.broadcasted_iota(jnp.int32, (BLK, 2 * BLK), 0)
    query = lax.broadcasted_iota(jnp.int32, (BLK, 2 * BLK), 1) & (BLK - 1)
    return key > query


def _fold(t, prev):
    return jnp.where(prev, t[0:BLK], t[BLK:])


def _unfold(t, prev):
    zero = jnp.zeros_like(t)
    return jnp.concatenate([jnp.where(prev, t, zero), jnp.where(prev, zero, t)], axis=0)


def _attn_probs(qs, k_ops, bias, prev, sinks_ref):
    sink = [_sink_rows(sinks_ref, g)[hi] for g, hi in ATT_PROBLEMS]
    s = [_fold(_dot_nt(k_ops[g][hi], qs[g]), prev) + bias for g, hi in ATT_PROBLEMS]
    m = [jnp.maximum(jnp.max(si, axis=0, keepdims=True), ki) for si, ki in zip(s, sink)]
    e = [jnp.exp(si - mi) for si, mi in zip(s, m)]
    es = [jnp.exp(ki - mi) for ki, mi in zip(sink, m)]
    inv = [1.0 / (jnp.sum(ei, axis=0, keepdims=True) + esi) for ei, esi in zip(e, es)]
    return [ei * ii for ei, ii in zip(e, inv)], [esi * ii for esi, ii in zip(es, inv)]


def _group_norm_all(outs):
    mu = [jnp.mean(o, axis=1, keepdims=True) for o in outs]
    xc = [o - m for o, m in zip(outs, mu)]
    var = [jnp.mean(c * c, axis=1, keepdims=True) for c in xc]
    rstd = [lax.rsqrt(v + GN_EPS) for v in var]
    return [c * r for c, r in zip(xc, rstd)], rstd


def _proj_cols(p_ref, rows, start, stop):
    pieces = []
    while start < stop:
        k = max(j for j in range(N_CHIPS) if WIN_START[j] <= start)
        end = min(stop, WIN_START[k] + SHARD_PAD)
        pieces.append(p_ref[k, rows, start - WIN_START[k]:end - WIN_START[k]])
        start = end
    return pieces[0] if len(pieces) == 1 else jnp.concatenate(pieces, axis=1)


def _retention_operands(p_ref, rows, cos_b, sin_b, qdec, kdec):
    qr = _rotate(_proj_cols(p_ref, rows, OFF_RQ, OFF_RQ + RET_QK_WIDTH).astype(F32), cos_b, sin_b)
    kr = _rotate(_proj_cols(p_ref, rows, OFF_RK, OFF_RK + RET_QK_WIDTH).astype(F32), cos_b, sin_b) * RET_SCALE
    return qr, kr, qr * qdec, kr * kdec


def _tile(t, i):
    return t[:, i * LANE:(i + 1) * LANE]


GATHER_SEMS = 7


def _allgather_steps(src_ref, full_ref, blk_ref, send_sems, recv_sems):
    block_rows = blk_ref.shape[0]
    assert block_rows % BF16_ROWS == 0 and full_ref.shape[0] == N_DEV * block_rows
    x, y, c = lax.axis_index("x"), lax.axis_index("y"), lax.axis_index("c")
    me, sibling = (x, y, c), (x, y, 1 - c)
    chips = [(1 - x, y), (x, 1 - y), (1 - x, 1 - y)]

    def rows(px, py, pc):
        return full_ref.at[pl.ds(pl.multiple_of((4 * px + 2 * py + pc) * block_rows, BF16_ROWS), block_rows), :]

    def copy(k, block, to, src=None):
        return pltpu.make_async_remote_copy(
            src_ref=rows(*block) if src is None else src, dst_ref=rows(*block),
            send_sem=send_sems.at[k], recv_sem=recv_sems.at[k], device_id=to, device_id_type=MESH_ID)

    first = [copy(0, me, sibling, src=blk_ref)] + [copy(1 + j, me, (*chip, c), src=blk_ref) for j, chip in enumerate(chips)]
    passed = [copy(4 + j, (*chip, c), sibling) for j, chip in enumerate(chips)]

    def start():
        blk_ref[...] = src_ref[pl.ds(pl.multiple_of(c * block_rows, BF16_ROWS), block_rows), :].astype(BF16)
        rows(*me)[...] = blk_ref[...]
        for cp in first:
            cp.start()

    def forward():
        for j, chip in enumerate(chips):
            copy(1 + j, (*chip, c), me).wait_recv()
            passed[j].start()

    def finish():
        copy(0, sibling, me).wait_recv()
        for j, chip in enumerate(chips):
            copy(4 + j, (*chip, 1 - c), me).wait_recv()
        for cp in first + passed:
            cp.wait_send()

    return start, forward, finish


WINDOW_HALF = SHARD_PAD // 2
WINDOW_SEMS = GATHER_SEMS + 1


def _window_gather(wi_ref, w4_ref, blk_ref, edge_ref, send_sems, recv_sems):
    x, y, c = lax.axis_index("x"), lax.axis_index("y"), lax.axis_index("c")
    me, sibling, pair = (x, y, c), (x, y, 1 - c), (x, 1 - y, c)
    chips = [(1 - x, y), (x, 1 - y), (1 - x, 1 - y)]
    own_rows, edge_rows = SHARD_IN - WINDOW_HALF, SHARD_SHIFT
    sends_edge = y == c

    def rows(px, py, pc):
        return w4_ref.at[pl.ds(pl.multiple_of((4 * px + 2 * py + pc) * WINDOW_HALF, BF16_ROWS), WINDOW_HALF), :]

    def copy(k, block, to, src=None):
        return pltpu.make_async_remote_copy(
            src_ref=rows(*block) if src is None else src, dst_ref=rows(*block),
            send_sem=send_sems.at[k], recv_sem=recv_sems.at[k], device_id=to, device_id_type=MESH_ID)

    edge_copy = pltpu.make_async_remote_copy(
        src_ref=edge_ref, dst_ref=edge_ref, send_sem=send_sems.at[GATHER_SEMS], recv_sem=recv_sems.at[GATHER_SEMS],
        device_id=pair, device_id_type=MESH_ID)
    first = [copy(0, me, sibling, src=blk_ref)] + [copy(1 + j, me, (*chip, c), src=blk_ref) for j, chip in enumerate(chips)]
    passed = [copy(4 + j, (*chip, c), sibling) for j, chip in enumerate(chips)]

    def at(start, size):
        return pl.ds(pl.multiple_of(start, BF16_ROWS), size)

    def start():
        @pl.when(sends_edge)
        def _():
            edge_ref[...] = wi_ref[at((1 - y) * (SHARD_IN - edge_rows), edge_rows), :].astype(BF16)
            edge_copy.start()
            blk_ref[...] = wi_ref[at(c * own_rows, WINDOW_HALF), :].astype(BF16)

        @pl.when(jnp.logical_not(sends_edge))
        def _():
            blk_ref[at(y * edge_rows, own_rows), :] = wi_ref[at(c * WINDOW_HALF, own_rows), :].astype(BF16)
            edge_copy.wait_recv()
            blk_ref[at((1 - y) * own_rows, edge_rows), :] = edge_ref[...]

        rows(*me)[...] = blk_ref[...]
        for cp in first[:-1]:
            cp.start()
        copy(0, sibling, me).wait_recv()

    def send_far():
        first[-1].start()

    def forward(j):
        copy(1 + j, (*chips[j], c), me).wait_recv()
        passed[j].start()

    def wait_forwarded(j):
        copy(4 + j, (*chips[j], 1 - c), me).wait_recv()

    def finish():
        for cp in first + passed:
            cp.wait_send()

        @pl.when(sends_edge)
        def _():
            edge_copy.wait_send()

    return start, send_far, forward, wait_forwarded, finish


def _in_proj(x, norm_g, w_in_t_shard, w_out_shard, tb):
    seq = x.shape[0]
    nblk = seq // tb
    last = nblk - 1
    chip_of_panel = (None, 1, 0, 2)

    def body(win_ref, x_ref, g_ref, wi_ref, wo_ref, p_ref, ht_ref, wt_ref, wout_ref,
             w4, blk, edge, hbuf, wout_full, blko, send_sems, recv_sems, send_sems_o, recv_sems_o):
        q, i = pl.program_id(0), pl.program_id(1)
        chip = 2 * lax.axis_index("x") + lax.axis_index("y")
        in_start, in_send_far, in_forward, in_wait_forwarded, in_finish = _window_gather(
            wi_ref, w4, blk, edge, send_sems, recv_sems)
        out_start, out_forward, out_finish = _allgather_steps(wo_ref, wout_full, blko, send_sems_o, recv_sems_o)
        rows = pl.ds(pl.multiple_of(i * tb, tb), tb)

        @pl.when(jnp.logical_and(q == 0, i == 0))
        def _():
            in_start()

        @pl.when(jnp.logical_and(q == 0, i == min(last // 2 + 1, last)))
        def _():
            in_send_far()

        @pl.when(jnp.logical_and(q == 1, i == 0))
        def _():
            out_start()

        for panel in range(1, N_CHIPS):
            @pl.when(jnp.logical_and(q == panel, i == 0))
            def _():
                in_wait_forwarded(chip_of_panel[panel])

        @pl.when(q == 0)
        def _():
            xv = x_ref[...]
            r = lax.rsqrt(jnp.mean(xv * xv, axis=1, keepdims=True) + RMS_EPS)
            h = (xv * r) * g_ref[...]
            hbuf[rows, :] = h.astype(BF16)
            ht_ref[...] = h.T.astype(BF16)

        owner = jnp.bitwise_xor(chip, q)
        window = w4[pl.ds(pl.multiple_of(owner * SHARD_PAD, SHARD_PAD), SHARD_PAD), :]
        p_ref[...] = _dot_nt(hbuf[rows, :], window).astype(BF16)

        for panel in range(1, N_CHIPS):
            @pl.when(jnp.logical_and(q == panel - 1, i == last))
            def _():
                in_forward(chip_of_panel[panel])

        @pl.when(jnp.logical_and(q == N_CHIPS - 1, i == 0))
        def _():
            out_forward()

        @pl.when(jnp.logical_and(q == N_CHIPS - 1, i == last))
        def _():
            in_finish()
            for k in range(N_CHIPS):
                first = k * SHARD_PAD + (k % 2) * SHARD_SHIFT
                wt_ref[k * SHARD_IN:(k + 1) * SHARD_IN, :] = w4[first:first + SHARD_IN, :]
            out_finish()
            wout_ref[...] = wout_full[...]

    whole = lambda shape: pl.BlockSpec(shape, lambda q, i, win: (0,) * len(shape), pipeline_mode=pl.Buffered(1))
    first_panel = lambda q, i: jnp.where(q == 0, i, last)
    grid_spec = pltpu.PrefetchScalarGridSpec(
        num_scalar_prefetch=1, grid=(N_CHIPS, nblk),
        in_specs=[pl.BlockSpec((tb, D_MODEL), lambda q, i, win: (first_panel(q, i), 0)), whole((1, D_MODEL)),
                  whole((SHARD_IN, D_MODEL)), whole((SHARD_OUT, D_MODEL))],
        out_specs=(pl.BlockSpec((None, tb, SHARD_PAD), lambda q, i, win: (win[q], i, 0)),
                   pl.BlockSpec((D_MODEL, tb), lambda q, i, win: (0, first_panel(q, i))),
                   whole((IN_WIDTH, D_MODEL)), whole((MIX_WIDTH, D_MODEL))),
        scratch_shapes=[
            pltpu.VMEM((N_CHIPS * SHARD_PAD, D_MODEL), BF16), pltpu.VMEM((WINDOW_HALF, D_MODEL), BF16),
            pltpu.VMEM((SHARD_SHIFT, D_MODEL), BF16), pltpu.VMEM((seq, D_MODEL), BF16),
            pltpu.VMEM((MIX_WIDTH, D_MODEL), BF16), pltpu.VMEM((SHARD_OUT // 2, D_MODEL), BF16),
            pltpu.SemaphoreType.DMA((WINDOW_SEMS,)), pltpu.SemaphoreType.DMA((WINDOW_SEMS,)),
            pltpu.SemaphoreType.DMA((GATHER_SEMS,)), pltpu.SemaphoreType.DMA((GATHER_SEMS,)),
        ])
    chip = 2 * lax.axis_index("x") + lax.axis_index("y")
    owner = jnp.bitwise_xor(chip, jnp.arange(N_CHIPS, dtype=jnp.int32))
    return _call(
        body, name="in_proj", grid_spec=grid_spec,
        out_shape=(jax.ShapeDtypeStruct((N_CHIPS, seq, SHARD_PAD), BF16), jax.ShapeDtypeStruct((D_MODEL, seq), BF16),
                   jax.ShapeDtypeStruct((IN_WIDTH, D_MODEL), BF16), jax.ShapeDtypeStruct((MIX_WIDTH, D_MODEL), BF16)),
        compiler_params=_params(60, ("arbitrary", "arbitrary")),
    )(owner.astype(jnp.int32), x, norm_g, w_in_t_shard, w_out_shard)


def _mix_fwd(proj, x, target, w_out, final_g, gn_gain, sinks, tables, tb):
    seq = x.shape[0]
    nsub = tb // BLK
    cos_t, sin_t, decay_in, qdec_t, kdec_t, cd_t, bias_t = tables

    def body(p_ref, x_ref, t_ref, cos_ref, sin_ref, wout_ref, fg_ref, gain_ref, sinks_ref, din_ref, qdec_ref,
             kdec_ref, cd_ref, bias_ref, mix_ref, dxo_ref, st_ref, loss_ref, gfin_ref, oatt_ref, probs_ref, psink_ref,
             kprev_ref, vprev_ref, state_ref):
        i = pl.program_id(0)

        @pl.when(i == 0)
        def _():
            kprev_ref[...] = jnp.zeros_like(kprev_ref)
            vprev_ref[...] = jnp.zeros_like(vprev_ref)
            state_ref[...] = jnp.zeros_like(state_ref)
            loss_ref[...] = jnp.zeros_like(loss_ref)
            gfin_ref[...] = jnp.zeros_like(gfin_ref)

        prev = _in_previous_block()

        def sub(j, carry):
            kp, vp, states = carry
            rows = pl.ds(pl.multiple_of(j * BLK, BLK), BLK)
            bias = bias_ref[jnp.where(jnp.logical_or(i > 0, j > 0), 1, 0)]

            aq = _proj_cols(p_ref, rows, OFF_AQ, OFF_AQ + ATT_WIDTH)
            ak = _proj_cols(p_ref, rows, OFF_AK, OFF_AK + ATT_KV_WIDTH).astype(F32)
            av = _proj_cols(p_ref, rows, OFF_AV, OFF_AV + ATT_KV_WIDTH).astype(F32)
            az = _proj_cols(p_ref, rows, OFF_AZ, OFF_AZ + ATT_WIDTH).astype(F32)
            k_ops = _kv_operands(jnp.concatenate([kp, ak], axis=0))
            v_ops = _kv_operands(jnp.concatenate([vp, av], axis=0))
            qs = [(_stack_tiles(aq, 2 * g) * ATT_SCALE).astype(BF16) for g in range(ATT_KV_HEADS)]
            p, p_sink = _attn_probs(qs, k_ops, bias, prev, sinks_ref)
            p = [pi.astype(BF16) for pi in p]
            probs_ref[j] = jnp.stack(p)
            psink_ref[j] = jnp.concatenate(p_sink, axis=0)
            o_tiles = []
            for g in range(ATT_KV_HEADS):
                p_cat = jnp.concatenate([_unfold(p[2 * g], prev), _unfold(p[2 * g + 1], prev)], axis=0)
                o = _dot_tn(p_cat, jnp.concatenate(v_ops[g], axis=0))
                o_tiles += [o[0:BLK], o[BLK:]]
            o_att = jnp.concatenate(o_tiles, axis=1)
            oatt_ref[rows, :] = o_att
            out = [o_att * (az * _sigmoid(az))]

            qr, kr, qd, kd = _retention_operands(p_ref, rows, cos_ref[rows, :], sin_ref[rows, :],
                                                 qdec_ref[...], kdec_ref[...])
            heads = [(t, hh) for t in range(PAIRS) for hh in range(2)]
            sc = [_dot_nt(jnp.concatenate(_split_heads(_tile(qr, t)), axis=0).astype(BF16), _tile(kr, t).astype(BF16))
                  * din_ref[t] for t in range(PAIRS)]
            qd_heads = [_split_heads(_tile(qd, t)) for t in range(PAIRS)]
            state_b = [states[t].astype(BF16) for t in range(PAIRS)]
            vs = [_proj_cols(p_ref, rows, OFF_RV + h * RET_V_DIM, OFF_RV + (h + 1) * RET_V_DIM) for h in range(RET_HEADS)]
            rzs = [_proj_cols(p_ref, rows, OFF_RZ + h * RET_V_DIM, OFF_RZ + (h + 1) * RET_V_DIM).astype(F32)
                   for h in range(RET_HEADS)]
            lhs = [jnp.concatenate([sc[t][hh * BLK:(hh + 1) * BLK].astype(BF16), qd_heads[t][hh].astype(BF16)], axis=1)
                   for t, hh in heads]
            ons, _ = _group_norm_all([_dot(lhs[2 * t + hh], jnp.concatenate([vs[2 * t + hh], state_b[t]], axis=0))
                                      for t, hh in heads])
            out += [(ons[h] * gain_ref[:, h * RET_V_DIM:(h + 1) * RET_V_DIM]) * (rzs[h] * _sigmoid(rzs[h]))
                    for h in range(RET_HEADS)]
            new_states = [states[t] * cd_ref[t]
                          + _dot_tn(jnp.concatenate(_split_heads(_tile(kd, t)), axis=0).astype(BF16),
                                    jnp.concatenate([vs[2 * t], vs[2 * t + 1]], axis=0)) for t in range(PAIRS)]
            mix_ref[rows, :] = jnp.concatenate(out, axis=1).astype(BF16)
            st_ref[j] = jnp.stack(states)
            return ak, av, tuple(new_states)

        carry = (kprev_ref[...], vprev_ref[...], tuple(state_ref[t] for t in range(PAIRS)))
        for jb in range(nsub):
            carry = sub(jnp.int32(jb), carry)
        kp, vp, states = carry
        kprev_ref[...] = kp
        vprev_ref[...] = vp
        state_ref[...] = jnp.stack(states)

        xo = x_ref[...] + _dot(mix_ref[...], wout_ref[...])
        r2 = lax.rsqrt(jnp.mean(xo * xo, axis=1, keepdims=True) + RMS_EPS)
        xn = xo * r2
        err = xn * fg_ref[...] - t_ref[...]
        loss_ref[...] += jnp.sum(err * err) * (0.5 / D_MODEL)
        dy = err * (1.0 / D_MODEL)
        gfin_ref[...] += jnp.sum(dy * xn, axis=0, keepdims=True)
        u = dy * fg_ref[...]
        dxo_ref[...] = r2 * u - xn * (r2 * jnp.mean(u * xn, axis=1, keepdims=True))

    blk_rows = lambda w: pl.BlockSpec((tb, w), lambda i: (i, 0))
    state_shape = (PAIRS, 2 * RET_QK_DIM, RET_V_DIM)
    return _call(
        body, name="mix_fwd", grid=(seq // tb,),
        out_shape=(
            jax.ShapeDtypeStruct((seq, MIX_WIDTH), BF16),
            jax.ShapeDtypeStruct((seq, D_MODEL), F32),
            jax.ShapeDtypeStruct((seq // BLK,) + state_shape, F32),
            jax.ShapeDtypeStruct((8, LANE), F32),
            jax.ShapeDtypeStruct((1, D_MODEL), F32),
            jax.ShapeDtypeStruct((seq, ATT_WIDTH), F32),
            jax.ShapeDtypeStruct((seq // BLK, len(ATT_PROBLEMS), BLK, 2 * BLK), BF16),
            jax.ShapeDtypeStruct((seq // BLK, len(ATT_PROBLEMS), 2 * BLK), F32),
        ),
        in_specs=[
            pl.BlockSpec((N_CHIPS, tb, SHARD_PAD), lambda i: (0, i, 0)),
            blk_rows(D_MODEL), blk_rows(D_MODEL), blk_rows(LANE), blk_rows(LANE),
            _const_spec((MIX_WIDTH, D_MODEL)), _const_spec((1, D_MODEL)), _const_spec((1, RET_WIDTH)),
            pl.BlockSpec(memory_space=pltpu.SMEM),
            _const_spec((PAIRS, 2 * BLK, BLK)), _const_spec((BLK, RET_QK_WIDTH)), _const_spec((BLK, RET_QK_WIDTH)),
            _const_spec(state_shape), _const_spec((2, BLK, 2 * BLK)),
        ],
        out_specs=(
            blk_rows(MIX_WIDTH), blk_rows(D_MODEL),
            pl.BlockSpec((nsub,) + state_shape, lambda i: (i, 0, 0, 0)),
            _const_spec((8, LANE)), _const_spec((1, D_MODEL)), blk_rows(ATT_WIDTH),
            pl.BlockSpec((nsub, len(ATT_PROBLEMS), BLK, 2 * BLK), lambda i: (i, 0, 0, 0)),
            pl.BlockSpec((nsub, len(ATT_PROBLEMS), 2 * BLK), lambda i: (i, 0, 0)),
        ),
        scratch_shapes=[
            pltpu.VMEM((BLK, ATT_KV_WIDTH), F32), pltpu.VMEM((BLK, ATT_KV_WIDTH), F32),
            pltpu.VMEM(state_shape, F32),
        ],
        compiler_params=_params(48, ("arbitrary",)),
    )(proj, x, target, cos_t, sin_t, w_out, final_g, gn_gain, sinks, decay_in, qdec_t, kdec_t, cd_t, bias_t)


def _mix_bwd(proj, dxo, mix, o_att, probs, p_sinks, states, x, w_out, w_in_t, norm_g, gn_gain, tables, tb):
    seq = dxo.shape[0]
    nsub = tb // BLK
    nblk = seq // tb
    cos_t, sin_t, decay_in, qdec_t, kdec_t, cd_t, _ = tables
    kv_cols = OFF_AK // (2 * ATT_KV_WIDTH)
    state_shape = (PAIRS, 2 * RET_QK_DIM, RET_V_DIM)

    def body(p_ref, pkv_ref, dxo_ref, mix_ref, oatt_ref, probs_ref, psink_ref, st_ref, cos_ref, sin_ref, x_ref, wout_ref,
             win_ref, g_ref, gain_ref, din_ref, qdec_ref, kdec_ref, cd_ref,
             dp_ref, gx_ref, gwout_ref, gnorm_ref, dgain_ref, dsink_ref,
             dmix_ref, kv_ref, dkc_ref, dvc_ref, gst_ref):
        i = pl.program_id(0)

        @pl.when(i == 0)
        def _():
            gwout_ref[...] = jnp.zeros_like(gwout_ref)
            gnorm_ref[...] = jnp.zeros_like(gnorm_ref)
            dgain_ref[...] = jnp.zeros_like(dgain_ref)
            dsink_ref[...] = jnp.zeros_like(dsink_ref)
            dkc_ref[...] = jnp.zeros_like(dkc_ref)
            dvc_ref[...] = jnp.zeros_like(dvc_ref)
            gst_ref[...] = jnp.zeros_like(gst_ref)

        dxo_b = dxo_ref[...].astype(BF16)
        dmix_ref[...] = _dot_nt(dxo_b, wout_ref[...])
        gwout_ref[...] += _dot_tn(mix_ref[...], dxo_b)
        kv_ref[0:BLK, :] = pkv_ref[...].astype(F32)
        kv_ref[BLK:, :] = _proj_cols(p_ref, slice(None), OFF_AK, OFF_AK + 2 * ATT_KV_WIDTH).astype(F32)
        low = _low_lanes((BLK, LANE))
        low2 = _low_lanes((2 * BLK, LANE))
        lane = lax.broadcasted_iota(jnp.int32, (1, LANE), 1)
        prev = _in_previous_block()

        def sub(jj, carry):
            dkc, dvc, gsts, dgain, dsink = carry
            j = nsub - 1 - jj
            rows = pl.ds(pl.multiple_of(j * BLK, BLK), BLK)
            both = pl.ds(pl.multiple_of(j * BLK, BLK), 2 * BLK)

            aq = _proj_cols(p_ref, rows, OFF_AQ, OFF_AQ + ATT_WIDTH)
            az = _proj_cols(p_ref, rows, OFF_AZ, OFF_AZ + ATT_WIDTH).astype(F32)
            k_ops = _kv_operands(kv_ref[both, 0:ATT_KV_WIDTH])
            v_ops = _kv_operands(kv_ref[both, ATT_KV_WIDTH:2 * ATT_KV_WIDTH])
            da = dmix_ref[rows, 0:ATT_WIDTH]
            sig = _sigmoid(az)
            d_o = da * (az * sig)
            qs = [(_stack_tiles(aq, 2 * g) * ATT_SCALE).astype(BF16) for g in range(ATT_KV_HEADS)]
            dos = [_stack_tiles(d_o, 2 * g).astype(BF16) for g in range(ATT_KV_HEADS)]
            p_b = [probs_ref[j, k] for k in range(len(ATT_PROBLEMS))]
            p = [pk.astype(F32) for pk in p_b]
            sink_all = psink_ref[j]
            p_sink = [sink_all[k:k + 1, :] for k in range(len(ATT_PROBLEMS))]
            dpr = [_fold(_dot_nt(v_ops[g][hi], dos[g]), prev) for g, hi in ATT_PROBLEMS]
            delta = [jnp.sum(pi * di, axis=0, keepdims=True) for pi, di in zip(p, dpr)]
            ds = [_unfold((pi * (di - ti)).astype(BF16), prev) for pi, di, ti in zip(p, dpr, delta)]
            for (g, hi), ki, ti in zip(ATT_PROBLEMS, p_sink, delta):
                sink_part = ki * ti
                for half in range(2):
                    tot = jnp.sum(sink_part[:, half * BLK:(half + 1) * BLK], axis=1, keepdims=True)
                    dsink = dsink - jnp.where(lane == 4 * g + 2 * half + hi, tot, 0.0)
            dq_tiles, dk_sums, dv_sums = [], [], []
            for g in range(ATT_KV_HEADS):
                ds_cat = jnp.concatenate([ds[2 * g], ds[2 * g + 1]], axis=0)
                p_cat = jnp.concatenate([_unfold(p_b[2 * g], prev), _unfold(p_b[2 * g + 1], prev)], axis=0)
                dqs = _dot_tn(ds_cat, jnp.concatenate(k_ops[g], axis=0)) * ATT_SCALE
                dq_tiles += [dqs[0:BLK], dqs[BLK:]]
                dk_sums.append(_dot(ds_cat, qs[g]))
                dv_sums.append(_dot(p_cat, dos[g]))
            daz = da * oatt_ref[rows, :] * (sig * (1.0 + az * (1.0 - sig)))

            def kv_grad(sums):
                (a0, b0), (a1, b1) = [(s[0:2 * BLK], s[2 * BLK:]) for s in sums]
                return jnp.where(low2, a0, b1) + pltpu.roll(jnp.where(low2, a1, b0), HALF_LANE, 1)

            dk_both, dv_both = kv_grad(dk_sums), kv_grad(dv_sums)
            dak = dk_both[BLK:] + dkc
            dav = dv_both[BLK:] + dvc

            cos_b, sin_b = cos_ref[rows, :], sin_ref[rows, :]
            qdec, kdec = qdec_ref[...], kdec_ref[...]
            qr, kr, qd, kd = _retention_operands(p_ref, rows, cos_b, sin_b, qdec, kdec)
            heads = [(t, hh) for t in range(PAIRS) for hh in range(2)]
            head_cols = [slice(h * RET_V_DIM, (h + 1) * RET_V_DIM) for h in range(RET_HEADS)]
            q_rows = [jnp.concatenate(_split_heads(_tile(qr, t)), axis=0).astype(BF16) for t in range(PAIRS)]
            k_rows = [jnp.concatenate(_split_heads(_tile(kr, t)), axis=0).astype(BF16) for t in range(PAIRS)]
            din = [din_ref[t] for t in range(PAIRS)]
            sc = [(_dot_nt(q_rows[t], _tile(kr, t).astype(BF16)) * din[t]).astype(BF16) for t in range(PAIRS)]
            qd_heads = [_split_heads(_tile(qd, t)) for t in range(PAIRS)]
            kd_heads = [_split_heads(_tile(kd, t)) for t in range(PAIRS)]
            state_b = [st_ref[j, t].astype(BF16) for t in range(PAIRS)]
            gst_b = [gsts[t].astype(BF16) for t in range(PAIRS)]
            vs = [_proj_cols(p_ref, rows, OFF_RV + h * RET_V_DIM, OFF_RV + (h + 1) * RET_V_DIM) for h in range(RET_HEADS)]
            rzs = [_proj_cols(p_ref, rows, OFF_RZ + h * RET_V_DIM, OFF_RZ + (h + 1) * RET_V_DIM).astype(F32)
                   for h in range(RET_HEADS)]
            drs = [dmix_ref[rows, ATT_WIDTH + h * RET_V_DIM:ATT_WIDTH + (h + 1) * RET_V_DIM] for h in range(RET_HEADS)]
            gains = [gain_ref[:, c] for c in head_cols]
            lhs = [jnp.concatenate([sc[t][hh * BLK:(hh + 1) * BLK], qd_heads[t][hh].astype(BF16)], axis=1) for t, hh in heads]
            rhs = [jnp.concatenate([vs[2 * t + hh], state_b[t]], axis=0) for t, hh in heads]
            ons, rstds = _group_norm_all([_dot(l, r) for l, r in zip(lhs, rhs)])
            sig_r = [_sigmoid(z) for z in rzs]
            dgn = [d * (z * g) for d, z, g in zip(drs, rzs, sig_r)]
            dz_parts = [d * (o * gn) * (g * (1.0 + z * (1.0 - g))) for d, o, gn, g, z in zip(drs, ons, gains, sig_r, rzs)]
            dgain_parts = [jnp.sum(d * o, axis=0, keepdims=True) for d, o in zip(dgn, ons)]
            don = [d * gn for d, gn in zip(dgn, gains)]
            mean_don = [jnp.mean(d, axis=1, keepdims=True) for d in don]
            mean_don_on = [jnp.mean(d * o, axis=1, keepdims=True) for d, o in zip(don, ons)]
            dob = [(r * (d - a - o * b)).astype(BF16) for r, d, a, o, b in zip(rstds, don, mean_don, ons, mean_don_on)]
            dlhs = [_dot_nt(d, r) for d, r in zip(dob, rhs)]
            drhs = [_dot_tn(l, d) for l, d in zip(lhs, dob)]
            dkds = [_dot_nt(vs[2 * t + hh], gst_b[t]) for t, hh in heads]
            dv_parts = [drhs[2 * t + hh][0:BLK] + _dot(kd_heads[t][hh].astype(BF16), gst_b[t]) for t, hh in heads]
            das = [(dlhs[2 * t + hh][:, 0:BLK] * din[t][hh * BLK:(hh + 1) * BLK]).astype(BF16) for t, hh in heads]
            new_gsts = [gsts[t] * cd_ref[t] + drhs[2 * t][BLK:] + drhs[2 * t + 1][BLK:] for t in range(PAIRS)]
            dq_parts = [_dot(jnp.concatenate([das[2 * t], das[2 * t + 1]], axis=1), k_rows[t])
                        + jnp.where(low, dlhs[2 * t][:, BLK:], dlhs[2 * t + 1][:, BLK:]) * _tile(qdec, t)
                        for t in range(PAIRS)]
            dk_parts = [_dot_tn(jnp.concatenate([das[2 * t], das[2 * t + 1]], axis=0), q_rows[t])
                        + jnp.where(low, dkds[2 * t], dkds[2 * t + 1]) * _tile(kdec, t) for t in range(PAIRS)]
            drq = _rotate_transposed(jnp.concatenate(dq_parts, axis=1), cos_b, sin_b)
            drk = _rotate_transposed(jnp.concatenate(dk_parts, axis=1) * RET_SCALE, cos_b, sin_b)

            dp_ref[rows, :] = jnp.concatenate(
                [jnp.concatenate(dq_tiles, axis=1), dak, dav, daz, drq, drk] + dv_parts + dz_parts, axis=1).astype(BF16)
            dgain = dgain + jnp.concatenate(dgain_parts, axis=1)
            return dk_both[0:BLK], dv_both[0:BLK], tuple(new_gsts), dgain, dsink

        carry = (dkc_ref[...], dvc_ref[...], tuple(gst_ref[t] for t in range(PAIRS)), dgain_ref[...], dsink_ref[...])
        for jb in range(nsub):
            carry = sub(jnp.int32(jb), carry)
        dkc, dvc, gsts, dgain, dsink = carry
        dkc_ref[...] = dkc
        dvc_ref[...] = dvc
        gst_ref[...] = jnp.stack(gsts)
        dgain_ref[...] = dgain
        dsink_ref[...] = dsink

        dh = _dot(dp_ref[...], win_ref[...])
        xv = x_ref[...]
        r = lax.rsqrt(jnp.mean(xv * xv, axis=1, keepdims=True) + RMS_EPS)
        xn = xv * r
        gnorm_ref[...] += jnp.sum(dh * xn, axis=0, keepdims=True)
        u = dh * g_ref[...]
        gx_ref[...] = dxo_ref[...] + r * u - xn * (r * jnp.mean(u * xn, axis=1, keepdims=True))

    rev_rows = lambda w: pl.BlockSpec((tb, w), lambda i: (nblk - 1 - i, 0))
    prev_kv = pl.BlockSpec((None, BLK, 2 * ATT_KV_WIDTH),
                           lambda i: (0, jnp.maximum((nblk - 1 - i) * nsub - 1, 0), kv_cols))
    return _call(
        body, name="mix_bwd", grid=(nblk,),
        out_shape=(
            jax.ShapeDtypeStruct((seq, IN_WIDTH), BF16),
            jax.ShapeDtypeStruct((seq, D_MODEL), F32),
            jax.ShapeDtypeStruct((MIX_WIDTH, D_MODEL), F32),
            jax.ShapeDtypeStruct((1, D_MODEL), F32),
            jax.ShapeDtypeStruct((1, RET_WIDTH), F32),
            jax.ShapeDtypeStruct((1, LANE), F32),
        ),
        in_specs=[
            pl.BlockSpec((N_CHIPS, tb, SHARD_PAD), lambda i: (0, nblk - 1 - i, 0)),
            prev_kv, rev_rows(D_MODEL), rev_rows(MIX_WIDTH), rev_rows(ATT_WIDTH),
            pl.BlockSpec((nsub, len(ATT_PROBLEMS), BLK, 2 * BLK), lambda i: (nblk - 1 - i, 0, 0, 0)),
            pl.BlockSpec((nsub, len(ATT_PROBLEMS), 2 * BLK), lambda i: (nblk - 1 - i, 0, 0)),
            pl.BlockSpec((nsub,) + state_shape, lambda i: (nblk - 1 - i, 0, 0, 0)),
            rev_rows(LANE), rev_rows(LANE), rev_rows(D_MODEL),
            _const_spec((MIX_WIDTH, D_MODEL)), _const_spec((IN_WIDTH, D_MODEL)), _const_spec((1, D_MODEL)),
            _const_spec((1, RET_WIDTH)),
            _const_spec((PAIRS, 2 * BLK, BLK)), _const_spec((BLK, RET_QK_WIDTH)), _const_spec((BLK, RET_QK_WIDTH)),
            _const_spec(state_shape),
        ],
        out_specs=(
            rev_rows(IN_WIDTH), rev_rows(D_MODEL), _const_spec((MIX_WIDTH, D_MODEL)), _const_spec((1, D_MODEL)),
            _const_spec((1, RET_WIDTH)), _const_spec((1, LANE)),
        ),
        scratch_shapes=[
            pltpu.VMEM((tb, MIX_WIDTH), F32),
            pltpu.VMEM((tb + BLK, 2 * ATT_KV_WIDTH), F32),
            pltpu.VMEM((BLK, ATT_KV_WIDTH), F32), pltpu.VMEM((BLK, ATT_KV_WIDTH), F32),
            pltpu.VMEM(state_shape, F32),
        ],
        compiler_params=_params(60, ("arbitrary",)),
    )(proj, proj, dxo, mix, o_att, probs, p_sinks, states, cos_t, sin_t, x, w_out, w_in_t, norm_g, gn_gain, decay_in,
      qdec_t, kdec_t, cd_t)


def _gw_in_reduce(h_t, dproj, gw_out, small, tb):
    seq = dproj.shape[0]
    nblk = seq // tb
    last = nblk - 1
    hand_on = min(1, last)
    half = D_MODEL // 2
    A, B, C, N_SEMS = 0, N_CHIPS, 2 * N_CHIPS, 2 * N_CHIPS + 1

    def body(win_ref, ht_ref, dp_ref, gwo_hbm, s0_ref, s1_ref, s2_ref, s3_ref, s4_ref, out_ref, fout_ref, packsum_ref,
             acc, sib, send_buf, b_in, fin, fout, mine_out, sib_out, send_out, b_out, pack_ref, packs,
             send_sems, recv_sems, local_sems):
        p, i = pl.program_id(0), pl.program_id(1)
        small_start, small_hand_on, small_finish = _small_exchange(
            gwo_hbm, (s0_ref, s1_ref, s2_ref, s3_ref, s4_ref), fout, mine_out, sib_out, send_out, b_out, pack_ref,
            packs, send_sems, recv_sems, local_sems, N_SEMS)

        @pl.when(jnp.logical_and(p == 0, i == 0))
        def _():
            small_start()

        @pl.when(jnp.logical_and(p == 0, i == hand_on))
        def _():
            small_hand_on()

        x, y, c = lax.axis_index("x"), lax.axis_index("y"), lax.axis_index("c")
        chip = 2 * x + y
        sibling = (x, y, 1 - c)
        mine = pl.ds(pl.multiple_of(c * half, half), half)
        other = pl.ds(pl.multiple_of((1 - c) * half, half), half)

        def remote(src, dst, send_k, recv_k, to):
            return pltpu.make_async_remote_copy(src_ref=src, dst_ref=dst, send_sem=send_sems.at[send_k],
                                                recv_sem=recv_sems.at[recv_k], device_id=to, device_id_type=MESH_ID)

        part = _dot(ht_ref[...], dp_ref[...])
        slot = p % 2

        @pl.when(i == 0)
        def _():
            acc[slot] = part

        @pl.when(i > 0)
        def _():
            acc[slot] += part

        for q in range(N_CHIPS):
            s = q % 2
            to_sibling = remote(acc.at[s, other, :], sib.at[s], A + q, A + q, sibling)

            @pl.when(jnp.logical_and(p == q, i == last))
            def _():
                to_sibling.start()

            if q < N_CHIPS - 1:
                dest = (chip + 1 + q) % N_CHIPS

                @pl.when(jnp.logical_and(p == q + 1, i == hand_on))
                def _():
                    to_sibling.wait_recv()
                    send_buf[q] = (acc[s, mine, :] + sib[s]).astype(BF16)
                    remote(send_buf.at[q], b_in.at[chip], B + q, B + chip, (dest // 2, dest % 2, c)).start()
                    to_sibling.wait_send()
            else:
                @pl.when(jnp.logical_and(p == q, i == last))
                def _():
                    to_sibling.wait_recv()
                    fin[mine, :] = acc[s, mine, :] + sib[s]
                    for j in range(N_CHIPS):
                        @pl.when(j != chip)
                        def _():
                            remote(b_in.at[j], b_in.at[j], B + j, B + j, sibling).wait_recv()
                            fin[mine, :] += b_in[j].astype(F32)
                    to_core = remote(fin.at[mine, :], fin.at[mine, :], C, C, sibling)
                    to_core.start()
                    remote(fin.at[other, :], fin.at[other, :], C, C, sibling).wait_recv()
                    out_ref[...] = fin[...]
                    to_core.wait_send()
                    to_sibling.wait_send()
                    for k in range(N_CHIPS - 1):
                        remote(send_buf.at[k], b_in.at[chip], B + k, B + k, sibling).wait_send()
                    packsum_ref[...] = small_finish()
                    fout_ref[...] = fout[...]

    whole = lambda shape: pl.BlockSpec(shape, lambda p, i, win: (0,) * len(shape), pipeline_mode=pl.Buffered(1))
    grid_spec = pltpu.PrefetchScalarGridSpec(
        num_scalar_prefetch=1, grid=(N_CHIPS, nblk),
        in_specs=[pl.BlockSpec((D_MODEL, tb), lambda p, i, win: (0, i)),
                  pl.BlockSpec((pl.Element(tb), pl.Element(SHARD_PAD)),
                               lambda p, i, win: (i * tb, pl.multiple_of(win[p] * LANE, LANE))),
                  pl.BlockSpec(memory_space=pl.ANY)] + [whole(s.shape) for s in small],
        out_specs=(whole((D_MODEL, SHARD_PAD)), whole((SHARD_OUT, D_MODEL)), whole((PACK_ROWS, D_MODEL))),
        scratch_shapes=[
            pltpu.VMEM((2, D_MODEL, SHARD_PAD), F32), pltpu.VMEM((2, half, SHARD_PAD), F32),
            pltpu.VMEM((N_CHIPS - 1, half, SHARD_PAD), BF16), pltpu.VMEM((N_CHIPS, half, SHARD_PAD), BF16),
            pltpu.VMEM((D_MODEL, SHARD_PAD), F32),
        ] + _small_exchange_scratch() + [
            pltpu.SemaphoreType.DMA((N_SEMS + SMALL_SEMS,)), pltpu.SemaphoreType.DMA((N_SEMS + SMALL_SEMS,)),
            pltpu.SemaphoreType.DMA((N_CHIPS,)),
        ])
    chip = 2 * lax.axis_index("x") + lax.axis_index("y")
    owner = (chip + 1 + jnp.arange(N_CHIPS, dtype=jnp.int32)) % N_CHIPS
    win_start = (owner * SHARD_IN) // LANE
    return _call(
        body, name="gw_in_reduce", grid_spec=grid_spec,
        out_shape=(jax.ShapeDtypeStruct((D_MODEL, SHARD_PAD), F32), jax.ShapeDtypeStruct((SHARD_OUT, D_MODEL), F32),
                   jax.ShapeDtypeStruct((PACK_ROWS, D_MODEL), F32)),
        compiler_params=_params(52, ("arbitrary", "arbitrary")),
    )(win_start.astype(jnp.int32), h_t, dproj, gw_out, *small)


SMALL_SEMS = 17


def _small_exchange(gwo_hbm, small_refs, fout_ref, mine_out, sib_out, send_out, b_out, pack_ref, packs,
                    send_sems, recv_sems, local_sems, base):
    half_out = SHARD_OUT // 2
    A_OUT, B_OUT, C_OUT, PACK = base, base + 4, base + 8, base + 9
    assert len(small_refs) == PACK_PARTS
    x, y, c = lax.axis_index("x"), lax.axis_index("y"), lax.axis_index("c")
    chip = 2 * x + y
    dev = 2 * chip + c
    sibling = (x, y, 1 - c)

    def remote(src, dst, send_k, recv_k, to):
        return pltpu.make_async_remote_copy(src_ref=src, dst_ref=dst, send_sem=send_sems.at[send_k],
                                            recv_sem=recv_sems.at[recv_k], device_id=to, device_id_type=MESH_ID)

    def out_rows(j, core):
        return pl.ds(pl.multiple_of(j * SHARD_OUT + core * half_out, half_out), half_out)

    my_out_rows = pl.ds(pl.multiple_of(c * half_out, half_out), half_out)
    local = [pltpu.make_async_copy(gwo_hbm.at[out_rows(j, c), :], mine_out.at[j], local_sems.at[j])
             for j in range(N_CHIPS)]
    stage_a = [remote(gwo_hbm.at[out_rows(j, 1 - c), :], sib_out.at[j], A_OUT + j, A_OUT + j, sibling)
               for j in range(N_CHIPS)]
    mine_half_out = fout_ref.at[my_out_rows, :]
    stage_c = [remote(mine_half_out, mine_half_out, C_OUT, C_OUT, sibling)]

    def start():
        pack_ref[...] = jnp.zeros_like(pack_ref)
        for k, s_ref in enumerate(small_refs):
            pack_ref[k:k + 1, 0:s_ref.shape[1]] = s_ref[0:1, :]
        packs[dev] = pack_ref[...]
        for d in range(N_DEV):
            to = (d // 4, (d // 2) % 2, d % 2)

            @pl.when(d != dev)
            def _():
                remote(pack_ref, packs.at[dev], PACK + d, PACK + dev, to).start()

        for cp in local + stage_a:
            cp.start()

    def hand_on():
        for cp in local:
            cp.wait()
        for cp in stage_a:
            cp.wait_recv()
        for j in range(N_CHIPS):
            mine_out[j] = mine_out[j] + sib_out[j]

        for j in range(N_CHIPS):
            to = (j // 2, j % 2, c)

            @pl.when(j != chip)
            def _():
                send_out[j] = mine_out[j].astype(BF16)
                remote(send_out.at[j], b_out.at[chip], B_OUT + j, B_OUT + chip, to).start()

            @pl.when(j == chip)
            def _():
                fout_ref[my_out_rows, :] = mine_out[j]

    def finish():
        for j in range(N_CHIPS):
            @pl.when(j != chip)
            def _():
                remote(b_out.at[j], b_out.at[j], B_OUT + j, B_OUT + j, sibling).wait_recv()
                fout_ref[my_out_rows, :] += b_out[j].astype(F32)

        for cp in stage_c:
            cp.start()
        other_half_out = fout_ref.at[pl.ds(pl.multiple_of((1 - c) * half_out, half_out), half_out), :]
        remote(other_half_out, other_half_out, C_OUT, C_OUT, sibling).wait_recv()

        for d in range(N_DEV):
            @pl.when(d != dev)
            def _():
                remote(pack_ref, packs.at[d], PACK + d, PACK + d, sibling).wait_recv()
        total = packs[0]
        for d in range(1, N_DEV):
            total = total + packs[d]

        for cp in stage_a + stage_c:
            cp.wait_send()
        for j in range(N_CHIPS):
            @pl.when(j != chip)
            def _():
                remote(b_out.at[j], b_out.at[j], B_OUT + j, B_OUT + j, sibling).wait_send()
        for d in range(N_DEV):
            @pl.when(d != dev)
            def _():
                remote(pack_ref, packs.at[d], PACK + d, PACK + d, sibling).wait_send()
        return total

    return start, hand_on, finish


def _small_exchange_scratch():
    half_out = SHARD_OUT // 2
    return [
        pltpu.VMEM((SHARD_OUT, D_MODEL), F32),
        pltpu.VMEM((N_CHIPS, half_out, D_MODEL), F32), pltpu.VMEM((N_CHIPS, half_out, D_MODEL), F32),
        pltpu.VMEM((N_CHIPS, half_out, D_MODEL), BF16), pltpu.VMEM((N_CHIPS, half_out, D_MODEL), BF16),
        pltpu.VMEM((PACK_ROWS, D_MODEL), F32), pltpu.VMEM((N_DEV, PACK_ROWS, D_MODEL), F32),
    ]


def _adam_math(w, g, m, v):
    mn = ADAM_B1 * m + (1.0 - ADAM_B1) * g
    vn = ADAM_B2 * v + (1.0 - ADAM_B2) * (g * g)
    m_hat = mn / (1.0 - ADAM_B1 ** ADAM_STEP)
    v_hat = vn / (1.0 - ADAM_B2 ** ADAM_STEP)
    return -ADAM_LR * (m_hat / (jnp.sqrt(v_hat) + ADAM_EPS) + ADAM_WD * w), mn, vn


def _adamw(name, w, g, m, v, tb):
    rows, cols = w.shape

    def body(w_ref, g_ref, m_ref, v_ref, go_ref, d_ref, mo_ref, vo_ref):
        gv = g_ref[...]
        go_ref[...] = gv
        d_ref[...], mo_ref[...], vo_ref[...] = _adam_math(w_ref[...], gv, m_ref[...], v_ref[...])

    spec = pl.BlockSpec((tb, cols), lambda i: (i, 0))
    shape = jax.ShapeDtypeStruct((rows, cols), F32)
    return _call(
        body, name=name, grid=(rows // tb,), out_shape=(shape,) * 4,
        in_specs=[spec] * 4, out_specs=(spec,) * 4,
        compiler_params=_params(32, ("arbitrary",)),
    )(w, g, m, v)


def _adamw_w_in(w_t, g_window, m_t, v_t, tb):
    def body(w_ref, g_ref, m_ref, v_ref, go_ref, d_ref, mo_ref, vo_ref, gt_ref):
        gt_ref[...] = g_ref[...].T
        gv = gt_ref[pl.ds(pl.multiple_of(lax.axis_index("y") * SHARD_SHIFT, SHARD_SHIFT), SHARD_IN), :]
        go_ref[...] = gv
        d_ref[...], mo_ref[...], vo_ref[...] = _adam_math(w_ref[...], gv, m_ref[...], v_ref[...])

    spec = pl.BlockSpec((SHARD_IN, tb), lambda i: (0, i))
    shape = jax.ShapeDtypeStruct((SHARD_IN, D_MODEL), F32)
    return _call(
        body, name="adamw_w_in", grid=(D_MODEL // tb,), out_shape=(shape,) * 4,
        in_specs=[spec, pl.BlockSpec((tb, SHARD_PAD), lambda i: (i, 0)), spec, spec], out_specs=(spec,) * 4,
        scratch_shapes=[pltpu.VMEM((SHARD_PAD, tb), F32)],
        compiler_params=_params(32, ("arbitrary",)),
    )(w_t, g_window, m_t, v_t)


def _adamw_small(sums, params):
    def body(sums_ref, *refs):
        ins, outs = refs[:3 * len(params)], refs[3 * len(params):]
        for k in range(len(params)):
            w_ref, m_ref, v_ref = ins[3 * k:3 * k + 3]
            g = sums_ref[k:k + 1, 0:w_ref.shape[1]]
            go_ref, d_ref, mo_ref, vo_ref = outs[4 * k:4 * k + 4]
            go_ref[...] = g
            d_ref[...], mo_ref[...], vo_ref[...] = _adam_math(w_ref[...], g, m_ref[...], v_ref[...])

    vmem = pl.BlockSpec(memory_space=pltpu.VMEM)
    flat = [a for p in params for a in p]
    shapes = tuple(jax.ShapeDtypeStruct(p[0].shape, F32) for p in params for _ in range(4))
    res = _call(body, name="adamw_small", out_shape=shapes, in_specs=[vmem] * (1 + len(flat)),
                out_specs=(vmem,) * len(shapes), compiler_params=_params(16))(sums, *flat)
    return [res[4 * k:4 * k + 4] for k in range(len(params))]


def kernel(x, norm_g, w_in, att_sinks, ret_gn_g, w_out, final_g, loss_target, m_norm_g, m_w_in, m_att_sinks, m_ret_gn_g, m_w_out, m_final_g, v_norm_g, v_w_in, v_att_sinks, v_ret_gn_g, v_w_out, v_final_g):
    seq = x.shape[1]
    xs, tgt = x[0], loss_target[0]
    final_g2 = final_g.reshape(1, D_MODEL)
    tables = _tables(seq)

    w_in_t, m_w_in_t, v_w_in_t = w_in[0].T, m_w_in[0].T, v_w_in[0].T
    proj, h_t, w_in_full, w_out_full = _in_proj(xs, norm_g, w_in_t, w_out[0], min(TOKENS_PROJ, seq))
    mix, dxo, states, loss_part, gfin, o_att, probs, p_sinks = _mix_fwd(
        proj, xs, tgt, w_out_full, final_g2, ret_gn_g, att_sinks, tables, min(TOKENS_MIX, seq))
    dproj, grad_x, gw_out, gnorm, dgain, dsink = _mix_bwd(
        proj, dxo, mix, o_att, probs, p_sinks, states, xs, w_out_full, w_in_full, norm_g, ret_gn_g, tables,
        min(TOKENS_MIX, seq))
    g_in, g_out, sums = _gw_in_reduce(h_t, dproj, gw_out, (gnorm, gfin, dgain, dsink, loss_part),
                                      min(TOKENS_GW, seq))

    res_in = [r.T for r in _adamw_w_in(w_in_t, g_in, m_w_in_t, v_w_in_t, 256)]
    res_out = _adamw("adamw_w_out", w_out[0], g_out, m_w_out[0], v_w_out[0], SHARD_OUT)
    as_row = lambda a: a.reshape(1, D_MODEL)
    r_norm, r_final, r_gain, r_sink = _adamw_small(sums, [
        (norm_g, m_norm_g, v_norm_g), (final_g2, as_row(m_final_g), as_row(v_final_g)),
        (ret_gn_g, m_ret_gn_g, v_ret_gn_g), (att_sinks, m_att_sinks, v_att_sinks)])

    outs = []
    for k in range(4):
        outs += [r_norm[k], res_in[k][None], r_sink[k], r_gain[k], res_out[k][None], r_final[k].reshape(D_MODEL)]
    return (sums[4, 0], grad_x[None], *outs)
```

```python
import jax
import jax.numpy as jnp
import numpy as np
from jax import lax
from jax.experimental import pallas as pl
from jax.experimental.pallas import tpu as pltpu

F32 = jnp.float32
BF16 = jnp.bfloat16

D_MODEL = 1024
ATT_HEADS = 8
ATT_KV_HEADS = 2
ATT_HEAD_DIM = 64
RET_HEADS = 4
RET_QK_DIM = 64
RET_V_DIM = 128
BLK = 128
ROT_BASE = 10000.0
RMS_EPS = 1e-6
GN_EPS = 1e-6
NEG_INF = -1e30
ATT_SCALE = ATT_HEAD_DIM ** -0.5
RET_SCALE = RET_QK_DIM ** -0.5

ATT_WIDTH = ATT_HEADS * ATT_HEAD_DIM
ATT_KV_WIDTH = ATT_KV_HEADS * ATT_HEAD_DIM
RET_QK_WIDTH = RET_HEADS * RET_QK_DIM
RET_WIDTH = RET_HEADS * RET_V_DIM
MIX_WIDTH = ATT_WIDTH + RET_WIDTH
OFF_AQ = 0
OFF_AK = OFF_AQ + ATT_WIDTH
OFF_AV = OFF_AK + ATT_KV_WIDTH
OFF_AZ = OFF_AV + ATT_KV_WIDTH
OFF_RQ = OFF_AZ + ATT_WIDTH
OFF_RK = OFF_RQ + RET_QK_WIDTH
OFF_RV = OFF_RK + RET_QK_WIDTH
OFF_RZ = OFF_RV + RET_WIDTH
IN_WIDTH = OFF_RZ + RET_WIDTH

LANE = 128
BF16_ROWS = 16
HALF_LANE = LANE // 2
PAIRS = RET_QK_WIDTH // LANE
assert ATT_HEAD_DIM == HALF_LANE and RET_QK_DIM == HALF_LANE and RET_V_DIM == LANE and ATT_KV_WIDTH == LANE

N_CHIPS = 4
N_DEV = 8
SHARD_IN = IN_WIDTH // N_CHIPS
SHARD_PAD = 768
SHARD_SHIFT = SHARD_PAD - SHARD_IN
WIN_START = tuple((j * SHARD_IN) // LANE * LANE for j in range(N_CHIPS))
SHARD_OUT = MIX_WIDTH // N_CHIPS

PACK_PARTS = 5
PACK_ROWS = 8
assert PACK_PARTS <= PACK_ROWS

ADAM_LR = 0.001
ADAM_B1 = 0.9
ADAM_B2 = 0.999
ADAM_EPS = 1e-08
ADAM_WD = 0.01
ADAM_STEP = 10

VMEM_CAP = 64 * 1024 * 1024
TOKENS_PROJ = 1024
TOKENS_MIX = 512
TOKENS_GW = 2048
MESH_ID = pl.DeviceIdType.MESH


def _call(body, **kw):
    return pl.pallas_call(body, **kw)


def _params(vmem_mb, semantics=None):
    assert vmem_mb * 1024 * 1024 < VMEM_CAP
    return pltpu.CompilerParams(dimension_semantics=semantics, vmem_limit_bytes=vmem_mb * 1024 * 1024)


def _dot(a, b):
    return jnp.dot(a, b, preferred_element_type=F32)


def _dot_nt(a, b):
    return lax.dot_general(a, b, (((1,), (1,)), ((), ())), preferred_element_type=F32)


def _dot_tn(a, b):
    return lax.dot_general(a, b, (((0,), (0,)), ((), ())), preferred_element_type=F32)


def _sigmoid(z):
    return 1.0 / (1.0 + jnp.exp(-z))


def _const_spec(shape):
    nd = len(shape)
    return pl.BlockSpec(shape, lambda i: (0,) * nd, pipeline_mode=pl.Buffered(1))


def _tables(seq):
    f32 = np.float32
    pos = np.arange(seq, dtype=f32)
    theta = (f32(1.0) / (f32(ROT_BASE) ** np.linspace(0.0, 1.0, RET_QK_DIM // 2, dtype=f32))).astype(f32)
    ang = (pos[:, None] * theta[None, :]).astype(f32)
    cos, sin = np.cos(ang), np.sin(ang)
    cos2 = np.repeat(cos, 2, axis=1)
    sin2 = np.stack([-sin, sin], axis=-1).reshape(seq, RET_QK_DIM)
    cos_t = np.tile(cos2, (1, 2))
    sin_t = np.tile(sin2, (1, 2))

    log_gamma = np.log(f32(1.0) - f32(2.0) ** (f32(-5.0) - np.arange(RET_HEADS, dtype=f32))).astype(f32)
    idx = np.arange(BLK, dtype=f32)
    rel = idx[:, None] - idx[None, :]
    decay_in = np.where(rel >= 0, np.exp(log_gamma[:, None, None] * np.maximum(rel, f32(0.0))), f32(0.0))
    k_dec = np.exp(log_gamma[:, None] * (BLK - 1 - idx)[None, :])
    q_dec = np.exp(log_gamma[:, None] * (idx + 1)[None, :])
    chunk_decay = np.exp(log_gamma * f32(BLK))
    kdec_t = np.repeat(k_dec.T, RET_QK_DIM, axis=1)
    qdec_t = np.repeat(q_dec.T, RET_QK_DIM, axis=1)
    cd_t = np.broadcast_to(chunk_decay[:, None, None], (RET_HEADS, RET_QK_DIM, RET_V_DIM))
    decay_in = decay_in.reshape(PAIRS, 2 * BLK, BLK)
    cd_t = cd_t.reshape(PAIRS, 2 * RET_QK_DIM, RET_V_DIM)

    key = np.arange(BLK)[:, None]
    query = np.arange(2 * BLK)[None, :] % BLK
    bias = np.stack([np.where(key > query, NEG_INF, 0.0), np.zeros((BLK, 2 * BLK))])
    return tuple(jnp.asarray(np.ascontiguousarray(a), F32) for a in (cos_t, sin_t, decay_in, qdec_t, kdec_t, cd_t, bias))


def _low_lanes(shape):
    lane = lax.broadcasted_iota(jnp.int32, shape, len(shape) - 1)
    return (lane & HALF_LANE) == 0


def _split_heads(t):
    low = _low_lanes(t.shape)
    zero = jnp.zeros_like(t)
    return jnp.where(low, t, zero), jnp.where(low, zero, t)


def _swap_pairs(t):
    lane = lax.broadcasted_iota(jnp.int32, t.shape, 1)
    nxt = pltpu.roll(t, t.shape[1] - 1, 1)
    prv = pltpu.roll(t, 1, 1)
    return jnp.where((lane & 1) == 0, nxt, prv)


def _per_tile(fn, t):
    return jnp.concatenate([fn(_tile(t, i)) for i in range(t.shape[1] // LANE)], axis=1)


def _rotate(t, cos_t, sin_t):
    return _per_tile(lambda a: a * cos_t + _swap_pairs(a) * sin_t, t)


def _rotate_transposed(d, cos_t, sin_t):
    return _per_tile(lambda a: a * cos_t + _swap_pairs(a * sin_t), d)


def _kv_operands(cat):
    low = _low_lanes(cat.shape)
    swapped = pltpu.roll(cat, HALF_LANE, 1)
    zero = jnp.zeros_like(cat)
    pick = lambda a, b: jnp.where(low, a, b).astype(BF16)
    return ((pick(cat, zero), pick(zero, swapped)), (pick(swapped, zero), pick(zero, cat)))


def _stack_tiles(t, first_tile):
    a = t[:, first_tile * LANE:(first_tile + 1) * LANE]
    b = t[:, (first_tile + 1) * LANE:(first_tile + 2) * LANE]
    return jnp.concatenate([a, b], axis=0)


def _sink_rows(sinks_ref, group):
    first = lax.broadcasted_iota(jnp.int32, (1, 2 * BLK), 1) < BLK

    def row(h0, h1):
        return jnp.where(first, sinks_ref[0, group * 4 + h0], sinks_ref[0, group * 4 + h1])
    return row(0, 2), row(1, 3)


ATT_PROBLEMS = tuple((g, hi) for g in range(ATT_KV_HEADS) for hi in range(2))


def _in_previous_block():
    key = lax.broadcasted_iota(jnp.int32, (BLK, 2 * BLK), 0)
    query = lax.broadcasted_iota(jnp.int32, (BLK, 2 * BLK), 1) & (BLK - 1)
    return key > query


def _fold(t, prev):
    return jnp.where(prev, t[0:BLK], t[BLK:])


def _unfold(t, prev):
    zero = jnp.zeros_like(t)
    return jnp.concatenate([jnp.where(prev, t, zero), jnp.where(prev, zero, t)], axis=0)


def _attn_scores(qs, k_ops, bias, prev):
    return [_fold(_dot_nt(k_ops[g][hi], qs[g]), prev) + bias for g, hi in ATT_PROBLEMS]


def _attn_softmax(s, sinks_ref):
    sink = [_sink_rows(sinks_ref, g)[hi] for g, hi in ATT_PROBLEMS]
    m = [jnp.maximum(jnp.max(si, axis=0, keepdims=True), ki) for si, ki in zip(s, sink)]
    e = [jnp.exp(si - mi) for si, mi in zip(s, m)]
    es = [jnp.exp(ki - mi) for ki, mi in zip(sink, m)]
    inv = [1.0 / (jnp.sum(ei, axis=0, keepdims=True) + esi) for ei, esi in zip(e, es)]
    return [ei * ii for ei, ii in zip(e, inv)], [esi * ii for esi, ii in zip(es, inv)]


def _group_norm_all(outs):
    mu = [jnp.mean(o, axis=1, keepdims=True) for o in outs]
    xc = [o - m for o, m in zip(outs, mu)]
    var = [jnp.mean(c * c, axis=1, keepdims=True) for c in xc]
    rstd = [lax.rsqrt(v + GN_EPS) for v in var]
    return [c * r for c, r in zip(xc, rstd)], rstd


def _proj_cols(p_ref, rows, start, stop):
    pieces = []
    while start < stop:
        k = max(j for j in range(N_CHIPS) if WIN_START[j] <= start)
        end = min(stop, WIN_START[k] + SHARD_PAD)
        pieces.append(p_ref[k, rows, start - WIN_START[k]:end - WIN_START[k]])
        start = end
    return pieces[0] if len(pieces) == 1 else jnp.concatenate(pieces, axis=1)


def _retention_operands(p_ref, rows, cos_b, sin_b, qdec, kdec):
    qr = _rotate(_proj_cols(p_ref, rows, OFF_RQ, OFF_RQ + RET_QK_WIDTH).astype(F32), cos_b, sin_b)
    kr = _rotate(_proj_cols(p_ref, rows, OFF_RK, OFF_RK + RET_QK_WIDTH).astype(F32), cos_b, sin_b) * RET_SCALE
    return qr, kr, qr * qdec, kr * kdec


def _tile(t, i):
    return t[:, i * LANE:(i + 1) * LANE]


GATHER_SEMS = 7


def _allgather_steps(src_ref, full_ref, blk_ref, send_sems, recv_sems):
    block_rows = blk_ref.shape[0]
    assert block_rows % BF16_ROWS == 0 and full_ref.shape[0] == N_DEV * block_rows
    x, y, c = lax.axis_index("x"), lax.axis_index("y"), lax.axis_index("c")
    me, sibling = (x, y, c), (x, y, 1 - c)
    chips = [(1 - x, y), (x, 1 - y), (1 - x, 1 - y)]

    def rows(px, py, pc):
        return full_ref.at[pl.ds(pl.multiple_of((4 * px + 2 * py + pc) * block_rows, BF16_ROWS), block_rows), :]

    def copy(k, block, to, src=None):
        return pltpu.make_async_remote_copy(
            src_ref=rows(*block) if src is None else src, dst_ref=rows(*block),
            send_sem=send_sems.at[k], recv_sem=recv_sems.at[k], device_id=to, device_id_type=MESH_ID)

    first = [copy(0, me, sibling, src=blk_ref)] + [copy(1 + j, me, (*chip, c), src=blk_ref) for j, chip in enumerate(chips)]
    passed = [copy(4 + j, (*chip, c), sibling) for j, chip in enumerate(chips)]

    def start():
        blk_ref[...] = src_ref[pl.ds(pl.multiple_of(c * block_rows, BF16_ROWS), block_rows), :].astype(BF16)
        rows(*me)[...] = blk_ref[...]
        for cp in first:
            cp.start()

    def forward():
        for j, chip in enumerate(chips):
            copy(1 + j, (*chip, c), me).wait_recv()
            passed[j].start()

    def finish():
        copy(0, sibling, me).wait_recv()
        for j, chip in enumerate(chips):
            copy(4 + j, (*chip, 1 - c), me).wait_recv()
        for cp in first + passed:
            cp.wait_send()

    return start, forward, finish


WINDOW_HALF = SHARD_PAD // 2
WINDOW_SEMS = GATHER_SEMS + 1


def _window_gather(wi_ref, w4_ref, blk_ref, edge_ref, send_sems, recv_sems):
    x, y, c = lax.axis_index("x"), lax.axis_index("y"), lax.axis_index("c")
    me, sibling, pair = (x, y, c), (x, y, 1 - c), (x, 1 - y, c)
    chips = [(1 - x, y), (x, 1 - y), (1 - x, 1 - y)]
    own_rows, edge_rows = SHARD_IN - WINDOW_HALF, SHARD_SHIFT
    sends_edge = y == c

    def rows(px, py, pc):
        return w4_ref.at[pl.ds(pl.multiple_of((4 * px + 2 * py + pc) * WINDOW_HALF, BF16_ROWS), WINDOW_HALF), :]

    def copy(k, block, to, src=None):
        return pltpu.make_async_remote_copy(
            src_ref=rows(*block) if src is None else src, dst_ref=rows(*block),
            send_sem=send_sems.at[k], recv_sem=recv_sems.at[k], device_id=to, device_id_type=MESH_ID)

    edge_copy = pltpu.make_async_remote_copy(
        src_ref=edge_ref, dst_ref=edge_ref, send_sem=send_sems.at[GATHER_SEMS], recv_sem=recv_sems.at[GATHER_SEMS],
        device_id=pair, device_id_type=MESH_ID)
    first = [copy(0, me, sibling, src=blk_ref)] + [copy(1 + j, me, (*chip, c), src=blk_ref) for j, chip in enumerate(chips)]
    passed = [copy(4 + j, (*chip, c), sibling) for j, chip in enumerate(chips)]

    def at(start, size):
        return pl.ds(pl.multiple_of(start, BF16_ROWS), size)

    def start():
        @pl.when(sends_edge)
        def _():
            edge_ref[...] = wi_ref[at((1 - y) * (SHARD_IN - edge_rows), edge_rows), :].astype(BF16)
            edge_copy.start()
            blk_ref[...] = wi_ref[at(c * own_rows, WINDOW_HALF), :].astype(BF16)

        @pl.when(jnp.logical_not(sends_edge))
        def _():
            blk_ref[at(y * edge_rows, own_rows), :] = wi_ref[at(c * WINDOW_HALF, own_rows), :].astype(BF16)
            edge_copy.wait_recv()
            blk_ref[at((1 - y) * own_rows, edge_rows), :] = edge_ref[...]

        rows(*me)[...] = blk_ref[...]
        for cp in first[:-1]:
            cp.start()
        copy(0, sibling, me).wait_recv()

    def send_far():
        first[-1].start()

    def forward(j):
        copy(1 + j, (*chips[j], c), me).wait_recv()
        passed[j].start()

    def wait_forwarded(j):
        copy(4 + j, (*chips[j], 1 - c), me).wait_recv()

    def finish():
        for cp in first + passed:
            cp.wait_send()

        @pl.when(sends_edge)
        def _():
            edge_copy.wait_send()

    return start, send_far, forward, wait_forwarded, finish


def _in_proj(x, norm_g, w_in_t_shard, w_out_shard, tb):
    seq = x.shape[0]
    nblk = seq // tb
    last = nblk - 1
    chip_of_panel = (None, 1, 0, 2)

    def body(win_ref, x_ref, g_ref, wi_ref, wo_ref, p_ref, ht_ref, wt_ref, wout_ref,
             w4, blk, edge, hbuf, wout_full, blko, send_sems, recv_sems, send_sems_o, recv_sems_o):
        q, i = pl.program_id(0), pl.program_id(1)
        chip = 2 * lax.axis_index("x") + lax.axis_index("y")
        in_start, in_send_far, in_forward, in_wait_forwarded, in_finish = _window_gather(
            wi_ref, w4, blk, edge, send_sems, recv_sems)
        out_start, out_forward, out_finish = _allgather_steps(wo_ref, wout_full, blko, send_sems_o, recv_sems_o)
        rows = pl.ds(pl.multiple_of(i * tb, tb), tb)

        @pl.when(jnp.logical_and(q == 0, i == 0))
        def _():
            in_start()

        @pl.when(jnp.logical_and(q == 0, i == min(last // 2 + 1, last)))
        def _():
            in_send_far()

        @pl.when(jnp.logical_and(q == 1, i == 0))
        def _():
            out_start()

        for panel in range(1, N_CHIPS):
            @pl.when(jnp.logical_and(q == panel, i == 0))
            def _():
                in_wait_forwarded(chip_of_panel[panel])

        @pl.when(q == 0)
        def _():
            xv = x_ref[...]
            r = lax.rsqrt(jnp.mean(xv * xv, axis=1, keepdims=True) + RMS_EPS)
            h = (xv * r) * g_ref[...]
            hbuf[rows, :] = h.astype(BF16)
            ht_ref[...] = h.T.astype(BF16)

        owner = jnp.bitwise_xor(chip, q)
        window = w4[pl.ds(pl.multiple_of(owner * SHARD_PAD, SHARD_PAD), SHARD_PAD), :]
        p_ref[...] = _dot_nt(hbuf[rows, :], window).astype(BF16)

        for panel in range(1, N_CHIPS):
            @pl.when(jnp.logical_and(q == panel - 1, i == last))
            def _():
                in_forward(chip_of_panel[panel])

        @pl.when(jnp.logical_and(q == N_CHIPS - 1, i == 0))
        def _():
            out_forward()

        @pl.when(jnp.logical_and(q == N_CHIPS - 1, i == last))
        def _():
            in_finish()
            for k in range(N_CHIPS):
                first = k * SHARD_PAD + (k % 2) * SHARD_SHIFT
                wt_ref[k * SHARD_IN:(k + 1) * SHARD_IN, :] = w4[first:first + SHARD_IN, :]
            out_finish()
            wout_ref[...] = wout_full[...]

    whole = lambda shape: pl.BlockSpec(shape, lambda q, i, win: (0,) * len(shape), pipeline_mode=pl.Buffered(1))
    first_panel = lambda q, i: jnp.where(q == 0, i, last)
    grid_spec = pltpu.PrefetchScalarGridSpec(
        num_scalar_prefetch=1, grid=(N_CHIPS, nblk),
        in_specs=[pl.BlockSpec((tb, D_MODEL), lambda q, i, win: (first_panel(q, i), 0)), whole((1, D_MODEL)),
                  whole((SHARD_IN, D_MODEL)), whole((SHARD_OUT, D_MODEL))],
        out_specs=(pl.BlockSpec((None, tb, SHARD_PAD), lambda q, i, win: (win[q], i, 0)),
                   pl.BlockSpec((D_MODEL, tb), lambda q, i, win: (0, first_panel(q, i))),
                   whole((IN_WIDTH, D_MODEL)), whole((MIX_WIDTH, D_MODEL))),
        scratch_shapes=[
            pltpu.VMEM((N_CHIPS * SHARD_PAD, D_MODEL), BF16), pltpu.VMEM((WINDOW_HALF, D_MODEL), BF16),
            pltpu.VMEM((SHARD_SHIFT, D_MODEL), BF16), pltpu.VMEM((seq, D_MODEL), BF16),
            pltpu.VMEM((MIX_WIDTH, D_MODEL), BF16), pltpu.VMEM((SHARD_OUT // 2, D_MODEL), BF16),
            pltpu.SemaphoreType.DMA((WINDOW_SEMS,)), pltpu.SemaphoreType.DMA((WINDOW_SEMS,)),
            pltpu.SemaphoreType.DMA((GATHER_SEMS,)), pltpu.SemaphoreType.DMA((GATHER_SEMS,)),
        ])
    chip = 2 * lax.axis_index("x") + lax.axis_index("y")
    owner = jnp.bitwise_xor(chip, jnp.arange(N_CHIPS, dtype=jnp.int32))
    return _call(
        body, name="in_proj", grid_spec=grid_spec,
        out_shape=(jax.ShapeDtypeStruct((N_CHIPS, seq, SHARD_PAD), BF16), jax.ShapeDtypeStruct((D_MODEL, seq), BF16),
                   jax.ShapeDtypeStruct((IN_WIDTH, D_MODEL), BF16), jax.ShapeDtypeStruct((MIX_WIDTH, D_MODEL), BF16)),
        compiler_params=_params(60, ("arbitrary", "arbitrary")),
    )(owner.astype(jnp.int32), x, norm_g, w_in_t_shard, w_out_shard)


def _mix_fwd(proj, x, target, w_out, final_g, gn_gain, sinks, tables, tb):
    seq = x.shape[0]
    nsub = tb // BLK
    cos_t, sin_t, decay_in, qdec_t, kdec_t, cd_t, bias_t = tables

    def body(p_ref, x_ref, t_ref, cos_ref, sin_ref, wout_ref, fg_ref, gain_ref, sinks_ref, din_ref, qdec_ref,
             kdec_ref, cd_ref, bias_ref, mix_ref, dxo_ref, st_ref, loss_ref, gfin_ref, oatt_ref, probs_ref, psink_ref,
             kprev_ref, vprev_ref, state_ref):
        i = pl.program_id(0)

        @pl.when(i == 0)
        def _():
            kprev_ref[...] = jnp.zeros_like(kprev_ref)
            vprev_ref[...] = jnp.zeros_like(vprev_ref)
            state_ref[...] = jnp.zeros_like(state_ref)
            loss_ref[...] = jnp.zeros_like(loss_ref)
            gfin_ref[...] = jnp.zeros_like(gfin_ref)

        prev = _in_previous_block()

        def sub(j, carry):
            kp, vp, states = carry
            rows = pl.ds(pl.multiple_of(j * BLK, BLK), BLK)
            bias = bias_ref[jnp.where(jnp.logical_or(i > 0, j > 0), 1, 0)]

            aq = _proj_cols(p_ref, rows, OFF_AQ, OFF_AQ + ATT_WIDTH)
            ak = _proj_cols(p_ref, rows, OFF_AK, OFF_AK + ATT_KV_WIDTH).astype(F32)
            av = _proj_cols(p_ref, rows, OFF_AV, OFF_AV + ATT_KV_WIDTH).astype(F32)
            az = _proj_cols(p_ref, rows, OFF_AZ, OFF_AZ + ATT_WIDTH).astype(F32)
            k_ops = _kv_operands(jnp.concatenate([kp, ak], axis=0))
            v_ops = _kv_operands(jnp.concatenate([vp, av], axis=0))
            qs = [(_stack_tiles(aq, 2 * g) * ATT_SCALE).astype(BF16) for g in range(ATT_KV_HEADS)]

            qr, kr, qd, kd = _retention_operands(p_ref, rows, cos_ref[rows, :], sin_ref[rows, :],
                                                 qdec_ref[...], kdec_ref[...])
            s = _attn_scores(qs, k_ops, bias, prev)
            heads = [(t, hh) for t in range(PAIRS) for hh in range(2)]
            sc = [_dot_nt(jnp.concatenate(_split_heads(_tile(qr, t)), axis=0).astype(BF16), _tile(kr, t).astype(BF16))
                  * din_ref[t] for t in range(PAIRS)]
            qd_heads = [_split_heads(_tile(qd, t)) for t in range(PAIRS)]
            state_b = [states[t].astype(BF16) for t in range(PAIRS)]
            vs = [_proj_cols(p_ref, rows, OFF_RV + h * RET_V_DIM, OFF_RV + (h + 1) * RET_V_DIM) for h in range(RET_HEADS)]
            rzs = [_proj_cols(p_ref, rows, OFF_RZ + h * RET_V_DIM, OFF_RZ + (h + 1) * RET_V_DIM).astype(F32)
                   for h in range(RET_HEADS)]
            lhs = [jnp.concatenate([sc[t][hh * BLK:(hh + 1) * BLK].astype(BF16), qd_heads[t][hh].astype(BF16)], axis=1)
                   for t, hh in heads]
            p, p_sink = _attn_softmax(s, sinks_ref)
            o_ret = [_dot(lhs[2 * t + hh], jnp.concatenate([vs[2 * t + hh], state_b[t]], axis=0)) for t, hh in heads]
            p = [pi.astype(BF16) for pi in p]
            o_tiles = []
            for g in range(ATT_KV_HEADS):
                p_cat = jnp.concatenate([_unfold(p[2 * g], prev), _unfold(p[2 * g + 1], prev)], axis=0)
                o = _dot_tn(p_cat, jnp.concatenate(v_ops[g], axis=0))
                o_tiles += [o[0:BLK], o[BLK:]]
            ons, _ = _group_norm_all(o_ret)
            new_states = [states[t] * cd_ref[t]
                          + _dot_tn(jnp.concatenate(_split_heads(_tile(kd, t)), axis=0).astype(BF16),
                                    jnp.concatenate([vs[2 * t], vs[2 * t + 1]], axis=0)) for t in range(PAIRS)]
            o_att = jnp.concatenate(o_tiles, axis=1)
            out = [o_att * (az * _sigmoid(az))]
            out += [(ons[h] * gain_ref[:, h * RET_V_DIM:(h + 1) * RET_V_DIM]) * (rzs[h] * _sigmoid(rzs[h]))
                    for h in range(RET_HEADS)]
            mix_ref[rows, :] = jnp.concatenate(out, axis=1).astype(BF16)
            oatt_ref[rows, :] = o_att
            probs_ref[j] = jnp.stack(p)
            psink_ref[j] = jnp.concatenate(p_sink, axis=0)
            st_ref[j] = jnp.stack(states)
            return ak, av, tuple(new_states)

        carry = (kprev_ref[...], vprev_ref[...], tuple(state_ref[t] for t in range(PAIRS)))
        for jb in range(nsub):
            carry = sub(jnp.int32(jb), carry)
        kp, vp, states = carry
        kprev_ref[...] = kp
        vprev_ref[...] = vp
        state_ref[...] = jnp.stack(states)

        xo = x_ref[...] + _dot(mix_ref[...], wout_ref[...])
        r2 = lax.rsqrt(jnp.mean(xo * xo, axis=1, keepdims=True) + RMS_EPS)
        xn = xo * r2
        err = xn * fg_ref[...] - t_ref[...]
        loss_ref[...] += jnp.sum(err * err) * (0.5 / D_MODEL)
        dy = err * (1.0 / D_MODEL)
        gfin_ref[...] += jnp.sum(dy * xn, axis=0, keepdims=True)
        u = dy * fg_ref[...]
        dxo_ref[...] = r2 * u - xn * (r2 * jnp.mean(u * xn, axis=1, keepdims=True))

    blk_rows = lambda w: pl.BlockSpec((tb, w), lambda i: (i, 0))
    state_shape = (PAIRS, 2 * RET_QK_DIM, RET_V_DIM)
    return _call(
        body, name="mix_fwd", grid=(seq // tb,),
        out_shape=(
            jax.ShapeDtypeStruct((seq, MIX_WIDTH), BF16),
            jax.ShapeDtypeStruct((seq, D_MODEL), F32),
            jax.ShapeDtypeStruct((seq // BLK,) + state_shape, F32),
            jax.ShapeDtypeStruct((8, LANE), F32),
            jax.ShapeDtypeStruct((1, D_MODEL), F32),
            jax.ShapeDtypeStruct((seq, ATT_WIDTH), F32),
            jax.ShapeDtypeStruct((seq // BLK, len(ATT_PROBLEMS), BLK, 2 * BLK), BF16),
            jax.ShapeDtypeStruct((seq // BLK, len(ATT_PROBLEMS), 2 * BLK), F32),
        ),
        in_specs=[
            pl.BlockSpec((N_CHIPS, tb, SHARD_PAD), lambda i: (0, i, 0)),
            blk_rows(D_MODEL), blk_rows(D_MODEL), blk_rows(LANE), blk_rows(LANE),
            _const_spec((MIX_WIDTH, D_MODEL)), _const_spec((1, D_MODEL)), _const_spec((1, RET_WIDTH)),
            pl.BlockSpec(memory_space=pltpu.SMEM),
            _const_spec((PAIRS, 2 * BLK, BLK)), _const_spec((BLK, RET_QK_WIDTH)), _const_spec((BLK, RET_QK_WIDTH)),
            _const_spec(state_shape), _const_spec((2, BLK, 2 * BLK)),
        ],
        out_specs=(
            blk_rows(MIX_WIDTH), blk_rows(D_MODEL),
            pl.BlockSpec((nsub,) + state_shape, lambda i: (i, 0, 0, 0)),
            _const_spec((8, LANE)), _const_spec((1, D_MODEL)), blk_rows(ATT_WIDTH),
            pl.BlockSpec((nsub, len(ATT_PROBLEMS), BLK, 2 * BLK), lambda i: (i, 0, 0, 0)),
            pl.BlockSpec((nsub, len(ATT_PROBLEMS), 2 * BLK), lambda i: (i, 0, 0)),
        ),
        scratch_shapes=[
            pltpu.VMEM((BLK, ATT_KV_WIDTH), F32), pltpu.VMEM((BLK, ATT_KV_WIDTH), F32),
            pltpu.VMEM(state_shape, F32),
        ],
        compiler_params=_params(48, ("arbitrary",)),
    )(proj, x, target, cos_t, sin_t, w_out, final_g, gn_gain, sinks, decay_in, qdec_t, kdec_t, cd_t, bias_t)


def _mix_bwd(proj, dxo, mix, o_att, probs, p_sinks, states, x, w_out, w_in_t, norm_g, gn_gain, tables, tb):
    seq = dxo.shape[0]
    nsub = tb // BLK
    nblk = seq // tb
    cos_t, sin_t, decay_in, qdec_t, kdec_t, cd_t, _ = tables
    kv_cols = OFF_AK // (2 * ATT_KV_WIDTH)
    state_shape = (PAIRS, 2 * RET_QK_DIM, RET_V_DIM)

    def body(p_ref, pkv_ref, dxo_ref, mix_ref, oatt_ref, probs_ref, psink_ref, st_ref, cos_ref, sin_ref, x_ref, wout_ref,
             win_ref, g_ref, gain_ref, din_ref, qdec_ref, kdec_ref, cd_ref,
             dp_ref, gx_ref, gwout_ref, gnorm_ref, dgain_ref, dsink_ref,
             dmix_ref, kv_ref, dkc_ref, dvc_ref, gst_ref):
        i = pl.program_id(0)

        @pl.when(i == 0)
        def _():
            gwout_ref[...] = jnp.zeros_like(gwout_ref)
            gnorm_ref[...] = jnp.zeros_like(gnorm_ref)
            dgain_ref[...] = jnp.zeros_like(dgain_ref)
            dsink_ref[...] = jnp.zeros_like(dsink_ref)
            dkc_ref[...] = jnp.zeros_like(dkc_ref)
            dvc_ref[...] = jnp.zeros_like(dvc_ref)
            gst_ref[...] = jnp.zeros_like(gst_ref)

        dxo_b = dxo_ref[...].astype(BF16)
        dmix_ref[...] = _dot_nt(dxo_b, wout_ref[...])
        gwout_ref[...] += _dot_tn(mix_ref[...], dxo_b)
        kv_ref[0:BLK, :] = pkv_ref[...].astype(F32)
        kv_ref[BLK:, :] = _proj_cols(p_ref, slice(None), OFF_AK, OFF_AK + 2 * ATT_KV_WIDTH).astype(F32)
        low = _low_lanes((BLK, LANE))
        low2 = _low_lanes((2 * BLK, LANE))
        lane = lax.broadcasted_iota(jnp.int32, (1, LANE), 1)
        prev = _in_previous_block()

        def sub(jj, carry):
            dkc, dvc, gsts, dgain, dsink = carry
            j = nsub - 1 - jj
            rows = pl.ds(pl.multiple_of(j * BLK, BLK), BLK)
            both = pl.ds(pl.multiple_of(j * BLK, BLK), 2 * BLK)


            aq = _proj_cols(p_ref, rows, OFF_AQ, OFF_AQ + ATT_WIDTH)
            az = _proj_cols(p_ref, rows, OFF_AZ, OFF_AZ + ATT_WIDTH).astype(F32)
            k_ops = _kv_operands(kv_ref[both, 0:ATT_KV_WIDTH])
            v_ops = _kv_operands(kv_ref[both, ATT_KV_WIDTH:2 * ATT_KV_WIDTH])
            da = dmix_ref[rows, 0:ATT_WIDTH]
            sig = _sigmoid(az)
            d_o = da * (az * sig)
            qs = [(_stack_tiles(aq, 2 * g) * ATT_SCALE).astype(BF16) for g in range(ATT_KV_HEADS)]
            dos = [_stack_tiles(d_o, 2 * g).astype(BF16) for g in range(ATT_KV_HEADS)]
            p_b = [probs_ref[j, k] for k in range(len(ATT_PROBLEMS))]
            p = [pk.astype(F32) for pk in p_b]
            sink_all = psink_ref[j]
            p_sink = [sink_all[k:k + 1, :] for k in range(len(ATT_PROBLEMS))]
            cos_b, sin_b = cos_ref[rows, :], sin_ref[rows, :]
            qdec, kdec = qdec_ref[...], kdec_ref[...]
            qr, kr, qd, kd = _retention_operands(p_ref, rows, cos_b, sin_b, qdec, kdec)
            heads = [(t, hh) for t in range(PAIRS) for hh in range(2)]
            head_cols = [slice(h * RET_V_DIM, (h + 1) * RET_V_DIM) for h in range(RET_HEADS)]
            q_rows = [jnp.concatenate(_split_heads(_tile(qr, t)), axis=0).astype(BF16) for t in range(PAIRS)]
            k_rows = [jnp.concatenate(_split_heads(_tile(kr, t)), axis=0).astype(BF16) for t in range(PAIRS)]
            din = [din_ref[t] for t in range(PAIRS)]

            dpr = [_fold(_dot_nt(v_ops[g][hi], dos[g]), prev) for g, hi in ATT_PROBLEMS]
            sc = [(_dot_nt(q_rows[t], _tile(kr, t).astype(BF16)) * din[t]).astype(BF16) for t in range(PAIRS)]
            qd_heads = [_split_heads(_tile(qd, t)) for t in range(PAIRS)]
            kd_heads = [_split_heads(_tile(kd, t)) for t in range(PAIRS)]
            state_b = [st_ref[j, t].astype(BF16) for t in range(PAIRS)]
            gst_b = [gsts[t].astype(BF16) for t in range(PAIRS)]
            vs = [_proj_cols(p_ref, rows, OFF_RV + h * RET_V_DIM, OFF_RV + (h + 1) * RET_V_DIM) for h in range(RET_HEADS)]
            rzs = [_proj_cols(p_ref, rows, OFF_RZ + h * RET_V_DIM, OFF_RZ + (h + 1) * RET_V_DIM).astype(F32)
                   for h in range(RET_HEADS)]
            drs = [dmix_ref[rows, ATT_WIDTH + h * RET_V_DIM:ATT_WIDTH + (h + 1) * RET_V_DIM] for h in range(RET_HEADS)]
            gains = [gain_ref[:, c] for c in head_cols]
            lhs = [jnp.concatenate([sc[t][hh * BLK:(hh + 1) * BLK], qd_heads[t][hh].astype(BF16)], axis=1) for t, hh in heads]
            rhs = [jnp.concatenate([vs[2 * t + hh], state_b[t]], axis=0) for t, hh in heads]

            delta = [jnp.sum(pi * di, axis=0, keepdims=True) for pi, di in zip(p, dpr)]
            ds = [_unfold((pi * (di - ti)).astype(BF16), prev) for pi, di, ti in zip(p, dpr, delta)]
            o_ret = [_dot(l, r) for l, r in zip(lhs, rhs)]
            for (g, hi), ki, ti in zip(ATT_PROBLEMS, p_sink, delta):
                sink_part = ki * ti
                for half in range(2):
                    tot = jnp.sum(sink_part[:, half * BLK:(half + 1) * BLK], axis=1, keepdims=True)
                    dsink = dsink - jnp.where(lane == 4 * g + 2 * half + hi, tot, 0.0)

            dq_tiles, dk_sums, dv_sums = [], [], []
            for g in range(ATT_KV_HEADS):
                ds_cat = jnp.concatenate([ds[2 * g], ds[2 * g + 1]], axis=0)
                p_cat = jnp.concatenate([_unfold(p_b[2 * g], prev), _unfold(p_b[2 * g + 1], prev)], axis=0)
                dqs = _dot_tn(ds_cat, jnp.concatenate(k_ops[g], axis=0)) * ATT_SCALE
                dq_tiles += [dqs[0:BLK], dqs[BLK:]]
                dk_sums.append(_dot(ds_cat, qs[g]))
                dv_sums.append(_dot(p_cat, dos[g]))
            ons, rstds = _group_norm_all(o_ret)
            sig_r = [_sigmoid(z) for z in rzs]
            dgn = [d * (z * g) for d, z, g in zip(drs, rzs, sig_r)]
            dz_parts = [d * (o * gn) * (g * (1.0 + z * (1.0 - g))) for d, o, gn, g, z in zip(drs, ons, gains, sig_r, rzs)]
            dgain_parts = [jnp.sum(d * o, axis=0, keepdims=True) for d, o in zip(dgn, ons)]
            don = [d * gn for d, gn in zip(dgn, gains)]
            mean_don = [jnp.mean(d, axis=1, keepdims=True) for d in don]
            mean_don_on = [jnp.mean(d * o, axis=1, keepdims=True) for d, o in zip(don, ons)]
            dob = [(r * (d - a - o * b)).astype(BF16) for r, d, a, o, b in zip(rstds, don, mean_don, ons, mean_don_on)]

            dlhs = [_dot_nt(d, r) for d, r in zip(dob, rhs)]
            drhs = [_dot_tn(l, d) for l, d in zip(lhs, dob)]
            dkds = [_dot_nt(vs[2 * t + hh], gst_b[t]) for t, hh in heads]
            dv_parts = [drhs[2 * t + hh][0:BLK] + _dot(kd_heads[t][hh].astype(BF16), gst_b[t]) for t, hh in heads]
            daz = da * oatt_ref[rows, :] * (sig * (1.0 + az * (1.0 - sig)))

            def kv_grad(sums):
                (a0, b0), (a1, b1) = [(s[0:2 * BLK], s[2 * BLK:]) for s in sums]
                return jnp.where(low2, a0, b1) + pltpu.roll(jnp.where(low2, a1, b0), HALF_LANE, 1)

            dk_both, dv_both = kv_grad(dk_sums), kv_grad(dv_sums)
            dak = dk_both[BLK:] + dkc
            dav = dv_both[BLK:] + dvc
            das = [(dlhs[2 * t + hh][:, 0:BLK] * din[t][hh * BLK:(hh + 1) * BLK]).astype(BF16) for t, hh in heads]
            new_gsts = [gsts[t] * cd_ref[t] + drhs[2 * t][BLK:] + drhs[2 * t + 1][BLK:] for t in range(PAIRS)]
            dq_parts = [_dot(jnp.concatenate([das[2 * t], das[2 * t + 1]], axis=1), k_rows[t])
                        + jnp.where(low, dlhs[2 * t][:, BLK:], dlhs[2 * t + 1][:, BLK:]) * _tile(qdec, t)
                        for t in range(PAIRS)]
            dk_parts = [_dot_tn(jnp.concatenate([das[2 * t], das[2 * t + 1]], axis=0), q_rows[t])
                        + jnp.where(low, dkds[2 * t], dkds[2 * t + 1]) * _tile(kdec, t) for t in range(PAIRS)]
            drq = _rotate_transposed(jnp.concatenate(dq_parts, axis=1), cos_b, sin_b)
            drk = _rotate_transposed(jnp.concatenate(dk_parts, axis=1) * RET_SCALE, cos_b, sin_b)

            dp_ref[rows, :] = jnp.concatenate(
                [jnp.concatenate(dq_tiles, axis=1), dak, dav, daz, drq, drk] + dv_parts + dz_parts, axis=1).astype(BF16)
            dgain = dgain + jnp.concatenate(dgain_parts, axis=1)
            return dk_both[0:BLK], dv_both[0:BLK], tuple(new_gsts), dgain, dsink

        carry = (dkc_ref[...], dvc_ref[...], tuple(gst_ref[t] for t in range(PAIRS)), dgain_ref[...], dsink_ref[...])
        for jb in range(nsub):
            carry = sub(jnp.int32(jb), carry)
        dkc, dvc, gsts, dgain, dsink = carry
        dkc_ref[...] = dkc
        dvc_ref[...] = dvc
        gst_ref[...] = jnp.stack(gsts)
        dgain_ref[...] = dgain
        dsink_ref[...] = dsink

        dh = _dot(dp_ref[...], win_ref[...])
        xv = x_ref[...]
        r = lax.rsqrt(jnp.mean(xv * xv, axis=1, keepdims=True) + RMS_EPS)
        xn = xv * r
        gnorm_ref[...] += jnp.sum(dh * xn, axis=0, keepdims=True)
        u = dh * g_ref[...]
        gx_ref[...] = dxo_ref[...] + r * u - xn * (r * jnp.mean(u * xn, axis=1, keepdims=True))

    rev_rows = lambda w: pl.BlockSpec((tb, w), lambda i: (nblk - 1 - i, 0))
    prev_kv = pl.BlockSpec((None, BLK, 2 * ATT_KV_WIDTH),
                           lambda i: (0, jnp.maximum((nblk - 1 - i) * nsub - 1, 0), kv_cols))
    return _call(
        body, name="mix_bwd", grid=(nblk,),
        out_shape=(
            jax.ShapeDtypeStruct((seq, IN_WIDTH), BF16),
            jax.ShapeDtypeStruct((seq, D_MODEL), F32),
            jax.ShapeDtypeStruct((MIX_WIDTH, D_MODEL), F32),
            jax.ShapeDtypeStruct((1, D_MODEL), F32),
            jax.ShapeDtypeStruct((1, RET_WIDTH), F32),
            jax.ShapeDtypeStruct((1, LANE), F32),
        ),
        in_specs=[
            pl.BlockSpec((N_CHIPS, tb, SHARD_PAD), lambda i: (0, nblk - 1 - i, 0)),
            prev_kv, rev_rows(D_MODEL), rev_rows(MIX_WIDTH), rev_rows(ATT_WIDTH),
            pl.BlockSpec((nsub, len(ATT_PROBLEMS), BLK, 2 * BLK), lambda i: (nblk - 1 - i, 0, 0, 0)),
            pl.BlockSpec((nsub, len(ATT_PROBLEMS), 2 * BLK), lambda i: (nblk - 1 - i, 0, 0)),
            pl.BlockSpec((nsub,) + state_shape, lambda i: (nblk - 1 - i, 0, 0, 0)),
            rev_rows(LANE), rev_rows(LANE), rev_rows(D_MODEL),
            _const_spec((MIX_WIDTH, D_MODEL)), _const_spec((IN_WIDTH, D_MODEL)), _const_spec((1, D_MODEL)),
            _const_spec((1, RET_WIDTH)),
            _const_spec((PAIRS, 2 * BLK, BLK)), _const_spec((BLK, RET_QK_WIDTH)), _const_spec((BLK, RET_QK_WIDTH)),
            _const_spec(state_shape),
        ],
        out_specs=(
            rev_rows(IN_WIDTH), rev_rows(D_MODEL), _const_spec((MIX_WIDTH, D_MODEL)), _const_spec((1, D_MODEL)),
            _const_spec((1, RET_WIDTH)), _const_spec((1, LANE)),
        ),
        scratch_shapes=[
            pltpu.VMEM((tb, MIX_WIDTH), F32),
            pltpu.VMEM((tb + BLK, 2 * ATT_KV_WIDTH), F32),
            pltpu.VMEM((BLK, ATT_KV_WIDTH), F32), pltpu.VMEM((BLK, ATT_KV_WIDTH), F32),
            pltpu.VMEM(state_shape, F32),
        ],
        compiler_params=_params(60, ("arbitrary",)),
    )(proj, proj, dxo, mix, o_att, probs, p_sinks, states, cos_t, sin_t, x, w_out, w_in_t, norm_g, gn_gain, decay_in,
      qdec_t, kdec_t, cd_t)


def _gw_in_reduce(h_t, dproj, gw_out, small, tb):
    seq = dproj.shape[0]
    nblk = seq // tb
    last = nblk - 1
    hand_on = min(1, last)
    half = D_MODEL // 2
    A, B, C, N_SEMS = 0, N_CHIPS, 2 * N_CHIPS, 2 * N_CHIPS + 1

    def body(win_ref, ht_ref, dp_ref, gwo_hbm, s0_ref, s1_ref, s2_ref, s3_ref, s4_ref, out_ref, fout_ref, packsum_ref,
             acc, sib, send_buf, b_in, fin, fout, mine_out, sib_out, send_out, b_out, pack_ref, packs,
             send_sems, recv_sems, local_sems):
        p, i = pl.program_id(0), pl.program_id(1)
        small_start, small_hand_on, small_finish = _small_exchange(
            gwo_hbm, (s0_ref, s1_ref, s2_ref, s3_ref, s4_ref), fout, mine_out, sib_out, send_out, b_out, pack_ref,
            packs, send_sems, recv_sems, local_sems, N_SEMS)

        @pl.when(jnp.logical_and(p == 0, i == 0))
        def _():
            small_start()

        @pl.when(jnp.logical_and(p == 0, i == hand_on))
        def _():
            small_hand_on()

        x, y, c = lax.axis_index("x"), lax.axis_index("y"), lax.axis_index("c")
        chip = 2 * x + y
        sibling = (x, y, 1 - c)
        mine = pl.ds(pl.multiple_of(c * half, half), half)
        other = pl.ds(pl.multiple_of((1 - c) * half, half), half)

        def remote(src, dst, send_k, recv_k, to):
            return pltpu.make_async_remote_copy(src_ref=src, dst_ref=dst, send_sem=send_sems.at[send_k],
                                                recv_sem=recv_sems.at[recv_k], device_id=to, device_id_type=MESH_ID)

        part = _dot(ht_ref[...], dp_ref[...])
        slot = p % 2

        @pl.when(i == 0)
        def _():
            acc[slot] = part

        @pl.when(i > 0)
        def _():
            acc[slot] += part

        for q in range(N_CHIPS):
            s = q % 2
            to_sibling = remote(acc.at[s, other, :], sib.at[s], A + q, A + q, sibling)

            @pl.when(jnp.logical_and(p == q, i == last))
            def _():
                to_sibling.start()

            if q < N_CHIPS - 1:
                dest = (chip + 1 + q) % N_CHIPS

                @pl.when(jnp.logical_and(p == q + 1, i == hand_on))
                def _():
                    to_sibling.wait_recv()
                    send_buf[q] = (acc[s, mine, :] + sib[s]).astype(BF16)
                    remote(send_buf.at[q], b_in.at[chip], B + q, B + chip, (dest // 2, dest % 2, c)).start()
                    to_sibling.wait_send()
            else:
                @pl.when(jnp.logical_and(p == q, i == last))
                def _():
                    to_sibling.wait_recv()
                    fin[mine, :] = acc[s, mine, :] + sib[s]
                    for j in range(N_CHIPS):
                        @pl.when(j != chip)
                        def _():
                            remote(b_in.at[j], b_in.at[j], B + j, B + j, sibling).wait_recv()
                            fin[mine, :] += b_in[j].astype(F32)
                    to_core = remote(fin.at[mine, :], fin.at[mine, :], C, C, sibling)
                    to_core.start()
                    remote(fin.at[other, :], fin.at[other, :], C, C, sibling).wait_recv()
                    out_ref[...] = fin[...]
                    to_core.wait_send()
                    to_sibling.wait_send()
                    for k in range(N_CHIPS - 1):
                        remote(send_buf.at[k], b_in.at[chip], B + k, B + k, sibling).wait_send()
                    packsum_ref[...] = small_finish()
                    fout_ref[...] = fout[...]

    whole = lambda shape: pl.BlockSpec(shape, lambda p, i, win: (0,) * len(shape), pipeline_mode=pl.Buffered(1))
    grid_spec = pltpu.PrefetchScalarGridSpec(
        num_scalar_prefetch=1, grid=(N_CHIPS, nblk),
        in_specs=[pl.BlockSpec((D_MODEL, tb), lambda p, i, win: (0, i)),
                  pl.BlockSpec((pl.Element(tb), pl.Element(SHARD_PAD)),
                               lambda p, i, win: (i * tb, pl.multiple_of(win[p] * LANE, LANE))),
                  pl.BlockSpec(memory_space=pl.ANY)] + [whole(s.shape) for s in small],
        out_specs=(whole((D_MODEL, SHARD_PAD)), whole((SHARD_OUT, D_MODEL)), whole((PACK_ROWS, D_MODEL))),
        scratch_shapes=[
            pltpu.VMEM((2, D_MODEL, SHARD_PAD), F32), pltpu.VMEM((2, half, SHARD_PAD), F32),
            pltpu.VMEM((N_CHIPS - 1, half, SHARD_PAD), BF16), pltpu.VMEM((N_CHIPS, half, SHARD_PAD), BF16),
            pltpu.VMEM((D_MODEL, SHARD_PAD), F32),
        ] + _small_exchange_scratch() + [
            pltpu.SemaphoreType.DMA((N_SEMS + SMALL_SEMS,)), pltpu.SemaphoreType.DMA((N_SEMS + SMALL_SEMS,)),
            pltpu.SemaphoreType.DMA((N_CHIPS,)),
        ])
    chip = 2 * lax.axis_index("x") + lax.axis_index("y")
    owner = (chip + 1 + jnp.arange(N_CHIPS, dtype=jnp.int32)) % N_CHIPS
    win_start = (owner * SHARD_IN) // LANE
    return _call(
        body, name="gw_in_reduce", grid_spec=grid_spec,
        out_shape=(jax.ShapeDtypeStruct((D_MODEL, SHARD_PAD), F32), jax.ShapeDtypeStruct((SHARD_OUT, D_MODEL), F32),
                   jax.ShapeDtypeStruct((PACK_ROWS, D_MODEL), F32)),
        compiler_params=_params(52, ("arbitrary", "arbitrary")),
    )(win_start.astype(jnp.int32), h_t, dproj, gw_out, *small)


SMALL_SEMS = 17


def _small_exchange(gwo_hbm, small_refs, fout_ref, mine_out, sib_out, send_out, b_out, pack_ref, packs,
                    send_sems, recv_sems, local_sems, base):
    half_out = SHARD_OUT // 2
    A_OUT, B_OUT, C_OUT, PACK = base, base + 4, base + 8, base + 9
    assert len(small_refs) == PACK_PARTS
    x, y, c = lax.axis_index("x"), lax.axis_index("y"), lax.axis_index("c")
    chip = 2 * x + y
    dev = 2 * chip + c
    sibling = (x, y, 1 - c)

    def remote(src, dst, send_k, recv_k, to):
        return pltpu.make_async_remote_copy(src_ref=src, dst_ref=dst, send_sem=send_sems.at[send_k],
                                            recv_sem=recv_sems.at[recv_k], device_id=to, device_id_type=MESH_ID)

    def out_rows(j, core):
        return pl.ds(pl.multiple_of(j * SHARD_OUT + core * half_out, half_out), half_out)

    my_out_rows = pl.ds(pl.multiple_of(c * half_out, half_out), half_out)
    local = [pltpu.make_async_copy(gwo_hbm.at[out_rows(j, c), :], mine_out.at[j], local_sems.at[j])
             for j in range(N_CHIPS)]
    stage_a = [remote(gwo_hbm.at[out_rows(j, 1 - c), :], sib_out.at[j], A_OUT + j, A_OUT + j, sibling)
               for j in range(N_CHIPS)]
    mine_half_out = fout_ref.at[my_out_rows, :]
    stage_c = [remote(mine_half_out, mine_half_out, C_OUT, C_OUT, sibling)]

    def start():
        pack_ref[...] = jnp.zeros_like(pack_ref)
        for k, s_ref in enumerate(small_refs):
            pack_ref[k:k + 1, 0:s_ref.shape[1]] = s_ref[0:1, :]
        packs[dev] = pack_ref[...]
        for d in range(N_DEV):
            to = (d // 4, (d // 2) % 2, d % 2)

            @pl.when(d != dev)
            def _():
                remote(pack_ref, packs.at[dev], PACK + d, PACK + dev, to).start()

        for cp in local + stage_a:
            cp.start()

    def hand_on():
        for cp in local:
            cp.wait()
        for cp in stage_a:
            cp.wait_recv()
        for j in range(N_CHIPS):
            mine_out[j] = mine_out[j] + sib_out[j]

        for j in range(N_CHIPS):
            to = (j // 2, j % 2, c)

            @pl.when(j != chip)
            def _():
                send_out[j] = mine_out[j].astype(BF16)
                remote(send_out.at[j], b_out.at[chip], B_OUT + j, B_OUT + chip, to).start()

            @pl.when(j == chip)
            def _():
                fout_ref[my_out_rows, :] = mine_out[j]

    def finish():
        for j in range(N_CHIPS):
            @pl.when(j != chip)
            def _():
                remote(b_out.at[j], b_out.at[j], B_OUT + j, B_OUT + j, sibling).wait_recv()
                fout_ref[my_out_rows, :] += b_out[j].astype(F32)

        for cp in stage_c:
            cp.start()
        other_half_out = fout_ref.at[pl.ds(pl.multiple_of((1 - c) * half_out, half_out), half_out), :]
        remote(other_half_out, other_half_out, C_OUT, C_OUT, sibling).wait_recv()

        for d in range(N_DEV):
            @pl.when(d != dev)
            def _():
                remote(pack_ref, packs.at[d], PACK + d, PACK + d, sibling).wait_recv()
        total = packs[0]
        for d in range(1, N_DEV):
            total = total + packs[d]

        for cp in stage_a + stage_c:
            cp.wait_send()
        for j in range(N_CHIPS):
            @pl.when(j != chip)
            def _():
                remote(b_out.at[j], b_out.at[j], B_OUT + j, B_OUT + j, sibling).wait_send()
        for d in range(N_DEV):
            @pl.when(d != dev)
            def _():
                remote(pack_ref, packs.at[d], PACK + d, PACK + d, sibling).wait_send()
        return total

    return start, hand_on, finish


def _small_exchange_scratch():
    half_out = SHARD_OUT // 2
    return [
        pltpu.VMEM((SHARD_OUT, D_MODEL), F32),
        pltpu.VMEM((N_CHIPS, half_out, D_MODEL), F32), pltpu.VMEM((N_CHIPS, half_out, D_MODEL), F32),
        pltpu.VMEM((N_CHIPS, half_out, D_MODEL), BF16), pltpu.VMEM((N_CHIPS, half_out, D_MODEL), BF16),
        pltpu.VMEM((PACK_ROWS, D_MODEL), F32), pltpu.VMEM((N_DEV, PACK_ROWS, D_MODEL), F32),
    ]


def _adam_math(w, g, m, v):
    mn = ADAM_B1 * m + (1.0 - ADAM_B1) * g
    vn = ADAM_B2 * v + (1.0 - ADAM_B2) * (g * g)
    m_hat = mn / (1.0 - ADAM_B1 ** ADAM_STEP)
    v_hat = vn / (1.0 - ADAM_B2 ** ADAM_STEP)
    return -ADAM_LR * (m_hat / (jnp.sqrt(v_hat) + ADAM_EPS) + ADAM_WD * w), mn, vn


def _adamw(name, w, g, m, v, tb):
    rows, cols = w.shape

    def body(w_ref, g_ref, m_ref, v_ref, go_ref, d_ref, mo_ref, vo_ref):
        gv = g_ref[...]
        go_ref[...] = gv
        d_ref[...], mo_ref[...], vo_ref[...] = _adam_math(w_ref[...], gv, m_ref[...], v_ref[...])

    spec = pl.BlockSpec((tb, cols), lambda i: (i, 0))
    shape = jax.ShapeDtypeStruct((rows, cols), F32)
    return _call(
        body, name=name, grid=(rows // tb,), out_shape=(shape,) * 4,
        in_specs=[spec] * 4, out_specs=(spec,) * 4,
        compiler_params=_params(32, ("arbitrary",)),
    )(w, g, m, v)


def _adamw_w_in(w_t, g_window, m_t, v_t, tb):
    def body(w_ref, g_ref, m_ref, v_ref, go_ref, d_ref, mo_ref, vo_ref, gt_ref):
        gt_ref[...] = g_ref[...].T
        gv = gt_ref[pl.ds(pl.multiple_of(lax.axis_index("y") * SHARD_SHIFT, SHARD_SHIFT), SHARD_IN), :]
        go_ref[...] = gv
        d_ref[...], mo_ref[...], vo_ref[...] = _adam_math(w_ref[...], gv, m_ref[...], v_ref[...])

    spec = pl.BlockSpec((SHARD_IN, tb), lambda i: (0, i))
    shape = jax.ShapeDtypeStruct((SHARD_IN, D_MODEL), F32)
    return _call(
        body, name="adamw_w_in", grid=(D_MODEL // tb,), out_shape=(shape,) * 4,
        in_specs=[spec, pl.BlockSpec((tb, SHARD_PAD), lambda i: (i, 0)), spec, spec], out_specs=(spec,) * 4,
        scratch_shapes=[pltpu.VMEM((SHARD_PAD, tb), F32)],
        compiler_params=_params(32, ("arbitrary",)),
    )(w_t, g_window, m_t, v_t)


def _adamw_small(sums, params):
    def body(sums_ref, *refs):
        ins, outs = refs[:3 * len(params)], refs[3 * len(params):]
        for k in range(len(params)):
            w_ref, m_ref, v_ref = ins[3 * k:3 * k + 3]
            g = sums_ref[k:k + 1, 0:w_ref.shape[1]]
            go_ref, d_ref, mo_ref, vo_ref = outs[4 * k:4 * k + 4]
            go_ref[...] = g
            d_ref[...], mo_ref[...], vo_ref[...] = _adam_math(w_ref[...], g, m_ref[...], v_ref[...])

    vmem = pl.BlockSpec(memory_space=pltpu.VMEM)
    flat = [a for p in params for a in p]
    shapes = tuple(jax.ShapeDtypeStruct(p[0].shape, F32) for p in params for _ in range(4))
    res = _call(body, name="adamw_small", out_shape=shapes, in_specs=[vmem] * (1 + len(flat)),
                out_specs=(vmem,) * len(shapes), compiler_params=_params(16))(sums, *flat)
    return [res[4 * k:4 * k + 4] for k in range(len(params))]


def kernel(x, norm_g, w_in, att_sinks, ret_gn_g, w_out, final_g, loss_target, m_norm_g, m_w_in, m_att_sinks, m_ret_gn_g, m_w_out, m_final_g, v_norm_g, v_w_in, v_att_sinks, v_ret_gn_g, v_w_out, v_final_g):
    seq = x.shape[1]
    xs, tgt = x[0], loss_target[0]
    final_g2 = final_g.reshape(1, D_MODEL)
    tables = _tables(seq)

    w_in_t, m_w_in_t, v_w_in_t = w_in[0].T, m_w_in[0].T, v_w_in[0].T
    proj, h_t, w_in_full, w_out_full = _in_proj(xs, norm_g, w_in_t, w_out[0], min(TOKENS_PROJ, seq))
    mix, dxo, states, loss_part, gfin, o_att, probs, p_sinks = _mix_fwd(
        proj, xs, tgt, w_out_full, final_g2, ret_gn_g, att_sinks, tables, min(TOKENS_MIX, seq))
    dproj, grad_x, gw_out, gnorm, dgain, dsink = _mix_bwd(
        proj, dxo, mix, o_att, probs, p_sinks, states, xs, w_out_full, w_in_full, norm_g, ret_gn_g, tables,
        min(TOKENS_MIX, seq))
    g_in, g_out, sums = _gw_in_reduce(h_t, dproj, gw_out, (gnorm, gfin, dgain, dsink, loss_part),
                                      min(TOKENS_GW, seq))

    res_in = [r.T for r in _adamw_w_in(w_in_t, g_in, m_w_in_t, v_w_in_t, 256)]
    res_out = _adamw("adamw_w_out", w_out[0], g_out, m_w_out[0], v_w_out[0], SHARD_OUT)
    as_row = lambda a: a.reshape(1, D_MODEL)
    r_norm, r_final, r_gain, r_sink = _adamw_small(sums, [
        (norm_g, m_norm_g, v_norm_g), (final_g2, as_row(m_final_g), as_row(v_final_g)),
        (ret_gn_g, m_ret_gn_g, v_ret_gn_g), (att_sinks, m_att_sinks, v_att_sinks)])

    outs = []
    for k in range(4):
        outs += [r_norm[k], res_in[k][None], r_sink[k], r_gain[k], res_out[k][None], r_final[k].reshape(D_MODEL)]
    return (sums[4, 0], grad_x[None], *outs)
```

```python
import jax
import jax.numpy as jnp
import numpy as np
from jax import lax
from jax.experimental import pallas as pl
from jax.experimental.pallas import tpu as pltpu

F32 = jnp.float32
BF16 = jnp.bfloat16

D_MODEL = 1024
ATT_HEADS = 8
ATT_KV_HEADS = 2
ATT_HEAD_DIM = 64
RET_HEADS = 4
RET_QK_DIM = 64
RET_V_DIM = 128
BLK = 128
ROT_BASE = 10000.0
RMS_EPS = 1e-6
GN_EPS = 1e-6
NEG_INF = -1e30
ATT_SCALE = ATT_HEAD_DIM ** -0.5
RET_SCALE = RET_QK_DIM ** -0.5

ATT_WIDTH = ATT_HEADS * ATT_HEAD_DIM
ATT_KV_WIDTH = ATT_KV_HEADS * ATT_HEAD_DIM
RET_QK_WIDTH = RET_HEADS * RET_QK_DIM
RET_WIDTH = RET_HEADS * RET_V_DIM
MIX_WIDTH = ATT_WIDTH + RET_WIDTH
OFF_AQ = 0
OFF_AK = OFF_AQ + ATT_WIDTH
OFF_AV = OFF_AK + ATT_KV_WIDTH
OFF_AZ = OFF_AV + ATT_KV_WIDTH
OFF_RQ = OFF_AZ + ATT_WIDTH
OFF_RK = OFF_RQ + RET_QK_WIDTH
OFF_RV = OFF_RK + RET_QK_WIDTH
OFF_RZ = OFF_RV + RET_WIDTH
IN_WIDTH = OFF_RZ + RET_WIDTH

LANE = 128
BF16_ROWS = 16
HALF_LANE = LANE // 2
PAIRS = RET_QK_WIDTH // LANE
assert ATT_HEAD_DIM == HALF_LANE and RET_QK_DIM == HALF_LANE and RET_V_DIM == LANE and ATT_KV_WIDTH == LANE

N_CHIPS = 4
N_DEV = 8
SHARD_IN = IN_WIDTH // N_CHIPS
SHARD_PAD = 768
SHARD_SHIFT = SHARD_PAD - SHARD_IN
WIN_START = tuple((j * SHARD_IN) // LANE * LANE for j in range(N_CHIPS))
SHARD_OUT = MIX_WIDTH // N_CHIPS

PACK_PARTS = 5
PACK_ROWS = 8
assert PACK_PARTS <= PACK_ROWS

ADAM_LR = 0.001
ADAM_B1 = 0.9
ADAM_B2 = 0.999
ADAM_EPS = 1e-08
ADAM_WD = 0.01
ADAM_STEP = 10

VMEM_CAP = 64 * 1024 * 1024
TOKENS_PROJ = 1024
TOKENS_MIX = 512
TOKENS_GW = 2048
MESH_ID = pl.DeviceIdType.MESH


def _call(body, **kw):
    return pl.pallas_call(body, **kw)


def _params(vmem_mb, semantics=None):
    assert vmem_mb * 1024 * 1024 < VMEM_CAP
    return pltpu.CompilerParams(dimension_semantics=semantics, vmem_limit_bytes=vmem_mb * 1024 * 1024)


def _dot(a, b):
    return jnp.dot(a, b, preferred_element_type=F32)


def _dot_nt(a, b):
    return lax.dot_general(a, b, (((1,), (1,)), ((), ())), preferred_element_type=F32)


def _dot_tn(a, b):
    return lax.dot_general(a, b, (((0,), (0,)), ((), ())), preferred_element_type=F32)


def _sigmoid(z):
    return 1.0 / (1.0 + jnp.exp(-z))


def _const_spec(shape):
    nd = len(shape)
    return pl.BlockSpec(shape, lambda i: (0,) * nd, pipeline_mode=pl.Buffered(1))


def _tables(seq):
    f32 = np.float32
    pos = np.arange(seq, dtype=f32)
    theta = (f32(1.0) / (f32(ROT_BASE) ** np.linspace(0.0, 1.0, RET_QK_DIM // 2, dtype=f32))).astype(f32)
    ang = (pos[:, None] * theta[None, :]).astype(f32)
    cos, sin = np.cos(ang), np.sin(ang)
    cos2 = np.repeat(cos, 2, axis=1)
    sin2 = np.stack([-sin, sin], axis=-1).reshape(seq, RET_QK_DIM)
    cos_t = np.tile(cos2, (1, 2))
    sin_t = np.tile(sin2, (1, 2))

    log_gamma = np.log(f32(1.0) - f32(2.0) ** (f32(-5.0) - np.arange(RET_HEADS, dtype=f32))).astype(f32)
    idx = np.arange(BLK, dtype=f32)
    rel = idx[:, None] - idx[None, :]
    decay_in = np.where(rel >= 0, np.exp(log_gamma[:, None, None] * np.maximum(rel, f32(0.0))), f32(0.0))
    k_dec = np.exp(log_gamma[:, None] * (BLK - 1 - idx)[None, :])
    q_dec = np.exp(log_gamma[:, None] * (idx + 1)[None, :])
    chunk_decay = np.exp(log_gamma * f32(BLK))
    kdec_t = np.repeat(k_dec.T, RET_QK_DIM, axis=1)
    qdec_t = np.repeat(q_dec.T, RET_QK_DIM, axis=1)
    cd_t = np.broadcast_to(chunk_decay[:, None, None], (RET_HEADS, RET_QK_DIM, RET_V_DIM))
    decay_in = decay_in.reshape(PAIRS, 2 * BLK, BLK)
    cd_t = cd_t.reshape(PAIRS, 2 * RET_QK_DIM, RET_V_DIM)

    key = np.arange(BLK)[:, None]
    query = np.arange(2 * BLK)[None, :] % BLK
    bias = np.stack([np.where(key > query, NEG_INF, 0.0), np.zeros((BLK, 2 * BLK))])
    return tuple(jnp.asarray(np.ascontiguousarray(a), F32) for a in (cos_t, sin_t, decay_in, qdec_t, kdec_t, cd_t, bias))


def _low_lanes(shape):
    lane = lax.broadcasted_iota(jnp.int32, shape, len(shape) - 1)
    return (lane & HALF_LANE) == 0


def _split_heads(t):
    low = _low_lanes(t.shape)
    zero = jnp.zeros_like(t)
    return jnp.where(low, t, zero), jnp.where(low, zero, t)


def _swap_pairs(t):
    lane = lax.broadcasted_iota(jnp.int32, t.shape, 1)
    nxt = pltpu.roll(t, t.shape[1] - 1, 1)
    prv = pltpu.roll(t, 1, 1)
    return jnp.where((lane & 1) == 0, nxt, prv)


def _per_tile(fn, t):
    return jnp.concatenate([fn(_tile(t, i)) for i in range(t.shape[1] // LANE)], axis=1)


def _rotate(t, cos_t, sin_t):
    return _per_tile(lambda a: a * cos_t + _swap_pairs(a) * sin_t, t)


def _rotate_transposed(d, cos_t, sin_t):
    return _per_tile(lambda a: a * cos_t + _swap_pairs(a * sin_t), d)


def _kv_operands(cat):
    low = _low_lanes(cat.shape)
    swapped = pltpu.roll(cat, HALF_LANE, 1)
    zero = jnp.zeros_like(cat)
    pick = lambda a, b: jnp.where(low, a, b).astype(BF16)
    return ((pick(cat, zero), pick(zero, swapped)), (pick(swapped, zero), pick(zero, cat)))


def _stack_tiles(t, first_tile):
    a = t[:, first_tile * LANE:(first_tile + 1) * LANE]
    b = t[:, (first_tile + 1) * LANE:(first_tile + 2) * LANE]
    return jnp.concatenate([a, b], axis=0)


def _sink_rows(sinks_ref, group):
    first = lax.broadcasted_iota(jnp.int32, (1, 2 * BLK), 1) < BLK

    def row(h0, h1):
        return jnp.where(first, sinks_ref[0, group * 4 + h0], sinks_ref[0, group * 4 + h1])
    return row(0, 2), row(1, 3)


ATT_PROBLEMS = tuple((g, hi) for g in range(ATT_KV_HEADS) for hi in range(2))


def _in_previous_block():
    key = lax.broadcasted_iota(jnp.int32, (BLK, 2 * BLK), 0)
    query = lax.broadcasted_iota(jnp.int32, (BLK, 2 * BLK), 1) & (BLK - 1)
    return key > query


def _fold(t, prev):
    return jnp.where(prev, t[0:BLK], t[BLK:])


def _unfold(t, prev):
    zero = jnp.zeros_like(t)
    return jnp.concatenate([jnp.where(prev, t, zero), jnp.where(prev, zero, t)], axis=0)


def _attn_scores(qs, k_ops, bias, prev):
    return [_fold(_dot_nt(k_ops[g][hi], qs[g]), prev) + bias for g, hi in ATT_PROBLEMS]


def _attn_softmax(s, sinks_ref):
    sink = [_sink_rows(sinks_ref, g)[hi] for g, hi in ATT_PROBLEMS]
    m = [jnp.maximum(jnp.max(si, axis=0, keepdims=True), ki) for si, ki in zip(s, sink)]
    e = [jnp.exp(si - mi) for si, mi in zip(s, m)]
    es = [jnp.exp(ki - mi) for ki, mi in zip(sink, m)]
    inv = [1.0 / (jnp.sum(ei, axis=0, keepdims=True) + esi) for ei, esi in zip(e, es)]
    return [ei * ii for ei, ii in zip(e, inv)], [esi * ii for esi, ii in zip(es, inv)]


def _group_norm_all(outs):
    mu = [jnp.mean(o, axis=1, keepdims=True) for o in outs]
    xc = [o - m for o, m in zip(outs, mu)]
    var = [jnp.mean(c * c, axis=1, keepdims=True) for c in xc]
    rstd = [lax.rsqrt(v + GN_EPS) for v in var]
    return [c * r for c, r in zip(xc, rstd)], rstd


def _proj_cols(p_ref, rows, start, stop):
    pieces = []
    while start < stop:
        k = max(j for j in range(N_CHIPS) if WIN_START[j] <= start)
        end = min(stop, WIN_START[k] + SHARD_PAD)
        pieces.append(p_ref[k, rows, start - WIN_START[k]:end - WIN_START[k]])
        start = end
    return pieces[0] if len(pieces) == 1 else jnp.concatenate(pieces, axis=1)


def _retention_operands(p_ref, rows, cos_b, sin_b, qdec, kdec):
    qr = _rotate(_proj_cols(p_ref, rows, OFF_RQ, OFF_RQ + RET_QK_WIDTH).astype(F32), cos_b, sin_b)
    kr = _rotate(_proj_cols(p_ref, rows, OFF_RK, OFF_RK + RET_QK_WIDTH).astype(F32), cos_b, sin_b) * RET_SCALE
    return qr, kr, qr * qdec, kr * kdec


def _tile(t, i):
    return t[:, i * LANE:(i + 1) * LANE]


GATHER_SEMS = 7


def _allgather_steps(src_ref, full_ref, blk_ref, send_sems, recv_sems):
    block_rows = blk_ref.shape[0]
    assert block_rows % BF16_ROWS == 0 and full_ref.shape[0] == N_DEV * block_rows
    x, y, c = lax.axis_index("x"), lax.axis_index("y"), lax.axis_index("c")
    me, sibling = (x, y, c), (x, y, 1 - c)
    chips = [(1 - x, y), (x, 1 - y), (1 - x, 1 - y)]

    def rows(px, py, pc):
        return full_ref.at[pl.ds(pl.multiple_of((4 * px + 2 * py + pc) * block_rows, BF16_ROWS), block_rows), :]

    def copy(k, block, to, src=None):
        return pltpu.make_async_remote_copy(
            src_ref=rows(*block) if src is None else src, dst_ref=rows(*block),
            send_sem=send_sems.at[k], recv_sem=recv_sems.at[k], device_id=to, device_id_type=MESH_ID)

    first = [copy(0, me, sibling, src=blk_ref)] + [copy(1 + j, me, (*chip, c), src=blk_ref) for j, chip in enumerate(chips)]
    passed = [copy(4 + j, (*chip, c), sibling) for j, chip in enumerate(chips)]

    def start():
        blk_ref[...] = src_ref[pl.ds(pl.multiple_of(c * block_rows, BF16_ROWS), block_rows), :].astype(BF16)
        rows(*me)[...] = blk_ref[...]
        for cp in first:
            cp.start()

    def forward():
        for j, chip in enumerate(chips):
            copy(1 + j, (*chip, c), me).wait_recv()
            passed[j].start()

    def finish():
        copy(0, sibling, me).wait_recv()
        for j, chip in enumerate(chips):
            copy(4 + j, (*chip, 1 - c), me).wait_recv()
        for cp in first + passed:
            cp.wait_send()

    return start, forward, finish


WINDOW_HALF = SHARD_PAD // 2
WINDOW_SEMS = GATHER_SEMS + 1


def _window_gather(wi_ref, w4_ref, blk_ref, edge_ref, send_sems, recv_sems):
    x, y, c = lax.axis_index("x"), lax.axis_index("y"), lax.axis_index("c")
    me, sibling, pair = (x, y, c), (x, y, 1 - c), (x, 1 - y, c)
    chips = [(1 - x, y), (x, 1 - y), (1 - x, 1 - y)]
    own_rows, edge_rows = SHARD_IN - WINDOW_HALF, SHARD_SHIFT
    sends_edge = y == c

    def rows(px, py, pc):
        return w4_ref.at[pl.ds(pl.multiple_of((4 * px + 2 * py + pc) * WINDOW_HALF, BF16_ROWS), WINDOW_HALF), :]

    def copy(k, block, to, src=None):
        return pltpu.make_async_remote_copy(
            src_ref=rows(*block) if src is None else src, dst_ref=rows(*block),
            send_sem=send_sems.at[k], recv_sem=recv_sems.at[k], device_id=to, device_id_type=MESH_ID)

    edge_copy = pltpu.make_async_remote_copy(
        src_ref=edge_ref, dst_ref=edge_ref, send_sem=send_sems.at[GATHER_SEMS], recv_sem=recv_sems.at[GATHER_SEMS],
        device_id=pair, device_id_type=MESH_ID)
    first = [copy(0, me, sibling, src=blk_ref)] + [copy(1 + j, me, (*chip, c), src=blk_ref) for j, chip in enumerate(chips)]
    passed = [copy(4 + j, (*chip, c), sibling) for j, chip in enumerate(chips)]

    def at(start, size):
        return pl.ds(pl.multiple_of(start, BF16_ROWS), size)

    def start():
        @pl.when(sends_edge)
        def _():
            edge_ref[...] = wi_ref[at((1 - y) * (SHARD_IN - edge_rows), edge_rows), :].astype(BF16)
            edge_copy.start()
            blk_ref[...] = wi_ref[at(c * own_rows, WINDOW_HALF), :].astype(BF16)

        @pl.when(jnp.logical_not(sends_edge))
        def _():
            blk_ref[at(y * edge_rows, own_rows), :] = wi_ref[at(c * WINDOW_HALF, own_rows), :].astype(BF16)
            edge_copy.wait_recv()
            blk_ref[at((1 - y) * own_rows, edge_rows), :] = edge_ref[...]

        rows(*me)[...] = blk_ref[...]
        for cp in first[:-1]:
            cp.start()
        copy(0, sibling, me).wait_recv()

    def send_far():
        first[-1].start()

    def forward(j):
        copy(1 + j, (*chips[j], c), me).wait_recv()
        passed[j].start()

    def wait_forwarded(j):
        copy(4 + j, (*chips[j], 1 - c), me).wait_recv()

    def finish():
        for cp in first + passed:
            cp.wait_send()

        @pl.when(sends_edge)
        def _():
            edge_copy.wait_send()

    return start, send_far, forward, wait_forwarded, finish


def _in_proj(x, norm_g, w_in_t_shard, w_out_shard, tb):
    seq = x.shape[0]
    nblk = seq // tb
    last = nblk - 1
    chip_of_panel = (None, 1, 0, 2)

    def body(win_ref, x_ref, g_ref, wi_ref, wo_ref, p_ref, ht_ref, wt_ref, wout_ref,
             w4, blk, edge, hbuf, wout_full, blko, send_sems, recv_sems, send_sems_o, recv_sems_o):
        q, i = pl.program_id(0), pl.program_id(1)
        chip = 2 * lax.axis_index("x") + lax.axis_index("y")
        in_start, in_send_far, in_forward, in_wait_forwarded, in_finish = _window_gather(
            wi_ref, w4, blk, edge, send_sems, recv_sems)
        out_start, out_forward, out_finish = _allgather_steps(wo_ref, wout_full, blko, send_sems_o, recv_sems_o)
        rows = pl.ds(pl.multiple_of(i * tb, tb), tb)

        @pl.when(jnp.logical_and(q == 0, i == 0))
        def _():
            in_start()

        @pl.when(jnp.logical_and(q == 0, i == min(last // 2 + 1, last)))
        def _():
            in_send_far()

        @pl.when(jnp.logical_and(q == 1, i == 0))
        def _():
            out_start()

        for panel in range(1, N_CHIPS):
            @pl.when(jnp.logical_and(q == panel, i == 0))
            def _():
                in_wait_forwarded(chip_of_panel[panel])

        @pl.when(q == 0)
        def _():
            xv = x_ref[...]
            r = lax.rsqrt(jnp.mean(xv * xv, axis=1, keepdims=True) + RMS_EPS)
            h = (xv * r) * g_ref[...]
            hbuf[rows, :] = h.astype(BF16)
            ht_ref[...] = h.T.astype(BF16)

        owner = jnp.bitwise_xor(chip, q)
        window = w4[pl.ds(pl.multiple_of(owner * SHARD_PAD, SHARD_PAD), SHARD_PAD), :]
        p_ref[...] = _dot_nt(hbuf[rows, :], window).astype(BF16)

        for panel in range(1, N_CHIPS):
            @pl.when(jnp.logical_and(q == panel - 1, i == last))
            def _():
                in_forward(chip_of_panel[panel])

        @pl.when(jnp.logical_and(q == N_CHIPS - 1, i == 0))
        def _():
            out_forward()

        @pl.when(jnp.logical_and(q == N_CHIPS - 1, i == last))
        def _():
            in_finish()
            for k in range(N_CHIPS):
                first = k * SHARD_PAD + (k % 2) * SHARD_SHIFT
                wt_ref[k * SHARD_IN:(k + 1) * SHARD_IN, :] = w4[first:first + SHARD_IN, :]
            out_finish()
            wout_ref[...] = wout_full[...]

    whole = lambda shape: pl.BlockSpec(shape, lambda q, i, win: (0,) * len(shape), pipeline_mode=pl.Buffered(1))
    first_panel = lambda q, i: jnp.where(q == 0, i, last)
    grid_spec = pltpu.PrefetchScalarGridSpec(
        num_scalar_prefetch=1, grid=(N_CHIPS, nblk),
        in_specs=[pl.BlockSpec((tb, D_MODEL), lambda q, i, win: (first_panel(q, i), 0)), whole((1, D_MODEL)),
                  whole((SHARD_IN, D_MODEL)), whole((SHARD_OUT, D_MODEL))],
        out_specs=(pl.BlockSpec((None, tb, SHARD_PAD), lambda q, i, win: (win[q], i, 0)),
                   pl.BlockSpec((D_MODEL, tb), lambda q, i, win: (0, first_panel(q, i))),
                   whole((IN_WIDTH, D_MODEL)), whole((MIX_WIDTH, D_MODEL))),
        scratch_shapes=[
            pltpu.VMEM((N_CHIPS * SHARD_PAD, D_MODEL), BF16), pltpu.VMEM((WINDOW_HALF, D_MODEL), BF16),
            pltpu.VMEM((SHARD_SHIFT, D_MODEL), BF16), pltpu.VMEM((seq, D_MODEL), BF16),
            pltpu.VMEM((MIX_WIDTH, D_MODEL), BF16), pltpu.VMEM((SHARD_OUT // 2, D_MODEL), BF16),
            pltpu.SemaphoreType.DMA((WINDOW_SEMS,)), pltpu.SemaphoreType.DMA((WINDOW_SEMS,)),
            pltpu.SemaphoreType.DMA((GATHER_SEMS,)), pltpu.SemaphoreType.DMA((GATHER_SEMS,)),
        ])
    chip = 2 * lax.axis_index("x") + lax.axis_index("y")
    owner = jnp.bitwise_xor(chip, jnp.arange(N_CHIPS, dtype=jnp.int32))
    return _call(
        body, name="in_proj", grid_spec=grid_spec,
        out_shape=(jax.ShapeDtypeStruct((N_CHIPS, seq, SHARD_PAD), BF16), jax.ShapeDtypeStruct((D_MODEL, seq), BF16),
                   jax.ShapeDtypeStruct((IN_WIDTH, D_MODEL), BF16), jax.ShapeDtypeStruct((MIX_WIDTH, D_MODEL), BF16)),
        compiler_params=_params(60, ("arbitrary", "arbitrary")),
    )(owner.astype(jnp.int32), x, norm_g, w_in_t_shard, w_out_shard)


def _mix_fwd(proj, x, target, w_out, final_g, gn_gain, sinks, tables, tb):
    seq = x.shape[0]
    nsub = tb // BLK
    cos_t, sin_t, decay_in, qdec_t, kdec_t, cd_t, bias_t = tables

    def body(p_ref, x_ref, t_ref, cos_ref, sin_ref, wout_ref, fg_ref, gain_ref, sinks_ref, din_ref, qdec_ref,
             kdec_ref, cd_ref, bias_ref, mix_ref, dxo_ref, st_ref, loss_ref, gfin_ref, oatt_ref, probs_ref, psink_ref,
             kprev_ref, vprev_ref, state_ref):
        i = pl.program_id(0)

        @pl.when(i == 0)
        def _():
            kprev_ref[...] = jnp.zeros_like(kprev_ref)
            vprev_ref[...] = jnp.zeros_like(vprev_ref)
            state_ref[...] = jnp.zeros_like(state_ref)
            loss_ref[...] = jnp.zeros_like(loss_ref)
            gfin_ref[...] = jnp.zeros_like(gfin_ref)

        prev = _in_previous_block()

        def sub(j, carry):
            kp, vp, states = carry
            rows = pl.ds(pl.multiple_of(j * BLK, BLK), BLK)
            bias = bias_ref[jnp.where(jnp.logical_or(i > 0, j > 0), 1, 0)]

            aq = _proj_cols(p_ref, rows, OFF_AQ, OFF_AQ + ATT_WIDTH)
            ak = _proj_cols(p_ref, rows, OFF_AK, OFF_AK + ATT_KV_WIDTH).astype(F32)
            av = _proj_cols(p_ref, rows, OFF_AV, OFF_AV + ATT_KV_WIDTH).astype(F32)
            az = _proj_cols(p_ref, rows, OFF_AZ, OFF_AZ + ATT_WIDTH).astype(F32)
            k_ops = _kv_operands(jnp.concatenate([kp, ak], axis=0))
            v_ops = _kv_operands(jnp.concatenate([vp, av], axis=0))
            qs = [(_stack_tiles(aq, 2 * g) * ATT_SCALE).astype(BF16) for g in range(ATT_KV_HEADS)]

            qr, kr, qd, kd = _retention_operands(p_ref, rows, cos_ref[rows, :], sin_ref[rows, :],
                                                 qdec_ref[...], kdec_ref[...])
            s = _attn_scores(qs, k_ops, bias, prev)
            heads = [(t, hh) for t in range(PAIRS) for hh in range(2)]
            sc = [_dot_nt(jnp.concatenate(_split_heads(_tile(qr, t)), axis=0).astype(BF16), _tile(kr, t).astype(BF16))
                  * din_ref[t] for t in range(PAIRS)]
            qd_heads = [_split_heads(_tile(qd, t)) for t in range(PAIRS)]
            state_b = [states[t].astype(BF16) for t in range(PAIRS)]
            vs = [_proj_cols(p_ref, rows, OFF_RV + h * RET_V_DIM, OFF_RV + (h + 1) * RET_V_DIM) for h in range(RET_HEADS)]
            rzs = [_proj_cols(p_ref, rows, OFF_RZ + h * RET_V_DIM, OFF_RZ + (h + 1) * RET_V_DIM).astype(F32)
                   for h in range(RET_HEADS)]
            lhs = [jnp.concatenate([sc[t][hh * BLK:(hh + 1) * BLK].astype(BF16), qd_heads[t][hh].astype(BF16)], axis=1)
                   for t, hh in heads]
            p, p_sink = _attn_softmax(s, sinks_ref)
            o_ret = [_dot(lhs[2 * t + hh], jnp.concatenate([vs[2 * t + hh], state_b[t]], axis=0)) for t, hh in heads]
            p = [pi.astype(BF16) for pi in p]
            o_tiles = []
            for g in range(ATT_KV_HEADS):
                p_cat = jnp.concatenate([_unfold(p[2 * g], prev), _unfold(p[2 * g + 1], prev)], axis=0)
                o = _dot_tn(p_cat, jnp.concatenate(v_ops[g], axis=0))
                o_tiles += [o[0:BLK], o[BLK:]]
            ons, _ = _group_norm_all(o_ret)
            new_states = [states[t] * cd_ref[t]
                          + _dot_tn(jnp.concatenate(_split_heads(_tile(kd, t)), axis=0).astype(BF16),
                                    jnp.concatenate([vs[2 * t], vs[2 * t + 1]], axis=0)) for t in range(PAIRS)]
            o_att = jnp.concatenate(o_tiles, axis=1)
            out = [o_att * (az * _sigmoid(az))]
            out += [(ons[h] * gain_ref[:, h * RET_V_DIM:(h + 1) * RET_V_DIM]) * (rzs[h] * _sigmoid(rzs[h]))
                    for h in range(RET_HEADS)]
            mix_ref[rows, :] = jnp.concatenate(out, axis=1).astype(BF16)
            oatt_ref[rows, :] = o_att
            probs_ref[j] = jnp.stack(p)
            psink_ref[j] = jnp.concatenate(p_sink, axis=0)
            st_ref[j] = jnp.stack(states)
            return ak, av, tuple(new_states)

        carry = (kprev_ref[...], vprev_ref[...], tuple(state_ref[t] for t in range(PAIRS)))
        for jb in range(nsub):
            carry = sub(jnp.int32(jb), carry)
        kp, vp, states = carry
        kprev_ref[...] = kp
        vprev_ref[...] = vp
        state_ref[...] = jnp.stack(states)

        xo = x_ref[...] + _dot(mix_ref[...], wout_ref[...])
        r2 = lax.rsqrt(jnp.mean(xo * xo, axis=1, keepdims=True) + RMS_EPS)
        xn = xo * r2
        err = xn * fg_ref[...] - t_ref[...]
        loss_ref[...] += jnp.sum(err * err) * (0.5 / D_MODEL)
        dy = err * (1.0 / D_MODEL)
        gfin_ref[...] += jnp.sum(dy * xn, axis=0, keepdims=True)
        u = dy * fg_ref[...]
        dxo_ref[...] = r2 * u - xn * (r2 * jnp.mean(u * xn, axis=1, keepdims=True))

    blk_rows = lambda w: pl.BlockSpec((tb, w), lambda i: (i, 0))
    state_shape = (PAIRS, 2 * RET_QK_DIM, RET_V_DIM)
    return _call(
        body, name="mix_fwd", grid=(seq // tb,),
        out_shape=(
            jax.ShapeDtypeStruct((seq, MIX_WIDTH), BF16),
            jax.ShapeDtypeStruct((seq, D_MODEL), F32),
            jax.ShapeDtypeStruct((seq // BLK,) + state_shape, F32),
            jax.ShapeDtypeStruct((8, LANE), F32),
            jax.ShapeDtypeStruct((1, D_MODEL), F32),
            jax.ShapeDtypeStruct((seq, ATT_WIDTH), F32),
            jax.ShapeDtypeStruct((seq // BLK, len(ATT_PROBLEMS), BLK, 2 * BLK), BF16),
            jax.ShapeDtypeStruct((seq // BLK, len(ATT_PROBLEMS), 2 * BLK), F32),
        ),
        in_specs=[
            pl.BlockSpec((N_CHIPS, tb, SHARD_PAD), lambda i: (0, i, 0)),
            blk_rows(D_MODEL), blk_rows(D_MODEL), blk_rows(LANE), blk_rows(LANE),
            _const_spec((MIX_WIDTH, D_MODEL)), _const_spec((1, D_MODEL)), _const_spec((1, RET_WIDTH)),
            pl.BlockSpec(memory_space=pltpu.SMEM),
            _const_spec((PAIRS, 2 * BLK, BLK)), _const_spec((BLK, RET_QK_WIDTH)), _const_spec((BLK, RET_QK_WIDTH)),
            _const_spec(state_shape), _const_spec((2, BLK, 2 * BLK)),
        ],
        out_specs=(
            blk_rows(MIX_WIDTH), blk_rows(D_MODEL),
            pl.BlockSpec((nsub,) + state_shape, lambda i: (i, 0, 0, 0)),
            _const_spec((8, LANE)), _const_spec((1, D_MODEL)), blk_rows(ATT_WIDTH),
            pl.BlockSpec((nsub, len(ATT_PROBLEMS), BLK, 2 * BLK), lambda i: (i, 0, 0, 0)),
            pl.BlockSpec((nsub, len(ATT_PROBLEMS), 2 * BLK), lambda i: (i, 0, 0)),
        ),
        scratch_shapes=[
            pltpu.VMEM((BLK, ATT_KV_WIDTH), F32), pltpu.VMEM((BLK, ATT_KV_WIDTH), F32),
            pltpu.VMEM(state_shape, F32),
        ],
        compiler_params=_params(48, ("arbitrary",)),
    )(proj, x, target, cos_t, sin_t, w_out, final_g, gn_gain, sinks, decay_in, qdec_t, kdec_t, cd_t, bias_t)


def _mix_bwd(proj, dxo, mix, o_att, probs, p_sinks, states, x, w_out, w_in_t, norm_g, gn_gain, tables, tb):
    seq = dxo.shape[0]
    nsub = tb // BLK
    nblk = seq // tb
    cos_t, sin_t, decay_in, qdec_t, kdec_t, cd_t, _ = tables
    kv_cols = OFF_AK // (2 * ATT_KV_WIDTH)
    state_shape = (PAIRS, 2 * RET_QK_DIM, RET_V_DIM)

    def body(p_ref, pkv_ref, dxo_ref, mix_ref, oatt_ref, probs_ref, psink_ref, st_ref, cos_ref, sin_ref, x_ref, wout_ref,
             win_ref, g_ref, gain_ref, din_ref, qdec_ref, kdec_ref, cd_ref,
             dp_ref, gx_ref, gwout_ref, gnorm_ref, dgain_ref, dsink_ref,
             dmix_ref, kv_ref, dkc_ref, dvc_ref, gst_ref):
        i = pl.program_id(0)

        @pl.when(i == 0)
        def _():
            gwout_ref[...] = jnp.zeros_like(gwout_ref)
            gnorm_ref[...] = jnp.zeros_like(gnorm_ref)
            dgain_ref[...] = jnp.zeros_like(dgain_ref)
            dsink_ref[...] = jnp.zeros_like(dsink_ref)
            dkc_ref[...] = jnp.zeros_like(dkc_ref)
            dvc_ref[...] = jnp.zeros_like(dvc_ref)
            gst_ref[...] = jnp.zeros_like(gst_ref)

        dxo_b = dxo_ref[...].astype(BF16)
        dmix_ref[...] = _dot_nt(dxo_b, wout_ref[...])
        gwout_ref[...] += _dot_tn(mix_ref[...], dxo_b)
        kv_ref[0:BLK, :] = pkv_ref[...].astype(F32)
        kv_ref[BLK:, :] = _proj_cols(p_ref, slice(None), OFF_AK, OFF_AK + 2 * ATT_KV_WIDTH).astype(F32)
        low = _low_lanes((BLK, LANE))
        low2 = _low_lanes((2 * BLK, LANE))
        lane = lax.broadcasted_iota(jnp.int32, (1, LANE), 1)
        prev = _in_previous_block()

        def sub(jj, carry):
            dkc, dvc, gsts, dgain, dsink = carry
            j = nsub - 1 - jj
            rows = pl.ds(pl.multiple_of(j * BLK, BLK), BLK)
            both = pl.ds(pl.multiple_of(j * BLK, BLK), 2 * BLK)


            aq = _proj_cols(p_ref, rows, OFF_AQ, OFF_AQ + ATT_WIDTH)
            az = _proj_cols(p_ref, rows, OFF_AZ, OFF_AZ + ATT_WIDTH).astype(F32)
            k_ops = _kv_operands(kv_ref[both, 0:ATT_KV_WIDTH])
            v_ops = _kv_operands(kv_ref[both, ATT_KV_WIDTH:2 * ATT_KV_WIDTH])
            da = dmix_ref[rows, 0:ATT_WIDTH]
            sig = _sigmoid(az)
            d_o = da * (az * sig)
            qs = [(_stack_tiles(aq, 2 * g) * ATT_SCALE).astype(BF16) for g in range(ATT_KV_HEADS)]
            dos = [_stack_tiles(d_o, 2 * g).astype(BF16) for g in range(ATT_KV_HEADS)]
            p_b = [probs_ref[j, k] for k in range(len(ATT_PROBLEMS))]
            p = [pk.astype(F32) for pk in p_b]
            sink_all = psink_ref[j]
            p_sink = [sink_all[k:k + 1, :] for k in range(len(ATT_PROBLEMS))]
            cos_b, sin_b = cos_ref[rows, :], sin_ref[rows, :]
            qdec, kdec = qdec_ref[...], kdec_ref[...]
            qr, kr, qd, kd = _retention_operands(p_ref, rows, cos_b, sin_b, qdec, kdec)
            heads = [(t, hh) for t in range(PAIRS) for hh in range(2)]
            head_cols = [slice(h * RET_V_DIM, (h + 1) * RET_V_DIM) for h in range(RET_HEADS)]
            q_rows = [jnp.concatenate(_split_heads(_tile(qr, t)), axis=0).astype(BF16) for t in range(PAIRS)]
            k_rows = [jnp.concatenate(_split_heads(_tile(kr, t)), axis=0).astype(BF16) for t in range(PAIRS)]
            din = [din_ref[t] for t in range(PAIRS)]

            dpr = [_fold(_dot_nt(v_ops[g][hi], dos[g]), prev) for g, hi in ATT_PROBLEMS]
            sc = [(_dot_nt(q_rows[t], _tile(kr, t).astype(BF16)) * din[t]).astype(BF16) for t in range(PAIRS)]
            qd_heads = [_split_heads(_tile(qd, t)) for t in range(PAIRS)]
            kd_heads = [_split_heads(_tile(kd, t)) for t in range(PAIRS)]
            state_b = [st_ref[j, t].astype(BF16) for t in range(PAIRS)]
            gst_b = [gsts[t].astype(BF16) for t in range(PAIRS)]
            vs = [_proj_cols(p_ref, rows, OFF_RV + h * RET_V_DIM, OFF_RV + (h + 1) * RET_V_DIM) for h in range(RET_HEADS)]
            rzs = [_proj_cols(p_ref, rows, OFF_RZ + h * RET_V_DIM, OFF_RZ + (h + 1) * RET_V_DIM).astype(F32)
                   for h in range(RET_HEADS)]
            drs = [dmix_ref[rows, ATT_WIDTH + h * RET_V_DIM:ATT_WIDTH + (h + 1) * RET_V_DIM] for h in range(RET_HEADS)]
            gains = [gain_ref[:, c] for c in head_cols]
            lhs = [jnp.concatenate([sc[t][hh * BLK:(hh + 1) * BLK], qd_heads[t][hh].astype(BF16)], axis=1) for t, hh in heads]
            rhs = [jnp.concatenate([vs[2 * t + hh], state_b[t]], axis=0) for t, hh in heads]

            delta = [jnp.sum(pi * di, axis=0, keepdims=True) for pi, di in zip(p, dpr)]
            ds = [_unfold((pi * (di - ti)).astype(BF16), prev) for pi, di, ti in zip(p, dpr, delta)]
            o_ret = [_dot(l, r) for l, r in zip(lhs, rhs)]
            for (g, hi), ki, ti in zip(ATT_PROBLEMS, p_sink, delta):
                sink_part = ki * ti
                for half in range(2):
                    tot = jnp.sum(sink_part[:, half * BLK:(half + 1) * BLK], axis=1, keepdims=True)
                    dsink = dsink - jnp.where(lane == 4 * g + 2 * half + hi, tot, 0.0)

            dq_tiles, dk_sums, dv_sums = [], [], []
            for g in range(ATT_KV_HEADS):
                ds_cat = jnp.concatenate([ds[2 * g], ds[2 * g + 1]], axis=0)
                p_cat = jnp.concatenate([_unfold(p_b[2 * g], prev), _unfold(p_b[2 * g + 1], prev)], axis=0)
                dqs = _dot_tn(ds_cat, jnp.concatenate(k_ops[g], axis=0)) * ATT_SCALE
                dq_tiles += [dqs[0:BLK], dqs[BLK:]]
                dk_sums.append(_dot(ds_cat, qs[g]))
                dv_sums.append(_dot(p_cat, dos[g]))
            ons, rstds = _group_norm_all(o_ret)
            sig_r = [_sigmoid(z) for z in rzs]
            dgn = [d * (z * g) for d, z, g in zip(drs, rzs, sig_r)]
            dz_parts = [d * (o * gn) * (g * (1.0 + z * (1.0 - g))) for d, o, gn, g, z in zip(drs, ons, gains, sig_r, rzs)]
            dgain_parts = [jnp.sum(d * o, axis=0, keepdims=True) for d, o in zip(dgn, ons)]
            don = [d * gn for d, gn in zip(dgn, gains)]
            mean_don = [jnp.mean(d, axis=1, keepdims=True) for d in don]
            mean_don_on = [jnp.mean(d * o, axis=1, keepdims=True) for d, o in zip(don, ons)]
            dob = [(r * (d - a - o * b)).astype(BF16) for r, d, a, o, b in zip(rstds, don, mean_don, ons, mean_don_on)]

            dlhs = [_dot_nt(d, r) for d, r in zip(dob, rhs)]
            drhs = [_dot_tn(l, d) for l, d in zip(lhs, dob)]
            dkds = [_dot_nt(vs[2 * t + hh], gst_b[t]) for t, hh in heads]
            dv_parts = [drhs[2 * t + hh][0:BLK] + _dot(kd_heads[t][hh].astype(BF16), gst_b[t]) for t, hh in heads]
            daz = da * oatt_ref[rows, :] * (sig * (1.0 + az * (1.0 - sig)))

            def kv_grad(sums):
                (a0, b0), (a1, b1) = [(s[0:2 * BLK], s[2 * BLK:]) for s in sums]
                return jnp.where(low2, a0, b1) + pltpu.roll(jnp.where(low2, a1, b0), HALF_LANE, 1)

            dk_both, dv_both = kv_grad(dk_sums), kv_grad(dv_sums)
            dak = dk_both[BLK:] + dkc
            dav = dv_both[BLK:] + dvc
            das = [(dlhs[2 * t + hh][:, 0:BLK] * din[t][hh * BLK:(hh + 1) * BLK]).astype(BF16) for t, hh in heads]
            new_gsts = [gsts[t] * cd_ref[t] + drhs[2 * t][BLK:] + drhs[2 * t + 1][BLK:] for t in range(PAIRS)]
            dq_parts = [_dot(jnp.concatenate([das[2 * t], das[2 * t + 1]], axis=1), k_rows[t])
                        + jnp.where(low, dlhs[2 * t][:, BLK:], dlhs[2 * t + 1][:, BLK:]) * _tile(qdec, t)
                        for t in range(PAIRS)]
            dk_parts = [_dot_tn(jnp.concatenate([das[2 * t], das[2 * t + 1]], axis=0), q_rows[t])
                        + jnp.where(low, dkds[2 * t], dkds[2 * t + 1]) * _tile(kdec, t) for t in range(PAIRS)]
            drq = _rotate_transposed(jnp.concatenate(dq_parts, axis=1), cos_b, sin_b)
            drk = _rotate_transposed(jnp.concatenate(dk_parts, axis=1) * RET_SCALE, cos_b, sin_b)

            dproj = jnp.concatenate(
                [jnp.concatenate(dq_tiles, axis=1), dak, dav, daz, drq, drk] + dv_parts + dz_parts, axis=1).astype(BF16)
            for k in range(N_CHIPS):
                dp_ref[k, rows, :] = dproj[:, WIN_START[k]:WIN_START[k] + SHARD_PAD]
            dgain = dgain + jnp.concatenate(dgain_parts, axis=1)
            return dk_both[0:BLK], dv_both[0:BLK], tuple(new_gsts), dgain, dsink

        carry = (dkc_ref[...], dvc_ref[...], tuple(gst_ref[t] for t in range(PAIRS)), dgain_ref[...], dsink_ref[...])
        for jb in range(nsub):
            carry = sub(jnp.int32(jb), carry)
        dkc, dvc, gsts, dgain, dsink = carry
        dkc_ref[...] = dkc
        dvc_ref[...] = dvc
        gst_ref[...] = jnp.stack(gsts)
        dgain_ref[...] = dgain
        dsink_ref[...] = dsink

        dh = None
        for k in range(N_CHIPS):
            skip = (k % 2) * LANE
            first = WIN_START[k] + skip
            part = _dot(dp_ref[k, :, skip:], win_ref[first:WIN_START[k] + SHARD_PAD, :])
            dh = part if dh is None else dh + part
        xv = x_ref[...]
        r = lax.rsqrt(jnp.mean(xv * xv, axis=1, keepdims=True) + RMS_EPS)
        xn = xv * r
        gnorm_ref[...] += jnp.sum(dh * xn, axis=0, keepdims=True)
        u = dh * g_ref[...]
        gx_ref[...] = dxo_ref[...] + r * u - xn * (r * jnp.mean(u * xn, axis=1, keepdims=True))

    rev_rows = lambda w: pl.BlockSpec((tb, w), lambda i: (nblk - 1 - i, 0))
    prev_kv = pl.BlockSpec((None, BLK, 2 * ATT_KV_WIDTH),
                           lambda i: (0, jnp.maximum((nblk - 1 - i) * nsub - 1, 0), kv_cols))
    return _call(
        body, name="mix_bwd", grid=(nblk,),
        out_shape=(
            jax.ShapeDtypeStruct((N_CHIPS, seq, SHARD_PAD), BF16),
            jax.ShapeDtypeStruct((seq, D_MODEL), F32),
            jax.ShapeDtypeStruct((MIX_WIDTH, D_MODEL), F32),
            jax.ShapeDtypeStruct((1, D_MODEL), F32),
            jax.ShapeDtypeStruct((1, RET_WIDTH), F32),
            jax.ShapeDtypeStruct((1, LANE), F32),
        ),
        in_specs=[
            pl.BlockSpec((N_CHIPS, tb, SHARD_PAD), lambda i: (0, nblk - 1 - i, 0)),
            prev_kv, rev_rows(D_MODEL), rev_rows(MIX_WIDTH), rev_rows(ATT_WIDTH),
            pl.BlockSpec((nsub, len(ATT_PROBLEMS), BLK, 2 * BLK), lambda i: (nblk - 1 - i, 0, 0, 0)),
            pl.BlockSpec((nsub, len(ATT_PROBLEMS), 2 * BLK), lambda i: (nblk - 1 - i, 0, 0)),
            pl.BlockSpec((nsub,) + state_shape, lambda i: (nblk - 1 - i, 0, 0, 0)),
            rev_rows(LANE), rev_rows(LANE), rev_rows(D_MODEL),
            _const_spec((MIX_WIDTH, D_MODEL)), _const_spec((IN_WIDTH, D_MODEL)), _const_spec((1, D_MODEL)),
            _const_spec((1, RET_WIDTH)),
            _const_spec((PAIRS, 2 * BLK, BLK)), _const_spec((BLK, RET_QK_WIDTH)), _const_spec((BLK, RET_QK_WIDTH)),
            _const_spec(state_shape),
        ],
        out_specs=(
            pl.BlockSpec((N_CHIPS, tb, SHARD_PAD), lambda i: (0, nblk - 1 - i, 0)),
            rev_rows(D_MODEL), _const_spec((MIX_WIDTH, D_MODEL)), _const_spec((1, D_MODEL)),
            _const_spec((1, RET_WIDTH)), _const_spec((1, LANE)),
        ),
        scratch_shapes=[
            pltpu.VMEM((tb, MIX_WIDTH), F32),
            pltpu.VMEM((tb + BLK, 2 * ATT_KV_WIDTH), F32),
            pltpu.VMEM((BLK, ATT_KV_WIDTH), F32), pltpu.VMEM((BLK, ATT_KV_WIDTH), F32),
            pltpu.VMEM(state_shape, F32),
        ],
        compiler_params=_params(60, ("arbitrary",)),
    )(proj, proj, dxo, mix, o_att, probs, p_sinks, states, cos_t, sin_t, x, w_out, w_in_t, norm_g, gn_gain, decay_in,
      qdec_t, kdec_t, cd_t)


def _gw_in_reduce(h_t, dproj, gw_out, small, tb):
    seq = dproj.shape[1]
    nblk = seq // tb
    last = nblk - 1
    hand_on = min(1, last)
    half = D_MODEL // 2
    A, B, C, N_SEMS = 0, N_CHIPS, 2 * N_CHIPS, 2 * N_CHIPS + 1

    def body(win_ref, ht_ref, dp_ref, gwo_hbm, s0_ref, s1_ref, s2_ref, s3_ref, s4_ref, out_ref, fout_ref, packsum_ref,
             acc, sib, send_buf, b_in, fin, fout, mine_out, sib_out, send_out, b_out, pack_ref, packs,
             send_sems, recv_sems, local_sems):
        p, i = pl.program_id(0), pl.program_id(1)
        small_start, small_hand_on, small_finish = _small_exchange(
            gwo_hbm, (s0_ref, s1_ref, s2_ref, s3_ref, s4_ref), fout, mine_out, sib_out, send_out, b_out, pack_ref,
            packs, send_sems, recv_sems, local_sems, N_SEMS)

        @pl.when(jnp.logical_and(p == 0, i == 0))
        def _():
            small_start()

        @pl.when(jnp.logical_and(p == 0, i == hand_on))
        def _():
            small_hand_on()

        x, y, c = lax.axis_index("x"), lax.axis_index("y"), lax.axis_index("c")
        chip = 2 * x + y
        sibling = (x, y, 1 - c)
        mine = pl.ds(pl.multiple_of(c * half, half), half)
        other = pl.ds(pl.multiple_of((1 - c) * half, half), half)

        def remote(src, dst, send_k, recv_k, to):
            return pltpu.make_async_remote_copy(src_ref=src, dst_ref=dst, send_sem=send_sems.at[send_k],
                                                recv_sem=recv_sems.at[recv_k], device_id=to, device_id_type=MESH_ID)

        part = _dot(ht_ref[...], dp_ref[...])
        slot = p % 2

        @pl.when(i == 0)
        def _():
            acc[slot] = part

        @pl.when(i > 0)
        def _():
            acc[slot] += part

        for q in range(N_CHIPS):
            s = q % 2
            to_sibling = remote(acc.at[s, other, :], sib.at[s], A + q, A + q, sibling)

            @pl.when(jnp.logical_and(p == q, i == last))
            def _():
                to_sibling.start()

            if q < N_CHIPS - 1:
                dest = (chip + 1 + q) % N_CHIPS

                @pl.when(jnp.logical_and(p == q + 1, i == hand_on))
                def _():
                    to_sibling.wait_recv()
                    send_buf[q] = (acc[s, mine, :] + sib[s]).astype(BF16)
                    remote(send_buf.at[q], b_in.at[chip], B + q, B + chip, (dest // 2, dest % 2, c)).start()
                    to_sibling.wait_send()
            else:
                @pl.when(jnp.logical_and(p == q, i == last))
                def _():
                    to_sibling.wait_recv()
                    fin[mine, :] = acc[s, mine, :] + sib[s]
                    for j in range(N_CHIPS):
                        @pl.when(j != chip)
                        def _():
                            remote(b_in.at[j], b_in.at[j], B + j, B + j, sibling).wait_recv()
                            fin[mine, :] += b_in[j].astype(F32)
                    to_core = remote(fin.at[mine, :], fin.at[mine, :], C, C, sibling)
                    to_core.start()
                    remote(fin.at[other, :], fin.at[other, :], C, C, sibling).wait_recv()
                    out_ref[...] = fin[...]
                    to_core.wait_send()
                    to_sibling.wait_send()
                    for k in range(N_CHIPS - 1):
                        remote(send_buf.at[k], b_in.at[chip], B + k, B + k, sibling).wait_send()
                    packsum_ref[...] = small_finish()
                    fout_ref[...] = fout[...]

    whole = lambda shape: pl.BlockSpec(shape, lambda p, i, win: (0,) * len(shape), pipeline_mode=pl.Buffered(1))
    grid_spec = pltpu.PrefetchScalarGridSpec(
        num_scalar_prefetch=1, grid=(N_CHIPS, nblk),
        in_specs=[pl.BlockSpec((D_MODEL, tb), lambda p, i, win: (0, i)),
                  pl.BlockSpec((None, tb, SHARD_PAD), lambda p, i, win: (win[p], i, 0)),
                  pl.BlockSpec(memory_space=pl.ANY)] + [whole(s.shape) for s in small],
        out_specs=(whole((D_MODEL, SHARD_PAD)), whole((SHARD_OUT, D_MODEL)), whole((PACK_ROWS, D_MODEL))),
        scratch_shapes=[
            pltpu.VMEM((2, D_MODEL, SHARD_PAD), F32), pltpu.VMEM((2, half, SHARD_PAD), F32),
            pltpu.VMEM((N_CHIPS - 1, half, SHARD_PAD), BF16), pltpu.VMEM((N_CHIPS, half, SHARD_PAD), BF16),
            pltpu.VMEM((D_MODEL, SHARD_PAD), F32),
        ] + _small_exchange_scratch() + [
            pltpu.SemaphoreType.DMA((N_SEMS + SMALL_SEMS,)), pltpu.SemaphoreType.DMA((N_SEMS + SMALL_SEMS,)),
            pltpu.SemaphoreType.DMA((N_CHIPS,)),
        ])
    chip = 2 * lax.axis_index("x") + lax.axis_index("y")
    owner = (chip + 1 + jnp.arange(N_CHIPS, dtype=jnp.int32)) % N_CHIPS
    return _call(
        body, name="gw_in_reduce", grid_spec=grid_spec,
        out_shape=(jax.ShapeDtypeStruct((D_MODEL, SHARD_PAD), F32), jax.ShapeDtypeStruct((SHARD_OUT, D_MODEL), F32),
                   jax.ShapeDtypeStruct((PACK_ROWS, D_MODEL), F32)),
        compiler_params=_params(52, ("arbitrary", "arbitrary")),
    )(owner.astype(jnp.int32), h_t, dproj, gw_out, *small)


SMALL_SEMS = 17


def _small_exchange(gwo_hbm, small_refs, fout_ref, mine_out, sib_out, send_out, b_out, pack_ref, packs,
                    send_sems, recv_sems, local_sems, base):
    half_out = SHARD_OUT // 2
    A_OUT, B_OUT, C_OUT, PACK = base, base + 4, base + 8, base + 9
    assert len(small_refs) == PACK_PARTS
    x, y, c = lax.axis_index("x"), lax.axis_index("y"), lax.axis_index("c")
    chip = 2 * x + y
    dev = 2 * chip + c
    sibling = (x, y, 1 - c)

    def remote(src, dst, send_k, recv_k, to):
        return pltpu.make_async_remote_copy(src_ref=src, dst_ref=dst, send_sem=send_sems.at[send_k],
                                            recv_sem=recv_sems.at[recv_k], device_id=to, device_id_type=MESH_ID)

    def out_rows(j, core):
        return pl.ds(pl.multiple_of(j * SHARD_OUT + core * half_out, half_out), half_out)

    my_out_rows = pl.ds(pl.multiple_of(c * half_out, half_out), half_out)
    local = [pltpu.make_async_copy(gwo_hbm.at[out_rows(j, c), :], mine_out.at[j], local_sems.at[j])
             for j in range(N_CHIPS)]
    stage_a = [remote(gwo_hbm.at[out_rows(j, 1 - c), :], sib_out.at[j], A_OUT + j, A_OUT + j, sibling)
               for j in range(N_CHIPS)]
    mine_half_out = fout_ref.at[my_out_rows, :]
    stage_c = [remote(mine_half_out, mine_half_out, C_OUT, C_OUT, sibling)]

    def start():
        pack_ref[...] = jnp.zeros_like(pack_ref)
        for k, s_ref in enumerate(small_refs):
            pack_ref[k:k + 1, 0:s_ref.shape[1]] = s_ref[0:1, :]
        packs[dev] = pack_ref[...]
        for d in range(N_DEV):
            to = (d // 4, (d // 2) % 2, d % 2)

            @pl.when(d != dev)
            def _():
                remote(pack_ref, packs.at[dev], PACK + d, PACK + dev, to).start()

        for cp in local + stage_a:
            cp.start()

    def hand_on():
        for cp in local:
            cp.wait()
        for cp in stage_a:
            cp.wait_recv()
        for j in range(N_CHIPS):
            mine_out[j] = mine_out[j] + sib_out[j]

        for j in range(N_CHIPS):
            to = (j // 2, j % 2, c)

            @pl.when(j != chip)
            def _():
                send_out[j] = mine_out[j].astype(BF16)
                remote(send_out.at[j], b_out.at[chip], B_OUT + j, B_OUT + chip, to).start()

            @pl.when(j == chip)
            def _():
                fout_ref[my_out_rows, :] = mine_out[j]

    def finish():
        for j in range(N_CHIPS):
            @pl.when(j != chip)
            def _():
                remote(b_out.at[j], b_out.at[j], B_OUT + j, B_OUT + j, sibling).wait_recv()
                fout_ref[my_out_rows, :] += b_out[j].astype(F32)

        for cp in stage_c:
            cp.start()
        other_half_out = fout_ref.at[pl.ds(pl.multiple_of((1 - c) * half_out, half_out), half_out), :]
        remote(other_half_out, other_half_out, C_OUT, C_OUT, sibling).wait_recv()

        for d in range(N_DEV):
            @pl.when(d != dev)
            def _():
                remote(pack_ref, packs.at[d], PACK + d, PACK + d, sibling).wait_recv()
        total = packs[0]
        for d in range(1, N_DEV):
            total = total + packs[d]

        for cp in stage_a + stage_c:
            cp.wait_send()
        for j in range(N_CHIPS):
            @pl.when(j != chip)
            def _():
                remote(b_out.at[j], b_out.at[j], B_OUT + j, B_OUT + j, sibling).wait_send()
        for d in range(N_DEV):
            @pl.when(d != dev)
            def _():
                remote(pack_ref, packs.at[d], PACK + d, PACK + d, sibling).wait_send()
        return total

    return start, hand_on, finish


def _small_exchange_scratch():
    half_out = SHARD_OUT // 2
    return [
        pltpu.VMEM((SHARD_OUT, D_MODEL), F32),
        pltpu.VMEM((N_CHIPS, half_out, D_MODEL), F32), pltpu.VMEM((N_CHIPS, half_out, D_MODEL), F32),
        pltpu.VMEM((N_CHIPS, half_out, D_MODEL), BF16), pltpu.VMEM((N_CHIPS, half_out, D_MODEL), BF16),
        pltpu.VMEM((PACK_ROWS, D_MODEL), F32), pltpu.VMEM((N_DEV, PACK_ROWS, D_MODEL), F32),
    ]


def _adam_math(w, g, m, v):
    mn = ADAM_B1 * m + (1.0 - ADAM_B1) * g
    vn = ADAM_B2 * v + (1.0 - ADAM_B2) * (g * g)
    m_hat = mn / (1.0 - ADAM_B1 ** ADAM_STEP)
    v_hat = vn / (1.0 - ADAM_B2 ** ADAM_STEP)
    return -ADAM_LR * (m_hat / (jnp.sqrt(v_hat) + ADAM_EPS) + ADAM_WD * w), mn, vn


def _adamw(name, w, g, m, v, tb):
    rows, cols = w.shape

    def body(w_ref, g_ref, m_ref, v_ref, go_ref, d_ref, mo_ref, vo_ref):
        gv = g_ref[...]
        go_ref[...] = gv
        d_ref[...], mo_ref[...], vo_ref[...] = _adam_math(w_ref[...], gv, m_ref[...], v_ref[...])

    spec = pl.BlockSpec((tb, cols), lambda i: (i, 0))
    shape = jax.ShapeDtypeStruct((rows, cols), F32)
    return _call(
        body, name=name, grid=(rows // tb,), out_shape=(shape,) * 4,
        in_specs=[spec] * 4, out_specs=(spec,) * 4,
        compiler_params=_params(32, ("arbitrary",)),
    )(w, g, m, v)


def _adamw_w_in(w_t, g_window, m_t, v_t, tb):
    def body(w_ref, g_ref, m_ref, v_ref, go_ref, d_ref, mo_ref, vo_ref, gt_ref):
        gt_ref[...] = g_ref[...].T
        gv = gt_ref[pl.ds(pl.multiple_of(lax.axis_index("y") * SHARD_SHIFT, SHARD_SHIFT), SHARD_IN), :]
        go_ref[...] = gv
        d_ref[...], mo_ref[...], vo_ref[...] = _adam_math(w_ref[...], gv, m_ref[...], v_ref[...])

    spec = pl.BlockSpec((SHARD_IN, tb), lambda i: (0, i))
    shape = jax.ShapeDtypeStruct((SHARD_IN, D_MODEL), F32)
    return _call(
        body, name="adamw_w_in", grid=(D_MODEL // tb,), out_shape=(shape,) * 4,
        in_specs=[spec, pl.BlockSpec((tb, SHARD_PAD), lambda i: (i, 0)), spec, spec], out_specs=(spec,) * 4,
        scratch_shapes=[pltpu.VMEM((SHARD_PAD, tb), F32)],
        compiler_params=_params(32, ("arbitrary",)),
    )(w_t, g_window, m_t, v_t)


def _adamw_small(sums, params):
    def body(sums_ref, *refs):
        ins, outs = refs[:3 * len(params)], refs[3 * len(params):]
        for k in range(len(params)):
            w_ref, m_ref, v_ref = ins[3 * k:3 * k + 3]
            g = sums_ref[k:k + 1, 0:w_ref.shape[1]]
            go_ref, d_ref, mo_ref, vo_ref = outs[4 * k:4 * k + 4]
            go_ref[...] = g
            d_ref[...], mo_ref[...], vo_ref[...] = _adam_math(w_ref[...], g, m_ref[...], v_ref[...])

    vmem = pl.BlockSpec(memory_space=pltpu.VMEM)
    flat = [a for p in params for a in p]
    shapes = tuple(jax.ShapeDtypeStruct(p[0].shape, F32) for p in params for _ in range(4))
    res = _call(body, name="adamw_small", out_shape=shapes, in_specs=[vmem] * (1 + len(flat)),
                out_specs=(vmem,) * len(shapes), compiler_params=_params(16))(sums, *flat)
    return [res[4 * k:4 * k + 4] for k in range(len(params))]


def kernel(x, norm_g, w_in, att_sinks, ret_gn_g, w_out, final_g, loss_target, m_norm_g, m_w_in, m_att_sinks, m_ret_gn_g, m_w_out, m_final_g, v_norm_g, v_w_in, v_att_sinks, v_ret_gn_g, v_w_out, v_final_g):
    seq = x.shape[1]
    xs, tgt = x[0], loss_target[0]
    final_g2 = final_g.reshape(1, D_MODEL)
    tables = _tables(seq)

    w_in_t, m_w_in_t, v_w_in_t = w_in[0].T, m_w_in[0].T, v_w_in[0].T
    proj, h_t, w_in_full, w_out_full = _in_proj(xs, norm_g, w_in_t, w_out[0], min(TOKENS_PROJ, seq))
    mix, dxo, states, loss_part, gfin, o_att, probs, p_sinks = _mix_fwd(
        proj, xs, tgt, w_out_full, final_g2, ret_gn_g, att_sinks, tables, min(TOKENS_MIX, seq))
    dproj, grad_x, gw_out, gnorm, dgain, dsink = _mix_bwd(
        proj, dxo, mix, o_att, probs, p_sinks, states, xs, w_out_full, w_in_full, norm_g, ret_gn_g, tables,
        min(TOKENS_MIX, seq))
    g_in, g_out, sums = _gw_in_reduce(h_t, dproj, gw_out, (gnorm, gfin, dgain, dsink, loss_part),
                                      min(TOKENS_GW, seq))

    res_in = [r.T for r in _adamw_w_in(w_in_t, g_in, m_w_in_t, v_w_in_t, 256)]
    res_out = _adamw("adamw_w_out", w_out[0], g_out, m_w_out[0], v_w_out[0], SHARD_OUT)
    as_row = lambda a: a.reshape(1, D_MODEL)
    r_norm, r_final, r_gain, r_sink = _adamw_small(sums, [
        (norm_g, m_norm_g, v_norm_g), (final_g2, as_row(m_final_g), as_row(v_final_g)),
        (ret_gn_g, m_ret_gn_g, v_ret_gn_g), (att_sinks, m_att_sinks, v_att_sinks)])

    outs = []
    for k in range(4):
        outs += [r_norm[k], res_in[k][None], r_sink[k], r_gain[k], res_out[k][None], r_final[k].reshape(D_MODEL)]
    return (sums[4, 0], grad_x[None], *outs)
```

```python
import jax
import jax.numpy as jnp
import numpy as np
from jax import lax
from jax.experimental import pallas as pl
from jax.experimental.pallas import tpu as pltpu

F32 = jnp.float32
BF16 = jnp.bfloat16

D_MODEL = 1024
ATT_HEADS = 8
ATT_KV_HEADS = 2
ATT_HEAD_DIM = 64
RET_HEADS = 4
RET_QK_DIM = 64
RET_V_DIM = 128
BLK = 128
ROT_BASE = 10000.0
RMS_EPS = 1e-6
GN_EPS = 1e-6
NEG_INF = -1e30
ATT_SCALE = ATT_HEAD_DIM ** -0.5
RET_SCALE = RET_QK_DIM ** -0.5

ATT_WIDTH = ATT_HEADS * ATT_HEAD_DIM
ATT_KV_WIDTH = ATT_KV_HEADS * ATT_HEAD_DIM
RET_QK_WIDTH = RET_HEADS * RET_QK_DIM
RET_WIDTH = RET_HEADS * RET_V_DIM
MIX_WIDTH = ATT_WIDTH + RET_WIDTH
OFF_AQ = 0
OFF_AK = OFF_AQ + ATT_WIDTH
OFF_AV = OFF_AK + ATT_KV_WIDTH
OFF_AZ = OFF_AV + ATT_KV_WIDTH
OFF_RQ = OFF_AZ + ATT_WIDTH
OFF_RK = OFF_RQ + RET_QK_WIDTH
OFF_RV = OFF_RK + RET_QK_WIDTH
OFF_RZ = OFF_RV + RET_WIDTH
IN_WIDTH = OFF_RZ + RET_WIDTH

LANE = 128
BF16_ROWS = 16
HALF_LANE = LANE // 2
PAIRS = RET_QK_WIDTH // LANE
assert ATT_HEAD_DIM == HALF_LANE and RET_QK_DIM == HALF_LANE and RET_V_DIM == LANE and ATT_KV_WIDTH == LANE

N_CHIPS = 4
N_DEV = 8
SHARD_IN = IN_WIDTH // N_CHIPS
SHARD_PAD = 768
SHARD_SHIFT = SHARD_PAD - SHARD_IN
WIN_START = tuple((j * SHARD_IN) // LANE * LANE for j in range(N_CHIPS))
SHARD_OUT = MIX_WIDTH // N_CHIPS

PACK_PARTS = 5
PACK_ROWS = 8
assert PACK_PARTS <= PACK_ROWS

ADAM_LR = 0.001
ADAM_B1 = 0.9
ADAM_B2 = 0.999
ADAM_EPS = 1e-08
ADAM_WD = 0.01
ADAM_STEP = 10

VMEM_CAP = 64 * 1024 * 1024
TOKENS_PROJ = 1024
TOKENS_MIX = 512
TOKENS_GW = 2048
MESH_ID = pl.DeviceIdType.MESH


def _call(body, **kw):
    return pl.pallas_call(body, **kw)


def _params(vmem_mb, semantics=None):
    assert vmem_mb * 1024 * 1024 < VMEM_CAP
    return pltpu.CompilerParams(dimension_semantics=semantics, vmem_limit_bytes=vmem_mb * 1024 * 1024)


def _dot(a, b):
    return jnp.dot(a, b, preferred_element_type=F32)


def _dot_nt(a, b):
    return lax.dot_general(a, b, (((1,), (1,)), ((), ())), preferred_element_type=F32)


def _dot_tn(a, b):
    return lax.dot_general(a, b, (((0,), (0,)), ((), ())), preferred_element_type=F32)


def _sigmoid(z):
    return 1.0 / (1.0 + jnp.exp(-z))


def _const_spec(shape):
    nd = len(shape)
    return pl.BlockSpec(shape, lambda i: (0,) * nd, pipeline_mode=pl.Buffered(1))


def _tables(seq):
    f32 = np.float32
    pos = np.arange(seq, dtype=f32)
    theta = (f32(1.0) / (f32(ROT_BASE) ** np.linspace(0.0, 1.0, RET_QK_DIM // 2, dtype=f32))).astype(f32)
    ang = (pos[:, None] * theta[None, :]).astype(f32)
    cos, sin = np.cos(ang), np.sin(ang)
    cos2 = np.repeat(cos, 2, axis=1)
    sin2 = np.stack([-sin, sin], axis=-1).reshape(seq, RET_QK_DIM)
    cos_t = np.tile(cos2, (1, 2))
    sin_t = np.tile(sin2, (1, 2))

    log_gamma = np.log(f32(1.0) - f32(2.0) ** (f32(-5.0) - np.arange(RET_HEADS, dtype=f32))).astype(f32)
    idx = np.arange(BLK, dtype=f32)
    rel = idx[:, None] - idx[None, :]
    decay_in = np.where(rel >= 0, np.exp(log_gamma[:, None, None] * np.maximum(rel, f32(0.0))), f32(0.0))
    k_dec = np.exp(log_gamma[:, None] * (BLK - 1 - idx)[None, :])
    q_dec = np.exp(log_gamma[:, None] * (idx + 1)[None, :])
    chunk_decay = np.exp(log_gamma * f32(BLK))
    kdec_t = np.repeat(k_dec.T, RET_QK_DIM, axis=1)
    qdec_t = np.repeat(q_dec.T, RET_QK_DIM, axis=1)
    cd_t = np.broadcast_to(chunk_decay[:, None, None], (RET_HEADS, RET_QK_DIM, RET_V_DIM))
    decay_in = decay_in.reshape(PAIRS, 2 * BLK, BLK)
    cd_t = cd_t.reshape(PAIRS, 2 * RET_QK_DIM, RET_V_DIM)

    key = np.arange(BLK)[:, None]
    query = np.arange(2 * BLK)[None, :] % BLK
    bias = np.stack([np.where(key > query, NEG_INF, 0.0), np.zeros((BLK, 2 * BLK))])
    return tuple(jnp.asarray(np.ascontiguousarray(a), F32) for a in (cos_t, sin_t, decay_in, qdec_t, kdec_t, cd_t, bias))


def _low_lanes(shape):
    lane = lax.broadcasted_iota(jnp.int32, shape, len(shape) - 1)
    return (lane & HALF_LANE) == 0


def _split_heads(t):
    low = _low_lanes(t.shape)
    zero = jnp.zeros_like(t)
    return jnp.where(low, t, zero), jnp.where(low, zero, t)


def _swap_pairs(t):
    lane = lax.broadcasted_iota(jnp.int32, t.shape, 1)
    nxt = pltpu.roll(t, t.shape[1] - 1, 1)
    prv = pltpu.roll(t, 1, 1)
    return jnp.where((lane & 1) == 0, nxt, prv)


def _per_tile(fn, t):
    return jnp.concatenate([fn(_tile(t, i)) for i in range(t.shape[1] // LANE)], axis=1)


def _rotate(t, cos_t, sin_t):
    return _per_tile(lambda a: a * cos_t + _swap_pairs(a) * sin_t, t)


def _rotate_transposed(d, cos_t, sin_t):
    return _per_tile(lambda a: a * cos_t + _swap_pairs(a * sin_t), d)


def _kv_operands(cat):
    low = _low_lanes(cat.shape)
    swapped = pltpu.roll(cat, HALF_LANE, 1)
    zero = jnp.zeros_like(cat)
    pick = lambda a, b: jnp.where(low, a, b).astype(BF16)
    return ((pick(cat, zero), pick(zero, swapped)), (pick(swapped, zero), pick(zero, cat)))


def _stack_tiles(t, first_tile):
    a = t[:, first_tile * LANE:(first_tile + 1) * LANE]
    b = t[:, (first_tile + 1) * LANE:(first_tile + 2) * LANE]
    return jnp.concatenate([a, b], axis=0)


def _sink_rows(sinks_ref, group):
    first = lax.broadcasted_iota(jnp.int32, (1, 2 * BLK), 1) < BLK

    def row(h0, h1):
        return jnp.where(first, sinks_ref[0, group * 4 + h0], sinks_ref[0, group * 4 + h1])
    return row(0, 2), row(1, 3)


ATT_PROBLEMS = tuple((g, hi) for g in range(ATT_KV_HEADS) for hi in range(2))


def _in_previous_block():
    key = lax.broadcasted_iota(jnp.int32, (BLK, 2 * BLK), 0)
    query = lax.broadcasted_iota(jnp.int32, (BLK, 2 * BLK), 1) & (BLK - 1)
    return key > query


def _fold(t, prev):
    return jnp.where(prev, t[0:BLK], t[BLK:])


def _unfold(t, prev):
    zero = jnp.zeros_like(t)
    return jnp.concatenate([jnp.where(prev, t, zero), jnp.where(prev, zero, t)], axis=0)


def _attn_scores(qs, k_ops, bias, prev):
    return [_fold(_dot_nt(k_ops[g][hi], qs[g]), prev) + bias for g, hi in ATT_PROBLEMS]


def _attn_softmax(s, sinks_ref):
    sink = [_sink_rows(sinks_ref, g)[hi] for g, hi in ATT_PROBLEMS]
    m = [jnp.maximum(jnp.max(si, axis=0, keepdims=True), ki) for si, ki in zip(s, sink)]
    e = [jnp.exp(si - mi) for si, mi in zip(s, m)]
    es = [jnp.exp(ki - mi) for ki, mi in zip(sink, m)]
    inv = [1.0 / (jnp.sum(ei, axis=0, keepdims=True) + esi) for ei, esi in zip(e, es)]
    return [ei * ii for ei, ii in zip(e, inv)], [esi * ii for esi, ii in zip(es, inv)]


def _group_norm_all(outs):
    mu = [jnp.mean(o, axis=1, keepdims=True) for o in outs]
    xc = [o - m for o, m in zip(outs, mu)]
    var = [jnp.mean(c * c, axis=1, keepdims=True) for c in xc]
    rstd = [lax.rsqrt(v + GN_EPS) for v in var]
    return [c * r for c, r in zip(xc, rstd)], rstd


def _proj_cols(p_ref, rows, start, stop):
    pieces = []
    while start < stop:
        k = max(j for j in range(N_CHIPS) if WIN_START[j] <= start)
        end = min(stop, WIN_START[k] + SHARD_PAD)
        pieces.append(p_ref[k, rows, start - WIN_START[k]:end - WIN_START[k]])
        start = end
    return pieces[0] if len(pieces) == 1 else jnp.concatenate(pieces, axis=1)


def _retention_operands(p_ref, rows, cos_b, sin_b, qdec, kdec):
    qr = _rotate(_proj_cols(p_ref, rows, OFF_RQ, OFF_RQ + RET_QK_WIDTH).astype(F32), cos_b, sin_b)
    kr = _rotate(_proj_cols(p_ref, rows, OFF_RK, OFF_RK + RET_QK_WIDTH).astype(F32), cos_b, sin_b) * RET_SCALE
    return qr, kr, qr * qdec, kr * kdec


def _tile(t, i):
    return t[:, i * LANE:(i + 1) * LANE]


GATHER_SEMS = 7


def _allgather_steps(src_ref, full_ref, blk_ref, send_sems, recv_sems):
    block_rows = blk_ref.shape[0]
    assert block_rows % BF16_ROWS == 0 and full_ref.shape[0] == N_DEV * block_rows
    x, y, c = lax.axis_index("x"), lax.axis_index("y"), lax.axis_index("c")
    me, sibling = (x, y, c), (x, y, 1 - c)
    chips = [(1 - x, y), (x, 1 - y), (1 - x, 1 - y)]

    def rows(px, py, pc):
        return full_ref.at[pl.ds(pl.multiple_of((4 * px + 2 * py + pc) * block_rows, BF16_ROWS), block_rows), :]

    def copy(k, block, to, src=None):
        return pltpu.make_async_remote_copy(
            src_ref=rows(*block) if src is None else src, dst_ref=rows(*block),
            send_sem=send_sems.at[k], recv_sem=recv_sems.at[k], device_id=to, device_id_type=MESH_ID)

    first = [copy(0, me, sibling, src=blk_ref)] + [copy(1 + j, me, (*chip, c), src=blk_ref) for j, chip in enumerate(chips)]
    passed = [copy(4 + j, (*chip, c), sibling) for j, chip in enumerate(chips)]

    def start():
        blk_ref[...] = src_ref[pl.ds(pl.multiple_of(c * block_rows, BF16_ROWS), block_rows), :].astype(BF16)
        rows(*me)[...] = blk_ref[...]
        for cp in first:
            cp.start()

    def forward():
        for j, chip in enumerate(chips):
            copy(1 + j, (*chip, c), me).wait_recv()
            passed[j].start()

    def finish():
        copy(0, sibling, me).wait_recv()
        for j, chip in enumerate(chips):
            copy(4 + j, (*chip, 1 - c), me).wait_recv()
        for cp in first + passed:
            cp.wait_send()

    return start, forward, finish


WINDOW_HALF = SHARD_PAD // 2
WINDOW_SEMS = GATHER_SEMS + 1


def _window_gather(wi_ref, w4_ref, blk_ref, edge_ref, send_sems, recv_sems):
    x, y, c = lax.axis_index("x"), lax.axis_index("y"), lax.axis_index("c")
    me, sibling, pair = (x, y, c), (x, y, 1 - c), (x, 1 - y, c)
    chips = [(1 - x, y), (x, 1 - y), (1 - x, 1 - y)]
    own_rows, edge_rows = SHARD_IN - WINDOW_HALF, SHARD_SHIFT
    sends_edge = y == c

    def rows(px, py, pc):
        return w4_ref.at[pl.ds(pl.multiple_of((4 * px + 2 * py + pc) * WINDOW_HALF, BF16_ROWS), WINDOW_HALF), :]

    def copy(k, block, to, src=None):
        return pltpu.make_async_remote_copy(
            src_ref=rows(*block) if src is None else src, dst_ref=rows(*block),
            send_sem=send_sems.at[k], recv_sem=recv_sems.at[k], device_id=to, device_id_type=MESH_ID)

    edge_copy = pltpu.make_async_remote_copy(
        src_ref=edge_ref, dst_ref=edge_ref, send_sem=send_sems.at[GATHER_SEMS], recv_sem=recv_sems.at[GATHER_SEMS],
        device_id=pair, device_id_type=MESH_ID)
    first = [copy(0, me, sibling, src=blk_ref)] + [copy(1 + j, me, (*chip, c), src=blk_ref) for j, chip in enumerate(chips)]
    passed = [copy(4 + j, (*chip, c), sibling) for j, chip in enumerate(chips)]

    def at(start, size):
        return pl.ds(pl.multiple_of(start, BF16_ROWS), size)

    def start():
        @pl.when(sends_edge)
        def _():
            edge_ref[...] = wi_ref[at((1 - y) * (SHARD_IN - edge_rows), edge_rows), :].astype(BF16)
            edge_copy.start()
            blk_ref[...] = wi_ref[at(c * own_rows, WINDOW_HALF), :].astype(BF16)

        @pl.when(jnp.logical_not(sends_edge))
        def _():
            blk_ref[at(y * edge_rows, own_rows), :] = wi_ref[at(c * WINDOW_HALF, own_rows), :].astype(BF16)
            edge_copy.wait_recv()
            blk_ref[at((1 - y) * own_rows, edge_rows), :] = edge_ref[...]

        rows(*me)[...] = blk_ref[...]
        for cp in first[:-1]:
            cp.start()
        copy(0, sibling, me).wait_recv()

    def send_far():
        first[-1].start()

    def forward(j):
        copy(1 + j, (*chips[j], c), me).wait_recv()
        passed[j].start()

    def wait_forwarded(j):
        copy(4 + j, (*chips[j], 1 - c), me).wait_recv()

    def finish():
        for cp in first + passed:
            cp.wait_send()

        @pl.when(sends_edge)
        def _():
            edge_copy.wait_send()

    return start, send_far, forward, wait_forwarded, finish


def _in_proj(x, norm_g, w_in_t_shard, w_out_shard, tb):
    seq = x.shape[0]
    nblk = seq // tb
    last = nblk - 1
    chip_of_panel = (None, 1, 0, 2)

    def body(win_ref, x_ref, g_ref, wi_ref, wo_ref, p_ref, ht_ref, wt_ref, wout_ref,
             w4, blk, edge, hbuf, wout_full, blko, send_sems, recv_sems, send_sems_o, recv_sems_o):
        q, i = pl.program_id(0), pl.program_id(1)
        chip = 2 * lax.axis_index("x") + lax.axis_index("y")
        in_start, in_send_far, in_forward, in_wait_forwarded, in_finish = _window_gather(
            wi_ref, w4, blk, edge, send_sems, recv_sems)
        out_start, out_forward, out_finish = _allgather_steps(wo_ref, wout_full, blko, send_sems_o, recv_sems_o)
        rows = pl.ds(pl.multiple_of(i * tb, tb), tb)

        @pl.when(jnp.logical_and(q == 0, i == 0))
        def _():
            in_start()

        @pl.when(jnp.logical_and(q == 0, i == min(last // 2 + 1, last)))
        def _():
            in_send_far()

        @pl.when(jnp.logical_and(q == 1, i == 0))
        def _():
            out_start()

        for panel in range(1, N_CHIPS):
            @pl.when(jnp.logical_and(q == panel, i == 0))
            def _():
                in_wait_forwarded(chip_of_panel[panel])

        @pl.when(q == 0)
        def _():
            xv = x_ref[...]
            r = lax.rsqrt(jnp.mean(xv * xv, axis=1, keepdims=True) + RMS_EPS)
            h = (xv * r) * g_ref[...]
            hbuf[rows, :] = h.astype(BF16)
            ht_ref[...] = h.T.astype(BF16)

        owner = jnp.bitwise_xor(chip, q)
        window = w4[pl.ds(pl.multiple_of(owner * SHARD_PAD, SHARD_PAD), SHARD_PAD), :]
        p_ref[...] = _dot_nt(hbuf[rows, :], window).astype(BF16)

        for panel in range(1, N_CHIPS):
            @pl.when(jnp.logical_and(q == panel - 1, i == last))
            def _():
                in_forward(chip_of_panel[panel])

        @pl.when(jnp.logical_and(q == N_CHIPS - 1, i == 0))
        def _():
            out_forward()

        @pl.when(jnp.logical_and(q == N_CHIPS - 1, i == last))
        def _():
            in_finish()
            for k in range(N_CHIPS):
                first = k * SHARD_PAD + (k % 2) * SHARD_SHIFT
                wt_ref[k * SHARD_IN:(k + 1) * SHARD_IN, :] = w4[first:first + SHARD_IN, :]
            out_finish()
            wout_ref[...] = wout_full[...]

    whole = lambda shape: pl.BlockSpec(shape, lambda q, i, win: (0,) * len(shape), pipeline_mode=pl.Buffered(1))
    first_panel = lambda q, i: jnp.where(q == 0, i, last)
    grid_spec = pltpu.PrefetchScalarGridSpec(
        num_scalar_prefetch=1, grid=(N_CHIPS, nblk),
        in_specs=[pl.BlockSpec((tb, D_MODEL), lambda q, i, win: (first_panel(q, i), 0)), whole((1, D_MODEL)),
                  whole((SHARD_IN, D_MODEL)), whole((SHARD_OUT, D_MODEL))],
        out_specs=(pl.BlockSpec((None, tb, SHARD_PAD), lambda q, i, win: (win[q], i, 0)),
                   pl.BlockSpec((D_MODEL, tb), lambda q, i, win: (0, first_panel(q, i))),
                   whole((IN_WIDTH, D_MODEL)), whole((MIX_WIDTH, D_MODEL))),
        scratch_shapes=[
            pltpu.VMEM((N_CHIPS * SHARD_PAD, D_MODEL), BF16), pltpu.VMEM((WINDOW_HALF, D_MODEL), BF16),
            pltpu.VMEM((SHARD_SHIFT, D_MODEL), BF16), pltpu.VMEM((seq, D_MODEL), BF16),
            pltpu.VMEM((MIX_WIDTH, D_MODEL), BF16), pltpu.VMEM((SHARD_OUT // 2, D_MODEL), BF16),
            pltpu.SemaphoreType.DMA((WINDOW_SEMS,)), pltpu.SemaphoreType.DMA((WINDOW_SEMS,)),
            pltpu.SemaphoreType.DMA((GATHER_SEMS,)), pltpu.SemaphoreType.DMA((GATHER_SEMS,)),
        ])
    chip = 2 * lax.axis_index("x") + lax.axis_index("y")
    owner = jnp.bitwise_xor(chip, jnp.arange(N_CHIPS, dtype=jnp.int32))
    return _call(
        body, name="in_proj", grid_spec=grid_spec,
        out_shape=(jax.ShapeDtypeStruct((N_CHIPS, seq, SHARD_PAD), BF16), jax.ShapeDtypeStruct((D_MODEL, seq), BF16),
                   jax.ShapeDtypeStruct((IN_WIDTH, D_MODEL), BF16), jax.ShapeDtypeStruct((MIX_WIDTH, D_MODEL), BF16)),
        compiler_params=_params(60, ("arbitrary", "arbitrary")),
    )(owner.astype(jnp.int32), x, norm_g, w_in_t_shard, w_out_shard)


def _mix_fwd(proj, x, target, w_out, final_g, gn_gain, sinks, tables, tb):
    seq = x.shape[0]
    nsub = tb // BLK
    cos_t, sin_t, decay_in, qdec_t, kdec_t, cd_t, bias_t = tables

    def body(p_ref, x_ref, t_ref, cos_ref, sin_ref, wout_ref, fg_ref, gain_ref, sinks_ref, din_ref, qdec_ref,
             kdec_ref, cd_ref, bias_ref, mix_ref, dxo_ref, st_ref, loss_ref, gfin_ref, oatt_ref, probs_ref, psink_ref,
             kprev_ref, vprev_ref, state_ref):
        i = pl.program_id(0)

        @pl.when(i == 0)
        def _():
            kprev_ref[...] = jnp.zeros_like(kprev_ref)
            vprev_ref[...] = jnp.zeros_like(vprev_ref)
            state_ref[...] = jnp.zeros_like(state_ref)
            loss_ref[...] = jnp.zeros_like(loss_ref)
            gfin_ref[...] = jnp.zeros_like(gfin_ref)

        prev = _in_previous_block()

        def sub(j, carry):
            kp, vp, states = carry
            rows = pl.ds(pl.multiple_of(j * BLK, BLK), BLK)
            bias = bias_ref[jnp.where(jnp.logical_or(i > 0, j > 0), 1, 0)]

            aq = _proj_cols(p_ref, rows, OFF_AQ, OFF_AQ + ATT_WIDTH)
            ak = _proj_cols(p_ref, rows, OFF_AK, OFF_AK + ATT_KV_WIDTH).astype(F32)
            av = _proj_cols(p_ref, rows, OFF_AV, OFF_AV + ATT_KV_WIDTH).astype(F32)
            az = _proj_cols(p_ref, rows, OFF_AZ, OFF_AZ + ATT_WIDTH).astype(F32)
            k_ops = _kv_operands(jnp.concatenate([kp, ak], axis=0))
            v_ops = _kv_operands(jnp.concatenate([vp, av], axis=0))
            qs = [(_stack_tiles(aq, 2 * g) * ATT_SCALE).astype(BF16) for g in range(ATT_KV_HEADS)]

            qr, kr, qd, kd = _retention_operands(p_ref, rows, cos_ref[rows, :], sin_ref[rows, :],
                                                 qdec_ref[...], kdec_ref[...])
            s = _attn_scores(qs, k_ops, bias, prev)
            heads = [(t, hh) for t in range(PAIRS) for hh in range(2)]
            sc = [_dot_nt(jnp.concatenate(_split_heads(_tile(qr, t)), axis=0).astype(BF16), _tile(kr, t).astype(BF16))
                  * din_ref[t] for t in range(PAIRS)]
            qd_heads = [_split_heads(_tile(qd, t)) for t in range(PAIRS)]
            state_b = [states[t].astype(BF16) for t in range(PAIRS)]
            vs = [_proj_cols(p_ref, rows, OFF_RV + h * RET_V_DIM, OFF_RV + (h + 1) * RET_V_DIM) for h in range(RET_HEADS)]
            rzs = [_proj_cols(p_ref, rows, OFF_RZ + h * RET_V_DIM, OFF_RZ + (h + 1) * RET_V_DIM).astype(F32)
                   for h in range(RET_HEADS)]
            lhs = [jnp.concatenate([sc[t][hh * BLK:(hh + 1) * BLK].astype(BF16), qd_heads[t][hh].astype(BF16)], axis=1)
                   for t, hh in heads]
            p, p_sink = _attn_softmax(s, sinks_ref)
            o_ret = [_dot(lhs[2 * t + hh], jnp.concatenate([vs[2 * t + hh], state_b[t]], axis=0)) for t, hh in heads]
            p = [pi.astype(BF16) for pi in p]
            o_tiles = []
            for g in range(ATT_KV_HEADS):
                p_cat = jnp.concatenate([_unfold(p[2 * g], prev), _unfold(p[2 * g + 1], prev)], axis=0)
                o = _dot_tn(p_cat, jnp.concatenate(v_ops[g], axis=0))
                o_tiles += [o[0:BLK], o[BLK:]]
            ons, _ = _group_norm_all(o_ret)
            new_states = [states[t] * cd_ref[t]
                          + _dot_tn(jnp.concatenate(_split_heads(_tile(kd, t)), axis=0).astype(BF16),
                                    jnp.concatenate([vs[2 * t], vs[2 * t + 1]], axis=0)) for t in range(PAIRS)]
            o_att = jnp.concatenate(o_tiles, axis=1)
            out = [o_att * (az * _sigmoid(az))]
            out += [(ons[h] * gain_ref[:, h * RET_V_DIM:(h + 1) * RET_V_DIM]) * (rzs[h] * _sigmoid(rzs[h]))
                    for h in range(RET_HEADS)]
            mix_ref[rows, :] = jnp.concatenate(out, axis=1).astype(BF16)
            oatt_ref[rows, :] = o_att
            probs_ref[j] = jnp.stack(p)
            psink_ref[j] = jnp.concatenate(p_sink, axis=0)
            st_ref[j] = jnp.stack(states)
            return ak, av, tuple(new_states)

        carry = (kprev_ref[...], vprev_ref[...], tuple(state_ref[t] for t in range(PAIRS)))
        for jb in range(nsub):
            carry = sub(jnp.int32(jb), carry)
        kp, vp, states = carry
        kprev_ref[...] = kp
        vprev_ref[...] = vp
        state_ref[...] = jnp.stack(states)

        xo = x_ref[...] + _dot(mix_ref[...], wout_ref[...])
        r2 = lax.rsqrt(jnp.mean(xo * xo, axis=1, keepdims=True) + RMS_EPS)
        xn = xo * r2
        err = xn * fg_ref[...] - t_ref[...]
        loss_ref[...] += jnp.sum(err * err) * (0.5 / D_MODEL)
        dy = err * (1.0 / D_MODEL)
        gfin_ref[...] += jnp.sum(dy * xn, axis=0, keepdims=True)
        u = dy * fg_ref[...]
        dxo_ref[...] = r2 * u - xn * (r2 * jnp.mean(u * xn, axis=1, keepdims=True))

    blk_rows = lambda w: pl.BlockSpec((tb, w), lambda i: (i, 0))
    state_shape = (PAIRS, 2 * RET_QK_DIM, RET_V_DIM)
    return _call(
        body, name="mix_fwd", grid=(seq // tb,),
        out_shape=(
            jax.ShapeDtypeStruct((seq, MIX_WIDTH), BF16),
            jax.ShapeDtypeStruct((seq, D_MODEL), F32),
            jax.ShapeDtypeStruct((seq // BLK,) + state_shape, F32),
            jax.ShapeDtypeStruct((8, LANE), F32),
            jax.ShapeDtypeStruct((1, D_MODEL), F32),
            jax.ShapeDtypeStruct((seq, ATT_WIDTH), F32),
            jax.ShapeDtypeStruct((seq // BLK, len(ATT_PROBLEMS), BLK, 2 * BLK), BF16),
            jax.ShapeDtypeStruct((seq // BLK, len(ATT_PROBLEMS), 2 * BLK), F32),
        ),
        in_specs=[
            pl.BlockSpec((N_CHIPS, tb, SHARD_PAD), lambda i: (0, i, 0)),
            blk_rows(D_MODEL), blk_rows(D_MODEL), blk_rows(LANE), blk_rows(LANE),
            _const_spec((MIX_WIDTH, D_MODEL)), _const_spec((1, D_MODEL)), _const_spec((1, RET_WIDTH)),
            pl.BlockSpec(memory_space=pltpu.SMEM),
            _const_spec((PAIRS, 2 * BLK, BLK)), _const_spec((BLK, RET_QK_WIDTH)), _const_spec((BLK, RET_QK_WIDTH)),
            _const_spec(state_shape), _const_spec((2, BLK, 2 * BLK)),
        ],
        out_specs=(
            blk_rows(MIX_WIDTH), blk_rows(D_MODEL),
            pl.BlockSpec((nsub,) + state_shape, lambda i: (i, 0, 0, 0)),
            _const_spec((8, LANE)), _const_spec((1, D_MODEL)), blk_rows(ATT_WIDTH),
            pl.BlockSpec((nsub, len(ATT_PROBLEMS), BLK, 2 * BLK), lambda i: (i, 0, 0, 0)),
            pl.BlockSpec((nsub, len(ATT_PROBLEMS), 2 * BLK), lambda i: (i, 0, 0)),
        ),
        scratch_shapes=[
            pltpu.VMEM((BLK, ATT_KV_WIDTH), F32), pltpu.VMEM((BLK, ATT_KV_WIDTH), F32),
            pltpu.VMEM(state_shape, F32),
        ],
        compiler_params=_params(48, ("arbitrary",)),
    )(proj, x, target, cos_t, sin_t, w_out, final_g, gn_gain, sinks, decay_in, qdec_t, kdec_t, cd_t, bias_t)


def _mix_bwd(proj, dxo, mix, o_att, probs, p_sinks, states, x, w_out, w_in_t, norm_g, gn_gain, tables, tb):
    seq = dxo.shape[0]
    nsub = tb // BLK
    nblk = seq // tb
    cos_t, sin_t, decay_in, qdec_t, kdec_t, cd_t, _ = tables
    kv_cols = OFF_AK // (2 * ATT_KV_WIDTH)
    state_shape = (PAIRS, 2 * RET_QK_DIM, RET_V_DIM)

    def body(p_ref, pkv_ref, dxo_ref, mix_ref, oatt_ref, probs_ref, psink_ref, st_ref, cos_ref, sin_ref, x_ref, wout_ref,
             win_ref, g_ref, gain_ref, din_ref, qdec_ref, kdec_ref, cd_ref,
             dp_ref, gx_ref, gwout_ref, gnorm_ref, dgain_ref, dsink_ref,
             dmix_ref, kv_ref, dkc_ref, dvc_ref, gst_ref):
        i = pl.program_id(0)

        @pl.when(i == 0)
        def _():
            gwout_ref[...] = jnp.zeros_like(gwout_ref)
            gnorm_ref[...] = jnp.zeros_like(gnorm_ref)
            dgain_ref[...] = jnp.zeros_like(dgain_ref)
            dsink_ref[...] = jnp.zeros_like(dsink_ref)
            dkc_ref[...] = jnp.zeros_like(dkc_ref)
            dvc_ref[...] = jnp.zeros_like(dvc_ref)
            gst_ref[...] = jnp.zeros_like(gst_ref)

        dxo_b = dxo_ref[...].astype(BF16)
        dmix_ref[...] = _dot_nt(dxo_b, wout_ref[...])
        gwout_ref[...] += _dot_tn(mix_ref[...], dxo_b)
        kv_ref[0:BLK, :] = pkv_ref[...].astype(F32)
        kv_ref[BLK:, :] = _proj_cols(p_ref, slice(None), OFF_AK, OFF_AK + 2 * ATT_KV_WIDTH).astype(F32)
        low = _low_lanes((BLK, LANE))
        low2 = _low_lanes((2 * BLK, LANE))
        lane = lax.broadcasted_iota(jnp.int32, (1, LANE), 1)
        prev = _in_previous_block()

        def sub(jj, carry):
            dkc, dvc, gsts, dgain, dsink = carry
            j = nsub - 1 - jj
            rows = pl.ds(pl.multiple_of(j * BLK, BLK), BLK)
            both = pl.ds(pl.multiple_of(j * BLK, BLK), 2 * BLK)


            aq = _proj_cols(p_ref, rows, OFF_AQ, OFF_AQ + ATT_WIDTH)
            az = _proj_cols(p_ref, rows, OFF_AZ, OFF_AZ + ATT_WIDTH).astype(F32)
            k_ops = _kv_operands(kv_ref[both, 0:ATT_KV_WIDTH])
            v_ops = _kv_operands(kv_ref[both, ATT_KV_WIDTH:2 * ATT_KV_WIDTH])
            da = dmix_ref[rows, 0:ATT_WIDTH]
            sig = _sigmoid(az)
            d_o = da * (az * sig)
            qs = [(_stack_tiles(aq, 2 * g) * ATT_SCALE).astype(BF16) for g in range(ATT_KV_HEADS)]
            dos = [_stack_tiles(d_o, 2 * g).astype(BF16) for g in range(ATT_KV_HEADS)]
            p_b = [probs_ref[j, k] for k in range(len(ATT_PROBLEMS))]
            p = [pk.astype(F32) for pk in p_b]
            sink_all = psink_ref[j]
            p_sink = [sink_all[k:k + 1, :] for k in range(len(ATT_PROBLEMS))]
            cos_b, sin_b = cos_ref[rows, :], sin_ref[rows, :]
            qdec, kdec = qdec_ref[...], kdec_ref[...]
            qr, kr, qd, kd = _retention_operands(p_ref, rows, cos_b, sin_b, qdec, kdec)
            heads = [(t, hh) for t in range(PAIRS) for hh in range(2)]
            head_cols = [slice(h * RET_V_DIM, (h + 1) * RET_V_DIM) for h in range(RET_HEADS)]
            q_rows = [jnp.concatenate(_split_heads(_tile(qr, t)), axis=0).astype(BF16) for t in range(PAIRS)]
            k_rows = [jnp.concatenate(_split_heads(_tile(kr, t)), axis=0).astype(BF16) for t in range(PAIRS)]
            din = [din_ref[t] for t in range(PAIRS)]

            dpr = [_fold(_dot_nt(v_ops[g][hi], dos[g]), prev) for g, hi in ATT_PROBLEMS]
            sc = [(_dot_nt(q_rows[t], _tile(kr, t).astype(BF16)) * din[t]).astype(BF16) for t in range(PAIRS)]
            qd_heads = [_split_heads(_tile(qd, t)) for t in range(PAIRS)]
            kd_heads = [_split_heads(_tile(kd, t)) for t in range(PAIRS)]
            state_b = [st_ref[j, t].astype(BF16) for t in range(PAIRS)]
            gst_b = [gsts[t].astype(BF16) for t in range(PAIRS)]
            vs = [_proj_cols(p_ref, rows, OFF_RV + h * RET_V_DIM, OFF_RV + (h + 1) * RET_V_DIM) for h in range(RET_HEADS)]
            rzs = [_proj_cols(p_ref, rows, OFF_RZ + h * RET_V_DIM, OFF_RZ + (h + 1) * RET_V_DIM).astype(F32)
                   for h in range(RET_HEADS)]
            drs = [dmix_ref[rows, ATT_WIDTH + h * RET_V_DIM:ATT_WIDTH + (h + 1) * RET_V_DIM] for h in range(RET_HEADS)]
            gains = [gain_ref[:, c] for c in head_cols]
            lhs = [jnp.concatenate([sc[t][hh * BLK:(hh + 1) * BLK], qd_heads[t][hh].astype(BF16)], axis=1) for t, hh in heads]
            rhs = [jnp.concatenate([vs[2 * t + hh], state_b[t]], axis=0) for t, hh in heads]

            delta = [jnp.sum(pi * di, axis=0, keepdims=True) for pi, di in zip(p, dpr)]
            ds = [_unfold((pi * (di - ti)).astype(BF16), prev) for pi, di, ti in zip(p, dpr, delta)]
            o_ret = [_dot(l, r) for l, r in zip(lhs, rhs)]
            for (g, hi), ki, ti in zip(ATT_PROBLEMS, p_sink, delta):
                sink_part = ki * ti
                for half in range(2):
                    tot = jnp.sum(sink_part[:, half * BLK:(half + 1) * BLK], axis=1, keepdims=True)
                    dsink = dsink - jnp.where(lane == 4 * g + 2 * half + hi, tot, 0.0)

            dq_tiles, dk_sums, dv_sums = [], [], []
            for g in range(ATT_KV_HEADS):
                ds_cat = jnp.concatenate([ds[2 * g], ds[2 * g + 1]], axis=0)
                p_cat = jnp.concatenate([_unfold(p_b[2 * g], prev), _unfold(p_b[2 * g + 1], prev)], axis=0)
                dqs = _dot_tn(ds_cat, jnp.concatenate(k_ops[g], axis=0)) * ATT_SCALE
                dq_tiles += [dqs[0:BLK], dqs[BLK:]]
                dk_sums.append(_dot(ds_cat, qs[g]))
                dv_sums.append(_dot(p_cat, dos[g]))
            ons, rstds = _group_norm_all(o_ret)
            sig_r = [_sigmoid(z) for z in rzs]
            dgn = [d * (z * g) for d, z, g in zip(drs, rzs, sig_r)]
            dz_parts = [d * (o * gn) * (g * (1.0 + z * (1.0 - g))) for d, o, gn, g, z in zip(drs, ons, gains, sig_r, rzs)]
            dgain_parts = [jnp.sum(d * o, axis=0, keepdims=True) for d, o in zip(dgn, ons)]
            don = [d * gn for d, gn in zip(dgn, gains)]
            mean_don = [jnp.mean(d, axis=1, keepdims=True) for d in don]
            mean_don_on = [jnp.mean(d * o, axis=1, keepdims=True) for d, o in zip(don, ons)]
            dob = [(r * (d - a - o * b)).astype(BF16) for r, d, a, o, b in zip(rstds, don, mean_don, ons, mean_don_on)]

            dlhs = [_dot_nt(d, r) for d, r in zip(dob, rhs)]
            drhs = [_dot_tn(l, d) for l, d in zip(lhs, dob)]
            dkds = [_dot_nt(vs[2 * t + hh], gst_b[t]) for t, hh in heads]
            dv_parts = [drhs[2 * t + hh][0:BLK] + _dot(kd_heads[t][hh].astype(BF16), gst_b[t]) for t, hh in heads]
            daz = da * oatt_ref[rows, :] * (sig * (1.0 + az * (1.0 - sig)))

            def kv_grad(sums):
                (a0, b0), (a1, b1) = [(s[0:2 * BLK], s[2 * BLK:]) for s in sums]
                return jnp.where(low2, a0, b1) + pltpu.roll(jnp.where(low2, a1, b0), HALF_LANE, 1)

            dk_both, dv_both = kv_grad(dk_sums), kv_grad(dv_sums)
            dak = dk_both[BLK:] + dkc
            dav = dv_both[BLK:] + dvc
            das = [(dlhs[2 * t + hh][:, 0:BLK] * din[t][hh * BLK:(hh + 1) * BLK]).astype(BF16) for t, hh in heads]
            new_gsts = [gsts[t] * cd_ref[t] + drhs[2 * t][BLK:] + drhs[2 * t + 1][BLK:] for t in range(PAIRS)]
            dq_parts = [_dot(jnp.concatenate([das[2 * t], das[2 * t + 1]], axis=1), k_rows[t])
                        + jnp.where(low, dlhs[2 * t][:, BLK:], dlhs[2 * t + 1][:, BLK:]) * _tile(qdec, t)
                        for t in range(PAIRS)]
            dk_parts = [_dot_tn(jnp.concatenate([das[2 * t], das[2 * t + 1]], axis=0), q_rows[t])
                        + jnp.where(low, dkds[2 * t], dkds[2 * t + 1]) * _tile(kdec, t) for t in range(PAIRS)]
            drq = _rotate_transposed(jnp.concatenate(dq_parts, axis=1), cos_b, sin_b)
            drk = _rotate_transposed(jnp.concatenate(dk_parts, axis=1) * RET_SCALE, cos_b, sin_b)

            dp_ref[rows, :] = jnp.concatenate(
                [jnp.concatenate(dq_tiles, axis=1), dak, dav, daz, drq, drk] + dv_parts + dz_parts, axis=1).astype(BF16)
            dgain = dgain + jnp.concatenate(dgain_parts, axis=1)
            return dk_both[0:BLK], dv_both[0:BLK], tuple(new_gsts), dgain, dsink

        carry = (dkc_ref[...], dvc_ref[...], tuple(gst_ref[t] for t in range(PAIRS)), dgain_ref[...], dsink_ref[...])
        for jb in range(nsub):
            carry = sub(jnp.int32(jb), carry)
        dkc, dvc, gsts, dgain, dsink = carry
        dkc_ref[...] = dkc
        dvc_ref[...] = dvc
        gst_ref[...] = jnp.stack(gsts)
        dgain_ref[...] = dgain
        dsink_ref[...] = dsink

        dh = _dot(dp_ref[...], win_ref[...])
        xv = x_ref[...]
        r = lax.rsqrt(jnp.mean(xv * xv, axis=1, keepdims=True) + RMS_EPS)
        xn = xv * r
        gnorm_ref[...] += jnp.sum(dh * xn, axis=0, keepdims=True)
        u = dh * g_ref[...]
        gx_ref[...] = dxo_ref[...] + r * u - xn * (r * jnp.mean(u * xn, axis=1, keepdims=True))

    rev_rows = lambda w: pl.BlockSpec((tb, w), lambda i: (nblk - 1 - i, 0))
    prev_kv = pl.BlockSpec((None, BLK, 2 * ATT_KV_WIDTH),
                           lambda i: (0, jnp.maximum((nblk - 1 - i) * nsub - 1, 0), kv_cols))
    return _call(
        body, name="mix_bwd", grid=(nblk,),
        out_shape=(
            jax.ShapeDtypeStruct((seq, IN_WIDTH), BF16),
            jax.ShapeDtypeStruct((seq, D_MODEL), F32),
            jax.ShapeDtypeStruct((MIX_WIDTH, D_MODEL), F32),
            jax.ShapeDtypeStruct((1, D_MODEL), F32),
            jax.ShapeDtypeStruct((1, RET_WIDTH), F32),
            jax.ShapeDtypeStruct((1, LANE), F32),
        ),
        in_specs=[
            pl.BlockSpec((N_CHIPS, tb, SHARD_PAD), lambda i: (0, nblk - 1 - i, 0)),
            prev_kv, rev_rows(D_MODEL), rev_rows(MIX_WIDTH), rev_rows(ATT_WIDTH),
            pl.BlockSpec((nsub, len(ATT_PROBLEMS), BLK, 2 * BLK), lambda i: (nblk - 1 - i, 0, 0, 0)),
            pl.BlockSpec((nsub, len(ATT_PROBLEMS), 2 * BLK), lambda i: (nblk - 1 - i, 0, 0)),
            pl.BlockSpec((nsub,) + state_shape, lambda i: (nblk - 1 - i, 0, 0, 0)),
            rev_rows(LANE), rev_rows(LANE), rev_rows(D_MODEL),
            _const_spec((MIX_WIDTH, D_MODEL)), _const_spec((IN_WIDTH, D_MODEL)), _const_spec((1, D_MODEL)),
            _const_spec((1, RET_WIDTH)),
            _const_spec((PAIRS, 2 * BLK, BLK)), _const_spec((BLK, RET_QK_WIDTH)), _const_spec((BLK, RET_QK_WIDTH)),
            _const_spec(state_shape),
        ],
        out_specs=(
            rev_rows(IN_WIDTH), rev_rows(D_MODEL), _const_spec((MIX_WIDTH, D_MODEL)), _const_spec((1, D_MODEL)),
            _const_spec((1, RET_WIDTH)), _const_spec((1, LANE)),
        ),
        scratch_shapes=[
            pltpu.VMEM((tb, MIX_WIDTH), F32),
            pltpu.VMEM((tb + BLK, 2 * ATT_KV_WIDTH), F32),
            pltpu.VMEM((BLK, ATT_KV_WIDTH), F32), pltpu.VMEM((BLK, ATT_KV_WIDTH), F32),
            pltpu.VMEM(state_shape, F32),
        ],
        compiler_params=_params(60, ("arbitrary",)),
    )(proj, proj, dxo, mix, o_att, probs, p_sinks, states, cos_t, sin_t, x, w_out, w_in_t, norm_g, gn_gain, decay_in,
      qdec_t, kdec_t, cd_t)


def _gw_in_reduce(h_t, dproj, gw_out, small, tb):
    seq = dproj.shape[0]
    nblk = seq // tb
    last = nblk - 1
    hand_on = min(1, last)
    half = D_MODEL // 2
    A, B, C, N_SEMS = 0, N_CHIPS, 2 * N_CHIPS, 2 * N_CHIPS + 1

    def body(win_ref, ht_ref, dp_ref, gwo_hbm, s0_ref, s1_ref, s2_ref, s3_ref, s4_ref, out_ref, fout_ref, packsum_ref,
             acc, sib, send_buf, b_in, fin, fout, mine_out, sib_out, send_out, b_out, pack_ref, packs,
             send_sems, recv_sems, local_sems):
        p, i = pl.program_id(0), pl.program_id(1)
        small_start, small_hand_on, small_finish = _small_exchange(
            gwo_hbm, (s0_ref, s1_ref, s2_ref, s3_ref, s4_ref), fout, mine_out, sib_out, send_out, b_out, pack_ref,
            packs, send_sems, recv_sems, local_sems, N_SEMS)

        @pl.when(jnp.logical_and(p == 0, i == 0))
        def _():
            small_start()

        @pl.when(jnp.logical_and(p == 0, i == hand_on))
        def _():
            small_hand_on()

        x, y, c = lax.axis_index("x"), lax.axis_index("y"), lax.axis_index("c")
        chip = 2 * x + y
        sibling = (x, y, 1 - c)
        mine = pl.ds(pl.multiple_of(c * half, half), half)
        other = pl.ds(pl.multiple_of((1 - c) * half, half), half)

        def remote(src, dst, send_k, recv_k, to):
            return pltpu.make_async_remote_copy(src_ref=src, dst_ref=dst, send_sem=send_sems.at[send_k],
                                                recv_sem=recv_sems.at[recv_k], device_id=to, device_id_type=MESH_ID)

        slot = p % 2

        @pl.when(i == 0)
        def _():
            acc[slot] = jnp.zeros(acc.shape[1:], F32)

        acc[slot] += _dot(ht_ref[...], dp_ref[...])

        for q in range(N_CHIPS):
            s = q % 2
            to_sibling = remote(acc.at[s, other, :], sib.at[s], A + q, A + q, sibling)

            @pl.when(jnp.logical_and(p == q, i == last))
            def _():
                to_sibling.start()

            if q < N_CHIPS - 1:
                dest = (chip + 1 + q) % N_CHIPS

                @pl.when(jnp.logical_and(p == q + 1, i == hand_on))
                def _():
                    to_sibling.wait_recv()
                    send_buf[q] = (acc[s, mine, :] + sib[s]).astype(BF16)
                    remote(send_buf.at[q], b_in.at[chip], B + q, B + chip, (dest // 2, dest % 2, c)).start()
                    to_sibling.wait_send()
            else:
                @pl.when(jnp.logical_and(p == q, i == last))
                def _():
                    to_sibling.wait_recv()
                    fin[mine, :] = acc[s, mine, :] + sib[s]
                    for j in range(N_CHIPS):
                        @pl.when(j != chip)
                        def _():
                            remote(b_in.at[j], b_in.at[j], B + j, B + j, sibling).wait_recv()
                            fin[mine, :] += b_in[j].astype(F32)
                    to_core = remote(fin.at[mine, :], fin.at[mine, :], C, C, sibling)
                    to_core.start()
                    remote(fin.at[other, :], fin.at[other, :], C, C, sibling).wait_recv()
                    out_ref[...] = fin[...]
                    to_core.wait_send()
                    to_sibling.wait_send()
                    for k in range(N_CHIPS - 1):
                        remote(send_buf.at[k], b_in.at[chip], B + k, B + k, sibling).wait_send()
                    packsum_ref[...] = small_finish()
                    fout_ref[...] = fout[...]

    whole = lambda shape: pl.BlockSpec(shape, lambda p, i, win: (0,) * len(shape), pipeline_mode=pl.Buffered(1))
    grid_spec = pltpu.PrefetchScalarGridSpec(
        num_scalar_prefetch=1, grid=(N_CHIPS, nblk),
        in_specs=[pl.BlockSpec((D_MODEL, tb), lambda p, i, win: (0, i)),
                  pl.BlockSpec((pl.Element(tb), pl.Element(SHARD_PAD)),
                               lambda p, i, win: (i * tb, pl.multiple_of(win[p] * LANE, LANE))),
                  pl.BlockSpec(memory_space=pl.ANY)] + [whole(s.shape) for s in small],
        out_specs=(whole((D_MODEL, SHARD_PAD)), whole((SHARD_OUT, D_MODEL)), whole((PACK_ROWS, D_MODEL))),
        scratch_shapes=[
            pltpu.VMEM((2, D_MODEL, SHARD_PAD), F32), pltpu.VMEM((2, half, SHARD_PAD), F32),
            pltpu.VMEM((N_CHIPS - 1, half, SHARD_PAD), BF16), pltpu.VMEM((N_CHIPS, half, SHARD_PAD), BF16),
            pltpu.VMEM((D_MODEL, SHARD_PAD), F32),
        ] + _small_exchange_scratch() + [
            pltpu.SemaphoreType.DMA((N_SEMS + SMALL_SEMS,)), pltpu.SemaphoreType.DMA((N_SEMS + SMALL_SEMS,)),
            pltpu.SemaphoreType.DMA((N_CHIPS,)),
        ])
    chip = 2 * lax.axis_index("x") + lax.axis_index("y")
    owner = (chip + 1 + jnp.arange(N_CHIPS, dtype=jnp.int32)) % N_CHIPS
    win_start = (owner * SHARD_IN) // LANE
    return _call(
        body, name="gw_in_reduce", grid_spec=grid_spec,
        out_shape=(jax.ShapeDtypeStruct((D_MODEL, SHARD_PAD), F32), jax.ShapeDtypeStruct((SHARD_OUT, D_MODEL), F32),
                   jax.ShapeDtypeStruct((PACK_ROWS, D_MODEL), F32)),
        compiler_params=_params(52, ("arbitrary", "arbitrary")),
    )(win_start.astype(jnp.int32), h_t, dproj, gw_out, *small)


SMALL_SEMS = 17


def _small_exchange(gwo_hbm, small_refs, fout_ref, mine_out, sib_out, send_out, b_out, pack_ref, packs,
                    send_sems, recv_sems, local_sems, base):
    half_out = SHARD_OUT // 2
    A_OUT, B_OUT, C_OUT, PACK = base, base + 4, base + 8, base + 9
    assert len(small_refs) == PACK_PARTS
    x, y, c = lax.axis_index("x"), lax.axis_index("y"), lax.axis_index("c")
    chip = 2 * x + y
    dev = 2 * chip + c
    sibling = (x, y, 1 - c)

    def remote(src, dst, send_k, recv_k, to):
        return pltpu.make_async_remote_copy(src_ref=src, dst_ref=dst, send_sem=send_sems.at[send_k],
                                            recv_sem=recv_sems.at[recv_k], device_id=to, device_id_type=MESH_ID)

    def out_rows(j, core):
        return pl.ds(pl.multiple_of(j * SHARD_OUT + core * half_out, half_out), half_out)

    my_out_rows = pl.ds(pl.multiple_of(c * half_out, half_out), half_out)
    local = [pltpu.make_async_copy(gwo_hbm.at[out_rows(j, c), :], mine_out.at[j], local_sems.at[j])
             for j in range(N_CHIPS)]
    stage_a = [remote(gwo_hbm.at[out_rows(j, 1 - c), :], sib_out.at[j], A_OUT + j, A_OUT + j, sibling)
               for j in range(N_CHIPS)]
    mine_half_out = fout_ref.at[my_out_rows, :]
    stage_c = [remote(mine_half_out, mine_half_out, C_OUT, C_OUT, sibling)]

    def start():
        pack_ref[...] = jnp.zeros_like(pack_ref)
        for k, s_ref in enumerate(small_refs):
            pack_ref[k:k + 1, 0:s_ref.shape[1]] = s_ref[0:1, :]
        packs[dev] = pack_ref[...]
        for d in range(N_DEV):
            to = (d // 4, (d // 2) % 2, d % 2)

            @pl.when(d != dev)
            def _():
                remote(pack_ref, packs.at[dev], PACK + d, PACK + dev, to).start()

        for cp in local + stage_a:
            cp.start()

    def hand_on():
        for cp in local:
            cp.wait()
        for cp in stage_a:
            cp.wait_recv()
        for j in range(N_CHIPS):
            mine_out[j] = mine_out[j] + sib_out[j]

        for j in range(N_CHIPS):
            to = (j // 2, j % 2, c)

            @pl.when(j != chip)
            def _():
                send_out[j] = mine_out[j].astype(BF16)
                remote(send_out.at[j], b_out.at[chip], B_OUT + j, B_OUT + chip, to).start()

            @pl.when(j == chip)
            def _():
                fout_ref[my_out_rows, :] = mine_out[j]

    def finish():
        for j in range(N_CHIPS):
            @pl.when(j != chip)
            def _():
                remote(b_out.at[j], b_out.at[j], B_OUT + j, B_OUT + j, sibling).wait_recv()
                fout_ref[my_out_rows, :] += b_out[j].astype(F32)

        for cp in stage_c:
            cp.start()
        other_half_out = fout_ref.at[pl.ds(pl.multiple_of((1 - c) * half_out, half_out), half_out), :]
        remote(other_half_out, other_half_out, C_OUT, C_OUT, sibling).wait_recv()

        for d in range(N_DEV):
            @pl.when(d != dev)
            def _():
                remote(pack_ref, packs.at[d], PACK + d, PACK + d, sibling).wait_recv()
        total = packs[0]
        for d in range(1, N_DEV):
            total = total + packs[d]

        for cp in stage_a + stage_c:
            cp.wait_send()
        for j in range(N_CHIPS):
            @pl.when(j != chip)
            def _():
                remote(b_out.at[j], b_out.at[j], B_OUT + j, B_OUT + j, sibling).wait_send()
        for d in range(N_DEV):
            @pl.when(d != dev)
            def _():
                remote(pack_ref, packs.at[d], PACK + d, PACK + d, sibling).wait_send()
        return total

    return start, hand_on, finish


def _small_exchange_scratch():
    half_out = SHARD_OUT // 2
    return [
        pltpu.VMEM((SHARD_OUT, D_MODEL), F32),
        pltpu.VMEM((N_CHIPS, half_out, D_MODEL), F32), pltpu.VMEM((N_CHIPS, half_out, D_MODEL), F32),
        pltpu.VMEM((N_CHIPS, half_out, D_MODEL), BF16), pltpu.VMEM((N_CHIPS, half_out, D_MODEL), BF16),
        pltpu.VMEM((PACK_ROWS, D_MODEL), F32), pltpu.VMEM((N_DEV, PACK_ROWS, D_MODEL), F32),
    ]


def _adam_math(w, g, m, v):
    mn = ADAM_B1 * m + (1.0 - ADAM_B1) * g
    vn = ADAM_B2 * v + (1.0 - ADAM_B2) * (g * g)
    m_hat = mn / (1.0 - ADAM_B1 ** ADAM_STEP)
    v_hat = vn / (1.0 - ADAM_B2 ** ADAM_STEP)
    return -ADAM_LR * (m_hat / (jnp.sqrt(v_hat) + ADAM_EPS) + ADAM_WD * w), mn, vn


def _adamw(name, w, g, m, v, tb):
    rows, cols = w.shape

    def body(w_ref, g_ref, m_ref, v_ref, go_ref, d_ref, mo_ref, vo_ref):
        gv = g_ref[...]
        go_ref[...] = gv
        d_ref[...], mo_ref[...], vo_ref[...] = _adam_math(w_ref[...], gv, m_ref[...], v_ref[...])

    spec = pl.BlockSpec((tb, cols), lambda i: (i, 0))
    shape = jax.ShapeDtypeStruct((rows, cols), F32)
    return _call(
        body, name=name, grid=(rows // tb,), out_shape=(shape,) * 4,
        in_specs=[spec] * 4, out_specs=(spec,) * 4,
        compiler_params=_params(32, ("arbitrary",)),
    )(w, g, m, v)


def _adamw_w_in(w_t, g_window, m_t, v_t, tb):
    def body(w_ref, g_ref, m_ref, v_ref, go_ref, d_ref, mo_ref, vo_ref, gt_ref):
        gt_ref[...] = g_ref[...].T
        gv = gt_ref[pl.ds(pl.multiple_of(lax.axis_index("y") * SHARD_SHIFT, SHARD_SHIFT), SHARD_IN), :]
        go_ref[...] = gv
        d_ref[...], mo_ref[...], vo_ref[...] = _adam_math(w_ref[...], gv, m_ref[...], v_ref[...])

    spec = pl.BlockSpec((SHARD_IN, tb), lambda i: (0, i))
    shape = jax.ShapeDtypeStruct((SHARD_IN, D_MODEL), F32)
    return _call(
        body, name="adamw_w_in", grid=(D_MODEL // tb,), out_shape=(shape,) * 4,
        in_specs=[spec, pl.BlockSpec((tb, SHARD_PAD), lambda i: (i, 0)), spec, spec], out_specs=(spec,) * 4,
        scratch_shapes=[pltpu.VMEM((SHARD_PAD, tb), F32)],
        compiler_params=_params(32, ("arbitrary",)),
    )(w_t, g_window, m_t, v_t)


def _adamw_small(sums, params):
    def body(sums_ref, *refs):
        ins, outs = refs[:3 * len(params)], refs[3 * len(params):]
        for k in range(len(params)):
            w_ref, m_ref, v_ref = ins[3 * k:3 * k + 3]
            g = sums_ref[k:k + 1, 0:w_ref.shape[1]]
            go_ref, d_ref, mo_ref, vo_ref = outs[4 * k:4 * k + 4]
            go_ref[...] = g
            d_ref[...], mo_ref[...], vo_ref[...] = _adam_math(w_ref[...], g, m_ref[...], v_ref[...])

    vmem = pl.BlockSpec(memory_space=pltpu.VMEM)
    flat = [a for p in params for a in p]
    shapes = tuple(jax.ShapeDtypeStruct(p[0].shape, F32) for p in params for _ in range(4))
    res = _call(body, name="adamw_small", out_shape=shapes, in_specs=[vmem] * (1 + len(flat)),
                out_specs=(vmem,) * len(shapes), compiler_params=_params(16))(sums, *flat)
    return [res[4 * k:4 * k + 4] for k in range(len(params))]


def kernel(x, norm_g, w_in, att_sinks, ret_gn_g, w_out, final_g, loss_target, m_norm_g, m_w_in, m_att_sinks, m_ret_gn_g, m_w_out, m_final_g, v_norm_g, v_w_in, v_att_sinks, v_ret_gn_g, v_w_out, v_final_g):
    seq = x.shape[1]
    xs, tgt = x[0], loss_target[0]
    final_g2 = final_g.reshape(1, D_MODEL)
    tables = _tables(seq)

    w_in_t, m_w_in_t, v_w_in_t = w_in[0].T, m_w_in[0].T, v_w_in[0].T
    proj, h_t, w_in_full, w_out_full = _in_proj(xs, norm_g, w_in_t, w_out[0], min(TOKENS_PROJ, seq))
    mix, dxo, states, loss_part, gfin, o_att, probs, p_sinks = _mix_fwd(
        proj, xs, tgt, w_out_full, final_g2, ret_gn_g, att_sinks, tables, min(TOKENS_MIX, seq))
    dproj, grad_x, gw_out, gnorm, dgain, dsink = _mix_bwd(
        proj, dxo, mix, o_att, probs, p_sinks, states, xs, w_out_full, w_in_full, norm_g, ret_gn_g, tables,
        min(TOKENS_MIX, seq))
    g_in, g_out, sums = _gw_in_reduce(h_t, dproj, gw_out, (gnorm, gfin, dgain, dsink, loss_part),
                                      min(TOKENS_GW, seq))

    res_in = [r.T for r in _adamw_w_in(w_in_t, g_in, m_w_in_t, v_w_in_t, 256)]
    res_out = _adamw("adamw_w_out", w_out[0], g_out, m_w_out[0], v_w_out[0], SHARD_OUT)
    as_row = lambda a: a.reshape(1, D_MODEL)
    r_norm, r_final, r_gain, r_sink = _adamw_small(sums, [
        (norm_g, m_norm_g, v_norm_g), (final_g2, as_row(m_final_g), as_row(v_final_g)),
        (ret_gn_g, m_ret_gn_g, v_ret_gn_g), (att_sinks, m_att_sinks, v_att_sinks)])

    outs = []
    for k in range(4):
        outs += [r_norm[k], res_in[k][None], r_sink[k], r_gain[k], res_out[k][None], r_final[k].reshape(D_MODEL)]
    return (sums[4, 0], grad_x[None], *outs)
```

```python
import jax
import jax.numpy as jnp
import numpy as np
from jax import lax
from jax.experimental import pallas as pl
from jax.experimental.pallas import tpu as pltpu

F32 = jnp.float32
BF16 = jnp.bfloat16

D_MODEL = 1024
ATT_HEADS = 8
ATT_KV_HEADS = 2
ATT_HEAD_DIM = 64
RET_HEADS = 4
RET_QK_DIM = 64
RET_V_DIM = 128
BLK = 128
ROT_BASE = 10000.0
RMS_EPS = 1e-6
GN_EPS = 1e-6
NEG_INF = -1e30
ATT_SCALE = ATT_HEAD_DIM ** -0.5
RET_SCALE = RET_QK_DIM ** -0.5

ATT_WIDTH = ATT_HEADS * ATT_HEAD_DIM
ATT_KV_WIDTH = ATT_KV_HEADS * ATT_HEAD_DIM
RET_QK_WIDTH = RET_HEADS * RET_QK_DIM
RET_WIDTH = RET_HEADS * RET_V_DIM
MIX_WIDTH = ATT_WIDTH + RET_WIDTH
OFF_AQ = 0
OFF_AK = OFF_AQ + ATT_WIDTH
OFF_AV = OFF_AK + ATT_KV_WIDTH
OFF_AZ = OFF_AV + ATT_KV_WIDTH
OFF_RQ = OFF_AZ + ATT_WIDTH
OFF_RK = OFF_RQ + RET_QK_WIDTH
OFF_RV = OFF_RK + RET_QK_WIDTH
OFF_RZ = OFF_RV + RET_WIDTH
IN_WIDTH = OFF_RZ + RET_WIDTH

LANE = 128
BF16_ROWS = 16
HALF_LANE = LANE // 2
PAIRS = RET_QK_WIDTH // LANE
assert ATT_HEAD_DIM == HALF_LANE and RET_QK_DIM == HALF_LANE and RET_V_DIM == LANE and ATT_KV_WIDTH == LANE

N_CHIPS = 4
N_DEV = 8
SHARD_IN = IN_WIDTH // N_CHIPS
SHARD_PAD = 768
SHARD_SHIFT = SHARD_PAD - SHARD_IN
WIN_START = tuple((j * SHARD_IN) // LANE * LANE for j in range(N_CHIPS))
SHARD_OUT = MIX_WIDTH // N_CHIPS

PACK_PARTS = 5
PACK_ROWS = 8
assert PACK_PARTS <= PACK_ROWS

ADAM_LR = 0.001
ADAM_B1 = 0.9
ADAM_B2 = 0.999
ADAM_EPS = 1e-08
ADAM_WD = 0.01
ADAM_STEP = 10

VMEM_CAP = 64 * 1024 * 1024
TOKENS_PROJ = 1024
TOKENS_MIX = 512
TOKENS_GW = 2048
MESH_ID = pl.DeviceIdType.MESH


def _call(body, **kw):
    return pl.pallas_call(body, **kw)


def _params(vmem_mb, semantics=None):
    assert vmem_mb * 1024 * 1024 < VMEM_CAP
    return pltpu.CompilerParams(dimension_semantics=semantics, vmem_limit_bytes=vmem_mb * 1024 * 1024)


def _dot(a, b):
    return jnp.dot(a, b, preferred_element_type=F32)


def _dot_nt(a, b):
    return lax.dot_general(a, b, (((1,), (1,)), ((), ())), preferred_element_type=F32)


def _dot_tn(a, b):
    return lax.dot_general(a, b, (((0,), (0,)), ((), ())), preferred_element_type=F32)


def _sigmoid(z):
    return 1.0 / (1.0 + jnp.exp(-z))


def _const_spec(shape):
    nd = len(shape)
    return pl.BlockSpec(shape, lambda i: (0,) * nd, pipeline_mode=pl.Buffered(1))


def _tables(seq):
    f32 = np.float32
    pos = np.arange(seq, dtype=f32)
    theta = (f32(1.0) / (f32(ROT_BASE) ** np.linspace(0.0, 1.0, RET_QK_DIM // 2, dtype=f32))).astype(f32)
    ang = (pos[:, None] * theta[None, :]).astype(f32)
    cos, sin = np.cos(ang), np.sin(ang)
    cos2 = np.repeat(cos, 2, axis=1)
    sin2 = np.stack([-sin, sin], axis=-1).reshape(seq, RET_QK_DIM)
    cos_t = np.tile(cos2, (1, 2))
    sin_t = np.tile(sin2, (1, 2))

    log_gamma = np.log(f32(1.0) - f32(2.0) ** (f32(-5.0) - np.arange(RET_HEADS, dtype=f32))).astype(f32)
    idx = np.arange(BLK, dtype=f32)
    rel = idx[:, None] - idx[None, :]
    decay_in = np.where(rel >= 0, np.exp(log_gamma[:, None, None] * np.maximum(rel, f32(0.0))), f32(0.0))
    k_dec = np.exp(log_gamma[:, None] * (BLK - 1 - idx)[None, :])
    q_dec = np.exp(log_gamma[:, None] * (idx + 1)[None, :])
    chunk_decay = np.exp(log_gamma * f32(BLK))
    kdec_t = np.repeat(k_dec.T, RET_QK_DIM, axis=1)
    qdec_t = np.repeat(q_dec.T, RET_QK_DIM, axis=1)
    cd_t = np.broadcast_to(chunk_decay[:, None, None], (RET_HEADS, RET_QK_DIM, RET_V_DIM))
    decay_in = decay_in.reshape(PAIRS, 2 * BLK, BLK)
    cd_t = cd_t.reshape(PAIRS, 2 * RET_QK_DIM, RET_V_DIM)

    key = np.arange(BLK)[:, None]
    query = np.arange(2 * BLK)[None, :] % BLK
    bias = np.stack([np.where(key > query, NEG_INF, 0.0), np.zeros((BLK, 2 * BLK))])
    return tuple(jnp.asarray(np.ascontiguousarray(a), F32) for a in (cos_t, sin_t, decay_in, qdec_t, kdec_t, cd_t, bias))


def _low_lanes(shape):
    lane = lax.broadcasted_iota(jnp.int32, shape, len(shape) - 1)
    return (lane & HALF_LANE) == 0


def _split_heads(t):
    low = _low_lanes(t.shape)
    zero = jnp.zeros_like(t)
    return jnp.where(low, t, zero), jnp.where(low, zero, t)


def _swap_pairs(t):
    lane = lax.broadcasted_iota(jnp.int32, t.shape, 1)
    nxt = pltpu.roll(t, t.shape[1] - 1, 1)
    prv = pltpu.roll(t, 1, 1)
    return jnp.where((lane & 1) == 0, nxt, prv)


def _per_tile(fn, t):
    return jnp.concatenate([fn(_tile(t, i)) for i in range(t.shape[1] // LANE)], axis=1)


def _rotate(t, cos_t, sin_t):
    return _per_tile(lambda a: a * cos_t + _swap_pairs(a) * sin_t, t)


def _rotate_transposed(d, cos_t, sin_t):
    return _per_tile(lambda a: a * cos_t + _swap_pairs(a * sin_t), d)


def _kv_operands(cat):
    low = _low_lanes(cat.shape)
    swapped = pltpu.roll(cat, HALF_LANE, 1)
    zero = jnp.zeros_like(cat)
    pick = lambda a, b: jnp.where(low, a, b).astype(BF16)
    return ((pick(cat, zero), pick(zero, swapped)), (pick(swapped, zero), pick(zero, cat)))


def _stack_tiles(t, first_tile):
    a = t[:, first_tile * LANE:(first_tile + 1) * LANE]
    b = t[:, (first_tile + 1) * LANE:(first_tile + 2) * LANE]
    return jnp.concatenate([a, b], axis=0)


def _sink_rows(sinks_ref, group):
    first = lax.broadcasted_iota(jnp.int32, (1, 2 * BLK), 1) < BLK

    def row(h0, h1):
        return jnp.where(first, sinks_ref[0, group * 4 + h0], sinks_ref[0, group * 4 + h1])
    return row(0, 2), row(1, 3)


ATT_PROBLEMS = tuple((g, hi) for g in range(ATT_KV_HEADS) for hi in range(2))


def _in_previous_block():
    key = lax.broadcasted_iota(jnp.int32, (BLK, 2 * BLK), 0)
    query = lax.broadcasted_iota(jnp.int32, (BLK, 2 * BLK), 1) & (BLK - 1)
    return key > query


def _fold(t, prev):
    return jnp.where(prev, t[0:BLK], t[BLK:])


def _unfold(t, prev):
    zero = jnp.zeros_like(t)
    return jnp.concatenate([jnp.where(prev, t, zero), jnp.where(prev, zero, t)], axis=0)


def _attn_scores(qs, k_ops, bias, prev):
    return [_fold(_dot_nt(k_ops[g][hi], qs[g]), prev) + bias for g, hi in ATT_PROBLEMS]


def _attn_softmax(s, sinks_ref):
    sink = [_sink_rows(sinks_ref, g)[hi] for g, hi in ATT_PROBLEMS]
    m = [jnp.maximum(jnp.max(si, axis=0, keepdims=True), ki) for si, ki in zip(s, sink)]
    e = [jnp.exp(si - mi) for si, mi in zip(s, m)]
    es = [jnp.exp(ki - mi) for ki, mi in zip(sink, m)]
    inv = [1.0 / (jnp.sum(ei, axis=0, keepdims=True) + esi) for ei, esi in zip(e, es)]
    return [ei * ii for ei, ii in zip(e, inv)], [esi * ii for esi, ii in zip(es, inv)]


def _group_norm_all(outs):
    mu = [jnp.mean(o, axis=1, keepdims=True) for o in outs]
    xc = [o - m for o, m in zip(outs, mu)]
    var = [jnp.mean(c * c, axis=1, keepdims=True) for c in xc]
    rstd = [lax.rsqrt(v + GN_EPS) for v in var]
    return [c * r for c, r in zip(xc, rstd)], rstd


def _proj_cols(p_ref, rows, start, stop):
    pieces = []
    while start < stop:
        k = max(j for j in range(N_CHIPS) if WIN_START[j] <= start)
        end = min(stop, WIN_START[k] + SHARD_PAD)
        pieces.append(p_ref[k, rows, start - WIN_START[k]:end - WIN_START[k]])
        start = end
    return pieces[0] if len(pieces) == 1 else jnp.concatenate(pieces, axis=1)


def _retention_operands(p_ref, rows, cos_b, sin_b, qdec, kdec):
    qr = _rotate(_proj_cols(p_ref, rows, OFF_RQ, OFF_RQ + RET_QK_WIDTH).astype(F32), cos_b, sin_b)
    kr = _rotate(_proj_cols(p_ref, rows, OFF_RK, OFF_RK + RET_QK_WIDTH).astype(F32), cos_b, sin_b) * RET_SCALE
    return qr, kr, qr * qdec, kr * kdec


def _tile(t, i):
    return t[:, i * LANE:(i + 1) * LANE]


GATHER_SEMS = 7


def _allgather_steps(src_ref, full_ref, blk_ref, send_sems, recv_sems):
    block_rows = blk_ref.shape[0]
    assert block_rows % BF16_ROWS == 0 and full_ref.shape[0] == N_DEV * block_rows
    x, y, c = lax.axis_index("x"), lax.axis_index("y"), lax.axis_index("c")
    me, sibling = (x, y, c), (x, y, 1 - c)
    chips = [(1 - x, y), (x, 1 - y), (1 - x, 1 - y)]

    def rows(px, py, pc):
        return full_ref.at[pl.ds(pl.multiple_of((4 * px + 2 * py + pc) * block_rows, BF16_ROWS), block_rows), :]

    def copy(k, block, to, src=None):
        return pltpu.make_async_remote_copy(
            src_ref=rows(*block) if src is None else src, dst_ref=rows(*block),
            send_sem=send_sems.at[k], recv_sem=recv_sems.at[k], device_id=to, device_id_type=MESH_ID)

    first = [copy(0, me, sibling, src=blk_ref)] + [copy(1 + j, me, (*chip, c), src=blk_ref) for j, chip in enumerate(chips)]
    passed = [copy(4 + j, (*chip, c), sibling) for j, chip in enumerate(chips)]

    def start():
        blk_ref[...] = src_ref[pl.ds(pl.multiple_of(c * block_rows, BF16_ROWS), block_rows), :].astype(BF16)
        rows(*me)[...] = blk_ref[...]
        for cp in first:
            cp.start()

    def forward():
        for j, chip in enumerate(chips):
            copy(1 + j, (*chip, c), me).wait_recv()
            passed[j].start()

    def finish():
        copy(0, sibling, me).wait_recv()
        for j, chip in enumerate(chips):
            copy(4 + j, (*chip, 1 - c), me).wait_recv()
        for cp in first + passed:
            cp.wait_send()

    return start, forward, finish


WINDOW_HALF = SHARD_PAD // 2
WINDOW_SEMS = GATHER_SEMS + 1


def _window_gather(wi_ref, w4_ref, blk_ref, edge_ref, send_sems, recv_sems):
    x, y, c = lax.axis_index("x"), lax.axis_index("y"), lax.axis_index("c")
    me, sibling, pair = (x, y, c), (x, y, 1 - c), (x, 1 - y, c)
    chips = [(1 - x, y), (x, 1 - y), (1 - x, 1 - y)]
    own_rows, edge_rows = SHARD_IN - WINDOW_HALF, SHARD_SHIFT
    sends_edge = y == c

    def rows(px, py, pc):
        return w4_ref.at[pl.ds(pl.multiple_of((4 * px + 2 * py + pc) * WINDOW_HALF, BF16_ROWS), WINDOW_HALF), :]

    def copy(k, block, to, src=None):
        return pltpu.make_async_remote_copy(
            src_ref=rows(*block) if src is None else src, dst_ref=rows(*block),
            send_sem=send_sems.at[k], recv_sem=recv_sems.at[k], device_id=to, device_id_type=MESH_ID)

    edge_copy = pltpu.make_async_remote_copy(
        src_ref=edge_ref, dst_ref=edge_ref, send_sem=send_sems.at[GATHER_SEMS], recv_sem=recv_sems.at[GATHER_SEMS],
        device_id=pair, device_id_type=MESH_ID)
    first = [copy(0, me, sibling, src=blk_ref)] + [copy(1 + j, me, (*chip, c), src=blk_ref) for j, chip in enumerate(chips)]
    passed = [copy(4 + j, (*chip, c), sibling) for j, chip in enumerate(chips)]

    def at(start, size):
        return pl.ds(pl.multiple_of(start, BF16_ROWS), size)

    def start():
        @pl.when(sends_edge)
        def _():
            edge_ref[...] = wi_ref[at((1 - y) * (SHARD_IN - edge_rows), edge_rows), :].astype(BF16)
            edge_copy.start()
            blk_ref[...] = wi_ref[at(c * own_rows, WINDOW_HALF), :].astype(BF16)

        @pl.when(jnp.logical_not(sends_edge))
        def _():
            blk_ref[at(y * edge_rows, own_rows), :] = wi_ref[at(c * WINDOW_HALF, own_rows), :].astype(BF16)
            edge_copy.wait_recv()
            blk_ref[at((1 - y) * own_rows, edge_rows), :] = edge_ref[...]

        rows(*me)[...] = blk_ref[...]
        for cp in first[:-1]:
            cp.start()
        copy(0, sibling, me).wait_recv()

    def send_far():
        first[-1].start()

    def forward(j):
        copy(1 + j, (*chips[j], c), me).wait_recv()
        passed[j].start()

    def wait_forwarded(j):
        copy(4 + j, (*chips[j], 1 - c), me).wait_recv()

    def finish():
        for cp in first + passed:
            cp.wait_send()

        @pl.when(sends_edge)
        def _():
            edge_copy.wait_send()

    return start, send_far, forward, wait_forwarded, finish


def _in_proj(x, norm_g, w_in_t_shard, w_out_shard, tb):
    seq = x.shape[0]
    nblk = seq // tb
    last = nblk - 1
    chip_of_panel = (None, 1, 0, 2)

    def body(win_ref, x_ref, g_ref, wi_ref, wo_ref, p_ref, ht_ref, wt_ref, wout_ref,
             w4, blk, edge, hbuf, wout_full, blko, send_sems, recv_sems, send_sems_o, recv_sems_o):
        q, i = pl.program_id(0), pl.program_id(1)
        chip = 2 * lax.axis_index("x") + lax.axis_index("y")
        in_start, in_send_far, in_forward, in_wait_forwarded, in_finish = _window_gather(
            wi_ref, w4, blk, edge, send_sems, recv_sems)
        out_start, out_forward, out_finish = _allgather_steps(wo_ref, wout_full, blko, send_sems_o, recv_sems_o)
        rows = pl.ds(pl.multiple_of(i * tb, tb), tb)

        @pl.when(jnp.logical_and(q == 0, i == 0))
        def _():
            in_start()

        @pl.when(jnp.logical_and(q == 0, i == min(last // 2 + 1, last)))
        def _():
            in_send_far()

        @pl.when(jnp.logical_and(q == 1, i == 0))
        def _():
            out_start()

        for panel in range(1, N_CHIPS):
            @pl.when(jnp.logical_and(q == panel, i == 0))
            def _():
                in_wait_forwarded(chip_of_panel[panel])

        @pl.when(q == 0)
        def _():
            xv = x_ref[...]
            r = lax.rsqrt(jnp.mean(xv * xv, axis=1, keepdims=True) + RMS_EPS)
            h = (xv * r) * g_ref[...]
            hbuf[rows, :] = h.astype(BF16)
            ht_ref[...] = h.T.astype(BF16)

        owner = jnp.bitwise_xor(chip, q)
        window = w4[pl.ds(pl.multiple_of(owner * SHARD_PAD, SHARD_PAD), SHARD_PAD), :]
        p_ref[...] = _dot_nt(hbuf[rows, :], window).astype(BF16)

        for panel in range(1, N_CHIPS):
            @pl.when(jnp.logical_and(q == panel - 1, i == last))
            def _():
                in_forward(chip_of_panel[panel])

        @pl.when(jnp.logical_and(q == N_CHIPS - 1, i == 0))
        def _():
            out_forward()

        @pl.when(jnp.logical_and(q == N_CHIPS - 1, i == last))
        def _():
            in_finish()
            for k in range(N_CHIPS):
                first = k * SHARD_PAD + (k % 2) * SHARD_SHIFT
                wt_ref[k * SHARD_IN:(k + 1) * SHARD_IN, :] = w4[first:first + SHARD_IN, :]
            out_finish()
            wout_ref[...] = wout_full[...]

    whole = lambda shape: pl.BlockSpec(shape, lambda q, i, win: (0,) * len(shape), pipeline_mode=pl.Buffered(1))
    first_panel = lambda q, i: jnp.where(q == 0, i, last)
    grid_spec = pltpu.PrefetchScalarGridSpec(
        num_scalar_prefetch=1, grid=(N_CHIPS, nblk),
        in_specs=[pl.BlockSpec((tb, D_MODEL), lambda q, i, win: (first_panel(q, i), 0)), whole((1, D_MODEL)),
                  whole((SHARD_IN, D_MODEL)), whole((SHARD_OUT, D_MODEL))],
        out_specs=(pl.BlockSpec((None, tb, SHARD_PAD), lambda q, i, win: (win[q], i, 0)),
                   pl.BlockSpec((D_MODEL, tb), lambda q, i, win: (0, first_panel(q, i))),
                   whole((IN_WIDTH, D_MODEL)), whole((MIX_WIDTH, D_MODEL))),
        scratch_shapes=[
            pltpu.VMEM((N_CHIPS * SHARD_PAD, D_MODEL), BF16), pltpu.VMEM((WINDOW_HALF, D_MODEL), BF16),
            pltpu.VMEM((SHARD_SHIFT, D_MODEL), BF16), pltpu.VMEM((seq, D_MODEL), BF16),
            pltpu.VMEM((MIX_WIDTH, D_MODEL), BF16), pltpu.VMEM((SHARD_OUT // 2, D_MODEL), BF16),
            pltpu.SemaphoreType.DMA((WINDOW_SEMS,)), pltpu.SemaphoreType.DMA((WINDOW_SEMS,)),
            pltpu.SemaphoreType.DMA((GATHER_SEMS,)), pltpu.SemaphoreType.DMA((GATHER_SEMS,)),
        ])
    chip = 2 * lax.axis_index("x") + lax.axis_index("y")
    owner = jnp.bitwise_xor(chip, jnp.arange(N_CHIPS, dtype=jnp.int32))
    return _call(
        body, name="in_proj", grid_spec=grid_spec,
        out_shape=(jax.ShapeDtypeStruct((N_CHIPS, seq, SHARD_PAD), BF16), jax.ShapeDtypeStruct((D_MODEL, seq), BF16),
                   jax.ShapeDtypeStruct((IN_WIDTH, D_MODEL), BF16), jax.ShapeDtypeStruct((MIX_WIDTH, D_MODEL), BF16)),
        compiler_params=_params(60, ("arbitrary", "arbitrary")),
    )(owner.astype(jnp.int32), x, norm_g, w_in_t_shard, w_out_shard)


def _mix_fwd(proj, x, target, w_out, final_g, gn_gain, sinks, tables, tb):
    seq = x.shape[0]
    nsub = tb // BLK
    cos_t, sin_t, decay_in, qdec_t, kdec_t, cd_t, bias_t = tables

    def body(p_ref, x_ref, t_ref, cos_ref, sin_ref, wout_ref, fg_ref, gain_ref, sinks_ref, din_ref, qdec_ref,
             kdec_ref, cd_ref, bias_ref, mix_ref, dxo_ref, st_ref, loss_ref, gfin_ref, oatt_ref, probs_ref, psink_ref,
             kprev_ref, vprev_ref, state_ref):
        i = pl.program_id(0)

        @pl.when(i == 0)
        def _():
            kprev_ref[...] = jnp.zeros_like(kprev_ref)
            vprev_ref[...] = jnp.zeros_like(vprev_ref)
            state_ref[...] = jnp.zeros_like(state_ref)
            loss_ref[...] = jnp.zeros_like(loss_ref)
            gfin_ref[...] = jnp.zeros_like(gfin_ref)

        prev = _in_previous_block()

        def sub(j, carry):
            kp, vp, states = carry
            rows = pl.ds(pl.multiple_of(j * BLK, BLK), BLK)
            bias = bias_ref[jnp.where(jnp.logical_or(i > 0, j > 0), 1, 0)]

            aq = _proj_cols(p_ref, rows, OFF_AQ, OFF_AQ + ATT_WIDTH)
            ak = _proj_cols(p_ref, rows, OFF_AK, OFF_AK + ATT_KV_WIDTH).astype(F32)
            av = _proj_cols(p_ref, rows, OFF_AV, OFF_AV + ATT_KV_WIDTH).astype(F32)
            az = _proj_cols(p_ref, rows, OFF_AZ, OFF_AZ + ATT_WIDTH).astype(F32)
            k_ops = _kv_operands(jnp.concatenate([kp, ak], axis=0))
            v_ops = _kv_operands(jnp.concatenate([vp, av], axis=0))
            qs = [(_stack_tiles(aq, 2 * g) * ATT_SCALE).astype(BF16) for g in range(ATT_KV_HEADS)]

            qr, kr, qd, kd = _retention_operands(p_ref, rows, cos_ref[rows, :], sin_ref[rows, :],
                                                 qdec_ref[...], kdec_ref[...])
            s = _attn_scores(qs, k_ops, bias, prev)
            heads = [(t, hh) for t in range(PAIRS) for hh in range(2)]
            sc = [_dot_nt(jnp.concatenate(_split_heads(_tile(qr, t)), axis=0).astype(BF16), _tile(kr, t).astype(BF16))
                  * din_ref[t] for t in range(PAIRS)]
            qd_heads = [_split_heads(_tile(qd, t)) for t in range(PAIRS)]
            state_b = [states[t].astype(BF16) for t in range(PAIRS)]
            vs = [_proj_cols(p_ref, rows, OFF_RV + h * RET_V_DIM, OFF_RV + (h + 1) * RET_V_DIM) for h in range(RET_HEADS)]
            rzs = [_proj_cols(p_ref, rows, OFF_RZ + h * RET_V_DIM, OFF_RZ + (h + 1) * RET_V_DIM).astype(F32)
                   for h in range(RET_HEADS)]
            lhs = [jnp.concatenate([sc[t][hh * BLK:(hh + 1) * BLK].astype(BF16), qd_heads[t][hh].astype(BF16)], axis=1)
                   for t, hh in heads]
            p, p_sink = _attn_softmax(s, sinks_ref)
            o_ret = [_dot(lhs[2 * t + hh], jnp.concatenate([vs[2 * t + hh], state_b[t]], axis=0)) for t, hh in heads]
            p = [pi.astype(BF16) for pi in p]
            o_tiles = []
            for g in range(ATT_KV_HEADS):
                p_cat = jnp.concatenate([_unfold(p[2 * g], prev), _unfold(p[2 * g + 1], prev)], axis=0)
                o = _dot_tn(p_cat, jnp.concatenate(v_ops[g], axis=0))
                o_tiles += [o[0:BLK], o[BLK:]]
            ons, _ = _group_norm_all(o_ret)
            new_states = [states[t] * cd_ref[t]
                          + _dot_tn(jnp.concatenate(_split_heads(_tile(kd, t)), axis=0).astype(BF16),
                                    jnp.concatenate([vs[2 * t], vs[2 * t + 1]], axis=0)) for t in range(PAIRS)]
            o_att = jnp.concatenate(o_tiles, axis=1)
            out = [o_att * (az * _sigmoid(az))]
            out += [(ons[h] * gain_ref[:, h * RET_V_DIM:(h + 1) * RET_V_DIM]) * (rzs[h] * _sigmoid(rzs[h]))
                    for h in range(RET_HEADS)]
            mix_ref[rows, :] = jnp.concatenate(out, axis=1).astype(BF16)
            oatt_ref[rows, :] = o_att.astype(BF16)
            probs_ref[j] = jnp.stack(p)
            psink_ref[j] = jnp.concatenate(p_sink, axis=0)
            st_ref[j] = jnp.stack(state_b)
            return ak, av, tuple(new_states)

        carry = (kprev_ref[...], vprev_ref[...], tuple(state_ref[t] for t in range(PAIRS)))
        for jb in range(nsub):
            carry = sub(jnp.int32(jb), carry)
        kp, vp, states = carry
        kprev_ref[...] = kp
        vprev_ref[...] = vp
        state_ref[...] = jnp.stack(states)

        xo = x_ref[...] + _dot(mix_ref[...], wout_ref[...])
        r2 = lax.rsqrt(jnp.mean(xo * xo, axis=1, keepdims=True) + RMS_EPS)
        xn = xo * r2
        err = xn * fg_ref[...] - t_ref[...]
        loss_ref[...] += jnp.sum(err * err) * (0.5 / D_MODEL)
        dy = err * (1.0 / D_MODEL)
        gfin_ref[...] += jnp.sum(dy * xn, axis=0, keepdims=True)
        u = dy * fg_ref[...]
        dxo_ref[...] = r2 * u - xn * (r2 * jnp.mean(u * xn, axis=1, keepdims=True))

    blk_rows = lambda w: pl.BlockSpec((tb, w), lambda i: (i, 0))
    state_shape = (PAIRS, 2 * RET_QK_DIM, RET_V_DIM)
    return _call(
        body, name="mix_fwd", grid=(seq // tb,),
        out_shape=(
            jax.ShapeDtypeStruct((seq, MIX_WIDTH), BF16),
            jax.ShapeDtypeStruct((seq, D_MODEL), F32),
            jax.ShapeDtypeStruct((seq // BLK,) + state_shape, BF16),
            jax.ShapeDtypeStruct((8, LANE), F32),
            jax.ShapeDtypeStruct((1, D_MODEL), F32),
            jax.ShapeDtypeStruct((seq, ATT_WIDTH), BF16),
            jax.ShapeDtypeStruct((seq // BLK, len(ATT_PROBLEMS), BLK, 2 * BLK), BF16),
            jax.ShapeDtypeStruct((seq // BLK, len(ATT_PROBLEMS), 2 * BLK), F32),
        ),
        in_specs=[
            pl.BlockSpec((N_CHIPS, tb, SHARD_PAD), lambda i: (0, i, 0)),
            blk_rows(D_MODEL), blk_rows(D_MODEL), blk_rows(LANE), blk_rows(LANE),
            _const_spec((MIX_WIDTH, D_MODEL)), _const_spec((1, D_MODEL)), _const_spec((1, RET_WIDTH)),
            pl.BlockSpec(memory_space=pltpu.SMEM),
            _const_spec((PAIRS, 2 * BLK, BLK)), _const_spec((BLK, RET_QK_WIDTH)), _const_spec((BLK, RET_QK_WIDTH)),
            _const_spec(state_shape), _const_spec((2, BLK, 2 * BLK)),
        ],
        out_specs=(
            blk_rows(MIX_WIDTH), blk_rows(D_MODEL),
            pl.BlockSpec((nsub,) + state_shape, lambda i: (i, 0, 0, 0)),
            _const_spec((8, LANE)), _const_spec((1, D_MODEL)), blk_rows(ATT_WIDTH),
            pl.BlockSpec((nsub, len(ATT_PROBLEMS), BLK, 2 * BLK), lambda i: (i, 0, 0, 0)),
            pl.BlockSpec((nsub, len(ATT_PROBLEMS), 2 * BLK), lambda i: (i, 0, 0)),
        ),
        scratch_shapes=[
            pltpu.VMEM((BLK, ATT_KV_WIDTH), F32), pltpu.VMEM((BLK, ATT_KV_WIDTH), F32),
            pltpu.VMEM(state_shape, F32),
        ],
        compiler_params=_params(48, ("arbitrary",)),
    )(proj, x, target, cos_t, sin_t, w_out, final_g, gn_gain, sinks, decay_in, qdec_t, kdec_t, cd_t, bias_t)


def _mix_bwd(proj, dxo, mix, o_att, probs, p_sinks, states, x, w_out, w_in_t, norm_g, gn_gain, tables, tb):
    seq = dxo.shape[0]
    nsub = tb // BLK
    nblk = seq // tb
    cos_t, sin_t, decay_in, qdec_t, kdec_t, cd_t, _ = tables
    kv_cols = OFF_AK // (2 * ATT_KV_WIDTH)
    state_shape = (PAIRS, 2 * RET_QK_DIM, RET_V_DIM)

    def body(p_ref, pkv_ref, dxo_ref, mix_ref, oatt_ref, probs_ref, psink_ref, st_ref, cos_ref, sin_ref, x_ref, wout_ref,
             win_ref, g_ref, gain_ref, din_ref, qdec_ref, kdec_ref, cd_ref,
             dp_ref, gx_ref, gwout_ref, gnorm_ref, dgain_ref, dsink_ref,
             dmix_ref, kv_ref, dkc_ref, dvc_ref, gst_ref):
        i = pl.program_id(0)

        @pl.when(i == 0)
        def _():
            gwout_ref[...] = jnp.zeros_like(gwout_ref)
            gnorm_ref[...] = jnp.zeros_like(gnorm_ref)
            dgain_ref[...] = jnp.zeros_like(dgain_ref)
            dsink_ref[...] = jnp.zeros_like(dsink_ref)
            dkc_ref[...] = jnp.zeros_like(dkc_ref)
            dvc_ref[...] = jnp.zeros_like(dvc_ref)
            gst_ref[...] = jnp.zeros_like(gst_ref)

        dxo_b = dxo_ref[...].astype(BF16)
        dmix_ref[...] = _dot_nt(dxo_b, wout_ref[...])
        gwout_ref[...] += _dot_tn(mix_ref[...], dxo_b)
        kv_ref[0:BLK, :] = pkv_ref[...].astype(F32)
        kv_ref[BLK:, :] = _proj_cols(p_ref, slice(None), OFF_AK, OFF_AK + 2 * ATT_KV_WIDTH).astype(F32)
        low = _low_lanes((BLK, LANE))
        low2 = _low_lanes((2 * BLK, LANE))
        lane = lax.broadcasted_iota(jnp.int32, (1, LANE), 1)
        prev = _in_previous_block()

        def sub(jj, carry):
            dkc, dvc, gsts, dgain, dsink = carry
            j = nsub - 1 - jj
            rows = pl.ds(pl.multiple_of(j * BLK, BLK), BLK)
            both = pl.ds(pl.multiple_of(j * BLK, BLK), 2 * BLK)


            aq = _proj_cols(p_ref, rows, OFF_AQ, OFF_AQ + ATT_WIDTH)
            az = _proj_cols(p_ref, rows, OFF_AZ, OFF_AZ + ATT_WIDTH).astype(F32)
            k_ops = _kv_operands(kv_ref[both, 0:ATT_KV_WIDTH])
            v_ops = _kv_operands(kv_ref[both, ATT_KV_WIDTH:2 * ATT_KV_WIDTH])
            da = dmix_ref[rows, 0:ATT_WIDTH]
            sig = _sigmoid(az)
            d_o = da * (az * sig)
            qs = [(_stack_tiles(aq, 2 * g) * ATT_SCALE).astype(BF16) for g in range(ATT_KV_HEADS)]
            dos = [_stack_tiles(d_o, 2 * g).astype(BF16) for g in range(ATT_KV_HEADS)]
            p_b = [probs_ref[j, k] for k in range(len(ATT_PROBLEMS))]
            p = [pk.astype(F32) for pk in p_b]
            sink_all = psink_ref[j]
            p_sink = [sink_all[k:k + 1, :] for k in range(len(ATT_PROBLEMS))]
            cos_b, sin_b = cos_ref[rows, :], sin_ref[rows, :]
            qdec, kdec = qdec_ref[...], kdec_ref[...]
            qr, kr, qd, kd = _retention_operands(p_ref, rows, cos_b, sin_b, qdec, kdec)
            heads = [(t, hh) for t in range(PAIRS) for hh in range(2)]
            head_cols = [slice(h * RET_V_DIM, (h + 1) * RET_V_DIM) for h in range(RET_HEADS)]
            q_rows = [jnp.concatenate(_split_heads(_tile(qr, t)), axis=0).astype(BF16) for t in range(PAIRS)]
            k_rows = [jnp.concatenate(_split_heads(_tile(kr, t)), axis=0).astype(BF16) for t in range(PAIRS)]
            din = [din_ref[t] for t in range(PAIRS)]

            dpr = [_fold(_dot_nt(v_ops[g][hi], dos[g]), prev) for g, hi in ATT_PROBLEMS]
            sc = [(_dot_nt(q_rows[t], _tile(kr, t).astype(BF16)) * din[t]).astype(BF16) for t in range(PAIRS)]
            qd_heads = [_split_heads(_tile(qd, t)) for t in range(PAIRS)]
            kd_heads = [_split_heads(_tile(kd, t)) for t in range(PAIRS)]
            state_b = [st_ref[j, t] for t in range(PAIRS)]
            gst_b = [gsts[t].astype(BF16) for t in range(PAIRS)]
            vs = [_proj_cols(p_ref, rows, OFF_RV + h * RET_V_DIM, OFF_RV + (h + 1) * RET_V_DIM) for h in range(RET_HEADS)]
            rzs = [_proj_cols(p_ref, rows, OFF_RZ + h * RET_V_DIM, OFF_RZ + (h + 1) * RET_V_DIM).astype(F32)
                   for h in range(RET_HEADS)]
            drs = [dmix_ref[rows, ATT_WIDTH + h * RET_V_DIM:ATT_WIDTH + (h + 1) * RET_V_DIM] for h in range(RET_HEADS)]
            gains = [gain_ref[:, c] for c in head_cols]
            lhs = [jnp.concatenate([sc[t][hh * BLK:(hh + 1) * BLK], qd_heads[t][hh].astype(BF16)], axis=1) for t, hh in heads]
            rhs = [jnp.concatenate([vs[2 * t + hh], state_b[t]], axis=0) for t, hh in heads]

            delta = [jnp.sum(pi * di, axis=0, keepdims=True) for pi, di in zip(p, dpr)]
            ds = [_unfold((pi * (di - ti)).astype(BF16), prev) for pi, di, ti in zip(p, dpr, delta)]
            o_ret = [_dot(l, r) for l, r in zip(lhs, rhs)]
            for (g, hi), ki, ti in zip(ATT_PROBLEMS, p_sink, delta):
                sink_part = ki * ti
                for half in range(2):
                    tot = jnp.sum(sink_part[:, half * BLK:(half + 1) * BLK], axis=1, keepdims=True)
                    dsink = dsink - jnp.where(lane == 4 * g + 2 * half + hi, tot, 0.0)

            dq_tiles, dk_sums, dv_sums = [], [], []
            for g in range(ATT_KV_HEADS):
                ds_cat = jnp.concatenate([ds[2 * g], ds[2 * g + 1]], axis=0)
                p_cat = jnp.concatenate([_unfold(p_b[2 * g], prev), _unfold(p_b[2 * g + 1], prev)], axis=0)
                dqs = _dot_tn(ds_cat, jnp.concatenate(k_ops[g], axis=0)) * ATT_SCALE
                dq_tiles += [dqs[0:BLK], dqs[BLK:]]
                dk_sums.append(_dot(ds_cat, qs[g]))
                dv_sums.append(_dot(p_cat, dos[g]))
            ons, rstds = _group_norm_all(o_ret)
            sig_r = [_sigmoid(z) for z in rzs]
            dgn = [d * (z * g) for d, z, g in zip(drs, rzs, sig_r)]
            dz_parts = [d * (o * gn) * (g * (1.0 + z * (1.0 - g))) for d, o, gn, g, z in zip(drs, ons, gains, sig_r, rzs)]
            dgain_parts = [jnp.sum(d * o, axis=0, keepdims=True) for d, o in zip(dgn, ons)]
            don = [d * gn for d, gn in zip(dgn, gains)]
            mean_don = [jnp.mean(d, axis=1, keepdims=True) for d in don]
            mean_don_on = [jnp.mean(d * o, axis=1, keepdims=True) for d, o in zip(don, ons)]
            dob = [(r * (d - a - o * b)).astype(BF16) for r, d, a, o, b in zip(rstds, don, mean_don, ons, mean_don_on)]

            dlhs = [_dot_nt(d, r) for d, r in zip(dob, rhs)]
            drhs = [_dot_tn(l, d) for l, d in zip(lhs, dob)]
            dkds = [_dot_nt(vs[2 * t + hh], gst_b[t]) for t, hh in heads]
            dv_parts = [drhs[2 * t + hh][0:BLK] + _dot(kd_heads[t][hh].astype(BF16), gst_b[t]) for t, hh in heads]
            daz = da * oatt_ref[rows, :].astype(F32) * (sig * (1.0 + az * (1.0 - sig)))

            def kv_grad(sums):
                (a0, b0), (a1, b1) = [(s[0:2 * BLK], s[2 * BLK:]) for s in sums]
                return jnp.where(low2, a0, b1) + pltpu.roll(jnp.where(low2, a1, b0), HALF_LANE, 1)

            dk_both, dv_both = kv_grad(dk_sums), kv_grad(dv_sums)
            dak = dk_both[BLK:] + dkc
            dav = dv_both[BLK:] + dvc
            das = [(dlhs[2 * t + hh][:, 0:BLK] * din[t][hh * BLK:(hh + 1) * BLK]).astype(BF16) for t, hh in heads]
            new_gsts = [gsts[t] * cd_ref[t] + drhs[2 * t][BLK:] + drhs[2 * t + 1][BLK:] for t in range(PAIRS)]
            dq_parts = [_dot(jnp.concatenate([das[2 * t], das[2 * t + 1]], axis=1), k_rows[t])
                        + jnp.where(low, dlhs[2 * t][:, BLK:], dlhs[2 * t + 1][:, BLK:]) * _tile(qdec, t)
                        for t in range(PAIRS)]
            dk_parts = [_dot_tn(jnp.concatenate([das[2 * t], das[2 * t + 1]], axis=0), q_rows[t])
                        + jnp.where(low, dkds[2 * t], dkds[2 * t + 1]) * _tile(kdec, t) for t in range(PAIRS)]
            drq = _rotate_transposed(jnp.concatenate(dq_parts, axis=1), cos_b, sin_b)
            drk = _rotate_transposed(jnp.concatenate(dk_parts, axis=1) * RET_SCALE, cos_b, sin_b)

            dp_ref[rows, :] = jnp.concatenate(
                [jnp.concatenate(dq_tiles, axis=1), dak, dav, daz, drq, drk] + dv_parts + dz_parts, axis=1).astype(BF16)
            dgain = dgain + jnp.concatenate(dgain_parts, axis=1)
            return dk_both[0:BLK], dv_both[0:BLK], tuple(new_gsts), dgain, dsink

        carry = (dkc_ref[...], dvc_ref[...], tuple(gst_ref[t] for t in range(PAIRS)), dgain_ref[...], dsink_ref[...])
        for jb in range(nsub):
            carry = sub(jnp.int32(jb), carry)
        dkc, dvc, gsts, dgain, dsink = carry
        dkc_ref[...] = dkc
        dvc_ref[...] = dvc
        gst_ref[...] = jnp.stack(gsts)
        dgain_ref[...] = dgain
        dsink_ref[...] = dsink

        dh = _dot(dp_ref[...], win_ref[...])
        xv = x_ref[...]
        r = lax.rsqrt(jnp.mean(xv * xv, axis=1, keepdims=True) + RMS_EPS)
        xn = xv * r
        gnorm_ref[...] += jnp.sum(dh * xn, axis=0, keepdims=True)
        u = dh * g_ref[...]
        gx_ref[...] = dxo_ref[...] + r * u - xn * (r * jnp.mean(u * xn, axis=1, keepdims=True))

    rev_rows = lambda w: pl.BlockSpec((tb, w), lambda i: (nblk - 1 - i, 0))
    prev_kv = pl.BlockSpec((None, BLK, 2 * ATT_KV_WIDTH),
                           lambda i: (0, jnp.maximum((nblk - 1 - i) * nsub - 1, 0), kv_cols))
    return _call(
        body, name="mix_bwd", grid=(nblk,),
        out_shape=(
            jax.ShapeDtypeStruct((seq, IN_WIDTH), BF16),
            jax.ShapeDtypeStruct((seq, D_MODEL), F32),
            jax.ShapeDtypeStruct((MIX_WIDTH, D_MODEL), F32),
            jax.ShapeDtypeStruct((1, D_MODEL), F32),
            jax.ShapeDtypeStruct((1, RET_WIDTH), F32),
            jax.ShapeDtypeStruct((1, LANE), F32),
        ),
        in_specs=[
            pl.BlockSpec((N_CHIPS, tb, SHARD_PAD), lambda i: (0, nblk - 1 - i, 0)),
            prev_kv, rev_rows(D_MODEL), rev_rows(MIX_WIDTH), rev_rows(ATT_WIDTH),
            pl.BlockSpec((nsub, len(ATT_PROBLEMS), BLK, 2 * BLK), lambda i: (nblk - 1 - i, 0, 0, 0)),
            pl.BlockSpec((nsub, len(ATT_PROBLEMS), 2 * BLK), lambda i: (nblk - 1 - i, 0, 0)),
            pl.BlockSpec((nsub,) + state_shape, lambda i: (nblk - 1 - i, 0, 0, 0)),
            rev_rows(LANE), rev_rows(LANE), rev_rows(D_MODEL),
            _const_spec((MIX_WIDTH, D_MODEL)), _const_spec((IN_WIDTH, D_MODEL)), _const_spec((1, D_MODEL)),
            _const_spec((1, RET_WIDTH)),
            _const_spec((PAIRS, 2 * BLK, BLK)), _const_spec((BLK, RET_QK_WIDTH)), _const_spec((BLK, RET_QK_WIDTH)),
            _const_spec(state_shape),
        ],
        out_specs=(
            rev_rows(IN_WIDTH), rev_rows(D_MODEL), _const_spec((MIX_WIDTH, D_MODEL)), _const_spec((1, D_MODEL)),
            _const_spec((1, RET_WIDTH)), _const_spec((1, LANE)),
        ),
        scratch_shapes=[
            pltpu.VMEM((tb, MIX_WIDTH), F32),
            pltpu.VMEM((tb + BLK, 2 * ATT_KV_WIDTH), F32),
            pltpu.VMEM((BLK, ATT_KV_WIDTH), F32), pltpu.VMEM((BLK, ATT_KV_WIDTH), F32),
            pltpu.VMEM(state_shape, F32),
        ],
        compiler_params=_params(60, ("arbitrary",)),
    )(proj, proj, dxo, mix, o_att, probs, p_sinks, states, cos_t, sin_t, x, w_out, w_in_t, norm_g, gn_gain, decay_in,
      qdec_t, kdec_t, cd_t)


def _gw_in_reduce(h_t, dproj, gw_out, small, tb):
    seq = dproj.shape[0]
    nblk = seq // tb
    last = nblk - 1
    hand_on = min(1, last)
    half = D_MODEL // 2
    A, B, C, N_SEMS = 0, N_CHIPS, 2 * N_CHIPS, 2 * N_CHIPS + 1

    def body(win_ref, ht_ref, dp_ref, gwo_hbm, s0_ref, s1_ref, s2_ref, s3_ref, s4_ref, out_ref, fout_ref, packsum_ref,
             acc, sib, send_buf, b_in, fin, fout, mine_out, sib_out, send_out, b_out, pack_ref, packs,
             send_sems, recv_sems, local_sems):
        p, i = pl.program_id(0), pl.program_id(1)
        small_start, small_hand_on, small_finish = _small_exchange(
            gwo_hbm, (s0_ref, s1_ref, s2_ref, s3_ref, s4_ref), fout, mine_out, sib_out, send_out, b_out, pack_ref,
            packs, send_sems, recv_sems, local_sems, N_SEMS)

        @pl.when(jnp.logical_and(p == 0, i == 0))
        def _():
            small_start()

        @pl.when(jnp.logical_and(p == 0, i == hand_on))
        def _():
            small_hand_on()

        x, y, c = lax.axis_index("x"), lax.axis_index("y"), lax.axis_index("c")
        chip = 2 * x + y
        sibling = (x, y, 1 - c)
        mine = pl.ds(pl.multiple_of(c * half, half), half)
        other = pl.ds(pl.multiple_of((1 - c) * half, half), half)

        def remote(src, dst, send_k, recv_k, to):
            return pltpu.make_async_remote_copy(src_ref=src, dst_ref=dst, send_sem=send_sems.at[send_k],
                                                recv_sem=recv_sems.at[recv_k], device_id=to, device_id_type=MESH_ID)

        slot = p % 2

        @pl.when(i == 0)
        def _():
            acc[slot] = jnp.zeros(acc.shape[1:], F32)

        acc[slot] += _dot(ht_ref[...], dp_ref[...])

        for q in range(N_CHIPS):
            s = q % 2
            to_sibling = remote(acc.at[s, other, :], sib.at[s], A + q, A + q, sibling)

            @pl.when(jnp.logical_and(p == q, i == last))
            def _():
                to_sibling.start()

            if q < N_CHIPS - 1:
                dest = (chip + 1 + q) % N_CHIPS

                @pl.when(jnp.logical_and(p == q + 1, i == hand_on))
                def _():
                    to_sibling.wait_recv()
                    send_buf[q] = (acc[s, mine, :] + sib[s]).astype(BF16)
                    remote(send_buf.at[q], b_in.at[chip], B + q, B + chip, (dest // 2, dest % 2, c)).start()
                    to_sibling.wait_send()
            else:
                @pl.when(jnp.logical_and(p == q, i == last))
                def _():
                    to_sibling.wait_recv()
                    fin[mine, :] = acc[s, mine, :] + sib[s]
                    for j in range(N_CHIPS):
                        @pl.when(j != chip)
                        def _():
                            remote(b_in.at[j], b_in.at[j], B + j, B + j, sibling).wait_recv()
                            fin[mine, :] += b_in[j].astype(F32)
                    to_core = remote(fin.at[mine, :], fin.at[mine, :], C, C, sibling)
                    to_core.start()
                    remote(fin.at[other, :], fin.at[other, :], C, C, sibling).wait_recv()
                    out_ref[...] = fin[...]
                    to_core.wait_send()
                    to_sibling.wait_send()
                    for k in range(N_CHIPS - 1):
                        remote(send_buf.at[k], b_in.at[chip], B + k, B + k, sibling).wait_send()
                    packsum_ref[...] = small_finish()
                    fout_ref[...] = fout[...]

    whole = lambda shape: pl.BlockSpec(shape, lambda p, i, win: (0,) * len(shape), pipeline_mode=pl.Buffered(1))
    grid_spec = pltpu.PrefetchScalarGridSpec(
        num_scalar_prefetch=1, grid=(N_CHIPS, nblk),
        in_specs=[pl.BlockSpec((D_MODEL, tb), lambda p, i, win: (0, i)),
                  pl.BlockSpec((pl.Element(tb), pl.Element(SHARD_PAD)),
                               lambda p, i, win: (i * tb, pl.multiple_of(win[p] * LANE, LANE))),
                  pl.BlockSpec(memory_space=pl.ANY)] + [whole(s.shape) for s in small],
        out_specs=(whole((D_MODEL, SHARD_PAD)), whole((SHARD_OUT, D_MODEL)), whole((PACK_ROWS, D_MODEL))),
        scratch_shapes=[
            pltpu.VMEM((2, D_MODEL, SHARD_PAD), F32), pltpu.VMEM((2, half, SHARD_PAD), F32),
            pltpu.VMEM((N_CHIPS - 1, half, SHARD_PAD), BF16), pltpu.VMEM((N_CHIPS, half, SHARD_PAD), BF16),
            pltpu.VMEM((D_MODEL, SHARD_PAD), F32),
        ] + _small_exchange_scratch() + [
            pltpu.SemaphoreType.DMA((N_SEMS + SMALL_SEMS,)), pltpu.SemaphoreType.DMA((N_SEMS + SMALL_SEMS,)),
            pltpu.SemaphoreType.DMA((N_CHIPS,)),
        ])
    chip = 2 * lax.axis_index("x") + lax.axis_index("y")
    owner = (chip + 1 + jnp.arange(N_CHIPS, dtype=jnp.int32)) % N_CHIPS
    win_start = (owner * SHARD_IN) // LANE
    return _call(
        body, name="gw_in_reduce", grid_spec=grid_spec,
        out_shape=(jax.ShapeDtypeStruct((D_MODEL, SHARD_PAD), F32), jax.ShapeDtypeStruct((SHARD_OUT, D_MODEL), F32),
                   jax.ShapeDtypeStruct((PACK_ROWS, D_MODEL), F32)),
        compiler_params=_params(52, ("arbitrary", "arbitrary")),
    )(win_start.astype(jnp.int32), h_t, dproj, gw_out, *small)


SMALL_SEMS = 17


def _small_exchange(gwo_hbm, small_refs, fout_ref, mine_out, sib_out, send_out, b_out, pack_ref, packs,
                    send_sems, recv_sems, local_sems, base):
    half_out = SHARD_OUT // 2
    A_OUT, B_OUT, C_OUT, PACK = base, base + 4, base + 8, base + 9
    assert len(small_refs) == PACK_PARTS
    x, y, c = lax.axis_index("x"), lax.axis_index("y"), lax.axis_index("c")
    chip = 2 * x + y
    dev = 2 * chip + c
    sibling = (x, y, 1 - c)

    def remote(src, dst, send_k, recv_k, to):
        return pltpu.make_async_remote_copy(src_ref=src, dst_ref=dst, send_sem=send_sems.at[send_k],
                                            recv_sem=recv_sems.at[recv_k], device_id=to, device_id_type=MESH_ID)

    def out_rows(j, core):
        return pl.ds(pl.multiple_of(j * SHARD_OUT + core * half_out, half_out), half_out)

    my_out_rows = pl.ds(pl.multiple_of(c * half_out, half_out), half_out)
    local = [pltpu.make_async_copy(gwo_hbm.at[out_rows(j, c), :], mine_out.at[j], local_sems.at[j])
             for j in range(N_CHIPS)]
    stage_a = [remote(gwo_hbm.at[out_rows(j, 1 - c), :], sib_out.at[j], A_OUT + j, A_OUT + j, sibling)
               for j in range(N_CHIPS)]
    mine_half_out = fout_ref.at[my_out_rows, :]
    stage_c = [remote(mine_half_out, mine_half_out, C_OUT, C_OUT, sibling)]

    def start():
        pack_ref[...] = jnp.zeros_like(pack_ref)
        for k, s_ref in enumerate(small_refs):
            pack_ref[k:k + 1, 0:s_ref.shape[1]] = s_ref[0:1, :]
        packs[dev] = pack_ref[...]
        for d in range(N_DEV):
            to = (d // 4, (d // 2) % 2, d % 2)

            @pl.when(d != dev)
            def _():
                remote(pack_ref, packs.at[dev], PACK + d, PACK + dev, to).start()

        for cp in local + stage_a:
            cp.start()

    def hand_on():
        for cp in local:
            cp.wait()
        for cp in stage_a:
            cp.wait_recv()
        for j in range(N_CHIPS):
            mine_out[j] = mine_out[j] + sib_out[j]

        for j in range(N_CHIPS):
            to = (j // 2, j % 2, c)

            @pl.when(j != chip)
            def _():
                send_out[j] = mine_out[j].astype(BF16)
                remote(send_out.at[j], b_out.at[chip], B_OUT + j, B_OUT + chip, to).start()

            @pl.when(j == chip)
            def _():
                fout_ref[my_out_rows, :] = mine_out[j]

    def finish():
        for j in range(N_CHIPS):
            @pl.when(j != chip)
            def _():
                remote(b_out.at[j], b_out.at[j], B_OUT + j, B_OUT + j, sibling).wait_recv()
                fout_ref[my_out_rows, :] += b_out[j].astype(F32)

        for cp in stage_c:
            cp.start()
        other_half_out = fout_ref.at[pl.ds(pl.multiple_of((1 - c) * half_out, half_out), half_out), :]
        remote(other_half_out, other_half_out, C_OUT, C_OUT, sibling).wait_recv()

        for d in range(N_DEV):
            @pl.when(d != dev)
            def _():
                remote(pack_ref, packs.at[d], PACK + d, PACK + d, sibling).wait_recv()
        total = packs[0]
        for d in range(1, N_DEV):
            total = total + packs[d]

        for cp in stage_a + stage_c:
            cp.wait_send()
        for j in range(N_CHIPS):
            @pl.when(j != chip)
            def _():
                remote(b_out.at[j], b_out.at[j], B_OUT + j, B_OUT + j, sibling).wait_send()
        for d in range(N_DEV):
            @pl.when(d != dev)
            def _():
                remote(pack_ref, packs.at[d], PACK + d, PACK + d, sibling).wait_send()
        return total

    return start, hand_on, finish


def _small_exchange_scratch():
    half_out = SHARD_OUT // 2
    return [
        pltpu.VMEM((SHARD_OUT, D_MODEL), F32),
        pltpu.VMEM((N_CHIPS, half_out, D_MODEL), F32), pltpu.VMEM((N_CHIPS, half_out, D_MODEL), F32),
        pltpu.VMEM((N_CHIPS, half_out, D_MODEL), BF16), pltpu.VMEM((N_CHIPS, half_out, D_MODEL), BF16),
        pltpu.VMEM((PACK_ROWS, D_MODEL), F32), pltpu.VMEM((N_DEV, PACK_ROWS, D_MODEL), F32),
    ]


def _adam_math(w, g, m, v):
    mn = ADAM_B1 * m + (1.0 - ADAM_B1) * g
    vn = ADAM_B2 * v + (1.0 - ADAM_B2) * (g * g)
    m_hat = mn / (1.0 - ADAM_B1 ** ADAM_STEP)
    v_hat = vn / (1.0 - ADAM_B2 ** ADAM_STEP)
    return -ADAM_LR * (m_hat / (jnp.sqrt(v_hat) + ADAM_EPS) + ADAM_WD * w), mn, vn


def _adamw(name, w, g, m, v, tb):
    rows, cols = w.shape

    def body(w_ref, g_ref, m_ref, v_ref, go_ref, d_ref, mo_ref, vo_ref):
        gv = g_ref[...]
        go_ref[...] = gv
        d_ref[...], mo_ref[...], vo_ref[...] = _adam_math(w_ref[...], gv, m_ref[...], v_ref[...])

    spec = pl.BlockSpec((tb, cols), lambda i: (i, 0))
    shape = jax.ShapeDtypeStruct((rows, cols), F32)
    return _call(
        body, name=name, grid=(rows // tb,), out_shape=(shape,) * 4,
        in_specs=[spec] * 4, out_specs=(spec,) * 4,
        compiler_params=_params(32, ("arbitrary",)),
    )(w, g, m, v)


def _adamw_w_in(w_t, g_window, m_t, v_t, tb):
    def body(w_ref, g_ref, m_ref, v_ref, go_ref, d_ref, mo_ref, vo_ref, gt_ref):
        gt_ref[...] = g_ref[...].T
        gv = gt_ref[pl.ds(pl.multiple_of(lax.axis_index("y") * SHARD_SHIFT, SHARD_SHIFT), SHARD_IN), :]
        go_ref[...] = gv
        d_ref[...], mo_ref[...], vo_ref[...] = _adam_math(w_ref[...], gv, m_ref[...], v_ref[...])

    spec = pl.BlockSpec((SHARD_IN, tb), lambda i: (0, i))
    shape = jax.ShapeDtypeStruct((SHARD_IN, D_MODEL), F32)
    return _call(
        body, name="adamw_w_in", grid=(D_MODEL // tb,), out_shape=(shape,) * 4,
        in_specs=[spec, pl.BlockSpec((tb, SHARD_PAD), lambda i: (i, 0)), spec, spec], out_specs=(spec,) * 4,
        scratch_shapes=[pltpu.VMEM((SHARD_PAD, tb), F32)],
        compiler_params=_params(32, ("arbitrary",)),
    )(w_t, g_window, m_t, v_t)


def _adamw_small(sums, params):
    def body(sums_ref, *refs):
        ins, outs = refs[:3 * len(params)], refs[3 * len(params):]
        for k in range(len(params)):
            w_ref, m_ref, v_ref = ins[3 * k:3 * k + 3]
            g = sums_ref[k:k + 1, 0:w_ref.shape[1]]
            go_ref, d_ref, mo_ref, vo_ref = outs[4 * k:4 * k + 4]
            go_ref[...] = g
            d_ref[...], mo_ref[...], vo_ref[...] = _adam_math(w_ref[...], g, m_ref[...], v_ref[...])

    vmem = pl.BlockSpec(memory_space=pltpu.VMEM)
    flat = [a for p in params for a in p]
    shapes = tuple(jax.ShapeDtypeStruct(p[0].shape, F32) for p in params for _ in range(4))
    res = _call(body, name="adamw_small", out_shape=shapes, in_specs=[vmem] * (1 + len(flat)),
                out_specs=(vmem,) * len(shapes), compiler_params=_params(16))(sums, *flat)
    return [res[4 * k:4 * k + 4] for k in range(len(params))]


def kernel(x, norm_g, w_in, att_sinks, ret_gn_g, w_out, final_g, loss_target, m_norm_g, m_w_in, m_att_sinks, m_ret_gn_g, m_w_out, m_final_g, v_norm_g, v_w_in, v_att_sinks, v_ret_gn_g, v_w_out, v_final_g):
    seq = x.shape[1]
    xs, tgt = x[0], loss_target[0]
    final_g2 = final_g.reshape(1, D_MODEL)
    tables = _tables(seq)

    w_in_t, m_w_in_t, v_w_in_t = w_in[0].T, m_w_in[0].T, v_w_in[0].T
    proj, h_t, w_in_full, w_out_full = _in_proj(xs, norm_g, w_in_t, w_out[0], min(TOKENS_PROJ, seq))
    mix, dxo, states, loss_part, gfin, o_att, probs, p_sinks = _mix_fwd(
        proj, xs, tgt, w_out_full, final_g2, ret_gn_g, att_sinks, tables, min(TOKENS_MIX, seq))
    dproj, grad_x, gw_out, gnorm, dgain, dsink = _mix_bwd(
        proj, dxo, mix, o_att, probs, p_sinks, states, xs, w_out_full, w_in_full, norm_g, ret_gn_g, tables,
        min(TOKENS_MIX, seq))
    g_in, g_out, sums = _gw_in_reduce(h_t, dproj, gw_out, (gnorm, gfin, dgain, dsink, loss_part),
                                      min(TOKENS_GW, seq))

    res_in = [r.T for r in _adamw_w_in(w_in_t, g_in, m_w_in_t, v_w_in_t, 256)]
    res_out = _adamw("adamw_w_out", w_out[0], g_out, m_w_out[0], v_w_out[0], SHARD_OUT)
    as_row = lambda a: a.reshape(1, D_MODEL)
    r_norm, r_final, r_gain, r_sink = _adamw_small(sums, [
        (norm_g, m_norm_g, v_norm_g), (final_g2, as_row(m_final_g), as_row(v_final_g)),
        (ret_gn_g, m_ret_gn_g, v_ret_gn_g), (att_sinks, m_att_sinks, v_att_sinks)])

    outs = []
    for k in range(4):
        outs += [r_norm[k], res_in[k][None], r_sink[k], r_gain[k], res_out[k][None], r_final[k].reshape(D_MODEL)]
    return (sums[4, 0], grad_x[None], *outs)
```

```python
import jax
import jax.numpy as jnp
import numpy as np
from jax import lax
from jax.experimental import pallas as pl
from jax.experimental.pallas import tpu as pltpu

F32 = jnp.float32
BF16 = jnp.bfloat16

D_MODEL = 1024
ATT_HEADS = 8
ATT_KV_HEADS = 2
ATT_HEAD_DIM = 64
RET_HEADS = 4
RET_QK_DIM = 64
RET_V_DIM = 128
BLK = 128
ROT_BASE = 10000.0
RMS_EPS = 1e-6
GN_EPS = 1e-6
NEG_INF = -1e30
ATT_SCALE = ATT_HEAD_DIM ** -0.5
RET_SCALE = RET_QK_DIM ** -0.5

ATT_WIDTH = ATT_HEADS * ATT_HEAD_DIM
ATT_KV_WIDTH = ATT_KV_HEADS * ATT_HEAD_DIM
RET_QK_WIDTH = RET_HEADS * RET_QK_DIM
RET_WIDTH = RET_HEADS * RET_V_DIM
MIX_WIDTH = ATT_WIDTH + RET_WIDTH
OFF_AQ = 0
OFF_AK = OFF_AQ + ATT_WIDTH
OFF_AV = OFF_AK + ATT_KV_WIDTH
OFF_AZ = OFF_AV + ATT_KV_WIDTH
OFF_RQ = OFF_AZ + ATT_WIDTH
OFF_RK = OFF_RQ + RET_QK_WIDTH
OFF_RV = OFF_RK + RET_QK_WIDTH
OFF_RZ = OFF_RV + RET_WIDTH
IN_WIDTH = OFF_RZ + RET_WIDTH

LANE = 128
BF16_ROWS = 16
HALF_LANE = LANE // 2
PAIRS = RET_QK_WIDTH // LANE
assert ATT_HEAD_DIM == HALF_LANE and RET_QK_DIM == HALF_LANE and RET_V_DIM == LANE and ATT_KV_WIDTH == LANE

N_CHIPS = 4
N_DEV = 8
SHARD_IN = IN_WIDTH // N_CHIPS
SHARD_PAD = 768
SHARD_SHIFT = SHARD_PAD - SHARD_IN
WIN_START = tuple((j * SHARD_IN) // LANE * LANE for j in range(N_CHIPS))
SHARD_OUT = MIX_WIDTH // N_CHIPS

PACK_PARTS = 5
PACK_ROWS = 8
assert PACK_PARTS <= PACK_ROWS

ADAM_LR = 0.001
ADAM_B1 = 0.9
ADAM_B2 = 0.999
ADAM_EPS = 1e-08
ADAM_WD = 0.01
ADAM_STEP = 10

VMEM_CAP = 64 * 1024 * 1024
TOKENS_PROJ = 1024
TOKENS_MIX = 512
TOKENS_GW = 2048
MESH_ID = pl.DeviceIdType.MESH


def _call(body, **kw):
    return pl.pallas_call(body, **kw)


def _params(vmem_mb, semantics=None):
    assert vmem_mb * 1024 * 1024 < VMEM_CAP
    return pltpu.CompilerParams(dimension_semantics=semantics, vmem_limit_bytes=vmem_mb * 1024 * 1024)


def _dot(a, b):
    return jnp.dot(a, b, preferred_element_type=F32)


def _dot_nt(a, b):
    return lax.dot_general(a, b, (((1,), (1,)), ((), ())), preferred_element_type=F32)


def _dot_tn(a, b):
    return lax.dot_general(a, b, (((0,), (0,)), ((), ())), preferred_element_type=F32)


def _sigmoid(z):
    return 1.0 / (1.0 + jnp.exp(-z))


def _const_spec(shape):
    nd = len(shape)
    return pl.BlockSpec(shape, lambda i: (0,) * nd, pipeline_mode=pl.Buffered(1))


def _tables(seq):
    f32 = np.float32
    pos = np.arange(seq, dtype=f32)
    theta = (f32(1.0) / (f32(ROT_BASE) ** np.linspace(0.0, 1.0, RET_QK_DIM // 2, dtype=f32))).astype(f32)
    ang = (pos[:, None] * theta[None, :]).astype(f32)
    cos, sin = np.cos(ang), np.sin(ang)
    cos2 = np.repeat(cos, 2, axis=1)
    sin2 = np.stack([-sin, sin], axis=-1).reshape(seq, RET_QK_DIM)
    cos_t = np.tile(cos2, (1, 2))
    sin_t = np.tile(sin2, (1, 2))

    log_gamma = np.log(f32(1.0) - f32(2.0) ** (f32(-5.0) - np.arange(RET_HEADS, dtype=f32))).astype(f32)
    idx = np.arange(BLK, dtype=f32)
    rel = idx[:, None] - idx[None, :]
    decay_in = np.where(rel >= 0, np.exp(log_gamma[:, None, None] * np.maximum(rel, f32(0.0))), f32(0.0))
    k_dec = np.exp(log_gamma[:, None] * (BLK - 1 - idx)[None, :])
    q_dec = np.exp(log_gamma[:, None] * (idx + 1)[None, :])
    chunk_decay = np.exp(log_gamma * f32(BLK))
    kdec_t = np.repeat(k_dec.T, RET_QK_DIM, axis=1)
    qdec_t = np.repeat(q_dec.T, RET_QK_DIM, axis=1)
    cd_t = np.broadcast_to(chunk_decay[:, None, None], (RET_HEADS, RET_QK_DIM, RET_V_DIM))
    decay_in = decay_in.reshape(PAIRS, 2 * BLK, BLK)
    cd_t = cd_t.reshape(PAIRS, 2 * RET_QK_DIM, RET_V_DIM)

    key = np.arange(BLK)[:, None]
    query = np.arange(2 * BLK)[None, :] % BLK
    bias = np.stack([np.where(key > query, NEG_INF, 0.0), np.zeros((BLK, 2 * BLK))])
    return tuple(jnp.asarray(np.ascontiguousarray(a), F32) for a in (cos_t, sin_t, decay_in, qdec_t, kdec_t, cd_t, bias))


def _low_lanes(shape):
    lane = lax.broadcasted_iota(jnp.int32, shape, len(shape) - 1)
    return (lane & HALF_LANE) == 0


def _split_heads(t):
    low = _low_lanes(t.shape)
    zero = jnp.zeros_like(t)
    return jnp.where(low, t, zero), jnp.where(low, zero, t)


def _swap_pairs(t):
    lane = lax.broadcasted_iota(jnp.int32, t.shape, 1)
    nxt = pltpu.roll(t, t.shape[1] - 1, 1)
    prv = pltpu.roll(t, 1, 1)
    return jnp.where((lane & 1) == 0, nxt, prv)


def _per_tile(fn, t):
    return jnp.concatenate([fn(_tile(t, i)) for i in range(t.shape[1] // LANE)], axis=1)


def _rotate(t, cos_t, sin_t):
    return _per_tile(lambda a: a * cos_t + _swap_pairs(a) * sin_t, t)


def _rotate_transposed(d, cos_t, sin_t):
    return _per_tile(lambda a: a * cos_t + _swap_pairs(a * sin_t), d)


def _kv_operands(cat):
    low = _low_lanes(cat.shape)
    swapped = pltpu.roll(cat, HALF_LANE, 1)
    zero = jnp.zeros_like(cat)
    pick = lambda a, b: jnp.where(low, a, b).astype(BF16)
    return ((pick(cat, zero), pick(zero, swapped)), (pick(swapped, zero), pick(zero, cat)))


def _stack_tiles(t, first_tile):
    a = t[:, first_tile * LANE:(first_tile + 1) * LANE]
    b = t[:, (first_tile + 1) * LANE:(first_tile + 2) * LANE]
    return jnp.concatenate([a, b], axis=0)


def _sink_rows(sinks_ref, group):
    first = lax.broadcasted_iota(jnp.int32, (1, 2 * BLK), 1) < BLK

    def row(h0, h1):
        return jnp.where(first, sinks_ref[0, group * 4 + h0], sinks_ref[0, group * 4 + h1])
    return row(0, 2), row(1, 3)


ATT_PROBLEMS = tuple((g, hi) for g in range(ATT_KV_HEADS) for hi in range(2))


def _in_previous_block():
    key = lax.broadcasted_iota(jnp.int32, (BLK, 2 * BLK), 0)
    query = lax.broadcasted_iota(jnp.int32, (BLK, 2 * BLK), 1) & (BLK - 1)
    return key > query


def _fold(t, prev):
    return jnp.where(prev, t[0:BLK], t[BLK:])


def _unfold(t, prev):
    zero = jnp.zeros_like(t)
    return jnp.concatenate([jnp.where(prev, t, zero), jnp.where(prev, zero, t)], axis=0)


def _attn_scores(qs, k_ops, bias, prev):
    return [_fold(_dot_nt(k_ops[g][hi], qs[g]), prev) + bias for g, hi in ATT_PROBLEMS]


def _attn_softmax(s, sinks_ref):
    sink = [_sink_rows(sinks_ref, g)[hi] for g, hi in ATT_PROBLEMS]
    m = [jnp.maximum(jnp.max(si, axis=0, keepdims=True), ki) for si, ki in zip(s, sink)]
    e = [jnp.exp(si - mi) for si, mi in zip(s, m)]
    es = [jnp.exp(ki - mi) for ki, mi in zip(sink, m)]
    inv = [1.0 / (jnp.sum(ei, axis=0, keepdims=True) + esi) for ei, esi in zip(e, es)]
    return [ei * ii for ei, ii in zip(e, inv)], [esi * ii for esi, ii in zip(es, inv)]


def _group_norm_all(outs):
    mu = [jnp.mean(o, axis=1, keepdims=True) for o in outs]
    xc = [o - m for o, m in zip(outs, mu)]
    var = [jnp.mean(c * c, axis=1, keepdims=True) for c in xc]
    rstd = [lax.rsqrt(v + GN_EPS) for v in var]
    return [c * r for c, r in zip(xc, rstd)], rstd


def _proj_cols(p_ref, rows, start, stop):
    pieces = []
    while start < stop:
        k = max(j for j in range(N_CHIPS) if WIN_START[j] <= start)
        end = min(stop, WIN_START[k] + SHARD_PAD)
        pieces.append(p_ref[k, rows, start - WIN_START[k]:end - WIN_START[k]])
        start = end
    return pieces[0] if len(pieces) == 1 else jnp.concatenate(pieces, axis=1)


def _retention_operands(p_ref, rows, cos_b, sin_b, qdec, kdec):
    qr = _rotate(_proj_cols(p_ref, rows, OFF_RQ, OFF_RQ + RET_QK_WIDTH).astype(F32), cos_b, sin_b)
    kr = _rotate(_proj_cols(p_ref, rows, OFF_RK, OFF_RK + RET_QK_WIDTH).astype(F32), cos_b, sin_b) * RET_SCALE
    return qr, kr, qr * qdec, kr * kdec


def _tile(t, i):
    return t[:, i * LANE:(i + 1) * LANE]


GATHER_SEMS = 7


def _allgather_steps(src_ref, full_ref, blk_ref, send_sems, recv_sems):
    block_rows = blk_ref.shape[0]
    assert block_rows % BF16_ROWS == 0 and full_ref.shape[0] == N_DEV * block_rows
    x, y, c = lax.axis_index("x"), lax.axis_index("y"), lax.axis_index("c")
    me, sibling = (x, y, c), (x, y, 1 - c)
    chips = [(1 - x, y), (x, 1 - y), (1 - x, 1 - y)]

    def rows(px, py, pc):
        return full_ref.at[pl.ds(pl.multiple_of((4 * px + 2 * py + pc) * block_rows, BF16_ROWS), block_rows), :]

    def copy(k, block, to, src=None):
        return pltpu.make_async_remote_copy(
            src_ref=rows(*block) if src is None else src, dst_ref=rows(*block),
            send_sem=send_sems.at[k], recv_sem=recv_sems.at[k], device_id=to, device_id_type=MESH_ID)

    first = [copy(0, me, sibling, src=blk_ref)] + [copy(1 + j, me, (*chip, c), src=blk_ref) for j, chip in enumerate(chips)]
    passed = [copy(4 + j, (*chip, c), sibling) for j, chip in enumerate(chips)]

    def start():
        blk_ref[...] = src_ref[pl.ds(pl.multiple_of(c * block_rows, BF16_ROWS), block_rows), :].astype(BF16)
        rows(*me)[...] = blk_ref[...]
        for cp in first:
            cp.start()

    def forward():
        for j, chip in enumerate(chips):
            copy(1 + j, (*chip, c), me).wait_recv()
            passed[j].start()

    def finish():
        copy(0, sibling, me).wait_recv()
        for j, chip in enumerate(chips):
            copy(4 + j, (*chip, 1 - c), me).wait_recv()
        for cp in first + passed:
            cp.wait_send()

    return start, forward, finish


WINDOW_HALF = SHARD_PAD // 2
WINDOW_SEMS = GATHER_SEMS + 1


def _window_gather(wi_ref, w4_ref, blk_ref, edge_ref, send_sems, recv_sems):
    x, y, c = lax.axis_index("x"), lax.axis_index("y"), lax.axis_index("c")
    me, sibling, pair = (x, y, c), (x, y, 1 - c), (x, 1 - y, c)
    chips = [(1 - x, y), (x, 1 - y), (1 - x, 1 - y)]
    own_rows, edge_rows = SHARD_IN - WINDOW_HALF, SHARD_SHIFT
    sends_edge = y == c

    def rows(px, py, pc):
        return w4_ref.at[pl.ds(pl.multiple_of((4 * px + 2 * py + pc) * WINDOW_HALF, BF16_ROWS), WINDOW_HALF), :]

    def copy(k, block, to, src=None):
        return pltpu.make_async_remote_copy(
            src_ref=rows(*block) if src is None else src, dst_ref=rows(*block),
            send_sem=send_sems.at[k], recv_sem=recv_sems.at[k], device_id=to, device_id_type=MESH_ID)

    edge_copy = pltpu.make_async_remote_copy(
        src_ref=edge_ref, dst_ref=edge_ref, send_sem=send_sems.at[GATHER_SEMS], recv_sem=recv_sems.at[GATHER_SEMS],
        device_id=pair, device_id_type=MESH_ID)
    first = [copy(0, me, sibling, src=blk_ref)] + [copy(1 + j, me, (*chip, c), src=blk_ref) for j, chip in enumerate(chips)]
    passed = [copy(4 + j, (*chip, c), sibling) for j, chip in enumerate(chips)]

    def at(start, size):
        return pl.ds(pl.multiple_of(start, BF16_ROWS), size)

    def start():
        @pl.when(sends_edge)
        def _():
            edge_ref[...] = wi_ref[at((1 - y) * (SHARD_IN - edge_rows), edge_rows), :].astype(BF16)
            edge_copy.start()
            blk_ref[...] = wi_ref[at(c * own_rows, WINDOW_HALF), :].astype(BF16)

        @pl.when(jnp.logical_not(sends_edge))
        def _():
            blk_ref[at(y * edge_rows, own_rows), :] = wi_ref[at(c * WINDOW_HALF, own_rows), :].astype(BF16)
            edge_copy.wait_recv()
            blk_ref[at((1 - y) * own_rows, edge_rows), :] = edge_ref[...]

        rows(*me)[...] = blk_ref[...]
        for cp in first[:-1]:
            cp.start()
        copy(0, sibling, me).wait_recv()

    def send_far():
        first[-1].start()

    def forward(j):
        copy(1 + j, (*chips[j], c), me).wait_recv()
        passed[j].start()

    def wait_forwarded(j):
        copy(4 + j, (*chips[j], 1 - c), me).wait_recv()

    def finish():
        for cp in first + passed:
            cp.wait_send()

        @pl.when(sends_edge)
        def _():
            edge_copy.wait_send()

    return start, send_far, forward, wait_forwarded, finish


def _in_proj(x, norm_g, w_in_t_shard, w_out_shard, tb):
    seq = x.shape[0]
    nblk = seq // tb
    last = nblk - 1
    chip_of_panel = (None, 1, 0, 2)

    def body(win_ref, x_ref, g_ref, wi_ref, wo_ref, p_ref, ht_ref, wt_ref, wout_ref,
             w4, blk, edge, hbuf, wout_full, blko, send_sems, recv_sems, send_sems_o, recv_sems_o):
        q, i = pl.program_id(0), pl.program_id(1)
        chip = 2 * lax.axis_index("x") + lax.axis_index("y")
        in_start, in_send_far, in_forward, in_wait_forwarded, in_finish = _window_gather(
            wi_ref, w4, blk, edge, send_sems, recv_sems)
        out_start, out_forward, out_finish = _allgather_steps(wo_ref, wout_full, blko, send_sems_o, recv_sems_o)
        rows = pl.ds(pl.multiple_of(i * tb, tb), tb)

        @pl.when(jnp.logical_and(q == 0, i == 0))
        def _():
            in_start()

        @pl.when(jnp.logical_and(q == 0, i == min(last // 2 + 1, last)))
        def _():
            in_send_far()

        @pl.when(jnp.logical_and(q == 1, i == 0))
        def _():
            out_start()

        for panel in range(1, N_CHIPS):
            @pl.when(jnp.logical_and(q == panel, i == 0))
            def _():
                in_wait_forwarded(chip_of_panel[panel])

        @pl.when(q == 0)
        def _():
            xv = x_ref[...]
            r = lax.rsqrt(jnp.mean(xv * xv, axis=1, keepdims=True) + RMS_EPS)
            h = (xv * r) * g_ref[...]
            hbuf[rows, :] = h.astype(BF16)
            ht_ref[...] = h.T.astype(BF16)

        owner = jnp.bitwise_xor(chip, q)
        window = w4[pl.ds(pl.multiple_of(owner * SHARD_PAD, SHARD_PAD), SHARD_PAD), :]
        p_ref[...] = _dot_nt(hbuf[rows, :], window).astype(BF16)

        for panel in range(1, N_CHIPS):
            @pl.when(jnp.logical_and(q == panel - 1, i == last))
            def _():
                in_forward(chip_of_panel[panel])

        @pl.when(jnp.logical_and(q == N_CHIPS - 1, i == 0))
        def _():
            out_forward()

        @pl.when(jnp.logical_and(q == N_CHIPS - 1, i == last))
        def _():
            in_finish()
            for k in range(N_CHIPS):
                first = k * SHARD_PAD + (k % 2) * SHARD_SHIFT
                wt_ref[k * SHARD_IN:(k + 1) * SHARD_IN, :] = w4[first:first + SHARD_IN, :]
            out_finish()
            wout_ref[...] = wout_full[...]

    whole = lambda shape: pl.BlockSpec(shape, lambda q, i, win: (0,) * len(shape), pipeline_mode=pl.Buffered(1))
    first_panel = lambda q, i: jnp.where(q == 0, i, last)
    grid_spec = pltpu.PrefetchScalarGridSpec(
        num_scalar_prefetch=1, grid=(N_CHIPS, nblk),
        in_specs=[pl.BlockSpec((tb, D_MODEL), lambda q, i, win: (first_panel(q, i), 0)), whole((1, D_MODEL)),
                  whole((SHARD_IN, D_MODEL)), whole((SHARD_OUT, D_MODEL))],
        out_specs=(pl.BlockSpec((None, tb, SHARD_PAD), lambda q, i, win: (win[q], i, 0)),
                   pl.BlockSpec((D_MODEL, tb), lambda q, i, win: (0, first_panel(q, i))),
                   whole((IN_WIDTH, D_MODEL)), whole((MIX_WIDTH, D_MODEL))),
        scratch_shapes=[
            pltpu.VMEM((N_CHIPS * SHARD_PAD, D_MODEL), BF16), pltpu.VMEM((WINDOW_HALF, D_MODEL), BF16),
            pltpu.VMEM((SHARD_SHIFT, D_MODEL), BF16), pltpu.VMEM((seq, D_MODEL), BF16),
            pltpu.VMEM((MIX_WIDTH, D_MODEL), BF16), pltpu.VMEM((SHARD_OUT // 2, D_MODEL), BF16),
            pltpu.SemaphoreType.DMA((WINDOW_SEMS,)), pltpu.SemaphoreType.DMA((WINDOW_SEMS,)),
            pltpu.SemaphoreType.DMA((GATHER_SEMS,)), pltpu.SemaphoreType.DMA((GATHER_SEMS,)),
        ])
    chip = 2 * lax.axis_index("x") + lax.axis_index("y")
    owner = jnp.bitwise_xor(chip, jnp.arange(N_CHIPS, dtype=jnp.int32))
    return _call(
        body, name="in_proj", grid_spec=grid_spec,
        out_shape=(jax.ShapeDtypeStruct((N_CHIPS, seq, SHARD_PAD), BF16), jax.ShapeDtypeStruct((D_MODEL, seq), BF16),
                   jax.ShapeDtypeStruct((IN_WIDTH, D_MODEL), BF16), jax.ShapeDtypeStruct((MIX_WIDTH, D_MODEL), BF16)),
        compiler_params=_params(60, ("arbitrary", "arbitrary")),
    )(owner.astype(jnp.int32), x, norm_g, w_in_t_shard, w_out_shard)


def _mix_fwd(proj, x, target, w_out, final_g, gn_gain, sinks, tables, tb):
    seq = x.shape[0]
    nsub = tb // BLK
    cos_t, sin_t, decay_in, qdec_t, kdec_t, cd_t, bias_t = tables

    def body(p_ref, x_ref, t_ref, cos_ref, sin_ref, wout_ref, fg_ref, gain_ref, sinks_ref, din_ref, qdec_ref,
             kdec_ref, cd_ref, bias_ref, mix_ref, dxo_ref, st_ref, loss_ref, gfin_ref, oatt_ref, probs_ref, psink_ref,
             kprev_ref, vprev_ref, state_ref):
        i = pl.program_id(0)

        @pl.when(i == 0)
        def _():
            kprev_ref[...] = jnp.zeros_like(kprev_ref)
            vprev_ref[...] = jnp.zeros_like(vprev_ref)
            state_ref[...] = jnp.zeros_like(state_ref)
            loss_ref[...] = jnp.zeros_like(loss_ref)
            gfin_ref[...] = jnp.zeros_like(gfin_ref)

        prev = _in_previous_block()

        def sub(j, carry):
            kp, vp, states = carry
            rows = pl.ds(pl.multiple_of(j * BLK, BLK), BLK)
            bias = bias_ref[jnp.where(jnp.logical_or(i > 0, j > 0), 1, 0)]

            aq = _proj_cols(p_ref, rows, OFF_AQ, OFF_AQ + ATT_WIDTH)
            ak = _proj_cols(p_ref, rows, OFF_AK, OFF_AK + ATT_KV_WIDTH).astype(F32)
            av = _proj_cols(p_ref, rows, OFF_AV, OFF_AV + ATT_KV_WIDTH).astype(F32)
            az = _proj_cols(p_ref, rows, OFF_AZ, OFF_AZ + ATT_WIDTH).astype(F32)
            k_ops = _kv_operands(jnp.concatenate([kp, ak], axis=0))
            v_ops = _kv_operands(jnp.concatenate([vp, av], axis=0))
            qs = [(_stack_tiles(aq, 2 * g) * ATT_SCALE).astype(BF16) for g in range(ATT_KV_HEADS)]

            qr, kr, qd, kd = _retention_operands(p_ref, rows, cos_ref[rows, :], sin_ref[rows, :],
                                                 qdec_ref[...], kdec_ref[...])
            s = _attn_scores(qs, k_ops, bias, prev)
            heads = [(t, hh) for t in range(PAIRS) for hh in range(2)]
            sc = [_dot_nt(jnp.concatenate(_split_heads(_tile(qr, t)), axis=0).astype(BF16), _tile(kr, t).astype(BF16))
                  * din_ref[t] for t in range(PAIRS)]
            qd_heads = [_split_heads(_tile(qd, t)) for t in range(PAIRS)]
            state_b = [states[t].astype(BF16) for t in range(PAIRS)]
            vs = [_proj_cols(p_ref, rows, OFF_RV + h * RET_V_DIM, OFF_RV + (h + 1) * RET_V_DIM) for h in range(RET_HEADS)]
            rzs = [_proj_cols(p_ref, rows, OFF_RZ + h * RET_V_DIM, OFF_RZ + (h + 1) * RET_V_DIM).astype(F32)
                   for h in range(RET_HEADS)]
            lhs = [jnp.concatenate([sc[t][hh * BLK:(hh + 1) * BLK].astype(BF16), qd_heads[t][hh].astype(BF16)], axis=1)
                   for t, hh in heads]
            p, p_sink = _attn_softmax(s, sinks_ref)
            o_ret = [_dot(lhs[2 * t + hh], jnp.concatenate([vs[2 * t + hh], state_b[t]], axis=0)) for t, hh in heads]
            p = [pi.astype(BF16) for pi in p]
            o_tiles = []
            for g in range(ATT_KV_HEADS):
                p_cat = jnp.concatenate([_unfold(p[2 * g], prev), _unfold(p[2 * g + 1], prev)], axis=0)
                o = _dot_tn(p_cat, jnp.concatenate(v_ops[g], axis=0))
                o_tiles += [o[0:BLK], o[BLK:]]
            ons, _ = _group_norm_all(o_ret)
            new_states = [states[t] * cd_ref[t]
                          + _dot_tn(jnp.concatenate(_split_heads(_tile(kd, t)), axis=0).astype(BF16),
                                    jnp.concatenate([vs[2 * t], vs[2 * t + 1]], axis=0)) for t in range(PAIRS)]
            o_att = jnp.concatenate(o_tiles, axis=1)
            out = [o_att * (az * _sigmoid(az))]
            out += [(ons[h] * gain_ref[:, h * RET_V_DIM:(h + 1) * RET_V_DIM]) * (rzs[h] * _sigmoid(rzs[h]))
                    for h in range(RET_HEADS)]
            mix_ref[rows, :] = jnp.concatenate(out, axis=1).astype(BF16)
            oatt_ref[rows, :] = o_att
            probs_ref[j] = jnp.stack(p)
            psink_ref[j] = jnp.concatenate(p_sink, axis=0)
            st_ref[j] = jnp.stack(states)
            return ak, av, tuple(new_states)

        carry = (kprev_ref[...], vprev_ref[...], tuple(state_ref[t] for t in range(PAIRS)))
        for jb in range(nsub):
            carry = sub(jnp.int32(jb), carry)
        kp, vp, states = carry
        kprev_ref[...] = kp
        vprev_ref[...] = vp
        state_ref[...] = jnp.stack(states)

        xo = x_ref[...] + _dot(mix_ref[...], wout_ref[...])
        r2 = lax.rsqrt(jnp.mean(xo * xo, axis=1, keepdims=True) + RMS_EPS)
        xn = xo * r2
        err = xn * fg_ref[...] - t_ref[...]
        loss_ref[...] += jnp.sum(err * err) * (0.5 / D_MODEL)
        dy = err * (1.0 / D_MODEL)
        gfin_ref[...] += jnp.sum(dy * xn, axis=0, keepdims=True)
        u = dy * fg_ref[...]
        dxo_ref[...] = r2 * u - xn * (r2 * jnp.mean(u * xn, axis=1, keepdims=True))

    blk_rows = lambda w: pl.BlockSpec((tb, w), lambda i: (i, 0))
    state_shape = (PAIRS, 2 * RET_QK_DIM, RET_V_DIM)
    return _call(
        body, name="mix_fwd", grid=(seq // tb,),
        out_shape=(
            jax.ShapeDtypeStruct((seq, MIX_WIDTH), BF16),
            jax.ShapeDtypeStruct((seq, D_MODEL), F32),
            jax.ShapeDtypeStruct((seq // BLK,) + state_shape, F32),
            jax.ShapeDtypeStruct((8, LANE), F32),
            jax.ShapeDtypeStruct((1, D_MODEL), F32),
            jax.ShapeDtypeStruct((seq, ATT_WIDTH), F32),
            jax.ShapeDtypeStruct((seq // BLK, len(ATT_PROBLEMS), BLK, 2 * BLK), BF16),
            jax.ShapeDtypeStruct((seq // BLK, len(ATT_PROBLEMS), 2 * BLK), F32),
        ),
        in_specs=[
            pl.BlockSpec((N_CHIPS, tb, SHARD_PAD), lambda i: (0, i, 0)),
            blk_rows(D_MODEL), blk_rows(D_MODEL), blk_rows(LANE), blk_rows(LANE),
            _const_spec((MIX_WIDTH, D_MODEL)), _const_spec((1, D_MODEL)), _const_spec((1, RET_WIDTH)),
            pl.BlockSpec(memory_space=pltpu.SMEM),
            _const_spec((PAIRS, 2 * BLK, BLK)), _const_spec((BLK, RET_QK_WIDTH)), _const_spec((BLK, RET_QK_WIDTH)),
            _const_spec(state_shape), _const_spec((2, BLK, 2 * BLK)),
        ],
        out_specs=(
            blk_rows(MIX_WIDTH), blk_rows(D_MODEL),
            pl.BlockSpec((nsub,) + state_shape, lambda i: (i, 0, 0, 0)),
            _const_spec((8, LANE)), _const_spec((1, D_MODEL)), blk_rows(ATT_WIDTH),
            pl.BlockSpec((nsub, len(ATT_PROBLEMS), BLK, 2 * BLK), lambda i: (i, 0, 0, 0)),
            pl.BlockSpec((nsub, len(ATT_PROBLEMS), 2 * BLK), lambda i: (i, 0, 0)),
        ),
        scratch_shapes=[
            pltpu.VMEM((BLK, ATT_KV_WIDTH), F32), pltpu.VMEM((BLK, ATT_KV_WIDTH), F32),
            pltpu.VMEM(state_shape, F32),
        ],
        compiler_params=_params(48, ("arbitrary",)),
    )(proj, x, target, cos_t, sin_t, w_out, final_g, gn_gain, sinks, decay_in, qdec_t, kdec_t, cd_t, bias_t)


def _mix_bwd(proj, dxo, mix, o_att, probs, p_sinks, states, x, w_out, w_in_t, norm_g, gn_gain, tables, tb):
    seq = dxo.shape[0]
    nsub = tb // BLK
    nblk = seq // tb
    cos_t, sin_t, decay_in, qdec_t, kdec_t, cd_t, _ = tables
    kv_cols = OFF_AK // (2 * ATT_KV_WIDTH)
    state_shape = (PAIRS, 2 * RET_QK_DIM, RET_V_DIM)

    def body(p_ref, pkv_ref, dxo_ref, mix_ref, oatt_ref, probs_ref, psink_ref, st_ref, cos_ref, sin_ref, x_ref, wout_ref,
             win_ref, g_ref, gain_ref, din_ref, qdec_ref, kdec_ref, cd_ref,
             dp_ref, gx_ref, gwout_ref, gnorm_ref, dgain_ref, dsink_ref,
             dmix_ref, kv_ref, dkc_ref, dvc_ref, gst_ref):
        i = pl.program_id(0)

        @pl.when(i == 0)
        def _():
            gwout_ref[...] = jnp.zeros_like(gwout_ref)
            gnorm_ref[...] = jnp.zeros_like(gnorm_ref)
            dgain_ref[...] = jnp.zeros_like(dgain_ref)
            dsink_ref[...] = jnp.zeros_like(dsink_ref)
            dkc_ref[...] = jnp.zeros_like(dkc_ref)
            dvc_ref[...] = jnp.zeros_like(dvc_ref)
            gst_ref[...] = jnp.zeros_like(gst_ref)

        dxo_b = dxo_ref[...].astype(BF16)
        dmix_ref[...] = _dot_nt(dxo_b, wout_ref[...])
        gwout_ref[...] += _dot_tn(mix_ref[...], dxo_b)
        kv_ref[0:BLK, :] = pkv_ref[...].astype(F32)
        kv_ref[BLK:, :] = _proj_cols(p_ref, slice(None), OFF_AK, OFF_AK + 2 * ATT_KV_WIDTH).astype(F32)
        low = _low_lanes((BLK, LANE))
        low2 = _low_lanes((2 * BLK, LANE))
        lane = lax.broadcasted_iota(jnp.int32, (1, LANE), 1)
        prev = _in_previous_block()

        def sub(jj, carry):
            dkc, dvc, gsts, dgain, dsink = carry
            j = nsub - 1 - jj
            rows = pl.ds(pl.multiple_of(j * BLK, BLK), BLK)
            both = pl.ds(pl.multiple_of(j * BLK, BLK), 2 * BLK)


            aq = _proj_cols(p_ref, rows, OFF_AQ, OFF_AQ + ATT_WIDTH)
            az = _proj_cols(p_ref, rows, OFF_AZ, OFF_AZ + ATT_WIDTH).astype(F32)
            k_ops = _kv_operands(kv_ref[both, 0:ATT_KV_WIDTH])
            v_ops = _kv_operands(kv_ref[both, ATT_KV_WIDTH:2 * ATT_KV_WIDTH])
            da = dmix_ref[rows, 0:ATT_WIDTH]
            sig = _sigmoid(az)
            d_o = da * (az * sig)
            qs = [(_stack_tiles(aq, 2 * g) * ATT_SCALE).astype(BF16) for g in range(ATT_KV_HEADS)]
            dos = [_stack_tiles(d_o, 2 * g).astype(BF16) for g in range(ATT_KV_HEADS)]
            p_b = [probs_ref[j, k] for k in range(len(ATT_PROBLEMS))]
            p = [pk.astype(F32) for pk in p_b]
            sink_all = psink_ref[j]
            p_sink = [sink_all[k:k + 1, :] for k in range(len(ATT_PROBLEMS))]
            cos_b, sin_b = cos_ref[rows, :], sin_ref[rows, :]
            qdec, kdec = qdec_ref[...], kdec_ref[...]
            qr, kr, qd, kd = _retention_operands(p_ref, rows, cos_b, sin_b, qdec, kdec)
            heads = [(t, hh) for t in range(PAIRS) for hh in range(2)]
            head_cols = [slice(h * RET_V_DIM, (h + 1) * RET_V_DIM) for h in range(RET_HEADS)]
            q_rows = [jnp.concatenate(_split_heads(_tile(qr, t)), axis=0).astype(BF16) for t in range(PAIRS)]
            k_rows = [jnp.concatenate(_split_heads(_tile(kr, t)), axis=0).astype(BF16) for t in range(PAIRS)]
            din = [din_ref[t] for t in range(PAIRS)]

            dpr = [_fold(_dot_nt(v_ops[g][hi], dos[g]), prev) for g, hi in ATT_PROBLEMS]
            sc = [(_dot_nt(q_rows[t], _tile(kr, t).astype(BF16)) * din[t]).astype(BF16) for t in range(PAIRS)]
            qd_heads = [_split_heads(_tile(qd, t)) for t in range(PAIRS)]
            kd_heads = [_split_heads(_tile(kd, t)) for t in range(PAIRS)]
            state_b = [st_ref[j, t].astype(BF16) for t in range(PAIRS)]
            gst_b = [gsts[t].astype(BF16) for t in range(PAIRS)]
            vs = [_proj_cols(p_ref, rows, OFF_RV + h * RET_V_DIM, OFF_RV + (h + 1) * RET_V_DIM) for h in range(RET_HEADS)]
            rzs = [_proj_cols(p_ref, rows, OFF_RZ + h * RET_V_DIM, OFF_RZ + (h + 1) * RET_V_DIM).astype(F32)
                   for h in range(RET_HEADS)]
            drs = [dmix_ref[rows, ATT_WIDTH + h * RET_V_DIM:ATT_WIDTH + (h + 1) * RET_V_DIM] for h in range(RET_HEADS)]
            gains = [gain_ref[:, c] for c in head_cols]
            lhs = [jnp.concatenate([sc[t][hh * BLK:(hh + 1) * BLK], qd_heads[t][hh].astype(BF16)], axis=1) for t, hh in heads]
            rhs = [jnp.concatenate([vs[2 * t + hh], state_b[t]], axis=0) for t, hh in heads]

            delta = [jnp.sum(pi * di, axis=0, keepdims=True) for pi, di in zip(p, dpr)]
            ds = [_unfold((pi * (di - ti)).astype(BF16), prev) for pi, di, ti in zip(p, dpr, delta)]
            o_ret = [_dot(l, r) for l, r in zip(lhs, rhs)]
            for (g, hi), ki, ti in zip(ATT_PROBLEMS, p_sink, delta):
                sink_part = ki * ti
                for half in range(2):
                    tot = jnp.sum(sink_part[:, half * BLK:(half + 1) * BLK], axis=1, keepdims=True)
                    dsink = dsink - jnp.where(lane == 4 * g + 2 * half + hi, tot, 0.0)

            dq_tiles, dk_sums, dv_sums = [], [], []
            for g in range(ATT_KV_HEADS):
                ds_cat = jnp.concatenate([ds[2 * g], ds[2 * g + 1]], axis=0)
                p_cat = jnp.concatenate([_unfold(p_b[2 * g], prev), _unfold(p_b[2 * g + 1], prev)], axis=0)
                dqs = _dot_tn(ds_cat, jnp.concatenate(k_ops[g], axis=0)) * ATT_SCALE
                dq_tiles += [dqs[0:BLK], dqs[BLK:]]
                dk_sums.append(_dot(ds_cat, qs[g]))
                dv_sums.append(_dot(p_cat, dos[g]))
            ons, rstds = _group_norm_all(o_ret)
            sig_r = [_sigmoid(z) for z in rzs]
            dgn = [d * (z * g) for d, z, g in zip(drs, rzs, sig_r)]
            dz_parts = [d * (o * gn) * (g * (1.0 + z * (1.0 - g))) for d, o, gn, g, z in zip(drs, ons, gains, sig_r, rzs)]
            dgain_parts = [jnp.sum(d * o, axis=0, keepdims=True) for d, o in zip(dgn, ons)]
            don = [d * gn for d, gn in zip(dgn, gains)]
            mean_don = [jnp.mean(d, axis=1, keepdims=True) for d in don]
            mean_don_on = [jnp.mean(d * o, axis=1, keepdims=True) for d, o in zip(don, ons)]
            dob = [(r * (d - a - o * b)).astype(BF16) for r, d, a, o, b in zip(rstds, don, mean_don, ons, mean_don_on)]

            dlhs = [_dot_nt(d, r) for d, r in zip(dob, rhs)]
            drhs = [_dot_tn(l, d) for l, d in zip(lhs, dob)]
            dkds = [_dot_nt(vs[2 * t + hh], gst_b[t]) for t, hh in heads]
            dv_parts = [drhs[2 * t + hh][0:BLK] + _dot(kd_heads[t][hh].astype(BF16), gst_b[t]) for t, hh in heads]
            daz = da * oatt_ref[rows, :] * (sig * (1.0 + az * (1.0 - sig)))

            def kv_grad(sums):
                (a0, b0), (a1, b1) = [(s[0:2 * BLK], s[2 * BLK:]) for s in sums]
                return jnp.where(low2, a0, b1) + pltpu.roll(jnp.where(low2, a1, b0), HALF_LANE, 1)

            dk_both, dv_both = kv_grad(dk_sums), kv_grad(dv_sums)
            dak = dk_both[BLK:] + dkc
            dav = dv_both[BLK:] + dvc
            das = [(dlhs[2 * t + hh][:, 0:BLK] * din[t][hh * BLK:(hh + 1) * BLK]).astype(BF16) for t, hh in heads]
            new_gsts = [gsts[t] * cd_ref[t] + drhs[2 * t][BLK:] + drhs[2 * t + 1][BLK:] for t in range(PAIRS)]
            dq_parts = [_dot(jnp.concatenate([das[2 * t], das[2 * t + 1]], axis=1), k_rows[t])
                        + jnp.where(low, dlhs[2 * t][:, BLK:], dlhs[2 * t + 1][:, BLK:]) * _tile(qdec, t)
                        for t in range(PAIRS)]
            dk_parts = [_dot_tn(jnp.concatenate([das[2 * t], das[2 * t + 1]], axis=0), q_rows[t])
                        + jnp.where(low, dkds[2 * t], dkds[2 * t + 1]) * _tile(kdec, t) for t in range(PAIRS)]
            drq = _rotate_transposed(jnp.concatenate(dq_parts, axis=1), cos_b, sin_b)
            drk = _rotate_transposed(jnp.concatenate(dk_parts, axis=1) * RET_SCALE, cos_b, sin_b)

            dp_ref[rows, :] = jnp.concatenate(
                [jnp.concatenate(dq_tiles, axis=1), dak, dav, daz, drq, drk] + dv_parts + dz_parts, axis=1).astype(BF16)
            dgain = dgain + jnp.concatenate(dgain_parts, axis=1)
            return dk_both[0:BLK], dv_both[0:BLK], tuple(new_gsts), dgain, dsink

        carry = (dkc_ref[...], dvc_ref[...], tuple(gst_ref[t] for t in range(PAIRS)), dgain_ref[...], dsink_ref[...])
        for jb in range(nsub):
            carry = sub(jnp.int32(jb), carry)
        dkc, dvc, gsts, dgain, dsink = carry
        dkc_ref[...] = dkc
        dvc_ref[...] = dvc
        gst_ref[...] = jnp.stack(gsts)
        dgain_ref[...] = dgain
        dsink_ref[...] = dsink

        dh = _dot(dp_ref[...], win_ref[...])
        xv = x_ref[...]
        r = lax.rsqrt(jnp.mean(xv * xv, axis=1, keepdims=True) + RMS_EPS)
        xn = xv * r
        gnorm_ref[...] += jnp.sum(dh * xn, axis=0, keepdims=True)
        u = dh * g_ref[...]
        gx_ref[...] = dxo_ref[...] + r * u - xn * (r * jnp.mean(u * xn, axis=1, keepdims=True))

    rev_rows = lambda w: pl.BlockSpec((tb, w), lambda i: (nblk - 1 - i, 0))
    prev_kv = pl.BlockSpec((None, BLK, 2 * ATT_KV_WIDTH),
                           lambda i: (0, jnp.maximum((nblk - 1 - i) * nsub - 1, 0), kv_cols))
    return _call(
        body, name="mix_bwd", grid=(nblk,),
        out_shape=(
            jax.ShapeDtypeStruct((seq, IN_WIDTH), BF16),
            jax.ShapeDtypeStruct((seq, D_MODEL), F32),
            jax.ShapeDtypeStruct((MIX_WIDTH, D_MODEL), F32),
            jax.ShapeDtypeStruct((1, D_MODEL), F32),
            jax.ShapeDtypeStruct((1, RET_WIDTH), F32),
            jax.ShapeDtypeStruct((1, LANE), F32),
        ),
        in_specs=[
            pl.BlockSpec((N_CHIPS, tb, SHARD_PAD), lambda i: (0, nblk - 1 - i, 0)),
            prev_kv, rev_rows(D_MODEL), rev_rows(MIX_WIDTH), rev_rows(ATT_WIDTH),
            pl.BlockSpec((nsub, len(ATT_PROBLEMS), BLK, 2 * BLK), lambda i: (nblk - 1 - i, 0, 0, 0)),
            pl.BlockSpec((nsub, len(ATT_PROBLEMS), 2 * BLK), lambda i: (nblk - 1 - i, 0, 0)),
            pl.BlockSpec((nsub,) + state_shape, lambda i: (nblk - 1 - i, 0, 0, 0)),
            rev_rows(LANE), rev_rows(LANE), rev_rows(D_MODEL),
            _const_spec((MIX_WIDTH, D_MODEL)), _const_spec((IN_WIDTH, D_MODEL)), _const_spec((1, D_MODEL)),
            _const_spec((1, RET_WIDTH)),
            _const_spec((PAIRS, 2 * BLK, BLK)), _const_spec((BLK, RET_QK_WIDTH)), _const_spec((BLK, RET_QK_WIDTH)),
            _const_spec(state_shape),
        ],
        out_specs=(
            rev_rows(IN_WIDTH), rev_rows(D_MODEL), _const_spec((MIX_WIDTH, D_MODEL)), _const_spec((1, D_MODEL)),
            _const_spec((1, RET_WIDTH)), _const_spec((1, LANE)),
        ),
        scratch_shapes=[
            pltpu.VMEM((tb, MIX_WIDTH), F32),
            pltpu.VMEM((tb + BLK, 2 * ATT_KV_WIDTH), F32),
            pltpu.VMEM((BLK, ATT_KV_WIDTH), F32), pltpu.VMEM((BLK, ATT_KV_WIDTH), F32),
            pltpu.VMEM(state_shape, F32),
        ],
        compiler_params=_params(60, ("arbitrary",)),
    )(proj, proj, dxo, mix, o_att, probs, p_sinks, states, cos_t, sin_t, x, w_out, w_in_t, norm_g, gn_gain, decay_in,
      qdec_t, kdec_t, cd_t)


def _gw_in_reduce(h_t, dproj, gw_out, small, tb):
    seq = dproj.shape[0]
    nblk = seq // tb
    last = nblk - 1
    hand_on = min(1, last)
    half = D_MODEL // 2
    A, B, C, N_SEMS = 0, N_CHIPS, 2 * N_CHIPS, 2 * N_CHIPS + 1

    def body(win_ref, ht_hbm, dp_ref, gwo_hbm, s0_ref, s1_ref, s2_ref, s3_ref, s4_ref, out_ref, fout_ref, packsum_ref,
             h_all, acc, sib, send_buf, b_in, fin, fout, mine_out, sib_out, send_out, b_out, pack_ref, packs,
             send_sems, recv_sems, local_sems, h_sems):
        p, i = pl.program_id(0), pl.program_id(1)
        small_start, small_hand_on, small_finish = _small_exchange(
            gwo_hbm, (s0_ref, s1_ref, s2_ref, s3_ref, s4_ref), fout, mine_out, sib_out, send_out, b_out, pack_ref,
            packs, send_sems, recv_sems, local_sems, N_SEMS)

        @pl.when(jnp.logical_and(p == 0, i == 0))
        def _():
            small_start()

        @pl.when(jnp.logical_and(p == 0, i == hand_on))
        def _():
            small_hand_on()

        x, y, c = lax.axis_index("x"), lax.axis_index("y"), lax.axis_index("c")
        chip = 2 * x + y
        sibling = (x, y, 1 - c)
        mine = pl.ds(pl.multiple_of(c * half, half), half)
        other = pl.ds(pl.multiple_of((1 - c) * half, half), half)

        def remote(src, dst, send_k, recv_k, to):
            return pltpu.make_async_remote_copy(src_ref=src, dst_ref=dst, send_sem=send_sems.at[send_k],
                                                recv_sem=recv_sems.at[recv_k], device_id=to, device_id_type=MESH_ID)

        slot = p % 2

        @pl.when(i == 0)
        def _():
            acc[slot] = jnp.zeros(acc.shape[1:], F32)

        loads = [pltpu.make_async_copy(ht_hbm.at[:, pl.ds(j * tb, tb)], h_all.at[:, pl.ds(j * tb, tb)], h_sems.at[j])
                 for j in range(nblk)]

        @pl.when(jnp.logical_and(p == 0, i == 0))
        def _():
            for load in loads:
                load.start()

        for j, load in enumerate(loads):
            @pl.when(jnp.logical_and(p == 0, i == j))
            def _():
                load.wait()

        acc[slot] += _dot(h_all[:, pl.ds(pl.multiple_of(i * tb, tb), tb)], dp_ref[...])

        for q in range(N_CHIPS):
            s = q % 2
            to_sibling = remote(acc.at[s, other, :], sib.at[s], A + q, A + q, sibling)

            @pl.when(jnp.logical_and(p == q, i == last))
            def _():
                to_sibling.start()

            if q < N_CHIPS - 1:
                dest = (chip + 1 + q) % N_CHIPS

                @pl.when(jnp.logical_and(p == q + 1, i == hand_on))
                def _():
                    to_sibling.wait_recv()
                    send_buf[q] = (acc[s, mine, :] + sib[s]).astype(BF16)
                    remote(send_buf.at[q], b_in.at[chip], B + q, B + chip, (dest // 2, dest % 2, c)).start()
                    to_sibling.wait_send()
            else:
                @pl.when(jnp.logical_and(p == q, i == last))
                def _():
                    to_sibling.wait_recv()
                    fin[mine, :] = acc[s, mine, :] + sib[s]
                    for j in range(N_CHIPS):
                        @pl.when(j != chip)
                        def _():
                            remote(b_in.at[j], b_in.at[j], B + j, B + j, sibling).wait_recv()
                            fin[mine, :] += b_in[j].astype(F32)
                    to_core = remote(fin.at[mine, :], fin.at[mine, :], C, C, sibling)
                    to_core.start()
                    remote(fin.at[other, :], fin.at[other, :], C, C, sibling).wait_recv()
                    out_ref[...] = fin[...]
                    to_core.wait_send()
                    to_sibling.wait_send()
                    for k in range(N_CHIPS - 1):
                        remote(send_buf.at[k], b_in.at[chip], B + k, B + k, sibling).wait_send()
                    packsum_ref[...] = small_finish()
                    fout_ref[...] = fout[...]

    whole = lambda shape: pl.BlockSpec(shape, lambda p, i, win: (0,) * len(shape), pipeline_mode=pl.Buffered(1))
    grid_spec = pltpu.PrefetchScalarGridSpec(
        num_scalar_prefetch=1, grid=(N_CHIPS, nblk),
        in_specs=[pl.BlockSpec(memory_space=pl.ANY),
                  pl.BlockSpec((pl.Element(tb), pl.Element(SHARD_PAD)),
                               lambda p, i, win: (i * tb, pl.multiple_of(win[p] * LANE, LANE))),
                  pl.BlockSpec(memory_space=pl.ANY)] + [whole(s.shape) for s in small],
        out_specs=(whole((D_MODEL, SHARD_PAD)), whole((SHARD_OUT, D_MODEL)), whole((PACK_ROWS, D_MODEL))),
        scratch_shapes=[
            pltpu.VMEM((D_MODEL, seq), BF16),
            pltpu.VMEM((2, D_MODEL, SHARD_PAD), F32), pltpu.VMEM((2, half, SHARD_PAD), F32),
            pltpu.VMEM((N_CHIPS - 1, half, SHARD_PAD), BF16), pltpu.VMEM((N_CHIPS, half, SHARD_PAD), BF16),
            pltpu.VMEM((D_MODEL, SHARD_PAD), F32),
        ] + _small_exchange_scratch() + [
            pltpu.SemaphoreType.DMA((N_SEMS + SMALL_SEMS,)), pltpu.SemaphoreType.DMA((N_SEMS + SMALL_SEMS,)),
            pltpu.SemaphoreType.DMA((N_CHIPS,)), pltpu.SemaphoreType.DMA((nblk,)),
        ])
    chip = 2 * lax.axis_index("x") + lax.axis_index("y")
    owner = (chip + 1 + jnp.arange(N_CHIPS, dtype=jnp.int32)) % N_CHIPS
    win_start = (owner * SHARD_IN) // LANE
    return _call(
        body, name="gw_in_reduce", grid_spec=grid_spec,
        out_shape=(jax.ShapeDtypeStruct((D_MODEL, SHARD_PAD), F32), jax.ShapeDtypeStruct((SHARD_OUT, D_MODEL), F32),
                   jax.ShapeDtypeStruct((PACK_ROWS, D_MODEL), F32)),
        compiler_params=_params(60, ("arbitrary", "arbitrary")),
    )(win_start.astype(jnp.int32), h_t, dproj, gw_out, *small)


SMALL_SEMS = 17


def _small_exchange(gwo_hbm, small_refs, fout_ref, mine_out, sib_out, send_out, b_out, pack_ref, packs,
                    send_sems, recv_sems, local_sems, base):
    half_out = SHARD_OUT // 2
    A_OUT, B_OUT, C_OUT, PACK = base, base + 4, base + 8, base + 9
    assert len(small_refs) == PACK_PARTS
    x, y, c = lax.axis_index("x"), lax.axis_index("y"), lax.axis_index("c")
    chip = 2 * x + y
    dev = 2 * chip + c
    sibling = (x, y, 1 - c)

    def remote(src, dst, send_k, recv_k, to):
        return pltpu.make_async_remote_copy(src_ref=src, dst_ref=dst, send_sem=send_sems.at[send_k],
                                            recv_sem=recv_sems.at[recv_k], device_id=to, device_id_type=MESH_ID)

    def out_rows(j, core):
        return pl.ds(pl.multiple_of(j * SHARD_OUT + core * half_out, half_out), half_out)

    my_out_rows = pl.ds(pl.multiple_of(c * half_out, half_out), half_out)
    local = [pltpu.make_async_copy(gwo_hbm.at[out_rows(j, c), :], mine_out.at[j], local_sems.at[j])
             for j in range(N_CHIPS)]
    stage_a = [remote(gwo_hbm.at[out_rows(j, 1 - c), :], sib_out.at[j], A_OUT + j, A_OUT + j, sibling)
               for j in range(N_CHIPS)]
    mine_half_out = fout_ref.at[my_out_rows, :]
    stage_c = [remote(mine_half_out, mine_half_out, C_OUT, C_OUT, sibling)]

    def start():
        pack_ref[...] = jnp.zeros_like(pack_ref)
        for k, s_ref in enumerate(small_refs):
            pack_ref[k:k + 1, 0:s_ref.shape[1]] = s_ref[0:1, :]
        packs[dev] = pack_ref[...]
        for d in range(N_DEV):
            to = (d // 4, (d // 2) % 2, d % 2)

            @pl.when(d != dev)
            def _():
                remote(pack_ref, packs.at[dev], PACK + d, PACK + dev, to).start()

        for cp in local + stage_a:
            cp.start()

    def hand_on():
        for cp in local:
            cp.wait()
        for cp in stage_a:
            cp.wait_recv()
        for j in range(N_CHIPS):
            mine_out[j] = mine_out[j] + sib_out[j]

        for j in range(N_CHIPS):
            to = (j // 2, j % 2, c)

            @pl.when(j != chip)
            def _():
                send_out[j] = mine_out[j].astype(BF16)
                remote(send_out.at[j], b_out.at[chip], B_OUT + j, B_OUT + chip, to).start()

            @pl.when(j == chip)
            def _():
                fout_ref[my_out_rows, :] = mine_out[j]

    def finish():
        for j in range(N_CHIPS):
            @pl.when(j != chip)
            def _():
                remote(b_out.at[j], b_out.at[j], B_OUT + j, B_OUT + j, sibling).wait_recv()
                fout_ref[my_out_rows, :] += b_out[j].astype(F32)

        for cp in stage_c:
            cp.start()
        other_half_out = fout_ref.at[pl.ds(pl.multiple_of((1 - c) * half_out, half_out), half_out), :]
        remote(other_half_out, other_half_out, C_OUT, C_OUT, sibling).wait_recv()

        for d in range(N_DEV):
            @pl.when(d != dev)
            def _():
                remote(pack_ref, packs.at[d], PACK + d, PACK + d, sibling).wait_recv()
        total = packs[0]
        for d in range(1, N_DEV):
            total = total + packs[d]

        for cp in stage_a + stage_c:
            cp.wait_send()
        for j in range(N_CHIPS):
            @pl.when(j != chip)
            def _():
                remote(b_out.at[j], b_out.at[j], B_OUT + j, B_OUT + j, sibling).wait_send()
        for d in range(N_DEV):
            @pl.when(d != dev)
            def _():
                remote(pack_ref, packs.at[d], PACK + d, PACK + d, sibling).wait_send()
        return total

    return start, hand_on, finish


def _small_exchange_scratch():
    half_out = SHARD_OUT // 2
    return [
        pltpu.VMEM((SHARD_OUT, D_MODEL), F32),
        pltpu.VMEM((N_CHIPS, half_out, D_MODEL), F32), pltpu.VMEM((N_CHIPS, half_out, D_MODEL), F32),
        pltpu.VMEM((N_CHIPS, half_out, D_MODEL), BF16), pltpu.VMEM((N_CHIPS, half_out, D_MODEL), BF16),
        pltpu.VMEM((PACK_ROWS, D_MODEL), F32), pltpu.VMEM((N_DEV, PACK_ROWS, D_MODEL), F32),
    ]


def _adam_math(w, g, m, v):
    mn = ADAM_B1 * m + (1.0 - ADAM_B1) * g
    vn = ADAM_B2 * v + (1.0 - ADAM_B2) * (g * g)
    m_hat = mn / (1.0 - ADAM_B1 ** ADAM_STEP)
    v_hat = vn / (1.0 - ADAM_B2 ** ADAM_STEP)
    return -ADAM_LR * (m_hat / (jnp.sqrt(v_hat) + ADAM_EPS) + ADAM_WD * w), mn, vn


def _adamw(name, w, g, m, v, tb):
    rows, cols = w.shape

    def body(w_ref, g_ref, m_ref, v_ref, go_ref, d_ref, mo_ref, vo_ref):
        gv = g_ref[...]
        go_ref[...] = gv
        d_ref[...], mo_ref[...], vo_ref[...] = _adam_math(w_ref[...], gv, m_ref[...], v_ref[...])

    spec = pl.BlockSpec((tb, cols), lambda i: (i, 0))
    shape = jax.ShapeDtypeStruct((rows, cols), F32)
    return _call(
        body, name=name, grid=(rows // tb,), out_shape=(shape,) * 4,
        in_specs=[spec] * 4, out_specs=(spec,) * 4,
        compiler_params=_params(32, ("arbitrary",)),
    )(w, g, m, v)


def _adamw_w_in(w_t, g_window, m_t, v_t, tb):
    def body(w_ref, g_ref, m_ref, v_ref, go_ref, d_ref, mo_ref, vo_ref, gt_ref):
        gt_ref[...] = g_ref[...].T
        gv = gt_ref[pl.ds(pl.multiple_of(lax.axis_index("y") * SHARD_SHIFT, SHARD_SHIFT), SHARD_IN), :]
        go_ref[...] = gv
        d_ref[...], mo_ref[...], vo_ref[...] = _adam_math(w_ref[...], gv, m_ref[...], v_ref[...])

    spec = pl.BlockSpec((SHARD_IN, tb), lambda i: (0, i))
    shape = jax.ShapeDtypeStruct((SHARD_IN, D_MODEL), F32)
    return _call(
        body, name="adamw_w_in", grid=(D_MODEL // tb,), out_shape=(shape,) * 4,
        in_specs=[spec, pl.BlockSpec((tb, SHARD_PAD), lambda i: (i, 0)), spec, spec], out_specs=(spec,) * 4,
        scratch_shapes=[pltpu.VMEM((SHARD_PAD, tb), F32)],
        compiler_params=_params(32, ("arbitrary",)),
    )(w_t, g_window, m_t, v_t)


def _adamw_small(sums, params):
    def body(sums_ref, *refs):
        ins, outs = refs[:3 * len(params)], refs[3 * len(params):]
        for k in range(len(params)):
            w_ref, m_ref, v_ref = ins[3 * k:3 * k + 3]
            g = sums_ref[k:k + 1, 0:w_ref.shape[1]]
            go_ref, d_ref, mo_ref, vo_ref = outs[4 * k:4 * k + 4]
            go_ref[...] = g
            d_ref[...], mo_ref[...], vo_ref[...] = _adam_math(w_ref[...], g, m_ref[...], v_ref[...])

    vmem = pl.BlockSpec(memory_space=pltpu.VMEM)
    flat = [a for p in params for a in p]
    shapes = tuple(jax.ShapeDtypeStruct(p[0].shape, F32) for p in params for _ in range(4))
    res = _call(body, name="adamw_small", out_shape=shapes, in_specs=[vmem] * (1 + len(flat)),
                out_specs=(vmem,) * len(shapes), compiler_params=_params(16))(sums, *flat)
    return [res[4 * k:4 * k + 4] for k in range(len(params))]


def kernel(x, norm_g, w_in, att_sinks, ret_gn_g, w_out, final_g, loss_target, m_norm_g, m_w_in, m_att_sinks, m_ret_gn_g, m_w_out, m_final_g, v_norm_g, v_w_in, v_att_sinks, v_ret_gn_g, v_w_out, v_final_g):
    seq = x.shape[1]
    xs, tgt = x[0], loss_target[0]
    final_g2 = final_g.reshape(1, D_MODEL)
    tables = _tables(seq)

    w_in_t, m_w_in_t, v_w_in_t = w_in[0].T, m_w_in[0].T, v_w_in[0].T
    proj, h_t, w_in_full, w_out_full = _in_proj(xs, norm_g, w_in_t, w_out[0], min(TOKENS_PROJ, seq))
    mix, dxo, states, loss_part, gfin, o_att, probs, p_sinks = _mix_fwd(
        proj, xs, tgt, w_out_full, final_g2, ret_gn_g, att_sinks, tables, min(TOKENS_MIX, seq))
    dproj, grad_x, gw_out, gnorm, dgain, dsink = _mix_bwd(
        proj, dxo, mix, o_att, probs, p_sinks, states, xs, w_out_full, w_in_full, norm_g, ret_gn_g, tables,
        min(TOKENS_MIX, seq))
    g_in, g_out, sums = _gw_in_reduce(h_t, dproj, gw_out, (gnorm, gfin, dgain, dsink, loss_part),
                                      min(TOKENS_GW, seq))

    res_in = [r.T for r in _adamw_w_in(w_in_t, g_in, m_w_in_t, v_w_in_t, 256)]
    res_out = _adamw("adamw_w_out", w_out[0], g_out, m_w_out[0], v_w_out[0], SHARD_OUT)
    as_row = lambda a: a.reshape(1, D_MODEL)
    r_norm, r_final, r_gain, r_sink = _adamw_small(sums, [
        (norm_g, m_norm_g, v_norm_g), (final_g2, as_row(m_final_g), as_row(v_final_g)),
        (ret_gn_g, m_ret_gn_g, v_ret_gn_g), (att_sinks, m_att_sinks, v_att_sinks)])

    outs = []
    for k in range(4):
        outs += [r_norm[k], res_in[k][None], r_sink[k], r_gain[k], res_out[k][None], r_final[k].reshape(D_MODEL)]
    return (sums[4, 0], grad_x[None], *outs)
```

```python
import jax
import jax.numpy as jnp
import numpy as np
from jax import lax
from jax.experimental import pallas as pl
from jax.experimental.pallas import tpu as pltpu

F32 = jnp.float32
BF16 = jnp.bfloat16

D_MODEL = 1024
ATT_HEADS = 8
ATT_KV_HEADS = 2
ATT_HEAD_DIM = 64
RET_HEADS = 4
RET_QK_DIM = 64
RET_V_DIM = 128
BLK = 128
ROT_BASE = 10000.0
RMS_EPS = 1e-6
GN_EPS = 1e-6
NEG_INF = -1e30
ATT_SCALE = ATT_HEAD_DIM ** -0.5
RET_SCALE = RET_QK_DIM ** -0.5

ATT_WIDTH = ATT_HEADS * ATT_HEAD_DIM
ATT_KV_WIDTH = ATT_KV_HEADS * ATT_HEAD_DIM
RET_QK_WIDTH = RET_HEADS * RET_QK_DIM
RET_WIDTH = RET_HEADS * RET_V_DIM
MIX_WIDTH = ATT_WIDTH + RET_WIDTH
OFF_AQ = 0
OFF_AK = OFF_AQ + ATT_WIDTH
OFF_AV = OFF_AK + ATT_KV_WIDTH
OFF_AZ = OFF_AV + ATT_KV_WIDTH
OFF_RQ = OFF_AZ + ATT_WIDTH
OFF_RK = OFF_RQ + RET_QK_WIDTH
OFF_RV = OFF_RK + RET_QK_WIDTH
OFF_RZ = OFF_RV + RET_WIDTH
IN_WIDTH = OFF_RZ + RET_WIDTH

LANE = 128
BF16_ROWS = 16
HALF_LANE = LANE // 2
PAIRS = RET_QK_WIDTH // LANE
assert ATT_HEAD_DIM == HALF_LANE and RET_QK_DIM == HALF_LANE and RET_V_DIM == LANE and ATT_KV_WIDTH == LANE

N_CHIPS = 4
N_DEV = 8
SHARD_IN = IN_WIDTH // N_CHIPS
SHARD_PAD = 768
SHARD_SHIFT = SHARD_PAD - SHARD_IN
WIN_START = tuple((j * SHARD_IN) // LANE * LANE for j in range(N_CHIPS))
SHARD_OUT = MIX_WIDTH // N_CHIPS

PACK_PARTS = 5
PACK_ROWS = 8
assert PACK_PARTS <= PACK_ROWS

ADAM_LR = 0.001
ADAM_B1 = 0.9
ADAM_B2 = 0.999
ADAM_EPS = 1e-08
ADAM_WD = 0.01
ADAM_STEP = 10

VMEM_CAP = 64 * 1024 * 1024
TOKENS_PROJ = 1024
TOKENS_MIX = 512
TOKENS_GW = 2048
MESH_ID = pl.DeviceIdType.MESH


def _call(body, **kw):
    return pl.pallas_call(body, **kw)


def _params(vmem_mb, semantics=None):
    assert vmem_mb * 1024 * 1024 < VMEM_CAP
    return pltpu.CompilerParams(dimension_semantics=semantics, vmem_limit_bytes=vmem_mb * 1024 * 1024)


def _dot(a, b):
    return jnp.dot(a, b, preferred_element_type=F32)


def _dot_nt(a, b):
    return lax.dot_general(a, b, (((1,), (1,)), ((), ())), preferred_element_type=F32)


def _dot_tn(a, b):
    return lax.dot_general(a, b, (((0,), (0,)), ((), ())), preferred_element_type=F32)


def _sigmoid(z):
    return 1.0 / (1.0 + jnp.exp(-z))


def _const_spec(shape):
    nd = len(shape)
    return pl.BlockSpec(shape, lambda i: (0,) * nd, pipeline_mode=pl.Buffered(1))


def _tables(seq):
    f32 = np.float32
    pos = np.arange(seq, dtype=f32)
    theta = (f32(1.0) / (f32(ROT_BASE) ** np.linspace(0.0, 1.0, RET_QK_DIM // 2, dtype=f32))).astype(f32)
    ang = (pos[:, None] * theta[None, :]).astype(f32)
    cos, sin = np.cos(ang), np.sin(ang)
    cos2 = np.repeat(cos, 2, axis=1)
    sin2 = np.stack([-sin, sin], axis=-1).reshape(seq, RET_QK_DIM)
    cos_t = np.tile(cos2, (1, 2))
    sin_t = np.tile(sin2, (1, 2))

    log_gamma = np.log(f32(1.0) - f32(2.0) ** (f32(-5.0) - np.arange(RET_HEADS, dtype=f32))).astype(f32)
    idx = np.arange(BLK, dtype=f32)
    rel = idx[:, None] - idx[None, :]
    decay_in = np.where(rel >= 0, np.exp(log_gamma[:, None, None] * np.maximum(rel, f32(0.0))), f32(0.0))
    k_dec = np.exp(log_gamma[:, None] * (BLK - 1 - idx)[None, :])
    q_dec = np.exp(log_gamma[:, None] * (idx + 1)[None, :])
    chunk_decay = np.exp(log_gamma * f32(BLK))
    kdec_t = np.repeat(k_dec.T, RET_QK_DIM, axis=1)
    qdec_t = np.repeat(q_dec.T, RET_QK_DIM, axis=1)
    cd_t = np.broadcast_to(chunk_decay[:, None, None], (RET_HEADS, RET_QK_DIM, RET_V_DIM))
    decay_in = decay_in.reshape(PAIRS, 2 * BLK, BLK)
    cd_t = cd_t.reshape(PAIRS, 2 * RET_QK_DIM, RET_V_DIM)

    key = np.arange(BLK)[:, None]
    query = np.arange(2 * BLK)[None, :] % BLK
    bias = np.stack([np.where(key > query, NEG_INF, 0.0), np.zeros((BLK, 2 * BLK))])
    return tuple(jnp.asarray(np.ascontiguousarray(a), F32) for a in (cos_t, sin_t, decay_in, qdec_t, kdec_t, cd_t, bias))


def _low_lanes(shape):
    lane = lax.broadcasted_iota(jnp.int32, shape, len(shape) - 1)
    return (lane & HALF_LANE) == 0


def _split_heads(t):
    low = _low_lanes(t.shape)
    zero = jnp.zeros_like(t)
    return jnp.where(low, t, zero), jnp.where(low, zero, t)


def _swap_pairs(t):
    lane = lax.broadcasted_iota(jnp.int32, t.shape, 1)
    nxt = pltpu.roll(t, t.shape[1] - 1, 1)
    prv = pltpu.roll(t, 1, 1)
    return jnp.where((lane & 1) == 0, nxt, prv)


def _per_tile(fn, t):
    return jnp.concatenate([fn(_tile(t, i)) for i in range(t.shape[1] // LANE)], axis=1)


def _rotate(t, cos_t, sin_t):
    return _per_tile(lambda a: a * cos_t + _swap_pairs(a) * sin_t, t)


def _rotate_transposed(d, cos_t, sin_t):
    return _per_tile(lambda a: a * cos_t + _swap_pairs(a * sin_t), d)


def _kv_operands(cat):
    low = _low_lanes(cat.shape)
    swapped = pltpu.roll(cat, HALF_LANE, 1)
    zero = jnp.zeros_like(cat)
    pick = lambda a, b: jnp.where(low, a, b).astype(BF16)
    return ((pick(cat, zero), pick(zero, swapped)), (pick(swapped, zero), pick(zero, cat)))


def _stack_tiles(t, first_tile):
    a = t[:, first_tile * LANE:(first_tile + 1) * LANE]
    b = t[:, (first_tile + 1) * LANE:(first_tile + 2) * LANE]
    return jnp.concatenate([a, b], axis=0)


def _sink_rows(sinks_ref, group):
    first = lax.broadcasted_iota(jnp.int32, (1, 2 * BLK), 1) < BLK

    def row(h0, h1):
        return jnp.where(first, sinks_ref[0, group * 4 + h0], sinks_ref[0, group * 4 + h1])
    return row(0, 2), row(1, 3)


ATT_PROBLEMS = tuple((g, hi) for g in range(ATT_KV_HEADS) for hi in range(2))


def _in_previous_block():
    key = lax.broadcasted_iota(jnp.int32, (BLK, 2 * BLK), 0)
    query = lax.broadcasted_iota(jnp.int32, (BLK, 2 * BLK), 1) & (BLK - 1)
    return key > query


def _fold(t, prev):
    return jnp.where(prev, t[0:BLK], t[BLK:])


def _unfold(t, prev):
    zero = jnp.zeros_like(t)
    return jnp.concatenate([jnp.where(prev, t, zero), jnp.where(prev, zero, t)], axis=0)


def _attn_scores(qs, k_ops, bias, prev):
    return [_fold(_dot_nt(k_ops[g][hi], qs[g]), prev) + bias for g, hi in ATT_PROBLEMS]


def _attn_softmax(s, sinks_ref):
    sink = [_sink_rows(sinks_ref, g)[hi] for g, hi in ATT_PROBLEMS]
    m = [jnp.maximum(jnp.max(si, axis=0, keepdims=True), ki) for si, ki in zip(s, sink)]
    e = [jnp.exp(si - mi) for si, mi in zip(s, m)]
    es = [jnp.exp(ki - mi) for ki, mi in zip(sink, m)]
    inv = [1.0 / (jnp.sum(ei, axis=0, keepdims=True) + esi) for ei, esi in zip(e, es)]
    return [ei * ii for ei, ii in zip(e, inv)], [esi * ii for esi, ii in zip(es, inv)]


def _group_norm_all(outs):
    mu = [jnp.mean(o, axis=1, keepdims=True) for o in outs]
    xc = [o - m for o, m in zip(outs, mu)]
    var = [jnp.mean(c * c, axis=1, keepdims=True) for c in xc]
    rstd = [lax.rsqrt(v + GN_EPS) for v in var]
    return [c * r for c, r in zip(xc, rstd)], rstd


def _proj_cols(p_ref, rows, start, stop):
    pieces = []
    while start < stop:
        k = max(j for j in range(N_CHIPS) if WIN_START[j] <= start)
        end = min(stop, WIN_START[k] + SHARD_PAD)
        pieces.append(p_ref[k, rows, start - WIN_START[k]:end - WIN_START[k]])
        start = end
    return pieces[0] if len(pieces) == 1 else jnp.concatenate(pieces, axis=1)


def _retention_operands(p_ref, rows, cos_b, sin_b, qdec, kdec):
    qr = _rotate(_proj_cols(p_ref, rows, OFF_RQ, OFF_RQ + RET_QK_WIDTH).astype(F32), cos_b, sin_b)
    kr = _rotate(_proj_cols(p_ref, rows, OFF_RK, OFF_RK + RET_QK_WIDTH).astype(F32), cos_b, sin_b) * RET_SCALE
    return qr, kr, qr * qdec, kr * kdec


def _tile(t, i):
    return t[:, i * LANE:(i + 1) * LANE]


GATHER_SEMS = 7


def _allgather_steps(src_ref, full_ref, blk_ref, send_sems, recv_sems):
    block_rows = blk_ref.shape[0]
    assert block_rows % BF16_ROWS == 0 and full_ref.shape[0] == N_DEV * block_rows
    x, y, c = lax.axis_index("x"), lax.axis_index("y"), lax.axis_index("c")
    me, sibling = (x, y, c), (x, y, 1 - c)
    chips = [(1 - x, y), (x, 1 - y), (1 - x, 1 - y)]

    def rows(px, py, pc):
        return full_ref.at[pl.ds(pl.multiple_of((4 * px + 2 * py + pc) * block_rows, BF16_ROWS), block_rows), :]

    def copy(k, block, to, src=None):
        return pltpu.make_async_remote_copy(
            src_ref=rows(*block) if src is None else src, dst_ref=rows(*block),
            send_sem=send_sems.at[k], recv_sem=recv_sems.at[k], device_id=to, device_id_type=MESH_ID)

    first = [copy(0, me, sibling, src=blk_ref)] + [copy(1 + j, me, (*chip, c), src=blk_ref) for j, chip in enumerate(chips)]
    passed = [copy(4 + j, (*chip, c), sibling) for j, chip in enumerate(chips)]

    def start():
        blk_ref[...] = src_ref[pl.ds(pl.multiple_of(c * block_rows, BF16_ROWS), block_rows), :].astype(BF16)
        rows(*me)[...] = blk_ref[...]
        for cp in first:
            cp.start()

    def forward():
        for j, chip in enumerate(chips):
            copy(1 + j, (*chip, c), me).wait_recv()
            passed[j].start()

    def finish():
        copy(0, sibling, me).wait_recv()
        for j, chip in enumerate(chips):
            copy(4 + j, (*chip, 1 - c), me).wait_recv()
        for cp in first + passed:
            cp.wait_send()

    return start, forward, finish


WINDOW_HALF = SHARD_PAD // 2
WINDOW_SEMS = GATHER_SEMS + 1


def _window_gather(wi_ref, w4_ref, blk_ref, edge_ref, send_sems, recv_sems):
    x, y, c = lax.axis_index("x"), lax.axis_index("y"), lax.axis_index("c")
    me, sibling, pair = (x, y, c), (x, y, 1 - c), (x, 1 - y, c)
    chips = [(1 - x, y), (x, 1 - y), (1 - x, 1 - y)]
    own_rows, edge_rows = SHARD_IN - WINDOW_HALF, SHARD_SHIFT
    sends_edge = y == c

    def rows(px, py, pc):
        return w4_ref.at[pl.ds(pl.multiple_of((4 * px + 2 * py + pc) * WINDOW_HALF, BF16_ROWS), WINDOW_HALF), :]

    def copy(k, block, to, src=None):
        return pltpu.make_async_remote_copy(
            src_ref=rows(*block) if src is None else src, dst_ref=rows(*block),
            send_sem=send_sems.at[k], recv_sem=recv_sems.at[k], device_id=to, device_id_type=MESH_ID)

    edge_copy = pltpu.make_async_remote_copy(
        src_ref=edge_ref, dst_ref=edge_ref, send_sem=send_sems.at[GATHER_SEMS], recv_sem=recv_sems.at[GATHER_SEMS],
        device_id=pair, device_id_type=MESH_ID)
    first = [copy(0, me, sibling, src=blk_ref)] + [copy(1 + j, me, (*chip, c), src=blk_ref) for j, chip in enumerate(chips)]
    passed = [copy(4 + j, (*chip, c), sibling) for j, chip in enumerate(chips)]

    def at(start, size):
        return pl.ds(pl.multiple_of(start, BF16_ROWS), size)

    def start():
        @pl.when(sends_edge)
        def _():
            edge_ref[...] = wi_ref[at((1 - y) * (SHARD_IN - edge_rows), edge_rows), :].astype(BF16)
            edge_copy.start()
            blk_ref[...] = wi_ref[at(c * own_rows, WINDOW_HALF), :].astype(BF16)

        @pl.when(jnp.logical_not(sends_edge))
        def _():
            blk_ref[at(y * edge_rows, own_rows), :] = wi_ref[at(c * WINDOW_HALF, own_rows), :].astype(BF16)
            edge_copy.wait_recv()
            blk_ref[at((1 - y) * own_rows, edge_rows), :] = edge_ref[...]

        rows(*me)[...] = blk_ref[...]
        for cp in first[:-1]:
            cp.start()
        copy(0, sibling, me).wait_recv()

    def send_far():
        first[-1].start()

    def forward(j):
        copy(1 + j, (*chips[j], c), me).wait_recv()
        passed[j].start()

    def wait_forwarded(j):
        copy(4 + j, (*chips[j], 1 - c), me).wait_recv()

    def finish():
        for cp in first + passed:
            cp.wait_send()

        @pl.when(sends_edge)
        def _():
            edge_copy.wait_send()

    return start, send_far, forward, wait_forwarded, finish


def _in_proj(x, norm_g, w_in_t_shard, w_out_shard, tb):
    seq = x.shape[0]
    nblk = seq // tb
    last = nblk - 1
    chip_of_panel = (None, 1, 0, 2)

    def body(win_ref, x_ref, g_ref, wi_ref, wo_ref, p_ref, ht_ref, wt_ref, wout_ref,
             w4, blk, edge, hbuf, wout_full, blko, send_sems, recv_sems, send_sems_o, recv_sems_o):
        q, i = pl.program_id(0), pl.program_id(1)
        chip = 2 * lax.axis_index("x") + lax.axis_index("y")
        in_start, in_send_far, in_forward, in_wait_forwarded, in_finish = _window_gather(
            wi_ref, w4, blk, edge, send_sems, recv_sems)
        out_start, out_forward, out_finish = _allgather_steps(wo_ref, wout_full, blko, send_sems_o, recv_sems_o)
        rows = pl.ds(pl.multiple_of(i * tb, tb), tb)

        @pl.when(jnp.logical_and(q == 0, i == 0))
        def _():
            in_start()

        @pl.when(jnp.logical_and(q == 0, i == min(last // 2 + 1, last)))
        def _():
            in_send_far()

        @pl.when(jnp.logical_and(q == 1, i == 0))
        def _():
            out_start()

        for panel in range(1, N_CHIPS):
            @pl.when(jnp.logical_and(q == panel, i == 0))
            def _():
                in_wait_forwarded(chip_of_panel[panel])

        @pl.when(q == 0)
        def _():
            xv = x_ref[...]
            r = lax.rsqrt(jnp.mean(xv * xv, axis=1, keepdims=True) + RMS_EPS)
            h = (xv * r) * g_ref[...]
            hbuf[rows, :] = h.astype(BF16)
            ht_ref[...] = h.T.astype(BF16)

        owner = jnp.bitwise_xor(chip, q)
        window = w4[pl.ds(pl.multiple_of(owner * SHARD_PAD, SHARD_PAD), SHARD_PAD), :]
        p_ref[...] = _dot_nt(hbuf[rows, :], window).astype(BF16)

        for panel in range(1, N_CHIPS):
            @pl.when(jnp.logical_and(q == panel - 1, i == last))
            def _():
                in_forward(chip_of_panel[panel])

        @pl.when(jnp.logical_and(q == N_CHIPS - 1, i == 0))
        def _():
            out_forward()

        @pl.when(jnp.logical_and(q == N_CHIPS - 1, i == last))
        def _():
            in_finish()
            for k in range(N_CHIPS):
                first = k * SHARD_PAD + (k % 2) * SHARD_SHIFT
                wt_ref[k * SHARD_IN:(k + 1) * SHARD_IN, :] = w4[first:first + SHARD_IN, :]
            out_finish()
            wout_ref[...] = wout_full[...]

    whole = lambda shape: pl.BlockSpec(shape, lambda q, i, win: (0,) * len(shape), pipeline_mode=pl.Buffered(1))
    first_panel = lambda q, i: jnp.where(q == 0, i, last)
    grid_spec = pltpu.PrefetchScalarGridSpec(
        num_scalar_prefetch=1, grid=(N_CHIPS, nblk),
        in_specs=[pl.BlockSpec((tb, D_MODEL), lambda q, i, win: (first_panel(q, i), 0)), whole((1, D_MODEL)),
                  whole((SHARD_IN, D_MODEL)), whole((SHARD_OUT, D_MODEL))],
        out_specs=(pl.BlockSpec((None, tb, SHARD_PAD), lambda q, i, win: (win[q], i, 0)),
                   pl.BlockSpec((D_MODEL, tb), lambda q, i, win: (0, first_panel(q, i))),
                   whole((IN_WIDTH, D_MODEL)), whole((MIX_WIDTH, D_MODEL))),
        scratch_shapes=[
            pltpu.VMEM((N_CHIPS * SHARD_PAD, D_MODEL), BF16), pltpu.VMEM((WINDOW_HALF, D_MODEL), BF16),
            pltpu.VMEM((SHARD_SHIFT, D_MODEL), BF16), pltpu.VMEM((seq, D_MODEL), BF16),
            pltpu.VMEM((MIX_WIDTH, D_MODEL), BF16), pltpu.VMEM((SHARD_OUT // 2, D_MODEL), BF16),
            pltpu.SemaphoreType.DMA((WINDOW_SEMS,)), pltpu.SemaphoreType.DMA((WINDOW_SEMS,)),
            pltpu.SemaphoreType.DMA((GATHER_SEMS,)), pltpu.SemaphoreType.DMA((GATHER_SEMS,)),
        ])
    chip = 2 * lax.axis_index("x") + lax.axis_index("y")
    owner = jnp.bitwise_xor(chip, jnp.arange(N_CHIPS, dtype=jnp.int32))
    return _call(
        body, name="in_proj", grid_spec=grid_spec,
        out_shape=(jax.ShapeDtypeStruct((N_CHIPS, seq, SHARD_PAD), BF16), jax.ShapeDtypeStruct((D_MODEL, seq), BF16),
                   jax.ShapeDtypeStruct((IN_WIDTH, D_MODEL), BF16), jax.ShapeDtypeStruct((MIX_WIDTH, D_MODEL), BF16)),
        compiler_params=_params(60, ("arbitrary", "arbitrary")),
    )(owner.astype(jnp.int32), x, norm_g, w_in_t_shard, w_out_shard)


def _mix_fwd(proj, x, target, w_out, final_g, gn_gain, sinks, tables, tb):
    seq = x.shape[0]
    nsub = tb // BLK
    cos_t, sin_t, decay_in, qdec_t, kdec_t, cd_t, bias_t = tables

    def body(p_ref, x_ref, t_ref, cos_ref, sin_ref, wout_ref, fg_ref, gain_ref, sinks_ref, din_ref, qdec_ref,
             kdec_ref, cd_ref, bias_ref, mix_ref, dxo_ref, st_ref, loss_ref, gfin_ref, oatt_ref, probs_ref, psink_ref,
             kprev_ref, vprev_ref, state_ref):
        i = pl.program_id(0)

        @pl.when(i == 0)
        def _():
            kprev_ref[...] = jnp.zeros_like(kprev_ref)
            vprev_ref[...] = jnp.zeros_like(vprev_ref)
            state_ref[...] = jnp.zeros_like(state_ref)
            loss_ref[...] = jnp.zeros_like(loss_ref)
            gfin_ref[...] = jnp.zeros_like(gfin_ref)

        prev = _in_previous_block()

        def sub(j, carry):
            kp, vp, states = carry
            rows = pl.ds(pl.multiple_of(j * BLK, BLK), BLK)
            bias = bias_ref[jnp.where(jnp.logical_or(i > 0, j > 0), 1, 0)]

            aq = _proj_cols(p_ref, rows, OFF_AQ, OFF_AQ + ATT_WIDTH)
            ak = _proj_cols(p_ref, rows, OFF_AK, OFF_AK + ATT_KV_WIDTH).astype(F32)
            av = _proj_cols(p_ref, rows, OFF_AV, OFF_AV + ATT_KV_WIDTH).astype(F32)
            az = _proj_cols(p_ref, rows, OFF_AZ, OFF_AZ + ATT_WIDTH).astype(F32)
            k_ops = _kv_operands(jnp.concatenate([kp, ak], axis=0))
            v_ops = _kv_operands(jnp.concatenate([vp, av], axis=0))
            qs = [(_stack_tiles(aq, 2 * g) * ATT_SCALE).astype(BF16) for g in range(ATT_KV_HEADS)]

            qr, kr, qd, kd = _retention_operands(p_ref, rows, cos_ref[rows, :], sin_ref[rows, :],
                                                 qdec_ref[...], kdec_ref[...])
            s = _attn_scores(qs, k_ops, bias, prev)
            heads = [(t, hh) for t in range(PAIRS) for hh in range(2)]
            sc = [_dot_nt(jnp.concatenate(_split_heads(_tile(qr, t)), axis=0).astype(BF16), _tile(kr, t).astype(BF16))
                  * din_ref[t] for t in range(PAIRS)]
            qd_heads = [_split_heads(_tile(qd, t)) for t in range(PAIRS)]
            state_b = [states[t].astype(BF16) for t in range(PAIRS)]
            vs = [_proj_cols(p_ref, rows, OFF_RV + h * RET_V_DIM, OFF_RV + (h + 1) * RET_V_DIM) for h in range(RET_HEADS)]
            rzs = [_proj_cols(p_ref, rows, OFF_RZ + h * RET_V_DIM, OFF_RZ + (h + 1) * RET_V_DIM).astype(F32)
                   for h in range(RET_HEADS)]
            lhs = [jnp.concatenate([sc[t][hh * BLK:(hh + 1) * BLK].astype(BF16), qd_heads[t][hh].astype(BF16)], axis=1)
                   for t, hh in heads]
            p, p_sink = _attn_softmax(s, sinks_ref)
            o_ret = [_dot(lhs[2 * t + hh], jnp.concatenate([vs[2 * t + hh], state_b[t]], axis=0)) for t, hh in heads]
            p = [pi.astype(BF16) for pi in p]
            o_tiles = []
            for g in range(ATT_KV_HEADS):
                p_cat = jnp.concatenate([_unfold(p[2 * g], prev), _unfold(p[2 * g + 1], prev)], axis=0)
                o = _dot_tn(p_cat, jnp.concatenate(v_ops[g], axis=0))
                o_tiles += [o[0:BLK], o[BLK:]]
            ons, _ = _group_norm_all(o_ret)
            new_states = [states[t] * cd_ref[t]
                          + _dot_tn(jnp.concatenate(_split_heads(_tile(kd, t)), axis=0).astype(BF16),
                                    jnp.concatenate([vs[2 * t], vs[2 * t + 1]], axis=0)) for t in range(PAIRS)]
            o_att = jnp.concatenate(o_tiles, axis=1)
            out = [o_att * (az * _sigmoid(az))]
            out += [(ons[h] * gain_ref[:, h * RET_V_DIM:(h + 1) * RET_V_DIM]) * (rzs[h] * _sigmoid(rzs[h]))
                    for h in range(RET_HEADS)]
            mix_ref[rows, :] = jnp.concatenate(out, axis=1).astype(BF16)
            oatt_ref[rows, :] = o_att
            probs_ref[j] = jnp.stack(p)
            psink_ref[j] = jnp.concatenate(p_sink, axis=0)
            st_ref[j] = jnp.stack(states)
            return ak, av, tuple(new_states)

        carry = (kprev_ref[...], vprev_ref[...], tuple(state_ref[t] for t in range(PAIRS)))
        for jb in range(nsub):
            carry = sub(jnp.int32(jb), carry)
        kp, vp, states = carry
        kprev_ref[...] = kp
        vprev_ref[...] = vp
        state_ref[...] = jnp.stack(states)

        xo = x_ref[...] + _dot(mix_ref[...], wout_ref[...])
        r2 = lax.rsqrt(jnp.mean(xo * xo, axis=1, keepdims=True) + RMS_EPS)
        xn = xo * r2
        err = xn * fg_ref[...] - t_ref[...]
        loss_ref[...] += jnp.sum(err * err) * (0.5 / D_MODEL)
        dy = err * (1.0 / D_MODEL)
        gfin_ref[...] += jnp.sum(dy * xn, axis=0, keepdims=True)
        u = dy * fg_ref[...]
        dxo_ref[...] = r2 * u - xn * (r2 * jnp.mean(u * xn, axis=1, keepdims=True))

    blk_rows = lambda w: pl.BlockSpec((tb, w), lambda i: (i, 0))
    state_shape = (PAIRS, 2 * RET_QK_DIM, RET_V_DIM)
    return _call(
        body, name="mix_fwd", grid=(seq // tb,),
        out_shape=(
            jax.ShapeDtypeStruct((seq, MIX_WIDTH), BF16),
            jax.ShapeDtypeStruct((seq, D_MODEL), F32),
            jax.ShapeDtypeStruct((seq // BLK,) + state_shape, F32),
            jax.ShapeDtypeStruct((8, LANE), F32),
            jax.ShapeDtypeStruct((1, D_MODEL), F32),
            jax.ShapeDtypeStruct((seq, ATT_WIDTH), F32),
            jax.ShapeDtypeStruct((seq // BLK, len(ATT_PROBLEMS), BLK, 2 * BLK), BF16),
            jax.ShapeDtypeStruct((seq // BLK, len(ATT_PROBLEMS), 2 * BLK), F32),
        ),
        in_specs=[
            pl.BlockSpec((N_CHIPS, tb, SHARD_PAD), lambda i: (0, i, 0)),
            blk_rows(D_MODEL), blk_rows(D_MODEL), blk_rows(LANE), blk_rows(LANE),
            _const_spec((MIX_WIDTH, D_MODEL)), _const_spec((1, D_MODEL)), _const_spec((1, RET_WIDTH)),
            pl.BlockSpec(memory_space=pltpu.SMEM),
            _const_spec((PAIRS, 2 * BLK, BLK)), _const_spec((BLK, RET_QK_WIDTH)), _const_spec((BLK, RET_QK_WIDTH)),
            _const_spec(state_shape), _const_spec((2, BLK, 2 * BLK)),
        ],
        out_specs=(
            blk_rows(MIX_WIDTH), blk_rows(D_MODEL),
            pl.BlockSpec((nsub,) + state_shape, lambda i: (i, 0, 0, 0)),
            _const_spec((8, LANE)), _const_spec((1, D_MODEL)), blk_rows(ATT_WIDTH),
            pl.BlockSpec((nsub, len(ATT_PROBLEMS), BLK, 2 * BLK), lambda i: (i, 0, 0, 0)),
            pl.BlockSpec((nsub, len(ATT_PROBLEMS), 2 * BLK), lambda i: (i, 0, 0)),
        ),
        scratch_shapes=[
            pltpu.VMEM((BLK, ATT_KV_WIDTH), F32), pltpu.VMEM((BLK, ATT_KV_WIDTH), F32),
            pltpu.VMEM(state_shape, F32),
        ],
        compiler_params=_params(48, ("arbitrary",)),
    )(proj, x, target, cos_t, sin_t, w_out, final_g, gn_gain, sinks, decay_in, qdec_t, kdec_t, cd_t, bias_t)


def _mix_bwd(proj, dxo, mix, o_att, probs, p_sinks, states, x, w_out, w_in_t, norm_g, gn_gain, tables, tb):
    seq = dxo.shape[0]
    nsub = tb // BLK
    nblk = seq // tb
    cos_t, sin_t, decay_in, qdec_t, kdec_t, cd_t, _ = tables
    kv_cols = OFF_AK // (2 * ATT_KV_WIDTH)
    state_shape = (PAIRS, 2 * RET_QK_DIM, RET_V_DIM)

    def body(p_ref, pkv_ref, dxo_ref, mix_ref, oatt_ref, probs_ref, psink_ref, st_ref, cos_ref, sin_ref, x_ref, wout_ref,
             win_ref, g_ref, gain_ref, din_ref, qdec_ref, kdec_ref, cd_ref,
             dp_ref, gx_ref, gwout_ref, gnorm_ref, dgain_ref, dsink_ref,
             dmix_ref, kv_ref, dkc_ref, dvc_ref, gst_ref):
        i = pl.program_id(0)

        @pl.when(i == 0)
        def _():
            gwout_ref[...] = jnp.zeros_like(gwout_ref)
            gnorm_ref[...] = jnp.zeros_like(gnorm_ref)
            dgain_ref[...] = jnp.zeros_like(dgain_ref)
            dsink_ref[...] = jnp.zeros_like(dsink_ref)
            dkc_ref[...] = jnp.zeros_like(dkc_ref)
            dvc_ref[...] = jnp.zeros_like(dvc_ref)
            gst_ref[...] = jnp.zeros_like(gst_ref)

        dxo_b = dxo_ref[...].astype(BF16)
        dmix_ref[...] = _dot_nt(dxo_b, wout_ref[...])
        gwout_ref[...] += _dot_tn(mix_ref[...], dxo_b)
        kv_ref[0:BLK, :] = pkv_ref[...].astype(F32)
        kv_ref[BLK:, :] = _proj_cols(p_ref, slice(None), OFF_AK, OFF_AK + 2 * ATT_KV_WIDTH).astype(F32)
        low = _low_lanes((BLK, LANE))
        low2 = _low_lanes((2 * BLK, LANE))
        lane = lax.broadcasted_iota(jnp.int32, (1, LANE), 1)
        prev = _in_previous_block()

        def sub(jj, carry):
            dkc, dvc, gsts, dgain, dsink = carry
            j = nsub - 1 - jj
            rows = pl.ds(pl.multiple_of(j * BLK, BLK), BLK)
            both = pl.ds(pl.multiple_of(j * BLK, BLK), 2 * BLK)


            aq = _proj_cols(p_ref, rows, OFF_AQ, OFF_AQ + ATT_WIDTH)
            az = _proj_cols(p_ref, rows, OFF_AZ, OFF_AZ + ATT_WIDTH).astype(F32)
            k_ops = _kv_operands(kv_ref[both, 0:ATT_KV_WIDTH])
            v_ops = _kv_operands(kv_ref[both, ATT_KV_WIDTH:2 * ATT_KV_WIDTH])
            da = dmix_ref[rows, 0:ATT_WIDTH]
            sig = _sigmoid(az)
            d_o = da * (az * sig)
            qs = [(_stack_tiles(aq, 2 * g) * ATT_SCALE).astype(BF16) for g in range(ATT_KV_HEADS)]
            dos = [_stack_tiles(d_o, 2 * g).astype(BF16) for g in range(ATT_KV_HEADS)]
            p_b = [probs_ref[j, k] for k in range(len(ATT_PROBLEMS))]
            p = [pk.astype(F32) for pk in p_b]
            sink_all = psink_ref[j]
            p_sink = [sink_all[k:k + 1, :] for k in range(len(ATT_PROBLEMS))]
            cos_b, sin_b = cos_ref[rows, :], sin_ref[rows, :]
            qdec, kdec = qdec_ref[...], kdec_ref[...]
            qr, kr, qd, kd = _retention_operands(p_ref, rows, cos_b, sin_b, qdec, kdec)
            heads = [(t, hh) for t in range(PAIRS) for hh in range(2)]
            head_cols = [slice(h * RET_V_DIM, (h + 1) * RET_V_DIM) for h in range(RET_HEADS)]
            q_rows = [jnp.concatenate(_split_heads(_tile(qr, t)), axis=0).astype(BF16) for t in range(PAIRS)]
            k_rows = [jnp.concatenate(_split_heads(_tile(kr, t)), axis=0).astype(BF16) for t in range(PAIRS)]
            din = [din_ref[t] for t in range(PAIRS)]

            dpr = [_fold(_dot_nt(v_ops[g][hi], dos[g]), prev) for g, hi in ATT_PROBLEMS]
            sc = [(_dot_nt(q_rows[t], _tile(kr, t).astype(BF16)) * din[t]).astype(BF16) for t in range(PAIRS)]
            qd_heads = [_split_heads(_tile(qd, t)) for t in range(PAIRS)]
            kd_heads = [_split_heads(_tile(kd, t)) for t in range(PAIRS)]
            state_b = [st_ref[j, t].astype(BF16) for t in range(PAIRS)]
            gst_b = [gsts[t].astype(BF16) for t in range(PAIRS)]
            vs = [_proj_cols(p_ref, rows, OFF_RV + h * RET_V_DIM, OFF_RV + (h + 1) * RET_V_DIM) for h in range(RET_HEADS)]
            rzs = [_proj_cols(p_ref, rows, OFF_RZ + h * RET_V_DIM, OFF_RZ + (h + 1) * RET_V_DIM).astype(F32)
                   for h in range(RET_HEADS)]
            drs = [dmix_ref[rows, ATT_WIDTH + h * RET_V_DIM:ATT_WIDTH + (h + 1) * RET_V_DIM] for h in range(RET_HEADS)]
            gains = [gain_ref[:, c] for c in head_cols]
            lhs = [jnp.concatenate([sc[t][hh * BLK:(hh + 1) * BLK], qd_heads[t][hh].astype(BF16)], axis=1) for t, hh in heads]
            rhs = [jnp.concatenate([vs[2 * t + hh], state_b[t]], axis=0) for t, hh in heads]

            delta = [jnp.sum(pi * di, axis=0, keepdims=True) for pi, di in zip(p, dpr)]
            ds = [_unfold((pi * (di - ti)).astype(BF16), prev) for pi, di, ti in zip(p, dpr, delta)]
            o_ret = [_dot(l, r) for l, r in zip(lhs, rhs)]
            for (g, hi), ki, ti in zip(ATT_PROBLEMS, p_sink, delta):
                sink_part = ki * ti
                for half in range(2):
                    tot = jnp.sum(sink_part[:, half * BLK:(half + 1) * BLK], axis=1, keepdims=True)
                    dsink = dsink - jnp.where(lane == 4 * g + 2 * half + hi, tot, 0.0)

            dq_tiles, dk_sums, dv_sums = [], [], []
            for g in range(ATT_KV_HEADS):
                ds_cat = jnp.concatenate([ds[2 * g], ds[2 * g + 1]], axis=0)
                p_cat = jnp.concatenate([_unfold(p_b[2 * g], prev), _unfold(p_b[2 * g + 1], prev)], axis=0)
                dqs = _dot_tn(ds_cat, jnp.concatenate(k_ops[g], axis=0)) * ATT_SCALE
                dq_tiles += [dqs[0:BLK], dqs[BLK:]]
                dk_sums.append(_dot(ds_cat, qs[g]))
                dv_sums.append(_dot(p_cat, dos[g]))
            ons, rstds = _group_norm_all(o_ret)
            sig_r = [_sigmoid(z) for z in rzs]
            dgn = [d * (z * g) for d, z, g in zip(drs, rzs, sig_r)]
            dz_parts = [d * (o * gn) * (g * (1.0 + z * (1.0 - g))) for d, o, gn, g, z in zip(drs, ons, gains, sig_r, rzs)]
            dgain_parts = [jnp.sum(d * o, axis=0, keepdims=True) for d, o in zip(dgn, ons)]
            don = [d * gn for d, gn in zip(dgn, gains)]
            mean_don = [jnp.mean(d, axis=1, keepdims=True) for d in don]
            mean_don_on = [jnp.mean(d * o, axis=1, keepdims=True) for d, o in zip(don, ons)]
            dob = [(r * (d - a - o * b)).astype(BF16) for r, d, a, o, b in zip(rstds, don, mean_don, ons, mean_don_on)]

            dlhs = [_dot_nt(d, r) for d, r in zip(dob, rhs)]
            drhs = [_dot_tn(l, d) for l, d in zip(lhs, dob)]
            dkds = [_dot_nt(vs[2 * t + hh], gst_b[t]) for t, hh in heads]
            dv_parts = [drhs[2 * t + hh][0:BLK] + _dot(kd_heads[t][hh].astype(BF16), gst_b[t]) for t, hh in heads]
            daz = da * oatt_ref[rows, :] * (sig * (1.0 + az * (1.0 - sig)))

            def kv_grad(sums):
                (a0, b0), (a1, b1) = [(s[0:2 * BLK], s[2 * BLK:]) for s in sums]
                return jnp.where(low2, a0, b1) + pltpu.roll(jnp.where(low2, a1, b0), HALF_LANE, 1)

            dk_both, dv_both = kv_grad(dk_sums), kv_grad(dv_sums)
            dak = dk_both[BLK:] + dkc
            dav = dv_both[BLK:] + dvc
            das = [(dlhs[2 * t + hh][:, 0:BLK] * din[t][hh * BLK:(hh + 1) * BLK]).astype(BF16) for t, hh in heads]
            new_gsts = [gsts[t] * cd_ref[t] + drhs[2 * t][BLK:] + drhs[2 * t + 1][BLK:] for t in range(PAIRS)]
            dq_parts = [_dot(jnp.concatenate([das[2 * t], das[2 * t + 1]], axis=1), k_rows[t])
                        + jnp.where(low, dlhs[2 * t][:, BLK:], dlhs[2 * t + 1][:, BLK:]) * _tile(qdec, t)
                        for t in range(PAIRS)]
            dk_parts = [_dot_tn(jnp.concatenate([das[2 * t], das[2 * t + 1]], axis=0), q_rows[t])
                        + jnp.where(low, dkds[2 * t], dkds[2 * t + 1]) * _tile(kdec, t) for t in range(PAIRS)]
            drq = _rotate_transposed(jnp.concatenate(dq_parts, axis=1), cos_b, sin_b)
            drk = _rotate_transposed(jnp.concatenate(dk_parts, axis=1) * RET_SCALE, cos_b, sin_b)

            dp_ref[rows, :] = jnp.concatenate(
                [jnp.concatenate(dq_tiles, axis=1), dak, dav, daz, drq, drk] + dv_parts + dz_parts, axis=1).astype(BF16)
            dgain = dgain + jnp.concatenate(dgain_parts, axis=1)
            return dk_both[0:BLK], dv_both[0:BLK], tuple(new_gsts), dgain, dsink

        carry = (dkc_ref[...], dvc_ref[...], tuple(gst_ref[t] for t in range(PAIRS)), dgain_ref[...], dsink_ref[...])
        for jb in range(nsub):
            carry = sub(jnp.int32(jb), carry)
        dkc, dvc, gsts, dgain, dsink = carry
        dkc_ref[...] = dkc
        dvc_ref[...] = dvc
        gst_ref[...] = jnp.stack(gsts)
        dgain_ref[...] = dgain
        dsink_ref[...] = dsink

        dh = _dot(dp_ref[...], win_ref[...])
        xv = x_ref[...]
        r = lax.rsqrt(jnp.mean(xv * xv, axis=1, keepdims=True) + RMS_EPS)
        xn = xv * r
        gnorm_ref[...] += jnp.sum(dh * xn, axis=0, keepdims=True)
        u = dh * g_ref[...]
        gx_ref[...] = dxo_ref[...] + r * u - xn * (r * jnp.mean(u * xn, axis=1, keepdims=True))

    rev_rows = lambda w: pl.BlockSpec((tb, w), lambda i: (nblk - 1 - i, 0))
    prev_kv = pl.BlockSpec((None, BLK, 2 * ATT_KV_WIDTH),
                           lambda i: (0, jnp.maximum((nblk - 1 - i) * nsub - 1, 0), kv_cols))
    return _call(
        body, name="mix_bwd", grid=(nblk,),
        out_shape=(
            jax.ShapeDtypeStruct((seq, IN_WIDTH), BF16),
            jax.ShapeDtypeStruct((seq, D_MODEL), F32),
            jax.ShapeDtypeStruct((MIX_WIDTH, D_MODEL), F32),
            jax.ShapeDtypeStruct((1, D_MODEL), F32),
            jax.ShapeDtypeStruct((1, RET_WIDTH), F32),
            jax.ShapeDtypeStruct((1, LANE), F32),
        ),
        in_specs=[
            pl.BlockSpec((N_CHIPS, tb, SHARD_PAD), lambda i: (0, nblk - 1 - i, 0)),
            prev_kv, rev_rows(D_MODEL), rev_rows(MIX_WIDTH), rev_rows(ATT_WIDTH),
            pl.BlockSpec((nsub, len(ATT_PROBLEMS), BLK, 2 * BLK), lambda i: (nblk - 1 - i, 0, 0, 0)),
            pl.BlockSpec((nsub, len(ATT_PROBLEMS), 2 * BLK), lambda i: (nblk - 1 - i, 0, 0)),
            pl.BlockSpec((nsub,) + state_shape, lambda i: (nblk - 1 - i, 0, 0, 0)),
            rev_rows(LANE), rev_rows(LANE), rev_rows(D_MODEL),
            _const_spec((MIX_WIDTH, D_MODEL)), _const_spec((IN_WIDTH, D_MODEL)), _const_spec((1, D_MODEL)),
            _const_spec((1, RET_WIDTH)),
            _const_spec((PAIRS, 2 * BLK, BLK)), _const_spec((BLK, RET_QK_WIDTH)), _const_spec((BLK, RET_QK_WIDTH)),
            _const_spec(state_shape),
        ],
        out_specs=(
            rev_rows(IN_WIDTH), rev_rows(D_MODEL), _const_spec((MIX_WIDTH, D_MODEL)), _const_spec((1, D_MODEL)),
            _const_spec((1, RET_WIDTH)), _const_spec((1, LANE)),
        ),
        scratch_shapes=[
            pltpu.VMEM((tb, MIX_WIDTH), F32),
            pltpu.VMEM((tb + BLK, 2 * ATT_KV_WIDTH), F32),
            pltpu.VMEM((BLK, ATT_KV_WIDTH), F32), pltpu.VMEM((BLK, ATT_KV_WIDTH), F32),
            pltpu.VMEM(state_shape, F32),
        ],
        compiler_params=_params(60, ("arbitrary",)),
    )(proj, proj, dxo, mix, o_att, probs, p_sinks, states, cos_t, sin_t, x, w_out, w_in_t, norm_g, gn_gain, decay_in,
      qdec_t, kdec_t, cd_t)


WINDOW_ORDER = (3, 1, 2, 0)


def _gw_in_reduce(h_t, dproj, gw_out, small, tb):
    seq = dproj.shape[0]
    nblk = seq // tb
    last = nblk - 1
    hand_on = min(1, last)
    half = D_MODEL // 2
    A, B, C, N_SEMS = 0, N_CHIPS, 2 * N_CHIPS, 2 * N_CHIPS + 1

    def body(win_ref, ht_ref, dp_ref, gwo_hbm, s0_ref, s1_ref, s2_ref, s3_ref, s4_ref, out_ref, fout_ref, packsum_ref,
             acc, sib, send_buf, b_in, fin, fout, mine_out, sib_out, send_out, b_out, pack_ref, packs,
             send_sems, recv_sems, local_sems):
        p, i = pl.program_id(0), pl.program_id(1)
        small_start, small_hand_on, small_finish = _small_exchange(
            gwo_hbm, (s0_ref, s1_ref, s2_ref, s3_ref, s4_ref), fout, mine_out, sib_out, send_out, b_out, pack_ref,
            packs, send_sems, recv_sems, local_sems, N_SEMS)

        @pl.when(jnp.logical_and(p == 0, i == 0))
        def _():
            small_start()

        @pl.when(jnp.logical_and(p == 0, i == hand_on))
        def _():
            small_hand_on()

        x, y, c = lax.axis_index("x"), lax.axis_index("y"), lax.axis_index("c")
        chip = 2 * x + y
        sibling = (x, y, 1 - c)
        mine = pl.ds(pl.multiple_of(c * half, half), half)
        other = pl.ds(pl.multiple_of((1 - c) * half, half), half)

        def remote(src, dst, send_k, recv_k, to):
            return pltpu.make_async_remote_copy(src_ref=src, dst_ref=dst, send_sem=send_sems.at[send_k],
                                                recv_sem=recv_sems.at[recv_k], device_id=to, device_id_type=MESH_ID)

        slot = p % 2

        @pl.when(i == 0)
        def _():
            acc[slot] = jnp.zeros(acc.shape[1:], F32)

        acc[slot] += _dot(ht_ref[...], dp_ref[...])

        for q in range(N_CHIPS):
            s = q % 2
            to_sibling = remote(acc.at[s, other, :], sib.at[s], A + q, A + q, sibling)

            @pl.when(jnp.logical_and(p == q, i == last))
            def _():
                to_sibling.start()

            if q < N_CHIPS - 1:
                dest = chip ^ WINDOW_ORDER[q]

                @pl.when(jnp.logical_and(p == q + 1, i == hand_on))
                def _():
                    to_sibling.wait_recv()
                    send_buf[q] = (acc[s, mine, :] + sib[s]).astype(BF16)
                    remote(send_buf.at[q], b_in.at[chip], B + q, B + chip, (dest // 2, dest % 2, c)).start()
                    to_sibling.wait_send()
            else:
                @pl.when(jnp.logical_and(p == q, i == last))
                def _():
                    to_sibling.wait_recv()
                    fin[mine, :] = acc[s, mine, :] + sib[s]
                    for j in range(N_CHIPS):
                        @pl.when(j != chip)
                        def _():
                            remote(b_in.at[j], b_in.at[j], B + j, B + j, sibling).wait_recv()
                            fin[mine, :] += b_in[j].astype(F32)
                    to_core = remote(fin.at[mine, :], fin.at[mine, :], C, C, sibling)
                    to_core.start()
                    remote(fin.at[other, :], fin.at[other, :], C, C, sibling).wait_recv()
                    out_ref[...] = fin[...]
                    to_core.wait_send()
                    to_sibling.wait_send()
                    for k in range(N_CHIPS - 1):
                        remote(send_buf.at[k], b_in.at[chip], B + k, B + k, sibling).wait_send()
                    packsum_ref[...] = small_finish()
                    fout_ref[...] = fout[...]

    whole = lambda shape: pl.BlockSpec(shape, lambda p, i, win: (0,) * len(shape), pipeline_mode=pl.Buffered(1))
    grid_spec = pltpu.PrefetchScalarGridSpec(
        num_scalar_prefetch=1, grid=(N_CHIPS, nblk),
        in_specs=[pl.BlockSpec((D_MODEL, tb), lambda p, i, win: (0, i)),
                  pl.BlockSpec((pl.Element(tb), pl.Element(SHARD_PAD)),
                               lambda p, i, win: (i * tb, pl.multiple_of(win[p] * LANE, LANE))),
                  pl.BlockSpec(memory_space=pl.ANY)] + [whole(s.shape) for s in small],
        out_specs=(whole((D_MODEL, SHARD_PAD)), whole((SHARD_OUT, D_MODEL)), whole((PACK_ROWS, D_MODEL))),
        scratch_shapes=[
            pltpu.VMEM((2, D_MODEL, SHARD_PAD), F32), pltpu.VMEM((2, half, SHARD_PAD), F32),
            pltpu.VMEM((N_CHIPS - 1, half, SHARD_PAD), BF16), pltpu.VMEM((N_CHIPS, half, SHARD_PAD), BF16),
            pltpu.VMEM((D_MODEL, SHARD_PAD), F32),
        ] + _small_exchange_scratch() + [
            pltpu.SemaphoreType.DMA((N_SEMS + SMALL_SEMS,)), pltpu.SemaphoreType.DMA((N_SEMS + SMALL_SEMS,)),
            pltpu.SemaphoreType.DMA((N_CHIPS,)),
        ])
    chip = 2 * lax.axis_index("x") + lax.axis_index("y")
    owner = chip ^ jnp.array(WINDOW_ORDER, dtype=jnp.int32)
    win_start = (owner * SHARD_IN) // LANE
    return _call(
        body, name="gw_in_reduce", grid_spec=grid_spec,
        out_shape=(jax.ShapeDtypeStruct((D_MODEL, SHARD_PAD), F32), jax.ShapeDtypeStruct((SHARD_OUT, D_MODEL), F32),
                   jax.ShapeDtypeStruct((PACK_ROWS, D_MODEL), F32)),
        compiler_params=_params(52, ("arbitrary", "arbitrary")),
    )(win_start.astype(jnp.int32), h_t, dproj, gw_out, *small)


SMALL_SEMS = 17


def _small_exchange(gwo_hbm, small_refs, fout_ref, mine_out, sib_out, send_out, b_out, pack_ref, packs,
                    send_sems, recv_sems, local_sems, base):
    half_out = SHARD_OUT // 2
    A_OUT, B_OUT, C_OUT, PACK = base, base + 4, base + 8, base + 9
    assert len(small_refs) == PACK_PARTS
    x, y, c = lax.axis_index("x"), lax.axis_index("y"), lax.axis_index("c")
    chip = 2 * x + y
    dev = 2 * chip + c
    sibling = (x, y, 1 - c)

    def remote(src, dst, send_k, recv_k, to):
        return pltpu.make_async_remote_copy(src_ref=src, dst_ref=dst, send_sem=send_sems.at[send_k],
                                            recv_sem=recv_sems.at[recv_k], device_id=to, device_id_type=MESH_ID)

    def out_rows(j, core):
        return pl.ds(pl.multiple_of(j * SHARD_OUT + core * half_out, half_out), half_out)

    my_out_rows = pl.ds(pl.multiple_of(c * half_out, half_out), half_out)
    local = [pltpu.make_async_copy(gwo_hbm.at[out_rows(j, c), :], mine_out.at[j], local_sems.at[j])
             for j in range(N_CHIPS)]
    stage_a = [remote(gwo_hbm.at[out_rows(j, 1 - c), :], sib_out.at[j], A_OUT + j, A_OUT + j, sibling)
               for j in range(N_CHIPS)]
    mine_half_out = fout_ref.at[my_out_rows, :]
    stage_c = [remote(mine_half_out, mine_half_out, C_OUT, C_OUT, sibling)]

    def start():
        pack_ref[...] = jnp.zeros_like(pack_ref)
        for k, s_ref in enumerate(small_refs):
            pack_ref[k:k + 1, 0:s_ref.shape[1]] = s_ref[0:1, :]
        packs[dev] = pack_ref[...]
        for d in range(N_DEV):
            to = (d // 4, (d // 2) % 2, d % 2)

            @pl.when(d != dev)
            def _():
                remote(pack_ref, packs.at[dev], PACK + d, PACK + dev, to).start()

        for cp in local + stage_a:
            cp.start()

    def hand_on():
        for cp in local:
            cp.wait()
        for cp in stage_a:
            cp.wait_recv()
        for j in range(N_CHIPS):
            mine_out[j] = mine_out[j] + sib_out[j]

        for j in range(N_CHIPS):
            to = (j // 2, j % 2, c)

            @pl.when(j != chip)
            def _():
                send_out[j] = mine_out[j].astype(BF16)
                remote(send_out.at[j], b_out.at[chip], B_OUT + j, B_OUT + chip, to).start()

            @pl.when(j == chip)
            def _():
                fout_ref[my_out_rows, :] = mine_out[j]

    def finish():
        for j in range(N_CHIPS):
            @pl.when(j != chip)
            def _():
                remote(b_out.at[j], b_out.at[j], B_OUT + j, B_OUT + j, sibling).wait_recv()
                fout_ref[my_out_rows, :] += b_out[j].astype(F32)

        for cp in stage_c:
            cp.start()
        other_half_out = fout_ref.at[pl.ds(pl.multiple_of((1 - c) * half_out, half_out), half_out), :]
        remote(other_half_out, other_half_out, C_OUT, C_OUT, sibling).wait_recv()

        for d in range(N_DEV):
            @pl.when(d != dev)
            def _():
                remote(pack_ref, packs.at[d], PACK + d, PACK + d, sibling).wait_recv()
        total = packs[0]
        for d in range(1, N_DEV):
            total = total + packs[d]

        for cp in stage_a + stage_c:
            cp.wait_send()
        for j in range(N_CHIPS):
            @pl.when(j != chip)
            def _():
                remote(b_out.at[j], b_out.at[j], B_OUT + j, B_OUT + j, sibling).wait_send()
        for d in range(N_DEV):
            @pl.when(d != dev)
            def _():
                remote(pack_ref, packs.at[d], PACK + d, PACK + d, sibling).wait_send()
        return total

    return start, hand_on, finish


def _small_exchange_scratch():
    half_out = SHARD_OUT // 2
    return [
        pltpu.VMEM((SHARD_OUT, D_MODEL), F32),
        pltpu.VMEM((N_CHIPS, half_out, D_MODEL), F32), pltpu.VMEM((N_CHIPS, half_out, D_MODEL), F32),
        pltpu.VMEM((N_CHIPS, half_out, D_MODEL), BF16), pltpu.VMEM((N_CHIPS, half_out, D_MODEL), BF16),
        pltpu.VMEM((PACK_ROWS, D_MODEL), F32), pltpu.VMEM((N_DEV, PACK_ROWS, D_MODEL), F32),
    ]


def _adam_math(w, g, m, v):
    mn = ADAM_B1 * m + (1.0 - ADAM_B1) * g
    vn = ADAM_B2 * v + (1.0 - ADAM_B2) * (g * g)
    m_hat = mn / (1.0 - ADAM_B1 ** ADAM_STEP)
    v_hat = vn / (1.0 - ADAM_B2 ** ADAM_STEP)
    return -ADAM_LR * (m_hat / (jnp.sqrt(v_hat) + ADAM_EPS) + ADAM_WD * w), mn, vn


def _adamw(name, w, g, m, v, tb):
    rows, cols = w.shape

    def body(w_ref, g_ref, m_ref, v_ref, go_ref, d_ref, mo_ref, vo_ref):
        gv = g_ref[...]
        go_ref[...] = gv
        d_ref[...], mo_ref[...], vo_ref[...] = _adam_math(w_ref[...], gv, m_ref[...], v_ref[...])

    spec = pl.BlockSpec((tb, cols), lambda i: (i, 0))
    shape = jax.ShapeDtypeStruct((rows, cols), F32)
    return _call(
        body, name=name, grid=(rows // tb,), out_shape=(shape,) * 4,
        in_specs=[spec] * 4, out_specs=(spec,) * 4,
        compiler_params=_params(32, ("arbitrary",)),
    )(w, g, m, v)


def _adamw_w_in(w_t, g_window, m_t, v_t, tb):
    def body(w_ref, g_ref, m_ref, v_ref, go_ref, d_ref, mo_ref, vo_ref, gt_ref):
        gt_ref[...] = g_ref[...].T
        gv = gt_ref[pl.ds(pl.multiple_of(lax.axis_index("y") * SHARD_SHIFT, SHARD_SHIFT), SHARD_IN), :]
        go_ref[...] = gv
        d_ref[...], mo_ref[...], vo_ref[...] = _adam_math(w_ref[...], gv, m_ref[...], v_ref[...])

    spec = pl.BlockSpec((SHARD_IN, tb), lambda i: (0, i))
    shape = jax.ShapeDtypeStruct((SHARD_IN, D_MODEL), F32)
    return _call(
        body, name="adamw_w_in", grid=(D_MODEL // tb,), out_shape=(shape,) * 4,
        in_specs=[spec, pl.BlockSpec((tb, SHARD_PAD), lambda i: (i, 0)), spec, spec], out_specs=(spec,) * 4,
        scratch_shapes=[pltpu.VMEM((SHARD_PAD, tb), F32)],
        compiler_params=_params(32, ("arbitrary",)),
    )(w_t, g_window, m_t, v_t)


def _adamw_small(sums, params):
    def body(sums_ref, *refs):
        ins, outs = refs[:3 * len(params)], refs[3 * len(params):]
        for k in range(len(params)):
            w_ref, m_ref, v_ref = ins[3 * k:3 * k + 3]
            g = sums_ref[k:k + 1, 0:w_ref.shape[1]]
            go_ref, d_ref, mo_ref, vo_ref = outs[4 * k:4 * k + 4]
            go_ref[...] = g
            d_ref[...], mo_ref[...], vo_ref[...] = _adam_math(w_ref[...], g, m_ref[...], v_ref[...])

    vmem = pl.BlockSpec(memory_space=pltpu.VMEM)
    flat = [a for p in params for a in p]
    shapes = tuple(jax.ShapeDtypeStruct(p[0].shape, F32) for p in params for _ in range(4))
    res = _call(body, name="adamw_small", out_shape=shapes, in_specs=[vmem] * (1 + len(flat)),
                out_specs=(vmem,) * len(shapes), compiler_params=_params(16))(sums, *flat)
    return [res[4 * k:4 * k + 4] for k in range(len(params))]


def kernel(x, norm_g, w_in, att_sinks, ret_gn_g, w_out, final_g, loss_target, m_norm_g, m_w_in, m_att_sinks, m_ret_gn_g, m_w_out, m_final_g, v_norm_g, v_w_in, v_att_sinks, v_ret_gn_g, v_w_out, v_final_g):
    seq = x.shape[1]
    xs, tgt = x[0], loss_target[0]
    final_g2 = final_g.reshape(1, D_MODEL)
    tables = _tables(seq)

    w_in_t, m_w_in_t, v_w_in_t = w_in[0].T, m_w_in[0].T, v_w_in[0].T
    proj, h_t, w_in_full, w_out_full = _in_proj(xs, norm_g, w_in_t, w_out[0], min(TOKENS_PROJ, seq))
    mix, dxo, states, loss_part, gfin, o_att, probs, p_sinks = _mix_fwd(
        proj, xs, tgt, w_out_full, final_g2, ret_gn_g, att_sinks, tables, min(TOKENS_MIX, seq))
    dproj, grad_x, gw_out, gnorm, dgain, dsink = _mix_bwd(
        proj, dxo, mix, o_att, probs, p_sinks, states, xs, w_out_full, w_in_full, norm_g, ret_gn_g, tables,
        min(TOKENS_MIX, seq))
    g_in, g_out, sums = _gw_in_reduce(h_t, dproj, gw_out, (gnorm, gfin, dgain, dsink, loss_part),
                                      min(TOKENS_GW, seq))

    res_in = [r.T for r in _adamw_w_in(w_in_t, g_in, m_w_in_t, v_w_in_t, 256)]
    res_out = _adamw("adamw_w_out", w_out[0], g_out, m_w_out[0], v_w_out[0], SHARD_OUT)
    as_row = lambda a: a.reshape(1, D_MODEL)
    r_norm, r_final, r_gain, r_sink = _adamw_small(sums, [
        (norm_g, m_norm_g, v_norm_g), (final_g2, as_row(m_final_g), as_row(v_final_g)),
        (ret_gn_g, m_ret_gn_g, v_ret_gn_g), (att_sinks, m_att_sinks, v_att_sinks)])

    outs = []
    for k in range(4):
        outs += [r_norm[k], res_in[k][None], r_sink[k], r_gain[k], res_out[k][None], r_final[k].reshape(D_MODEL)]
    return (sums[4, 0], grad_x[None], *outs)
```

```python
import jax
import jax.numpy as jnp
import numpy as np
from jax import lax
from jax.experimental import pallas as pl
from jax.experimental.pallas import tpu as pltpu

F32 = jnp.float32
BF16 = jnp.bfloat16

D_MODEL = 1024
ATT_HEADS = 8
ATT_KV_HEADS = 2
ATT_HEAD_DIM = 64
RET_HEADS = 4
RET_QK_DIM = 64
RET_V_DIM = 128
BLK = 128
ROT_BASE = 10000.0
RMS_EPS = 1e-6
GN_EPS = 1e-6
NEG_INF = -1e30
ATT_SCALE = ATT_HEAD_DIM ** -0.5
RET_SCALE = RET_QK_DIM ** -0.5

ATT_WIDTH = ATT_HEADS * ATT_HEAD_DIM
ATT_KV_WIDTH = ATT_KV_HEADS * ATT_HEAD_DIM
RET_QK_WIDTH = RET_HEADS * RET_QK_DIM
RET_WIDTH = RET_HEADS * RET_V_DIM
MIX_WIDTH = ATT_WIDTH + RET_WIDTH
OFF_AQ = 0
OFF_AK = OFF_AQ + ATT_WIDTH
OFF_AV = OFF_AK + ATT_KV_WIDTH
OFF_AZ = OFF_AV + ATT_KV_WIDTH
OFF_RQ = OFF_AZ + ATT_WIDTH
OFF_RK = OFF_RQ + RET_QK_WIDTH
OFF_RV = OFF_RK + RET_QK_WIDTH
OFF_RZ = OFF_RV + RET_WIDTH
IN_WIDTH = OFF_RZ + RET_WIDTH

LANE = 128
BF16_ROWS = 16
HALF_LANE = LANE // 2
PAIRS = RET_QK_WIDTH // LANE
assert ATT_HEAD_DIM == HALF_LANE and RET_QK_DIM == HALF_LANE and RET_V_DIM == LANE and ATT_KV_WIDTH == LANE

N_CHIPS = 4
N_DEV = 8
SHARD_IN = IN_WIDTH // N_CHIPS
SHARD_PAD = 768
SHARD_SHIFT = SHARD_PAD - SHARD_IN
WIN_START = tuple((j * SHARD_IN) // LANE * LANE for j in range(N_CHIPS))
SHARD_OUT = MIX_WIDTH // N_CHIPS

PACK_PARTS = 5
PACK_ROWS = 8
assert PACK_PARTS <= PACK_ROWS

ADAM_LR = 0.001
ADAM_B1 = 0.9
ADAM_B2 = 0.999
ADAM_EPS = 1e-08
ADAM_WD = 0.01
ADAM_STEP = 10

VMEM_CAP = 64 * 1024 * 1024
TOKENS_PROJ = 1024
TOKENS_MIX = 512
TOKENS_GW = 2048
MESH_ID = pl.DeviceIdType.MESH


def _call(body, **kw):
    return pl.pallas_call(body, **kw)


def _params(vmem_mb, semantics=None):
    assert vmem_mb * 1024 * 1024 < VMEM_CAP
    return pltpu.CompilerParams(dimension_semantics=semantics, vmem_limit_bytes=vmem_mb * 1024 * 1024)


def _dot(a, b):
    return jnp.dot(a, b, preferred_element_type=F32)


def _dot_nt(a, b):
    return lax.dot_general(a, b, (((1,), (1,)), ((), ())), preferred_element_type=F32)


def _dot_tn(a, b):
    return lax.dot_general(a, b, (((0,), (0,)), ((), ())), preferred_element_type=F32)


def _sigmoid(z):
    return 1.0 / (1.0 + jnp.exp(-z))


def _const_spec(shape):
    nd = len(shape)
    return pl.BlockSpec(shape, lambda i: (0,) * nd, pipeline_mode=pl.Buffered(1))


def _tables(seq):
    f32 = np.float32
    pos = np.arange(seq, dtype=f32)
    theta = (f32(1.0) / (f32(ROT_BASE) ** np.linspace(0.0, 1.0, RET_QK_DIM // 2, dtype=f32))).astype(f32)
    ang = (pos[:, None] * theta[None, :]).astype(f32)
    cos, sin = np.cos(ang), np.sin(ang)
    cos2 = np.repeat(cos, 2, axis=1)
    sin2 = np.stack([-sin, sin], axis=-1).reshape(seq, RET_QK_DIM)
    cos_t = np.tile(cos2, (1, 2))
    sin_t = np.tile(sin2, (1, 2))

    log_gamma = np.log(f32(1.0) - f32(2.0) ** (f32(-5.0) - np.arange(RET_HEADS, dtype=f32))).astype(f32)
    idx = np.arange(BLK, dtype=f32)
    rel = idx[:, None] - idx[None, :]
    decay_in = np.where(rel >= 0, np.exp(log_gamma[:, None, None] * np.maximum(rel, f32(0.0))), f32(0.0))
    k_dec = np.exp(log_gamma[:, None] * (BLK - 1 - idx)[None, :])
    q_dec = np.exp(log_gamma[:, None] * (idx + 1)[None, :])
    chunk_decay = np.exp(log_gamma * f32(BLK))
    kdec_t = np.repeat(k_dec.T, RET_QK_DIM, axis=1)
    qdec_t = np.repeat(q_dec.T, RET_QK_DIM, axis=1)
    cd_t = np.broadcast_to(chunk_decay[:, None, None], (RET_HEADS, RET_QK_DIM, RET_V_DIM))
    decay_in = decay_in.reshape(PAIRS, 2 * BLK, BLK)
    cd_t = cd_t.reshape(PAIRS, 2 * RET_QK_DIM, RET_V_DIM)

    key = np.arange(BLK)[:, None]
    query = np.arange(2 * BLK)[None, :] % BLK
    bias = np.stack([np.where(key > query, NEG_INF, 0.0), np.zeros((BLK, 2 * BLK))])
    return tuple(jnp.asarray(np.ascontiguousarray(a), F32) for a in (cos_t, sin_t, decay_in, qdec_t, kdec_t, cd_t, bias))


def _low_lanes(shape):
    lane = lax.broadcasted_iota(jnp.int32, shape, len(shape) - 1)
    return (lane & HALF_LANE) == 0


def _split_heads(t):
    low = _low_lanes(t.shape)
    zero = jnp.zeros_like(t)
    return jnp.where(low, t, zero), jnp.where(low, zero, t)


def _swap_pairs(t):
    lane = lax.broadcasted_iota(jnp.int32, t.shape, 1)
    nxt = pltpu.roll(t, t.shape[1] - 1, 1)
    prv = pltpu.roll(t, 1, 1)
    return jnp.where((lane & 1) == 0, nxt, prv)


def _per_tile(fn, t):
    return jnp.concatenate([fn(_tile(t, i)) for i in range(t.shape[1] // LANE)], axis=1)


def _rotate(t, cos_t, sin_t):
    return _per_tile(lambda a: a * cos_t + _swap_pairs(a) * sin_t, t)


def _rotate_transposed(d, cos_t, sin_t):
    return _per_tile(lambda a: a * cos_t + _swap_pairs(a * sin_t), d)


def _kv_operands(cat):
    low = _low_lanes(cat.shape)
    swapped = pltpu.roll(cat, HALF_LANE, 1)
    zero = jnp.zeros_like(cat)
    pick = lambda a, b: jnp.where(low, a, b).astype(BF16)
    return ((pick(cat, zero), pick(zero, swapped)), (pick(swapped, zero), pick(zero, cat)))


def _stack_tiles(t, first_tile):
    a = t[:, first_tile * LANE:(first_tile + 1) * LANE]
    b = t[:, (first_tile + 1) * LANE:(first_tile + 2) * LANE]
    return jnp.concatenate([a, b], axis=0)


def _sink_rows(sinks_ref, group):
    first = lax.broadcasted_iota(jnp.int32, (1, 2 * BLK), 1) < BLK

    def row(h0, h1):
        return jnp.where(first, sinks_ref[0, group * 4 + h0], sinks_ref[0, group * 4 + h1])
    return row(0, 2), row(1, 3)


ATT_PROBLEMS = tuple((g, hi) for g in range(ATT_KV_HEADS) for hi in range(2))


def _in_previous_block():
    key = lax.broadcasted_iota(jnp.int32, (BLK, 2 * BLK), 0)
    query = lax.broadcasted_iota(jnp.int32, (BLK, 2 * BLK), 1) & (BLK - 1)
    return key > query


def _fold(t, prev):
    return jnp.where(prev, t[0:BLK], t[BLK:])


def _unfold(t, prev):
    zero = jnp.zeros_like(t)
    return jnp.concatenate([jnp.where(prev, t, zero), jnp.where(prev, zero, t)], axis=0)


def _attn_scores(qs, k_ops, bias, prev):
    return [_fold(_dot_nt(k_ops[g][hi], qs[g]), prev) + bias for g, hi in ATT_PROBLEMS]


def _attn_softmax(s, sinks_ref):
    sink = [_sink_rows(sinks_ref, g)[hi] for g, hi in ATT_PROBLEMS]
    m = [jnp.maximum(jnp.max(si, axis=0, keepdims=True), ki) for si, ki in zip(s, sink)]
    e = [jnp.exp(si - mi) for si, mi in zip(s, m)]
    es = [jnp.exp(ki - mi) for ki, mi in zip(sink, m)]
    inv = [1.0 / (jnp.sum(ei, axis=0, keepdims=True) + esi) for ei, esi in zip(e, es)]
    return [ei * ii for ei, ii in zip(e, inv)], [esi * ii for esi, ii in zip(es, inv)]


def _group_norm_all(outs):
    mu = [jnp.mean(o, axis=1, keepdims=True) for o in outs]
    xc = [o - m for o, m in zip(outs, mu)]
    var = [jnp.mean(c * c, axis=1, keepdims=True) for c in xc]
    rstd = [lax.rsqrt(v + GN_EPS) for v in var]
    return [c * r for c, r in zip(xc, rstd)], rstd


def _proj_cols(p_ref, rows, start, stop):
    pieces = []
    while start < stop:
        k = max(j for j in range(N_CHIPS) if WIN_START[j] <= start)
        end = min(stop, WIN_START[k] + SHARD_PAD)
        pieces.append(p_ref[k, rows, start - WIN_START[k]:end - WIN_START[k]])
        start = end
    return pieces[0] if len(pieces) == 1 else jnp.concatenate(pieces, axis=1)


def _retention_operands(p_ref, rows, cos_b, sin_b, qdec, kdec):
    qr = _rotate(_proj_cols(p_ref, rows, OFF_RQ, OFF_RQ + RET_QK_WIDTH).astype(F32), cos_b, sin_b)
    kr = _rotate(_proj_cols(p_ref, rows, OFF_RK, OFF_RK + RET_QK_WIDTH).astype(F32), cos_b, sin_b) * RET_SCALE
    return qr, kr, qr * qdec, kr * kdec


def _tile(t, i):
    return t[:, i * LANE:(i + 1) * LANE]


GATHER_SEMS = 7


def _allgather_steps(src_ref, full_ref, blk_ref, send_sems, recv_sems):
    block_rows = blk_ref.shape[0]
    assert block_rows % BF16_ROWS == 0 and full_ref.shape[0] == N_DEV * block_rows
    x, y, c = lax.axis_index("x"), lax.axis_index("y"), lax.axis_index("c")
    me, sibling = (x, y, c), (x, y, 1 - c)
    chips = [(1 - x, y), (x, 1 - y), (1 - x, 1 - y)]

    def rows(px, py, pc):
        return full_ref.at[pl.ds(pl.multiple_of((4 * px + 2 * py + pc) * block_rows, BF16_ROWS), block_rows), :]

    def copy(k, block, to, src=None):
        return pltpu.make_async_remote_copy(
            src_ref=rows(*block) if src is None else src, dst_ref=rows(*block),
            send_sem=send_sems.at[k], recv_sem=recv_sems.at[k], device_id=to, device_id_type=MESH_ID)

    first = [copy(0, me, sibling, src=blk_ref)] + [copy(1 + j, me, (*chip, c), src=blk_ref) for j, chip in enumerate(chips)]
    passed = [copy(4 + j, (*chip, c), sibling) for j, chip in enumerate(chips)]

    def start():
        blk_ref[...] = src_ref[pl.ds(pl.multiple_of(c * block_rows, BF16_ROWS), block_rows), :].astype(BF16)
        rows(*me)[...] = blk_ref[...]
        for cp in first:
            cp.start()

    def forward():
        for j, chip in enumerate(chips):
            copy(1 + j, (*chip, c), me).wait_recv()
            passed[j].start()

    def finish():
        copy(0, sibling, me).wait_recv()
        for j, chip in enumerate(chips):
            copy(4 + j, (*chip, 1 - c), me).wait_recv()
        for cp in first + passed:
            cp.wait_send()

    return start, forward, finish


WINDOW_HALF = SHARD_PAD // 2
WINDOW_SEMS = GATHER_SEMS + 1


def _window_gather(wi_ref, w4_ref, blk_ref, edge_ref, send_sems, recv_sems):
    x, y, c = lax.axis_index("x"), lax.axis_index("y"), lax.axis_index("c")
    me, sibling, pair = (x, y, c), (x, y, 1 - c), (x, 1 - y, c)
    chips = [(1 - x, y), (x, 1 - y), (1 - x, 1 - y)]
    own_rows, edge_rows = SHARD_IN - WINDOW_HALF, SHARD_SHIFT
    sends_edge = y == c

    def rows(px, py, pc):
        return w4_ref.at[pl.ds(pl.multiple_of((4 * px + 2 * py + pc) * WINDOW_HALF, BF16_ROWS), WINDOW_HALF), :]

    def copy(k, block, to, src=None):
        return pltpu.make_async_remote_copy(
            src_ref=rows(*block) if src is None else src, dst_ref=rows(*block),
            send_sem=send_sems.at[k], recv_sem=recv_sems.at[k], device_id=to, device_id_type=MESH_ID)

    edge_copy = pltpu.make_async_remote_copy(
        src_ref=edge_ref, dst_ref=edge_ref, send_sem=send_sems.at[GATHER_SEMS], recv_sem=recv_sems.at[GATHER_SEMS],
        device_id=pair, device_id_type=MESH_ID)
    first = [copy(0, me, sibling, src=blk_ref)] + [copy(1 + j, me, (*chip, c), src=blk_ref) for j, chip in enumerate(chips)]
    passed = [copy(4 + j, (*chip, c), sibling) for j, chip in enumerate(chips)]

    def at(start, size):
        return pl.ds(pl.multiple_of(start, BF16_ROWS), size)

    def start():
        @pl.when(sends_edge)
        def _():
            edge_ref[...] = wi_ref[at((1 - y) * (SHARD_IN - edge_rows), edge_rows), :].astype(BF16)
            edge_copy.start()
            blk_ref[...] = wi_ref[at(c * own_rows, WINDOW_HALF), :].astype(BF16)

        @pl.when(jnp.logical_not(sends_edge))
        def _():
            blk_ref[at(y * edge_rows, own_rows), :] = wi_ref[at(c * WINDOW_HALF, own_rows), :].astype(BF16)
            edge_copy.wait_recv()
            blk_ref[at((1 - y) * own_rows, edge_rows), :] = edge_ref[...]

        rows(*me)[...] = blk_ref[...]
        for cp in first[:-1]:
            cp.start()
        copy(0, sibling, me).wait_recv()

    def send_far():
        first[-1].start()

    def forward(j):
        copy(1 + j, (*chips[j], c), me).wait_recv()
        passed[j].start()

    def wait_forwarded(j):
        copy(4 + j, (*chips[j], 1 - c), me).wait_recv()

    def finish():
        for cp in first + passed:
            cp.wait_send()

        @pl.when(sends_edge)
        def _():
            edge_copy.wait_send()

    return start, send_far, forward, wait_forwarded, finish


def _in_proj(x, norm_g, w_in_t_shard, w_out_shard, tb):
    seq = x.shape[0]
    nblk = seq // tb
    last = nblk - 1
    chip_of_panel = (None, 1, 0, 2)

    def body(win_ref, x_ref, g_ref, wi_ref, wo_ref, p_ref, ht_ref, wt_ref, wout_ref,
             w4, blk, edge, hbuf, wout_full, blko, send_sems, recv_sems, send_sems_o, recv_sems_o):
        q, i = pl.program_id(0), pl.program_id(1)
        chip = 2 * lax.axis_index("x") + lax.axis_index("y")
        in_start, in_send_far, in_forward, in_wait_forwarded, in_finish = _window_gather(
            wi_ref, w4, blk, edge, send_sems, recv_sems)
        out_start, out_forward, out_finish = _allgather_steps(wo_ref, wout_full, blko, send_sems_o, recv_sems_o)
        rows = pl.ds(pl.multiple_of(i * tb, tb), tb)

        @pl.when(jnp.logical_and(q == 0, i == 0))
        def _():
            in_start()

        @pl.when(jnp.logical_and(q == 0, i == min(last // 2 + 1, last)))
        def _():
            in_send_far()

        @pl.when(jnp.logical_and(q == 1, i == 0))
        def _():
            out_start()

        for panel in range(1, N_CHIPS):
            @pl.when(jnp.logical_and(q == panel, i == 0))
            def _():
                in_wait_forwarded(chip_of_panel[panel])

        @pl.when(q == 0)
        def _():
            xv = x_ref[...]
            r = lax.rsqrt(jnp.mean(xv * xv, axis=1, keepdims=True) + RMS_EPS)
            h = (xv * r) * g_ref[...]
            hbuf[rows, :] = h.astype(BF16)
            ht_ref[...] = h.T.astype(BF16)

        owner = jnp.bitwise_xor(chip, q)
        window = w4[pl.ds(pl.multiple_of(owner * SHARD_PAD, SHARD_PAD), SHARD_PAD), :]
        p_ref[...] = _dot_nt(hbuf[rows, :], window).astype(BF16)

        for panel in range(1, N_CHIPS):
            @pl.when(jnp.logical_and(q == panel - 1, i == last))
            def _():
                in_forward(chip_of_panel[panel])

        @pl.when(jnp.logical_and(q == N_CHIPS - 1, i == 0))
        def _():
            out_forward()

        @pl.when(jnp.logical_and(q == N_CHIPS - 1, i == last))
        def _():
            in_finish()
            for k in range(N_CHIPS):
                first = k * SHARD_PAD + (k % 2) * SHARD_SHIFT
                wt_ref[k * SHARD_IN:(k + 1) * SHARD_IN, :] = w4[first:first + SHARD_IN, :]
            out_finish()
            wout_ref[...] = wout_full[...]

    whole = lambda shape: pl.BlockSpec(shape, lambda q, i, win: (0,) * len(shape), pipeline_mode=pl.Buffered(1))
    first_panel = lambda q, i: jnp.where(q == 0, i, last)
    grid_spec = pltpu.PrefetchScalarGridSpec(
        num_scalar_prefetch=1, grid=(N_CHIPS, nblk),
        in_specs=[pl.BlockSpec((tb, D_MODEL), lambda q, i, win: (first_panel(q, i), 0)), whole((1, D_MODEL)),
                  whole((SHARD_IN, D_MODEL)), whole((SHARD_OUT, D_MODEL))],
        out_specs=(pl.BlockSpec((None, tb, SHARD_PAD), lambda q, i, win: (win[q], i, 0)),
                   pl.BlockSpec((D_MODEL, tb), lambda q, i, win: (0, first_panel(q, i))),
                   whole((IN_WIDTH, D_MODEL)), whole((MIX_WIDTH, D_MODEL))),
        scratch_shapes=[
            pltpu.VMEM((N_CHIPS * SHARD_PAD, D_MODEL), BF16), pltpu.VMEM((WINDOW_HALF, D_MODEL), BF16),
            pltpu.VMEM((SHARD_SHIFT, D_MODEL), BF16), pltpu.VMEM((seq, D_MODEL), BF16),
            pltpu.VMEM((MIX_WIDTH, D_MODEL), BF16), pltpu.VMEM((SHARD_OUT // 2, D_MODEL), BF16),
            pltpu.SemaphoreType.DMA((WINDOW_SEMS,)), pltpu.SemaphoreType.DMA((WINDOW_SEMS,)),
            pltpu.SemaphoreType.DMA((GATHER_SEMS,)), pltpu.SemaphoreType.DMA((GATHER_SEMS,)),
        ])
    chip = 2 * lax.axis_index("x") + lax.axis_index("y")
    owner = jnp.bitwise_xor(chip, jnp.arange(N_CHIPS, dtype=jnp.int32))
    return _call(
        body, name="in_proj", grid_spec=grid_spec,
        out_shape=(jax.ShapeDtypeStruct((N_CHIPS, seq, SHARD_PAD), BF16), jax.ShapeDtypeStruct((D_MODEL, seq), BF16),
                   jax.ShapeDtypeStruct((IN_WIDTH, D_MODEL), BF16), jax.ShapeDtypeStruct((MIX_WIDTH, D_MODEL), BF16)),
        compiler_params=_params(60, ("arbitrary", "arbitrary")),
    )(owner.astype(jnp.int32), x, norm_g, w_in_t_shard, w_out_shard)


def _mix_fwd(proj, x, target, w_out, final_g, gn_gain, sinks, tables, tb):
    seq = x.shape[0]
    nsub = tb // BLK
    cos_t, sin_t, decay_in, qdec_t, kdec_t, cd_t, bias_t = tables

    def body(p_ref, x_ref, t_ref, cos_ref, sin_ref, wout_ref, fg_ref, gain_ref, sinks_ref, din_ref, qdec_ref,
             kdec_ref, cd_ref, bias_ref, mix_ref, dxo_ref, st_ref, loss_ref, gfin_ref, oatt_ref, probs_ref, psink_ref,
             kprev_ref, vprev_ref, state_ref):
        i = pl.program_id(0)

        @pl.when(i == 0)
        def _():
            kprev_ref[...] = jnp.zeros_like(kprev_ref)
            vprev_ref[...] = jnp.zeros_like(vprev_ref)
            state_ref[...] = jnp.zeros_like(state_ref)
            loss_ref[...] = jnp.zeros_like(loss_ref)
            gfin_ref[...] = jnp.zeros_like(gfin_ref)

        prev = _in_previous_block()

        def sub(j, carry):
            kp, vp, states = carry
            rows = pl.ds(pl.multiple_of(j * BLK, BLK), BLK)
            bias = bias_ref[jnp.where(jnp.logical_or(i > 0, j > 0), 1, 0)]

            aq = _proj_cols(p_ref, rows, OFF_AQ, OFF_AQ + ATT_WIDTH)
            ak = _proj_cols(p_ref, rows, OFF_AK, OFF_AK + ATT_KV_WIDTH).astype(F32)
            av = _proj_cols(p_ref, rows, OFF_AV, OFF_AV + ATT_KV_WIDTH).astype(F32)
            az = _proj_cols(p_ref, rows, OFF_AZ, OFF_AZ + ATT_WIDTH).astype(F32)
            k_ops = _kv_operands(jnp.concatenate([kp, ak], axis=0))
            v_ops = _kv_operands(jnp.concatenate([vp, av], axis=0))
            qs = [(_stack_tiles(aq, 2 * g) * ATT_SCALE).astype(BF16) for g in range(ATT_KV_HEADS)]

            qr, kr, qd, kd = _retention_operands(p_ref, rows, cos_ref[rows, :], sin_ref[rows, :],
                                                 qdec_ref[...], kdec_ref[...])
            s = _attn_scores(qs, k_ops, bias, prev)
            heads = [(t, hh) for t in range(PAIRS) for hh in range(2)]
            sc = [_dot_nt(jnp.concatenate(_split_heads(_tile(qr, t)), axis=0).astype(BF16), _tile(kr, t).astype(BF16))
                  * din_ref[t] for t in range(PAIRS)]
            qd_heads = [_split_heads(_tile(qd, t)) for t in range(PAIRS)]
            state_b = [states[t].astype(BF16) for t in range(PAIRS)]
            vs = [_proj_cols(p_ref, rows, OFF_RV + h * RET_V_DIM, OFF_RV + (h + 1) * RET_V_DIM) for h in range(RET_HEADS)]
            rzs = [_proj_cols(p_ref, rows, OFF_RZ + h * RET_V_DIM, OFF_RZ + (h + 1) * RET_V_DIM).astype(F32)
                   for h in range(RET_HEADS)]
            lhs = [jnp.concatenate([sc[t][hh * BLK:(hh + 1) * BLK].astype(BF16), qd_heads[t][hh].astype(BF16)], axis=1)
                   for t, hh in heads]
            p, p_sink = _attn_softmax(s, sinks_ref)
            o_ret = [_dot(lhs[2 * t + hh], jnp.concatenate([vs[2 * t + hh], state_b[t]], axis=0)) for t, hh in heads]
            p = [pi.astype(BF16) for pi in p]
            o_tiles = []
            for g in range(ATT_KV_HEADS):
                p_cat = jnp.concatenate([_unfold(p[2 * g], prev), _unfold(p[2 * g + 1], prev)], axis=0)
                o = _dot_tn(p_cat, jnp.concatenate(v_ops[g], axis=0))
                o_tiles += [o[0:BLK], o[BLK:]]
            ons, _ = _group_norm_all(o_ret)
            new_states = [states[t] * cd_ref[t]
                          + _dot_tn(jnp.concatenate(_split_heads(_tile(kd, t)), axis=0).astype(BF16),
                                    jnp.concatenate([vs[2 * t], vs[2 * t + 1]], axis=0)) for t in range(PAIRS)]
            o_att = jnp.concatenate(o_tiles, axis=1)
            out = [o_att * (az * _sigmoid(az))]
            out += [(ons[h] * gain_ref[:, h * RET_V_DIM:(h + 1) * RET_V_DIM]) * (rzs[h] * _sigmoid(rzs[h]))
                    for h in range(RET_HEADS)]
            mix_ref[rows, :] = jnp.concatenate(out, axis=1).astype(BF16)
            oatt_ref[rows, :] = o_att
            probs_ref[j] = jnp.stack(p)
            psink_ref[j] = jnp.concatenate(p_sink, axis=0)
            st_ref[j] = jnp.stack(states)
            return ak, av, tuple(new_states)

        carry = (kprev_ref[...], vprev_ref[...], tuple(state_ref[t] for t in range(PAIRS)))
        for jb in range(nsub):
            carry = sub(jnp.int32(jb), carry)
        kp, vp, states = carry
        kprev_ref[...] = kp
        vprev_ref[...] = vp
        state_ref[...] = jnp.stack(states)

        xo = x_ref[...] + _dot(mix_ref[...], wout_ref[...])
        r2 = lax.rsqrt(jnp.mean(xo * xo, axis=1, keepdims=True) + RMS_EPS)
        xn = xo * r2
        err = xn * fg_ref[...] - t_ref[...]
        loss_ref[...] += jnp.sum(err * err) * (0.5 / D_MODEL)
        dy = err * (1.0 / D_MODEL)
        gfin_ref[...] += jnp.sum(dy * xn, axis=0, keepdims=True)
        u = dy * fg_ref[...]
        dxo_ref[...] = r2 * u - xn * (r2 * jnp.mean(u * xn, axis=1, keepdims=True))

    blk_rows = lambda w: pl.BlockSpec((tb, w), lambda i: (i, 0))
    state_shape = (PAIRS, 2 * RET_QK_DIM, RET_V_DIM)
    return _call(
        body, name="mix_fwd", grid=(seq // tb,),
        out_shape=(
            jax.ShapeDtypeStruct((seq, MIX_WIDTH), BF16),
            jax.ShapeDtypeStruct((seq, D_MODEL), F32),
            jax.ShapeDtypeStruct((seq // BLK,) + state_shape, F32),
            jax.ShapeDtypeStruct((8, LANE), F32),
            jax.ShapeDtypeStruct((1, D_MODEL), F32),
            jax.ShapeDtypeStruct((seq, ATT_WIDTH), F32),
            jax.ShapeDtypeStruct((seq // BLK, len(ATT_PROBLEMS), BLK, 2 * BLK), BF16),
            jax.ShapeDtypeStruct((seq // BLK, len(ATT_PROBLEMS), 2 * BLK), F32),
        ),
        in_specs=[
            pl.BlockSpec((N_CHIPS, tb, SHARD_PAD), lambda i: (0, i, 0)),
            blk_rows(D_MODEL), blk_rows(D_MODEL), blk_rows(LANE), blk_rows(LANE),
            _const_spec((MIX_WIDTH, D_MODEL)), _const_spec((1, D_MODEL)), _const_spec((1, RET_WIDTH)),
            pl.BlockSpec(memory_space=pltpu.SMEM),
            _const_spec((PAIRS, 2 * BLK, BLK)), _const_spec((BLK, RET_QK_WIDTH)), _const_spec((BLK, RET_QK_WIDTH)),
            _const_spec(state_shape), _const_spec((2, BLK, 2 * BLK)),
        ],
        out_specs=(
            blk_rows(MIX_WIDTH), blk_rows(D_MODEL),
            pl.BlockSpec((nsub,) + state_shape, lambda i: (i, 0, 0, 0)),
            _const_spec((8, LANE)), _const_spec((1, D_MODEL)), blk_rows(ATT_WIDTH),
            pl.BlockSpec((nsub, len(ATT_PROBLEMS), BLK, 2 * BLK), lambda i: (i, 0, 0, 0)),
            pl.BlockSpec((nsub, len(ATT_PROBLEMS), 2 * BLK), lambda i: (i, 0, 0)),
        ),
        scratch_shapes=[
            pltpu.VMEM((BLK, ATT_KV_WIDTH), F32), pltpu.VMEM((BLK, ATT_KV_WIDTH), F32),
            pltpu.VMEM(state_shape, F32),
        ],
        compiler_params=_params(48, ("arbitrary",)),
    )(proj, x, target, cos_t, sin_t, w_out, final_g, gn_gain, sinks, decay_in, qdec_t, kdec_t, cd_t, bias_t)


def _mix_bwd(proj, dxo, mix, o_att, probs, p_sinks, states, x, w_out, w_in_t, norm_g, gn_gain, tables, tb):
    seq = dxo.shape[0]
    nsub = tb // BLK
    nblk = seq // tb
    cos_t, sin_t, decay_in, qdec_t, kdec_t, cd_t, _ = tables
    kv_cols = OFF_AK // (2 * ATT_KV_WIDTH)
    state_shape = (PAIRS, 2 * RET_QK_DIM, RET_V_DIM)

    def body(p_ref, pkv_ref, dxo_ref, mix_ref, oatt_ref, probs_ref, psink_ref, st_ref, cos_ref, sin_ref, x_ref, wout_ref,
             win_ref, g_ref, gain_ref, din_ref, qdec_ref, kdec_ref, cd_ref,
             dp_ref, gx_ref, gwout_ref, gnorm_ref, dgain_ref, dsink_ref,
             dmix_ref, kv_ref, dkc_ref, dvc_ref, gst_ref):
        i = pl.program_id(0)

        @pl.when(i == 0)
        def _():
            gwout_ref[...] = jnp.zeros_like(gwout_ref)
            gnorm_ref[...] = jnp.zeros_like(gnorm_ref)
            dgain_ref[...] = jnp.zeros_like(dgain_ref)
            dsink_ref[...] = jnp.zeros_like(dsink_ref)
            dkc_ref[...] = jnp.zeros_like(dkc_ref)
            dvc_ref[...] = jnp.zeros_like(dvc_ref)
            gst_ref[...] = jnp.zeros_like(gst_ref)

        dxo_b = dxo_ref[...].astype(BF16)
        dmix_ref[...] = _dot_nt(dxo_b, wout_ref[...])
        gwout_ref[...] += _dot_tn(mix_ref[...], dxo_b)
        kv_ref[0:BLK, :] = pkv_ref[...].astype(F32)
        kv_ref[BLK:, :] = _proj_cols(p_ref, slice(None), OFF_AK, OFF_AK + 2 * ATT_KV_WIDTH).astype(F32)
        low = _low_lanes((BLK, LANE))
        low2 = _low_lanes((2 * BLK, LANE))
        lane = lax.broadcasted_iota(jnp.int32, (1, LANE), 1)
        prev = _in_previous_block()

        def sub(jj, carry):
            dkc, dvc, gsts, dgain, dsink = carry
            j = nsub - 1 - jj
            rows = pl.ds(pl.multiple_of(j * BLK, BLK), BLK)
            both = pl.ds(pl.multiple_of(j * BLK, BLK), 2 * BLK)


            aq = _proj_cols(p_ref, rows, OFF_AQ, OFF_AQ + ATT_WIDTH)
            az = _proj_cols(p_ref, rows, OFF_AZ, OFF_AZ + ATT_WIDTH).astype(F32)
            k_ops = _kv_operands(kv_ref[both, 0:ATT_KV_WIDTH])
            v_ops = _kv_operands(kv_ref[both, ATT_KV_WIDTH:2 * ATT_KV_WIDTH])
            da = dmix_ref[rows, 0:ATT_WIDTH]
            sig = _sigmoid(az)
            d_o = da * (az * sig)
            qs = [(_stack_tiles(aq, 2 * g) * ATT_SCALE).astype(BF16) for g in range(ATT_KV_HEADS)]
            dos = [_stack_tiles(d_o, 2 * g).astype(BF16) for g in range(ATT_KV_HEADS)]
            p_b = [probs_ref[j, k] for k in range(len(ATT_PROBLEMS))]
            p = [pk.astype(F32) for pk in p_b]
            sink_all = psink_ref[j]
            p_sink = [sink_all[k:k + 1, :] for k in range(len(ATT_PROBLEMS))]
            cos_b, sin_b = cos_ref[rows, :], sin_ref[rows, :]
            qdec, kdec = qdec_ref[...], kdec_ref[...]
            qr, kr, qd, kd = _retention_operands(p_ref, rows, cos_b, sin_b, qdec, kdec)
            heads = [(t, hh) for t in range(PAIRS) for hh in range(2)]
            head_cols = [slice(h * RET_V_DIM, (h + 1) * RET_V_DIM) for h in range(RET_HEADS)]
            q_rows = [jnp.concatenate(_split_heads(_tile(qr, t)), axis=0).astype(BF16) for t in range(PAIRS)]
            k_rows = [jnp.concatenate(_split_heads(_tile(kr, t)), axis=0).astype(BF16) for t in range(PAIRS)]
            din = [din_ref[t] for t in range(PAIRS)]

            dpr = [_fold(_dot_nt(v_ops[g][hi], dos[g]), prev) for g, hi in ATT_PROBLEMS]
            sc = [(_dot_nt(q_rows[t], _tile(kr, t).astype(BF16)) * din[t]).astype(BF16) for t in range(PAIRS)]
            qd_heads = [_split_heads(_tile(qd, t)) for t in range(PAIRS)]
            kd_heads = [_split_heads(_tile(kd, t)) for t in range(PAIRS)]
            state_b = [st_ref[j, t].astype(BF16) for t in range(PAIRS)]
            gst_b = [gsts[t].astype(BF16) for t in range(PAIRS)]
            vs = [_proj_cols(p_ref, rows, OFF_RV + h * RET_V_DIM, OFF_RV + (h + 1) * RET_V_DIM) for h in range(RET_HEADS)]
            rzs = [_proj_cols(p_ref, rows, OFF_RZ + h * RET_V_DIM, OFF_RZ + (h + 1) * RET_V_DIM).astype(F32)
                   for h in range(RET_HEADS)]
            drs = [dmix_ref[rows, ATT_WIDTH + h * RET_V_DIM:ATT_WIDTH + (h + 1) * RET_V_DIM] for h in range(RET_HEADS)]
            gains = [gain_ref[:, c] for c in head_cols]
            lhs = [jnp.concatenate([sc[t][hh * BLK:(hh + 1) * BLK], qd_heads[t][hh].astype(BF16)], axis=1) for t, hh in heads]
            rhs = [jnp.concatenate([vs[2 * t + hh], state_b[t]], axis=0) for t, hh in heads]

            delta = [jnp.sum(pi * di, axis=0, keepdims=True) for pi, di in zip(p, dpr)]
            ds = [_unfold((pi * (di - ti)).astype(BF16), prev) for pi, di, ti in zip(p, dpr, delta)]
            o_ret = [_dot(l, r) for l, r in zip(lhs, rhs)]
            for (g, hi), ki, ti in zip(ATT_PROBLEMS, p_sink, delta):
                sink_part = ki * ti
                for half in range(2):
                    tot = jnp.sum(sink_part[:, half * BLK:(half + 1) * BLK], axis=1, keepdims=True)
                    dsink = dsink - jnp.where(lane == 4 * g + 2 * half + hi, tot, 0.0)

            dq_tiles, dk_sums, dv_sums = [], [], []
            for g in range(ATT_KV_HEADS):
                ds_cat = jnp.concatenate([ds[2 * g], ds[2 * g + 1]], axis=0)
                p_cat = jnp.concatenate([_unfold(p_b[2 * g], prev), _unfold(p_b[2 * g + 1], prev)], axis=0)
                dqs = _dot_tn(ds_cat, jnp.concatenate(k_ops[g], axis=0)) * ATT_SCALE
                dq_tiles += [dqs[0:BLK], dqs[BLK:]]
                dk_sums.append(_dot(ds_cat, qs[g]))
                dv_sums.append(_dot(p_cat, dos[g]))
            ons, rstds = _group_norm_all(o_ret)
            sig_r = [_sigmoid(z) for z in rzs]
            dgn = [d * (z * g) for d, z, g in zip(drs, rzs, sig_r)]
            dz_parts = [d * (o * gn) * (g * (1.0 + z * (1.0 - g))) for d, o, gn, g, z in zip(drs, ons, gains, sig_r, rzs)]
            dgain_parts = [jnp.sum(d * o, axis=0, keepdims=True) for d, o in zip(dgn, ons)]
            don = [d * gn for d, gn in zip(dgn, gains)]
            mean_don = [jnp.mean(d, axis=1, keepdims=True) for d in don]
            mean_don_on = [jnp.mean(d * o, axis=1, keepdims=True) for d, o in zip(don, ons)]
            dob = [(r * (d - a - o * b)).astype(BF16) for r, d, a, o, b in zip(rstds, don, mean_don, ons, mean_don_on)]

            dlhs = [_dot_nt(d, r) for d, r in zip(dob, rhs)]
            drhs = [_dot_tn(l, d) for l, d in zip(lhs, dob)]
            dkds = [_dot_nt(vs[2 * t + hh], gst_b[t]) for t, hh in heads]
            dv_parts = [drhs[2 * t + hh][0:BLK] + _dot(kd_heads[t][hh].astype(BF16), gst_b[t]) for t, hh in heads]
            daz = da * oatt_ref[rows, :] * (sig * (1.0 + az * (1.0 - sig)))

            def kv_grad(sums):
                (a0, b0), (a1, b1) = [(s[0:2 * BLK], s[2 * BLK:]) for s in sums]
                return jnp.where(low2, a0, b1) + pltpu.roll(jnp.where(low2, a1, b0), HALF_LANE, 1)

            dk_both, dv_both = kv_grad(dk_sums), kv_grad(dv_sums)
            dak = dk_both[BLK:] + dkc
            dav = dv_both[BLK:] + dvc
            das = [(dlhs[2 * t + hh][:, 0:BLK] * din[t][hh * BLK:(hh + 1) * BLK]).astype(BF16) for t, hh in heads]
            new_gsts = [gsts[t] * cd_ref[t] + drhs[2 * t][BLK:] + drhs[2 * t + 1][BLK:] for t in range(PAIRS)]
            dq_parts = [_dot(jnp.concatenate([das[2 * t], das[2 * t + 1]], axis=1), k_rows[t])
                        + jnp.where(low, dlhs[2 * t][:, BLK:], dlhs[2 * t + 1][:, BLK:]) * _tile(qdec, t)
                        for t in range(PAIRS)]
            dk_parts = [_dot_tn(jnp.concatenate([das[2 * t], das[2 * t + 1]], axis=0), q_rows[t])
                        + jnp.where(low, dkds[2 * t], dkds[2 * t + 1]) * _tile(kdec, t) for t in range(PAIRS)]
            drq = _rotate_transposed(jnp.concatenate(dq_parts, axis=1), cos_b, sin_b)
            drk = _rotate_transposed(jnp.concatenate(dk_parts, axis=1) * RET_SCALE, cos_b, sin_b)

            dp_ref[rows, :] = jnp.concatenate(
                [jnp.concatenate(dq_tiles, axis=1), dak, dav, daz, drq, drk] + dv_parts + dz_parts, axis=1).astype(BF16)
            dgain = dgain + jnp.concatenate(dgain_parts, axis=1)
            return dk_both[0:BLK], dv_both[0:BLK], tuple(new_gsts), dgain, dsink

        carry = (dkc_ref[...], dvc_ref[...], tuple(gst_ref[t] for t in range(PAIRS)), dgain_ref[...], dsink_ref[...])
        for jb in range(nsub):
            carry = sub(jnp.int32(jb), carry)
        dkc, dvc, gsts, dgain, dsink = carry
        dkc_ref[...] = dkc
        dvc_ref[...] = dvc
        gst_ref[...] = jnp.stack(gsts)
        dgain_ref[...] = dgain
        dsink_ref[...] = dsink

        dh = _dot(dp_ref[...], win_ref[...])
        xv = x_ref[...]
        r = lax.rsqrt(jnp.mean(xv * xv, axis=1, keepdims=True) + RMS_EPS)
        xn = xv * r
        gnorm_ref[...] += jnp.sum(dh * xn, axis=0, keepdims=True)
        u = dh * g_ref[...]
        gx_ref[...] = dxo_ref[...] + r * u - xn * (r * jnp.mean(u * xn, axis=1, keepdims=True))

    rev_rows = lambda w: pl.BlockSpec((tb, w), lambda i: (nblk - 1 - i, 0))
    prev_kv = pl.BlockSpec((None, BLK, 2 * ATT_KV_WIDTH),
                           lambda i: (0, jnp.maximum((nblk - 1 - i) * nsub - 1, 0), kv_cols))
    return _call(
        body, name="mix_bwd", grid=(nblk,),
        out_shape=(
            jax.ShapeDtypeStruct((seq, IN_WIDTH), BF16),
            jax.ShapeDtypeStruct((seq, D_MODEL), F32),
            jax.ShapeDtypeStruct((MIX_WIDTH, D_MODEL), F32),
            jax.ShapeDtypeStruct((1, D_MODEL), F32),
            jax.ShapeDtypeStruct((1, RET_WIDTH), F32),
            jax.ShapeDtypeStruct((1, LANE), F32),
        ),
        in_specs=[
            pl.BlockSpec((N_CHIPS, tb, SHARD_PAD), lambda i: (0, nblk - 1 - i, 0)),
            prev_kv, rev_rows(D_MODEL), rev_rows(MIX_WIDTH), rev_rows(ATT_WIDTH),
            pl.BlockSpec((nsub, len(ATT_PROBLEMS), BLK, 2 * BLK), lambda i: (nblk - 1 - i, 0, 0, 0)),
            pl.BlockSpec((nsub, len(ATT_PROBLEMS), 2 * BLK), lambda i: (nblk - 1 - i, 0, 0)),
            pl.BlockSpec((nsub,) + state_shape, lambda i: (nblk - 1 - i, 0, 0, 0)),
            rev_rows(LANE), rev_rows(LANE), rev_rows(D_MODEL),
            _const_spec((MIX_WIDTH, D_MODEL)), _const_spec((IN_WIDTH, D_MODEL)), _const_spec((1, D_MODEL)),
            _const_spec((1, RET_WIDTH)),
            _const_spec((PAIRS, 2 * BLK, BLK)), _const_spec((BLK, RET_QK_WIDTH)), _const_spec((BLK, RET_QK_WIDTH)),
            _const_spec(state_shape),
        ],
        out_specs=(
            rev_rows(IN_WIDTH), rev_rows(D_MODEL), _const_spec((MIX_WIDTH, D_MODEL)), _const_spec((1, D_MODEL)),
            _const_spec((1, RET_WIDTH)), _const_spec((1, LANE)),
        ),
        scratch_shapes=[
            pltpu.VMEM((tb, MIX_WIDTH), F32),
            pltpu.VMEM((tb + BLK, 2 * ATT_KV_WIDTH), F32),
            pltpu.VMEM((BLK, ATT_KV_WIDTH), F32), pltpu.VMEM((BLK, ATT_KV_WIDTH), F32),
            pltpu.VMEM(state_shape, F32),
        ],
        compiler_params=_params(60, ("arbitrary",)),
    )(proj, proj, dxo, mix, o_att, probs, p_sinks, states, cos_t, sin_t, x, w_out, w_in_t, norm_g, gn_gain, decay_in,
      qdec_t, kdec_t, cd_t)


WINDOW_XOR = (3, 2, 1, 0)


def _gw_in_reduce(h_t, dproj, gw_out, small, tb):
    seq = dproj.shape[0]
    nblk = seq // tb
    last = nblk - 1
    hand_on = min(1, last)
    half = D_MODEL // 2
    A, B, C, N_SEMS = 0, N_CHIPS, 2 * N_CHIPS, 2 * N_CHIPS + 1

    def body(win_ref, ht_ref, dp_ref, gwo_hbm, s0_ref, s1_ref, s2_ref, s3_ref, s4_ref, out_ref, fout_ref, packsum_ref,
             acc, sib, send_buf, b_in, fin, fout, mine_out, sib_out, send_out, b_out, pack_ref, packs,
             send_sems, recv_sems, local_sems):
        p, i = pl.program_id(0), pl.program_id(1)
        small_start, small_hand_on, small_finish = _small_exchange(
            gwo_hbm, (s0_ref, s1_ref, s2_ref, s3_ref, s4_ref), fout, mine_out, sib_out, send_out, b_out, pack_ref,
            packs, send_sems, recv_sems, local_sems, N_SEMS)

        @pl.when(jnp.logical_and(p == 0, i == 0))
        def _():
            small_start()

        @pl.when(jnp.logical_and(p == 0, i == hand_on))
        def _():
            small_hand_on()

        x, y, c = lax.axis_index("x"), lax.axis_index("y"), lax.axis_index("c")
        chip = 2 * x + y
        sibling = (x, y, 1 - c)
        mine = pl.ds(pl.multiple_of(c * half, half), half)
        other = pl.ds(pl.multiple_of((1 - c) * half, half), half)

        def remote(src, dst, send_k, recv_k, to):
            return pltpu.make_async_remote_copy(src_ref=src, dst_ref=dst, send_sem=send_sems.at[send_k],
                                                recv_sem=recv_sems.at[recv_k], device_id=to, device_id_type=MESH_ID)

        slot = p % 2

        @pl.when(i == 0)
        def _():
            acc[slot] = jnp.zeros(acc.shape[1:], F32)

        acc[slot] += _dot(ht_ref[...], dp_ref[...])

        first_axis = 2 - c
        for q in range(N_CHIPS):
            s = q % 2
            to_sibling = remote(acc.at[s, other, :], sib.at[s], A + q, A + q, sibling)

            @pl.when(jnp.logical_and(p == q, i == last))
            def _():
                to_sibling.start()

            if q < N_CHIPS - 1:
                hop = first_axis if q == 0 else WINDOW_XOR[q]
                into = 0 if q == 0 else WINDOW_XOR[q]
                dest = chip ^ hop

                @pl.when(jnp.logical_and(p == q + 1, i == hand_on))
                def _():
                    to_sibling.wait_recv()
                    if q == 0:
                        send_buf[q] = (acc[s, mine, :] + sib[s]).astype(BF16)
                    else:
                        @pl.when(hop != first_axis)
                        def _():
                            remote(b_in.at[0], b_in.at[0], B, B, sibling).wait_recv()
                            send_buf[q] = (acc[s, mine, :] + sib[s] + b_in[0].astype(F32)).astype(BF16)

                        @pl.when(hop == first_axis)
                        def _():
                            send_buf[q] = (acc[s, mine, :] + sib[s]).astype(BF16)

                    remote(send_buf.at[q], b_in.at[into], B + q, B + into, (dest // 2, dest % 2, c)).start()
                    to_sibling.wait_send()
            else:
                @pl.when(jnp.logical_and(p == q, i == last))
                def _():
                    to_sibling.wait_recv()
                    fin[mine, :] = acc[s, mine, :] + sib[s]
                    for j in (1, 2):
                        remote(b_in.at[j], b_in.at[j], B + j, B + j, sibling).wait_recv()
                        fin[mine, :] += b_in[j].astype(F32)
                    to_core = remote(fin.at[mine, :], fin.at[mine, :], C, C, sibling)
                    to_core.start()
                    remote(fin.at[other, :], fin.at[other, :], C, C, sibling).wait_recv()
                    out_ref[...] = fin[...]
                    to_core.wait_send()
                    to_sibling.wait_send()
                    for k in range(N_CHIPS - 1):
                        remote(send_buf.at[k], b_in.at[k], B + k, B + k, sibling).wait_send()
                    packsum_ref[...] = small_finish()
                    fout_ref[...] = fout[...]

    whole = lambda shape: pl.BlockSpec(shape, lambda p, i, win: (0,) * len(shape), pipeline_mode=pl.Buffered(1))
    grid_spec = pltpu.PrefetchScalarGridSpec(
        num_scalar_prefetch=1, grid=(N_CHIPS, nblk),
        in_specs=[pl.BlockSpec((D_MODEL, tb), lambda p, i, win: (0, i)),
                  pl.BlockSpec((pl.Element(tb), pl.Element(SHARD_PAD)),
                               lambda p, i, win: (i * tb, pl.multiple_of(win[p] * LANE, LANE))),
                  pl.BlockSpec(memory_space=pl.ANY)] + [whole(s.shape) for s in small],
        out_specs=(whole((D_MODEL, SHARD_PAD)), whole((SHARD_OUT, D_MODEL)), whole((PACK_ROWS, D_MODEL))),
        scratch_shapes=[
            pltpu.VMEM((2, D_MODEL, SHARD_PAD), F32), pltpu.VMEM((2, half, SHARD_PAD), F32),
            pltpu.VMEM((N_CHIPS - 1, half, SHARD_PAD), BF16), pltpu.VMEM((N_CHIPS, half, SHARD_PAD), BF16),
            pltpu.VMEM((D_MODEL, SHARD_PAD), F32),
        ] + _small_exchange_scratch() + [
            pltpu.SemaphoreType.DMA((N_SEMS + SMALL_SEMS,)), pltpu.SemaphoreType.DMA((N_SEMS + SMALL_SEMS,)),
            pltpu.SemaphoreType.DMA((N_CHIPS,)),
        ])
    chip = 2 * lax.axis_index("x") + lax.axis_index("y")
    owner = chip ^ jnp.array(WINDOW_XOR, dtype=jnp.int32)
    win_start = (owner * SHARD_IN) // LANE
    return _call(
        body, name="gw_in_reduce", grid_spec=grid_spec,
        out_shape=(jax.ShapeDtypeStruct((D_MODEL, SHARD_PAD), F32), jax.ShapeDtypeStruct((SHARD_OUT, D_MODEL), F32),
                   jax.ShapeDtypeStruct((PACK_ROWS, D_MODEL), F32)),
        compiler_params=_params(52, ("arbitrary", "arbitrary")),
    )(win_start.astype(jnp.int32), h_t, dproj, gw_out, *small)


SMALL_SEMS = 17


def _small_exchange(gwo_hbm, small_refs, fout_ref, mine_out, sib_out, send_out, b_out, pack_ref, packs,
                    send_sems, recv_sems, local_sems, base):
    half_out = SHARD_OUT // 2
    A_OUT, B_OUT, C_OUT, PACK = base, base + 4, base + 8, base + 9
    assert len(small_refs) == PACK_PARTS
    x, y, c = lax.axis_index("x"), lax.axis_index("y"), lax.axis_index("c")
    chip = 2 * x + y
    dev = 2 * chip + c
    sibling = (x, y, 1 - c)

    def remote(src, dst, send_k, recv_k, to):
        return pltpu.make_async_remote_copy(src_ref=src, dst_ref=dst, send_sem=send_sems.at[send_k],
                                            recv_sem=recv_sems.at[recv_k], device_id=to, device_id_type=MESH_ID)

    def out_rows(j, core):
        return pl.ds(pl.multiple_of(j * SHARD_OUT + core * half_out, half_out), half_out)

    my_out_rows = pl.ds(pl.multiple_of(c * half_out, half_out), half_out)
    local = [pltpu.make_async_copy(gwo_hbm.at[out_rows(j, c), :], mine_out.at[j], local_sems.at[j])
             for j in range(N_CHIPS)]
    stage_a = [remote(gwo_hbm.at[out_rows(j, 1 - c), :], sib_out.at[j], A_OUT + j, A_OUT + j, sibling)
               for j in range(N_CHIPS)]
    mine_half_out = fout_ref.at[my_out_rows, :]
    stage_c = [remote(mine_half_out, mine_half_out, C_OUT, C_OUT, sibling)]

    def start():
        pack_ref[...] = jnp.zeros_like(pack_ref)
        for k, s_ref in enumerate(small_refs):
            pack_ref[k:k + 1, 0:s_ref.shape[1]] = s_ref[0:1, :]
        packs[dev] = pack_ref[...]
        for d in range(N_DEV):
            to = (d // 4, (d // 2) % 2, d % 2)

            @pl.when(d != dev)
            def _():
                remote(pack_ref, packs.at[dev], PACK + d, PACK + dev, to).start()

        for cp in local + stage_a:
            cp.start()

    def hand_on():
        for cp in local:
            cp.wait()
        for cp in stage_a:
            cp.wait_recv()
        for j in range(N_CHIPS):
            mine_out[j] = mine_out[j] + sib_out[j]

        for j in range(N_CHIPS):
            to = (j // 2, j % 2, c)

            @pl.when(j != chip)
            def _():
                send_out[j] = mine_out[j].astype(BF16)
                remote(send_out.at[j], b_out.at[chip], B_OUT + j, B_OUT + chip, to).start()

            @pl.when(j == chip)
            def _():
                fout_ref[my_out_rows, :] = mine_out[j]

    def finish():
        for j in range(N_CHIPS):
            @pl.when(j != chip)
            def _():
                remote(b_out.at[j], b_out.at[j], B_OUT + j, B_OUT + j, sibling).wait_recv()
                fout_ref[my_out_rows, :] += b_out[j].astype(F32)

        for cp in stage_c:
            cp.start()
        other_half_out = fout_ref.at[pl.ds(pl.multiple_of((1 - c) * half_out, half_out), half_out), :]
        remote(other_half_out, other_half_out, C_OUT, C_OUT, sibling).wait_recv()

        for d in range(N_DEV):
            @pl.when(d != dev)
            def _():
                remote(pack_ref, packs.at[d], PACK + d, PACK + d, sibling).wait_recv()
        total = packs[0]
        for d in range(1, N_DEV):
            total = total + packs[d]

        for cp in stage_a + stage_c:
            cp.wait_send()
        for j in range(N_CHIPS):
            @pl.when(j != chip)
            def _():
                remote(b_out.at[j], b_out.at[j], B_OUT + j, B_OUT + j, sibling).wait_send()
        for d in range(N_DEV):
            @pl.when(d != dev)
            def _():
                remote(pack_ref, packs.at[d], PACK + d, PACK + d, sibling).wait_send()
        return total

    return start, hand_on, finish


def _small_exchange_scratch():
    half_out = SHARD_OUT // 2
    return [
        pltpu.VMEM((SHARD_OUT, D_MODEL), F32),
        pltpu.VMEM((N_CHIPS, half_out, D_MODEL), F32), pltpu.VMEM((N_CHIPS, half_out, D_MODEL), F32),
        pltpu.VMEM((N_CHIPS, half_out, D_MODEL), BF16), pltpu.VMEM((N_CHIPS, half_out, D_MODEL), BF16),
        pltpu.VMEM((PACK_ROWS, D_MODEL), F32), pltpu.VMEM((N_DEV, PACK_ROWS, D_MODEL), F32),
    ]


def _adam_math(w, g, m, v):
    mn = ADAM_B1 * m + (1.0 - ADAM_B1) * g
    vn = ADAM_B2 * v + (1.0 - ADAM_B2) * (g * g)
    m_hat = mn / (1.0 - ADAM_B1 ** ADAM_STEP)
    v_hat = vn / (1.0 - ADAM_B2 ** ADAM_STEP)
    return -ADAM_LR * (m_hat / (jnp.sqrt(v_hat) + ADAM_EPS) + ADAM_WD * w), mn, vn


def _adamw(name, w, g, m, v, tb):
    rows, cols = w.shape

    def body(w_ref, g_ref, m_ref, v_ref, go_ref, d_ref, mo_ref, vo_ref):
        gv = g_ref[...]
        go_ref[...] = gv
        d_ref[...], mo_ref[...], vo_ref[...] = _adam_math(w_ref[...], gv, m_ref[...], v_ref[...])

    spec = pl.BlockSpec((tb, cols), lambda i: (i, 0))
    shape = jax.ShapeDtypeStruct((rows, cols), F32)
    return _call(
        body, name=name, grid=(rows // tb,), out_shape=(shape,) * 4,
        in_specs=[spec] * 4, out_specs=(spec,) * 4,
        compiler_params=_params(32, ("arbitrary",)),
    )(w, g, m, v)


def _adamw_w_in(w_t, g_window, m_t, v_t, tb):
    def body(w_ref, g_ref, m_ref, v_ref, go_ref, d_ref, mo_ref, vo_ref, gt_ref):
        gt_ref[...] = g_ref[...].T
        gv = gt_ref[pl.ds(pl.multiple_of(lax.axis_index("y") * SHARD_SHIFT, SHARD_SHIFT), SHARD_IN), :]
        go_ref[...] = gv
        d_ref[...], mo_ref[...], vo_ref[...] = _adam_math(w_ref[...], gv, m_ref[...], v_ref[...])

    spec = pl.BlockSpec((SHARD_IN, tb), lambda i: (0, i))
    shape = jax.ShapeDtypeStruct((SHARD_IN, D_MODEL), F32)
    return _call(
        body, name="adamw_w_in", grid=(D_MODEL // tb,), out_shape=(shape,) * 4,
        in_specs=[spec, pl.BlockSpec((tb, SHARD_PAD), lambda i: (i, 0)), spec, spec], out_specs=(spec,) * 4,
        scratch_shapes=[pltpu.VMEM((SHARD_PAD, tb), F32)],
        compiler_params=_params(32, ("arbitrary",)),
    )(w_t, g_window, m_t, v_t)


def _adamw_small(sums, params):
    def body(sums_ref, *refs):
        ins, outs = refs[:3 * len(params)], refs[3 * len(params):]
        for k in range(len(params)):
            w_ref, m_ref, v_ref = ins[3 * k:3 * k + 3]
            g = sums_ref[k:k + 1, 0:w_ref.shape[1]]
            go_ref, d_ref, mo_ref, vo_ref = outs[4 * k:4 * k + 4]
            go_ref[...] = g
            d_ref[...], mo_ref[...], vo_ref[...] = _adam_math(w_ref[...], g, m_ref[...], v_ref[...])

    vmem = pl.BlockSpec(memory_space=pltpu.VMEM)
    flat = [a for p in params for a in p]
    shapes = tuple(jax.ShapeDtypeStruct(p[0].shape, F32) for p in params for _ in range(4))
    res = _call(body, name="adamw_small", out_shape=shapes, in_specs=[vmem] * (1 + len(flat)),
                out_specs=(vmem,) * len(shapes), compiler_params=_params(16))(sums, *flat)
    return [res[4 * k:4 * k + 4] for k in range(len(params))]


def kernel(x, norm_g, w_in, att_sinks, ret_gn_g, w_out, final_g, loss_target, m_norm_g, m_w_in, m_att_sinks, m_ret_gn_g, m_w_out, m_final_g, v_norm_g, v_w_in, v_att_sinks, v_ret_gn_g, v_w_out, v_final_g):
    seq = x.shape[1]
    xs, tgt = x[0], loss_target[0]
    final_g2 = final_g.reshape(1, D_MODEL)
    tables = _tables(seq)

    w_in_t, m_w_in_t, v_w_in_t = w_in[0].T, m_w_in[0].T, v_w_in[0].T
    proj, h_t, w_in_full, w_out_full = _in_proj(xs, norm_g, w_in_t, w_out[0], min(TOKENS_PROJ, seq))
    mix, dxo, states, loss_part, gfin, o_att, probs, p_sinks = _mix_fwd(
        proj, xs, tgt, w_out_full, final_g2, ret_gn_g, att_sinks, tables, min(TOKENS_MIX, seq))
    dproj, grad_x, gw_out, gnorm, dgain, dsink = _mix_bwd(
        proj, dxo, mix, o_att, probs, p_sinks, states, xs, w_out_full, w_in_full, norm_g, ret_gn_g, tables,
        min(TOKENS_MIX, seq))
    g_in, g_out, sums = _gw_in_reduce(h_t, dproj, gw_out, (gnorm, gfin, dgain, dsink, loss_part),
                                      min(TOKENS_GW, seq))

    res_in = [r.T for r in _adamw_w_in(w_in_t, g_in, m_w_in_t, v_w_in_t, 256)]
    res_out = _adamw("adamw_w_out", w_out[0], g_out, m_w_out[0], v_w_out[0], SHARD_OUT)
    as_row = lambda a: a.reshape(1, D_MODEL)
    r_norm, r_final, r_gain, r_sink = _adamw_small(sums, [
        (norm_g, m_norm_g, v_norm_g), (final_g2, as_row(m_final_g), as_row(v_final_g)),
        (ret_gn_g, m_ret_gn_g, v_ret_gn_g), (att_sinks, m_att_sinks, v_att_sinks)])

    outs = []
    for k in range(4):
        outs += [r_norm[k], res_in[k][None], r_sink[k], r_gain[k], res_out[k][None], r_final[k].reshape(D_MODEL)]
    return (sums[4, 0], grad_x[None], *outs)
```

```python
import jax
import jax.numpy as jnp
import numpy as np
from jax import lax
from jax.experimental import pallas as pl
from jax.experimental.pallas import tpu as pltpu

F32 = jnp.float32
BF16 = jnp.bfloat16

D_MODEL = 1024
ATT_HEADS = 8
ATT_KV_HEADS = 2
ATT_HEAD_DIM = 64
RET_HEADS = 4
RET_QK_DIM = 64
RET_V_DIM = 128
BLK = 128
ROT_BASE = 10000.0
RMS_EPS = 1e-6
GN_EPS = 1e-6
NEG_INF = -1e30
ATT_SCALE = ATT_HEAD_DIM ** -0.5
RET_SCALE = RET_QK_DIM ** -0.5

ATT_WIDTH = ATT_HEADS * ATT_HEAD_DIM
ATT_KV_WIDTH = ATT_KV_HEADS * ATT_HEAD_DIM
RET_QK_WIDTH = RET_HEADS * RET_QK_DIM
RET_WIDTH = RET_HEADS * RET_V_DIM
MIX_WIDTH = ATT_WIDTH + RET_WIDTH
OFF_AQ = 0
OFF_AK = OFF_AQ + ATT_WIDTH
OFF_AV = OFF_AK + ATT_KV_WIDTH
OFF_AZ = OFF_AV + ATT_KV_WIDTH
OFF_RQ = OFF_AZ + ATT_WIDTH
OFF_RK = OFF_RQ + RET_QK_WIDTH
OFF_RV = OFF_RK + RET_QK_WIDTH
OFF_RZ = OFF_RV + RET_WIDTH
IN_WIDTH = OFF_RZ + RET_WIDTH

LANE = 128
BF16_ROWS = 16
HALF_LANE = LANE // 2
PAIRS = RET_QK_WIDTH // LANE
assert ATT_HEAD_DIM == HALF_LANE and RET_QK_DIM == HALF_LANE and RET_V_DIM == LANE and ATT_KV_WIDTH == LANE

N_CHIPS = 4
N_DEV = 8
SHARD_IN = IN_WIDTH // N_CHIPS
SHARD_PAD = 768
SHARD_SHIFT = SHARD_PAD - SHARD_IN
WIN_START = tuple((j * SHARD_IN) // LANE * LANE for j in range(N_CHIPS))
SHARD_OUT = MIX_WIDTH // N_CHIPS

PACK_PARTS = 5
PACK_ROWS = 8
assert PACK_PARTS <= PACK_ROWS

ADAM_LR = 0.001
ADAM_B1 = 0.9
ADAM_B2 = 0.999
ADAM_EPS = 1e-08
ADAM_WD = 0.01
ADAM_STEP = 10

VMEM_CAP = 64 * 1024 * 1024
TOKENS_PROJ = 1024
TOKENS_MIX = 512
TOKENS_GW = 2048
MESH_ID = pl.DeviceIdType.MESH


def _call(body, **kw):
    return pl.pallas_call(body, **kw)


def _params(vmem_mb, semantics=None):
    assert vmem_mb * 1024 * 1024 < VMEM_CAP
    return pltpu.CompilerParams(dimension_semantics=semantics, vmem_limit_bytes=vmem_mb * 1024 * 1024)


def _dot(a, b):
    return jnp.dot(a, b, preferred_element_type=F32)


def _dot_nt(a, b):
    return lax.dot_general(a, b, (((1,), (1,)), ((), ())), preferred_element_type=F32)


def _dot_tn(a, b):
    return lax.dot_general(a, b, (((0,), (0,)), ((), ())), preferred_element_type=F32)


def _sigmoid(z):
    return 1.0 / (1.0 + jnp.exp(-z))


def _const_spec(shape):
    nd = len(shape)
    return pl.BlockSpec(shape, lambda i: (0,) * nd, pipeline_mode=pl.Buffered(1))


def _tables(seq):
    f32 = np.float32
    pos = np.arange(seq, dtype=f32)
    theta = (f32(1.0) / (f32(ROT_BASE) ** np.linspace(0.0, 1.0, RET_QK_DIM // 2, dtype=f32))).astype(f32)
    ang = (pos[:, None] * theta[None, :]).astype(f32)
    cos, sin = np.cos(ang), np.sin(ang)
    cos2 = np.repeat(cos, 2, axis=1)
    sin2 = np.stack([-sin, sin], axis=-1).reshape(seq, RET_QK_DIM)
    cos_t = np.tile(cos2, (1, 2))
    sin_t = np.tile(sin2, (1, 2))

    log_gamma = np.log(f32(1.0) - f32(2.0) ** (f32(-5.0) - np.arange(RET_HEADS, dtype=f32))).astype(f32)
    idx = np.arange(BLK, dtype=f32)
    rel = idx[:, None] - idx[None, :]
    decay_in = np.where(rel >= 0, np.exp(log_gamma[:, None, None] * np.maximum(rel, f32(0.0))), f32(0.0))
    k_dec = np.exp(log_gamma[:, None] * (BLK - 1 - idx)[None, :])
    q_dec = np.exp(log_gamma[:, None] * (idx + 1)[None, :])
    chunk_decay = np.exp(log_gamma * f32(BLK))
    kdec_t = np.repeat(k_dec.T, RET_QK_DIM, axis=1)
    qdec_t = np.repeat(q_dec.T, RET_QK_DIM, axis=1)
    cd_t = np.broadcast_to(chunk_decay[:, None, None], (RET_HEADS, RET_QK_DIM, RET_V_DIM))
    decay_in = decay_in.reshape(PAIRS, 2 * BLK, BLK)
    cd_t = cd_t.reshape(PAIRS, 2 * RET_QK_DIM, RET_V_DIM)

    key = np.arange(BLK)[:, None]
    query = np.arange(2 * BLK)[None, :] % BLK
    bias = np.stack([np.where(key > query, NEG_INF, 0.0), np.zeros((BLK, 2 * BLK))])
    return tuple(jnp.asarray(np.ascontiguousarray(a), F32) for a in (cos_t, sin_t, decay_in, qdec_t, kdec_t, cd_t, bias))


def _low_lanes(shape):
    lane = lax.broadcasted_iota(jnp.int32, shape, len(shape) - 1)
    return (lane & HALF_LANE) == 0


def _split_heads(t):
    low = _low_lanes(t.shape)
    zero = jnp.zeros_like(t)
    return jnp.where(low, t, zero), jnp.where(low, zero, t)


def _swap_pairs(t):
    lane = lax.broadcasted_iota(jnp.int32, t.shape, 1)
    nxt = pltpu.roll(t, t.shape[1] - 1, 1)
    prv = pltpu.roll(t, 1, 1)
    return jnp.where((lane & 1) == 0, nxt, prv)


def _per_tile(fn, t):
    return jnp.concatenate([fn(_tile(t, i)) for i in range(t.shape[1] // LANE)], axis=1)


def _rotate(t, cos_t, sin_t):
    return _per_tile(lambda a: a * cos_t + _swap_pairs(a) * sin_t, t)


def _rotate_transposed(d, cos_t, sin_t):
    return _per_tile(lambda a: a * cos_t + _swap_pairs(a * sin_t), d)


def _kv_operands(cat):
    low = _low_lanes(cat.shape)
    swapped = pltpu.roll(cat, HALF_LANE, 1)
    zero = jnp.zeros_like(cat)
    pick = lambda a, b: jnp.where(low, a, b).astype(BF16)
    return ((pick(cat, zero), pick(zero, swapped)), (pick(swapped, zero), pick(zero, cat)))


def _stack_tiles(t, first_tile):
    a = t[:, first_tile * LANE:(first_tile + 1) * LANE]
    b = t[:, (first_tile + 1) * LANE:(first_tile + 2) * LANE]
    return jnp.concatenate([a, b], axis=0)


def _sink_rows(sinks_ref, group):
    first = lax.broadcasted_iota(jnp.int32, (1, 2 * BLK), 1) < BLK

    def row(h0, h1):
        return jnp.where(first, sinks_ref[0, group * 4 + h0], sinks_ref[0, group * 4 + h1])
    return row(0, 2), row(1, 3)


ATT_PROBLEMS = tuple((g, hi) for g in range(ATT_KV_HEADS) for hi in range(2))


def _in_previous_block():
    key = lax.broadcasted_iota(jnp.int32, (BLK, 2 * BLK), 0)
    query = lax.broadcasted_iota(jnp.int32, (BLK, 2 * BLK), 1) & (BLK - 1)
    return key > query


def _fold(t, prev):
    return jnp.where(prev, t[0:BLK], t[BLK:])


def _unfold(t, prev):
    zero = jnp.zeros_like(t)
    return jnp.concatenate([jnp.where(prev, t, zero), jnp.where(prev, zero, t)], axis=0)


def _attn_scores(qs, k_ops, bias, prev):
    return [_fold(_dot_nt(k_ops[g][hi], qs[g]), prev) + bias for g, hi in ATT_PROBLEMS]


def _attn_softmax(s, sinks_ref):
    sink = [_sink_rows(sinks_ref, g)[hi] for g, hi in ATT_PROBLEMS]
    m = [jnp.maximum(jnp.max(si, axis=0, keepdims=True), ki) for si, ki in zip(s, sink)]
    e = [jnp.exp(si - mi) for si, mi in zip(s, m)]
    es = [jnp.exp(ki - mi) for ki, mi in zip(sink, m)]
    inv = [1.0 / (jnp.sum(ei, axis=0, keepdims=True) + esi) for ei, esi in zip(e, es)]
    return [ei * ii for ei, ii in zip(e, inv)], [esi * ii for esi, ii in zip(es, inv)]


def _group_norm_all(outs):
    mu = [jnp.mean(o, axis=1, keepdims=True) for o in outs]
    xc = [o - m for o, m in zip(outs, mu)]
    var = [jnp.mean(c * c, axis=1, keepdims=True) for c in xc]
    rstd = [lax.rsqrt(v + GN_EPS) for v in var]
    return [c * r for c, r in zip(xc, rstd)], rstd


def _proj_cols(p_ref, rows, start, stop):
    pieces = []
    while start < stop:
        k = max(j for j in range(N_CHIPS) if WIN_START[j] <= start)
        end = min(stop, WIN_START[k] + SHARD_PAD)
        pieces.append(p_ref[k, rows, start - WIN_START[k]:end - WIN_START[k]])
        start = end
    return pieces[0] if len(pieces) == 1 else jnp.concatenate(pieces, axis=1)


def _retention_operands(p_ref, rows, cos_b, sin_b, qdec, kdec):
    qr = _rotate(_proj_cols(p_ref, rows, OFF_RQ, OFF_RQ + RET_QK_WIDTH).astype(F32), cos_b, sin_b)
    kr = _rotate(_proj_cols(p_ref, rows, OFF_RK, OFF_RK + RET_QK_WIDTH).astype(F32), cos_b, sin_b) * RET_SCALE
    return qr, kr, qr * qdec, kr * kdec


def _tile(t, i):
    return t[:, i * LANE:(i + 1) * LANE]


GATHER_SEMS = 7


def _allgather_steps(src_ref, full_ref, blk_ref, send_sems, recv_sems):
    block_rows = blk_ref.shape[0]
    assert block_rows % BF16_ROWS == 0 and full_ref.shape[0] == N_DEV * block_rows
    x, y, c = lax.axis_index("x"), lax.axis_index("y"), lax.axis_index("c")
    me, sibling = (x, y, c), (x, y, 1 - c)
    chips = [(1 - x, y), (x, 1 - y), (1 - x, 1 - y)]

    def rows(px, py, pc):
        return full_ref.at[pl.ds(pl.multiple_of((4 * px + 2 * py + pc) * block_rows, BF16_ROWS), block_rows), :]

    def copy(k, block, to, src=None):
        return pltpu.make_async_remote_copy(
            src_ref=rows(*block) if src is None else src, dst_ref=rows(*block),
            send_sem=send_sems.at[k], recv_sem=recv_sems.at[k], device_id=to, device_id_type=MESH_ID)

    first = [copy(0, me, sibling, src=blk_ref)] + [copy(1 + j, me, (*chip, c), src=blk_ref) for j, chip in enumerate(chips)]
    passed = [copy(4 + j, (*chip, c), sibling) for j, chip in enumerate(chips)]

    def start():
        blk_ref[...] = src_ref[pl.ds(pl.multiple_of(c * block_rows, BF16_ROWS), block_rows), :].astype(BF16)
        rows(*me)[...] = blk_ref[...]
        for cp in first:
            cp.start()

    def forward():
        for j, chip in enumerate(chips):
            copy(1 + j, (*chip, c), me).wait_recv()
            passed[j].start()

    def finish():
        copy(0, sibling, me).wait_recv()
        for j, chip in enumerate(chips):
            copy(4 + j, (*chip, 1 - c), me).wait_recv()
        for cp in first + passed:
            cp.wait_send()

    return start, forward, finish


WINDOW_HALF = SHARD_PAD // 2
WINDOW_SEMS = GATHER_SEMS + 1


def _window_gather(wi_ref, w4_ref, blk_ref, edge_ref, send_sems, recv_sems):
    x, y, c = lax.axis_index("x"), lax.axis_index("y"), lax.axis_index("c")
    me, sibling, pair = (x, y, c), (x, y, 1 - c), (x, 1 - y, c)
    chips = [(1 - x, y), (x, 1 - y), (1 - x, 1 - y)]
    own_rows, edge_rows = SHARD_IN - WINDOW_HALF, SHARD_SHIFT
    sends_edge = y == c

    def rows(px, py, pc):
        return w4_ref.at[pl.ds(pl.multiple_of((4 * px + 2 * py + pc) * WINDOW_HALF, BF16_ROWS), WINDOW_HALF), :]

    def copy(k, block, to, src=None):
        return pltpu.make_async_remote_copy(
            src_ref=rows(*block) if src is None else src, dst_ref=rows(*block),
            send_sem=send_sems.at[k], recv_sem=recv_sems.at[k], device_id=to, device_id_type=MESH_ID)

    edge_copy = pltpu.make_async_remote_copy(
        src_ref=edge_ref, dst_ref=edge_ref, send_sem=send_sems.at[GATHER_SEMS], recv_sem=recv_sems.at[GATHER_SEMS],
        device_id=pair, device_id_type=MESH_ID)
    first = [copy(0, me, sibling, src=blk_ref)] + [copy(1 + j, me, (*chip, c), src=blk_ref) for j, chip in enumerate(chips)]
    passed = [copy(4 + j, (*chip, c), sibling) for j, chip in enumerate(chips)]

    def at(start, size):
        return pl.ds(pl.multiple_of(start, BF16_ROWS), size)

    def start():
        @pl.when(sends_edge)
        def _():
            edge_ref[...] = wi_ref[at((1 - y) * (SHARD_IN - edge_rows), edge_rows), :].astype(BF16)
            edge_copy.start()
            blk_ref[...] = wi_ref[at(c * own_rows, WINDOW_HALF), :].astype(BF16)

        @pl.when(jnp.logical_not(sends_edge))
        def _():
            blk_ref[at(y * edge_rows, own_rows), :] = wi_ref[at(c * WINDOW_HALF, own_rows), :].astype(BF16)
            edge_copy.wait_recv()
            blk_ref[at((1 - y) * own_rows, edge_rows), :] = edge_ref[...]

        rows(*me)[...] = blk_ref[...]
        for cp in first[:-1]:
            cp.start()
        copy(0, sibling, me).wait_recv()

    def send_far():
        first[-1].start()

    def forward(j):
        copy(1 + j, (*chips[j], c), me).wait_recv()
        passed[j].start()

    def wait_forwarded(j):
        copy(4 + j, (*chips[j], 1 - c), me).wait_recv()

    def finish():
        for cp in first + passed:
            cp.wait_send()

        @pl.when(sends_edge)
        def _():
            edge_copy.wait_send()

    return start, send_far, forward, wait_forwarded, finish


def _in_proj(x, norm_g, w_in_t_shard, w_out_shard, tb):
    seq = x.shape[0]
    nblk = seq // tb
    last = nblk - 1
    chip_of_panel = (None, 1, 0, 2)

    def body(win_ref, x_ref, g_ref, wi_ref, wo_ref, p_ref, ht_ref, wt_ref, wout_ref,
             w4, blk, edge, hbuf, wout_full, blko, send_sems, recv_sems, send_sems_o, recv_sems_o):
        q, i = pl.program_id(0), pl.program_id(1)
        chip = 2 * lax.axis_index("x") + lax.axis_index("y")
        in_start, in_send_far, in_forward, in_wait_forwarded, in_finish = _window_gather(
            wi_ref, w4, blk, edge, send_sems, recv_sems)
        out_start, out_forward, out_finish = _allgather_steps(wo_ref, wout_full, blko, send_sems_o, recv_sems_o)
        rows = pl.ds(pl.multiple_of(i * tb, tb), tb)

        @pl.when(jnp.logical_and(q == 0, i == 0))
        def _():
            in_start()

        @pl.when(jnp.logical_and(q == 0, i == min(last // 2 + 1, last)))
        def _():
            in_send_far()

        @pl.when(jnp.logical_and(q == 1, i == 0))
        def _():
            out_start()

        for panel in range(1, N_CHIPS):
            @pl.when(jnp.logical_and(q == panel, i == 0))
            def _():
                in_wait_forwarded(chip_of_panel[panel])

        @pl.when(q == 0)
        def _():
            xv = x_ref[...]
            r = lax.rsqrt(jnp.mean(xv * xv, axis=1, keepdims=True) + RMS_EPS)
            h = (xv * r) * g_ref[...]
            hbuf[rows, :] = h.astype(BF16)
            ht_ref[...] = h.T.astype(BF16)

        owner = jnp.bitwise_xor(chip, q)
        window = w4[pl.ds(pl.multiple_of(owner * SHARD_PAD, SHARD_PAD), SHARD_PAD), :]
        p_ref[...] = _dot_nt(hbuf[rows, :], window).astype(BF16)

        for panel in range(1, N_CHIPS):
            @pl.when(jnp.logical_and(q == panel - 1, i == last))
            def _():
                in_forward(chip_of_panel[panel])

        @pl.when(jnp.logical_and(q == N_CHIPS - 1, i == 0))
        def _():
            out_forward()

        @pl.when(jnp.logical_and(q == N_CHIPS - 1, i == last))
        def _():
            in_finish()
            for k in range(N_CHIPS):
                first = k * SHARD_PAD + (k % 2) * SHARD_SHIFT
                wt_ref[k * SHARD_IN:(k + 1) * SHARD_IN, :] = w4[first:first + SHARD_IN, :]
            out_finish()
            wout_ref[...] = wout_full[...]

    whole = lambda shape: pl.BlockSpec(shape, lambda q, i, win: (0,) * len(shape), pipeline_mode=pl.Buffered(1))
    first_panel = lambda q, i: jnp.where(q == 0, i, last)
    grid_spec = pltpu.PrefetchScalarGridSpec(
        num_scalar_prefetch=1, grid=(N_CHIPS, nblk),
        in_specs=[pl.BlockSpec((tb, D_MODEL), lambda q, i, win: (first_panel(q, i), 0)), whole((1, D_MODEL)),
                  whole((SHARD_IN, D_MODEL)), whole((SHARD_OUT, D_MODEL))],
        out_specs=(pl.BlockSpec((None, tb, SHARD_PAD), lambda q, i, win: (win[q], i, 0)),
                   pl.BlockSpec((D_MODEL, tb), lambda q, i, win: (0, first_panel(q, i))),
                   whole((IN_WIDTH, D_MODEL)), whole((MIX_WIDTH, D_MODEL))),
        scratch_shapes=[
            pltpu.VMEM((N_CHIPS * SHARD_PAD, D_MODEL), BF16), pltpu.VMEM((WINDOW_HALF, D_MODEL), BF16),
            pltpu.VMEM((SHARD_SHIFT, D_MODEL), BF16), pltpu.VMEM((seq, D_MODEL), BF16),
            pltpu.VMEM((MIX_WIDTH, D_MODEL), BF16), pltpu.VMEM((SHARD_OUT // 2, D_MODEL), BF16),
            pltpu.SemaphoreType.DMA((WINDOW_SEMS,)), pltpu.SemaphoreType.DMA((WINDOW_SEMS,)),
            pltpu.SemaphoreType.DMA((GATHER_SEMS,)), pltpu.SemaphoreType.DMA((GATHER_SEMS,)),
        ])
    chip = 2 * lax.axis_index("x") + lax.axis_index("y")
    owner = jnp.bitwise_xor(chip, jnp.arange(N_CHIPS, dtype=jnp.int32))
    return _call(
        body, name="in_proj", grid_spec=grid_spec,
        out_shape=(jax.ShapeDtypeStruct((N_CHIPS, seq, SHARD_PAD), BF16), jax.ShapeDtypeStruct((D_MODEL, seq), BF16),
                   jax.ShapeDtypeStruct((IN_WIDTH, D_MODEL), BF16), jax.ShapeDtypeStruct((MIX_WIDTH, D_MODEL), BF16)),
        compiler_params=_params(60, ("arbitrary", "arbitrary")),
    )(owner.astype(jnp.int32), x, norm_g, w_in_t_shard, w_out_shard)


def _mix_fwd(proj, x, target, w_out, final_g, gn_gain, sinks, tables, tb):
    seq = x.shape[0]
    nsub = tb // BLK
    cos_t, sin_t, decay_in, qdec_t, kdec_t, cd_t, bias_t = tables

    def body(p_ref, x_ref, t_ref, cos_ref, sin_ref, wout_ref, fg_ref, gain_ref, sinks_ref, din_ref, qdec_ref,
             kdec_ref, cd_ref, bias_ref, mix_ref, dxo_ref, st_ref, loss_ref, gfin_ref, oatt_ref, probs_ref, psink_ref,
             kprev_ref, vprev_ref, state_ref):
        i = pl.program_id(0)

        @pl.when(i == 0)
        def _():
            kprev_ref[...] = jnp.zeros_like(kprev_ref)
            vprev_ref[...] = jnp.zeros_like(vprev_ref)
            state_ref[...] = jnp.zeros_like(state_ref)
            loss_ref[...] = jnp.zeros_like(loss_ref)
            gfin_ref[...] = jnp.zeros_like(gfin_ref)

        prev = _in_previous_block()

        def sub(j, carry):
            kp, vp, states = carry
            rows = pl.ds(pl.multiple_of(j * BLK, BLK), BLK)
            bias = bias_ref[jnp.where(jnp.logical_or(i > 0, j > 0), 1, 0)]

            aq = _proj_cols(p_ref, rows, OFF_AQ, OFF_AQ + ATT_WIDTH)
            ak = _proj_cols(p_ref, rows, OFF_AK, OFF_AK + ATT_KV_WIDTH).astype(F32)
            av = _proj_cols(p_ref, rows, OFF_AV, OFF_AV + ATT_KV_WIDTH).astype(F32)
            az = _proj_cols(p_ref, rows, OFF_AZ, OFF_AZ + ATT_WIDTH).astype(F32)
            k_ops = _kv_operands(jnp.concatenate([kp, ak], axis=0))
            v_ops = _kv_operands(jnp.concatenate([vp, av], axis=0))
            qs = [(_stack_tiles(aq, 2 * g) * ATT_SCALE).astype(BF16) for g in range(ATT_KV_HEADS)]

            qr, kr, qd, kd = _retention_operands(p_ref, rows, cos_ref[rows, :], sin_ref[rows, :],
                                                 qdec_ref[...], kdec_ref[...])
            s = _attn_scores(qs, k_ops, bias, prev)
            heads = [(t, hh) for t in range(PAIRS) for hh in range(2)]
            sc = [_dot_nt(jnp.concatenate(_split_heads(_tile(qr, t)), axis=0).astype(BF16), _tile(kr, t).astype(BF16))
                  * din_ref[t] for t in range(PAIRS)]
            qd_heads = [_split_heads(_tile(qd, t)) for t in range(PAIRS)]
            state_b = [states[t].astype(BF16) for t in range(PAIRS)]
            vs = [_proj_cols(p_ref, rows, OFF_RV + h * RET_V_DIM, OFF_RV + (h + 1) * RET_V_DIM) for h in range(RET_HEADS)]
            rzs = [_proj_cols(p_ref, rows, OFF_RZ + h * RET_V_DIM, OFF_RZ + (h + 1) * RET_V_DIM).astype(F32)
                   for h in range(RET_HEADS)]
            lhs = [jnp.concatenate([sc[t][hh * BLK:(hh + 1) * BLK].astype(BF16), qd_heads[t][hh].astype(BF16)], axis=1)
                   for t, hh in heads]
            p, p_sink = _attn_softmax(s, sinks_ref)
            o_ret = [_dot(lhs[2 * t + hh], jnp.concatenate([vs[2 * t + hh], state_b[t]], axis=0)) for t, hh in heads]
            p = [pi.astype(BF16) for pi in p]
            o_tiles = []
            for g in range(ATT_KV_HEADS):
                p_cat = jnp.concatenate([_unfold(p[2 * g], prev), _unfold(p[2 * g + 1], prev)], axis=0)
                o = _dot_tn(p_cat, jnp.concatenate(v_ops[g], axis=0))
                o_tiles += [o[0:BLK], o[BLK:]]
            ons, _ = _group_norm_all(o_ret)
            new_states = [states[t] * cd_ref[t]
                          + _dot_tn(jnp.concatenate(_split_heads(_tile(kd, t)), axis=0).astype(BF16),
                                    jnp.concatenate([vs[2 * t], vs[2 * t + 1]], axis=0)) for t in range(PAIRS)]
            o_att = jnp.concatenate(o_tiles, axis=1)
            out = [o_att * (az * _sigmoid(az))]
            out += [(ons[h] * gain_ref[:, h * RET_V_DIM:(h + 1) * RET_V_DIM]) * (rzs[h] * _sigmoid(rzs[h]))
                    for h in range(RET_HEADS)]
            mix_ref[rows, :] = jnp.concatenate(out, axis=1).astype(BF16)
            oatt_ref[rows, :] = o_att
            probs_ref[j] = jnp.stack(p)
            psink_ref[j] = jnp.concatenate(p_sink, axis=0)
            st_ref[j] = jnp.stack(states)
            return ak, av, tuple(new_states)

        carry = (kprev_ref[...], vprev_ref[...], tuple(state_ref[t] for t in range(PAIRS)))
        for jb in range(nsub):
            carry = sub(jnp.int32(jb), carry)
        kp, vp, states = carry
        kprev_ref[...] = kp
        vprev_ref[...] = vp
        state_ref[...] = jnp.stack(states)

        xo = x_ref[...] + _dot(mix_ref[...], wout_ref[...])
        r2 = lax.rsqrt(jnp.mean(xo * xo, axis=1, keepdims=True) + RMS_EPS)
        xn = xo * r2
        err = xn * fg_ref[...] - t_ref[...]
        loss_ref[...] += jnp.sum(err * err) * (0.5 / D_MODEL)
        dy = err * (1.0 / D_MODEL)
        gfin_ref[...] += jnp.sum(dy * xn, axis=0, keepdims=True)
        u = dy * fg_ref[...]
        dxo_ref[...] = r2 * u - xn * (r2 * jnp.mean(u * xn, axis=1, keepdims=True))

    blk_rows = lambda w: pl.BlockSpec((tb, w), lambda i: (i, 0))
    state_shape = (PAIRS, 2 * RET_QK_DIM, RET_V_DIM)
    return _call(
        body, name="mix_fwd", grid=(seq // tb,),
        out_shape=(
            jax.ShapeDtypeStruct((seq, MIX_WIDTH), BF16),
            jax.ShapeDtypeStruct((seq, D_MODEL), F32),
            jax.ShapeDtypeStruct((seq // BLK,) + state_shape, F32),
            jax.ShapeDtypeStruct((8, LANE), F32),
            jax.ShapeDtypeStruct((1, D_MODEL), F32),
            jax.ShapeDtypeStruct((seq, ATT_WIDTH), F32),
            jax.ShapeDtypeStruct((seq // BLK, len(ATT_PROBLEMS), BLK, 2 * BLK), BF16),
            jax.ShapeDtypeStruct((seq // BLK, len(ATT_PROBLEMS), 2 * BLK), F32),
        ),
        in_specs=[
            pl.BlockSpec((N_CHIPS, tb, SHARD_PAD), lambda i: (0, i, 0)),
            blk_rows(D_MODEL), blk_rows(D_MODEL), blk_rows(LANE), blk_rows(LANE),
            _const_spec((MIX_WIDTH, D_MODEL)), _const_spec((1, D_MODEL)), _const_spec((1, RET_WIDTH)),
            pl.BlockSpec(memory_space=pltpu.SMEM),
            _const_spec((PAIRS, 2 * BLK, BLK)), _const_spec((BLK, RET_QK_WIDTH)), _const_spec((BLK, RET_QK_WIDTH)),
            _const_spec(state_shape), _const_spec((2, BLK, 2 * BLK)),
        ],
        out_specs=(
            blk_rows(MIX_WIDTH), blk_rows(D_MODEL),
            pl.BlockSpec((nsub,) + state_shape, lambda i: (i, 0, 0, 0)),
            _const_spec((8, LANE)), _const_spec((1, D_MODEL)), blk_rows(ATT_WIDTH),
            pl.BlockSpec((nsub, len(ATT_PROBLEMS), BLK, 2 * BLK), lambda i: (i, 0, 0, 0)),
            pl.BlockSpec((nsub, len(ATT_PROBLEMS), 2 * BLK), lambda i: (i, 0, 0)),
        ),
        scratch_shapes=[
            pltpu.VMEM((BLK, ATT_KV_WIDTH), F32), pltpu.VMEM((BLK, ATT_KV_WIDTH), F32),
            pltpu.VMEM(state_shape, F32),
        ],
        compiler_params=_params(48, ("arbitrary",)),
    )(proj, x, target, cos_t, sin_t, w_out, final_g, gn_gain, sinks, decay_in, qdec_t, kdec_t, cd_t, bias_t)


def _mix_bwd(proj, dxo, mix, o_att, probs, p_sinks, states, x, w_out, w_in_t, norm_g, gn_gain, tables, tb):
    seq = dxo.shape[0]
    nsub = tb // BLK
    nblk = seq // tb
    cos_t, sin_t, decay_in, qdec_t, kdec_t, cd_t, _ = tables
    kv_cols = OFF_AK // (2 * ATT_KV_WIDTH)
    state_shape = (PAIRS, 2 * RET_QK_DIM, RET_V_DIM)

    def body(p_ref, pkv_ref, dxo_ref, mix_ref, oatt_ref, probs_ref, psink_ref, st_ref, cos_ref, sin_ref, x_ref, wout_ref,
             win_ref, g_ref, gain_ref, din_ref, qdec_ref, kdec_ref, cd_ref,
             dp_ref, gx_ref, gwout_ref, gnorm_ref, dgain_ref, dsink_ref,
             dmix_ref, kv_ref, dkc_ref, dvc_ref, gst_ref):
        i = pl.program_id(0)

        @pl.when(i == 0)
        def _():
            gwout_ref[...] = jnp.zeros_like(gwout_ref)
            gnorm_ref[...] = jnp.zeros_like(gnorm_ref)
            dgain_ref[...] = jnp.zeros_like(dgain_ref)
            dsink_ref[...] = jnp.zeros_like(dsink_ref)
            dkc_ref[...] = jnp.zeros_like(dkc_ref)
            dvc_ref[...] = jnp.zeros_like(dvc_ref)
            gst_ref[...] = jnp.zeros_like(gst_ref)

        dxo_b = dxo_ref[...].astype(BF16)
        dmix_ref[...] = _dot_nt(dxo_b, wout_ref[...])
        gwout_ref[...] += _dot_tn(mix_ref[...], dxo_b)
        kv_ref[0:BLK, :] = pkv_ref[...].astype(F32)
        kv_ref[BLK:, :] = _proj_cols(p_ref, slice(None), OFF_AK, OFF_AK + 2 * ATT_KV_WIDTH).astype(F32)
        low = _low_lanes((BLK, LANE))
        low2 = _low_lanes((2 * BLK, LANE))
        lane = lax.broadcasted_iota(jnp.int32, (1, LANE), 1)
        prev = _in_previous_block()

        def sub(jj, carry):
            dkc, dvc, gsts, dgain, dsink = carry
            j = nsub - 1 - jj
            rows = pl.ds(pl.multiple_of(j * BLK, BLK), BLK)
            both = pl.ds(pl.multiple_of(j * BLK, BLK), 2 * BLK)


            aq = _proj_cols(p_ref, rows, OFF_AQ, OFF_AQ + ATT_WIDTH)
            az = _proj_cols(p_ref, rows, OFF_AZ, OFF_AZ + ATT_WIDTH).astype(F32)
            k_ops = _kv_operands(kv_ref[both, 0:ATT_KV_WIDTH])
            v_ops = _kv_operands(kv_ref[both, ATT_KV_WIDTH:2 * ATT_KV_WIDTH])
            da = dmix_ref[rows, 0:ATT_WIDTH]
            sig = _sigmoid(az)
            d_o = da * (az * sig)
            qs = [(_stack_tiles(aq, 2 * g) * ATT_SCALE).astype(BF16) for g in range(ATT_KV_HEADS)]
            dos = [_stack_tiles(d_o, 2 * g).astype(BF16) for g in range(ATT_KV_HEADS)]
            p_b = [probs_ref[j, k] for k in range(len(ATT_PROBLEMS))]
            p = [pk.astype(F32) for pk in p_b]
            sink_all = psink_ref[j]
            p_sink = [sink_all[k:k + 1, :] for k in range(len(ATT_PROBLEMS))]
            cos_b, sin_b = cos_ref[rows, :], sin_ref[rows, :]
            qdec, kdec = qdec_ref[...], kdec_ref[...]
            qr, kr, qd, kd = _retention_operands(p_ref, rows, cos_b, sin_b, qdec, kdec)
            heads = [(t, hh) for t in range(PAIRS) for hh in range(2)]
            head_cols = [slice(h * RET_V_DIM, (h + 1) * RET_V_DIM) for h in range(RET_HEADS)]
            q_rows = [jnp.concatenate(_split_heads(_tile(qr, t)), axis=0).astype(BF16) for t in range(PAIRS)]
            k_rows = [jnp.concatenate(_split_heads(_tile(kr, t)), axis=0).astype(BF16) for t in range(PAIRS)]
            din = [din_ref[t] for t in range(PAIRS)]

            dpr = [_fold(_dot_nt(v_ops[g][hi], dos[g]), prev) for g, hi in ATT_PROBLEMS]
            sc = [(_dot_nt(q_rows[t], _tile(kr, t).astype(BF16)) * din[t]).astype(BF16) for t in range(PAIRS)]
            qd_heads = [_split_heads(_tile(qd, t)) for t in range(PAIRS)]
            kd_heads = [_split_heads(_tile(kd, t)) for t in range(PAIRS)]
            state_b = [st_ref[j, t].astype(BF16) for t in range(PAIRS)]
            gst_b = [gsts[t].astype(BF16) for t in range(PAIRS)]
            vs = [_proj_cols(p_ref, rows, OFF_RV + h * RET_V_DIM, OFF_RV + (h + 1) * RET_V_DIM) for h in range(RET_HEADS)]
            rzs = [_proj_cols(p_ref, rows, OFF_RZ + h * RET_V_DIM, OFF_RZ + (h + 1) * RET_V_DIM).astype(F32)
                   for h in range(RET_HEADS)]
            drs = [dmix_ref[rows, ATT_WIDTH + h * RET_V_DIM:ATT_WIDTH + (h + 1) * RET_V_DIM] for h in range(RET_HEADS)]
            gains = [gain_ref[:, c] for c in head_cols]
            lhs = [jnp.concatenate([sc[t][hh * BLK:(hh + 1) * BLK], qd_heads[t][hh].astype(BF16)], axis=1) for t, hh in heads]
            rhs = [jnp.concatenate([vs[2 * t + hh], state_b[t]], axis=0) for t, hh in heads]

            delta = [jnp.sum(pi * di, axis=0, keepdims=True) for pi, di in zip(p, dpr)]
            ds = [_unfold((pi * (di - ti)).astype(BF16), prev) for pi, di, ti in zip(p, dpr, delta)]
            o_ret = [_dot(l, r) for l, r in zip(lhs, rhs)]
            for (g, hi), ki, ti in zip(ATT_PROBLEMS, p_sink, delta):
                sink_part = ki * ti
                for half in range(2):
                    tot = jnp.sum(sink_part[:, half * BLK:(half + 1) * BLK], axis=1, keepdims=True)
                    dsink = dsink - jnp.where(lane == 4 * g + 2 * half + hi, tot, 0.0)

            dq_tiles, dk_sums, dv_sums = [], [], []
            for g in range(ATT_KV_HEADS):
                ds_cat = jnp.concatenate([ds[2 * g], ds[2 * g + 1]], axis=0)
                p_cat = jnp.concatenate([_unfold(p_b[2 * g], prev), _unfold(p_b[2 * g + 1], prev)], axis=0)
                dqs = _dot_tn(ds_cat, jnp.concatenate(k_ops[g], axis=0)) * ATT_SCALE
                dq_tiles += [dqs[0:BLK], dqs[BLK:]]
                dk_sums.append(_dot(ds_cat, qs[g]))
                dv_sums.append(_dot(p_cat, dos[g]))
            ons, rstds = _group_norm_all(o_ret)
            sig_r = [_sigmoid(z) for z in rzs]
            dgn = [d * (z * g) for d, z, g in zip(drs, rzs, sig_r)]
            dz_parts = [d * (o * gn) * (g * (1.0 + z * (1.0 - g))) for d, o, gn, g, z in zip(drs, ons, gains, sig_r, rzs)]
            dgain_parts = [jnp.sum(d * o, axis=0, keepdims=True) for d, o in zip(dgn, ons)]
            don = [d * gn for d, gn in zip(dgn, gains)]
            mean_don = [jnp.mean(d, axis=1, keepdims=True) for d in don]
            mean_don_on = [jnp.mean(d * o, axis=1, keepdims=True) for d, o in zip(don, ons)]
            dob = [(r * (d - a - o * b)).astype(BF16) for r, d, a, o, b in zip(rstds, don, mean_don, ons, mean_don_on)]

            dlhs = [_dot_nt(d, r) for d, r in zip(dob, rhs)]
            drhs = [_dot_tn(l, d) for l, d in zip(lhs, dob)]
            dkds = [_dot_nt(vs[2 * t + hh], gst_b[t]) for t, hh in heads]
            dv_parts = [drhs[2 * t + hh][0:BLK] + _dot(kd_heads[t][hh].astype(BF16), gst_b[t]) for t, hh in heads]
            daz = da * oatt_ref[rows, :] * (sig * (1.0 + az * (1.0 - sig)))

            def kv_grad(sums):
                (a0, b0), (a1, b1) = [(s[0:2 * BLK], s[2 * BLK:]) for s in sums]
                return jnp.where(low2, a0, b1) + pltpu.roll(jnp.where(low2, a1, b0), HALF_LANE, 1)

            dk_both, dv_both = kv_grad(dk_sums), kv_grad(dv_sums)
            dak = dk_both[BLK:] + dkc
            dav = dv_both[BLK:] + dvc
            das = [(dlhs[2 * t + hh][:, 0:BLK] * din[t][hh * BLK:(hh + 1) * BLK]).astype(BF16) for t, hh in heads]
            new_gsts = [gsts[t] * cd_ref[t] + drhs[2 * t][BLK:] + drhs[2 * t + 1][BLK:] for t in range(PAIRS)]
            dq_parts = [_dot(jnp.concatenate([das[2 * t], das[2 * t + 1]], axis=1), k_rows[t])
                        + jnp.where(low, dlhs[2 * t][:, BLK:], dlhs[2 * t + 1][:, BLK:]) * _tile(qdec, t)
                        for t in range(PAIRS)]
            dk_parts = [_dot_tn(jnp.concatenate([das[2 * t], das[2 * t + 1]], axis=0), q_rows[t])
                        + jnp.where(low, dkds[2 * t], dkds[2 * t + 1]) * _tile(kdec, t) for t in range(PAIRS)]
            drq = _rotate_transposed(jnp.concatenate(dq_parts, axis=1), cos_b, sin_b)
            drk = _rotate_transposed(jnp.concatenate(dk_parts, axis=1) * RET_SCALE, cos_b, sin_b)

            dp_ref[rows, :] = jnp.concatenate(
                [jnp.concatenate(dq_tiles, axis=1), dak, dav, daz, drq, drk] + dv_parts + dz_parts, axis=1).astype(BF16)
            dgain = dgain + jnp.concatenate(dgain_parts, axis=1)
            return dk_both[0:BLK], dv_both[0:BLK], tuple(new_gsts), dgain, dsink

        carry = (dkc_ref[...], dvc_ref[...], tuple(gst_ref[t] for t in range(PAIRS)), dgain_ref[...], dsink_ref[...])
        for jb in range(nsub):
            carry = sub(jnp.int32(jb), carry)
        dkc, dvc, gsts, dgain, dsink = carry
        dkc_ref[...] = dkc
        dvc_ref[...] = dvc
        gst_ref[...] = jnp.stack(gsts)
        dgain_ref[...] = dgain
        dsink_ref[...] = dsink

        dh = _dot(dp_ref[...], win_ref[...])
        xv = x_ref[...]
        r = lax.rsqrt(jnp.mean(xv * xv, axis=1, keepdims=True) + RMS_EPS)
        xn = xv * r
        gnorm_ref[...] += jnp.sum(dh * xn, axis=0, keepdims=True)
        u = dh * g_ref[...]
        gx_ref[...] = dxo_ref[...] + r * u - xn * (r * jnp.mean(u * xn, axis=1, keepdims=True))

    rev_rows = lambda w: pl.BlockSpec((tb, w), lambda i: (nblk - 1 - i, 0))
    prev_kv = pl.BlockSpec((None, BLK, 2 * ATT_KV_WIDTH),
                           lambda i: (0, jnp.maximum((nblk - 1 - i) * nsub - 1, 0), kv_cols))
    return _call(
        body, name="mix_bwd", grid=(nblk,),
        out_shape=(
            jax.ShapeDtypeStruct((seq, IN_WIDTH), BF16),
            jax.ShapeDtypeStruct((seq, D_MODEL), F32),
            jax.ShapeDtypeStruct((MIX_WIDTH, D_MODEL), F32),
            jax.ShapeDtypeStruct((1, D_MODEL), F32),
            jax.ShapeDtypeStruct((1, RET_WIDTH), F32),
            jax.ShapeDtypeStruct((1, LANE), F32),
        ),
        in_specs=[
            pl.BlockSpec((N_CHIPS, tb, SHARD_PAD), lambda i: (0, nblk - 1 - i, 0)),
            prev_kv, rev_rows(D_MODEL), rev_rows(MIX_WIDTH), rev_rows(ATT_WIDTH),
            pl.BlockSpec((nsub, len(ATT_PROBLEMS), BLK, 2 * BLK), lambda i: (nblk - 1 - i, 0, 0, 0)),
            pl.BlockSpec((nsub, len(ATT_PROBLEMS), 2 * BLK), lambda i: (nblk - 1 - i, 0, 0)),
            pl.BlockSpec((nsub,) + state_shape, lambda i: (nblk - 1 - i, 0, 0, 0)),
            rev_rows(LANE), rev_rows(LANE), rev_rows(D_MODEL),
            _const_spec((MIX_WIDTH, D_MODEL)), _const_spec((IN_WIDTH, D_MODEL)), _const_spec((1, D_MODEL)),
            _const_spec((1, RET_WIDTH)),
            _const_spec((PAIRS, 2 * BLK, BLK)), _const_spec((BLK, RET_QK_WIDTH)), _const_spec((BLK, RET_QK_WIDTH)),
            _const_spec(state_shape),
        ],
        out_specs=(
            rev_rows(IN_WIDTH), rev_rows(D_MODEL), _const_spec((MIX_WIDTH, D_MODEL)), _const_spec((1, D_MODEL)),
            _const_spec((1, RET_WIDTH)), _const_spec((1, LANE)),
        ),
        scratch_shapes=[
            pltpu.VMEM((tb, MIX_WIDTH), F32),
            pltpu.VMEM((tb + BLK, 2 * ATT_KV_WIDTH), F32),
            pltpu.VMEM((BLK, ATT_KV_WIDTH), F32), pltpu.VMEM((BLK, ATT_KV_WIDTH), F32),
            pltpu.VMEM(state_shape, F32),
        ],
        compiler_params=_params(60, ("arbitrary",)),
    )(proj, proj, dxo, mix, o_att, probs, p_sinks, states, cos_t, sin_t, x, w_out, w_in_t, norm_g, gn_gain, decay_in,
      qdec_t, kdec_t, cd_t)


WINDOW_XOR = (3, 2, 1, 0)


def _gw_in_reduce(h_t, dproj, gw_out, small, tb):
    seq = dproj.shape[0]
    nblk = seq // tb
    last = nblk - 1
    hand_on = min(2, last)
    half = D_MODEL // 2
    A, B, C, N_SEMS = 0, N_CHIPS, 2 * N_CHIPS, 2 * N_CHIPS + 1

    def body(win_ref, ht_ref, dp_ref, gwo_hbm, s0_ref, s1_ref, s2_ref, s3_ref, s4_ref, out_ref, fout_ref, packsum_ref,
             acc, sib, send_buf, b_in, fin, fout, mine_out, sib_out, send_out, b_out, pack_ref, packs,
             send_sems, recv_sems, local_sems):
        p, i = pl.program_id(0), pl.program_id(1)
        small_start, small_hand_on, small_finish = _small_exchange(
            gwo_hbm, (s0_ref, s1_ref, s2_ref, s3_ref, s4_ref), fout, mine_out, sib_out, send_out, b_out, pack_ref,
            packs, send_sems, recv_sems, local_sems, N_SEMS)

        @pl.when(jnp.logical_and(p == 0, i == 0))
        def _():
            small_start()

        @pl.when(jnp.logical_and(p == 0, i == hand_on))
        def _():
            small_hand_on()

        x, y, c = lax.axis_index("x"), lax.axis_index("y"), lax.axis_index("c")
        chip = 2 * x + y
        sibling = (x, y, 1 - c)
        mine = pl.ds(pl.multiple_of(c * half, half), half)
        other = pl.ds(pl.multiple_of((1 - c) * half, half), half)

        def remote(src, dst, send_k, recv_k, to):
            return pltpu.make_async_remote_copy(src_ref=src, dst_ref=dst, send_sem=send_sems.at[send_k],
                                                recv_sem=recv_sems.at[recv_k], device_id=to, device_id_type=MESH_ID)

        slot = p % 2

        @pl.when(i == 0)
        def _():
            acc[slot] = jnp.zeros(acc.shape[1:], F32)

        acc[slot] += _dot(ht_ref[...], dp_ref[...])

        first_axis = 2 - c
        for q in range(N_CHIPS):
            s = q % 2
            to_sibling = remote(acc.at[s, other, :], sib.at[s], A + q, A + q, sibling)

            @pl.when(jnp.logical_and(p == q, i == last))
            def _():
                to_sibling.start()

            if q < N_CHIPS - 1:
                hop = first_axis if q == 0 else WINDOW_XOR[q]
                into = 0 if q == 0 else WINDOW_XOR[q]
                dest = chip ^ hop

                @pl.when(jnp.logical_and(p == q + 1, i == hand_on))
                def _():
                    to_sibling.wait_recv()
                    if q == 0:
                        send_buf[q] = (acc[s, mine, :] + sib[s]).astype(BF16)
                    else:
                        @pl.when(hop != first_axis)
                        def _():
                            remote(b_in.at[0], b_in.at[0], B, B, sibling).wait_recv()
                            send_buf[q] = (acc[s, mine, :] + sib[s] + b_in[0].astype(F32)).astype(BF16)

                        @pl.when(hop == first_axis)
                        def _():
                            send_buf[q] = (acc[s, mine, :] + sib[s]).astype(BF16)

                    remote(send_buf.at[q], b_in.at[into], B + q, B + into, (dest // 2, dest % 2, c)).start()
                    to_sibling.wait_send()
            else:
                @pl.when(jnp.logical_and(p == q, i == last))
                def _():
                    to_sibling.wait_recv()
                    fin[mine, :] = acc[s, mine, :] + sib[s]
                    for j in (1, 2):
                        remote(b_in.at[j], b_in.at[j], B + j, B + j, sibling).wait_recv()
                        fin[mine, :] += b_in[j].astype(F32)
                    to_core = remote(fin.at[mine, :], fin.at[mine, :], C, C, sibling)
                    to_core.start()
                    remote(fin.at[other, :], fin.at[other, :], C, C, sibling).wait_recv()
                    out_ref[...] = fin[...]
                    to_core.wait_send()
                    to_sibling.wait_send()
                    for k in range(N_CHIPS - 1):
                        remote(send_buf.at[k], b_in.at[k], B + k, B + k, sibling).wait_send()
                    packsum_ref[...] = small_finish()
                    fout_ref[...] = fout[...]

    whole = lambda shape: pl.BlockSpec(shape, lambda p, i, win: (0,) * len(shape), pipeline_mode=pl.Buffered(1))
    grid_spec = pltpu.PrefetchScalarGridSpec(
        num_scalar_prefetch=1, grid=(N_CHIPS, nblk),
        in_specs=[pl.BlockSpec((D_MODEL, tb), lambda p, i, win: (0, i)),
                  pl.BlockSpec((pl.Element(tb), pl.Element(SHARD_PAD)),
                               lambda p, i, win: (i * tb, pl.multiple_of(win[p] * LANE, LANE))),
                  pl.BlockSpec(memory_space=pl.ANY)] + [whole(s.shape) for s in small],
        out_specs=(whole((D_MODEL, SHARD_PAD)), whole((SHARD_OUT, D_MODEL)), whole((PACK_ROWS, D_MODEL))),
        scratch_shapes=[
            pltpu.VMEM((2, D_MODEL, SHARD_PAD), F32), pltpu.VMEM((2, half, SHARD_PAD), F32),
            pltpu.VMEM((N_CHIPS - 1, half, SHARD_PAD), BF16), pltpu.VMEM((N_CHIPS, half, SHARD_PAD), BF16),
            pltpu.VMEM((D_MODEL, SHARD_PAD), F32),
        ] + _small_exchange_scratch() + [
            pltpu.SemaphoreType.DMA((N_SEMS + SMALL_SEMS,)), pltpu.SemaphoreType.DMA((N_SEMS + SMALL_SEMS,)),
            pltpu.SemaphoreType.DMA((N_CHIPS,)),
        ])
    chip = 2 * lax.axis_index("x") + lax.axis_index("y")
    owner = chip ^ jnp.array(WINDOW_XOR, dtype=jnp.int32)
    win_start = (owner * SHARD_IN) // LANE
    return _call(
        body, name="gw_in_reduce", grid_spec=grid_spec,
        out_shape=(jax.ShapeDtypeStruct((D_MODEL, SHARD_PAD), F32), jax.ShapeDtypeStruct((SHARD_OUT, D_MODEL), F32),
                   jax.ShapeDtypeStruct((PACK_ROWS, D_MODEL), F32)),
        compiler_params=_params(52, ("arbitrary", "arbitrary")),
    )(win_start.astype(jnp.int32), h_t, dproj, gw_out, *small)


SMALL_SEMS = 17


def _small_exchange(gwo_hbm, small_refs, fout_ref, mine_out, sib_out, send_out, b_out, pack_ref, packs,
                    send_sems, recv_sems, local_sems, base):
    half_out = SHARD_OUT // 2
    A_OUT, B_OUT, C_OUT, PACK = base, base + 4, base + 8, base + 9
    assert len(small_refs) == PACK_PARTS
    x, y, c = lax.axis_index("x"), lax.axis_index("y"), lax.axis_index("c")
    chip = 2 * x + y
    dev = 2 * chip + c
    sibling = (x, y, 1 - c)

    def remote(src, dst, send_k, recv_k, to):
        return pltpu.make_async_remote_copy(src_ref=src, dst_ref=dst, send_sem=send_sems.at[send_k],
                                            recv_sem=recv_sems.at[recv_k], device_id=to, device_id_type=MESH_ID)

    def out_rows(j, core):
        return pl.ds(pl.multiple_of(j * SHARD_OUT + core * half_out, half_out), half_out)

    my_out_rows = pl.ds(pl.multiple_of(c * half_out, half_out), half_out)
    local = [pltpu.make_async_copy(gwo_hbm.at[out_rows(j, c), :], mine_out.at[j], local_sems.at[j])
             for j in range(N_CHIPS)]
    stage_a = [remote(gwo_hbm.at[out_rows(j, 1 - c), :], sib_out.at[j], A_OUT + j, A_OUT + j, sibling)
               for j in range(N_CHIPS)]
    mine_half_out = fout_ref.at[my_out_rows, :]
    stage_c = [remote(mine_half_out, mine_half_out, C_OUT, C_OUT, sibling)]

    def start():
        pack_ref[...] = jnp.zeros_like(pack_ref)
        for k, s_ref in enumerate(small_refs):
            pack_ref[k:k + 1, 0:s_ref.shape[1]] = s_ref[0:1, :]
        packs[dev] = pack_ref[...]
        for d in range(N_DEV):
            to = (d // 4, (d // 2) % 2, d % 2)

            @pl.when(d != dev)
            def _():
                remote(pack_ref, packs.at[dev], PACK + d, PACK + dev, to).start()

        for cp in local + stage_a:
            cp.start()

    def hand_on():
        for cp in local:
            cp.wait()
        for cp in stage_a:
            cp.wait_recv()
        for j in range(N_CHIPS):
            mine_out[j] = mine_out[j] + sib_out[j]

        for j in range(N_CHIPS):
            to = (j // 2, j % 2, c)

            @pl.when(j != chip)
            def _():
                send_out[j] = mine_out[j].astype(BF16)
                remote(send_out.at[j], b_out.at[chip], B_OUT + j, B_OUT + chip, to).start()

            @pl.when(j == chip)
            def _():
                fout_ref[my_out_rows, :] = mine_out[j]

    def finish():
        for j in range(N_CHIPS):
            @pl.when(j != chip)
            def _():
                remote(b_out.at[j], b_out.at[j], B_OUT + j, B_OUT + j, sibling).wait_recv()
                fout_ref[my_out_rows, :] += b_out[j].astype(F32)

        for cp in stage_c:
            cp.start()
        other_half_out = fout_ref.at[pl.ds(pl.multiple_of((1 - c) * half_out, half_out), half_out), :]
        remote(other_half_out, other_half_out, C_OUT, C_OUT, sibling).wait_recv()

        for d in range(N_DEV):
            @pl.when(d != dev)
            def _():
                remote(pack_ref, packs.at[d], PACK + d, PACK + d, sibling).wait_recv()
        total = packs[0]
        for d in range(1, N_DEV):
            total = total + packs[d]

        for cp in stage_a + stage_c:
            cp.wait_send()
        for j in range(N_CHIPS):
            @pl.when(j != chip)
            def _():
                remote(b_out.at[j], b_out.at[j], B_OUT + j, B_OUT + j, sibling).wait_send()
        for d in range(N_DEV):
            @pl.when(d != dev)
            def _():
                remote(pack_ref, packs.at[d], PACK + d, PACK + d, sibling).wait_send()
        return total

    return start, hand_on, finish


def _small_exchange_scratch():
    half_out = SHARD_OUT // 2
    return [
        pltpu.VMEM((SHARD_OUT, D_MODEL), F32),
        pltpu.VMEM((N_CHIPS, half_out, D_MODEL), F32), pltpu.VMEM((N_CHIPS, half_out, D_MODEL), F32),
        pltpu.VMEM((N_CHIPS, half_out, D_MODEL), BF16), pltpu.VMEM((N_CHIPS, half_out, D_MODEL), BF16),
        pltpu.VMEM((PACK_ROWS, D_MODEL), F32), pltpu.VMEM((N_DEV, PACK_ROWS, D_MODEL), F32),
    ]


def _adam_math(w, g, m, v):
    mn = ADAM_B1 * m + (1.0 - ADAM_B1) * g
    vn = ADAM_B2 * v + (1.0 - ADAM_B2) * (g * g)
    m_hat = mn / (1.0 - ADAM_B1 ** ADAM_STEP)
    v_hat = vn / (1.0 - ADAM_B2 ** ADAM_STEP)
    return -ADAM_LR * (m_hat / (jnp.sqrt(v_hat) + ADAM_EPS) + ADAM_WD * w), mn, vn


def _adamw(name, w, g, m, v, tb):
    rows, cols = w.shape

    def body(w_ref, g_ref, m_ref, v_ref, go_ref, d_ref, mo_ref, vo_ref):
        gv = g_ref[...]
        go_ref[...] = gv
        d_ref[...], mo_ref[...], vo_ref[...] = _adam_math(w_ref[...], gv, m_ref[...], v_ref[...])

    spec = pl.BlockSpec((tb, cols), lambda i: (i, 0))
    shape = jax.ShapeDtypeStruct((rows, cols), F32)
    return _call(
        body, name=name, grid=(rows // tb,), out_shape=(shape,) * 4,
        in_specs=[spec] * 4, out_specs=(spec,) * 4,
        compiler_params=_params(32, ("arbitrary",)),
    )(w, g, m, v)


def _adamw_w_in(w_t, g_window, m_t, v_t, tb):
    def body(w_ref, g_ref, m_ref, v_ref, go_ref, d_ref, mo_ref, vo_ref, gt_ref):
        gt_ref[...] = g_ref[...].T
        gv = gt_ref[pl.ds(pl.multiple_of(lax.axis_index("y") * SHARD_SHIFT, SHARD_SHIFT), SHARD_IN), :]
        go_ref[...] = gv
        d_ref[...], mo_ref[...], vo_ref[...] = _adam_math(w_ref[...], gv, m_ref[...], v_ref[...])

    spec = pl.BlockSpec((SHARD_IN, tb), lambda i: (0, i))
    shape = jax.ShapeDtypeStruct((SHARD_IN, D_MODEL), F32)
    return _call(
        body, name="adamw_w_in", grid=(D_MODEL // tb,), out_shape=(shape,) * 4,
        in_specs=[spec, pl.BlockSpec((tb, SHARD_PAD), lambda i: (i, 0)), spec, spec], out_specs=(spec,) * 4,
        scratch_shapes=[pltpu.VMEM((SHARD_PAD, tb), F32)],
        compiler_params=_params(32, ("arbitrary",)),
    )(w_t, g_window, m_t, v_t)


def _adamw_small(sums, params):
    def body(sums_ref, *refs):
        ins, outs = refs[:3 * len(params)], refs[3 * len(params):]
        for k in range(len(params)):
            w_ref, m_ref, v_ref = ins[3 * k:3 * k + 3]
            g = sums_ref[k:k + 1, 0:w_ref.shape[1]]
            go_ref, d_ref, mo_ref, vo_ref = outs[4 * k:4 * k + 4]
            go_ref[...] = g
            d_ref[...], mo_ref[...], vo_ref[...] = _adam_math(w_ref[...], g, m_ref[...], v_ref[...])

    vmem = pl.BlockSpec(memory_space=pltpu.VMEM)
    flat = [a for p in params for a in p]
    shapes = tuple(jax.ShapeDtypeStruct(p[0].shape, F32) for p in params for _ in range(4))
    res = _call(body, name="adamw_small", out_shape=shapes, in_specs=[vmem] * (1 + len(flat)),
                out_specs=(vmem,) * len(shapes), compiler_params=_params(16))(sums, *flat)
    return [res[4 * k:4 * k + 4] for k in range(len(params))]


def kernel(x, norm_g, w_in, att_sinks, ret_gn_g, w_out, final_g, loss_target, m_norm_g, m_w_in, m_att_sinks, m_ret_gn_g, m_w_out, m_final_g, v_norm_g, v_w_in, v_att_sinks, v_ret_gn_g, v_w_out, v_final_g):
    seq = x.shape[1]
    xs, tgt = x[0], loss_target[0]
    final_g2 = final_g.reshape(1, D_MODEL)
    tables = _tables(seq)

    w_in_t, m_w_in_t, v_w_in_t = w_in[0].T, m_w_in[0].T, v_w_in[0].T
    proj, h_t, w_in_full, w_out_full = _in_proj(xs, norm_g, w_in_t, w_out[0], min(TOKENS_PROJ, seq))
    mix, dxo, states, loss_part, gfin, o_att, probs, p_sinks = _mix_fwd(
        proj, xs, tgt, w_out_full, final_g2, ret_gn_g, att_sinks, tables, min(TOKENS_MIX, seq))
    dproj, grad_x, gw_out, gnorm, dgain, dsink = _mix_bwd(
        proj, dxo, mix, o_att, probs, p_sinks, states, xs, w_out_full, w_in_full, norm_g, ret_gn_g, tables,
        min(TOKENS_MIX, seq))
    g_in, g_out, sums = _gw_in_reduce(h_t, dproj, gw_out, (gnorm, gfin, dgain, dsink, loss_part),
                                      min(TOKENS_GW, seq))

    res_in = [r.T for r in _adamw_w_in(w_in_t, g_in, m_w_in_t, v_w_in_t, 256)]
    res_out = _adamw("adamw_w_out", w_out[0], g_out, m_w_out[0], v_w_out[0], SHARD_OUT)
    as_row = lambda a: a.reshape(1, D_MODEL)
    r_norm, r_final, r_gain, r_sink = _adamw_small(sums, [
        (norm_g, m_norm_g, v_norm_g), (final_g2, as_row(m_final_g), as_row(v_final_g)),
        (ret_gn_g, m_ret_gn_g, v_ret_gn_g), (att_sinks, m_att_sinks, v_att_sinks)])

    outs = []
    for k in range(4):
        outs += [r_norm[k], res_in[k][None], r_sink[k], r_gain[k], res_out[k][None], r_final[k].reshape(D_MODEL)]
    return (sums[4, 0], grad_x[None], *outs)
```

```python
import jax
import jax.numpy as jnp
import numpy as np
from jax import lax
from jax.experimental import pallas as pl
from jax.experimental.pallas import tpu as pltpu

F32 = jnp.float32
BF16 = jnp.bfloat16

D_MODEL = 1024
ATT_HEADS = 8
ATT_KV_HEADS = 2
ATT_HEAD_DIM = 64
RET_HEADS = 4
RET_QK_DIM = 64
RET_V_DIM = 128
BLK = 128
ROT_BASE = 10000.0
RMS_EPS = 1e-6
GN_EPS = 1e-6
NEG_INF = -1e30
ATT_SCALE = ATT_HEAD_DIM ** -0.5
RET_SCALE = RET_QK_DIM ** -0.5

ATT_WIDTH = ATT_HEADS * ATT_HEAD_DIM
ATT_KV_WIDTH = ATT_KV_HEADS * ATT_HEAD_DIM
RET_QK_WIDTH = RET_HEADS * RET_QK_DIM
RET_WIDTH = RET_HEADS * RET_V_DIM
MIX_WIDTH = ATT_WIDTH + RET_WIDTH
OFF_AQ = 0
OFF_AK = OFF_AQ + ATT_WIDTH
OFF_AV = OFF_AK + ATT_KV_WIDTH
OFF_AZ = OFF_AV + ATT_KV_WIDTH
OFF_RQ = OFF_AZ + ATT_WIDTH
OFF_RK = OFF_RQ + RET_QK_WIDTH
OFF_RV = OFF_RK + RET_QK_WIDTH
OFF_RZ = OFF_RV + RET_WIDTH
IN_WIDTH = OFF_RZ + RET_WIDTH

LANE = 128
BF16_ROWS = 16
HALF_LANE = LANE // 2
PAIRS = RET_QK_WIDTH // LANE
assert ATT_HEAD_DIM == HALF_LANE and RET_QK_DIM == HALF_LANE and RET_V_DIM == LANE and ATT_KV_WIDTH == LANE

N_CHIPS = 4
N_DEV = 8
SHARD_IN = IN_WIDTH // N_CHIPS
SHARD_PAD = 768
SHARD_SHIFT = SHARD_PAD - SHARD_IN
WIN_START = tuple((j * SHARD_IN) // LANE * LANE for j in range(N_CHIPS))
SHARD_OUT = MIX_WIDTH // N_CHIPS

PACK_PARTS = 5
PACK_ROWS = 8
assert PACK_PARTS <= PACK_ROWS

ADAM_LR = 0.001
ADAM_B1 = 0.9
ADAM_B2 = 0.999
ADAM_EPS = 1e-08
ADAM_WD = 0.01
ADAM_STEP = 10

VMEM_CAP = 64 * 1024 * 1024
TOKENS_PROJ = 1024
TOKENS_MIX = 512
TOKENS_GW = 2048
MESH_ID = pl.DeviceIdType.MESH


def _call(body, **kw):
    return pl.pallas_call(body, **kw)


def _params(vmem_mb, semantics=None):
    assert vmem_mb * 1024 * 1024 < VMEM_CAP
    return pltpu.CompilerParams(dimension_semantics=semantics, vmem_limit_bytes=vmem_mb * 1024 * 1024)


def _dot(a, b):
    return jnp.dot(a, b, preferred_element_type=F32)


def _dot_nt(a, b):
    return lax.dot_general(a, b, (((1,), (1,)), ((), ())), preferred_element_type=F32)


def _dot_tn(a, b):
    return lax.dot_general(a, b, (((0,), (0,)), ((), ())), preferred_element_type=F32)


def _sigmoid(z):
    return 1.0 / (1.0 + jnp.exp(-z))


def _const_spec(shape):
    nd = len(shape)
    return pl.BlockSpec(shape, lambda i: (0,) * nd, pipeline_mode=pl.Buffered(1))


def _tables(seq):
    f32 = np.float32
    pos = np.arange(seq, dtype=f32)
    theta = (f32(1.0) / (f32(ROT_BASE) ** np.linspace(0.0, 1.0, RET_QK_DIM // 2, dtype=f32))).astype(f32)
    ang = (pos[:, None] * theta[None, :]).astype(f32)
    cos, sin = np.cos(ang), np.sin(ang)
    cos2 = np.repeat(cos, 2, axis=1)
    sin2 = np.stack([-sin, sin], axis=-1).reshape(seq, RET_QK_DIM)
    cos_t = np.tile(cos2, (1, 2))
    sin_t = np.tile(sin2, (1, 2))

    log_gamma = np.log(f32(1.0) - f32(2.0) ** (f32(-5.0) - np.arange(RET_HEADS, dtype=f32))).astype(f32)
    idx = np.arange(BLK, dtype=f32)
    rel = idx[:, None] - idx[None, :]
    decay_in = np.where(rel >= 0, np.exp(log_gamma[:, None, None] * np.maximum(rel, f32(0.0))), f32(0.0))
    k_dec = np.exp(log_gamma[:, None] * (BLK - 1 - idx)[None, :])
    q_dec = np.exp(log_gamma[:, None] * (idx + 1)[None, :])
    chunk_decay = np.exp(log_gamma * f32(BLK))
    kdec_t = np.repeat(k_dec.T, RET_QK_DIM, axis=1)
    qdec_t = np.repeat(q_dec.T, RET_QK_DIM, axis=1)
    cd_t = np.broadcast_to(chunk_decay[:, None, None], (RET_HEADS, RET_QK_DIM, RET_V_DIM))
    decay_in = decay_in.reshape(PAIRS, 2 * BLK, BLK)
    cd_t = cd_t.reshape(PAIRS, 2 * RET_QK_DIM, RET_V_DIM)

    key = np.arange(BLK)[:, None]
    query = np.arange(2 * BLK)[None, :] % BLK
    bias = np.stack([np.where(key > query, NEG_INF, 0.0), np.zeros((BLK, 2 * BLK))])
    return tuple(jnp.asarray(np.ascontiguousarray(a), F32) for a in (cos_t, sin_t, decay_in, qdec_t, kdec_t, cd_t, bias))


def _low_lanes(shape):
    lane = lax.broadcasted_iota(jnp.int32, shape, len(shape) - 1)
    return (lane & HALF_LANE) == 0


def _split_heads(t):
    low = _low_lanes(t.shape)
    zero = jnp.zeros_like(t)
    return jnp.where(low, t, zero), jnp.where(low, zero, t)


def _swap_pairs(t):
    lane = lax.broadcasted_iota(jnp.int32, t.shape, 1)
    nxt = pltpu.roll(t, t.shape[1] - 1, 1)
    prv = pltpu.roll(t, 1, 1)
    return jnp.where((lane & 1) == 0, nxt, prv)


def _per_tile(fn, t):
    return jnp.concatenate([fn(_tile(t, i)) for i in range(t.shape[1] // LANE)], axis=1)


def _rotate(t, cos_t, sin_t):
    return _per_tile(lambda a: a * cos_t + _swap_pairs(a) * sin_t, t)


def _rotate_transposed(d, cos_t, sin_t):
    return _per_tile(lambda a: a * cos_t + _swap_pairs(a * sin_t), d)


def _kv_operands(cat):
    low = _low_lanes(cat.shape)
    swapped = pltpu.roll(cat, HALF_LANE, 1)
    zero = jnp.zeros_like(cat)
    pick = lambda a, b: jnp.where(low, a, b).astype(BF16)
    return ((pick(cat, zero), pick(zero, swapped)), (pick(swapped, zero), pick(zero, cat)))


def _stack_tiles(t, first_tile):
    a = t[:, first_tile * LANE:(first_tile + 1) * LANE]
    b = t[:, (first_tile + 1) * LANE:(first_tile + 2) * LANE]
    return jnp.concatenate([a, b], axis=0)


def _sink_rows(sinks_ref, group):
    first = lax.broadcasted_iota(jnp.int32, (1, 2 * BLK), 1) < BLK

    def row(h0, h1):
        return jnp.where(first, sinks_ref[0, group * 4 + h0], sinks_ref[0, group * 4 + h1])
    return row(0, 2), row(1, 3)


ATT_PROBLEMS = tuple((g, hi) for g in range(ATT_KV_HEADS) for hi in range(2))


def _in_previous_block():
    key = lax.broadcasted_iota(jnp.int32, (BLK, 2 * BLK), 0)
    query = lax.broadcasted_iota(jnp.int32, (BLK, 2 * BLK), 1) & (BLK - 1)
    return key > query


def _fold(t, prev):
    return jnp.where(prev, t[0:BLK], t[BLK:])


def _unfold(t, prev):
    zero = jnp.zeros_like(t)
    return jnp.concatenate([jnp.where(prev, t, zero), jnp.where(prev, zero, t)], axis=0)


def _attn_scores(qs, k_ops, bias, prev):
    return [_fold(_dot_nt(k_ops[g][hi], qs[g]), prev) + bias for g, hi in ATT_PROBLEMS]


def _attn_softmax(s, sinks_ref):
    sink = [_sink_rows(sinks_ref, g)[hi] for g, hi in ATT_PROBLEMS]
    m = [jnp.maximum(jnp.max(si, axis=0, keepdims=True), ki) for si, ki in zip(s, sink)]
    e = [jnp.exp(si - mi) for si, mi in zip(s, m)]
    es = [jnp.exp(ki - mi) for ki, mi in zip(sink, m)]
    inv = [1.0 / (jnp.sum(ei, axis=0, keepdims=True) + esi) for ei, esi in zip(e, es)]
    return [ei * ii for ei, ii in zip(e, inv)], [esi * ii for esi, ii in zip(es, inv)]


def _group_norm_all(outs):
    mu = [jnp.mean(o, axis=1, keepdims=True) for o in outs]
    xc = [o - m for o, m in zip(outs, mu)]
    var = [jnp.mean(c * c, axis=1, keepdims=True) for c in xc]
    rstd = [lax.rsqrt(v + GN_EPS) for v in var]
    return [c * r for c, r in zip(xc, rstd)], rstd


def _proj_cols(p_ref, rows, start, stop):
    pieces = []
    while start < stop:
        k = max(j for j in range(N_CHIPS) if WIN_START[j] <= start)
        end = min(stop, WIN_START[k] + SHARD_PAD)
        pieces.append(p_ref[k, rows, start - WIN_START[k]:end - WIN_START[k]])
        start = end
    return pieces[0] if len(pieces) == 1 else jnp.concatenate(pieces, axis=1)


def _retention_operands(p_ref, rows, cos_b, sin_b, qdec, kdec):
    qr = _rotate(_proj_cols(p_ref, rows, OFF_RQ, OFF_RQ + RET_QK_WIDTH).astype(F32), cos_b, sin_b)
    kr = _rotate(_proj_cols(p_ref, rows, OFF_RK, OFF_RK + RET_QK_WIDTH).astype(F32), cos_b, sin_b) * RET_SCALE
    return qr, kr, qr * qdec, kr * kdec


def _tile(t, i):
    return t[:, i * LANE:(i + 1) * LANE]


GATHER_SEMS = 7


def _allgather_steps(src_ref, full_ref, blk_ref, send_sems, recv_sems):
    block_rows = blk_ref.shape[0]
    assert block_rows % BF16_ROWS == 0 and full_ref.shape[0] == N_DEV * block_rows
    x, y, c = lax.axis_index("x"), lax.axis_index("y"), lax.axis_index("c")
    me, sibling = (x, y, c), (x, y, 1 - c)
    chips = [(1 - x, y), (x, 1 - y), (1 - x, 1 - y)]

    def rows(px, py, pc):
        return full_ref.at[pl.ds(pl.multiple_of((4 * px + 2 * py + pc) * block_rows, BF16_ROWS), block_rows), :]

    def copy(k, block, to, src=None):
        return pltpu.make_async_remote_copy(
            src_ref=rows(*block) if src is None else src, dst_ref=rows(*block),
            send_sem=send_sems.at[k], recv_sem=recv_sems.at[k], device_id=to, device_id_type=MESH_ID)

    first = [copy(0, me, sibling, src=blk_ref)] + [copy(1 + j, me, (*chip, c), src=blk_ref) for j, chip in enumerate(chips)]
    passed = [copy(4 + j, (*chip, c), sibling) for j, chip in enumerate(chips)]

    def start():
        blk_ref[...] = src_ref[pl.ds(pl.multiple_of(c * block_rows, BF16_ROWS), block_rows), :].astype(BF16)
        rows(*me)[...] = blk_ref[...]
        for cp in first:
            cp.start()

    def forward():
        for j, chip in enumerate(chips):
            copy(1 + j, (*chip, c), me).wait_recv()
            passed[j].start()

    def finish():
        copy(0, sibling, me).wait_recv()
        for j, chip in enumerate(chips):
            copy(4 + j, (*chip, 1 - c), me).wait_recv()
        for cp in first + passed:
            cp.wait_send()

    return start, forward, finish


WINDOW_HALF = SHARD_PAD // 2
WINDOW_SEMS = GATHER_SEMS + 1


def _window_gather(wi_ref, w4_ref, blk_ref, edge_ref, send_sems, recv_sems):
    x, y, c = lax.axis_index("x"), lax.axis_index("y"), lax.axis_index("c")
    me, sibling, pair = (x, y, c), (x, y, 1 - c), (x, 1 - y, c)
    chips = [(1 - x, y), (x, 1 - y), (1 - x, 1 - y)]
    own_rows, edge_rows = SHARD_IN - WINDOW_HALF, SHARD_SHIFT
    sends_edge = y == c

    def rows(px, py, pc):
        return w4_ref.at[pl.ds(pl.multiple_of((4 * px + 2 * py + pc) * WINDOW_HALF, BF16_ROWS), WINDOW_HALF), :]

    def copy(k, block, to, src=None):
        return pltpu.make_async_remote_copy(
            src_ref=rows(*block) if src is None else src, dst_ref=rows(*block),
            send_sem=send_sems.at[k], recv_sem=recv_sems.at[k], device_id=to, device_id_type=MESH_ID)

    edge_copy = pltpu.make_async_remote_copy(
        src_ref=edge_ref, dst_ref=edge_ref, send_sem=send_sems.at[GATHER_SEMS], recv_sem=recv_sems.at[GATHER_SEMS],
        device_id=pair, device_id_type=MESH_ID)
    first = [copy(0, me, sibling, src=blk_ref)] + [copy(1 + j, me, (*chip, c), src=blk_ref) for j, chip in enumerate(chips)]
    passed = [copy(4 + j, (*chip, c), sibling) for j, chip in enumerate(chips)]

    def at(start, size):
        return pl.ds(pl.multiple_of(start, BF16_ROWS), size)

    def start():
        @pl.when(sends_edge)
        def _():
            edge_ref[...] = wi_ref[at((1 - y) * (SHARD_IN - edge_rows), edge_rows), :].astype(BF16)
            edge_copy.start()
            blk_ref[...] = wi_ref[at(c * own_rows, WINDOW_HALF), :].astype(BF16)

        @pl.when(jnp.logical_not(sends_edge))
        def _():
            blk_ref[at(y * edge_rows, own_rows), :] = wi_ref[at(c * WINDOW_HALF, own_rows), :].astype(BF16)
            edge_copy.wait_recv()
            blk_ref[at((1 - y) * own_rows, edge_rows), :] = edge_ref[...]

        rows(*me)[...] = blk_ref[...]
        for cp in first[:-1]:
            cp.start()
        copy(0, sibling, me).wait_recv()

    def send_far():
        first[-1].start()

    def forward(j):
        copy(1 + j, (*chips[j], c), me).wait_recv()
        passed[j].start()

    def wait_forwarded(j):
        copy(4 + j, (*chips[j], 1 - c), me).wait_recv()

    def finish():
        for cp in first + passed:
            cp.wait_send()

        @pl.when(sends_edge)
        def _():
            edge_copy.wait_send()

    return start, send_far, forward, wait_forwarded, finish


def _in_proj(x, norm_g, w_in_t_shard, w_out_shard, tb):
    seq = x.shape[0]
    nblk = seq // tb
    last = nblk - 1
    chip_of_panel = (None, 1, 0, 2)

    def body(win_ref, x_ref, g_ref, wi_ref, wo_ref, p_ref, ht_ref, wt_ref, wout_ref,
             w4, blk, edge, hbuf, wout_full, blko, send_sems, recv_sems, send_sems_o, recv_sems_o):
        q, i = pl.program_id(0), pl.program_id(1)
        chip = 2 * lax.axis_index("x") + lax.axis_index("y")
        in_start, in_send_far, in_forward, in_wait_forwarded, in_finish = _window_gather(
            wi_ref, w4, blk, edge, send_sems, recv_sems)
        out_start, out_forward, out_finish = _allgather_steps(wo_ref, wout_full, blko, send_sems_o, recv_sems_o)
        rows = pl.ds(pl.multiple_of(i * tb, tb), tb)

        @pl.when(jnp.logical_and(q == 0, i == 0))
        def _():
            in_start()

        @pl.when(jnp.logical_and(q == 0, i == min(last // 2 + 1, last)))
        def _():
            in_send_far()

        @pl.when(jnp.logical_and(q == 1, i == 0))
        def _():
            out_start()

        for panel in range(1, N_CHIPS):
            @pl.when(jnp.logical_and(q == panel, i == 0))
            def _():
                in_wait_forwarded(chip_of_panel[panel])

        @pl.when(q == 0)
        def _():
            xv = x_ref[...]
            r = lax.rsqrt(jnp.mean(xv * xv, axis=1, keepdims=True) + RMS_EPS)
            h = (xv * r) * g_ref[...]
            hbuf[rows, :] = h.astype(BF16)
            ht_ref[...] = h.T.astype(BF16)

        owner = jnp.bitwise_xor(chip, q)
        window = w4[pl.ds(pl.multiple_of(owner * SHARD_PAD, SHARD_PAD), SHARD_PAD), :]
        p_ref[...] = _dot_nt(hbuf[rows, :], window).astype(BF16)

        for panel in range(1, N_CHIPS):
            @pl.when(jnp.logical_and(q == panel - 1, i == last))
            def _():
                in_forward(chip_of_panel[panel])

        @pl.when(jnp.logical_and(q == N_CHIPS - 1, i == 0))
        def _():
            out_forward()

        @pl.when(jnp.logical_and(q == N_CHIPS - 1, i == last))
        def _():
            in_finish()
            for k in range(N_CHIPS):
                first = k * SHARD_PAD + (k % 2) * SHARD_SHIFT
                wt_ref[k * SHARD_IN:(k + 1) * SHARD_IN, :] = w4[first:first + SHARD_IN, :]
            out_finish()
            wout_ref[...] = wout_full[...]

    whole = lambda shape: pl.BlockSpec(shape, lambda q, i, win: (0,) * len(shape), pipeline_mode=pl.Buffered(1))
    first_panel = lambda q, i: jnp.where(q == 0, i, last)
    grid_spec = pltpu.PrefetchScalarGridSpec(
        num_scalar_prefetch=1, grid=(N_CHIPS, nblk),
        in_specs=[pl.BlockSpec((tb, D_MODEL), lambda q, i, win: (first_panel(q, i), 0)), whole((1, D_MODEL)),
                  whole((SHARD_IN, D_MODEL)), whole((SHARD_OUT, D_MODEL))],
        out_specs=(pl.BlockSpec((None, tb, SHARD_PAD), lambda q, i, win: (win[q], i, 0)),
                   pl.BlockSpec((D_MODEL, tb), lambda q, i, win: (0, first_panel(q, i))),
                   whole((IN_WIDTH, D_MODEL)), whole((MIX_WIDTH, D_MODEL))),
        scratch_shapes=[
            pltpu.VMEM((N_CHIPS * SHARD_PAD, D_MODEL), BF16), pltpu.VMEM((WINDOW_HALF, D_MODEL), BF16),
            pltpu.VMEM((SHARD_SHIFT, D_MODEL), BF16), pltpu.VMEM((seq, D_MODEL), BF16),
            pltpu.VMEM((MIX_WIDTH, D_MODEL), BF16), pltpu.VMEM((SHARD_OUT // 2, D_MODEL), BF16),
            pltpu.SemaphoreType.DMA((WINDOW_SEMS,)), pltpu.SemaphoreType.DMA((WINDOW_SEMS,)),
            pltpu.SemaphoreType.DMA((GATHER_SEMS,)), pltpu.SemaphoreType.DMA((GATHER_SEMS,)),
        ])
    chip = 2 * lax.axis_index("x") + lax.axis_index("y")
    owner = jnp.bitwise_xor(chip, jnp.arange(N_CHIPS, dtype=jnp.int32))
    return _call(
        body, name="in_proj", grid_spec=grid_spec,
        out_shape=(jax.ShapeDtypeStruct((N_CHIPS, seq, SHARD_PAD), BF16), jax.ShapeDtypeStruct((D_MODEL, seq), BF16),
                   jax.ShapeDtypeStruct((IN_WIDTH, D_MODEL), BF16), jax.ShapeDtypeStruct((MIX_WIDTH, D_MODEL), BF16)),
        compiler_params=_params(60, ("arbitrary", "arbitrary")),
    )(owner.astype(jnp.int32), x, norm_g, w_in_t_shard, w_out_shard)


def _mix_fwd(proj, x, target, w_out, final_g, gn_gain, sinks, tables, tb):
    seq = x.shape[0]
    nsub = tb // BLK
    cos_t, sin_t, decay_in, qdec_t, kdec_t, cd_t, bias_t = tables

    def body(p_ref, x_ref, t_ref, cos_ref, sin_ref, wout_ref, fg_ref, gain_ref, sinks_ref, din_ref, qdec_ref,
             kdec_ref, cd_ref, bias_ref, mix_ref, dxo_ref, st_ref, loss_ref, gfin_ref, oatt_ref, probs_ref, psink_ref,
             kprev_ref, vprev_ref, state_ref):
        i = pl.program_id(0)

        @pl.when(i == 0)
        def _():
            kprev_ref[...] = jnp.zeros_like(kprev_ref)
            vprev_ref[...] = jnp.zeros_like(vprev_ref)
            state_ref[...] = jnp.zeros_like(state_ref)
            loss_ref[...] = jnp.zeros_like(loss_ref)
            gfin_ref[...] = jnp.zeros_like(gfin_ref)

        prev = _in_previous_block()

        def sub(j, carry):
            kp, vp, states = carry
            rows = pl.ds(pl.multiple_of(j * BLK, BLK), BLK)
            bias = bias_ref[jnp.where(jnp.logical_or(i > 0, j > 0), 1, 0)]

            aq = _proj_cols(p_ref, rows, OFF_AQ, OFF_AQ + ATT_WIDTH)
            ak = _proj_cols(p_ref, rows, OFF_AK, OFF_AK + ATT_KV_WIDTH).astype(F32)
            av = _proj_cols(p_ref, rows, OFF_AV, OFF_AV + ATT_KV_WIDTH).astype(F32)
            az = _proj_cols(p_ref, rows, OFF_AZ, OFF_AZ + ATT_WIDTH).astype(F32)
            k_ops = _kv_operands(jnp.concatenate([kp, ak], axis=0))
            v_ops = _kv_operands(jnp.concatenate([vp, av], axis=0))
            qs = [(_stack_tiles(aq, 2 * g) * ATT_SCALE).astype(BF16) for g in range(ATT_KV_HEADS)]

            qr, kr, qd, kd = _retention_operands(p_ref, rows, cos_ref[rows, :], sin_ref[rows, :],
                                                 qdec_ref[...], kdec_ref[...])
            s = _attn_scores(qs, k_ops, bias, prev)
            heads = [(t, hh) for t in range(PAIRS) for hh in range(2)]
            sc = [_dot_nt(jnp.concatenate(_split_heads(_tile(qr, t)), axis=0).astype(BF16), _tile(kr, t).astype(BF16))
                  * din_ref[t] for t in range(PAIRS)]
            qd_heads = [_split_heads(_tile(qd, t)) for t in range(PAIRS)]
            state_b = [states[t].astype(BF16) for t in range(PAIRS)]
            vs = [_proj_cols(p_ref, rows, OFF_RV + h * RET_V_DIM, OFF_RV + (h + 1) * RET_V_DIM) for h in range(RET_HEADS)]
            rzs = [_proj_cols(p_ref, rows, OFF_RZ + h * RET_V_DIM, OFF_RZ + (h + 1) * RET_V_DIM).astype(F32)
                   for h in range(RET_HEADS)]
            lhs = [jnp.concatenate([sc[t][hh * BLK:(hh + 1) * BLK].astype(BF16), qd_heads[t][hh].astype(BF16)], axis=1)
                   for t, hh in heads]
            p, p_sink = _attn_softmax(s, sinks_ref)
            o_ret = [_dot(lhs[2 * t + hh], jnp.concatenate([vs[2 * t + hh], state_b[t]], axis=0)) for t, hh in heads]
            p = [pi.astype(BF16) for pi in p]
            o_tiles = []
            for g in range(ATT_KV_HEADS):
                p_cat = jnp.concatenate([_unfold(p[2 * g], prev), _unfold(p[2 * g + 1], prev)], axis=0)
                o = _dot_tn(p_cat, jnp.concatenate(v_ops[g], axis=0))
                o_tiles += [o[0:BLK], o[BLK:]]
            ons, _ = _group_norm_all(o_ret)
            new_states = [states[t] * cd_ref[t]
                          + _dot_tn(jnp.concatenate(_split_heads(_tile(kd, t)), axis=0).astype(BF16),
                                    jnp.concatenate([vs[2 * t], vs[2 * t + 1]], axis=0)) for t in range(PAIRS)]
            o_att = jnp.concatenate(o_tiles, axis=1)
            out = [o_att * (az * _sigmoid(az))]
            out += [(ons[h] * gain_ref[:, h * RET_V_DIM:(h + 1) * RET_V_DIM]) * (rzs[h] * _sigmoid(rzs[h]))
                    for h in range(RET_HEADS)]
            mix_ref[rows, :] = jnp.concatenate(out, axis=1).astype(BF16)
            oatt_ref[rows, :] = o_att
            probs_ref[j] = jnp.stack(p)
            psink_ref[j] = jnp.concatenate(p_sink, axis=0)
            st_ref[j] = jnp.stack(states)
            return ak, av, tuple(new_states)

        carry = (kprev_ref[...], vprev_ref[...], tuple(state_ref[t] for t in range(PAIRS)))
        for jb in range(nsub):
            carry = sub(jnp.int32(jb), carry)
        kp, vp, states = carry
        kprev_ref[...] = kp
        vprev_ref[...] = vp
        state_ref[...] = jnp.stack(states)

        xo = x_ref[...] + _dot(mix_ref[...], wout_ref[...])
        r2 = lax.rsqrt(jnp.mean(xo * xo, axis=1, keepdims=True) + RMS_EPS)
        xn = xo * r2
        err = xn * fg_ref[...] - t_ref[...]
        loss_ref[...] += jnp.sum(err * err) * (0.5 / D_MODEL)
        dy = err * (1.0 / D_MODEL)
        gfin_ref[...] += jnp.sum(dy * xn, axis=0, keepdims=True)
        u = dy * fg_ref[...]
        dxo_ref[...] = r2 * u - xn * (r2 * jnp.mean(u * xn, axis=1, keepdims=True))

    blk_rows = lambda w: pl.BlockSpec((tb, w), lambda i: (i, 0))
    state_shape = (PAIRS, 2 * RET_QK_DIM, RET_V_DIM)
    return _call(
        body, name="mix_fwd", grid=(seq // tb,),
        out_shape=(
            jax.ShapeDtypeStruct((seq, MIX_WIDTH), BF16),
            jax.ShapeDtypeStruct((seq, D_MODEL), F32),
            jax.ShapeDtypeStruct((seq // BLK,) + state_shape, F32),
            jax.ShapeDtypeStruct((8, LANE), F32),
            jax.ShapeDtypeStruct((1, D_MODEL), F32),
            jax.ShapeDtypeStruct((seq, ATT_WIDTH), F32),
            jax.ShapeDtypeStruct((seq // BLK, len(ATT_PROBLEMS), BLK, 2 * BLK), BF16),
            jax.ShapeDtypeStruct((seq // BLK, len(ATT_PROBLEMS), 2 * BLK), F32),
        ),
        in_specs=[
            pl.BlockSpec((N_CHIPS, tb, SHARD_PAD), lambda i: (0, i, 0)),
            blk_rows(D_MODEL), blk_rows(D_MODEL), blk_rows(LANE), blk_rows(LANE),
            _const_spec((MIX_WIDTH, D_MODEL)), _const_spec((1, D_MODEL)), _const_spec((1, RET_WIDTH)),
            pl.BlockSpec(memory_space=pltpu.SMEM),
            _const_spec((PAIRS, 2 * BLK, BLK)), _const_spec((BLK, RET_QK_WIDTH)), _const_spec((BLK, RET_QK_WIDTH)),
            _const_spec(state_shape), _const_spec((2, BLK, 2 * BLK)),
        ],
        out_specs=(
            blk_rows(MIX_WIDTH), blk_rows(D_MODEL),
            pl.BlockSpec((nsub,) + state_shape, lambda i: (i, 0, 0, 0)),
            _const_spec((8, LANE)), _const_spec((1, D_MODEL)), blk_rows(ATT_WIDTH),
            pl.BlockSpec((nsub, len(ATT_PROBLEMS), BLK, 2 * BLK), lambda i: (i, 0, 0, 0)),
            pl.BlockSpec((nsub, len(ATT_PROBLEMS), 2 * BLK), lambda i: (i, 0, 0)),
        ),
        scratch_shapes=[
            pltpu.VMEM((BLK, ATT_KV_WIDTH), F32), pltpu.VMEM((BLK, ATT_KV_WIDTH), F32),
            pltpu.VMEM(state_shape, F32),
        ],
        compiler_params=_params(48, ("arbitrary",)),
    )(proj, x, target, cos_t, sin_t, w_out, final_g, gn_gain, sinks, decay_in, qdec_t, kdec_t, cd_t, bias_t)


def _mix_bwd(proj, dxo, mix, o_att, probs, p_sinks, states, x, w_out, w_in_t, norm_g, gn_gain, tables, tb):
    seq = dxo.shape[0]
    nsub = tb // BLK
    nblk = seq // tb
    cos_t, sin_t, decay_in, qdec_t, kdec_t, cd_t, _ = tables
    kv_cols = OFF_AK // (2 * ATT_KV_WIDTH)
    state_shape = (PAIRS, 2 * RET_QK_DIM, RET_V_DIM)

    def body(p_ref, pkv_ref, dxo_ref, mix_ref, oatt_ref, probs_ref, psink_ref, st_ref, cos_ref, sin_ref, x_ref, wout_ref,
             win_ref, g_ref, gain_ref, din_ref, qdec_ref, kdec_ref, cd_ref,
             dp_ref, gx_ref, gwout_ref, gnorm_ref, dgain_ref, dsink_ref,
             dmix_ref, kv_ref, dkc_ref, dvc_ref, gst_ref):
        i = pl.program_id(0)

        @pl.when(i == 0)
        def _():
            gwout_ref[...] = jnp.zeros_like(gwout_ref)
            gnorm_ref[...] = jnp.zeros_like(gnorm_ref)
            dgain_ref[...] = jnp.zeros_like(dgain_ref)
            dsink_ref[...] = jnp.zeros_like(dsink_ref)
            dkc_ref[...] = jnp.zeros_like(dkc_ref)
            dvc_ref[...] = jnp.zeros_like(dvc_ref)
            gst_ref[...] = jnp.zeros_like(gst_ref)

        dxo_b = dxo_ref[...].astype(BF16)
        dmix_ref[...] = _dot_nt(dxo_b, wout_ref[...])
        gwout_ref[...] += _dot_tn(mix_ref[...], dxo_b)
        kv_ref[0:BLK, :] = pkv_ref[...].astype(F32)
        kv_ref[BLK:, :] = _proj_cols(p_ref, slice(None), OFF_AK, OFF_AK + 2 * ATT_KV_WIDTH).astype(F32)
        low = _low_lanes((BLK, LANE))
        low2 = _low_lanes((2 * BLK, LANE))
        lane = lax.broadcasted_iota(jnp.int32, (1, LANE), 1)
        prev = _in_previous_block()

        def sub(jj, carry):
            dkc, dvc, gsts, dgain, dsink = carry
            j = nsub - 1 - jj
            rows = pl.ds(pl.multiple_of(j * BLK, BLK), BLK)
            both = pl.ds(pl.multiple_of(j * BLK, BLK), 2 * BLK)


            aq = _proj_cols(p_ref, rows, OFF_AQ, OFF_AQ + ATT_WIDTH)
            az = _proj_cols(p_ref, rows, OFF_AZ, OFF_AZ + ATT_WIDTH).astype(F32)
            k_ops = _kv_operands(kv_ref[both, 0:ATT_KV_WIDTH])
            v_ops = _kv_operands(kv_ref[both, ATT_KV_WIDTH:2 * ATT_KV_WIDTH])
            da = dmix_ref[rows, 0:ATT_WIDTH]
            sig = _sigmoid(az)
            d_o = da * (az * sig)
            qs = [(_stack_tiles(aq, 2 * g) * ATT_SCALE).astype(BF16) for g in range(ATT_KV_HEADS)]
            dos = [_stack_tiles(d_o, 2 * g).astype(BF16) for g in range(ATT_KV_HEADS)]
            p_b = [probs_ref[j, k] for k in range(len(ATT_PROBLEMS))]
            p = [pk.astype(F32) for pk in p_b]
            sink_all = psink_ref[j]
            p_sink = [sink_all[k:k + 1, :] for k in range(len(ATT_PROBLEMS))]
            cos_b, sin_b = cos_ref[rows, :], sin_ref[rows, :]
            qdec, kdec = qdec_ref[...], kdec_ref[...]
            qr, kr, qd, kd = _retention_operands(p_ref, rows, cos_b, sin_b, qdec, kdec)
            heads = [(t, hh) for t in range(PAIRS) for hh in range(2)]
            head_cols = [slice(h * RET_V_DIM, (h + 1) * RET_V_DIM) for h in range(RET_HEADS)]
            q_rows = [jnp.concatenate(_split_heads(_tile(qr, t)), axis=0).astype(BF16) for t in range(PAIRS)]
            k_rows = [jnp.concatenate(_split_heads(_tile(kr, t)), axis=0).astype(BF16) for t in range(PAIRS)]
            din = [din_ref[t] for t in range(PAIRS)]

            dpr = [_fold(_dot_nt(v_ops[g][hi], dos[g]), prev) for g, hi in ATT_PROBLEMS]
            sc = [(_dot_nt(q_rows[t], _tile(kr, t).astype(BF16)) * din[t]).astype(BF16) for t in range(PAIRS)]
            qd_heads = [_split_heads(_tile(qd, t)) for t in range(PAIRS)]
            kd_heads = [_split_heads(_tile(kd, t)) for t in range(PAIRS)]
            state_b = [st_ref[j, t].astype(BF16) for t in range(PAIRS)]
            gst_b = [gsts[t].astype(BF16) for t in range(PAIRS)]
            vs = [_proj_cols(p_ref, rows, OFF_RV + h * RET_V_DIM, OFF_RV + (h + 1) * RET_V_DIM) for h in range(RET_HEADS)]
            rzs = [_proj_cols(p_ref, rows, OFF_RZ + h * RET_V_DIM, OFF_RZ + (h + 1) * RET_V_DIM).astype(F32)
                   for h in range(RET_HEADS)]
            drs = [dmix_ref[rows, ATT_WIDTH + h * RET_V_DIM:ATT_WIDTH + (h + 1) * RET_V_DIM] for h in range(RET_HEADS)]
            gains = [gain_ref[:, c] for c in head_cols]
            lhs = [jnp.concatenate([sc[t][hh * BLK:(hh + 1) * BLK], qd_heads[t][hh].astype(BF16)], axis=1) for t, hh in heads]
            rhs = [jnp.concatenate([vs[2 * t + hh], state_b[t]], axis=0) for t, hh in heads]

            delta = [jnp.sum(pi * di, axis=0, keepdims=True) for pi, di in zip(p, dpr)]
            ds = [_unfold((pi * (di - ti)).astype(BF16), prev) for pi, di, ti in zip(p, dpr, delta)]
            o_ret = [_dot(l, r) for l, r in zip(lhs, rhs)]
            for (g, hi), ki, ti in zip(ATT_PROBLEMS, p_sink, delta):
                sink_part = ki * ti
                for half in range(2):
                    tot = jnp.sum(sink_part[:, half * BLK:(half + 1) * BLK], axis=1, keepdims=True)
                    dsink = dsink - jnp.where(lane == 4 * g + 2 * half + hi, tot, 0.0)

            dq_tiles, dk_sums, dv_sums = [], [], []
            for g in range(ATT_KV_HEADS):
                ds_cat = jnp.concatenate([ds[2 * g], ds[2 * g + 1]], axis=0)
                p_cat = jnp.concatenate([_unfold(p_b[2 * g], prev), _unfold(p_b[2 * g + 1], prev)], axis=0)
                dqs = _dot_tn(ds_cat, jnp.concatenate(k_ops[g], axis=0)) * ATT_SCALE
                dq_tiles += [dqs[0:BLK], dqs[BLK:]]
                dk_sums.append(_dot(ds_cat, qs[g]))
                dv_sums.append(_dot(p_cat, dos[g]))
            ons, rstds = _group_norm_all(o_ret)
            sig_r = [_sigmoid(z) for z in rzs]
            dgn = [d * (z * g) for d, z, g in zip(drs, rzs, sig_r)]
            dz_parts = [d * (o * gn) * (g * (1.0 + z * (1.0 - g))) for d, o, gn, g, z in zip(drs, ons, gains, sig_r, rzs)]
            dgain_parts = [jnp.sum(d * o, axis=0, keepdims=True) for d, o in zip(dgn, ons)]
            don = [d * gn for d, gn in zip(dgn, gains)]
            mean_don = [jnp.mean(d, axis=1, keepdims=True) for d in don]
            mean_don_on = [jnp.mean(d * o, axis=1, keepdims=True) for d, o in zip(don, ons)]
            dob = [(r * (d - a - o * b)).astype(BF16) for r, d, a, o, b in zip(rstds, don, mean_don, ons, mean_don_on)]

            dlhs = [_dot_nt(d, r) for d, r in zip(dob, rhs)]
            drhs = [_dot_tn(l, d) for l, d in zip(lhs, dob)]
            dkds = [_dot_nt(vs[2 * t + hh], gst_b[t]) for t, hh in heads]
            dv_parts = [drhs[2 * t + hh][0:BLK] + _dot(kd_heads[t][hh].astype(BF16), gst_b[t]) for t, hh in heads]
            daz = da * oatt_ref[rows, :] * (sig * (1.0 + az * (1.0 - sig)))

            def kv_grad(sums):
                (a0, b0), (a1, b1) = [(s[0:2 * BLK], s[2 * BLK:]) for s in sums]
                return jnp.where(low2, a0, b1) + pltpu.roll(jnp.where(low2, a1, b0), HALF_LANE, 1)

            dk_both, dv_both = kv_grad(dk_sums), kv_grad(dv_sums)
            dak = dk_both[BLK:] + dkc
            dav = dv_both[BLK:] + dvc
            das = [(dlhs[2 * t + hh][:, 0:BLK] * din[t][hh * BLK:(hh + 1) * BLK]).astype(BF16) for t, hh in heads]
            new_gsts = [gsts[t] * cd_ref[t] + drhs[2 * t][BLK:] + drhs[2 * t + 1][BLK:] for t in range(PAIRS)]
            dq_parts = [_dot(jnp.concatenate([das[2 * t], das[2 * t + 1]], axis=1), k_rows[t])
                        + jnp.where(low, dlhs[2 * t][:, BLK:], dlhs[2 * t + 1][:, BLK:]) * _tile(qdec, t)
                        for t in range(PAIRS)]
            dk_parts = [_dot_tn(jnp.concatenate([das[2 * t], das[2 * t + 1]], axis=0), q_rows[t])
                        + jnp.where(low, dkds[2 * t], dkds[2 * t + 1]) * _tile(kdec, t) for t in range(PAIRS)]
            drq = _rotate_transposed(jnp.concatenate(dq_parts, axis=1), cos_b, sin_b)
            drk = _rotate_transposed(jnp.concatenate(dk_parts, axis=1) * RET_SCALE, cos_b, sin_b)

            dp_ref[rows, :] = jnp.concatenate(
                [jnp.concatenate(dq_tiles, axis=1), dak, dav, daz, drq, drk] + dv_parts + dz_parts, axis=1).astype(BF16)
            dgain = dgain + jnp.concatenate(dgain_parts, axis=1)
            return dk_both[0:BLK], dv_both[0:BLK], tuple(new_gsts), dgain, dsink

        carry = (dkc_ref[...], dvc_ref[...], tuple(gst_ref[t] for t in range(PAIRS)), dgain_ref[...], dsink_ref[...])
        for jb in range(nsub):
            carry = sub(jnp.int32(jb), carry)
        dkc, dvc, gsts, dgain, dsink = carry
        dkc_ref[...] = dkc
        dvc_ref[...] = dvc
        gst_ref[...] = jnp.stack(gsts)
        dgain_ref[...] = dgain
        dsink_ref[...] = dsink

        dh = _dot(dp_ref[...], win_ref[...])
        xv = x_ref[...]
        r = lax.rsqrt(jnp.mean(xv * xv, axis=1, keepdims=True) + RMS_EPS)
        xn = xv * r
        gnorm_ref[...] += jnp.sum(dh * xn, axis=0, keepdims=True)
        u = dh * g_ref[...]
        gx_ref[...] = dxo_ref[...] + r * u - xn * (r * jnp.mean(u * xn, axis=1, keepdims=True))

    rev_rows = lambda w: pl.BlockSpec((tb, w), lambda i: (nblk - 1 - i, 0))
    prev_kv = pl.BlockSpec((None, BLK, 2 * ATT_KV_WIDTH),
                           lambda i: (0, jnp.maximum((nblk - 1 - i) * nsub - 1, 0), kv_cols))
    return _call(
        body, name="mix_bwd", grid=(nblk,),
        out_shape=(
            jax.ShapeDtypeStruct((seq, IN_WIDTH), BF16),
            jax.ShapeDtypeStruct((seq, D_MODEL), F32),
            jax.ShapeDtypeStruct((MIX_WIDTH, D_MODEL), F32),
            jax.ShapeDtypeStruct((1, D_MODEL), F32),
            jax.ShapeDtypeStruct((1, RET_WIDTH), F32),
            jax.ShapeDtypeStruct((1, LANE), F32),
        ),
        in_specs=[
            pl.BlockSpec((N_CHIPS, tb, SHARD_PAD), lambda i: (0, nblk - 1 - i, 0)),
            prev_kv, rev_rows(D_MODEL), rev_rows(MIX_WIDTH), rev_rows(ATT_WIDTH),
            pl.BlockSpec((nsub, len(ATT_PROBLEMS), BLK, 2 * BLK), lambda i: (nblk - 1 - i, 0, 0, 0)),
            pl.BlockSpec((nsub, len(ATT_PROBLEMS), 2 * BLK), lambda i: (nblk - 1 - i, 0, 0)),
            pl.BlockSpec((nsub,) + state_shape, lambda i: (nblk - 1 - i, 0, 0, 0)),
            rev_rows(LANE), rev_rows(LANE), rev_rows(D_MODEL),
            _const_spec((MIX_WIDTH, D_MODEL)), _const_spec((IN_WIDTH, D_MODEL)), _const_spec((1, D_MODEL)),
            _const_spec((1, RET_WIDTH)),
            _const_spec((PAIRS, 2 * BLK, BLK)), _const_spec((BLK, RET_QK_WIDTH)), _const_spec((BLK, RET_QK_WIDTH)),
            _const_spec(state_shape),
        ],
        out_specs=(
            rev_rows(IN_WIDTH), rev_rows(D_MODEL), _const_spec((MIX_WIDTH, D_MODEL)), _const_spec((1, D_MODEL)),
            _const_spec((1, RET_WIDTH)), _const_spec((1, LANE)),
        ),
        scratch_shapes=[
            pltpu.VMEM((tb, MIX_WIDTH), F32),
            pltpu.VMEM((tb + BLK, 2 * ATT_KV_WIDTH), F32),
            pltpu.VMEM((BLK, ATT_KV_WIDTH), F32), pltpu.VMEM((BLK, ATT_KV_WIDTH), F32),
            pltpu.VMEM(state_shape, F32),
        ],
        compiler_params=_params(60, ("arbitrary",)),
    )(proj, proj, dxo, mix, o_att, probs, p_sinks, states, cos_t, sin_t, x, w_out, w_in_t, norm_g, gn_gain, decay_in,
      qdec_t, kdec_t, cd_t)


WINDOW_XOR = (3, 2, 1, 0)


def _gw_in_reduce(h_t, dproj, gw_out, small, tb):
    seq = dproj.shape[0]
    nblk = seq // tb
    last = nblk - 1
    hand_on = min(1, last)
    half = D_MODEL // 2
    A, B, C, N_SEMS = 0, N_CHIPS, 2 * N_CHIPS, 2 * N_CHIPS + 1

    def body(win_ref, ht_ref, dp_ref, gwo_hbm, s0_ref, s1_ref, s2_ref, s3_ref, s4_ref, out_ref, fout_ref, packsum_ref,
             acc, sib, to_sib, send_buf, b_in, fin, fout, mine_out, sib_out, send_out, b_out, pack_ref, packs,
             send_sems, recv_sems, local_sems):
        p, i = pl.program_id(0), pl.program_id(1)
        small_start, small_hand_on, small_finish = _small_exchange(
            gwo_hbm, (s0_ref, s1_ref, s2_ref, s3_ref, s4_ref), fout, mine_out, sib_out, send_out, b_out, pack_ref,
            packs, send_sems, recv_sems, local_sems, N_SEMS)

        @pl.when(jnp.logical_and(p == 0, i == 0))
        def _():
            small_start()

        @pl.when(jnp.logical_and(p == 0, i == hand_on))
        def _():
            small_hand_on()

        x, y, c = lax.axis_index("x"), lax.axis_index("y"), lax.axis_index("c")
        chip = 2 * x + y
        sibling = (x, y, 1 - c)
        mine = pl.ds(pl.multiple_of(c * half, half), half)
        other = pl.ds(pl.multiple_of((1 - c) * half, half), half)

        def remote(src, dst, send_k, recv_k, to):
            return pltpu.make_async_remote_copy(src_ref=src, dst_ref=dst, send_sem=send_sems.at[send_k],
                                                recv_sem=recv_sems.at[recv_k], device_id=to, device_id_type=MESH_ID)

        slot = p % 2

        @pl.when(i == 0)
        def _():
            acc[slot] = jnp.zeros(acc.shape[1:], F32)

        acc[slot] += _dot(ht_ref[...], dp_ref[...])

        first_axis = 2 - c
        for q in range(N_CHIPS):
            s = q % 2
            to_sibling = remote(to_sib.at[s], sib.at[s], A + q, A + q, sibling)

            @pl.when(jnp.logical_and(p == q, i == last))
            def _():
                to_sib[s] = acc[s, other, :].astype(BF16)
                to_sibling.start()

            if q < N_CHIPS - 1:
                hop = first_axis if q == 0 else WINDOW_XOR[q]
                into = 0 if q == 0 else WINDOW_XOR[q]
                dest = chip ^ hop

                @pl.when(jnp.logical_and(p == q + 1, i == hand_on))
                def _():
                    to_sibling.wait_recv()
                    if q == 0:
                        send_buf[q] = (acc[s, mine, :] + sib[s].astype(F32)).astype(BF16)
                    else:
                        @pl.when(hop != first_axis)
                        def _():
                            remote(b_in.at[0], b_in.at[0], B, B, sibling).wait_recv()
                            send_buf[q] = (acc[s, mine, :] + sib[s].astype(F32) + b_in[0].astype(F32)).astype(BF16)

                        @pl.when(hop == first_axis)
                        def _():
                            send_buf[q] = (acc[s, mine, :] + sib[s].astype(F32)).astype(BF16)

                    remote(send_buf.at[q], b_in.at[into], B + q, B + into, (dest // 2, dest % 2, c)).start()
                    to_sibling.wait_send()
            else:
                @pl.when(jnp.logical_and(p == q, i == last))
                def _():
                    to_sibling.wait_recv()
                    fin[mine, :] = acc[s, mine, :] + sib[s].astype(F32)
                    for j in (1, 2):
                        remote(b_in.at[j], b_in.at[j], B + j, B + j, sibling).wait_recv()
                        fin[mine, :] += b_in[j].astype(F32)
                    to_core = remote(fin.at[mine, :], fin.at[mine, :], C, C, sibling)
                    to_core.start()
                    remote(fin.at[other, :], fin.at[other, :], C, C, sibling).wait_recv()
                    out_ref[...] = fin[...]
                    to_core.wait_send()
                    to_sibling.wait_send()
                    for k in range(N_CHIPS - 1):
                        remote(send_buf.at[k], b_in.at[k], B + k, B + k, sibling).wait_send()
                    packsum_ref[...] = small_finish()
                    fout_ref[...] = fout[...]

    whole = lambda shape: pl.BlockSpec(shape, lambda p, i, win: (0,) * len(shape), pipeline_mode=pl.Buffered(1))
    grid_spec = pltpu.PrefetchScalarGridSpec(
        num_scalar_prefetch=1, grid=(N_CHIPS, nblk),
        in_specs=[pl.BlockSpec((D_MODEL, tb), lambda p, i, win: (0, i)),
                  pl.BlockSpec((pl.Element(tb), pl.Element(SHARD_PAD)),
                               lambda p, i, win: (i * tb, pl.multiple_of(win[p] * LANE, LANE))),
                  pl.BlockSpec(memory_space=pl.ANY)] + [whole(s.shape) for s in small],
        out_specs=(whole((D_MODEL, SHARD_PAD)), whole((SHARD_OUT, D_MODEL)), whole((PACK_ROWS, D_MODEL))),
        scratch_shapes=[
            pltpu.VMEM((2, D_MODEL, SHARD_PAD), F32), pltpu.VMEM((2, half, SHARD_PAD), BF16),
            pltpu.VMEM((2, half, SHARD_PAD), BF16),
            pltpu.VMEM((N_CHIPS - 1, half, SHARD_PAD), BF16), pltpu.VMEM((N_CHIPS, half, SHARD_PAD), BF16),
            pltpu.VMEM((D_MODEL, SHARD_PAD), F32),
        ] + _small_exchange_scratch() + [
            pltpu.SemaphoreType.DMA((N_SEMS + SMALL_SEMS,)), pltpu.SemaphoreType.DMA((N_SEMS + SMALL_SEMS,)),
            pltpu.SemaphoreType.DMA((N_CHIPS,)),
        ])
    chip = 2 * lax.axis_index("x") + lax.axis_index("y")
    owner = chip ^ jnp.array(WINDOW_XOR, dtype=jnp.int32)
    win_start = (owner * SHARD_IN) // LANE
    return _call(
        body, name="gw_in_reduce", grid_spec=grid_spec,
        out_shape=(jax.ShapeDtypeStruct((D_MODEL, SHARD_PAD), F32), jax.ShapeDtypeStruct((SHARD_OUT, D_MODEL), F32),
                   jax.ShapeDtypeStruct((PACK_ROWS, D_MODEL), F32)),
        compiler_params=_params(52, ("arbitrary", "arbitrary")),
    )(win_start.astype(jnp.int32), h_t, dproj, gw_out, *small)


SMALL_SEMS = 17


def _small_exchange(gwo_hbm, small_refs, fout_ref, mine_out, sib_out, send_out, b_out, pack_ref, packs,
                    send_sems, recv_sems, local_sems, base):
    half_out = SHARD_OUT // 2
    A_OUT, B_OUT, C_OUT, PACK = base, base + 4, base + 8, base + 9
    assert len(small_refs) == PACK_PARTS
    x, y, c = lax.axis_index("x"), lax.axis_index("y"), lax.axis_index("c")
    chip = 2 * x + y
    dev = 2 * chip + c
    sibling = (x, y, 1 - c)

    def remote(src, dst, send_k, recv_k, to):
        return pltpu.make_async_remote_copy(src_ref=src, dst_ref=dst, send_sem=send_sems.at[send_k],
                                            recv_sem=recv_sems.at[recv_k], device_id=to, device_id_type=MESH_ID)

    def out_rows(j, core):
        return pl.ds(pl.multiple_of(j * SHARD_OUT + core * half_out, half_out), half_out)

    my_out_rows = pl.ds(pl.multiple_of(c * half_out, half_out), half_out)
    local = [pltpu.make_async_copy(gwo_hbm.at[out_rows(j, c), :], mine_out.at[j], local_sems.at[j])
             for j in range(N_CHIPS)]
    stage_a = [remote(gwo_hbm.at[out_rows(j, 1 - c), :], sib_out.at[j], A_OUT + j, A_OUT + j, sibling)
               for j in range(N_CHIPS)]
    mine_half_out = fout_ref.at[my_out_rows, :]
    stage_c = [remote(mine_half_out, mine_half_out, C_OUT, C_OUT, sibling)]

    def start():
        pack_ref[...] = jnp.zeros_like(pack_ref)
        for k, s_ref in enumerate(small_refs):
            pack_ref[k:k + 1, 0:s_ref.shape[1]] = s_ref[0:1, :]
        packs[dev] = pack_ref[...]
        for d in range(N_DEV):
            to = (d // 4, (d // 2) % 2, d % 2)

            @pl.when(d != dev)
            def _():
                remote(pack_ref, packs.at[dev], PACK + d, PACK + dev, to).start()

        for cp in local + stage_a:
            cp.start()

    def hand_on():
        for cp in local:
            cp.wait()
        for cp in stage_a:
            cp.wait_recv()
        for j in range(N_CHIPS):
            mine_out[j] = mine_out[j] + sib_out[j]

        for j in range(N_CHIPS):
            to = (j // 2, j % 2, c)

            @pl.when(j != chip)
            def _():
                send_out[j] = mine_out[j].astype(BF16)
                remote(send_out.at[j], b_out.at[chip], B_OUT + j, B_OUT + chip, to).start()

            @pl.when(j == chip)
            def _():
                fout_ref[my_out_rows, :] = mine_out[j]

    def finish():
        for j in range(N_CHIPS):
            @pl.when(j != chip)
            def _():
                remote(b_out.at[j], b_out.at[j], B_OUT + j, B_OUT + j, sibling).wait_recv()
                fout_ref[my_out_rows, :] += b_out[j].astype(F32)

        for cp in stage_c:
            cp.start()
        other_half_out = fout_ref.at[pl.ds(pl.multiple_of((1 - c) * half_out, half_out), half_out), :]
        remote(other_half_out, other_half_out, C_OUT, C_OUT, sibling).wait_recv()

        for d in range(N_DEV):
            @pl.when(d != dev)
            def _():
                remote(pack_ref, packs.at[d], PACK + d, PACK + d, sibling).wait_recv()
        total = packs[0]
        for d in range(1, N_DEV):
            total = total + packs[d]

        for cp in stage_a + stage_c:
            cp.wait_send()
        for j in range(N_CHIPS):
            @pl.when(j != chip)
            def _():
                remote(b_out.at[j], b_out.at[j], B_OUT + j, B_OUT + j, sibling).wait_send()
        for d in range(N_DEV):
            @pl.when(d != dev)
            def _():
                remote(pack_ref, packs.at[d], PACK + d, PACK + d, sibling).wait_send()
        return total

    return start, hand_on, finish


def _small_exchange_scratch():
    half_out = SHARD_OUT // 2
    return [
        pltpu.VMEM((SHARD_OUT, D_MODEL), F32),
        pltpu.VMEM((N_CHIPS, half_out, D_MODEL), F32), pltpu.VMEM((N_CHIPS, half_out, D_MODEL), F32),
        pltpu.VMEM((N_CHIPS, half_out, D_MODEL), BF16), pltpu.VMEM((N_CHIPS, half_out, D_MODEL), BF16),
        pltpu.VMEM((PACK_ROWS, D_MODEL), F32), pltpu.VMEM((N_DEV, PACK_ROWS, D_MODEL), F32),
    ]


def _adam_math(w, g, m, v):
    mn = ADAM_B1 * m + (1.0 - ADAM_B1) * g
    vn = ADAM_B2 * v + (1.0 - ADAM_B2) * (g * g)
    m_hat = mn / (1.0 - ADAM_B1 ** ADAM_STEP)
    v_hat = vn / (1.0 - ADAM_B2 ** ADAM_STEP)
    return -ADAM_LR * (m_hat / (jnp.sqrt(v_hat) + ADAM_EPS) + ADAM_WD * w), mn, vn


def _adamw(name, w, g, m, v, tb):
    rows, cols = w.shape

    def body(w_ref, g_ref, m_ref, v_ref, go_ref, d_ref, mo_ref, vo_ref):
        gv = g_ref[...]
        go_ref[...] = gv
        d_ref[...], mo_ref[...], vo_ref[...] = _adam_math(w_ref[...], gv, m_ref[...], v_ref[...])

    spec = pl.BlockSpec((tb, cols), lambda i: (i, 0))
    shape = jax.ShapeDtypeStruct((rows, cols), F32)
    return _call(
        body, name=name, grid=(rows // tb,), out_shape=(shape,) * 4,
        in_specs=[spec] * 4, out_specs=(spec,) * 4,
        compiler_params=_params(32, ("arbitrary",)),
    )(w, g, m, v)


def _adamw_w_in(w_t, g_window, m_t, v_t, tb):
    def body(w_ref, g_ref, m_ref, v_ref, go_ref, d_ref, mo_ref, vo_ref, gt_ref):
        gt_ref[...] = g_ref[...].T
        gv = gt_ref[pl.ds(pl.multiple_of(lax.axis_index("y") * SHARD_SHIFT, SHARD_SHIFT), SHARD_IN), :]
        go_ref[...] = gv
        d_ref[...], mo_ref[...], vo_ref[...] = _adam_math(w_ref[...], gv, m_ref[...], v_ref[...])

    spec = pl.BlockSpec((SHARD_IN, tb), lambda i: (0, i))
    shape = jax.ShapeDtypeStruct((SHARD_IN, D_MODEL), F32)
    return _call(
        body, name="adamw_w_in", grid=(D_MODEL // tb,), out_shape=(shape,) * 4,
        in_specs=[spec, pl.BlockSpec((tb, SHARD_PAD), lambda i: (i, 0)), spec, spec], out_specs=(spec,) * 4,
        scratch_shapes=[pltpu.VMEM((SHARD_PAD, tb), F32)],
        compiler_params=_params(32, ("arbitrary",)),
    )(w_t, g_window, m_t, v_t)


def _adamw_small(sums, params):
    def body(sums_ref, *refs):
        ins, outs = refs[:3 * len(params)], refs[3 * len(params):]
        for k in range(len(params)):
            w_ref, m_ref, v_ref = ins[3 * k:3 * k + 3]
            g = sums_ref[k:k + 1, 0:w_ref.shape[1]]
            go_ref, d_ref, mo_ref, vo_ref = outs[4 * k:4 * k + 4]
            go_ref[...] = g
            d_ref[...], mo_ref[...], vo_ref[...] = _adam_math(w_ref[...], g, m_ref[...], v_ref[...])

    vmem = pl.BlockSpec(memory_space=pltpu.VMEM)
    flat = [a for p in params for a in p]
    shapes = tuple(jax.ShapeDtypeStruct(p[0].shape, F32) for p in params for _ in range(4))
    res = _call(body, name="adamw_small", out_shape=shapes, in_specs=[vmem] * (1 + len(flat)),
                out_specs=(vmem,) * len(shapes), compiler_params=_params(16))(sums, *flat)
    return [res[4 * k:4 * k + 4] for k in range(len(params))]


def kernel(x, norm_g, w_in, att_sinks, ret_gn_g, w_out, final_g, loss_target, m_norm_g, m_w_in, m_att_sinks, m_ret_gn_g, m_w_out, m_final_g, v_norm_g, v_w_in, v_att_sinks, v_ret_gn_g, v_w_out, v_final_g):
    seq = x.shape[1]
    xs, tgt = x[0], loss_target[0]
    final_g2 = final_g.reshape(1, D_MODEL)
    tables = _tables(seq)

    w_in_t, m_w_in_t, v_w_in_t = w_in[0].T, m_w_in[0].T, v_w_in[0].T
    proj, h_t, w_in_full, w_out_full = _in_proj(xs, norm_g, w_in_t, w_out[0], min(TOKENS_PROJ, seq))
    mix, dxo, states, loss_part, gfin, o_att, probs, p_sinks = _mix_fwd(
        proj, xs, tgt, w_out_full, final_g2, ret_gn_g, att_sinks, tables, min(TOKENS_MIX, seq))
    dproj, grad_x, gw_out, gnorm, dgain, dsink = _mix_bwd(
        proj, dxo, mix, o_att, probs, p_sinks, states, xs, w_out_full, w_in_full, norm_g, ret_gn_g, tables,
        min(TOKENS_MIX, seq))
    g_in, g_out, sums = _gw_in_reduce(h_t, dproj, gw_out, (gnorm, gfin, dgain, dsink, loss_part),
                                      min(TOKENS_GW, seq))

    res_in = [r.T for r in _adamw_w_in(w_in_t, g_in, m_w_in_t, v_w_in_t, 256)]
    res_out = _adamw("adamw_w_out", w_out[0], g_out, m_w_out[0], v_w_out[0], SHARD_OUT)
    as_row = lambda a: a.reshape(1, D_MODEL)
    r_norm, r_final, r_gain, r_sink = _adamw_small(sums, [
        (norm_g, m_norm_g, v_norm_g), (final_g2, as_row(m_final_g), as_row(v_final_g)),
        (ret_gn_g, m_ret_gn_g, v_ret_gn_g), (att_sinks, m_att_sinks, v_att_sinks)])

    outs = []
    for k in range(4):
        outs += [r_norm[k], res_in[k][None], r_sink[k], r_gain[k], res_out[k][None], r_final[k].reshape(D_MODEL)]
    return (sums[4, 0], grad_x[None], *outs)
```

```python
import jax
import jax.numpy as jnp
import numpy as np
from jax import lax
from jax.experimental import pallas as pl
from jax.experimental.pallas import tpu as pltpu

F32 = jnp.float32
BF16 = jnp.bfloat16

D_MODEL = 1024
ATT_HEADS = 8
ATT_KV_HEADS = 2
ATT_HEAD_DIM = 64
RET_HEADS = 4
RET_QK_DIM = 64
RET_V_DIM = 128
BLK = 128
ROT_BASE = 10000.0
RMS_EPS = 1e-6
GN_EPS = 1e-6
NEG_INF = -1e30
ATT_SCALE = ATT_HEAD_DIM ** -0.5
RET_SCALE = RET_QK_DIM ** -0.5

ATT_WIDTH = ATT_HEADS * ATT_HEAD_DIM
ATT_KV_WIDTH = ATT_KV_HEADS * ATT_HEAD_DIM
RET_QK_WIDTH = RET_HEADS * RET_QK_DIM
RET_WIDTH = RET_HEADS * RET_V_DIM
MIX_WIDTH = ATT_WIDTH + RET_WIDTH
OFF_AQ = 0
OFF_AK = OFF_AQ + ATT_WIDTH
OFF_AV = OFF_AK + ATT_KV_WIDTH
OFF_AZ = OFF_AV + ATT_KV_WIDTH
OFF_RQ = OFF_AZ + ATT_WIDTH
OFF_RK = OFF_RQ + RET_QK_WIDTH
OFF_RV = OFF_RK + RET_QK_WIDTH
OFF_RZ = OFF_RV + RET_WIDTH
IN_WIDTH = OFF_RZ + RET_WIDTH

LANE = 128
BF16_ROWS = 16
HALF_LANE = LANE // 2
PAIRS = RET_QK_WIDTH // LANE
assert ATT_HEAD_DIM == HALF_LANE and RET_QK_DIM == HALF_LANE and RET_V_DIM == LANE and ATT_KV_WIDTH == LANE

N_CHIPS = 4
N_DEV = 8
SHARD_IN = IN_WIDTH // N_CHIPS
SHARD_PAD = 768
SHARD_SHIFT = SHARD_PAD - SHARD_IN
WIN_START = tuple((j * SHARD_IN) // LANE * LANE for j in range(N_CHIPS))
SHARD_OUT = MIX_WIDTH // N_CHIPS

PACK_PARTS = 5
PACK_ROWS = 8
assert PACK_PARTS <= PACK_ROWS

ADAM_LR = 0.001
ADAM_B1 = 0.9
ADAM_B2 = 0.999
ADAM_EPS = 1e-08
ADAM_WD = 0.01
ADAM_STEP = 10

VMEM_CAP = 64 * 1024 * 1024
TOKENS_PROJ = 1024
TOKENS_MIX = 512
TOKENS_GW = 2048
MESH_ID = pl.DeviceIdType.MESH


def _call(body, **kw):
    return pl.pallas_call(body, **kw)


def _params(vmem_mb, semantics=None):
    assert vmem_mb * 1024 * 1024 < VMEM_CAP
    return pltpu.CompilerParams(dimension_semantics=semantics, vmem_limit_bytes=vmem_mb * 1024 * 1024)


def _dot(a, b):
    return jnp.dot(a, b, preferred_element_type=F32)


def _dot_nt(a, b):
    return lax.dot_general(a, b, (((1,), (1,)), ((), ())), preferred_element_type=F32)


def _dot_tn(a, b):
    return lax.dot_general(a, b, (((0,), (0,)), ((), ())), preferred_element_type=F32)


def _sigmoid(z):
    return 1.0 / (1.0 + jnp.exp(-z))


def _const_spec(shape):
    nd = len(shape)
    return pl.BlockSpec(shape, lambda i: (0,) * nd, pipeline_mode=pl.Buffered(1))


def _tables(seq):
    f32 = np.float32
    pos = np.arange(seq, dtype=f32)
    theta = (f32(1.0) / (f32(ROT_BASE) ** np.linspace(0.0, 1.0, RET_QK_DIM // 2, dtype=f32))).astype(f32)
    ang = (pos[:, None] * theta[None, :]).astype(f32)
    cos, sin = np.cos(ang), np.sin(ang)
    cos2 = np.repeat(cos, 2, axis=1)
    sin2 = np.stack([-sin, sin], axis=-1).reshape(seq, RET_QK_DIM)
    cos_t = np.tile(cos2, (1, 2))
    sin_t = np.tile(sin2, (1, 2))

    log_gamma = np.log(f32(1.0) - f32(2.0) ** (f32(-5.0) - np.arange(RET_HEADS, dtype=f32))).astype(f32)
    idx = np.arange(BLK, dtype=f32)
    rel = idx[:, None] - idx[None, :]
    decay_in = np.where(rel >= 0, np.exp(log_gamma[:, None, None] * np.maximum(rel, f32(0.0))), f32(0.0))
    k_dec = np.exp(log_gamma[:, None] * (BLK - 1 - idx)[None, :])
    q_dec = np.exp(log_gamma[:, None] * (idx + 1)[None, :])
    chunk_decay = np.exp(log_gamma * f32(BLK))
    kdec_t = np.repeat(k_dec.T, RET_QK_DIM, axis=1)
    qdec_t = np.repeat(q_dec.T, RET_QK_DIM, axis=1)
    cd_t = np.broadcast_to(chunk_decay[:, None, None], (RET_HEADS, RET_QK_DIM, RET_V_DIM))
    decay_in = decay_in.reshape(PAIRS, 2 * BLK, BLK)
    cd_t = cd_t.reshape(PAIRS, 2 * RET_QK_DIM, RET_V_DIM)

    key = np.arange(BLK)[:, None]
    query = np.arange(2 * BLK)[None, :] % BLK
    bias = np.stack([np.where(key > query, NEG_INF, 0.0), np.zeros((BLK, 2 * BLK))])
    return tuple(jnp.asarray(np.ascontiguousarray(a), F32) for a in (cos_t, sin_t, decay_in, qdec_t, kdec_t, cd_t, bias))


def _low_lanes(shape):
    lane = lax.broadcasted_iota(jnp.int32, shape, len(shape) - 1)
    return (lane & HALF_LANE) == 0


def _split_heads(t):
    low = _low_lanes(t.shape)
    zero = jnp.zeros_like(t)
    return jnp.where(low, t, zero), jnp.where(low, zero, t)


def _swap_pairs(t):
    lane = lax.broadcasted_iota(jnp.int32, t.shape, 1)
    nxt = pltpu.roll(t, t.shape[1] - 1, 1)
    prv = pltpu.roll(t, 1, 1)
    return jnp.where((lane & 1) == 0, nxt, prv)


def _per_tile(fn, t):
    return jnp.concatenate([fn(_tile(t, i)) for i in range(t.shape[1] // LANE)], axis=1)


def _rotate(t, cos_t, sin_t):
    return _per_tile(lambda a: a * cos_t + _swap_pairs(a) * sin_t, t)


def _rotate_transposed(d, cos_t, sin_t):
    return _per_tile(lambda a: a * cos_t + _swap_pairs(a * sin_t), d)


def _kv_operands(cat):
    low = _low_lanes(cat.shape)
    swapped = pltpu.roll(cat, HALF_LANE, 1)
    zero = jnp.zeros_like(cat)
    pick = lambda a, b: jnp.where(low, a, b).astype(BF16)
    return ((pick(cat, zero), pick(zero, swapped)), (pick(swapped, zero), pick(zero, cat)))


def _stack_tiles(t, first_tile):
    a = t[:, first_tile * LANE:(first_tile + 1) * LANE]
    b = t[:, (first_tile + 1) * LANE:(first_tile + 2) * LANE]
    return jnp.concatenate([a, b], axis=0)


def _sink_rows(sinks_ref, group):
    first = lax.broadcasted_iota(jnp.int32, (1, 2 * BLK), 1) < BLK

    def row(h0, h1):
        return jnp.where(first, sinks_ref[0, group * 4 + h0], sinks_ref[0, group * 4 + h1])
    return row(0, 2), row(1, 3)


ATT_PROBLEMS = tuple((g, hi) for g in range(ATT_KV_HEADS) for hi in range(2))


def _in_previous_block():
    key = lax.broadcasted_iota(jnp.int32, (BLK, 2 * BLK), 0)
    query = lax.broadcasted_iota(jnp.int32, (BLK, 2 * BLK), 1) & (BLK - 1)
    return key > query


def _fold(t, prev):
    return jnp.where(prev, t[0:BLK], t[BLK:])


def _unfold(t, prev):
    zero = jnp.zeros_like(t)
    return jnp.concatenate([jnp.where(prev, t, zero), jnp.where(prev, zero, t)], axis=0)


def _attn_scores(qs, k_ops, bias, prev):
    return [_fold(_dot_nt(k_ops[g][hi], qs[g]), prev) + bias for g, hi in ATT_PROBLEMS]


def _attn_softmax(s, sinks_ref):
    sink = [_sink_rows(sinks_ref, g)[hi] for g, hi in ATT_PROBLEMS]
    m = [jnp.maximum(jnp.max(si, axis=0, keepdims=True), ki) for si, ki in zip(s, sink)]
    e = [jnp.exp(si - mi) for si, mi in zip(s, m)]
    es = [jnp.exp(ki - mi) for ki, mi in zip(sink, m)]
    inv = [1.0 / (jnp.sum(ei, axis=0, keepdims=True) + esi) for ei, esi in zip(e, es)]
    return [ei * ii for ei, ii in zip(e, inv)], [esi * ii for esi, ii in zip(es, inv)]


def _group_norm_all(outs):
    mu = [jnp.mean(o, axis=1, keepdims=True) for o in outs]
    xc = [o - m for o, m in zip(outs, mu)]
    var = [jnp.mean(c * c, axis=1, keepdims=True) for c in xc]
    rstd = [lax.rsqrt(v + GN_EPS) for v in var]
    return [c * r for c, r in zip(xc, rstd)], rstd


def _proj_cols(p_ref, rows, start, stop):
    pieces = []
    while start < stop:
        k = max(j for j in range(N_CHIPS) if WIN_START[j] <= start)
        end = min(stop, WIN_START[k] + SHARD_PAD)
        pieces.append(p_ref[k, rows, start - WIN_START[k]:end - WIN_START[k]])
        start = end
    return pieces[0] if len(pieces) == 1 else jnp.concatenate(pieces, axis=1)


def _retention_operands(p_ref, rows, cos_b, sin_b, qdec, kdec):
    qr = _rotate(_proj_cols(p_ref, rows, OFF_RQ, OFF_RQ + RET_QK_WIDTH).astype(F32), cos_b, sin_b)
    kr = _rotate(_proj_cols(p_ref, rows, OFF_RK, OFF_RK + RET_QK_WIDTH).astype(F32), cos_b, sin_b) * RET_SCALE
    return qr, kr, qr * qdec, kr * kdec


def _tile(t, i):
    return t[:, i * LANE:(i + 1) * LANE]


GATHER_SEMS = 7


def _allgather_steps(src_ref, full_ref, blk_ref, send_sems, recv_sems):
    block_rows = blk_ref.shape[0]
    assert block_rows % BF16_ROWS == 0 and full_ref.shape[0] == N_DEV * block_rows
    x, y, c = lax.axis_index("x"), lax.axis_index("y"), lax.axis_index("c")
    me, sibling = (x, y, c), (x, y, 1 - c)
    chips = [(1 - x, y), (x, 1 - y), (1 - x, 1 - y)]

    def rows(px, py, pc):
        return full_ref.at[pl.ds(pl.multiple_of((4 * px + 2 * py + pc) * block_rows, BF16_ROWS), block_rows), :]

    def copy(k, block, to, src=None):
        return pltpu.make_async_remote_copy(
            src_ref=rows(*block) if src is None else src, dst_ref=rows(*block),
            send_sem=send_sems.at[k], recv_sem=recv_sems.at[k], device_id=to, device_id_type=MESH_ID)

    first = [copy(0, me, sibling, src=blk_ref)] + [copy(1 + j, me, (*chip, c), src=blk_ref) for j, chip in enumerate(chips)]
    passed = [copy(4 + j, (*chip, c), sibling) for j, chip in enumerate(chips)]

    def start():
        blk_ref[...] = src_ref[pl.ds(pl.multiple_of(c * block_rows, BF16_ROWS), block_rows), :].astype(BF16)
        rows(*me)[...] = blk_ref[...]
        for cp in first:
            cp.start()

    def forward():
        for j, chip in enumerate(chips):
            copy(1 + j, (*chip, c), me).wait_recv()
            passed[j].start()

    def finish():
        copy(0, sibling, me).wait_recv()
        for j, chip in enumerate(chips):
            copy(4 + j, (*chip, 1 - c), me).wait_recv()
        for cp in first + passed:
            cp.wait_send()

    return start, forward, finish


WINDOW_HALF = SHARD_PAD // 2
WINDOW_SEMS = GATHER_SEMS + 1


def _window_gather(wi_ref, w4_ref, blk_ref, edge_ref, send_sems, recv_sems):
    x, y, c = lax.axis_index("x"), lax.axis_index("y"), lax.axis_index("c")
    me, sibling, pair = (x, y, c), (x, y, 1 - c), (x, 1 - y, c)
    chips = [(1 - x, y), (x, 1 - y), (1 - x, 1 - y)]
    own_rows, edge_rows = SHARD_IN - WINDOW_HALF, SHARD_SHIFT
    sends_edge = y == c

    def rows(px, py, pc):
        return w4_ref.at[pl.ds(pl.multiple_of((4 * px + 2 * py + pc) * WINDOW_HALF, BF16_ROWS), WINDOW_HALF), :]

    def copy(k, block, to, src=None):
        return pltpu.make_async_remote_copy(
            src_ref=rows(*block) if src is None else src, dst_ref=rows(*block),
            send_sem=send_sems.at[k], recv_sem=recv_sems.at[k], device_id=to, device_id_type=MESH_ID)

    edge_copy = pltpu.make_async_remote_copy(
        src_ref=edge_ref, dst_ref=edge_ref, send_sem=send_sems.at[GATHER_SEMS], recv_sem=recv_sems.at[GATHER_SEMS],
        device_id=pair, device_id_type=MESH_ID)
    first = [copy(0, me, sibling, src=blk_ref)] + [copy(1 + j, me, (*chip, c), src=blk_ref) for j, chip in enumerate(chips)]
    passed = [copy(4 + j, (*chip, c), sibling) for j, chip in enumerate(chips)]

    def at(start, size):
        return pl.ds(pl.multiple_of(start, BF16_ROWS), size)

    def start():
        @pl.when(sends_edge)
        def _():
            edge_ref[...] = wi_ref[at((1 - y) * (SHARD_IN - edge_rows), edge_rows), :].astype(BF16)
            edge_copy.start()
            blk_ref[...] = wi_ref[at(c * own_rows, WINDOW_HALF), :].astype(BF16)

        @pl.when(jnp.logical_not(sends_edge))
        def _():
            blk_ref[at(y * edge_rows, own_rows), :] = wi_ref[at(c * WINDOW_HALF, own_rows), :].astype(BF16)
            edge_copy.wait_recv()
            blk_ref[at((1 - y) * own_rows, edge_rows), :] = edge_ref[...]

        rows(*me)[...] = blk_ref[...]
        for cp in first[:-1]:
            cp.start()
        copy(0, sibling, me).wait_recv()

    def send_far():
        first[-1].start()

    def forward(j):
        copy(1 + j, (*chips[j], c), me).wait_recv()
        passed[j].start()

    def wait_forwarded(j):
        copy(4 + j, (*chips[j], 1 - c), me).wait_recv()

    def finish():
        for cp in first + passed:
            cp.wait_send()

        @pl.when(sends_edge)
        def _():
            edge_copy.wait_send()

    return start, send_far, forward, wait_forwarded, finish


def _in_proj(x, norm_g, w_in_t_shard, w_out_shard, tb):
    seq = x.shape[0]
    nblk = seq // tb
    last = nblk - 1
    chip_of_panel = (None, 1, 0, 2)

    def body(win_ref, x_ref, g_ref, wi_ref, wo_ref, p_ref, ht_ref, wt_ref, wout_ref,
             w4, blk, edge, hbuf, wout_full, blko, send_sems, recv_sems, send_sems_o, recv_sems_o):
        q, i = pl.program_id(0), pl.program_id(1)
        chip = 2 * lax.axis_index("x") + lax.axis_index("y")
        in_start, in_send_far, in_forward, in_wait_forwarded, in_finish = _window_gather(
            wi_ref, w4, blk, edge, send_sems, recv_sems)
        out_start, out_forward, out_finish = _allgather_steps(wo_ref, wout_full, blko, send_sems_o, recv_sems_o)
        rows = pl.ds(pl.multiple_of(i * tb, tb), tb)

        @pl.when(jnp.logical_and(q == 0, i == 0))
        def _():
            in_start()

        @pl.when(jnp.logical_and(q == 0, i == min(last // 2 + 1, last)))
        def _():
            in_send_far()

        @pl.when(jnp.logical_and(q == 1, i == 0))
        def _():
            out_start()

        for panel in range(1, N_CHIPS):
            @pl.when(jnp.logical_and(q == panel, i == 0))
            def _():
                in_wait_forwarded(chip_of_panel[panel])

        @pl.when(q == 0)
        def _():
            xv = x_ref[...]
            r = lax.rsqrt(jnp.mean(xv * xv, axis=1, keepdims=True) + RMS_EPS)
            h = (xv * r) * g_ref[...]
            hbuf[rows, :] = h.astype(BF16)
            ht_ref[...] = h.T.astype(BF16)

        owner = jnp.bitwise_xor(chip, q)
        window = w4[pl.ds(pl.multiple_of(owner * SHARD_PAD, SHARD_PAD), SHARD_PAD), :]
        p_ref[...] = _dot_nt(hbuf[rows, :], window).astype(BF16)

        for panel in range(1, N_CHIPS):
            @pl.when(jnp.logical_and(q == panel - 1, i == last))
            def _():
                in_forward(chip_of_panel[panel])

        @pl.when(jnp.logical_and(q == N_CHIPS - 1, i == 0))
        def _():
            out_forward()

        @pl.when(jnp.logical_and(q == N_CHIPS - 1, i == last))
        def _():
            in_finish()
            for k in range(N_CHIPS):
                first = k * SHARD_PAD + (k % 2) * SHARD_SHIFT
                wt_ref[k * SHARD_IN:(k + 1) * SHARD_IN, :] = w4[first:first + SHARD_IN, :]
            out_finish()
            wout_ref[...] = wout_full[...]

    whole = lambda shape: pl.BlockSpec(shape, lambda q, i, win: (0,) * len(shape), pipeline_mode=pl.Buffered(1))
    first_panel = lambda q, i: jnp.where(q == 0, i, last)
    grid_spec = pltpu.PrefetchScalarGridSpec(
        num_scalar_prefetch=1, grid=(N_CHIPS, nblk),
        in_specs=[pl.BlockSpec((tb, D_MODEL), lambda q, i, win: (first_panel(q, i), 0)), whole((1, D_MODEL)),
                  whole((SHARD_IN, D_MODEL)), whole((SHARD_OUT, D_MODEL))],
        out_specs=(pl.BlockSpec((None, tb, SHARD_PAD), lambda q, i, win: (win[q], i, 0)),
                   pl.BlockSpec((D_MODEL, tb), lambda q, i, win: (0, first_panel(q, i))),
                   whole((IN_WIDTH, D_MODEL)), whole((MIX_WIDTH, D_MODEL))),
        scratch_shapes=[
            pltpu.VMEM((N_CHIPS * SHARD_PAD, D_MODEL), BF16), pltpu.VMEM((WINDOW_HALF, D_MODEL), BF16),
            pltpu.VMEM((SHARD_SHIFT, D_MODEL), BF16), pltpu.VMEM((seq, D_MODEL), BF16),
            pltpu.VMEM((MIX_WIDTH, D_MODEL), BF16), pltpu.VMEM((SHARD_OUT // 2, D_MODEL), BF16),
            pltpu.SemaphoreType.DMA((WINDOW_SEMS,)), pltpu.SemaphoreType.DMA((WINDOW_SEMS,)),
            pltpu.SemaphoreType.DMA((GATHER_SEMS,)), pltpu.SemaphoreType.DMA((GATHER_SEMS,)),
        ])
    chip = 2 * lax.axis_index("x") + lax.axis_index("y")
    owner = jnp.bitwise_xor(chip, jnp.arange(N_CHIPS, dtype=jnp.int32))
    return _call(
        body, name="in_proj", grid_spec=grid_spec,
        out_shape=(jax.ShapeDtypeStruct((N_CHIPS, seq, SHARD_PAD), BF16), jax.ShapeDtypeStruct((D_MODEL, seq), BF16),
                   jax.ShapeDtypeStruct((IN_WIDTH, D_MODEL), BF16), jax.ShapeDtypeStruct((MIX_WIDTH, D_MODEL), BF16)),
        compiler_params=_params(60, ("arbitrary", "arbitrary")),
    )(owner.astype(jnp.int32), x, norm_g, w_in_t_shard, w_out_shard)


def _mix_fwd(proj, x, target, w_out, final_g, gn_gain, sinks, tables, tb):
    seq = x.shape[0]
    nsub = tb // BLK
    cos_t, sin_t, decay_in, qdec_t, kdec_t, cd_t, bias_t = tables

    def body(p_ref, x_ref, t_ref, cos_ref, sin_ref, wout_ref, fg_ref, gain_ref, sinks_ref, din_ref, qdec_ref,
             kdec_ref, cd_ref, bias_ref, mix_ref, dxo_ref, st_ref, loss_ref, gfin_ref, oatt_ref, probs_ref, psink_ref,
             kprev_ref, vprev_ref, state_ref):
        i = pl.program_id(0)

        @pl.when(i == 0)
        def _():
            kprev_ref[...] = jnp.zeros_like(kprev_ref)
            vprev_ref[...] = jnp.zeros_like(vprev_ref)
            state_ref[...] = jnp.zeros_like(state_ref)
            loss_ref[...] = jnp.zeros_like(loss_ref)
            gfin_ref[...] = jnp.zeros_like(gfin_ref)

        prev = _in_previous_block()

        def sub(j, carry):
            kp, vp, states = carry
            rows = pl.ds(pl.multiple_of(j * BLK, BLK), BLK)
            bias = bias_ref[jnp.where(jnp.logical_or(i > 0, j > 0), 1, 0)]

            aq = _proj_cols(p_ref, rows, OFF_AQ, OFF_AQ + ATT_WIDTH)
            ak = _proj_cols(p_ref, rows, OFF_AK, OFF_AK + ATT_KV_WIDTH).astype(F32)
            av = _proj_cols(p_ref, rows, OFF_AV, OFF_AV + ATT_KV_WIDTH).astype(F32)
            az = _proj_cols(p_ref, rows, OFF_AZ, OFF_AZ + ATT_WIDTH).astype(F32)
            k_ops = _kv_operands(jnp.concatenate([kp, ak], axis=0))
            v_ops = _kv_operands(jnp.concatenate([vp, av], axis=0))
            qs = [(_stack_tiles(aq, 2 * g) * ATT_SCALE).astype(BF16) for g in range(ATT_KV_HEADS)]

            qr, kr, qd, kd = _retention_operands(p_ref, rows, cos_ref[rows, :], sin_ref[rows, :],
                                                 qdec_ref[...], kdec_ref[...])
            s = _attn_scores(qs, k_ops, bias, prev)
            heads = [(t, hh) for t in range(PAIRS) for hh in range(2)]
            sc = [_dot_nt(jnp.concatenate(_split_heads(_tile(qr, t)), axis=0).astype(BF16), _tile(kr, t).astype(BF16))
                  * din_ref[t] for t in range(PAIRS)]
            qd_heads = [_split_heads(_tile(qd, t)) for t in range(PAIRS)]
            state_b = [states[t].astype(BF16) for t in range(PAIRS)]
            vs = [_proj_cols(p_ref, rows, OFF_RV + h * RET_V_DIM, OFF_RV + (h + 1) * RET_V_DIM) for h in range(RET_HEADS)]
            rzs = [_proj_cols(p_ref, rows, OFF_RZ + h * RET_V_DIM, OFF_RZ + (h + 1) * RET_V_DIM).astype(F32)
                   for h in range(RET_HEADS)]
            lhs = [jnp.concatenate([sc[t][hh * BLK:(hh + 1) * BLK].astype(BF16), qd_heads[t][hh].astype(BF16)], axis=1)
                   for t, hh in heads]
            p, p_sink = _attn_softmax(s, sinks_ref)
            o_ret = [_dot(lhs[2 * t + hh], jnp.concatenate([vs[2 * t + hh], state_b[t]], axis=0)) for t, hh in heads]
            p = [pi.astype(BF16) for pi in p]
            o_tiles = []
            for g in range(ATT_KV_HEADS):
                p_cat = jnp.concatenate([_unfold(p[2 * g], prev), _unfold(p[2 * g + 1], prev)], axis=0)
                o = _dot_tn(p_cat, jnp.concatenate(v_ops[g], axis=0))
                o_tiles += [o[0:BLK], o[BLK:]]
            ons, _ = _group_norm_all(o_ret)
            new_states = [states[t] * cd_ref[t]
                          + _dot_tn(jnp.concatenate(_split_heads(_tile(kd, t)), axis=0).astype(BF16),
                                    jnp.concatenate([vs[2 * t], vs[2 * t + 1]], axis=0)) for t in range(PAIRS)]
            o_att = jnp.concatenate(o_tiles, axis=1)
            out = [o_att * (az * _sigmoid(az))]
            out += [(ons[h] * gain_ref[:, h * RET_V_DIM:(h + 1) * RET_V_DIM]) * (rzs[h] * _sigmoid(rzs[h]))
                    for h in range(RET_HEADS)]
            mix_ref[rows, :] = jnp.concatenate(out, axis=1).astype(BF16)
            oatt_ref[rows, :] = o_att
            probs_ref[j] = jnp.stack(p)
            psink_ref[j] = jnp.concatenate(p_sink, axis=0)
            st_ref[j] = jnp.stack(states)
            return ak, av, tuple(new_states)

        carry = (kprev_ref[...], vprev_ref[...], tuple(state_ref[t] for t in range(PAIRS)))
        for jb in range(nsub):
            carry = sub(jnp.int32(jb), carry)
        kp, vp, states = carry
        kprev_ref[...] = kp
        vprev_ref[...] = vp
        state_ref[...] = jnp.stack(states)

        xo = x_ref[...] + _dot(mix_ref[...], wout_ref[...])
        r2 = lax.rsqrt(jnp.mean(xo * xo, axis=1, keepdims=True) + RMS_EPS)
        xn = xo * r2
        err = xn * fg_ref[...] - t_ref[...]
        loss_ref[...] += jnp.sum(err * err) * (0.5 / D_MODEL)
        dy = err * (1.0 / D_MODEL)
        gfin_ref[...] += jnp.sum(dy * xn, axis=0, keepdims=True)
        u = dy * fg_ref[...]
        dxo_ref[...] = r2 * u - xn * (r2 * jnp.mean(u * xn, axis=1, keepdims=True))

    blk_rows = lambda w: pl.BlockSpec((tb, w), lambda i: (i, 0))
    state_shape = (PAIRS, 2 * RET_QK_DIM, RET_V_DIM)
    return _call(
        body, name="mix_fwd", grid=(seq // tb,),
        out_shape=(
            jax.ShapeDtypeStruct((seq, MIX_WIDTH), BF16),
            jax.ShapeDtypeStruct((seq, D_MODEL), F32),
            jax.ShapeDtypeStruct((seq // BLK,) + state_shape, F32),
            jax.ShapeDtypeStruct((8, LANE), F32),
            jax.ShapeDtypeStruct((1, D_MODEL), F32),
            jax.ShapeDtypeStruct((seq, ATT_WIDTH), F32),
            jax.ShapeDtypeStruct((seq // BLK, len(ATT_PROBLEMS), BLK, 2 * BLK), BF16),
            jax.ShapeDtypeStruct((seq // BLK, len(ATT_PROBLEMS), 2 * BLK), F32),
        ),
        in_specs=[
            pl.BlockSpec((N_CHIPS, tb, SHARD_PAD), lambda i: (0, i, 0)),
            blk_rows(D_MODEL), blk_rows(D_MODEL), blk_rows(LANE), blk_rows(LANE),
            _const_spec((MIX_WIDTH, D_MODEL)), _const_spec((1, D_MODEL)), _const_spec((1, RET_WIDTH)),
            pl.BlockSpec(memory_space=pltpu.SMEM),
            _const_spec((PAIRS, 2 * BLK, BLK)), _const_spec((BLK, RET_QK_WIDTH)), _const_spec((BLK, RET_QK_WIDTH)),
            _const_spec(state_shape), _const_spec((2, BLK, 2 * BLK)),
        ],
        out_specs=(
            blk_rows(MIX_WIDTH), blk_rows(D_MODEL),
            pl.BlockSpec((nsub,) + state_shape, lambda i: (i, 0, 0, 0)),
            _const_spec((8, LANE)), _const_spec((1, D_MODEL)), blk_rows(ATT_WIDTH),
            pl.BlockSpec((nsub, len(ATT_PROBLEMS), BLK, 2 * BLK), lambda i: (i, 0, 0, 0)),
            pl.BlockSpec((nsub, len(ATT_PROBLEMS), 2 * BLK), lambda i: (i, 0, 0)),
        ),
        scratch_shapes=[
            pltpu.VMEM((BLK, ATT_KV_WIDTH), F32), pltpu.VMEM((BLK, ATT_KV_WIDTH), F32),
            pltpu.VMEM(state_shape, F32),
        ],
        compiler_params=_params(48, ("arbitrary",)),
    )(proj, x, target, cos_t, sin_t, w_out, final_g, gn_gain, sinks, decay_in, qdec_t, kdec_t, cd_t, bias_t)


def _mix_bwd(proj, dxo, mix, o_att, probs, p_sinks, states, x, w_out, w_in_t, norm_g, gn_gain, tables, tb):
    seq = dxo.shape[0]
    nsub = tb // BLK
    nblk = seq // tb
    cos_t, sin_t, decay_in, qdec_t, kdec_t, cd_t, _ = tables
    kv_cols = OFF_AK // (2 * ATT_KV_WIDTH)
    state_shape = (PAIRS, 2 * RET_QK_DIM, RET_V_DIM)

    def body(p_ref, pkv_ref, dxo_ref, mix_ref, oatt_ref, probs_ref, psink_ref, st_ref, cos_ref, sin_ref, x_ref, wout_ref,
             win_ref, g_ref, gain_ref, din_ref, qdec_ref, kdec_ref, cd_ref,
             dp_ref, gx_ref, gwout_ref, gnorm_ref, dgain_ref, dsink_ref,
             dmix_ref, kv_ref, dkc_ref, dvc_ref, gst_ref):
        i = pl.program_id(0)

        @pl.when(i == 0)
        def _():
            gwout_ref[...] = jnp.zeros_like(gwout_ref)
            gnorm_ref[...] = jnp.zeros_like(gnorm_ref)
            dgain_ref[...] = jnp.zeros_like(dgain_ref)
            dsink_ref[...] = jnp.zeros_like(dsink_ref)
            dkc_ref[...] = jnp.zeros_like(dkc_ref)
            dvc_ref[...] = jnp.zeros_like(dvc_ref)
            gst_ref[...] = jnp.zeros_like(gst_ref)

        dxo_b = dxo_ref[...].astype(BF16)
        dmix_ref[...] = _dot_nt(dxo_b, wout_ref[...])
        gwout_ref[...] += _dot_tn(mix_ref[...], dxo_b)
        kv_ref[0:BLK, :] = pkv_ref[...].astype(F32)
        kv_ref[BLK:, :] = _proj_cols(p_ref, slice(None), OFF_AK, OFF_AK + 2 * ATT_KV_WIDTH).astype(F32)
        low = _low_lanes((BLK, LANE))
        low2 = _low_lanes((2 * BLK, LANE))
        lane = lax.broadcasted_iota(jnp.int32, (1, LANE), 1)
        prev = _in_previous_block()

        def sub(jj, carry):
            dkc, dvc, gsts, dgain, dsink = carry
            j = nsub - 1 - jj
            rows = pl.ds(pl.multiple_of(j * BLK, BLK), BLK)
            both = pl.ds(pl.multiple_of(j * BLK, BLK), 2 * BLK)


            aq = _proj_cols(p_ref, rows, OFF_AQ, OFF_AQ + ATT_WIDTH)
            az = _proj_cols(p_ref, rows, OFF_AZ, OFF_AZ + ATT_WIDTH).astype(F32)
            k_ops = _kv_operands(kv_ref[both, 0:ATT_KV_WIDTH])
            v_ops = _kv_operands(kv_ref[both, ATT_KV_WIDTH:2 * ATT_KV_WIDTH])
            da = dmix_ref[rows, 0:ATT_WIDTH]
            sig = _sigmoid(az)
            d_o = da * (az * sig)
            qs = [(_stack_tiles(aq, 2 * g) * ATT_SCALE).astype(BF16) for g in range(ATT_KV_HEADS)]
            dos = [_stack_tiles(d_o, 2 * g).astype(BF16) for g in range(ATT_KV_HEADS)]
            p_b = [probs_ref[j, k] for k in range(len(ATT_PROBLEMS))]
            p = [pk.astype(F32) for pk in p_b]
            sink_all = psink_ref[j]
            p_sink = [sink_all[k:k + 1, :] for k in range(len(ATT_PROBLEMS))]
            cos_b, sin_b = cos_ref[rows, :], sin_ref[rows, :]
            qdec, kdec = qdec_ref[...], kdec_ref[...]
            qr, kr, qd, kd = _retention_operands(p_ref, rows, cos_b, sin_b, qdec, kdec)
            heads = [(t, hh) for t in range(PAIRS) for hh in range(2)]
            head_cols = [slice(h * RET_V_DIM, (h + 1) * RET_V_DIM) for h in range(RET_HEADS)]
            q_rows = [jnp.concatenate(_split_heads(_tile(qr, t)), axis=0).astype(BF16) for t in range(PAIRS)]
            k_rows = [jnp.concatenate(_split_heads(_tile(kr, t)), axis=0).astype(BF16) for t in range(PAIRS)]
            din = [din_ref[t] for t in range(PAIRS)]

            dpr = [_fold(_dot_nt(v_ops[g][hi], dos[g]), prev) for g, hi in ATT_PROBLEMS]
            sc = [(_dot_nt(q_rows[t], _tile(kr, t).astype(BF16)) * din[t]).astype(BF16) for t in range(PAIRS)]
            qd_heads = [_split_heads(_tile(qd, t)) for t in range(PAIRS)]
            kd_heads = [_split_heads(_tile(kd, t)) for t in range(PAIRS)]
            state_b = [st_ref[j, t].astype(BF16) for t in range(PAIRS)]
            gst_b = [gsts[t].astype(BF16) for t in range(PAIRS)]
            vs = [_proj_cols(p_ref, rows, OFF_RV + h * RET_V_DIM, OFF_RV + (h + 1) * RET_V_DIM) for h in range(RET_HEADS)]
            rzs = [_proj_cols(p_ref, rows, OFF_RZ + h * RET_V_DIM, OFF_RZ + (h + 1) * RET_V_DIM).astype(F32)
                   for h in range(RET_HEADS)]
            drs = [dmix_ref[rows, ATT_WIDTH + h * RET_V_DIM:ATT_WIDTH + (h + 1) * RET_V_DIM] for h in range(RET_HEADS)]
            gains = [gain_ref[:, c] for c in head_cols]
            lhs = [jnp.concatenate([sc[t][hh * BLK:(hh + 1) * BLK], qd_heads[t][hh].astype(BF16)], axis=1) for t, hh in heads]
            rhs = [jnp.concatenate([vs[2 * t + hh], state_b[t]], axis=0) for t, hh in heads]

            delta = [jnp.sum(pi * di, axis=0, keepdims=True) for pi, di in zip(p, dpr)]
            ds = [_unfold((pi * (di - ti)).astype(BF16), prev) for pi, di, ti in zip(p, dpr, delta)]
            o_ret = [_dot(l, r) for l, r in zip(lhs, rhs)]
            for (g, hi), ki, ti in zip(ATT_PROBLEMS, p_sink, delta):
                sink_part = ki * ti
                for half in range(2):
                    tot = jnp.sum(sink_part[:, half * BLK:(half + 1) * BLK], axis=1, keepdims=True)
                    dsink = dsink - jnp.where(lane == 4 * g + 2 * half + hi, tot, 0.0)

            dq_tiles, dk_sums, dv_sums = [], [], []
            for g in range(ATT_KV_HEADS):
                ds_cat = jnp.concatenate([ds[2 * g], ds[2 * g + 1]], axis=0)
                p_cat = jnp.concatenate([_unfold(p_b[2 * g], prev), _unfold(p_b[2 * g + 1], prev)], axis=0)
                dqs = _dot_tn(ds_cat, jnp.concatenate(k_ops[g], axis=0)) * ATT_SCALE
                dq_tiles += [dqs[0:BLK], dqs[BLK:]]
                dk_sums.append(_dot(ds_cat, qs[g]))
                dv_sums.append(_dot(p_cat, dos[g]))
            ons, rstds = _group_norm_all(o_ret)
            sig_r = [_sigmoid(z) for z in rzs]
            dgn = [d * (z * g) for d, z, g in zip(drs, rzs, sig_r)]
            dz_parts = [d * (o * gn) * (g * (1.0 + z * (1.0 - g))) for d, o, gn, g, z in zip(drs, ons, gains, sig_r, rzs)]
            dgain_parts = [jnp.sum(d * o, axis=0, keepdims=True) for d, o in zip(dgn, ons)]
            don = [d * gn for d, gn in zip(dgn, gains)]
            mean_don = [jnp.mean(d, axis=1, keepdims=True) for d in don]
            mean_don_on = [jnp.mean(d * o, axis=1, keepdims=True) for d, o in zip(don, ons)]
            dob = [(r * (d - a - o * b)).astype(BF16) for r, d, a, o, b in zip(rstds, don, mean_don, ons, mean_don_on)]

            dlhs = [_dot_nt(d, r) for d, r in zip(dob, rhs)]
            drhs = [_dot_tn(l, d) for l, d in zip(lhs, dob)]
            dkds = [_dot_nt(vs[2 * t + hh], gst_b[t]) for t, hh in heads]
            dv_parts = [drhs[2 * t + hh][0:BLK] + _dot(kd_heads[t][hh].astype(BF16), gst_b[t]) for t, hh in heads]
            daz = da * oatt_ref[rows, :] * (sig * (1.0 + az * (1.0 - sig)))

            def kv_grad(sums):
                (a0, b0), (a1, b1) = [(s[0:2 * BLK], s[2 * BLK:]) for s in sums]
                return jnp.where(low2, a0, b1) + pltpu.roll(jnp.where(low2, a1, b0), HALF_LANE, 1)

            dk_both, dv_both = kv_grad(dk_sums), kv_grad(dv_sums)
            dak = dk_both[BLK:] + dkc
            dav = dv_both[BLK:] + dvc
            das = [(dlhs[2 * t + hh][:, 0:BLK] * din[t][hh * BLK:(hh + 1) * BLK]).astype(BF16) for t, hh in heads]
            new_gsts = [gsts[t] * cd_ref[t] + drhs[2 * t][BLK:] + drhs[2 * t + 1][BLK:] for t in range(PAIRS)]
            dq_parts = [_dot(jnp.concatenate([das[2 * t], das[2 * t + 1]], axis=1), k_rows[t])
                        + jnp.where(low, dlhs[2 * t][:, BLK:], dlhs[2 * t + 1][:, BLK:]) * _tile(qdec, t)
                        for t in range(PAIRS)]
            dk_parts = [_dot_tn(jnp.concatenate([das[2 * t], das[2 * t + 1]], axis=0), q_rows[t])
                        + jnp.where(low, dkds[2 * t], dkds[2 * t + 1]) * _tile(kdec, t) for t in range(PAIRS)]
            drq = _rotate_transposed(jnp.concatenate(dq_parts, axis=1), cos_b, sin_b)
            drk = _rotate_transposed(jnp.concatenate(dk_parts, axis=1) * RET_SCALE, cos_b, sin_b)

            dp_ref[rows, :] = jnp.concatenate(
                [jnp.concatenate(dq_tiles, axis=1), dak, dav, daz, drq, drk] + dv_parts + dz_parts, axis=1).astype(BF16)
            dgain = dgain + jnp.concatenate(dgain_parts, axis=1)
            return dk_both[0:BLK], dv_both[0:BLK], tuple(new_gsts), dgain, dsink

        carry = (dkc_ref[...], dvc_ref[...], tuple(gst_ref[t] for t in range(PAIRS)), dgain_ref[...], dsink_ref[...])
        for jb in range(nsub):
            carry = sub(jnp.int32(jb), carry)
        dkc, dvc, gsts, dgain, dsink = carry
        dkc_ref[...] = dkc
        dvc_ref[...] = dvc
        gst_ref[...] = jnp.stack(gsts)
        dgain_ref[...] = dgain
        dsink_ref[...] = dsink

        dh = _dot(dp_ref[...], win_ref[...])
        xv = x_ref[...]
        r = lax.rsqrt(jnp.mean(xv * xv, axis=1, keepdims=True) + RMS_EPS)
        xn = xv * r
        gnorm_ref[...] += jnp.sum(dh * xn, axis=0, keepdims=True)
        u = dh * g_ref[...]
        gx_ref[...] = dxo_ref[...] + r * u - xn * (r * jnp.mean(u * xn, axis=1, keepdims=True))

    rev_rows = lambda w: pl.BlockSpec((tb, w), lambda i: (nblk - 1 - i, 0))
    prev_kv = pl.BlockSpec((None, BLK, 2 * ATT_KV_WIDTH),
                           lambda i: (0, jnp.maximum((nblk - 1 - i) * nsub - 1, 0), kv_cols))
    return _call(
        body, name="mix_bwd", grid=(nblk,),
        out_shape=(
            jax.ShapeDtypeStruct((seq, IN_WIDTH), BF16),
            jax.ShapeDtypeStruct((seq, D_MODEL), F32),
            jax.ShapeDtypeStruct((MIX_WIDTH, D_MODEL), F32),
            jax.ShapeDtypeStruct((1, D_MODEL), F32),
            jax.ShapeDtypeStruct((1, RET_WIDTH), F32),
            jax.ShapeDtypeStruct((1, LANE), F32),
        ),
        in_specs=[
            pl.BlockSpec((N_CHIPS, tb, SHARD_PAD), lambda i: (0, nblk - 1 - i, 0)),
            prev_kv, rev_rows(D_MODEL), rev_rows(MIX_WIDTH), rev_rows(ATT_WIDTH),
            pl.BlockSpec((nsub, len(ATT_PROBLEMS), BLK, 2 * BLK), lambda i: (nblk - 1 - i, 0, 0, 0)),
            pl.BlockSpec((nsub, len(ATT_PROBLEMS), 2 * BLK), lambda i: (nblk - 1 - i, 0, 0)),
            pl.BlockSpec((nsub,) + state_shape, lambda i: (nblk - 1 - i, 0, 0, 0)),
            rev_rows(LANE), rev_rows(LANE), rev_rows(D_MODEL),
            _const_spec((MIX_WIDTH, D_MODEL)), _const_spec((IN_WIDTH, D_MODEL)), _const_spec((1, D_MODEL)),
            _const_spec((1, RET_WIDTH)),
            _const_spec((PAIRS, 2 * BLK, BLK)), _const_spec((BLK, RET_QK_WIDTH)), _const_spec((BLK, RET_QK_WIDTH)),
            _const_spec(state_shape),
        ],
        out_specs=(
            rev_rows(IN_WIDTH), rev_rows(D_MODEL), _const_spec((MIX_WIDTH, D_MODEL)), _const_spec((1, D_MODEL)),
            _const_spec((1, RET_WIDTH)), _const_spec((1, LANE)),
        ),
        scratch_shapes=[
            pltpu.VMEM((tb, MIX_WIDTH), F32),
            pltpu.VMEM((tb + BLK, 2 * ATT_KV_WIDTH), F32),
            pltpu.VMEM((BLK, ATT_KV_WIDTH), F32), pltpu.VMEM((BLK, ATT_KV_WIDTH), F32),
            pltpu.VMEM(state_shape, F32),
        ],
        compiler_params=_params(60, ("arbitrary",)),
    )(proj, proj, dxo, mix, o_att, probs, p_sinks, states, cos_t, sin_t, x, w_out, w_in_t, norm_g, gn_gain, decay_in,
      qdec_t, kdec_t, cd_t)


def _gw_in_reduce(h_t, dproj, gw_out, small, tb):
    seq = dproj.shape[0]
    nblk = seq // tb
    last = nblk - 1
    hand_on = min(1, last)
    half = D_MODEL // 2
    A, B, C, N_SEMS = 0, N_CHIPS, 2 * N_CHIPS, 2 * N_CHIPS + 1

    def body(win_ref, ht_ref, dp_ref, gwo_hbm, s0_ref, s1_ref, s2_ref, s3_ref, s4_ref, out_ref, fout_ref, packsum_ref,
             acc, sib, send_buf, b_in, fin, fout, mine_out, sib_out, send_out, b_out, pack_ref, packs,
             send_sems, recv_sems, local_sems):
        p, i = pl.program_id(0), pl.program_id(1)
        small_start, small_hand_on, small_sum_up, small_finish = _small_exchange(
            gwo_hbm, (s0_ref, s1_ref, s2_ref, s3_ref, s4_ref), fout, mine_out, sib_out, send_out, b_out, pack_ref,
            packs, send_sems, recv_sems, local_sems, N_SEMS)

        @pl.when(jnp.logical_and(p == 0, i == 0))
        def _():
            small_start()

        @pl.when(jnp.logical_and(p == 0, i == hand_on))
        def _():
            small_hand_on()

        @pl.when(jnp.logical_and(p == N_CHIPS - 1, i == 0))
        def _():
            small_sum_up()

        x, y, c = lax.axis_index("x"), lax.axis_index("y"), lax.axis_index("c")
        chip = 2 * x + y
        sibling = (x, y, 1 - c)
        mine = pl.ds(pl.multiple_of(c * half, half), half)
        other = pl.ds(pl.multiple_of((1 - c) * half, half), half)

        def remote(src, dst, send_k, recv_k, to):
            return pltpu.make_async_remote_copy(src_ref=src, dst_ref=dst, send_sem=send_sems.at[send_k],
                                                recv_sem=recv_sems.at[recv_k], device_id=to, device_id_type=MESH_ID)

        slot = p % 2

        @pl.when(i == 0)
        def _():
            acc[slot] = jnp.zeros(acc.shape[1:], F32)

        acc[slot] += _dot(ht_ref[...], dp_ref[...])

        for q in range(N_CHIPS):
            s = q % 2
            to_sibling = remote(acc.at[s, other, :], sib.at[s], A + q, A + q, sibling)

            @pl.when(jnp.logical_and(p == q, i == last))
            def _():
                to_sibling.start()

            if q < N_CHIPS - 1:
                dest = (chip + 1 + q) % N_CHIPS

                @pl.when(jnp.logical_and(p == q + 1, i == hand_on))
                def _():
                    to_sibling.wait_recv()
                    send_buf[q] = (acc[s, mine, :] + sib[s]).astype(BF16)
                    remote(send_buf.at[q], b_in.at[chip], B + q, B + chip, (dest // 2, dest % 2, c)).start()
                    to_sibling.wait_send()
            else:
                @pl.when(jnp.logical_and(p == q, i == last))
                def _():
                    to_sibling.wait_recv()
                    fin[mine, :] = acc[s, mine, :] + sib[s]
                    for j in range(N_CHIPS):
                        @pl.when(j != chip)
                        def _():
                            remote(b_in.at[j], b_in.at[j], B + j, B + j, sibling).wait_recv()
                            fin[mine, :] += b_in[j].astype(F32)
                    to_core = remote(fin.at[mine, :], fin.at[mine, :], C, C, sibling)
                    to_core.start()
                    remote(fin.at[other, :], fin.at[other, :], C, C, sibling).wait_recv()
                    out_ref[...] = fin[...]
                    to_core.wait_send()
                    to_sibling.wait_send()
                    for k in range(N_CHIPS - 1):
                        remote(send_buf.at[k], b_in.at[chip], B + k, B + k, sibling).wait_send()
                    packsum_ref[...] = small_finish()
                    fout_ref[...] = fout[...]

    whole = lambda shape: pl.BlockSpec(shape, lambda p, i, win: (0,) * len(shape), pipeline_mode=pl.Buffered(1))
    grid_spec = pltpu.PrefetchScalarGridSpec(
        num_scalar_prefetch=1, grid=(N_CHIPS, nblk),
        in_specs=[pl.BlockSpec((D_MODEL, tb), lambda p, i, win: (0, i)),
                  pl.BlockSpec((pl.Element(tb), pl.Element(SHARD_PAD)),
                               lambda p, i, win: (i * tb, pl.multiple_of(win[p] * LANE, LANE))),
                  pl.BlockSpec(memory_space=pl.ANY)] + [whole(s.shape) for s in small],
        out_specs=(whole((D_MODEL, SHARD_PAD)), whole((SHARD_OUT, D_MODEL)), whole((PACK_ROWS, D_MODEL))),
        scratch_shapes=[
            pltpu.VMEM((2, D_MODEL, SHARD_PAD), F32), pltpu.VMEM((2, half, SHARD_PAD), F32),
            pltpu.VMEM((N_CHIPS - 1, half, SHARD_PAD), BF16), pltpu.VMEM((N_CHIPS, half, SHARD_PAD), BF16),
            pltpu.VMEM((D_MODEL, SHARD_PAD), F32),
        ] + _small_exchange_scratch() + [
            pltpu.SemaphoreType.DMA((N_SEMS + SMALL_SEMS,)), pltpu.SemaphoreType.DMA((N_SEMS + SMALL_SEMS,)),
            pltpu.SemaphoreType.DMA((N_CHIPS,)),
        ])
    chip = 2 * lax.axis_index("x") + lax.axis_index("y")
    owner = (chip + 1 + jnp.arange(N_CHIPS, dtype=jnp.int32)) % N_CHIPS
    win_start = (owner * SHARD_IN) // LANE
    return _call(
        body, name="gw_in_reduce", grid_spec=grid_spec,
        out_shape=(jax.ShapeDtypeStruct((D_MODEL, SHARD_PAD), F32), jax.ShapeDtypeStruct((SHARD_OUT, D_MODEL), F32),
                   jax.ShapeDtypeStruct((PACK_ROWS, D_MODEL), F32)),
        compiler_params=_params(52, ("arbitrary", "arbitrary")),
    )(win_start.astype(jnp.int32), h_t, dproj, gw_out, *small)


SMALL_SEMS = 17


def _small_exchange(gwo_hbm, small_refs, fout_ref, mine_out, sib_out, send_out, b_out, pack_ref, packs,
                    send_sems, recv_sems, local_sems, base):
    half_out = SHARD_OUT // 2
    A_OUT, B_OUT, C_OUT, PACK = base, base + 4, base + 8, base + 9
    assert len(small_refs) == PACK_PARTS
    x, y, c = lax.axis_index("x"), lax.axis_index("y"), lax.axis_index("c")
    chip = 2 * x + y
    dev = 2 * chip + c
    sibling = (x, y, 1 - c)

    def remote(src, dst, send_k, recv_k, to):
        return pltpu.make_async_remote_copy(src_ref=src, dst_ref=dst, send_sem=send_sems.at[send_k],
                                            recv_sem=recv_sems.at[recv_k], device_id=to, device_id_type=MESH_ID)

    def out_rows(j, core):
        return pl.ds(pl.multiple_of(j * SHARD_OUT + core * half_out, half_out), half_out)

    my_out_rows = pl.ds(pl.multiple_of(c * half_out, half_out), half_out)
    local = [pltpu.make_async_copy(gwo_hbm.at[out_rows(j, c), :], mine_out.at[j], local_sems.at[j])
             for j in range(N_CHIPS)]
    stage_a = [remote(gwo_hbm.at[out_rows(j, 1 - c), :], sib_out.at[j], A_OUT + j, A_OUT + j, sibling)
               for j in range(N_CHIPS)]
    mine_half_out = fout_ref.at[my_out_rows, :]
    stage_c = [remote(mine_half_out, mine_half_out, C_OUT, C_OUT, sibling)]

    def start():
        pack_ref[...] = jnp.zeros_like(pack_ref)
        for k, s_ref in enumerate(small_refs):
            pack_ref[k:k + 1, 0:s_ref.shape[1]] = s_ref[0:1, :]
        packs[dev] = pack_ref[...]
        for d in range(N_DEV):
            to = (d // 4, (d // 2) % 2, d % 2)

            @pl.when(d != dev)
            def _():
                remote(pack_ref, packs.at[dev], PACK + d, PACK + dev, to).start()

        for cp in local + stage_a:
            cp.start()

    def hand_on():
        for cp in local:
            cp.wait()
        for cp in stage_a:
            cp.wait_recv()
        for j in range(N_CHIPS):
            mine_out[j] = mine_out[j] + sib_out[j]

        for j in range(N_CHIPS):
            to = (j // 2, j % 2, c)

            @pl.when(j != chip)
            def _():
                send_out[j] = mine_out[j].astype(BF16)
                remote(send_out.at[j], b_out.at[chip], B_OUT + j, B_OUT + chip, to).start()

            @pl.when(j == chip)
            def _():
                fout_ref[my_out_rows, :] = mine_out[j]

    def sum_up():
        for j in range(N_CHIPS):
            @pl.when(j != chip)
            def _():
                remote(b_out.at[j], b_out.at[j], B_OUT + j, B_OUT + j, sibling).wait_recv()
                fout_ref[my_out_rows, :] += b_out[j].astype(F32)

        for cp in stage_c:
            cp.start()

    def finish():
        other_half_out = fout_ref.at[pl.ds(pl.multiple_of((1 - c) * half_out, half_out), half_out), :]
        remote(other_half_out, other_half_out, C_OUT, C_OUT, sibling).wait_recv()

        for d in range(N_DEV):
            @pl.when(d != dev)
            def _():
                remote(pack_ref, packs.at[d], PACK + d, PACK + d, sibling).wait_recv()
        total = packs[0]
        for d in range(1, N_DEV):
            total = total + packs[d]

        for cp in stage_a + stage_c:
            cp.wait_send()
        for j in range(N_CHIPS):
            @pl.when(j != chip)
            def _():
                remote(b_out.at[j], b_out.at[j], B_OUT + j, B_OUT + j, sibling).wait_send()
        for d in range(N_DEV):
            @pl.when(d != dev)
            def _():
                remote(pack_ref, packs.at[d], PACK + d, PACK + d, sibling).wait_send()
        return total

    return start, hand_on, sum_up, finish


def _small_exchange_scratch():
    half_out = SHARD_OUT // 2
    return [
        pltpu.VMEM((SHARD_OUT, D_MODEL), F32),
        pltpu.VMEM((N_CHIPS, half_out, D_MODEL), F32), pltpu.VMEM((N_CHIPS, half_out, D_MODEL), F32),
        pltpu.VMEM((N_CHIPS, half_out, D_MODEL), BF16), pltpu.VMEM((N_CHIPS, half_out, D_MODEL), BF16),
        pltpu.VMEM((PACK_ROWS, D_MODEL), F32), pltpu.VMEM((N_DEV, PACK_ROWS, D_MODEL), F32),
    ]


def _adam_math(w, g, m, v):
    mn = ADAM_B1 * m + (1.0 - ADAM_B1) * g
    vn = ADAM_B2 * v + (1.0 - ADAM_B2) * (g * g)
    m_hat = mn / (1.0 - ADAM_B1 ** ADAM_STEP)
    v_hat = vn / (1.0 - ADAM_B2 ** ADAM_STEP)
    return -ADAM_LR * (m_hat / (jnp.sqrt(v_hat) + ADAM_EPS) + ADAM_WD * w), mn, vn


def _adamw(name, w, g, m, v, tb):
    rows, cols = w.shape

    def body(w_ref, g_ref, m_ref, v_ref, go_ref, d_ref, mo_ref, vo_ref):
        gv = g_ref[...]
        go_ref[...] = gv
        d_ref[...], mo_ref[...], vo_ref[...] = _adam_math(w_ref[...], gv, m_ref[...], v_ref[...])

    spec = pl.BlockSpec((tb, cols), lambda i: (i, 0))
    shape = jax.ShapeDtypeStruct((rows, cols), F32)
    return _call(
        body, name=name, grid=(rows // tb,), out_shape=(shape,) * 4,
        in_specs=[spec] * 4, out_specs=(spec,) * 4,
        compiler_params=_params(32, ("arbitrary",)),
    )(w, g, m, v)


def _adamw_w_in(w_t, g_window, m_t, v_t, tb):
    def body(w_ref, g_ref, m_ref, v_ref, go_ref, d_ref, mo_ref, vo_ref, gt_ref):
        gt_ref[...] = g_ref[...].T
        gv = gt_ref[pl.ds(pl.multiple_of(lax.axis_index("y") * SHARD_SHIFT, SHARD_SHIFT), SHARD_IN), :]
        go_ref[...] = gv
        d_ref[...], mo_ref[...], vo_ref[...] = _adam_math(w_ref[...], gv, m_ref[...], v_ref[...])

    spec = pl.BlockSpec((SHARD_IN, tb), lambda i: (0, i))
    shape = jax.ShapeDtypeStruct((SHARD_IN, D_MODEL), F32)
    return _call(
        body, name="adamw_w_in", grid=(D_MODEL // tb,), out_shape=(shape,) * 4,
        in_specs=[spec, pl.BlockSpec((tb, SHARD_PAD), lambda i: (i, 0)), spec, spec], out_specs=(spec,) * 4,
        scratch_shapes=[pltpu.VMEM((SHARD_PAD, tb), F32)],
        compiler_params=_params(32, ("arbitrary",)),
    )(w_t, g_window, m_t, v_t)


def _adamw_small(sums, params):
    def body(sums_ref, *refs):
        ins, outs = refs[:3 * len(params)], refs[3 * len(params):]
        for k in range(len(params)):
            w_ref, m_ref, v_ref = ins[3 * k:3 * k + 3]
            g = sums_ref[k:k + 1, 0:w_ref.shape[1]]
            go_ref, d_ref, mo_ref, vo_ref = outs[4 * k:4 * k + 4]
            go_ref[...] = g
            d_ref[...], mo_ref[...], vo_ref[...] = _adam_math(w_ref[...], g, m_ref[...], v_ref[...])

    vmem = pl.BlockSpec(memory_space=pltpu.VMEM)
    flat = [a for p in params for a in p]
    shapes = tuple(jax.ShapeDtypeStruct(p[0].shape, F32) for p in params for _ in range(4))
    res = _call(body, name="adamw_small", out_shape=shapes, in_specs=[vmem] * (1 + len(flat)),
                out_specs=(vmem,) * len(shapes), compiler_params=_params(16))(sums, *flat)
    return [res[4 * k:4 * k + 4] for k in range(len(params))]


def kernel(x, norm_g, w_in, att_sinks, ret_gn_g, w_out, final_g, loss_target, m_norm_g, m_w_in, m_att_sinks, m_ret_gn_g, m_w_out, m_final_g, v_norm_g, v_w_in, v_att_sinks, v_ret_gn_g, v_w_out, v_final_g):
    seq = x.shape[1]
    xs, tgt = x[0], loss_target[0]
    final_g2 = final_g.reshape(1, D_MODEL)
    tables = _tables(seq)

    w_in_t, m_w_in_t, v_w_in_t = w_in[0].T, m_w_in[0].T, v_w_in[0].T
    proj, h_t, w_in_full, w_out_full = _in_proj(xs, norm_g, w_in_t, w_out[0], min(TOKENS_PROJ, seq))
    mix, dxo, states, loss_part, gfin, o_att, probs, p_sinks = _mix_fwd(
        proj, xs, tgt, w_out_full, final_g2, ret_gn_g, att_sinks, tables, min(TOKENS_MIX, seq))
    dproj, grad_x, gw_out, gnorm, dgain, dsink = _mix_bwd(
        proj, dxo, mix, o_att, probs, p_sinks, states, xs, w_out_full, w_in_full, norm_g, ret_gn_g, tables,
        min(TOKENS_MIX, seq))
    g_in, g_out, sums = _gw_in_reduce(h_t, dproj, gw_out, (gnorm, gfin, dgain, dsink, loss_part),
                                      min(TOKENS_GW, seq))

    res_in = [r.T for r in _adamw_w_in(w_in_t, g_in, m_w_in_t, v_w_in_t, 256)]
    res_out = _adamw("adamw_w_out", w_out[0], g_out, m_w_out[0], v_w_out[0], SHARD_OUT)
    as_row = lambda a: a.reshape(1, D_MODEL)
    r_norm, r_final, r_gain, r_sink = _adamw_small(sums, [
        (norm_g, m_norm_g, v_norm_g), (final_g2, as_row(m_final_g), as_row(v_final_g)),
        (ret_gn_g, m_ret_gn_g, v_ret_gn_g), (att_sinks, m_att_sinks, v_att_sinks)])

    outs = []
    for k in range(4):
        outs += [r_norm[k], res_in[k][None], r_sink[k], r_gain[k], res_out[k][None], r_final[k].reshape(D_MODEL)]
    return (sums[4, 0], grad_x[None], *outs)
```

```python
import jax
import jax.numpy as jnp
import numpy as np
from jax import lax
from jax.experimental import pallas as pl
from jax.experimental.pallas import tpu as pltpu

F32 = jnp.float32
BF16 = jnp.bfloat16

D_MODEL = 1024
ATT_HEADS = 8
ATT_KV_HEADS = 2
ATT_HEAD_DIM = 64
RET_HEADS = 4
RET_QK_DIM = 64
RET_V_DIM = 128
BLK = 128
ROT_BASE = 10000.0
RMS_EPS = 1e-6
GN_EPS = 1e-6
NEG_INF = -1e30
ATT_SCALE = ATT_HEAD_DIM ** -0.5
RET_SCALE = RET_QK_DIM ** -0.5

ATT_WIDTH = ATT_HEADS * ATT_HEAD_DIM
ATT_KV_WIDTH = ATT_KV_HEADS * ATT_HEAD_DIM
RET_QK_WIDTH = RET_HEADS * RET_QK_DIM
RET_WIDTH = RET_HEADS * RET_V_DIM
MIX_WIDTH = ATT_WIDTH + RET_WIDTH
OFF_AQ = 0
OFF_AK = OFF_AQ + ATT_WIDTH
OFF_AV = OFF_AK + ATT_KV_WIDTH
OFF_AZ = OFF_AV + ATT_KV_WIDTH
OFF_RQ = OFF_AZ + ATT_WIDTH
OFF_RK = OFF_RQ + RET_QK_WIDTH
OFF_RV = OFF_RK + RET_QK_WIDTH
OFF_RZ = OFF_RV + RET_WIDTH
IN_WIDTH = OFF_RZ + RET_WIDTH

LANE = 128
BF16_ROWS = 16
HALF_LANE = LANE // 2
PAIRS = RET_QK_WIDTH // LANE
assert ATT_HEAD_DIM == HALF_LANE and RET_QK_DIM == HALF_LANE and RET_V_DIM == LANE and ATT_KV_WIDTH == LANE

N_CHIPS = 4
N_DEV = 8
SHARD_IN = IN_WIDTH // N_CHIPS
SHARD_PAD = 768
SHARD_SHIFT = SHARD_PAD - SHARD_IN
WIN_START = tuple((j * SHARD_IN) // LANE * LANE for j in range(N_CHIPS))
SHARD_OUT = MIX_WIDTH // N_CHIPS

PACK_PARTS = 5
PACK_ROWS = 8
assert PACK_PARTS <= PACK_ROWS

ADAM_LR = 0.001
ADAM_B1 = 0.9
ADAM_B2 = 0.999
ADAM_EPS = 1e-08
ADAM_WD = 0.01
ADAM_STEP = 10

VMEM_CAP = 64 * 1024 * 1024
TOKENS_PROJ = 1024
TOKENS_MIX = 512
TOKENS_GW = 2048
MESH_ID = pl.DeviceIdType.MESH


def _call(body, **kw):
    return pl.pallas_call(body, **kw)


def _params(vmem_mb, semantics=None):
    assert vmem_mb * 1024 * 1024 < VMEM_CAP
    return pltpu.CompilerParams(dimension_semantics=semantics, vmem_limit_bytes=vmem_mb * 1024 * 1024)


def _dot(a, b):
    return jnp.dot(a, b, preferred_element_type=F32)


def _dot_nt(a, b):
    return lax.dot_general(a, b, (((1,), (1,)), ((), ())), preferred_element_type=F32)


def _dot_tn(a, b):
    return lax.dot_general(a, b, (((0,), (0,)), ((), ())), preferred_element_type=F32)


def _sigmoid(z):
    return 1.0 / (1.0 + jnp.exp(-z))


def _const_spec(shape):
    nd = len(shape)
    return pl.BlockSpec(shape, lambda i: (0,) * nd, pipeline_mode=pl.Buffered(1))


def _tables(seq):
    f32 = np.float32
    pos = np.arange(seq, dtype=f32)
    theta = (f32(1.0) / (f32(ROT_BASE) ** np.linspace(0.0, 1.0, RET_QK_DIM // 2, dtype=f32))).astype(f32)
    ang = (pos[:, None] * theta[None, :]).astype(f32)
    cos, sin = np.cos(ang), np.sin(ang)
    cos2 = np.repeat(cos, 2, axis=1)
    sin2 = np.stack([-sin, sin], axis=-1).reshape(seq, RET_QK_DIM)
    cos_t = np.tile(cos2, (1, 2))
    sin_t = np.tile(sin2, (1, 2))

    log_gamma = np.log(f32(1.0) - f32(2.0) ** (f32(-5.0) - np.arange(RET_HEADS, dtype=f32))).astype(f32)
    idx = np.arange(BLK, dtype=f32)
    rel = idx[:, None] - idx[None, :]
    decay_in = np.where(rel >= 0, np.exp(log_gamma[:, None, None] * np.maximum(rel, f32(0.0))), f32(0.0))
    k_dec = np.exp(log_gamma[:, None] * (BLK - 1 - idx)[None, :])
    q_dec = np.exp(log_gamma[:, None] * (idx + 1)[None, :])
    chunk_decay = np.exp(log_gamma * f32(BLK))
    kdec_t = np.repeat(k_dec.T, RET_QK_DIM, axis=1)
    qdec_t = np.repeat(q_dec.T, RET_QK_DIM, axis=1)
    cd_t = np.broadcast_to(chunk_decay[:, None, None], (RET_HEADS, RET_QK_DIM, RET_V_DIM))
    decay_in = decay_in.reshape(PAIRS, 2 * BLK, BLK)
    cd_t = cd_t.reshape(PAIRS, 2 * RET_QK_DIM, RET_V_DIM)

    key = np.arange(BLK)[:, None]
    query = np.arange(2 * BLK)[None, :] % BLK
    bias = np.stack([np.where(key > query, NEG_INF, 0.0), np.zeros((BLK, 2 * BLK))])
    return tuple(jnp.asarray(np.ascontiguousarray(a), F32) for a in (cos_t, sin_t, decay_in, qdec_t, kdec_t, cd_t, bias))


def _low_lanes(shape):
    lane = lax.broadcasted_iota(jnp.int32, shape, len(shape) - 1)
    return (lane & HALF_LANE) == 0


def _split_heads(t):
    low = _low_lanes(t.shape)
    zero = jnp.zeros_like(t)
    return jnp.where(low, t, zero), jnp.where(low, zero, t)


def _swap_pairs(t):
    lane = lax.broadcasted_iota(jnp.int32, t.shape, 1)
    nxt = pltpu.roll(t, t.shape[1] - 1, 1)
    prv = pltpu.roll(t, 1, 1)
    return jnp.where((lane & 1) == 0, nxt, prv)


def _per_tile(fn, t):
    return jnp.concatenate([fn(_tile(t, i)) for i in range(t.shape[1] // LANE)], axis=1)


def _rotate(t, cos_t, sin_t):
    return _per_tile(lambda a: a * cos_t + _swap_pairs(a) * sin_t, t)


def _rotate_transposed(d, cos_t, sin_t):
    return _per_tile(lambda a: a * cos_t + _swap_pairs(a * sin_t), d)


def _kv_operands(cat):
    low = _low_lanes(cat.shape)
    swapped = pltpu.roll(cat, HALF_LANE, 1)
    zero = jnp.zeros_like(cat)
    pick = lambda a, b: jnp.where(low, a, b).astype(BF16)
    return ((pick(cat, zero), pick(zero, swapped)), (pick(swapped, zero), pick(zero, cat)))


def _stack_tiles(t, first_tile):
    a = t[:, first_tile * LANE:(first_tile + 1) * LANE]
    b = t[:, (first_tile + 1) * LANE:(first_tile + 2) * LANE]
    return jnp.concatenate([a, b], axis=0)


def _sink_rows(sinks_ref, group):
    first = lax.broadcasted_iota(jnp.int32, (1, 2 * BLK), 1) < BLK

    def row(h0, h1):
        return jnp.where(first, sinks_ref[0, group * 4 + h0], sinks_ref[0, group * 4 + h1])
    return row(0, 2), row(1, 3)


ATT_PROBLEMS = tuple((g, hi) for g in range(ATT_KV_HEADS) for hi in range(2))


def _in_previous_block():
    key = lax.broadcasted_iota(jnp.int32, (BLK, 2 * BLK), 0)
    query = lax.broadcasted_iota(jnp.int32, (BLK, 2 * BLK), 1) & (BLK - 1)
    return key > query


def _fold(t, prev):
    return jnp.where(prev, t[0:BLK], t[BLK:])


def _unfold(t, prev):
    zero = jnp.zeros_like(t)
    return jnp.concatenate([jnp.where(prev, t, zero), jnp.where(prev, zero, t)], axis=0)


def _attn_scores(qs, k_ops, bias, prev):
    return [_fold(_dot_nt(k_ops[g][hi], qs[g]), prev) + bias for g, hi in ATT_PROBLEMS]


def _attn_softmax(s, sinks_ref):
    sink = [_sink_rows(sinks_ref, g)[hi] for g, hi in ATT_PROBLEMS]
    m = [jnp.maximum(jnp.max(si, axis=0, keepdims=True), ki) for si, ki in zip(s, sink)]
    e = [jnp.exp(si - mi) for si, mi in zip(s, m)]
    es = [jnp.exp(ki - mi) for ki, mi in zip(sink, m)]
    inv = [1.0 / (jnp.sum(ei, axis=0, keepdims=True) + esi) for ei, esi in zip(e, es)]
    return [ei * ii for ei, ii in zip(e, inv)], [esi * ii for esi, ii in zip(es, inv)]


def _group_norm_all(outs):
    mu = [jnp.mean(o, axis=1, keepdims=True) for o in outs]
    xc = [o - m for o, m in zip(outs, mu)]
    var = [jnp.mean(c * c, axis=1, keepdims=True) for c in xc]
    rstd = [lax.rsqrt(v + GN_EPS) for v in var]
    return [c * r for c, r in zip(xc, rstd)], rstd


def _proj_cols(p_ref, rows, start, stop):
    pieces = []
    while start < stop:
        k = max(j for j in range(N_CHIPS) if WIN_START[j] <= start)
        end = min(stop, WIN_START[k] + SHARD_PAD)
        pieces.append(p_ref[k, rows, start - WIN_START[k]:end - WIN_START[k]])
        start = end
    return pieces[0] if len(pieces) == 1 else jnp.concatenate(pieces, axis=1)


def _retention_operands(p_ref, rows, cos_b, sin_b, qdec, kdec):
    qr = _rotate(_proj_cols(p_ref, rows, OFF_RQ, OFF_RQ + RET_QK_WIDTH).astype(F32), cos_b, sin_b)
    kr = _rotate(_proj_cols(p_ref, rows, OFF_RK, OFF_RK + RET_QK_WIDTH).astype(F32), cos_b, sin_b) * RET_SCALE
    return qr, kr, qr * qdec, kr * kdec


def _tile(t, i):
    return t[:, i * LANE:(i + 1) * LANE]


GATHER_SEMS = 7


def _allgather_steps(src_ref, full_ref, blk_ref, send_sems, recv_sems):
    block_rows = blk_ref.shape[0]
    assert block_rows % BF16_ROWS == 0 and full_ref.shape[0] == N_DEV * block_rows
    x, y, c = lax.axis_index("x"), lax.axis_index("y"), lax.axis_index("c")
    me, sibling = (x, y, c), (x, y, 1 - c)
    chips = [(1 - x, y), (x, 1 - y), (1 - x, 1 - y)]

    def rows(px, py, pc):
        return full_ref.at[pl.ds(pl.multiple_of((4 * px + 2 * py + pc) * block_rows, BF16_ROWS), block_rows), :]

    def copy(k, block, to, src=None):
        return pltpu.make_async_remote_copy(
            src_ref=rows(*block) if src is None else src, dst_ref=rows(*block),
            send_sem=send_sems.at[k], recv_sem=recv_sems.at[k], device_id=to, device_id_type=MESH_ID)

    first = [copy(0, me, sibling, src=blk_ref)] + [copy(1 + j, me, (*chip, c), src=blk_ref) for j, chip in enumerate(chips)]
    passed = [copy(4 + j, (*chip, c), sibling) for j, chip in enumerate(chips)]

    def start():
        blk_ref[...] = src_ref[pl.ds(pl.multiple_of(c * block_rows, BF16_ROWS), block_rows), :].astype(BF16)
        rows(*me)[...] = blk_ref[...]
        for cp in first:
            cp.start()

    def forward():
        for j, chip in enumerate(chips):
            copy(1 + j, (*chip, c), me).wait_recv()
            passed[j].start()

    def finish():
        copy(0, sibling, me).wait_recv()
        for j, chip in enumerate(chips):
            copy(4 + j, (*chip, 1 - c), me).wait_recv()
        for cp in first + passed:
            cp.wait_send()

    return start, forward, finish


WINDOW_HALF = SHARD_PAD // 2
WINDOW_SEMS = GATHER_SEMS + 1


def _window_gather(wi_ref, w4_ref, blk_ref, edge_ref, send_sems, recv_sems):
    x, y, c = lax.axis_index("x"), lax.axis_index("y"), lax.axis_index("c")
    me, sibling, pair = (x, y, c), (x, y, 1 - c), (x, 1 - y, c)
    chips = [(1 - x, y), (x, 1 - y), (1 - x, 1 - y)]
    own_rows, edge_rows = SHARD_IN - WINDOW_HALF, SHARD_SHIFT
    sends_edge = y == c

    def rows(px, py, pc):
        return w4_ref.at[pl.ds(pl.multiple_of((4 * px + 2 * py + pc) * WINDOW_HALF, BF16_ROWS), WINDOW_HALF), :]

    def copy(k, block, to, src=None):
        return pltpu.make_async_remote_copy(
            src_ref=rows(*block) if src is None else src, dst_ref=rows(*block),
            send_sem=send_sems.at[k], recv_sem=recv_sems.at[k], device_id=to, device_id_type=MESH_ID)

    edge_copy = pltpu.make_async_remote_copy(
        src_ref=edge_ref, dst_ref=edge_ref, send_sem=send_sems.at[GATHER_SEMS], recv_sem=recv_sems.at[GATHER_SEMS],
        device_id=pair, device_id_type=MESH_ID)
    first = [copy(0, me, sibling, src=blk_ref)] + [copy(1 + j, me, (*chip, c), src=blk_ref) for j, chip in enumerate(chips)]
    passed = [copy(4 + j, (*chip, c), sibling) for j, chip in enumerate(chips)]

    def at(start, size):
        return pl.ds(pl.multiple_of(start, BF16_ROWS), size)

    def start():
        @pl.when(sends_edge)
        def _():
            edge_ref[...] = wi_ref[at((1 - y) * (SHARD_IN - edge_rows), edge_rows), :].astype(BF16)
            edge_copy.start()
            blk_ref[...] = wi_ref[at(c * own_rows, WINDOW_HALF), :].astype(BF16)

        @pl.when(jnp.logical_not(sends_edge))
        def _():
            blk_ref[at(y * edge_rows, own_rows), :] = wi_ref[at(c * WINDOW_HALF, own_rows), :].astype(BF16)
            edge_copy.wait_recv()
            blk_ref[at((1 - y) * own_rows, edge_rows), :] = edge_ref[...]

        rows(*me)[...] = blk_ref[...]
        for cp in first[:-1]:
            cp.start()
        copy(0, sibling, me).wait_recv()

    def send_far():
        first[-1].start()

    def forward(j):
        copy(1 + j, (*chips[j], c), me).wait_recv()
        passed[j].start()

    def wait_forwarded(j):
        copy(4 + j, (*chips[j], 1 - c), me).wait_recv()

    def finish():
        for cp in first + passed:
            cp.wait_send()

        @pl.when(sends_edge)
        def _():
            edge_copy.wait_send()

    return start, send_far, forward, wait_forwarded, finish


def _in_proj(x, norm_g, w_in_t_shard, w_out_shard, tb):
    seq = x.shape[0]
    nblk = seq // tb
    last = nblk - 1
    chip_of_panel = (None, 1, 0, 2)

    def body(win_ref, x_ref, g_ref, wi_ref, wo_ref, p_ref, ht_ref, wt_ref, wout_ref,
             w4, blk, edge, hbuf, wout_full, blko, send_sems, recv_sems, send_sems_o, recv_sems_o):
        q, i = pl.program_id(0), pl.program_id(1)
        chip = 2 * lax.axis_index("x") + lax.axis_index("y")
        in_start, in_send_far, in_forward, in_wait_forwarded, in_finish = _window_gather(
            wi_ref, w4, blk, edge, send_sems, recv_sems)
        out_start, out_forward, out_finish = _allgather_steps(wo_ref, wout_full, blko, send_sems_o, recv_sems_o)
        rows = pl.ds(pl.multiple_of(i * tb, tb), tb)

        @pl.when(jnp.logical_and(q == 0, i == 0))
        def _():
            in_start()

        @pl.when(jnp.logical_and(q == 0, i == min(last // 2 + 1, last)))
        def _():
            in_send_far()

        @pl.when(jnp.logical_and(q == 1, i == 0))
        def _():
            out_start()

        for panel in range(1, N_CHIPS):
            @pl.when(jnp.logical_and(q == panel, i == 0))
            def _():
                in_wait_forwarded(chip_of_panel[panel])

        @pl.when(q == 0)
        def _():
            xv = x_ref[...]
            r = lax.rsqrt(jnp.mean(xv * xv, axis=1, keepdims=True) + RMS_EPS)
            h = (xv * r) * g_ref[...]
            hbuf[rows, :] = h.astype(BF16)
            ht_ref[...] = h.T.astype(BF16)

        owner = jnp.bitwise_xor(chip, q)
        window = w4[pl.ds(pl.multiple_of(owner * SHARD_PAD, SHARD_PAD), SHARD_PAD), :]
        p_ref[...] = _dot_nt(hbuf[rows, :], window).astype(BF16)

        for panel in range(1, N_CHIPS):
            @pl.when(jnp.logical_and(q == panel - 1, i == last))
            def _():
                in_forward(chip_of_panel[panel])

        @pl.when(jnp.logical_and(q == N_CHIPS - 1, i == 0))
        def _():
            out_forward()

        @pl.when(jnp.logical_and(q == N_CHIPS - 1, i == last))
        def _():
            in_finish()
            for k in range(N_CHIPS):
                first = k * SHARD_PAD + (k % 2) * SHARD_SHIFT
                wt_ref[k * SHARD_IN:(k + 1) * SHARD_IN, :] = w4[first:first + SHARD_IN, :]
            out_finish()
            wout_ref[...] = wout_full[...]

    whole = lambda shape: pl.BlockSpec(shape, lambda q, i, win: (0,) * len(shape), pipeline_mode=pl.Buffered(1))
    first_panel = lambda q, i: jnp.where(q == 0, i, last)
    grid_spec = pltpu.PrefetchScalarGridSpec(
        num_scalar_prefetch=1, grid=(N_CHIPS, nblk),
        in_specs=[pl.BlockSpec((tb, D_MODEL), lambda q, i, win: (first_panel(q, i), 0)), whole((1, D_MODEL)),
                  whole((SHARD_IN, D_MODEL)), whole((SHARD_OUT, D_MODEL))],
        out_specs=(pl.BlockSpec((None, tb, SHARD_PAD), lambda q, i, win: (win[q], i, 0)),
                   pl.BlockSpec((D_MODEL, tb), lambda q, i, win: (0, first_panel(q, i))),
                   whole((IN_WIDTH, D_MODEL)), whole((MIX_WIDTH, D_MODEL))),
        scratch_shapes=[
            pltpu.VMEM((N_CHIPS * SHARD_PAD, D_MODEL), BF16), pltpu.VMEM((WINDOW_HALF, D_MODEL), BF16),
            pltpu.VMEM((SHARD_SHIFT, D_MODEL), BF16), pltpu.VMEM((seq, D_MODEL), BF16),
            pltpu.VMEM((MIX_WIDTH, D_MODEL), BF16), pltpu.VMEM((SHARD_OUT // 2, D_MODEL), BF16),
            pltpu.SemaphoreType.DMA((WINDOW_SEMS,)), pltpu.SemaphoreType.DMA((WINDOW_SEMS,)),
            pltpu.SemaphoreType.DMA((GATHER_SEMS,)), pltpu.SemaphoreType.DMA((GATHER_SEMS,)),
        ])
    chip = 2 * lax.axis_index("x") + lax.axis_index("y")
    owner = jnp.bitwise_xor(chip, jnp.arange(N_CHIPS, dtype=jnp.int32))
    return _call(
        body, name="in_proj", grid_spec=grid_spec,
        out_shape=(jax.ShapeDtypeStruct((N_CHIPS, seq, SHARD_PAD), BF16), jax.ShapeDtypeStruct((D_MODEL, seq), BF16),
                   jax.ShapeDtypeStruct((IN_WIDTH, D_MODEL), BF16), jax.ShapeDtypeStruct((MIX_WIDTH, D_MODEL), BF16)),
        compiler_params=_params(60, ("arbitrary", "arbitrary")),
    )(owner.astype(jnp.int32), x, norm_g, w_in_t_shard, w_out_shard)


def _mix_fwd(proj, x, target, w_out, final_g, gn_gain, sinks, tables, tb):
    seq = x.shape[0]
    nsub = tb // BLK
    cos_t, sin_t, decay_in, qdec_t, kdec_t, cd_t, bias_t = tables

    def body(p_ref, x_ref, t_ref, cos_ref, sin_ref, wout_ref, fg_ref, gain_ref, sinks_ref, din_ref, qdec_ref,
             kdec_ref, cd_ref, bias_ref, mix_ref, dxo_ref, st_ref, loss_ref, gfin_ref, oatt_ref, probs_ref, psink_ref,
             kprev_ref, vprev_ref, state_ref):
        i = pl.program_id(0)

        @pl.when(i == 0)
        def _():
            kprev_ref[...] = jnp.zeros_like(kprev_ref)
            vprev_ref[...] = jnp.zeros_like(vprev_ref)
            state_ref[...] = jnp.zeros_like(state_ref)
            loss_ref[...] = jnp.zeros_like(loss_ref)
            gfin_ref[...] = jnp.zeros_like(gfin_ref)

        prev = _in_previous_block()

        def sub(j, carry):
            kp, vp, states = carry
            rows = pl.ds(pl.multiple_of(j * BLK, BLK), BLK)
            bias = bias_ref[jnp.where(jnp.logical_or(i > 0, j > 0), 1, 0)]

            aq = _proj_cols(p_ref, rows, OFF_AQ, OFF_AQ + ATT_WIDTH)
            ak = _proj_cols(p_ref, rows, OFF_AK, OFF_AK + ATT_KV_WIDTH).astype(F32)
            av = _proj_cols(p_ref, rows, OFF_AV, OFF_AV + ATT_KV_WIDTH).astype(F32)
            az = _proj_cols(p_ref, rows, OFF_AZ, OFF_AZ + ATT_WIDTH).astype(F32)
            k_ops = _kv_operands(jnp.concatenate([kp, ak], axis=0))
            v_ops = _kv_operands(jnp.concatenate([vp, av], axis=0))
            qs = [(_stack_tiles(aq, 2 * g) * ATT_SCALE).astype(BF16) for g in range(ATT_KV_HEADS)]

            qr, kr, qd, kd = _retention_operands(p_ref, rows, cos_ref[rows, :], sin_ref[rows, :],
                                                 qdec_ref[...], kdec_ref[...])
            s = _attn_scores(qs, k_ops, bias, prev)
            heads = [(t, hh) for t in range(PAIRS) for hh in range(2)]
            sc = [_dot_nt(jnp.concatenate(_split_heads(_tile(qr, t)), axis=0).astype(BF16), _tile(kr, t).astype(BF16))
                  * din_ref[t] for t in range(PAIRS)]
            qd_heads = [_split_heads(_tile(qd, t)) for t in range(PAIRS)]
            state_b = [states[t].astype(BF16) for t in range(PAIRS)]
            vs = [_proj_cols(p_ref, rows, OFF_RV + h * RET_V_DIM, OFF_RV + (h + 1) * RET_V_DIM) for h in range(RET_HEADS)]
            rzs = [_proj_cols(p_ref, rows, OFF_RZ + h * RET_V_DIM, OFF_RZ + (h + 1) * RET_V_DIM).astype(F32)
                   for h in range(RET_HEADS)]
            lhs = [jnp.concatenate([sc[t][hh * BLK:(hh + 1) * BLK].astype(BF16), qd_heads[t][hh].astype(BF16)], axis=1)
                   for t, hh in heads]
            p, p_sink = _attn_softmax(s, sinks_ref)
            o_ret = [_dot(lhs[2 * t + hh], jnp.concatenate([vs[2 * t + hh], state_b[t]], axis=0)) for t, hh in heads]
            p = [pi.astype(BF16) for pi in p]
            o_tiles = []
            for g in range(ATT_KV_HEADS):
                p_cat = jnp.concatenate([_unfold(p[2 * g], prev), _unfold(p[2 * g + 1], prev)], axis=0)
                o = _dot_tn(p_cat, jnp.concatenate(v_ops[g], axis=0))
                o_tiles += [o[0:BLK], o[BLK:]]
            ons, _ = _group_norm_all(o_ret)
            new_states = [states[t] * cd_ref[t]
                          + _dot_tn(jnp.concatenate(_split_heads(_tile(kd, t)), axis=0).astype(BF16),
                                    jnp.concatenate([vs[2 * t], vs[2 * t + 1]], axis=0)) for t in range(PAIRS)]
            o_att = jnp.concatenate(o_tiles, axis=1)
            out = [o_att * (az * _sigmoid(az))]
            out += [(ons[h] * gain_ref[:, h * RET_V_DIM:(h + 1) * RET_V_DIM]) * (rzs[h] * _sigmoid(rzs[h]))
                    for h in range(RET_HEADS)]
            mix_ref[rows, :] = jnp.concatenate(out, axis=1).astype(BF16)
            oatt_ref[rows, :] = o_att
            probs_ref[j] = jnp.stack(p)
            psink_ref[j] = jnp.concatenate(p_sink, axis=0)
            st_ref[j] = jnp.stack(states)
            return ak, av, tuple(new_states)

        carry = (kprev_ref[...], vprev_ref[...], tuple(state_ref[t] for t in range(PAIRS)))
        for jb in range(nsub):
            carry = sub(jnp.int32(jb), carry)
        kp, vp, states = carry
        kprev_ref[...] = kp
        vprev_ref[...] = vp
        state_ref[...] = jnp.stack(states)

        xo = x_ref[...] + _dot(mix_ref[...], wout_ref[...])
        r2 = lax.rsqrt(jnp.mean(xo * xo, axis=1, keepdims=True) + RMS_EPS)
        xn = xo * r2
        err = xn * fg_ref[...] - t_ref[...]
        loss_ref[...] += jnp.sum(err * err) * (0.5 / D_MODEL)
        dy = err * (1.0 / D_MODEL)
        gfin_ref[...] += jnp.sum(dy * xn, axis=0, keepdims=True)
        u = dy * fg_ref[...]
        dxo_ref[...] = r2 * u - xn * (r2 * jnp.mean(u * xn, axis=1, keepdims=True))

    blk_rows = lambda w: pl.BlockSpec((tb, w), lambda i: (i, 0))
    state_shape = (PAIRS, 2 * RET_QK_DIM, RET_V_DIM)
    return _call(
        body, name="mix_fwd", grid=(seq // tb,),
        out_shape=(
            jax.ShapeDtypeStruct((seq, MIX_WIDTH), BF16),
            jax.ShapeDtypeStruct((seq, D_MODEL), F32),
            jax.ShapeDtypeStruct((seq // BLK,) + state_shape, F32),
            jax.ShapeDtypeStruct((8, LANE), F32),
            jax.ShapeDtypeStruct((1, D_MODEL), F32),
            jax.ShapeDtypeStruct((seq, ATT_WIDTH), F32),
            jax.ShapeDtypeStruct((seq // BLK, len(ATT_PROBLEMS), BLK, 2 * BLK), BF16),
            jax.ShapeDtypeStruct((seq // BLK, len(ATT_PROBLEMS), 2 * BLK), F32),
        ),
        in_specs=[
            pl.BlockSpec((N_CHIPS, tb, SHARD_PAD), lambda i: (0, i, 0)),
            blk_rows(D_MODEL), blk_rows(D_MODEL), blk_rows(LANE), blk_rows(LANE),
            _const_spec((MIX_WIDTH, D_MODEL)), _const_spec((1, D_MODEL)), _const_spec((1, RET_WIDTH)),
            pl.BlockSpec(memory_space=pltpu.SMEM),
            _const_spec((PAIRS, 2 * BLK, BLK)), _const_spec((BLK, RET_QK_WIDTH)), _const_spec((BLK, RET_QK_WIDTH)),
            _const_spec(state_shape), _const_spec((2, BLK, 2 * BLK)),
        ],
        out_specs=(
            blk_rows(MIX_WIDTH), blk_rows(D_MODEL),
            pl.BlockSpec((nsub,) + state_shape, lambda i: (i, 0, 0, 0)),
            _const_spec((8, LANE)), _const_spec((1, D_MODEL)), blk_rows(ATT_WIDTH),
            pl.BlockSpec((nsub, len(ATT_PROBLEMS), BLK, 2 * BLK), lambda i: (i, 0, 0, 0)),
            pl.BlockSpec((nsub, len(ATT_PROBLEMS), 2 * BLK), lambda i: (i, 0, 0)),
        ),
        scratch_shapes=[
            pltpu.VMEM((BLK, ATT_KV_WIDTH), F32), pltpu.VMEM((BLK, ATT_KV_WIDTH), F32),
            pltpu.VMEM(state_shape, F32),
        ],
        compiler_params=_params(48, ("arbitrary",)),
    )(proj, x, target, cos_t, sin_t, w_out, final_g, gn_gain, sinks, decay_in, qdec_t, kdec_t, cd_t, bias_t)


def _mix_bwd(proj, dxo, mix, o_att, probs, p_sinks, states, x, w_out, w_in_t, norm_g, gn_gain, tables, tb):
    seq = dxo.shape[0]
    nsub = tb // BLK
    nblk = seq // tb
    cos_t, sin_t, decay_in, qdec_t, kdec_t, cd_t, _ = tables
    kv_cols = OFF_AK // (2 * ATT_KV_WIDTH)
    state_shape = (PAIRS, 2 * RET_QK_DIM, RET_V_DIM)

    def body(p_ref, pkv_ref, dxo_ref, mix_ref, oatt_ref, probs_ref, psink_ref, st_ref, cos_ref, sin_ref, x_ref, wout_ref,
             win_ref, g_ref, gain_ref, din_ref, qdec_ref, kdec_ref, cd_ref,
             dp_ref, gx_ref, gwout_ref, gnorm_ref, dgain_ref, dsink_ref,
             dmix_ref, kv_ref, dkc_ref, dvc_ref, gst_ref):
        i = pl.program_id(0)

        @pl.when(i == 0)
        def _():
            gwout_ref[...] = jnp.zeros_like(gwout_ref)
            gnorm_ref[...] = jnp.zeros_like(gnorm_ref)
            dgain_ref[...] = jnp.zeros_like(dgain_ref)
            dsink_ref[...] = jnp.zeros_like(dsink_ref)
            dkc_ref[...] = jnp.zeros_like(dkc_ref)
            dvc_ref[...] = jnp.zeros_like(dvc_ref)
            gst_ref[...] = jnp.zeros_like(gst_ref)

        dxo_b = dxo_ref[...].astype(BF16)
        dmix_ref[...] = _dot_nt(dxo_b, wout_ref[...])
        gwout_ref[...] += _dot_tn(mix_ref[...], dxo_b)
        kv_ref[0:BLK, :] = pkv_ref[...].astype(F32)
        kv_ref[BLK:, :] = _proj_cols(p_ref, slice(None), OFF_AK, OFF_AK + 2 * ATT_KV_WIDTH).astype(F32)
        low = _low_lanes((BLK, LANE))
        low2 = _low_lanes((2 * BLK, LANE))
        lane = lax.broadcasted_iota(jnp.int32, (1, LANE), 1)
        prev = _in_previous_block()

        def sub(jj, carry):
            dkc, dvc, gsts, dgain, dsink = carry
            j = nsub - 1 - jj
            rows = pl.ds(pl.multiple_of(j * BLK, BLK), BLK)
            both = pl.ds(pl.multiple_of(j * BLK, BLK), 2 * BLK)


            aq = _proj_cols(p_ref, rows, OFF_AQ, OFF_AQ + ATT_WIDTH)
            az = _proj_cols(p_ref, rows, OFF_AZ, OFF_AZ + ATT_WIDTH).astype(F32)
            k_ops = _kv_operands(kv_ref[both, 0:ATT_KV_WIDTH])
            v_ops = _kv_operands(kv_ref[both, ATT_KV_WIDTH:2 * ATT_KV_WIDTH])
            da = dmix_ref[rows, 0:ATT_WIDTH]
            sig = _sigmoid(az)
            d_o = da * (az * sig)
            qs = [(_stack_tiles(aq, 2 * g) * ATT_SCALE).astype(BF16) for g in range(ATT_KV_HEADS)]
            dos = [_stack_tiles(d_o, 2 * g).astype(BF16) for g in range(ATT_KV_HEADS)]
            p_b = [probs_ref[j, k] for k in range(len(ATT_PROBLEMS))]
            p = [pk.astype(F32) for pk in p_b]
            sink_all = psink_ref[j]
            p_sink = [sink_all[k:k + 1, :] for k in range(len(ATT_PROBLEMS))]
            cos_b, sin_b = cos_ref[rows, :], sin_ref[rows, :]
            qdec, kdec = qdec_ref[...], kdec_ref[...]
            qr, kr, qd, kd = _retention_operands(p_ref, rows, cos_b, sin_b, qdec, kdec)
            heads = [(t, hh) for t in range(PAIRS) for hh in range(2)]
            head_cols = [slice(h * RET_V_DIM, (h + 1) * RET_V_DIM) for h in range(RET_HEADS)]
            q_rows = [jnp.concatenate(_split_heads(_tile(qr, t)), axis=0).astype(BF16) for t in range(PAIRS)]
            k_rows = [jnp.concatenate(_split_heads(_tile(kr, t)), axis=0).astype(BF16) for t in range(PAIRS)]
            din = [din_ref[t] for t in range(PAIRS)]

            dpr = [_fold(_dot_nt(v_ops[g][hi], dos[g]), prev) for g, hi in ATT_PROBLEMS]
            sc = [(_dot_nt(q_rows[t], _tile(kr, t).astype(BF16)) * din[t]).astype(BF16) for t in range(PAIRS)]
            qd_heads = [_split_heads(_tile(qd, t)) for t in range(PAIRS)]
            kd_heads = [_split_heads(_tile(kd, t)) for t in range(PAIRS)]
            state_b = [st_ref[j, t].astype(BF16) for t in range(PAIRS)]
            gst_b = [gsts[t].astype(BF16) for t in range(PAIRS)]
            vs = [_proj_cols(p_ref, rows, OFF_RV + h * RET_V_DIM, OFF_RV + (h + 1) * RET_V_DIM) for h in range(RET_HEADS)]
            rzs = [_proj_cols(p_ref, rows, OFF_RZ + h * RET_V_DIM, OFF_RZ + (h + 1) * RET_V_DIM).astype(F32)
                   for h in range(RET_HEADS)]
            drs = [dmix_ref[rows, ATT_WIDTH + h * RET_V_DIM:ATT_WIDTH + (h + 1) * RET_V_DIM] for h in range(RET_HEADS)]
            gains = [gain_ref[:, c] for c in head_cols]
            lhs = [jnp.concatenate([sc[t][hh * BLK:(hh + 1) * BLK], qd_heads[t][hh].astype(BF16)], axis=1) for t, hh in heads]
            rhs = [jnp.concatenate([vs[2 * t + hh], state_b[t]], axis=0) for t, hh in heads]

            delta = [jnp.sum(pi * di, axis=0, keepdims=True) for pi, di in zip(p, dpr)]
            ds = [_unfold((pi * (di - ti)).astype(BF16), prev) for pi, di, ti in zip(p, dpr, delta)]
            o_ret = [_dot(l, r) for l, r in zip(lhs, rhs)]
            for (g, hi), ki, ti in zip(ATT_PROBLEMS, p_sink, delta):
                sink_part = ki * ti
                for half in range(2):
                    tot = jnp.sum(sink_part[:, half * BLK:(half + 1) * BLK], axis=1, keepdims=True)
                    dsink = dsink - jnp.where(lane == 4 * g + 2 * half + hi, tot, 0.0)

            dq_tiles, dk_sums, dv_sums = [], [], []
            for g in range(ATT_KV_HEADS):
                ds_cat = jnp.concatenate([ds[2 * g], ds[2 * g + 1]], axis=0)
                p_cat = jnp.concatenate([_unfold(p_b[2 * g], prev), _unfold(p_b[2 * g + 1], prev)], axis=0)
                dqs = _dot_tn(ds_cat, jnp.concatenate(k_ops[g], axis=0)) * ATT_SCALE
                dq_tiles += [dqs[0:BLK], dqs[BLK:]]
                dk_sums.append(_dot(ds_cat, qs[g]))
                dv_sums.append(_dot(p_cat, dos[g]))
            ons, rstds = _group_norm_all(o_ret)
            sig_r = [_sigmoid(z) for z in rzs]
            dgn = [d * (z * g) for d, z, g in zip(drs, rzs, sig_r)]
            dz_parts = [d * (o * gn) * (g * (1.0 + z * (1.0 - g))) for d, o, gn, g, z in zip(drs, ons, gains, sig_r, rzs)]
            dgain_parts = [jnp.sum(d * o, axis=0, keepdims=True) for d, o in zip(dgn, ons)]
            don = [d * gn for d, gn in zip(dgn, gains)]
            mean_don = [jnp.mean(d, axis=1, keepdims=True) for d in don]
            mean_don_on = [jnp.mean(d * o, axis=1, keepdims=True) for d, o in zip(don, ons)]
            dob = [(r * (d - a - o * b)).astype(BF16) for r, d, a, o, b in zip(rstds, don, mean_don, ons, mean_don_on)]

            dlhs = [_dot_nt(d, r) for d, r in zip(dob, rhs)]
            drhs = [_dot_tn(l, d) for l, d in zip(lhs, dob)]
            dkds = [_dot_nt(vs[2 * t + hh], gst_b[t]) for t, hh in heads]
            dv_parts = [drhs[2 * t + hh][0:BLK] + _dot(kd_heads[t][hh].astype(BF16), gst_b[t]) for t, hh in heads]
            daz = da * oatt_ref[rows, :] * (sig * (1.0 + az * (1.0 - sig)))

            def kv_grad(sums):
                (a0, b0), (a1, b1) = [(s[0:2 * BLK], s[2 * BLK:]) for s in sums]
                return jnp.where(low2, a0, b1) + pltpu.roll(jnp.where(low2, a1, b0), HALF_LANE, 1)

            dk_both, dv_both = kv_grad(dk_sums), kv_grad(dv_sums)
            dak = dk_both[BLK:] + dkc
            dav = dv_both[BLK:] + dvc
            das = [(dlhs[2 * t + hh][:, 0:BLK] * din[t][hh * BLK:(hh + 1) * BLK]).astype(BF16) for t, hh in heads]
            new_gsts = [gsts[t] * cd_ref[t] + drhs[2 * t][BLK:] + drhs[2 * t + 1][BLK:] for t in range(PAIRS)]
            dq_parts = [_dot(jnp.concatenate([das[2 * t], das[2 * t + 1]], axis=1), k_rows[t])
                        + jnp.where(low, dlhs[2 * t][:, BLK:], dlhs[2 * t + 1][:, BLK:]) * _tile(qdec, t)
                        for t in range(PAIRS)]
            dk_parts = [_dot_tn(jnp.concatenate([das[2 * t], das[2 * t + 1]], axis=0), q_rows[t])
                        + jnp.where(low, dkds[2 * t], dkds[2 * t + 1]) * _tile(kdec, t) for t in range(PAIRS)]
            drq = _rotate_transposed(jnp.concatenate(dq_parts, axis=1), cos_b, sin_b)
            drk = _rotate_transposed(jnp.concatenate(dk_parts, axis=1) * RET_SCALE, cos_b, sin_b)

            dp_ref[rows, :] = jnp.concatenate(
                [jnp.concatenate(dq_tiles, axis=1), dak, dav, daz, drq, drk] + dv_parts + dz_parts, axis=1).astype(BF16)
            dgain = dgain + jnp.concatenate(dgain_parts, axis=1)
            return dk_both[0:BLK], dv_both[0:BLK], tuple(new_gsts), dgain, dsink

        carry = (dkc_ref[...], dvc_ref[...], tuple(gst_ref[t] for t in range(PAIRS)), dgain_ref[...], dsink_ref[...])
        for jb in range(nsub):
            carry = sub(jnp.int32(jb), carry)
        dkc, dvc, gsts, dgain, dsink = carry
        dkc_ref[...] = dkc
        dvc_ref[...] = dvc
        gst_ref[...] = jnp.stack(gsts)
        dgain_ref[...] = dgain
        dsink_ref[...] = dsink

        dh = _dot(dp_ref[...], win_ref[...])
        xv = x_ref[...]
        r = lax.rsqrt(jnp.mean(xv * xv, axis=1, keepdims=True) + RMS_EPS)
        xn = xv * r
        gnorm_ref[...] += jnp.sum(dh * xn, axis=0, keepdims=True)
        u = dh * g_ref[...]
        gx_ref[...] = dxo_ref[...] + r * u - xn * (r * jnp.mean(u * xn, axis=1, keepdims=True))

    rev_rows = lambda w: pl.BlockSpec((tb, w), lambda i: (nblk - 1 - i, 0))
    prev_kv = pl.BlockSpec((None, BLK, 2 * ATT_KV_WIDTH),
                           lambda i: (0, jnp.maximum((nblk - 1 - i) * nsub - 1, 0), kv_cols))
    return _call(
        body, name="mix_bwd", grid=(nblk,),
        out_shape=(
            jax.ShapeDtypeStruct((seq, IN_WIDTH), BF16),
            jax.ShapeDtypeStruct((seq, D_MODEL), F32),
            jax.ShapeDtypeStruct((MIX_WIDTH, D_MODEL), F32),
            jax.ShapeDtypeStruct((1, D_MODEL), F32),
            jax.ShapeDtypeStruct((1, RET_WIDTH), F32),
            jax.ShapeDtypeStruct((1, LANE), F32),
        ),
        in_specs=[
            pl.BlockSpec((N_CHIPS, tb, SHARD_PAD), lambda i: (0, nblk - 1 - i, 0)),
            prev_kv, rev_rows(D_MODEL), rev_rows(MIX_WIDTH), rev_rows(ATT_WIDTH),
            pl.BlockSpec((nsub, len(ATT_PROBLEMS), BLK, 2 * BLK), lambda i: (nblk - 1 - i, 0, 0, 0)),
            pl.BlockSpec((nsub, len(ATT_PROBLEMS), 2 * BLK), lambda i: (nblk - 1 - i, 0, 0)),
            pl.BlockSpec((nsub,) + state_shape, lambda i: (nblk - 1 - i, 0, 0, 0)),
            rev_rows(LANE), rev_rows(LANE), rev_rows(D_MODEL),
            _const_spec((MIX_WIDTH, D_MODEL)), _const_spec((IN_WIDTH, D_MODEL)), _const_spec((1, D_MODEL)),
            _const_spec((1, RET_WIDTH)),
            _const_spec((PAIRS, 2 * BLK, BLK)), _const_spec((BLK, RET_QK_WIDTH)), _const_spec((BLK, RET_QK_WIDTH)),
            _const_spec(state_shape),
        ],
        out_specs=(
            rev_rows(IN_WIDTH), rev_rows(D_MODEL), _const_spec((MIX_WIDTH, D_MODEL)), _const_spec((1, D_MODEL)),
            _const_spec((1, RET_WIDTH)), _const_spec((1, LANE)),
        ),
        scratch_shapes=[
            pltpu.VMEM((tb, MIX_WIDTH), F32),
            pltpu.VMEM((tb + BLK, 2 * ATT_KV_WIDTH), F32),
            pltpu.VMEM((BLK, ATT_KV_WIDTH), F32), pltpu.VMEM((BLK, ATT_KV_WIDTH), F32),
            pltpu.VMEM(state_shape, F32),
        ],
        compiler_params=_params(60, ("arbitrary",)),
    )(proj, proj, dxo, mix, o_att, probs, p_sinks, states, cos_t, sin_t, x, w_out, w_in_t, norm_g, gn_gain, decay_in,
      qdec_t, kdec_t, cd_t)


def _gw_in_reduce(h_t, dproj, gw_out, small, tb):
    seq = dproj.shape[0]
    nblk = seq // tb
    last = nblk - 1
    hand_on = min(1, last)
    half = D_MODEL // 2
    A, B, C, N_SEMS = 0, N_CHIPS, 2 * N_CHIPS, 2 * N_CHIPS + 1

    def body(win_ref, ht_ref, dp_ref, gwo_hbm, s0_ref, s1_ref, s2_ref, s3_ref, s4_ref, out_ref, fout_ref, packsum_ref,
             acc, sib, to_sib, send_buf, b_in, fin, fout, mine_out, sib_out, send_out, b_out, pack_ref, packs,
             send_sems, recv_sems, local_sems):
        p, i = pl.program_id(0), pl.program_id(1)
        small_start, small_hand_on, small_sum_up, small_finish = _small_exchange(
            gwo_hbm, (s0_ref, s1_ref, s2_ref, s3_ref, s4_ref), fout, mine_out, sib_out, send_out, b_out, pack_ref,
            packs, send_sems, recv_sems, local_sems, N_SEMS)

        @pl.when(jnp.logical_and(p == 0, i == 0))
        def _():
            small_start()

        @pl.when(jnp.logical_and(p == 0, i == hand_on))
        def _():
            small_hand_on()

        @pl.when(jnp.logical_and(p == N_CHIPS - 1, i == 0))
        def _():
            small_sum_up()

        x, y, c = lax.axis_index("x"), lax.axis_index("y"), lax.axis_index("c")
        chip = 2 * x + y
        sibling = (x, y, 1 - c)
        mine = pl.ds(pl.multiple_of(c * half, half), half)
        other = pl.ds(pl.multiple_of((1 - c) * half, half), half)

        def remote(src, dst, send_k, recv_k, to):
            return pltpu.make_async_remote_copy(src_ref=src, dst_ref=dst, send_sem=send_sems.at[send_k],
                                                recv_sem=recv_sems.at[recv_k], device_id=to, device_id_type=MESH_ID)

        slot = p % 2

        @pl.when(i == 0)
        def _():
            acc[slot] = jnp.zeros(acc.shape[1:], F32)

        acc[slot] += _dot(ht_ref[...], dp_ref[...])

        for q in range(N_CHIPS):
            s = q % 2
            to_sibling = remote(to_sib.at[s], sib.at[s], A + q, A + q, sibling)

            @pl.when(jnp.logical_and(p == q, i == last))
            def _():
                to_sib[s] = acc[s, other, :].astype(BF16)
                to_sibling.start()

            if q < N_CHIPS - 1:
                dest = (chip + 1 + q) % N_CHIPS

                @pl.when(jnp.logical_and(p == q + 1, i == hand_on))
                def _():
                    to_sibling.wait_recv()
                    send_buf[q] = (acc[s, mine, :] + sib[s].astype(F32)).astype(BF16)
                    remote(send_buf.at[q], b_in.at[chip], B + q, B + chip, (dest // 2, dest % 2, c)).start()
                    to_sibling.wait_send()
            else:
                @pl.when(jnp.logical_and(p == q, i == last))
                def _():
                    to_sibling.wait_recv()
                    fin[mine, :] = acc[s, mine, :] + sib[s].astype(F32)
                    for j in range(N_CHIPS):
                        @pl.when(j != chip)
                        def _():
                            remote(b_in.at[j], b_in.at[j], B + j, B + j, sibling).wait_recv()
                            fin[mine, :] += b_in[j].astype(F32)
                    to_core = remote(fin.at[mine, :], fin.at[mine, :], C, C, sibling)
                    to_core.start()
                    remote(fin.at[other, :], fin.at[other, :], C, C, sibling).wait_recv()
                    out_ref[...] = fin[...]
                    to_core.wait_send()
                    to_sibling.wait_send()
                    for k in range(N_CHIPS - 1):
                        remote(send_buf.at[k], b_in.at[chip], B + k, B + k, sibling).wait_send()
                    packsum_ref[...] = small_finish()
                    fout_ref[...] = fout[...]

    whole = lambda shape: pl.BlockSpec(shape, lambda p, i, win: (0,) * len(shape), pipeline_mode=pl.Buffered(1))
    grid_spec = pltpu.PrefetchScalarGridSpec(
        num_scalar_prefetch=1, grid=(N_CHIPS, nblk),
        in_specs=[pl.BlockSpec((D_MODEL, tb), lambda p, i, win: (0, i)),
                  pl.BlockSpec((pl.Element(tb), pl.Element(SHARD_PAD)),
                               lambda p, i, win: (i * tb, pl.multiple_of(win[p] * LANE, LANE))),
                  pl.BlockSpec(memory_space=pl.ANY)] + [whole(s.shape) for s in small],
        out_specs=(whole((D_MODEL, SHARD_PAD)), whole((SHARD_OUT, D_MODEL)), whole((PACK_ROWS, D_MODEL))),
        scratch_shapes=[
            pltpu.VMEM((2, D_MODEL, SHARD_PAD), F32), pltpu.VMEM((2, half, SHARD_PAD), BF16),
            pltpu.VMEM((2, half, SHARD_PAD), BF16),
            pltpu.VMEM((N_CHIPS - 1, half, SHARD_PAD), BF16), pltpu.VMEM((N_CHIPS, half, SHARD_PAD), BF16),
            pltpu.VMEM((D_MODEL, SHARD_PAD), F32),
        ] + _small_exchange_scratch() + [
            pltpu.SemaphoreType.DMA((N_SEMS + SMALL_SEMS,)), pltpu.SemaphoreType.DMA((N_SEMS + SMALL_SEMS,)),
            pltpu.SemaphoreType.DMA((N_CHIPS,)),
        ])
    chip = 2 * lax.axis_index("x") + lax.axis_index("y")
    owner = (chip + 1 + jnp.arange(N_CHIPS, dtype=jnp.int32)) % N_CHIPS
    win_start = (owner * SHARD_IN) // LANE
    return _call(
        body, name="gw_in_reduce", grid_spec=grid_spec,
        out_shape=(jax.ShapeDtypeStruct((D_MODEL, SHARD_PAD), F32), jax.ShapeDtypeStruct((SHARD_OUT, D_MODEL), F32),
                   jax.ShapeDtypeStruct((PACK_ROWS, D_MODEL), F32)),
        compiler_params=_params(52, ("arbitrary", "arbitrary")),
    )(win_start.astype(jnp.int32), h_t, dproj, gw_out, *small)


SMALL_SEMS = 17


def _small_exchange(gwo_hbm, small_refs, fout_ref, mine_out, sib_out, send_out, b_out, pack_ref, packs,
                    send_sems, recv_sems, local_sems, base):
    half_out = SHARD_OUT // 2
    A_OUT, B_OUT, C_OUT, PACK = base, base + 4, base + 8, base + 9
    assert len(small_refs) == PACK_PARTS
    x, y, c = lax.axis_index("x"), lax.axis_index("y"), lax.axis_index("c")
    chip = 2 * x + y
    dev = 2 * chip + c
    sibling = (x, y, 1 - c)

    def remote(src, dst, send_k, recv_k, to):
        return pltpu.make_async_remote_copy(src_ref=src, dst_ref=dst, send_sem=send_sems.at[send_k],
                                            recv_sem=recv_sems.at[recv_k], device_id=to, device_id_type=MESH_ID)

    def out_rows(j, core):
        return pl.ds(pl.multiple_of(j * SHARD_OUT + core * half_out, half_out), half_out)

    my_out_rows = pl.ds(pl.multiple_of(c * half_out, half_out), half_out)
    local = [pltpu.make_async_copy(gwo_hbm.at[out_rows(j, c), :], mine_out.at[j], local_sems.at[j])
             for j in range(N_CHIPS)]
    stage_a = [remote(gwo_hbm.at[out_rows(j, 1 - c), :], sib_out.at[j], A_OUT + j, A_OUT + j, sibling)
               for j in range(N_CHIPS)]
    mine_half_out = fout_ref.at[my_out_rows, :]
    stage_c = [remote(mine_half_out, mine_half_out, C_OUT, C_OUT, sibling)]

    def start():
        pack_ref[...] = jnp.zeros_like(pack_ref)
        for k, s_ref in enumerate(small_refs):
            pack_ref[k:k + 1, 0:s_ref.shape[1]] = s_ref[0:1, :]
        packs[dev] = pack_ref[...]
        for d in range(N_DEV):
            to = (d // 4, (d // 2) % 2, d % 2)

            @pl.when(d != dev)
            def _():
                remote(pack_ref, packs.at[dev], PACK + d, PACK + dev, to).start()

        for cp in local + stage_a:
            cp.start()

    def hand_on():
        for cp in local:
            cp.wait()
        for cp in stage_a:
            cp.wait_recv()
        for j in range(N_CHIPS):
            mine_out[j] = mine_out[j] + sib_out[j]

        for j in range(N_CHIPS):
            to = (j // 2, j % 2, c)

            @pl.when(j != chip)
            def _():
                send_out[j] = mine_out[j].astype(BF16)
                remote(send_out.at[j], b_out.at[chip], B_OUT + j, B_OUT + chip, to).start()

            @pl.when(j == chip)
            def _():
                fout_ref[my_out_rows, :] = mine_out[j]

    def sum_up():
        for j in range(N_CHIPS):
            @pl.when(j != chip)
            def _():
                remote(b_out.at[j], b_out.at[j], B_OUT + j, B_OUT + j, sibling).wait_recv()
                fout_ref[my_out_rows, :] += b_out[j].astype(F32)

        for cp in stage_c:
            cp.start()

    def finish():
        other_half_out = fout_ref.at[pl.ds(pl.multiple_of((1 - c) * half_out, half_out), half_out), :]
        remote(other_half_out, other_half_out, C_OUT, C_OUT, sibling).wait_recv()

        for d in range(N_DEV):
            @pl.when(d != dev)
            def _():
                remote(pack_ref, packs.at[d], PACK + d, PACK + d, sibling).wait_recv()
        total = packs[0]
        for d in range(1, N_DEV):
            total = total + packs[d]

        for cp in stage_a + stage_c:
            cp.wait_send()
        for j in range(N_CHIPS):
            @pl.when(j != chip)
            def _():
                remote(b_out.at[j], b_out.at[j], B_OUT + j, B_OUT + j, sibling).wait_send()
        for d in range(N_DEV):
            @pl.when(d != dev)
            def _():
                remote(pack_ref, packs.at[d], PACK + d, PACK + d, sibling).wait_send()
        return total

    return start, hand_on, sum_up, finish


def _small_exchange_scratch():
    half_out = SHARD_OUT // 2
    return [
        pltpu.VMEM((SHARD_OUT, D_MODEL), F32),
        pltpu.VMEM((N_CHIPS, half_out, D_MODEL), F32), pltpu.VMEM((N_CHIPS, half_out, D_MODEL), F32),
        pltpu.VMEM((N_CHIPS, half_out, D_MODEL), BF16), pltpu.VMEM((N_CHIPS, half_out, D_MODEL), BF16),
        pltpu.VMEM((PACK_ROWS, D_MODEL), F32), pltpu.VMEM((N_DEV, PACK_ROWS, D_MODEL), F32),
    ]


def _adam_math(w, g, m, v):
    mn = ADAM_B1 * m + (1.0 - ADAM_B1) * g
    vn = ADAM_B2 * v + (1.0 - ADAM_B2) * (g * g)
    m_hat = mn / (1.0 - ADAM_B1 ** ADAM_STEP)
    v_hat = vn / (1.0 - ADAM_B2 ** ADAM_STEP)
    return -ADAM_LR * (m_hat / (jnp.sqrt(v_hat) + ADAM_EPS) + ADAM_WD * w), mn, vn


def _adamw(name, w, g, m, v, tb):
    rows, cols = w.shape

    def body(w_ref, g_ref, m_ref, v_ref, go_ref, d_ref, mo_ref, vo_ref):
        gv = g_ref[...]
        go_ref[...] = gv
        d_ref[...], mo_ref[...], vo_ref[...] = _adam_math(w_ref[...], gv, m_ref[...], v_ref[...])

    spec = pl.BlockSpec((tb, cols), lambda i: (i, 0))
    shape = jax.ShapeDtypeStruct((rows, cols), F32)
    return _call(
        body, name=name, grid=(rows // tb,), out_shape=(shape,) * 4,
        in_specs=[spec] * 4, out_specs=(spec,) * 4,
        compiler_params=_params(32, ("arbitrary",)),
    )(w, g, m, v)


def _adamw_w_in(w_t, g_window, m_t, v_t, tb):
    def body(w_ref, g_ref, m_ref, v_ref, go_ref, d_ref, mo_ref, vo_ref, gt_ref):
        gt_ref[...] = g_ref[...].T
        gv = gt_ref[pl.ds(pl.multiple_of(lax.axis_index("y") * SHARD_SHIFT, SHARD_SHIFT), SHARD_IN), :]
        go_ref[...] = gv
        d_ref[...], mo_ref[...], vo_ref[...] = _adam_math(w_ref[...], gv, m_ref[...], v_ref[...])

    spec = pl.BlockSpec((SHARD_IN, tb), lambda i: (0, i))
    shape = jax.ShapeDtypeStruct((SHARD_IN, D_MODEL), F32)
    return _call(
        body, name="adamw_w_in", grid=(D_MODEL // tb,), out_shape=(shape,) * 4,
        in_specs=[spec, pl.BlockSpec((tb, SHARD_PAD), lambda i: (i, 0)), spec, spec], out_specs=(spec,) * 4,
        scratch_shapes=[pltpu.VMEM((SHARD_PAD, tb), F32)],
        compiler_params=_params(32, ("arbitrary",)),
    )(w_t, g_window, m_t, v_t)


def _adamw_small(sums, params):
    def body(sums_ref, *refs):
        ins, outs = refs[:3 * len(params)], refs[3 * len(params):]
        for k in range(len(params)):
            w_ref, m_ref, v_ref = ins[3 * k:3 * k + 3]
            g = sums_ref[k:k + 1, 0:w_ref.shape[1]]
            go_ref, d_ref, mo_ref, vo_ref = outs[4 * k:4 * k + 4]
            go_ref[...] = g
            d_ref[...], mo_ref[...], vo_ref[...] = _adam_math(w_ref[...], g, m_ref[...], v_ref[...])

    vmem = pl.BlockSpec(memory_space=pltpu.VMEM)
    flat = [a for p in params for a in p]
    shapes = tuple(jax.ShapeDtypeStruct(p[0].shape, F32) for p in params for _ in range(4))
    res = _call(body, name="adamw_small", out_shape=shapes, in_specs=[vmem] * (1 + len(flat)),
                out_specs=(vmem,) * len(shapes), compiler_params=_params(16))(sums, *flat)
    return [res[4 * k:4 * k + 4] for k in range(len(params))]


def kernel(x, norm_g, w_in, att_sinks, ret_gn_g, w_out, final_g, loss_target, m_norm_g, m_w_in, m_att_sinks, m_ret_gn_g, m_w_out, m_final_g, v_norm_g, v_w_in, v_att_sinks, v_ret_gn_g, v_w_out, v_final_g):
    seq = x.shape[1]
    xs, tgt = x[0], loss_target[0]
    final_g2 = final_g.reshape(1, D_MODEL)
    tables = _tables(seq)

    w_in_t, m_w_in_t, v_w_in_t = w_in[0].T, m_w_in[0].T, v_w_in[0].T
    proj, h_t, w_in_full, w_out_full = _in_proj(xs, norm_g, w_in_t, w_out[0], min(TOKENS_PROJ, seq))
    mix, dxo, states, loss_part, gfin, o_att, probs, p_sinks = _mix_fwd(
        proj, xs, tgt, w_out_full, final_g2, ret_gn_g, att_sinks, tables, min(TOKENS_MIX, seq))
    dproj, grad_x, gw_out, gnorm, dgain, dsink = _mix_bwd(
        proj, dxo, mix, o_att, probs, p_sinks, states, xs, w_out_full, w_in_full, norm_g, ret_gn_g, tables,
        min(TOKENS_MIX, seq))
    g_in, g_out, sums = _gw_in_reduce(h_t, dproj, gw_out, (gnorm, gfin, dgain, dsink, loss_part),
                                      min(TOKENS_GW, seq))

    res_in = [r.T for r in _adamw_w_in(w_in_t, g_in, m_w_in_t, v_w_in_t, 256)]
    res_out = _adamw("adamw_w_out", w_out[0], g_out, m_w_out[0], v_w_out[0], SHARD_OUT)
    as_row = lambda a: a.reshape(1, D_MODEL)
    r_norm, r_final, r_gain, r_sink = _adamw_small(sums, [
        (norm_g, m_norm_g, v_norm_g), (final_g2, as_row(m_final_g), as_row(v_final_g)),
        (ret_gn_g, m_ret_gn_g, v_ret_gn_g), (att_sinks, m_att_sinks, v_att_sinks)])

    outs = []
    for k in range(4):
        outs += [r_norm[k], res_in[k][None], r_sink[k], r_gain[k], res_out[k][None], r_final[k].reshape(D_MODEL)]
    return (sums[4, 0], grad_x[None], *outs)
```
